```python
import jax
import jax.numpy as jnp
from jax import lax
import numpy as np


D_MODEL = 1024
BATCH = 32
SEQ = 2048
DEPTH = 1

CHUNK = 64
HEAD_DIM = 64
N_RWKV_HEADS = 8
N_FOX_HEADS = 8
D_RWKV = N_RWKV_HEADS * HEAD_DIM
D_FOX = N_FOX_HEADS * HEAD_DIM
D_MIX = D_RWKV + D_FOX
DECAY_LORA = 64
AAA_LORA = 64
GATE_LORA = 128
R_OFF = 0
K_OFF = D_RWKV
V_OFF = 2 * D_RWKV
WD_OFF = 3 * D_RWKV
AD_OFF = WD_OFF + DECAY_LORA
GD_OFF = AD_OFF + AAA_LORA
RWKV_COLS = GD_OFF + GATE_LORA
FOX_COLS = 4 * D_FOX + N_FOX_HEADS
D_IN = RWKV_COLS + FOX_COLS
Q_BLOCK = 128
N_EXPERTS = 32
TOP_K = 4
D_EXPERT = D_MODEL
SWIGLU_ALPHA = 1.702
SWIGLU_LIMIT = 7.0
EXPERT_BLOCK = 256
NORM_EPS = 1e-6
GN_EPS = 64e-5
N_MOD = 6

kernel_name = 'hybrid_rwkv7_fox_moe_adaln_block'


def rms_norm(x, g):
    xf = x.astype(jnp.float32)
    y = xf * lax.rsqrt(jnp.mean(xf * xf, axis=-1, keepdims=True) + NORM_EPS)
    return (y * g.astype(jnp.float32)).astype(x.dtype)


def modulate(h, shift, scale):
    return h * (1 + scale[:, None, :]) + shift[:, None, :]


def rwkv7_mix(p, mu_shift, w0, w2, a0, a2, g2, k_k, k_a, r_k, gn_w, gn_b):
    B, S, _ = p.shape
    out_dtype = p.dtype
    pf = p.astype(jnp.float32)
    p_prev = jnp.pad(pf, ((0, 0), (1, 0), (0, 0)))[:, :-1]
    pf = pf + mu_shift * (p_prev - pf)
    r = pf[..., R_OFF:K_OFF]
    k = pf[..., K_OFF:V_OFF]
    v = pf[..., V_OFF:WD_OFF]
    wd = pf[..., WD_OFF:AD_OFF]
    ad = pf[..., AD_OFF:GD_OFF]
    gd = pf[..., GD_OFF:RWKV_COLS]
    w = -jax.nn.softplus(-(w0 + jnp.tanh(wd) @ w2)) - 0.5
    decay = jnp.exp(-jnp.exp(w))
    a = jax.nn.sigmoid(a0 + ad @ a2)
    g = jax.nn.sigmoid(gd) @ g2
    hs = lambda t: t.reshape(B, S, N_RWKV_HEADS, HEAD_DIM)
    kk = hs(k * k_k)
    kk = kk / jnp.maximum(jnp.sqrt(jnp.sum(kk * kk, axis=-1, keepdims=True)), 1e-12)
    k = k * (1 + (a - 1) * k_a)
    r_h, k_h, v_h, d_h, a_h = hs(r), hs(k), hs(v), hs(decay), hs(a)

    def step(state, inp):
        r_t, k_t, v_t, d_t, kk_t, a_t = inp
        sa = jnp.einsum('bhvk,bhk->bhv', state, -kk_t)
        state = (state * d_t[:, :, None, :]
                 + sa[..., None] * (kk_t * a_t)[:, :, None, :]
                 + v_t[..., None] * k_t[:, :, None, :])
        y_t = jnp.einsum('bhvk,bhk->bhv', state, r_t)
        return state, y_t

    xs = tuple(jnp.moveaxis(t, 1, 0) for t in (r_h, k_h, v_h, d_h, kk, a_h))
    s0 = jnp.zeros((B, N_RWKV_HEADS, HEAD_DIM, HEAD_DIM), jnp.float32)
    _, y = lax.scan(step, s0, xs)
    y = jnp.moveaxis(y, 0, 1)
    mean = jnp.mean(y, axis=-1, keepdims=True)
    var = jnp.mean(jnp.square(y - mean), axis=-1, keepdims=True)
    y = ((y - mean) * lax.rsqrt(var + GN_EPS)).reshape(B, S, D_RWKV) * gn_w + gn_b
    bonus = jnp.sum(r_h * k_h * r_k, axis=-1, keepdims=True) * v_h
    y = y + bonus.reshape(B, S, D_RWKV)
    return (y * g).astype(out_dtype)


def fox_mix(p, b_f, q_norm_g, k_norm_g, o_norm_g):
    B, S, _ = p.shape
    hs = lambda t: t.reshape(B, S, N_FOX_HEADS, HEAD_DIM)
    q = rms_norm(hs(p[..., 0:D_FOX]), q_norm_g)
    k = rms_norm(hs(p[..., D_FOX:2 * D_FOX]), k_norm_g)
    v = hs(p[..., 2 * D_FOX:3 * D_FOX])
    og = p[..., 3 * D_FOX:4 * D_FOX]
    log_f = jax.nn.log_sigmoid(p[..., 4 * D_FOX:].astype(jnp.float32) + b_f.astype(jnp.float32))
    cum = jnp.transpose(jnp.cumsum(log_f, axis=1), (0, 2, 1))
    q, k, v = (jnp.transpose(t, (0, 2, 1, 3)) for t in (q, k, v))
    scale = HEAD_DIM ** -0.5
    outs = []
    for blk in range(S // Q_BLOCK):
        q0 = blk * Q_BLOCK
        kv_len = q0 + Q_BLOCK
        qb = q[:, :, q0:kv_len]
        kb = k[:, :, :kv_len]
        vb = v[:, :, :kv_len]
        s = jnp.einsum('bhqd,bhkd->bhqk', qb, kb).astype(jnp.float32) * scale
        s = s + cum[:, :, q0:kv_len, None] - cum[:, :, None, :kv_len]
        mask = (q0 + jnp.arange(Q_BLOCK))[:, None] >= jnp.arange(kv_len)[None, :]
        s = jnp.where(mask, s, -jnp.inf)
        pr = jax.nn.softmax(s, axis=-1)
        outs.append(jnp.einsum('bhqk,bhkd->bhqd', pr.astype(vb.dtype), vb))
    o = jnp.transpose(jnp.concatenate(outs, axis=2), (0, 2, 1, 3))
    o = rms_norm(o, o_norm_g).reshape(B, S, D_FOX)
    return o * jax.nn.sigmoid(og)


def moe_ffn(h, w_router, b_router, w_gate_up, b_gate_up, w_down, b_down):
    B, S, D = h.shape
    T = B * S
    ht = h.reshape(T, D)
    logits = (ht @ w_router).astype(jnp.float32) + b_router.astype(jnp.float32)
    top_val, top_idx = lax.top_k(logits, TOP_K)
    gates = jax.nn.softmax(top_val, axis=-1)
    n_slots = T * TOP_K
    flat_e = top_idx.reshape(-1)
    flat_tok = jnp.arange(n_slots, dtype=jnp.int32) // TOP_K
    flat_gate = gates.reshape(-1)
    order = jnp.argsort(flat_e)
    sorted_e = flat_e[order]
    counts = jnp.zeros((N_EXPERTS,), jnp.int32).at[flat_e].add(1)
    starts = jnp.cumsum(counts) - counts
    padded = (counts + EXPERT_BLOCK - 1) // EXPERT_BLOCK * EXPERT_BLOCK
    pad_ends = jnp.cumsum(padded)
    pad_starts = pad_ends - padded
    dest = pad_starts[sorted_e] + (jnp.arange(n_slots, dtype=jnp.int32) - starts[sorted_e])
    n_blocks = -(-n_slots // EXPERT_BLOCK) + N_EXPERTS
    cap = n_blocks * EXPERT_BLOCK
    buf_tok = jnp.full((cap,), T, jnp.int32).at[dest].set(flat_tok[order])
    buf_gate = jnp.zeros((cap,), jnp.float32).at[dest].set(flat_gate[order])
    block_starts = jnp.arange(n_blocks, dtype=jnp.int32) * EXPERT_BLOCK
    block_e = jnp.minimum(jnp.searchsorted(pad_ends, block_starts, side='right'), N_EXPERTS - 1)
    h_pad = jnp.concatenate([ht, jnp.zeros((1, D), ht.dtype)], axis=0)
    xb = h_pad[buf_tok].reshape(n_blocks, EXPERT_BLOCK, D)

    def expert_block(args):
        xblk, e = args
        gu = xblk @ w_gate_up[e] + b_gate_up[e]
        gate = jnp.minimum(gu[:, :D_EXPERT], SWIGLU_LIMIT)
        up = jnp.clip(gu[:, D_EXPERT:], -SWIGLU_LIMIT, SWIGLU_LIMIT)
        act = gate * jax.nn.sigmoid(SWIGLU_ALPHA * gate) * (up + 1)
        return act @ w_down[e] + b_down[e]

    yb = lax.map(expert_block, (xb, block_e))
    y = yb.reshape(cap, D) * buf_gate[:, None].astype(yb.dtype)
    out = jax.ops.segment_sum(y, buf_tok, num_segments=T + 1)[:T]
    return out.reshape(B, S, D)


def setup_inputs(seed: int = 0):
    key = jax.random.key(seed)
    ks = jax.random.split(key, 32)
    f32 = jnp.float32
    L = DEPTH
    nrm = lambda k, shape, s: jax.random.normal(k, shape, f32) * s
    gain = lambda k, shape: 1.0 + 0.05 * jax.random.normal(k, shape, f32)
    return {
        'x': nrm(ks[0], (BATCH, SEQ, D_MODEL), 1.0),
        'c': nrm(ks[1], (BATCH, D_MODEL), 1.0),
        'w_ada': nrm(ks[2], (L, D_MODEL, N_MOD * D_MODEL), D_MODEL ** -0.5),
        'b_ada': nrm(ks[3], (L, N_MOD * D_MODEL), 0.02),
        'norm1_g': gain(ks[4], (L, D_MODEL)),
        'w_in': nrm(ks[5], (L, D_MODEL, D_IN), D_MODEL ** -0.5),
        'mu_shift': jax.random.uniform(ks[6], (L, RWKV_COLS), f32),
        'w0': jax.random.uniform(ks[7], (L, D_RWKV), f32, -6.0, -1.0),
        'w2': nrm(ks[8], (L, DECAY_LORA, D_RWKV), 0.5 * DECAY_LORA ** -0.5),
        'a0': nrm(ks[9], (L, D_RWKV), 0.5),
        'a2': nrm(ks[10], (L, AAA_LORA, D_RWKV), AAA_LORA ** -0.5),
        'g2': nrm(ks[11], (L, GATE_LORA, D_RWKV), GATE_LORA ** -0.5),
        'k_k': 0.85 + nrm(ks[12], (L, D_RWKV), 0.05),
        'k_a': gain(ks[13], (L, D_RWKV)),
        'r_k': nrm(ks[14], (L, N_RWKV_HEADS, HEAD_DIM), 0.1),
        'gn_w': gain(ks[15], (L, D_RWKV)),
        'gn_b': nrm(ks[16], (L, D_RWKV), 0.02),
        'b_f': jax.random.uniform(ks[17], (L, N_FOX_HEADS), f32, 1.0, 5.0),
        'q_norm_g': gain(ks[18], (L, HEAD_DIM)),
        'k_norm_g': gain(ks[19], (L, HEAD_DIM)),
        'o_norm_g': gain(ks[20], (L, HEAD_DIM)),
        'w_out': nrm(ks[21], (L, D_MIX, D_MODEL), D_MIX ** -0.5),
        'norm2_g': gain(ks[22], (L, D_MODEL)),
        'w_router': nrm(ks[23], (L, D_MODEL, N_EXPERTS), D_MODEL ** -0.5),
        'b_router': nrm(ks[24], (L, N_EXPERTS), 0.01),
        'w_gate_up': nrm(ks[25], (L, N_EXPERTS, D_MODEL, 2 * D_EXPERT), D_MODEL ** -0.5),
        'b_gate_up': nrm(ks[26], (L, N_EXPERTS, 2 * D_EXPERT), 0.02),
        'w_down': nrm(ks[27], (L, N_EXPERTS, D_EXPERT, D_MODEL), D_EXPERT ** -0.5),
        'b_down': nrm(ks[28], (L, N_EXPERTS, D_MODEL), 0.02),
        'final_g': gain(ks[29], (D_MODEL,)),
    }


def reference(x, c, w_ada, b_ada, norm1_g, w_in, mu_shift, w0, w2, a0, a2, g2, k_k, k_a,
              r_k, gn_w, gn_b, b_f, q_norm_g, k_norm_g, o_norm_g, w_out, norm2_g,
              w_router, b_router, w_gate_up, b_gate_up, w_down, b_down, final_g):
    c_act = jax.nn.silu(c)
    for l in range(DEPTH):
        mod = c_act @ w_ada[l] + b_ada[l]
        shift1, scale1, gate1, shift2, scale2, gate2 = jnp.split(mod, N_MOD, axis=-1)
        h = modulate(rms_norm(x, norm1_g[l]), shift1, scale1)
        p = h @ w_in[l]
        y_rwkv = rwkv7_mix(p[..., :RWKV_COLS], mu_shift[l], w0[l], w2[l], a0[l], a2[l], g2[l],
                           k_k[l], k_a[l], r_k[l], gn_w[l], gn_b[l])
        y_fox = fox_mix(p[..., RWKV_COLS:], b_f[l], q_norm_g[l], k_norm_g[l], o_norm_g[l])
        y = jnp.concatenate([y_rwkv, y_fox], axis=-1) @ w_out[l]
        x = x + gate1[:, None, :] * y
        h = modulate(rms_norm(x, norm2_g[l]), shift2, scale2)
        x = x + gate2[:, None, :] * moe_ffn(h, w_router[l], b_router[l], w_gate_up[l],
                                            b_gate_up[l], w_down[l], b_down[l])
    return rms_norm(x, final_g)
```

```python
import functools

import jax
import jax.numpy as jnp
from jax import lax
from jax.experimental import pallas as pl
from jax.experimental.pallas import tpu as pltpu

F32 = jnp.float32
BF16 = jnp.bfloat16
HIGHEST = lax.Precision.HIGHEST

D_MODEL = 1024
HEAD_DIM = 64
N_HEADS = 8
D_GRP = N_HEADS * HEAD_DIM
RWKV_COLS = 1792
LORA_OFF = 3 * D_GRP
GATE_OFF = LORA_OFF + 128
FOX_MAIN = 4 * D_GRP
N_EXPERTS = 32
TOP_K = 4
EXPERT_BLOCK = 256
SWIGLU_ALPHA = 1.702
SWIGLU_LIMIT = 7.0
NORM_EPS = 1e-6
GN_EPS = 64e-5
LANES = 128
CHUNK = 64
HEADS_PER_SCAN = 4
SCAN_W = HEADS_PER_SCAN * HEAD_DIM
VMEM_LIMIT = 56 * 1024 * 1024


def _dot(a, b):
    return jnp.dot(a.astype(BF16), b.astype(BF16), preferred_element_type=F32)


def _dot_nt(a, b):
    return lax.dot_general(a.astype(BF16), b.astype(BF16), (((1,), (1,)), ((), ())),
                           preferred_element_type=F32)


def _dot_tn(a, b):
    return lax.dot_general(a.astype(BF16), b.astype(BF16), (((0,), (0,)), ((), ())),
                           preferred_element_type=F32)


def _fdot(a, b):
    return jnp.dot(a, b, precision=HIGHEST, preferred_element_type=F32)


def _split_dot(x, m, terms=2, left=False):
    acc = None
    rem = x
    for _ in range(terms):
        part = rem.astype(BF16)
        rem = rem - part.astype(F32)
        d = (jnp.dot(m, part, preferred_element_type=F32) if left
             else jnp.dot(part, m, preferred_element_type=F32))
        acc = d if acc is None else acc + d
    return acc


def _iota(shape, dim):
    return lax.broadcasted_iota(jnp.int32, shape, dim)


def _seg_reduce_mat(n):
    return (_iota((n, LANES), 0) // HEAD_DIM == _iota((n, LANES), 1)).astype(BF16)


def _seg_expand_mat(n):
    return (_iota((LANES, n), 1) // HEAD_DIM == _iota((LANES, n), 0)).astype(BF16)


def _tri(n, strict):
    r, c = _iota((n, n), 0), _iota((n, n), 1)
    return ((r > c) if strict else (r >= c)).astype(BF16)


def _log_sigmoid(z):
    return jnp.minimum(z, 0.0) - jnp.log(1.0 + jnp.exp(-jnp.abs(z)))


def _sigmoid(z):
    return 1.0 / (1.0 + jnp.exp(-z))


def _adaln_kernel(c_ref, w_ref, b_ref, o_ref):
    c = c_ref[...]
    o_ref[...] = _fdot(c * _sigmoid(c), w_ref[...]) + b_ref[...]


def _adaln(c, w_ada, b_ada):
    bsz = c.shape[0]
    n_mod = w_ada.shape[1] // D_MODEL
    return pl.pallas_call(
        _adaln_kernel,
        grid=(n_mod,),
        in_specs=[pl.BlockSpec((bsz, D_MODEL), lambda j: (0, 0)),
                  pl.BlockSpec((D_MODEL, D_MODEL), lambda j: (0, j)),
                  pl.BlockSpec((1, D_MODEL), lambda j: (0, j))],
        out_specs=pl.BlockSpec((bsz, D_MODEL), lambda j: (0, j)),
        out_shape=jax.ShapeDtypeStruct((bsz, n_mod * D_MODEL), F32),
        name="adaln",
    )(c, w_ada, b_ada.reshape(1, -1))


def _inproj_kernel(x_ref, sh_ref, sc_ref, g_ref, wr_ref, wx_ref, wfh_ref, wfl_ref, bf_ref, qkg_ref,
                   pr_ref, px_ref, f_ref, carry_ref):
    @pl.when(pl.program_id(1) == 0)
    def _():
        carry_ref[...] = jnp.zeros_like(carry_ref)

    x = x_ref[...]
    tm = x.shape[0]
    h = x * lax.rsqrt(jnp.mean(x * x, axis=-1, keepdims=True) + NORM_EPS) * g_ref[...]
    h = h * (1.0 + sc_ref[...]) + sh_ref[...]
    hb = h.astype(BF16)
    h_lo = (h - hb.astype(F32)).astype(BF16)

    pr_ref[...] = jnp.dot(hb, wr_ref[...], preferred_element_type=F32).astype(BF16)

    px = jnp.dot(hb, wx_ref[...], preferred_element_type=F32)
    qk = px[:, :2 * D_GRP]
    ss = _split_dot(qk * qk, _seg_reduce_mat(2 * D_GRP))
    inv = lax.rsqrt(ss * (1.0 / HEAD_DIM) + NORM_EPS)
    qk = qk * _split_dot(inv, _seg_expand_mat(2 * D_GRP)) * qkg_ref[...]
    px_ref[:, :2 * D_GRP] = qk.astype(BF16)
    px_ref[:, 2 * D_GRP:] = px[:, 2 * D_GRP:].astype(BF16)

    z = (jnp.dot(hb, wfh_ref[...], preferred_element_type=F32)
         + jnp.dot(h_lo, wfh_ref[...], preferred_element_type=F32)
         + jnp.dot(hb, wfl_ref[...], preferred_element_type=F32)) + bf_ref[...]
    cum = _split_dot(_log_sigmoid(z), _tri(tm, False), terms=3, left=True) + carry_ref[...]
    f_ref[...] = cum
    carry_ref[...] = cum[tm - 1:tm, :]


def _inproj(x, shift, scale, g, w_r, w_x, w_f, b_f, qk_gain, tm):
    w_f_hi = w_f.astype(BF16)
    w_f_lo = (w_f - w_f_hi.astype(F32)).astype(BF16)
    bsz, seq, _ = x.shape
    const = lambda b, s: (0, 0)
    return pl.pallas_call(
        _inproj_kernel,
        grid=(bsz, seq // tm),
        in_specs=[pl.BlockSpec((None, tm, D_MODEL), lambda b, s: (b, s, 0)),
                  pl.BlockSpec((None, 1, D_MODEL), lambda b, s: (b, 0, 0)),
                  pl.BlockSpec((None, 1, D_MODEL), lambda b, s: (b, 0, 0)),
                  pl.BlockSpec((1, D_MODEL), const),
                  pl.BlockSpec((D_MODEL, RWKV_COLS), const),
                  pl.BlockSpec((D_MODEL, FOX_MAIN), const),
                  pl.BlockSpec((D_MODEL, LANES), const),
                  pl.BlockSpec((D_MODEL, LANES), const),
                  pl.BlockSpec((1, LANES), const),
                  pl.BlockSpec((1, 2 * D_GRP), const)],
        out_specs=[pl.BlockSpec((None, tm, RWKV_COLS), lambda b, s: (b, s, 0)),
                   pl.BlockSpec((None, tm, FOX_MAIN), lambda b, s: (b, s, 0)),
                   pl.BlockSpec((None, tm, LANES), lambda b, s: (b, s, 0))],
        out_shape=[jax.ShapeDtypeStruct((bsz, seq, RWKV_COLS), BF16),
                   jax.ShapeDtypeStruct((bsz, seq, FOX_MAIN), BF16),
                   jax.ShapeDtypeStruct((bsz, seq, LANES), F32)],
        scratch_shapes=[pltpu.VMEM((1, LANES), F32)],
        compiler_params=pltpu.CompilerParams(
            dimension_semantics=("parallel", "arbitrary"), vmem_limit_bytes=VMEM_LIMIT),
        name="inproj",
    )(x, shift, scale, g, w_r, w_x, w_f_hi, w_f_lo, b_f, qk_gain)


def _stack_heads(x, mask):
    return jnp.concatenate([x] * HEADS_PER_SCAN, axis=0) * mask


def _rwkv_kernel(p_ref, mu_ref, w0_ref, w2_ref, a0_ref, a2_ref, g2_ref, kk_ref, ka_ref, rk_ref,
                 gnw_ref, gnb_ref, o_ref, last_ref, state_ref):
    @pl.when(pl.program_id(1) == 0)
    def _():
        last_ref[...] = jnp.zeros_like(last_ref)
        state_ref[...] = jnp.zeros_like(state_ref)

    p = p_ref[...].astype(F32)
    prev = jnp.where(_iota((CHUNK, 1), 0) == 0, last_ref[...], pltpu.roll(p, 1, axis=0))
    last_ref[...] = p[CHUNK - 1:CHUNK, :]
    pf = p + mu_ref[...] * (prev - p)
    r = pf[:, 0:D_GRP]
    k = pf[:, D_GRP:2 * D_GRP]
    v = pf[:, 2 * D_GRP:3 * D_GRP]
    lora = pf[:, LORA_OFF:GATE_OFF]
    gd = pf[:, GATE_OFF:RWKV_COLS]

    wlog = w0_ref[...] + _dot(jnp.tanh(lora), w2_ref[...])
    neg = -wlog
    softplus = jnp.maximum(neg, 0.0) + jnp.log(1.0 + jnp.exp(-jnp.abs(neg)))
    ld = -jnp.exp(-softplus - 0.5)
    a = _sigmoid(a0_ref[...] + _dot(lora, a2_ref[...]))
    g = _dot(_sigmoid(gd), g2_ref[...])

    red, exp_m = _seg_reduce_mat(D_GRP), _seg_expand_mat(D_GRP)
    kk = k * kk_ref[...]
    n2 = _split_dot(kk * kk, red)
    kk = kk * _split_dot(1.0 / jnp.maximum(jnp.sqrt(n2), 1e-12), exp_m)
    k2 = k * (1.0 + (a - 1.0) * ka_ref[...])

    cl = _split_dot(ld, _tri(CHUNK, False), terms=3, left=True)
    e_in = jnp.exp(cl)
    e_out = jnp.exp(-cl)
    e_rem = jnp.exp(cl[CHUNK - 1:CHUNK, :] - cl)
    alpha = -kk * jnp.exp(cl - ld)
    kka = kk * a
    beta = kka * e_out
    kt = k2 * e_out
    rt = r * e_in
    beta_end = kka * e_rem
    k_end = k2 * e_rem
    p_end = e_in[CHUNK - 1:CHUNK, :]

    n = HEADS_PER_SCAN * CHUNK
    rr, cc = _iota((n, SCAN_W), 0), _iota((n, SCAN_W), 1)
    head_mask = (rr // CHUNK == cc // HEAD_DIM).astype(F32)
    ri, ci = _iota((n, n), 0), _iota((n, n), 1)
    same = ri // CHUNK == ci // CHUNK
    strict = same & (ri > ci)
    incl = same & (ri >= ci)
    eye = (ri == ci).astype(F32)

    ys = []
    for grp in range(N_HEADS // HEADS_PER_SCAN):
        sl = slice(grp * SCAN_W, (grp + 1) * SCAN_W)
        xa = _stack_heads(alpha[:, sl], head_mask)
        xb = _stack_heads(beta[:, sl], head_mask)
        xk = _stack_heads(kt[:, sl], head_mask)
        xr = _stack_heads(rt[:, sl], head_mask)
        xv = _stack_heads(v[:, sl], head_mask)
        xbe = _stack_heads(beta_end[:, sl], head_mask)
        xke = _stack_heads(k_end[:, sl], head_mask)
        s = state_ref[grp]

        nab = jnp.where(strict, _sdot_nt(xa, xb), 0.0)
        aak = jnp.where(strict, _sdot_nt(xa, xk), 0.0)
        arb = jnp.where(incl, _sdot_nt(xr, xb), 0.0)
        ark = jnp.where(incl, _sdot_nt(xr, xk), 0.0)
        t_inv = eye + nab
        pw = nab
        for _ in range(5):
            pw = _sdot(pw, pw)
            t_inv = t_inv + _sdot(t_inv, pw)
        sa = _sdot(t_inv, _sdot_nt(xa, s) + _sdot(aak, xv))
        y = _sdot_nt(xr, s) + _sdot(arb, sa) + _sdot(ark, xv)
        state_ref[grp] = s * p_end[:, sl] + _sdot_tn(sa, xbe) + _sdot_tn(xv, xke)
        ys.append(y[0:CHUNK] + y[CHUNK:2 * CHUNK] + y[2 * CHUNK:3 * CHUNK] + y[3 * CHUNK:4 * CHUNK])
    y = jnp.concatenate(ys, axis=1)

    mean = _split_dot(_split_dot(y, red) * (1.0 / HEAD_DIM), exp_m)
    d = y - mean
    var = _split_dot(d * d, red) * (1.0 / HEAD_DIM)
    yn = d * _split_dot(lax.rsqrt(var + GN_EPS), exp_m) * gnw_ref[...] + gnb_ref[...]
    bonus = _split_dot(_split_dot(r * k2 * rk_ref[...], red), exp_m) * v
    o_ref[...] = ((yn + bonus) * g).astype(BF16)


def _sdot(a, b):
    return jnp.dot(a, b, precision=HIGHEST, preferred_element_type=F32)


def _sdot_nt(a, b):
    return lax.dot_general(a, b, (((1,), (1,)), ((), ())), precision=HIGHEST,
                           preferred_element_type=F32)


def _sdot_tn(a, b):
    return lax.dot_general(a, b, (((0,), (0,)), ((), ())), precision=HIGHEST,
                           preferred_element_type=F32)


def _rwkv(p_r, mu, w0, w2p, a0, a2p, g2, k_k, k_a, r_k, gn_w, gn_b):
    bsz, seq, _ = p_r.shape
    const = lambda b, s: (0, 0)
    vec = pl.BlockSpec((1, D_GRP), const)
    return pl.pallas_call(
        _rwkv_kernel,
        grid=(bsz, seq // CHUNK),
        in_specs=[pl.BlockSpec((None, CHUNK, RWKV_COLS), lambda b, s: (b, s, 0)),
                  pl.BlockSpec((1, RWKV_COLS), const),
                  vec, pl.BlockSpec((LANES, D_GRP), const),
                  vec, pl.BlockSpec((LANES, D_GRP), const),
                  pl.BlockSpec((LANES, D_GRP), const),
                  vec, vec, vec, vec, vec],
        out_specs=pl.BlockSpec((None, CHUNK, D_GRP), lambda b, s: (b, s, 0)),
        out_shape=jax.ShapeDtypeStruct((bsz, seq, D_GRP), BF16),
        scratch_shapes=[pltpu.VMEM((1, RWKV_COLS), F32),
                        pltpu.VMEM((N_HEADS // HEADS_PER_SCAN, SCAN_W, SCAN_W), F32)],
        compiler_params=pltpu.CompilerParams(
            dimension_semantics=("parallel", "arbitrary"), vmem_limit_bytes=VMEM_LIMIT),
        name="rwkv",
    )(p_r, mu, w0, w2p, a0, a2p, g2, k_k, k_a, r_k, gn_w, gn_b)


def _fox_kernel(q_ref, k_ref, v_ref, og_ref, fc_ref, fr_ref, ong_ref, o_ref, *, tq):
    hp = pl.program_id(1)
    qi = pl.program_id(2)
    lane = _iota((1, LANES), 1)
    q = q_ref[...]
    fc = fc_ref[...]
    row = _iota((tq, tq), 0)
    col = _iota((tq, tq), 1)
    outs = []
    for hh in range(2):
        head = hp * 2 + hh
        in_head = (lane >= hh * HEAD_DIM) & (lane < (hh + 1) * HEAD_DIM)
        qh = jnp.where(in_head, q, jnp.zeros_like(q))
        f_q = jnp.sum(jnp.where(lane == head, fc, 0.0), axis=-1, keepdims=True)

        def tile(j, carry, masked):
            m, l, acc = carry
            start = pl.multiple_of(j * tq, tq)
            kt = k_ref[pl.ds(start, tq), :]
            vt = v_ref[pl.ds(start, tq), :]
            f_k = fr_ref[pl.ds(head, 1), pl.ds(start, tq)]
            s = lax.dot_general(qh, kt, (((1,), (1,)), ((), ())), preferred_element_type=F32)
            s = s + (f_q - f_k)
            if masked:
                s = jnp.where(row >= col, s, -jnp.inf)
            m_new = jnp.maximum(m, jnp.max(s, axis=-1, keepdims=True))
            alpha = jnp.exp(m - m_new)
            pr = jnp.exp(s - m_new)
            l = alpha * l + jnp.sum(pr, axis=-1, keepdims=True)
            acc = alpha * acc + jnp.dot(pr.astype(BF16), vt, preferred_element_type=F32)
            return m_new, l, acc

        init = (jnp.full((tq, 1), -jnp.inf, F32), jnp.zeros((tq, 1), F32),
                jnp.zeros((tq, LANES), F32))
        carry = lax.fori_loop(0, qi, functools.partial(tile, masked=False), init)
        m, l, acc = tile(qi, carry, True)
        outs.append((in_head, acc / l))
    o = jnp.where(outs[0][0], outs[0][1], outs[1][1])
    blk = (_iota((LANES, LANES), 0) // HEAD_DIM == _iota((LANES, LANES), 1) // HEAD_DIM).astype(BF16)
    ms = _split_dot(o * o, blk) * (1.0 / HEAD_DIM)
    o = o * lax.rsqrt(ms + NORM_EPS) * ong_ref[...]
    o_ref[...] = (o * _sigmoid(og_ref[...].astype(F32))).astype(BF16)


def _fox(p_x, f_cum, f_rows, o_gain, tq):
    bsz, seq, _ = p_x.shape
    npair = N_HEADS // 2
    return pl.pallas_call(
        functools.partial(_fox_kernel, tq=tq),
        grid=(bsz, npair, seq // tq),
        in_specs=[pl.BlockSpec((None, tq, LANES), lambda b, h, i: (b, i, h)),
                  pl.BlockSpec((None, seq, LANES), lambda b, h, i: (b, 0, npair + h)),
                  pl.BlockSpec((None, seq, LANES), lambda b, h, i: (b, 0, 2 * npair + h)),
                  pl.BlockSpec((None, tq, LANES), lambda b, h, i: (b, i, 3 * npair + h)),
                  pl.BlockSpec((None, tq, LANES), lambda b, h, i: (b, i, 0)),
                  pl.BlockSpec((None, N_HEADS, seq), lambda b, h, i: (b, 0, 0)),
                  pl.BlockSpec((1, LANES), lambda b, h, i: (0, 0))],
        out_specs=pl.BlockSpec((None, tq, LANES), lambda b, h, i: (b, i, h)),
        out_shape=jax.ShapeDtypeStruct((bsz, seq, D_GRP), BF16),
        compiler_params=pltpu.CompilerParams(
            dimension_semantics=("parallel", "parallel", "arbitrary"),
            vmem_limit_bytes=VMEM_LIMIT),
        name="fox",
    )(p_x, p_x, p_x, p_x, f_cum, f_rows, o_gain)


def _outproj_kernel(x_ref, yr_ref, yf_ref, g1_ref, sh_ref, sc_ref, ng_ref, wor_ref, wof_ref,
                    wrt_ref, brt_ref, x1_ref, h2_ref, idx_ref, gate_ref, rank_ref, cnt_ref,
                    carry_ref):
    @pl.when(pl.program_id(0) == 0)
    def _():
        carry_ref[...] = jnp.zeros_like(carry_ref)

    y = (jnp.dot(yr_ref[...], wor_ref[...], preferred_element_type=F32)
         + jnp.dot(yf_ref[...], wof_ref[...], preferred_element_type=F32))
    x1 = x_ref[...] + g1_ref[...] * y
    x1_ref[...] = x1
    tm = x1.shape[0]
    h = x1 * lax.rsqrt(jnp.mean(x1 * x1, axis=-1, keepdims=True) + NORM_EPS) * ng_ref[...]
    h2 = h * (1.0 + sc_ref[...]) + sh_ref[...]
    h2_ref[...] = h2

    lane = _iota((tm, LANES), 1)
    logits = _fdot(h2, wrt_ref[...]) + brt_ref[...]
    lg = jnp.where(lane < N_EXPERTS, logits, -jnp.inf)
    picks = []
    hot_sum = jnp.zeros((tm, LANES), F32)
    for _ in range(TOP_K):
        m = jnp.max(lg, axis=-1, keepdims=True)
        sel = jnp.min(jnp.where(lg == m, lane, LANES), axis=-1, keepdims=True)
        hot = lane == sel
        picks.append((m, sel, hot))
        hot_sum = hot_sum + hot.astype(F32)
        lg = jnp.where(hot, -jnp.inf, lg)
    es = [jnp.exp(m - picks[0][0]) for m, _, _ in picks]
    den = es[0] + es[1] + es[2] + es[3]

    before = jnp.dot(_tri(tm, True), hot_sum.astype(BF16), preferred_element_type=F32)
    before = before + carry_ref[...]
    idx_out = jnp.zeros((tm, LANES), jnp.int32)
    gate_out = jnp.zeros((tm, LANES), F32)
    rank_out = jnp.zeros((tm, LANES), jnp.int32)
    for kk, (m, sel, hot) in enumerate(picks):
        rk = jnp.sum(jnp.where(hot, before, 0.0), axis=-1, keepdims=True).astype(jnp.int32)
        idx_out = jnp.where(lane == kk, sel, idx_out)
        gate_out = jnp.where(lane == kk, es[kk] / den, gate_out)
        rank_out = jnp.where(lane == kk, rk, rank_out)
    idx_ref[...] = idx_out
    gate_ref[...] = gate_out
    rank_ref[...] = rank_out
    carry_ref[...] = carry_ref[...] + jnp.sum(hot_sum, axis=0, keepdims=True)
    cnt_ref[...] = carry_ref[...]


def _outproj(x2d, y_r, y_f, gate1, shift2, scale2, norm_g, wo_r, wo_f, w_rt, b_rt, tm, seq):
    t = x2d.shape[0]
    per_b = seq // tm
    const = lambda i: (0, 0)
    rows = lambda i: (i, 0)
    mod = pl.BlockSpec((None, 1, D_MODEL), lambda i: (i // per_b, 0, 0))
    return pl.pallas_call(
        _outproj_kernel,
        grid=(t // tm,),
        in_specs=[pl.BlockSpec((tm, D_MODEL), rows),
                  pl.BlockSpec((tm, D_GRP), rows),
                  pl.BlockSpec((tm, D_GRP), rows),
                  mod, mod, mod,
                  pl.BlockSpec((1, D_MODEL), const),
                  pl.BlockSpec((D_GRP, D_MODEL), const),
                  pl.BlockSpec((D_GRP, D_MODEL), const),
                  pl.BlockSpec((D_MODEL, LANES), const),
                  pl.BlockSpec((1, LANES), const)],
        out_specs=[pl.BlockSpec((tm, D_MODEL), rows),
                   pl.BlockSpec((tm, D_MODEL), rows),
                   pl.BlockSpec((tm, LANES), rows),
                   pl.BlockSpec((tm, LANES), rows),
                   pl.BlockSpec((tm, LANES), rows),
                   pl.BlockSpec((1, LANES), const)],
        out_shape=[jax.ShapeDtypeStruct((t, D_MODEL), F32),
                   jax.ShapeDtypeStruct((t, D_MODEL), F32),
                   jax.ShapeDtypeStruct((t, LANES), jnp.int32),
                   jax.ShapeDtypeStruct((t, LANES), F32),
                   jax.ShapeDtypeStruct((t, LANES), jnp.int32),
                   jax.ShapeDtypeStruct((1, LANES), F32)],
        scratch_shapes=[pltpu.VMEM((1, LANES), F32)],
        compiler_params=pltpu.CompilerParams(
            dimension_semantics=("arbitrary",), vmem_limit_bytes=VMEM_LIMIT),
        name="outproj",
    )(x2d, y_r, y_f, gate1, shift2, scale2, norm_g, wo_r, wo_f, w_rt, b_rt)


GATHER_ROWS = 1024


def _row_copy(src_hbm, dst, src_row, dst_row, sem):
    return pltpu.make_async_copy(src_hbm.at[pl.ds(src_row, 1), :], dst.at[pl.ds(dst_row, 1), :], sem)


def _gather_kernel(idx_hbm, src_hbm, o_ref, idx_smem, isem, sem):
    i = pl.program_id(0)
    cp = pltpu.make_async_copy(idx_hbm.at[i], idx_smem, isem)
    cp.start()
    cp.wait()

    def issue(r, c):
        _row_copy(src_hbm, o_ref, idx_smem[r], r, sem).start()
        return c

    lax.fori_loop(0, GATHER_ROWS, issue, 0)

    def drain(r, c):
        _row_copy(src_hbm, o_ref, 0, r, sem).wait()
        return c

    lax.fori_loop(0, GATHER_ROWS, drain, 0)


def _gather_rows(idx2d, src):
    nblk = idx2d.shape[0]
    d = src.shape[1]
    return pl.pallas_call(
        _gather_kernel,
        grid=(nblk,),
        in_specs=[pl.BlockSpec(memory_space=pl.ANY), pl.BlockSpec(memory_space=pl.ANY)],
        out_specs=pl.BlockSpec((GATHER_ROWS, d), lambda i: (i, 0)),
        out_shape=jax.ShapeDtypeStruct((nblk * GATHER_ROWS, d), src.dtype),
        scratch_shapes=[pltpu.SMEM((GATHER_ROWS,), jnp.int32),
                        pltpu.SemaphoreType.DMA, pltpu.SemaphoreType.DMA],
        compiler_params=pltpu.CompilerParams(
            dimension_semantics=("arbitrary",), vmem_limit_bytes=VMEM_LIMIT),
        name="gather",
    )(idx2d, src)


def _expert_kernel(be_ref, x_ref, wgu_ref, bgu_ref, wd_ref, bd_ref, o_ref):
    del be_ref
    gu = jnp.dot(x_ref[...].astype(BF16), wgu_ref[...], preferred_element_type=F32) + bgu_ref[...]
    gate = jnp.minimum(gu[:, :D_MODEL], SWIGLU_LIMIT)
    up = jnp.clip(gu[:, D_MODEL:], -SWIGLU_LIMIT, SWIGLU_LIMIT)
    act = gate * _sigmoid(SWIGLU_ALPHA * gate) * (up + 1.0)
    o_ref[...] = jnp.dot(act.astype(BF16), wd_ref[...], preferred_element_type=F32) + bd_ref[...]


def _experts(block_e, xs, w_gu, b_gu, w_d, b_d):
    n_blocks = block_e.shape[0]
    grid_spec = pltpu.PrefetchScalarGridSpec(
        num_scalar_prefetch=1,
        grid=(n_blocks,),
        in_specs=[pl.BlockSpec((EXPERT_BLOCK, D_MODEL), lambda j, be: (j, 0)),
                  pl.BlockSpec((None, D_MODEL, 2 * D_MODEL), lambda j, be: (be[j], 0, 0)),
                  pl.BlockSpec((None, 1, 2 * D_MODEL), lambda j, be: (be[j], 0, 0)),
                  pl.BlockSpec((None, D_MODEL, D_MODEL), lambda j, be: (be[j], 0, 0)),
                  pl.BlockSpec((None, 1, D_MODEL), lambda j, be: (be[j], 0, 0))],
        out_specs=pl.BlockSpec((EXPERT_BLOCK, D_MODEL), lambda j, be: (j, 0)),
    )
    return pl.pallas_call(
        _expert_kernel,
        grid_spec=grid_spec,
        out_shape=jax.ShapeDtypeStruct(xs.shape, F32),
        compiler_params=pltpu.CompilerParams(
            dimension_semantics=("arbitrary",), vmem_limit_bytes=VMEM_LIMIT),
        name="experts",
    )(block_e, xs, w_gu, b_gu, w_d, b_d)


COMBINE_TOKENS = GATHER_ROWS // TOP_K


def _combine_kernel(dest_hbm, yb_hbm, x1_ref, gate_ref, g2_ref, fg_ref, o_ref,
                    rows_ref, idx_smem, isem, sem):
    i = pl.program_id(0)
    cp = pltpu.make_async_copy(dest_hbm.at[i], idx_smem, isem)
    cp.start()
    cp.wait()

    def issue(r, c):
        _row_copy(yb_hbm, rows_ref, idx_smem[r], r, sem).start()
        return c

    lax.fori_loop(0, GATHER_ROWS, issue, 0)

    def drain(r, c):
        _row_copy(yb_hbm, rows_ref, 0, r, sem).wait()
        return c

    lax.fori_loop(0, GATHER_ROWS, drain, 0)

    gates = gate_ref[...]
    acc = None
    for kk in range(TOP_K):
        part = gates[:, kk:kk + 1] * rows_ref[kk * COMBINE_TOKENS:(kk + 1) * COMBINE_TOKENS, :]
        acc = part if acc is None else acc + part
    x2 = x1_ref[...] + g2_ref[...] * acc
    o_ref[...] = x2 * lax.rsqrt(jnp.mean(x2 * x2, axis=-1, keepdims=True) + NORM_EPS) * fg_ref[...]


def _combine(dest2d, yb, x1, gates, gate2, final_g, seq):
    t = x1.shape[0]
    tm = COMBINE_TOKENS
    per_b = seq // tm
    rows = lambda i: (i, 0)
    return pl.pallas_call(
        _combine_kernel,
        grid=(t // tm,),
        in_specs=[pl.BlockSpec(memory_space=pl.ANY), pl.BlockSpec(memory_space=pl.ANY),
                  pl.BlockSpec((tm, D_MODEL), rows),
                  pl.BlockSpec((tm, LANES), rows),
                  pl.BlockSpec((None, 1, D_MODEL), lambda i: (i // per_b, 0, 0)),
                  pl.BlockSpec((1, D_MODEL), lambda i: (0, 0))],
        out_specs=pl.BlockSpec((tm, D_MODEL), rows),
        out_shape=jax.ShapeDtypeStruct((t, D_MODEL), F32),
        scratch_shapes=[pltpu.VMEM((GATHER_ROWS, D_MODEL), F32),
                        pltpu.SMEM((GATHER_ROWS,), jnp.int32),
                        pltpu.SemaphoreType.DMA, pltpu.SemaphoreType.DMA],
        compiler_params=pltpu.CompilerParams(
            dimension_semantics=("arbitrary",), vmem_limit_bytes=VMEM_LIMIT),
        name="combine",
    )(dest2d, yb, x1, gates, gate2, final_g)


def _moe(h2, idx, gates, rank, counts, x1, gate2, final_g, w_gu, b_gu, w_d, b_d, seq):
    t = h2.shape[0]
    n_slots = t * TOP_K
    n_blocks = -(-n_slots // EXPERT_BLOCK) + N_EXPERTS
    cap = n_blocks * EXPERT_BLOCK
    assert cap % GATHER_ROWS == 0 and n_slots % GATHER_ROWS == 0
    padded = (counts + EXPERT_BLOCK - 1) // EXPERT_BLOCK * EXPERT_BLOCK
    pad_ends = jnp.cumsum(padded)
    pad_starts = pad_ends - padded
    dest = pad_starts[idx] + rank
    tok = jnp.broadcast_to(jnp.arange(t, dtype=jnp.int32)[:, None], (t, TOP_K))
    buf_tok = jnp.zeros((cap,), jnp.int32).at[dest.reshape(-1)].set(tok.reshape(-1))
    block_starts = jnp.arange(n_blocks, dtype=jnp.int32) * EXPERT_BLOCK
    block_e = jnp.minimum(jnp.searchsorted(pad_ends, block_starts, side="right"),
                          N_EXPERTS - 1).astype(jnp.int32)

    xs = _gather_rows(buf_tok.reshape(-1, GATHER_ROWS), h2)
    yb = _experts(block_e, xs, w_gu, b_gu, w_d, b_d)
    dest_blocks = dest.reshape(-1, COMBINE_TOKENS, TOP_K).transpose(0, 2, 1).reshape(-1, GATHER_ROWS)
    return _combine(dest_blocks, yb, x1, gates, gate2, final_g, seq)


def _layer(x, c_mod, norm1_g, w_in, mu_shift, w0, w2, a0, a2, g2, k_k, k_a, r_k, gn_w, gn_b, b_f,
           q_norm_g, k_norm_g, o_norm_g, w_out, norm2_g, w_router, b_router, w_gate_up,
           b_gate_up, w_down, b_down, final_g, tm_in, tq, tm_out):
    bsz, seq, _ = x.shape
    shift1, scale1, gate1, shift2, scale2, gate2 = (
        m.reshape(bsz, 1, D_MODEL) for m in jnp.split(c_mod, 6, axis=-1))
    row = lambda v: v.reshape(1, -1)

    w_r = w_in[:, :RWKV_COLS].astype(BF16)
    w_x = w_in[:, RWKV_COLS:RWKV_COLS + FOX_MAIN].astype(BF16)
    w_f = jnp.pad(w_in[:, RWKV_COLS + FOX_MAIN:], ((0, 0), (0, LANES - N_HEADS)))
    b_fp = jnp.pad(b_f, (0, LANES - N_HEADS)).reshape(1, LANES)
    qk_gain = jnp.concatenate([jnp.tile(q_norm_g, N_HEADS) * HEAD_DIM ** -0.5,
                               jnp.tile(k_norm_g, N_HEADS)]).reshape(1, -1)
    p_r, p_x, f_cum = _inproj(x, shift1, scale1, row(norm1_g), w_r, w_x, w_f, b_fp, qk_gain, tm_in)

    zeros = jnp.zeros((LANES - 64, D_GRP), F32)
    w2p = jnp.concatenate([w2, zeros], axis=0).astype(BF16)
    a2p = jnp.concatenate([zeros, a2], axis=0).astype(BF16)
    y_r = _rwkv(p_r, row(mu_shift), row(w0), w2p, row(a0), a2p, g2.astype(BF16), row(k_k),
                row(k_a), row(r_k), row(gn_w), row(gn_b))

    f_rows = jnp.transpose(f_cum[:, :, :N_HEADS], (0, 2, 1))
    y_f = _fox(p_x, f_cum, f_rows, jnp.tile(o_norm_g, 2).reshape(1, LANES), tq)

    t = bsz * seq
    w_rt = jnp.pad(w_router, ((0, 0), (0, LANES - N_EXPERTS)))
    b_rt = jnp.pad(b_router, (0, LANES - N_EXPERTS)).reshape(1, LANES)
    wo = w_out.astype(BF16)
    x1, h2, idx, gates, rank, cnt = _outproj(
        x.reshape(t, D_MODEL), y_r.reshape(t, D_GRP), y_f.reshape(t, D_GRP), gate1, shift2,
        scale2, row(norm2_g), wo[:D_GRP], wo[D_GRP:], w_rt, b_rt, tm_out, seq)

    counts = cnt[0, :N_EXPERTS].astype(jnp.int32)
    out = _moe(h2, idx[:, :TOP_K], gates, rank[:, :TOP_K], counts, x1, gate2, row(final_g),
               w_gate_up.astype(BF16), b_gate_up.reshape(N_EXPERTS, 1, -1),
               w_down.astype(BF16), b_down.reshape(N_EXPERTS, 1, -1), seq)
    return out.reshape(bsz, seq, D_MODEL)


def kernel(x, c, w_ada, b_ada, norm1_g, w_in, mu_shift, w0, w2, a0, a2, g2, k_k, k_a, r_k, gn_w,
           gn_b, b_f, q_norm_g, k_norm_g, o_norm_g, w_out, norm2_g, w_router, b_router, w_gate_up,
           b_gate_up, w_down, b_down, final_g):
    assert w_ada.shape[0] == 1, "single-layer block"
    c_mod = _adaln(c, w_ada[0], b_ada[0])
    return _layer(x, c_mod, norm1_g[0], w_in[0], mu_shift[0], w0[0], w2[0], a0[0], a2[0], g2[0],
                  k_k[0], k_a[0], r_k[0], gn_w[0], gn_b[0], b_f[0], q_norm_g[0], k_norm_g[0],
                  o_norm_g[0], w_out[0], norm2_g[0], w_router[0], b_router[0], w_gate_up[0],
                  b_gate_up[0], w_down[0], b_down[0], final_g,
                  tm_in=min(512, x.shape[1]), tq=min(256, x.shape[1]), tm_out=min(512, x.shape[1]))
```

```python
import functools

import jax
import jax.numpy as jnp
from jax import lax
from jax.experimental import pallas as pl
from jax.experimental.pallas import tpu as pltpu

F32 = jnp.float32
BF16 = jnp.bfloat16
HIGHEST = lax.Precision.HIGHEST

D_MODEL = 1024
HEAD_DIM = 64
N_HEADS = 8
D_GRP = N_HEADS * HEAD_DIM
RWKV_COLS = 1792
LORA_OFF = 3 * D_GRP
GATE_OFF = LORA_OFF + 128
FOX_MAIN = 4 * D_GRP
N_EXPERTS = 32
TOP_K = 4
EXPERT_BLOCK = 256
SWIGLU_ALPHA = 1.702
SWIGLU_LIMIT = 7.0
NORM_EPS = 1e-6
GN_EPS = 64e-5
LANES = 128
CHUNK = 64
HEADS_PER_SCAN = 4
SCAN_W = HEADS_PER_SCAN * HEAD_DIM
VMEM_LIMIT = 56 * 1024 * 1024


def _dot(a, b):
    return jnp.dot(a.astype(BF16), b.astype(BF16), preferred_element_type=F32)


def _dot_nt(a, b):
    return lax.dot_general(a.astype(BF16), b.astype(BF16), (((1,), (1,)), ((), ())),
                           preferred_element_type=F32)


def _dot_tn(a, b):
    return lax.dot_general(a.astype(BF16), b.astype(BF16), (((0,), (0,)), ((), ())),
                           preferred_element_type=F32)


def _fdot(a, b):
    return jnp.dot(a, b, precision=HIGHEST, preferred_element_type=F32)


def _split_dot(x, m, terms=2, left=False):
    acc = None
    rem = x
    for _ in range(terms):
        part = rem.astype(BF16)
        rem = rem - part.astype(F32)
        d = (jnp.dot(m, part, preferred_element_type=F32) if left
             else jnp.dot(part, m, preferred_element_type=F32))
        acc = d if acc is None else acc + d
    return acc


def _iota(shape, dim):
    return lax.broadcasted_iota(jnp.int32, shape, dim)


def _seg_reduce_mat(n):
    return (_iota((n, LANES), 0) // HEAD_DIM == _iota((n, LANES), 1)).astype(BF16)


def _seg_expand_mat(n):
    return (_iota((LANES, n), 1) // HEAD_DIM == _iota((LANES, n), 0)).astype(BF16)


def _tri(n, strict):
    r, c = _iota((n, n), 0), _iota((n, n), 1)
    return ((r > c) if strict else (r >= c)).astype(BF16)


def _log_sigmoid(z):
    return jnp.minimum(z, 0.0) - jnp.log(1.0 + jnp.exp(-jnp.abs(z)))


def _sigmoid(z):
    return 1.0 / (1.0 + jnp.exp(-z))


def _adaln_kernel(c_ref, w_ref, b_ref, o_ref):
    c = c_ref[...]
    o_ref[...] = _fdot(c * _sigmoid(c), w_ref[...]) + b_ref[...]


def _adaln(c, w_ada, b_ada):
    bsz = c.shape[0]
    n_mod = w_ada.shape[1] // D_MODEL
    return pl.pallas_call(
        _adaln_kernel,
        grid=(n_mod,),
        in_specs=[pl.BlockSpec((bsz, D_MODEL), lambda j: (0, 0)),
                  pl.BlockSpec((D_MODEL, D_MODEL), lambda j: (0, j)),
                  pl.BlockSpec((1, D_MODEL), lambda j: (0, j))],
        out_specs=pl.BlockSpec((bsz, D_MODEL), lambda j: (0, j)),
        out_shape=jax.ShapeDtypeStruct((bsz, n_mod * D_MODEL), F32),
        name="adaln",
    )(c, w_ada, b_ada.reshape(1, -1))


def _inproj_kernel(x_ref, sh_ref, sc_ref, g_ref, wr_ref, wx_ref, wfh_ref, wfl_ref, bf_ref, qkg_ref,
                   pr_ref, px_ref, f_ref, carry_ref):
    @pl.when(pl.program_id(1) == 0)
    def _():
        carry_ref[...] = jnp.zeros_like(carry_ref)

    x = x_ref[...]
    tm = x.shape[0]
    h = x * lax.rsqrt(jnp.mean(x * x, axis=-1, keepdims=True) + NORM_EPS) * g_ref[...]
    h = h * (1.0 + sc_ref[...]) + sh_ref[...]
    hb = h.astype(BF16)
    h_lo = (h - hb.astype(F32)).astype(BF16)

    pr_ref[...] = jnp.dot(hb, wr_ref[...], preferred_element_type=F32).astype(BF16)

    px = jnp.dot(hb, wx_ref[...], preferred_element_type=F32)
    qk = px[:, :2 * D_GRP]
    ss = _split_dot(qk * qk, _seg_reduce_mat(2 * D_GRP))
    inv = lax.rsqrt(ss * (1.0 / HEAD_DIM) + NORM_EPS)
    qk = qk * _split_dot(inv, _seg_expand_mat(2 * D_GRP)) * qkg_ref[...]
    px_ref[:, :2 * D_GRP] = qk.astype(BF16)
    px_ref[:, 2 * D_GRP:] = px[:, 2 * D_GRP:].astype(BF16)

    z = (jnp.dot(hb, wfh_ref[...], preferred_element_type=F32)
         + jnp.dot(h_lo, wfh_ref[...], preferred_element_type=F32)
         + jnp.dot(hb, wfl_ref[...], preferred_element_type=F32)) + bf_ref[...]
    cum = _split_dot(_log_sigmoid(z), _tri(tm, False), terms=3, left=True) + carry_ref[...]
    f_ref[...] = cum
    carry_ref[...] = cum[tm - 1:tm, :]


def _inproj(x, shift, scale, g, w_r, w_x, w_f, b_f, qk_gain, tm):
    w_f_hi = w_f.astype(BF16)
    w_f_lo = (w_f - w_f_hi.astype(F32)).astype(BF16)
    bsz, seq, _ = x.shape
    const = lambda b, s: (0, 0)
    return pl.pallas_call(
        _inproj_kernel,
        grid=(bsz, seq // tm),
        in_specs=[pl.BlockSpec((None, tm, D_MODEL), lambda b, s: (b, s, 0)),
                  pl.BlockSpec((None, 1, D_MODEL), lambda b, s: (b, 0, 0)),
                  pl.BlockSpec((None, 1, D_MODEL), lambda b, s: (b, 0, 0)),
                  pl.BlockSpec((1, D_MODEL), const),
                  pl.BlockSpec((D_MODEL, RWKV_COLS), const),
                  pl.BlockSpec((D_MODEL, FOX_MAIN), const),
                  pl.BlockSpec((D_MODEL, LANES), const),
                  pl.BlockSpec((D_MODEL, LANES), const),
                  pl.BlockSpec((1, LANES), const),
                  pl.BlockSpec((1, 2 * D_GRP), const)],
        out_specs=[pl.BlockSpec((None, tm, RWKV_COLS), lambda b, s: (b, s, 0)),
                   pl.BlockSpec((None, tm, FOX_MAIN), lambda b, s: (b, s, 0)),
                   pl.BlockSpec((None, tm, LANES), lambda b, s: (b, s, 0))],
        out_shape=[jax.ShapeDtypeStruct((bsz, seq, RWKV_COLS), BF16),
                   jax.ShapeDtypeStruct((bsz, seq, FOX_MAIN), BF16),
                   jax.ShapeDtypeStruct((bsz, seq, LANES), F32)],
        scratch_shapes=[pltpu.VMEM((1, LANES), F32)],
        compiler_params=pltpu.CompilerParams(
            dimension_semantics=("parallel", "arbitrary"), vmem_limit_bytes=VMEM_LIMIT),
        name="inproj",
    )(x, shift, scale, g, w_r, w_x, w_f_hi, w_f_lo, b_f, qk_gain)


def _stack_heads(x, mask):
    return jnp.concatenate([x] * HEADS_PER_SCAN, axis=0) * mask


def _rwkv_kernel(p_ref, mu_ref, w0_ref, w2_ref, a0_ref, a2_ref, g2_ref, kk_ref, ka_ref, rk_ref,
                 gnw_ref, gnb_ref, o_ref, last_ref, state_ref):
    @pl.when(pl.program_id(1) == 0)
    def _():
        last_ref[...] = jnp.zeros_like(last_ref)
        state_ref[...] = jnp.zeros_like(state_ref)

    p = p_ref[...].astype(F32)
    prev = jnp.where(_iota((CHUNK, 1), 0) == 0, last_ref[...], pltpu.roll(p, 1, axis=0))
    last_ref[...] = p[CHUNK - 1:CHUNK, :]
    pf = p + mu_ref[...] * (prev - p)
    r = pf[:, 0:D_GRP]
    k = pf[:, D_GRP:2 * D_GRP]
    v = pf[:, 2 * D_GRP:3 * D_GRP]
    lora = pf[:, LORA_OFF:GATE_OFF]
    gd = pf[:, GATE_OFF:RWKV_COLS]

    wlog = w0_ref[...] + _dot(jnp.tanh(lora), w2_ref[...])
    neg = -wlog
    softplus = jnp.maximum(neg, 0.0) + jnp.log(1.0 + jnp.exp(-jnp.abs(neg)))
    ld = -jnp.exp(-softplus - 0.5)
    a = _sigmoid(a0_ref[...] + _dot(lora, a2_ref[...]))
    g = _dot(_sigmoid(gd), g2_ref[...])

    red, exp_m = _seg_reduce_mat(D_GRP), _seg_expand_mat(D_GRP)
    kk = k * kk_ref[...]
    n2 = _split_dot(kk * kk, red)
    kk = kk * _split_dot(1.0 / jnp.maximum(jnp.sqrt(n2), 1e-12), exp_m)
    k2 = k * (1.0 + (a - 1.0) * ka_ref[...])

    cl = _split_dot(ld, _tri(CHUNK, False), terms=3, left=True)
    e_in = jnp.exp(cl)
    e_out = jnp.exp(-cl)
    e_rem = jnp.exp(cl[CHUNK - 1:CHUNK, :] - cl)
    alpha = -kk * jnp.exp(cl - ld)
    kka = kk * a
    beta = kka * e_out
    kt = k2 * e_out
    rt = r * e_in
    beta_end = kka * e_rem
    k_end = k2 * e_rem
    p_end = e_in[CHUNK - 1:CHUNK, :]

    n = HEADS_PER_SCAN * CHUNK
    rr, cc = _iota((n, SCAN_W), 0), _iota((n, SCAN_W), 1)
    head_mask = (rr // CHUNK == cc // HEAD_DIM).astype(F32)
    ri, ci = _iota((n, n), 0), _iota((n, n), 1)
    same = ri // CHUNK == ci // CHUNK
    strict = same & (ri > ci)
    incl = same & (ri >= ci)
    eye = (ri == ci).astype(F32)

    ys = []
    for grp in range(N_HEADS // HEADS_PER_SCAN):
        sl = slice(grp * SCAN_W, (grp + 1) * SCAN_W)
        xa = _stack_heads(alpha[:, sl], head_mask)
        xb = _stack_heads(beta[:, sl], head_mask)
        xk = _stack_heads(kt[:, sl], head_mask)
        xr = _stack_heads(rt[:, sl], head_mask)
        xv = _stack_heads(v[:, sl], head_mask)
        xbe = _stack_heads(beta_end[:, sl], head_mask)
        xke = _stack_heads(k_end[:, sl], head_mask)
        s = state_ref[grp]

        nab = jnp.where(strict, _mdot(xa, xb, _NT, P_NAB), 0.0)
        aak = jnp.where(strict, _mdot(xa, xk, _NT, P_A), 0.0)
        arb = jnp.where(incl, _mdot(xr, xb, _NT, P_A), 0.0)
        ark = jnp.where(incl, _mdot(xr, xk, _NT, P_A), 0.0)
        t_inv = eye + jnp.where(ri // 2 == ci // 2, nab, 0.0)
        blk = 2
        while blk < CHUNK:
            off = jnp.where((ri // (2 * blk) == ci // (2 * blk)) & (ri // blk != ci // blk), nab, 0.0)
            t_inv = t_inv + _mdot(_mdot(t_inv, off, _NN, P_INV), t_inv, _NN, P_INV)
            blk *= 2
        sa = _mdot(t_inv, _mdot(xa, s, _NT, P_STATE) + _mdot(aak, xv, _NN, P_APPLY), _NN, P_SOLVE)
        y = (_mdot(xr, s, _NT, P_STATE) + _mdot(arb, sa, _NN, P_APPLY)
             + _mdot(ark, xv, _NN, P_APPLY))
        state_ref[grp] = (s * p_end[:, sl] + _mdot(sa, xbe, _TN, P_STATE)
                          + _mdot(xv, xke, _TN, P_STATE))
        ys.append(y[0:CHUNK] + y[CHUNK:2 * CHUNK] + y[2 * CHUNK:3 * CHUNK] + y[3 * CHUNK:4 * CHUNK])
    y = jnp.concatenate(ys, axis=1)

    mean = _split_dot(_split_dot(y, red) * (1.0 / HEAD_DIM), exp_m)
    d = y - mean
    var = _split_dot(d * d, red) * (1.0 / HEAD_DIM)
    yn = d * _split_dot(lax.rsqrt(var + GN_EPS), exp_m) * gnw_ref[...] + gnb_ref[...]
    bonus = _split_dot(_split_dot(r * k2 * rk_ref[...], red), exp_m) * v
    o_ref[...] = ((yn + bonus) * g).astype(BF16)


_NN = (((1,), (0,)), ((), ()))
_NT = (((1,), (1,)), ((), ()))
_TN = (((0,), (0,)), ((), ()))
P_NAB = 1
P_A = 1
P_INV = 1
P_SOLVE = 1
P_APPLY = 1
P_STATE = 1


def _mdot(a, b, dims, passes):
    if passes == 6:
        return lax.dot_general(a, b, dims, precision=HIGHEST, preferred_element_type=F32)
    a_hi = a.astype(BF16)
    b_hi = b.astype(BF16)
    out = lax.dot_general(a_hi, b_hi, dims, preferred_element_type=F32)
    if passes == 3:
        a_lo = (a - a_hi.astype(F32)).astype(BF16)
        b_lo = (b - b_hi.astype(F32)).astype(BF16)
        out = (out + lax.dot_general(a_lo, b_hi, dims, preferred_element_type=F32)
               + lax.dot_general(a_hi, b_lo, dims, preferred_element_type=F32))
    return out


def _rwkv(p_r, mu, w0, w2p, a0, a2p, g2, k_k, k_a, r_k, gn_w, gn_b):
    bsz, seq, _ = p_r.shape
    const = lambda b, s: (0, 0)
    vec = pl.BlockSpec((1, D_GRP), const)
    return pl.pallas_call(
        _rwkv_kernel,
        grid=(bsz, seq // CHUNK),
        in_specs=[pl.BlockSpec((None, CHUNK, RWKV_COLS), lambda b, s: (b, s, 0)),
                  pl.BlockSpec((1, RWKV_COLS), const),
                  vec, pl.BlockSpec((LANES, D_GRP), const),
                  vec, pl.BlockSpec((LANES, D_GRP), const),
                  pl.BlockSpec((LANES, D_GRP), const),
                  vec, vec, vec, vec, vec],
        out_specs=pl.BlockSpec((None, CHUNK, D_GRP), lambda b, s: (b, s, 0)),
        out_shape=jax.ShapeDtypeStruct((bsz, seq, D_GRP), BF16),
        scratch_shapes=[pltpu.VMEM((1, RWKV_COLS), F32),
                        pltpu.VMEM((N_HEADS // HEADS_PER_SCAN, SCAN_W, SCAN_W), F32)],
        compiler_params=pltpu.CompilerParams(
            dimension_semantics=("parallel", "arbitrary"), vmem_limit_bytes=VMEM_LIMIT),
        name="rwkv",
    )(p_r, mu, w0, w2p, a0, a2p, g2, k_k, k_a, r_k, gn_w, gn_b)


def _fox_kernel(q_ref, k_ref, v_ref, og_ref, fc_ref, fr_ref, ong_ref, o_ref, *, tq):
    hp = pl.program_id(1)
    qi = pl.program_id(2)
    lane = _iota((1, LANES), 1)
    q = q_ref[...]
    fc = fc_ref[...]
    row = _iota((tq, tq), 0)
    col = _iota((tq, tq), 1)
    in_head = [(lane >= hh * HEAD_DIM) & (lane < (hh + 1) * HEAD_DIM) for hh in range(2)]
    qh = [jnp.where(msk, q, jnp.zeros_like(q)) for msk in in_head]
    f_q = [jnp.sum(jnp.where(lane == hp * 2 + hh, fc, 0.0), axis=-1, keepdims=True)
           for hh in range(2)]

    def tile(j, carry, masked):
        start = pl.multiple_of(j * tq, tq)
        kt = k_ref[pl.ds(start, tq), :]
        vt = v_ref[pl.ds(start, tq), :]
        new = []
        for hh in range(2):
            m, l, acc = carry[hh]
            f_k = fr_ref[pl.ds(hp * 2 + hh, 1), pl.ds(start, tq)]
            s = lax.dot_general(qh[hh], kt, _NT, preferred_element_type=F32)
            s = s + (f_q[hh] - f_k)
            if masked:
                s = jnp.where(row >= col, s, -jnp.inf)
            m_new = jnp.maximum(m, jnp.max(s, axis=-1, keepdims=True))
            alpha = jnp.exp(m - m_new)
            pr = jnp.exp(s - m_new)
            l = alpha * l + jnp.sum(pr, axis=-1, keepdims=True)
            acc = alpha * acc + jnp.dot(pr.astype(BF16), vt, preferred_element_type=F32)
            new.append((m_new, l, acc))
        return tuple(new)

    init = (jnp.full((tq, 1), -jnp.inf, F32), jnp.zeros((tq, 1), F32),
            jnp.zeros((tq, LANES), F32))
    carry = lax.fori_loop(0, qi, functools.partial(tile, masked=False), (init, init))
    (_, l0, acc0), (_, l1, acc1) = tile(qi, carry, True)
    o = jnp.where(in_head[0], acc0 / l0, acc1 / l1)
    blk = (_iota((LANES, LANES), 0) // HEAD_DIM == _iota((LANES, LANES), 1) // HEAD_DIM).astype(BF16)
    ms = _split_dot(o * o, blk) * (1.0 / HEAD_DIM)
    o = o * lax.rsqrt(ms + NORM_EPS) * ong_ref[...]
    o_ref[...] = (o * _sigmoid(og_ref[...].astype(F32))).astype(BF16)


def _fox(p_x, f_cum, f_rows, o_gain, tq):
    bsz, seq, _ = p_x.shape
    npair = N_HEADS // 2
    return pl.pallas_call(
        functools.partial(_fox_kernel, tq=tq),
        grid=(bsz, npair, seq // tq),
        in_specs=[pl.BlockSpec((None, tq, LANES), lambda b, h, i: (b, i, h)),
                  pl.BlockSpec((None, seq, LANES), lambda b, h, i: (b, 0, npair + h)),
                  pl.BlockSpec((None, seq, LANES), lambda b, h, i: (b, 0, 2 * npair + h)),
                  pl.BlockSpec((None, tq, LANES), lambda b, h, i: (b, i, 3 * npair + h)),
                  pl.BlockSpec((None, tq, LANES), lambda b, h, i: (b, i, 0)),
                  pl.BlockSpec((None, N_HEADS, seq), lambda b, h, i: (b, 0, 0)),
                  pl.BlockSpec((1, LANES), lambda b, h, i: (0, 0))],
        out_specs=pl.BlockSpec((None, tq, LANES), lambda b, h, i: (b, i, h)),
        out_shape=jax.ShapeDtypeStruct((bsz, seq, D_GRP), BF16),
        compiler_params=pltpu.CompilerParams(
            dimension_semantics=("parallel", "parallel", "arbitrary"),
            vmem_limit_bytes=VMEM_LIMIT),
        name="fox",
    )(p_x, p_x, p_x, p_x, f_cum, f_rows, o_gain)


def _outproj_kernel(x_ref, yr_ref, yf_ref, g1_ref, sh_ref, sc_ref, ng_ref, wor_ref, wof_ref,
                    wrt_ref, brt_ref, x1_ref, h2_ref, idx_ref, gate_ref, rank_ref, cnt_ref,
                    carry_ref):
    @pl.when(pl.program_id(0) == 0)
    def _():
        carry_ref[...] = jnp.zeros_like(carry_ref)

    y = (jnp.dot(yr_ref[...], wor_ref[...], preferred_element_type=F32)
         + jnp.dot(yf_ref[...], wof_ref[...], preferred_element_type=F32))
    x1 = x_ref[...] + g1_ref[...] * y
    x1_ref[...] = x1
    tm = x1.shape[0]
    h = x1 * lax.rsqrt(jnp.mean(x1 * x1, axis=-1, keepdims=True) + NORM_EPS) * ng_ref[...]
    h2 = h * (1.0 + sc_ref[...]) + sh_ref[...]
    h2_ref[...] = h2

    lane = _iota((tm, LANES), 1)
    logits = _fdot(h2, wrt_ref[...]) + brt_ref[...]
    lg = jnp.where(lane < N_EXPERTS, logits, -jnp.inf)
    picks = []
    hot_sum = jnp.zeros((tm, LANES), F32)
    for _ in range(TOP_K):
        m = jnp.max(lg, axis=-1, keepdims=True)
        sel = jnp.min(jnp.where(lg == m, lane, LANES), axis=-1, keepdims=True)
        hot = lane == sel
        picks.append((m, sel, hot))
        hot_sum = hot_sum + hot.astype(F32)
        lg = jnp.where(hot, -jnp.inf, lg)
    es = [jnp.exp(m - picks[0][0]) for m, _, _ in picks]
    den = es[0] + es[1] + es[2] + es[3]

    before = jnp.dot(_tri(tm, True), hot_sum.astype(BF16), preferred_element_type=F32)
    before = before + carry_ref[...]
    idx_out = jnp.zeros((tm, LANES), jnp.int32)
    gate_out = jnp.zeros((tm, LANES), F32)
    rank_out = jnp.zeros((tm, LANES), jnp.int32)
    for kk, (m, sel, hot) in enumerate(picks):
        rk = jnp.sum(jnp.where(hot, before, 0.0), axis=-1, keepdims=True).astype(jnp.int32)
        idx_out = jnp.where(lane == kk, sel, idx_out)
        gate_out = jnp.where(lane == kk, es[kk] / den, gate_out)
        rank_out = jnp.where(lane == kk, rk, rank_out)
    idx_ref[...] = idx_out
    gate_ref[...] = gate_out
    rank_ref[...] = rank_out
    carry_ref[...] = carry_ref[...] + jnp.sum(hot_sum, axis=0, keepdims=True)
    cnt_ref[...] = carry_ref[...]


def _outproj(x2d, y_r, y_f, gate1, shift2, scale2, norm_g, wo_r, wo_f, w_rt, b_rt, tm, seq):
    t = x2d.shape[0]
    per_b = seq // tm
    const = lambda i: (0, 0)
    rows = lambda i: (i, 0)
    mod = pl.BlockSpec((None, 1, D_MODEL), lambda i: (i // per_b, 0, 0))
    return pl.pallas_call(
        _outproj_kernel,
        grid=(t // tm,),
        in_specs=[pl.BlockSpec((tm, D_MODEL), rows),
                  pl.BlockSpec((tm, D_GRP), rows),
                  pl.BlockSpec((tm, D_GRP), rows),
                  mod, mod, mod,
                  pl.BlockSpec((1, D_MODEL), const),
                  pl.BlockSpec((D_GRP, D_MODEL), const),
                  pl.BlockSpec((D_GRP, D_MODEL), const),
                  pl.BlockSpec((D_MODEL, LANES), const),
                  pl.BlockSpec((1, LANES), const)],
        out_specs=[pl.BlockSpec((tm, D_MODEL), rows),
                   pl.BlockSpec((tm, D_MODEL), rows),
                   pl.BlockSpec((tm, LANES), rows),
                   pl.BlockSpec((tm, LANES), rows),
                   pl.BlockSpec((tm, LANES), rows),
                   pl.BlockSpec((1, LANES), const)],
        out_shape=[jax.ShapeDtypeStruct((t, D_MODEL), F32),
                   jax.ShapeDtypeStruct((t, D_MODEL), F32),
                   jax.ShapeDtypeStruct((t, LANES), jnp.int32),
                   jax.ShapeDtypeStruct((t, LANES), F32),
                   jax.ShapeDtypeStruct((t, LANES), jnp.int32),
                   jax.ShapeDtypeStruct((1, LANES), F32)],
        scratch_shapes=[pltpu.VMEM((1, LANES), F32)],
        compiler_params=pltpu.CompilerParams(
            dimension_semantics=("arbitrary",), vmem_limit_bytes=VMEM_LIMIT),
        name="outproj",
    )(x2d, y_r, y_f, gate1, shift2, scale2, norm_g, wo_r, wo_f, w_rt, b_rt)


GATHER_ROWS = 1024


def _row_copy(src_hbm, dst, src_row, dst_row, sem):
    return pltpu.make_async_copy(src_hbm.at[pl.ds(src_row, 1), :], dst.at[pl.ds(dst_row, 1), :], sem)


def _gather_kernel(idx_hbm, src_hbm, o_ref, idx_smem, isem, sem):
    i = pl.program_id(0)
    cp = pltpu.make_async_copy(idx_hbm.at[i], idx_smem, isem)
    cp.start()
    cp.wait()

    def issue(r, c):
        _row_copy(src_hbm, o_ref, idx_smem[r], r, sem).start()
        return c

    lax.fori_loop(0, GATHER_ROWS, issue, 0)

    pltpu.make_async_copy(src_hbm.at[pl.ds(0, GATHER_ROWS), :], o_ref, sem).wait()


def _gather_rows(idx2d, src):
    nblk = idx2d.shape[0]
    d = src.shape[1]
    return pl.pallas_call(
        _gather_kernel,
        grid=(nblk,),
        in_specs=[pl.BlockSpec(memory_space=pl.ANY), pl.BlockSpec(memory_space=pl.ANY)],
        out_specs=pl.BlockSpec((GATHER_ROWS, d), lambda i: (i, 0)),
        out_shape=jax.ShapeDtypeStruct((nblk * GATHER_ROWS, d), src.dtype),
        scratch_shapes=[pltpu.SMEM((GATHER_ROWS,), jnp.int32),
                        pltpu.SemaphoreType.DMA, pltpu.SemaphoreType.DMA],
        compiler_params=pltpu.CompilerParams(
            dimension_semantics=("arbitrary",), vmem_limit_bytes=VMEM_LIMIT),
        name="gather",
    )(idx2d, src)


def _expert_kernel(be_ref, x_ref, wgu_ref, bgu_ref, wd_ref, bd_ref, o_ref):
    del be_ref
    gu = jnp.dot(x_ref[...].astype(BF16), wgu_ref[...], preferred_element_type=F32) + bgu_ref[...]
    gate = jnp.minimum(gu[:, :D_MODEL], SWIGLU_LIMIT)
    up = jnp.clip(gu[:, D_MODEL:], -SWIGLU_LIMIT, SWIGLU_LIMIT)
    act = gate * _sigmoid(SWIGLU_ALPHA * gate) * (up + 1.0)
    o_ref[...] = jnp.dot(act.astype(BF16), wd_ref[...], preferred_element_type=F32) + bd_ref[...]


def _experts(block_e, xs, w_gu, b_gu, w_d, b_d):
    n_blocks = block_e.shape[0]
    grid_spec = pltpu.PrefetchScalarGridSpec(
        num_scalar_prefetch=1,
        grid=(n_blocks,),
        in_specs=[pl.BlockSpec((EXPERT_BLOCK, D_MODEL), lambda j, be: (j, 0)),
                  pl.BlockSpec((None, D_MODEL, 2 * D_MODEL), lambda j, be: (be[j], 0, 0)),
                  pl.BlockSpec((None, 1, 2 * D_MODEL), lambda j, be: (be[j], 0, 0)),
                  pl.BlockSpec((None, D_MODEL, D_MODEL), lambda j, be: (be[j], 0, 0)),
                  pl.BlockSpec((None, 1, D_MODEL), lambda j, be: (be[j], 0, 0))],
        out_specs=pl.BlockSpec((EXPERT_BLOCK, D_MODEL), lambda j, be: (j, 0)),
    )
    return pl.pallas_call(
        _expert_kernel,
        grid_spec=grid_spec,
        out_shape=jax.ShapeDtypeStruct(xs.shape, F32),
        compiler_params=pltpu.CompilerParams(
            dimension_semantics=("arbitrary",), vmem_limit_bytes=VMEM_LIMIT),
        name="experts",
    )(block_e, xs, w_gu, b_gu, w_d, b_d)


COMBINE_TOKENS = GATHER_ROWS // TOP_K


def _combine_kernel(dest_hbm, yb_hbm, x1_ref, gate_ref, g2_ref, fg_ref, o_ref,
                    rows_ref, idx_smem, isem, sem):
    i = pl.program_id(0)
    cp = pltpu.make_async_copy(dest_hbm.at[i], idx_smem, isem)
    cp.start()
    cp.wait()

    def issue(r, c):
        _row_copy(yb_hbm, rows_ref, idx_smem[r], r, sem).start()
        return c

    lax.fori_loop(0, GATHER_ROWS, issue, 0)

    pltpu.make_async_copy(yb_hbm.at[pl.ds(0, GATHER_ROWS), :], rows_ref, sem).wait()

    gates = gate_ref[...]
    acc = None
    for kk in range(TOP_K):
        part = gates[:, kk:kk + 1] * rows_ref[kk * COMBINE_TOKENS:(kk + 1) * COMBINE_TOKENS, :]
        acc = part if acc is None else acc + part
    x2 = x1_ref[...] + g2_ref[...] * acc
    o_ref[...] = x2 * lax.rsqrt(jnp.mean(x2 * x2, axis=-1, keepdims=True) + NORM_EPS) * fg_ref[...]


def _combine(dest2d, yb, x1, gates, gate2, final_g, seq):
    t = x1.shape[0]
    tm = COMBINE_TOKENS
    per_b = seq // tm
    rows = lambda i: (i, 0)
    return pl.pallas_call(
        _combine_kernel,
        grid=(t // tm,),
        in_specs=[pl.BlockSpec(memory_space=pl.ANY), pl.BlockSpec(memory_space=pl.ANY),
                  pl.BlockSpec((tm, D_MODEL), rows),
                  pl.BlockSpec((tm, LANES), rows),
                  pl.BlockSpec((None, 1, D_MODEL), lambda i: (i // per_b, 0, 0)),
                  pl.BlockSpec((1, D_MODEL), lambda i: (0, 0))],
        out_specs=pl.BlockSpec((tm, D_MODEL), rows),
        out_shape=jax.ShapeDtypeStruct((t, D_MODEL), F32),
        scratch_shapes=[pltpu.VMEM((GATHER_ROWS, D_MODEL), F32),
                        pltpu.SMEM((GATHER_ROWS,), jnp.int32),
                        pltpu.SemaphoreType.DMA, pltpu.SemaphoreType.DMA],
        compiler_params=pltpu.CompilerParams(
            dimension_semantics=("arbitrary",), vmem_limit_bytes=VMEM_LIMIT),
        name="combine",
    )(dest2d, yb, x1, gates, gate2, final_g)


def _moe(h2, idx, gates, rank, counts, x1, gate2, final_g, w_gu, b_gu, w_d, b_d, seq):
    t = h2.shape[0]
    n_slots = t * TOP_K
    n_blocks = -(-n_slots // EXPERT_BLOCK) + N_EXPERTS
    cap = n_blocks * EXPERT_BLOCK
    assert cap % GATHER_ROWS == 0 and n_slots % GATHER_ROWS == 0
    padded = (counts + EXPERT_BLOCK - 1) // EXPERT_BLOCK * EXPERT_BLOCK
    pad_ends = jnp.cumsum(padded)
    pad_starts = pad_ends - padded
    dest = pad_starts[idx] + rank
    tok = jnp.broadcast_to(jnp.arange(t, dtype=jnp.int32)[:, None], (t, TOP_K))
    buf_tok = jnp.zeros((cap,), jnp.int32).at[dest.reshape(-1)].set(tok.reshape(-1))
    block_starts = jnp.arange(n_blocks, dtype=jnp.int32) * EXPERT_BLOCK
    block_e = jnp.minimum(jnp.sum(block_starts[:, None] >= pad_ends[None, :], axis=1),
                          N_EXPERTS - 1).astype(jnp.int32)

    xs = _gather_rows(buf_tok.reshape(-1, GATHER_ROWS), h2)
    yb = _experts(block_e, xs, w_gu, b_gu, w_d, b_d)
    dest_blocks = dest.reshape(-1, COMBINE_TOKENS, TOP_K).transpose(0, 2, 1).reshape(-1, GATHER_ROWS)
    return _combine(dest_blocks, yb, x1, gates, gate2, final_g, seq)


def _layer(x, c_mod, norm1_g, w_in, mu_shift, w0, w2, a0, a2, g2, k_k, k_a, r_k, gn_w, gn_b, b_f,
           q_norm_g, k_norm_g, o_norm_g, w_out, norm2_g, w_router, b_router, w_gate_up,
           b_gate_up, w_down, b_down, final_g, tm_in, tq, tm_out):
    bsz, seq, _ = x.shape
    shift1, scale1, gate1, shift2, scale2, gate2 = (
        m.reshape(bsz, 1, D_MODEL) for m in jnp.split(c_mod, 6, axis=-1))
    row = lambda v: v.reshape(1, -1)

    w_r = w_in[:, :RWKV_COLS].astype(BF16)
    w_x = w_in[:, RWKV_COLS:RWKV_COLS + FOX_MAIN].astype(BF16)
    w_f = jnp.pad(w_in[:, RWKV_COLS + FOX_MAIN:], ((0, 0), (0, LANES - N_HEADS)))
    b_fp = jnp.pad(b_f, (0, LANES - N_HEADS)).reshape(1, LANES)
    qk_gain = jnp.concatenate([jnp.tile(q_norm_g, N_HEADS) * HEAD_DIM ** -0.5,
                               jnp.tile(k_norm_g, N_HEADS)]).reshape(1, -1)
    p_r, p_x, f_cum = _inproj(x, shift1, scale1, row(norm1_g), w_r, w_x, w_f, b_fp, qk_gain, tm_in)

    zeros = jnp.zeros((LANES - 64, D_GRP), F32)
    w2p = jnp.concatenate([w2, zeros], axis=0).astype(BF16)
    a2p = jnp.concatenate([zeros, a2], axis=0).astype(BF16)
    y_r = _rwkv(p_r, row(mu_shift), row(w0), w2p, row(a0), a2p, g2.astype(BF16), row(k_k),
                row(k_a), row(r_k), row(gn_w), row(gn_b))

    f_rows = jnp.transpose(f_cum[:, :, :N_HEADS], (0, 2, 1))
    y_f = _fox(p_x, f_cum, f_rows, jnp.tile(o_norm_g, 2).reshape(1, LANES), tq)

    t = bsz * seq
    w_rt = jnp.pad(w_router, ((0, 0), (0, LANES - N_EXPERTS)))
    b_rt = jnp.pad(b_router, (0, LANES - N_EXPERTS)).reshape(1, LANES)
    wo = w_out.astype(BF16)
    x1, h2, idx, gates, rank, cnt = _outproj(
        x.reshape(t, D_MODEL), y_r.reshape(t, D_GRP), y_f.reshape(t, D_GRP), gate1, shift2,
        scale2, row(norm2_g), wo[:D_GRP], wo[D_GRP:], w_rt, b_rt, tm_out, seq)

    counts = cnt[0, :N_EXPERTS].astype(jnp.int32)
    out = _moe(h2, idx[:, :TOP_K], gates, rank[:, :TOP_K], counts, x1, gate2, row(final_g),
               w_gate_up.astype(BF16), b_gate_up.reshape(N_EXPERTS, 1, -1),
               w_down.astype(BF16), b_down.reshape(N_EXPERTS, 1, -1), seq)
    return out.reshape(bsz, seq, D_MODEL)


def kernel(x, c, w_ada, b_ada, norm1_g, w_in, mu_shift, w0, w2, a0, a2, g2, k_k, k_a, r_k, gn_w,
           gn_b, b_f, q_norm_g, k_norm_g, o_norm_g, w_out, norm2_g, w_router, b_router, w_gate_up,
           b_gate_up, w_down, b_down, final_g):
    assert w_ada.shape[0] == 1, "single-layer block"
    c_mod = _adaln(c, w_ada[0], b_ada[0])
    return _layer(x, c_mod, norm1_g[0], w_in[0], mu_shift[0], w0[0], w2[0], a0[0], a2[0], g2[0],
                  k_k[0], k_a[0], r_k[0], gn_w[0], gn_b[0], b_f[0], q_norm_g[0], k_norm_g[0],
                  o_norm_g[0], w_out[0], norm2_g[0], w_router[0], b_router[0], w_gate_up[0],
                  b_gate_up[0], w_down[0], b_down[0], final_g,
                  tm_in=min(512, x.shape[1]), tq=min(256, x.shape[1]), tm_out=min(512, x.shape[1]))
```

```python
import functools

import jax
import jax.numpy as jnp
from jax import lax
from jax.experimental import pallas as pl
from jax.experimental.pallas import tpu as pltpu

F32 = jnp.float32
BF16 = jnp.bfloat16
HIGHEST = lax.Precision.HIGHEST

D_MODEL = 1024
HEAD_DIM = 64
N_HEADS = 8
D_GRP = N_HEADS * HEAD_DIM
RWKV_COLS = 1792
LORA_OFF = 3 * D_GRP
GATE_OFF = LORA_OFF + 128
FOX_MAIN = 4 * D_GRP
N_EXPERTS = 32
TOP_K = 4
EXPERT_BLOCK = 256
SWIGLU_ALPHA = 1.702
SWIGLU_LIMIT = 7.0
NORM_EPS = 1e-6
GN_EPS = 64e-5
LANES = 128
CHUNK = 64
HEADS_PER_SCAN = 4
SCAN_W = HEADS_PER_SCAN * HEAD_DIM
VMEM_LIMIT = 56 * 1024 * 1024


def _dot(a, b):
    return jnp.dot(a.astype(BF16), b.astype(BF16), preferred_element_type=F32)


def _dot_nt(a, b):
    return lax.dot_general(a.astype(BF16), b.astype(BF16), (((1,), (1,)), ((), ())),
                           preferred_element_type=F32)


def _dot_tn(a, b):
    return lax.dot_general(a.astype(BF16), b.astype(BF16), (((0,), (0,)), ((), ())),
                           preferred_element_type=F32)


def _fdot(a, b):
    return jnp.dot(a, b, precision=HIGHEST, preferred_element_type=F32)


def _split_dot(x, m, terms=2, left=False):
    acc = None
    rem = x
    for _ in range(terms):
        part = rem.astype(BF16)
        rem = rem - part.astype(F32)
        d = (jnp.dot(m, part, preferred_element_type=F32) if left
             else jnp.dot(part, m, preferred_element_type=F32))
        acc = d if acc is None else acc + d
    return acc


def _iota(shape, dim):
    return lax.broadcasted_iota(jnp.int32, shape, dim)


def _seg_reduce_mat(n):
    return (_iota((n, LANES), 0) // HEAD_DIM == _iota((n, LANES), 1)).astype(BF16)


def _seg_expand_mat(n):
    return (_iota((LANES, n), 1) // HEAD_DIM == _iota((LANES, n), 0)).astype(BF16)


def _tri(n, strict):
    r, c = _iota((n, n), 0), _iota((n, n), 1)
    return ((r > c) if strict else (r >= c)).astype(BF16)


def _log_sigmoid(z):
    return jnp.minimum(z, 0.0) - jnp.log(1.0 + jnp.exp(-jnp.abs(z)))


def _sigmoid(z):
    return 1.0 / (1.0 + jnp.exp(-z))


def _adaln_kernel(c_ref, w_ref, b_ref, o_ref):
    c = c_ref[...]
    o_ref[...] = _fdot(c * _sigmoid(c), w_ref[...]) + b_ref[...]


def _adaln(c, w_ada, b_ada):
    bsz = c.shape[0]
    n_mod = w_ada.shape[1] // D_MODEL
    return pl.pallas_call(
        _adaln_kernel,
        grid=(n_mod,),
        in_specs=[pl.BlockSpec((bsz, D_MODEL), lambda j: (0, 0)),
                  pl.BlockSpec((D_MODEL, D_MODEL), lambda j: (0, j)),
                  pl.BlockSpec((1, D_MODEL), lambda j: (0, j))],
        out_specs=pl.BlockSpec((bsz, D_MODEL), lambda j: (0, j)),
        out_shape=jax.ShapeDtypeStruct((bsz, n_mod * D_MODEL), F32),
        name="adaln",
    )(c, w_ada, b_ada.reshape(1, -1))


def _inproj_kernel(x_ref, sh_ref, sc_ref, g_ref, wr_ref, wx_ref, wfh_ref, wfl_ref, bf_ref, qkg_ref,
                   pr_ref, px_ref, kb_ref, qb_ref, carry_ref):
    @pl.when(pl.program_id(1) == 0)
    def _():
        carry_ref[...] = jnp.zeros_like(carry_ref)

    x = x_ref[...]
    tm = x.shape[0]
    h = x * lax.rsqrt(jnp.mean(x * x, axis=-1, keepdims=True) + NORM_EPS) * g_ref[...]
    h = h * (1.0 + sc_ref[...]) + sh_ref[...]
    hb = h.astype(BF16)
    h_lo = (h - hb.astype(F32)).astype(BF16)

    pr_ref[...] = jnp.dot(hb, wr_ref[...], preferred_element_type=F32).astype(BF16)

    px = jnp.dot(hb, wx_ref[...], preferred_element_type=F32)
    qk = px[:, :2 * D_GRP]
    ss = _split_dot(qk * qk, _seg_reduce_mat(2 * D_GRP))
    inv = lax.rsqrt(ss * (1.0 / HEAD_DIM) + NORM_EPS)
    qk = qk * _split_dot(inv, _seg_expand_mat(2 * D_GRP)) * qkg_ref[...]
    px_ref[:, :2 * D_GRP] = qk.astype(BF16)
    px_ref[:, 2 * D_GRP:] = px[:, 2 * D_GRP:].astype(BF16)

    z = (jnp.dot(hb, wfh_ref[...], preferred_element_type=F32)
         + jnp.dot(h_lo, wfh_ref[...], preferred_element_type=F32)
         + jnp.dot(hb, wfl_ref[...], preferred_element_type=F32)) + bf_ref[...]
    cum = _split_dot(_log_sigmoid(z), _tri(tm, False), terms=3, left=True) + carry_ref[...]
    carry_ref[...] = cum[tm - 1:tm, :]

    src, dst = _iota((LANES, LANES), 0), _iota((LANES, LANES), 1)
    parts = []
    rem = cum
    for _ in range(3):
        part = rem.astype(BF16)
        rem = rem - part.astype(F32)
        parts.append(part)

    def spread(offset):
        return sum(jnp.dot(part, ((dst == 8 * src + offset + t) & (src < N_HEADS)).astype(BF16),
                           preferred_element_type=F32) for t, part in enumerate(parts))

    slot = _iota((1, LANES), 1) % 8
    kb_ref[...] = (jnp.where((slot >= 3) & (slot < 6), 1.0, 0.0) - spread(0)).astype(BF16)
    qb_ref[...] = (jnp.where(slot < 3, 1.0, 0.0) + spread(3)).astype(BF16)


def _inproj(x, shift, scale, g, w_r, w_x, w_f, b_f, qk_gain, tm):
    w_f_hi = w_f.astype(BF16)
    w_f_lo = (w_f - w_f_hi.astype(F32)).astype(BF16)
    bsz, seq, _ = x.shape
    const = lambda b, s: (0, 0)
    return pl.pallas_call(
        _inproj_kernel,
        grid=(bsz, seq // tm),
        in_specs=[pl.BlockSpec((None, tm, D_MODEL), lambda b, s: (b, s, 0)),
                  pl.BlockSpec((None, 1, D_MODEL), lambda b, s: (b, 0, 0)),
                  pl.BlockSpec((None, 1, D_MODEL), lambda b, s: (b, 0, 0)),
                  pl.BlockSpec((1, D_MODEL), const),
                  pl.BlockSpec((D_MODEL, RWKV_COLS), const),
                  pl.BlockSpec((D_MODEL, FOX_MAIN), const),
                  pl.BlockSpec((D_MODEL, LANES), const),
                  pl.BlockSpec((D_MODEL, LANES), const),
                  pl.BlockSpec((1, LANES), const),
                  pl.BlockSpec((1, 2 * D_GRP), const)],
        out_specs=[pl.BlockSpec((None, tm, RWKV_COLS), lambda b, s: (b, s, 0)),
                   pl.BlockSpec((None, tm, FOX_MAIN), lambda b, s: (b, s, 0)),
                   pl.BlockSpec((None, tm, LANES), lambda b, s: (b, s, 0)),
                   pl.BlockSpec((None, tm, LANES), lambda b, s: (b, s, 0))],
        out_shape=[jax.ShapeDtypeStruct((bsz, seq, RWKV_COLS), BF16),
                   jax.ShapeDtypeStruct((bsz, seq, FOX_MAIN), BF16),
                   jax.ShapeDtypeStruct((bsz, seq, LANES), BF16),
                   jax.ShapeDtypeStruct((bsz, seq, LANES), BF16)],
        scratch_shapes=[pltpu.VMEM((1, LANES), F32)],
        compiler_params=pltpu.CompilerParams(
            dimension_semantics=("parallel", "arbitrary"), vmem_limit_bytes=VMEM_LIMIT),
        name="inproj",
    )(x, shift, scale, g, w_r, w_x, w_f_hi, w_f_lo, b_f, qk_gain)


_NN = (((1,), (0,)), ((), ()))
_NT = (((1,), (1,)), ((), ()))
_TN = (((0,), (0,)), ((), ()))
SCAN_N = HEADS_PER_SCAN * CHUNK
BATCH_PER_STEP = 2
INV_LEVELS = 5
M_HEAD, M_STRICT, M_INCL, M_EYE, M_BASE, M_OFF = 0, 1, 2, 3, 4, 5


def _bdot(a, b, dims):
    return lax.dot_general(a, b, dims, preferred_element_type=F32)


def _scan_masks():
    rr, cc = _iota((SCAN_N, SCAN_W), 0), _iota((SCAN_N, SCAN_W), 1)
    ri, ci = _iota((SCAN_N, SCAN_N), 0), _iota((SCAN_N, SCAN_N), 1)
    same = ri // CHUNK == ci // CHUNK
    masks = [rr // CHUNK == cc // HEAD_DIM, same & (ri > ci), same & (ri >= ci), ri == ci,
             (ri // 2 == ci // 2) & (ri > ci)]
    blk = 2
    while blk < CHUNK:
        masks.append((ri // (2 * blk) == ci // (2 * blk)) & (ri // blk != ci // blk) & (ri > ci))
        blk *= 2
    return jnp.stack(masks).astype(BF16)


def _rwkv_chunk(p, last_ref, state_ref, prm, masks_ref):
    mu, w0, w2, a0, a2, g2, k_k, k_a, r_k, gn_w, gn_b = prm
    prev = jnp.where(_iota((CHUNK, 1), 0) == 0, last_ref[...], pltpu.roll(p, 1, axis=0))
    last_ref[...] = p[CHUNK - 1:CHUNK, :]
    pf = p + mu * (prev - p)
    r = pf[:, 0:D_GRP]
    k = pf[:, D_GRP:2 * D_GRP]
    v = pf[:, 2 * D_GRP:3 * D_GRP]
    lora = pf[:, LORA_OFF:GATE_OFF]
    gd = pf[:, GATE_OFF:RWKV_COLS]

    wlog = w0 + _dot(jnp.tanh(lora), w2)
    neg = -wlog
    softplus = jnp.maximum(neg, 0.0) + jnp.log(1.0 + jnp.exp(-jnp.abs(neg)))
    ld = -jnp.exp(-softplus - 0.5)
    a = _sigmoid(a0 + _dot(lora, a2))
    g = _dot(_sigmoid(gd), g2)

    red, exp_m = _seg_reduce_mat(D_GRP), _seg_expand_mat(D_GRP)
    kk = k * k_k
    n2 = _split_dot(kk * kk, red)
    kk = kk * _split_dot(1.0 / jnp.maximum(jnp.sqrt(n2), 1e-12), exp_m)
    k2 = k * (1.0 + (a - 1.0) * k_a)

    cl = _split_dot(ld, _tri(CHUNK, False), terms=3, left=True)
    e_in = jnp.exp(cl)
    e_out = jnp.exp(-cl)
    e_rem = jnp.exp(cl[CHUNK - 1:CHUNK, :] - cl)
    kka = kk * a
    p_end = e_in[CHUNK - 1:CHUNK, :]
    ops = [(-kk * jnp.exp(cl - ld)).astype(BF16), (kka * e_out).astype(BF16),
           (k2 * e_out).astype(BF16), (r * e_in).astype(BF16), v.astype(BF16),
           (kka * e_rem).astype(BF16), (k2 * e_rem).astype(BF16)]

    head_mask = masks_ref[M_HEAD]
    strict, incl = masks_ref[M_STRICT], masks_ref[M_INCL]
    ys = []
    for grp in range(N_HEADS // HEADS_PER_SCAN):
        sl = slice(grp * SCAN_W, (grp + 1) * SCAN_W)
        xa, xb, xk, xr, xv, xbe, xke = (
            jnp.concatenate([o[:, sl]] * HEADS_PER_SCAN, axis=0) * head_mask for o in ops)
        s = state_ref[grp]
        sb = s.astype(BF16)

        nab = _bdot(xa, xb, _NT).astype(BF16)
        aak = _bdot(xa, xk, _NT).astype(BF16) * strict
        arb = _bdot(xr, xb, _NT).astype(BF16) * incl
        ark = _bdot(xr, xk, _NT).astype(BF16) * incl
        t_inv = masks_ref[M_EYE] + nab * masks_ref[M_BASE]
        for lvl in range(INV_LEVELS):
            off = nab * masks_ref[M_OFF + lvl]
            t_inv = t_inv + _bdot(_bdot(t_inv, off, _NN).astype(BF16), t_inv, _NN).astype(BF16)
        rhs = _bdot(xa, sb, _NT) + _bdot(aak, xv, _NN)
        sa = _bdot(t_inv, rhs.astype(BF16), _NN).astype(BF16)
        y = _bdot(xr, sb, _NT) + _bdot(arb, sa, _NN) + _bdot(ark, xv, _NN)
        state_ref[grp] = s * p_end[:, sl] + _bdot(sa, xbe, _TN) + _bdot(xv, xke, _TN)
        ys.append(y[0:CHUNK] + y[CHUNK:2 * CHUNK] + y[2 * CHUNK:3 * CHUNK] + y[3 * CHUNK:4 * CHUNK])
    y = jnp.concatenate(ys, axis=1)

    mean = _split_dot(_split_dot(y, red) * (1.0 / HEAD_DIM), exp_m)
    d = y - mean
    var = _split_dot(d * d, red) * (1.0 / HEAD_DIM)
    yn = d * _split_dot(lax.rsqrt(var + GN_EPS), exp_m) * gn_w + gn_b
    bonus = _split_dot(_split_dot(r * k2 * r_k, red), exp_m) * v
    return (yn + bonus) * g


def _rwkv_kernel(p_ref, masks_ref, mu_ref, w0_ref, w2_ref, a0_ref, a2_ref, g2_ref, kk_ref, ka_ref,
                 rk_ref, gnw_ref, gnb_ref, o_ref, last_ref, state_ref):
    @pl.when(pl.program_id(1) == 0)
    def _():
        last_ref[...] = jnp.zeros_like(last_ref)
        state_ref[...] = jnp.zeros_like(state_ref)

    prm = tuple(ref[...] for ref in (mu_ref, w0_ref, w2_ref, a0_ref, a2_ref, g2_ref, kk_ref,
                                     ka_ref, rk_ref, gnw_ref, gnb_ref))
    for bb in range(BATCH_PER_STEP):
        out = _rwkv_chunk(p_ref[bb].astype(F32), last_ref.at[bb], state_ref.at[bb], prm, masks_ref)
        o_ref[bb] = out.astype(BF16)


def _rwkv(p_r, mu, w0, w2p, a0, a2p, g2, k_k, k_a, r_k, gn_w, gn_b):
    bsz, seq, _ = p_r.shape
    assert bsz % BATCH_PER_STEP == 0
    masks = _scan_masks()
    const = lambda b, s: (0, 0)
    vec = pl.BlockSpec((1, D_GRP), const)
    return pl.pallas_call(
        _rwkv_kernel,
        grid=(bsz // BATCH_PER_STEP, seq // CHUNK),
        in_specs=[pl.BlockSpec((BATCH_PER_STEP, CHUNK, RWKV_COLS), lambda b, s: (b, s, 0)),
                  pl.BlockSpec(masks.shape, lambda b, s: (0, 0, 0)),
                  pl.BlockSpec((1, RWKV_COLS), const),
                  vec, pl.BlockSpec((LANES, D_GRP), const),
                  vec, pl.BlockSpec((LANES, D_GRP), const),
                  pl.BlockSpec((LANES, D_GRP), const),
                  vec, vec, vec, vec, vec],
        out_specs=pl.BlockSpec((BATCH_PER_STEP, CHUNK, D_GRP), lambda b, s: (b, s, 0)),
        out_shape=jax.ShapeDtypeStruct((bsz, seq, D_GRP), BF16),
        scratch_shapes=[pltpu.VMEM((BATCH_PER_STEP, 1, RWKV_COLS), F32),
                        pltpu.VMEM((BATCH_PER_STEP, N_HEADS // HEADS_PER_SCAN, SCAN_W, SCAN_W), F32)],
        compiler_params=pltpu.CompilerParams(
            dimension_semantics=("parallel", "arbitrary"), vmem_limit_bytes=VMEM_LIMIT),
        name="rwkv",
    )(p_r, masks, mu, w0, w2p, a0, a2p, g2, k_k, k_a, r_k, gn_w, gn_b)


def _fox_kernel(q_ref, qb_ref, k_ref, kb_ref, vt_ref, og_ref, ong_ref, o_ref, *, tq):
    hp = pl.program_id(1)
    qi = pl.program_id(2)
    lane = _iota((1, LANES), 1)
    q = q_ref[...]
    qb = qb_ref[...]
    zero = jnp.zeros_like(q)
    qcat = [jnp.concatenate([jnp.where(lane // HEAD_DIM == hh, q, zero),
                             jnp.where(lane // 8 == hp * 2 + hh, qb, zero)], axis=1)
            for hh in range(2)]
    key_pos = _iota((tq, tq), 0)
    qry_pos = _iota((tq, tq), 1)

    def tile(j, carry, masked):
        start = pl.multiple_of(j * tq, tq)
        kcat = jnp.concatenate([k_ref[pl.ds(start, tq), :], kb_ref[pl.ds(start, tq), :]], axis=1)
        vt = vt_ref[:, pl.ds(start, tq)]
        new = []
        for hh in range(2):
            m, l, acc = carry[hh]
            st = lax.dot_general(kcat, qcat[hh], _NT, preferred_element_type=F32)
            if masked:
                st = jnp.where(qry_pos >= key_pos, st, -jnp.inf)
            m_new = jnp.maximum(m, jnp.max(st, axis=0, keepdims=True))
            alpha = jnp.exp(m - m_new)
            pt = jnp.exp(st - m_new)
            l = alpha * l + jnp.sum(pt, axis=0, keepdims=True)
            pv = jnp.dot(vt, pt.astype(BF16), preferred_element_type=F32)
            acc = alpha * acc + pv[hh * HEAD_DIM:(hh + 1) * HEAD_DIM, :]
            new.append((m_new, l, acc))
        return tuple(new)

    init = (jnp.full((1, tq), -jnp.inf, F32), jnp.zeros((1, tq), F32),
            jnp.zeros((HEAD_DIM, tq), F32))
    carry = lax.fori_loop(0, qi, functools.partial(tile, masked=False), (init, init))
    outs = []
    for _, l, acc in tile(qi, carry, True):
        o = acc / l
        outs.append(o * lax.rsqrt(jnp.mean(o * o, axis=0, keepdims=True) + NORM_EPS))
    o = jnp.concatenate(outs, axis=0).T
    o_ref[...] = (o * ong_ref[...] * _sigmoid(og_ref[...].astype(F32))).astype(BF16)


def _fox(p_x, k_bias, q_bias, o_gain, tq):
    bsz, seq, _ = p_x.shape
    npair = N_HEADS // 2
    v_t = jnp.transpose(p_x[:, :, 2 * D_GRP:3 * D_GRP], (0, 2, 1))
    return pl.pallas_call(
        functools.partial(_fox_kernel, tq=tq),
        grid=(bsz, npair, seq // tq),
        in_specs=[pl.BlockSpec((None, tq, LANES), lambda b, h, i: (b, i, h)),
                  pl.BlockSpec((None, tq, LANES), lambda b, h, i: (b, i, 0)),
                  pl.BlockSpec((None, seq, LANES), lambda b, h, i: (b, 0, npair + h)),
                  pl.BlockSpec((None, seq, LANES), lambda b, h, i: (b, 0, 0)),
                  pl.BlockSpec((None, LANES, seq), lambda b, h, i: (b, h, 0)),
                  pl.BlockSpec((None, tq, LANES), lambda b, h, i: (b, i, 3 * npair + h)),
                  pl.BlockSpec((1, LANES), lambda b, h, i: (0, 0))],
        out_specs=pl.BlockSpec((None, tq, LANES), lambda b, h, i: (b, i, h)),
        out_shape=jax.ShapeDtypeStruct((bsz, seq, D_GRP), BF16),
        compiler_params=pltpu.CompilerParams(
            dimension_semantics=("parallel", "parallel", "arbitrary"),
            vmem_limit_bytes=VMEM_LIMIT),
        name="fox",
    )(p_x, q_bias, p_x, k_bias, v_t, p_x, o_gain)


def _outproj_kernel(x_ref, yr_ref, yf_ref, g1_ref, sh_ref, sc_ref, ng_ref, wor_ref, wof_ref,
                    wrt_ref, brt_ref, x1_ref, h2_ref, idx_ref, gate_ref, rank_ref, cnt_ref,
                    carry_ref):
    @pl.when(pl.program_id(0) == 0)
    def _():
        carry_ref[...] = jnp.zeros_like(carry_ref)

    y = (jnp.dot(yr_ref[...], wor_ref[...], preferred_element_type=F32)
         + jnp.dot(yf_ref[...], wof_ref[...], preferred_element_type=F32))
    x1 = x_ref[...] + g1_ref[...] * y
    x1_ref[...] = x1
    tm = x1.shape[0]
    h = x1 * lax.rsqrt(jnp.mean(x1 * x1, axis=-1, keepdims=True) + NORM_EPS) * ng_ref[...]
    h2 = h * (1.0 + sc_ref[...]) + sh_ref[...]
    h2_ref[...] = h2

    lane = _iota((tm, LANES), 1)
    logits = _fdot(h2, wrt_ref[...]) + brt_ref[...]
    lg = jnp.where(lane < N_EXPERTS, logits, -jnp.inf)
    picks = []
    hot_sum = jnp.zeros((tm, LANES), F32)
    for _ in range(TOP_K):
        m = jnp.max(lg, axis=-1, keepdims=True)
        sel = jnp.min(jnp.where(lg == m, lane, LANES), axis=-1, keepdims=True)
        hot = lane == sel
        picks.append((m, sel, hot))
        hot_sum = hot_sum + hot.astype(F32)
        lg = jnp.where(hot, -jnp.inf, lg)
    es = [jnp.exp(m - picks[0][0]) for m, _, _ in picks]
    den = es[0] + es[1] + es[2] + es[3]

    before = jnp.dot(_tri(tm, True), hot_sum.astype(BF16), preferred_element_type=F32)
    before = before + carry_ref[...]
    idx_out = jnp.zeros((tm, LANES), jnp.int32)
    gate_out = jnp.zeros((tm, LANES), F32)
    rank_out = jnp.zeros((tm, LANES), jnp.int32)
    for kk, (m, sel, hot) in enumerate(picks):
        rk = jnp.sum(jnp.where(hot, before, 0.0), axis=-1, keepdims=True).astype(jnp.int32)
        idx_out = jnp.where(lane == kk, sel, idx_out)
        gate_out = jnp.where(lane == kk, es[kk] / den, gate_out)
        rank_out = jnp.where(lane == kk, rk, rank_out)
    idx_ref[...] = idx_out
    gate_ref[...] = gate_out
    rank_ref[...] = rank_out
    carry_ref[...] = carry_ref[...] + jnp.sum(hot_sum, axis=0, keepdims=True)
    cnt_ref[...] = carry_ref[...]


def _outproj(x2d, y_r, y_f, gate1, shift2, scale2, norm_g, wo_r, wo_f, w_rt, b_rt, tm, seq):
    t = x2d.shape[0]
    per_b = seq // tm
    const = lambda i: (0, 0)
    rows = lambda i: (i, 0)
    mod = pl.BlockSpec((None, 1, D_MODEL), lambda i: (i // per_b, 0, 0))
    return pl.pallas_call(
        _outproj_kernel,
        grid=(t // tm,),
        in_specs=[pl.BlockSpec((tm, D_MODEL), rows),
                  pl.BlockSpec((tm, D_GRP), rows),
                  pl.BlockSpec((tm, D_GRP), rows),
                  mod, mod, mod,
                  pl.BlockSpec((1, D_MODEL), const),
                  pl.BlockSpec((D_GRP, D_MODEL), const),
                  pl.BlockSpec((D_GRP, D_MODEL), const),
                  pl.BlockSpec((D_MODEL, LANES), const),
                  pl.BlockSpec((1, LANES), const)],
        out_specs=[pl.BlockSpec((tm, D_MODEL), rows),
                   pl.BlockSpec((tm, D_MODEL), rows),
                   pl.BlockSpec((tm, LANES), rows),
                   pl.BlockSpec((tm, LANES), rows),
                   pl.BlockSpec((tm, LANES), rows),
                   pl.BlockSpec((1, LANES), const)],
        out_shape=[jax.ShapeDtypeStruct((t, D_MODEL), F32),
                   jax.ShapeDtypeStruct((t, D_MODEL), F32),
                   jax.ShapeDtypeStruct((t, LANES), jnp.int32),
                   jax.ShapeDtypeStruct((t, LANES), F32),
                   jax.ShapeDtypeStruct((t, LANES), jnp.int32),
                   jax.ShapeDtypeStruct((1, LANES), F32)],
        scratch_shapes=[pltpu.VMEM((1, LANES), F32)],
        compiler_params=pltpu.CompilerParams(
            dimension_semantics=("arbitrary",), vmem_limit_bytes=VMEM_LIMIT),
        name="outproj",
    )(x2d, y_r, y_f, gate1, shift2, scale2, norm_g, wo_r, wo_f, w_rt, b_rt)


GATHER_ROWS = 1024


def _row_copy(src_hbm, dst, src_row, dst_row, sem):
    return pltpu.make_async_copy(src_hbm.at[pl.ds(src_row, 1), :], dst.at[pl.ds(dst_row, 1), :], sem)


def _gather_kernel(idx_hbm, src_hbm, o_ref, idx_smem, isem, sem):
    i = pl.program_id(0)
    cp = pltpu.make_async_copy(idx_hbm.at[i], idx_smem, isem)
    cp.start()
    cp.wait()

    def issue(r, c):
        _row_copy(src_hbm, o_ref, idx_smem[r], r, sem).start()
        return c

    lax.fori_loop(0, GATHER_ROWS, issue, 0)

    pltpu.make_async_copy(src_hbm.at[pl.ds(0, GATHER_ROWS), :], o_ref, sem).wait()


def _gather_rows(idx2d, src):
    nblk = idx2d.shape[0]
    d = src.shape[1]
    return pl.pallas_call(
        _gather_kernel,
        grid=(nblk,),
        in_specs=[pl.BlockSpec(memory_space=pl.ANY), pl.BlockSpec(memory_space=pl.ANY)],
        out_specs=pl.BlockSpec((GATHER_ROWS, d), lambda i: (i, 0)),
        out_shape=jax.ShapeDtypeStruct((nblk * GATHER_ROWS, d), src.dtype),
        scratch_shapes=[pltpu.SMEM((GATHER_ROWS,), jnp.int32),
                        pltpu.SemaphoreType.DMA, pltpu.SemaphoreType.DMA],
        compiler_params=pltpu.CompilerParams(
            dimension_semantics=("arbitrary",), vmem_limit_bytes=VMEM_LIMIT),
        name="gather",
    )(idx2d, src)


def _expert_kernel(be_ref, x_ref, wgu_ref, bgu_ref, wd_ref, bd_ref, o_ref):
    del be_ref
    gu = jnp.dot(x_ref[...].astype(BF16), wgu_ref[...], preferred_element_type=F32) + bgu_ref[...]
    gate = jnp.minimum(gu[:, :D_MODEL], SWIGLU_LIMIT)
    up = jnp.clip(gu[:, D_MODEL:], -SWIGLU_LIMIT, SWIGLU_LIMIT)
    act = gate * _sigmoid(SWIGLU_ALPHA * gate) * (up + 1.0)
    o_ref[...] = jnp.dot(act.astype(BF16), wd_ref[...], preferred_element_type=F32) + bd_ref[...]


def _experts(block_e, xs, w_gu, b_gu, w_d, b_d):
    n_blocks = block_e.shape[0]
    grid_spec = pltpu.PrefetchScalarGridSpec(
        num_scalar_prefetch=1,
        grid=(n_blocks,),
        in_specs=[pl.BlockSpec((EXPERT_BLOCK, D_MODEL), lambda j, be: (j, 0)),
                  pl.BlockSpec((None, D_MODEL, 2 * D_MODEL), lambda j, be: (be[j], 0, 0)),
                  pl.BlockSpec((None, 1, 2 * D_MODEL), lambda j, be: (be[j], 0, 0)),
                  pl.BlockSpec((None, D_MODEL, D_MODEL), lambda j, be: (be[j], 0, 0)),
                  pl.BlockSpec((None, 1, D_MODEL), lambda j, be: (be[j], 0, 0))],
        out_specs=pl.BlockSpec((EXPERT_BLOCK, D_MODEL), lambda j, be: (j, 0)),
    )
    return pl.pallas_call(
        _expert_kernel,
        grid_spec=grid_spec,
        out_shape=jax.ShapeDtypeStruct(xs.shape, F32),
        compiler_params=pltpu.CompilerParams(
            dimension_semantics=("arbitrary",), vmem_limit_bytes=VMEM_LIMIT),
        name="experts",
    )(block_e, xs, w_gu, b_gu, w_d, b_d)


COMBINE_TOKENS = GATHER_ROWS // TOP_K


def _combine_kernel(dest_hbm, yb_hbm, x1_ref, gate_ref, g2_ref, fg_ref, o_ref,
                    rows_ref, idx_smem, isem, sem):
    i = pl.program_id(0)
    cp = pltpu.make_async_copy(dest_hbm.at[i], idx_smem, isem)
    cp.start()
    cp.wait()

    def issue(r, c):
        _row_copy(yb_hbm, rows_ref, idx_smem[r], r, sem).start()
        return c

    lax.fori_loop(0, GATHER_ROWS, issue, 0)

    pltpu.make_async_copy(yb_hbm.at[pl.ds(0, GATHER_ROWS), :], rows_ref, sem).wait()

    gates = gate_ref[...]
    acc = None
    for kk in range(TOP_K):
        part = gates[:, kk:kk + 1] * rows_ref[kk * COMBINE_TOKENS:(kk + 1) * COMBINE_TOKENS, :]
        acc = part if acc is None else acc + part
    x2 = x1_ref[...] + g2_ref[...] * acc
    o_ref[...] = x2 * lax.rsqrt(jnp.mean(x2 * x2, axis=-1, keepdims=True) + NORM_EPS) * fg_ref[...]


def _combine(dest2d, yb, x1, gates, gate2, final_g, seq):
    t = x1.shape[0]
    tm = COMBINE_TOKENS
    per_b = seq // tm
    rows = lambda i: (i, 0)
    return pl.pallas_call(
        _combine_kernel,
        grid=(t // tm,),
        in_specs=[pl.BlockSpec(memory_space=pl.ANY), pl.BlockSpec(memory_space=pl.ANY),
                  pl.BlockSpec((tm, D_MODEL), rows),
                  pl.BlockSpec((tm, LANES), rows),
                  pl.BlockSpec((None, 1, D_MODEL), lambda i: (i // per_b, 0, 0)),
                  pl.BlockSpec((1, D_MODEL), lambda i: (0, 0))],
        out_specs=pl.BlockSpec((tm, D_MODEL), rows),
        out_shape=jax.ShapeDtypeStruct((t, D_MODEL), F32),
        scratch_shapes=[pltpu.VMEM((GATHER_ROWS, D_MODEL), F32),
                        pltpu.SMEM((GATHER_ROWS,), jnp.int32),
                        pltpu.SemaphoreType.DMA, pltpu.SemaphoreType.DMA],
        compiler_params=pltpu.CompilerParams(
            dimension_semantics=("arbitrary",), vmem_limit_bytes=VMEM_LIMIT),
        name="combine",
    )(dest2d, yb, x1, gates, gate2, final_g)


def _moe(h2, idx, gates, rank, counts, x1, gate2, final_g, w_gu, b_gu, w_d, b_d, seq):
    t = h2.shape[0]
    n_slots = t * TOP_K
    n_blocks = -(-n_slots // EXPERT_BLOCK) + N_EXPERTS
    cap = n_blocks * EXPERT_BLOCK
    assert cap % GATHER_ROWS == 0 and n_slots % GATHER_ROWS == 0
    padded = (counts + EXPERT_BLOCK - 1) // EXPERT_BLOCK * EXPERT_BLOCK
    pad_ends = jnp.cumsum(padded)
    pad_starts = pad_ends - padded
    dest = pad_starts[idx] + rank
    tok = jnp.broadcast_to(jnp.arange(t, dtype=jnp.int32)[:, None], (t, TOP_K))
    buf_tok = jnp.zeros((cap,), jnp.int32).at[dest.reshape(-1)].set(tok.reshape(-1))
    block_starts = jnp.arange(n_blocks, dtype=jnp.int32) * EXPERT_BLOCK
    block_e = jnp.minimum(jnp.sum(block_starts[:, None] >= pad_ends[None, :], axis=1),
                          N_EXPERTS - 1).astype(jnp.int32)

    xs = _gather_rows(buf_tok.reshape(-1, GATHER_ROWS), h2)
    yb = _experts(block_e, xs, w_gu, b_gu, w_d, b_d)
    dest_blocks = dest.reshape(-1, COMBINE_TOKENS, TOP_K).transpose(0, 2, 1).reshape(-1, GATHER_ROWS)
    return _combine(dest_blocks, yb, x1, gates, gate2, final_g, seq)


def _layer(x, c_mod, norm1_g, w_in, mu_shift, w0, w2, a0, a2, g2, k_k, k_a, r_k, gn_w, gn_b, b_f,
           q_norm_g, k_norm_g, o_norm_g, w_out, norm2_g, w_router, b_router, w_gate_up,
           b_gate_up, w_down, b_down, final_g, tm_in, tq, tm_out):
    bsz, seq, _ = x.shape
    shift1, scale1, gate1, shift2, scale2, gate2 = (
        m.reshape(bsz, 1, D_MODEL) for m in jnp.split(c_mod, 6, axis=-1))
    row = lambda v: v.reshape(1, -1)

    w_r = w_in[:, :RWKV_COLS].astype(BF16)
    w_x = w_in[:, RWKV_COLS:RWKV_COLS + FOX_MAIN].astype(BF16)
    w_f = jnp.pad(w_in[:, RWKV_COLS + FOX_MAIN:], ((0, 0), (0, LANES - N_HEADS)))
    b_fp = jnp.pad(b_f, (0, LANES - N_HEADS)).reshape(1, LANES)
    qk_gain = jnp.concatenate([jnp.tile(q_norm_g, N_HEADS) * HEAD_DIM ** -0.5,
                               jnp.tile(k_norm_g, N_HEADS)]).reshape(1, -1)
    p_r, p_x, k_bias, q_bias = _inproj(x, shift1, scale1, row(norm1_g), w_r, w_x, w_f, b_fp,
                                       qk_gain, tm_in)

    zeros = jnp.zeros((LANES - 64, D_GRP), F32)
    w2p = jnp.concatenate([w2, zeros], axis=0).astype(BF16)
    a2p = jnp.concatenate([zeros, a2], axis=0).astype(BF16)
    y_r = _rwkv(p_r, row(mu_shift), row(w0), w2p, row(a0), a2p, g2.astype(BF16), row(k_k),
                row(k_a), row(r_k), row(gn_w), row(gn_b))

    y_f = _fox(p_x, k_bias, q_bias, jnp.tile(o_norm_g, 2).reshape(1, LANES), tq)

    t = bsz * seq
    w_rt = jnp.pad(w_router, ((0, 0), (0, LANES - N_EXPERTS)))
    b_rt = jnp.pad(b_router, (0, LANES - N_EXPERTS)).reshape(1, LANES)
    wo = w_out.astype(BF16)
    x1, h2, idx, gates, rank, cnt = _outproj(
        x.reshape(t, D_MODEL), y_r.reshape(t, D_GRP), y_f.reshape(t, D_GRP), gate1, shift2,
        scale2, row(norm2_g), wo[:D_GRP], wo[D_GRP:], w_rt, b_rt, tm_out, seq)

    counts = cnt[0, :N_EXPERTS].astype(jnp.int32)
    out = _moe(h2, idx[:, :TOP_K], gates, rank[:, :TOP_K], counts, x1, gate2, row(final_g),
               w_gate_up.astype(BF16), b_gate_up.reshape(N_EXPERTS, 1, -1),
               w_down.astype(BF16), b_down.reshape(N_EXPERTS, 1, -1), seq)
    return out.reshape(bsz, seq, D_MODEL)


def kernel(x, c, w_ada, b_ada, norm1_g, w_in, mu_shift, w0, w2, a0, a2, g2, k_k, k_a, r_k, gn_w,
           gn_b, b_f, q_norm_g, k_norm_g, o_norm_g, w_out, norm2_g, w_router, b_router, w_gate_up,
           b_gate_up, w_down, b_down, final_g):
    assert w_ada.shape[0] == 1, "single-layer block"
    c_mod = _adaln(c, w_ada[0], b_ada[0])
    return _layer(x, c_mod, norm1_g[0], w_in[0], mu_shift[0], w0[0], w2[0], a0[0], a2[0], g2[0],
                  k_k[0], k_a[0], r_k[0], gn_w[0], gn_b[0], b_f[0], q_norm_g[0], k_norm_g[0],
                  o_norm_g[0], w_out[0], norm2_g[0], w_router[0], b_router[0], w_gate_up[0],
                  b_gate_up[0], w_down[0], b_down[0], final_g,
                  tm_in=min(512, x.shape[1]), tq=min(256, x.shape[1]), tm_out=min(512, x.shape[1]))
```

```python
import functools

import jax
import jax.numpy as jnp
from jax import lax
from jax.experimental import pallas as pl
from jax.experimental.pallas import tpu as pltpu

F32 = jnp.float32
BF16 = jnp.bfloat16
HIGHEST = lax.Precision.HIGHEST

D_MODEL = 1024
HEAD_DIM = 64
N_HEADS = 8
D_GRP = N_HEADS * HEAD_DIM
RWKV_COLS = 1792
LORA_OFF = 3 * D_GRP
GATE_OFF = LORA_OFF + 128
FOX_MAIN = 4 * D_GRP
N_EXPERTS = 32
TOP_K = 4
EXPERT_BLOCK = 256
SWIGLU_ALPHA = 1.702
SWIGLU_LIMIT = 7.0
NORM_EPS = 1e-6
GN_EPS = 64e-5
LANES = 128
CHUNK = 64
HEADS_PER_SCAN = 4
SCAN_W = HEADS_PER_SCAN * HEAD_DIM
VMEM_LIMIT = 56 * 1024 * 1024


def _dot(a, b):
    return jnp.dot(a.astype(BF16), b.astype(BF16), preferred_element_type=F32)


def _dot_nt(a, b):
    return lax.dot_general(a.astype(BF16), b.astype(BF16), (((1,), (1,)), ((), ())),
                           preferred_element_type=F32)


def _dot_tn(a, b):
    return lax.dot_general(a.astype(BF16), b.astype(BF16), (((0,), (0,)), ((), ())),
                           preferred_element_type=F32)


def _fdot(a, b):
    return jnp.dot(a, b, precision=HIGHEST, preferred_element_type=F32)


def _split_dot(x, m, terms=2, left=False):
    acc = None
    rem = x
    for _ in range(terms):
        part = rem.astype(BF16)
        rem = rem - part.astype(F32)
        d = (jnp.dot(m, part, preferred_element_type=F32) if left
             else jnp.dot(part, m, preferred_element_type=F32))
        acc = d if acc is None else acc + d
    return acc


def _iota(shape, dim):
    return lax.broadcasted_iota(jnp.int32, shape, dim)


def _seg_reduce_mat(n):
    return (_iota((n, LANES), 0) // HEAD_DIM == _iota((n, LANES), 1)).astype(BF16)


def _seg_expand_mat(n):
    return (_iota((LANES, n), 1) // HEAD_DIM == _iota((LANES, n), 0)).astype(BF16)


def _tri(n, strict):
    r, c = _iota((n, n), 0), _iota((n, n), 1)
    return ((r > c) if strict else (r >= c)).astype(BF16)


def _log_sigmoid(z):
    return jnp.minimum(z, 0.0) - jnp.log(1.0 + jnp.exp(-jnp.abs(z)))


def _sigmoid(z):
    return 1.0 / (1.0 + jnp.exp(-z))


def _adaln_kernel(c_ref, w_ref, b_ref, o_ref):
    c = c_ref[...]
    o_ref[...] = _fdot(c * _sigmoid(c), w_ref[...]) + b_ref[...]


def _adaln(c, w_ada, b_ada):
    bsz = c.shape[0]
    n_mod = w_ada.shape[1] // D_MODEL
    return pl.pallas_call(
        _adaln_kernel,
        grid=(n_mod,),
        in_specs=[pl.BlockSpec((bsz, D_MODEL), lambda j: (0, 0)),
                  pl.BlockSpec((D_MODEL, D_MODEL), lambda j: (0, j)),
                  pl.BlockSpec((1, D_MODEL), lambda j: (0, j))],
        out_specs=pl.BlockSpec((bsz, D_MODEL), lambda j: (0, j)),
        out_shape=jax.ShapeDtypeStruct((bsz, n_mod * D_MODEL), F32),
        name="adaln",
    )(c, w_ada, b_ada.reshape(1, -1))


def _inproj_kernel(x_ref, sh_ref, sc_ref, g_ref, wr_ref, wx_ref, wfh_ref, wfl_ref, bf_ref, qkg_ref,
                   pr_ref, px_ref, kb_ref, qb_ref, carry_ref):
    @pl.when(pl.program_id(1) == 0)
    def _():
        carry_ref[...] = jnp.zeros_like(carry_ref)

    x = x_ref[...]
    tm = x.shape[0]
    h = x * lax.rsqrt(jnp.mean(x * x, axis=-1, keepdims=True) + NORM_EPS) * g_ref[...]
    h = h * (1.0 + sc_ref[...]) + sh_ref[...]
    hb = h.astype(BF16)
    h_lo = (h - hb.astype(F32)).astype(BF16)

    pr_ref[...] = jnp.dot(hb, wr_ref[...], preferred_element_type=F32).astype(BF16)

    px = jnp.dot(hb, wx_ref[...], preferred_element_type=F32)
    qk = px[:, :2 * D_GRP]
    ss = _split_dot(qk * qk, _seg_reduce_mat(2 * D_GRP))
    inv = lax.rsqrt(ss * (1.0 / HEAD_DIM) + NORM_EPS)
    qk = qk * _split_dot(inv, _seg_expand_mat(2 * D_GRP)) * qkg_ref[...]
    px_ref[:, :2 * D_GRP] = qk.astype(BF16)
    px_ref[:, 2 * D_GRP:] = px[:, 2 * D_GRP:].astype(BF16)

    z = (jnp.dot(hb, wfh_ref[...], preferred_element_type=F32)
         + jnp.dot(h_lo, wfh_ref[...], preferred_element_type=F32)
         + jnp.dot(hb, wfl_ref[...], preferred_element_type=F32)) + bf_ref[...]
    cum = _split_dot(_log_sigmoid(z), _tri(tm, False), terms=3, left=True) + carry_ref[...]
    carry_ref[...] = cum[tm - 1:tm, :]

    src, dst = _iota((LANES, LANES), 0), _iota((LANES, LANES), 1)
    parts = []
    rem = cum
    for _ in range(3):
        part = rem.astype(BF16)
        rem = rem - part.astype(F32)
        parts.append(part)

    def spread(offset):
        return sum(jnp.dot(part, ((dst == 8 * src + offset + t) & (src < N_HEADS)).astype(BF16),
                           preferred_element_type=F32) for t, part in enumerate(parts))

    slot = _iota((1, LANES), 1) % 8
    kb_ref[...] = (jnp.where((slot >= 3) & (slot < 6), 1.0, 0.0) - spread(0)).astype(BF16)
    qb_ref[...] = (jnp.where(slot < 3, 1.0, 0.0) + spread(3)).astype(BF16)


def _inproj(x, shift, scale, g, w_r, w_x, w_f, b_f, qk_gain, tm):
    w_f_hi = w_f.astype(BF16)
    w_f_lo = (w_f - w_f_hi.astype(F32)).astype(BF16)
    bsz, seq, _ = x.shape
    const = lambda b, s: (0, 0)
    return pl.pallas_call(
        _inproj_kernel,
        grid=(bsz, seq // tm),
        in_specs=[pl.BlockSpec((None, tm, D_MODEL), lambda b, s: (b, s, 0)),
                  pl.BlockSpec((None, 1, D_MODEL), lambda b, s: (b, 0, 0)),
                  pl.BlockSpec((None, 1, D_MODEL), lambda b, s: (b, 0, 0)),
                  pl.BlockSpec((1, D_MODEL), const),
                  pl.BlockSpec((D_MODEL, RWKV_COLS), const),
                  pl.BlockSpec((D_MODEL, FOX_MAIN), const),
                  pl.BlockSpec((D_MODEL, LANES), const),
                  pl.BlockSpec((D_MODEL, LANES), const),
                  pl.BlockSpec((1, LANES), const),
                  pl.BlockSpec((1, 2 * D_GRP), const)],
        out_specs=[pl.BlockSpec((None, tm, RWKV_COLS), lambda b, s: (b, s, 0)),
                   pl.BlockSpec((None, tm, FOX_MAIN), lambda b, s: (b, s, 0)),
                   pl.BlockSpec((None, tm, LANES), lambda b, s: (b, s, 0)),
                   pl.BlockSpec((None, tm, LANES), lambda b, s: (b, s, 0))],
        out_shape=[jax.ShapeDtypeStruct((bsz, seq, RWKV_COLS), BF16),
                   jax.ShapeDtypeStruct((bsz, seq, FOX_MAIN), BF16),
                   jax.ShapeDtypeStruct((bsz, seq, LANES), BF16),
                   jax.ShapeDtypeStruct((bsz, seq, LANES), BF16)],
        scratch_shapes=[pltpu.VMEM((1, LANES), F32)],
        compiler_params=pltpu.CompilerParams(
            dimension_semantics=("parallel", "arbitrary"), vmem_limit_bytes=VMEM_LIMIT),
        name="inproj",
    )(x, shift, scale, g, w_r, w_x, w_f_hi, w_f_lo, b_f, qk_gain)


_NN = (((1,), (0,)), ((), ()))
_NT = (((1,), (1,)), ((), ()))
_TN = (((0,), (0,)), ((), ()))
SCAN_N = HEADS_PER_SCAN * CHUNK
BATCH_PER_STEP = 2
INV_LEVELS = 5
M_HEAD, M_STRICT, M_INCL, M_EYE, M_BASE, M_OFF = 0, 1, 2, 3, 4, 5


def _bdot(a, b, dims):
    return lax.dot_general(a, b, dims, preferred_element_type=F32)


def _scan_masks():
    rr, cc = _iota((SCAN_N, SCAN_W), 0), _iota((SCAN_N, SCAN_W), 1)
    ri, ci = _iota((SCAN_N, SCAN_N), 0), _iota((SCAN_N, SCAN_N), 1)
    same = ri // CHUNK == ci // CHUNK
    masks = [rr // CHUNK == cc // HEAD_DIM, same & (ri > ci), same & (ri >= ci), ri == ci,
             (ri // 2 == ci // 2) & (ri > ci)]
    blk = 2
    while blk < CHUNK:
        masks.append((ri // (2 * blk) == ci // (2 * blk)) & (ri // blk != ci // blk) & (ri > ci))
        blk *= 2
    return jnp.stack(masks).astype(BF16)


def _rwkv_kernel(p_ref, masks_ref, mu_ref, w0_ref, w2_ref, a0_ref, a2_ref, g2_ref, kk_ref, ka_ref,
                 rk_ref, gnw_ref, gnb_ref, o_ref, last_ref, state_ref):
    @pl.when(pl.program_id(1) == 0)
    def _():
        last_ref[...] = jnp.zeros_like(last_ref)
        state_ref[...] = jnp.zeros_like(state_ref)

    mu, w0, w2, a0, a2, g2, k_k, k_a, r_k, gn_w, gn_b = (
        ref[...] for ref in (mu_ref, w0_ref, w2_ref, a0_ref, a2_ref, g2_ref, kk_ref, ka_ref,
                             rk_ref, gnw_ref, gnb_ref))
    rows = BATCH_PER_STEP * CHUNK
    p = p_ref[...].astype(F32).reshape(rows, RWKV_COLS)
    row_id = _iota((rows, 1), 0)
    prev = pltpu.roll(p, 1, axis=0)
    for bb in range(BATCH_PER_STEP):
        prev = jnp.where(row_id == bb * CHUNK, last_ref[bb], prev)
        last_ref[bb] = p[(bb + 1) * CHUNK - 1:(bb + 1) * CHUNK, :]
    pf = p + mu * (prev - p)
    r = pf[:, 0:D_GRP]
    k = pf[:, D_GRP:2 * D_GRP]
    v = pf[:, 2 * D_GRP:3 * D_GRP]
    lora = pf[:, LORA_OFF:GATE_OFF]
    gd = pf[:, GATE_OFF:RWKV_COLS]

    wlog = w0 + _dot(jnp.tanh(lora), w2)
    neg = -wlog
    softplus = jnp.maximum(neg, 0.0) + jnp.log(1.0 + jnp.exp(-jnp.abs(neg)))
    ld = -jnp.exp(-softplus - 0.5)
    a = _sigmoid(a0 + _dot(lora, a2))
    g = _dot(_sigmoid(gd), g2)

    red, exp_m = _seg_reduce_mat(D_GRP), _seg_expand_mat(D_GRP)
    kk = k * k_k
    n2 = _split_dot(kk * kk, red)
    kk = kk * _split_dot(1.0 / jnp.maximum(jnp.sqrt(n2), 1e-12), exp_m)
    k2 = k * (1.0 + (a - 1.0) * k_a)

    tr, tc = _iota((rows, rows), 0), _iota((rows, rows), 1)
    tri = ((tr >= tc) & (tr // CHUNK == tc // CHUNK)).astype(BF16)
    cl = _split_dot(ld, tri, terms=3, left=True)
    cl_end = jnp.concatenate(
        [jnp.broadcast_to(cl[(bb + 1) * CHUNK - 1:(bb + 1) * CHUNK, :], (CHUNK, D_GRP))
         for bb in range(BATCH_PER_STEP)], axis=0)
    e_in = jnp.exp(cl)
    e_out = jnp.exp(-cl)
    e_rem = jnp.exp(cl_end - cl)
    p_end = jnp.exp(cl_end)
    kka = kk * a
    ops = [(-kk * jnp.exp(cl - ld)).astype(BF16), (kka * e_out).astype(BF16),
           (k2 * e_out).astype(BF16), (r * e_in).astype(BF16), v.astype(BF16),
           (kka * e_rem).astype(BF16), (k2 * e_rem).astype(BF16)]

    chains = [(bb, grp) for bb in range(BATCH_PER_STEP)
              for grp in range(N_HEADS // HEADS_PER_SCAN)]
    head_mask = masks_ref[M_HEAD]
    strict, incl = masks_ref[M_STRICT], masks_ref[M_INCL]

    def stacked(op, bb, grp):
        part = op[bb * CHUNK:(bb + 1) * CHUNK, grp * SCAN_W:(grp + 1) * SCAN_W]
        return jnp.concatenate([part] * HEADS_PER_SCAN, axis=0) * head_mask

    xs = [[stacked(op, bb, grp) for op in ops] for bb, grp in chains]
    st = [state_ref[bb, grp] for bb, grp in chains]
    sb = [s.astype(BF16) for s in st]
    nab = [_bdot(x[0], x[1], _NT).astype(BF16) for x in xs]
    aak = [_bdot(x[0], x[2], _NT).astype(BF16) * strict for x in xs]
    arb = [_bdot(x[3], x[1], _NT).astype(BF16) * incl for x in xs]
    ark = [_bdot(x[3], x[2], _NT).astype(BF16) * incl for x in xs]
    t_inv = [masks_ref[M_EYE] + n * masks_ref[M_BASE] for n in nab]
    for lvl in range(INV_LEVELS):
        half = [_bdot(t, n * masks_ref[M_OFF + lvl], _NN).astype(BF16) for t, n in zip(t_inv, nab)]
        t_inv = [t + _bdot(h, t, _NN).astype(BF16) for t, h in zip(t_inv, half)]
    rhs = [(_bdot(x[0], s, _NT) + _bdot(k, x[4], _NN)).astype(BF16)
           for x, s, k in zip(xs, sb, aak)]
    sa = [_bdot(t, h, _NN).astype(BF16) for t, h in zip(t_inv, rhs)]
    ys = [_bdot(x[3], s, _NT) + _bdot(b, u, _NN) + _bdot(k, x[4], _NN)
          for x, s, b, u, k in zip(xs, sb, arb, sa, ark)]
    for (bb, grp), x, s, u in zip(chains, xs, st, sa):
        decay = p_end[bb * CHUNK:bb * CHUNK + 1, grp * SCAN_W:(grp + 1) * SCAN_W]
        state_ref[bb, grp] = s * decay + _bdot(u, x[5], _TN) + _bdot(x[4], x[6], _TN)
    ys = [y[0:CHUNK] + y[CHUNK:2 * CHUNK] + y[2 * CHUNK:3 * CHUNK] + y[3 * CHUNK:4 * CHUNK]
          for y in ys]
    n_grp = N_HEADS // HEADS_PER_SCAN
    y = jnp.concatenate([jnp.concatenate(ys[bb * n_grp:(bb + 1) * n_grp], axis=1)
                         for bb in range(BATCH_PER_STEP)], axis=0)

    mean = _split_dot(_split_dot(y, red) * (1.0 / HEAD_DIM), exp_m)
    d = y - mean
    var = _split_dot(d * d, red) * (1.0 / HEAD_DIM)
    yn = d * _split_dot(lax.rsqrt(var + GN_EPS), exp_m) * gn_w + gn_b
    bonus = _split_dot(_split_dot(r * k2 * r_k, red), exp_m) * v
    o_ref[...] = ((yn + bonus) * g).astype(BF16).reshape(BATCH_PER_STEP, CHUNK, D_GRP)


def _rwkv(p_r, mu, w0, w2p, a0, a2p, g2, k_k, k_a, r_k, gn_w, gn_b):
    bsz, seq, _ = p_r.shape
    assert bsz % BATCH_PER_STEP == 0
    masks = _scan_masks()
    const = lambda b, s: (0, 0)
    vec = pl.BlockSpec((1, D_GRP), const)
    return pl.pallas_call(
        _rwkv_kernel,
        grid=(bsz // BATCH_PER_STEP, seq // CHUNK),
        in_specs=[pl.BlockSpec((BATCH_PER_STEP, CHUNK, RWKV_COLS), lambda b, s: (b, s, 0)),
                  pl.BlockSpec(masks.shape, lambda b, s: (0, 0, 0)),
                  pl.BlockSpec((1, RWKV_COLS), const),
                  vec, pl.BlockSpec((LANES, D_GRP), const),
                  vec, pl.BlockSpec((LANES, D_GRP), const),
                  pl.BlockSpec((LANES, D_GRP), const),
                  vec, vec, vec, vec, vec],
        out_specs=pl.BlockSpec((BATCH_PER_STEP, CHUNK, D_GRP), lambda b, s: (b, s, 0)),
        out_shape=jax.ShapeDtypeStruct((bsz, seq, D_GRP), BF16),
        scratch_shapes=[pltpu.VMEM((BATCH_PER_STEP, 1, RWKV_COLS), F32),
                        pltpu.VMEM((BATCH_PER_STEP, N_HEADS // HEADS_PER_SCAN, SCAN_W, SCAN_W), F32)],
        compiler_params=pltpu.CompilerParams(
            dimension_semantics=("parallel", "arbitrary"), vmem_limit_bytes=VMEM_LIMIT),
        name="rwkv",
    )(p_r, masks, mu, w0, w2p, a0, a2p, g2, k_k, k_a, r_k, gn_w, gn_b)


def _fox_kernel(q_ref, qb_ref, k_ref, kb_ref, vt_ref, og_ref, ong_ref, o_ref, *, tq):
    hp = pl.program_id(1)
    qi = pl.program_id(2)
    lane = _iota((1, LANES), 1)
    q = q_ref[...]
    qb = qb_ref[...]
    zero = jnp.zeros_like(q)
    qcat = [jnp.concatenate([jnp.where(lane // HEAD_DIM == hh, q, zero),
                             jnp.where(lane // 8 == hp * 2 + hh, qb, zero)], axis=1)
            for hh in range(2)]
    key_pos = _iota((tq, tq), 0)
    qry_pos = _iota((tq, tq), 1)

    def tile(j, carry, masked):
        start = pl.multiple_of(j * tq, tq)
        kcat = jnp.concatenate([k_ref[pl.ds(start, tq), :], kb_ref[pl.ds(start, tq), :]], axis=1)
        vt = vt_ref[:, pl.ds(start, tq)]
        sts = [lax.dot_general(kcat, qc, _NT, preferred_element_type=F32) for qc in qcat]
        if masked:
            sts = [jnp.where(qry_pos >= key_pos, st, -jnp.inf) for st in sts]
        m_new = [jnp.maximum(c[0], jnp.max(st, axis=0, keepdims=True)) for c, st in zip(carry, sts)]
        pts = [jnp.exp(st - m) for st, m in zip(sts, m_new)]
        pvs = [jnp.dot(vt, pt.astype(BF16), preferred_element_type=F32) for pt in pts]
        new = []
        for hh in range(2):
            m, l, acc = carry[hh]
            alpha = jnp.exp(m - m_new[hh])
            l = alpha * l + jnp.sum(pts[hh], axis=0, keepdims=True)
            acc = alpha * acc + pvs[hh][hh * HEAD_DIM:(hh + 1) * HEAD_DIM, :]
            new.append((m_new[hh], l, acc))
        return tuple(new)

    init = (jnp.full((1, tq), -jnp.inf, F32), jnp.zeros((1, tq), F32),
            jnp.zeros((HEAD_DIM, tq), F32))
    carry = lax.fori_loop(0, qi, functools.partial(tile, masked=False), (init, init))
    outs = []
    for _, l, acc in tile(qi, carry, True):
        o = acc / l
        outs.append(o * lax.rsqrt(jnp.mean(o * o, axis=0, keepdims=True) + NORM_EPS))
    o = jnp.concatenate(outs, axis=0).T
    o_ref[...] = (o * ong_ref[...] * _sigmoid(og_ref[...].astype(F32))).astype(BF16)


def _fox(p_x, k_bias, q_bias, o_gain, tq):
    bsz, seq, _ = p_x.shape
    npair = N_HEADS // 2
    v_t = jnp.transpose(p_x[:, :, 2 * D_GRP:3 * D_GRP], (0, 2, 1))
    return pl.pallas_call(
        functools.partial(_fox_kernel, tq=tq),
        grid=(bsz, npair, seq // tq),
        in_specs=[pl.BlockSpec((None, tq, LANES), lambda b, h, i: (b, i, h)),
                  pl.BlockSpec((None, tq, LANES), lambda b, h, i: (b, i, 0)),
                  pl.BlockSpec((None, seq, LANES), lambda b, h, i: (b, 0, npair + h)),
                  pl.BlockSpec((None, seq, LANES), lambda b, h, i: (b, 0, 0)),
                  pl.BlockSpec((None, LANES, seq), lambda b, h, i: (b, h, 0)),
                  pl.BlockSpec((None, tq, LANES), lambda b, h, i: (b, i, 3 * npair + h)),
                  pl.BlockSpec((1, LANES), lambda b, h, i: (0, 0))],
        out_specs=pl.BlockSpec((None, tq, LANES), lambda b, h, i: (b, i, h)),
        out_shape=jax.ShapeDtypeStruct((bsz, seq, D_GRP), BF16),
        compiler_params=pltpu.CompilerParams(
            dimension_semantics=("parallel", "parallel", "arbitrary"),
            vmem_limit_bytes=VMEM_LIMIT),
        name="fox",
    )(p_x, q_bias, p_x, k_bias, v_t, p_x, o_gain)


def _outproj_kernel(x_ref, yr_ref, yf_ref, g1_ref, sh_ref, sc_ref, ng_ref, wor_ref, wof_ref,
                    wrt_ref, brt_ref, x1_ref, h2_ref, idx_ref, gate_ref, rank_ref, cnt_ref,
                    carry_ref):
    @pl.when(pl.program_id(0) == 0)
    def _():
        carry_ref[...] = jnp.zeros_like(carry_ref)

    y = (jnp.dot(yr_ref[...], wor_ref[...], preferred_element_type=F32)
         + jnp.dot(yf_ref[...], wof_ref[...], preferred_element_type=F32))
    x1 = x_ref[...] + g1_ref[...] * y
    x1_ref[...] = x1
    tm = x1.shape[0]
    h = x1 * lax.rsqrt(jnp.mean(x1 * x1, axis=-1, keepdims=True) + NORM_EPS) * ng_ref[...]
    h2 = h * (1.0 + sc_ref[...]) + sh_ref[...]
    h2_ref[...] = h2

    lane = _iota((tm, LANES), 1)
    logits = _fdot(h2, wrt_ref[...]) + brt_ref[...]
    lg = jnp.where(lane < N_EXPERTS, logits, -jnp.inf)
    picks = []
    hot_sum = jnp.zeros((tm, LANES), F32)
    for _ in range(TOP_K):
        m = jnp.max(lg, axis=-1, keepdims=True)
        sel = jnp.min(jnp.where(lg == m, lane, LANES), axis=-1, keepdims=True)
        hot = lane == sel
        picks.append((m, sel, hot))
        hot_sum = hot_sum + hot.astype(F32)
        lg = jnp.where(hot, -jnp.inf, lg)
    es = [jnp.exp(m - picks[0][0]) for m, _, _ in picks]
    den = es[0] + es[1] + es[2] + es[3]

    before = jnp.dot(_tri(tm, True), hot_sum.astype(BF16), preferred_element_type=F32)
    before = before + carry_ref[...]
    idx_out = jnp.zeros((tm, LANES), jnp.int32)
    gate_out = jnp.zeros((tm, LANES), F32)
    rank_out = jnp.zeros((tm, LANES), jnp.int32)
    for kk, (m, sel, hot) in enumerate(picks):
        rk = jnp.sum(jnp.where(hot, before, 0.0), axis=-1, keepdims=True).astype(jnp.int32)
        idx_out = jnp.where(lane == kk, sel, idx_out)
        gate_out = jnp.where(lane == kk, es[kk] / den, gate_out)
        rank_out = jnp.where(lane == kk, rk, rank_out)
    idx_ref[...] = idx_out
    gate_ref[...] = gate_out
    rank_ref[...] = rank_out
    carry_ref[...] = carry_ref[...] + jnp.sum(hot_sum, axis=0, keepdims=True)
    cnt_ref[...] = carry_ref[...]


def _outproj(x2d, y_r, y_f, gate1, shift2, scale2, norm_g, wo_r, wo_f, w_rt, b_rt, tm, seq):
    t = x2d.shape[0]
    per_b = seq // tm
    const = lambda i: (0, 0)
    rows = lambda i: (i, 0)
    mod = pl.BlockSpec((None, 1, D_MODEL), lambda i: (i // per_b, 0, 0))
    return pl.pallas_call(
        _outproj_kernel,
        grid=(t // tm,),
        in_specs=[pl.BlockSpec((tm, D_MODEL), rows),
                  pl.BlockSpec((tm, D_GRP), rows),
                  pl.BlockSpec((tm, D_GRP), rows),
                  mod, mod, mod,
                  pl.BlockSpec((1, D_MODEL), const),
                  pl.BlockSpec((D_GRP, D_MODEL), const),
                  pl.BlockSpec((D_GRP, D_MODEL), const),
                  pl.BlockSpec((D_MODEL, LANES), const),
                  pl.BlockSpec((1, LANES), const)],
        out_specs=[pl.BlockSpec((tm, D_MODEL), rows),
                   pl.BlockSpec((tm, D_MODEL), rows),
                   pl.BlockSpec((tm, LANES), rows),
                   pl.BlockSpec((tm, LANES), rows),
                   pl.BlockSpec((tm, LANES), rows),
                   pl.BlockSpec((1, LANES), const)],
        out_shape=[jax.ShapeDtypeStruct((t, D_MODEL), F32),
                   jax.ShapeDtypeStruct((t, D_MODEL), F32),
                   jax.ShapeDtypeStruct((t, LANES), jnp.int32),
                   jax.ShapeDtypeStruct((t, LANES), F32),
                   jax.ShapeDtypeStruct((t, LANES), jnp.int32),
                   jax.ShapeDtypeStruct((1, LANES), F32)],
        scratch_shapes=[pltpu.VMEM((1, LANES), F32)],
        compiler_params=pltpu.CompilerParams(
            dimension_semantics=("arbitrary",), vmem_limit_bytes=VMEM_LIMIT),
        name="outproj",
    )(x2d, y_r, y_f, gate1, shift2, scale2, norm_g, wo_r, wo_f, w_rt, b_rt)


GATHER_ROWS = 1024


def _row_copy(src_hbm, dst, src_row, dst_row, sem):
    return pltpu.make_async_copy(src_hbm.at[pl.ds(src_row, 1), :], dst.at[pl.ds(dst_row, 1), :], sem)


def _gather_kernel(idx_hbm, src_hbm, o_ref, idx_smem, isem, sem):
    i = pl.program_id(0)
    cp = pltpu.make_async_copy(idx_hbm.at[i], idx_smem, isem)
    cp.start()
    cp.wait()

    def issue(r, c):
        _row_copy(src_hbm, o_ref, idx_smem[r], r, sem).start()
        return c

    lax.fori_loop(0, GATHER_ROWS, issue, 0)

    pltpu.make_async_copy(src_hbm.at[pl.ds(0, GATHER_ROWS), :], o_ref, sem).wait()


def _gather_rows(idx2d, src):
    nblk = idx2d.shape[0]
    d = src.shape[1]
    return pl.pallas_call(
        _gather_kernel,
        grid=(nblk,),
        in_specs=[pl.BlockSpec(memory_space=pl.ANY), pl.BlockSpec(memory_space=pl.ANY)],
        out_specs=pl.BlockSpec((GATHER_ROWS, d), lambda i: (i, 0)),
        out_shape=jax.ShapeDtypeStruct((nblk * GATHER_ROWS, d), src.dtype),
        scratch_shapes=[pltpu.SMEM((GATHER_ROWS,), jnp.int32),
                        pltpu.SemaphoreType.DMA, pltpu.SemaphoreType.DMA],
        compiler_params=pltpu.CompilerParams(
            dimension_semantics=("arbitrary",), vmem_limit_bytes=VMEM_LIMIT),
        name="gather",
    )(idx2d, src)


def _expert_kernel(be_ref, x_ref, wgu_ref, bgu_ref, wd_ref, bd_ref, o_ref):
    del be_ref
    gu = jnp.dot(x_ref[...].astype(BF16), wgu_ref[...], preferred_element_type=F32) + bgu_ref[...]
    gate = jnp.minimum(gu[:, :D_MODEL], SWIGLU_LIMIT)
    up = jnp.clip(gu[:, D_MODEL:], -SWIGLU_LIMIT, SWIGLU_LIMIT)
    act = gate * _sigmoid(SWIGLU_ALPHA * gate) * (up + 1.0)
    o_ref[...] = jnp.dot(act.astype(BF16), wd_ref[...], preferred_element_type=F32) + bd_ref[...]


def _experts(block_e, xs, w_gu, b_gu, w_d, b_d):
    n_blocks = block_e.shape[0]
    grid_spec = pltpu.PrefetchScalarGridSpec(
        num_scalar_prefetch=1,
        grid=(n_blocks,),
        in_specs=[pl.BlockSpec((EXPERT_BLOCK, D_MODEL), lambda j, be: (j, 0)),
                  pl.BlockSpec((None, D_MODEL, 2 * D_MODEL), lambda j, be: (be[j], 0, 0)),
                  pl.BlockSpec((None, 1, 2 * D_MODEL), lambda j, be: (be[j], 0, 0)),
                  pl.BlockSpec((None, D_MODEL, D_MODEL), lambda j, be: (be[j], 0, 0)),
                  pl.BlockSpec((None, 1, D_MODEL), lambda j, be: (be[j], 0, 0))],
        out_specs=pl.BlockSpec((EXPERT_BLOCK, D_MODEL), lambda j, be: (j, 0)),
    )
    return pl.pallas_call(
        _expert_kernel,
        grid_spec=grid_spec,
        out_shape=jax.ShapeDtypeStruct(xs.shape, F32),
        compiler_params=pltpu.CompilerParams(
            dimension_semantics=("arbitrary",), vmem_limit_bytes=VMEM_LIMIT),
        name="experts",
    )(block_e, xs, w_gu, b_gu, w_d, b_d)


COMBINE_TOKENS = GATHER_ROWS // TOP_K


def _combine_kernel(dest_hbm, yb_hbm, x1_ref, gate_ref, g2_ref, fg_ref, o_ref,
                    rows_ref, idx_smem, isem, sem):
    i = pl.program_id(0)
    cp = pltpu.make_async_copy(dest_hbm.at[i], idx_smem, isem)
    cp.start()
    cp.wait()

    def issue(r, c):
        _row_copy(yb_hbm, rows_ref, idx_smem[r], r, sem).start()
        return c

    lax.fori_loop(0, GATHER_ROWS, issue, 0)

    pltpu.make_async_copy(yb_hbm.at[pl.ds(0, GATHER_ROWS), :], rows_ref, sem).wait()

    gates = gate_ref[...]
    acc = None
    for kk in range(TOP_K):
        part = gates[:, kk:kk + 1] * rows_ref[kk * COMBINE_TOKENS:(kk + 1) * COMBINE_TOKENS, :]
        acc = part if acc is None else acc + part
    x2 = x1_ref[...] + g2_ref[...] * acc
    o_ref[...] = x2 * lax.rsqrt(jnp.mean(x2 * x2, axis=-1, keepdims=True) + NORM_EPS) * fg_ref[...]


def _combine(dest2d, yb, x1, gates, gate2, final_g, seq):
    t = x1.shape[0]
    tm = COMBINE_TOKENS
    per_b = seq // tm
    rows = lambda i: (i, 0)
    return pl.pallas_call(
        _combine_kernel,
        grid=(t // tm,),
        in_specs=[pl.BlockSpec(memory_space=pl.ANY), pl.BlockSpec(memory_space=pl.ANY),
                  pl.BlockSpec((tm, D_MODEL), rows),
                  pl.BlockSpec((tm, LANES), rows),
                  pl.BlockSpec((None, 1, D_MODEL), lambda i: (i // per_b, 0, 0)),
                  pl.BlockSpec((1, D_MODEL), lambda i: (0, 0))],
        out_specs=pl.BlockSpec((tm, D_MODEL), rows),
        out_shape=jax.ShapeDtypeStruct((t, D_MODEL), F32),
        scratch_shapes=[pltpu.VMEM((GATHER_ROWS, D_MODEL), F32),
                        pltpu.SMEM((GATHER_ROWS,), jnp.int32),
                        pltpu.SemaphoreType.DMA, pltpu.SemaphoreType.DMA],
        compiler_params=pltpu.CompilerParams(
            dimension_semantics=("arbitrary",), vmem_limit_bytes=VMEM_LIMIT),
        name="combine",
    )(dest2d, yb, x1, gates, gate2, final_g)


def _moe(h2, idx, gates, rank, counts, x1, gate2, final_g, w_gu, b_gu, w_d, b_d, seq):
    t = h2.shape[0]
    n_slots = t * TOP_K
    n_blocks = -(-n_slots // EXPERT_BLOCK) + N_EXPERTS
    cap = n_blocks * EXPERT_BLOCK
    assert cap % GATHER_ROWS == 0 and n_slots % GATHER_ROWS == 0
    padded = (counts + EXPERT_BLOCK - 1) // EXPERT_BLOCK * EXPERT_BLOCK
    pad_ends = jnp.cumsum(padded)
    pad_starts = pad_ends - padded
    dest = pad_starts[idx] + rank
    tok = jnp.broadcast_to(jnp.arange(t, dtype=jnp.int32)[:, None], (t, TOP_K))
    buf_tok = jnp.zeros((cap,), jnp.int32).at[dest.reshape(-1)].set(tok.reshape(-1))
    block_starts = jnp.arange(n_blocks, dtype=jnp.int32) * EXPERT_BLOCK
    block_e = jnp.minimum(jnp.sum(block_starts[:, None] >= pad_ends[None, :], axis=1),
                          N_EXPERTS - 1).astype(jnp.int32)

    xs = _gather_rows(buf_tok.reshape(-1, GATHER_ROWS), h2)
    yb = _experts(block_e, xs, w_gu, b_gu, w_d, b_d)
    dest_blocks = dest.reshape(-1, COMBINE_TOKENS, TOP_K).transpose(0, 2, 1).reshape(-1, GATHER_ROWS)
    return _combine(dest_blocks, yb, x1, gates, gate2, final_g, seq)


def _layer(x, c_mod, norm1_g, w_in, mu_shift, w0, w2, a0, a2, g2, k_k, k_a, r_k, gn_w, gn_b, b_f,
           q_norm_g, k_norm_g, o_norm_g, w_out, norm2_g, w_router, b_router, w_gate_up,
           b_gate_up, w_down, b_down, final_g, tm_in, tq, tm_out):
    bsz, seq, _ = x.shape
    shift1, scale1, gate1, shift2, scale2, gate2 = (
        m.reshape(bsz, 1, D_MODEL) for m in jnp.split(c_mod, 6, axis=-1))
    row = lambda v: v.reshape(1, -1)

    w_r = w_in[:, :RWKV_COLS].astype(BF16)
    w_x = w_in[:, RWKV_COLS:RWKV_COLS + FOX_MAIN].astype(BF16)
    w_f = jnp.pad(w_in[:, RWKV_COLS + FOX_MAIN:], ((0, 0), (0, LANES - N_HEADS)))
    b_fp = jnp.pad(b_f, (0, LANES - N_HEADS)).reshape(1, LANES)
    qk_gain = jnp.concatenate([jnp.tile(q_norm_g, N_HEADS) * HEAD_DIM ** -0.5,
                               jnp.tile(k_norm_g, N_HEADS)]).reshape(1, -1)
    p_r, p_x, k_bias, q_bias = _inproj(x, shift1, scale1, row(norm1_g), w_r, w_x, w_f, b_fp,
                                       qk_gain, tm_in)

    zeros = jnp.zeros((LANES - 64, D_GRP), F32)
    w2p = jnp.concatenate([w2, zeros], axis=0).astype(BF16)
    a2p = jnp.concatenate([zeros, a2], axis=0).astype(BF16)
    y_r = _rwkv(p_r, row(mu_shift), row(w0), w2p, row(a0), a2p, g2.astype(BF16), row(k_k),
                row(k_a), row(r_k), row(gn_w), row(gn_b))

    y_f = _fox(p_x, k_bias, q_bias, jnp.tile(o_norm_g, 2).reshape(1, LANES), tq)

    t = bsz * seq
    w_rt = jnp.pad(w_router, ((0, 0), (0, LANES - N_EXPERTS)))
    b_rt = jnp.pad(b_router, (0, LANES - N_EXPERTS)).reshape(1, LANES)
    wo = w_out.astype(BF16)
    x1, h2, idx, gates, rank, cnt = _outproj(
        x.reshape(t, D_MODEL), y_r.reshape(t, D_GRP), y_f.reshape(t, D_GRP), gate1, shift2,
        scale2, row(norm2_g), wo[:D_GRP], wo[D_GRP:], w_rt, b_rt, tm_out, seq)

    counts = cnt[0, :N_EXPERTS].astype(jnp.int32)
    out = _moe(h2, idx[:, :TOP_K], gates, rank[:, :TOP_K], counts, x1, gate2, row(final_g),
               w_gate_up.astype(BF16), b_gate_up.reshape(N_EXPERTS, 1, -1),
               w_down.astype(BF16), b_down.reshape(N_EXPERTS, 1, -1), seq)
    return out.reshape(bsz, seq, D_MODEL)


def kernel(x, c, w_ada, b_ada, norm1_g, w_in, mu_shift, w0, w2, a0, a2, g2, k_k, k_a, r_k, gn_w,
           gn_b, b_f, q_norm_g, k_norm_g, o_norm_g, w_out, norm2_g, w_router, b_router, w_gate_up,
           b_gate_up, w_down, b_down, final_g):
    assert w_ada.shape[0] == 1, "single-layer block"
    c_mod = _adaln(c, w_ada[0], b_ada[0])
    return _layer(x, c_mod, norm1_g[0], w_in[0], mu_shift[0], w0[0], w2[0], a0[0], a2[0], g2[0],
                  k_k[0], k_a[0], r_k[0], gn_w[0], gn_b[0], b_f[0], q_norm_g[0], k_norm_g[0],
                  o_norm_g[0], w_out[0], norm2_g[0], w_router[0], b_router[0], w_gate_up[0],
                  b_gate_up[0], w_down[0], b_down[0], final_g,
                  tm_in=min(512, x.shape[1]), tq=min(256, x.shape[1]), tm_out=min(512, x.shape[1]))
```

```python
import functools

import jax
import jax.numpy as jnp
from jax import lax
from jax.experimental import pallas as pl
from jax.experimental.pallas import tpu as pltpu

F32 = jnp.float32
BF16 = jnp.bfloat16
HIGHEST = lax.Precision.HIGHEST

D_MODEL = 1024
HEAD_DIM = 64
N_HEADS = 8
D_GRP = N_HEADS * HEAD_DIM
RWKV_COLS = 1792
LORA_OFF = 3 * D_GRP
GATE_OFF = LORA_OFF + 128
FOX_MAIN = 4 * D_GRP
N_EXPERTS = 32
TOP_K = 4
EXPERT_BLOCK = 256
SWIGLU_ALPHA = 1.702
SWIGLU_LIMIT = 7.0
NORM_EPS = 1e-6
GN_EPS = 64e-5
LANES = 128
CHUNK = 64
HEADS_PER_SCAN = 4
SCAN_W = HEADS_PER_SCAN * HEAD_DIM
VMEM_LIMIT = 56 * 1024 * 1024


def _dot(a, b):
    return jnp.dot(a.astype(BF16), b.astype(BF16), preferred_element_type=F32)


def _dot_nt(a, b):
    return lax.dot_general(a.astype(BF16), b.astype(BF16), (((1,), (1,)), ((), ())),
                           preferred_element_type=F32)


def _dot_tn(a, b):
    return lax.dot_general(a.astype(BF16), b.astype(BF16), (((0,), (0,)), ((), ())),
                           preferred_element_type=F32)


def _fdot(a, b):
    return jnp.dot(a, b, precision=HIGHEST, preferred_element_type=F32)


def _split_dot(x, m, terms=2, left=False):
    acc = None
    rem = x
    for _ in range(terms):
        part = rem.astype(BF16)
        rem = rem - part.astype(F32)
        d = (jnp.dot(m, part, preferred_element_type=F32) if left
             else jnp.dot(part, m, preferred_element_type=F32))
        acc = d if acc is None else acc + d
    return acc


def _iota(shape, dim):
    return lax.broadcasted_iota(jnp.int32, shape, dim)


def _seg_reduce_mat(n):
    return (_iota((n, LANES), 0) // HEAD_DIM == _iota((n, LANES), 1)).astype(BF16)


def _seg_expand_mat(n):
    return (_iota((LANES, n), 1) // HEAD_DIM == _iota((LANES, n), 0)).astype(BF16)


def _tri(n, strict):
    r, c = _iota((n, n), 0), _iota((n, n), 1)
    return ((r > c) if strict else (r >= c)).astype(BF16)


def _log_sigmoid(z):
    return jnp.minimum(z, 0.0) - jnp.log(1.0 + jnp.exp(-jnp.abs(z)))


def _sigmoid(z):
    return 1.0 / (1.0 + jnp.exp(-z))


def _adaln_kernel(c_ref, w_ref, b_ref, o_ref):
    c = c_ref[...]
    o_ref[...] = _fdot(c * _sigmoid(c), w_ref[...]) + b_ref[...]


def _adaln(c, w_ada, b_ada):
    bsz = c.shape[0]
    n_mod = w_ada.shape[1] // D_MODEL
    return pl.pallas_call(
        _adaln_kernel,
        grid=(n_mod,),
        in_specs=[pl.BlockSpec((bsz, D_MODEL), lambda j: (0, 0)),
                  pl.BlockSpec((D_MODEL, D_MODEL), lambda j: (0, j)),
                  pl.BlockSpec((1, D_MODEL), lambda j: (0, j))],
        out_specs=pl.BlockSpec((bsz, D_MODEL), lambda j: (0, j)),
        out_shape=jax.ShapeDtypeStruct((bsz, n_mod * D_MODEL), F32),
        name="adaln",
    )(c, w_ada, b_ada.reshape(1, -1))


def _inproj_kernel(x_ref, sh_ref, sc_ref, g_ref, wr_ref, wx_ref, wfh_ref, wfl_ref, bf_ref, qkg_ref,
                   pr_ref, px_ref, kb_ref, qb_ref, carry_ref):
    @pl.when(pl.program_id(1) == 0)
    def _():
        carry_ref[...] = jnp.zeros_like(carry_ref)

    x = x_ref[...]
    tm = x.shape[0]
    h = x * lax.rsqrt(jnp.mean(x * x, axis=-1, keepdims=True) + NORM_EPS) * g_ref[...]
    h = h * (1.0 + sc_ref[...]) + sh_ref[...]
    hb = h.astype(BF16)
    h_lo = (h - hb.astype(F32)).astype(BF16)

    pr_ref[...] = jnp.dot(hb, wr_ref[...], preferred_element_type=F32).astype(BF16)

    px = jnp.dot(hb, wx_ref[...], preferred_element_type=F32)
    qk = px[:, :2 * D_GRP]
    ss = _split_dot(qk * qk, _seg_reduce_mat(2 * D_GRP))
    inv = lax.rsqrt(ss * (1.0 / HEAD_DIM) + NORM_EPS)
    qk = qk * _split_dot(inv, _seg_expand_mat(2 * D_GRP)) * qkg_ref[...]
    px_ref[:, :2 * D_GRP] = qk.astype(BF16)
    px_ref[:, 2 * D_GRP:] = px[:, 2 * D_GRP:].astype(BF16)

    z = (jnp.dot(hb, wfh_ref[...], preferred_element_type=F32)
         + jnp.dot(h_lo, wfh_ref[...], preferred_element_type=F32)
         + jnp.dot(hb, wfl_ref[...], preferred_element_type=F32)) + bf_ref[...]
    cum = _split_dot(_log_sigmoid(z), _tri(tm, False), terms=3, left=True) + carry_ref[...]
    carry_ref[...] = cum[tm - 1:tm, :]

    src, dst = _iota((LANES, LANES), 0), _iota((LANES, LANES), 1)
    parts = []
    rem = cum
    for _ in range(3):
        part = rem.astype(BF16)
        rem = rem - part.astype(F32)
        parts.append(part)

    def spread(offset):
        return sum(jnp.dot(part, ((dst == 8 * src + offset + t) & (src < N_HEADS)).astype(BF16),
                           preferred_element_type=F32) for t, part in enumerate(parts))

    slot = _iota((1, LANES), 1) % 8
    kb_ref[...] = (jnp.where((slot >= 3) & (slot < 6), 1.0, 0.0) - spread(0)).astype(BF16)
    qb_ref[...] = (jnp.where(slot < 3, 1.0, 0.0) + spread(3)).astype(BF16)


def _inproj(x, shift, scale, g, w_r, w_x, w_f, b_f, qk_gain, tm):
    w_f_hi = w_f.astype(BF16)
    w_f_lo = (w_f - w_f_hi.astype(F32)).astype(BF16)
    bsz, seq, _ = x.shape
    const = lambda b, s: (0, 0)
    return pl.pallas_call(
        _inproj_kernel,
        grid=(bsz, seq // tm),
        in_specs=[pl.BlockSpec((None, tm, D_MODEL), lambda b, s: (b, s, 0)),
                  pl.BlockSpec((None, 1, D_MODEL), lambda b, s: (b, 0, 0)),
                  pl.BlockSpec((None, 1, D_MODEL), lambda b, s: (b, 0, 0)),
                  pl.BlockSpec((1, D_MODEL), const),
                  pl.BlockSpec((D_MODEL, RWKV_COLS), const),
                  pl.BlockSpec((D_MODEL, FOX_MAIN), const),
                  pl.BlockSpec((D_MODEL, LANES), const),
                  pl.BlockSpec((D_MODEL, LANES), const),
                  pl.BlockSpec((1, LANES), const),
                  pl.BlockSpec((1, 2 * D_GRP), const)],
        out_specs=[pl.BlockSpec((None, tm, RWKV_COLS), lambda b, s: (b, s, 0)),
                   pl.BlockSpec((None, tm, FOX_MAIN), lambda b, s: (b, s, 0)),
                   pl.BlockSpec((None, tm, LANES), lambda b, s: (b, s, 0)),
                   pl.BlockSpec((None, tm, LANES), lambda b, s: (b, s, 0))],
        out_shape=[jax.ShapeDtypeStruct((bsz, seq, RWKV_COLS), BF16),
                   jax.ShapeDtypeStruct((bsz, seq, FOX_MAIN), BF16),
                   jax.ShapeDtypeStruct((bsz, seq, LANES), BF16),
                   jax.ShapeDtypeStruct((bsz, seq, LANES), BF16)],
        scratch_shapes=[pltpu.VMEM((1, LANES), F32)],
        compiler_params=pltpu.CompilerParams(
            dimension_semantics=("parallel", "arbitrary"), vmem_limit_bytes=VMEM_LIMIT),
        name="inproj",
    )(x, shift, scale, g, w_r, w_x, w_f_hi, w_f_lo, b_f, qk_gain)


_NN = (((1,), (0,)), ((), ()))
_NT = (((1,), (1,)), ((), ()))
_TN = (((0,), (0,)), ((), ()))
SCAN_N = HEADS_PER_SCAN * CHUNK
BATCH_PER_STEP = 4
INV_LEVELS = 5
M_HEAD, M_STRICT, M_INCL, M_EYE, M_BASE, M_OFF = 0, 1, 2, 3, 4, 5


def _bdot(a, b, dims):
    return lax.dot_general(a, b, dims, preferred_element_type=F32)


def _scan_masks():
    rr, cc = _iota((SCAN_N, SCAN_W), 0), _iota((SCAN_N, SCAN_W), 1)
    ri, ci = _iota((SCAN_N, SCAN_N), 0), _iota((SCAN_N, SCAN_N), 1)
    same = ri // CHUNK == ci // CHUNK
    masks = [rr // CHUNK == cc // HEAD_DIM, same & (ri > ci), same & (ri >= ci), ri == ci,
             (ri // 2 == ci // 2) & (ri > ci)]
    blk = 2
    while blk < CHUNK:
        masks.append((ri // (2 * blk) == ci // (2 * blk)) & (ri // blk != ci // blk) & (ri > ci))
        blk *= 2
    return jnp.stack(masks).astype(BF16)


def _rwkv_kernel(p_ref, masks_ref, mu_ref, w0_ref, w2_ref, a0_ref, a2_ref, g2_ref, kk_ref, ka_ref,
                 rk_ref, gnw_ref, gnb_ref, o_ref, last_ref, state_ref):
    @pl.when(pl.program_id(1) == 0)
    def _():
        last_ref[...] = jnp.zeros_like(last_ref)
        state_ref[...] = jnp.zeros_like(state_ref)

    mu, w0, w2, a0, a2, g2, k_k, k_a, r_k, gn_w, gn_b = (
        ref[...] for ref in (mu_ref, w0_ref, w2_ref, a0_ref, a2_ref, g2_ref, kk_ref, ka_ref,
                             rk_ref, gnw_ref, gnb_ref))
    rows = BATCH_PER_STEP * CHUNK
    p = p_ref[...].astype(F32).reshape(rows, RWKV_COLS)
    row_id = _iota((rows, 1), 0)
    prev = pltpu.roll(p, 1, axis=0)
    for bb in range(BATCH_PER_STEP):
        prev = jnp.where(row_id == bb * CHUNK, last_ref[bb], prev)
        last_ref[bb] = p[(bb + 1) * CHUNK - 1:(bb + 1) * CHUNK, :]
    pf = p + mu * (prev - p)
    r = pf[:, 0:D_GRP]
    k = pf[:, D_GRP:2 * D_GRP]
    v = pf[:, 2 * D_GRP:3 * D_GRP]
    lora = pf[:, LORA_OFF:GATE_OFF]
    gd = pf[:, GATE_OFF:RWKV_COLS]

    wlog = w0 + _dot(jnp.tanh(lora), w2)
    neg = -wlog
    softplus = jnp.maximum(neg, 0.0) + jnp.log(1.0 + jnp.exp(-jnp.abs(neg)))
    ld = -jnp.exp(-softplus - 0.5)
    a = _sigmoid(a0 + _dot(lora, a2))
    g = _dot(_sigmoid(gd), g2)

    red, exp_m = _seg_reduce_mat(D_GRP), _seg_expand_mat(D_GRP)
    kk = k * k_k
    n2 = _split_dot(kk * kk, red)
    kk = kk * _split_dot(1.0 / jnp.maximum(jnp.sqrt(n2), 1e-12), exp_m)
    k2 = k * (1.0 + (a - 1.0) * k_a)

    tr, tc = _iota((rows, rows), 0), _iota((rows, rows), 1)
    tri = ((tr >= tc) & (tr // CHUNK == tc // CHUNK)).astype(BF16)
    cl = _split_dot(ld, tri, terms=3, left=True)
    cl_end = jnp.concatenate(
        [jnp.broadcast_to(cl[(bb + 1) * CHUNK - 1:(bb + 1) * CHUNK, :], (CHUNK, D_GRP))
         for bb in range(BATCH_PER_STEP)], axis=0)
    e_in = jnp.exp(cl)
    e_out = jnp.exp(-cl)
    e_rem = jnp.exp(cl_end - cl)
    p_end = jnp.exp(cl_end)
    kka = kk * a
    ops = [(-kk * jnp.exp(cl - ld)).astype(BF16), (kka * e_out).astype(BF16),
           (k2 * e_out).astype(BF16), (r * e_in).astype(BF16), v.astype(BF16),
           (kka * e_rem).astype(BF16), (k2 * e_rem).astype(BF16)]

    chains = [(bb, grp) for bb in range(BATCH_PER_STEP)
              for grp in range(N_HEADS // HEADS_PER_SCAN)]
    head_mask = masks_ref[M_HEAD]
    strict, incl = masks_ref[M_STRICT], masks_ref[M_INCL]

    def stacked(op, bb, grp):
        part = op[bb * CHUNK:(bb + 1) * CHUNK, grp * SCAN_W:(grp + 1) * SCAN_W]
        return jnp.concatenate([part] * HEADS_PER_SCAN, axis=0) * head_mask

    xs = [[stacked(op, bb, grp) for op in ops] for bb, grp in chains]
    st = [state_ref[bb, grp] for bb, grp in chains]
    sb = [s.astype(BF16) for s in st]
    nab = [_bdot(x[0], x[1], _NT).astype(BF16) for x in xs]
    aak = [_bdot(x[0], x[2], _NT).astype(BF16) * strict for x in xs]
    arb = [_bdot(x[3], x[1], _NT).astype(BF16) * incl for x in xs]
    ark = [_bdot(x[3], x[2], _NT).astype(BF16) * incl for x in xs]
    t_inv = [masks_ref[M_EYE] + n * masks_ref[M_BASE] for n in nab]
    for lvl in range(INV_LEVELS):
        half = [_bdot(t, n * masks_ref[M_OFF + lvl], _NN).astype(BF16) for t, n in zip(t_inv, nab)]
        t_inv = [t + _bdot(h, t, _NN).astype(BF16) for t, h in zip(t_inv, half)]
    rhs = [(_bdot(x[0], s, _NT) + _bdot(k, x[4], _NN)).astype(BF16)
           for x, s, k in zip(xs, sb, aak)]
    sa = [_bdot(t, h, _NN).astype(BF16) for t, h in zip(t_inv, rhs)]
    ys = [_bdot(x[3], s, _NT) + _bdot(b, u, _NN) + _bdot(k, x[4], _NN)
          for x, s, b, u, k in zip(xs, sb, arb, sa, ark)]
    for (bb, grp), x, s, u in zip(chains, xs, st, sa):
        decay = p_end[bb * CHUNK:bb * CHUNK + 1, grp * SCAN_W:(grp + 1) * SCAN_W]
        state_ref[bb, grp] = s * decay + _bdot(u, x[5], _TN) + _bdot(x[4], x[6], _TN)
    ys = [y[0:CHUNK] + y[CHUNK:2 * CHUNK] + y[2 * CHUNK:3 * CHUNK] + y[3 * CHUNK:4 * CHUNK]
          for y in ys]
    n_grp = N_HEADS // HEADS_PER_SCAN
    y = jnp.concatenate([jnp.concatenate(ys[bb * n_grp:(bb + 1) * n_grp], axis=1)
                         for bb in range(BATCH_PER_STEP)], axis=0)

    mean = _split_dot(_split_dot(y, red) * (1.0 / HEAD_DIM), exp_m)
    d = y - mean
    var = _split_dot(d * d, red) * (1.0 / HEAD_DIM)
    yn = d * _split_dot(lax.rsqrt(var + GN_EPS), exp_m) * gn_w + gn_b
    bonus = _split_dot(_split_dot(r * k2 * r_k, red), exp_m) * v
    o_ref[...] = ((yn + bonus) * g).astype(BF16).reshape(BATCH_PER_STEP, CHUNK, D_GRP)


def _rwkv(p_r, mu, w0, w2p, a0, a2p, g2, k_k, k_a, r_k, gn_w, gn_b):
    bsz, seq, _ = p_r.shape
    assert bsz % BATCH_PER_STEP == 0
    masks = _scan_masks()
    const = lambda b, s: (0, 0)
    vec = pl.BlockSpec((1, D_GRP), const)
    return pl.pallas_call(
        _rwkv_kernel,
        grid=(bsz // BATCH_PER_STEP, seq // CHUNK),
        in_specs=[pl.BlockSpec((BATCH_PER_STEP, CHUNK, RWKV_COLS), lambda b, s: (b, s, 0)),
                  pl.BlockSpec(masks.shape, lambda b, s: (0, 0, 0)),
                  pl.BlockSpec((1, RWKV_COLS), const),
                  vec, pl.BlockSpec((LANES, D_GRP), const),
                  vec, pl.BlockSpec((LANES, D_GRP), const),
                  pl.BlockSpec((LANES, D_GRP), const),
                  vec, vec, vec, vec, vec],
        out_specs=pl.BlockSpec((BATCH_PER_STEP, CHUNK, D_GRP), lambda b, s: (b, s, 0)),
        out_shape=jax.ShapeDtypeStruct((bsz, seq, D_GRP), BF16),
        scratch_shapes=[pltpu.VMEM((BATCH_PER_STEP, 1, RWKV_COLS), F32),
                        pltpu.VMEM((BATCH_PER_STEP, N_HEADS // HEADS_PER_SCAN, SCAN_W, SCAN_W), F32)],
        compiler_params=pltpu.CompilerParams(
            dimension_semantics=("parallel", "arbitrary"), vmem_limit_bytes=VMEM_LIMIT),
        name="rwkv",
    )(p_r, masks, mu, w0, w2p, a0, a2p, g2, k_k, k_a, r_k, gn_w, gn_b)


def _fox_kernel(q_ref, qb_ref, k_ref, kb_ref, vt_ref, og_ref, ong_ref, o_ref, *, tq):
    hp = pl.program_id(1)
    qi = pl.program_id(2)
    lane = _iota((1, LANES), 1)
    q = q_ref[...]
    qb = qb_ref[...]
    zero = jnp.zeros_like(q)
    qcat = [jnp.concatenate([jnp.where(lane // HEAD_DIM == hh, q, zero),
                             jnp.where(lane // 8 == hp * 2 + hh, qb, zero)], axis=1)
            for hh in range(2)]
    key_pos = _iota((tq, tq), 0)
    qry_pos = _iota((tq, tq), 1)

    def tile(j, carry, masked):
        start = pl.multiple_of(j * tq, tq)
        kcat = jnp.concatenate([k_ref[pl.ds(start, tq), :], kb_ref[pl.ds(start, tq), :]], axis=1)
        vt = vt_ref[:, pl.ds(start, tq)]
        sts = [lax.dot_general(kcat, qc, _NT, preferred_element_type=F32) for qc in qcat]
        if masked:
            sts = [jnp.where(qry_pos >= key_pos, st, -jnp.inf) for st in sts]
        m_new = [jnp.maximum(c[0], jnp.max(st, axis=0, keepdims=True)) for c, st in zip(carry, sts)]
        pts = [jnp.exp(st - m) for st, m in zip(sts, m_new)]
        pvs = [jnp.dot(vt, pt.astype(BF16), preferred_element_type=F32) for pt in pts]
        new = []
        for hh in range(2):
            m, l, acc = carry[hh]
            alpha = jnp.exp(m - m_new[hh])
            l = alpha * l + jnp.sum(pts[hh], axis=0, keepdims=True)
            acc = alpha * acc + pvs[hh][hh * HEAD_DIM:(hh + 1) * HEAD_DIM, :]
            new.append((m_new[hh], l, acc))
        return tuple(new)

    init = (jnp.full((1, tq), -jnp.inf, F32), jnp.zeros((1, tq), F32),
            jnp.zeros((HEAD_DIM, tq), F32))
    carry = lax.fori_loop(0, qi, functools.partial(tile, masked=False), (init, init))
    outs = []
    for _, l, acc in tile(qi, carry, True):
        o = acc / l
        outs.append(o * lax.rsqrt(jnp.mean(o * o, axis=0, keepdims=True) + NORM_EPS))
    o = jnp.concatenate(outs, axis=0).T
    o_ref[...] = (o * ong_ref[...] * _sigmoid(og_ref[...].astype(F32))).astype(BF16)


def _fox(p_x, k_bias, q_bias, o_gain, tq):
    bsz, seq, _ = p_x.shape
    npair = N_HEADS // 2
    v_t = jnp.transpose(p_x[:, :, 2 * D_GRP:3 * D_GRP], (0, 2, 1))
    return pl.pallas_call(
        functools.partial(_fox_kernel, tq=tq),
        grid=(bsz, npair, seq // tq),
        in_specs=[pl.BlockSpec((None, tq, LANES), lambda b, h, i: (b, i, h)),
                  pl.BlockSpec((None, tq, LANES), lambda b, h, i: (b, i, 0)),
                  pl.BlockSpec((None, seq, LANES), lambda b, h, i: (b, 0, npair + h)),
                  pl.BlockSpec((None, seq, LANES), lambda b, h, i: (b, 0, 0)),
                  pl.BlockSpec((None, LANES, seq), lambda b, h, i: (b, h, 0)),
                  pl.BlockSpec((None, tq, LANES), lambda b, h, i: (b, i, 3 * npair + h)),
                  pl.BlockSpec((1, LANES), lambda b, h, i: (0, 0))],
        out_specs=pl.BlockSpec((None, tq, LANES), lambda b, h, i: (b, i, h)),
        out_shape=jax.ShapeDtypeStruct((bsz, seq, D_GRP), BF16),
        compiler_params=pltpu.CompilerParams(
            dimension_semantics=("parallel", "parallel", "arbitrary"),
            vmem_limit_bytes=VMEM_LIMIT),
        name="fox",
    )(p_x, q_bias, p_x, k_bias, v_t, p_x, o_gain)


def _outproj_kernel(x_ref, yr_ref, yf_ref, g1_ref, sh_ref, sc_ref, ng_ref, wor_ref, wof_ref,
                    wrt_ref, brt_ref, x1_ref, h2_ref, idx_ref, gate_ref, rank_ref, cnt_ref,
                    carry_ref):
    @pl.when(pl.program_id(0) == 0)
    def _():
        carry_ref[...] = jnp.zeros_like(carry_ref)

    y = (jnp.dot(yr_ref[...], wor_ref[...], preferred_element_type=F32)
         + jnp.dot(yf_ref[...], wof_ref[...], preferred_element_type=F32))
    x1 = x_ref[...] + g1_ref[...] * y
    x1_ref[...] = x1
    tm = x1.shape[0]
    h = x1 * lax.rsqrt(jnp.mean(x1 * x1, axis=-1, keepdims=True) + NORM_EPS) * ng_ref[...]
    h2 = h * (1.0 + sc_ref[...]) + sh_ref[...]
    h2_ref[...] = h2

    lane = _iota((tm, LANES), 1)
    logits = _fdot(h2, wrt_ref[...]) + brt_ref[...]
    lg = jnp.where(lane < N_EXPERTS, logits, -jnp.inf)
    picks = []
    hot_sum = jnp.zeros((tm, LANES), F32)
    for _ in range(TOP_K):
        m = jnp.max(lg, axis=-1, keepdims=True)
        sel = jnp.min(jnp.where(lg == m, lane, LANES), axis=-1, keepdims=True)
        hot = lane == sel
        picks.append((m, sel, hot))
        hot_sum = hot_sum + hot.astype(F32)
        lg = jnp.where(hot, -jnp.inf, lg)
    es = [jnp.exp(m - picks[0][0]) for m, _, _ in picks]
    den = es[0] + es[1] + es[2] + es[3]

    before = jnp.dot(_tri(tm, True), hot_sum.astype(BF16), preferred_element_type=F32)
    before = before + carry_ref[...]
    idx_out = jnp.zeros((tm, LANES), jnp.int32)
    gate_out = jnp.zeros((tm, LANES), F32)
    rank_out = jnp.zeros((tm, LANES), jnp.int32)
    for kk, (m, sel, hot) in enumerate(picks):
        rk = jnp.sum(jnp.where(hot, before, 0.0), axis=-1, keepdims=True).astype(jnp.int32)
        idx_out = jnp.where(lane == kk, sel, idx_out)
        gate_out = jnp.where(lane == kk, es[kk] / den, gate_out)
        rank_out = jnp.where(lane == kk, rk, rank_out)
    idx_ref[...] = idx_out
    gate_ref[...] = gate_out
    rank_ref[...] = rank_out
    carry_ref[...] = carry_ref[...] + jnp.sum(hot_sum, axis=0, keepdims=True)
    cnt_ref[...] = carry_ref[...]


def _outproj(x2d, y_r, y_f, gate1, shift2, scale2, norm_g, wo_r, wo_f, w_rt, b_rt, tm, seq):
    t = x2d.shape[0]
    per_b = seq // tm
    const = lambda i: (0, 0)
    rows = lambda i: (i, 0)
    mod = pl.BlockSpec((None, 1, D_MODEL), lambda i: (i // per_b, 0, 0))
    return pl.pallas_call(
        _outproj_kernel,
        grid=(t // tm,),
        in_specs=[pl.BlockSpec((tm, D_MODEL), rows),
                  pl.BlockSpec((tm, D_GRP), rows),
                  pl.BlockSpec((tm, D_GRP), rows),
                  mod, mod, mod,
                  pl.BlockSpec((1, D_MODEL), const),
                  pl.BlockSpec((D_GRP, D_MODEL), const),
                  pl.BlockSpec((D_GRP, D_MODEL), const),
                  pl.BlockSpec((D_MODEL, LANES), const),
                  pl.BlockSpec((1, LANES), const)],
        out_specs=[pl.BlockSpec((tm, D_MODEL), rows),
                   pl.BlockSpec((tm, D_MODEL), rows),
                   pl.BlockSpec((tm, LANES), rows),
                   pl.BlockSpec((tm, LANES), rows),
                   pl.BlockSpec((tm, LANES), rows),
                   pl.BlockSpec((1, LANES), const)],
        out_shape=[jax.ShapeDtypeStruct((t, D_MODEL), F32),
                   jax.ShapeDtypeStruct((t, D_MODEL), F32),
                   jax.ShapeDtypeStruct((t, LANES), jnp.int32),
                   jax.ShapeDtypeStruct((t, LANES), F32),
                   jax.ShapeDtypeStruct((t, LANES), jnp.int32),
                   jax.ShapeDtypeStruct((1, LANES), F32)],
        scratch_shapes=[pltpu.VMEM((1, LANES), F32)],
        compiler_params=pltpu.CompilerParams(
            dimension_semantics=("arbitrary",), vmem_limit_bytes=VMEM_LIMIT),
        name="outproj",
    )(x2d, y_r, y_f, gate1, shift2, scale2, norm_g, wo_r, wo_f, w_rt, b_rt)


GATHER_ROWS = 1024


def _row_copy(src_hbm, dst, src_row, dst_row, sem):
    return pltpu.make_async_copy(src_hbm.at[pl.ds(src_row, 1), :], dst.at[pl.ds(dst_row, 1), :], sem)


def _gather_kernel(idx_hbm, src_hbm, o_ref, idx_smem, isem, sem):
    i = pl.program_id(0)
    cp = pltpu.make_async_copy(idx_hbm.at[i], idx_smem, isem)
    cp.start()
    cp.wait()

    def issue(r, c):
        _row_copy(src_hbm, o_ref, idx_smem[r], r, sem).start()
        return c

    lax.fori_loop(0, GATHER_ROWS, issue, 0)

    pltpu.make_async_copy(src_hbm.at[pl.ds(0, GATHER_ROWS), :], o_ref, sem).wait()


def _gather_rows(idx2d, src):
    nblk = idx2d.shape[0]
    d = src.shape[1]
    return pl.pallas_call(
        _gather_kernel,
        grid=(nblk,),
        in_specs=[pl.BlockSpec(memory_space=pl.ANY), pl.BlockSpec(memory_space=pl.ANY)],
        out_specs=pl.BlockSpec((GATHER_ROWS, d), lambda i: (i, 0)),
        out_shape=jax.ShapeDtypeStruct((nblk * GATHER_ROWS, d), src.dtype),
        scratch_shapes=[pltpu.SMEM((GATHER_ROWS,), jnp.int32),
                        pltpu.SemaphoreType.DMA, pltpu.SemaphoreType.DMA],
        compiler_params=pltpu.CompilerParams(
            dimension_semantics=("arbitrary",), vmem_limit_bytes=VMEM_LIMIT),
        name="gather",
    )(idx2d, src)


def _expert_kernel(be_ref, x_ref, wgu_ref, bgu_ref, wd_ref, bd_ref, o_ref):
    del be_ref
    gu = jnp.dot(x_ref[...].astype(BF16), wgu_ref[...], preferred_element_type=F32) + bgu_ref[...]
    gate = jnp.minimum(gu[:, :D_MODEL], SWIGLU_LIMIT)
    up = jnp.clip(gu[:, D_MODEL:], -SWIGLU_LIMIT, SWIGLU_LIMIT)
    act = gate * _sigmoid(SWIGLU_ALPHA * gate) * (up + 1.0)
    o_ref[...] = jnp.dot(act.astype(BF16), wd_ref[...], preferred_element_type=F32) + bd_ref[...]


def _experts(block_e, xs, w_gu, b_gu, w_d, b_d):
    n_blocks = block_e.shape[0]
    grid_spec = pltpu.PrefetchScalarGridSpec(
        num_scalar_prefetch=1,
        grid=(n_blocks,),
        in_specs=[pl.BlockSpec((EXPERT_BLOCK, D_MODEL), lambda j, be: (j, 0)),
                  pl.BlockSpec((None, D_MODEL, 2 * D_MODEL), lambda j, be: (be[j], 0, 0)),
                  pl.BlockSpec((None, 1, 2 * D_MODEL), lambda j, be: (be[j], 0, 0)),
                  pl.BlockSpec((None, D_MODEL, D_MODEL), lambda j, be: (be[j], 0, 0)),
                  pl.BlockSpec((None, 1, D_MODEL), lambda j, be: (be[j], 0, 0))],
        out_specs=pl.BlockSpec((EXPERT_BLOCK, D_MODEL), lambda j, be: (j, 0)),
    )
    return pl.pallas_call(
        _expert_kernel,
        grid_spec=grid_spec,
        out_shape=jax.ShapeDtypeStruct(xs.shape, F32),
        compiler_params=pltpu.CompilerParams(
            dimension_semantics=("arbitrary",), vmem_limit_bytes=VMEM_LIMIT),
        name="experts",
    )(block_e, xs, w_gu, b_gu, w_d, b_d)


COMBINE_TOKENS = GATHER_ROWS // TOP_K


def _combine_kernel(dest_hbm, yb_hbm, x1_ref, gate_ref, g2_ref, fg_ref, o_ref,
                    rows_ref, idx_smem, isem, sem):
    i = pl.program_id(0)
    cp = pltpu.make_async_copy(dest_hbm.at[i], idx_smem, isem)
    cp.start()
    cp.wait()

    def issue(r, c):
        _row_copy(yb_hbm, rows_ref, idx_smem[r], r, sem).start()
        return c

    lax.fori_loop(0, GATHER_ROWS, issue, 0)

    pltpu.make_async_copy(yb_hbm.at[pl.ds(0, GATHER_ROWS), :], rows_ref, sem).wait()

    gates = gate_ref[...]
    acc = None
    for kk in range(TOP_K):
        part = gates[:, kk:kk + 1] * rows_ref[kk * COMBINE_TOKENS:(kk + 1) * COMBINE_TOKENS, :]
        acc = part if acc is None else acc + part
    x2 = x1_ref[...] + g2_ref[...] * acc
    o_ref[...] = x2 * lax.rsqrt(jnp.mean(x2 * x2, axis=-1, keepdims=True) + NORM_EPS) * fg_ref[...]


def _combine(dest2d, yb, x1, gates, gate2, final_g, seq):
    t = x1.shape[0]
    tm = COMBINE_TOKENS
    per_b = seq // tm
    rows = lambda i: (i, 0)
    return pl.pallas_call(
        _combine_kernel,
        grid=(t // tm,),
        in_specs=[pl.BlockSpec(memory_space=pl.ANY), pl.BlockSpec(memory_space=pl.ANY),
                  pl.BlockSpec((tm, D_MODEL), rows),
                  pl.BlockSpec((tm, LANES), rows),
                  pl.BlockSpec((None, 1, D_MODEL), lambda i: (i // per_b, 0, 0)),
                  pl.BlockSpec((1, D_MODEL), lambda i: (0, 0))],
        out_specs=pl.BlockSpec((tm, D_MODEL), rows),
        out_shape=jax.ShapeDtypeStruct((t, D_MODEL), F32),
        scratch_shapes=[pltpu.VMEM((GATHER_ROWS, D_MODEL), F32),
                        pltpu.SMEM((GATHER_ROWS,), jnp.int32),
                        pltpu.SemaphoreType.DMA, pltpu.SemaphoreType.DMA],
        compiler_params=pltpu.CompilerParams(
            dimension_semantics=("arbitrary",), vmem_limit_bytes=VMEM_LIMIT),
        name="combine",
    )(dest2d, yb, x1, gates, gate2, final_g)


def _moe(h2, idx, gates, rank, counts, x1, gate2, final_g, w_gu, b_gu, w_d, b_d, seq):
    t = h2.shape[0]
    n_slots = t * TOP_K
    n_blocks = -(-n_slots // EXPERT_BLOCK) + N_EXPERTS
    cap = n_blocks * EXPERT_BLOCK
    assert cap % GATHER_ROWS == 0 and n_slots % GATHER_ROWS == 0
    padded = (counts + EXPERT_BLOCK - 1) // EXPERT_BLOCK * EXPERT_BLOCK
    pad_ends = jnp.cumsum(padded)
    pad_starts = pad_ends - padded
    dest = pad_starts[idx] + rank
    tok = jnp.broadcast_to(jnp.arange(t, dtype=jnp.int32)[:, None], (t, TOP_K))
    buf_tok = jnp.zeros((cap,), jnp.int32).at[dest.reshape(-1)].set(tok.reshape(-1))
    block_starts = jnp.arange(n_blocks, dtype=jnp.int32) * EXPERT_BLOCK
    block_e = jnp.minimum(jnp.sum(block_starts[:, None] >= pad_ends[None, :], axis=1),
                          N_EXPERTS - 1).astype(jnp.int32)

    xs = _gather_rows(buf_tok.reshape(-1, GATHER_ROWS), h2)
    yb = _experts(block_e, xs, w_gu, b_gu, w_d, b_d)
    dest_blocks = dest.reshape(-1, COMBINE_TOKENS, TOP_K).transpose(0, 2, 1).reshape(-1, GATHER_ROWS)
    return _combine(dest_blocks, yb, x1, gates, gate2, final_g, seq)


def _layer(x, c_mod, norm1_g, w_in, mu_shift, w0, w2, a0, a2, g2, k_k, k_a, r_k, gn_w, gn_b, b_f,
           q_norm_g, k_norm_g, o_norm_g, w_out, norm2_g, w_router, b_router, w_gate_up,
           b_gate_up, w_down, b_down, final_g, tm_in, tq, tm_out):
    bsz, seq, _ = x.shape
    shift1, scale1, gate1, shift2, scale2, gate2 = (
        m.reshape(bsz, 1, D_MODEL) for m in jnp.split(c_mod, 6, axis=-1))
    row = lambda v: v.reshape(1, -1)

    w_r = w_in[:, :RWKV_COLS].astype(BF16)
    w_x = w_in[:, RWKV_COLS:RWKV_COLS + FOX_MAIN].astype(BF16)
    w_f = jnp.pad(w_in[:, RWKV_COLS + FOX_MAIN:], ((0, 0), (0, LANES - N_HEADS)))
    b_fp = jnp.pad(b_f, (0, LANES - N_HEADS)).reshape(1, LANES)
    qk_gain = jnp.concatenate([jnp.tile(q_norm_g, N_HEADS) * HEAD_DIM ** -0.5,
                               jnp.tile(k_norm_g, N_HEADS)]).reshape(1, -1)
    p_r, p_x, k_bias, q_bias = _inproj(x, shift1, scale1, row(norm1_g), w_r, w_x, w_f, b_fp,
                                       qk_gain, tm_in)

    zeros = jnp.zeros((LANES - 64, D_GRP), F32)
    w2p = jnp.concatenate([w2, zeros], axis=0).astype(BF16)
    a2p = jnp.concatenate([zeros, a2], axis=0).astype(BF16)
    y_r = _rwkv(p_r, row(mu_shift), row(w0), w2p, row(a0), a2p, g2.astype(BF16), row(k_k),
                row(k_a), row(r_k), row(gn_w), row(gn_b))

    y_f = _fox(p_x, k_bias, q_bias, jnp.tile(o_norm_g, 2).reshape(1, LANES), tq)

    t = bsz * seq
    w_rt = jnp.pad(w_router, ((0, 0), (0, LANES - N_EXPERTS)))
    b_rt = jnp.pad(b_router, (0, LANES - N_EXPERTS)).reshape(1, LANES)
    wo = w_out.astype(BF16)
    x1, h2, idx, gates, rank, cnt = _outproj(
        x.reshape(t, D_MODEL), y_r.reshape(t, D_GRP), y_f.reshape(t, D_GRP), gate1, shift2,
        scale2, row(norm2_g), wo[:D_GRP], wo[D_GRP:], w_rt, b_rt, tm_out, seq)

    counts = cnt[0, :N_EXPERTS].astype(jnp.int32)
    out = _moe(h2, idx[:, :TOP_K], gates, rank[:, :TOP_K], counts, x1, gate2, row(final_g),
               w_gate_up.astype(BF16), b_gate_up.reshape(N_EXPERTS, 1, -1),
               w_down.astype(BF16), b_down.reshape(N_EXPERTS, 1, -1), seq)
    return out.reshape(bsz, seq, D_MODEL)


def kernel(x, c, w_ada, b_ada, norm1_g, w_in, mu_shift, w0, w2, a0, a2, g2, k_k, k_a, r_k, gn_w,
           gn_b, b_f, q_norm_g, k_norm_g, o_norm_g, w_out, norm2_g, w_router, b_router, w_gate_up,
           b_gate_up, w_down, b_down, final_g):
    assert w_ada.shape[0] == 1, "single-layer block"
    c_mod = _adaln(c, w_ada[0], b_ada[0])
    return _layer(x, c_mod, norm1_g[0], w_in[0], mu_shift[0], w0[0], w2[0], a0[0], a2[0], g2[0],
                  k_k[0], k_a[0], r_k[0], gn_w[0], gn_b[0], b_f[0], q_norm_g[0], k_norm_g[0],
                  o_norm_g[0], w_out[0], norm2_g[0], w_router[0], b_router[0], w_gate_up[0],
                  b_gate_up[0], w_down[0], b_down[0], final_g,
                  tm_in=min(512, x.shape[1]), tq=min(512, x.shape[1]), tm_out=min(512, x.shape[1]))
```

```python
import functools

import jax
import jax.numpy as jnp
from jax import lax
from jax.experimental import pallas as pl
from jax.experimental.pallas import tpu as pltpu
from jax.experimental.pallas import tpu_sc as plsc

F32 = jnp.float32
BF16 = jnp.bfloat16
HIGHEST = lax.Precision.HIGHEST

D_MODEL = 1024
HEAD_DIM = 64
N_HEADS = 8
D_GRP = N_HEADS * HEAD_DIM
RWKV_COLS = 1792
LORA_OFF = 3 * D_GRP
GATE_OFF = LORA_OFF + 128
FOX_MAIN = 4 * D_GRP
N_EXPERTS = 32
TOP_K = 4
EXPERT_BLOCK = 256
SWIGLU_ALPHA = 1.702
SWIGLU_LIMIT = 7.0
NORM_EPS = 1e-6
GN_EPS = 64e-5
LANES = 128
CHUNK = 64
HEADS_PER_SCAN = 4
SCAN_W = HEADS_PER_SCAN * HEAD_DIM
VMEM_LIMIT = 56 * 1024 * 1024


def _dot(a, b):
    return jnp.dot(a.astype(BF16), b.astype(BF16), preferred_element_type=F32)


def _dot_nt(a, b):
    return lax.dot_general(a.astype(BF16), b.astype(BF16), (((1,), (1,)), ((), ())),
                           preferred_element_type=F32)


def _dot_tn(a, b):
    return lax.dot_general(a.astype(BF16), b.astype(BF16), (((0,), (0,)), ((), ())),
                           preferred_element_type=F32)


def _fdot(a, b):
    return jnp.dot(a, b, precision=HIGHEST, preferred_element_type=F32)


def _split_dot(x, m, terms=2, left=False):
    acc = None
    rem = x
    for _ in range(terms):
        part = rem.astype(BF16)
        rem = rem - part.astype(F32)
        d = (jnp.dot(m, part, preferred_element_type=F32) if left
             else jnp.dot(part, m, preferred_element_type=F32))
        acc = d if acc is None else acc + d
    return acc


def _iota(shape, dim):
    return lax.broadcasted_iota(jnp.int32, shape, dim)


def _seg_reduce_mat(n):
    return (_iota((n, LANES), 0) // HEAD_DIM == _iota((n, LANES), 1)).astype(BF16)


def _seg_expand_mat(n):
    return (_iota((LANES, n), 1) // HEAD_DIM == _iota((LANES, n), 0)).astype(BF16)


def _tri(n, strict):
    r, c = _iota((n, n), 0), _iota((n, n), 1)
    return ((r > c) if strict else (r >= c)).astype(BF16)


def _log_sigmoid(z):
    return jnp.minimum(z, 0.0) - jnp.log(1.0 + jnp.exp(-jnp.abs(z)))


def _sigmoid(z):
    return 1.0 / (1.0 + jnp.exp(-z))


def _adaln_kernel(c_ref, w_ref, b_ref, o_ref):
    c = c_ref[...]
    o_ref[...] = _fdot(c * _sigmoid(c), w_ref[...]) + b_ref[...]


def _adaln(c, w_ada, b_ada):
    bsz = c.shape[0]
    n_mod = w_ada.shape[1] // D_MODEL
    return pl.pallas_call(
        _adaln_kernel,
        grid=(n_mod,),
        in_specs=[pl.BlockSpec((bsz, D_MODEL), lambda j: (0, 0)),
                  pl.BlockSpec((D_MODEL, D_MODEL), lambda j: (0, j)),
                  pl.BlockSpec((1, D_MODEL), lambda j: (0, j))],
        out_specs=pl.BlockSpec((bsz, D_MODEL), lambda j: (0, j)),
        out_shape=jax.ShapeDtypeStruct((bsz, n_mod * D_MODEL), F32),
        name="adaln",
    )(c, w_ada, b_ada.reshape(1, -1))


def _inproj_kernel(x_ref, sh_ref, sc_ref, g_ref, wr_ref, wx_ref, wfh_ref, wfl_ref, bf_ref, qkg_ref,
                   pr_ref, px_ref, kb_ref, qb_ref, carry_ref):
    @pl.when(pl.program_id(1) == 0)
    def _():
        carry_ref[...] = jnp.zeros_like(carry_ref)

    x = x_ref[...]
    tm = x.shape[0]
    h = x * lax.rsqrt(jnp.mean(x * x, axis=-1, keepdims=True) + NORM_EPS) * g_ref[...]
    h = h * (1.0 + sc_ref[...]) + sh_ref[...]
    hb = h.astype(BF16)
    h_lo = (h - hb.astype(F32)).astype(BF16)

    pr_ref[...] = jnp.dot(hb, wr_ref[...], preferred_element_type=F32).astype(BF16)

    px = jnp.dot(hb, wx_ref[...], preferred_element_type=F32)
    qk = px[:, :2 * D_GRP]
    ss = _split_dot(qk * qk, _seg_reduce_mat(2 * D_GRP))
    inv = lax.rsqrt(ss * (1.0 / HEAD_DIM) + NORM_EPS)
    qk = qk * _split_dot(inv, _seg_expand_mat(2 * D_GRP)) * qkg_ref[...]
    px_ref[:, :2 * D_GRP] = qk.astype(BF16)
    px_ref[:, 2 * D_GRP:] = px[:, 2 * D_GRP:].astype(BF16)

    z = (jnp.dot(hb, wfh_ref[...], preferred_element_type=F32)
         + jnp.dot(h_lo, wfh_ref[...], preferred_element_type=F32)
         + jnp.dot(hb, wfl_ref[...], preferred_element_type=F32)) + bf_ref[...]
    cum = _split_dot(_log_sigmoid(z), _tri(tm, False), terms=3, left=True) + carry_ref[...]
    carry_ref[...] = cum[tm - 1:tm, :]

    src, dst = _iota((LANES, LANES), 0), _iota((LANES, LANES), 1)
    parts = []
    rem = cum
    for _ in range(3):
        part = rem.astype(BF16)
        rem = rem - part.astype(F32)
        parts.append(part)

    def spread(offset):
        return sum(jnp.dot(part, ((dst == 8 * src + offset + t) & (src < N_HEADS)).astype(BF16),
                           preferred_element_type=F32) for t, part in enumerate(parts))

    slot = _iota((1, LANES), 1) % 8
    kb_ref[...] = (jnp.where((slot >= 3) & (slot < 6), 1.0, 0.0) - spread(0)).astype(BF16)
    qb_ref[...] = (jnp.where(slot < 3, 1.0, 0.0) + spread(3)).astype(BF16)


def _inproj(x, shift, scale, g, w_r, w_x, w_f, b_f, qk_gain, tm):
    w_f_hi = w_f.astype(BF16)
    w_f_lo = (w_f - w_f_hi.astype(F32)).astype(BF16)
    bsz, seq, _ = x.shape
    const = lambda b, s: (0, 0)
    return pl.pallas_call(
        _inproj_kernel,
        grid=(bsz, seq // tm),
        in_specs=[pl.BlockSpec((None, tm, D_MODEL), lambda b, s: (b, s, 0)),
                  pl.BlockSpec((None, 1, D_MODEL), lambda b, s: (b, 0, 0)),
                  pl.BlockSpec((None, 1, D_MODEL), lambda b, s: (b, 0, 0)),
                  pl.BlockSpec((1, D_MODEL), const),
                  pl.BlockSpec((D_MODEL, RWKV_COLS), const),
                  pl.BlockSpec((D_MODEL, FOX_MAIN), const),
                  pl.BlockSpec((D_MODEL, LANES), const),
                  pl.BlockSpec((D_MODEL, LANES), const),
                  pl.BlockSpec((1, LANES), const),
                  pl.BlockSpec((1, 2 * D_GRP), const)],
        out_specs=[pl.BlockSpec((None, tm, RWKV_COLS), lambda b, s: (b, s, 0)),
                   pl.BlockSpec((None, tm, FOX_MAIN), lambda b, s: (b, s, 0)),
                   pl.BlockSpec((None, tm, LANES), lambda b, s: (b, s, 0)),
                   pl.BlockSpec((None, tm, LANES), lambda b, s: (b, s, 0))],
        out_shape=[jax.ShapeDtypeStruct((bsz, seq, RWKV_COLS), BF16),
                   jax.ShapeDtypeStruct((bsz, seq, FOX_MAIN), BF16),
                   jax.ShapeDtypeStruct((bsz, seq, LANES), BF16),
                   jax.ShapeDtypeStruct((bsz, seq, LANES), BF16)],
        scratch_shapes=[pltpu.VMEM((1, LANES), F32)],
        compiler_params=pltpu.CompilerParams(
            dimension_semantics=("parallel", "arbitrary"), vmem_limit_bytes=VMEM_LIMIT),
        name="inproj",
    )(x, shift, scale, g, w_r, w_x, w_f_hi, w_f_lo, b_f, qk_gain)


_NN = (((1,), (0,)), ((), ()))
_NT = (((1,), (1,)), ((), ()))
_TN = (((0,), (0,)), ((), ()))
SCAN_N = HEADS_PER_SCAN * CHUNK
BATCH_PER_STEP = 4
INV_LEVELS = 5
M_HEAD, M_STRICT, M_INCL, M_EYE, M_BASE, M_OFF = 0, 1, 2, 3, 4, 5


def _bdot(a, b, dims):
    return lax.dot_general(a, b, dims, preferred_element_type=F32)


def _scan_masks():
    rr, cc = _iota((SCAN_N, SCAN_W), 0), _iota((SCAN_N, SCAN_W), 1)
    ri, ci = _iota((SCAN_N, SCAN_N), 0), _iota((SCAN_N, SCAN_N), 1)
    same = ri // CHUNK == ci // CHUNK
    masks = [rr // CHUNK == cc // HEAD_DIM, same & (ri > ci), same & (ri >= ci), ri == ci,
             (ri // 2 == ci // 2) & (ri > ci)]
    blk = 2
    while blk < CHUNK:
        masks.append((ri // (2 * blk) == ci // (2 * blk)) & (ri // blk != ci // blk) & (ri > ci))
        blk *= 2
    return jnp.stack(masks).astype(BF16)


def _rwkv_kernel(p_ref, masks_ref, mu_ref, w0_ref, w2_ref, a0_ref, a2_ref, g2_ref, kk_ref, ka_ref,
                 rk_ref, gnw_ref, gnb_ref, o_ref, last_ref, state_ref):
    @pl.when(pl.program_id(1) == 0)
    def _():
        last_ref[...] = jnp.zeros_like(last_ref)
        state_ref[...] = jnp.zeros_like(state_ref)

    mu, w0, w2, a0, a2, g2, k_k, k_a, r_k, gn_w, gn_b = (
        ref[...] for ref in (mu_ref, w0_ref, w2_ref, a0_ref, a2_ref, g2_ref, kk_ref, ka_ref,
                             rk_ref, gnw_ref, gnb_ref))
    rows = BATCH_PER_STEP * CHUNK
    p = p_ref[...].astype(F32).reshape(rows, RWKV_COLS)
    row_id = _iota((rows, 1), 0)
    prev = pltpu.roll(p, 1, axis=0)
    for bb in range(BATCH_PER_STEP):
        prev = jnp.where(row_id == bb * CHUNK, last_ref[bb], prev)
        last_ref[bb] = p[(bb + 1) * CHUNK - 1:(bb + 1) * CHUNK, :]
    pf = p + mu * (prev - p)
    r = pf[:, 0:D_GRP]
    k = pf[:, D_GRP:2 * D_GRP]
    v = pf[:, 2 * D_GRP:3 * D_GRP]
    lora = pf[:, LORA_OFF:GATE_OFF]
    gd = pf[:, GATE_OFF:RWKV_COLS]

    wlog = w0 + _dot(jnp.tanh(lora), w2)
    neg = -wlog
    softplus = jnp.maximum(neg, 0.0) + jnp.log(1.0 + jnp.exp(-jnp.abs(neg)))
    ld = -jnp.exp(-softplus - 0.5)
    a = _sigmoid(a0 + _dot(lora, a2))
    g = _dot(_sigmoid(gd), g2)

    red, exp_m = _seg_reduce_mat(D_GRP), _seg_expand_mat(D_GRP)
    kk = k * k_k
    n2 = _split_dot(kk * kk, red)
    kk = kk * _split_dot(1.0 / jnp.maximum(jnp.sqrt(n2), 1e-12), exp_m)
    k2 = k * (1.0 + (a - 1.0) * k_a)

    tr, tc = _iota((rows, rows), 0), _iota((rows, rows), 1)
    tri = ((tr >= tc) & (tr // CHUNK == tc // CHUNK)).astype(BF16)
    cl = _split_dot(ld, tri, terms=3, left=True)
    cl_end = jnp.concatenate(
        [jnp.broadcast_to(cl[(bb + 1) * CHUNK - 1:(bb + 1) * CHUNK, :], (CHUNK, D_GRP))
         for bb in range(BATCH_PER_STEP)], axis=0)
    e_in = jnp.exp(cl)
    e_out = jnp.exp(-cl)
    e_rem = jnp.exp(cl_end - cl)
    p_end = jnp.exp(cl_end)
    kka = kk * a
    ops = [(-kk * jnp.exp(cl - ld)).astype(BF16), (kka * e_out).astype(BF16),
           (k2 * e_out).astype(BF16), (r * e_in).astype(BF16), v.astype(BF16),
           (kka * e_rem).astype(BF16), (k2 * e_rem).astype(BF16)]

    chains = [(bb, grp) for bb in range(BATCH_PER_STEP)
              for grp in range(N_HEADS // HEADS_PER_SCAN)]
    head_mask = masks_ref[M_HEAD]
    strict, incl = masks_ref[M_STRICT], masks_ref[M_INCL]

    def stacked(op, bb, grp):
        part = op[bb * CHUNK:(bb + 1) * CHUNK, grp * SCAN_W:(grp + 1) * SCAN_W]
        return jnp.concatenate([part] * HEADS_PER_SCAN, axis=0) * head_mask

    xs = [[stacked(op, bb, grp) for op in ops] for bb, grp in chains]
    st = [state_ref[bb, grp] for bb, grp in chains]
    sb = [s.astype(BF16) for s in st]
    nab = [_bdot(x[0], x[1], _NT).astype(BF16) for x in xs]
    aak = [_bdot(x[0], x[2], _NT).astype(BF16) * strict for x in xs]
    arb = [_bdot(x[3], x[1], _NT).astype(BF16) * incl for x in xs]
    ark = [_bdot(x[3], x[2], _NT).astype(BF16) * incl for x in xs]
    t_inv = [masks_ref[M_EYE] + n * masks_ref[M_BASE] for n in nab]
    for lvl in range(INV_LEVELS):
        half = [_bdot(t, n * masks_ref[M_OFF + lvl], _NN).astype(BF16) for t, n in zip(t_inv, nab)]
        t_inv = [t + _bdot(h, t, _NN).astype(BF16) for t, h in zip(t_inv, half)]
    rhs = [(_bdot(x[0], s, _NT) + _bdot(k, x[4], _NN)).astype(BF16)
           for x, s, k in zip(xs, sb, aak)]
    sa = [_bdot(t, h, _NN).astype(BF16) for t, h in zip(t_inv, rhs)]
    ys = [_bdot(x[3], s, _NT) + _bdot(b, u, _NN) + _bdot(k, x[4], _NN)
          for x, s, b, u, k in zip(xs, sb, arb, sa, ark)]
    for (bb, grp), x, s, u in zip(chains, xs, st, sa):
        decay = p_end[bb * CHUNK:bb * CHUNK + 1, grp * SCAN_W:(grp + 1) * SCAN_W]
        state_ref[bb, grp] = s * decay + _bdot(u, x[5], _TN) + _bdot(x[4], x[6], _TN)
    ys = [y[0:CHUNK] + y[CHUNK:2 * CHUNK] + y[2 * CHUNK:3 * CHUNK] + y[3 * CHUNK:4 * CHUNK]
          for y in ys]
    n_grp = N_HEADS // HEADS_PER_SCAN
    y = jnp.concatenate([jnp.concatenate(ys[bb * n_grp:(bb + 1) * n_grp], axis=1)
                         for bb in range(BATCH_PER_STEP)], axis=0)

    mean = _split_dot(_split_dot(y, red) * (1.0 / HEAD_DIM), exp_m)
    d = y - mean
    var = _split_dot(d * d, red) * (1.0 / HEAD_DIM)
    yn = d * _split_dot(lax.rsqrt(var + GN_EPS), exp_m) * gn_w + gn_b
    bonus = _split_dot(_split_dot(r * k2 * r_k, red), exp_m) * v
    o_ref[...] = ((yn + bonus) * g).astype(BF16).reshape(BATCH_PER_STEP, CHUNK, D_GRP)


def _rwkv(p_r, mu, w0, w2p, a0, a2p, g2, k_k, k_a, r_k, gn_w, gn_b):
    bsz, seq, _ = p_r.shape
    assert bsz % BATCH_PER_STEP == 0
    masks = _scan_masks()
    const = lambda b, s: (0, 0)
    vec = pl.BlockSpec((1, D_GRP), const)
    return pl.pallas_call(
        _rwkv_kernel,
        grid=(bsz // BATCH_PER_STEP, seq // CHUNK),
        in_specs=[pl.BlockSpec((BATCH_PER_STEP, CHUNK, RWKV_COLS), lambda b, s: (b, s, 0)),
                  pl.BlockSpec(masks.shape, lambda b, s: (0, 0, 0)),
                  pl.BlockSpec((1, RWKV_COLS), const),
                  vec, pl.BlockSpec((LANES, D_GRP), const),
                  vec, pl.BlockSpec((LANES, D_GRP), const),
                  pl.BlockSpec((LANES, D_GRP), const),
                  vec, vec, vec, vec, vec],
        out_specs=pl.BlockSpec((BATCH_PER_STEP, CHUNK, D_GRP), lambda b, s: (b, s, 0)),
        out_shape=jax.ShapeDtypeStruct((bsz, seq, D_GRP), BF16),
        scratch_shapes=[pltpu.VMEM((BATCH_PER_STEP, 1, RWKV_COLS), F32),
                        pltpu.VMEM((BATCH_PER_STEP, N_HEADS // HEADS_PER_SCAN, SCAN_W, SCAN_W), F32)],
        compiler_params=pltpu.CompilerParams(
            dimension_semantics=("parallel", "arbitrary"), vmem_limit_bytes=VMEM_LIMIT),
        name="rwkv",
    )(p_r, masks, mu, w0, w2p, a0, a2p, g2, k_k, k_a, r_k, gn_w, gn_b)


def _fox_kernel(q_ref, qb_ref, k_ref, kb_ref, vt_ref, og_ref, ong_ref, o_ref, *, tq):
    hp = pl.program_id(1)
    qi = pl.program_id(2)
    lane = _iota((1, LANES), 1)
    q = q_ref[...]
    qb = qb_ref[...]
    zero = jnp.zeros_like(q)
    qcat = [jnp.concatenate([jnp.where(lane // HEAD_DIM == hh, q, zero),
                             jnp.where(lane // 8 == hp * 2 + hh, qb, zero)], axis=1)
            for hh in range(2)]
    key_pos = _iota((tq, tq), 0)
    qry_pos = _iota((tq, tq), 1)

    def tile(j, carry, masked):
        start = pl.multiple_of(j * tq, tq)
        kcat = jnp.concatenate([k_ref[pl.ds(start, tq), :], kb_ref[pl.ds(start, tq), :]], axis=1)
        vt = vt_ref[:, pl.ds(start, tq)]
        sts = [lax.dot_general(kcat, qc, _NT, preferred_element_type=F32) for qc in qcat]
        if masked:
            sts = [jnp.where(qry_pos >= key_pos, st, -jnp.inf) for st in sts]
        m_new = [jnp.maximum(c[0], jnp.max(st, axis=0, keepdims=True)) for c, st in zip(carry, sts)]
        pts = [jnp.exp(st - m) for st, m in zip(sts, m_new)]
        pvs = [jnp.dot(vt, pt.astype(BF16), preferred_element_type=F32) for pt in pts]
        new = []
        for hh in range(2):
            m, l, acc = carry[hh]
            alpha = jnp.exp(m - m_new[hh])
            l = alpha * l + jnp.sum(pts[hh], axis=0, keepdims=True)
            acc = alpha * acc + pvs[hh][hh * HEAD_DIM:(hh + 1) * HEAD_DIM, :]
            new.append((m_new[hh], l, acc))
        return tuple(new)

    init = (jnp.full((1, tq), -jnp.inf, F32), jnp.zeros((1, tq), F32),
            jnp.zeros((HEAD_DIM, tq), F32))
    carry = lax.fori_loop(0, qi, functools.partial(tile, masked=False), (init, init))
    outs = []
    for _, l, acc in tile(qi, carry, True):
        o = acc / l
        outs.append(o * lax.rsqrt(jnp.mean(o * o, axis=0, keepdims=True) + NORM_EPS))
    o = jnp.concatenate(outs, axis=0).T
    o_ref[...] = (o * ong_ref[...] * _sigmoid(og_ref[...].astype(F32))).astype(BF16)


def _fox(p_x, k_bias, q_bias, o_gain, tq):
    bsz, seq, _ = p_x.shape
    npair = N_HEADS // 2
    v_t = jnp.transpose(p_x[:, :, 2 * D_GRP:3 * D_GRP], (0, 2, 1))
    return pl.pallas_call(
        functools.partial(_fox_kernel, tq=tq),
        grid=(bsz, npair, seq // tq),
        in_specs=[pl.BlockSpec((None, tq, LANES), lambda b, h, i: (b, i, h)),
                  pl.BlockSpec((None, tq, LANES), lambda b, h, i: (b, i, 0)),
                  pl.BlockSpec((None, seq, LANES), lambda b, h, i: (b, 0, npair + h)),
                  pl.BlockSpec((None, seq, LANES), lambda b, h, i: (b, 0, 0)),
                  pl.BlockSpec((None, LANES, seq), lambda b, h, i: (b, h, 0)),
                  pl.BlockSpec((None, tq, LANES), lambda b, h, i: (b, i, 3 * npair + h)),
                  pl.BlockSpec((1, LANES), lambda b, h, i: (0, 0))],
        out_specs=pl.BlockSpec((None, tq, LANES), lambda b, h, i: (b, i, h)),
        out_shape=jax.ShapeDtypeStruct((bsz, seq, D_GRP), BF16),
        compiler_params=pltpu.CompilerParams(
            dimension_semantics=("parallel", "parallel", "arbitrary"),
            vmem_limit_bytes=VMEM_LIMIT),
        name="fox",
    )(p_x, q_bias, p_x, k_bias, v_t, p_x, o_gain)


def _outproj_kernel(x_ref, yr_ref, yf_ref, g1_ref, sh_ref, sc_ref, ng_ref, wor_ref, wof_ref,
                    wrt_ref, brt_ref, x1_ref, h2_ref, idx_ref, gate_ref, rank_ref, cnt_ref,
                    carry_ref):
    @pl.when(pl.program_id(0) == 0)
    def _():
        carry_ref[...] = jnp.zeros_like(carry_ref)

    y = (jnp.dot(yr_ref[...], wor_ref[...], preferred_element_type=F32)
         + jnp.dot(yf_ref[...], wof_ref[...], preferred_element_type=F32))
    x1 = x_ref[...] + g1_ref[...] * y
    x1_ref[...] = x1
    tm = x1.shape[0]
    h = x1 * lax.rsqrt(jnp.mean(x1 * x1, axis=-1, keepdims=True) + NORM_EPS) * ng_ref[...]
    h2 = h * (1.0 + sc_ref[...]) + sh_ref[...]
    h2_ref[...] = h2

    lane = _iota((tm, LANES), 1)
    logits = _fdot(h2, wrt_ref[...]) + brt_ref[...]
    lg = jnp.where(lane < N_EXPERTS, logits, -jnp.inf)
    picks = []
    hot_sum = jnp.zeros((tm, LANES), F32)
    for _ in range(TOP_K):
        m = jnp.max(lg, axis=-1, keepdims=True)
        sel = jnp.min(jnp.where(lg == m, lane, LANES), axis=-1, keepdims=True)
        hot = lane == sel
        picks.append((m, sel, hot))
        hot_sum = hot_sum + hot.astype(F32)
        lg = jnp.where(hot, -jnp.inf, lg)
    es = [jnp.exp(m - picks[0][0]) for m, _, _ in picks]
    den = es[0] + es[1] + es[2] + es[3]

    before = jnp.dot(_tri(tm, True), hot_sum.astype(BF16), preferred_element_type=F32)
    before = before + carry_ref[...]
    idx_out = jnp.zeros((tm, LANES), jnp.int32)
    gate_out = jnp.zeros((tm, LANES), F32)
    rank_out = jnp.zeros((tm, LANES), jnp.int32)
    for kk, (m, sel, hot) in enumerate(picks):
        rk = jnp.sum(jnp.where(hot, before, 0.0), axis=-1, keepdims=True).astype(jnp.int32)
        idx_out = jnp.where(lane == kk, sel, idx_out)
        gate_out = jnp.where(lane == kk, es[kk] / den, gate_out)
        rank_out = jnp.where(lane == kk, rk, rank_out)
    idx_ref[...] = idx_out
    gate_ref[...] = gate_out
    rank_ref[...] = rank_out
    carry_ref[...] = carry_ref[...] + jnp.sum(hot_sum, axis=0, keepdims=True)
    cnt_ref[...] = carry_ref[...]


def _outproj(x2d, y_r, y_f, gate1, shift2, scale2, norm_g, wo_r, wo_f, w_rt, b_rt, tm, seq):
    t = x2d.shape[0]
    per_b = seq // tm
    const = lambda i: (0, 0)
    rows = lambda i: (i, 0)
    mod = pl.BlockSpec((None, 1, D_MODEL), lambda i: (i // per_b, 0, 0))
    return pl.pallas_call(
        _outproj_kernel,
        grid=(t // tm,),
        in_specs=[pl.BlockSpec((tm, D_MODEL), rows),
                  pl.BlockSpec((tm, D_GRP), rows),
                  pl.BlockSpec((tm, D_GRP), rows),
                  mod, mod, mod,
                  pl.BlockSpec((1, D_MODEL), const),
                  pl.BlockSpec((D_GRP, D_MODEL), const),
                  pl.BlockSpec((D_GRP, D_MODEL), const),
                  pl.BlockSpec((D_MODEL, LANES), const),
                  pl.BlockSpec((1, LANES), const)],
        out_specs=[pl.BlockSpec((tm, D_MODEL), rows),
                   pl.BlockSpec((tm, D_MODEL), rows),
                   pl.BlockSpec((tm, LANES), rows),
                   pl.BlockSpec((tm, LANES), rows),
                   pl.BlockSpec((tm, LANES), rows),
                   pl.BlockSpec((1, LANES), const)],
        out_shape=[jax.ShapeDtypeStruct((t, D_MODEL), F32),
                   jax.ShapeDtypeStruct((t, D_MODEL), F32),
                   jax.ShapeDtypeStruct((t, LANES), jnp.int32),
                   jax.ShapeDtypeStruct((t, LANES), F32),
                   jax.ShapeDtypeStruct((t, LANES), jnp.int32),
                   jax.ShapeDtypeStruct((1, LANES), F32)],
        scratch_shapes=[pltpu.VMEM((1, LANES), F32)],
        compiler_params=pltpu.CompilerParams(
            dimension_semantics=("arbitrary",), vmem_limit_bytes=VMEM_LIMIT),
        name="outproj",
    )(x2d, y_r, y_f, gate1, shift2, scale2, norm_g, wo_r, wo_f, w_rt, b_rt)


GATHER_ROWS = 1024


def _row_copy(src_hbm, dst, src_row, dst_row, sem):
    return pltpu.make_async_copy(src_hbm.at[pl.ds(src_row, 1), :], dst.at[pl.ds(dst_row, 1), :], sem)


def _gather_kernel(idx_hbm, src_hbm, o_ref, idx_smem, isem, sem):
    i = pl.program_id(0)
    cp = pltpu.make_async_copy(idx_hbm.at[i], idx_smem, isem)
    cp.start()
    cp.wait()

    def issue(r, c):
        _row_copy(src_hbm, o_ref, idx_smem[r], r, sem).start()
        return c

    lax.fori_loop(0, GATHER_ROWS, issue, 0)

    pltpu.make_async_copy(src_hbm.at[pl.ds(0, GATHER_ROWS), :], o_ref, sem).wait()


def _gather_rows(idx2d, src):
    nblk = idx2d.shape[0]
    d = src.shape[1]
    return pl.pallas_call(
        _gather_kernel,
        grid=(nblk,),
        in_specs=[pl.BlockSpec(memory_space=pl.ANY), pl.BlockSpec(memory_space=pl.ANY)],
        out_specs=pl.BlockSpec((GATHER_ROWS, d), lambda i: (i, 0)),
        out_shape=jax.ShapeDtypeStruct((nblk * GATHER_ROWS, d), src.dtype),
        scratch_shapes=[pltpu.SMEM((GATHER_ROWS,), jnp.int32),
                        pltpu.SemaphoreType.DMA, pltpu.SemaphoreType.DMA],
        compiler_params=pltpu.CompilerParams(
            dimension_semantics=("arbitrary",), vmem_limit_bytes=VMEM_LIMIT),
        name="gather",
    )(idx2d, src)


SC_CORES = 2
SC_SUBCORES = 16
SC_ROWS = 32


def _sc_gather_rows(idx, src):
    n_workers = SC_CORES * SC_SUBCORES
    m = idx.shape[0]
    d = src.shape[1]
    assert m % (n_workers * SC_ROWS) == 0
    n_chunks = m // (n_workers * SC_ROWS)
    mesh = plsc.VectorSubcoreMesh(core_axis_name="c", subcore_axis_name="s")

    @functools.partial(
        pl.kernel, mesh=mesh,
        out_type=jax.ShapeDtypeStruct((m, d), src.dtype),
        scratch_types=[pltpu.VMEM((n_chunks, SC_ROWS), jnp.int32),
                       pltpu.VMEM((SC_ROWS, d), src.dtype),
                       pltpu.SemaphoreType.DMA],
        name="sc_gather")
    def gather(src_hbm, idx_hbm, out_hbm, idx_v, rows_v, sem):
        wid = lax.axis_index("s") * SC_CORES + lax.axis_index("c")
        pltpu.sync_copy(idx_hbm.at[wid], idx_v)

        @pl.loop(0, n_chunks)
        def _(j):
            pltpu.async_copy(src_hbm.at[idx_v.at[j]], rows_v, sem).wait()
            pltpu.sync_copy(rows_v, out_hbm.at[pl.ds((wid * n_chunks + j) * SC_ROWS, SC_ROWS)])

    return gather(src, idx.reshape(n_workers, n_chunks, SC_ROWS))


def _expert_kernel(be_ref, x_ref, wgu_ref, bgu_ref, wd_ref, bd_ref, o_ref):
    del be_ref
    gu = jnp.dot(x_ref[...].astype(BF16), wgu_ref[...], preferred_element_type=F32) + bgu_ref[...]
    gate = jnp.minimum(gu[:, :D_MODEL], SWIGLU_LIMIT)
    up = jnp.clip(gu[:, D_MODEL:], -SWIGLU_LIMIT, SWIGLU_LIMIT)
    act = gate * _sigmoid(SWIGLU_ALPHA * gate) * (up + 1.0)
    o_ref[...] = jnp.dot(act.astype(BF16), wd_ref[...], preferred_element_type=F32) + bd_ref[...]


def _experts(block_e, xs, w_gu, b_gu, w_d, b_d):
    n_blocks = block_e.shape[0]
    grid_spec = pltpu.PrefetchScalarGridSpec(
        num_scalar_prefetch=1,
        grid=(n_blocks,),
        in_specs=[pl.BlockSpec((EXPERT_BLOCK, D_MODEL), lambda j, be: (j, 0)),
                  pl.BlockSpec((None, D_MODEL, 2 * D_MODEL), lambda j, be: (be[j], 0, 0)),
                  pl.BlockSpec((None, 1, 2 * D_MODEL), lambda j, be: (be[j], 0, 0)),
                  pl.BlockSpec((None, D_MODEL, D_MODEL), lambda j, be: (be[j], 0, 0)),
                  pl.BlockSpec((None, 1, D_MODEL), lambda j, be: (be[j], 0, 0))],
        out_specs=pl.BlockSpec((EXPERT_BLOCK, D_MODEL), lambda j, be: (j, 0)),
    )
    return pl.pallas_call(
        _expert_kernel,
        grid_spec=grid_spec,
        out_shape=jax.ShapeDtypeStruct(xs.shape, F32),
        compiler_params=pltpu.CompilerParams(
            dimension_semantics=("arbitrary",), vmem_limit_bytes=VMEM_LIMIT),
        name="experts",
    )(block_e, xs, w_gu, b_gu, w_d, b_d)


COMBINE_TOKENS = GATHER_ROWS // TOP_K


def _combine_kernel(dest_hbm, yb_hbm, x1_ref, gate_ref, g2_ref, fg_ref, o_ref,
                    rows_ref, idx_smem, isem, sem):
    i = pl.program_id(0)
    cp = pltpu.make_async_copy(dest_hbm.at[i], idx_smem, isem)
    cp.start()
    cp.wait()

    def issue(r, c):
        _row_copy(yb_hbm, rows_ref, idx_smem[r], r, sem).start()
        return c

    lax.fori_loop(0, GATHER_ROWS, issue, 0)

    pltpu.make_async_copy(yb_hbm.at[pl.ds(0, GATHER_ROWS), :], rows_ref, sem).wait()

    gates = gate_ref[...]
    acc = None
    for kk in range(TOP_K):
        part = gates[:, kk:kk + 1] * rows_ref[kk * COMBINE_TOKENS:(kk + 1) * COMBINE_TOKENS, :]
        acc = part if acc is None else acc + part
    x2 = x1_ref[...] + g2_ref[...] * acc
    o_ref[...] = x2 * lax.rsqrt(jnp.mean(x2 * x2, axis=-1, keepdims=True) + NORM_EPS) * fg_ref[...]


def _combine(dest2d, yb, x1, gates, gate2, final_g, seq):
    t = x1.shape[0]
    tm = COMBINE_TOKENS
    per_b = seq // tm
    rows = lambda i: (i, 0)
    return pl.pallas_call(
        _combine_kernel,
        grid=(t // tm,),
        in_specs=[pl.BlockSpec(memory_space=pl.ANY), pl.BlockSpec(memory_space=pl.ANY),
                  pl.BlockSpec((tm, D_MODEL), rows),
                  pl.BlockSpec((tm, LANES), rows),
                  pl.BlockSpec((None, 1, D_MODEL), lambda i: (i // per_b, 0, 0)),
                  pl.BlockSpec((1, D_MODEL), lambda i: (0, 0))],
        out_specs=pl.BlockSpec((tm, D_MODEL), rows),
        out_shape=jax.ShapeDtypeStruct((t, D_MODEL), F32),
        scratch_shapes=[pltpu.VMEM((GATHER_ROWS, D_MODEL), F32),
                        pltpu.SMEM((GATHER_ROWS,), jnp.int32),
                        pltpu.SemaphoreType.DMA, pltpu.SemaphoreType.DMA],
        compiler_params=pltpu.CompilerParams(
            dimension_semantics=("arbitrary",), vmem_limit_bytes=VMEM_LIMIT),
        name="combine",
    )(dest2d, yb, x1, gates, gate2, final_g)


def _moe(h2, idx, gates, rank, counts, x1, gate2, final_g, w_gu, b_gu, w_d, b_d, seq):
    t = h2.shape[0]
    n_slots = t * TOP_K
    n_blocks = -(-n_slots // EXPERT_BLOCK) + N_EXPERTS
    cap = n_blocks * EXPERT_BLOCK
    assert cap % GATHER_ROWS == 0 and n_slots % GATHER_ROWS == 0
    padded = (counts + EXPERT_BLOCK - 1) // EXPERT_BLOCK * EXPERT_BLOCK
    pad_ends = jnp.cumsum(padded)
    pad_starts = pad_ends - padded
    dest = pad_starts[idx] + rank
    tok = jnp.broadcast_to(jnp.arange(t, dtype=jnp.int32)[:, None], (t, TOP_K))
    buf_tok = jnp.zeros((cap,), jnp.int32).at[dest.reshape(-1)].set(tok.reshape(-1))
    block_starts = jnp.arange(n_blocks, dtype=jnp.int32) * EXPERT_BLOCK
    block_e = jnp.minimum(jnp.sum(block_starts[:, None] >= pad_ends[None, :], axis=1),
                          N_EXPERTS - 1).astype(jnp.int32)

    xs = _sc_gather_rows(buf_tok, h2)
    yb = _experts(block_e, xs, w_gu, b_gu, w_d, b_d)
    dest_blocks = dest.reshape(-1, COMBINE_TOKENS, TOP_K).transpose(0, 2, 1).reshape(-1, GATHER_ROWS)
    return _combine(dest_blocks, yb, x1, gates, gate2, final_g, seq)


def _layer(x, c_mod, norm1_g, w_in, mu_shift, w0, w2, a0, a2, g2, k_k, k_a, r_k, gn_w, gn_b, b_f,
           q_norm_g, k_norm_g, o_norm_g, w_out, norm2_g, w_router, b_router, w_gate_up,
           b_gate_up, w_down, b_down, final_g, tm_in, tq, tm_out):
    bsz, seq, _ = x.shape
    shift1, scale1, gate1, shift2, scale2, gate2 = (
        m.reshape(bsz, 1, D_MODEL) for m in jnp.split(c_mod, 6, axis=-1))
    row = lambda v: v.reshape(1, -1)

    w_r = w_in[:, :RWKV_COLS].astype(BF16)
    w_x = w_in[:, RWKV_COLS:RWKV_COLS + FOX_MAIN].astype(BF16)
    w_f = jnp.pad(w_in[:, RWKV_COLS + FOX_MAIN:], ((0, 0), (0, LANES - N_HEADS)))
    b_fp = jnp.pad(b_f, (0, LANES - N_HEADS)).reshape(1, LANES)
    qk_gain = jnp.concatenate([jnp.tile(q_norm_g, N_HEADS) * HEAD_DIM ** -0.5,
                               jnp.tile(k_norm_g, N_HEADS)]).reshape(1, -1)
    p_r, p_x, k_bias, q_bias = _inproj(x, shift1, scale1, row(norm1_g), w_r, w_x, w_f, b_fp,
                                       qk_gain, tm_in)

    zeros = jnp.zeros((LANES - 64, D_GRP), F32)
    w2p = jnp.concatenate([w2, zeros], axis=0).astype(BF16)
    a2p = jnp.concatenate([zeros, a2], axis=0).astype(BF16)
    y_r = _rwkv(p_r, row(mu_shift), row(w0), w2p, row(a0), a2p, g2.astype(BF16), row(k_k),
                row(k_a), row(r_k), row(gn_w), row(gn_b))

    y_f = _fox(p_x, k_bias, q_bias, jnp.tile(o_norm_g, 2).reshape(1, LANES), tq)

    t = bsz * seq
    w_rt = jnp.pad(w_router, ((0, 0), (0, LANES - N_EXPERTS)))
    b_rt = jnp.pad(b_router, (0, LANES - N_EXPERTS)).reshape(1, LANES)
    wo = w_out.astype(BF16)
    x1, h2, idx, gates, rank, cnt = _outproj(
        x.reshape(t, D_MODEL), y_r.reshape(t, D_GRP), y_f.reshape(t, D_GRP), gate1, shift2,
        scale2, row(norm2_g), wo[:D_GRP], wo[D_GRP:], w_rt, b_rt, tm_out, seq)

    counts = cnt[0, :N_EXPERTS].astype(jnp.int32)
    out = _moe(h2, idx[:, :TOP_K], gates, rank[:, :TOP_K], counts, x1, gate2, row(final_g),
               w_gate_up.astype(BF16), b_gate_up.reshape(N_EXPERTS, 1, -1),
               w_down.astype(BF16), b_down.reshape(N_EXPERTS, 1, -1), seq)
    return out.reshape(bsz, seq, D_MODEL)


def kernel(x, c, w_ada, b_ada, norm1_g, w_in, mu_shift, w0, w2, a0, a2, g2, k_k, k_a, r_k, gn_w,
           gn_b, b_f, q_norm_g, k_norm_g, o_norm_g, w_out, norm2_g, w_router, b_router, w_gate_up,
           b_gate_up, w_down, b_down, final_g):
    assert w_ada.shape[0] == 1, "single-layer block"
    c_mod = _adaln(c, w_ada[0], b_ada[0])
    return _layer(x, c_mod, norm1_g[0], w_in[0], mu_shift[0], w0[0], w2[0], a0[0], a2[0], g2[0],
                  k_k[0], k_a[0], r_k[0], gn_w[0], gn_b[0], b_f[0], q_norm_g[0], k_norm_g[0],
                  o_norm_g[0], w_out[0], norm2_g[0], w_router[0], b_router[0], w_gate_up[0],
                  b_gate_up[0], w_down[0], b_down[0], final_g,
                  tm_in=min(512, x.shape[1]), tq=min(512, x.shape[1]), tm_out=min(512, x.shape[1]))
```

```python
import functools

import jax
import jax.numpy as jnp
from jax import lax
from jax.experimental import pallas as pl
from jax.experimental.pallas import tpu as pltpu
from jax.experimental.pallas import tpu_sc as plsc

F32 = jnp.float32
BF16 = jnp.bfloat16
HIGHEST = lax.Precision.HIGHEST

D_MODEL = 1024
HEAD_DIM = 64
N_HEADS = 8
D_GRP = N_HEADS * HEAD_DIM
RWKV_COLS = 1792
LORA_OFF = 3 * D_GRP
GATE_OFF = LORA_OFF + 128
FOX_MAIN = 4 * D_GRP
N_EXPERTS = 32
TOP_K = 4
EXPERT_BLOCK = 256
SWIGLU_ALPHA = 1.702
SWIGLU_LIMIT = 7.0
NORM_EPS = 1e-6
GN_EPS = 64e-5
LANES = 128
CHUNK = 64
HEADS_PER_SCAN = 4
SCAN_W = HEADS_PER_SCAN * HEAD_DIM
VMEM_LIMIT = 56 * 1024 * 1024


def _dot(a, b):
    return jnp.dot(a.astype(BF16), b.astype(BF16), preferred_element_type=F32)


def _dot_nt(a, b):
    return lax.dot_general(a.astype(BF16), b.astype(BF16), (((1,), (1,)), ((), ())),
                           preferred_element_type=F32)


def _dot_tn(a, b):
    return lax.dot_general(a.astype(BF16), b.astype(BF16), (((0,), (0,)), ((), ())),
                           preferred_element_type=F32)


def _fdot(a, b):
    return jnp.dot(a, b, precision=HIGHEST, preferred_element_type=F32)


def _split_dot(x, m, terms=2, left=False):
    acc = None
    rem = x
    for _ in range(terms):
        part = rem.astype(BF16)
        rem = rem - part.astype(F32)
        d = (jnp.dot(m, part, preferred_element_type=F32) if left
             else jnp.dot(part, m, preferred_element_type=F32))
        acc = d if acc is None else acc + d
    return acc


def _iota(shape, dim):
    return lax.broadcasted_iota(jnp.int32, shape, dim)


def _seg_reduce_mat(n):
    return (_iota((n, LANES), 0) // HEAD_DIM == _iota((n, LANES), 1)).astype(BF16)


def _seg_expand_mat(n):
    return (_iota((LANES, n), 1) // HEAD_DIM == _iota((LANES, n), 0)).astype(BF16)


def _tri(n, strict):
    r, c = _iota((n, n), 0), _iota((n, n), 1)
    return ((r > c) if strict else (r >= c)).astype(BF16)


def _log_sigmoid(z):
    return jnp.minimum(z, 0.0) - jnp.log(1.0 + jnp.exp(-jnp.abs(z)))


def _sigmoid(z):
    return 1.0 / (1.0 + jnp.exp(-z))


def _adaln_kernel(c_ref, w_ref, b_ref, o_ref):
    c = c_ref[...]
    o_ref[...] = _fdot(c * _sigmoid(c), w_ref[...]) + b_ref[...]


def _adaln(c, w_ada, b_ada):
    bsz = c.shape[0]
    n_mod = w_ada.shape[1] // D_MODEL
    return pl.pallas_call(
        _adaln_kernel,
        grid=(n_mod,),
        in_specs=[pl.BlockSpec((bsz, D_MODEL), lambda j: (0, 0)),
                  pl.BlockSpec((D_MODEL, D_MODEL), lambda j: (0, j)),
                  pl.BlockSpec((1, D_MODEL), lambda j: (0, j))],
        out_specs=pl.BlockSpec((bsz, D_MODEL), lambda j: (0, j)),
        out_shape=jax.ShapeDtypeStruct((bsz, n_mod * D_MODEL), F32),
        name="adaln",
    )(c, w_ada, b_ada.reshape(1, -1))


def _inproj_kernel(x_ref, sh_ref, sc_ref, g_ref, wr_ref, wx_ref, wfh_ref, wfl_ref, bf_ref, qkg_ref,
                   pr_ref, px_ref, kb_ref, qb_ref, carry_ref):
    @pl.when(pl.program_id(1) == 0)
    def _():
        carry_ref[...] = jnp.zeros_like(carry_ref)

    x = x_ref[...]
    tm = x.shape[0]
    h = x * lax.rsqrt(jnp.mean(x * x, axis=-1, keepdims=True) + NORM_EPS) * g_ref[...]
    h = h * (1.0 + sc_ref[...]) + sh_ref[...]
    hb = h.astype(BF16)
    h_lo = (h - hb.astype(F32)).astype(BF16)

    pr_ref[...] = jnp.dot(hb, wr_ref[...], preferred_element_type=F32).astype(BF16)

    px = jnp.dot(hb, wx_ref[...], preferred_element_type=F32)
    qk = px[:, :2 * D_GRP]
    ss = _split_dot(qk * qk, _seg_reduce_mat(2 * D_GRP))
    inv = lax.rsqrt(ss * (1.0 / HEAD_DIM) + NORM_EPS)
    qk = qk * _split_dot(inv, _seg_expand_mat(2 * D_GRP)) * qkg_ref[...]
    px_ref[:, :2 * D_GRP] = qk.astype(BF16)
    px_ref[:, 2 * D_GRP:] = px[:, 2 * D_GRP:].astype(BF16)

    z = (jnp.dot(hb, wfh_ref[...], preferred_element_type=F32)
         + jnp.dot(h_lo, wfh_ref[...], preferred_element_type=F32)
         + jnp.dot(hb, wfl_ref[...], preferred_element_type=F32)) + bf_ref[...]
    cum = _split_dot(_log_sigmoid(z), _tri(tm, False), terms=3, left=True) + carry_ref[...]
    carry_ref[...] = cum[tm - 1:tm, :]

    src, dst = _iota((LANES, LANES), 0), _iota((LANES, LANES), 1)
    parts = []
    rem = cum
    for _ in range(3):
        part = rem.astype(BF16)
        rem = rem - part.astype(F32)
        parts.append(part)

    def spread(offset):
        return sum(jnp.dot(part, ((dst == 8 * src + offset + t) & (src < N_HEADS)).astype(BF16),
                           preferred_element_type=F32) for t, part in enumerate(parts))

    slot = _iota((1, LANES), 1) % 8
    kb_ref[...] = (jnp.where((slot >= 3) & (slot < 6), 1.0, 0.0) - spread(0)).astype(BF16)
    qb_ref[...] = (jnp.where(slot < 3, 1.0, 0.0) + spread(3)).astype(BF16)


def _inproj(x, shift, scale, g, w_r, w_x, w_f, b_f, qk_gain, tm):
    w_f_hi = w_f.astype(BF16)
    w_f_lo = (w_f - w_f_hi.astype(F32)).astype(BF16)
    bsz, seq, _ = x.shape
    const = lambda b, s: (0, 0)
    return pl.pallas_call(
        _inproj_kernel,
        grid=(bsz, seq // tm),
        in_specs=[pl.BlockSpec((None, tm, D_MODEL), lambda b, s: (b, s, 0)),
                  pl.BlockSpec((None, 1, D_MODEL), lambda b, s: (b, 0, 0)),
                  pl.BlockSpec((None, 1, D_MODEL), lambda b, s: (b, 0, 0)),
                  pl.BlockSpec((1, D_MODEL), const),
                  pl.BlockSpec((D_MODEL, RWKV_COLS), const),
                  pl.BlockSpec((D_MODEL, FOX_MAIN), const),
                  pl.BlockSpec((D_MODEL, LANES), const),
                  pl.BlockSpec((D_MODEL, LANES), const),
                  pl.BlockSpec((1, LANES), const),
                  pl.BlockSpec((1, 2 * D_GRP), const)],
        out_specs=[pl.BlockSpec((None, tm, RWKV_COLS), lambda b, s: (b, s, 0)),
                   pl.BlockSpec((None, tm, FOX_MAIN), lambda b, s: (b, s, 0)),
                   pl.BlockSpec((None, tm, LANES), lambda b, s: (b, s, 0)),
                   pl.BlockSpec((None, tm, LANES), lambda b, s: (b, s, 0))],
        out_shape=[jax.ShapeDtypeStruct((bsz, seq, RWKV_COLS), BF16),
                   jax.ShapeDtypeStruct((bsz, seq, FOX_MAIN), BF16),
                   jax.ShapeDtypeStruct((bsz, seq, LANES), BF16),
                   jax.ShapeDtypeStruct((bsz, seq, LANES), BF16)],
        scratch_shapes=[pltpu.VMEM((1, LANES), F32)],
        compiler_params=pltpu.CompilerParams(
            dimension_semantics=("parallel", "arbitrary"), vmem_limit_bytes=VMEM_LIMIT),
        name="inproj",
    )(x, shift, scale, g, w_r, w_x, w_f_hi, w_f_lo, b_f, qk_gain)


_NN = (((1,), (0,)), ((), ()))
_NT = (((1,), (1,)), ((), ()))
_TN = (((0,), (0,)), ((), ()))
SCAN_N = HEADS_PER_SCAN * CHUNK
BATCH_PER_STEP = 4
INV_LEVELS = 5
M_HEAD, M_STRICT, M_INCL, M_EYE, M_BASE, M_OFF = 0, 1, 2, 3, 4, 5


def _bdot(a, b, dims):
    return lax.dot_general(a, b, dims, preferred_element_type=F32)


def _scan_masks():
    rr, cc = _iota((SCAN_N, SCAN_W), 0), _iota((SCAN_N, SCAN_W), 1)
    ri, ci = _iota((SCAN_N, SCAN_N), 0), _iota((SCAN_N, SCAN_N), 1)
    same = ri // CHUNK == ci // CHUNK
    masks = [rr // CHUNK == cc // HEAD_DIM, same & (ri > ci), same & (ri >= ci), ri == ci,
             (ri // 2 == ci // 2) & (ri > ci)]
    blk = 2
    while blk < CHUNK:
        masks.append((ri // (2 * blk) == ci // (2 * blk)) & (ri // blk != ci // blk) & (ri > ci))
        blk *= 2
    return jnp.stack(masks).astype(BF16)


def _rwkv_kernel(p_ref, masks_ref, mu_ref, w0_ref, w2_ref, a0_ref, a2_ref, g2_ref, kk_ref, ka_ref,
                 rk_ref, gnw_ref, gnb_ref, o_ref, last_ref, state_ref):
    @pl.when(pl.program_id(1) == 0)
    def _():
        last_ref[...] = jnp.zeros_like(last_ref)
        state_ref[...] = jnp.zeros_like(state_ref)

    mu, w0, w2, a0, a2, g2, k_k, k_a, r_k, gn_w, gn_b = (
        ref[...] for ref in (mu_ref, w0_ref, w2_ref, a0_ref, a2_ref, g2_ref, kk_ref, ka_ref,
                             rk_ref, gnw_ref, gnb_ref))
    rows = BATCH_PER_STEP * CHUNK
    p = p_ref[...].astype(F32).reshape(rows, RWKV_COLS)
    row_id = _iota((rows, 1), 0)
    prev = pltpu.roll(p, 1, axis=0)
    for bb in range(BATCH_PER_STEP):
        prev = jnp.where(row_id == bb * CHUNK, last_ref[bb], prev)
        last_ref[bb] = p[(bb + 1) * CHUNK - 1:(bb + 1) * CHUNK, :]
    pf = p + mu * (prev - p)
    r = pf[:, 0:D_GRP]
    k = pf[:, D_GRP:2 * D_GRP]
    v = pf[:, 2 * D_GRP:3 * D_GRP]
    lora = pf[:, LORA_OFF:GATE_OFF]
    gd = pf[:, GATE_OFF:RWKV_COLS]

    wlog = w0 + _dot(jnp.tanh(lora), w2)
    neg = -wlog
    softplus = jnp.maximum(neg, 0.0) + jnp.log(1.0 + jnp.exp(-jnp.abs(neg)))
    ld = -jnp.exp(-softplus - 0.5)
    a = _sigmoid(a0 + _dot(lora, a2))
    g = _dot(_sigmoid(gd), g2)

    red, exp_m = _seg_reduce_mat(D_GRP), _seg_expand_mat(D_GRP)
    kk = k * k_k
    n2 = _split_dot(kk * kk, red)
    kk = kk * _split_dot(1.0 / jnp.maximum(jnp.sqrt(n2), 1e-12), exp_m)
    k2 = k * (1.0 + (a - 1.0) * k_a)

    tr, tc = _iota((rows, rows), 0), _iota((rows, rows), 1)
    tri = ((tr >= tc) & (tr // CHUNK == tc // CHUNK)).astype(BF16)
    cl = _split_dot(ld, tri, terms=3, left=True)
    cl_end = jnp.concatenate(
        [jnp.broadcast_to(cl[(bb + 1) * CHUNK - 1:(bb + 1) * CHUNK, :], (CHUNK, D_GRP))
         for bb in range(BATCH_PER_STEP)], axis=0)
    e_in = jnp.exp(cl)
    e_out = jnp.exp(-cl)
    e_rem = jnp.exp(cl_end - cl)
    p_end = jnp.exp(cl_end)
    kka = kk * a
    ops = [(-kk * jnp.exp(cl - ld)).astype(BF16), (kka * e_out).astype(BF16),
           (k2 * e_out).astype(BF16), (r * e_in).astype(BF16), v.astype(BF16),
           (kka * e_rem).astype(BF16), (k2 * e_rem).astype(BF16)]

    chains = [(bb, grp) for bb in range(BATCH_PER_STEP)
              for grp in range(N_HEADS // HEADS_PER_SCAN)]
    head_mask = masks_ref[M_HEAD]
    strict, incl = masks_ref[M_STRICT], masks_ref[M_INCL]

    def stacked(op, bb, grp):
        part = op[bb * CHUNK:(bb + 1) * CHUNK, grp * SCAN_W:(grp + 1) * SCAN_W]
        return jnp.concatenate([part] * HEADS_PER_SCAN, axis=0) * head_mask

    xs = [[stacked(op, bb, grp) for op in ops] for bb, grp in chains]
    st = [state_ref[bb, grp] for bb, grp in chains]
    sb = [s.astype(BF16) for s in st]
    nab = [_bdot(x[0], x[1], _NT).astype(BF16) for x in xs]
    aak = [_bdot(x[0], x[2], _NT).astype(BF16) * strict for x in xs]
    arb = [_bdot(x[3], x[1], _NT).astype(BF16) * incl for x in xs]
    ark = [_bdot(x[3], x[2], _NT).astype(BF16) * incl for x in xs]
    t_inv = [masks_ref[M_EYE] + n * masks_ref[M_BASE] for n in nab]
    for lvl in range(INV_LEVELS):
        half = [_bdot(t, n * masks_ref[M_OFF + lvl], _NN).astype(BF16) for t, n in zip(t_inv, nab)]
        t_inv = [t + _bdot(h, t, _NN).astype(BF16) for t, h in zip(t_inv, half)]
    rhs = [(_bdot(x[0], s, _NT) + _bdot(k, x[4], _NN)).astype(BF16)
           for x, s, k in zip(xs, sb, aak)]
    sa = [_bdot(t, h, _NN).astype(BF16) for t, h in zip(t_inv, rhs)]
    ys = [_bdot(x[3], s, _NT) + _bdot(b, u, _NN) + _bdot(k, x[4], _NN)
          for x, s, b, u, k in zip(xs, sb, arb, sa, ark)]
    for (bb, grp), x, s, u in zip(chains, xs, st, sa):
        decay = p_end[bb * CHUNK:bb * CHUNK + 1, grp * SCAN_W:(grp + 1) * SCAN_W]
        state_ref[bb, grp] = s * decay + _bdot(u, x[5], _TN) + _bdot(x[4], x[6], _TN)
    ys = [y[0:CHUNK] + y[CHUNK:2 * CHUNK] + y[2 * CHUNK:3 * CHUNK] + y[3 * CHUNK:4 * CHUNK]
          for y in ys]
    n_grp = N_HEADS // HEADS_PER_SCAN
    y = jnp.concatenate([jnp.concatenate(ys[bb * n_grp:(bb + 1) * n_grp], axis=1)
                         for bb in range(BATCH_PER_STEP)], axis=0)

    mean = _split_dot(_split_dot(y, red) * (1.0 / HEAD_DIM), exp_m)
    d = y - mean
    var = _split_dot(d * d, red) * (1.0 / HEAD_DIM)
    yn = d * _split_dot(lax.rsqrt(var + GN_EPS), exp_m) * gn_w + gn_b
    bonus = _split_dot(_split_dot(r * k2 * r_k, red), exp_m) * v
    o_ref[...] = ((yn + bonus) * g).astype(BF16).reshape(BATCH_PER_STEP, CHUNK, D_GRP)


def _rwkv(p_r, mu, w0, w2p, a0, a2p, g2, k_k, k_a, r_k, gn_w, gn_b):
    bsz, seq, _ = p_r.shape
    assert bsz % BATCH_PER_STEP == 0
    masks = _scan_masks()
    const = lambda b, s: (0, 0)
    vec = pl.BlockSpec((1, D_GRP), const)
    return pl.pallas_call(
        _rwkv_kernel,
        grid=(bsz // BATCH_PER_STEP, seq // CHUNK),
        in_specs=[pl.BlockSpec((BATCH_PER_STEP, CHUNK, RWKV_COLS), lambda b, s: (b, s, 0)),
                  pl.BlockSpec(masks.shape, lambda b, s: (0, 0, 0)),
                  pl.BlockSpec((1, RWKV_COLS), const),
                  vec, pl.BlockSpec((LANES, D_GRP), const),
                  vec, pl.BlockSpec((LANES, D_GRP), const),
                  pl.BlockSpec((LANES, D_GRP), const),
                  vec, vec, vec, vec, vec],
        out_specs=pl.BlockSpec((BATCH_PER_STEP, CHUNK, D_GRP), lambda b, s: (b, s, 0)),
        out_shape=jax.ShapeDtypeStruct((bsz, seq, D_GRP), BF16),
        scratch_shapes=[pltpu.VMEM((BATCH_PER_STEP, 1, RWKV_COLS), F32),
                        pltpu.VMEM((BATCH_PER_STEP, N_HEADS // HEADS_PER_SCAN, SCAN_W, SCAN_W), F32)],
        compiler_params=pltpu.CompilerParams(
            dimension_semantics=("parallel", "arbitrary"), vmem_limit_bytes=VMEM_LIMIT),
        name="rwkv",
    )(p_r, masks, mu, w0, w2p, a0, a2p, g2, k_k, k_a, r_k, gn_w, gn_b)


def _fox_kernel(q_ref, qb_ref, k_ref, kb_ref, vt_ref, og_ref, ong_ref, o_ref, *, tq):
    hp = pl.program_id(1)
    qi = pl.program_id(2)
    lane = _iota((1, LANES), 1)
    q = q_ref[...]
    qb = qb_ref[...]
    zero = jnp.zeros_like(q)
    qcat = [jnp.concatenate([jnp.where(lane // HEAD_DIM == hh, q, zero),
                             jnp.where(lane // 8 == hp * 2 + hh, qb, zero)], axis=1)
            for hh in range(2)]
    key_pos = _iota((tq, tq), 0)
    qry_pos = _iota((tq, tq), 1)

    def tile(j, carry, masked):
        start = pl.multiple_of(j * tq, tq)
        kcat = jnp.concatenate([k_ref[pl.ds(start, tq), :], kb_ref[pl.ds(start, tq), :]], axis=1)
        vt = vt_ref[:, pl.ds(start, tq)]
        sts = [lax.dot_general(kcat, qc, _NT, preferred_element_type=F32) for qc in qcat]
        if masked:
            sts = [jnp.where(qry_pos >= key_pos, st, -jnp.inf) for st in sts]
        m_new = [jnp.maximum(c[0], jnp.max(st, axis=0, keepdims=True)) for c, st in zip(carry, sts)]
        pts = [jnp.exp(st - m) for st, m in zip(sts, m_new)]
        pvs = [jnp.dot(vt, pt.astype(BF16), preferred_element_type=F32) for pt in pts]
        new = []
        for hh in range(2):
            m, l, acc = carry[hh]
            alpha = jnp.exp(m - m_new[hh])
            l = alpha * l + jnp.sum(pts[hh], axis=0, keepdims=True)
            acc = alpha * acc + pvs[hh][hh * HEAD_DIM:(hh + 1) * HEAD_DIM, :]
            new.append((m_new[hh], l, acc))
        return tuple(new)

    init = (jnp.full((1, tq), -jnp.inf, F32), jnp.zeros((1, tq), F32),
            jnp.zeros((HEAD_DIM, tq), F32))
    carry = lax.fori_loop(0, qi, functools.partial(tile, masked=False), (init, init))
    outs = []
    for _, l, acc in tile(qi, carry, True):
        o = acc / l
        outs.append(o * lax.rsqrt(jnp.mean(o * o, axis=0, keepdims=True) + NORM_EPS))
    o = jnp.concatenate(outs, axis=0).T
    o_ref[...] = (o * ong_ref[...] * _sigmoid(og_ref[...].astype(F32))).astype(BF16)


def _fox(p_x, k_bias, q_bias, o_gain, tq):
    bsz, seq, _ = p_x.shape
    npair = N_HEADS // 2
    v_t = jnp.transpose(p_x[:, :, 2 * D_GRP:3 * D_GRP], (0, 2, 1))
    return pl.pallas_call(
        functools.partial(_fox_kernel, tq=tq),
        grid=(bsz, npair, seq // tq),
        in_specs=[pl.BlockSpec((None, tq, LANES), lambda b, h, i: (b, i, h)),
                  pl.BlockSpec((None, tq, LANES), lambda b, h, i: (b, i, 0)),
                  pl.BlockSpec((None, seq, LANES), lambda b, h, i: (b, 0, npair + h)),
                  pl.BlockSpec((None, seq, LANES), lambda b, h, i: (b, 0, 0)),
                  pl.BlockSpec((None, LANES, seq), lambda b, h, i: (b, h, 0)),
                  pl.BlockSpec((None, tq, LANES), lambda b, h, i: (b, i, 3 * npair + h)),
                  pl.BlockSpec((1, LANES), lambda b, h, i: (0, 0))],
        out_specs=pl.BlockSpec((None, tq, LANES), lambda b, h, i: (b, i, h)),
        out_shape=jax.ShapeDtypeStruct((bsz, seq, D_GRP), BF16),
        compiler_params=pltpu.CompilerParams(
            dimension_semantics=("parallel", "parallel", "arbitrary"),
            vmem_limit_bytes=VMEM_LIMIT),
        name="fox",
    )(p_x, q_bias, p_x, k_bias, v_t, p_x, o_gain)


def _outproj_kernel(x_ref, yr_ref, yf_ref, g1_ref, sh_ref, sc_ref, ng_ref, wor_ref, wof_ref,
                    wrt_ref, brt_ref, x1_ref, h2_ref, idx_ref, gate_ref, rank_ref, cnt_ref,
                    carry_ref):
    @pl.when(pl.program_id(0) == 0)
    def _():
        carry_ref[...] = jnp.zeros_like(carry_ref)

    y = (jnp.dot(yr_ref[...], wor_ref[...], preferred_element_type=F32)
         + jnp.dot(yf_ref[...], wof_ref[...], preferred_element_type=F32))
    x1 = x_ref[...] + g1_ref[...] * y
    x1_ref[...] = x1
    tm = x1.shape[0]
    h = x1 * lax.rsqrt(jnp.mean(x1 * x1, axis=-1, keepdims=True) + NORM_EPS) * ng_ref[...]
    h2 = h * (1.0 + sc_ref[...]) + sh_ref[...]
    h2_ref[...] = h2

    lane = _iota((tm, LANES), 1)
    logits = _fdot(h2, wrt_ref[...]) + brt_ref[...]
    lg = jnp.where(lane < N_EXPERTS, logits, -jnp.inf)
    picks = []
    hot_sum = jnp.zeros((tm, LANES), F32)
    for _ in range(TOP_K):
        m = jnp.max(lg, axis=-1, keepdims=True)
        sel = jnp.min(jnp.where(lg == m, lane, LANES), axis=-1, keepdims=True)
        hot = lane == sel
        picks.append((m, sel, hot))
        hot_sum = hot_sum + hot.astype(F32)
        lg = jnp.where(hot, -jnp.inf, lg)
    es = [jnp.exp(m - picks[0][0]) for m, _, _ in picks]
    den = es[0] + es[1] + es[2] + es[3]

    before = jnp.dot(_tri(tm, True), hot_sum.astype(BF16), preferred_element_type=F32)
    before = before + carry_ref[...]
    idx_out = jnp.zeros((tm, LANES), jnp.int32)
    gate_out = jnp.zeros((tm, LANES), F32)
    rank_out = jnp.zeros((tm, LANES), jnp.int32)
    for kk, (m, sel, hot) in enumerate(picks):
        rk = jnp.sum(jnp.where(hot, before, 0.0), axis=-1, keepdims=True).astype(jnp.int32)
        idx_out = jnp.where(lane == kk, sel, idx_out)
        gate_out = jnp.where(lane == kk, es[kk] / den, gate_out)
        rank_out = jnp.where(lane == kk, rk, rank_out)
    idx_ref[...] = idx_out
    gate_ref[...] = gate_out
    rank_ref[...] = rank_out
    carry_ref[...] = carry_ref[...] + jnp.sum(hot_sum, axis=0, keepdims=True)
    cnt_ref[...] = carry_ref[...]


def _outproj(x2d, y_r, y_f, gate1, shift2, scale2, norm_g, wo_r, wo_f, w_rt, b_rt, tm, seq):
    t = x2d.shape[0]
    per_b = seq // tm
    const = lambda i: (0, 0)
    rows = lambda i: (i, 0)
    mod = pl.BlockSpec((None, 1, D_MODEL), lambda i: (i // per_b, 0, 0))
    return pl.pallas_call(
        _outproj_kernel,
        grid=(t // tm,),
        in_specs=[pl.BlockSpec((tm, D_MODEL), rows),
                  pl.BlockSpec((tm, D_GRP), rows),
                  pl.BlockSpec((tm, D_GRP), rows),
                  mod, mod, mod,
                  pl.BlockSpec((1, D_MODEL), const),
                  pl.BlockSpec((D_GRP, D_MODEL), const),
                  pl.BlockSpec((D_GRP, D_MODEL), const),
                  pl.BlockSpec((D_MODEL, LANES), const),
                  pl.BlockSpec((1, LANES), const)],
        out_specs=[pl.BlockSpec((tm, D_MODEL), rows),
                   pl.BlockSpec((tm, D_MODEL), rows),
                   pl.BlockSpec((tm, LANES), rows),
                   pl.BlockSpec((tm, LANES), rows),
                   pl.BlockSpec((tm, LANES), rows),
                   pl.BlockSpec((1, LANES), const)],
        out_shape=[jax.ShapeDtypeStruct((t, D_MODEL), F32),
                   jax.ShapeDtypeStruct((t, D_MODEL), F32),
                   jax.ShapeDtypeStruct((t, LANES), jnp.int32),
                   jax.ShapeDtypeStruct((t, LANES), F32),
                   jax.ShapeDtypeStruct((t, LANES), jnp.int32),
                   jax.ShapeDtypeStruct((1, LANES), F32)],
        scratch_shapes=[pltpu.VMEM((1, LANES), F32)],
        compiler_params=pltpu.CompilerParams(
            dimension_semantics=("arbitrary",), vmem_limit_bytes=VMEM_LIMIT),
        name="outproj",
    )(x2d, y_r, y_f, gate1, shift2, scale2, norm_g, wo_r, wo_f, w_rt, b_rt)


SC_CORES = 2
SC_SUBCORES = 16
SC_ROWS = 32


def _sc_gather_rows(idx, src):
    n_workers = SC_CORES * SC_SUBCORES
    m = idx.shape[0]
    d = src.shape[1]
    assert m % (n_workers * SC_ROWS) == 0
    n_chunks = m // (n_workers * SC_ROWS)
    mesh = plsc.VectorSubcoreMesh(core_axis_name="c", subcore_axis_name="s")

    @functools.partial(
        pl.kernel, mesh=mesh,
        out_type=jax.ShapeDtypeStruct((m, d), src.dtype),
        scratch_types=[pltpu.VMEM((n_chunks, SC_ROWS), jnp.int32),
                       pltpu.VMEM((SC_ROWS, d), src.dtype),
                       pltpu.SemaphoreType.DMA],
        name="sc_gather")
    def gather(src_hbm, idx_hbm, out_hbm, idx_v, rows_v, sem):
        wid = lax.axis_index("s") * SC_CORES + lax.axis_index("c")
        pltpu.sync_copy(idx_hbm.at[wid], idx_v)

        @pl.loop(0, n_chunks)
        def _(j):
            pltpu.async_copy(src_hbm.at[idx_v.at[j]], rows_v, sem).wait()
            pltpu.sync_copy(rows_v, out_hbm.at[pl.ds((wid * n_chunks + j) * SC_ROWS, SC_ROWS)])

    return gather(src, idx.reshape(n_workers, n_chunks, SC_ROWS))


def _expert_kernel(be_ref, x_ref, wgu_ref, bgu_ref, wd_ref, bd_ref, o_ref):
    del be_ref
    gu = jnp.dot(x_ref[...].astype(BF16), wgu_ref[...], preferred_element_type=F32) + bgu_ref[...]
    gate = jnp.minimum(gu[:, :D_MODEL], SWIGLU_LIMIT)
    up = jnp.clip(gu[:, D_MODEL:], -SWIGLU_LIMIT, SWIGLU_LIMIT)
    act = gate * _sigmoid(SWIGLU_ALPHA * gate) * (up + 1.0)
    o_ref[...] = jnp.dot(act.astype(BF16), wd_ref[...], preferred_element_type=F32) + bd_ref[...]


def _experts(block_e, xs, w_gu, b_gu, w_d, b_d):
    n_blocks = block_e.shape[0]
    grid_spec = pltpu.PrefetchScalarGridSpec(
        num_scalar_prefetch=1,
        grid=(n_blocks,),
        in_specs=[pl.BlockSpec((EXPERT_BLOCK, D_MODEL), lambda j, be: (j, 0)),
                  pl.BlockSpec((None, D_MODEL, 2 * D_MODEL), lambda j, be: (be[j], 0, 0)),
                  pl.BlockSpec((None, 1, 2 * D_MODEL), lambda j, be: (be[j], 0, 0)),
                  pl.BlockSpec((None, D_MODEL, D_MODEL), lambda j, be: (be[j], 0, 0)),
                  pl.BlockSpec((None, 1, D_MODEL), lambda j, be: (be[j], 0, 0))],
        out_specs=pl.BlockSpec((EXPERT_BLOCK, D_MODEL), lambda j, be: (j, 0)),
    )
    return pl.pallas_call(
        _expert_kernel,
        grid_spec=grid_spec,
        out_shape=jax.ShapeDtypeStruct(xs.shape, F32),
        compiler_params=pltpu.CompilerParams(
            dimension_semantics=("arbitrary",), vmem_limit_bytes=VMEM_LIMIT),
        name="experts",
    )(block_e, xs, w_gu, b_gu, w_d, b_d)


COMBINE_TOKENS = 256


def _combine_kernel(yg_ref, x1_ref, gate_ref, g2_ref, fg_ref, o_ref):
    gates = gate_ref[...]
    acc = None
    for kk in range(TOP_K):
        part = gates[:, kk:kk + 1] * yg_ref[kk * COMBINE_TOKENS:(kk + 1) * COMBINE_TOKENS, :]
        acc = part if acc is None else acc + part
    x2 = x1_ref[...] + g2_ref[...] * acc
    o_ref[...] = x2 * lax.rsqrt(jnp.mean(x2 * x2, axis=-1, keepdims=True) + NORM_EPS) * fg_ref[...]


def _combine(yg, x1, gates, gate2, final_g, seq):
    t = x1.shape[0]
    tm = COMBINE_TOKENS
    per_b = seq // tm
    rows = lambda i: (i, 0)
    return pl.pallas_call(
        _combine_kernel,
        grid=(t // tm,),
        in_specs=[pl.BlockSpec((TOP_K * tm, D_MODEL), rows),
                  pl.BlockSpec((tm, D_MODEL), rows),
                  pl.BlockSpec((tm, LANES), rows),
                  pl.BlockSpec((None, 1, D_MODEL), lambda i: (i // per_b, 0, 0)),
                  pl.BlockSpec((1, D_MODEL), lambda i: (0, 0))],
        out_specs=pl.BlockSpec((tm, D_MODEL), rows),
        out_shape=jax.ShapeDtypeStruct((t, D_MODEL), F32),
        compiler_params=pltpu.CompilerParams(
            dimension_semantics=("parallel",), vmem_limit_bytes=VMEM_LIMIT),
        name="combine",
    )(yg, x1, gates, gate2, final_g)


def _moe(h2, idx, gates, rank, counts, x1, gate2, final_g, w_gu, b_gu, w_d, b_d, seq):
    t = h2.shape[0]
    n_slots = t * TOP_K
    n_blocks = -(-n_slots // EXPERT_BLOCK) + N_EXPERTS
    cap = n_blocks * EXPERT_BLOCK
    padded = (counts + EXPERT_BLOCK - 1) // EXPERT_BLOCK * EXPERT_BLOCK
    pad_ends = jnp.cumsum(padded)
    pad_starts = pad_ends - padded
    dest = pad_starts[idx] + rank
    tok = jnp.broadcast_to(jnp.arange(t, dtype=jnp.int32)[:, None], (t, TOP_K))
    buf_tok = jnp.zeros((cap,), jnp.int32).at[dest.reshape(-1)].set(tok.reshape(-1))
    block_starts = jnp.arange(n_blocks, dtype=jnp.int32) * EXPERT_BLOCK
    block_e = jnp.minimum(jnp.sum(block_starts[:, None] >= pad_ends[None, :], axis=1),
                          N_EXPERTS - 1).astype(jnp.int32)

    xs = _sc_gather_rows(buf_tok, h2)
    yb = _experts(block_e, xs, w_gu, b_gu, w_d, b_d)
    dest_blocks = dest.reshape(-1, COMBINE_TOKENS, TOP_K).transpose(0, 2, 1).reshape(-1)
    yg = _sc_gather_rows(dest_blocks, yb)
    return _combine(yg, x1, gates, gate2, final_g, seq)


def _layer(x, c_mod, norm1_g, w_in, mu_shift, w0, w2, a0, a2, g2, k_k, k_a, r_k, gn_w, gn_b, b_f,
           q_norm_g, k_norm_g, o_norm_g, w_out, norm2_g, w_router, b_router, w_gate_up,
           b_gate_up, w_down, b_down, final_g, tm_in, tq, tm_out):
    bsz, seq, _ = x.shape
    shift1, scale1, gate1, shift2, scale2, gate2 = (
        m.reshape(bsz, 1, D_MODEL) for m in jnp.split(c_mod, 6, axis=-1))
    row = lambda v: v.reshape(1, -1)

    w_r = w_in[:, :RWKV_COLS].astype(BF16)
    w_x = w_in[:, RWKV_COLS:RWKV_COLS + FOX_MAIN].astype(BF16)
    w_f = jnp.pad(w_in[:, RWKV_COLS + FOX_MAIN:], ((0, 0), (0, LANES - N_HEADS)))
    b_fp = jnp.pad(b_f, (0, LANES - N_HEADS)).reshape(1, LANES)
    qk_gain = jnp.concatenate([jnp.tile(q_norm_g, N_HEADS) * HEAD_DIM ** -0.5,
                               jnp.tile(k_norm_g, N_HEADS)]).reshape(1, -1)
    p_r, p_x, k_bias, q_bias = _inproj(x, shift1, scale1, row(norm1_g), w_r, w_x, w_f, b_fp,
                                       qk_gain, tm_in)

    zeros = jnp.zeros((LANES - 64, D_GRP), F32)
    w2p = jnp.concatenate([w2, zeros], axis=0).astype(BF16)
    a2p = jnp.concatenate([zeros, a2], axis=0).astype(BF16)
    y_r = _rwkv(p_r, row(mu_shift), row(w0), w2p, row(a0), a2p, g2.astype(BF16), row(k_k),
                row(k_a), row(r_k), row(gn_w), row(gn_b))

    y_f = _fox(p_x, k_bias, q_bias, jnp.tile(o_norm_g, 2).reshape(1, LANES), tq)

    t = bsz * seq
    w_rt = jnp.pad(w_router, ((0, 0), (0, LANES - N_EXPERTS)))
    b_rt = jnp.pad(b_router, (0, LANES - N_EXPERTS)).reshape(1, LANES)
    wo = w_out.astype(BF16)
    x1, h2, idx, gates, rank, cnt = _outproj(
        x.reshape(t, D_MODEL), y_r.reshape(t, D_GRP), y_f.reshape(t, D_GRP), gate1, shift2,
        scale2, row(norm2_g), wo[:D_GRP], wo[D_GRP:], w_rt, b_rt, tm_out, seq)

    counts = cnt[0, :N_EXPERTS].astype(jnp.int32)
    out = _moe(h2, idx[:, :TOP_K], gates, rank[:, :TOP_K], counts, x1, gate2, row(final_g),
               w_gate_up.astype(BF16), b_gate_up.reshape(N_EXPERTS, 1, -1),
               w_down.astype(BF16), b_down.reshape(N_EXPERTS, 1, -1), seq)
    return out.reshape(bsz, seq, D_MODEL)


def kernel(x, c, w_ada, b_ada, norm1_g, w_in, mu_shift, w0, w2, a0, a2, g2, k_k, k_a, r_k, gn_w,
           gn_b, b_f, q_norm_g, k_norm_g, o_norm_g, w_out, norm2_g, w_router, b_router, w_gate_up,
           b_gate_up, w_down, b_down, final_g):
    assert w_ada.shape[0] == 1, "single-layer block"
    c_mod = _adaln(c, w_ada[0], b_ada[0])
    return _layer(x, c_mod, norm1_g[0], w_in[0], mu_shift[0], w0[0], w2[0], a0[0], a2[0], g2[0],
                  k_k[0], k_a[0], r_k[0], gn_w[0], gn_b[0], b_f[0], q_norm_g[0], k_norm_g[0],
                  o_norm_g[0], w_out[0], norm2_g[0], w_router[0], b_router[0], w_gate_up[0],
                  b_gate_up[0], w_down[0], b_down[0], final_g,
                  tm_in=min(512, x.shape[1]), tq=min(512, x.shape[1]), tm_out=min(512, x.shape[1]))
```

```python
import functools

import jax
import jax.numpy as jnp
from jax import lax
from jax.experimental import pallas as pl
from jax.experimental.pallas import tpu as pltpu
from jax.experimental.pallas import tpu_sc as plsc

F32 = jnp.float32
BF16 = jnp.bfloat16
HIGHEST = lax.Precision.HIGHEST

D_MODEL = 1024
HEAD_DIM = 64
N_HEADS = 8
D_GRP = N_HEADS * HEAD_DIM
RWKV_COLS = 1792
LORA_OFF = 3 * D_GRP
GATE_OFF = LORA_OFF + 128
FOX_MAIN = 4 * D_GRP
N_EXPERTS = 32
TOP_K = 4
EXPERT_BLOCK = 256
SWIGLU_ALPHA = 1.702
SWIGLU_LIMIT = 7.0
NORM_EPS = 1e-6
GN_EPS = 64e-5
LANES = 128
CHUNK = 64
HEADS_PER_SCAN = 4
SCAN_W = HEADS_PER_SCAN * HEAD_DIM
VMEM_LIMIT = 56 * 1024 * 1024


def _dot(a, b):
    return jnp.dot(a.astype(BF16), b.astype(BF16), preferred_element_type=F32)


def _dot_nt(a, b):
    return lax.dot_general(a.astype(BF16), b.astype(BF16), (((1,), (1,)), ((), ())),
                           preferred_element_type=F32)


def _dot_tn(a, b):
    return lax.dot_general(a.astype(BF16), b.astype(BF16), (((0,), (0,)), ((), ())),
                           preferred_element_type=F32)


def _fdot(a, b):
    return jnp.dot(a, b, precision=HIGHEST, preferred_element_type=F32)


def _split_dot(x, m, terms=2, left=False):
    acc = None
    rem = x
    for _ in range(terms):
        part = rem.astype(BF16)
        rem = rem - part.astype(F32)
        d = (jnp.dot(m, part, preferred_element_type=F32) if left
             else jnp.dot(part, m, preferred_element_type=F32))
        acc = d if acc is None else acc + d
    return acc


def _iota(shape, dim):
    return lax.broadcasted_iota(jnp.int32, shape, dim)


def _seg_reduce_mat(n):
    return (_iota((n, LANES), 0) // HEAD_DIM == _iota((n, LANES), 1)).astype(BF16)


def _seg_expand_mat(n):
    return (_iota((LANES, n), 1) // HEAD_DIM == _iota((LANES, n), 0)).astype(BF16)


def _tri(n, strict):
    r, c = _iota((n, n), 0), _iota((n, n), 1)
    return ((r > c) if strict else (r >= c)).astype(BF16)


def _log_sigmoid(z):
    return jnp.minimum(z, 0.0) - jnp.log(1.0 + jnp.exp(-jnp.abs(z)))


def _sigmoid(z):
    return 1.0 / (1.0 + jnp.exp(-z))


def _adaln_kernel(c_ref, w_ref, b_ref, o_ref):
    c = c_ref[...]
    o_ref[...] = _fdot(c * _sigmoid(c), w_ref[...]) + b_ref[...]


def _adaln(c, w_ada, b_ada):
    bsz = c.shape[0]
    n_mod = w_ada.shape[1] // D_MODEL
    return pl.pallas_call(
        _adaln_kernel,
        grid=(n_mod,),
        in_specs=[pl.BlockSpec((bsz, D_MODEL), lambda j: (0, 0)),
                  pl.BlockSpec((D_MODEL, D_MODEL), lambda j: (0, j)),
                  pl.BlockSpec((1, D_MODEL), lambda j: (0, j))],
        out_specs=pl.BlockSpec((bsz, D_MODEL), lambda j: (0, j)),
        out_shape=jax.ShapeDtypeStruct((bsz, n_mod * D_MODEL), F32),
        name="adaln",
    )(c, w_ada, b_ada.reshape(1, -1))


def _inproj_kernel(x_ref, sh_ref, sc_ref, g_ref, wr_ref, wx_ref, wfh_ref, wfl_ref, bf_ref, qkg_ref,
                   pr_ref, px_ref, kb_ref, qb_ref, carry_ref):
    @pl.when(pl.program_id(1) == 0)
    def _():
        carry_ref[...] = jnp.zeros_like(carry_ref)

    x = x_ref[...]
    tm = x.shape[0]
    h = x * lax.rsqrt(jnp.mean(x * x, axis=-1, keepdims=True) + NORM_EPS) * g_ref[...]
    h = h * (1.0 + sc_ref[...]) + sh_ref[...]
    hb = h.astype(BF16)
    h_lo = (h - hb.astype(F32)).astype(BF16)

    pr_ref[...] = jnp.dot(hb, wr_ref[...], preferred_element_type=F32).astype(BF16)

    px = jnp.dot(hb, wx_ref[...], preferred_element_type=F32)
    qk = px[:, :2 * D_GRP]
    ss = _split_dot(qk * qk, _seg_reduce_mat(2 * D_GRP))
    inv = lax.rsqrt(ss * (1.0 / HEAD_DIM) + NORM_EPS)
    qk = qk * _split_dot(inv, _seg_expand_mat(2 * D_GRP)) * qkg_ref[...]
    px_ref[:, :2 * D_GRP] = qk.astype(BF16)
    px_ref[:, 2 * D_GRP:] = px[:, 2 * D_GRP:].astype(BF16)

    z = (jnp.dot(hb, wfh_ref[...], preferred_element_type=F32)
         + jnp.dot(h_lo, wfh_ref[...], preferred_element_type=F32)
         + jnp.dot(hb, wfl_ref[...], preferred_element_type=F32)) + bf_ref[...]
    cum = _split_dot(_log_sigmoid(z), _tri(tm, False), terms=3, left=True) + carry_ref[...]
    carry_ref[...] = cum[tm - 1:tm, :]

    src, dst = _iota((LANES, LANES), 0), _iota((LANES, LANES), 1)
    parts = []
    rem = cum
    for _ in range(3):
        part = rem.astype(BF16)
        rem = rem - part.astype(F32)
        parts.append(part)

    def spread(offset):
        return sum(jnp.dot(part, ((dst == 8 * src + offset + t) & (src < N_HEADS)).astype(BF16),
                           preferred_element_type=F32) for t, part in enumerate(parts))

    slot = _iota((1, LANES), 1) % 8
    kb_ref[...] = (jnp.where((slot >= 3) & (slot < 6), 1.0, 0.0) - spread(0)).astype(BF16)
    qb_ref[...] = (jnp.where(slot < 3, 1.0, 0.0) + spread(3)).astype(BF16)


def _inproj(x, shift, scale, g, w_r, w_x, w_f, b_f, qk_gain, tm):
    w_f_hi = w_f.astype(BF16)
    w_f_lo = (w_f - w_f_hi.astype(F32)).astype(BF16)
    bsz, seq, _ = x.shape
    const = lambda b, s: (0, 0)
    return pl.pallas_call(
        _inproj_kernel,
        grid=(bsz, seq // tm),
        in_specs=[pl.BlockSpec((None, tm, D_MODEL), lambda b, s: (b, s, 0)),
                  pl.BlockSpec((None, 1, D_MODEL), lambda b, s: (b, 0, 0)),
                  pl.BlockSpec((None, 1, D_MODEL), lambda b, s: (b, 0, 0)),
                  pl.BlockSpec((1, D_MODEL), const),
                  pl.BlockSpec((D_MODEL, RWKV_COLS), const),
                  pl.BlockSpec((D_MODEL, FOX_MAIN), const),
                  pl.BlockSpec((D_MODEL, LANES), const),
                  pl.BlockSpec((D_MODEL, LANES), const),
                  pl.BlockSpec((1, LANES), const),
                  pl.BlockSpec((1, 2 * D_GRP), const)],
        out_specs=[pl.BlockSpec((None, tm, RWKV_COLS), lambda b, s: (b, s, 0)),
                   pl.BlockSpec((None, tm, FOX_MAIN), lambda b, s: (b, s, 0)),
                   pl.BlockSpec((None, tm, LANES), lambda b, s: (b, s, 0)),
                   pl.BlockSpec((None, tm, LANES), lambda b, s: (b, s, 0))],
        out_shape=[jax.ShapeDtypeStruct((bsz, seq, RWKV_COLS), BF16),
                   jax.ShapeDtypeStruct((bsz, seq, FOX_MAIN), BF16),
                   jax.ShapeDtypeStruct((bsz, seq, LANES), BF16),
                   jax.ShapeDtypeStruct((bsz, seq, LANES), BF16)],
        scratch_shapes=[pltpu.VMEM((1, LANES), F32)],
        compiler_params=pltpu.CompilerParams(
            dimension_semantics=("parallel", "arbitrary"), vmem_limit_bytes=VMEM_LIMIT),
        name="inproj",
    )(x, shift, scale, g, w_r, w_x, w_f_hi, w_f_lo, b_f, qk_gain)


_NN = (((1,), (0,)), ((), ()))
_NT = (((1,), (1,)), ((), ()))
_TN = (((0,), (0,)), ((), ()))
SCAN_N = HEADS_PER_SCAN * CHUNK
BATCH_PER_STEP = 4
INV_LEVELS = 5
M_HEAD, M_STRICT, M_INCL, M_EYE, M_BASE, M_OFF = 0, 1, 2, 3, 4, 5


def _bdot(a, b, dims):
    return lax.dot_general(a, b, dims, preferred_element_type=F32)


def _scan_masks():
    rr, cc = _iota((SCAN_N, SCAN_W), 0), _iota((SCAN_N, SCAN_W), 1)
    ri, ci = _iota((SCAN_N, SCAN_N), 0), _iota((SCAN_N, SCAN_N), 1)
    same = ri // CHUNK == ci // CHUNK
    masks = [rr // CHUNK == cc // HEAD_DIM, same & (ri > ci), same & (ri >= ci), ri == ci,
             (ri // 2 == ci // 2) & (ri > ci)]
    blk = 2
    while blk < CHUNK:
        masks.append((ri // (2 * blk) == ci // (2 * blk)) & (ri // blk != ci // blk) & (ri > ci))
        blk *= 2
    return jnp.stack(masks).astype(BF16)


def _rwkv_kernel(p_ref, masks_ref, mu_ref, w0_ref, w2_ref, a0_ref, a2_ref, g2_ref, kk_ref, ka_ref,
                 rk_ref, gnw_ref, gnb_ref, o_ref, last_ref, state_ref):
    @pl.when(pl.program_id(1) == 0)
    def _():
        last_ref[...] = jnp.zeros_like(last_ref)
        state_ref[...] = jnp.zeros_like(state_ref)

    mu, w0, w2, a0, a2, g2, k_k, k_a, r_k, gn_w, gn_b = (
        ref[...] for ref in (mu_ref, w0_ref, w2_ref, a0_ref, a2_ref, g2_ref, kk_ref, ka_ref,
                             rk_ref, gnw_ref, gnb_ref))
    rows = BATCH_PER_STEP * CHUNK
    p = p_ref[...].astype(F32).reshape(rows, RWKV_COLS)
    row_id = _iota((rows, 1), 0)
    prev = pltpu.roll(p, 1, axis=0)
    for bb in range(BATCH_PER_STEP):
        prev = jnp.where(row_id == bb * CHUNK, last_ref[bb], prev)
        last_ref[bb] = p[(bb + 1) * CHUNK - 1:(bb + 1) * CHUNK, :]
    pf = p + mu * (prev - p)
    r = pf[:, 0:D_GRP]
    k = pf[:, D_GRP:2 * D_GRP]
    v = pf[:, 2 * D_GRP:3 * D_GRP]
    lora = pf[:, LORA_OFF:GATE_OFF]
    gd = pf[:, GATE_OFF:RWKV_COLS]

    wlog = w0 + _dot(jnp.tanh(lora), w2)
    neg = -wlog
    softplus = jnp.maximum(neg, 0.0) + jnp.log(1.0 + jnp.exp(-jnp.abs(neg)))
    ld = -jnp.exp(-softplus - 0.5)
    a = _sigmoid(a0 + _dot(lora, a2))
    g = _dot(_sigmoid(gd), g2)

    red, exp_m = _seg_reduce_mat(D_GRP), _seg_expand_mat(D_GRP)
    kk = k * k_k
    n2 = _split_dot(kk * kk, red)
    kk = kk * _split_dot(1.0 / jnp.maximum(jnp.sqrt(n2), 1e-12), exp_m)
    k2 = k * (1.0 + (a - 1.0) * k_a)

    tr, tc = _iota((rows, rows), 0), _iota((rows, rows), 1)
    tri = ((tr >= tc) & (tr // CHUNK == tc // CHUNK)).astype(BF16)
    cl = _split_dot(ld, tri, terms=3, left=True)
    cl_end = jnp.concatenate(
        [jnp.broadcast_to(cl[(bb + 1) * CHUNK - 1:(bb + 1) * CHUNK, :], (CHUNK, D_GRP))
         for bb in range(BATCH_PER_STEP)], axis=0)
    e_in = jnp.exp(cl)
    e_out = jnp.exp(-cl)
    e_rem = jnp.exp(cl_end - cl)
    p_end = jnp.exp(cl_end)
    kka = kk * a
    ops = [(-kk * jnp.exp(cl - ld)).astype(BF16), (kka * e_out).astype(BF16),
           (k2 * e_out).astype(BF16), (r * e_in).astype(BF16), v.astype(BF16),
           (kka * e_rem).astype(BF16), (k2 * e_rem).astype(BF16)]

    chains = [(bb, grp) for bb in range(BATCH_PER_STEP)
              for grp in range(N_HEADS // HEADS_PER_SCAN)]
    head_mask = masks_ref[M_HEAD]
    strict, incl = masks_ref[M_STRICT], masks_ref[M_INCL]

    def stacked(op, bb, grp):
        part = op[bb * CHUNK:(bb + 1) * CHUNK, grp * SCAN_W:(grp + 1) * SCAN_W]
        return jnp.concatenate([part] * HEADS_PER_SCAN, axis=0) * head_mask

    xs = [[stacked(op, bb, grp) for op in ops] for bb, grp in chains]
    st = [state_ref[bb, grp] for bb, grp in chains]
    sb = [s.astype(BF16) for s in st]
    nab = [_bdot(x[0], x[1], _NT).astype(BF16) for x in xs]
    aak = [_bdot(x[0], x[2], _NT).astype(BF16) * strict for x in xs]
    arb = [_bdot(x[3], x[1], _NT).astype(BF16) * incl for x in xs]
    ark = [_bdot(x[3], x[2], _NT).astype(BF16) * incl for x in xs]
    t_inv = [masks_ref[M_EYE] + n * masks_ref[M_BASE] for n in nab]
    for lvl in range(INV_LEVELS):
        half = [_bdot(t, n * masks_ref[M_OFF + lvl], _NN).astype(BF16) for t, n in zip(t_inv, nab)]
        t_inv = [t + _bdot(h, t, _NN).astype(BF16) for t, h in zip(t_inv, half)]
    rhs = [(_bdot(x[0], s, _NT) + _bdot(k, x[4], _NN)).astype(BF16)
           for x, s, k in zip(xs, sb, aak)]
    sa = [_bdot(t, h, _NN).astype(BF16) for t, h in zip(t_inv, rhs)]
    ys = [_bdot(x[3], s, _NT) + _bdot(b, u, _NN) + _bdot(k, x[4], _NN)
          for x, s, b, u, k in zip(xs, sb, arb, sa, ark)]
    for (bb, grp), x, s, u in zip(chains, xs, st, sa):
        decay = p_end[bb * CHUNK:bb * CHUNK + 1, grp * SCAN_W:(grp + 1) * SCAN_W]
        state_ref[bb, grp] = s * decay + _bdot(u, x[5], _TN) + _bdot(x[4], x[6], _TN)
    ys = [y[0:CHUNK] + y[CHUNK:2 * CHUNK] + y[2 * CHUNK:3 * CHUNK] + y[3 * CHUNK:4 * CHUNK]
          for y in ys]
    n_grp = N_HEADS // HEADS_PER_SCAN
    y = jnp.concatenate([jnp.concatenate(ys[bb * n_grp:(bb + 1) * n_grp], axis=1)
                         for bb in range(BATCH_PER_STEP)], axis=0)

    mean = _split_dot(_split_dot(y, red) * (1.0 / HEAD_DIM), exp_m)
    d = y - mean
    var = _split_dot(d * d, red) * (1.0 / HEAD_DIM)
    yn = d * _split_dot(lax.rsqrt(var + GN_EPS), exp_m) * gn_w + gn_b
    bonus = _split_dot(_split_dot(r * k2 * r_k, red), exp_m) * v
    o_ref[...] = ((yn + bonus) * g).astype(BF16).reshape(BATCH_PER_STEP, CHUNK, D_GRP)


def _rwkv(p_r, mu, w0, w2p, a0, a2p, g2, k_k, k_a, r_k, gn_w, gn_b):
    bsz, seq, _ = p_r.shape
    assert bsz % BATCH_PER_STEP == 0
    masks = _scan_masks()
    const = lambda b, s: (0, 0)
    vec = pl.BlockSpec((1, D_GRP), const)
    return pl.pallas_call(
        _rwkv_kernel,
        grid=(bsz // BATCH_PER_STEP, seq // CHUNK),
        in_specs=[pl.BlockSpec((BATCH_PER_STEP, CHUNK, RWKV_COLS), lambda b, s: (b, s, 0)),
                  pl.BlockSpec(masks.shape, lambda b, s: (0, 0, 0)),
                  pl.BlockSpec((1, RWKV_COLS), const),
                  vec, pl.BlockSpec((LANES, D_GRP), const),
                  vec, pl.BlockSpec((LANES, D_GRP), const),
                  pl.BlockSpec((LANES, D_GRP), const),
                  vec, vec, vec, vec, vec],
        out_specs=pl.BlockSpec((BATCH_PER_STEP, CHUNK, D_GRP), lambda b, s: (b, s, 0)),
        out_shape=jax.ShapeDtypeStruct((bsz, seq, D_GRP), BF16),
        scratch_shapes=[pltpu.VMEM((BATCH_PER_STEP, 1, RWKV_COLS), F32),
                        pltpu.VMEM((BATCH_PER_STEP, N_HEADS // HEADS_PER_SCAN, SCAN_W, SCAN_W), F32)],
        compiler_params=pltpu.CompilerParams(
            dimension_semantics=("parallel", "arbitrary"), vmem_limit_bytes=VMEM_LIMIT),
        name="rwkv",
    )(p_r, masks, mu, w0, w2p, a0, a2p, g2, k_k, k_a, r_k, gn_w, gn_b)


def _fox_kernel(q_ref, qb_ref, k_ref, kb_ref, vt_ref, og_ref, ong_ref, o_ref, *, tq):
    hp = pl.program_id(1)
    qi = pl.program_id(2)
    lane = _iota((1, LANES), 1)
    q = q_ref[...]
    qb = qb_ref[...]
    zero = jnp.zeros_like(q)
    qcat = [jnp.concatenate([jnp.where(lane // HEAD_DIM == hh, q, zero),
                             jnp.where(lane // 8 == hp * 2 + hh, qb, zero)], axis=1)
            for hh in range(2)]
    key_pos = _iota((tq, tq), 0)
    qry_pos = _iota((tq, tq), 1)

    def tile(j, carry, masked):
        start = pl.multiple_of(j * tq, tq)
        kcat = jnp.concatenate([k_ref[pl.ds(start, tq), :], kb_ref[pl.ds(start, tq), :]], axis=1)
        vt = vt_ref[:, pl.ds(start, tq)]
        sts = [lax.dot_general(kcat, qc, _NT, preferred_element_type=F32) for qc in qcat]
        if masked:
            sts = [jnp.where(qry_pos >= key_pos, st, -jnp.inf) for st in sts]
        m_new = [jnp.maximum(c[0], jnp.max(st, axis=0, keepdims=True)) for c, st in zip(carry, sts)]
        pts = [jnp.exp(st - m) for st, m in zip(sts, m_new)]
        pvs = [jnp.dot(vt, pt.astype(BF16), preferred_element_type=F32) for pt in pts]
        new = []
        for hh in range(2):
            m, l, acc = carry[hh]
            alpha = jnp.exp(m - m_new[hh])
            l = alpha * l + jnp.sum(pts[hh], axis=0, keepdims=True)
            acc = alpha * acc + pvs[hh][hh * HEAD_DIM:(hh + 1) * HEAD_DIM, :]
            new.append((m_new[hh], l, acc))
        return tuple(new)

    init = (jnp.full((1, tq), -jnp.inf, F32), jnp.zeros((1, tq), F32),
            jnp.zeros((HEAD_DIM, tq), F32))
    carry = lax.fori_loop(0, qi, functools.partial(tile, masked=False), (init, init))
    outs = []
    for _, l, acc in tile(qi, carry, True):
        o = acc / l
        outs.append(o * lax.rsqrt(jnp.mean(o * o, axis=0, keepdims=True) + NORM_EPS))
    o = jnp.concatenate(outs, axis=0).T
    o_ref[...] = (o * ong_ref[...] * _sigmoid(og_ref[...].astype(F32))).astype(BF16)


def _fox(p_x, k_bias, q_bias, o_gain, tq):
    bsz, seq, _ = p_x.shape
    npair = N_HEADS // 2
    v_t = jnp.transpose(p_x[:, :, 2 * D_GRP:3 * D_GRP], (0, 2, 1))
    return pl.pallas_call(
        functools.partial(_fox_kernel, tq=tq),
        grid=(bsz, npair, seq // tq),
        in_specs=[pl.BlockSpec((None, tq, LANES), lambda b, h, i: (b, i, h)),
                  pl.BlockSpec((None, tq, LANES), lambda b, h, i: (b, i, 0)),
                  pl.BlockSpec((None, seq, LANES), lambda b, h, i: (b, 0, npair + h)),
                  pl.BlockSpec((None, seq, LANES), lambda b, h, i: (b, 0, 0)),
                  pl.BlockSpec((None, LANES, seq), lambda b, h, i: (b, h, 0)),
                  pl.BlockSpec((None, tq, LANES), lambda b, h, i: (b, i, 3 * npair + h)),
                  pl.BlockSpec((1, LANES), lambda b, h, i: (0, 0))],
        out_specs=pl.BlockSpec((None, tq, LANES), lambda b, h, i: (b, i, h)),
        out_shape=jax.ShapeDtypeStruct((bsz, seq, D_GRP), BF16),
        compiler_params=pltpu.CompilerParams(
            dimension_semantics=("parallel", "parallel", "arbitrary"),
            vmem_limit_bytes=VMEM_LIMIT),
        name="fox",
    )(p_x, q_bias, p_x, k_bias, v_t, p_x, o_gain)


def _outproj_kernel(x_ref, yr_ref, yf_ref, g1_ref, sh_ref, sc_ref, ng_ref, wor_ref, wof_ref,
                    wrt_ref, brt_ref, x1_ref, h2_ref, idx_ref, gate_ref, rank_ref, cnt_ref,
                    carry_ref):
    @pl.when(pl.program_id(0) == 0)
    def _():
        carry_ref[...] = jnp.zeros_like(carry_ref)

    y = (jnp.dot(yr_ref[...], wor_ref[...], preferred_element_type=F32)
         + jnp.dot(yf_ref[...], wof_ref[...], preferred_element_type=F32))
    x1 = x_ref[...] + g1_ref[...] * y
    x1_ref[...] = x1
    tm = x1.shape[0]
    h = x1 * lax.rsqrt(jnp.mean(x1 * x1, axis=-1, keepdims=True) + NORM_EPS) * ng_ref[...]
    h2 = h * (1.0 + sc_ref[...]) + sh_ref[...]
    h2_ref[...] = h2

    lane = _iota((tm, LANES), 1)
    logits = _fdot(h2, wrt_ref[...]) + brt_ref[...]
    lg = jnp.where(lane < N_EXPERTS, logits, -jnp.inf)
    picks = []
    hot_sum = jnp.zeros((tm, LANES), F32)
    for _ in range(TOP_K):
        m = jnp.max(lg, axis=-1, keepdims=True)
        sel = jnp.min(jnp.where(lg == m, lane, LANES), axis=-1, keepdims=True)
        hot = lane == sel
        picks.append((m, sel, hot))
        hot_sum = hot_sum + hot.astype(F32)
        lg = jnp.where(hot, -jnp.inf, lg)
    es = [jnp.exp(m - picks[0][0]) for m, _, _ in picks]
    den = es[0] + es[1] + es[2] + es[3]

    before = jnp.dot(_tri(tm, True), hot_sum.astype(BF16), preferred_element_type=F32)
    before = before + carry_ref[...]
    idx_out = jnp.zeros((tm, LANES), jnp.int32)
    gate_out = jnp.zeros((tm, LANES), F32)
    rank_out = jnp.zeros((tm, LANES), jnp.int32)
    for kk, (m, sel, hot) in enumerate(picks):
        rk = jnp.sum(jnp.where(hot, before, 0.0), axis=-1, keepdims=True).astype(jnp.int32)
        idx_out = jnp.where(lane == kk, sel, idx_out)
        gate_out = jnp.where(lane == kk, es[kk] / den, gate_out)
        rank_out = jnp.where(lane == kk, rk, rank_out)
    idx_ref[...] = idx_out
    gate_ref[...] = gate_out
    rank_ref[...] = rank_out
    carry_ref[...] = carry_ref[...] + jnp.sum(hot_sum, axis=0, keepdims=True)
    cnt_ref[...] = carry_ref[...]


def _outproj(x2d, y_r, y_f, gate1, shift2, scale2, norm_g, wo_r, wo_f, w_rt, b_rt, tm, seq):
    t = x2d.shape[0]
    per_b = seq // tm
    const = lambda i: (0, 0)
    rows = lambda i: (i, 0)
    mod = pl.BlockSpec((None, 1, D_MODEL), lambda i: (i // per_b, 0, 0))
    return pl.pallas_call(
        _outproj_kernel,
        grid=(t // tm,),
        in_specs=[pl.BlockSpec((tm, D_MODEL), rows),
                  pl.BlockSpec((tm, D_GRP), rows),
                  pl.BlockSpec((tm, D_GRP), rows),
                  mod, mod, mod,
                  pl.BlockSpec((1, D_MODEL), const),
                  pl.BlockSpec((D_GRP, D_MODEL), const),
                  pl.BlockSpec((D_GRP, D_MODEL), const),
                  pl.BlockSpec((D_MODEL, LANES), const),
                  pl.BlockSpec((1, LANES), const)],
        out_specs=[pl.BlockSpec((tm, D_MODEL), rows),
                   pl.BlockSpec((tm, D_MODEL), rows),
                   pl.BlockSpec((tm, LANES), rows),
                   pl.BlockSpec((tm, LANES), rows),
                   pl.BlockSpec((tm, LANES), rows),
                   pl.BlockSpec((1, LANES), const)],
        out_shape=[jax.ShapeDtypeStruct((t, D_MODEL), F32),
                   jax.ShapeDtypeStruct((t, D_MODEL), F32),
                   jax.ShapeDtypeStruct((t, LANES), jnp.int32),
                   jax.ShapeDtypeStruct((t, LANES), F32),
                   jax.ShapeDtypeStruct((t, LANES), jnp.int32),
                   jax.ShapeDtypeStruct((1, LANES), F32)],
        scratch_shapes=[pltpu.VMEM((1, LANES), F32)],
        compiler_params=pltpu.CompilerParams(
            dimension_semantics=("arbitrary",), vmem_limit_bytes=VMEM_LIMIT),
        name="outproj",
    )(x2d, y_r, y_f, gate1, shift2, scale2, norm_g, wo_r, wo_f, w_rt, b_rt)


SC_CORES = 2
SC_SUBCORES = 16
SC_ROWS = 32


def _sc_gather_rows(idx, src):
    n_workers = SC_CORES * SC_SUBCORES
    m = idx.shape[0]
    d = src.shape[1]
    assert m % (n_workers * SC_ROWS) == 0
    n_chunks = m // (n_workers * SC_ROWS)
    mesh = plsc.VectorSubcoreMesh(core_axis_name="c", subcore_axis_name="s")

    @functools.partial(
        pl.kernel, mesh=mesh,
        out_type=jax.ShapeDtypeStruct((m, d), src.dtype),
        scratch_types=[pltpu.VMEM((n_chunks, SC_ROWS), jnp.int32),
                       pltpu.VMEM((SC_ROWS, d), src.dtype),
                       pltpu.SemaphoreType.DMA],
        name="sc_gather")
    def gather(src_hbm, idx_hbm, out_hbm, idx_v, rows_v, sem):
        wid = lax.axis_index("s") * SC_CORES + lax.axis_index("c")
        pltpu.sync_copy(idx_hbm.at[wid], idx_v)

        @pl.loop(0, n_chunks)
        def _(j):
            pltpu.async_copy(src_hbm.at[idx_v.at[j]], rows_v, sem).wait()
            pltpu.sync_copy(rows_v, out_hbm.at[pl.ds((wid * n_chunks + j) * SC_ROWS, SC_ROWS)])

    return gather(src, idx.reshape(n_workers, n_chunks, SC_ROWS))


def _sc_scatter_rows(src, dest, n_out):
    n_workers = SC_CORES * SC_SUBCORES
    t, d = src.shape
    n_slot = dest.shape[1]
    assert t % (n_workers * SC_ROWS) == 0
    n_chunks = t // (n_workers * SC_ROWS)
    mesh = plsc.VectorSubcoreMesh(core_axis_name="c", subcore_axis_name="s")
    idx = dest.reshape(n_workers, n_chunks, SC_ROWS, n_slot).transpose(0, 1, 3, 2)
    idx = idx.reshape(n_workers, n_chunks * n_slot, SC_ROWS)

    @functools.partial(
        pl.kernel, mesh=mesh,
        out_type=jax.ShapeDtypeStruct((n_out, d), src.dtype),
        scratch_types=[pltpu.VMEM((n_chunks * n_slot, SC_ROWS), jnp.int32),
                       pltpu.VMEM((SC_ROWS, d), src.dtype)],
        name="sc_scatter")
    def scatter(src_hbm, idx_hbm, out_hbm, idx_v, rows_v):
        wid = lax.axis_index("s") * SC_CORES + lax.axis_index("c")
        pltpu.sync_copy(idx_hbm.at[wid], idx_v)

        @pl.loop(0, n_chunks)
        def _(j):
            pltpu.sync_copy(src_hbm.at[pl.ds((wid * n_chunks + j) * SC_ROWS, SC_ROWS)], rows_v)
            for k in range(n_slot):
                pltpu.sync_copy(rows_v, out_hbm.at[idx_v.at[j * n_slot + k]])

    return scatter(src, idx)


def _expert_kernel(be_ref, nv_ref, x_ref, wgu_ref, bgu_ref, wd_ref, bd_ref, o_ref):
    del be_ref
    valid = _iota((EXPERT_BLOCK, 1), 0) < nv_ref[pl.program_id(0)]
    x = jnp.where(valid, x_ref[...], 0.0)
    gu = jnp.dot(x.astype(BF16), wgu_ref[...], preferred_element_type=F32) + bgu_ref[...]
    gate = jnp.minimum(gu[:, :D_MODEL], SWIGLU_LIMIT)
    up = jnp.clip(gu[:, D_MODEL:], -SWIGLU_LIMIT, SWIGLU_LIMIT)
    act = gate * _sigmoid(SWIGLU_ALPHA * gate) * (up + 1.0)
    o_ref[...] = jnp.dot(act.astype(BF16), wd_ref[...], preferred_element_type=F32) + bd_ref[...]


def _experts(block_e, n_valid, xs, w_gu, b_gu, w_d, b_d):
    n_blocks = block_e.shape[0]
    grid_spec = pltpu.PrefetchScalarGridSpec(
        num_scalar_prefetch=2,
        grid=(n_blocks,),
        in_specs=[pl.BlockSpec((EXPERT_BLOCK, D_MODEL), lambda j, be, nv: (j, 0)),
                  pl.BlockSpec((None, D_MODEL, 2 * D_MODEL), lambda j, be, nv: (be[j], 0, 0)),
                  pl.BlockSpec((None, 1, 2 * D_MODEL), lambda j, be, nv: (be[j], 0, 0)),
                  pl.BlockSpec((None, D_MODEL, D_MODEL), lambda j, be, nv: (be[j], 0, 0)),
                  pl.BlockSpec((None, 1, D_MODEL), lambda j, be, nv: (be[j], 0, 0))],
        out_specs=pl.BlockSpec((EXPERT_BLOCK, D_MODEL), lambda j, be, nv: (j, 0)),
    )
    return pl.pallas_call(
        _expert_kernel,
        grid_spec=grid_spec,
        out_shape=jax.ShapeDtypeStruct(xs.shape, F32),
        compiler_params=pltpu.CompilerParams(
            dimension_semantics=("arbitrary",), vmem_limit_bytes=VMEM_LIMIT),
        name="experts",
    )(block_e, n_valid, xs, w_gu, b_gu, w_d, b_d)


COMBINE_TOKENS = 256


def _combine_kernel(yg_ref, x1_ref, gate_ref, g2_ref, fg_ref, o_ref):
    gates = gate_ref[...]
    acc = None
    for kk in range(TOP_K):
        part = gates[:, kk:kk + 1] * yg_ref[kk * COMBINE_TOKENS:(kk + 1) * COMBINE_TOKENS, :]
        acc = part if acc is None else acc + part
    x2 = x1_ref[...] + g2_ref[...] * acc
    o_ref[...] = x2 * lax.rsqrt(jnp.mean(x2 * x2, axis=-1, keepdims=True) + NORM_EPS) * fg_ref[...]


def _combine(yg, x1, gates, gate2, final_g, seq):
    t = x1.shape[0]
    tm = COMBINE_TOKENS
    per_b = seq // tm
    rows = lambda i: (i, 0)
    return pl.pallas_call(
        _combine_kernel,
        grid=(t // tm,),
        in_specs=[pl.BlockSpec((TOP_K * tm, D_MODEL), rows),
                  pl.BlockSpec((tm, D_MODEL), rows),
                  pl.BlockSpec((tm, LANES), rows),
                  pl.BlockSpec((None, 1, D_MODEL), lambda i: (i // per_b, 0, 0)),
                  pl.BlockSpec((1, D_MODEL), lambda i: (0, 0))],
        out_specs=pl.BlockSpec((tm, D_MODEL), rows),
        out_shape=jax.ShapeDtypeStruct((t, D_MODEL), F32),
        compiler_params=pltpu.CompilerParams(
            dimension_semantics=("parallel",), vmem_limit_bytes=VMEM_LIMIT),
        name="combine",
    )(yg, x1, gates, gate2, final_g)


def _moe(h2, idx, gates, rank, counts, x1, gate2, final_g, w_gu, b_gu, w_d, b_d, seq):
    t = h2.shape[0]
    n_slots = t * TOP_K
    n_blocks = -(-n_slots // EXPERT_BLOCK) + N_EXPERTS
    cap = n_blocks * EXPERT_BLOCK
    padded = (counts + EXPERT_BLOCK - 1) // EXPERT_BLOCK * EXPERT_BLOCK
    pad_ends = jnp.cumsum(padded)
    pad_starts = pad_ends - padded
    dest = pad_starts[idx] + rank
    block_starts = jnp.arange(n_blocks, dtype=jnp.int32) * EXPERT_BLOCK
    block_e = jnp.minimum(jnp.sum(block_starts[:, None] >= pad_ends[None, :], axis=1),
                          N_EXPERTS - 1).astype(jnp.int32)
    n_valid = jnp.clip(counts[block_e] - (block_starts - pad_starts[block_e]), 0, EXPERT_BLOCK)

    xs = _sc_scatter_rows(h2, dest, cap)
    yb = _experts(block_e, n_valid.astype(jnp.int32), xs, w_gu, b_gu, w_d, b_d)
    dest_blocks = dest.reshape(-1, COMBINE_TOKENS, TOP_K).transpose(0, 2, 1).reshape(-1)
    yg = _sc_gather_rows(dest_blocks, yb)
    return _combine(yg, x1, gates, gate2, final_g, seq)


def _layer(x, c_mod, norm1_g, w_in, mu_shift, w0, w2, a0, a2, g2, k_k, k_a, r_k, gn_w, gn_b, b_f,
           q_norm_g, k_norm_g, o_norm_g, w_out, norm2_g, w_router, b_router, w_gate_up,
           b_gate_up, w_down, b_down, final_g, tm_in, tq, tm_out):
    bsz, seq, _ = x.shape
    shift1, scale1, gate1, shift2, scale2, gate2 = (
        m.reshape(bsz, 1, D_MODEL) for m in jnp.split(c_mod, 6, axis=-1))
    row = lambda v: v.reshape(1, -1)

    w_r = w_in[:, :RWKV_COLS].astype(BF16)
    w_x = w_in[:, RWKV_COLS:RWKV_COLS + FOX_MAIN].astype(BF16)
    w_f = jnp.pad(w_in[:, RWKV_COLS + FOX_MAIN:], ((0, 0), (0, LANES - N_HEADS)))
    b_fp = jnp.pad(b_f, (0, LANES - N_HEADS)).reshape(1, LANES)
    qk_gain = jnp.concatenate([jnp.tile(q_norm_g, N_HEADS) * HEAD_DIM ** -0.5,
                               jnp.tile(k_norm_g, N_HEADS)]).reshape(1, -1)
    p_r, p_x, k_bias, q_bias = _inproj(x, shift1, scale1, row(norm1_g), w_r, w_x, w_f, b_fp,
                                       qk_gain, tm_in)

    zeros = jnp.zeros((LANES - 64, D_GRP), F32)
    w2p = jnp.concatenate([w2, zeros], axis=0).astype(BF16)
    a2p = jnp.concatenate([zeros, a2], axis=0).astype(BF16)
    y_r = _rwkv(p_r, row(mu_shift), row(w0), w2p, row(a0), a2p, g2.astype(BF16), row(k_k),
                row(k_a), row(r_k), row(gn_w), row(gn_b))

    y_f = _fox(p_x, k_bias, q_bias, jnp.tile(o_norm_g, 2).reshape(1, LANES), tq)

    t = bsz * seq
    w_rt = jnp.pad(w_router, ((0, 0), (0, LANES - N_EXPERTS)))
    b_rt = jnp.pad(b_router, (0, LANES - N_EXPERTS)).reshape(1, LANES)
    wo = w_out.astype(BF16)
    x1, h2, idx, gates, rank, cnt = _outproj(
        x.reshape(t, D_MODEL), y_r.reshape(t, D_GRP), y_f.reshape(t, D_GRP), gate1, shift2,
        scale2, row(norm2_g), wo[:D_GRP], wo[D_GRP:], w_rt, b_rt, tm_out, seq)

    counts = cnt[0, :N_EXPERTS].astype(jnp.int32)
    out = _moe(h2, idx[:, :TOP_K], gates, rank[:, :TOP_K], counts, x1, gate2, row(final_g),
               w_gate_up.astype(BF16), b_gate_up.reshape(N_EXPERTS, 1, -1),
               w_down.astype(BF16), b_down.reshape(N_EXPERTS, 1, -1), seq)
    return out.reshape(bsz, seq, D_MODEL)


def kernel(x, c, w_ada, b_ada, norm1_g, w_in, mu_shift, w0, w2, a0, a2, g2, k_k, k_a, r_k, gn_w,
           gn_b, b_f, q_norm_g, k_norm_g, o_norm_g, w_out, norm2_g, w_router, b_router, w_gate_up,
           b_gate_up, w_down, b_down, final_g):
    assert w_ada.shape[0] == 1, "single-layer block"
    c_mod = _adaln(c, w_ada[0], b_ada[0])
    return _layer(x, c_mod, norm1_g[0], w_in[0], mu_shift[0], w0[0], w2[0], a0[0], a2[0], g2[0],
                  k_k[0], k_a[0], r_k[0], gn_w[0], gn_b[0], b_f[0], q_norm_g[0], k_norm_g[0],
                  o_norm_g[0], w_out[0], norm2_g[0], w_router[0], b_router[0], w_gate_up[0],
                  b_gate_up[0], w_down[0], b_down[0], final_g,
                  tm_in=min(512, x.shape[1]), tq=min(512, x.shape[1]), tm_out=min(512, x.shape[1]))
```

```python
import functools

import jax
import jax.numpy as jnp
from jax import lax
from jax.experimental import pallas as pl
from jax.experimental.pallas import tpu as pltpu
from jax.experimental.pallas import tpu_sc as plsc

F32 = jnp.float32
BF16 = jnp.bfloat16
HIGHEST = lax.Precision.HIGHEST

D_MODEL = 1024
HEAD_DIM = 64
N_HEADS = 8
D_GRP = N_HEADS * HEAD_DIM
RWKV_COLS = 1792
LORA_OFF = 3 * D_GRP
GATE_OFF = LORA_OFF + 128
FOX_MAIN = 4 * D_GRP
N_EXPERTS = 32
TOP_K = 4
EXPERT_BLOCK = 256
SWIGLU_ALPHA = 1.702
SWIGLU_LIMIT = 7.0
NORM_EPS = 1e-6
GN_EPS = 64e-5
LANES = 128
CHUNK = 64
HEADS_PER_SCAN = 4
SCAN_W = HEADS_PER_SCAN * HEAD_DIM
VMEM_LIMIT = 56 * 1024 * 1024


def _dot(a, b):
    return jnp.dot(a.astype(BF16), b.astype(BF16), preferred_element_type=F32)


def _dot_nt(a, b):
    return lax.dot_general(a.astype(BF16), b.astype(BF16), (((1,), (1,)), ((), ())),
                           preferred_element_type=F32)


def _dot_tn(a, b):
    return lax.dot_general(a.astype(BF16), b.astype(BF16), (((0,), (0,)), ((), ())),
                           preferred_element_type=F32)


def _fdot(a, b):
    return jnp.dot(a, b, precision=HIGHEST, preferred_element_type=F32)


def _split_dot(x, m, terms=2, left=False):
    acc = None
    rem = x
    for _ in range(terms):
        part = rem.astype(BF16)
        rem = rem - part.astype(F32)
        d = (jnp.dot(m, part, preferred_element_type=F32) if left
             else jnp.dot(part, m, preferred_element_type=F32))
        acc = d if acc is None else acc + d
    return acc


def _iota(shape, dim):
    return lax.broadcasted_iota(jnp.int32, shape, dim)


def _seg_reduce_mat(n):
    return (_iota((n, LANES), 0) // HEAD_DIM == _iota((n, LANES), 1)).astype(BF16)


def _seg_expand_mat(n):
    return (_iota((LANES, n), 1) // HEAD_DIM == _iota((LANES, n), 0)).astype(BF16)


def _tri(n, strict):
    r, c = _iota((n, n), 0), _iota((n, n), 1)
    return ((r > c) if strict else (r >= c)).astype(BF16)


def _log_sigmoid(z):
    return jnp.minimum(z, 0.0) - jnp.log(1.0 + jnp.exp(-jnp.abs(z)))


def _sigmoid(z):
    return 1.0 / (1.0 + jnp.exp(-z))


def _adaln_kernel(c_ref, w_ref, b_ref, o_ref):
    c = c_ref[...]
    o_ref[...] = _fdot(c * _sigmoid(c), w_ref[...]) + b_ref[...]


def _adaln(c, w_ada, b_ada):
    bsz = c.shape[0]
    n_mod = w_ada.shape[1] // D_MODEL
    return pl.pallas_call(
        _adaln_kernel,
        grid=(n_mod,),
        in_specs=[pl.BlockSpec((bsz, D_MODEL), lambda j: (0, 0)),
                  pl.BlockSpec((D_MODEL, D_MODEL), lambda j: (0, j)),
                  pl.BlockSpec((1, D_MODEL), lambda j: (0, j))],
        out_specs=pl.BlockSpec((bsz, D_MODEL), lambda j: (0, j)),
        out_shape=jax.ShapeDtypeStruct((bsz, n_mod * D_MODEL), F32),
        name="adaln",
    )(c, w_ada, b_ada.reshape(1, -1))


def _inproj_kernel(x_ref, sh_ref, sc_ref, g_ref, wr_ref, wx_ref, wfh_ref, wfl_ref, bf_ref, qkg_ref,
                   pr_ref, px_ref, kb_ref, qb_ref, carry_ref):
    @pl.when(pl.program_id(1) == 0)
    def _():
        carry_ref[...] = jnp.zeros_like(carry_ref)

    x = x_ref[...]
    tm = x.shape[0]
    h = x * lax.rsqrt(jnp.mean(x * x, axis=-1, keepdims=True) + NORM_EPS) * g_ref[...]
    h = h * (1.0 + sc_ref[...]) + sh_ref[...]
    hb = h.astype(BF16)
    h_lo = (h - hb.astype(F32)).astype(BF16)

    pr_ref[...] = jnp.dot(hb, wr_ref[...], preferred_element_type=F32).astype(BF16)

    px = jnp.dot(hb, wx_ref[...], preferred_element_type=F32)
    qk = px[:, :2 * D_GRP]
    ss = _split_dot(qk * qk, _seg_reduce_mat(2 * D_GRP))
    inv = lax.rsqrt(ss * (1.0 / HEAD_DIM) + NORM_EPS)
    qk = qk * _split_dot(inv, _seg_expand_mat(2 * D_GRP)) * qkg_ref[...]
    px_ref[:, :2 * D_GRP] = qk.astype(BF16)
    px_ref[:, 2 * D_GRP:] = px[:, 2 * D_GRP:].astype(BF16)

    z = (jnp.dot(hb, wfh_ref[...], preferred_element_type=F32)
         + jnp.dot(h_lo, wfh_ref[...], preferred_element_type=F32)
         + jnp.dot(hb, wfl_ref[...], preferred_element_type=F32)) + bf_ref[...]
    cum = _split_dot(_log_sigmoid(z), _tri(tm, False), terms=3, left=True) + carry_ref[...]
    carry_ref[...] = cum[tm - 1:tm, :]

    src, dst = _iota((LANES, LANES), 0), _iota((LANES, LANES), 1)
    parts = []
    rem = cum
    for _ in range(3):
        part = rem.astype(BF16)
        rem = rem - part.astype(F32)
        parts.append(part)

    def spread(offset):
        return sum(jnp.dot(part, ((dst == 8 * src + offset + t) & (src < N_HEADS)).astype(BF16),
                           preferred_element_type=F32) for t, part in enumerate(parts))

    slot = _iota((1, LANES), 1) % 8
    kb_ref[...] = (jnp.where((slot >= 3) & (slot < 6), 1.0, 0.0) - spread(0)).astype(BF16)
    qb_ref[...] = (jnp.where(slot < 3, 1.0, 0.0) + spread(3)).astype(BF16)


def _inproj(x, shift, scale, g, w_r, w_x, w_f, b_f, qk_gain, tm):
    w_f_hi = w_f.astype(BF16)
    w_f_lo = (w_f - w_f_hi.astype(F32)).astype(BF16)
    bsz, seq, _ = x.shape
    const = lambda b, s: (0, 0)
    return pl.pallas_call(
        _inproj_kernel,
        grid=(bsz, seq // tm),
        in_specs=[pl.BlockSpec((None, tm, D_MODEL), lambda b, s: (b, s, 0)),
                  pl.BlockSpec((None, 1, D_MODEL), lambda b, s: (b, 0, 0)),
                  pl.BlockSpec((None, 1, D_MODEL), lambda b, s: (b, 0, 0)),
                  pl.BlockSpec((1, D_MODEL), const),
                  pl.BlockSpec((D_MODEL, RWKV_COLS), const),
                  pl.BlockSpec((D_MODEL, FOX_MAIN), const),
                  pl.BlockSpec((D_MODEL, LANES), const),
                  pl.BlockSpec((D_MODEL, LANES), const),
                  pl.BlockSpec((1, LANES), const),
                  pl.BlockSpec((1, 2 * D_GRP), const)],
        out_specs=[pl.BlockSpec((None, tm, RWKV_COLS), lambda b, s: (b, s, 0)),
                   pl.BlockSpec((None, tm, FOX_MAIN), lambda b, s: (b, s, 0)),
                   pl.BlockSpec((None, tm, LANES), lambda b, s: (b, s, 0)),
                   pl.BlockSpec((None, tm, LANES), lambda b, s: (b, s, 0))],
        out_shape=[jax.ShapeDtypeStruct((bsz, seq, RWKV_COLS), BF16),
                   jax.ShapeDtypeStruct((bsz, seq, FOX_MAIN), BF16),
                   jax.ShapeDtypeStruct((bsz, seq, LANES), BF16),
                   jax.ShapeDtypeStruct((bsz, seq, LANES), BF16)],
        scratch_shapes=[pltpu.VMEM((1, LANES), F32)],
        compiler_params=pltpu.CompilerParams(
            dimension_semantics=("parallel", "arbitrary"), vmem_limit_bytes=VMEM_LIMIT),
        name="inproj",
    )(x, shift, scale, g, w_r, w_x, w_f_hi, w_f_lo, b_f, qk_gain)


_NN = (((1,), (0,)), ((), ()))
_NT = (((1,), (1,)), ((), ()))
_TN = (((0,), (0,)), ((), ()))
SCAN_N = HEADS_PER_SCAN * CHUNK
BATCH_PER_STEP = 4
INV_LEVELS = 5
M_HEAD, M_STRICT, M_INCL, M_EYE, M_BASE, M_OFF = 0, 1, 2, 3, 4, 5


def _bdot(a, b, dims):
    return lax.dot_general(a, b, dims, preferred_element_type=F32)


def _scan_masks():
    rr, cc = _iota((SCAN_N, SCAN_W), 0), _iota((SCAN_N, SCAN_W), 1)
    ri, ci = _iota((SCAN_N, SCAN_N), 0), _iota((SCAN_N, SCAN_N), 1)
    same = ri // CHUNK == ci // CHUNK
    masks = [rr // CHUNK == cc // HEAD_DIM, same & (ri > ci), same & (ri >= ci), ri == ci,
             (ri // 2 == ci // 2) & (ri > ci)]
    blk = 2
    while blk < CHUNK:
        masks.append((ri // (2 * blk) == ci // (2 * blk)) & (ri // blk != ci // blk) & (ri > ci))
        blk *= 2
    return jnp.stack(masks).astype(BF16)


def _rwkv_kernel(p_ref, masks_ref, mu_ref, w0_ref, w2_ref, a0_ref, a2_ref, g2_ref, kk_ref, ka_ref,
                 rk_ref, gnw_ref, gnb_ref, o_ref, last_ref, state_ref):
    @pl.when(pl.program_id(1) == 0)
    def _():
        last_ref[...] = jnp.zeros_like(last_ref)
        state_ref[...] = jnp.zeros_like(state_ref)

    mu, w0, w2, a0, a2, g2, k_k, k_a, r_k, gn_w, gn_b = (
        ref[...] for ref in (mu_ref, w0_ref, w2_ref, a0_ref, a2_ref, g2_ref, kk_ref, ka_ref,
                             rk_ref, gnw_ref, gnb_ref))
    rows = BATCH_PER_STEP * CHUNK
    p = p_ref[...].astype(F32).reshape(rows, RWKV_COLS)
    row_id = _iota((rows, 1), 0)
    prev = pltpu.roll(p, 1, axis=0)
    for bb in range(BATCH_PER_STEP):
        prev = jnp.where(row_id == bb * CHUNK, last_ref[bb], prev)
        last_ref[bb] = p[(bb + 1) * CHUNK - 1:(bb + 1) * CHUNK, :]
    pf = p + mu * (prev - p)
    r = pf[:, 0:D_GRP]
    k = pf[:, D_GRP:2 * D_GRP]
    v = pf[:, 2 * D_GRP:3 * D_GRP]
    lora = pf[:, LORA_OFF:GATE_OFF]
    gd = pf[:, GATE_OFF:RWKV_COLS]

    wlog = w0 + _dot(jnp.tanh(lora), w2)
    neg = -wlog
    softplus = jnp.maximum(neg, 0.0) + jnp.log(1.0 + jnp.exp(-jnp.abs(neg)))
    ld = -jnp.exp(-softplus - 0.5)
    a = _sigmoid(a0 + _dot(lora, a2))
    g = _dot(_sigmoid(gd), g2)

    red, exp_m = _seg_reduce_mat(D_GRP), _seg_expand_mat(D_GRP)
    kk = k * k_k
    n2 = _split_dot(kk * kk, red)
    kk = kk * _split_dot(1.0 / jnp.maximum(jnp.sqrt(n2), 1e-12), exp_m)
    k2 = k * (1.0 + (a - 1.0) * k_a)

    tr, tc = _iota((rows, rows), 0), _iota((rows, rows), 1)
    tri = ((tr >= tc) & (tr // CHUNK == tc // CHUNK)).astype(BF16)
    cl = _split_dot(ld, tri, terms=3, left=True)
    cl_end = jnp.concatenate(
        [jnp.broadcast_to(cl[(bb + 1) * CHUNK - 1:(bb + 1) * CHUNK, :], (CHUNK, D_GRP))
         for bb in range(BATCH_PER_STEP)], axis=0)
    e_in = jnp.exp(cl)
    e_out = jnp.exp(-cl)
    e_rem = jnp.exp(cl_end - cl)
    p_end = jnp.exp(cl_end)
    kka = kk * a
    ops = [(-kk * jnp.exp(cl - ld)).astype(BF16), (kka * e_out).astype(BF16),
           (k2 * e_out).astype(BF16), (r * e_in).astype(BF16), v.astype(BF16),
           (kka * e_rem).astype(BF16), (k2 * e_rem).astype(BF16)]

    chains = [(bb, grp) for bb in range(BATCH_PER_STEP)
              for grp in range(N_HEADS // HEADS_PER_SCAN)]
    head_mask = masks_ref[M_HEAD]
    strict, incl = masks_ref[M_STRICT], masks_ref[M_INCL]

    def stacked(op, bb, grp):
        part = op[bb * CHUNK:(bb + 1) * CHUNK, grp * SCAN_W:(grp + 1) * SCAN_W]
        return jnp.concatenate([part] * HEADS_PER_SCAN, axis=0) * head_mask

    xs = [[stacked(op, bb, grp) for op in ops] for bb, grp in chains]
    st = [state_ref[bb, grp] for bb, grp in chains]
    sb = [s.astype(BF16) for s in st]
    nab = [_bdot(x[0], x[1], _NT).astype(BF16) for x in xs]
    aak = [_bdot(x[0], x[2], _NT).astype(BF16) * strict for x in xs]
    arb = [_bdot(x[3], x[1], _NT).astype(BF16) * incl for x in xs]
    ark = [_bdot(x[3], x[2], _NT).astype(BF16) * incl for x in xs]
    t_inv = [masks_ref[M_EYE] + n * masks_ref[M_BASE] for n in nab]
    for lvl in range(INV_LEVELS):
        half = [_bdot(t, n * masks_ref[M_OFF + lvl], _NN).astype(BF16) for t, n in zip(t_inv, nab)]
        t_inv = [t + _bdot(h, t, _NN).astype(BF16) for t, h in zip(t_inv, half)]
    rhs = [(_bdot(x[0], s, _NT) + _bdot(k, x[4], _NN)).astype(BF16)
           for x, s, k in zip(xs, sb, aak)]
    sa = [_bdot(t, h, _NN).astype(BF16) for t, h in zip(t_inv, rhs)]
    ys = [_bdot(x[3], s, _NT) + _bdot(b, u, _NN) + _bdot(k, x[4], _NN)
          for x, s, b, u, k in zip(xs, sb, arb, sa, ark)]
    for (bb, grp), x, s, u in zip(chains, xs, st, sa):
        decay = p_end[bb * CHUNK:bb * CHUNK + 1, grp * SCAN_W:(grp + 1) * SCAN_W]
        state_ref[bb, grp] = s * decay + _bdot(u, x[5], _TN) + _bdot(x[4], x[6], _TN)
    ys = [y[0:CHUNK] + y[CHUNK:2 * CHUNK] + y[2 * CHUNK:3 * CHUNK] + y[3 * CHUNK:4 * CHUNK]
          for y in ys]
    n_grp = N_HEADS // HEADS_PER_SCAN
    y = jnp.concatenate([jnp.concatenate(ys[bb * n_grp:(bb + 1) * n_grp], axis=1)
                         for bb in range(BATCH_PER_STEP)], axis=0)

    mean = _split_dot(_split_dot(y, red) * (1.0 / HEAD_DIM), exp_m)
    d = y - mean
    var = _split_dot(d * d, red) * (1.0 / HEAD_DIM)
    yn = d * _split_dot(lax.rsqrt(var + GN_EPS), exp_m) * gn_w + gn_b
    bonus = _split_dot(_split_dot(r * k2 * r_k, red), exp_m) * v
    o_ref[...] = ((yn + bonus) * g).astype(BF16).reshape(BATCH_PER_STEP, CHUNK, D_GRP)


def _rwkv(p_r, mu, w0, w2p, a0, a2p, g2, k_k, k_a, r_k, gn_w, gn_b):
    bsz, seq, _ = p_r.shape
    assert bsz % BATCH_PER_STEP == 0
    masks = _scan_masks()
    const = lambda b, s: (0, 0)
    vec = pl.BlockSpec((1, D_GRP), const)
    return pl.pallas_call(
        _rwkv_kernel,
        grid=(bsz // BATCH_PER_STEP, seq // CHUNK),
        in_specs=[pl.BlockSpec((BATCH_PER_STEP, CHUNK, RWKV_COLS), lambda b, s: (b, s, 0)),
                  pl.BlockSpec(masks.shape, lambda b, s: (0, 0, 0)),
                  pl.BlockSpec((1, RWKV_COLS), const),
                  vec, pl.BlockSpec((LANES, D_GRP), const),
                  vec, pl.BlockSpec((LANES, D_GRP), const),
                  pl.BlockSpec((LANES, D_GRP), const),
                  vec, vec, vec, vec, vec],
        out_specs=pl.BlockSpec((BATCH_PER_STEP, CHUNK, D_GRP), lambda b, s: (b, s, 0)),
        out_shape=jax.ShapeDtypeStruct((bsz, seq, D_GRP), BF16),
        scratch_shapes=[pltpu.VMEM((BATCH_PER_STEP, 1, RWKV_COLS), F32),
                        pltpu.VMEM((BATCH_PER_STEP, N_HEADS // HEADS_PER_SCAN, SCAN_W, SCAN_W), F32)],
        compiler_params=pltpu.CompilerParams(
            dimension_semantics=("parallel", "arbitrary"), vmem_limit_bytes=VMEM_LIMIT),
        name="rwkv",
    )(p_r, masks, mu, w0, w2p, a0, a2p, g2, k_k, k_a, r_k, gn_w, gn_b)


def _fox_kernel(q_ref, qb_ref, k_ref, kb_ref, vt_ref, og_ref, ong_ref, o_ref, *, tq):
    hp = pl.program_id(1)
    qi = pl.program_id(2)
    lane = _iota((1, LANES), 1)
    q = q_ref[...]
    qb = qb_ref[...]
    zero = jnp.zeros_like(q)
    qcat = [jnp.concatenate([jnp.where(lane // HEAD_DIM == hh, q, zero),
                             jnp.where(lane // 8 == hp * 2 + hh, qb, zero)], axis=1)
            for hh in range(2)]
    key_pos = _iota((tq, tq), 0)
    qry_pos = _iota((tq, tq), 1)

    def tile(j, carry, masked):
        start = pl.multiple_of(j * tq, tq)
        kcat = jnp.concatenate([k_ref[pl.ds(start, tq), :], kb_ref[pl.ds(start, tq), :]], axis=1)
        vt = vt_ref[:, pl.ds(start, tq)]
        sts = [lax.dot_general(kcat, qc, _NT, preferred_element_type=F32) for qc in qcat]
        if masked:
            sts = [jnp.where(qry_pos >= key_pos, st, -jnp.inf) for st in sts]
        m_new = [jnp.maximum(c[0], jnp.max(st, axis=0, keepdims=True)) for c, st in zip(carry, sts)]
        pts = [jnp.exp(st - m) for st, m in zip(sts, m_new)]
        pvs = [jnp.dot(vt, pt.astype(BF16), preferred_element_type=F32) for pt in pts]
        new = []
        for hh in range(2):
            m, l, acc = carry[hh]
            alpha = jnp.exp(m - m_new[hh])
            l = alpha * l + jnp.sum(pts[hh], axis=0, keepdims=True)
            acc = alpha * acc + pvs[hh][hh * HEAD_DIM:(hh + 1) * HEAD_DIM, :]
            new.append((m_new[hh], l, acc))
        return tuple(new)

    init = (jnp.full((1, tq), -jnp.inf, F32), jnp.zeros((1, tq), F32),
            jnp.zeros((HEAD_DIM, tq), F32))
    carry = lax.fori_loop(0, qi, functools.partial(tile, masked=False), (init, init))
    outs = []
    for _, l, acc in tile(qi, carry, True):
        o = acc / l
        outs.append(o * lax.rsqrt(jnp.mean(o * o, axis=0, keepdims=True) + NORM_EPS))
    o = jnp.concatenate(outs, axis=0).T
    o_ref[...] = (o * ong_ref[...] * _sigmoid(og_ref[...].astype(F32))).astype(BF16)


def _fox(p_x, k_bias, q_bias, o_gain, tq):
    bsz, seq, _ = p_x.shape
    npair = N_HEADS // 2
    v_t = jnp.transpose(p_x[:, :, 2 * D_GRP:3 * D_GRP], (0, 2, 1))
    return pl.pallas_call(
        functools.partial(_fox_kernel, tq=tq),
        grid=(bsz, npair, seq // tq),
        in_specs=[pl.BlockSpec((None, tq, LANES), lambda b, h, i: (b, i, h)),
                  pl.BlockSpec((None, tq, LANES), lambda b, h, i: (b, i, 0)),
                  pl.BlockSpec((None, seq, LANES), lambda b, h, i: (b, 0, npair + h)),
                  pl.BlockSpec((None, seq, LANES), lambda b, h, i: (b, 0, 0)),
                  pl.BlockSpec((None, LANES, seq), lambda b, h, i: (b, h, 0)),
                  pl.BlockSpec((None, tq, LANES), lambda b, h, i: (b, i, 3 * npair + h)),
                  pl.BlockSpec((1, LANES), lambda b, h, i: (0, 0))],
        out_specs=pl.BlockSpec((None, tq, LANES), lambda b, h, i: (b, i, h)),
        out_shape=jax.ShapeDtypeStruct((bsz, seq, D_GRP), BF16),
        compiler_params=pltpu.CompilerParams(
            dimension_semantics=("parallel", "parallel", "arbitrary"),
            vmem_limit_bytes=VMEM_LIMIT),
        name="fox",
    )(p_x, q_bias, p_x, k_bias, v_t, p_x, o_gain)


def _outproj_kernel(x_ref, yr_ref, yf_ref, g1_ref, sh_ref, sc_ref, ng_ref, wor_ref, wof_ref,
                    wrt_ref, brt_ref, x1_ref, h2_ref, idx_ref, gate_ref, rank_ref, cnt_ref,
                    carry_ref):
    @pl.when(pl.program_id(0) == 0)
    def _():
        carry_ref[...] = jnp.zeros_like(carry_ref)

    y = (jnp.dot(yr_ref[...], wor_ref[...], preferred_element_type=F32)
         + jnp.dot(yf_ref[...], wof_ref[...], preferred_element_type=F32))
    x1 = x_ref[...] + g1_ref[...] * y
    x1_ref[...] = x1
    tm = x1.shape[0]
    h = x1 * lax.rsqrt(jnp.mean(x1 * x1, axis=-1, keepdims=True) + NORM_EPS) * ng_ref[...]
    h2 = h * (1.0 + sc_ref[...]) + sh_ref[...]
    h2_ref[...] = h2

    lane = _iota((tm, LANES), 1)
    logits = _fdot(h2, wrt_ref[...]) + brt_ref[...]
    lg = jnp.where(lane < N_EXPERTS, logits, -jnp.inf)
    picks = []
    hot_sum = jnp.zeros((tm, LANES), F32)
    for _ in range(TOP_K):
        m = jnp.max(lg, axis=-1, keepdims=True)
        sel = jnp.min(jnp.where(lg == m, lane, LANES), axis=-1, keepdims=True)
        hot = lane == sel
        picks.append((m, sel, hot))
        hot_sum = hot_sum + hot.astype(F32)
        lg = jnp.where(hot, -jnp.inf, lg)
    es = [jnp.exp(m - picks[0][0]) for m, _, _ in picks]
    den = es[0] + es[1] + es[2] + es[3]

    before = jnp.dot(_tri(tm, True), hot_sum.astype(BF16), preferred_element_type=F32)
    before = before + carry_ref[...]
    idx_out = jnp.zeros((tm, LANES), jnp.int32)
    gate_out = jnp.zeros((tm, LANES), F32)
    rank_out = jnp.zeros((tm, LANES), jnp.int32)
    for kk, (m, sel, hot) in enumerate(picks):
        rk = jnp.sum(jnp.where(hot, before, 0.0), axis=-1, keepdims=True).astype(jnp.int32)
        idx_out = jnp.where(lane == kk, sel, idx_out)
        gate_out = jnp.where(lane == kk, es[kk] / den, gate_out)
        rank_out = jnp.where(lane == kk, rk, rank_out)
    idx_ref[...] = idx_out
    gate_ref[...] = gate_out
    rank_ref[...] = rank_out
    carry_ref[...] = carry_ref[...] + jnp.sum(hot_sum, axis=0, keepdims=True)
    cnt_ref[...] = carry_ref[...]


def _outproj(x2d, y_r, y_f, gate1, shift2, scale2, norm_g, wo_r, wo_f, w_rt, b_rt, tm, seq,
             row0, t):
    per_b = seq // tm
    blk0 = row0 // tm
    const = lambda i: (0, 0)
    rows = lambda i: (i, 0)
    rows_in = lambda i: (i + blk0, 0)
    mod = pl.BlockSpec((None, 1, D_MODEL), lambda i: ((i + blk0) // per_b, 0, 0))
    return pl.pallas_call(
        _outproj_kernel,
        grid=(t // tm,),
        in_specs=[pl.BlockSpec((tm, D_MODEL), rows_in),
                  pl.BlockSpec((tm, D_GRP), rows_in),
                  pl.BlockSpec((tm, D_GRP), rows_in),
                  mod, mod, mod,
                  pl.BlockSpec((1, D_MODEL), const),
                  pl.BlockSpec((D_GRP, D_MODEL), const),
                  pl.BlockSpec((D_GRP, D_MODEL), const),
                  pl.BlockSpec((D_MODEL, LANES), const),
                  pl.BlockSpec((1, LANES), const)],
        out_specs=[pl.BlockSpec((tm, D_MODEL), rows),
                   pl.BlockSpec((tm, D_MODEL), rows),
                   pl.BlockSpec((tm, LANES), rows),
                   pl.BlockSpec((tm, LANES), rows),
                   pl.BlockSpec((tm, LANES), rows),
                   pl.BlockSpec((1, LANES), const)],
        out_shape=[jax.ShapeDtypeStruct((t, D_MODEL), F32),
                   jax.ShapeDtypeStruct((t, D_MODEL), F32),
                   jax.ShapeDtypeStruct((t, LANES), jnp.int32),
                   jax.ShapeDtypeStruct((t, LANES), F32),
                   jax.ShapeDtypeStruct((t, LANES), jnp.int32),
                   jax.ShapeDtypeStruct((1, LANES), F32)],
        scratch_shapes=[pltpu.VMEM((1, LANES), F32)],
        compiler_params=pltpu.CompilerParams(
            dimension_semantics=("arbitrary",), vmem_limit_bytes=VMEM_LIMIT),
        name="outproj",
    )(x2d, y_r, y_f, gate1, shift2, scale2, norm_g, wo_r, wo_f, w_rt, b_rt)


SC_CORES = 2
SC_SUBCORES = 16
SC_ROWS = 32


def _sc_gather_rows(idx, src):
    n_workers = SC_CORES * SC_SUBCORES
    m = idx.shape[0]
    d = src.shape[1]
    assert m % (n_workers * SC_ROWS) == 0
    n_chunks = m // (n_workers * SC_ROWS)
    mesh = plsc.VectorSubcoreMesh(core_axis_name="c", subcore_axis_name="s")

    @functools.partial(
        pl.kernel, mesh=mesh,
        out_type=jax.ShapeDtypeStruct((m, d), src.dtype),
        scratch_types=[pltpu.VMEM((n_chunks, SC_ROWS), jnp.int32),
                       pltpu.VMEM((SC_ROWS, d), src.dtype),
                       pltpu.SemaphoreType.DMA],
        name="sc_gather")
    def gather(src_hbm, idx_hbm, out_hbm, idx_v, rows_v, sem):
        wid = lax.axis_index("s") * SC_CORES + lax.axis_index("c")
        pltpu.sync_copy(idx_hbm.at[wid], idx_v)

        @pl.loop(0, n_chunks)
        def _(j):
            pltpu.async_copy(src_hbm.at[idx_v.at[j]], rows_v, sem).wait()
            pltpu.sync_copy(rows_v, out_hbm.at[pl.ds((wid * n_chunks + j) * SC_ROWS, SC_ROWS)])

    return gather(src, idx.reshape(n_workers, n_chunks, SC_ROWS))


def _sc_scatter_rows(src, dest, n_out):
    n_workers = SC_CORES * SC_SUBCORES
    t, d = src.shape
    n_slot = dest.shape[1]
    assert t % (n_workers * SC_ROWS) == 0
    n_chunks = t // (n_workers * SC_ROWS)
    mesh = plsc.VectorSubcoreMesh(core_axis_name="c", subcore_axis_name="s")
    idx = dest.reshape(n_workers, n_chunks, SC_ROWS, n_slot).transpose(0, 1, 3, 2)
    idx = idx.reshape(n_workers, n_chunks * n_slot, SC_ROWS)

    @functools.partial(
        pl.kernel, mesh=mesh,
        out_type=jax.ShapeDtypeStruct((n_out, d), src.dtype),
        scratch_types=[pltpu.VMEM((n_chunks * n_slot, SC_ROWS), jnp.int32),
                       pltpu.VMEM((SC_ROWS, d), src.dtype)],
        name="sc_scatter")
    def scatter(src_hbm, idx_hbm, out_hbm, idx_v, rows_v):
        wid = lax.axis_index("s") * SC_CORES + lax.axis_index("c")
        pltpu.sync_copy(idx_hbm.at[wid], idx_v)

        @pl.loop(0, n_chunks)
        def _(j):
            pltpu.sync_copy(src_hbm.at[pl.ds((wid * n_chunks + j) * SC_ROWS, SC_ROWS)], rows_v)
            for k in range(n_slot):
                pltpu.sync_copy(rows_v, out_hbm.at[idx_v.at[j * n_slot + k]])

    return scatter(src, idx)


def _expert_kernel(be_ref, nv_ref, x_ref, wgu_ref, bgu_ref, wd_ref, bd_ref, o_ref):
    del be_ref
    valid = _iota((EXPERT_BLOCK, 1), 0) < nv_ref[pl.program_id(0)]
    x = jnp.where(valid, x_ref[...], 0.0)
    gu = jnp.dot(x.astype(BF16), wgu_ref[...], preferred_element_type=F32) + bgu_ref[...]
    gate = jnp.minimum(gu[:, :D_MODEL], SWIGLU_LIMIT)
    up = jnp.clip(gu[:, D_MODEL:], -SWIGLU_LIMIT, SWIGLU_LIMIT)
    act = gate * _sigmoid(SWIGLU_ALPHA * gate) * (up + 1.0)
    o_ref[...] = jnp.dot(act.astype(BF16), wd_ref[...], preferred_element_type=F32) + bd_ref[...]


def _experts(block_e, n_valid, xs, w_gu, b_gu, w_d, b_d):
    n_blocks = block_e.shape[0]
    grid_spec = pltpu.PrefetchScalarGridSpec(
        num_scalar_prefetch=2,
        grid=(n_blocks,),
        in_specs=[pl.BlockSpec((EXPERT_BLOCK, D_MODEL), lambda j, be, nv: (j, 0)),
                  pl.BlockSpec((None, D_MODEL, 2 * D_MODEL), lambda j, be, nv: (be[j], 0, 0)),
                  pl.BlockSpec((None, 1, 2 * D_MODEL), lambda j, be, nv: (be[j], 0, 0)),
                  pl.BlockSpec((None, D_MODEL, D_MODEL), lambda j, be, nv: (be[j], 0, 0)),
                  pl.BlockSpec((None, 1, D_MODEL), lambda j, be, nv: (be[j], 0, 0))],
        out_specs=pl.BlockSpec((EXPERT_BLOCK, D_MODEL), lambda j, be, nv: (j, 0)),
    )
    return pl.pallas_call(
        _expert_kernel,
        grid_spec=grid_spec,
        out_shape=jax.ShapeDtypeStruct(xs.shape, F32),
        compiler_params=pltpu.CompilerParams(
            dimension_semantics=("arbitrary",), vmem_limit_bytes=VMEM_LIMIT),
        name="experts",
    )(block_e, n_valid, xs, w_gu, b_gu, w_d, b_d)


COMBINE_TOKENS = 256
MOE_SPLITS = 2


def _combine_kernel(yg_ref, x1_ref, gate_ref, g2_ref, fg_ref, o_ref):
    gates = gate_ref[...]
    acc = None
    for kk in range(TOP_K):
        part = gates[:, kk:kk + 1] * yg_ref[kk * COMBINE_TOKENS:(kk + 1) * COMBINE_TOKENS, :]
        acc = part if acc is None else acc + part
    x2 = x1_ref[...] + g2_ref[...] * acc
    o_ref[...] = x2 * lax.rsqrt(jnp.mean(x2 * x2, axis=-1, keepdims=True) + NORM_EPS) * fg_ref[...]


def _combine_kernel_into(prev_ref, *refs):
    del prev_ref
    _combine_kernel(*refs)


def _combine(yg, x1, gates, gate2, final_g, seq, row0, t_total, prev):
    t = x1.shape[0]
    tm = COMBINE_TOKENS
    per_b = seq // tm
    blk0 = row0 // tm
    rows = lambda i: (i, 0)
    in_specs = [pl.BlockSpec((TOP_K * tm, D_MODEL), rows),
                pl.BlockSpec((tm, D_MODEL), rows),
                pl.BlockSpec((tm, LANES), rows),
                pl.BlockSpec((None, 1, D_MODEL), lambda i: ((i + blk0) // per_b, 0, 0)),
                pl.BlockSpec((1, D_MODEL), lambda i: (0, 0))]
    args = (yg, x1, gates, gate2, final_g)
    if prev is not None:
        in_specs = [pl.BlockSpec(memory_space=pl.ANY)] + in_specs
        args = (prev,) + args
    return pl.pallas_call(
        _combine_kernel if prev is None else _combine_kernel_into,
        grid=(t // tm,),
        in_specs=in_specs,
        out_specs=pl.BlockSpec((tm, D_MODEL), lambda i: (i + blk0, 0)),
        out_shape=jax.ShapeDtypeStruct((t_total, D_MODEL), F32),
        input_output_aliases={} if prev is None else {0: 0},
        compiler_params=pltpu.CompilerParams(
            dimension_semantics=("parallel",), vmem_limit_bytes=VMEM_LIMIT),
        name="combine",
    )(*args)


def _moe(h2, idx, gates, rank, counts, x1, gate2, final_g, w_gu, b_gu, w_d, b_d, seq,
         row0, t_total, prev):
    t = h2.shape[0]
    n_slots = t * TOP_K
    n_blocks = -(-n_slots // EXPERT_BLOCK) + N_EXPERTS
    cap = n_blocks * EXPERT_BLOCK
    padded = (counts + EXPERT_BLOCK - 1) // EXPERT_BLOCK * EXPERT_BLOCK
    pad_ends = jnp.cumsum(padded)
    pad_starts = pad_ends - padded
    dest = pad_starts[idx] + rank
    block_starts = jnp.arange(n_blocks, dtype=jnp.int32) * EXPERT_BLOCK
    block_e = jnp.minimum(jnp.sum(block_starts[:, None] >= pad_ends[None, :], axis=1),
                          N_EXPERTS - 1).astype(jnp.int32)
    n_valid = jnp.clip(counts[block_e] - (block_starts - pad_starts[block_e]), 0, EXPERT_BLOCK)

    xs = _sc_scatter_rows(h2, dest, cap)
    yb = _experts(block_e, n_valid.astype(jnp.int32), xs, w_gu, b_gu, w_d, b_d)
    dest_blocks = dest.reshape(-1, COMBINE_TOKENS, TOP_K).transpose(0, 2, 1).reshape(-1)
    yg = _sc_gather_rows(dest_blocks, yb)
    return _combine(yg, x1, gates, gate2, final_g, seq, row0, t_total, prev)


def _layer(x, c_mod, norm1_g, w_in, mu_shift, w0, w2, a0, a2, g2, k_k, k_a, r_k, gn_w, gn_b, b_f,
           q_norm_g, k_norm_g, o_norm_g, w_out, norm2_g, w_router, b_router, w_gate_up,
           b_gate_up, w_down, b_down, final_g, tm_in, tq, tm_out):
    bsz, seq, _ = x.shape
    shift1, scale1, gate1, shift2, scale2, gate2 = (
        m.reshape(bsz, 1, D_MODEL) for m in jnp.split(c_mod, 6, axis=-1))
    row = lambda v: v.reshape(1, -1)

    w_r = w_in[:, :RWKV_COLS].astype(BF16)
    w_x = w_in[:, RWKV_COLS:RWKV_COLS + FOX_MAIN].astype(BF16)
    w_f = jnp.pad(w_in[:, RWKV_COLS + FOX_MAIN:], ((0, 0), (0, LANES - N_HEADS)))
    b_fp = jnp.pad(b_f, (0, LANES - N_HEADS)).reshape(1, LANES)
    qk_gain = jnp.concatenate([jnp.tile(q_norm_g, N_HEADS) * HEAD_DIM ** -0.5,
                               jnp.tile(k_norm_g, N_HEADS)]).reshape(1, -1)
    p_r, p_x, k_bias, q_bias = _inproj(x, shift1, scale1, row(norm1_g), w_r, w_x, w_f, b_fp,
                                       qk_gain, tm_in)

    zeros = jnp.zeros((LANES - 64, D_GRP), F32)
    w2p = jnp.concatenate([w2, zeros], axis=0).astype(BF16)
    a2p = jnp.concatenate([zeros, a2], axis=0).astype(BF16)
    y_r = _rwkv(p_r, row(mu_shift), row(w0), w2p, row(a0), a2p, g2.astype(BF16), row(k_k),
                row(k_a), row(r_k), row(gn_w), row(gn_b))

    y_f = _fox(p_x, k_bias, q_bias, jnp.tile(o_norm_g, 2).reshape(1, LANES), tq)

    t = bsz * seq
    w_rt = jnp.pad(w_router, ((0, 0), (0, LANES - N_EXPERTS)))
    b_rt = jnp.pad(b_router, (0, LANES - N_EXPERTS)).reshape(1, LANES)
    wo = w_out.astype(BF16)
    w_gu, w_d = w_gate_up.astype(BF16), w_down.astype(BF16)
    b_gu, b_d = b_gate_up.reshape(N_EXPERTS, 1, -1), b_down.reshape(N_EXPERTS, 1, -1)
    t_part = t // MOE_SPLITS
    out = None
    for part in range(MOE_SPLITS):
        row0 = part * t_part
        x1, h2, idx, gates, rank, cnt = _outproj(
            x.reshape(t, D_MODEL), y_r.reshape(t, D_GRP), y_f.reshape(t, D_GRP), gate1, shift2,
            scale2, row(norm2_g), wo[:D_GRP], wo[D_GRP:], w_rt, b_rt, tm_out, seq, row0, t_part)
        counts = cnt[0, :N_EXPERTS].astype(jnp.int32)
        out = _moe(h2, idx[:, :TOP_K], gates, rank[:, :TOP_K], counts, x1, gate2, row(final_g),
                   w_gu, b_gu, w_d, b_d, seq, row0, t, out)
    return out.reshape(bsz, seq, D_MODEL)


def kernel(x, c, w_ada, b_ada, norm1_g, w_in, mu_shift, w0, w2, a0, a2, g2, k_k, k_a, r_k, gn_w,
           gn_b, b_f, q_norm_g, k_norm_g, o_norm_g, w_out, norm2_g, w_router, b_router, w_gate_up,
           b_gate_up, w_down, b_down, final_g):
    assert w_ada.shape[0] == 1, "single-layer block"
    c_mod = _adaln(c, w_ada[0], b_ada[0])
    return _layer(x, c_mod, norm1_g[0], w_in[0], mu_shift[0], w0[0], w2[0], a0[0], a2[0], g2[0],
                  k_k[0], k_a[0], r_k[0], gn_w[0], gn_b[0], b_f[0], q_norm_g[0], k_norm_g[0],
                  o_norm_g[0], w_out[0], norm2_g[0], w_router[0], b_router[0], w_gate_up[0],
                  b_gate_up[0], w_down[0], b_down[0], final_g,
                  tm_in=min(512, x.shape[1]), tq=min(512, x.shape[1]), tm_out=min(512, x.shape[1]))
```

```python
import functools

import jax
import jax.numpy as jnp
from jax import lax
from jax.experimental import pallas as pl
from jax.experimental.pallas import tpu as pltpu
from jax.experimental.pallas import tpu_sc as plsc

F32 = jnp.float32
BF16 = jnp.bfloat16
HIGHEST = lax.Precision.HIGHEST

D_MODEL = 1024
HEAD_DIM = 64
N_HEADS = 8
D_GRP = N_HEADS * HEAD_DIM
RWKV_COLS = 1792
LORA_OFF = 3 * D_GRP
GATE_OFF = LORA_OFF + 128
FOX_MAIN = 4 * D_GRP
N_EXPERTS = 32
TOP_K = 4
EXPERT_BLOCK = 256
EXPERT_CHUNK = 256
SWIGLU_ALPHA = 1.702
SWIGLU_LIMIT = 7.0
NORM_EPS = 1e-6
GN_EPS = 64e-5
LANES = 128
CHUNK = 64
HEADS_PER_SCAN = 4
SCAN_W = HEADS_PER_SCAN * HEAD_DIM
VMEM_LIMIT = 56 * 1024 * 1024


def _dot(a, b):
    return jnp.dot(a.astype(BF16), b.astype(BF16), preferred_element_type=F32)


def _dot_nt(a, b):
    return lax.dot_general(a.astype(BF16), b.astype(BF16), (((1,), (1,)), ((), ())),
                           preferred_element_type=F32)


def _dot_tn(a, b):
    return lax.dot_general(a.astype(BF16), b.astype(BF16), (((0,), (0,)), ((), ())),
                           preferred_element_type=F32)


def _fdot(a, b):
    return jnp.dot(a, b, precision=HIGHEST, preferred_element_type=F32)


def _split_dot(x, m, terms=2, left=False):
    acc = None
    rem = x
    for _ in range(terms):
        part = rem.astype(BF16)
        rem = rem - part.astype(F32)
        d = (jnp.dot(m, part, preferred_element_type=F32) if left
             else jnp.dot(part, m, preferred_element_type=F32))
        acc = d if acc is None else acc + d
    return acc


def _iota(shape, dim):
    return lax.broadcasted_iota(jnp.int32, shape, dim)


def _seg_reduce_mat(n):
    return (_iota((n, LANES), 0) // HEAD_DIM == _iota((n, LANES), 1)).astype(BF16)


def _seg_expand_mat(n):
    return (_iota((LANES, n), 1) // HEAD_DIM == _iota((LANES, n), 0)).astype(BF16)


def _tri(n, strict):
    r, c = _iota((n, n), 0), _iota((n, n), 1)
    return ((r > c) if strict else (r >= c)).astype(BF16)


def _log_sigmoid(z):
    return jnp.minimum(z, 0.0) - jnp.log(1.0 + jnp.exp(-jnp.abs(z)))


def _sigmoid(z):
    return 1.0 / (1.0 + jnp.exp(-z))


def _adaln_kernel(c_ref, w_ref, b_ref, o_ref):
    c = c_ref[...]
    o_ref[...] = _fdot(c * _sigmoid(c), w_ref[...]) + b_ref[...]


def _adaln(c, w_ada, b_ada):
    bsz = c.shape[0]
    n_mod = w_ada.shape[1] // D_MODEL
    return pl.pallas_call(
        _adaln_kernel,
        grid=(n_mod,),
        in_specs=[pl.BlockSpec((bsz, D_MODEL), lambda j: (0, 0)),
                  pl.BlockSpec((D_MODEL, D_MODEL), lambda j: (0, j)),
                  pl.BlockSpec((1, D_MODEL), lambda j: (0, j))],
        out_specs=pl.BlockSpec((bsz, D_MODEL), lambda j: (0, j)),
        out_shape=jax.ShapeDtypeStruct((bsz, n_mod * D_MODEL), F32),
        name="adaln",
    )(c, w_ada, b_ada.reshape(1, -1))


def _inproj_kernel(x_ref, sh_ref, sc_ref, g_ref, wr_ref, wx_ref, wfh_ref, wfl_ref, bf_ref, qkg_ref,
                   pr_ref, px_ref, kb_ref, qb_ref, carry_ref):
    @pl.when(pl.program_id(1) == 0)
    def _():
        carry_ref[...] = jnp.zeros_like(carry_ref)

    x = x_ref[...]
    tm = x.shape[0]
    h = x * lax.rsqrt(jnp.mean(x * x, axis=-1, keepdims=True) + NORM_EPS) * g_ref[...]
    h = h * (1.0 + sc_ref[...]) + sh_ref[...]
    hb = h.astype(BF16)
    h_lo = (h - hb.astype(F32)).astype(BF16)

    pr_ref[...] = jnp.dot(hb, wr_ref[...], preferred_element_type=F32).astype(BF16)

    px = jnp.dot(hb, wx_ref[...], preferred_element_type=F32)
    qk = px[:, :2 * D_GRP]
    ss = _split_dot(qk * qk, _seg_reduce_mat(2 * D_GRP))
    inv = lax.rsqrt(ss * (1.0 / HEAD_DIM) + NORM_EPS)
    qk = qk * _split_dot(inv, _seg_expand_mat(2 * D_GRP)) * qkg_ref[...]
    px_ref[:, :2 * D_GRP] = qk.astype(BF16)
    px_ref[:, 2 * D_GRP:] = px[:, 2 * D_GRP:].astype(BF16)

    z = (jnp.dot(hb, wfh_ref[...], preferred_element_type=F32)
         + jnp.dot(h_lo, wfh_ref[...], preferred_element_type=F32)
         + jnp.dot(hb, wfl_ref[...], preferred_element_type=F32)) + bf_ref[...]
    cum = _split_dot(_log_sigmoid(z), _tri(tm, False), terms=3, left=True) + carry_ref[...]
    carry_ref[...] = cum[tm - 1:tm, :]

    src, dst = _iota((LANES, LANES), 0), _iota((LANES, LANES), 1)
    parts = []
    rem = cum
    for _ in range(3):
        part = rem.astype(BF16)
        rem = rem - part.astype(F32)
        parts.append(part)

    def spread(offset):
        return sum(jnp.dot(part, ((dst == 8 * src + offset + t) & (src < N_HEADS)).astype(BF16),
                           preferred_element_type=F32) for t, part in enumerate(parts))

    slot = _iota((1, LANES), 1) % 8
    kb_ref[...] = (jnp.where((slot >= 3) & (slot < 6), 1.0, 0.0) - spread(0)).astype(BF16)
    qb_ref[...] = (jnp.where(slot < 3, 1.0, 0.0) + spread(3)).astype(BF16)


def _inproj(x, shift, scale, g, w_r, w_x, w_f, b_f, qk_gain, tm):
    w_f_hi = w_f.astype(BF16)
    w_f_lo = (w_f - w_f_hi.astype(F32)).astype(BF16)
    bsz, seq, _ = x.shape
    const = lambda b, s: (0, 0)
    return pl.pallas_call(
        _inproj_kernel,
        grid=(bsz, seq // tm),
        in_specs=[pl.BlockSpec((None, tm, D_MODEL), lambda b, s: (b, s, 0)),
                  pl.BlockSpec((None, 1, D_MODEL), lambda b, s: (b, 0, 0)),
                  pl.BlockSpec((None, 1, D_MODEL), lambda b, s: (b, 0, 0)),
                  pl.BlockSpec((1, D_MODEL), const),
                  pl.BlockSpec((D_MODEL, RWKV_COLS), const),
                  pl.BlockSpec((D_MODEL, FOX_MAIN), const),
                  pl.BlockSpec((D_MODEL, LANES), const),
                  pl.BlockSpec((D_MODEL, LANES), const),
                  pl.BlockSpec((1, LANES), const),
                  pl.BlockSpec((1, 2 * D_GRP), const)],
        out_specs=[pl.BlockSpec((None, tm, RWKV_COLS), lambda b, s: (b, s, 0)),
                   pl.BlockSpec((None, tm, FOX_MAIN), lambda b, s: (b, s, 0)),
                   pl.BlockSpec((None, tm, LANES), lambda b, s: (b, s, 0)),
                   pl.BlockSpec((None, tm, LANES), lambda b, s: (b, s, 0))],
        out_shape=[jax.ShapeDtypeStruct((bsz, seq, RWKV_COLS), BF16),
                   jax.ShapeDtypeStruct((bsz, seq, FOX_MAIN), BF16),
                   jax.ShapeDtypeStruct((bsz, seq, LANES), BF16),
                   jax.ShapeDtypeStruct((bsz, seq, LANES), BF16)],
        scratch_shapes=[pltpu.VMEM((1, LANES), F32)],
        compiler_params=pltpu.CompilerParams(
            dimension_semantics=("parallel", "arbitrary"), vmem_limit_bytes=VMEM_LIMIT),
        name="inproj",
    )(x, shift, scale, g, w_r, w_x, w_f_hi, w_f_lo, b_f, qk_gain)


_NN = (((1,), (0,)), ((), ()))
_NT = (((1,), (1,)), ((), ()))
_TN = (((0,), (0,)), ((), ()))
SCAN_N = HEADS_PER_SCAN * CHUNK
BATCH_PER_STEP = 4
INV_LEVELS = 5
M_HEAD, M_STRICT, M_INCL, M_EYE, M_BASE, M_OFF = 0, 1, 2, 3, 4, 5


def _bdot(a, b, dims):
    return lax.dot_general(a, b, dims, preferred_element_type=F32)


def _scan_masks():
    rr, cc = _iota((SCAN_N, SCAN_W), 0), _iota((SCAN_N, SCAN_W), 1)
    ri, ci = _iota((SCAN_N, SCAN_N), 0), _iota((SCAN_N, SCAN_N), 1)
    same = ri // CHUNK == ci // CHUNK
    masks = [rr // CHUNK == cc // HEAD_DIM, same & (ri > ci), same & (ri >= ci), ri == ci,
             (ri // 2 == ci // 2) & (ri > ci)]
    blk = 2
    while blk < CHUNK:
        masks.append((ri // (2 * blk) == ci // (2 * blk)) & (ri // blk != ci // blk) & (ri > ci))
        blk *= 2
    return jnp.stack(masks).astype(BF16)


def _rwkv_kernel(p_ref, masks_ref, mu_ref, w0_ref, w2_ref, a0_ref, a2_ref, g2_ref, kk_ref, ka_ref,
                 rk_ref, gnw_ref, gnb_ref, o_ref, last_ref, state_ref):
    @pl.when(pl.program_id(1) == 0)
    def _():
        last_ref[...] = jnp.zeros_like(last_ref)
        state_ref[...] = jnp.zeros_like(state_ref)

    mu, w0, w2, a0, a2, g2, k_k, k_a, r_k, gn_w, gn_b = (
        ref[...] for ref in (mu_ref, w0_ref, w2_ref, a0_ref, a2_ref, g2_ref, kk_ref, ka_ref,
                             rk_ref, gnw_ref, gnb_ref))
    rows = BATCH_PER_STEP * CHUNK
    p = p_ref[...].astype(F32).reshape(rows, RWKV_COLS)
    row_id = _iota((rows, 1), 0)
    prev = pltpu.roll(p, 1, axis=0)
    for bb in range(BATCH_PER_STEP):
        prev = jnp.where(row_id == bb * CHUNK, last_ref[bb], prev)
        last_ref[bb] = p[(bb + 1) * CHUNK - 1:(bb + 1) * CHUNK, :]
    pf = p + mu * (prev - p)
    r = pf[:, 0:D_GRP]
    k = pf[:, D_GRP:2 * D_GRP]
    v = pf[:, 2 * D_GRP:3 * D_GRP]
    lora = pf[:, LORA_OFF:GATE_OFF]
    gd = pf[:, GATE_OFF:RWKV_COLS]

    wlog = w0 + _dot(jnp.tanh(lora), w2)
    neg = -wlog
    softplus = jnp.maximum(neg, 0.0) + jnp.log(1.0 + jnp.exp(-jnp.abs(neg)))
    ld = -jnp.exp(-softplus - 0.5)
    a = _sigmoid(a0 + _dot(lora, a2))
    g = _dot(_sigmoid(gd), g2)

    red, exp_m = _seg_reduce_mat(D_GRP), _seg_expand_mat(D_GRP)
    kk = k * k_k
    n2 = _split_dot(kk * kk, red)
    kk = kk * _split_dot(1.0 / jnp.maximum(jnp.sqrt(n2), 1e-12), exp_m)
    k2 = k * (1.0 + (a - 1.0) * k_a)

    tr, tc = _iota((rows, rows), 0), _iota((rows, rows), 1)
    tri = ((tr >= tc) & (tr // CHUNK == tc // CHUNK)).astype(BF16)
    cl = _split_dot(ld, tri, terms=3, left=True)
    cl_end = jnp.concatenate(
        [jnp.broadcast_to(cl[(bb + 1) * CHUNK - 1:(bb + 1) * CHUNK, :], (CHUNK, D_GRP))
         for bb in range(BATCH_PER_STEP)], axis=0)
    e_in = jnp.exp(cl)
    e_out = jnp.exp(-cl)
    e_rem = jnp.exp(cl_end - cl)
    p_end = jnp.exp(cl_end)
    kka = kk * a
    ops = [(-kk * jnp.exp(cl - ld)).astype(BF16), (kka * e_out).astype(BF16),
           (k2 * e_out).astype(BF16), (r * e_in).astype(BF16), v.astype(BF16),
           (kka * e_rem).astype(BF16), (k2 * e_rem).astype(BF16)]

    chains = [(bb, grp) for bb in range(BATCH_PER_STEP)
              for grp in range(N_HEADS // HEADS_PER_SCAN)]
    head_mask = masks_ref[M_HEAD]
    strict, incl = masks_ref[M_STRICT], masks_ref[M_INCL]

    def stacked(op, bb, grp):
        part = op[bb * CHUNK:(bb + 1) * CHUNK, grp * SCAN_W:(grp + 1) * SCAN_W]
        return jnp.concatenate([part] * HEADS_PER_SCAN, axis=0) * head_mask

    xs = [[stacked(op, bb, grp) for op in ops] for bb, grp in chains]
    st = [state_ref[bb, grp] for bb, grp in chains]
    sb = [s.astype(BF16) for s in st]
    nab = [_bdot(x[0], x[1], _NT).astype(BF16) for x in xs]
    aak = [_bdot(x[0], x[2], _NT).astype(BF16) * strict for x in xs]
    arb = [_bdot(x[3], x[1], _NT).astype(BF16) * incl for x in xs]
    ark = [_bdot(x[3], x[2], _NT).astype(BF16) * incl for x in xs]
    t_inv = [masks_ref[M_EYE] + n * masks_ref[M_BASE] for n in nab]
    for lvl in range(INV_LEVELS):
        half = [_bdot(t, n * masks_ref[M_OFF + lvl], _NN).astype(BF16) for t, n in zip(t_inv, nab)]
        t_inv = [t + _bdot(h, t, _NN).astype(BF16) for t, h in zip(t_inv, half)]
    rhs = [(_bdot(x[0], s, _NT) + _bdot(k, x[4], _NN)).astype(BF16)
           for x, s, k in zip(xs, sb, aak)]
    sa = [_bdot(t, h, _NN).astype(BF16) for t, h in zip(t_inv, rhs)]
    ys = [_bdot(x[3], s, _NT) + _bdot(b, u, _NN) + _bdot(k, x[4], _NN)
          for x, s, b, u, k in zip(xs, sb, arb, sa, ark)]
    for (bb, grp), x, s, u in zip(chains, xs, st, sa):
        decay = p_end[bb * CHUNK:bb * CHUNK + 1, grp * SCAN_W:(grp + 1) * SCAN_W]
        state_ref[bb, grp] = s * decay + _bdot(u, x[5], _TN) + _bdot(x[4], x[6], _TN)
    ys = [y[0:CHUNK] + y[CHUNK:2 * CHUNK] + y[2 * CHUNK:3 * CHUNK] + y[3 * CHUNK:4 * CHUNK]
          for y in ys]
    n_grp = N_HEADS // HEADS_PER_SCAN
    y = jnp.concatenate([jnp.concatenate(ys[bb * n_grp:(bb + 1) * n_grp], axis=1)
                         for bb in range(BATCH_PER_STEP)], axis=0)

    mean = _split_dot(_split_dot(y, red) * (1.0 / HEAD_DIM), exp_m)
    d = y - mean
    var = _split_dot(d * d, red) * (1.0 / HEAD_DIM)
    yn = d * _split_dot(lax.rsqrt(var + GN_EPS), exp_m) * gn_w + gn_b
    bonus = _split_dot(_split_dot(r * k2 * r_k, red), exp_m) * v
    o_ref[...] = ((yn + bonus) * g).astype(BF16).reshape(BATCH_PER_STEP, CHUNK, D_GRP)


def _rwkv(p_r, mu, w0, w2p, a0, a2p, g2, k_k, k_a, r_k, gn_w, gn_b):
    bsz, seq, _ = p_r.shape
    assert bsz % BATCH_PER_STEP == 0
    masks = _scan_masks()
    const = lambda b, s: (0, 0)
    vec = pl.BlockSpec((1, D_GRP), const)
    return pl.pallas_call(
        _rwkv_kernel,
        grid=(bsz // BATCH_PER_STEP, seq // CHUNK),
        in_specs=[pl.BlockSpec((BATCH_PER_STEP, CHUNK, RWKV_COLS), lambda b, s: (b, s, 0)),
                  pl.BlockSpec(masks.shape, lambda b, s: (0, 0, 0)),
                  pl.BlockSpec((1, RWKV_COLS), const),
                  vec, pl.BlockSpec((LANES, D_GRP), const),
                  vec, pl.BlockSpec((LANES, D_GRP), const),
                  pl.BlockSpec((LANES, D_GRP), const),
                  vec, vec, vec, vec, vec],
        out_specs=pl.BlockSpec((BATCH_PER_STEP, CHUNK, D_GRP), lambda b, s: (b, s, 0)),
        out_shape=jax.ShapeDtypeStruct((bsz, seq, D_GRP), BF16),
        scratch_shapes=[pltpu.VMEM((BATCH_PER_STEP, 1, RWKV_COLS), F32),
                        pltpu.VMEM((BATCH_PER_STEP, N_HEADS // HEADS_PER_SCAN, SCAN_W, SCAN_W), F32)],
        compiler_params=pltpu.CompilerParams(
            dimension_semantics=("parallel", "arbitrary"), vmem_limit_bytes=VMEM_LIMIT),
        name="rwkv",
    )(p_r, masks, mu, w0, w2p, a0, a2p, g2, k_k, k_a, r_k, gn_w, gn_b)


def _fox_kernel(q_ref, qb_ref, k_ref, kb_ref, vt_ref, og_ref, ong_ref, o_ref, *, tq):
    hp = pl.program_id(1)
    qi = pl.program_id(2)
    lane = _iota((1, LANES), 1)
    q = q_ref[...]
    qb = qb_ref[...]
    zero = jnp.zeros_like(q)
    qcat = [jnp.concatenate([jnp.where(lane // HEAD_DIM == hh, q, zero),
                             jnp.where(lane // 8 == hp * 2 + hh, qb, zero)], axis=1)
            for hh in range(2)]
    key_pos = _iota((tq, tq), 0)
    qry_pos = _iota((tq, tq), 1)

    def tile(j, carry, masked):
        start = pl.multiple_of(j * tq, tq)
        kcat = jnp.concatenate([k_ref[pl.ds(start, tq), :], kb_ref[pl.ds(start, tq), :]], axis=1)
        vt = vt_ref[:, pl.ds(start, tq)]
        sts = [lax.dot_general(kcat, qc, _NT, preferred_element_type=F32) for qc in qcat]
        if masked:
            sts = [jnp.where(qry_pos >= key_pos, st, -jnp.inf) for st in sts]
        m_new = [jnp.maximum(c[0], jnp.max(st, axis=0, keepdims=True)) for c, st in zip(carry, sts)]
        pts = [jnp.exp(st - m) for st, m in zip(sts, m_new)]
        pvs = [jnp.dot(vt, pt.astype(BF16), preferred_element_type=F32) for pt in pts]
        new = []
        for hh in range(2):
            m, l, acc = carry[hh]
            alpha = jnp.exp(m - m_new[hh])
            l = alpha * l + jnp.sum(pts[hh], axis=0, keepdims=True)
            acc = alpha * acc + pvs[hh][hh * HEAD_DIM:(hh + 1) * HEAD_DIM, :]
            new.append((m_new[hh], l, acc))
        return tuple(new)

    init = (jnp.full((1, tq), -jnp.inf, F32), jnp.zeros((1, tq), F32),
            jnp.zeros((HEAD_DIM, tq), F32))
    carry = lax.fori_loop(0, qi, functools.partial(tile, masked=False), (init, init))
    outs = []
    for _, l, acc in tile(qi, carry, True):
        o = acc / l
        outs.append(o * lax.rsqrt(jnp.mean(o * o, axis=0, keepdims=True) + NORM_EPS))
    o = jnp.concatenate(outs, axis=0).T
    o_ref[...] = (o * ong_ref[...] * _sigmoid(og_ref[...].astype(F32))).astype(BF16)


def _fox(p_x, k_bias, q_bias, o_gain, tq):
    bsz, seq, _ = p_x.shape
    npair = N_HEADS // 2
    v_t = jnp.transpose(p_x[:, :, 2 * D_GRP:3 * D_GRP], (0, 2, 1))
    return pl.pallas_call(
        functools.partial(_fox_kernel, tq=tq),
        grid=(bsz, npair, seq // tq),
        in_specs=[pl.BlockSpec((None, tq, LANES), lambda b, h, i: (b, i, h)),
                  pl.BlockSpec((None, tq, LANES), lambda b, h, i: (b, i, 0)),
                  pl.BlockSpec((None, seq, LANES), lambda b, h, i: (b, 0, npair + h)),
                  pl.BlockSpec((None, seq, LANES), lambda b, h, i: (b, 0, 0)),
                  pl.BlockSpec((None, LANES, seq), lambda b, h, i: (b, h, 0)),
                  pl.BlockSpec((None, tq, LANES), lambda b, h, i: (b, i, 3 * npair + h)),
                  pl.BlockSpec((1, LANES), lambda b, h, i: (0, 0))],
        out_specs=pl.BlockSpec((None, tq, LANES), lambda b, h, i: (b, i, h)),
        out_shape=jax.ShapeDtypeStruct((bsz, seq, D_GRP), BF16),
        compiler_params=pltpu.CompilerParams(
            dimension_semantics=("parallel", "parallel", "arbitrary"),
            vmem_limit_bytes=VMEM_LIMIT),
        name="fox",
    )(p_x, q_bias, p_x, k_bias, v_t, p_x, o_gain)


def _outproj_kernel(x_ref, yr_ref, yf_ref, g1_ref, sh_ref, sc_ref, ng_ref, wor_ref, wof_ref,
                    wrt_ref, wrl_ref, brt_ref, x1_ref, h2_ref, idx_ref, gate_ref, rank_ref, cnt_ref,
                    carry_ref):
    @pl.when(pl.program_id(0) == 0)
    def _():
        carry_ref[...] = jnp.zeros_like(carry_ref)

    y = (jnp.dot(yr_ref[...], wor_ref[...], preferred_element_type=F32)
         + jnp.dot(yf_ref[...], wof_ref[...], preferred_element_type=F32))
    x1 = x_ref[...] + g1_ref[...] * y
    x1_ref[...] = x1
    tm = x1.shape[0]
    h = x1 * lax.rsqrt(jnp.mean(x1 * x1, axis=-1, keepdims=True) + NORM_EPS) * ng_ref[...]
    h2 = h * (1.0 + sc_ref[...]) + sh_ref[...]
    h2_ref[...] = h2

    lane = _iota((tm, LANES), 1)
    h_hi = h2.astype(BF16)
    h_lo = (h2 - h_hi.astype(F32)).astype(BF16)
    logits = (jnp.dot(h_hi, wrt_ref[...], preferred_element_type=F32)
              + jnp.dot(h_lo, wrt_ref[...], preferred_element_type=F32)
              + jnp.dot(h_hi, wrl_ref[...], preferred_element_type=F32)) + brt_ref[...]
    lg = jnp.where(lane < N_EXPERTS, logits, -jnp.inf)
    picks = []
    hot_sum = jnp.zeros((tm, LANES), F32)
    for _ in range(TOP_K):
        m = jnp.max(lg, axis=-1, keepdims=True)
        sel = jnp.min(jnp.where(lg == m, lane, LANES), axis=-1, keepdims=True)
        hot = lane == sel
        picks.append((m, sel, hot))
        hot_sum = hot_sum + hot.astype(F32)
        lg = jnp.where(hot, -jnp.inf, lg)
    es = [jnp.exp(m - picks[0][0]) for m, _, _ in picks]
    den = es[0] + es[1] + es[2] + es[3]

    before = jnp.dot(_tri(tm, True), hot_sum.astype(BF16), preferred_element_type=F32)
    before = before + carry_ref[...]
    idx_out = jnp.zeros((tm, LANES), jnp.int32)
    gate_out = jnp.zeros((tm, LANES), F32)
    rank_out = jnp.zeros((tm, LANES), jnp.int32)
    for kk, (m, sel, hot) in enumerate(picks):
        rk = jnp.sum(jnp.where(hot, before, 0.0), axis=-1, keepdims=True).astype(jnp.int32)
        idx_out = jnp.where(lane == kk, sel, idx_out)
        gate_out = jnp.where(lane == kk, es[kk] / den, gate_out)
        rank_out = jnp.where(lane == kk, rk, rank_out)
    idx_ref[...] = idx_out
    gate_ref[...] = gate_out
    rank_ref[...] = rank_out
    carry_ref[...] = carry_ref[...] + jnp.sum(hot_sum, axis=0, keepdims=True)
    cnt_ref[...] = carry_ref[...]


def _outproj(x2d, y_r, y_f, gate1, shift2, scale2, norm_g, wo_r, wo_f, w_rt, b_rt, tm, seq,
             row0, t):
    w_rt_hi = w_rt.astype(BF16)
    w_rt_lo = (w_rt - w_rt_hi.astype(F32)).astype(BF16)
    per_b = seq // tm
    blk0 = row0 // tm
    const = lambda i: (0, 0)
    rows = lambda i: (i, 0)
    rows_in = lambda i: (i + blk0, 0)
    mod = pl.BlockSpec((None, 1, D_MODEL), lambda i: ((i + blk0) // per_b, 0, 0))
    return pl.pallas_call(
        _outproj_kernel,
        grid=(t // tm,),
        in_specs=[pl.BlockSpec((tm, D_MODEL), rows_in),
                  pl.BlockSpec((tm, D_GRP), rows_in),
                  pl.BlockSpec((tm, D_GRP), rows_in),
                  mod, mod, mod,
                  pl.BlockSpec((1, D_MODEL), const),
                  pl.BlockSpec((D_GRP, D_MODEL), const),
                  pl.BlockSpec((D_GRP, D_MODEL), const),
                  pl.BlockSpec((D_MODEL, LANES), const),
                  pl.BlockSpec((D_MODEL, LANES), const),
                  pl.BlockSpec((1, LANES), const)],
        out_specs=[pl.BlockSpec((tm, D_MODEL), rows),
                   pl.BlockSpec((tm, D_MODEL), rows),
                   pl.BlockSpec((tm, LANES), rows),
                   pl.BlockSpec((tm, LANES), rows),
                   pl.BlockSpec((tm, LANES), rows),
                   pl.BlockSpec((1, LANES), const)],
        out_shape=[jax.ShapeDtypeStruct((t, D_MODEL), F32),
                   jax.ShapeDtypeStruct((t, D_MODEL), F32),
                   jax.ShapeDtypeStruct((t, LANES), jnp.int32),
                   jax.ShapeDtypeStruct((t, LANES), F32),
                   jax.ShapeDtypeStruct((t, LANES), jnp.int32),
                   jax.ShapeDtypeStruct((1, LANES), F32)],
        scratch_shapes=[pltpu.VMEM((1, LANES), F32)],
        compiler_params=pltpu.CompilerParams(
            dimension_semantics=("arbitrary",), vmem_limit_bytes=VMEM_LIMIT),
        name="outproj",
    )(x2d, y_r, y_f, gate1, shift2, scale2, norm_g, wo_r, wo_f, w_rt_hi, w_rt_lo, b_rt)


SC_CORES = 2
SC_SUBCORES = 16
SC_ROWS = 32


def _sc_gather_rows(idx, src):
    n_workers = SC_CORES * SC_SUBCORES
    m = idx.shape[0]
    d = src.shape[1]
    assert m % (n_workers * SC_ROWS) == 0
    n_chunks = m // (n_workers * SC_ROWS)
    mesh = plsc.VectorSubcoreMesh(core_axis_name="c", subcore_axis_name="s")

    @functools.partial(
        pl.kernel, mesh=mesh,
        out_type=jax.ShapeDtypeStruct((m, d), src.dtype),
        scratch_types=[pltpu.VMEM((n_chunks, SC_ROWS), jnp.int32),
                       pltpu.VMEM((SC_ROWS, d), src.dtype),
                       pltpu.SemaphoreType.DMA],
        name="sc_gather")
    def gather(src_hbm, idx_hbm, out_hbm, idx_v, rows_v, sem):
        wid = lax.axis_index("s") * SC_CORES + lax.axis_index("c")
        pltpu.sync_copy(idx_hbm.at[wid], idx_v)

        @pl.loop(0, n_chunks)
        def _(j):
            pltpu.async_copy(src_hbm.at[idx_v.at[j]], rows_v, sem).wait()
            pltpu.sync_copy(rows_v, out_hbm.at[pl.ds((wid * n_chunks + j) * SC_ROWS, SC_ROWS)])

    return gather(src, idx.reshape(n_workers, n_chunks, SC_ROWS))


def _sc_scatter_rows(src, dest, n_out):
    n_workers = SC_CORES * SC_SUBCORES
    t, d = src.shape
    n_slot = dest.shape[1]
    assert t % (n_workers * SC_ROWS) == 0
    n_chunks = t // (n_workers * SC_ROWS)
    mesh = plsc.VectorSubcoreMesh(core_axis_name="c", subcore_axis_name="s")
    idx = dest.reshape(n_workers, n_chunks, SC_ROWS, n_slot).transpose(0, 1, 3, 2)
    idx = idx.reshape(n_workers, n_chunks * n_slot, SC_ROWS)

    @functools.partial(
        pl.kernel, mesh=mesh,
        out_type=jax.ShapeDtypeStruct((n_out, d), src.dtype),
        scratch_types=[pltpu.VMEM((n_chunks * n_slot, SC_ROWS), jnp.int32),
                       pltpu.VMEM((SC_ROWS, d), src.dtype)],
        name="sc_scatter")
    def scatter(src_hbm, idx_hbm, out_hbm, idx_v, rows_v):
        wid = lax.axis_index("s") * SC_CORES + lax.axis_index("c")
        pltpu.sync_copy(idx_hbm.at[wid], idx_v)

        @pl.loop(0, n_chunks)
        def _(j):
            pltpu.sync_copy(src_hbm.at[pl.ds((wid * n_chunks + j) * SC_ROWS, SC_ROWS)], rows_v)
            for k in range(n_slot):
                pltpu.sync_copy(rows_v, out_hbm.at[idx_v.at[j * n_slot + k]])

    return scatter(src, idx)


def _expert_kernel(be_ref, nv_ref, x_ref, wgu_ref, bgu_ref, wd_ref, bd_ref, o_ref):
    del be_ref
    valid = _iota((EXPERT_BLOCK, 1), 0) < nv_ref[pl.program_id(0)]
    x = jnp.where(valid, x_ref[...], 0.0).astype(BF16)

    def gate_up(c):
        lo = c * EXPERT_CHUNK
        g = (jnp.dot(x, wgu_ref[:, lo:lo + EXPERT_CHUNK], preferred_element_type=F32)
             + bgu_ref[:, lo:lo + EXPERT_CHUNK])
        u = (jnp.dot(x, wgu_ref[:, D_MODEL + lo:D_MODEL + lo + EXPERT_CHUNK],
                     preferred_element_type=F32)
             + bgu_ref[:, D_MODEL + lo:D_MODEL + lo + EXPERT_CHUNK])
        return g, u

    n_chunk = D_MODEL // EXPERT_CHUNK
    acc = None
    pending = gate_up(0)
    for c in range(n_chunk):
        nxt = gate_up(c + 1) if c + 1 < n_chunk else None
        gate = jnp.minimum(pending[0], SWIGLU_LIMIT)
        up = jnp.clip(pending[1], -SWIGLU_LIMIT, SWIGLU_LIMIT)
        act = (gate * _sigmoid(SWIGLU_ALPHA * gate) * (up + 1.0)).astype(BF16)
        part = jnp.dot(act, wd_ref[c * EXPERT_CHUNK:(c + 1) * EXPERT_CHUNK, :],
                       preferred_element_type=F32)
        acc = part if acc is None else acc + part
        pending = nxt
    o_ref[...] = acc + bd_ref[...]


def _experts(block_e, n_valid, xs, w_gu, b_gu, w_d, b_d):
    n_blocks = block_e.shape[0]
    grid_spec = pltpu.PrefetchScalarGridSpec(
        num_scalar_prefetch=2,
        grid=(n_blocks,),
        in_specs=[pl.BlockSpec((EXPERT_BLOCK, D_MODEL), lambda j, be, nv: (j, 0)),
                  pl.BlockSpec((None, D_MODEL, 2 * D_MODEL), lambda j, be, nv: (be[j], 0, 0)),
                  pl.BlockSpec((None, 1, 2 * D_MODEL), lambda j, be, nv: (be[j], 0, 0)),
                  pl.BlockSpec((None, D_MODEL, D_MODEL), lambda j, be, nv: (be[j], 0, 0)),
                  pl.BlockSpec((None, 1, D_MODEL), lambda j, be, nv: (be[j], 0, 0))],
        out_specs=pl.BlockSpec((EXPERT_BLOCK, D_MODEL), lambda j, be, nv: (j, 0)),
    )
    return pl.pallas_call(
        _expert_kernel,
        grid_spec=grid_spec,
        out_shape=jax.ShapeDtypeStruct(xs.shape, F32),
        compiler_params=pltpu.CompilerParams(
            dimension_semantics=("arbitrary",), vmem_limit_bytes=VMEM_LIMIT),
        name="experts",
    )(block_e, n_valid, xs, w_gu, b_gu, w_d, b_d)


COMBINE_TOKENS = 256
MOE_SPLITS = 2


def _combine_kernel(yg_ref, x1_ref, gate_ref, g2_ref, fg_ref, o_ref):
    gates = gate_ref[...]
    acc = None
    for kk in range(TOP_K):
        part = gates[:, kk:kk + 1] * yg_ref[kk * COMBINE_TOKENS:(kk + 1) * COMBINE_TOKENS, :]
        acc = part if acc is None else acc + part
    x2 = x1_ref[...] + g2_ref[...] * acc
    o_ref[...] = x2 * lax.rsqrt(jnp.mean(x2 * x2, axis=-1, keepdims=True) + NORM_EPS) * fg_ref[...]


def _combine_kernel_into(prev_ref, *refs):
    del prev_ref
    _combine_kernel(*refs)


def _combine(yg, x1, gates, gate2, final_g, seq, row0, t_total, prev):
    t = x1.shape[0]
    tm = COMBINE_TOKENS
    per_b = seq // tm
    blk0 = row0 // tm
    rows = lambda i: (i, 0)
    in_specs = [pl.BlockSpec((TOP_K * tm, D_MODEL), rows),
                pl.BlockSpec((tm, D_MODEL), rows),
                pl.BlockSpec((tm, LANES), rows),
                pl.BlockSpec((None, 1, D_MODEL), lambda i: ((i + blk0) // per_b, 0, 0)),
                pl.BlockSpec((1, D_MODEL), lambda i: (0, 0))]
    args = (yg, x1, gates, gate2, final_g)
    if prev is not None:
        in_specs = [pl.BlockSpec(memory_space=pl.ANY)] + in_specs
        args = (prev,) + args
    return pl.pallas_call(
        _combine_kernel if prev is None else _combine_kernel_into,
        grid=(t // tm,),
        in_specs=in_specs,
        out_specs=pl.BlockSpec((tm, D_MODEL), lambda i: (i + blk0, 0)),
        out_shape=jax.ShapeDtypeStruct((t_total, D_MODEL), F32),
        input_output_aliases={} if prev is None else {0: 0},
        compiler_params=pltpu.CompilerParams(
            dimension_semantics=("parallel",), vmem_limit_bytes=VMEM_LIMIT),
        name="combine",
    )(*args)


def _moe(h2, idx, gates, rank, counts, x1, gate2, final_g, w_gu, b_gu, w_d, b_d, seq,
         row0, t_total, prev):
    t = h2.shape[0]
    n_slots = t * TOP_K
    n_blocks = -(-n_slots // EXPERT_BLOCK) + N_EXPERTS
    cap = n_blocks * EXPERT_BLOCK
    padded = (counts + EXPERT_BLOCK - 1) // EXPERT_BLOCK * EXPERT_BLOCK
    pad_ends = jnp.cumsum(padded)
    pad_starts = pad_ends - padded
    dest = pad_starts[idx] + rank
    block_starts = jnp.arange(n_blocks, dtype=jnp.int32) * EXPERT_BLOCK
    block_e = jnp.minimum(jnp.sum(block_starts[:, None] >= pad_ends[None, :], axis=1),
                          N_EXPERTS - 1).astype(jnp.int32)
    n_valid = jnp.clip(counts[block_e] - (block_starts - pad_starts[block_e]), 0, EXPERT_BLOCK)

    xs = _sc_scatter_rows(h2, dest, cap)
    yb = _experts(block_e, n_valid.astype(jnp.int32), xs, w_gu, b_gu, w_d, b_d)
    dest_blocks = dest.reshape(-1, COMBINE_TOKENS, TOP_K).transpose(0, 2, 1).reshape(-1)
    yg = _sc_gather_rows(dest_blocks, yb)
    return _combine(yg, x1, gates, gate2, final_g, seq, row0, t_total, prev)


def _layer(x, c_mod, norm1_g, w_in, mu_shift, w0, w2, a0, a2, g2, k_k, k_a, r_k, gn_w, gn_b, b_f,
           q_norm_g, k_norm_g, o_norm_g, w_out, norm2_g, w_router, b_router, w_gate_up,
           b_gate_up, w_down, b_down, final_g, tm_in, tq, tm_out):
    bsz, seq, _ = x.shape
    shift1, scale1, gate1, shift2, scale2, gate2 = (
        m.reshape(bsz, 1, D_MODEL) for m in jnp.split(c_mod, 6, axis=-1))
    row = lambda v: v.reshape(1, -1)

    w_r = w_in[:, :RWKV_COLS].astype(BF16)
    w_x = w_in[:, RWKV_COLS:RWKV_COLS + FOX_MAIN].astype(BF16)
    w_f = jnp.pad(w_in[:, RWKV_COLS + FOX_MAIN:], ((0, 0), (0, LANES - N_HEADS)))
    b_fp = jnp.pad(b_f, (0, LANES - N_HEADS)).reshape(1, LANES)
    qk_gain = jnp.concatenate([jnp.tile(q_norm_g, N_HEADS) * HEAD_DIM ** -0.5,
                               jnp.tile(k_norm_g, N_HEADS)]).reshape(1, -1)
    p_r, p_x, k_bias, q_bias = _inproj(x, shift1, scale1, row(norm1_g), w_r, w_x, w_f, b_fp,
                                       qk_gain, tm_in)

    zeros = jnp.zeros((LANES - 64, D_GRP), F32)
    w2p = jnp.concatenate([w2, zeros], axis=0).astype(BF16)
    a2p = jnp.concatenate([zeros, a2], axis=0).astype(BF16)
    y_r = _rwkv(p_r, row(mu_shift), row(w0), w2p, row(a0), a2p, g2.astype(BF16), row(k_k),
                row(k_a), row(r_k), row(gn_w), row(gn_b))

    y_f = _fox(p_x, k_bias, q_bias, jnp.tile(o_norm_g, 2).reshape(1, LANES), tq)

    t = bsz * seq
    w_rt = jnp.pad(w_router, ((0, 0), (0, LANES - N_EXPERTS)))
    b_rt = jnp.pad(b_router, (0, LANES - N_EXPERTS)).reshape(1, LANES)
    wo = w_out.astype(BF16)
    w_gu, w_d = w_gate_up.astype(BF16), w_down.astype(BF16)
    b_gu, b_d = b_gate_up.reshape(N_EXPERTS, 1, -1), b_down.reshape(N_EXPERTS, 1, -1)
    t_part = t // MOE_SPLITS
    out = None
    for part in range(MOE_SPLITS):
        row0 = part * t_part
        x1, h2, idx, gates, rank, cnt = _outproj(
            x.reshape(t, D_MODEL), y_r.reshape(t, D_GRP), y_f.reshape(t, D_GRP), gate1, shift2,
            scale2, row(norm2_g), wo[:D_GRP], wo[D_GRP:], w_rt, b_rt, tm_out, seq, row0, t_part)
        counts = cnt[0, :N_EXPERTS].astype(jnp.int32)
        out = _moe(h2, idx[:, :TOP_K], gates, rank[:, :TOP_K], counts, x1, gate2, row(final_g),
                   w_gu, b_gu, w_d, b_d, seq, row0, t, out)
    return out.reshape(bsz, seq, D_MODEL)


def kernel(x, c, w_ada, b_ada, norm1_g, w_in, mu_shift, w0, w2, a0, a2, g2, k_k, k_a, r_k, gn_w,
           gn_b, b_f, q_norm_g, k_norm_g, o_norm_g, w_out, norm2_g, w_router, b_router, w_gate_up,
           b_gate_up, w_down, b_down, final_g):
    assert w_ada.shape[0] == 1, "single-layer block"
    c_mod = _adaln(c, w_ada[0], b_ada[0])
    return _layer(x, c_mod, norm1_g[0], w_in[0], mu_shift[0], w0[0], w2[0], a0[0], a2[0], g2[0],
                  k_k[0], k_a[0], r_k[0], gn_w[0], gn_b[0], b_f[0], q_norm_g[0], k_norm_g[0],
                  o_norm_g[0], w_out[0], norm2_g[0], w_router[0], b_router[0], w_gate_up[0],
                  b_gate_up[0], w_down[0], b_down[0], final_g,
                  tm_in=min(512, x.shape[1]), tq=min(512, x.shape[1]), tm_out=min(512, x.shape[1]))
```

```python
import functools

import jax
import jax.numpy as jnp
from jax import lax
from jax.experimental import pallas as pl
from jax.experimental.pallas import tpu as pltpu
from jax.experimental.pallas import tpu_sc as plsc

F32 = jnp.float32
BF16 = jnp.bfloat16
HIGHEST = lax.Precision.HIGHEST

D_MODEL = 1024
HEAD_DIM = 64
N_HEADS = 8
D_GRP = N_HEADS * HEAD_DIM
RWKV_COLS = 1792
LORA_OFF = 3 * D_GRP
GATE_OFF = LORA_OFF + 128
FOX_MAIN = 4 * D_GRP
N_EXPERTS = 32
TOP_K = 4
EXPERT_BLOCK = 256
SWIGLU_ALPHA = 1.702
SWIGLU_LIMIT = 7.0
NORM_EPS = 1e-6
GN_EPS = 64e-5
LANES = 128
CHUNK = 64
HEADS_PER_SCAN = 4
SCAN_W = HEADS_PER_SCAN * HEAD_DIM
SEG_TERMS = 1
CUM_TERMS = 2
VMEM_LIMIT = 56 * 1024 * 1024


def _dot(a, b):
    return jnp.dot(a.astype(BF16), b.astype(BF16), preferred_element_type=F32)


def _dot_nt(a, b):
    return lax.dot_general(a.astype(BF16), b.astype(BF16), (((1,), (1,)), ((), ())),
                           preferred_element_type=F32)


def _dot_tn(a, b):
    return lax.dot_general(a.astype(BF16), b.astype(BF16), (((0,), (0,)), ((), ())),
                           preferred_element_type=F32)


def _fdot(a, b):
    return jnp.dot(a, b, precision=HIGHEST, preferred_element_type=F32)


def _split_dot(x, m, terms=2, left=False):
    acc = None
    rem = x
    for _ in range(terms):
        part = rem.astype(BF16)
        rem = rem - part.astype(F32)
        d = (jnp.dot(m, part, preferred_element_type=F32) if left
             else jnp.dot(part, m, preferred_element_type=F32))
        acc = d if acc is None else acc + d
    return acc


def _iota(shape, dim):
    return lax.broadcasted_iota(jnp.int32, shape, dim)


def _seg_reduce_mat(n):
    return (_iota((n, LANES), 0) // HEAD_DIM == _iota((n, LANES), 1)).astype(BF16)


def _seg_expand_mat(n):
    return (_iota((LANES, n), 1) // HEAD_DIM == _iota((LANES, n), 0)).astype(BF16)


def _tri(n, strict):
    r, c = _iota((n, n), 0), _iota((n, n), 1)
    return ((r > c) if strict else (r >= c)).astype(BF16)


def _log_sigmoid(z):
    return jnp.minimum(z, 0.0) - jnp.log(1.0 + jnp.exp(-jnp.abs(z)))


def _sigmoid(z):
    return 1.0 / (1.0 + jnp.exp(-z))


def _adaln_kernel(c_ref, w_ref, b_ref, o_ref):
    c = c_ref[...]
    o_ref[...] = _fdot(c * _sigmoid(c), w_ref[...]) + b_ref[...]


def _adaln(c, w_ada, b_ada):
    bsz = c.shape[0]
    n_mod = w_ada.shape[1] // D_MODEL
    return pl.pallas_call(
        _adaln_kernel,
        grid=(n_mod,),
        in_specs=[pl.BlockSpec((bsz, D_MODEL), lambda j: (0, 0)),
                  pl.BlockSpec((D_MODEL, D_MODEL), lambda j: (0, j)),
                  pl.BlockSpec((1, D_MODEL), lambda j: (0, j))],
        out_specs=pl.BlockSpec((bsz, D_MODEL), lambda j: (0, j)),
        out_shape=jax.ShapeDtypeStruct((bsz, n_mod * D_MODEL), F32),
        name="adaln",
    )(c, w_ada, b_ada.reshape(1, -1))


def _inproj_kernel(x_ref, sh_ref, sc_ref, g_ref, wr_ref, wx_ref, wfh_ref, wfl_ref, bf_ref, qkg_ref,
                   pr_ref, px_ref, kb_ref, qb_ref, carry_ref):
    @pl.when(pl.program_id(1) == 0)
    def _():
        carry_ref[...] = jnp.zeros_like(carry_ref)

    x = x_ref[...]
    tm = x.shape[0]
    h = x * lax.rsqrt(jnp.mean(x * x, axis=-1, keepdims=True) + NORM_EPS) * g_ref[...]
    h = h * (1.0 + sc_ref[...]) + sh_ref[...]
    hb = h.astype(BF16)
    h_lo = (h - hb.astype(F32)).astype(BF16)

    pr_ref[...] = jnp.dot(hb, wr_ref[...], preferred_element_type=F32).astype(BF16)

    px = jnp.dot(hb, wx_ref[...], preferred_element_type=F32)
    qk = px[:, :2 * D_GRP]
    ss = _split_dot(qk * qk, _seg_reduce_mat(2 * D_GRP), SEG_TERMS)
    inv = lax.rsqrt(ss * (1.0 / HEAD_DIM) + NORM_EPS)
    qk = qk * _split_dot(inv, _seg_expand_mat(2 * D_GRP), SEG_TERMS) * qkg_ref[...]
    px_ref[:, :2 * D_GRP] = qk.astype(BF16)
    px_ref[:, 2 * D_GRP:] = px[:, 2 * D_GRP:].astype(BF16)

    z = (jnp.dot(hb, wfh_ref[...], preferred_element_type=F32)
         + jnp.dot(h_lo, wfh_ref[...], preferred_element_type=F32)
         + jnp.dot(hb, wfl_ref[...], preferred_element_type=F32)) + bf_ref[...]
    cum = _split_dot(_log_sigmoid(z), _tri(tm, False), terms=3, left=True) + carry_ref[...]
    carry_ref[...] = cum[tm - 1:tm, :]

    src, dst = _iota((LANES, LANES), 0), _iota((LANES, LANES), 1)
    parts = []
    rem = cum
    for _ in range(3):
        part = rem.astype(BF16)
        rem = rem - part.astype(F32)
        parts.append(part)

    def spread(offset):
        return sum(jnp.dot(part, ((dst == 8 * src + offset + t) & (src < N_HEADS)).astype(BF16),
                           preferred_element_type=F32) for t, part in enumerate(parts))

    slot = _iota((1, LANES), 1) % 8
    kb_ref[...] = (jnp.where((slot >= 3) & (slot < 6), 1.0, 0.0) - spread(0)).astype(BF16)
    qb_ref[...] = (jnp.where(slot < 3, 1.0, 0.0) + spread(3)).astype(BF16)


def _inproj(x, shift, scale, g, w_r, w_x, w_f, b_f, qk_gain, tm):
    w_f_hi = w_f.astype(BF16)
    w_f_lo = (w_f - w_f_hi.astype(F32)).astype(BF16)
    bsz, seq, _ = x.shape
    const = lambda b, s: (0, 0)
    return pl.pallas_call(
        _inproj_kernel,
        grid=(bsz, seq // tm),
        in_specs=[pl.BlockSpec((None, tm, D_MODEL), lambda b, s: (b, s, 0)),
                  pl.BlockSpec((None, 1, D_MODEL), lambda b, s: (b, 0, 0)),
                  pl.BlockSpec((None, 1, D_MODEL), lambda b, s: (b, 0, 0)),
                  pl.BlockSpec((1, D_MODEL), const),
                  pl.BlockSpec((D_MODEL, RWKV_COLS), const),
                  pl.BlockSpec((D_MODEL, FOX_MAIN), const),
                  pl.BlockSpec((D_MODEL, LANES), const),
                  pl.BlockSpec((D_MODEL, LANES), const),
                  pl.BlockSpec((1, LANES), const),
                  pl.BlockSpec((1, 2 * D_GRP), const)],
        out_specs=[pl.BlockSpec((None, tm, RWKV_COLS), lambda b, s: (b, s, 0)),
                   pl.BlockSpec((None, tm, FOX_MAIN), lambda b, s: (b, s, 0)),
                   pl.BlockSpec((None, tm, LANES), lambda b, s: (b, s, 0)),
                   pl.BlockSpec((None, tm, LANES), lambda b, s: (b, s, 0))],
        out_shape=[jax.ShapeDtypeStruct((bsz, seq, RWKV_COLS), BF16),
                   jax.ShapeDtypeStruct((bsz, seq, FOX_MAIN), BF16),
                   jax.ShapeDtypeStruct((bsz, seq, LANES), BF16),
                   jax.ShapeDtypeStruct((bsz, seq, LANES), BF16)],
        scratch_shapes=[pltpu.VMEM((1, LANES), F32)],
        compiler_params=pltpu.CompilerParams(
            dimension_semantics=("parallel", "arbitrary"), vmem_limit_bytes=VMEM_LIMIT),
        name="inproj",
    )(x, shift, scale, g, w_r, w_x, w_f_hi, w_f_lo, b_f, qk_gain)


_NN = (((1,), (0,)), ((), ()))
_NT = (((1,), (1,)), ((), ()))
_TN = (((0,), (0,)), ((), ()))
SCAN_N = HEADS_PER_SCAN * CHUNK
BATCH_PER_STEP = 4
INV_LEVELS = 5
M_HEAD, M_STRICT, M_INCL, M_EYE, M_BASE, M_OFF = 0, 1, 2, 3, 4, 5


def _bdot(a, b, dims):
    return lax.dot_general(a, b, dims, preferred_element_type=F32)


def _scan_masks():
    rr, cc = _iota((SCAN_N, SCAN_W), 0), _iota((SCAN_N, SCAN_W), 1)
    ri, ci = _iota((SCAN_N, SCAN_N), 0), _iota((SCAN_N, SCAN_N), 1)
    same = ri // CHUNK == ci // CHUNK
    masks = [rr // CHUNK == cc // HEAD_DIM, same & (ri > ci), same & (ri >= ci), ri == ci,
             (ri // 2 == ci // 2) & (ri > ci)]
    blk = 2
    while blk < CHUNK:
        masks.append((ri // (2 * blk) == ci // (2 * blk)) & (ri // blk != ci // blk) & (ri > ci))
        blk *= 2
    return jnp.stack(masks).astype(BF16)


def _rwkv_kernel(p_ref, masks_ref, mu_ref, w0_ref, w2_ref, a0_ref, a2_ref, g2_ref, kk_ref, ka_ref,
                 rk_ref, gnw_ref, gnb_ref, o_ref, last_ref, state_ref):
    @pl.when(pl.program_id(1) == 0)
    def _():
        last_ref[...] = jnp.zeros_like(last_ref)
        state_ref[...] = jnp.zeros_like(state_ref)

    mu, w0, w2, a0, a2, g2, k_k, k_a, r_k, gn_w, gn_b = (
        ref[...] for ref in (mu_ref, w0_ref, w2_ref, a0_ref, a2_ref, g2_ref, kk_ref, ka_ref,
                             rk_ref, gnw_ref, gnb_ref))
    rows = BATCH_PER_STEP * CHUNK
    p = p_ref[...].astype(F32).reshape(rows, RWKV_COLS)
    row_id = _iota((rows, 1), 0)
    prev = pltpu.roll(p, 1, axis=0)
    for bb in range(BATCH_PER_STEP):
        prev = jnp.where(row_id == bb * CHUNK, last_ref[bb], prev)
        last_ref[bb] = p[(bb + 1) * CHUNK - 1:(bb + 1) * CHUNK, :]
    pf = p + mu * (prev - p)
    r = pf[:, 0:D_GRP]
    k = pf[:, D_GRP:2 * D_GRP]
    v = pf[:, 2 * D_GRP:3 * D_GRP]
    lora = pf[:, LORA_OFF:GATE_OFF]
    gd = pf[:, GATE_OFF:RWKV_COLS]

    wlog = w0 + _dot(jnp.tanh(lora), w2)
    neg = -wlog
    softplus = jnp.maximum(neg, 0.0) + jnp.log(1.0 + jnp.exp(-jnp.abs(neg)))
    ld = -jnp.exp(-softplus - 0.5)
    a = _sigmoid(a0 + _dot(lora, a2))
    g = _dot(_sigmoid(gd), g2)

    red, exp_m = _seg_reduce_mat(D_GRP), _seg_expand_mat(D_GRP)
    kk = k * k_k
    n2 = _split_dot(kk * kk, red, SEG_TERMS)
    kk = kk * _split_dot(1.0 / jnp.maximum(jnp.sqrt(n2), 1e-12), exp_m, SEG_TERMS)
    k2 = k * (1.0 + (a - 1.0) * k_a)

    tr, tc = _iota((rows, rows), 0), _iota((rows, rows), 1)
    tri = ((tr >= tc) & (tr // CHUNK == tc // CHUNK)).astype(BF16)
    cl = _split_dot(ld, tri, terms=CUM_TERMS, left=True)
    cl_end = jnp.concatenate(
        [jnp.broadcast_to(cl[(bb + 1) * CHUNK - 1:(bb + 1) * CHUNK, :], (CHUNK, D_GRP))
         for bb in range(BATCH_PER_STEP)], axis=0)
    e_in = jnp.exp(cl)
    e_out = jnp.exp(-cl)
    e_rem = jnp.exp(cl_end - cl)
    p_end = jnp.exp(cl_end)
    kka = kk * a
    ops = [(-kk * jnp.exp(cl - ld)).astype(BF16), (kka * e_out).astype(BF16),
           (k2 * e_out).astype(BF16), (r * e_in).astype(BF16), v.astype(BF16),
           (kka * e_rem).astype(BF16), (k2 * e_rem).astype(BF16)]

    chains = [(bb, grp) for bb in range(BATCH_PER_STEP)
              for grp in range(N_HEADS // HEADS_PER_SCAN)]
    head_mask = masks_ref[M_HEAD]
    strict, incl = masks_ref[M_STRICT], masks_ref[M_INCL]

    def stacked(op, bb, grp):
        part = op[bb * CHUNK:(bb + 1) * CHUNK, grp * SCAN_W:(grp + 1) * SCAN_W]
        return jnp.concatenate([part] * HEADS_PER_SCAN, axis=0) * head_mask

    xs = [[stacked(op, bb, grp) for op in ops] for bb, grp in chains]
    st = [state_ref[bb, grp] for bb, grp in chains]
    sb = [s.astype(BF16) for s in st]
    nab = [_bdot(x[0], x[1], _NT).astype(BF16) for x in xs]
    aak = [_bdot(x[0], x[2], _NT).astype(BF16) * strict for x in xs]
    arb = [_bdot(x[3], x[1], _NT).astype(BF16) * incl for x in xs]
    ark = [_bdot(x[3], x[2], _NT).astype(BF16) * incl for x in xs]
    t_inv = [masks_ref[M_EYE] + n * masks_ref[M_BASE] for n in nab]
    for lvl in range(INV_LEVELS):
        half = [_bdot(t, n * masks_ref[M_OFF + lvl], _NN).astype(BF16) for t, n in zip(t_inv, nab)]
        t_inv = [t + _bdot(h, t, _NN).astype(BF16) for t, h in zip(t_inv, half)]
    rhs = [(_bdot(x[0], s, _NT) + _bdot(k, x[4], _NN)).astype(BF16)
           for x, s, k in zip(xs, sb, aak)]
    sa = [_bdot(t, h, _NN).astype(BF16) for t, h in zip(t_inv, rhs)]
    ys = [_bdot(x[3], s, _NT) + _bdot(b, u, _NN) + _bdot(k, x[4], _NN)
          for x, s, b, u, k in zip(xs, sb, arb, sa, ark)]
    for (bb, grp), x, s, u in zip(chains, xs, st, sa):
        decay = p_end[bb * CHUNK:bb * CHUNK + 1, grp * SCAN_W:(grp + 1) * SCAN_W]
        state_ref[bb, grp] = s * decay + _bdot(u, x[5], _TN) + _bdot(x[4], x[6], _TN)
    ys = [y[0:CHUNK] + y[CHUNK:2 * CHUNK] + y[2 * CHUNK:3 * CHUNK] + y[3 * CHUNK:4 * CHUNK]
          for y in ys]
    n_grp = N_HEADS // HEADS_PER_SCAN
    y = jnp.concatenate([jnp.concatenate(ys[bb * n_grp:(bb + 1) * n_grp], axis=1)
                         for bb in range(BATCH_PER_STEP)], axis=0)

    mean = _split_dot(_split_dot(y, red, SEG_TERMS) * (1.0 / HEAD_DIM), exp_m, SEG_TERMS)
    d = y - mean
    var = _split_dot(d * d, red, SEG_TERMS) * (1.0 / HEAD_DIM)
    yn = d * _split_dot(lax.rsqrt(var + GN_EPS), exp_m, SEG_TERMS) * gn_w + gn_b
    bonus = _split_dot(_split_dot(r * k2 * r_k, red, SEG_TERMS), exp_m, SEG_TERMS) * v
    o_ref[...] = ((yn + bonus) * g).astype(BF16).reshape(BATCH_PER_STEP, CHUNK, D_GRP)


def _rwkv(p_r, mu, w0, w2p, a0, a2p, g2, k_k, k_a, r_k, gn_w, gn_b):
    bsz, seq, _ = p_r.shape
    assert bsz % BATCH_PER_STEP == 0
    masks = _scan_masks()
    const = lambda b, s: (0, 0)
    vec = pl.BlockSpec((1, D_GRP), const)
    return pl.pallas_call(
        _rwkv_kernel,
        grid=(bsz // BATCH_PER_STEP, seq // CHUNK),
        in_specs=[pl.BlockSpec((BATCH_PER_STEP, CHUNK, RWKV_COLS), lambda b, s: (b, s, 0)),
                  pl.BlockSpec(masks.shape, lambda b, s: (0, 0, 0)),
                  pl.BlockSpec((1, RWKV_COLS), const),
                  vec, pl.BlockSpec((LANES, D_GRP), const),
                  vec, pl.BlockSpec((LANES, D_GRP), const),
                  pl.BlockSpec((LANES, D_GRP), const),
                  vec, vec, vec, vec, vec],
        out_specs=pl.BlockSpec((BATCH_PER_STEP, CHUNK, D_GRP), lambda b, s: (b, s, 0)),
        out_shape=jax.ShapeDtypeStruct((bsz, seq, D_GRP), BF16),
        scratch_shapes=[pltpu.VMEM((BATCH_PER_STEP, 1, RWKV_COLS), F32),
                        pltpu.VMEM((BATCH_PER_STEP, N_HEADS // HEADS_PER_SCAN, SCAN_W, SCAN_W), F32)],
        compiler_params=pltpu.CompilerParams(
            dimension_semantics=("parallel", "arbitrary"), vmem_limit_bytes=VMEM_LIMIT),
        name="rwkv",
    )(p_r, masks, mu, w0, w2p, a0, a2p, g2, k_k, k_a, r_k, gn_w, gn_b)


def _fox_kernel(q_ref, qb_ref, k_ref, kb_ref, vt_ref, og_ref, ong_ref, o_ref, *, tq):
    hp = pl.program_id(1)
    qi = pl.program_id(2)
    lane = _iota((1, LANES), 1)
    q = q_ref[...]
    qb = qb_ref[...]
    zero = jnp.zeros_like(q)
    qcat = [jnp.concatenate([jnp.where(lane // HEAD_DIM == hh, q, zero),
                             jnp.where(lane // 8 == hp * 2 + hh, qb, zero)], axis=1)
            for hh in range(2)]
    key_pos = _iota((tq, tq), 0)
    qry_pos = _iota((tq, tq), 1)

    def scores(j):
        start = pl.multiple_of(j * tq, tq)
        kcat = jnp.concatenate([k_ref[pl.ds(start, tq), :], kb_ref[pl.ds(start, tq), :]], axis=1)
        return tuple(lax.dot_general(kcat, qc, _NT, preferred_element_type=F32) for qc in qcat)

    def consume(j, sts, stats, masked):
        vt = vt_ref[:, pl.ds(pl.multiple_of(j * tq, tq), tq)]
        if masked:
            sts = [jnp.where(qry_pos >= key_pos, st, -jnp.inf) for st in sts]
        m_new = [jnp.maximum(c[0], jnp.max(st, axis=0, keepdims=True)) for c, st in zip(stats, sts)]
        pts = [jnp.exp(st - m) for st, m in zip(sts, m_new)]
        pvs = [jnp.dot(vt, pt.astype(BF16), preferred_element_type=F32) for pt in pts]
        new = []
        for hh in range(2):
            m, l, acc = stats[hh]
            alpha = jnp.exp(m - m_new[hh])
            l = alpha * l + jnp.sum(pts[hh], axis=0, keepdims=True)
            acc = alpha * acc + pvs[hh][hh * HEAD_DIM:(hh + 1) * HEAD_DIM, :]
            new.append((m_new[hh], l, acc))
        return tuple(new)

    def body(j, carry):
        sts, stats = carry
        nxt = scores(j + 1)
        return nxt, consume(j, sts, stats, False)

    init = (jnp.full((1, tq), -jnp.inf, F32), jnp.zeros((1, tq), F32),
            jnp.zeros((HEAD_DIM, tq), F32))
    sts, stats = lax.fori_loop(0, qi, body, (scores(0), (init, init)))
    outs = []
    for _, l, acc in consume(qi, sts, stats, True):
        o = acc / l
        outs.append(o * lax.rsqrt(jnp.mean(o * o, axis=0, keepdims=True) + NORM_EPS))
    o = jnp.concatenate(outs, axis=0).T
    o_ref[...] = (o * ong_ref[...] * _sigmoid(og_ref[...].astype(F32))).astype(BF16)


def _fox(p_x, k_bias, q_bias, o_gain, tq):
    bsz, seq, _ = p_x.shape
    npair = N_HEADS // 2
    v_t = jnp.transpose(p_x[:, :, 2 * D_GRP:3 * D_GRP], (0, 2, 1))
    return pl.pallas_call(
        functools.partial(_fox_kernel, tq=tq),
        grid=(bsz, npair, seq // tq),
        in_specs=[pl.BlockSpec((None, tq, LANES), lambda b, h, i: (b, i, h)),
                  pl.BlockSpec((None, tq, LANES), lambda b, h, i: (b, i, 0)),
                  pl.BlockSpec((None, seq, LANES), lambda b, h, i: (b, 0, npair + h)),
                  pl.BlockSpec((None, seq, LANES), lambda b, h, i: (b, 0, 0)),
                  pl.BlockSpec((None, LANES, seq), lambda b, h, i: (b, h, 0)),
                  pl.BlockSpec((None, tq, LANES), lambda b, h, i: (b, i, 3 * npair + h)),
                  pl.BlockSpec((1, LANES), lambda b, h, i: (0, 0))],
        out_specs=pl.BlockSpec((None, tq, LANES), lambda b, h, i: (b, i, h)),
        out_shape=jax.ShapeDtypeStruct((bsz, seq, D_GRP), BF16),
        compiler_params=pltpu.CompilerParams(
            dimension_semantics=("parallel", "parallel", "arbitrary"),
            vmem_limit_bytes=VMEM_LIMIT),
        name="fox",
    )(p_x, q_bias, p_x, k_bias, v_t, p_x, o_gain)


def _outproj_kernel(x_ref, yr_ref, yf_ref, g1_ref, sh_ref, sc_ref, ng_ref, wor_ref, wof_ref,
                    wrt_ref, wrl_ref, brt_ref, x1_ref, h2_ref, idx_ref, gate_ref, rank_ref, cnt_ref,
                    carry_ref):
    @pl.when(pl.program_id(0) == 0)
    def _():
        carry_ref[...] = jnp.zeros_like(carry_ref)

    y = (jnp.dot(yr_ref[...], wor_ref[...], preferred_element_type=F32)
         + jnp.dot(yf_ref[...], wof_ref[...], preferred_element_type=F32))
    x1 = x_ref[...] + g1_ref[...] * y
    x1_ref[...] = x1
    tm = x1.shape[0]
    h = x1 * lax.rsqrt(jnp.mean(x1 * x1, axis=-1, keepdims=True) + NORM_EPS) * ng_ref[...]
    h2 = h * (1.0 + sc_ref[...]) + sh_ref[...]
    h2_ref[...] = h2

    lane = _iota((tm, LANES), 1)
    h_hi = h2.astype(BF16)
    h_lo = (h2 - h_hi.astype(F32)).astype(BF16)
    logits = (jnp.dot(h_hi, wrt_ref[...], preferred_element_type=F32)
              + jnp.dot(h_lo, wrt_ref[...], preferred_element_type=F32)
              + jnp.dot(h_hi, wrl_ref[...], preferred_element_type=F32)) + brt_ref[...]
    lg = jnp.where(lane < N_EXPERTS, logits, -jnp.inf)
    picks = []
    hot_sum = jnp.zeros((tm, LANES), F32)
    for _ in range(TOP_K):
        m = jnp.max(lg, axis=-1, keepdims=True)
        sel = jnp.min(jnp.where(lg == m, lane, LANES), axis=-1, keepdims=True)
        hot = lane == sel
        picks.append((m, sel, hot))
        hot_sum = hot_sum + hot.astype(F32)
        lg = jnp.where(hot, -jnp.inf, lg)
    es = [jnp.exp(m - picks[0][0]) for m, _, _ in picks]
    den = es[0] + es[1] + es[2] + es[3]

    before = jnp.dot(_tri(tm, True), hot_sum.astype(BF16), preferred_element_type=F32)
    before = before + carry_ref[...]
    idx_out = jnp.zeros((tm, LANES), jnp.int32)
    gate_out = jnp.zeros((tm, LANES), F32)
    rank_out = jnp.zeros((tm, LANES), jnp.int32)
    for kk, (m, sel, hot) in enumerate(picks):
        rk = jnp.sum(jnp.where(hot, before, 0.0), axis=-1, keepdims=True).astype(jnp.int32)
        idx_out = jnp.where(lane == kk, sel, idx_out)
        gate_out = jnp.where(lane == kk, es[kk] / den, gate_out)
        rank_out = jnp.where(lane == kk, rk, rank_out)
    idx_ref[...] = idx_out
    gate_ref[...] = gate_out
    rank_ref[...] = rank_out
    carry_ref[...] = carry_ref[...] + jnp.sum(hot_sum, axis=0, keepdims=True)
    cnt_ref[...] = carry_ref[...]


def _outproj(x2d, y_r, y_f, gate1, shift2, scale2, norm_g, wo_r, wo_f, w_rt, b_rt, tm, seq,
             row0, t):
    w_rt_hi = w_rt.astype(BF16)
    w_rt_lo = (w_rt - w_rt_hi.astype(F32)).astype(BF16)
    per_b = seq // tm
    blk0 = row0 // tm
    const = lambda i: (0, 0)
    rows = lambda i: (i, 0)
    rows_in = lambda i: (i + blk0, 0)
    mod = pl.BlockSpec((None, 1, D_MODEL), lambda i: ((i + blk0) // per_b, 0, 0))
    return pl.pallas_call(
        _outproj_kernel,
        grid=(t // tm,),
        in_specs=[pl.BlockSpec((tm, D_MODEL), rows_in),
                  pl.BlockSpec((tm, D_GRP), rows_in),
                  pl.BlockSpec((tm, D_GRP), rows_in),
                  mod, mod, mod,
                  pl.BlockSpec((1, D_MODEL), const),
                  pl.BlockSpec((D_GRP, D_MODEL), const),
                  pl.BlockSpec((D_GRP, D_MODEL), const),
                  pl.BlockSpec((D_MODEL, LANES), const),
                  pl.BlockSpec((D_MODEL, LANES), const),
                  pl.BlockSpec((1, LANES), const)],
        out_specs=[pl.BlockSpec((tm, D_MODEL), rows),
                   pl.BlockSpec((tm, D_MODEL), rows),
                   pl.BlockSpec((tm, LANES), rows),
                   pl.BlockSpec((tm, LANES), rows),
                   pl.BlockSpec((tm, LANES), rows),
                   pl.BlockSpec((1, LANES), const)],
        out_shape=[jax.ShapeDtypeStruct((t, D_MODEL), F32),
                   jax.ShapeDtypeStruct((t, D_MODEL), F32),
                   jax.ShapeDtypeStruct((t, LANES), jnp.int32),
                   jax.ShapeDtypeStruct((t, LANES), F32),
                   jax.ShapeDtypeStruct((t, LANES), jnp.int32),
                   jax.ShapeDtypeStruct((1, LANES), F32)],
        scratch_shapes=[pltpu.VMEM((1, LANES), F32)],
        compiler_params=pltpu.CompilerParams(
            dimension_semantics=("arbitrary",), vmem_limit_bytes=VMEM_LIMIT),
        name="outproj",
    )(x2d, y_r, y_f, gate1, shift2, scale2, norm_g, wo_r, wo_f, w_rt_hi, w_rt_lo, b_rt)


SC_CORES = 2
SC_SUBCORES = 16
SC_ROWS = 32


def _sc_gather_rows(idx, src):
    n_workers = SC_CORES * SC_SUBCORES
    m = idx.shape[0]
    d = src.shape[1]
    assert m % (n_workers * SC_ROWS) == 0
    n_chunks = m // (n_workers * SC_ROWS)
    mesh = plsc.VectorSubcoreMesh(core_axis_name="c", subcore_axis_name="s")

    @functools.partial(
        pl.kernel, mesh=mesh,
        out_type=jax.ShapeDtypeStruct((m, d), src.dtype),
        scratch_types=[pltpu.VMEM((n_chunks, SC_ROWS), jnp.int32),
                       pltpu.VMEM((SC_ROWS, d), src.dtype),
                       pltpu.SemaphoreType.DMA],
        name="sc_gather")
    def gather(src_hbm, idx_hbm, out_hbm, idx_v, rows_v, sem):
        wid = lax.axis_index("s") * SC_CORES + lax.axis_index("c")
        pltpu.sync_copy(idx_hbm.at[wid], idx_v)

        @pl.loop(0, n_chunks)
        def _(j):
            pltpu.async_copy(src_hbm.at[idx_v.at[j]], rows_v, sem).wait()
            pltpu.sync_copy(rows_v, out_hbm.at[pl.ds((wid * n_chunks + j) * SC_ROWS, SC_ROWS)])

    return gather(src, idx.reshape(n_workers, n_chunks, SC_ROWS))


def _sc_scatter_rows(src, dest, n_out):
    n_workers = SC_CORES * SC_SUBCORES
    t, d = src.shape
    n_slot = dest.shape[1]
    assert t % (n_workers * SC_ROWS) == 0
    n_chunks = t // (n_workers * SC_ROWS)
    mesh = plsc.VectorSubcoreMesh(core_axis_name="c", subcore_axis_name="s")
    idx = dest.reshape(n_workers, n_chunks, SC_ROWS, n_slot).transpose(0, 1, 3, 2)
    idx = idx.reshape(n_workers, n_chunks * n_slot, SC_ROWS)

    @functools.partial(
        pl.kernel, mesh=mesh,
        out_type=jax.ShapeDtypeStruct((n_out, d), src.dtype),
        scratch_types=[pltpu.VMEM((n_chunks * n_slot, SC_ROWS), jnp.int32),
                       pltpu.VMEM((SC_ROWS, d), src.dtype)],
        name="sc_scatter")
    def scatter(src_hbm, idx_hbm, out_hbm, idx_v, rows_v):
        wid = lax.axis_index("s") * SC_CORES + lax.axis_index("c")
        pltpu.sync_copy(idx_hbm.at[wid], idx_v)

        @pl.loop(0, n_chunks)
        def _(j):
            pltpu.sync_copy(src_hbm.at[pl.ds((wid * n_chunks + j) * SC_ROWS, SC_ROWS)], rows_v)
            for k in range(n_slot):
                pltpu.sync_copy(rows_v, out_hbm.at[idx_v.at[j * n_slot + k]])

    return scatter(src, idx)


def _expert_kernel(be_ref, nv_ref, x_ref, wgu_ref, bgu_ref, wd_ref, bd_ref, o_ref):
    del be_ref
    valid = _iota((EXPERT_BLOCK, 1), 0) < nv_ref[pl.program_id(0)]
    x = jnp.where(valid, x_ref[...], 0.0)
    gu = jnp.dot(x.astype(BF16), wgu_ref[...], preferred_element_type=F32) + bgu_ref[...]
    gate = jnp.minimum(gu[:, :D_MODEL], SWIGLU_LIMIT)
    up = jnp.clip(gu[:, D_MODEL:], -SWIGLU_LIMIT, SWIGLU_LIMIT)
    act = gate * _sigmoid(SWIGLU_ALPHA * gate) * (up + 1.0)
    o_ref[...] = jnp.dot(act.astype(BF16), wd_ref[...], preferred_element_type=F32) + bd_ref[...]


def _experts(block_e, n_valid, xs, w_gu, b_gu, w_d, b_d):
    n_blocks = block_e.shape[0]
    grid_spec = pltpu.PrefetchScalarGridSpec(
        num_scalar_prefetch=2,
        grid=(n_blocks,),
        in_specs=[pl.BlockSpec((EXPERT_BLOCK, D_MODEL), lambda j, be, nv: (j, 0)),
                  pl.BlockSpec((None, D_MODEL, 2 * D_MODEL), lambda j, be, nv: (be[j], 0, 0)),
                  pl.BlockSpec((None, 1, 2 * D_MODEL), lambda j, be, nv: (be[j], 0, 0)),
                  pl.BlockSpec((None, D_MODEL, D_MODEL), lambda j, be, nv: (be[j], 0, 0)),
                  pl.BlockSpec((None, 1, D_MODEL), lambda j, be, nv: (be[j], 0, 0))],
        out_specs=pl.BlockSpec((EXPERT_BLOCK, D_MODEL), lambda j, be, nv: (j, 0)),
    )
    return pl.pallas_call(
        _expert_kernel,
        grid_spec=grid_spec,
        out_shape=jax.ShapeDtypeStruct(xs.shape, F32),
        compiler_params=pltpu.CompilerParams(
            dimension_semantics=("arbitrary",), vmem_limit_bytes=VMEM_LIMIT),
        name="experts",
    )(block_e, n_valid, xs, w_gu, b_gu, w_d, b_d)


COMBINE_TOKENS = 256
MOE_SPLITS = 2


def _combine_kernel(yg_ref, x1_ref, gate_ref, g2_ref, fg_ref, o_ref):
    gates = gate_ref[...]
    acc = None
    for kk in range(TOP_K):
        part = gates[:, kk:kk + 1] * yg_ref[kk * COMBINE_TOKENS:(kk + 1) * COMBINE_TOKENS, :]
        acc = part if acc is None else acc + part
    x2 = x1_ref[...] + g2_ref[...] * acc
    o_ref[...] = x2 * lax.rsqrt(jnp.mean(x2 * x2, axis=-1, keepdims=True) + NORM_EPS) * fg_ref[...]


def _combine_kernel_into(prev_ref, *refs):
    del prev_ref
    _combine_kernel(*refs)


def _combine(yg, x1, gates, gate2, final_g, seq, row0, t_total, prev):
    t = x1.shape[0]
    tm = COMBINE_TOKENS
    per_b = seq // tm
    blk0 = row0 // tm
    rows = lambda i: (i, 0)
    in_specs = [pl.BlockSpec((TOP_K * tm, D_MODEL), rows),
                pl.BlockSpec((tm, D_MODEL), rows),
                pl.BlockSpec((tm, LANES), rows),
                pl.BlockSpec((None, 1, D_MODEL), lambda i: ((i + blk0) // per_b, 0, 0)),
                pl.BlockSpec((1, D_MODEL), lambda i: (0, 0))]
    args = (yg, x1, gates, gate2, final_g)
    if prev is not None:
        in_specs = [pl.BlockSpec(memory_space=pl.ANY)] + in_specs
        args = (prev,) + args
    return pl.pallas_call(
        _combine_kernel if prev is None else _combine_kernel_into,
        grid=(t // tm,),
        in_specs=in_specs,
        out_specs=pl.BlockSpec((tm, D_MODEL), lambda i: (i + blk0, 0)),
        out_shape=jax.ShapeDtypeStruct((t_total, D_MODEL), F32),
        input_output_aliases={} if prev is None else {0: 0},
        compiler_params=pltpu.CompilerParams(
            dimension_semantics=("parallel",), vmem_limit_bytes=VMEM_LIMIT),
        name="combine",
    )(*args)


def _moe(h2, idx, gates, rank, counts, x1, gate2, final_g, w_gu, b_gu, w_d, b_d, seq,
         row0, t_total, prev):
    t = h2.shape[0]
    n_slots = t * TOP_K
    n_blocks = -(-n_slots // EXPERT_BLOCK) + N_EXPERTS
    cap = n_blocks * EXPERT_BLOCK
    padded = (counts + EXPERT_BLOCK - 1) // EXPERT_BLOCK * EXPERT_BLOCK
    pad_ends = jnp.cumsum(padded)
    pad_starts = pad_ends - padded
    dest = pad_starts[idx] + rank
    block_starts = jnp.arange(n_blocks, dtype=jnp.int32) * EXPERT_BLOCK
    block_e = jnp.minimum(jnp.sum(block_starts[:, None] >= pad_ends[None, :], axis=1),
                          N_EXPERTS - 1).astype(jnp.int32)
    n_valid = jnp.clip(counts[block_e] - (block_starts - pad_starts[block_e]), 0, EXPERT_BLOCK)

    xs = _sc_scatter_rows(h2, dest, cap)
    yb = _experts(block_e, n_valid.astype(jnp.int32), xs, w_gu, b_gu, w_d, b_d)
    dest_blocks = dest.reshape(-1, COMBINE_TOKENS, TOP_K).transpose(0, 2, 1).reshape(-1)
    yg = _sc_gather_rows(dest_blocks, yb)
    return _combine(yg, x1, gates, gate2, final_g, seq, row0, t_total, prev)


def _layer(x, c_mod, norm1_g, w_in, mu_shift, w0, w2, a0, a2, g2, k_k, k_a, r_k, gn_w, gn_b, b_f,
           q_norm_g, k_norm_g, o_norm_g, w_out, norm2_g, w_router, b_router, w_gate_up,
           b_gate_up, w_down, b_down, final_g, tm_in, tq, tm_out):
    bsz, seq, _ = x.shape
    shift1, scale1, gate1, shift2, scale2, gate2 = (
        m.reshape(bsz, 1, D_MODEL) for m in jnp.split(c_mod, 6, axis=-1))
    row = lambda v: v.reshape(1, -1)

    w_r = w_in[:, :RWKV_COLS].astype(BF16)
    w_x = w_in[:, RWKV_COLS:RWKV_COLS + FOX_MAIN].astype(BF16)
    w_f = jnp.pad(w_in[:, RWKV_COLS + FOX_MAIN:], ((0, 0), (0, LANES - N_HEADS)))
    b_fp = jnp.pad(b_f, (0, LANES - N_HEADS)).reshape(1, LANES)
    qk_gain = jnp.concatenate([jnp.tile(q_norm_g, N_HEADS) * HEAD_DIM ** -0.5,
                               jnp.tile(k_norm_g, N_HEADS)]).reshape(1, -1)
    p_r, p_x, k_bias, q_bias = _inproj(x, shift1, scale1, row(norm1_g), w_r, w_x, w_f, b_fp,
                                       qk_gain, tm_in)

    zeros = jnp.zeros((LANES - 64, D_GRP), F32)
    w2p = jnp.concatenate([w2, zeros], axis=0).astype(BF16)
    a2p = jnp.concatenate([zeros, a2], axis=0).astype(BF16)
    y_r = _rwkv(p_r, row(mu_shift), row(w0), w2p, row(a0), a2p, g2.astype(BF16), row(k_k),
                row(k_a), row(r_k), row(gn_w), row(gn_b))

    y_f = _fox(p_x, k_bias, q_bias, jnp.tile(o_norm_g, 2).reshape(1, LANES), tq)

    t = bsz * seq
    w_rt = jnp.pad(w_router, ((0, 0), (0, LANES - N_EXPERTS)))
    b_rt = jnp.pad(b_router, (0, LANES - N_EXPERTS)).reshape(1, LANES)
    wo = w_out.astype(BF16)
    w_gu, w_d = w_gate_up.astype(BF16), w_down.astype(BF16)
    b_gu, b_d = b_gate_up.reshape(N_EXPERTS, 1, -1), b_down.reshape(N_EXPERTS, 1, -1)
    t_part = t // MOE_SPLITS
    out = None
    for part in range(MOE_SPLITS):
        row0 = part * t_part
        x1, h2, idx, gates, rank, cnt = _outproj(
            x.reshape(t, D_MODEL), y_r.reshape(t, D_GRP), y_f.reshape(t, D_GRP), gate1, shift2,
            scale2, row(norm2_g), wo[:D_GRP], wo[D_GRP:], w_rt, b_rt, tm_out, seq, row0, t_part)
        counts = cnt[0, :N_EXPERTS].astype(jnp.int32)
        out = _moe(h2, idx[:, :TOP_K], gates, rank[:, :TOP_K], counts, x1, gate2, row(final_g),
                   w_gu, b_gu, w_d, b_d, seq, row0, t, out)
    return out.reshape(bsz, seq, D_MODEL)


def kernel(x, c, w_ada, b_ada, norm1_g, w_in, mu_shift, w0, w2, a0, a2, g2, k_k, k_a, r_k, gn_w,
           gn_b, b_f, q_norm_g, k_norm_g, o_norm_g, w_out, norm2_g, w_router, b_router, w_gate_up,
           b_gate_up, w_down, b_down, final_g):
    assert w_ada.shape[0] == 1, "single-layer block"
    c_mod = _adaln(c, w_ada[0], b_ada[0])
    return _layer(x, c_mod, norm1_g[0], w_in[0], mu_shift[0], w0[0], w2[0], a0[0], a2[0], g2[0],
                  k_k[0], k_a[0], r_k[0], gn_w[0], gn_b[0], b_f[0], q_norm_g[0], k_norm_g[0],
                  o_norm_g[0], w_out[0], norm2_g[0], w_router[0], b_router[0], w_gate_up[0],
                  b_gate_up[0], w_down[0], b_down[0], final_g,
                  tm_in=min(512, x.shape[1]), tq=min(512, x.shape[1]), tm_out=min(512, x.shape[1]))
```

```python
import functools

import jax
import jax.numpy as jnp
from jax import lax
from jax.experimental import pallas as pl
from jax.experimental.pallas import tpu as pltpu
from jax.experimental.pallas import tpu_sc as plsc

F32 = jnp.float32
BF16 = jnp.bfloat16
HIGHEST = lax.Precision.HIGHEST

D_MODEL = 1024
HEAD_DIM = 64
N_HEADS = 8
D_GRP = N_HEADS * HEAD_DIM
RWKV_COLS = 1792
LORA_OFF = 3 * D_GRP
GATE_OFF = LORA_OFF + 128
FOX_MAIN = 4 * D_GRP
N_EXPERTS = 32
TOP_K = 4
EXPERT_BLOCK = 256
SWIGLU_ALPHA = 1.702
SWIGLU_LIMIT = 7.0
NORM_EPS = 1e-6
GN_EPS = 64e-5
LANES = 128
CHUNK = 64
FOX_SUB_KEYS = 256
HEADS_PER_SCAN = 4
SCAN_W = HEADS_PER_SCAN * HEAD_DIM
SEG_TERMS = 1
CUM_TERMS = 2
VMEM_LIMIT = 56 * 1024 * 1024


def _dot(a, b):
    return jnp.dot(a.astype(BF16), b.astype(BF16), preferred_element_type=F32)


def _dot_nt(a, b):
    return lax.dot_general(a.astype(BF16), b.astype(BF16), (((1,), (1,)), ((), ())),
                           preferred_element_type=F32)


def _dot_tn(a, b):
    return lax.dot_general(a.astype(BF16), b.astype(BF16), (((0,), (0,)), ((), ())),
                           preferred_element_type=F32)


def _fdot(a, b):
    return jnp.dot(a, b, precision=HIGHEST, preferred_element_type=F32)


def _split_dot(x, m, terms=2, left=False):
    acc = None
    rem = x
    for _ in range(terms):
        part = rem.astype(BF16)
        rem = rem - part.astype(F32)
        d = (jnp.dot(m, part, preferred_element_type=F32) if left
             else jnp.dot(part, m, preferred_element_type=F32))
        acc = d if acc is None else acc + d
    return acc


def _iota(shape, dim):
    return lax.broadcasted_iota(jnp.int32, shape, dim)


def _seg_reduce_mat(n):
    return (_iota((n, LANES), 0) // HEAD_DIM == _iota((n, LANES), 1)).astype(BF16)


def _seg_expand_mat(n):
    return (_iota((LANES, n), 1) // HEAD_DIM == _iota((LANES, n), 0)).astype(BF16)


def _tri(n, strict):
    r, c = _iota((n, n), 0), _iota((n, n), 1)
    return ((r > c) if strict else (r >= c)).astype(BF16)


def _log_sigmoid(z):
    return jnp.minimum(z, 0.0) - jnp.log(1.0 + jnp.exp(-jnp.abs(z)))


def _sigmoid(z):
    return 1.0 / (1.0 + jnp.exp(-z))


def _adaln_kernel(c_ref, w_ref, b_ref, o_ref):
    c = c_ref[...]
    o_ref[...] = _fdot(c * _sigmoid(c), w_ref[...]) + b_ref[...]


def _adaln(c, w_ada, b_ada):
    bsz = c.shape[0]
    n_mod = w_ada.shape[1] // D_MODEL
    return pl.pallas_call(
        _adaln_kernel,
        grid=(n_mod,),
        in_specs=[pl.BlockSpec((bsz, D_MODEL), lambda j: (0, 0)),
                  pl.BlockSpec((D_MODEL, D_MODEL), lambda j: (0, j)),
                  pl.BlockSpec((1, D_MODEL), lambda j: (0, j))],
        out_specs=pl.BlockSpec((bsz, D_MODEL), lambda j: (0, j)),
        out_shape=jax.ShapeDtypeStruct((bsz, n_mod * D_MODEL), F32),
        name="adaln",
    )(c, w_ada, b_ada.reshape(1, -1))


def _inproj_kernel(x_ref, sh_ref, sc_ref, g_ref, wr_ref, wx_ref, wfh_ref, wfl_ref, bf_ref, qkg_ref,
                   pr_ref, px_ref, kb_ref, qb_ref, carry_ref):
    @pl.when(pl.program_id(1) == 0)
    def _():
        carry_ref[...] = jnp.zeros_like(carry_ref)

    x = x_ref[...]
    tm = x.shape[0]
    h = x * lax.rsqrt(jnp.mean(x * x, axis=-1, keepdims=True) + NORM_EPS) * g_ref[...]
    h = h * (1.0 + sc_ref[...]) + sh_ref[...]
    hb = h.astype(BF16)
    h_lo = (h - hb.astype(F32)).astype(BF16)

    pr_ref[...] = jnp.dot(hb, wr_ref[...], preferred_element_type=F32).astype(BF16)

    px = jnp.dot(hb, wx_ref[...], preferred_element_type=F32)
    qk = px[:, :2 * D_GRP]
    ss = _split_dot(qk * qk, _seg_reduce_mat(2 * D_GRP), SEG_TERMS)
    inv = lax.rsqrt(ss * (1.0 / HEAD_DIM) + NORM_EPS)
    qk = qk * _split_dot(inv, _seg_expand_mat(2 * D_GRP), SEG_TERMS) * qkg_ref[...]
    px_ref[:, :2 * D_GRP] = qk.astype(BF16)
    px_ref[:, 2 * D_GRP:] = px[:, 2 * D_GRP:].astype(BF16)

    z = (jnp.dot(hb, wfh_ref[...], preferred_element_type=F32)
         + jnp.dot(h_lo, wfh_ref[...], preferred_element_type=F32)
         + jnp.dot(hb, wfl_ref[...], preferred_element_type=F32)) + bf_ref[...]
    cum = _split_dot(_log_sigmoid(z), _tri(tm, False), terms=3, left=True) + carry_ref[...]
    carry_ref[...] = cum[tm - 1:tm, :]

    src, dst = _iota((LANES, LANES), 0), _iota((LANES, LANES), 1)
    parts = []
    rem = cum
    for _ in range(3):
        part = rem.astype(BF16)
        rem = rem - part.astype(F32)
        parts.append(part)

    def spread(offset):
        return sum(jnp.dot(part, ((dst == 8 * src + offset + t) & (src < N_HEADS)).astype(BF16),
                           preferred_element_type=F32) for t, part in enumerate(parts))

    slot = _iota((1, LANES), 1) % 8
    kb_ref[...] = (jnp.where((slot >= 3) & (slot < 6), 1.0, 0.0) - spread(0)).astype(BF16)
    qb_ref[...] = (jnp.where(slot < 3, 1.0, 0.0) + spread(3)).astype(BF16)


def _inproj(x, shift, scale, g, w_r, w_x, w_f, b_f, qk_gain, tm):
    w_f_hi = w_f.astype(BF16)
    w_f_lo = (w_f - w_f_hi.astype(F32)).astype(BF16)
    bsz, seq, _ = x.shape
    const = lambda b, s: (0, 0)
    return pl.pallas_call(
        _inproj_kernel,
        grid=(bsz, seq // tm),
        in_specs=[pl.BlockSpec((None, tm, D_MODEL), lambda b, s: (b, s, 0)),
                  pl.BlockSpec((None, 1, D_MODEL), lambda b, s: (b, 0, 0)),
                  pl.BlockSpec((None, 1, D_MODEL), lambda b, s: (b, 0, 0)),
                  pl.BlockSpec((1, D_MODEL), const),
                  pl.BlockSpec((D_MODEL, RWKV_COLS), const),
                  pl.BlockSpec((D_MODEL, FOX_MAIN), const),
                  pl.BlockSpec((D_MODEL, LANES), const),
                  pl.BlockSpec((D_MODEL, LANES), const),
                  pl.BlockSpec((1, LANES), const),
                  pl.BlockSpec((1, 2 * D_GRP), const)],
        out_specs=[pl.BlockSpec((None, tm, RWKV_COLS), lambda b, s: (b, s, 0)),
                   pl.BlockSpec((None, tm, FOX_MAIN), lambda b, s: (b, s, 0)),
                   pl.BlockSpec((None, tm, LANES), lambda b, s: (b, s, 0)),
                   pl.BlockSpec((None, tm, LANES), lambda b, s: (b, s, 0))],
        out_shape=[jax.ShapeDtypeStruct((bsz, seq, RWKV_COLS), BF16),
                   jax.ShapeDtypeStruct((bsz, seq, FOX_MAIN), BF16),
                   jax.ShapeDtypeStruct((bsz, seq, LANES), BF16),
                   jax.ShapeDtypeStruct((bsz, seq, LANES), BF16)],
        scratch_shapes=[pltpu.VMEM((1, LANES), F32)],
        compiler_params=pltpu.CompilerParams(
            dimension_semantics=("parallel", "arbitrary"), vmem_limit_bytes=VMEM_LIMIT),
        name="inproj",
    )(x, shift, scale, g, w_r, w_x, w_f_hi, w_f_lo, b_f, qk_gain)


_NN = (((1,), (0,)), ((), ()))
_NT = (((1,), (1,)), ((), ()))
_TN = (((0,), (0,)), ((), ()))
SCAN_N = HEADS_PER_SCAN * CHUNK
BATCH_PER_STEP = 4
INV_LEVELS = 5
M_HEAD, M_STRICT, M_INCL, M_EYE, M_BASE, M_OFF = 0, 1, 2, 3, 4, 5


def _bdot(a, b, dims):
    return lax.dot_general(a, b, dims, preferred_element_type=F32)


def _scan_masks():
    rr, cc = _iota((SCAN_N, SCAN_W), 0), _iota((SCAN_N, SCAN_W), 1)
    ri, ci = _iota((SCAN_N, SCAN_N), 0), _iota((SCAN_N, SCAN_N), 1)
    same = ri // CHUNK == ci // CHUNK
    masks = [rr // CHUNK == cc // HEAD_DIM, same & (ri > ci), same & (ri >= ci), ri == ci,
             (ri // 2 == ci // 2) & (ri > ci)]
    blk = 2
    while blk < CHUNK:
        masks.append((ri // (2 * blk) == ci // (2 * blk)) & (ri // blk != ci // blk) & (ri > ci))
        blk *= 2
    return jnp.stack(masks).astype(BF16)


def _rwkv_kernel(p_ref, masks_ref, mu_ref, w0_ref, w2_ref, a0_ref, a2_ref, g2_ref, kk_ref, ka_ref,
                 rk_ref, gnw_ref, gnb_ref, o_ref, last_ref, state_ref):
    @pl.when(pl.program_id(1) == 0)
    def _():
        last_ref[...] = jnp.zeros_like(last_ref)
        state_ref[...] = jnp.zeros_like(state_ref)

    mu, w0, w2, a0, a2, g2, k_k, k_a, r_k, gn_w, gn_b = (
        ref[...] for ref in (mu_ref, w0_ref, w2_ref, a0_ref, a2_ref, g2_ref, kk_ref, ka_ref,
                             rk_ref, gnw_ref, gnb_ref))
    rows = BATCH_PER_STEP * CHUNK
    p = p_ref[...].astype(F32).reshape(rows, RWKV_COLS)
    row_id = _iota((rows, 1), 0)
    prev = pltpu.roll(p, 1, axis=0)
    for bb in range(BATCH_PER_STEP):
        prev = jnp.where(row_id == bb * CHUNK, last_ref[bb], prev)
        last_ref[bb] = p[(bb + 1) * CHUNK - 1:(bb + 1) * CHUNK, :]
    pf = p + mu * (prev - p)
    r = pf[:, 0:D_GRP]
    k = pf[:, D_GRP:2 * D_GRP]
    v = pf[:, 2 * D_GRP:3 * D_GRP]
    lora = pf[:, LORA_OFF:GATE_OFF]
    gd = pf[:, GATE_OFF:RWKV_COLS]

    wlog = w0 + _dot(jnp.tanh(lora), w2)
    neg = -wlog
    softplus = jnp.maximum(neg, 0.0) + jnp.log(1.0 + jnp.exp(-jnp.abs(neg)))
    ld = -jnp.exp(-softplus - 0.5)
    a = _sigmoid(a0 + _dot(lora, a2))
    g = _dot(_sigmoid(gd), g2)

    red, exp_m = _seg_reduce_mat(D_GRP), _seg_expand_mat(D_GRP)
    kk = k * k_k
    n2 = _split_dot(kk * kk, red, SEG_TERMS)
    kk = kk * _split_dot(1.0 / jnp.maximum(jnp.sqrt(n2), 1e-12), exp_m, SEG_TERMS)
    k2 = k * (1.0 + (a - 1.0) * k_a)

    tr, tc = _iota((rows, rows), 0), _iota((rows, rows), 1)
    tri = ((tr >= tc) & (tr // CHUNK == tc // CHUNK)).astype(BF16)
    cl = _split_dot(ld, tri, terms=CUM_TERMS, left=True)
    cl_end = jnp.concatenate(
        [jnp.broadcast_to(cl[(bb + 1) * CHUNK - 1:(bb + 1) * CHUNK, :], (CHUNK, D_GRP))
         for bb in range(BATCH_PER_STEP)], axis=0)
    e_in = jnp.exp(cl)
    e_out = jnp.exp(-cl)
    e_rem = jnp.exp(cl_end - cl)
    p_end = jnp.exp(cl_end)
    kka = kk * a
    ops = [(-kk * jnp.exp(cl - ld)).astype(BF16), (kka * e_out).astype(BF16),
           (k2 * e_out).astype(BF16), (r * e_in).astype(BF16), v.astype(BF16),
           (kka * e_rem).astype(BF16), (k2 * e_rem).astype(BF16)]

    chains = [(bb, grp) for bb in range(BATCH_PER_STEP)
              for grp in range(N_HEADS // HEADS_PER_SCAN)]
    head_mask = masks_ref[M_HEAD]
    strict, incl = masks_ref[M_STRICT], masks_ref[M_INCL]

    def stacked(op, bb, grp):
        part = op[bb * CHUNK:(bb + 1) * CHUNK, grp * SCAN_W:(grp + 1) * SCAN_W]
        return jnp.concatenate([part] * HEADS_PER_SCAN, axis=0) * head_mask

    xs = [[stacked(op, bb, grp) for op in ops] for bb, grp in chains]
    st = [state_ref[bb, grp] for bb, grp in chains]
    sb = [s.astype(BF16) for s in st]
    nab = [_bdot(x[0], x[1], _NT).astype(BF16) for x in xs]
    aak = [_bdot(x[0], x[2], _NT).astype(BF16) * strict for x in xs]
    arb = [_bdot(x[3], x[1], _NT).astype(BF16) * incl for x in xs]
    ark = [_bdot(x[3], x[2], _NT).astype(BF16) * incl for x in xs]
    t_inv = [masks_ref[M_EYE] + n * masks_ref[M_BASE] for n in nab]
    for lvl in range(INV_LEVELS):
        half = [_bdot(t, n * masks_ref[M_OFF + lvl], _NN).astype(BF16) for t, n in zip(t_inv, nab)]
        t_inv = [t + _bdot(h, t, _NN).astype(BF16) for t, h in zip(t_inv, half)]
    rhs = [(_bdot(x[0], s, _NT) + _bdot(k, x[4], _NN)).astype(BF16)
           for x, s, k in zip(xs, sb, aak)]
    sa = [_bdot(t, h, _NN).astype(BF16) for t, h in zip(t_inv, rhs)]
    ys = [_bdot(x[3], s, _NT) + _bdot(b, u, _NN) + _bdot(k, x[4], _NN)
          for x, s, b, u, k in zip(xs, sb, arb, sa, ark)]
    for (bb, grp), x, s, u in zip(chains, xs, st, sa):
        decay = p_end[bb * CHUNK:bb * CHUNK + 1, grp * SCAN_W:(grp + 1) * SCAN_W]
        state_ref[bb, grp] = s * decay + _bdot(u, x[5], _TN) + _bdot(x[4], x[6], _TN)
    ys = [y[0:CHUNK] + y[CHUNK:2 * CHUNK] + y[2 * CHUNK:3 * CHUNK] + y[3 * CHUNK:4 * CHUNK]
          for y in ys]
    n_grp = N_HEADS // HEADS_PER_SCAN
    y = jnp.concatenate([jnp.concatenate(ys[bb * n_grp:(bb + 1) * n_grp], axis=1)
                         for bb in range(BATCH_PER_STEP)], axis=0)

    mean = _split_dot(_split_dot(y, red, SEG_TERMS) * (1.0 / HEAD_DIM), exp_m, SEG_TERMS)
    d = y - mean
    var = _split_dot(d * d, red, SEG_TERMS) * (1.0 / HEAD_DIM)
    yn = d * _split_dot(lax.rsqrt(var + GN_EPS), exp_m, SEG_TERMS) * gn_w + gn_b
    bonus = _split_dot(_split_dot(r * k2 * r_k, red, SEG_TERMS), exp_m, SEG_TERMS) * v
    o_ref[...] = ((yn + bonus) * g).astype(BF16).reshape(BATCH_PER_STEP, CHUNK, D_GRP)


def _rwkv(p_r, mu, w0, w2p, a0, a2p, g2, k_k, k_a, r_k, gn_w, gn_b):
    bsz, seq, _ = p_r.shape
    assert bsz % BATCH_PER_STEP == 0
    masks = _scan_masks()
    const = lambda b, s: (0, 0)
    vec = pl.BlockSpec((1, D_GRP), const)
    return pl.pallas_call(
        _rwkv_kernel,
        grid=(bsz // BATCH_PER_STEP, seq // CHUNK),
        in_specs=[pl.BlockSpec((BATCH_PER_STEP, CHUNK, RWKV_COLS), lambda b, s: (b, s, 0)),
                  pl.BlockSpec(masks.shape, lambda b, s: (0, 0, 0)),
                  pl.BlockSpec((1, RWKV_COLS), const),
                  vec, pl.BlockSpec((LANES, D_GRP), const),
                  vec, pl.BlockSpec((LANES, D_GRP), const),
                  pl.BlockSpec((LANES, D_GRP), const),
                  vec, vec, vec, vec, vec],
        out_specs=pl.BlockSpec((BATCH_PER_STEP, CHUNK, D_GRP), lambda b, s: (b, s, 0)),
        out_shape=jax.ShapeDtypeStruct((bsz, seq, D_GRP), BF16),
        scratch_shapes=[pltpu.VMEM((BATCH_PER_STEP, 1, RWKV_COLS), F32),
                        pltpu.VMEM((BATCH_PER_STEP, N_HEADS // HEADS_PER_SCAN, SCAN_W, SCAN_W), F32)],
        compiler_params=pltpu.CompilerParams(
            dimension_semantics=("parallel", "arbitrary"), vmem_limit_bytes=VMEM_LIMIT),
        name="rwkv",
    )(p_r, masks, mu, w0, w2p, a0, a2p, g2, k_k, k_a, r_k, gn_w, gn_b)


def _fox_kernel(q_ref, qb_ref, k_ref, kb_ref, vt_ref, og_ref, ong_ref, o_ref, *, tq):
    hp = pl.program_id(1)
    qi = pl.program_id(2)
    lane = _iota((1, LANES), 1)
    q = q_ref[...]
    qb = qb_ref[...]
    zero = jnp.zeros_like(q)
    qcat = [jnp.concatenate([jnp.where(lane // HEAD_DIM == hh, q, zero),
                             jnp.where(lane // 8 == hp * 2 + hh, qb, zero)], axis=1)
            for hh in range(2)]
    FOX_KEYS = min(FOX_SUB_KEYS, tq)
    n_sub = tq // FOX_KEYS
    key_pos = _iota((FOX_KEYS, tq), 0)
    qry_pos = _iota((FOX_KEYS, tq), 1)

    def tile(j, stats, masked):
        base = j * tq
        sts = []
        for s in range(n_sub):
            start = pl.multiple_of(base + s * FOX_KEYS, FOX_KEYS)
            kcat = jnp.concatenate([k_ref[pl.ds(start, FOX_KEYS), :],
                                    kb_ref[pl.ds(start, FOX_KEYS), :]], axis=1)
            sts.append([lax.dot_general(kcat, qc, _NT, preferred_element_type=F32)
                        for qc in qcat])
        for s in range(n_sub):
            start = pl.multiple_of(base + s * FOX_KEYS, FOX_KEYS)
            vt = vt_ref[:, pl.ds(start, FOX_KEYS)]
            st = sts[s]
            if masked:
                st = [jnp.where(qry_pos >= key_pos + s * FOX_KEYS, x, -jnp.inf) for x in st]
            m_new = [jnp.maximum(c[0], jnp.max(x, axis=0, keepdims=True))
                     for c, x in zip(stats, st)]
            pts = [jnp.exp(x - m) for x, m in zip(st, m_new)]
            pvs = [jnp.dot(vt, pt.astype(BF16), preferred_element_type=F32) for pt in pts]
            new = []
            for hh in range(2):
                m, l, acc = stats[hh]
                alpha = jnp.exp(m - m_new[hh])
                l = alpha * l + jnp.sum(pts[hh], axis=0, keepdims=True)
                acc = alpha * acc + pvs[hh][hh * HEAD_DIM:(hh + 1) * HEAD_DIM, :]
                new.append((m_new[hh], l, acc))
            stats = tuple(new)
        return stats

    init = (jnp.full((1, tq), -jnp.inf, F32), jnp.zeros((1, tq), F32),
            jnp.zeros((HEAD_DIM, tq), F32))
    stats = lax.fori_loop(0, qi, functools.partial(tile, masked=False), (init, init))
    outs = []
    for _, l, acc in tile(qi, stats, True):
        o = acc / l
        outs.append(o * lax.rsqrt(jnp.mean(o * o, axis=0, keepdims=True) + NORM_EPS))
    o = jnp.concatenate(outs, axis=0).T
    o_ref[...] = (o * ong_ref[...] * _sigmoid(og_ref[...].astype(F32))).astype(BF16)


def _fox(p_x, k_bias, q_bias, o_gain, tq):
    bsz, seq, _ = p_x.shape
    npair = N_HEADS // 2
    v_t = jnp.transpose(p_x[:, :, 2 * D_GRP:3 * D_GRP], (0, 2, 1))
    return pl.pallas_call(
        functools.partial(_fox_kernel, tq=tq),
        grid=(bsz, npair, seq // tq),
        in_specs=[pl.BlockSpec((None, tq, LANES), lambda b, h, i: (b, i, h)),
                  pl.BlockSpec((None, tq, LANES), lambda b, h, i: (b, i, 0)),
                  pl.BlockSpec((None, seq, LANES), lambda b, h, i: (b, 0, npair + h)),
                  pl.BlockSpec((None, seq, LANES), lambda b, h, i: (b, 0, 0)),
                  pl.BlockSpec((None, LANES, seq), lambda b, h, i: (b, h, 0)),
                  pl.BlockSpec((None, tq, LANES), lambda b, h, i: (b, i, 3 * npair + h)),
                  pl.BlockSpec((1, LANES), lambda b, h, i: (0, 0))],
        out_specs=pl.BlockSpec((None, tq, LANES), lambda b, h, i: (b, i, h)),
        out_shape=jax.ShapeDtypeStruct((bsz, seq, D_GRP), BF16),
        compiler_params=pltpu.CompilerParams(
            dimension_semantics=("parallel", "parallel", "arbitrary"),
            vmem_limit_bytes=VMEM_LIMIT),
        name="fox",
    )(p_x, q_bias, p_x, k_bias, v_t, p_x, o_gain)


def _outproj_kernel(x_ref, yr_ref, yf_ref, g1_ref, sh_ref, sc_ref, ng_ref, wor_ref, wof_ref,
                    wrt_ref, wrl_ref, brt_ref, x1_ref, h2_ref, idx_ref, gate_ref, rank_ref, cnt_ref,
                    carry_ref):
    @pl.when(pl.program_id(0) == 0)
    def _():
        carry_ref[...] = jnp.zeros_like(carry_ref)

    y = (jnp.dot(yr_ref[...], wor_ref[...], preferred_element_type=F32)
         + jnp.dot(yf_ref[...], wof_ref[...], preferred_element_type=F32))
    x1 = x_ref[...] + g1_ref[...] * y
    x1_ref[...] = x1
    tm = x1.shape[0]
    h = x1 * lax.rsqrt(jnp.mean(x1 * x1, axis=-1, keepdims=True) + NORM_EPS) * ng_ref[...]
    h2 = h * (1.0 + sc_ref[...]) + sh_ref[...]
    h2_ref[...] = h2

    lane = _iota((tm, LANES), 1)
    h_hi = h2.astype(BF16)
    h_lo = (h2 - h_hi.astype(F32)).astype(BF16)
    logits = (jnp.dot(h_hi, wrt_ref[...], preferred_element_type=F32)
              + jnp.dot(h_lo, wrt_ref[...], preferred_element_type=F32)
              + jnp.dot(h_hi, wrl_ref[...], preferred_element_type=F32)) + brt_ref[...]
    lg = jnp.where(lane < N_EXPERTS, logits, -jnp.inf)
    picks = []
    hot_sum = jnp.zeros((tm, LANES), F32)
    for _ in range(TOP_K):
        m = jnp.max(lg, axis=-1, keepdims=True)
        sel = jnp.min(jnp.where(lg == m, lane, LANES), axis=-1, keepdims=True)
        hot = lane == sel
        picks.append((m, sel, hot))
        hot_sum = hot_sum + hot.astype(F32)
        lg = jnp.where(hot, -jnp.inf, lg)
    es = [jnp.exp(m - picks[0][0]) for m, _, _ in picks]
    den = es[0] + es[1] + es[2] + es[3]

    before = jnp.dot(_tri(tm, True), hot_sum.astype(BF16), preferred_element_type=F32)
    before = before + carry_ref[...]
    idx_out = jnp.zeros((tm, LANES), jnp.int32)
    gate_out = jnp.zeros((tm, LANES), F32)
    rank_out = jnp.zeros((tm, LANES), jnp.int32)
    for kk, (m, sel, hot) in enumerate(picks):
        rk = jnp.sum(jnp.where(hot, before, 0.0), axis=-1, keepdims=True).astype(jnp.int32)
        idx_out = jnp.where(lane == kk, sel, idx_out)
        gate_out = jnp.where(lane == kk, es[kk] / den, gate_out)
        rank_out = jnp.where(lane == kk, rk, rank_out)
    idx_ref[...] = idx_out
    gate_ref[...] = gate_out
    rank_ref[...] = rank_out
    carry_ref[...] = carry_ref[...] + jnp.sum(hot_sum, axis=0, keepdims=True)
    cnt_ref[...] = carry_ref[...]


def _outproj(x2d, y_r, y_f, gate1, shift2, scale2, norm_g, wo_r, wo_f, w_rt, b_rt, tm, seq,
             row0, t):
    w_rt_hi = w_rt.astype(BF16)
    w_rt_lo = (w_rt - w_rt_hi.astype(F32)).astype(BF16)
    per_b = seq // tm
    blk0 = row0 // tm
    const = lambda i: (0, 0)
    rows = lambda i: (i, 0)
    rows_in = lambda i: (i + blk0, 0)
    mod = pl.BlockSpec((None, 1, D_MODEL), lambda i: ((i + blk0) // per_b, 0, 0))
    return pl.pallas_call(
        _outproj_kernel,
        grid=(t // tm,),
        in_specs=[pl.BlockSpec((tm, D_MODEL), rows_in),
                  pl.BlockSpec((tm, D_GRP), rows_in),
                  pl.BlockSpec((tm, D_GRP), rows_in),
                  mod, mod, mod,
                  pl.BlockSpec((1, D_MODEL), const),
                  pl.BlockSpec((D_GRP, D_MODEL), const),
                  pl.BlockSpec((D_GRP, D_MODEL), const),
                  pl.BlockSpec((D_MODEL, LANES), const),
                  pl.BlockSpec((D_MODEL, LANES), const),
                  pl.BlockSpec((1, LANES), const)],
        out_specs=[pl.BlockSpec((tm, D_MODEL), rows),
                   pl.BlockSpec((tm, D_MODEL), rows),
                   pl.BlockSpec((tm, LANES), rows),
                   pl.BlockSpec((tm, LANES), rows),
                   pl.BlockSpec((tm, LANES), rows),
                   pl.BlockSpec((1, LANES), const)],
        out_shape=[jax.ShapeDtypeStruct((t, D_MODEL), F32),
                   jax.ShapeDtypeStruct((t, D_MODEL), F32),
                   jax.ShapeDtypeStruct((t, LANES), jnp.int32),
                   jax.ShapeDtypeStruct((t, LANES), F32),
                   jax.ShapeDtypeStruct((t, LANES), jnp.int32),
                   jax.ShapeDtypeStruct((1, LANES), F32)],
        scratch_shapes=[pltpu.VMEM((1, LANES), F32)],
        compiler_params=pltpu.CompilerParams(
            dimension_semantics=("arbitrary",), vmem_limit_bytes=VMEM_LIMIT),
        name="outproj",
    )(x2d, y_r, y_f, gate1, shift2, scale2, norm_g, wo_r, wo_f, w_rt_hi, w_rt_lo, b_rt)


SC_CORES = 2
SC_SUBCORES = 16
SC_ROWS = 32


def _sc_gather_rows(idx, src):
    n_workers = SC_CORES * SC_SUBCORES
    m = idx.shape[0]
    d = src.shape[1]
    assert m % (n_workers * SC_ROWS) == 0
    n_chunks = m // (n_workers * SC_ROWS)
    mesh = plsc.VectorSubcoreMesh(core_axis_name="c", subcore_axis_name="s")

    @functools.partial(
        pl.kernel, mesh=mesh,
        out_type=jax.ShapeDtypeStruct((m, d), src.dtype),
        scratch_types=[pltpu.VMEM((n_chunks, SC_ROWS), jnp.int32),
                       pltpu.VMEM((SC_ROWS, d), src.dtype),
                       pltpu.SemaphoreType.DMA],
        name="sc_gather")
    def gather(src_hbm, idx_hbm, out_hbm, idx_v, rows_v, sem):
        wid = lax.axis_index("s") * SC_CORES + lax.axis_index("c")
        pltpu.sync_copy(idx_hbm.at[wid], idx_v)

        @pl.loop(0, n_chunks)
        def _(j):
            pltpu.async_copy(src_hbm.at[idx_v.at[j]], rows_v, sem).wait()
            pltpu.sync_copy(rows_v, out_hbm.at[pl.ds((wid * n_chunks + j) * SC_ROWS, SC_ROWS)])

    return gather(src, idx.reshape(n_workers, n_chunks, SC_ROWS))


def _sc_scatter_rows(src, dest, n_out):
    n_workers = SC_CORES * SC_SUBCORES
    t, d = src.shape
    n_slot = dest.shape[1]
    assert t % (n_workers * SC_ROWS) == 0
    n_chunks = t // (n_workers * SC_ROWS)
    mesh = plsc.VectorSubcoreMesh(core_axis_name="c", subcore_axis_name="s")
    idx = dest.reshape(n_workers, n_chunks, SC_ROWS, n_slot).transpose(0, 1, 3, 2)
    idx = idx.reshape(n_workers, n_chunks * n_slot, SC_ROWS)

    @functools.partial(
        pl.kernel, mesh=mesh,
        out_type=jax.ShapeDtypeStruct((n_out, d), src.dtype),
        scratch_types=[pltpu.VMEM((n_chunks * n_slot, SC_ROWS), jnp.int32),
                       pltpu.VMEM((SC_ROWS, d), src.dtype)],
        name="sc_scatter")
    def scatter(src_hbm, idx_hbm, out_hbm, idx_v, rows_v):
        wid = lax.axis_index("s") * SC_CORES + lax.axis_index("c")
        pltpu.sync_copy(idx_hbm.at[wid], idx_v)

        @pl.loop(0, n_chunks)
        def _(j):
            pltpu.sync_copy(src_hbm.at[pl.ds((wid * n_chunks + j) * SC_ROWS, SC_ROWS)], rows_v)
            for k in range(n_slot):
                pltpu.sync_copy(rows_v, out_hbm.at[idx_v.at[j * n_slot + k]])

    return scatter(src, idx)


def _expert_kernel(be_ref, nv_ref, x_ref, wgu_ref, bgu_ref, wd_ref, bd_ref, o_ref):
    del be_ref
    valid = _iota((EXPERT_BLOCK, 1), 0) < nv_ref[pl.program_id(0)]
    x = jnp.where(valid, x_ref[...], 0.0)
    gu = jnp.dot(x.astype(BF16), wgu_ref[...], preferred_element_type=F32) + bgu_ref[...]
    gate = jnp.minimum(gu[:, :D_MODEL], SWIGLU_LIMIT)
    up = jnp.clip(gu[:, D_MODEL:], -SWIGLU_LIMIT, SWIGLU_LIMIT)
    act = gate * _sigmoid(SWIGLU_ALPHA * gate) * (up + 1.0)
    o_ref[...] = jnp.dot(act.astype(BF16), wd_ref[...], preferred_element_type=F32) + bd_ref[...]


def _experts(block_e, n_valid, xs, w_gu, b_gu, w_d, b_d):
    n_blocks = block_e.shape[0]
    grid_spec = pltpu.PrefetchScalarGridSpec(
        num_scalar_prefetch=2,
        grid=(n_blocks,),
        in_specs=[pl.BlockSpec((EXPERT_BLOCK, D_MODEL), lambda j, be, nv: (j, 0)),
                  pl.BlockSpec((None, D_MODEL, 2 * D_MODEL), lambda j, be, nv: (be[j], 0, 0)),
                  pl.BlockSpec((None, 1, 2 * D_MODEL), lambda j, be, nv: (be[j], 0, 0)),
                  pl.BlockSpec((None, D_MODEL, D_MODEL), lambda j, be, nv: (be[j], 0, 0)),
                  pl.BlockSpec((None, 1, D_MODEL), lambda j, be, nv: (be[j], 0, 0))],
        out_specs=pl.BlockSpec((EXPERT_BLOCK, D_MODEL), lambda j, be, nv: (j, 0)),
    )
    return pl.pallas_call(
        _expert_kernel,
        grid_spec=grid_spec,
        out_shape=jax.ShapeDtypeStruct(xs.shape, F32),
        compiler_params=pltpu.CompilerParams(
            dimension_semantics=("arbitrary",), vmem_limit_bytes=VMEM_LIMIT),
        name="experts",
    )(block_e, n_valid, xs, w_gu, b_gu, w_d, b_d)


COMBINE_TOKENS = 256
MOE_SPLITS = 2


def _combine_kernel(yg_ref, x1_ref, gate_ref, g2_ref, fg_ref, o_ref):
    gates = gate_ref[...]
    acc = None
    for kk in range(TOP_K):
        part = gates[:, kk:kk + 1] * yg_ref[kk * COMBINE_TOKENS:(kk + 1) * COMBINE_TOKENS, :]
        acc = part if acc is None else acc + part
    x2 = x1_ref[...] + g2_ref[...] * acc
    o_ref[...] = x2 * lax.rsqrt(jnp.mean(x2 * x2, axis=-1, keepdims=True) + NORM_EPS) * fg_ref[...]


def _combine_kernel_into(prev_ref, *refs):
    del prev_ref
    _combine_kernel(*refs)


def _combine(yg, x1, gates, gate2, final_g, seq, row0, t_total, prev):
    t = x1.shape[0]
    tm = COMBINE_TOKENS
    per_b = seq // tm
    blk0 = row0 // tm
    rows = lambda i: (i, 0)
    in_specs = [pl.BlockSpec((TOP_K * tm, D_MODEL), rows),
                pl.BlockSpec((tm, D_MODEL), rows),
                pl.BlockSpec((tm, LANES), rows),
                pl.BlockSpec((None, 1, D_MODEL), lambda i: ((i + blk0) // per_b, 0, 0)),
                pl.BlockSpec((1, D_MODEL), lambda i: (0, 0))]
    args = (yg, x1, gates, gate2, final_g)
    if prev is not None:
        in_specs = [pl.BlockSpec(memory_space=pl.ANY)] + in_specs
        args = (prev,) + args
    return pl.pallas_call(
        _combine_kernel if prev is None else _combine_kernel_into,
        grid=(t // tm,),
        in_specs=in_specs,
        out_specs=pl.BlockSpec((tm, D_MODEL), lambda i: (i + blk0, 0)),
        out_shape=jax.ShapeDtypeStruct((t_total, D_MODEL), F32),
        input_output_aliases={} if prev is None else {0: 0},
        compiler_params=pltpu.CompilerParams(
            dimension_semantics=("parallel",), vmem_limit_bytes=VMEM_LIMIT),
        name="combine",
    )(*args)


def _moe(h2, idx, gates, rank, counts, x1, gate2, final_g, w_gu, b_gu, w_d, b_d, seq,
         row0, t_total, prev):
    t = h2.shape[0]
    n_slots = t * TOP_K
    n_blocks = -(-n_slots // EXPERT_BLOCK) + N_EXPERTS
    cap = n_blocks * EXPERT_BLOCK
    padded = (counts + EXPERT_BLOCK - 1) // EXPERT_BLOCK * EXPERT_BLOCK
    pad_ends = jnp.cumsum(padded)
    pad_starts = pad_ends - padded
    dest = pad_starts[idx] + rank
    block_starts = jnp.arange(n_blocks, dtype=jnp.int32) * EXPERT_BLOCK
    block_e = jnp.minimum(jnp.sum(block_starts[:, None] >= pad_ends[None, :], axis=1),
                          N_EXPERTS - 1).astype(jnp.int32)
    n_valid = jnp.clip(counts[block_e] - (block_starts - pad_starts[block_e]), 0, EXPERT_BLOCK)

    xs = _sc_scatter_rows(h2, dest, cap)
    yb = _experts(block_e, n_valid.astype(jnp.int32), xs, w_gu, b_gu, w_d, b_d)
    dest_blocks = dest.reshape(-1, COMBINE_TOKENS, TOP_K).transpose(0, 2, 1).reshape(-1)
    yg = _sc_gather_rows(dest_blocks, yb)
    return _combine(yg, x1, gates, gate2, final_g, seq, row0, t_total, prev)


def _layer(x, c_mod, norm1_g, w_in, mu_shift, w0, w2, a0, a2, g2, k_k, k_a, r_k, gn_w, gn_b, b_f,
           q_norm_g, k_norm_g, o_norm_g, w_out, norm2_g, w_router, b_router, w_gate_up,
           b_gate_up, w_down, b_down, final_g, tm_in, tq, tm_out):
    bsz, seq, _ = x.shape
    shift1, scale1, gate1, shift2, scale2, gate2 = (
        m.reshape(bsz, 1, D_MODEL) for m in jnp.split(c_mod, 6, axis=-1))
    row = lambda v: v.reshape(1, -1)

    w_r = w_in[:, :RWKV_COLS].astype(BF16)
    w_x = w_in[:, RWKV_COLS:RWKV_COLS + FOX_MAIN].astype(BF16)
    w_f = jnp.pad(w_in[:, RWKV_COLS + FOX_MAIN:], ((0, 0), (0, LANES - N_HEADS)))
    b_fp = jnp.pad(b_f, (0, LANES - N_HEADS)).reshape(1, LANES)
    qk_gain = jnp.concatenate([jnp.tile(q_norm_g, N_HEADS) * HEAD_DIM ** -0.5,
                               jnp.tile(k_norm_g, N_HEADS)]).reshape(1, -1)
    p_r, p_x, k_bias, q_bias = _inproj(x, shift1, scale1, row(norm1_g), w_r, w_x, w_f, b_fp,
                                       qk_gain, tm_in)

    zeros = jnp.zeros((LANES - 64, D_GRP), F32)
    w2p = jnp.concatenate([w2, zeros], axis=0).astype(BF16)
    a2p = jnp.concatenate([zeros, a2], axis=0).astype(BF16)
    y_r = _rwkv(p_r, row(mu_shift), row(w0), w2p, row(a0), a2p, g2.astype(BF16), row(k_k),
                row(k_a), row(r_k), row(gn_w), row(gn_b))

    y_f = _fox(p_x, k_bias, q_bias, jnp.tile(o_norm_g, 2).reshape(1, LANES), tq)

    t = bsz * seq
    w_rt = jnp.pad(w_router, ((0, 0), (0, LANES - N_EXPERTS)))
    b_rt = jnp.pad(b_router, (0, LANES - N_EXPERTS)).reshape(1, LANES)
    wo = w_out.astype(BF16)
    w_gu, w_d = w_gate_up.astype(BF16), w_down.astype(BF16)
    b_gu, b_d = b_gate_up.reshape(N_EXPERTS, 1, -1), b_down.reshape(N_EXPERTS, 1, -1)
    t_part = t // MOE_SPLITS
    out = None
    for part in range(MOE_SPLITS):
        row0 = part * t_part
        x1, h2, idx, gates, rank, cnt = _outproj(
            x.reshape(t, D_MODEL), y_r.reshape(t, D_GRP), y_f.reshape(t, D_GRP), gate1, shift2,
            scale2, row(norm2_g), wo[:D_GRP], wo[D_GRP:], w_rt, b_rt, tm_out, seq, row0, t_part)
        counts = cnt[0, :N_EXPERTS].astype(jnp.int32)
        out = _moe(h2, idx[:, :TOP_K], gates, rank[:, :TOP_K], counts, x1, gate2, row(final_g),
                   w_gu, b_gu, w_d, b_d, seq, row0, t, out)
    return out.reshape(bsz, seq, D_MODEL)


def kernel(x, c, w_ada, b_ada, norm1_g, w_in, mu_shift, w0, w2, a0, a2, g2, k_k, k_a, r_k, gn_w,
           gn_b, b_f, q_norm_g, k_norm_g, o_norm_g, w_out, norm2_g, w_router, b_router, w_gate_up,
           b_gate_up, w_down, b_down, final_g):
    assert w_ada.shape[0] == 1, "single-layer block"
    c_mod = _adaln(c, w_ada[0], b_ada[0])
    return _layer(x, c_mod, norm1_g[0], w_in[0], mu_shift[0], w0[0], w2[0], a0[0], a2[0], g2[0],
                  k_k[0], k_a[0], r_k[0], gn_w[0], gn_b[0], b_f[0], q_norm_g[0], k_norm_g[0],
                  o_norm_g[0], w_out[0], norm2_g[0], w_router[0], b_router[0], w_gate_up[0],
                  b_gate_up[0], w_down[0], b_down[0], final_g,
                  tm_in=min(512, x.shape[1]), tq=min(512, x.shape[1]), tm_out=min(512, x.shape[1]))
```

```python
import functools

import jax
import jax.numpy as jnp
from jax import lax
from jax.experimental import pallas as pl
from jax.experimental.pallas import tpu as pltpu
from jax.experimental.pallas import tpu_sc as plsc

F32 = jnp.float32
BF16 = jnp.bfloat16
HIGHEST = lax.Precision.HIGHEST

D_MODEL = 1024
HEAD_DIM = 64
N_HEADS = 8
D_GRP = N_HEADS * HEAD_DIM
RWKV_COLS = 1792
LORA_OFF = 3 * D_GRP
GATE_OFF = LORA_OFF + 128
FOX_MAIN = 4 * D_GRP
N_EXPERTS = 32
TOP_K = 4
EXPERT_BLOCK = 512
SWIGLU_ALPHA = 1.702
SWIGLU_LIMIT = 7.0
NORM_EPS = 1e-6
GN_EPS = 64e-5
LANES = 128
CHUNK = 64
FOX_SUB_KEYS = 256
HEADS_PER_SCAN = 4
SCAN_W = HEADS_PER_SCAN * HEAD_DIM
SEG_TERMS = 1
CUM_TERMS = 2
VMEM_LIMIT = 56 * 1024 * 1024


def _dot(a, b):
    return jnp.dot(a.astype(BF16), b.astype(BF16), preferred_element_type=F32)


def _dot_nt(a, b):
    return lax.dot_general(a.astype(BF16), b.astype(BF16), (((1,), (1,)), ((), ())),
                           preferred_element_type=F32)


def _dot_tn(a, b):
    return lax.dot_general(a.astype(BF16), b.astype(BF16), (((0,), (0,)), ((), ())),
                           preferred_element_type=F32)


def _fdot(a, b):
    return jnp.dot(a, b, precision=HIGHEST, preferred_element_type=F32)


def _split_dot(x, m, terms=2, left=False):
    acc = None
    rem = x
    for _ in range(terms):
        part = rem.astype(BF16)
        rem = rem - part.astype(F32)
        d = (jnp.dot(m, part, preferred_element_type=F32) if left
             else jnp.dot(part, m, preferred_element_type=F32))
        acc = d if acc is None else acc + d
    return acc


def _iota(shape, dim):
    return lax.broadcasted_iota(jnp.int32, shape, dim)


def _seg_reduce_mat(n):
    return (_iota((n, LANES), 0) // HEAD_DIM == _iota((n, LANES), 1)).astype(BF16)


def _seg_expand_mat(n):
    return (_iota((LANES, n), 1) // HEAD_DIM == _iota((LANES, n), 0)).astype(BF16)


def _tri(n, strict):
    r, c = _iota((n, n), 0), _iota((n, n), 1)
    return ((r > c) if strict else (r >= c)).astype(BF16)


def _log_sigmoid(z):
    return jnp.minimum(z, 0.0) - jnp.log(1.0 + jnp.exp(-jnp.abs(z)))


def _sigmoid(z):
    return 1.0 / (1.0 + jnp.exp(-z))


def _adaln_kernel(c_ref, w_ref, b_ref, o_ref):
    c = c_ref[...]
    o_ref[...] = _fdot(c * _sigmoid(c), w_ref[...]) + b_ref[...]


def _adaln(c, w_ada, b_ada):
    bsz = c.shape[0]
    n_mod = w_ada.shape[1] // D_MODEL
    return pl.pallas_call(
        _adaln_kernel,
        grid=(n_mod,),
        in_specs=[pl.BlockSpec((bsz, D_MODEL), lambda j: (0, 0)),
                  pl.BlockSpec((D_MODEL, D_MODEL), lambda j: (0, j)),
                  pl.BlockSpec((1, D_MODEL), lambda j: (0, j))],
        out_specs=pl.BlockSpec((bsz, D_MODEL), lambda j: (0, j)),
        out_shape=jax.ShapeDtypeStruct((bsz, n_mod * D_MODEL), F32),
        name="adaln",
    )(c, w_ada, b_ada.reshape(1, -1))


def _inproj_kernel(x_ref, sh_ref, sc_ref, g_ref, wr_ref, wx_ref, wfh_ref, wfl_ref, bf_ref, qkg_ref,
                   pr_ref, px_ref, kb_ref, qb_ref, carry_ref):
    @pl.when(pl.program_id(1) == 0)
    def _():
        carry_ref[...] = jnp.zeros_like(carry_ref)

    x = x_ref[...]
    tm = x.shape[0]
    h = x * lax.rsqrt(jnp.mean(x * x, axis=-1, keepdims=True) + NORM_EPS) * g_ref[...]
    h = h * (1.0 + sc_ref[...]) + sh_ref[...]
    hb = h.astype(BF16)
    h_lo = (h - hb.astype(F32)).astype(BF16)

    pr_ref[...] = jnp.dot(hb, wr_ref[...], preferred_element_type=F32).astype(BF16)

    px = jnp.dot(hb, wx_ref[...], preferred_element_type=F32)
    qk = px[:, :2 * D_GRP]
    ss = _split_dot(qk * qk, _seg_reduce_mat(2 * D_GRP), SEG_TERMS)
    inv = lax.rsqrt(ss * (1.0 / HEAD_DIM) + NORM_EPS)
    qk = qk * _split_dot(inv, _seg_expand_mat(2 * D_GRP), SEG_TERMS) * qkg_ref[...]
    px_ref[:, :2 * D_GRP] = qk.astype(BF16)
    px_ref[:, 2 * D_GRP:] = px[:, 2 * D_GRP:].astype(BF16)

    z = (jnp.dot(hb, wfh_ref[...], preferred_element_type=F32)
         + jnp.dot(h_lo, wfh_ref[...], preferred_element_type=F32)
         + jnp.dot(hb, wfl_ref[...], preferred_element_type=F32)) + bf_ref[...]
    cum = _split_dot(_log_sigmoid(z), _tri(tm, False), terms=3, left=True) + carry_ref[...]
    carry_ref[...] = cum[tm - 1:tm, :]

    src, dst = _iota((LANES, LANES), 0), _iota((LANES, LANES), 1)
    parts = []
    rem = cum
    for _ in range(3):
        part = rem.astype(BF16)
        rem = rem - part.astype(F32)
        parts.append(part)

    def spread(offset):
        return sum(jnp.dot(part, ((dst == 8 * src + offset + t) & (src < N_HEADS)).astype(BF16),
                           preferred_element_type=F32) for t, part in enumerate(parts))

    slot = _iota((1, LANES), 1) % 8
    kb_ref[...] = (jnp.where((slot >= 3) & (slot < 6), 1.0, 0.0) - spread(0)).astype(BF16)
    qb_ref[...] = (jnp.where(slot < 3, 1.0, 0.0) + spread(3)).astype(BF16)


def _inproj(x, shift, scale, g, w_r, w_x, w_f, b_f, qk_gain, tm):
    w_f_hi = w_f.astype(BF16)
    w_f_lo = (w_f - w_f_hi.astype(F32)).astype(BF16)
    bsz, seq, _ = x.shape
    const = lambda b, s: (0, 0)
    return pl.pallas_call(
        _inproj_kernel,
        grid=(bsz, seq // tm),
        in_specs=[pl.BlockSpec((None, tm, D_MODEL), lambda b, s: (b, s, 0)),
                  pl.BlockSpec((None, 1, D_MODEL), lambda b, s: (b, 0, 0)),
                  pl.BlockSpec((None, 1, D_MODEL), lambda b, s: (b, 0, 0)),
                  pl.BlockSpec((1, D_MODEL), const),
                  pl.BlockSpec((D_MODEL, RWKV_COLS), const),
                  pl.BlockSpec((D_MODEL, FOX_MAIN), const),
                  pl.BlockSpec((D_MODEL, LANES), const),
                  pl.BlockSpec((D_MODEL, LANES), const),
                  pl.BlockSpec((1, LANES), const),
                  pl.BlockSpec((1, 2 * D_GRP), const)],
        out_specs=[pl.BlockSpec((None, tm, RWKV_COLS), lambda b, s: (b, s, 0)),
                   pl.BlockSpec((None, tm, FOX_MAIN), lambda b, s: (b, s, 0)),
                   pl.BlockSpec((None, tm, LANES), lambda b, s: (b, s, 0)),
                   pl.BlockSpec((None, tm, LANES), lambda b, s: (b, s, 0))],
        out_shape=[jax.ShapeDtypeStruct((bsz, seq, RWKV_COLS), BF16),
                   jax.ShapeDtypeStruct((bsz, seq, FOX_MAIN), BF16),
                   jax.ShapeDtypeStruct((bsz, seq, LANES), BF16),
                   jax.ShapeDtypeStruct((bsz, seq, LANES), BF16)],
        scratch_shapes=[pltpu.VMEM((1, LANES), F32)],
        compiler_params=pltpu.CompilerParams(
            dimension_semantics=("parallel", "arbitrary"), vmem_limit_bytes=VMEM_LIMIT),
        name="inproj",
    )(x, shift, scale, g, w_r, w_x, w_f_hi, w_f_lo, b_f, qk_gain)


_NN = (((1,), (0,)), ((), ()))
_NT = (((1,), (1,)), ((), ()))
_TN = (((0,), (0,)), ((), ()))
SCAN_N = HEADS_PER_SCAN * CHUNK
BATCH_PER_STEP = 4
INV_LEVELS = 5
M_HEAD, M_STRICT, M_INCL, M_EYE, M_BASE, M_OFF = 0, 1, 2, 3, 4, 5


def _bdot(a, b, dims):
    return lax.dot_general(a, b, dims, preferred_element_type=F32)


def _scan_masks():
    rr, cc = _iota((SCAN_N, SCAN_W), 0), _iota((SCAN_N, SCAN_W), 1)
    ri, ci = _iota((SCAN_N, SCAN_N), 0), _iota((SCAN_N, SCAN_N), 1)
    same = ri // CHUNK == ci // CHUNK
    masks = [rr // CHUNK == cc // HEAD_DIM, same & (ri > ci), same & (ri >= ci), ri == ci,
             (ri // 2 == ci // 2) & (ri > ci)]
    blk = 2
    while blk < CHUNK:
        masks.append((ri // (2 * blk) == ci // (2 * blk)) & (ri // blk != ci // blk) & (ri > ci))
        blk *= 2
    return jnp.stack(masks).astype(BF16)


def _rwkv_kernel(p_ref, masks_ref, mu_ref, w0_ref, w2_ref, a0_ref, a2_ref, g2_ref, kk_ref, ka_ref,
                 rk_ref, gnw_ref, gnb_ref, o_ref, last_ref, state_ref):
    @pl.when(pl.program_id(1) == 0)
    def _():
        last_ref[...] = jnp.zeros_like(last_ref)
        state_ref[...] = jnp.zeros_like(state_ref)

    mu, w0, w2, a0, a2, g2, k_k, k_a, r_k, gn_w, gn_b = (
        ref[...] for ref in (mu_ref, w0_ref, w2_ref, a0_ref, a2_ref, g2_ref, kk_ref, ka_ref,
                             rk_ref, gnw_ref, gnb_ref))
    rows = BATCH_PER_STEP * CHUNK
    p = p_ref[...].astype(F32).reshape(rows, RWKV_COLS)
    row_id = _iota((rows, 1), 0)
    prev = pltpu.roll(p, 1, axis=0)
    for bb in range(BATCH_PER_STEP):
        prev = jnp.where(row_id == bb * CHUNK, last_ref[bb], prev)
        last_ref[bb] = p[(bb + 1) * CHUNK - 1:(bb + 1) * CHUNK, :]
    pf = p + mu * (prev - p)
    r = pf[:, 0:D_GRP]
    k = pf[:, D_GRP:2 * D_GRP]
    v = pf[:, 2 * D_GRP:3 * D_GRP]
    lora = pf[:, LORA_OFF:GATE_OFF]
    gd = pf[:, GATE_OFF:RWKV_COLS]

    wlog = w0 + _dot(jnp.tanh(lora), w2)
    neg = -wlog
    softplus = jnp.maximum(neg, 0.0) + jnp.log(1.0 + jnp.exp(-jnp.abs(neg)))
    ld = -jnp.exp(-softplus - 0.5)
    a = _sigmoid(a0 + _dot(lora, a2))
    g = _dot(_sigmoid(gd), g2)

    red, exp_m = _seg_reduce_mat(D_GRP), _seg_expand_mat(D_GRP)
    kk = k * k_k
    n2 = _split_dot(kk * kk, red, SEG_TERMS)
    kk = kk * _split_dot(1.0 / jnp.maximum(jnp.sqrt(n2), 1e-12), exp_m, SEG_TERMS)
    k2 = k * (1.0 + (a - 1.0) * k_a)

    tr, tc = _iota((rows, rows), 0), _iota((rows, rows), 1)
    tri = ((tr >= tc) & (tr // CHUNK == tc // CHUNK)).astype(BF16)
    cl = _split_dot(ld, tri, terms=CUM_TERMS, left=True)
    cl_end = jnp.concatenate(
        [jnp.broadcast_to(cl[(bb + 1) * CHUNK - 1:(bb + 1) * CHUNK, :], (CHUNK, D_GRP))
         for bb in range(BATCH_PER_STEP)], axis=0)
    e_in = jnp.exp(cl)
    e_out = jnp.exp(-cl)
    e_rem = jnp.exp(cl_end - cl)
    p_end = jnp.exp(cl_end)
    kka = kk * a
    ops = [(-kk * jnp.exp(cl - ld)).astype(BF16), (kka * e_out).astype(BF16),
           (k2 * e_out).astype(BF16), (r * e_in).astype(BF16), v.astype(BF16),
           (kka * e_rem).astype(BF16), (k2 * e_rem).astype(BF16)]

    chains = [(bb, grp) for bb in range(BATCH_PER_STEP)
              for grp in range(N_HEADS // HEADS_PER_SCAN)]
    head_mask = masks_ref[M_HEAD]
    strict, incl = masks_ref[M_STRICT], masks_ref[M_INCL]

    def stacked(op, bb, grp):
        part = op[bb * CHUNK:(bb + 1) * CHUNK, grp * SCAN_W:(grp + 1) * SCAN_W]
        return jnp.concatenate([part] * HEADS_PER_SCAN, axis=0) * head_mask

    xs = [[stacked(op, bb, grp) for op in ops] for bb, grp in chains]
    st = [state_ref[bb, grp] for bb, grp in chains]
    sb = [s.astype(BF16) for s in st]
    nab = [_bdot(x[0], x[1], _NT).astype(BF16) for x in xs]
    aak = [_bdot(x[0], x[2], _NT).astype(BF16) * strict for x in xs]
    arb = [_bdot(x[3], x[1], _NT).astype(BF16) * incl for x in xs]
    ark = [_bdot(x[3], x[2], _NT).astype(BF16) * incl for x in xs]
    t_inv = [masks_ref[M_EYE] + n * masks_ref[M_BASE] for n in nab]
    for lvl in range(INV_LEVELS):
        half = [_bdot(t, n * masks_ref[M_OFF + lvl], _NN).astype(BF16) for t, n in zip(t_inv, nab)]
        t_inv = [t + _bdot(h, t, _NN).astype(BF16) for t, h in zip(t_inv, half)]
    rhs = [(_bdot(x[0], s, _NT) + _bdot(k, x[4], _NN)).astype(BF16)
           for x, s, k in zip(xs, sb, aak)]
    sa = [_bdot(t, h, _NN).astype(BF16) for t, h in zip(t_inv, rhs)]
    ys = [_bdot(x[3], s, _NT) + _bdot(b, u, _NN) + _bdot(k, x[4], _NN)
          for x, s, b, u, k in zip(xs, sb, arb, sa, ark)]
    for (bb, grp), x, s, u in zip(chains, xs, st, sa):
        decay = p_end[bb * CHUNK:bb * CHUNK + 1, grp * SCAN_W:(grp + 1) * SCAN_W]
        state_ref[bb, grp] = s * decay + _bdot(u, x[5], _TN) + _bdot(x[4], x[6], _TN)
    ys = [y[0:CHUNK] + y[CHUNK:2 * CHUNK] + y[2 * CHUNK:3 * CHUNK] + y[3 * CHUNK:4 * CHUNK]
          for y in ys]
    n_grp = N_HEADS // HEADS_PER_SCAN
    y = jnp.concatenate([jnp.concatenate(ys[bb * n_grp:(bb + 1) * n_grp], axis=1)
                         for bb in range(BATCH_PER_STEP)], axis=0)

    mean = _split_dot(_split_dot(y, red, SEG_TERMS) * (1.0 / HEAD_DIM), exp_m, SEG_TERMS)
    d = y - mean
    var = _split_dot(d * d, red, SEG_TERMS) * (1.0 / HEAD_DIM)
    yn = d * _split_dot(lax.rsqrt(var + GN_EPS), exp_m, SEG_TERMS) * gn_w + gn_b
    bonus = _split_dot(_split_dot(r * k2 * r_k, red, SEG_TERMS), exp_m, SEG_TERMS) * v
    o_ref[...] = ((yn + bonus) * g).astype(BF16).reshape(BATCH_PER_STEP, CHUNK, D_GRP)


def _rwkv(p_r, mu, w0, w2p, a0, a2p, g2, k_k, k_a, r_k, gn_w, gn_b):
    bsz, seq, _ = p_r.shape
    assert bsz % BATCH_PER_STEP == 0
    masks = _scan_masks()
    const = lambda b, s: (0, 0)
    vec = pl.BlockSpec((1, D_GRP), const)
    return pl.pallas_call(
        _rwkv_kernel,
        grid=(bsz // BATCH_PER_STEP, seq // CHUNK),
        in_specs=[pl.BlockSpec((BATCH_PER_STEP, CHUNK, RWKV_COLS), lambda b, s: (b, s, 0)),
                  pl.BlockSpec(masks.shape, lambda b, s: (0, 0, 0)),
                  pl.BlockSpec((1, RWKV_COLS), const),
                  vec, pl.BlockSpec((LANES, D_GRP), const),
                  vec, pl.BlockSpec((LANES, D_GRP), const),
                  pl.BlockSpec((LANES, D_GRP), const),
                  vec, vec, vec, vec, vec],
        out_specs=pl.BlockSpec((BATCH_PER_STEP, CHUNK, D_GRP), lambda b, s: (b, s, 0)),
        out_shape=jax.ShapeDtypeStruct((bsz, seq, D_GRP), BF16),
        scratch_shapes=[pltpu.VMEM((BATCH_PER_STEP, 1, RWKV_COLS), F32),
                        pltpu.VMEM((BATCH_PER_STEP, N_HEADS // HEADS_PER_SCAN, SCAN_W, SCAN_W), F32)],
        compiler_params=pltpu.CompilerParams(
            dimension_semantics=("parallel", "arbitrary"), vmem_limit_bytes=VMEM_LIMIT),
        name="rwkv",
    )(p_r, masks, mu, w0, w2p, a0, a2p, g2, k_k, k_a, r_k, gn_w, gn_b)


def _fox_kernel(q_ref, qb_ref, k_ref, kb_ref, vt_ref, og_ref, ong_ref, o_ref, *, tq):
    hp = pl.program_id(1)
    qi = pl.program_id(2)
    lane = _iota((1, LANES), 1)
    q = q_ref[...]
    qb = qb_ref[...]
    zero = jnp.zeros_like(q)
    qcat = [jnp.concatenate([jnp.where(lane // HEAD_DIM == hh, q, zero),
                             jnp.where(lane // 8 == hp * 2 + hh, qb, zero)], axis=1)
            for hh in range(2)]
    FOX_KEYS = min(FOX_SUB_KEYS, tq)
    n_sub = tq // FOX_KEYS
    key_pos = _iota((FOX_KEYS, tq), 0)
    qry_pos = _iota((FOX_KEYS, tq), 1)

    def tile(j, stats, masked):
        base = j * tq
        sts = []
        for s in range(n_sub):
            start = pl.multiple_of(base + s * FOX_KEYS, FOX_KEYS)
            kcat = jnp.concatenate([k_ref[pl.ds(start, FOX_KEYS), :],
                                    kb_ref[pl.ds(start, FOX_KEYS), :]], axis=1)
            sts.append([lax.dot_general(kcat, qc, _NT, preferred_element_type=F32)
                        for qc in qcat])
        for s in range(n_sub):
            start = pl.multiple_of(base + s * FOX_KEYS, FOX_KEYS)
            vt = vt_ref[:, pl.ds(start, FOX_KEYS)]
            st = sts[s]
            if masked:
                st = [jnp.where(qry_pos >= key_pos + s * FOX_KEYS, x, -jnp.inf) for x in st]
            m_new = [jnp.maximum(c[0], jnp.max(x, axis=0, keepdims=True))
                     for c, x in zip(stats, st)]
            pts = [jnp.exp(x - m) for x, m in zip(st, m_new)]
            pvs = [jnp.dot(vt, pt.astype(BF16), preferred_element_type=F32) for pt in pts]
            new = []
            for hh in range(2):
                m, l, acc = stats[hh]
                alpha = jnp.exp(m - m_new[hh])
                l = alpha * l + jnp.sum(pts[hh], axis=0, keepdims=True)
                acc = alpha * acc + pvs[hh][hh * HEAD_DIM:(hh + 1) * HEAD_DIM, :]
                new.append((m_new[hh], l, acc))
            stats = tuple(new)
        return stats

    init = (jnp.full((1, tq), -jnp.inf, F32), jnp.zeros((1, tq), F32),
            jnp.zeros((HEAD_DIM, tq), F32))
    stats = lax.fori_loop(0, qi, functools.partial(tile, masked=False), (init, init))
    outs = []
    for _, l, acc in tile(qi, stats, True):
        o = acc / l
        outs.append(o * lax.rsqrt(jnp.mean(o * o, axis=0, keepdims=True) + NORM_EPS))
    o = jnp.concatenate(outs, axis=0).T
    o_ref[...] = (o * ong_ref[...] * _sigmoid(og_ref[...].astype(F32))).astype(BF16)


def _fox(p_x, k_bias, q_bias, o_gain, tq):
    bsz, seq, _ = p_x.shape
    npair = N_HEADS // 2
    v_t = jnp.transpose(p_x[:, :, 2 * D_GRP:3 * D_GRP], (0, 2, 1))
    return pl.pallas_call(
        functools.partial(_fox_kernel, tq=tq),
        grid=(bsz, npair, seq // tq),
        in_specs=[pl.BlockSpec((None, tq, LANES), lambda b, h, i: (b, i, h)),
                  pl.BlockSpec((None, tq, LANES), lambda b, h, i: (b, i, 0)),
                  pl.BlockSpec((None, seq, LANES), lambda b, h, i: (b, 0, npair + h)),
                  pl.BlockSpec((None, seq, LANES), lambda b, h, i: (b, 0, 0)),
                  pl.BlockSpec((None, LANES, seq), lambda b, h, i: (b, h, 0)),
                  pl.BlockSpec((None, tq, LANES), lambda b, h, i: (b, i, 3 * npair + h)),
                  pl.BlockSpec((1, LANES), lambda b, h, i: (0, 0))],
        out_specs=pl.BlockSpec((None, tq, LANES), lambda b, h, i: (b, i, h)),
        out_shape=jax.ShapeDtypeStruct((bsz, seq, D_GRP), BF16),
        compiler_params=pltpu.CompilerParams(
            dimension_semantics=("parallel", "parallel", "arbitrary"),
            vmem_limit_bytes=VMEM_LIMIT),
        name="fox",
    )(p_x, q_bias, p_x, k_bias, v_t, p_x, o_gain)


def _outproj_kernel(x_ref, yr_ref, yf_ref, g1_ref, sh_ref, sc_ref, ng_ref, wor_ref, wof_ref,
                    wrt_ref, wrl_ref, brt_ref, x1_ref, h2_ref, idx_ref, gate_ref, rank_ref, cnt_ref,
                    carry_ref):
    @pl.when(pl.program_id(0) == 0)
    def _():
        carry_ref[...] = jnp.zeros_like(carry_ref)

    y = (jnp.dot(yr_ref[...], wor_ref[...], preferred_element_type=F32)
         + jnp.dot(yf_ref[...], wof_ref[...], preferred_element_type=F32))
    x1 = x_ref[...] + g1_ref[...] * y
    x1_ref[...] = x1
    tm = x1.shape[0]
    h = x1 * lax.rsqrt(jnp.mean(x1 * x1, axis=-1, keepdims=True) + NORM_EPS) * ng_ref[...]
    h2 = h * (1.0 + sc_ref[...]) + sh_ref[...]
    h2_ref[...] = h2

    lane = _iota((tm, LANES), 1)
    h_hi = h2.astype(BF16)
    h_lo = (h2 - h_hi.astype(F32)).astype(BF16)
    logits = (jnp.dot(h_hi, wrt_ref[...], preferred_element_type=F32)
              + jnp.dot(h_lo, wrt_ref[...], preferred_element_type=F32)
              + jnp.dot(h_hi, wrl_ref[...], preferred_element_type=F32)) + brt_ref[...]
    lg = jnp.where(lane < N_EXPERTS, logits, -jnp.inf)
    picks = []
    hot_sum = jnp.zeros((tm, LANES), F32)
    for _ in range(TOP_K):
        m = jnp.max(lg, axis=-1, keepdims=True)
        sel = jnp.min(jnp.where(lg == m, lane, LANES), axis=-1, keepdims=True)
        hot = lane == sel
        picks.append((m, sel, hot))
        hot_sum = hot_sum + hot.astype(F32)
        lg = jnp.where(hot, -jnp.inf, lg)
    es = [jnp.exp(m - picks[0][0]) for m, _, _ in picks]
    den = es[0] + es[1] + es[2] + es[3]

    before = jnp.dot(_tri(tm, True), hot_sum.astype(BF16), preferred_element_type=F32)
    before = before + carry_ref[...]
    idx_out = jnp.zeros((tm, LANES), jnp.int32)
    gate_out = jnp.zeros((tm, LANES), F32)
    rank_out = jnp.zeros((tm, LANES), jnp.int32)
    for kk, (m, sel, hot) in enumerate(picks):
        rk = jnp.sum(jnp.where(hot, before, 0.0), axis=-1, keepdims=True).astype(jnp.int32)
        idx_out = jnp.where(lane == kk, sel, idx_out)
        gate_out = jnp.where(lane == kk, es[kk] / den, gate_out)
        rank_out = jnp.where(lane == kk, rk, rank_out)
    idx_ref[...] = idx_out
    gate_ref[...] = gate_out
    rank_ref[...] = rank_out
    carry_ref[...] = carry_ref[...] + jnp.sum(hot_sum, axis=0, keepdims=True)
    cnt_ref[...] = carry_ref[...]


def _outproj(x2d, y_r, y_f, gate1, shift2, scale2, norm_g, wo_r, wo_f, w_rt, b_rt, tm, seq,
             row0, t):
    w_rt_hi = w_rt.astype(BF16)
    w_rt_lo = (w_rt - w_rt_hi.astype(F32)).astype(BF16)
    per_b = seq // tm
    blk0 = row0 // tm
    const = lambda i: (0, 0)
    rows = lambda i: (i, 0)
    rows_in = lambda i: (i + blk0, 0)
    mod = pl.BlockSpec((None, 1, D_MODEL), lambda i: ((i + blk0) // per_b, 0, 0))
    return pl.pallas_call(
        _outproj_kernel,
        grid=(t // tm,),
        in_specs=[pl.BlockSpec((tm, D_MODEL), rows_in),
                  pl.BlockSpec((tm, D_GRP), rows_in),
                  pl.BlockSpec((tm, D_GRP), rows_in),
                  mod, mod, mod,
                  pl.BlockSpec((1, D_MODEL), const),
                  pl.BlockSpec((D_GRP, D_MODEL), const),
                  pl.BlockSpec((D_GRP, D_MODEL), const),
                  pl.BlockSpec((D_MODEL, LANES), const),
                  pl.BlockSpec((D_MODEL, LANES), const),
                  pl.BlockSpec((1, LANES), const)],
        out_specs=[pl.BlockSpec((tm, D_MODEL), rows),
                   pl.BlockSpec((tm, D_MODEL), rows),
                   pl.BlockSpec((tm, LANES), rows),
                   pl.BlockSpec((tm, LANES), rows),
                   pl.BlockSpec((tm, LANES), rows),
                   pl.BlockSpec((1, LANES), const)],
        out_shape=[jax.ShapeDtypeStruct((t, D_MODEL), F32),
                   jax.ShapeDtypeStruct((t, D_MODEL), F32),
                   jax.ShapeDtypeStruct((t, LANES), jnp.int32),
                   jax.ShapeDtypeStruct((t, LANES), F32),
                   jax.ShapeDtypeStruct((t, LANES), jnp.int32),
                   jax.ShapeDtypeStruct((1, LANES), F32)],
        scratch_shapes=[pltpu.VMEM((1, LANES), F32)],
        compiler_params=pltpu.CompilerParams(
            dimension_semantics=("arbitrary",), vmem_limit_bytes=VMEM_LIMIT),
        name="outproj",
    )(x2d, y_r, y_f, gate1, shift2, scale2, norm_g, wo_r, wo_f, w_rt_hi, w_rt_lo, b_rt)


SC_CORES = 2
SC_SUBCORES = 16
SC_ROWS = 32


def _sc_gather_rows(idx, src):
    n_workers = SC_CORES * SC_SUBCORES
    m = idx.shape[0]
    d = src.shape[1]
    assert m % (n_workers * SC_ROWS) == 0
    n_chunks = m // (n_workers * SC_ROWS)
    mesh = plsc.VectorSubcoreMesh(core_axis_name="c", subcore_axis_name="s")

    @functools.partial(
        pl.kernel, mesh=mesh,
        out_type=jax.ShapeDtypeStruct((m, d), src.dtype),
        scratch_types=[pltpu.VMEM((n_chunks, SC_ROWS), jnp.int32),
                       pltpu.VMEM((SC_ROWS, d), src.dtype),
                       pltpu.SemaphoreType.DMA],
        name="sc_gather")
    def gather(src_hbm, idx_hbm, out_hbm, idx_v, rows_v, sem):
        wid = lax.axis_index("s") * SC_CORES + lax.axis_index("c")
        pltpu.sync_copy(idx_hbm.at[wid], idx_v)

        @pl.loop(0, n_chunks)
        def _(j):
            pltpu.async_copy(src_hbm.at[idx_v.at[j]], rows_v, sem).wait()
            pltpu.sync_copy(rows_v, out_hbm.at[pl.ds((wid * n_chunks + j) * SC_ROWS, SC_ROWS)])

    return gather(src, idx.reshape(n_workers, n_chunks, SC_ROWS))


def _sc_scatter_rows(src, dest, n_out):
    n_workers = SC_CORES * SC_SUBCORES
    t, d = src.shape
    n_slot = dest.shape[1]
    assert t % (n_workers * SC_ROWS) == 0
    n_chunks = t // (n_workers * SC_ROWS)
    mesh = plsc.VectorSubcoreMesh(core_axis_name="c", subcore_axis_name="s")
    idx = dest.reshape(n_workers, n_chunks, SC_ROWS, n_slot).transpose(0, 1, 3, 2)
    idx = idx.reshape(n_workers, n_chunks * n_slot, SC_ROWS)

    @functools.partial(
        pl.kernel, mesh=mesh,
        out_type=jax.ShapeDtypeStruct((n_out, d), src.dtype),
        scratch_types=[pltpu.VMEM((n_chunks * n_slot, SC_ROWS), jnp.int32),
                       pltpu.VMEM((SC_ROWS, d), src.dtype)],
        name="sc_scatter")
    def scatter(src_hbm, idx_hbm, out_hbm, idx_v, rows_v):
        wid = lax.axis_index("s") * SC_CORES + lax.axis_index("c")
        pltpu.sync_copy(idx_hbm.at[wid], idx_v)

        @pl.loop(0, n_chunks)
        def _(j):
            pltpu.sync_copy(src_hbm.at[pl.ds((wid * n_chunks + j) * SC_ROWS, SC_ROWS)], rows_v)
            for k in range(n_slot):
                pltpu.sync_copy(rows_v, out_hbm.at[idx_v.at[j * n_slot + k]])

    return scatter(src, idx)


def _expert_kernel(be_ref, nv_ref, x_ref, wgu_ref, bgu_ref, wd_ref, bd_ref, o_ref):
    del be_ref
    valid = _iota((EXPERT_BLOCK, 1), 0) < nv_ref[pl.program_id(0)]
    x = jnp.where(valid, x_ref[...], 0.0)
    gu = jnp.dot(x.astype(BF16), wgu_ref[...], preferred_element_type=F32) + bgu_ref[...]
    gate = jnp.minimum(gu[:, :D_MODEL], SWIGLU_LIMIT)
    up = jnp.clip(gu[:, D_MODEL:], -SWIGLU_LIMIT, SWIGLU_LIMIT)
    act = gate * _sigmoid(SWIGLU_ALPHA * gate) * (up + 1.0)
    o_ref[...] = jnp.dot(act.astype(BF16), wd_ref[...], preferred_element_type=F32) + bd_ref[...]


def _experts(block_e, n_valid, xs, w_gu, b_gu, w_d, b_d):
    n_blocks = block_e.shape[0]
    grid_spec = pltpu.PrefetchScalarGridSpec(
        num_scalar_prefetch=2,
        grid=(n_blocks,),
        in_specs=[pl.BlockSpec((EXPERT_BLOCK, D_MODEL), lambda j, be, nv: (j, 0)),
                  pl.BlockSpec((None, D_MODEL, 2 * D_MODEL), lambda j, be, nv: (be[j], 0, 0)),
                  pl.BlockSpec((None, 1, 2 * D_MODEL), lambda j, be, nv: (be[j], 0, 0)),
                  pl.BlockSpec((None, D_MODEL, D_MODEL), lambda j, be, nv: (be[j], 0, 0)),
                  pl.BlockSpec((None, 1, D_MODEL), lambda j, be, nv: (be[j], 0, 0))],
        out_specs=pl.BlockSpec((EXPERT_BLOCK, D_MODEL), lambda j, be, nv: (j, 0)),
    )
    return pl.pallas_call(
        _expert_kernel,
        grid_spec=grid_spec,
        out_shape=jax.ShapeDtypeStruct(xs.shape, F32),
        compiler_params=pltpu.CompilerParams(
            dimension_semantics=("arbitrary",), vmem_limit_bytes=VMEM_LIMIT),
        name="experts",
    )(block_e, n_valid, xs, w_gu, b_gu, w_d, b_d)


COMBINE_TOKENS = 256
MOE_SPLITS = 2


def _combine_kernel(yg_ref, x1_ref, gate_ref, g2_ref, fg_ref, o_ref):
    gates = gate_ref[...]
    acc = None
    for kk in range(TOP_K):
        part = gates[:, kk:kk + 1] * yg_ref[kk * COMBINE_TOKENS:(kk + 1) * COMBINE_TOKENS, :]
        acc = part if acc is None else acc + part
    x2 = x1_ref[...] + g2_ref[...] * acc
    o_ref[...] = x2 * lax.rsqrt(jnp.mean(x2 * x2, axis=-1, keepdims=True) + NORM_EPS) * fg_ref[...]


def _combine_kernel_into(prev_ref, *refs):
    del prev_ref
    _combine_kernel(*refs)


def _combine(yg, x1, gates, gate2, final_g, seq, row0, t_total, prev):
    t = x1.shape[0]
    tm = COMBINE_TOKENS
    per_b = seq // tm
    blk0 = row0 // tm
    rows = lambda i: (i, 0)
    in_specs = [pl.BlockSpec((TOP_K * tm, D_MODEL), rows),
                pl.BlockSpec((tm, D_MODEL), rows),
                pl.BlockSpec((tm, LANES), rows),
                pl.BlockSpec((None, 1, D_MODEL), lambda i: ((i + blk0) // per_b, 0, 0)),
                pl.BlockSpec((1, D_MODEL), lambda i: (0, 0))]
    args = (yg, x1, gates, gate2, final_g)
    if prev is not None:
        in_specs = [pl.BlockSpec(memory_space=pl.ANY)] + in_specs
        args = (prev,) + args
    return pl.pallas_call(
        _combine_kernel if prev is None else _combine_kernel_into,
        grid=(t // tm,),
        in_specs=in_specs,
        out_specs=pl.BlockSpec((tm, D_MODEL), lambda i: (i + blk0, 0)),
        out_shape=jax.ShapeDtypeStruct((t_total, D_MODEL), F32),
        input_output_aliases={} if prev is None else {0: 0},
        compiler_params=pltpu.CompilerParams(
            dimension_semantics=("parallel",), vmem_limit_bytes=VMEM_LIMIT),
        name="combine",
    )(*args)


def _moe(h2, idx, gates, rank, counts, x1, gate2, final_g, w_gu, b_gu, w_d, b_d, seq,
         row0, t_total, prev):
    t = h2.shape[0]
    n_slots = t * TOP_K
    n_blocks = -(-n_slots // EXPERT_BLOCK) + N_EXPERTS
    cap = n_blocks * EXPERT_BLOCK
    padded = (counts + EXPERT_BLOCK - 1) // EXPERT_BLOCK * EXPERT_BLOCK
    pad_ends = jnp.cumsum(padded)
    pad_starts = pad_ends - padded
    dest = pad_starts[idx] + rank
    block_starts = jnp.arange(n_blocks, dtype=jnp.int32) * EXPERT_BLOCK
    block_e = jnp.minimum(jnp.sum(block_starts[:, None] >= pad_ends[None, :], axis=1),
                          N_EXPERTS - 1).astype(jnp.int32)
    n_valid = jnp.clip(counts[block_e] - (block_starts - pad_starts[block_e]), 0, EXPERT_BLOCK)

    xs = _sc_scatter_rows(h2, dest, cap)
    yb = _experts(block_e, n_valid.astype(jnp.int32), xs, w_gu, b_gu, w_d, b_d)
    dest_blocks = dest.reshape(-1, COMBINE_TOKENS, TOP_K).transpose(0, 2, 1).reshape(-1)
    yg = _sc_gather_rows(dest_blocks, yb)
    return _combine(yg, x1, gates, gate2, final_g, seq, row0, t_total, prev)


def _layer(x, c_mod, norm1_g, w_in, mu_shift, w0, w2, a0, a2, g2, k_k, k_a, r_k, gn_w, gn_b, b_f,
           q_norm_g, k_norm_g, o_norm_g, w_out, norm2_g, w_router, b_router, w_gate_up,
           b_gate_up, w_down, b_down, final_g, tm_in, tq, tm_out):
    bsz, seq, _ = x.shape
    shift1, scale1, gate1, shift2, scale2, gate2 = (
        m.reshape(bsz, 1, D_MODEL) for m in jnp.split(c_mod, 6, axis=-1))
    row = lambda v: v.reshape(1, -1)

    w_r = w_in[:, :RWKV_COLS].astype(BF16)
    w_x = w_in[:, RWKV_COLS:RWKV_COLS + FOX_MAIN].astype(BF16)
    w_f = jnp.pad(w_in[:, RWKV_COLS + FOX_MAIN:], ((0, 0), (0, LANES - N_HEADS)))
    b_fp = jnp.pad(b_f, (0, LANES - N_HEADS)).reshape(1, LANES)
    qk_gain = jnp.concatenate([jnp.tile(q_norm_g, N_HEADS) * HEAD_DIM ** -0.5,
                               jnp.tile(k_norm_g, N_HEADS)]).reshape(1, -1)
    p_r, p_x, k_bias, q_bias = _inproj(x, shift1, scale1, row(norm1_g), w_r, w_x, w_f, b_fp,
                                       qk_gain, tm_in)

    zeros = jnp.zeros((LANES - 64, D_GRP), F32)
    w2p = jnp.concatenate([w2, zeros], axis=0).astype(BF16)
    a2p = jnp.concatenate([zeros, a2], axis=0).astype(BF16)
    y_r = _rwkv(p_r, row(mu_shift), row(w0), w2p, row(a0), a2p, g2.astype(BF16), row(k_k),
                row(k_a), row(r_k), row(gn_w), row(gn_b))

    y_f = _fox(p_x, k_bias, q_bias, jnp.tile(o_norm_g, 2).reshape(1, LANES), tq)

    t = bsz * seq
    w_rt = jnp.pad(w_router, ((0, 0), (0, LANES - N_EXPERTS)))
    b_rt = jnp.pad(b_router, (0, LANES - N_EXPERTS)).reshape(1, LANES)
    wo = w_out.astype(BF16)
    w_gu, w_d = w_gate_up.astype(BF16), w_down.astype(BF16)
    b_gu, b_d = b_gate_up.reshape(N_EXPERTS, 1, -1), b_down.reshape(N_EXPERTS, 1, -1)
    t_part = t // MOE_SPLITS
    out = None
    for part in range(MOE_SPLITS):
        row0 = part * t_part
        x1, h2, idx, gates, rank, cnt = _outproj(
            x.reshape(t, D_MODEL), y_r.reshape(t, D_GRP), y_f.reshape(t, D_GRP), gate1, shift2,
            scale2, row(norm2_g), wo[:D_GRP], wo[D_GRP:], w_rt, b_rt, tm_out, seq, row0, t_part)
        counts = cnt[0, :N_EXPERTS].astype(jnp.int32)
        out = _moe(h2, idx[:, :TOP_K], gates, rank[:, :TOP_K], counts, x1, gate2, row(final_g),
                   w_gu, b_gu, w_d, b_d, seq, row0, t, out)
    return out.reshape(bsz, seq, D_MODEL)


def kernel(x, c, w_ada, b_ada, norm1_g, w_in, mu_shift, w0, w2, a0, a2, g2, k_k, k_a, r_k, gn_w,
           gn_b, b_f, q_norm_g, k_norm_g, o_norm_g, w_out, norm2_g, w_router, b_router, w_gate_up,
           b_gate_up, w_down, b_down, final_g):
    assert w_ada.shape[0] == 1, "single-layer block"
    c_mod = _adaln(c, w_ada[0], b_ada[0])
    return _layer(x, c_mod, norm1_g[0], w_in[0], mu_shift[0], w0[0], w2[0], a0[0], a2[0], g2[0],
                  k_k[0], k_a[0], r_k[0], gn_w[0], gn_b[0], b_f[0], q_norm_g[0], k_norm_g[0],
                  o_norm_g[0], w_out[0], norm2_g[0], w_router[0], b_router[0], w_gate_up[0],
                  b_gate_up[0], w_down[0], b_down[0], final_g,
                  tm_in=min(512, x.shape[1]), tq=min(512, x.shape[1]), tm_out=min(512, x.shape[1]))
```

```python
import functools

import jax
import jax.numpy as jnp
from jax import lax
from jax.experimental import pallas as pl
from jax.experimental.pallas import tpu as pltpu
from jax.experimental.pallas import tpu_sc as plsc

F32 = jnp.float32
BF16 = jnp.bfloat16
HIGHEST = lax.Precision.HIGHEST

D_MODEL = 1024
HEAD_DIM = 64
N_HEADS = 8
D_GRP = N_HEADS * HEAD_DIM
RWKV_COLS = 1792
LORA_OFF = 3 * D_GRP
GATE_OFF = LORA_OFF + 128
FOX_MAIN = 4 * D_GRP
N_EXPERTS = 32
TOP_K = 4
EXPERT_BLOCK = 512
SWIGLU_ALPHA = 1.702
SWIGLU_LIMIT = 7.0
NORM_EPS = 1e-6
GN_EPS = 64e-5
LANES = 128
CHUNK = 64
FOX_SUB_KEYS = 256
HEADS_PER_SCAN = 4
SCAN_W = HEADS_PER_SCAN * HEAD_DIM
SEG_TERMS = 1
CUM_TERMS = 2
VMEM_LIMIT = 56 * 1024 * 1024


def _dot(a, b):
    return jnp.dot(a.astype(BF16), b.astype(BF16), preferred_element_type=F32)


def _dot_nt(a, b):
    return lax.dot_general(a.astype(BF16), b.astype(BF16), (((1,), (1,)), ((), ())),
                           preferred_element_type=F32)


def _dot_tn(a, b):
    return lax.dot_general(a.astype(BF16), b.astype(BF16), (((0,), (0,)), ((), ())),
                           preferred_element_type=F32)


def _fdot(a, b):
    return jnp.dot(a, b, precision=HIGHEST, preferred_element_type=F32)


def _split_dot(x, m, terms=2, left=False):
    acc = None
    rem = x
    for _ in range(terms):
        part = rem.astype(BF16)
        rem = rem - part.astype(F32)
        d = (jnp.dot(m, part, preferred_element_type=F32) if left
             else jnp.dot(part, m, preferred_element_type=F32))
        acc = d if acc is None else acc + d
    return acc


def _iota(shape, dim):
    return lax.broadcasted_iota(jnp.int32, shape, dim)


def _seg_reduce_mat(n):
    return (_iota((n, LANES), 0) // HEAD_DIM == _iota((n, LANES), 1)).astype(BF16)


def _seg_expand_mat(n):
    return (_iota((LANES, n), 1) // HEAD_DIM == _iota((LANES, n), 0)).astype(BF16)


def _tri(n, strict):
    r, c = _iota((n, n), 0), _iota((n, n), 1)
    return ((r > c) if strict else (r >= c)).astype(BF16)


D_PACK = D_MODEL // 2


def _pack_rows(x):
    lo = lax.bitcast_convert_type(x[:, :D_PACK].astype(BF16).astype(F32), jnp.uint32)
    hi = lax.bitcast_convert_type(x[:, D_PACK:].astype(BF16).astype(F32), jnp.uint32)
    return hi | (lo >> 16)


def _unpack_rows(p):
    lo = lax.bitcast_convert_type(p << 16, F32)
    hi = lax.bitcast_convert_type(p & jnp.uint32(0xFFFF0000), F32)
    return lo, hi


def _log_sigmoid(z):
    return jnp.minimum(z, 0.0) - jnp.log(1.0 + jnp.exp(-jnp.abs(z)))


def _sigmoid(z):
    return 1.0 / (1.0 + jnp.exp(-z))


def _adaln_kernel(c_ref, w_ref, b_ref, o_ref):
    c = c_ref[...]
    o_ref[...] = _fdot(c * _sigmoid(c), w_ref[...]) + b_ref[...]


def _adaln(c, w_ada, b_ada):
    bsz = c.shape[0]
    n_mod = w_ada.shape[1] // D_MODEL
    return pl.pallas_call(
        _adaln_kernel,
        grid=(n_mod,),
        in_specs=[pl.BlockSpec((bsz, D_MODEL), lambda j: (0, 0)),
                  pl.BlockSpec((D_MODEL, D_MODEL), lambda j: (0, j)),
                  pl.BlockSpec((1, D_MODEL), lambda j: (0, j))],
        out_specs=pl.BlockSpec((bsz, D_MODEL), lambda j: (0, j)),
        out_shape=jax.ShapeDtypeStruct((bsz, n_mod * D_MODEL), F32),
        name="adaln",
    )(c, w_ada, b_ada.reshape(1, -1))


def _inproj_kernel(x_ref, sh_ref, sc_ref, g_ref, wr_ref, wx_ref, wfh_ref, wfl_ref, bf_ref, qkg_ref,
                   pr_ref, px_ref, kb_ref, qb_ref, carry_ref):
    @pl.when(pl.program_id(1) == 0)
    def _():
        carry_ref[...] = jnp.zeros_like(carry_ref)

    x = x_ref[...]
    tm = x.shape[0]
    h = x * lax.rsqrt(jnp.mean(x * x, axis=-1, keepdims=True) + NORM_EPS) * g_ref[...]
    h = h * (1.0 + sc_ref[...]) + sh_ref[...]
    hb = h.astype(BF16)
    h_lo = (h - hb.astype(F32)).astype(BF16)

    pr_ref[...] = jnp.dot(hb, wr_ref[...], preferred_element_type=F32).astype(BF16)

    px = jnp.dot(hb, wx_ref[...], preferred_element_type=F32)
    qk = px[:, :2 * D_GRP]
    ss = _split_dot(qk * qk, _seg_reduce_mat(2 * D_GRP), SEG_TERMS)
    inv = lax.rsqrt(ss * (1.0 / HEAD_DIM) + NORM_EPS)
    qk = qk * _split_dot(inv, _seg_expand_mat(2 * D_GRP), SEG_TERMS) * qkg_ref[...]
    px_ref[:, :2 * D_GRP] = qk.astype(BF16)
    px_ref[:, 2 * D_GRP:] = px[:, 2 * D_GRP:].astype(BF16)

    z = (jnp.dot(hb, wfh_ref[...], preferred_element_type=F32)
         + jnp.dot(h_lo, wfh_ref[...], preferred_element_type=F32)
         + jnp.dot(hb, wfl_ref[...], preferred_element_type=F32)) + bf_ref[...]
    cum = _split_dot(_log_sigmoid(z), _tri(tm, False), terms=3, left=True) + carry_ref[...]
    carry_ref[...] = cum[tm - 1:tm, :]

    src, dst = _iota((LANES, LANES), 0), _iota((LANES, LANES), 1)
    parts = []
    rem = cum
    for _ in range(3):
        part = rem.astype(BF16)
        rem = rem - part.astype(F32)
        parts.append(part)

    def spread(offset):
        return sum(jnp.dot(part, ((dst == 8 * src + offset + t) & (src < N_HEADS)).astype(BF16),
                           preferred_element_type=F32) for t, part in enumerate(parts))

    slot = _iota((1, LANES), 1) % 8
    kb_ref[...] = (jnp.where((slot >= 3) & (slot < 6), 1.0, 0.0) - spread(0)).astype(BF16)
    qb_ref[...] = (jnp.where(slot < 3, 1.0, 0.0) + spread(3)).astype(BF16)


def _inproj(x, shift, scale, g, w_r, w_x, w_f, b_f, qk_gain, tm):
    w_f_hi = w_f.astype(BF16)
    w_f_lo = (w_f - w_f_hi.astype(F32)).astype(BF16)
    bsz, seq, _ = x.shape
    const = lambda b, s: (0, 0)
    return pl.pallas_call(
        _inproj_kernel,
        grid=(bsz, seq // tm),
        in_specs=[pl.BlockSpec((None, tm, D_MODEL), lambda b, s: (b, s, 0)),
                  pl.BlockSpec((None, 1, D_MODEL), lambda b, s: (b, 0, 0)),
                  pl.BlockSpec((None, 1, D_MODEL), lambda b, s: (b, 0, 0)),
                  pl.BlockSpec((1, D_MODEL), const),
                  pl.BlockSpec((D_MODEL, RWKV_COLS), const),
                  pl.BlockSpec((D_MODEL, FOX_MAIN), const),
                  pl.BlockSpec((D_MODEL, LANES), const),
                  pl.BlockSpec((D_MODEL, LANES), const),
                  pl.BlockSpec((1, LANES), const),
                  pl.BlockSpec((1, 2 * D_GRP), const)],
        out_specs=[pl.BlockSpec((None, tm, RWKV_COLS), lambda b, s: (b, s, 0)),
                   pl.BlockSpec((None, tm, FOX_MAIN), lambda b, s: (b, s, 0)),
                   pl.BlockSpec((None, tm, LANES), lambda b, s: (b, s, 0)),
                   pl.BlockSpec((None, tm, LANES), lambda b, s: (b, s, 0))],
        out_shape=[jax.ShapeDtypeStruct((bsz, seq, RWKV_COLS), BF16),
                   jax.ShapeDtypeStruct((bsz, seq, FOX_MAIN), BF16),
                   jax.ShapeDtypeStruct((bsz, seq, LANES), BF16),
                   jax.ShapeDtypeStruct((bsz, seq, LANES), BF16)],
        scratch_shapes=[pltpu.VMEM((1, LANES), F32)],
        compiler_params=pltpu.CompilerParams(
            dimension_semantics=("parallel", "arbitrary"), vmem_limit_bytes=VMEM_LIMIT),
        name="inproj",
    )(x, shift, scale, g, w_r, w_x, w_f_hi, w_f_lo, b_f, qk_gain)


_NN = (((1,), (0,)), ((), ()))
_NT = (((1,), (1,)), ((), ()))
_TN = (((0,), (0,)), ((), ()))
SCAN_N = HEADS_PER_SCAN * CHUNK
BATCH_PER_STEP = 4
INV_LEVELS = 5
M_HEAD, M_STRICT, M_INCL, M_EYE, M_BASE, M_OFF = 0, 1, 2, 3, 4, 5


def _bdot(a, b, dims):
    return lax.dot_general(a, b, dims, preferred_element_type=F32)


def _scan_masks():
    rr, cc = _iota((SCAN_N, SCAN_W), 0), _iota((SCAN_N, SCAN_W), 1)
    ri, ci = _iota((SCAN_N, SCAN_N), 0), _iota((SCAN_N, SCAN_N), 1)
    same = ri // CHUNK == ci // CHUNK
    masks = [rr // CHUNK == cc // HEAD_DIM, same & (ri > ci), same & (ri >= ci), ri == ci,
             (ri // 2 == ci // 2) & (ri > ci)]
    blk = 2
    while blk < CHUNK:
        masks.append((ri // (2 * blk) == ci // (2 * blk)) & (ri // blk != ci // blk) & (ri > ci))
        blk *= 2
    return jnp.stack(masks).astype(BF16)


def _rwkv_kernel(p_ref, masks_ref, mu_ref, w0_ref, w2_ref, a0_ref, a2_ref, g2_ref, kk_ref, ka_ref,
                 rk_ref, gnw_ref, gnb_ref, o_ref, last_ref, state_ref):
    @pl.when(pl.program_id(1) == 0)
    def _():
        last_ref[...] = jnp.zeros_like(last_ref)
        state_ref[...] = jnp.zeros_like(state_ref)

    mu, w0, w2, a0, a2, g2, k_k, k_a, r_k, gn_w, gn_b = (
        ref[...] for ref in (mu_ref, w0_ref, w2_ref, a0_ref, a2_ref, g2_ref, kk_ref, ka_ref,
                             rk_ref, gnw_ref, gnb_ref))
    rows = BATCH_PER_STEP * CHUNK
    p = p_ref[...].astype(F32).reshape(rows, RWKV_COLS)
    row_id = _iota((rows, 1), 0)
    prev = pltpu.roll(p, 1, axis=0)
    for bb in range(BATCH_PER_STEP):
        prev = jnp.where(row_id == bb * CHUNK, last_ref[bb], prev)
        last_ref[bb] = p[(bb + 1) * CHUNK - 1:(bb + 1) * CHUNK, :]
    pf = p + mu * (prev - p)
    r = pf[:, 0:D_GRP]
    k = pf[:, D_GRP:2 * D_GRP]
    v = pf[:, 2 * D_GRP:3 * D_GRP]
    lora = pf[:, LORA_OFF:GATE_OFF]
    gd = pf[:, GATE_OFF:RWKV_COLS]

    wlog = w0 + _dot(jnp.tanh(lora), w2)
    neg = -wlog
    softplus = jnp.maximum(neg, 0.0) + jnp.log(1.0 + jnp.exp(-jnp.abs(neg)))
    ld = -jnp.exp(-softplus - 0.5)
    a = _sigmoid(a0 + _dot(lora, a2))
    g = _dot(_sigmoid(gd), g2)

    red, exp_m = _seg_reduce_mat(D_GRP), _seg_expand_mat(D_GRP)
    kk = k * k_k
    n2 = _split_dot(kk * kk, red, SEG_TERMS)
    kk = kk * _split_dot(1.0 / jnp.maximum(jnp.sqrt(n2), 1e-12), exp_m, SEG_TERMS)
    k2 = k * (1.0 + (a - 1.0) * k_a)

    tr, tc = _iota((rows, rows), 0), _iota((rows, rows), 1)
    tri = ((tr >= tc) & (tr // CHUNK == tc // CHUNK)).astype(BF16)
    cl = _split_dot(ld, tri, terms=CUM_TERMS, left=True)
    cl_end = jnp.concatenate(
        [jnp.broadcast_to(cl[(bb + 1) * CHUNK - 1:(bb + 1) * CHUNK, :], (CHUNK, D_GRP))
         for bb in range(BATCH_PER_STEP)], axis=0)
    e_in = jnp.exp(cl)
    e_out = jnp.exp(-cl)
    e_rem = jnp.exp(cl_end - cl)
    p_end = jnp.exp(cl_end)
    kka = kk * a
    ops = [(-kk * jnp.exp(cl - ld)).astype(BF16), (kka * e_out).astype(BF16),
           (k2 * e_out).astype(BF16), (r * e_in).astype(BF16), v.astype(BF16),
           (kka * e_rem).astype(BF16), (k2 * e_rem).astype(BF16)]

    chains = [(bb, grp) for bb in range(BATCH_PER_STEP)
              for grp in range(N_HEADS // HEADS_PER_SCAN)]
    head_mask = masks_ref[M_HEAD]
    strict, incl = masks_ref[M_STRICT], masks_ref[M_INCL]

    def stacked(op, bb, grp):
        part = op[bb * CHUNK:(bb + 1) * CHUNK, grp * SCAN_W:(grp + 1) * SCAN_W]
        return jnp.concatenate([part] * HEADS_PER_SCAN, axis=0) * head_mask

    xs = [[stacked(op, bb, grp) for op in ops] for bb, grp in chains]
    st = [state_ref[bb, grp] for bb, grp in chains]
    sb = [s.astype(BF16) for s in st]
    nab = [_bdot(x[0], x[1], _NT).astype(BF16) for x in xs]
    aak = [_bdot(x[0], x[2], _NT).astype(BF16) * strict for x in xs]
    arb = [_bdot(x[3], x[1], _NT).astype(BF16) * incl for x in xs]
    ark = [_bdot(x[3], x[2], _NT).astype(BF16) * incl for x in xs]
    t_inv = [masks_ref[M_EYE] + n * masks_ref[M_BASE] for n in nab]
    for lvl in range(INV_LEVELS):
        half = [_bdot(t, n * masks_ref[M_OFF + lvl], _NN).astype(BF16) for t, n in zip(t_inv, nab)]
        t_inv = [t + _bdot(h, t, _NN).astype(BF16) for t, h in zip(t_inv, half)]
    rhs = [(_bdot(x[0], s, _NT) + _bdot(k, x[4], _NN)).astype(BF16)
           for x, s, k in zip(xs, sb, aak)]
    sa = [_bdot(t, h, _NN).astype(BF16) for t, h in zip(t_inv, rhs)]
    ys = [_bdot(x[3], s, _NT) + _bdot(b, u, _NN) + _bdot(k, x[4], _NN)
          for x, s, b, u, k in zip(xs, sb, arb, sa, ark)]
    for (bb, grp), x, s, u in zip(chains, xs, st, sa):
        decay = p_end[bb * CHUNK:bb * CHUNK + 1, grp * SCAN_W:(grp + 1) * SCAN_W]
        state_ref[bb, grp] = s * decay + _bdot(u, x[5], _TN) + _bdot(x[4], x[6], _TN)
    ys = [y[0:CHUNK] + y[CHUNK:2 * CHUNK] + y[2 * CHUNK:3 * CHUNK] + y[3 * CHUNK:4 * CHUNK]
          for y in ys]
    n_grp = N_HEADS // HEADS_PER_SCAN
    y = jnp.concatenate([jnp.concatenate(ys[bb * n_grp:(bb + 1) * n_grp], axis=1)
                         for bb in range(BATCH_PER_STEP)], axis=0)

    mean = _split_dot(_split_dot(y, red, SEG_TERMS) * (1.0 / HEAD_DIM), exp_m, SEG_TERMS)
    d = y - mean
    var = _split_dot(d * d, red, SEG_TERMS) * (1.0 / HEAD_DIM)
    yn = d * _split_dot(lax.rsqrt(var + GN_EPS), exp_m, SEG_TERMS) * gn_w + gn_b
    bonus = _split_dot(_split_dot(r * k2 * r_k, red, SEG_TERMS), exp_m, SEG_TERMS) * v
    o_ref[...] = ((yn + bonus) * g).astype(BF16).reshape(BATCH_PER_STEP, CHUNK, D_GRP)


def _rwkv(p_r, mu, w0, w2p, a0, a2p, g2, k_k, k_a, r_k, gn_w, gn_b):
    bsz, seq, _ = p_r.shape
    assert bsz % BATCH_PER_STEP == 0
    masks = _scan_masks()
    const = lambda b, s: (0, 0)
    vec = pl.BlockSpec((1, D_GRP), const)
    return pl.pallas_call(
        _rwkv_kernel,
        grid=(bsz // BATCH_PER_STEP, seq // CHUNK),
        in_specs=[pl.BlockSpec((BATCH_PER_STEP, CHUNK, RWKV_COLS), lambda b, s: (b, s, 0)),
                  pl.BlockSpec(masks.shape, lambda b, s: (0, 0, 0)),
                  pl.BlockSpec((1, RWKV_COLS), const),
                  vec, pl.BlockSpec((LANES, D_GRP), const),
                  vec, pl.BlockSpec((LANES, D_GRP), const),
                  pl.BlockSpec((LANES, D_GRP), const),
                  vec, vec, vec, vec, vec],
        out_specs=pl.BlockSpec((BATCH_PER_STEP, CHUNK, D_GRP), lambda b, s: (b, s, 0)),
        out_shape=jax.ShapeDtypeStruct((bsz, seq, D_GRP), BF16),
        scratch_shapes=[pltpu.VMEM((BATCH_PER_STEP, 1, RWKV_COLS), F32),
                        pltpu.VMEM((BATCH_PER_STEP, N_HEADS // HEADS_PER_SCAN, SCAN_W, SCAN_W), F32)],
        compiler_params=pltpu.CompilerParams(
            dimension_semantics=("parallel", "arbitrary"), vmem_limit_bytes=VMEM_LIMIT),
        name="rwkv",
    )(p_r, masks, mu, w0, w2p, a0, a2p, g2, k_k, k_a, r_k, gn_w, gn_b)


def _fox_kernel(q_ref, qb_ref, k_ref, kb_ref, vt_ref, og_ref, ong_ref, o_ref, *, tq):
    hp = pl.program_id(1)
    qi = pl.program_id(2)
    lane = _iota((1, LANES), 1)
    q = q_ref[...]
    qb = qb_ref[...]
    zero = jnp.zeros_like(q)
    qcat = [jnp.concatenate([jnp.where(lane // HEAD_DIM == hh, q, zero),
                             jnp.where(lane // 8 == hp * 2 + hh, qb, zero)], axis=1)
            for hh in range(2)]
    FOX_KEYS = min(FOX_SUB_KEYS, tq)
    n_sub = tq // FOX_KEYS
    key_pos = _iota((FOX_KEYS, tq), 0)
    qry_pos = _iota((FOX_KEYS, tq), 1)

    def tile(j, stats, masked):
        base = j * tq
        sts = []
        for s in range(n_sub):
            start = pl.multiple_of(base + s * FOX_KEYS, FOX_KEYS)
            kcat = jnp.concatenate([k_ref[pl.ds(start, FOX_KEYS), :],
                                    kb_ref[pl.ds(start, FOX_KEYS), :]], axis=1)
            sts.append([lax.dot_general(kcat, qc, _NT, preferred_element_type=F32)
                        for qc in qcat])
        for s in range(n_sub):
            start = pl.multiple_of(base + s * FOX_KEYS, FOX_KEYS)
            vt = vt_ref[:, pl.ds(start, FOX_KEYS)]
            st = sts[s]
            if masked:
                st = [jnp.where(qry_pos >= key_pos + s * FOX_KEYS, x, -jnp.inf) for x in st]
            m_new = [jnp.maximum(c[0], jnp.max(x, axis=0, keepdims=True))
                     for c, x in zip(stats, st)]
            pts = [jnp.exp(x - m) for x, m in zip(st, m_new)]
            pvs = [jnp.dot(vt, pt.astype(BF16), preferred_element_type=F32) for pt in pts]
            new = []
            for hh in range(2):
                m, l, acc = stats[hh]
                alpha = jnp.exp(m - m_new[hh])
                l = alpha * l + jnp.sum(pts[hh], axis=0, keepdims=True)
                acc = alpha * acc + pvs[hh][hh * HEAD_DIM:(hh + 1) * HEAD_DIM, :]
                new.append((m_new[hh], l, acc))
            stats = tuple(new)
        return stats

    init = (jnp.full((1, tq), -jnp.inf, F32), jnp.zeros((1, tq), F32),
            jnp.zeros((HEAD_DIM, tq), F32))
    stats = lax.fori_loop(0, qi, functools.partial(tile, masked=False), (init, init))
    outs = []
    for _, l, acc in tile(qi, stats, True):
        o = acc / l
        outs.append(o * lax.rsqrt(jnp.mean(o * o, axis=0, keepdims=True) + NORM_EPS))
    o = jnp.concatenate(outs, axis=0).T
    o_ref[...] = (o * ong_ref[...] * _sigmoid(og_ref[...].astype(F32))).astype(BF16)


def _fox(p_x, k_bias, q_bias, o_gain, tq):
    bsz, seq, _ = p_x.shape
    npair = N_HEADS // 2
    v_t = jnp.transpose(p_x[:, :, 2 * D_GRP:3 * D_GRP], (0, 2, 1))
    return pl.pallas_call(
        functools.partial(_fox_kernel, tq=tq),
        grid=(bsz, npair, seq // tq),
        in_specs=[pl.BlockSpec((None, tq, LANES), lambda b, h, i: (b, i, h)),
                  pl.BlockSpec((None, tq, LANES), lambda b, h, i: (b, i, 0)),
                  pl.BlockSpec((None, seq, LANES), lambda b, h, i: (b, 0, npair + h)),
                  pl.BlockSpec((None, seq, LANES), lambda b, h, i: (b, 0, 0)),
                  pl.BlockSpec((None, LANES, seq), lambda b, h, i: (b, h, 0)),
                  pl.BlockSpec((None, tq, LANES), lambda b, h, i: (b, i, 3 * npair + h)),
                  pl.BlockSpec((1, LANES), lambda b, h, i: (0, 0))],
        out_specs=pl.BlockSpec((None, tq, LANES), lambda b, h, i: (b, i, h)),
        out_shape=jax.ShapeDtypeStruct((bsz, seq, D_GRP), BF16),
        compiler_params=pltpu.CompilerParams(
            dimension_semantics=("parallel", "parallel", "arbitrary"),
            vmem_limit_bytes=VMEM_LIMIT),
        name="fox",
    )(p_x, q_bias, p_x, k_bias, v_t, p_x, o_gain)


def _outproj_kernel(x_ref, yr_ref, yf_ref, g1_ref, sh_ref, sc_ref, ng_ref, wor_ref, wof_ref,
                    wrt_ref, wrl_ref, brt_ref, x1_ref, h2_ref, idx_ref, gate_ref, rank_ref, cnt_ref,
                    carry_ref):
    @pl.when(pl.program_id(0) == 0)
    def _():
        carry_ref[...] = jnp.zeros_like(carry_ref)

    y = (jnp.dot(yr_ref[...], wor_ref[...], preferred_element_type=F32)
         + jnp.dot(yf_ref[...], wof_ref[...], preferred_element_type=F32))
    x1 = x_ref[...] + g1_ref[...] * y
    x1_ref[...] = x1
    tm = x1.shape[0]
    h = x1 * lax.rsqrt(jnp.mean(x1 * x1, axis=-1, keepdims=True) + NORM_EPS) * ng_ref[...]
    h2 = h * (1.0 + sc_ref[...]) + sh_ref[...]
    h2_ref[...] = _pack_rows(h2)

    lane = _iota((tm, LANES), 1)
    h_hi = h2.astype(BF16)
    h_lo = (h2 - h_hi.astype(F32)).astype(BF16)
    logits = (jnp.dot(h_hi, wrt_ref[...], preferred_element_type=F32)
              + jnp.dot(h_lo, wrt_ref[...], preferred_element_type=F32)
              + jnp.dot(h_hi, wrl_ref[...], preferred_element_type=F32)) + brt_ref[...]
    lg = jnp.where(lane < N_EXPERTS, logits, -jnp.inf)
    picks = []
    hot_sum = jnp.zeros((tm, LANES), F32)
    for _ in range(TOP_K):
        m = jnp.max(lg, axis=-1, keepdims=True)
        sel = jnp.min(jnp.where(lg == m, lane, LANES), axis=-1, keepdims=True)
        hot = lane == sel
        picks.append((m, sel, hot))
        hot_sum = hot_sum + hot.astype(F32)
        lg = jnp.where(hot, -jnp.inf, lg)
    es = [jnp.exp(m - picks[0][0]) for m, _, _ in picks]
    den = es[0] + es[1] + es[2] + es[3]

    before = jnp.dot(_tri(tm, True), hot_sum.astype(BF16), preferred_element_type=F32)
    before = before + carry_ref[...]
    idx_out = jnp.zeros((tm, LANES), jnp.int32)
    gate_out = jnp.zeros((tm, LANES), F32)
    rank_out = jnp.zeros((tm, LANES), jnp.int32)
    for kk, (m, sel, hot) in enumerate(picks):
        rk = jnp.sum(jnp.where(hot, before, 0.0), axis=-1, keepdims=True).astype(jnp.int32)
        idx_out = jnp.where(lane == kk, sel, idx_out)
        gate_out = jnp.where(lane == kk, es[kk] / den, gate_out)
        rank_out = jnp.where(lane == kk, rk, rank_out)
    idx_ref[...] = idx_out
    gate_ref[...] = gate_out
    rank_ref[...] = rank_out
    carry_ref[...] = carry_ref[...] + jnp.sum(hot_sum, axis=0, keepdims=True)
    cnt_ref[...] = carry_ref[...]


def _outproj(x2d, y_r, y_f, gate1, shift2, scale2, norm_g, wo_r, wo_f, w_rt, b_rt, tm, seq,
             row0, t):
    w_rt_hi = w_rt.astype(BF16)
    w_rt_lo = (w_rt - w_rt_hi.astype(F32)).astype(BF16)
    per_b = seq // tm
    blk0 = row0 // tm
    const = lambda i: (0, 0)
    rows = lambda i: (i, 0)
    rows_in = lambda i: (i + blk0, 0)
    mod = pl.BlockSpec((None, 1, D_MODEL), lambda i: ((i + blk0) // per_b, 0, 0))
    return pl.pallas_call(
        _outproj_kernel,
        grid=(t // tm,),
        in_specs=[pl.BlockSpec((tm, D_MODEL), rows_in),
                  pl.BlockSpec((tm, D_GRP), rows_in),
                  pl.BlockSpec((tm, D_GRP), rows_in),
                  mod, mod, mod,
                  pl.BlockSpec((1, D_MODEL), const),
                  pl.BlockSpec((D_GRP, D_MODEL), const),
                  pl.BlockSpec((D_GRP, D_MODEL), const),
                  pl.BlockSpec((D_MODEL, LANES), const),
                  pl.BlockSpec((D_MODEL, LANES), const),
                  pl.BlockSpec((1, LANES), const)],
        out_specs=[pl.BlockSpec((tm, D_MODEL), rows),
                   pl.BlockSpec((tm, D_PACK), rows),
                   pl.BlockSpec((tm, LANES), rows),
                   pl.BlockSpec((tm, LANES), rows),
                   pl.BlockSpec((tm, LANES), rows),
                   pl.BlockSpec((1, LANES), const)],
        out_shape=[jax.ShapeDtypeStruct((t, D_MODEL), F32),
                   jax.ShapeDtypeStruct((t, D_PACK), jnp.uint32),
                   jax.ShapeDtypeStruct((t, LANES), jnp.int32),
                   jax.ShapeDtypeStruct((t, LANES), F32),
                   jax.ShapeDtypeStruct((t, LANES), jnp.int32),
                   jax.ShapeDtypeStruct((1, LANES), F32)],
        scratch_shapes=[pltpu.VMEM((1, LANES), F32)],
        compiler_params=pltpu.CompilerParams(
            dimension_semantics=("arbitrary",), vmem_limit_bytes=VMEM_LIMIT),
        name="outproj",
    )(x2d, y_r, y_f, gate1, shift2, scale2, norm_g, wo_r, wo_f, w_rt_hi, w_rt_lo, b_rt)


SC_CORES = 2
SC_SUBCORES = 16
SC_ROWS = 32


def _sc_gather_rows(idx, src):
    n_workers = SC_CORES * SC_SUBCORES
    m = idx.shape[0]
    d = src.shape[1]
    assert m % (n_workers * SC_ROWS) == 0
    n_chunks = m // (n_workers * SC_ROWS)
    mesh = plsc.VectorSubcoreMesh(core_axis_name="c", subcore_axis_name="s")

    @functools.partial(
        pl.kernel, mesh=mesh,
        out_type=jax.ShapeDtypeStruct((m, d), src.dtype),
        scratch_types=[pltpu.VMEM((n_chunks, SC_ROWS), jnp.int32),
                       pltpu.VMEM((SC_ROWS, d), src.dtype),
                       pltpu.SemaphoreType.DMA],
        name="sc_gather")
    def gather(src_hbm, idx_hbm, out_hbm, idx_v, rows_v, sem):
        wid = lax.axis_index("s") * SC_CORES + lax.axis_index("c")
        pltpu.sync_copy(idx_hbm.at[wid], idx_v)

        @pl.loop(0, n_chunks)
        def _(j):
            pltpu.async_copy(src_hbm.at[idx_v.at[j]], rows_v, sem).wait()
            pltpu.sync_copy(rows_v, out_hbm.at[pl.ds((wid * n_chunks + j) * SC_ROWS, SC_ROWS)])

    return gather(src, idx.reshape(n_workers, n_chunks, SC_ROWS))


def _sc_scatter_rows(src, dest, n_out):
    n_workers = SC_CORES * SC_SUBCORES
    t, d = src.shape
    n_slot = dest.shape[1]
    assert t % (n_workers * SC_ROWS) == 0
    n_chunks = t // (n_workers * SC_ROWS)
    mesh = plsc.VectorSubcoreMesh(core_axis_name="c", subcore_axis_name="s")
    idx = dest.reshape(n_workers, n_chunks, SC_ROWS, n_slot).transpose(0, 1, 3, 2)
    idx = idx.reshape(n_workers, n_chunks * n_slot, SC_ROWS)

    @functools.partial(
        pl.kernel, mesh=mesh,
        out_type=jax.ShapeDtypeStruct((n_out, d), src.dtype),
        scratch_types=[pltpu.VMEM((n_chunks * n_slot, SC_ROWS), jnp.int32),
                       pltpu.VMEM((SC_ROWS, d), src.dtype)],
        name="sc_scatter")
    def scatter(src_hbm, idx_hbm, out_hbm, idx_v, rows_v):
        wid = lax.axis_index("s") * SC_CORES + lax.axis_index("c")
        pltpu.sync_copy(idx_hbm.at[wid], idx_v)

        @pl.loop(0, n_chunks)
        def _(j):
            pltpu.sync_copy(src_hbm.at[pl.ds((wid * n_chunks + j) * SC_ROWS, SC_ROWS)], rows_v)
            for k in range(n_slot):
                pltpu.sync_copy(rows_v, out_hbm.at[idx_v.at[j * n_slot + k]])

    return scatter(src, idx)


def _expert_kernel(be_ref, nv_ref, x_ref, wgu_ref, bgu_ref, wd_ref, bd_ref, o_ref):
    del be_ref
    valid = _iota((EXPERT_BLOCK, 1), 0) < nv_ref[pl.program_id(0)]
    lo, hi = _unpack_rows(jnp.where(valid, x_ref[...], jnp.uint32(0)))
    x = jnp.concatenate([lo.astype(BF16), hi.astype(BF16)], axis=1)
    gu = jnp.dot(x, wgu_ref[...], preferred_element_type=F32) + bgu_ref[...]
    gate = jnp.minimum(gu[:, :D_MODEL], SWIGLU_LIMIT)
    up = jnp.clip(gu[:, D_MODEL:], -SWIGLU_LIMIT, SWIGLU_LIMIT)
    act = gate * _sigmoid(SWIGLU_ALPHA * gate) * (up + 1.0)
    o_ref[...] = _pack_rows(
        jnp.dot(act.astype(BF16), wd_ref[...], preferred_element_type=F32) + bd_ref[...])


def _experts(block_e, n_valid, xs, w_gu, b_gu, w_d, b_d):
    n_blocks = block_e.shape[0]
    grid_spec = pltpu.PrefetchScalarGridSpec(
        num_scalar_prefetch=2,
        grid=(n_blocks,),
        in_specs=[pl.BlockSpec((EXPERT_BLOCK, D_PACK), lambda j, be, nv: (j, 0)),
                  pl.BlockSpec((None, D_MODEL, 2 * D_MODEL), lambda j, be, nv: (be[j], 0, 0)),
                  pl.BlockSpec((None, 1, 2 * D_MODEL), lambda j, be, nv: (be[j], 0, 0)),
                  pl.BlockSpec((None, D_MODEL, D_MODEL), lambda j, be, nv: (be[j], 0, 0)),
                  pl.BlockSpec((None, 1, D_MODEL), lambda j, be, nv: (be[j], 0, 0))],
        out_specs=pl.BlockSpec((EXPERT_BLOCK, D_PACK), lambda j, be, nv: (j, 0)),
    )
    return pl.pallas_call(
        _expert_kernel,
        grid_spec=grid_spec,
        out_shape=jax.ShapeDtypeStruct(xs.shape, jnp.uint32),
        compiler_params=pltpu.CompilerParams(
            dimension_semantics=("arbitrary",), vmem_limit_bytes=VMEM_LIMIT),
        name="experts",
    )(block_e, n_valid, xs, w_gu, b_gu, w_d, b_d)


COMBINE_TOKENS = 256
MOE_SPLITS = 2


def _combine_kernel(yg_ref, x1_ref, gate_ref, g2_ref, fg_ref, o_ref):
    gates = gate_ref[...]
    acc_lo = acc_hi = None
    for kk in range(TOP_K):
        lo, hi = _unpack_rows(yg_ref[kk * COMBINE_TOKENS:(kk + 1) * COMBINE_TOKENS, :])
        g = gates[:, kk:kk + 1]
        acc_lo = g * lo if acc_lo is None else acc_lo + g * lo
        acc_hi = g * hi if acc_hi is None else acc_hi + g * hi
    x2 = x1_ref[...] + g2_ref[...] * jnp.concatenate([acc_lo, acc_hi], axis=1)
    o_ref[...] = x2 * lax.rsqrt(jnp.mean(x2 * x2, axis=-1, keepdims=True) + NORM_EPS) * fg_ref[...]


def _combine_kernel_into(prev_ref, *refs):
    del prev_ref
    _combine_kernel(*refs)


def _combine(yg, x1, gates, gate2, final_g, seq, row0, t_total, prev):
    t = x1.shape[0]
    tm = COMBINE_TOKENS
    per_b = seq // tm
    blk0 = row0 // tm
    rows = lambda i: (i, 0)
    in_specs = [pl.BlockSpec((TOP_K * tm, D_PACK), rows),
                pl.BlockSpec((tm, D_MODEL), rows),
                pl.BlockSpec((tm, LANES), rows),
                pl.BlockSpec((None, 1, D_MODEL), lambda i: ((i + blk0) // per_b, 0, 0)),
                pl.BlockSpec((1, D_MODEL), lambda i: (0, 0))]
    args = (yg, x1, gates, gate2, final_g)
    if prev is not None:
        in_specs = [pl.BlockSpec(memory_space=pl.ANY)] + in_specs
        args = (prev,) + args
    return pl.pallas_call(
        _combine_kernel if prev is None else _combine_kernel_into,
        grid=(t // tm,),
        in_specs=in_specs,
        out_specs=pl.BlockSpec((tm, D_MODEL), lambda i: (i + blk0, 0)),
        out_shape=jax.ShapeDtypeStruct((t_total, D_MODEL), F32),
        input_output_aliases={} if prev is None else {0: 0},
        compiler_params=pltpu.CompilerParams(
            dimension_semantics=("parallel",), vmem_limit_bytes=VMEM_LIMIT),
        name="combine",
    )(*args)


def _moe(h2, idx, gates, rank, counts, x1, gate2, final_g, w_gu, b_gu, w_d, b_d, seq,
         row0, t_total, prev):
    t = h2.shape[0]
    n_slots = t * TOP_K
    n_blocks = -(-n_slots // EXPERT_BLOCK) + N_EXPERTS
    cap = n_blocks * EXPERT_BLOCK
    padded = (counts + EXPERT_BLOCK - 1) // EXPERT_BLOCK * EXPERT_BLOCK
    pad_ends = jnp.cumsum(padded)
    pad_starts = pad_ends - padded
    dest = pad_starts[idx] + rank
    block_starts = jnp.arange(n_blocks, dtype=jnp.int32) * EXPERT_BLOCK
    block_e = jnp.minimum(jnp.sum(block_starts[:, None] >= pad_ends[None, :], axis=1),
                          N_EXPERTS - 1).astype(jnp.int32)
    n_valid = jnp.clip(counts[block_e] - (block_starts - pad_starts[block_e]), 0, EXPERT_BLOCK)

    xs = _sc_scatter_rows(h2, dest, cap)
    yb = _experts(block_e, n_valid.astype(jnp.int32), xs, w_gu, b_gu, w_d, b_d)
    dest_blocks = dest.reshape(-1, COMBINE_TOKENS, TOP_K).transpose(0, 2, 1).reshape(-1)
    yg = _sc_gather_rows(dest_blocks, yb)
    return _combine(yg, x1, gates, gate2, final_g, seq, row0, t_total, prev)


def _layer(x, c_mod, norm1_g, w_in, mu_shift, w0, w2, a0, a2, g2, k_k, k_a, r_k, gn_w, gn_b, b_f,
           q_norm_g, k_norm_g, o_norm_g, w_out, norm2_g, w_router, b_router, w_gate_up,
           b_gate_up, w_down, b_down, final_g, tm_in, tq, tm_out):
    bsz, seq, _ = x.shape
    shift1, scale1, gate1, shift2, scale2, gate2 = (
        m.reshape(bsz, 1, D_MODEL) for m in jnp.split(c_mod, 6, axis=-1))
    row = lambda v: v.reshape(1, -1)

    w_r = w_in[:, :RWKV_COLS].astype(BF16)
    w_x = w_in[:, RWKV_COLS:RWKV_COLS + FOX_MAIN].astype(BF16)
    w_f = jnp.pad(w_in[:, RWKV_COLS + FOX_MAIN:], ((0, 0), (0, LANES - N_HEADS)))
    b_fp = jnp.pad(b_f, (0, LANES - N_HEADS)).reshape(1, LANES)
    qk_gain = jnp.concatenate([jnp.tile(q_norm_g, N_HEADS) * HEAD_DIM ** -0.5,
                               jnp.tile(k_norm_g, N_HEADS)]).reshape(1, -1)
    p_r, p_x, k_bias, q_bias = _inproj(x, shift1, scale1, row(norm1_g), w_r, w_x, w_f, b_fp,
                                       qk_gain, tm_in)

    zeros = jnp.zeros((LANES - 64, D_GRP), F32)
    w2p = jnp.concatenate([w2, zeros], axis=0).astype(BF16)
    a2p = jnp.concatenate([zeros, a2], axis=0).astype(BF16)
    y_r = _rwkv(p_r, row(mu_shift), row(w0), w2p, row(a0), a2p, g2.astype(BF16), row(k_k),
                row(k_a), row(r_k), row(gn_w), row(gn_b))

    y_f = _fox(p_x, k_bias, q_bias, jnp.tile(o_norm_g, 2).reshape(1, LANES), tq)

    t = bsz * seq
    w_rt = jnp.pad(w_router, ((0, 0), (0, LANES - N_EXPERTS)))
    b_rt = jnp.pad(b_router, (0, LANES - N_EXPERTS)).reshape(1, LANES)
    wo = w_out.astype(BF16)
    w_gu, w_d = w_gate_up.astype(BF16), w_down.astype(BF16)
    b_gu, b_d = b_gate_up.reshape(N_EXPERTS, 1, -1), b_down.reshape(N_EXPERTS, 1, -1)
    t_part = t // MOE_SPLITS
    out = None
    for part in range(MOE_SPLITS):
        row0 = part * t_part
        x1, h2, idx, gates, rank, cnt = _outproj(
            x.reshape(t, D_MODEL), y_r.reshape(t, D_GRP), y_f.reshape(t, D_GRP), gate1, shift2,
            scale2, row(norm2_g), wo[:D_GRP], wo[D_GRP:], w_rt, b_rt, tm_out, seq, row0, t_part)
        counts = cnt[0, :N_EXPERTS].astype(jnp.int32)
        out = _moe(h2, idx[:, :TOP_K], gates, rank[:, :TOP_K], counts, x1, gate2, row(final_g),
                   w_gu, b_gu, w_d, b_d, seq, row0, t, out)
    return out.reshape(bsz, seq, D_MODEL)


def kernel(x, c, w_ada, b_ada, norm1_g, w_in, mu_shift, w0, w2, a0, a2, g2, k_k, k_a, r_k, gn_w,
           gn_b, b_f, q_norm_g, k_norm_g, o_norm_g, w_out, norm2_g, w_router, b_router, w_gate_up,
           b_gate_up, w_down, b_down, final_g):
    assert w_ada.shape[0] == 1, "single-layer block"
    c_mod = _adaln(c, w_ada[0], b_ada[0])
    return _layer(x, c_mod, norm1_g[0], w_in[0], mu_shift[0], w0[0], w2[0], a0[0], a2[0], g2[0],
                  k_k[0], k_a[0], r_k[0], gn_w[0], gn_b[0], b_f[0], q_norm_g[0], k_norm_g[0],
                  o_norm_g[0], w_out[0], norm2_g[0], w_router[0], b_router[0], w_gate_up[0],
                  b_gate_up[0], w_down[0], b_down[0], final_g,
                  tm_in=min(512, x.shape[1]), tq=min(512, x.shape[1]), tm_out=min(512, x.shape[1]))
```

```python
import functools

import jax
import jax.numpy as jnp
from jax import lax
from jax.experimental import pallas as pl
from jax.experimental.pallas import tpu as pltpu
from jax.experimental.pallas import tpu_sc as plsc

F32 = jnp.float32
BF16 = jnp.bfloat16
HIGHEST = lax.Precision.HIGHEST

D_MODEL = 1024
HEAD_DIM = 64
N_HEADS = 8
D_GRP = N_HEADS * HEAD_DIM
RWKV_COLS = 1792
LORA_OFF = 3 * D_GRP
GATE_OFF = LORA_OFF + 128
FOX_MAIN = 4 * D_GRP
N_EXPERTS = 32
TOP_K = 4
EXPERT_BLOCK = 512
SWIGLU_ALPHA = 1.702
SWIGLU_LIMIT = 7.0
NORM_EPS = 1e-6
GN_EPS = 64e-5
LANES = 128
CHUNK = 64
FOX_SUB_KEYS = 256
HEADS_PER_SCAN = 4
SCAN_W = HEADS_PER_SCAN * HEAD_DIM
SEG_TERMS = 1
CUM_TERMS = 2
VMEM_LIMIT = 56 * 1024 * 1024


def _dot(a, b):
    return jnp.dot(a.astype(BF16), b.astype(BF16), preferred_element_type=F32)


def _dot_nt(a, b):
    return lax.dot_general(a.astype(BF16), b.astype(BF16), (((1,), (1,)), ((), ())),
                           preferred_element_type=F32)


def _dot_tn(a, b):
    return lax.dot_general(a.astype(BF16), b.astype(BF16), (((0,), (0,)), ((), ())),
                           preferred_element_type=F32)


def _fdot(a, b):
    return jnp.dot(a, b, precision=HIGHEST, preferred_element_type=F32)


def _split_dot(x, m, terms=2, left=False):
    acc = None
    rem = x
    for _ in range(terms):
        part = rem.astype(BF16)
        rem = rem - part.astype(F32)
        d = (jnp.dot(m, part, preferred_element_type=F32) if left
             else jnp.dot(part, m, preferred_element_type=F32))
        acc = d if acc is None else acc + d
    return acc


def _iota(shape, dim):
    return lax.broadcasted_iota(jnp.int32, shape, dim)


def _seg_reduce_mat(n):
    return (_iota((n, LANES), 0) // HEAD_DIM == _iota((n, LANES), 1)).astype(BF16)


def _seg_expand_mat(n):
    return (_iota((LANES, n), 1) // HEAD_DIM == _iota((LANES, n), 0)).astype(BF16)


def _tri(n, strict):
    r, c = _iota((n, n), 0), _iota((n, n), 1)
    return ((r > c) if strict else (r >= c)).astype(BF16)


D_PACK = D_MODEL // 2


def _pack_rows(x):
    lo = lax.bitcast_convert_type(x[:, :D_PACK].astype(BF16).astype(F32), jnp.uint32)
    hi = lax.bitcast_convert_type(x[:, D_PACK:].astype(BF16).astype(F32), jnp.uint32)
    return hi | (lo >> 16)


def _unpack_rows(p):
    lo = lax.bitcast_convert_type(p << 16, F32)
    hi = lax.bitcast_convert_type(p & jnp.uint32(0xFFFF0000), F32)
    return lo, hi


def _log_sigmoid(z):
    return jnp.minimum(z, 0.0) - jnp.log(1.0 + jnp.exp(-jnp.abs(z)))


def _sigmoid(z):
    return 1.0 / (1.0 + jnp.exp(-z))


def _adaln_kernel(c_ref, w_ref, b_ref, o_ref):
    c = c_ref[...]
    o_ref[...] = _fdot(c * _sigmoid(c), w_ref[...]) + b_ref[...]


def _adaln(c, w_ada, b_ada):
    bsz = c.shape[0]
    n_mod = w_ada.shape[1] // D_MODEL
    return pl.pallas_call(
        _adaln_kernel,
        grid=(n_mod,),
        in_specs=[pl.BlockSpec((bsz, D_MODEL), lambda j: (0, 0)),
                  pl.BlockSpec((D_MODEL, D_MODEL), lambda j: (0, j)),
                  pl.BlockSpec((1, D_MODEL), lambda j: (0, j))],
        out_specs=pl.BlockSpec((bsz, D_MODEL), lambda j: (0, j)),
        out_shape=jax.ShapeDtypeStruct((bsz, n_mod * D_MODEL), F32),
        name="adaln",
    )(c, w_ada, b_ada.reshape(1, -1))


def _inproj_kernel(x_ref, sh_ref, sc_ref, g_ref, wr_ref, wx_ref, wfh_ref, wfl_ref, bf_ref, qkg_ref,
                   pr_ref, px_ref, kb_ref, qb_ref, carry_ref):
    @pl.when(pl.program_id(1) == 0)
    def _():
        carry_ref[...] = jnp.zeros_like(carry_ref)

    x = x_ref[...]
    tm = x.shape[0]
    h = x * lax.rsqrt(jnp.mean(x * x, axis=-1, keepdims=True) + NORM_EPS) * g_ref[...]
    h = h * (1.0 + sc_ref[...]) + sh_ref[...]
    hb = h.astype(BF16)
    h_lo = (h - hb.astype(F32)).astype(BF16)

    pr_ref[...] = jnp.dot(hb, wr_ref[...], preferred_element_type=F32).astype(BF16)

    px = jnp.dot(hb, wx_ref[...], preferred_element_type=F32)
    qk = px[:, :2 * D_GRP]
    ss = _split_dot(qk * qk, _seg_reduce_mat(2 * D_GRP), SEG_TERMS)
    inv = lax.rsqrt(ss * (1.0 / HEAD_DIM) + NORM_EPS)
    qk = qk * _split_dot(inv, _seg_expand_mat(2 * D_GRP), SEG_TERMS) * qkg_ref[...]
    px_ref[:, :2 * D_GRP] = qk.astype(BF16)
    px_ref[:, 2 * D_GRP:] = px[:, 2 * D_GRP:].astype(BF16)

    z = (jnp.dot(hb, wfh_ref[...], preferred_element_type=F32)
         + jnp.dot(h_lo, wfh_ref[...], preferred_element_type=F32)
         + jnp.dot(hb, wfl_ref[...], preferred_element_type=F32)) + bf_ref[...]
    cum = _split_dot(_log_sigmoid(z), _tri(tm, False), terms=3, left=True) + carry_ref[...]
    carry_ref[...] = cum[tm - 1:tm, :]

    src, dst = _iota((LANES, LANES), 0), _iota((LANES, LANES), 1)
    parts = []
    rem = cum
    for _ in range(3):
        part = rem.astype(BF16)
        rem = rem - part.astype(F32)
        parts.append(part)

    def spread(offset):
        return sum(jnp.dot(part, ((dst == 8 * src + offset + t) & (src < N_HEADS)).astype(BF16),
                           preferred_element_type=F32) for t, part in enumerate(parts))

    slot = _iota((1, LANES), 1) % 8
    kb_ref[...] = (jnp.where((slot >= 3) & (slot < 6), 1.0, 0.0) - spread(0)).astype(BF16)
    qb_ref[...] = (jnp.where(slot < 3, 1.0, 0.0) + spread(3)).astype(BF16)


def _inproj(x, shift, scale, g, w_r, w_x, w_f, b_f, qk_gain, tm):
    w_f_hi = w_f.astype(BF16)
    w_f_lo = (w_f - w_f_hi.astype(F32)).astype(BF16)
    bsz, seq, _ = x.shape
    const = lambda b, s: (0, 0)
    return pl.pallas_call(
        _inproj_kernel,
        grid=(bsz, seq // tm),
        in_specs=[pl.BlockSpec((None, tm, D_MODEL), lambda b, s: (b, s, 0)),
                  pl.BlockSpec((None, 1, D_MODEL), lambda b, s: (b, 0, 0)),
                  pl.BlockSpec((None, 1, D_MODEL), lambda b, s: (b, 0, 0)),
                  pl.BlockSpec((1, D_MODEL), const),
                  pl.BlockSpec((D_MODEL, RWKV_COLS), const),
                  pl.BlockSpec((D_MODEL, FOX_MAIN), const),
                  pl.BlockSpec((D_MODEL, LANES), const),
                  pl.BlockSpec((D_MODEL, LANES), const),
                  pl.BlockSpec((1, LANES), const),
                  pl.BlockSpec((1, 2 * D_GRP), const)],
        out_specs=[pl.BlockSpec((None, tm, RWKV_COLS), lambda b, s: (b, s, 0)),
                   pl.BlockSpec((None, tm, FOX_MAIN), lambda b, s: (b, s, 0)),
                   pl.BlockSpec((None, tm, LANES), lambda b, s: (b, s, 0)),
                   pl.BlockSpec((None, tm, LANES), lambda b, s: (b, s, 0))],
        out_shape=[jax.ShapeDtypeStruct((bsz, seq, RWKV_COLS), BF16),
                   jax.ShapeDtypeStruct((bsz, seq, FOX_MAIN), BF16),
                   jax.ShapeDtypeStruct((bsz, seq, LANES), BF16),
                   jax.ShapeDtypeStruct((bsz, seq, LANES), BF16)],
        scratch_shapes=[pltpu.VMEM((1, LANES), F32)],
        compiler_params=pltpu.CompilerParams(
            dimension_semantics=("parallel", "arbitrary"), vmem_limit_bytes=VMEM_LIMIT),
        name="inproj",
    )(x, shift, scale, g, w_r, w_x, w_f_hi, w_f_lo, b_f, qk_gain)


_NN = (((1,), (0,)), ((), ()))
_NT = (((1,), (1,)), ((), ()))
_TN = (((0,), (0,)), ((), ()))
SCAN_N = HEADS_PER_SCAN * CHUNK
BATCH_PER_STEP = 4
INV_LEVELS = 5
M_HEAD, M_STRICT, M_INCL, M_EYE, M_BASE, M_OFF = 0, 1, 2, 3, 4, 5


def _bdot(a, b, dims):
    return lax.dot_general(a, b, dims, preferred_element_type=F32)


def _scan_masks():
    rr, cc = _iota((SCAN_N, SCAN_W), 0), _iota((SCAN_N, SCAN_W), 1)
    ri, ci = _iota((SCAN_N, SCAN_N), 0), _iota((SCAN_N, SCAN_N), 1)
    same = ri // CHUNK == ci // CHUNK
    masks = [rr // CHUNK == cc // HEAD_DIM, same & (ri > ci), same & (ri >= ci), ri == ci,
             (ri // 2 == ci // 2) & (ri > ci)]
    blk = 2
    while blk < CHUNK:
        masks.append((ri // (2 * blk) == ci // (2 * blk)) & (ri // blk != ci // blk) & (ri > ci))
        blk *= 2
    return jnp.stack(masks).astype(BF16)


def _rwkv_kernel(p_ref, masks_ref, mu_ref, w0_ref, w2_ref, a0_ref, a2_ref, g2_ref, kk_ref, ka_ref,
                 rk_ref, gnw_ref, gnb_ref, o_ref, last_ref, state_ref):
    @pl.when(pl.program_id(1) == 0)
    def _():
        last_ref[...] = jnp.zeros_like(last_ref)
        state_ref[...] = jnp.zeros_like(state_ref)

    mu, w0, w2, a0, a2, g2, k_k, k_a, r_k, gn_w, gn_b = (
        ref[...] for ref in (mu_ref, w0_ref, w2_ref, a0_ref, a2_ref, g2_ref, kk_ref, ka_ref,
                             rk_ref, gnw_ref, gnb_ref))
    rows = BATCH_PER_STEP * CHUNK
    p = p_ref[...].astype(F32).reshape(rows, RWKV_COLS)
    row_id = _iota((rows, 1), 0)
    prev = pltpu.roll(p, 1, axis=0)
    for bb in range(BATCH_PER_STEP):
        prev = jnp.where(row_id == bb * CHUNK, last_ref[bb], prev)
        last_ref[bb] = p[(bb + 1) * CHUNK - 1:(bb + 1) * CHUNK, :]
    pf = p + mu * (prev - p)
    r = pf[:, 0:D_GRP]
    k = pf[:, D_GRP:2 * D_GRP]
    v = pf[:, 2 * D_GRP:3 * D_GRP]
    lora = pf[:, LORA_OFF:GATE_OFF]
    gd = pf[:, GATE_OFF:RWKV_COLS]

    wlog = w0 + _dot(jnp.tanh(lora), w2)
    neg = -wlog
    softplus = jnp.maximum(neg, 0.0) + jnp.log(1.0 + jnp.exp(-jnp.abs(neg)))
    ld = -jnp.exp(-softplus - 0.5)
    a = _sigmoid(a0 + _dot(lora, a2))
    g = _dot(_sigmoid(gd), g2)

    red, exp_m = _seg_reduce_mat(D_GRP), _seg_expand_mat(D_GRP)
    kk = k * k_k
    n2 = _split_dot(kk * kk, red, SEG_TERMS)
    kk = kk * _split_dot(1.0 / jnp.maximum(jnp.sqrt(n2), 1e-12), exp_m, SEG_TERMS)
    k2 = k * (1.0 + (a - 1.0) * k_a)

    tr, tc = _iota((rows, rows), 0), _iota((rows, rows), 1)
    tri = ((tr >= tc) & (tr // CHUNK == tc // CHUNK)).astype(BF16)
    cl = _split_dot(ld, tri, terms=CUM_TERMS, left=True)
    cl_end = jnp.concatenate(
        [jnp.broadcast_to(cl[(bb + 1) * CHUNK - 1:(bb + 1) * CHUNK, :], (CHUNK, D_GRP))
         for bb in range(BATCH_PER_STEP)], axis=0)
    e_in = jnp.exp(cl)
    e_out = jnp.exp(-cl)
    e_rem = jnp.exp(cl_end - cl)
    p_end = jnp.exp(cl_end)
    kka = kk * a
    ops = [(-kk * jnp.exp(cl - ld)).astype(BF16), (kka * e_out).astype(BF16),
           (k2 * e_out).astype(BF16), (r * e_in).astype(BF16), v.astype(BF16),
           (kka * e_rem).astype(BF16), (k2 * e_rem).astype(BF16)]

    chains = [(bb, grp) for bb in range(BATCH_PER_STEP)
              for grp in range(N_HEADS // HEADS_PER_SCAN)]
    head_mask = masks_ref[M_HEAD]
    strict, incl = masks_ref[M_STRICT], masks_ref[M_INCL]

    def stacked(op, bb, grp):
        part = op[bb * CHUNK:(bb + 1) * CHUNK, grp * SCAN_W:(grp + 1) * SCAN_W]
        return jnp.concatenate([part] * HEADS_PER_SCAN, axis=0) * head_mask

    xs = [[stacked(op, bb, grp) for op in ops] for bb, grp in chains]
    st = [state_ref[bb, grp] for bb, grp in chains]
    sb = [s.astype(BF16) for s in st]
    nab = [_bdot(x[0], x[1], _NT).astype(BF16) for x in xs]
    aak = [_bdot(x[0], x[2], _NT).astype(BF16) * strict for x in xs]
    arb = [_bdot(x[3], x[1], _NT).astype(BF16) * incl for x in xs]
    ark = [_bdot(x[3], x[2], _NT).astype(BF16) * incl for x in xs]
    t_inv = [masks_ref[M_EYE] + n * masks_ref[M_BASE] for n in nab]
    for lvl in range(INV_LEVELS):
        half = [_bdot(t, n * masks_ref[M_OFF + lvl], _NN).astype(BF16) for t, n in zip(t_inv, nab)]
        t_inv = [t + _bdot(h, t, _NN).astype(BF16) for t, h in zip(t_inv, half)]
    rhs = [(_bdot(x[0], s, _NT) + _bdot(k, x[4], _NN)).astype(BF16)
           for x, s, k in zip(xs, sb, aak)]
    sa = [_bdot(t, h, _NN).astype(BF16) for t, h in zip(t_inv, rhs)]
    ys = [_bdot(x[3], s, _NT) + _bdot(b, u, _NN) + _bdot(k, x[4], _NN)
          for x, s, b, u, k in zip(xs, sb, arb, sa, ark)]
    for (bb, grp), x, s, u in zip(chains, xs, st, sa):
        decay = p_end[bb * CHUNK:bb * CHUNK + 1, grp * SCAN_W:(grp + 1) * SCAN_W]
        state_ref[bb, grp] = s * decay + _bdot(u, x[5], _TN) + _bdot(x[4], x[6], _TN)
    ys = [y[0:CHUNK] + y[CHUNK:2 * CHUNK] + y[2 * CHUNK:3 * CHUNK] + y[3 * CHUNK:4 * CHUNK]
          for y in ys]
    n_grp = N_HEADS // HEADS_PER_SCAN
    y = jnp.concatenate([jnp.concatenate(ys[bb * n_grp:(bb + 1) * n_grp], axis=1)
                         for bb in range(BATCH_PER_STEP)], axis=0)

    mean = _split_dot(_split_dot(y, red, SEG_TERMS) * (1.0 / HEAD_DIM), exp_m, SEG_TERMS)
    d = y - mean
    var = _split_dot(d * d, red, SEG_TERMS) * (1.0 / HEAD_DIM)
    yn = d * _split_dot(lax.rsqrt(var + GN_EPS), exp_m, SEG_TERMS) * gn_w + gn_b
    bonus = _split_dot(_split_dot(r * k2 * r_k, red, SEG_TERMS), exp_m, SEG_TERMS) * v
    o_ref[...] = ((yn + bonus) * g).astype(BF16).reshape(BATCH_PER_STEP, CHUNK, D_GRP)


def _rwkv(p_r, mu, w0, w2p, a0, a2p, g2, k_k, k_a, r_k, gn_w, gn_b):
    bsz, seq, _ = p_r.shape
    assert bsz % BATCH_PER_STEP == 0
    masks = _scan_masks()
    const = lambda b, s: (0, 0)
    vec = pl.BlockSpec((1, D_GRP), const)
    return pl.pallas_call(
        _rwkv_kernel,
        grid=(bsz // BATCH_PER_STEP, seq // CHUNK),
        in_specs=[pl.BlockSpec((BATCH_PER_STEP, CHUNK, RWKV_COLS), lambda b, s: (b, s, 0)),
                  pl.BlockSpec(masks.shape, lambda b, s: (0, 0, 0)),
                  pl.BlockSpec((1, RWKV_COLS), const),
                  vec, pl.BlockSpec((LANES, D_GRP), const),
                  vec, pl.BlockSpec((LANES, D_GRP), const),
                  pl.BlockSpec((LANES, D_GRP), const),
                  vec, vec, vec, vec, vec],
        out_specs=pl.BlockSpec((BATCH_PER_STEP, CHUNK, D_GRP), lambda b, s: (b, s, 0)),
        out_shape=jax.ShapeDtypeStruct((bsz, seq, D_GRP), BF16),
        scratch_shapes=[pltpu.VMEM((BATCH_PER_STEP, 1, RWKV_COLS), F32),
                        pltpu.VMEM((BATCH_PER_STEP, N_HEADS // HEADS_PER_SCAN, SCAN_W, SCAN_W), F32)],
        compiler_params=pltpu.CompilerParams(
            dimension_semantics=("parallel", "arbitrary"), vmem_limit_bytes=VMEM_LIMIT),
        name="rwkv",
    )(p_r, masks, mu, w0, w2p, a0, a2p, g2, k_k, k_a, r_k, gn_w, gn_b)


def _fox_kernel(q_ref, qb_ref, k_ref, kb_ref, vt_ref, og_ref, ong_ref, o_ref, *, tq):
    hp = pl.program_id(1)
    qi = pl.program_id(2)
    lane = _iota((1, LANES), 1)
    q = q_ref[...]
    qb = qb_ref[...]
    zero = jnp.zeros_like(q)
    qcat = [jnp.concatenate([jnp.where(lane // HEAD_DIM == hh, q, zero),
                             jnp.where(lane // 8 == hp * 2 + hh, qb, zero)], axis=1)
            for hh in range(2)]
    FOX_KEYS = min(FOX_SUB_KEYS, tq)
    n_sub = tq // FOX_KEYS
    key_pos = _iota((FOX_KEYS, tq), 0)
    qry_pos = _iota((FOX_KEYS, tq), 1)

    def tile(j, stats, masked):
        base = j * tq
        sts = []
        for s in range(n_sub):
            start = pl.multiple_of(base + s * FOX_KEYS, FOX_KEYS)
            kcat = jnp.concatenate([k_ref[pl.ds(start, FOX_KEYS), :],
                                    kb_ref[pl.ds(start, FOX_KEYS), :]], axis=1)
            sts.append([lax.dot_general(kcat, qc, _NT, preferred_element_type=F32)
                        for qc in qcat])
        for s in range(n_sub):
            start = pl.multiple_of(base + s * FOX_KEYS, FOX_KEYS)
            vt = vt_ref[:, pl.ds(start, FOX_KEYS)]
            st = sts[s]
            if masked:
                st = [jnp.where(qry_pos >= key_pos + s * FOX_KEYS, x, -jnp.inf) for x in st]
            m_new = [jnp.maximum(c[0], jnp.max(x, axis=0, keepdims=True))
                     for c, x in zip(stats, st)]
            pts = [jnp.exp(x - m) for x, m in zip(st, m_new)]
            pvs = [jnp.dot(vt, pt.astype(BF16), preferred_element_type=F32) for pt in pts]
            new = []
            for hh in range(2):
                m, l, acc = stats[hh]
                alpha = jnp.exp(m - m_new[hh])
                l = alpha * l + jnp.sum(pts[hh], axis=0, keepdims=True)
                acc = alpha * acc + pvs[hh][hh * HEAD_DIM:(hh + 1) * HEAD_DIM, :]
                new.append((m_new[hh], l, acc))
            stats = tuple(new)
        return stats

    init = (jnp.full((1, tq), -jnp.inf, F32), jnp.zeros((1, tq), F32),
            jnp.zeros((HEAD_DIM, tq), F32))
    stats = lax.fori_loop(0, qi, functools.partial(tile, masked=False), (init, init))
    outs = []
    for _, l, acc in tile(qi, stats, True):
        o = acc / l
        outs.append(o * lax.rsqrt(jnp.mean(o * o, axis=0, keepdims=True) + NORM_EPS))
    o = jnp.concatenate(outs, axis=0).T
    o_ref[...] = (o * ong_ref[...] * _sigmoid(og_ref[...].astype(F32))).astype(BF16)


def _fox(p_x, k_bias, q_bias, o_gain, tq):
    bsz, seq, _ = p_x.shape
    npair = N_HEADS // 2
    v_t = jnp.transpose(p_x[:, :, 2 * D_GRP:3 * D_GRP], (0, 2, 1))
    return pl.pallas_call(
        functools.partial(_fox_kernel, tq=tq),
        grid=(bsz, npair, seq // tq),
        in_specs=[pl.BlockSpec((None, tq, LANES), lambda b, h, i: (b, i, h)),
                  pl.BlockSpec((None, tq, LANES), lambda b, h, i: (b, i, 0)),
                  pl.BlockSpec((None, seq, LANES), lambda b, h, i: (b, 0, npair + h)),
                  pl.BlockSpec((None, seq, LANES), lambda b, h, i: (b, 0, 0)),
                  pl.BlockSpec((None, LANES, seq), lambda b, h, i: (b, h, 0)),
                  pl.BlockSpec((None, tq, LANES), lambda b, h, i: (b, i, 3 * npair + h)),
                  pl.BlockSpec((1, LANES), lambda b, h, i: (0, 0))],
        out_specs=pl.BlockSpec((None, tq, LANES), lambda b, h, i: (b, i, h)),
        out_shape=jax.ShapeDtypeStruct((bsz, seq, D_GRP), BF16),
        compiler_params=pltpu.CompilerParams(
            dimension_semantics=("parallel", "parallel", "arbitrary"),
            vmem_limit_bytes=VMEM_LIMIT),
        name="fox",
    )(p_x, q_bias, p_x, k_bias, v_t, p_x, o_gain)


def _outproj_kernel(x_ref, yr_ref, yf_ref, g1_ref, sh_ref, sc_ref, ng_ref, wor_ref, wof_ref,
                    wrt_ref, wrl_ref, brt_ref, x1_ref, h2_ref, idx_ref, gate_ref, rank_ref, cnt_ref,
                    carry_ref):
    @pl.when(pl.program_id(0) == 0)
    def _():
        carry_ref[...] = jnp.zeros_like(carry_ref)

    y = (jnp.dot(yr_ref[...], wor_ref[...], preferred_element_type=F32)
         + jnp.dot(yf_ref[...], wof_ref[...], preferred_element_type=F32))
    x1 = x_ref[...] + g1_ref[...] * y
    x1_ref[...] = x1
    tm = x1.shape[0]
    h = x1 * lax.rsqrt(jnp.mean(x1 * x1, axis=-1, keepdims=True) + NORM_EPS) * ng_ref[...]
    h2 = h * (1.0 + sc_ref[...]) + sh_ref[...]
    h2_ref[...] = _pack_rows(h2)

    lane = _iota((tm, LANES), 1)
    h_hi = h2.astype(BF16)
    h_lo = (h2 - h_hi.astype(F32)).astype(BF16)
    logits = (jnp.dot(h_hi, wrt_ref[...], preferred_element_type=F32)
              + jnp.dot(h_lo, wrt_ref[...], preferred_element_type=F32)
              + jnp.dot(h_hi, wrl_ref[...], preferred_element_type=F32)) + brt_ref[...]
    lg = jnp.where(lane < N_EXPERTS, logits, -jnp.inf)
    picks = []
    hot_sum = jnp.zeros((tm, LANES), F32)
    for _ in range(TOP_K):
        m = jnp.max(lg, axis=-1, keepdims=True)
        sel = jnp.min(jnp.where(lg == m, lane, LANES), axis=-1, keepdims=True)
        hot = lane == sel
        picks.append((m, sel, hot))
        hot_sum = hot_sum + hot.astype(F32)
        lg = jnp.where(hot, -jnp.inf, lg)
    es = [jnp.exp(m - picks[0][0]) for m, _, _ in picks]
    den = es[0] + es[1] + es[2] + es[3]

    before = jnp.dot(_tri(tm, True), hot_sum.astype(BF16), preferred_element_type=F32)
    before = before + carry_ref[...]
    idx_out = jnp.zeros((tm, LANES), jnp.int32)
    gate_out = jnp.zeros((tm, LANES), F32)
    rank_out = jnp.zeros((tm, LANES), jnp.int32)
    for kk, (m, sel, hot) in enumerate(picks):
        rk = jnp.sum(jnp.where(hot, before, 0.0), axis=-1, keepdims=True).astype(jnp.int32)
        idx_out = jnp.where(lane == kk, sel, idx_out)
        gate_out = jnp.where(lane == kk, es[kk] / den, gate_out)
        rank_out = jnp.where(lane == kk, rk, rank_out)
    idx_ref[...] = idx_out
    gate_ref[...] = gate_out
    rank_ref[...] = rank_out
    carry_ref[...] = carry_ref[...] + jnp.sum(hot_sum, axis=0, keepdims=True)
    cnt_ref[...] = carry_ref[...]


def _outproj(x2d, y_r, y_f, gate1, shift2, scale2, norm_g, wo_r, wo_f, w_rt, b_rt, tm, seq,
             row0, t):
    w_rt_hi = w_rt.astype(BF16)
    w_rt_lo = (w_rt - w_rt_hi.astype(F32)).astype(BF16)
    per_b = seq // tm
    blk0 = row0 // tm
    const = lambda i: (0, 0)
    rows = lambda i: (i, 0)
    rows_in = lambda i: (i + blk0, 0)
    mod = pl.BlockSpec((None, 1, D_MODEL), lambda i: ((i + blk0) // per_b, 0, 0))
    return pl.pallas_call(
        _outproj_kernel,
        grid=(t // tm,),
        in_specs=[pl.BlockSpec((tm, D_MODEL), rows_in),
                  pl.BlockSpec((tm, D_GRP), rows_in),
                  pl.BlockSpec((tm, D_GRP), rows_in),
                  mod, mod, mod,
                  pl.BlockSpec((1, D_MODEL), const),
                  pl.BlockSpec((D_GRP, D_MODEL), const),
                  pl.BlockSpec((D_GRP, D_MODEL), const),
                  pl.BlockSpec((D_MODEL, LANES), const),
                  pl.BlockSpec((D_MODEL, LANES), const),
                  pl.BlockSpec((1, LANES), const)],
        out_specs=[pl.BlockSpec((tm, D_MODEL), rows),
                   pl.BlockSpec((tm, D_PACK), rows),
                   pl.BlockSpec((tm, LANES), rows),
                   pl.BlockSpec((tm, LANES), rows),
                   pl.BlockSpec((tm, LANES), rows),
                   pl.BlockSpec((1, LANES), const)],
        out_shape=[jax.ShapeDtypeStruct((t, D_MODEL), F32),
                   jax.ShapeDtypeStruct((t, D_PACK), jnp.uint32),
                   jax.ShapeDtypeStruct((t, LANES), jnp.int32),
                   jax.ShapeDtypeStruct((t, LANES), F32),
                   jax.ShapeDtypeStruct((t, LANES), jnp.int32),
                   jax.ShapeDtypeStruct((1, LANES), F32)],
        scratch_shapes=[pltpu.VMEM((1, LANES), F32)],
        compiler_params=pltpu.CompilerParams(
            dimension_semantics=("arbitrary",), vmem_limit_bytes=VMEM_LIMIT),
        name="outproj",
    )(x2d, y_r, y_f, gate1, shift2, scale2, norm_g, wo_r, wo_f, w_rt_hi, w_rt_lo, b_rt)


SC_CORES = 2
SC_SUBCORES = 16
SC_ROWS = 32


def _sc_gather_rows(idx, src):
    n_workers = SC_CORES * SC_SUBCORES
    m = idx.shape[0]
    d = src.shape[1]
    assert m % (n_workers * SC_ROWS) == 0
    n_chunks = m // (n_workers * SC_ROWS)
    mesh = plsc.VectorSubcoreMesh(core_axis_name="c", subcore_axis_name="s")

    @functools.partial(
        pl.kernel, mesh=mesh,
        out_type=jax.ShapeDtypeStruct((m, d), src.dtype),
        scratch_types=[pltpu.VMEM((n_chunks, SC_ROWS), jnp.int32),
                       pltpu.VMEM((SC_ROWS, d), src.dtype),
                       pltpu.SemaphoreType.DMA],
        name="sc_gather")
    def gather(src_hbm, idx_hbm, out_hbm, idx_v, rows_v, sem):
        wid = lax.axis_index("s") * SC_CORES + lax.axis_index("c")
        pltpu.sync_copy(idx_hbm.at[wid], idx_v)

        @pl.loop(0, n_chunks)
        def _(j):
            pltpu.async_copy(src_hbm.at[idx_v.at[j]], rows_v, sem).wait()
            pltpu.sync_copy(rows_v, out_hbm.at[pl.ds((wid * n_chunks + j) * SC_ROWS, SC_ROWS)])

    return gather(src, idx.reshape(n_workers, n_chunks, SC_ROWS))


def _sc_scatter_rows(src, dest, n_out):
    n_workers = SC_CORES * SC_SUBCORES
    t, d = src.shape
    n_slot = dest.shape[1]
    assert t % (n_workers * SC_ROWS) == 0
    n_chunks = t // (n_workers * SC_ROWS)
    mesh = plsc.VectorSubcoreMesh(core_axis_name="c", subcore_axis_name="s")
    idx = dest.reshape(n_workers, n_chunks, SC_ROWS, n_slot).transpose(0, 1, 3, 2)
    idx = idx.reshape(n_workers, n_chunks * n_slot, SC_ROWS)

    @functools.partial(
        pl.kernel, mesh=mesh,
        out_type=jax.ShapeDtypeStruct((n_out, d), src.dtype),
        scratch_types=[pltpu.VMEM((n_chunks * n_slot, SC_ROWS), jnp.int32),
                       pltpu.VMEM((SC_ROWS, d), src.dtype)],
        name="sc_scatter")
    def scatter(src_hbm, idx_hbm, out_hbm, idx_v, rows_v):
        wid = lax.axis_index("s") * SC_CORES + lax.axis_index("c")
        pltpu.sync_copy(idx_hbm.at[wid], idx_v)

        @pl.loop(0, n_chunks)
        def _(j):
            pltpu.sync_copy(src_hbm.at[pl.ds((wid * n_chunks + j) * SC_ROWS, SC_ROWS)], rows_v)
            for k in range(n_slot):
                pltpu.sync_copy(rows_v, out_hbm.at[idx_v.at[j * n_slot + k]])

    return scatter(src, idx)


def _expert_kernel(be_ref, nv_ref, x_ref, wgu_ref, bgu_ref, wd_ref, bd_ref, o_ref):
    del be_ref
    valid = _iota((EXPERT_BLOCK, 1), 0) < nv_ref[pl.program_id(0)]
    lo, hi = _unpack_rows(jnp.where(valid, x_ref[...], jnp.uint32(0)))
    x = jnp.concatenate([lo.astype(BF16), hi.astype(BF16)], axis=1)
    gu = jnp.dot(x, wgu_ref[...], preferred_element_type=F32) + bgu_ref[...]
    gate = jnp.minimum(gu[:, :D_MODEL], SWIGLU_LIMIT)
    up = jnp.clip(gu[:, D_MODEL:], -SWIGLU_LIMIT, SWIGLU_LIMIT)
    act = gate * _sigmoid(SWIGLU_ALPHA * gate) * (up + 1.0)
    o_ref[...] = _pack_rows(
        jnp.dot(act.astype(BF16), wd_ref[...], preferred_element_type=F32) + bd_ref[...])


def _experts(block_e, n_valid, xs, w_gu, b_gu, w_d, b_d):
    n_blocks = block_e.shape[0]
    grid_spec = pltpu.PrefetchScalarGridSpec(
        num_scalar_prefetch=2,
        grid=(n_blocks,),
        in_specs=[pl.BlockSpec((EXPERT_BLOCK, D_PACK), lambda j, be, nv: (j, 0)),
                  pl.BlockSpec((None, D_MODEL, 2 * D_MODEL), lambda j, be, nv: (be[j], 0, 0)),
                  pl.BlockSpec((None, 1, 2 * D_MODEL), lambda j, be, nv: (be[j], 0, 0)),
                  pl.BlockSpec((None, D_MODEL, D_MODEL), lambda j, be, nv: (be[j], 0, 0)),
                  pl.BlockSpec((None, 1, D_MODEL), lambda j, be, nv: (be[j], 0, 0))],
        out_specs=pl.BlockSpec((EXPERT_BLOCK, D_PACK), lambda j, be, nv: (j, 0)),
    )
    return pl.pallas_call(
        _expert_kernel,
        grid_spec=grid_spec,
        out_shape=jax.ShapeDtypeStruct(xs.shape, jnp.uint32),
        compiler_params=pltpu.CompilerParams(
            dimension_semantics=("arbitrary",), vmem_limit_bytes=VMEM_LIMIT),
        name="experts",
    )(block_e, n_valid, xs, w_gu, b_gu, w_d, b_d)


COMBINE_TOKENS = 512
MOE_SPLITS = 2


def _combine_kernel(yg_ref, x1_ref, gate_ref, g2_ref, fg_ref, o_ref):
    gates = gate_ref[...]
    acc_lo = acc_hi = None
    for kk in range(TOP_K):
        lo, hi = _unpack_rows(yg_ref[kk * COMBINE_TOKENS:(kk + 1) * COMBINE_TOKENS, :])
        g = gates[:, kk:kk + 1]
        acc_lo = g * lo if acc_lo is None else acc_lo + g * lo
        acc_hi = g * hi if acc_hi is None else acc_hi + g * hi
    x2 = x1_ref[...] + g2_ref[...] * jnp.concatenate([acc_lo, acc_hi], axis=1)
    o_ref[...] = x2 * lax.rsqrt(jnp.mean(x2 * x2, axis=-1, keepdims=True) + NORM_EPS) * fg_ref[...]


def _combine_kernel_into(prev_ref, *refs):
    del prev_ref
    _combine_kernel(*refs)


def _combine(yg, x1, gates, gate2, final_g, seq, row0, t_total, prev):
    t = x1.shape[0]
    tm = COMBINE_TOKENS
    per_b = seq // tm
    blk0 = row0 // tm
    rows = lambda i: (i, 0)
    in_specs = [pl.BlockSpec((TOP_K * tm, D_PACK), rows),
                pl.BlockSpec((tm, D_MODEL), rows),
                pl.BlockSpec((tm, LANES), rows),
                pl.BlockSpec((None, 1, D_MODEL), lambda i: ((i + blk0) // per_b, 0, 0)),
                pl.BlockSpec((1, D_MODEL), lambda i: (0, 0))]
    args = (yg, x1, gates, gate2, final_g)
    if prev is not None:
        in_specs = [pl.BlockSpec(memory_space=pl.ANY)] + in_specs
        args = (prev,) + args
    return pl.pallas_call(
        _combine_kernel if prev is None else _combine_kernel_into,
        grid=(t // tm,),
        in_specs=in_specs,
        out_specs=pl.BlockSpec((tm, D_MODEL), lambda i: (i + blk0, 0)),
        out_shape=jax.ShapeDtypeStruct((t_total, D_MODEL), F32),
        input_output_aliases={} if prev is None else {0: 0},
        compiler_params=pltpu.CompilerParams(
            dimension_semantics=("parallel",), vmem_limit_bytes=VMEM_LIMIT),
        name="combine",
    )(*args)


def _moe(h2, idx, gates, rank, counts, x1, gate2, final_g, w_gu, b_gu, w_d, b_d, seq,
         row0, t_total, prev):
    t = h2.shape[0]
    n_slots = t * TOP_K
    n_blocks = -(-n_slots // EXPERT_BLOCK) + N_EXPERTS
    cap = n_blocks * EXPERT_BLOCK
    padded = (counts + EXPERT_BLOCK - 1) // EXPERT_BLOCK * EXPERT_BLOCK
    pad_ends = jnp.cumsum(padded)
    pad_starts = pad_ends - padded
    dest = pad_starts[idx] + rank
    block_starts = jnp.arange(n_blocks, dtype=jnp.int32) * EXPERT_BLOCK
    block_e = jnp.minimum(jnp.sum(block_starts[:, None] >= pad_ends[None, :], axis=1),
                          N_EXPERTS - 1).astype(jnp.int32)
    n_valid = jnp.clip(counts[block_e] - (block_starts - pad_starts[block_e]), 0, EXPERT_BLOCK)

    xs = _sc_scatter_rows(h2, dest, cap)
    yb = _experts(block_e, n_valid.astype(jnp.int32), xs, w_gu, b_gu, w_d, b_d)
    dest_blocks = dest.reshape(-1, COMBINE_TOKENS, TOP_K).transpose(0, 2, 1).reshape(-1)
    yg = _sc_gather_rows(dest_blocks, yb)
    return _combine(yg, x1, gates, gate2, final_g, seq, row0, t_total, prev)


def _layer(x, c_mod, norm1_g, w_in, mu_shift, w0, w2, a0, a2, g2, k_k, k_a, r_k, gn_w, gn_b, b_f,
           q_norm_g, k_norm_g, o_norm_g, w_out, norm2_g, w_router, b_router, w_gate_up,
           b_gate_up, w_down, b_down, final_g, tm_in, tq, tm_out):
    bsz, seq, _ = x.shape
    shift1, scale1, gate1, shift2, scale2, gate2 = (
        m.reshape(bsz, 1, D_MODEL) for m in jnp.split(c_mod, 6, axis=-1))
    row = lambda v: v.reshape(1, -1)

    w_r = w_in[:, :RWKV_COLS].astype(BF16)
    w_x = w_in[:, RWKV_COLS:RWKV_COLS + FOX_MAIN].astype(BF16)
    w_f = jnp.pad(w_in[:, RWKV_COLS + FOX_MAIN:], ((0, 0), (0, LANES - N_HEADS)))
    b_fp = jnp.pad(b_f, (0, LANES - N_HEADS)).reshape(1, LANES)
    qk_gain = jnp.concatenate([jnp.tile(q_norm_g, N_HEADS) * HEAD_DIM ** -0.5,
                               jnp.tile(k_norm_g, N_HEADS)]).reshape(1, -1)
    p_r, p_x, k_bias, q_bias = _inproj(x, shift1, scale1, row(norm1_g), w_r, w_x, w_f, b_fp,
                                       qk_gain, tm_in)

    zeros = jnp.zeros((LANES - 64, D_GRP), F32)
    w2p = jnp.concatenate([w2, zeros], axis=0).astype(BF16)
    a2p = jnp.concatenate([zeros, a2], axis=0).astype(BF16)
    y_r = _rwkv(p_r, row(mu_shift), row(w0), w2p, row(a0), a2p, g2.astype(BF16), row(k_k),
                row(k_a), row(r_k), row(gn_w), row(gn_b))

    y_f = _fox(p_x, k_bias, q_bias, jnp.tile(o_norm_g, 2).reshape(1, LANES), tq)

    t = bsz * seq
    w_rt = jnp.pad(w_router, ((0, 0), (0, LANES - N_EXPERTS)))
    b_rt = jnp.pad(b_router, (0, LANES - N_EXPERTS)).reshape(1, LANES)
    wo = w_out.astype(BF16)
    w_gu, w_d = w_gate_up.astype(BF16), w_down.astype(BF16)
    b_gu, b_d = b_gate_up.reshape(N_EXPERTS, 1, -1), b_down.reshape(N_EXPERTS, 1, -1)
    t_part = t // MOE_SPLITS
    out = None
    for part in range(MOE_SPLITS):
        row0 = part * t_part
        x1, h2, idx, gates, rank, cnt = _outproj(
            x.reshape(t, D_MODEL), y_r.reshape(t, D_GRP), y_f.reshape(t, D_GRP), gate1, shift2,
            scale2, row(norm2_g), wo[:D_GRP], wo[D_GRP:], w_rt, b_rt, tm_out, seq, row0, t_part)
        counts = cnt[0, :N_EXPERTS].astype(jnp.int32)
        out = _moe(h2, idx[:, :TOP_K], gates, rank[:, :TOP_K], counts, x1, gate2, row(final_g),
                   w_gu, b_gu, w_d, b_d, seq, row0, t, out)
    return out.reshape(bsz, seq, D_MODEL)


def kernel(x, c, w_ada, b_ada, norm1_g, w_in, mu_shift, w0, w2, a0, a2, g2, k_k, k_a, r_k, gn_w,
           gn_b, b_f, q_norm_g, k_norm_g, o_norm_g, w_out, norm2_g, w_router, b_router, w_gate_up,
           b_gate_up, w_down, b_down, final_g):
    assert w_ada.shape[0] == 1, "single-layer block"
    c_mod = _adaln(c, w_ada[0], b_ada[0])
    return _layer(x, c_mod, norm1_g[0], w_in[0], mu_shift[0], w0[0], w2[0], a0[0], a2[0], g2[0],
                  k_k[0], k_a[0], r_k[0], gn_w[0], gn_b[0], b_f[0], q_norm_g[0], k_norm_g[0],
                  o_norm_g[0], w_out[0], norm2_g[0], w_router[0], b_router[0], w_gate_up[0],
                  b_gate_up[0], w_down[0], b_down[0], final_g,
                  tm_in=min(512, x.shape[1]), tq=min(512, x.shape[1]), tm_out=min(1024, x.shape[1]))
```

```python
import functools

import jax
import jax.numpy as jnp
from jax import lax
from jax.experimental import pallas as pl
from jax.experimental.pallas import tpu as pltpu
from jax.experimental.pallas import tpu_sc as plsc

F32 = jnp.float32
BF16 = jnp.bfloat16
HIGHEST = lax.Precision.HIGHEST

D_MODEL = 1024
HEAD_DIM = 64
N_HEADS = 8
D_GRP = N_HEADS * HEAD_DIM
RWKV_COLS = 1792
LORA_OFF = 3 * D_GRP
GATE_OFF = LORA_OFF + 128
FOX_MAIN = 4 * D_GRP
N_EXPERTS = 32
TOP_K = 4
EXPERT_BLOCK = 512
SWIGLU_ALPHA = 1.702
SWIGLU_LIMIT = 7.0
NORM_EPS = 1e-6
GN_EPS = 64e-5
LANES = 128
CHUNK = 64
FOX_SUB_KEYS = 256
HEADS_PER_SCAN = 4
SCAN_W = HEADS_PER_SCAN * HEAD_DIM
SEG_TERMS = 1
CUM_TERMS = 2
VMEM_LIMIT = 56 * 1024 * 1024


def _dot(a, b):
    return jnp.dot(a.astype(BF16), b.astype(BF16), preferred_element_type=F32)


def _dot_nt(a, b):
    return lax.dot_general(a.astype(BF16), b.astype(BF16), (((1,), (1,)), ((), ())),
                           preferred_element_type=F32)


def _dot_tn(a, b):
    return lax.dot_general(a.astype(BF16), b.astype(BF16), (((0,), (0,)), ((), ())),
                           preferred_element_type=F32)


def _fdot(a, b):
    return jnp.dot(a, b, precision=HIGHEST, preferred_element_type=F32)


def _split_dot(x, m, terms=2, left=False):
    acc = None
    rem = x
    for _ in range(terms):
        part = rem.astype(BF16)
        rem = rem - part.astype(F32)
        d = (jnp.dot(m, part, preferred_element_type=F32) if left
             else jnp.dot(part, m, preferred_element_type=F32))
        acc = d if acc is None else acc + d
    return acc


def _iota(shape, dim):
    return lax.broadcasted_iota(jnp.int32, shape, dim)


def _seg_reduce_mat(n):
    return (_iota((n, LANES), 0) // HEAD_DIM == _iota((n, LANES), 1)).astype(BF16)


def _seg_expand_mat(n):
    return (_iota((LANES, n), 1) // HEAD_DIM == _iota((LANES, n), 0)).astype(BF16)


def _tri(n, strict):
    r, c = _iota((n, n), 0), _iota((n, n), 1)
    return ((r > c) if strict else (r >= c)).astype(BF16)


D_PACK = D_MODEL // 2


def _pack_rows(x):
    lo = lax.bitcast_convert_type(x[:, :D_PACK].astype(BF16).astype(F32), jnp.uint32)
    hi = lax.bitcast_convert_type(x[:, D_PACK:].astype(BF16).astype(F32), jnp.uint32)
    return hi | (lo >> 16)


def _unpack_rows(p):
    lo = lax.bitcast_convert_type(p << 16, F32)
    hi = lax.bitcast_convert_type(p & jnp.uint32(0xFFFF0000), F32)
    return lo, hi


def _log_sigmoid(z):
    return jnp.minimum(z, 0.0) - jnp.log(1.0 + jnp.exp(-jnp.abs(z)))


def _sigmoid(z):
    return 1.0 / (1.0 + jnp.exp(-z))


def _adaln_kernel(c_ref, w_ref, b_ref, o_ref):
    c = c_ref[...]
    o_ref[...] = _fdot(c * _sigmoid(c), w_ref[...]) + b_ref[...]


def _adaln(c, w_ada, b_ada):
    bsz = c.shape[0]
    n_mod = w_ada.shape[1] // D_MODEL
    return pl.pallas_call(
        _adaln_kernel,
        grid=(n_mod,),
        in_specs=[pl.BlockSpec((bsz, D_MODEL), lambda j: (0, 0)),
                  pl.BlockSpec((D_MODEL, D_MODEL), lambda j: (0, j)),
                  pl.BlockSpec((1, D_MODEL), lambda j: (0, j))],
        out_specs=pl.BlockSpec((bsz, D_MODEL), lambda j: (0, j)),
        out_shape=jax.ShapeDtypeStruct((bsz, n_mod * D_MODEL), F32),
        name="adaln",
    )(c, w_ada, b_ada.reshape(1, -1))


def _inproj_kernel(x_ref, sh_ref, sc_ref, g_ref, wr_ref, wx_ref, wfh_ref, wfl_ref, bf_ref, qkg_ref,
                   pr_ref, px_ref, kb_ref, qb_ref, carry_ref):
    @pl.when(pl.program_id(1) == 0)
    def _():
        carry_ref[...] = jnp.zeros_like(carry_ref)

    x = x_ref[...]
    tm = x.shape[0]
    h = x * lax.rsqrt(jnp.mean(x * x, axis=-1, keepdims=True) + NORM_EPS) * g_ref[...]
    h = h * (1.0 + sc_ref[...]) + sh_ref[...]
    hb = h.astype(BF16)
    h_lo = (h - hb.astype(F32)).astype(BF16)

    pr_ref[...] = jnp.dot(hb, wr_ref[...], preferred_element_type=F32).astype(BF16)

    px = jnp.dot(hb, wx_ref[...], preferred_element_type=F32)
    qk = px[:, :2 * D_GRP]
    ss = _split_dot(qk * qk, _seg_reduce_mat(2 * D_GRP), SEG_TERMS)
    inv = lax.rsqrt(ss * (1.0 / HEAD_DIM) + NORM_EPS)
    qk = qk * _split_dot(inv, _seg_expand_mat(2 * D_GRP), SEG_TERMS) * qkg_ref[...]
    px_ref[:, :2 * D_GRP] = qk.astype(BF16)
    px_ref[:, 2 * D_GRP:] = px[:, 2 * D_GRP:].astype(BF16)

    z = (jnp.dot(hb, wfh_ref[...], preferred_element_type=F32)
         + jnp.dot(h_lo, wfh_ref[...], preferred_element_type=F32)
         + jnp.dot(hb, wfl_ref[...], preferred_element_type=F32)) + bf_ref[...]
    cum = _split_dot(_log_sigmoid(z), _tri(tm, False), terms=3, left=True) + carry_ref[...]
    carry_ref[...] = cum[tm - 1:tm, :]

    src, dst = _iota((LANES, LANES), 0), _iota((LANES, LANES), 1)
    parts = []
    rem = cum
    for _ in range(3):
        part = rem.astype(BF16)
        rem = rem - part.astype(F32)
        parts.append(part)

    def spread(offset):
        return sum(jnp.dot(part, ((dst == 8 * src + offset + t) & (src < N_HEADS)).astype(BF16),
                           preferred_element_type=F32) for t, part in enumerate(parts))

    slot = _iota((1, LANES), 1) % 8
    kb_ref[...] = (jnp.where((slot >= 3) & (slot < 6), 1.0, 0.0) - spread(0)).astype(BF16)
    qb_ref[...] = (jnp.where(slot < 3, 1.0, 0.0) + spread(3)).astype(BF16)


def _inproj(x, shift, scale, g, w_r, w_x, w_f, b_f, qk_gain, tm):
    w_f_hi = w_f.astype(BF16)
    w_f_lo = (w_f - w_f_hi.astype(F32)).astype(BF16)
    bsz, seq, _ = x.shape
    const = lambda b, s: (0, 0)
    return pl.pallas_call(
        _inproj_kernel,
        grid=(bsz, seq // tm),
        in_specs=[pl.BlockSpec((None, tm, D_MODEL), lambda b, s: (b, s, 0)),
                  pl.BlockSpec((None, 1, D_MODEL), lambda b, s: (b, 0, 0)),
                  pl.BlockSpec((None, 1, D_MODEL), lambda b, s: (b, 0, 0)),
                  pl.BlockSpec((1, D_MODEL), const),
                  pl.BlockSpec((D_MODEL, RWKV_COLS), const),
                  pl.BlockSpec((D_MODEL, FOX_MAIN), const),
                  pl.BlockSpec((D_MODEL, LANES), const),
                  pl.BlockSpec((D_MODEL, LANES), const),
                  pl.BlockSpec((1, LANES), const),
                  pl.BlockSpec((1, 2 * D_GRP), const)],
        out_specs=[pl.BlockSpec((None, tm, RWKV_COLS), lambda b, s: (b, s, 0)),
                   pl.BlockSpec((None, tm, FOX_MAIN), lambda b, s: (b, s, 0)),
                   pl.BlockSpec((None, tm, LANES), lambda b, s: (b, s, 0)),
                   pl.BlockSpec((None, tm, LANES), lambda b, s: (b, s, 0))],
        out_shape=[jax.ShapeDtypeStruct((bsz, seq, RWKV_COLS), BF16),
                   jax.ShapeDtypeStruct((bsz, seq, FOX_MAIN), BF16),
                   jax.ShapeDtypeStruct((bsz, seq, LANES), BF16),
                   jax.ShapeDtypeStruct((bsz, seq, LANES), BF16)],
        scratch_shapes=[pltpu.VMEM((1, LANES), F32)],
        compiler_params=pltpu.CompilerParams(
            dimension_semantics=("parallel", "arbitrary"), vmem_limit_bytes=VMEM_LIMIT),
        name="inproj",
    )(x, shift, scale, g, w_r, w_x, w_f_hi, w_f_lo, b_f, qk_gain)


_NN = (((1,), (0,)), ((), ()))
_NT = (((1,), (1,)), ((), ()))
_TN = (((0,), (0,)), ((), ()))
SCAN_N = HEADS_PER_SCAN * CHUNK
BATCH_PER_STEP = 8
INV_LEVELS = 5
M_HEAD, M_STRICT, M_INCL, M_EYE, M_BASE, M_OFF = 0, 1, 2, 3, 4, 5


def _bdot(a, b, dims):
    return lax.dot_general(a, b, dims, preferred_element_type=F32)


def _scan_masks():
    rr, cc = _iota((SCAN_N, SCAN_W), 0), _iota((SCAN_N, SCAN_W), 1)
    ri, ci = _iota((SCAN_N, SCAN_N), 0), _iota((SCAN_N, SCAN_N), 1)
    same = ri // CHUNK == ci // CHUNK
    masks = [rr // CHUNK == cc // HEAD_DIM, same & (ri > ci), same & (ri >= ci), ri == ci,
             (ri // 2 == ci // 2) & (ri > ci)]
    blk = 2
    while blk < CHUNK:
        masks.append((ri // (2 * blk) == ci // (2 * blk)) & (ri // blk != ci // blk) & (ri > ci))
        blk *= 2
    return jnp.stack(masks).astype(BF16)


def _rwkv_kernel(p_ref, masks_ref, mu_ref, w0_ref, w2_ref, a0_ref, a2_ref, g2_ref, kk_ref, ka_ref,
                 rk_ref, gnw_ref, gnb_ref, o_ref, last_ref, state_ref):
    @pl.when(pl.program_id(1) == 0)
    def _():
        last_ref[...] = jnp.zeros_like(last_ref)
        state_ref[...] = jnp.zeros_like(state_ref)

    mu, w0, w2, a0, a2, g2, k_k, k_a, r_k, gn_w, gn_b = (
        ref[...] for ref in (mu_ref, w0_ref, w2_ref, a0_ref, a2_ref, g2_ref, kk_ref, ka_ref,
                             rk_ref, gnw_ref, gnb_ref))
    rows = BATCH_PER_STEP * CHUNK
    p = p_ref[...].astype(F32).reshape(rows, RWKV_COLS)
    row_id = _iota((rows, 1), 0)
    prev = pltpu.roll(p, 1, axis=0)
    for bb in range(BATCH_PER_STEP):
        prev = jnp.where(row_id == bb * CHUNK, last_ref[bb], prev)
        last_ref[bb] = p[(bb + 1) * CHUNK - 1:(bb + 1) * CHUNK, :]
    pf = p + mu * (prev - p)
    r = pf[:, 0:D_GRP]
    k = pf[:, D_GRP:2 * D_GRP]
    v = pf[:, 2 * D_GRP:3 * D_GRP]
    lora = pf[:, LORA_OFF:GATE_OFF]
    gd = pf[:, GATE_OFF:RWKV_COLS]

    wlog = w0 + _dot(jnp.tanh(lora), w2)
    neg = -wlog
    softplus = jnp.maximum(neg, 0.0) + jnp.log(1.0 + jnp.exp(-jnp.abs(neg)))
    ld = -jnp.exp(-softplus - 0.5)
    a = _sigmoid(a0 + _dot(lora, a2))
    g = _dot(_sigmoid(gd), g2)

    red, exp_m = _seg_reduce_mat(D_GRP), _seg_expand_mat(D_GRP)
    kk = k * k_k
    n2 = _split_dot(kk * kk, red, SEG_TERMS)
    kk = kk * _split_dot(1.0 / jnp.maximum(jnp.sqrt(n2), 1e-12), exp_m, SEG_TERMS)
    k2 = k * (1.0 + (a - 1.0) * k_a)

    tr, tc = _iota((rows, rows), 0), _iota((rows, rows), 1)
    tri = ((tr >= tc) & (tr // CHUNK == tc // CHUNK)).astype(BF16)
    cl = _split_dot(ld, tri, terms=CUM_TERMS, left=True)
    cl_end = jnp.concatenate(
        [jnp.broadcast_to(cl[(bb + 1) * CHUNK - 1:(bb + 1) * CHUNK, :], (CHUNK, D_GRP))
         for bb in range(BATCH_PER_STEP)], axis=0)
    e_in = jnp.exp(cl)
    e_out = jnp.exp(-cl)
    e_rem = jnp.exp(cl_end - cl)
    p_end = jnp.exp(cl_end)
    kka = kk * a
    ops = [(-kk * jnp.exp(cl - ld)).astype(BF16), (kka * e_out).astype(BF16),
           (k2 * e_out).astype(BF16), (r * e_in).astype(BF16), v.astype(BF16),
           (kka * e_rem).astype(BF16), (k2 * e_rem).astype(BF16)]

    chains = [(bb, grp) for bb in range(BATCH_PER_STEP)
              for grp in range(N_HEADS // HEADS_PER_SCAN)]
    head_mask = masks_ref[M_HEAD]
    strict, incl = masks_ref[M_STRICT], masks_ref[M_INCL]

    def stacked(op, bb, grp):
        part = op[bb * CHUNK:(bb + 1) * CHUNK, grp * SCAN_W:(grp + 1) * SCAN_W]
        return jnp.concatenate([part] * HEADS_PER_SCAN, axis=0) * head_mask

    xs = [[stacked(op, bb, grp) for op in ops] for bb, grp in chains]
    st = [state_ref[bb, grp] for bb, grp in chains]
    sb = [s.astype(BF16) for s in st]
    nab = [_bdot(x[0], x[1], _NT).astype(BF16) for x in xs]
    aak = [_bdot(x[0], x[2], _NT).astype(BF16) * strict for x in xs]
    arb = [_bdot(x[3], x[1], _NT).astype(BF16) * incl for x in xs]
    ark = [_bdot(x[3], x[2], _NT).astype(BF16) * incl for x in xs]
    t_inv = [masks_ref[M_EYE] + n * masks_ref[M_BASE] for n in nab]
    for lvl in range(INV_LEVELS):
        half = [_bdot(t, n * masks_ref[M_OFF + lvl], _NN).astype(BF16) for t, n in zip(t_inv, nab)]
        t_inv = [t + _bdot(h, t, _NN).astype(BF16) for t, h in zip(t_inv, half)]
    rhs = [(_bdot(x[0], s, _NT) + _bdot(k, x[4], _NN)).astype(BF16)
           for x, s, k in zip(xs, sb, aak)]
    sa = [_bdot(t, h, _NN).astype(BF16) for t, h in zip(t_inv, rhs)]
    ys = [_bdot(x[3], s, _NT) + _bdot(b, u, _NN) + _bdot(k, x[4], _NN)
          for x, s, b, u, k in zip(xs, sb, arb, sa, ark)]
    for (bb, grp), x, s, u in zip(chains, xs, st, sa):
        decay = p_end[bb * CHUNK:bb * CHUNK + 1, grp * SCAN_W:(grp + 1) * SCAN_W]
        state_ref[bb, grp] = s * decay + _bdot(u, x[5], _TN) + _bdot(x[4], x[6], _TN)
    ys = [y[0:CHUNK] + y[CHUNK:2 * CHUNK] + y[2 * CHUNK:3 * CHUNK] + y[3 * CHUNK:4 * CHUNK]
          for y in ys]
    n_grp = N_HEADS // HEADS_PER_SCAN
    y = jnp.concatenate([jnp.concatenate(ys[bb * n_grp:(bb + 1) * n_grp], axis=1)
                         for bb in range(BATCH_PER_STEP)], axis=0)

    mean = _split_dot(_split_dot(y, red, SEG_TERMS) * (1.0 / HEAD_DIM), exp_m, SEG_TERMS)
    d = y - mean
    var = _split_dot(d * d, red, SEG_TERMS) * (1.0 / HEAD_DIM)
    yn = d * _split_dot(lax.rsqrt(var + GN_EPS), exp_m, SEG_TERMS) * gn_w + gn_b
    bonus = _split_dot(_split_dot(r * k2 * r_k, red, SEG_TERMS), exp_m, SEG_TERMS) * v
    o_ref[...] = ((yn + bonus) * g).astype(BF16).reshape(BATCH_PER_STEP, CHUNK, D_GRP)


def _rwkv(p_r, mu, w0, w2p, a0, a2p, g2, k_k, k_a, r_k, gn_w, gn_b):
    bsz, seq, _ = p_r.shape
    assert bsz % BATCH_PER_STEP == 0
    masks = _scan_masks()
    const = lambda b, s: (0, 0)
    vec = pl.BlockSpec((1, D_GRP), const)
    return pl.pallas_call(
        _rwkv_kernel,
        grid=(bsz // BATCH_PER_STEP, seq // CHUNK),
        in_specs=[pl.BlockSpec((BATCH_PER_STEP, CHUNK, RWKV_COLS), lambda b, s: (b, s, 0)),
                  pl.BlockSpec(masks.shape, lambda b, s: (0, 0, 0)),
                  pl.BlockSpec((1, RWKV_COLS), const),
                  vec, pl.BlockSpec((LANES, D_GRP), const),
                  vec, pl.BlockSpec((LANES, D_GRP), const),
                  pl.BlockSpec((LANES, D_GRP), const),
                  vec, vec, vec, vec, vec],
        out_specs=pl.BlockSpec((BATCH_PER_STEP, CHUNK, D_GRP), lambda b, s: (b, s, 0)),
        out_shape=jax.ShapeDtypeStruct((bsz, seq, D_GRP), BF16),
        scratch_shapes=[pltpu.VMEM((BATCH_PER_STEP, 1, RWKV_COLS), F32),
                        pltpu.VMEM((BATCH_PER_STEP, N_HEADS // HEADS_PER_SCAN, SCAN_W, SCAN_W), F32)],
        compiler_params=pltpu.CompilerParams(
            dimension_semantics=("parallel", "arbitrary"), vmem_limit_bytes=VMEM_LIMIT),
        name="rwkv",
    )(p_r, masks, mu, w0, w2p, a0, a2p, g2, k_k, k_a, r_k, gn_w, gn_b)


def _fox_kernel(q_ref, qb_ref, k_ref, kb_ref, vt_ref, og_ref, ong_ref, o_ref, *, tq):
    hp = pl.program_id(1)
    qi = pl.program_id(2)
    lane = _iota((1, LANES), 1)
    q = q_ref[...]
    qb = qb_ref[...]
    zero = jnp.zeros_like(q)
    qcat = [jnp.concatenate([jnp.where(lane // HEAD_DIM == hh, q, zero),
                             jnp.where(lane // 8 == hp * 2 + hh, qb, zero)], axis=1)
            for hh in range(2)]
    FOX_KEYS = min(FOX_SUB_KEYS, tq)
    n_sub = tq // FOX_KEYS
    key_pos = _iota((FOX_KEYS, tq), 0)
    qry_pos = _iota((FOX_KEYS, tq), 1)

    def tile(j, stats, masked):
        base = j * tq
        sts = []
        for s in range(n_sub):
            start = pl.multiple_of(base + s * FOX_KEYS, FOX_KEYS)
            kcat = jnp.concatenate([k_ref[pl.ds(start, FOX_KEYS), :],
                                    kb_ref[pl.ds(start, FOX_KEYS), :]], axis=1)
            sts.append([lax.dot_general(kcat, qc, _NT, preferred_element_type=F32)
                        for qc in qcat])
        for s in range(n_sub):
            start = pl.multiple_of(base + s * FOX_KEYS, FOX_KEYS)
            vt = vt_ref[:, pl.ds(start, FOX_KEYS)]
            st = sts[s]
            if masked:
                st = [jnp.where(qry_pos >= key_pos + s * FOX_KEYS, x, -jnp.inf) for x in st]
            m_new = [jnp.maximum(c[0], jnp.max(x, axis=0, keepdims=True))
                     for c, x in zip(stats, st)]
            pts = [jnp.exp(x - m) for x, m in zip(st, m_new)]
            pvs = [jnp.dot(vt, pt.astype(BF16), preferred_element_type=F32) for pt in pts]
            new = []
            for hh in range(2):
                m, l, acc = stats[hh]
                alpha = jnp.exp(m - m_new[hh])
                l = alpha * l + jnp.sum(pts[hh], axis=0, keepdims=True)
                acc = alpha * acc + pvs[hh][hh * HEAD_DIM:(hh + 1) * HEAD_DIM, :]
                new.append((m_new[hh], l, acc))
            stats = tuple(new)
        return stats

    init = (jnp.full((1, tq), -jnp.inf, F32), jnp.zeros((1, tq), F32),
            jnp.zeros((HEAD_DIM, tq), F32))
    stats = lax.fori_loop(0, qi, functools.partial(tile, masked=False), (init, init))
    outs = []
    for _, l, acc in tile(qi, stats, True):
        o = acc / l
        outs.append(o * lax.rsqrt(jnp.mean(o * o, axis=0, keepdims=True) + NORM_EPS))
    o = jnp.concatenate(outs, axis=0).T
    o_ref[...] = (o * ong_ref[...] * _sigmoid(og_ref[...].astype(F32))).astype(BF16)


def _fox(p_x, k_bias, q_bias, o_gain, tq):
    bsz, seq, _ = p_x.shape
    npair = N_HEADS // 2
    v_t = jnp.transpose(p_x[:, :, 2 * D_GRP:3 * D_GRP], (0, 2, 1))
    return pl.pallas_call(
        functools.partial(_fox_kernel, tq=tq),
        grid=(bsz, npair, seq // tq),
        in_specs=[pl.BlockSpec((None, tq, LANES), lambda b, h, i: (b, i, h)),
                  pl.BlockSpec((None, tq, LANES), lambda b, h, i: (b, i, 0)),
                  pl.BlockSpec((None, seq, LANES), lambda b, h, i: (b, 0, npair + h)),
                  pl.BlockSpec((None, seq, LANES), lambda b, h, i: (b, 0, 0)),
                  pl.BlockSpec((None, LANES, seq), lambda b, h, i: (b, h, 0)),
                  pl.BlockSpec((None, tq, LANES), lambda b, h, i: (b, i, 3 * npair + h)),
                  pl.BlockSpec((1, LANES), lambda b, h, i: (0, 0))],
        out_specs=pl.BlockSpec((None, tq, LANES), lambda b, h, i: (b, i, h)),
        out_shape=jax.ShapeDtypeStruct((bsz, seq, D_GRP), BF16),
        compiler_params=pltpu.CompilerParams(
            dimension_semantics=("parallel", "parallel", "arbitrary"),
            vmem_limit_bytes=VMEM_LIMIT),
        name="fox",
    )(p_x, q_bias, p_x, k_bias, v_t, p_x, o_gain)


def _outproj_kernel(x_ref, yr_ref, yf_ref, g1_ref, sh_ref, sc_ref, ng_ref, wor_ref, wof_ref,
                    wrt_ref, wrl_ref, brt_ref, x1_ref, h2_ref, idx_ref, gate_ref, rank_ref, cnt_ref,
                    carry_ref):
    @pl.when(pl.program_id(0) == 0)
    def _():
        carry_ref[...] = jnp.zeros_like(carry_ref)

    y = (jnp.dot(yr_ref[...], wor_ref[...], preferred_element_type=F32)
         + jnp.dot(yf_ref[...], wof_ref[...], preferred_element_type=F32))
    x1 = x_ref[...] + g1_ref[...] * y
    x1_ref[...] = x1
    tm = x1.shape[0]
    h = x1 * lax.rsqrt(jnp.mean(x1 * x1, axis=-1, keepdims=True) + NORM_EPS) * ng_ref[...]
    h2 = h * (1.0 + sc_ref[...]) + sh_ref[...]
    h2_ref[...] = _pack_rows(h2)

    lane = _iota((tm, LANES), 1)
    h_hi = h2.astype(BF16)
    h_lo = (h2 - h_hi.astype(F32)).astype(BF16)
    logits = (jnp.dot(h_hi, wrt_ref[...], preferred_element_type=F32)
              + jnp.dot(h_lo, wrt_ref[...], preferred_element_type=F32)
              + jnp.dot(h_hi, wrl_ref[...], preferred_element_type=F32)) + brt_ref[...]
    lg = jnp.where(lane < N_EXPERTS, logits, -jnp.inf)
    picks = []
    hot_sum = jnp.zeros((tm, LANES), F32)
    for _ in range(TOP_K):
        m = jnp.max(lg, axis=-1, keepdims=True)
        sel = jnp.min(jnp.where(lg == m, lane, LANES), axis=-1, keepdims=True)
        hot = lane == sel
        picks.append((m, sel, hot))
        hot_sum = hot_sum + hot.astype(F32)
        lg = jnp.where(hot, -jnp.inf, lg)
    es = [jnp.exp(m - picks[0][0]) for m, _, _ in picks]
    den = es[0] + es[1] + es[2] + es[3]

    before = jnp.dot(_tri(tm, True), hot_sum.astype(BF16), preferred_element_type=F32)
    before = before + carry_ref[...]
    idx_out = jnp.zeros((tm, LANES), jnp.int32)
    gate_out = jnp.zeros((tm, LANES), F32)
    rank_out = jnp.zeros((tm, LANES), jnp.int32)
    for kk, (m, sel, hot) in enumerate(picks):
        rk = jnp.sum(jnp.where(hot, before, 0.0), axis=-1, keepdims=True).astype(jnp.int32)
        idx_out = jnp.where(lane == kk, sel, idx_out)
        gate_out = jnp.where(lane == kk, es[kk] / den, gate_out)
        rank_out = jnp.where(lane == kk, rk, rank_out)
    idx_ref[...] = idx_out
    gate_ref[...] = gate_out
    rank_ref[...] = rank_out
    carry_ref[...] = carry_ref[...] + jnp.sum(hot_sum, axis=0, keepdims=True)
    cnt_ref[...] = carry_ref[...]


def _outproj(x2d, y_r, y_f, gate1, shift2, scale2, norm_g, wo_r, wo_f, w_rt, b_rt, tm, seq,
             row0, t):
    w_rt_hi = w_rt.astype(BF16)
    w_rt_lo = (w_rt - w_rt_hi.astype(F32)).astype(BF16)
    per_b = seq // tm
    blk0 = row0 // tm
    const = lambda i: (0, 0)
    rows = lambda i: (i, 0)
    rows_in = lambda i: (i + blk0, 0)
    mod = pl.BlockSpec((None, 1, D_MODEL), lambda i: ((i + blk0) // per_b, 0, 0))
    return pl.pallas_call(
        _outproj_kernel,
        grid=(t // tm,),
        in_specs=[pl.BlockSpec((tm, D_MODEL), rows_in),
                  pl.BlockSpec((tm, D_GRP), rows_in),
                  pl.BlockSpec((tm, D_GRP), rows_in),
                  mod, mod, mod,
                  pl.BlockSpec((1, D_MODEL), const),
                  pl.BlockSpec((D_GRP, D_MODEL), const),
                  pl.BlockSpec((D_GRP, D_MODEL), const),
                  pl.BlockSpec((D_MODEL, LANES), const),
                  pl.BlockSpec((D_MODEL, LANES), const),
                  pl.BlockSpec((1, LANES), const)],
        out_specs=[pl.BlockSpec((tm, D_MODEL), rows),
                   pl.BlockSpec((tm, D_PACK), rows),
                   pl.BlockSpec((tm, LANES), rows),
                   pl.BlockSpec((tm, LANES), rows),
                   pl.BlockSpec((tm, LANES), rows),
                   pl.BlockSpec((1, LANES), const)],
        out_shape=[jax.ShapeDtypeStruct((t, D_MODEL), F32),
                   jax.ShapeDtypeStruct((t, D_PACK), jnp.uint32),
                   jax.ShapeDtypeStruct((t, LANES), jnp.int32),
                   jax.ShapeDtypeStruct((t, LANES), F32),
                   jax.ShapeDtypeStruct((t, LANES), jnp.int32),
                   jax.ShapeDtypeStruct((1, LANES), F32)],
        scratch_shapes=[pltpu.VMEM((1, LANES), F32)],
        compiler_params=pltpu.CompilerParams(
            dimension_semantics=("arbitrary",), vmem_limit_bytes=VMEM_LIMIT),
        name="outproj",
    )(x2d, y_r, y_f, gate1, shift2, scale2, norm_g, wo_r, wo_f, w_rt_hi, w_rt_lo, b_rt)


SC_CORES = 2
SC_SUBCORES = 16
SC_ROWS = 32


def _sc_gather_rows(idx, src):
    n_workers = SC_CORES * SC_SUBCORES
    m = idx.shape[0]
    d = src.shape[1]
    assert m % (n_workers * SC_ROWS) == 0
    n_chunks = m // (n_workers * SC_ROWS)
    mesh = plsc.VectorSubcoreMesh(core_axis_name="c", subcore_axis_name="s")

    @functools.partial(
        pl.kernel, mesh=mesh,
        out_type=jax.ShapeDtypeStruct((m, d), src.dtype),
        scratch_types=[pltpu.VMEM((n_chunks, SC_ROWS), jnp.int32),
                       pltpu.VMEM((SC_ROWS, d), src.dtype),
                       pltpu.SemaphoreType.DMA],
        name="sc_gather")
    def gather(src_hbm, idx_hbm, out_hbm, idx_v, rows_v, sem):
        wid = lax.axis_index("s") * SC_CORES + lax.axis_index("c")
        pltpu.sync_copy(idx_hbm.at[wid], idx_v)

        @pl.loop(0, n_chunks)
        def _(j):
            pltpu.async_copy(src_hbm.at[idx_v.at[j]], rows_v, sem).wait()
            pltpu.sync_copy(rows_v, out_hbm.at[pl.ds((wid * n_chunks + j) * SC_ROWS, SC_ROWS)])

    return gather(src, idx.reshape(n_workers, n_chunks, SC_ROWS))


def _sc_scatter_rows(src, dest, n_out):
    n_workers = SC_CORES * SC_SUBCORES
    t, d = src.shape
    n_slot = dest.shape[1]
    assert t % (n_workers * SC_ROWS) == 0
    n_chunks = t // (n_workers * SC_ROWS)
    mesh = plsc.VectorSubcoreMesh(core_axis_name="c", subcore_axis_name="s")
    idx = dest.reshape(n_workers, n_chunks, SC_ROWS, n_slot).transpose(0, 1, 3, 2)
    idx = idx.reshape(n_workers, n_chunks * n_slot, SC_ROWS)

    @functools.partial(
        pl.kernel, mesh=mesh,
        out_type=jax.ShapeDtypeStruct((n_out, d), src.dtype),
        scratch_types=[pltpu.VMEM((n_chunks * n_slot, SC_ROWS), jnp.int32),
                       pltpu.VMEM((SC_ROWS, d), src.dtype)],
        name="sc_scatter")
    def scatter(src_hbm, idx_hbm, out_hbm, idx_v, rows_v):
        wid = lax.axis_index("s") * SC_CORES + lax.axis_index("c")
        pltpu.sync_copy(idx_hbm.at[wid], idx_v)

        @pl.loop(0, n_chunks)
        def _(j):
            pltpu.sync_copy(src_hbm.at[pl.ds((wid * n_chunks + j) * SC_ROWS, SC_ROWS)], rows_v)
            for k in range(n_slot):
                pltpu.sync_copy(rows_v, out_hbm.at[idx_v.at[j * n_slot + k]])

    return scatter(src, idx)


def _expert_kernel(be_ref, nv_ref, x_ref, wgu_ref, bgu_ref, wd_ref, bd_ref, o_ref):
    del be_ref
    valid = _iota((EXPERT_BLOCK, 1), 0) < nv_ref[pl.program_id(0)]
    lo, hi = _unpack_rows(jnp.where(valid, x_ref[...], jnp.uint32(0)))
    x = jnp.concatenate([lo.astype(BF16), hi.astype(BF16)], axis=1)
    gu = jnp.dot(x, wgu_ref[...], preferred_element_type=F32) + bgu_ref[...]
    gate = jnp.minimum(gu[:, :D_MODEL], SWIGLU_LIMIT)
    up = jnp.clip(gu[:, D_MODEL:], -SWIGLU_LIMIT, SWIGLU_LIMIT)
    act = gate * _sigmoid(SWIGLU_ALPHA * gate) * (up + 1.0)
    o_ref[...] = _pack_rows(
        jnp.dot(act.astype(BF16), wd_ref[...], preferred_element_type=F32) + bd_ref[...])


def _experts(block_e, n_valid, xs, w_gu, b_gu, w_d, b_d):
    n_blocks = block_e.shape[0]
    grid_spec = pltpu.PrefetchScalarGridSpec(
        num_scalar_prefetch=2,
        grid=(n_blocks,),
        in_specs=[pl.BlockSpec((EXPERT_BLOCK, D_PACK), lambda j, be, nv: (j, 0)),
                  pl.BlockSpec((None, D_MODEL, 2 * D_MODEL), lambda j, be, nv: (be[j], 0, 0)),
                  pl.BlockSpec((None, 1, 2 * D_MODEL), lambda j, be, nv: (be[j], 0, 0)),
                  pl.BlockSpec((None, D_MODEL, D_MODEL), lambda j, be, nv: (be[j], 0, 0)),
                  pl.BlockSpec((None, 1, D_MODEL), lambda j, be, nv: (be[j], 0, 0))],
        out_specs=pl.BlockSpec((EXPERT_BLOCK, D_PACK), lambda j, be, nv: (j, 0)),
    )
    return pl.pallas_call(
        _expert_kernel,
        grid_spec=grid_spec,
        out_shape=jax.ShapeDtypeStruct(xs.shape, jnp.uint32),
        compiler_params=pltpu.CompilerParams(
            dimension_semantics=("arbitrary",), vmem_limit_bytes=VMEM_LIMIT),
        name="experts",
    )(block_e, n_valid, xs, w_gu, b_gu, w_d, b_d)


COMBINE_TOKENS = 512
MOE_SPLITS = 2


def _combine_kernel(yg_ref, x1_ref, gate_ref, g2_ref, fg_ref, o_ref):
    gates = gate_ref[...]
    acc_lo = acc_hi = None
    for kk in range(TOP_K):
        lo, hi = _unpack_rows(yg_ref[kk * COMBINE_TOKENS:(kk + 1) * COMBINE_TOKENS, :])
        g = gates[:, kk:kk + 1]
        acc_lo = g * lo if acc_lo is None else acc_lo + g * lo
        acc_hi = g * hi if acc_hi is None else acc_hi + g * hi
    x2 = x1_ref[...] + g2_ref[...] * jnp.concatenate([acc_lo, acc_hi], axis=1)
    o_ref[...] = x2 * lax.rsqrt(jnp.mean(x2 * x2, axis=-1, keepdims=True) + NORM_EPS) * fg_ref[...]


def _combine_kernel_into(prev_ref, *refs):
    del prev_ref
    _combine_kernel(*refs)


def _combine(yg, x1, gates, gate2, final_g, seq, row0, t_total, prev):
    t = x1.shape[0]
    tm = COMBINE_TOKENS
    per_b = seq // tm
    blk0 = row0 // tm
    rows = lambda i: (i, 0)
    in_specs = [pl.BlockSpec((TOP_K * tm, D_PACK), rows),
                pl.BlockSpec((tm, D_MODEL), rows),
                pl.BlockSpec((tm, LANES), rows),
                pl.BlockSpec((None, 1, D_MODEL), lambda i: ((i + blk0) // per_b, 0, 0)),
                pl.BlockSpec((1, D_MODEL), lambda i: (0, 0))]
    args = (yg, x1, gates, gate2, final_g)
    if prev is not None:
        in_specs = [pl.BlockSpec(memory_space=pl.ANY)] + in_specs
        args = (prev,) + args
    return pl.pallas_call(
        _combine_kernel if prev is None else _combine_kernel_into,
        grid=(t // tm,),
        in_specs=in_specs,
        out_specs=pl.BlockSpec((tm, D_MODEL), lambda i: (i + blk0, 0)),
        out_shape=jax.ShapeDtypeStruct((t_total, D_MODEL), F32),
        input_output_aliases={} if prev is None else {0: 0},
        compiler_params=pltpu.CompilerParams(
            dimension_semantics=("parallel",), vmem_limit_bytes=VMEM_LIMIT),
        name="combine",
    )(*args)


def _moe(h2, idx, gates, rank, counts, x1, gate2, final_g, w_gu, b_gu, w_d, b_d, seq,
         row0, t_total, prev):
    t = h2.shape[0]
    n_slots = t * TOP_K
    n_blocks = -(-n_slots // EXPERT_BLOCK) + N_EXPERTS
    cap = n_blocks * EXPERT_BLOCK
    padded = (counts + EXPERT_BLOCK - 1) // EXPERT_BLOCK * EXPERT_BLOCK
    pad_ends = jnp.cumsum(padded)
    pad_starts = pad_ends - padded
    dest = pad_starts[idx] + rank
    block_starts = jnp.arange(n_blocks, dtype=jnp.int32) * EXPERT_BLOCK
    block_e = jnp.minimum(jnp.sum(block_starts[:, None] >= pad_ends[None, :], axis=1),
                          N_EXPERTS - 1).astype(jnp.int32)
    n_valid = jnp.clip(counts[block_e] - (block_starts - pad_starts[block_e]), 0, EXPERT_BLOCK)

    xs = _sc_scatter_rows(h2, dest, cap)
    yb = _experts(block_e, n_valid.astype(jnp.int32), xs, w_gu, b_gu, w_d, b_d)
    dest_blocks = dest.reshape(-1, COMBINE_TOKENS, TOP_K).transpose(0, 2, 1).reshape(-1)
    yg = _sc_gather_rows(dest_blocks, yb)
    return _combine(yg, x1, gates, gate2, final_g, seq, row0, t_total, prev)


def _layer(x, c_mod, norm1_g, w_in, mu_shift, w0, w2, a0, a2, g2, k_k, k_a, r_k, gn_w, gn_b, b_f,
           q_norm_g, k_norm_g, o_norm_g, w_out, norm2_g, w_router, b_router, w_gate_up,
           b_gate_up, w_down, b_down, final_g, tm_in, tq, tm_out):
    bsz, seq, _ = x.shape
    shift1, scale1, gate1, shift2, scale2, gate2 = (
        m.reshape(bsz, 1, D_MODEL) for m in jnp.split(c_mod, 6, axis=-1))
    row = lambda v: v.reshape(1, -1)

    w_r = w_in[:, :RWKV_COLS].astype(BF16)
    w_x = w_in[:, RWKV_COLS:RWKV_COLS + FOX_MAIN].astype(BF16)
    w_f = jnp.pad(w_in[:, RWKV_COLS + FOX_MAIN:], ((0, 0), (0, LANES - N_HEADS)))
    b_fp = jnp.pad(b_f, (0, LANES - N_HEADS)).reshape(1, LANES)
    qk_gain = jnp.concatenate([jnp.tile(q_norm_g, N_HEADS) * HEAD_DIM ** -0.5,
                               jnp.tile(k_norm_g, N_HEADS)]).reshape(1, -1)
    p_r, p_x, k_bias, q_bias = _inproj(x, shift1, scale1, row(norm1_g), w_r, w_x, w_f, b_fp,
                                       qk_gain, tm_in)

    zeros = jnp.zeros((LANES - 64, D_GRP), F32)
    w2p = jnp.concatenate([w2, zeros], axis=0).astype(BF16)
    a2p = jnp.concatenate([zeros, a2], axis=0).astype(BF16)
    y_r = _rwkv(p_r, row(mu_shift), row(w0), w2p, row(a0), a2p, g2.astype(BF16), row(k_k),
                row(k_a), row(r_k), row(gn_w), row(gn_b))

    y_f = _fox(p_x, k_bias, q_bias, jnp.tile(o_norm_g, 2).reshape(1, LANES), tq)

    t = bsz * seq
    w_rt = jnp.pad(w_router, ((0, 0), (0, LANES - N_EXPERTS)))
    b_rt = jnp.pad(b_router, (0, LANES - N_EXPERTS)).reshape(1, LANES)
    wo = w_out.astype(BF16)
    w_gu, w_d = w_gate_up.astype(BF16), w_down.astype(BF16)
    b_gu, b_d = b_gate_up.reshape(N_EXPERTS, 1, -1), b_down.reshape(N_EXPERTS, 1, -1)
    t_part = t // MOE_SPLITS
    out = None
    for part in range(MOE_SPLITS):
        row0 = part * t_part
        x1, h2, idx, gates, rank, cnt = _outproj(
            x.reshape(t, D_MODEL), y_r.reshape(t, D_GRP), y_f.reshape(t, D_GRP), gate1, shift2,
            scale2, row(norm2_g), wo[:D_GRP], wo[D_GRP:], w_rt, b_rt, tm_out, seq, row0, t_part)
        counts = cnt[0, :N_EXPERTS].astype(jnp.int32)
        out = _moe(h2, idx[:, :TOP_K], gates, rank[:, :TOP_K], counts, x1, gate2, row(final_g),
                   w_gu, b_gu, w_d, b_d, seq, row0, t, out)
    return out.reshape(bsz, seq, D_MODEL)


def kernel(x, c, w_ada, b_ada, norm1_g, w_in, mu_shift, w0, w2, a0, a2, g2, k_k, k_a, r_k, gn_w,
           gn_b, b_f, q_norm_g, k_norm_g, o_norm_g, w_out, norm2_g, w_router, b_router, w_gate_up,
           b_gate_up, w_down, b_down, final_g):
    assert w_ada.shape[0] == 1, "single-layer block"
    c_mod = _adaln(c, w_ada[0], b_ada[0])
    return _layer(x, c_mod, norm1_g[0], w_in[0], mu_shift[0], w0[0], w2[0], a0[0], a2[0], g2[0],
                  k_k[0], k_a[0], r_k[0], gn_w[0], gn_b[0], b_f[0], q_norm_g[0], k_norm_g[0],
                  o_norm_g[0], w_out[0], norm2_g[0], w_router[0], b_router[0], w_gate_up[0],
                  b_gate_up[0], w_down[0], b_down[0], final_g,
                  tm_in=min(512, x.shape[1]), tq=min(512, x.shape[1]), tm_out=min(1024, x.shape[1]))
```

```python
import functools

import jax
import jax.numpy as jnp
from jax import lax
from jax.experimental import pallas as pl
from jax.experimental.pallas import tpu as pltpu
from jax.experimental.pallas import tpu_sc as plsc

F32 = jnp.float32
BF16 = jnp.bfloat16
HIGHEST = lax.Precision.HIGHEST

D_MODEL = 1024
HEAD_DIM = 64
N_HEADS = 8
D_GRP = N_HEADS * HEAD_DIM
RWKV_COLS = 1792
LORA_OFF = 3 * D_GRP
GATE_OFF = LORA_OFF + 128
FOX_MAIN = 4 * D_GRP
N_EXPERTS = 32
TOP_K = 4
EXPERT_BLOCK = 512
CAST_COLS = 256
SWIGLU_ALPHA = 1.702
SWIGLU_LIMIT = 7.0
NORM_EPS = 1e-6
GN_EPS = 64e-5
LANES = 128
CHUNK = 64
FOX_SUB_KEYS = 256
HEADS_PER_SCAN = 4
SCAN_W = HEADS_PER_SCAN * HEAD_DIM
SEG_TERMS = 1
CUM_TERMS = 2
VMEM_LIMIT = 56 * 1024 * 1024


def _dot(a, b):
    return jnp.dot(a.astype(BF16), b.astype(BF16), preferred_element_type=F32)


def _dot_nt(a, b):
    return lax.dot_general(a.astype(BF16), b.astype(BF16), (((1,), (1,)), ((), ())),
                           preferred_element_type=F32)


def _dot_tn(a, b):
    return lax.dot_general(a.astype(BF16), b.astype(BF16), (((0,), (0,)), ((), ())),
                           preferred_element_type=F32)


def _fdot(a, b):
    return jnp.dot(a, b, precision=HIGHEST, preferred_element_type=F32)


def _split_dot(x, m, terms=2, left=False):
    acc = None
    rem = x
    for _ in range(terms):
        part = rem.astype(BF16)
        rem = rem - part.astype(F32)
        d = (jnp.dot(m, part, preferred_element_type=F32) if left
             else jnp.dot(part, m, preferred_element_type=F32))
        acc = d if acc is None else acc + d
    return acc


def _iota(shape, dim):
    return lax.broadcasted_iota(jnp.int32, shape, dim)


def _seg_reduce_mat(n):
    return (_iota((n, LANES), 0) // HEAD_DIM == _iota((n, LANES), 1)).astype(BF16)


def _seg_expand_mat(n):
    return (_iota((LANES, n), 1) // HEAD_DIM == _iota((LANES, n), 0)).astype(BF16)


def _tri(n, strict):
    r, c = _iota((n, n), 0), _iota((n, n), 1)
    return ((r > c) if strict else (r >= c)).astype(BF16)


D_PACK = D_MODEL // 2


def _pack_rows(x):
    lo = lax.bitcast_convert_type(x[:, :D_PACK].astype(BF16).astype(F32), jnp.uint32)
    hi = lax.bitcast_convert_type(x[:, D_PACK:].astype(BF16).astype(F32), jnp.uint32)
    return hi | (lo >> 16)


def _unpack_rows(p):
    lo = lax.bitcast_convert_type(p << 16, F32)
    hi = lax.bitcast_convert_type(p & jnp.uint32(0xFFFF0000), F32)
    return lo, hi


def _log_sigmoid(z):
    return jnp.minimum(z, 0.0) - jnp.log(1.0 + jnp.exp(-jnp.abs(z)))


def _sigmoid(z):
    return 1.0 / (1.0 + jnp.exp(-z))


def _adaln_kernel(c_ref, w_ref, b_ref, o_ref):
    c = c_ref[...]
    o_ref[...] = _fdot(c * _sigmoid(c), w_ref[...]) + b_ref[...]


def _adaln(c, w_ada, b_ada):
    bsz = c.shape[0]
    n_mod = w_ada.shape[1] // D_MODEL
    return pl.pallas_call(
        _adaln_kernel,
        grid=(n_mod,),
        in_specs=[pl.BlockSpec((bsz, D_MODEL), lambda j: (0, 0)),
                  pl.BlockSpec((D_MODEL, D_MODEL), lambda j: (0, j)),
                  pl.BlockSpec((1, D_MODEL), lambda j: (0, j))],
        out_specs=pl.BlockSpec((bsz, D_MODEL), lambda j: (0, j)),
        out_shape=jax.ShapeDtypeStruct((bsz, n_mod * D_MODEL), F32),
        name="adaln",
    )(c, w_ada, b_ada.reshape(1, -1))


def _inproj_kernel(x_ref, sh_ref, sc_ref, g_ref, wr_ref, wx_ref, wfh_ref, wfl_ref, bf_ref, qkg_ref,
                   pr_ref, px_ref, kb_ref, qb_ref, carry_ref):
    @pl.when(pl.program_id(1) == 0)
    def _():
        carry_ref[...] = jnp.zeros_like(carry_ref)

    x = x_ref[...]
    tm = x.shape[0]
    h = x * lax.rsqrt(jnp.mean(x * x, axis=-1, keepdims=True) + NORM_EPS) * g_ref[...]
    h = h * (1.0 + sc_ref[...]) + sh_ref[...]
    hb = h.astype(BF16)
    h_lo = (h - hb.astype(F32)).astype(BF16)

    pr_ref[...] = jnp.dot(hb, wr_ref[...], preferred_element_type=F32).astype(BF16)

    px = jnp.dot(hb, wx_ref[...], preferred_element_type=F32)
    qk = px[:, :2 * D_GRP]
    ss = _split_dot(qk * qk, _seg_reduce_mat(2 * D_GRP), SEG_TERMS)
    inv = lax.rsqrt(ss * (1.0 / HEAD_DIM) + NORM_EPS)
    qk = qk * _split_dot(inv, _seg_expand_mat(2 * D_GRP), SEG_TERMS) * qkg_ref[...]
    px_ref[:, :2 * D_GRP] = qk.astype(BF16)
    px_ref[:, 2 * D_GRP:] = px[:, 2 * D_GRP:].astype(BF16)

    z = (jnp.dot(hb, wfh_ref[...], preferred_element_type=F32)
         + jnp.dot(h_lo, wfh_ref[...], preferred_element_type=F32)
         + jnp.dot(hb, wfl_ref[...], preferred_element_type=F32)) + bf_ref[...]
    cum = _split_dot(_log_sigmoid(z), _tri(tm, False), terms=3, left=True) + carry_ref[...]
    carry_ref[...] = cum[tm - 1:tm, :]

    src, dst = _iota((LANES, LANES), 0), _iota((LANES, LANES), 1)
    parts = []
    rem = cum
    for _ in range(3):
        part = rem.astype(BF16)
        rem = rem - part.astype(F32)
        parts.append(part)

    def spread(offset):
        return sum(jnp.dot(part, ((dst == 8 * src + offset + t) & (src < N_HEADS)).astype(BF16),
                           preferred_element_type=F32) for t, part in enumerate(parts))

    slot = _iota((1, LANES), 1) % 8
    kb_ref[...] = (jnp.where((slot >= 3) & (slot < 6), 1.0, 0.0) - spread(0)).astype(BF16)
    qb_ref[...] = (jnp.where(slot < 3, 1.0, 0.0) + spread(3)).astype(BF16)


def _inproj(x, shift, scale, g, w_r, w_x, w_f, b_f, qk_gain, tm):
    w_f_hi = w_f.astype(BF16)
    w_f_lo = (w_f - w_f_hi.astype(F32)).astype(BF16)
    bsz, seq, _ = x.shape
    const = lambda b, s: (0, 0)
    return pl.pallas_call(
        _inproj_kernel,
        grid=(bsz, seq // tm),
        in_specs=[pl.BlockSpec((None, tm, D_MODEL), lambda b, s: (b, s, 0)),
                  pl.BlockSpec((None, 1, D_MODEL), lambda b, s: (b, 0, 0)),
                  pl.BlockSpec((None, 1, D_MODEL), lambda b, s: (b, 0, 0)),
                  pl.BlockSpec((1, D_MODEL), const),
                  pl.BlockSpec((D_MODEL, RWKV_COLS), const),
                  pl.BlockSpec((D_MODEL, FOX_MAIN), const),
                  pl.BlockSpec((D_MODEL, LANES), const),
                  pl.BlockSpec((D_MODEL, LANES), const),
                  pl.BlockSpec((1, LANES), const),
                  pl.BlockSpec((1, 2 * D_GRP), const)],
        out_specs=[pl.BlockSpec((None, tm, RWKV_COLS), lambda b, s: (b, s, 0)),
                   pl.BlockSpec((None, tm, FOX_MAIN), lambda b, s: (b, s, 0)),
                   pl.BlockSpec((None, tm, LANES), lambda b, s: (b, s, 0)),
                   pl.BlockSpec((None, tm, LANES), lambda b, s: (b, s, 0))],
        out_shape=[jax.ShapeDtypeStruct((bsz, seq, RWKV_COLS), BF16),
                   jax.ShapeDtypeStruct((bsz, seq, FOX_MAIN), BF16),
                   jax.ShapeDtypeStruct((bsz, seq, LANES), BF16),
                   jax.ShapeDtypeStruct((bsz, seq, LANES), BF16)],
        scratch_shapes=[pltpu.VMEM((1, LANES), F32)],
        compiler_params=pltpu.CompilerParams(
            dimension_semantics=("parallel", "arbitrary"), vmem_limit_bytes=VMEM_LIMIT),
        name="inproj",
    )(x, shift, scale, g, w_r, w_x, w_f_hi, w_f_lo, b_f, qk_gain)


_NN = (((1,), (0,)), ((), ()))
_NT = (((1,), (1,)), ((), ()))
_TN = (((0,), (0,)), ((), ()))
SCAN_N = HEADS_PER_SCAN * CHUNK
BATCH_PER_STEP = 8
INV_LEVELS = 5
M_HEAD, M_STRICT, M_INCL, M_EYE, M_BASE, M_OFF = 0, 1, 2, 3, 4, 5


def _bdot(a, b, dims):
    return lax.dot_general(a, b, dims, preferred_element_type=F32)


def _scan_masks():
    rr, cc = _iota((SCAN_N, SCAN_W), 0), _iota((SCAN_N, SCAN_W), 1)
    ri, ci = _iota((SCAN_N, SCAN_N), 0), _iota((SCAN_N, SCAN_N), 1)
    same = ri // CHUNK == ci // CHUNK
    masks = [rr // CHUNK == cc // HEAD_DIM, same & (ri > ci), same & (ri >= ci), ri == ci,
             (ri // 2 == ci // 2) & (ri > ci)]
    blk = 2
    while blk < CHUNK:
        masks.append((ri // (2 * blk) == ci // (2 * blk)) & (ri // blk != ci // blk) & (ri > ci))
        blk *= 2
    return jnp.stack(masks).astype(BF16)


def _rwkv_kernel(p_ref, masks_ref, mu_ref, w0_ref, w2_ref, a0_ref, a2_ref, g2_ref, kk_ref, ka_ref,
                 rk_ref, gnw_ref, gnb_ref, o_ref, last_ref, state_ref):
    @pl.when(pl.program_id(1) == 0)
    def _():
        last_ref[...] = jnp.zeros_like(last_ref)
        state_ref[...] = jnp.zeros_like(state_ref)

    mu, w0, w2, a0, a2, g2, k_k, k_a, r_k, gn_w, gn_b = (
        ref[...] for ref in (mu_ref, w0_ref, w2_ref, a0_ref, a2_ref, g2_ref, kk_ref, ka_ref,
                             rk_ref, gnw_ref, gnb_ref))
    rows = BATCH_PER_STEP * CHUNK
    p = p_ref[...].astype(F32).reshape(rows, RWKV_COLS)
    row_id = _iota((rows, 1), 0)
    prev = pltpu.roll(p, 1, axis=0)
    for bb in range(BATCH_PER_STEP):
        prev = jnp.where(row_id == bb * CHUNK, last_ref[bb], prev)
        last_ref[bb] = p[(bb + 1) * CHUNK - 1:(bb + 1) * CHUNK, :]
    pf = p + mu * (prev - p)
    r = pf[:, 0:D_GRP]
    k = pf[:, D_GRP:2 * D_GRP]
    v = pf[:, 2 * D_GRP:3 * D_GRP]
    lora = pf[:, LORA_OFF:GATE_OFF]
    gd = pf[:, GATE_OFF:RWKV_COLS]

    wlog = w0 + _dot(jnp.tanh(lora), w2)
    neg = -wlog
    softplus = jnp.maximum(neg, 0.0) + jnp.log(1.0 + jnp.exp(-jnp.abs(neg)))
    ld = -jnp.exp(-softplus - 0.5)
    a = _sigmoid(a0 + _dot(lora, a2))
    g = _dot(_sigmoid(gd), g2)

    red, exp_m = _seg_reduce_mat(D_GRP), _seg_expand_mat(D_GRP)
    kk = k * k_k
    n2 = _split_dot(kk * kk, red, SEG_TERMS)
    kk = kk * _split_dot(1.0 / jnp.maximum(jnp.sqrt(n2), 1e-12), exp_m, SEG_TERMS)
    k2 = k * (1.0 + (a - 1.0) * k_a)

    tr, tc = _iota((rows, rows), 0), _iota((rows, rows), 1)
    tri = ((tr >= tc) & (tr // CHUNK == tc // CHUNK)).astype(BF16)
    cl = _split_dot(ld, tri, terms=CUM_TERMS, left=True)
    cl_end = jnp.concatenate(
        [jnp.broadcast_to(cl[(bb + 1) * CHUNK - 1:(bb + 1) * CHUNK, :], (CHUNK, D_GRP))
         for bb in range(BATCH_PER_STEP)], axis=0)
    e_in = jnp.exp(cl)
    e_out = jnp.exp(-cl)
    e_rem = jnp.exp(cl_end - cl)
    p_end = jnp.exp(cl_end)
    kka = kk * a
    ops = [(-kk * jnp.exp(cl - ld)).astype(BF16), (kka * e_out).astype(BF16),
           (k2 * e_out).astype(BF16), (r * e_in).astype(BF16), v.astype(BF16),
           (kka * e_rem).astype(BF16), (k2 * e_rem).astype(BF16)]

    chains = [(bb, grp) for bb in range(BATCH_PER_STEP)
              for grp in range(N_HEADS // HEADS_PER_SCAN)]
    head_mask = masks_ref[M_HEAD]
    strict, incl = masks_ref[M_STRICT], masks_ref[M_INCL]

    def stacked(op, bb, grp):
        part = op[bb * CHUNK:(bb + 1) * CHUNK, grp * SCAN_W:(grp + 1) * SCAN_W]
        return jnp.concatenate([part] * HEADS_PER_SCAN, axis=0) * head_mask

    xs = [[stacked(op, bb, grp) for op in ops] for bb, grp in chains]
    st = [state_ref[bb, grp] for bb, grp in chains]
    sb = [s.astype(BF16) for s in st]
    nab = [_bdot(x[0], x[1], _NT).astype(BF16) for x in xs]
    aak = [_bdot(x[0], x[2], _NT).astype(BF16) * strict for x in xs]
    arb = [_bdot(x[3], x[1], _NT).astype(BF16) * incl for x in xs]
    ark = [_bdot(x[3], x[2], _NT).astype(BF16) * incl for x in xs]
    t_inv = [masks_ref[M_EYE] + n * masks_ref[M_BASE] for n in nab]
    for lvl in range(INV_LEVELS):
        half = [_bdot(t, n * masks_ref[M_OFF + lvl], _NN).astype(BF16) for t, n in zip(t_inv, nab)]
        t_inv = [t + _bdot(h, t, _NN).astype(BF16) for t, h in zip(t_inv, half)]
    rhs = [(_bdot(x[0], s, _NT) + _bdot(k, x[4], _NN)).astype(BF16)
           for x, s, k in zip(xs, sb, aak)]
    sa = [_bdot(t, h, _NN).astype(BF16) for t, h in zip(t_inv, rhs)]
    ys = [_bdot(x[3], s, _NT) + _bdot(b, u, _NN) + _bdot(k, x[4], _NN)
          for x, s, b, u, k in zip(xs, sb, arb, sa, ark)]
    for (bb, grp), x, s, u in zip(chains, xs, st, sa):
        decay = p_end[bb * CHUNK:bb * CHUNK + 1, grp * SCAN_W:(grp + 1) * SCAN_W]
        state_ref[bb, grp] = s * decay + _bdot(u, x[5], _TN) + _bdot(x[4], x[6], _TN)
    ys = [y[0:CHUNK] + y[CHUNK:2 * CHUNK] + y[2 * CHUNK:3 * CHUNK] + y[3 * CHUNK:4 * CHUNK]
          for y in ys]
    n_grp = N_HEADS // HEADS_PER_SCAN
    y = jnp.concatenate([jnp.concatenate(ys[bb * n_grp:(bb + 1) * n_grp], axis=1)
                         for bb in range(BATCH_PER_STEP)], axis=0)

    mean = _split_dot(_split_dot(y, red, SEG_TERMS) * (1.0 / HEAD_DIM), exp_m, SEG_TERMS)
    d = y - mean
    var = _split_dot(d * d, red, SEG_TERMS) * (1.0 / HEAD_DIM)
    yn = d * _split_dot(lax.rsqrt(var + GN_EPS), exp_m, SEG_TERMS) * gn_w + gn_b
    bonus = _split_dot(_split_dot(r * k2 * r_k, red, SEG_TERMS), exp_m, SEG_TERMS) * v
    o_ref[...] = ((yn + bonus) * g).astype(BF16).reshape(BATCH_PER_STEP, CHUNK, D_GRP)


def _rwkv(p_r, mu, w0, w2p, a0, a2p, g2, k_k, k_a, r_k, gn_w, gn_b):
    bsz, seq, _ = p_r.shape
    assert bsz % BATCH_PER_STEP == 0
    masks = _scan_masks()
    const = lambda b, s: (0, 0)
    vec = pl.BlockSpec((1, D_GRP), const)
    return pl.pallas_call(
        _rwkv_kernel,
        grid=(bsz // BATCH_PER_STEP, seq // CHUNK),
        in_specs=[pl.BlockSpec((BATCH_PER_STEP, CHUNK, RWKV_COLS), lambda b, s: (b, s, 0)),
                  pl.BlockSpec(masks.shape, lambda b, s: (0, 0, 0)),
                  pl.BlockSpec((1, RWKV_COLS), const),
                  vec, pl.BlockSpec((LANES, D_GRP), const),
                  vec, pl.BlockSpec((LANES, D_GRP), const),
                  pl.BlockSpec((LANES, D_GRP), const),
                  vec, vec, vec, vec, vec],
        out_specs=pl.BlockSpec((BATCH_PER_STEP, CHUNK, D_GRP), lambda b, s: (b, s, 0)),
        out_shape=jax.ShapeDtypeStruct((bsz, seq, D_GRP), BF16),
        scratch_shapes=[pltpu.VMEM((BATCH_PER_STEP, 1, RWKV_COLS), F32),
                        pltpu.VMEM((BATCH_PER_STEP, N_HEADS // HEADS_PER_SCAN, SCAN_W, SCAN_W), F32)],
        compiler_params=pltpu.CompilerParams(
            dimension_semantics=("parallel", "arbitrary"), vmem_limit_bytes=VMEM_LIMIT),
        name="rwkv",
    )(p_r, masks, mu, w0, w2p, a0, a2p, g2, k_k, k_a, r_k, gn_w, gn_b)


def _fox_kernel(q_ref, qb_ref, k_ref, kb_ref, vt_ref, og_ref, ong_ref, o_ref, *, tq):
    hp = pl.program_id(1)
    qi = pl.program_id(2)
    lane = _iota((1, LANES), 1)
    q = q_ref[...]
    qb = qb_ref[...]
    zero = jnp.zeros_like(q)
    qcat = [jnp.concatenate([jnp.where(lane // HEAD_DIM == hh, q, zero),
                             jnp.where(lane // 8 == hp * 2 + hh, qb, zero)], axis=1)
            for hh in range(2)]
    FOX_KEYS = min(FOX_SUB_KEYS, tq)
    n_sub = tq // FOX_KEYS
    key_pos = _iota((FOX_KEYS, tq), 0)
    qry_pos = _iota((FOX_KEYS, tq), 1)

    def tile(j, stats, masked):
        base = j * tq
        sts = []
        for s in range(n_sub):
            start = pl.multiple_of(base + s * FOX_KEYS, FOX_KEYS)
            kcat = jnp.concatenate([k_ref[pl.ds(start, FOX_KEYS), :],
                                    kb_ref[pl.ds(start, FOX_KEYS), :]], axis=1)
            sts.append([lax.dot_general(kcat, qc, _NT, preferred_element_type=F32)
                        for qc in qcat])
        for s in range(n_sub):
            start = pl.multiple_of(base + s * FOX_KEYS, FOX_KEYS)
            vt = vt_ref[:, pl.ds(start, FOX_KEYS)]
            st = sts[s]
            if masked:
                st = [jnp.where(qry_pos >= key_pos + s * FOX_KEYS, x, -jnp.inf) for x in st]
            m_new = [jnp.maximum(c[0], jnp.max(x, axis=0, keepdims=True))
                     for c, x in zip(stats, st)]
            pts = [jnp.exp(x - m) for x, m in zip(st, m_new)]
            pvs = [jnp.dot(vt, pt.astype(BF16), preferred_element_type=F32) for pt in pts]
            new = []
            for hh in range(2):
                m, l, acc = stats[hh]
                alpha = jnp.exp(m - m_new[hh])
                l = alpha * l + jnp.sum(pts[hh], axis=0, keepdims=True)
                acc = alpha * acc + pvs[hh][hh * HEAD_DIM:(hh + 1) * HEAD_DIM, :]
                new.append((m_new[hh], l, acc))
            stats = tuple(new)
        return stats

    init = (jnp.full((1, tq), -jnp.inf, F32), jnp.zeros((1, tq), F32),
            jnp.zeros((HEAD_DIM, tq), F32))
    stats = lax.fori_loop(0, qi, functools.partial(tile, masked=False), (init, init))
    outs = []
    for _, l, acc in tile(qi, stats, True):
        o = acc / l
        outs.append(o * lax.rsqrt(jnp.mean(o * o, axis=0, keepdims=True) + NORM_EPS))
    o = jnp.concatenate(outs, axis=0).T
    o_ref[...] = (o * ong_ref[...] * _sigmoid(og_ref[...].astype(F32))).astype(BF16)


def _fox(p_x, k_bias, q_bias, o_gain, tq):
    bsz, seq, _ = p_x.shape
    npair = N_HEADS // 2
    v_t = jnp.transpose(p_x[:, :, 2 * D_GRP:3 * D_GRP], (0, 2, 1))
    return pl.pallas_call(
        functools.partial(_fox_kernel, tq=tq),
        grid=(bsz, npair, seq // tq),
        in_specs=[pl.BlockSpec((None, tq, LANES), lambda b, h, i: (b, i, h)),
                  pl.BlockSpec((None, tq, LANES), lambda b, h, i: (b, i, 0)),
                  pl.BlockSpec((None, seq, LANES), lambda b, h, i: (b, 0, npair + h)),
                  pl.BlockSpec((None, seq, LANES), lambda b, h, i: (b, 0, 0)),
                  pl.BlockSpec((None, LANES, seq), lambda b, h, i: (b, h, 0)),
                  pl.BlockSpec((None, tq, LANES), lambda b, h, i: (b, i, 3 * npair + h)),
                  pl.BlockSpec((1, LANES), lambda b, h, i: (0, 0))],
        out_specs=pl.BlockSpec((None, tq, LANES), lambda b, h, i: (b, i, h)),
        out_shape=jax.ShapeDtypeStruct((bsz, seq, D_GRP), BF16),
        compiler_params=pltpu.CompilerParams(
            dimension_semantics=("parallel", "parallel", "arbitrary"),
            vmem_limit_bytes=VMEM_LIMIT),
        name="fox",
    )(p_x, q_bias, p_x, k_bias, v_t, p_x, o_gain)


def _outproj_kernel(x_ref, yr_ref, yf_ref, g1_ref, sh_ref, sc_ref, ng_ref, wor_ref, wof_ref,
                    wrt_ref, wrl_ref, brt_ref, x1_ref, h2_ref, idx_ref, gate_ref, rank_ref, cnt_ref,
                    carry_ref):
    @pl.when(pl.program_id(0) == 0)
    def _():
        carry_ref[...] = jnp.zeros_like(carry_ref)

    y = (jnp.dot(yr_ref[...], wor_ref[...], preferred_element_type=F32)
         + jnp.dot(yf_ref[...], wof_ref[...], preferred_element_type=F32))
    x1 = x_ref[...] + g1_ref[...] * y
    x1_ref[...] = x1
    tm = x1.shape[0]
    h = x1 * lax.rsqrt(jnp.mean(x1 * x1, axis=-1, keepdims=True) + NORM_EPS) * ng_ref[...]
    h2 = h * (1.0 + sc_ref[...]) + sh_ref[...]
    h2_ref[...] = _pack_rows(h2)

    lane = _iota((tm, LANES), 1)
    h_hi = h2.astype(BF16)
    h_lo = (h2 - h_hi.astype(F32)).astype(BF16)
    logits = (jnp.dot(h_hi, wrt_ref[...], preferred_element_type=F32)
              + jnp.dot(h_lo, wrt_ref[...], preferred_element_type=F32)
              + jnp.dot(h_hi, wrl_ref[...], preferred_element_type=F32)) + brt_ref[...]
    lg = jnp.where(lane < N_EXPERTS, logits, -jnp.inf)
    picks = []
    hot_sum = jnp.zeros((tm, LANES), F32)
    for _ in range(TOP_K):
        m = jnp.max(lg, axis=-1, keepdims=True)
        sel = jnp.min(jnp.where(lg == m, lane, LANES), axis=-1, keepdims=True)
        hot = lane == sel
        picks.append((m, sel, hot))
        hot_sum = hot_sum + hot.astype(F32)
        lg = jnp.where(hot, -jnp.inf, lg)
    es = [jnp.exp(m - picks[0][0]) for m, _, _ in picks]
    den = es[0] + es[1] + es[2] + es[3]

    before = jnp.dot(_tri(tm, True), hot_sum.astype(BF16), preferred_element_type=F32)
    before = before + carry_ref[...]
    idx_out = jnp.zeros((tm, LANES), jnp.int32)
    gate_out = jnp.zeros((tm, LANES), F32)
    rank_out = jnp.zeros((tm, LANES), jnp.int32)
    for kk, (m, sel, hot) in enumerate(picks):
        rk = jnp.sum(jnp.where(hot, before, 0.0), axis=-1, keepdims=True).astype(jnp.int32)
        idx_out = jnp.where(lane == kk, sel, idx_out)
        gate_out = jnp.where(lane == kk, es[kk] / den, gate_out)
        rank_out = jnp.where(lane == kk, rk, rank_out)
    idx_ref[...] = idx_out
    gate_ref[...] = gate_out
    rank_ref[...] = rank_out
    carry_ref[...] = carry_ref[...] + jnp.sum(hot_sum, axis=0, keepdims=True)
    cnt_ref[...] = carry_ref[...]


def _outproj(x2d, y_r, y_f, gate1, shift2, scale2, norm_g, wo_r, wo_f, w_rt, b_rt, tm, seq,
             row0, t):
    w_rt_hi = w_rt.astype(BF16)
    w_rt_lo = (w_rt - w_rt_hi.astype(F32)).astype(BF16)
    per_b = seq // tm
    blk0 = row0 // tm
    const = lambda i: (0, 0)
    rows = lambda i: (i, 0)
    rows_in = lambda i: (i + blk0, 0)
    mod = pl.BlockSpec((None, 1, D_MODEL), lambda i: ((i + blk0) // per_b, 0, 0))
    return pl.pallas_call(
        _outproj_kernel,
        grid=(t // tm,),
        in_specs=[pl.BlockSpec((tm, D_MODEL), rows_in),
                  pl.BlockSpec((tm, D_GRP), rows_in),
                  pl.BlockSpec((tm, D_GRP), rows_in),
                  mod, mod, mod,
                  pl.BlockSpec((1, D_MODEL), const),
                  pl.BlockSpec((D_GRP, D_MODEL), const),
                  pl.BlockSpec((D_GRP, D_MODEL), const),
                  pl.BlockSpec((D_MODEL, LANES), const),
                  pl.BlockSpec((D_MODEL, LANES), const),
                  pl.BlockSpec((1, LANES), const)],
        out_specs=[pl.BlockSpec((tm, D_MODEL), rows),
                   pl.BlockSpec((tm, D_PACK), rows),
                   pl.BlockSpec((tm, LANES), rows),
                   pl.BlockSpec((tm, LANES), rows),
                   pl.BlockSpec((tm, LANES), rows),
                   pl.BlockSpec((1, LANES), const)],
        out_shape=[jax.ShapeDtypeStruct((t, D_MODEL), F32),
                   jax.ShapeDtypeStruct((t, D_PACK), jnp.uint32),
                   jax.ShapeDtypeStruct((t, LANES), jnp.int32),
                   jax.ShapeDtypeStruct((t, LANES), F32),
                   jax.ShapeDtypeStruct((t, LANES), jnp.int32),
                   jax.ShapeDtypeStruct((1, LANES), F32)],
        scratch_shapes=[pltpu.VMEM((1, LANES), F32)],
        compiler_params=pltpu.CompilerParams(
            dimension_semantics=("arbitrary",), vmem_limit_bytes=VMEM_LIMIT),
        name="outproj",
    )(x2d, y_r, y_f, gate1, shift2, scale2, norm_g, wo_r, wo_f, w_rt_hi, w_rt_lo, b_rt)


SC_CORES = 2
SC_SUBCORES = 16
SC_ROWS = 32


def _sc_gather_rows(idx, src):
    n_workers = SC_CORES * SC_SUBCORES
    m = idx.shape[0]
    d = src.shape[1]
    assert m % (n_workers * SC_ROWS) == 0
    n_chunks = m // (n_workers * SC_ROWS)
    mesh = plsc.VectorSubcoreMesh(core_axis_name="c", subcore_axis_name="s")

    @functools.partial(
        pl.kernel, mesh=mesh,
        out_type=jax.ShapeDtypeStruct((m, d), src.dtype),
        scratch_types=[pltpu.VMEM((n_chunks, SC_ROWS), jnp.int32),
                       pltpu.VMEM((SC_ROWS, d), src.dtype),
                       pltpu.SemaphoreType.DMA],
        name="sc_gather")
    def gather(src_hbm, idx_hbm, out_hbm, idx_v, rows_v, sem):
        wid = lax.axis_index("s") * SC_CORES + lax.axis_index("c")
        pltpu.sync_copy(idx_hbm.at[wid], idx_v)

        @pl.loop(0, n_chunks)
        def _(j):
            pltpu.async_copy(src_hbm.at[idx_v.at[j]], rows_v, sem).wait()
            pltpu.sync_copy(rows_v, out_hbm.at[pl.ds((wid * n_chunks + j) * SC_ROWS, SC_ROWS)])

    return gather(src, idx.reshape(n_workers, n_chunks, SC_ROWS))


def _sc_scatter_rows(src, dest, n_out):
    n_workers = SC_CORES * SC_SUBCORES
    t, d = src.shape
    n_slot = dest.shape[1]
    assert t % (n_workers * SC_ROWS) == 0
    n_chunks = t // (n_workers * SC_ROWS)
    mesh = plsc.VectorSubcoreMesh(core_axis_name="c", subcore_axis_name="s")
    idx = dest.reshape(n_workers, n_chunks, SC_ROWS, n_slot).transpose(0, 1, 3, 2)
    idx = idx.reshape(n_workers, n_chunks * n_slot, SC_ROWS)

    @functools.partial(
        pl.kernel, mesh=mesh,
        out_type=jax.ShapeDtypeStruct((n_out, d), src.dtype),
        scratch_types=[pltpu.VMEM((n_chunks * n_slot, SC_ROWS), jnp.int32),
                       pltpu.VMEM((SC_ROWS, d), src.dtype)],
        name="sc_scatter")
    def scatter(src_hbm, idx_hbm, out_hbm, idx_v, rows_v):
        wid = lax.axis_index("s") * SC_CORES + lax.axis_index("c")
        pltpu.sync_copy(idx_hbm.at[wid], idx_v)

        @pl.loop(0, n_chunks)
        def _(j):
            pltpu.sync_copy(src_hbm.at[pl.ds((wid * n_chunks + j) * SC_ROWS, SC_ROWS)], rows_v)
            for k in range(n_slot):
                pltpu.sync_copy(rows_v, out_hbm.at[idx_v.at[j * n_slot + k]])

    return scatter(src, idx)


def _expert_kernel(be_ref, nv_ref, x_ref, wgu_ref, bgu_ref, wd_ref, bd_ref, o_ref, wgu_bf, wd_bf):
    j = pl.program_id(0)

    @pl.when((j == 0) | (be_ref[j] != be_ref[jnp.maximum(j - 1, 0)]))
    def _():
        for c in range(0, 2 * D_MODEL, CAST_COLS):
            wgu_bf[:, c:c + CAST_COLS] = wgu_ref[:, c:c + CAST_COLS].astype(BF16)
        for c in range(0, D_MODEL, CAST_COLS):
            wd_bf[:, c:c + CAST_COLS] = wd_ref[:, c:c + CAST_COLS].astype(BF16)

    valid = _iota((EXPERT_BLOCK, 1), 0) < nv_ref[j]
    lo, hi = _unpack_rows(jnp.where(valid, x_ref[...], jnp.uint32(0)))
    x = jnp.concatenate([lo.astype(BF16), hi.astype(BF16)], axis=1)
    gu = jnp.dot(x, wgu_bf[...], preferred_element_type=F32) + bgu_ref[...]
    gate = jnp.minimum(gu[:, :D_MODEL], SWIGLU_LIMIT)
    up = jnp.clip(gu[:, D_MODEL:], -SWIGLU_LIMIT, SWIGLU_LIMIT)
    act = gate * _sigmoid(SWIGLU_ALPHA * gate) * (up + 1.0)
    o_ref[...] = _pack_rows(
        jnp.dot(act.astype(BF16), wd_bf[...], preferred_element_type=F32) + bd_ref[...])


def _experts(block_e, n_valid, xs, w_gu, b_gu, w_d, b_d):
    n_blocks = block_e.shape[0]
    grid_spec = pltpu.PrefetchScalarGridSpec(
        num_scalar_prefetch=2,
        grid=(n_blocks,),
        in_specs=[pl.BlockSpec((EXPERT_BLOCK, D_PACK), lambda j, be, nv: (j, 0)),
                  pl.BlockSpec((None, D_MODEL, 2 * D_MODEL), lambda j, be, nv: (be[j], 0, 0)),
                  pl.BlockSpec((None, 1, 2 * D_MODEL), lambda j, be, nv: (be[j], 0, 0)),
                  pl.BlockSpec((None, D_MODEL, D_MODEL), lambda j, be, nv: (be[j], 0, 0)),
                  pl.BlockSpec((None, 1, D_MODEL), lambda j, be, nv: (be[j], 0, 0))],
        out_specs=pl.BlockSpec((EXPERT_BLOCK, D_PACK), lambda j, be, nv: (j, 0)),
        scratch_shapes=[pltpu.VMEM((D_MODEL, 2 * D_MODEL), BF16),
                        pltpu.VMEM((D_MODEL, D_MODEL), BF16)],
    )
    return pl.pallas_call(
        _expert_kernel,
        grid_spec=grid_spec,
        out_shape=jax.ShapeDtypeStruct(xs.shape, jnp.uint32),
        compiler_params=pltpu.CompilerParams(
            dimension_semantics=("arbitrary",), vmem_limit_bytes=VMEM_LIMIT),
        name="experts",
    )(block_e, n_valid, xs, w_gu, b_gu, w_d, b_d)


COMBINE_TOKENS = 512
MOE_SPLITS = 2


def _combine_kernel(yg_ref, x1_ref, gate_ref, g2_ref, fg_ref, o_ref):
    gates = gate_ref[...]
    acc_lo = acc_hi = None
    for kk in range(TOP_K):
        lo, hi = _unpack_rows(yg_ref[kk * COMBINE_TOKENS:(kk + 1) * COMBINE_TOKENS, :])
        g = gates[:, kk:kk + 1]
        acc_lo = g * lo if acc_lo is None else acc_lo + g * lo
        acc_hi = g * hi if acc_hi is None else acc_hi + g * hi
    x2 = x1_ref[...] + g2_ref[...] * jnp.concatenate([acc_lo, acc_hi], axis=1)
    o_ref[...] = x2 * lax.rsqrt(jnp.mean(x2 * x2, axis=-1, keepdims=True) + NORM_EPS) * fg_ref[...]


def _combine_kernel_into(prev_ref, *refs):
    del prev_ref
    _combine_kernel(*refs)


def _combine(yg, x1, gates, gate2, final_g, seq, row0, t_total, prev):
    t = x1.shape[0]
    tm = COMBINE_TOKENS
    per_b = seq // tm
    blk0 = row0 // tm
    rows = lambda i: (i, 0)
    in_specs = [pl.BlockSpec((TOP_K * tm, D_PACK), rows),
                pl.BlockSpec((tm, D_MODEL), rows),
                pl.BlockSpec((tm, LANES), rows),
                pl.BlockSpec((None, 1, D_MODEL), lambda i: ((i + blk0) // per_b, 0, 0)),
                pl.BlockSpec((1, D_MODEL), lambda i: (0, 0))]
    args = (yg, x1, gates, gate2, final_g)
    if prev is not None:
        in_specs = [pl.BlockSpec(memory_space=pl.ANY)] + in_specs
        args = (prev,) + args
    return pl.pallas_call(
        _combine_kernel if prev is None else _combine_kernel_into,
        grid=(t // tm,),
        in_specs=in_specs,
        out_specs=pl.BlockSpec((tm, D_MODEL), lambda i: (i + blk0, 0)),
        out_shape=jax.ShapeDtypeStruct((t_total, D_MODEL), F32),
        input_output_aliases={} if prev is None else {0: 0},
        compiler_params=pltpu.CompilerParams(
            dimension_semantics=("parallel",), vmem_limit_bytes=VMEM_LIMIT),
        name="combine",
    )(*args)


def _moe(h2, idx, gates, rank, counts, x1, gate2, final_g, w_gu, b_gu, w_d, b_d, seq,
         row0, t_total, prev):
    t = h2.shape[0]
    n_slots = t * TOP_K
    n_blocks = -(-n_slots // EXPERT_BLOCK) + N_EXPERTS
    cap = n_blocks * EXPERT_BLOCK
    padded = (counts + EXPERT_BLOCK - 1) // EXPERT_BLOCK * EXPERT_BLOCK
    pad_ends = jnp.cumsum(padded)
    pad_starts = pad_ends - padded
    dest = pad_starts[idx] + rank
    block_starts = jnp.arange(n_blocks, dtype=jnp.int32) * EXPERT_BLOCK
    block_e = jnp.minimum(jnp.sum(block_starts[:, None] >= pad_ends[None, :], axis=1),
                          N_EXPERTS - 1).astype(jnp.int32)
    n_valid = jnp.clip(counts[block_e] - (block_starts - pad_starts[block_e]), 0, EXPERT_BLOCK)

    xs = _sc_scatter_rows(h2, dest, cap)
    yb = _experts(block_e, n_valid.astype(jnp.int32), xs, w_gu, b_gu, w_d, b_d)
    dest_blocks = dest.reshape(-1, COMBINE_TOKENS, TOP_K).transpose(0, 2, 1).reshape(-1)
    yg = _sc_gather_rows(dest_blocks, yb)
    return _combine(yg, x1, gates, gate2, final_g, seq, row0, t_total, prev)


def _layer(x, c_mod, norm1_g, w_in, mu_shift, w0, w2, a0, a2, g2, k_k, k_a, r_k, gn_w, gn_b, b_f,
           q_norm_g, k_norm_g, o_norm_g, w_out, norm2_g, w_router, b_router, w_gate_up,
           b_gate_up, w_down, b_down, final_g, tm_in, tq, tm_out):
    bsz, seq, _ = x.shape
    shift1, scale1, gate1, shift2, scale2, gate2 = (
        m.reshape(bsz, 1, D_MODEL) for m in jnp.split(c_mod, 6, axis=-1))
    row = lambda v: v.reshape(1, -1)

    w_r = w_in[:, :RWKV_COLS].astype(BF16)
    w_x = w_in[:, RWKV_COLS:RWKV_COLS + FOX_MAIN].astype(BF16)
    w_f = jnp.pad(w_in[:, RWKV_COLS + FOX_MAIN:], ((0, 0), (0, LANES - N_HEADS)))
    b_fp = jnp.pad(b_f, (0, LANES - N_HEADS)).reshape(1, LANES)
    qk_gain = jnp.concatenate([jnp.tile(q_norm_g, N_HEADS) * HEAD_DIM ** -0.5,
                               jnp.tile(k_norm_g, N_HEADS)]).reshape(1, -1)
    p_r, p_x, k_bias, q_bias = _inproj(x, shift1, scale1, row(norm1_g), w_r, w_x, w_f, b_fp,
                                       qk_gain, tm_in)

    zeros = jnp.zeros((LANES - 64, D_GRP), F32)
    w2p = jnp.concatenate([w2, zeros], axis=0).astype(BF16)
    a2p = jnp.concatenate([zeros, a2], axis=0).astype(BF16)
    y_r = _rwkv(p_r, row(mu_shift), row(w0), w2p, row(a0), a2p, g2.astype(BF16), row(k_k),
                row(k_a), row(r_k), row(gn_w), row(gn_b))

    y_f = _fox(p_x, k_bias, q_bias, jnp.tile(o_norm_g, 2).reshape(1, LANES), tq)

    t = bsz * seq
    w_rt = jnp.pad(w_router, ((0, 0), (0, LANES - N_EXPERTS)))
    b_rt = jnp.pad(b_router, (0, LANES - N_EXPERTS)).reshape(1, LANES)
    wo = w_out.astype(BF16)
    w_gu, w_d = w_gate_up, w_down
    b_gu, b_d = b_gate_up.reshape(N_EXPERTS, 1, -1), b_down.reshape(N_EXPERTS, 1, -1)
    t_part = t // MOE_SPLITS
    out = None
    for part in range(MOE_SPLITS):
        row0 = part * t_part
        x1, h2, idx, gates, rank, cnt = _outproj(
            x.reshape(t, D_MODEL), y_r.reshape(t, D_GRP), y_f.reshape(t, D_GRP), gate1, shift2,
            scale2, row(norm2_g), wo[:D_GRP], wo[D_GRP:], w_rt, b_rt, tm_out, seq, row0, t_part)
        counts = cnt[0, :N_EXPERTS].astype(jnp.int32)
        out = _moe(h2, idx[:, :TOP_K], gates, rank[:, :TOP_K], counts, x1, gate2, row(final_g),
                   w_gu, b_gu, w_d, b_d, seq, row0, t, out)
    return out.reshape(bsz, seq, D_MODEL)


def kernel(x, c, w_ada, b_ada, norm1_g, w_in, mu_shift, w0, w2, a0, a2, g2, k_k, k_a, r_k, gn_w,
           gn_b, b_f, q_norm_g, k_norm_g, o_norm_g, w_out, norm2_g, w_router, b_router, w_gate_up,
           b_gate_up, w_down, b_down, final_g):
    assert w_ada.shape[0] == 1, "single-layer block"
    c_mod = _adaln(c, w_ada[0], b_ada[0])
    return _layer(x, c_mod, norm1_g[0], w_in[0], mu_shift[0], w0[0], w2[0], a0[0], a2[0], g2[0],
                  k_k[0], k_a[0], r_k[0], gn_w[0], gn_b[0], b_f[0], q_norm_g[0], k_norm_g[0],
                  o_norm_g[0], w_out[0], norm2_g[0], w_router[0], b_router[0], w_gate_up[0],
                  b_gate_up[0], w_down[0], b_down[0], final_g,
                  tm_in=min(512, x.shape[1]), tq=min(512, x.shape[1]), tm_out=min(1024, x.shape[1]))
```

```python
import functools

import jax
import jax.numpy as jnp
from jax import lax
from jax.experimental import pallas as pl
from jax.experimental.pallas import tpu as pltpu
from jax.experimental.pallas import tpu_sc as plsc

F32 = jnp.float32
BF16 = jnp.bfloat16
HIGHEST = lax.Precision.HIGHEST

D_MODEL = 1024
HEAD_DIM = 64
N_HEADS = 8
D_GRP = N_HEADS * HEAD_DIM
RWKV_COLS = 1792
LORA_OFF = 3 * D_GRP
GATE_OFF = LORA_OFF + 128
FOX_MAIN = 4 * D_GRP
N_EXPERTS = 32
TOP_K = 4
EXPERT_BLOCK = 512
CAST_COLS = 256
SWIGLU_ALPHA = 1.702
SWIGLU_LIMIT = 7.0
NORM_EPS = 1e-6
GN_EPS = 64e-5
LANES = 128
CHUNK = 64
FOX_SUB_KEYS = 256
HEADS_PER_SCAN = 4
SCAN_W = HEADS_PER_SCAN * HEAD_DIM
SEG_TERMS = 1
CUM_TERMS = 2
VMEM_LIMIT = 56 * 1024 * 1024


def _dot(a, b):
    return jnp.dot(a.astype(BF16), b.astype(BF16), preferred_element_type=F32)


def _dot_nt(a, b):
    return lax.dot_general(a.astype(BF16), b.astype(BF16), (((1,), (1,)), ((), ())),
                           preferred_element_type=F32)


def _dot_tn(a, b):
    return lax.dot_general(a.astype(BF16), b.astype(BF16), (((0,), (0,)), ((), ())),
                           preferred_element_type=F32)


def _fdot(a, b):
    return jnp.dot(a, b, precision=HIGHEST, preferred_element_type=F32)


def _split_dot(x, m, terms=2, left=False):
    acc = None
    rem = x
    for _ in range(terms):
        part = rem.astype(BF16)
        rem = rem - part.astype(F32)
        d = (jnp.dot(m, part, preferred_element_type=F32) if left
             else jnp.dot(part, m, preferred_element_type=F32))
        acc = d if acc is None else acc + d
    return acc


def _iota(shape, dim):
    return lax.broadcasted_iota(jnp.int32, shape, dim)


def _seg_reduce_mat(n):
    return (_iota((n, LANES), 0) // HEAD_DIM == _iota((n, LANES), 1)).astype(BF16)


def _seg_expand_mat(n):
    return (_iota((LANES, n), 1) // HEAD_DIM == _iota((LANES, n), 0)).astype(BF16)


def _tri(n, strict):
    r, c = _iota((n, n), 0), _iota((n, n), 1)
    return ((r > c) if strict else (r >= c)).astype(BF16)


D_PACK = D_MODEL // 2


def _pack_rows(x):
    lo = lax.bitcast_convert_type(x[:, :D_PACK].astype(BF16).astype(F32), jnp.uint32)
    hi = lax.bitcast_convert_type(x[:, D_PACK:].astype(BF16).astype(F32), jnp.uint32)
    return hi | (lo >> 16)


def _unpack_rows(p):
    lo = lax.bitcast_convert_type(p << 16, F32)
    hi = lax.bitcast_convert_type(p & jnp.uint32(0xFFFF0000), F32)
    return lo, hi


def _log_sigmoid(z):
    return jnp.minimum(z, 0.0) - jnp.log(1.0 + jnp.exp(-jnp.abs(z)))


def _sigmoid(z):
    return 1.0 / (1.0 + jnp.exp(-z))


def _adaln_kernel(c_ref, w_ref, b_ref, o_ref):
    c = c_ref[...]
    o_ref[...] = _fdot(c * _sigmoid(c), w_ref[...]) + b_ref[...]


def _adaln(c, w_ada, b_ada):
    bsz = c.shape[0]
    n_mod = w_ada.shape[1] // D_MODEL
    return pl.pallas_call(
        _adaln_kernel,
        grid=(n_mod,),
        in_specs=[pl.BlockSpec((bsz, D_MODEL), lambda j: (0, 0)),
                  pl.BlockSpec((D_MODEL, D_MODEL), lambda j: (0, j)),
                  pl.BlockSpec((1, D_MODEL), lambda j: (0, j))],
        out_specs=pl.BlockSpec((bsz, D_MODEL), lambda j: (0, j)),
        out_shape=jax.ShapeDtypeStruct((bsz, n_mod * D_MODEL), F32),
        name="adaln",
    )(c, w_ada, b_ada.reshape(1, -1))


def _inproj_kernel(x_ref, sh_ref, sc_ref, g_ref, wr_ref, wx_ref, wfh_ref, wfl_ref, bf_ref, qkg_ref,
                   pr_ref, px_ref, kb_ref, qb_ref, carry_ref):
    @pl.when(pl.program_id(1) == 0)
    def _():
        carry_ref[...] = jnp.zeros_like(carry_ref)

    x = x_ref[...]
    tm = x.shape[0]
    h = x * lax.rsqrt(jnp.mean(x * x, axis=-1, keepdims=True) + NORM_EPS) * g_ref[...]
    h = h * (1.0 + sc_ref[...]) + sh_ref[...]
    hb = h.astype(BF16)
    h_lo = (h - hb.astype(F32)).astype(BF16)

    pr_ref[...] = jnp.dot(hb, wr_ref[...], preferred_element_type=F32).astype(BF16)

    px = jnp.dot(hb, wx_ref[...], preferred_element_type=F32)
    qk = px[:, :2 * D_GRP]
    ss = _split_dot(qk * qk, _seg_reduce_mat(2 * D_GRP), SEG_TERMS)
    inv = lax.rsqrt(ss * (1.0 / HEAD_DIM) + NORM_EPS)
    qk = qk * _split_dot(inv, _seg_expand_mat(2 * D_GRP), SEG_TERMS) * qkg_ref[...]
    px_ref[:, :2 * D_GRP] = qk.astype(BF16)
    px_ref[:, 2 * D_GRP:] = px[:, 2 * D_GRP:].astype(BF16)

    z = (jnp.dot(hb, wfh_ref[...], preferred_element_type=F32)
         + jnp.dot(h_lo, wfh_ref[...], preferred_element_type=F32)
         + jnp.dot(hb, wfl_ref[...], preferred_element_type=F32)) + bf_ref[...]
    cum = _split_dot(_log_sigmoid(z), _tri(tm, False), terms=3, left=True) + carry_ref[...]
    carry_ref[...] = cum[tm - 1:tm, :]

    src, dst = _iota((LANES, LANES), 0), _iota((LANES, LANES), 1)
    parts = []
    rem = cum
    for _ in range(3):
        part = rem.astype(BF16)
        rem = rem - part.astype(F32)
        parts.append(part)

    def spread(offset):
        return sum(jnp.dot(part, ((dst == 8 * src + offset + t) & (src < N_HEADS)).astype(BF16),
                           preferred_element_type=F32) for t, part in enumerate(parts))

    slot = _iota((1, LANES), 1) % 8
    kb_ref[...] = (jnp.where((slot >= 3) & (slot < 6), 1.0, 0.0) - spread(0)).astype(BF16)
    qb_ref[...] = (jnp.where(slot < 3, 1.0, 0.0) + spread(3)).astype(BF16)


def _inproj(x, shift, scale, g, w_r, w_x, w_f, b_f, qk_gain, tm):
    w_f_hi = w_f.astype(BF16)
    w_f_lo = (w_f - w_f_hi.astype(F32)).astype(BF16)
    bsz, seq, _ = x.shape
    const = lambda b, s: (0, 0)
    return pl.pallas_call(
        _inproj_kernel,
        grid=(bsz, seq // tm),
        in_specs=[pl.BlockSpec((None, tm, D_MODEL), lambda b, s: (b, s, 0)),
                  pl.BlockSpec((None, 1, D_MODEL), lambda b, s: (b, 0, 0)),
                  pl.BlockSpec((None, 1, D_MODEL), lambda b, s: (b, 0, 0)),
                  pl.BlockSpec((1, D_MODEL), const),
                  pl.BlockSpec((D_MODEL, RWKV_COLS), const),
                  pl.BlockSpec((D_MODEL, FOX_MAIN), const),
                  pl.BlockSpec((D_MODEL, LANES), const),
                  pl.BlockSpec((D_MODEL, LANES), const),
                  pl.BlockSpec((1, LANES), const),
                  pl.BlockSpec((1, 2 * D_GRP), const)],
        out_specs=[pl.BlockSpec((None, tm, RWKV_COLS), lambda b, s: (b, s, 0)),
                   pl.BlockSpec((None, tm, FOX_MAIN), lambda b, s: (b, s, 0)),
                   pl.BlockSpec((None, tm, LANES), lambda b, s: (b, s, 0)),
                   pl.BlockSpec((None, tm, LANES), lambda b, s: (b, s, 0))],
        out_shape=[jax.ShapeDtypeStruct((bsz, seq, RWKV_COLS), BF16),
                   jax.ShapeDtypeStruct((bsz, seq, FOX_MAIN), BF16),
                   jax.ShapeDtypeStruct((bsz, seq, LANES), BF16),
                   jax.ShapeDtypeStruct((bsz, seq, LANES), BF16)],
        scratch_shapes=[pltpu.VMEM((1, LANES), F32)],
        compiler_params=pltpu.CompilerParams(
            dimension_semantics=("parallel", "arbitrary"), vmem_limit_bytes=VMEM_LIMIT),
        name="inproj",
    )(x, shift, scale, g, w_r, w_x, w_f_hi, w_f_lo, b_f, qk_gain)


_NN = (((1,), (0,)), ((), ()))
_NT = (((1,), (1,)), ((), ()))
_TN = (((0,), (0,)), ((), ()))
SCAN_N = HEADS_PER_SCAN * CHUNK
BATCH_PER_STEP = 8
INV_LEVELS = 5
M_HEAD, M_STRICT, M_INCL, M_EYE, M_BASE, M_OFF = 0, 1, 2, 3, 4, 5


def _bdot(a, b, dims):
    return lax.dot_general(a, b, dims, preferred_element_type=F32)


def _scan_masks():
    rr, cc = _iota((SCAN_N, SCAN_W), 0), _iota((SCAN_N, SCAN_W), 1)
    ri, ci = _iota((SCAN_N, SCAN_N), 0), _iota((SCAN_N, SCAN_N), 1)
    same = ri // CHUNK == ci // CHUNK
    masks = [rr // CHUNK == cc // HEAD_DIM, same & (ri > ci), same & (ri >= ci), ri == ci,
             (ri // 2 == ci // 2) & (ri > ci)]
    blk = 2
    while blk < CHUNK:
        masks.append((ri // (2 * blk) == ci // (2 * blk)) & (ri // blk != ci // blk) & (ri > ci))
        blk *= 2
    return jnp.stack(masks).astype(BF16)


def _rwkv_kernel(p_ref, masks_ref, mu_ref, w0_ref, w2_ref, a0_ref, a2_ref, g2_ref, kk_ref, ka_ref,
                 rk_ref, gnw_ref, gnb_ref, o_ref, last_ref, state_ref):
    @pl.when(pl.program_id(1) == 0)
    def _():
        last_ref[...] = jnp.zeros_like(last_ref)
        state_ref[...] = jnp.zeros_like(state_ref)

    mu, w0, w2, a0, a2, g2, k_k, k_a, r_k, gn_w, gn_b = (
        ref[...] for ref in (mu_ref, w0_ref, w2_ref, a0_ref, a2_ref, g2_ref, kk_ref, ka_ref,
                             rk_ref, gnw_ref, gnb_ref))
    rows = BATCH_PER_STEP * CHUNK
    p = p_ref[...].astype(F32).reshape(rows, RWKV_COLS)
    row_id = _iota((rows, 1), 0)
    prev = pltpu.roll(p, 1, axis=0)
    for bb in range(BATCH_PER_STEP):
        prev = jnp.where(row_id == bb * CHUNK, last_ref[bb], prev)
        last_ref[bb] = p[(bb + 1) * CHUNK - 1:(bb + 1) * CHUNK, :]
    pf = p + mu * (prev - p)
    r = pf[:, 0:D_GRP]
    k = pf[:, D_GRP:2 * D_GRP]
    v = pf[:, 2 * D_GRP:3 * D_GRP]
    lora = pf[:, LORA_OFF:GATE_OFF]
    gd = pf[:, GATE_OFF:RWKV_COLS]

    wlog = w0 + _dot(jnp.tanh(lora), w2)
    neg = -wlog
    softplus = jnp.maximum(neg, 0.0) + jnp.log(1.0 + jnp.exp(-jnp.abs(neg)))
    ld = -jnp.exp(-softplus - 0.5)
    a = _sigmoid(a0 + _dot(lora, a2))
    g = _dot(_sigmoid(gd), g2)

    red, exp_m = _seg_reduce_mat(D_GRP), _seg_expand_mat(D_GRP)
    kk = k * k_k
    n2 = _split_dot(kk * kk, red, SEG_TERMS)
    kk = kk * _split_dot(1.0 / jnp.maximum(jnp.sqrt(n2), 1e-12), exp_m, SEG_TERMS)
    k2 = k * (1.0 + (a - 1.0) * k_a)

    tr, tc = _iota((rows, rows), 0), _iota((rows, rows), 1)
    tri = ((tr >= tc) & (tr // CHUNK == tc // CHUNK)).astype(BF16)
    cl = _split_dot(ld, tri, terms=CUM_TERMS, left=True)
    cl_end = jnp.concatenate(
        [jnp.broadcast_to(cl[(bb + 1) * CHUNK - 1:(bb + 1) * CHUNK, :], (CHUNK, D_GRP))
         for bb in range(BATCH_PER_STEP)], axis=0)
    e_in = jnp.exp(cl)
    e_out = jnp.exp(-cl)
    e_rem = jnp.exp(cl_end - cl)
    p_end = jnp.exp(cl_end)
    kka = kk * a
    ops = [(-kk * jnp.exp(cl - ld)).astype(BF16), (kka * e_out).astype(BF16),
           (k2 * e_out).astype(BF16), (r * e_in).astype(BF16), v.astype(BF16),
           (kka * e_rem).astype(BF16), (k2 * e_rem).astype(BF16)]

    chains = [(bb, grp) for bb in range(BATCH_PER_STEP)
              for grp in range(N_HEADS // HEADS_PER_SCAN)]
    head_mask = masks_ref[M_HEAD]
    strict, incl = masks_ref[M_STRICT], masks_ref[M_INCL]

    def stacked(op, bb, grp):
        part = op[bb * CHUNK:(bb + 1) * CHUNK, grp * SCAN_W:(grp + 1) * SCAN_W]
        return jnp.concatenate([part] * HEADS_PER_SCAN, axis=0) * head_mask

    xs = [[stacked(op, bb, grp) for op in ops] for bb, grp in chains]
    st = [state_ref[bb, grp] for bb, grp in chains]
    sb = [s.astype(BF16) for s in st]
    nab = [_bdot(x[0], x[1], _NT).astype(BF16) for x in xs]
    aak = [_bdot(x[0], x[2], _NT).astype(BF16) * strict for x in xs]
    arb = [_bdot(x[3], x[1], _NT).astype(BF16) * incl for x in xs]
    ark = [_bdot(x[3], x[2], _NT).astype(BF16) * incl for x in xs]
    t_inv = [masks_ref[M_EYE] + n * masks_ref[M_BASE] for n in nab]
    for lvl in range(INV_LEVELS):
        half = [_bdot(t, n * masks_ref[M_OFF + lvl], _NN).astype(BF16) for t, n in zip(t_inv, nab)]
        t_inv = [t + _bdot(h, t, _NN).astype(BF16) for t, h in zip(t_inv, half)]
    rhs = [(_bdot(x[0], s, _NT) + _bdot(k, x[4], _NN)).astype(BF16)
           for x, s, k in zip(xs, sb, aak)]
    sa = [_bdot(t, h, _NN).astype(BF16) for t, h in zip(t_inv, rhs)]
    ys = [_bdot(x[3], s, _NT) + _bdot(b, u, _NN) + _bdot(k, x[4], _NN)
          for x, s, b, u, k in zip(xs, sb, arb, sa, ark)]
    for (bb, grp), x, s, u in zip(chains, xs, st, sa):
        decay = p_end[bb * CHUNK:bb * CHUNK + 1, grp * SCAN_W:(grp + 1) * SCAN_W]
        state_ref[bb, grp] = s * decay + _bdot(u, x[5], _TN) + _bdot(x[4], x[6], _TN)
    ys = [y[0:CHUNK] + y[CHUNK:2 * CHUNK] + y[2 * CHUNK:3 * CHUNK] + y[3 * CHUNK:4 * CHUNK]
          for y in ys]
    n_grp = N_HEADS // HEADS_PER_SCAN
    y = jnp.concatenate([jnp.concatenate(ys[bb * n_grp:(bb + 1) * n_grp], axis=1)
                         for bb in range(BATCH_PER_STEP)], axis=0)

    mean = _split_dot(_split_dot(y, red, SEG_TERMS) * (1.0 / HEAD_DIM), exp_m, SEG_TERMS)
    d = y - mean
    var = _split_dot(d * d, red, SEG_TERMS) * (1.0 / HEAD_DIM)
    yn = d * _split_dot(lax.rsqrt(var + GN_EPS), exp_m, SEG_TERMS) * gn_w + gn_b
    bonus = _split_dot(_split_dot(r * k2 * r_k, red, SEG_TERMS), exp_m, SEG_TERMS) * v
    o_ref[...] = ((yn + bonus) * g).astype(BF16).reshape(BATCH_PER_STEP, CHUNK, D_GRP)


def _rwkv(p_r, mu, w0, w2p, a0, a2p, g2, k_k, k_a, r_k, gn_w, gn_b):
    bsz, seq, _ = p_r.shape
    assert bsz % BATCH_PER_STEP == 0
    masks = _scan_masks()
    const = lambda b, s: (0, 0)
    vec = pl.BlockSpec((1, D_GRP), const)
    return pl.pallas_call(
        _rwkv_kernel,
        grid=(bsz // BATCH_PER_STEP, seq // CHUNK),
        in_specs=[pl.BlockSpec((BATCH_PER_STEP, CHUNK, RWKV_COLS), lambda b, s: (b, s, 0)),
                  pl.BlockSpec(masks.shape, lambda b, s: (0, 0, 0)),
                  pl.BlockSpec((1, RWKV_COLS), const),
                  vec, pl.BlockSpec((LANES, D_GRP), const),
                  vec, pl.BlockSpec((LANES, D_GRP), const),
                  pl.BlockSpec((LANES, D_GRP), const),
                  vec, vec, vec, vec, vec],
        out_specs=pl.BlockSpec((BATCH_PER_STEP, CHUNK, D_GRP), lambda b, s: (b, s, 0)),
        out_shape=jax.ShapeDtypeStruct((bsz, seq, D_GRP), BF16),
        scratch_shapes=[pltpu.VMEM((BATCH_PER_STEP, 1, RWKV_COLS), F32),
                        pltpu.VMEM((BATCH_PER_STEP, N_HEADS // HEADS_PER_SCAN, SCAN_W, SCAN_W), F32)],
        compiler_params=pltpu.CompilerParams(
            dimension_semantics=("parallel", "arbitrary"), vmem_limit_bytes=VMEM_LIMIT),
        name="rwkv",
    )(p_r, masks, mu, w0, w2p, a0, a2p, g2, k_k, k_a, r_k, gn_w, gn_b)


def _fox_kernel(q_ref, qb_ref, k_ref, kb_ref, vt_ref, og_ref, ong_ref, o_ref, *, tq):
    hp = pl.program_id(1)
    qi = pl.program_id(2)
    lane = _iota((1, LANES), 1)
    q = q_ref[...]
    qb = qb_ref[...]
    zero = jnp.zeros_like(q)
    qcat = [jnp.concatenate([jnp.where(lane // HEAD_DIM == hh, q, zero),
                             jnp.where(lane // 8 == hp * 2 + hh, qb, zero)], axis=1)
            for hh in range(2)]
    FOX_KEYS = min(FOX_SUB_KEYS, tq)
    n_sub = tq // FOX_KEYS
    key_pos = _iota((FOX_KEYS, tq), 0)
    qry_pos = _iota((FOX_KEYS, tq), 1)

    def tile(j, stats, masked):
        base = j * tq
        sts = []
        for s in range(n_sub):
            start = pl.multiple_of(base + s * FOX_KEYS, FOX_KEYS)
            kcat = jnp.concatenate([k_ref[pl.ds(start, FOX_KEYS), :],
                                    kb_ref[pl.ds(start, FOX_KEYS), :]], axis=1)
            sts.append([lax.dot_general(kcat, qc, _NT, preferred_element_type=F32)
                        for qc in qcat])
        for s in range(n_sub):
            start = pl.multiple_of(base + s * FOX_KEYS, FOX_KEYS)
            vt = vt_ref[:, pl.ds(start, FOX_KEYS)]
            st = sts[s]
            if masked:
                st = [jnp.where(qry_pos >= key_pos + s * FOX_KEYS, x, -jnp.inf) for x in st]
            m_new = [jnp.maximum(c[0], jnp.max(x, axis=0, keepdims=True))
                     for c, x in zip(stats, st)]
            pts = [jnp.exp(x - m) for x, m in zip(st, m_new)]
            pvs = [jnp.dot(vt, pt.astype(BF16), preferred_element_type=F32) for pt in pts]
            new = []
            for hh in range(2):
                m, l, acc = stats[hh]
                alpha = jnp.exp(m - m_new[hh])
                l = alpha * l + jnp.sum(pts[hh], axis=0, keepdims=True)
                acc = alpha * acc + pvs[hh][hh * HEAD_DIM:(hh + 1) * HEAD_DIM, :]
                new.append((m_new[hh], l, acc))
            stats = tuple(new)
        return stats

    init = (jnp.full((1, tq), -jnp.inf, F32), jnp.zeros((1, tq), F32),
            jnp.zeros((HEAD_DIM, tq), F32))
    stats = lax.fori_loop(0, qi, functools.partial(tile, masked=False), (init, init))
    outs = []
    for _, l, acc in tile(qi, stats, True):
        o = acc / l
        outs.append(o * lax.rsqrt(jnp.mean(o * o, axis=0, keepdims=True) + NORM_EPS))
    o = jnp.concatenate(outs, axis=0).T
    o_ref[...] = (o * ong_ref[...] * _sigmoid(og_ref[...].astype(F32))).astype(BF16)


def _fox(p_x, k_bias, q_bias, o_gain, tq):
    bsz, seq, _ = p_x.shape
    npair = N_HEADS // 2
    v_t = jnp.transpose(p_x[:, :, 2 * D_GRP:3 * D_GRP], (0, 2, 1))
    return pl.pallas_call(
        functools.partial(_fox_kernel, tq=tq),
        grid=(bsz, npair, seq // tq),
        in_specs=[pl.BlockSpec((None, tq, LANES), lambda b, h, i: (b, i, h)),
                  pl.BlockSpec((None, tq, LANES), lambda b, h, i: (b, i, 0)),
                  pl.BlockSpec((None, seq, LANES), lambda b, h, i: (b, 0, npair + h)),
                  pl.BlockSpec((None, seq, LANES), lambda b, h, i: (b, 0, 0)),
                  pl.BlockSpec((None, LANES, seq), lambda b, h, i: (b, h, 0)),
                  pl.BlockSpec((None, tq, LANES), lambda b, h, i: (b, i, 3 * npair + h)),
                  pl.BlockSpec((1, LANES), lambda b, h, i: (0, 0))],
        out_specs=pl.BlockSpec((None, tq, LANES), lambda b, h, i: (b, i, h)),
        out_shape=jax.ShapeDtypeStruct((bsz, seq, D_GRP), BF16),
        compiler_params=pltpu.CompilerParams(
            dimension_semantics=("parallel", "parallel", "arbitrary"),
            vmem_limit_bytes=VMEM_LIMIT),
        name="fox",
    )(p_x, q_bias, p_x, k_bias, v_t, p_x, o_gain)


def _outproj_kernel(x_ref, yr_ref, yf_ref, g1_ref, sh_ref, sc_ref, ng_ref, wor_ref, wof_ref,
                    wrt_ref, wrl_ref, brt_ref, x1_ref, h2_ref, idx_ref, gate_ref, rank_ref, cnt_ref,
                    carry_ref):
    @pl.when(pl.program_id(0) == 0)
    def _():
        carry_ref[...] = jnp.zeros_like(carry_ref)

    y = (jnp.dot(yr_ref[...], wor_ref[...], preferred_element_type=F32)
         + jnp.dot(yf_ref[...], wof_ref[...], preferred_element_type=F32))
    x1 = x_ref[...] + g1_ref[...] * y
    x1_ref[...] = x1
    tm = x1.shape[0]
    h = x1 * lax.rsqrt(jnp.mean(x1 * x1, axis=-1, keepdims=True) + NORM_EPS) * ng_ref[...]
    h2 = h * (1.0 + sc_ref[...]) + sh_ref[...]
    h2_ref[...] = _pack_rows(h2)

    lane = _iota((tm, LANES), 1)
    h_hi = h2.astype(BF16)
    h_lo = (h2 - h_hi.astype(F32)).astype(BF16)
    logits = (jnp.dot(h_hi, wrt_ref[...], preferred_element_type=F32)
              + jnp.dot(h_lo, wrt_ref[...], preferred_element_type=F32)
              + jnp.dot(h_hi, wrl_ref[...], preferred_element_type=F32)) + brt_ref[...]
    lg = jnp.where(lane < N_EXPERTS, logits, -jnp.inf)
    picks = []
    hot_sum = jnp.zeros((tm, LANES), F32)
    for _ in range(TOP_K):
        m = jnp.max(lg, axis=-1, keepdims=True)
        sel = jnp.min(jnp.where(lg == m, lane, LANES), axis=-1, keepdims=True)
        hot = lane == sel
        picks.append((m, sel, hot))
        hot_sum = hot_sum + hot.astype(F32)
        lg = jnp.where(hot, -jnp.inf, lg)
    es = [jnp.exp(m - picks[0][0]) for m, _, _ in picks]
    den = es[0] + es[1] + es[2] + es[3]

    before = jnp.dot(_tri(tm, True), hot_sum.astype(BF16), preferred_element_type=F32)
    before = before + carry_ref[...]
    idx_out = jnp.zeros((tm, LANES), jnp.int32)
    gate_out = jnp.zeros((tm, LANES), F32)
    rank_out = jnp.zeros((tm, LANES), jnp.int32)
    for kk, (m, sel, hot) in enumerate(picks):
        rk = jnp.sum(jnp.where(hot, before, 0.0), axis=-1, keepdims=True).astype(jnp.int32)
        idx_out = jnp.where(lane == kk, sel, idx_out)
        gate_out = jnp.where(lane == kk, es[kk] / den, gate_out)
        rank_out = jnp.where(lane == kk, rk, rank_out)
    idx_ref[...] = idx_out
    gate_ref[...] = gate_out
    rank_ref[...] = rank_out
    carry_ref[...] = carry_ref[...] + jnp.sum(hot_sum, axis=0, keepdims=True)
    cnt_ref[...] = carry_ref[...]


def _outproj(x2d, y_r, y_f, gate1, shift2, scale2, norm_g, wo_r, wo_f, w_rt, b_rt, tm, seq,
             row0, t):
    w_rt_hi = w_rt.astype(BF16)
    w_rt_lo = (w_rt - w_rt_hi.astype(F32)).astype(BF16)
    per_b = seq // tm
    blk0 = row0 // tm
    const = lambda i: (0, 0)
    rows = lambda i: (i, 0)
    rows_in = lambda i: (i + blk0, 0)
    mod = pl.BlockSpec((None, 1, D_MODEL), lambda i: ((i + blk0) // per_b, 0, 0))
    return pl.pallas_call(
        _outproj_kernel,
        grid=(t // tm,),
        in_specs=[pl.BlockSpec((tm, D_MODEL), rows_in),
                  pl.BlockSpec((tm, D_GRP), rows_in),
                  pl.BlockSpec((tm, D_GRP), rows_in),
                  mod, mod, mod,
                  pl.BlockSpec((1, D_MODEL), const),
                  pl.BlockSpec((D_GRP, D_MODEL), const),
                  pl.BlockSpec((D_GRP, D_MODEL), const),
                  pl.BlockSpec((D_MODEL, LANES), const),
                  pl.BlockSpec((D_MODEL, LANES), const),
                  pl.BlockSpec((1, LANES), const)],
        out_specs=[pl.BlockSpec((tm, D_MODEL), rows),
                   pl.BlockSpec((tm, D_PACK), rows),
                   pl.BlockSpec((tm, LANES), rows),
                   pl.BlockSpec((tm, LANES), rows),
                   pl.BlockSpec((tm, LANES), rows),
                   pl.BlockSpec((1, LANES), const)],
        out_shape=[jax.ShapeDtypeStruct((t, D_MODEL), F32),
                   jax.ShapeDtypeStruct((t, D_PACK), jnp.uint32),
                   jax.ShapeDtypeStruct((t, LANES), jnp.int32),
                   jax.ShapeDtypeStruct((t, LANES), F32),
                   jax.ShapeDtypeStruct((t, LANES), jnp.int32),
                   jax.ShapeDtypeStruct((1, LANES), F32)],
        scratch_shapes=[pltpu.VMEM((1, LANES), F32)],
        compiler_params=pltpu.CompilerParams(
            dimension_semantics=("arbitrary",), vmem_limit_bytes=VMEM_LIMIT),
        name="outproj",
    )(x2d, y_r, y_f, gate1, shift2, scale2, norm_g, wo_r, wo_f, w_rt_hi, w_rt_lo, b_rt)


SC_CORES = 2
SC_SUBCORES = 16
SC_ROWS = 32


def _sc_gather_rows(idx, src):
    n_workers = SC_CORES * SC_SUBCORES
    m = idx.shape[0]
    d = src.shape[1]
    assert m % (n_workers * SC_ROWS) == 0
    n_chunks = m // (n_workers * SC_ROWS)
    mesh = plsc.VectorSubcoreMesh(core_axis_name="c", subcore_axis_name="s")

    @functools.partial(
        pl.kernel, mesh=mesh,
        out_type=jax.ShapeDtypeStruct((m, d), src.dtype),
        scratch_types=[pltpu.VMEM((n_chunks, SC_ROWS), jnp.int32),
                       pltpu.VMEM((SC_ROWS, d), src.dtype),
                       pltpu.SemaphoreType.DMA],
        name="sc_gather")
    def gather(src_hbm, idx_hbm, out_hbm, idx_v, rows_v, sem):
        wid = lax.axis_index("s") * SC_CORES + lax.axis_index("c")
        pltpu.sync_copy(idx_hbm.at[wid], idx_v)

        @pl.loop(0, n_chunks)
        def _(j):
            pltpu.async_copy(src_hbm.at[idx_v.at[j]], rows_v, sem).wait()
            pltpu.sync_copy(rows_v, out_hbm.at[pl.ds((wid * n_chunks + j) * SC_ROWS, SC_ROWS)])

    return gather(src, idx.reshape(n_workers, n_chunks, SC_ROWS))


def _sc_scatter_rows(src, dest, n_out):
    n_workers = SC_CORES * SC_SUBCORES
    t, d = src.shape
    n_slot = dest.shape[1]
    assert t % (n_workers * SC_ROWS) == 0
    n_chunks = t // (n_workers * SC_ROWS)
    mesh = plsc.VectorSubcoreMesh(core_axis_name="c", subcore_axis_name="s")
    idx = dest.reshape(n_workers, n_chunks, SC_ROWS, n_slot).transpose(0, 1, 3, 2)
    idx = idx.reshape(n_workers, n_chunks * n_slot, SC_ROWS)

    @functools.partial(
        pl.kernel, mesh=mesh,
        out_type=jax.ShapeDtypeStruct((n_out, d), src.dtype),
        scratch_types=[pltpu.VMEM((n_chunks * n_slot, SC_ROWS), jnp.int32),
                       pltpu.VMEM((SC_ROWS, d), src.dtype)],
        name="sc_scatter")
    def scatter(src_hbm, idx_hbm, out_hbm, idx_v, rows_v):
        wid = lax.axis_index("s") * SC_CORES + lax.axis_index("c")
        pltpu.sync_copy(idx_hbm.at[wid], idx_v)

        @pl.loop(0, n_chunks)
        def _(j):
            pltpu.sync_copy(src_hbm.at[pl.ds((wid * n_chunks + j) * SC_ROWS, SC_ROWS)], rows_v)
            for k in range(n_slot):
                pltpu.sync_copy(rows_v, out_hbm.at[idx_v.at[j * n_slot + k]])

    return scatter(src, idx)


def _expert_kernel(be_ref, nv_ref, x_ref, wgu_ref, bgu_ref, wd_ref, bd_ref, o_ref, wgu_bf, wd_bf):
    j = pl.program_id(0)

    @pl.when((j == 0) | (be_ref[j] != be_ref[jnp.maximum(j - 1, 0)]))
    def _():
        for c in range(0, 2 * D_MODEL, CAST_COLS):
            wgu_bf[:, c:c + CAST_COLS] = wgu_ref[:, c:c + CAST_COLS].astype(BF16)
        for c in range(0, D_MODEL, CAST_COLS):
            wd_bf[:, c:c + CAST_COLS] = wd_ref[:, c:c + CAST_COLS].astype(BF16)

    valid = _iota((EXPERT_BLOCK, 1), 0) < nv_ref[j]
    lo, hi = _unpack_rows(jnp.where(valid, x_ref[...], jnp.uint32(0)))
    x = jnp.concatenate([lo.astype(BF16), hi.astype(BF16)], axis=1)
    gu = jnp.dot(x, wgu_bf[...], preferred_element_type=F32) + bgu_ref[...]
    gate = jnp.minimum(gu[:, :D_MODEL], SWIGLU_LIMIT)
    up = jnp.clip(gu[:, D_MODEL:], -SWIGLU_LIMIT, SWIGLU_LIMIT)
    act = gate * _sigmoid(SWIGLU_ALPHA * gate) * (up + 1.0)
    o_ref[...] = _pack_rows(
        jnp.dot(act.astype(BF16), wd_bf[...], preferred_element_type=F32) + bd_ref[...])


def _experts(block_e, n_valid, xs, w_gu, b_gu, w_d, b_d):
    n_blocks = block_e.shape[0]
    grid_spec = pltpu.PrefetchScalarGridSpec(
        num_scalar_prefetch=2,
        grid=(n_blocks,),
        in_specs=[pl.BlockSpec((EXPERT_BLOCK, D_PACK), lambda j, be, nv: (j, 0)),
                  pl.BlockSpec((None, D_MODEL, 2 * D_MODEL), lambda j, be, nv: (be[j], 0, 0)),
                  pl.BlockSpec((None, 1, 2 * D_MODEL), lambda j, be, nv: (be[j], 0, 0)),
                  pl.BlockSpec((None, D_MODEL, D_MODEL), lambda j, be, nv: (be[j], 0, 0)),
                  pl.BlockSpec((None, 1, D_MODEL), lambda j, be, nv: (be[j], 0, 0))],
        out_specs=pl.BlockSpec((EXPERT_BLOCK, D_PACK), lambda j, be, nv: (j, 0)),
        scratch_shapes=[pltpu.VMEM((D_MODEL, 2 * D_MODEL), BF16),
                        pltpu.VMEM((D_MODEL, D_MODEL), BF16)],
    )
    return pl.pallas_call(
        _expert_kernel,
        grid_spec=grid_spec,
        out_shape=jax.ShapeDtypeStruct(xs.shape, jnp.uint32),
        compiler_params=pltpu.CompilerParams(
            dimension_semantics=("arbitrary",), vmem_limit_bytes=VMEM_LIMIT),
        name="experts",
    )(block_e, n_valid, xs, w_gu, b_gu, w_d, b_d)


COMBINE_TOKENS = 512
MOE_SPLITS = 2


def _combine_kernel(yg_ref, x1_ref, gate_ref, g2_ref, fg_ref, o_ref):
    gates = gate_ref[...]
    acc_lo = acc_hi = None
    for kk in range(TOP_K):
        lo, hi = _unpack_rows(yg_ref[kk * COMBINE_TOKENS:(kk + 1) * COMBINE_TOKENS, :])
        g = gates[:, kk:kk + 1]
        acc_lo = g * lo if acc_lo is None else acc_lo + g * lo
        acc_hi = g * hi if acc_hi is None else acc_hi + g * hi
    x2 = x1_ref[...] + g2_ref[...] * jnp.concatenate([acc_lo, acc_hi], axis=1)
    o_ref[...] = x2 * lax.rsqrt(jnp.mean(x2 * x2, axis=-1, keepdims=True) + NORM_EPS) * fg_ref[...]


def _combine_kernel_into(prev_ref, *refs):
    del prev_ref
    _combine_kernel(*refs)


def _combine(yg, x1, gates, gate2, final_g, seq, row0, t_total, prev):
    t = x1.shape[0]
    tm = COMBINE_TOKENS
    per_b = seq // tm
    blk0 = row0 // tm
    rows = lambda i: (i, 0)
    in_specs = [pl.BlockSpec((TOP_K * tm, D_PACK), rows),
                pl.BlockSpec((tm, D_MODEL), rows),
                pl.BlockSpec((tm, LANES), rows),
                pl.BlockSpec((None, 1, D_MODEL), lambda i: ((i + blk0) // per_b, 0, 0)),
                pl.BlockSpec((1, D_MODEL), lambda i: (0, 0))]
    args = (yg, x1, gates, gate2, final_g)
    if prev is not None:
        in_specs = [pl.BlockSpec(memory_space=pl.ANY)] + in_specs
        args = (prev,) + args
    return pl.pallas_call(
        _combine_kernel if prev is None else _combine_kernel_into,
        grid=(t // tm,),
        in_specs=in_specs,
        out_specs=pl.BlockSpec((tm, D_MODEL), lambda i: (i + blk0, 0)),
        out_shape=jax.ShapeDtypeStruct((t_total, D_MODEL), F32),
        input_output_aliases={} if prev is None else {0: 0},
        compiler_params=pltpu.CompilerParams(
            dimension_semantics=("parallel",), vmem_limit_bytes=VMEM_LIMIT),
        name="combine",
    )(*args)


def _moe(h2, idx, gates, rank, counts, x1, gate2, final_g, w_gu, b_gu, w_d, b_d, seq,
         row0, t_total, prev):
    t = h2.shape[0]
    n_slots = t * TOP_K
    n_blocks = -(-n_slots // EXPERT_BLOCK) + N_EXPERTS
    cap = n_blocks * EXPERT_BLOCK
    padded = (counts + EXPERT_BLOCK - 1) // EXPERT_BLOCK * EXPERT_BLOCK
    pad_ends = jnp.cumsum(padded)
    pad_starts = pad_ends - padded
    dest = pad_starts[idx] + rank
    block_starts = jnp.arange(n_blocks, dtype=jnp.int32) * EXPERT_BLOCK
    block_e = jnp.minimum(jnp.sum(block_starts[:, None] >= pad_ends[None, :], axis=1),
                          N_EXPERTS - 1).astype(jnp.int32)
    n_valid = jnp.clip(counts[block_e] - (block_starts - pad_starts[block_e]), 0, EXPERT_BLOCK)

    xs = _sc_scatter_rows(h2, dest, cap)
    yb = _experts(block_e, n_valid.astype(jnp.int32), xs, w_gu, b_gu, w_d, b_d)
    dest_blocks = dest.reshape(-1, COMBINE_TOKENS, TOP_K).transpose(0, 2, 1).reshape(-1)
    yg = _sc_gather_rows(dest_blocks, yb)
    return _combine(yg, x1, gates, gate2, final_g, seq, row0, t_total, prev)


def _layer(x, c_mod, norm1_g, w_in, mu_shift, w0, w2, a0, a2, g2, k_k, k_a, r_k, gn_w, gn_b, b_f,
           q_norm_g, k_norm_g, o_norm_g, w_out, norm2_g, w_router, b_router, w_gate_up,
           b_gate_up, w_down, b_down, final_g, tm_in, tq, tm_out):
    bsz, seq, _ = x.shape
    shift1, scale1, gate1, shift2, scale2, gate2 = (
        m.reshape(bsz, 1, D_MODEL) for m in jnp.split(c_mod, 6, axis=-1))
    row = lambda v: v.reshape(1, -1)

    w_r = w_in[:, :RWKV_COLS].astype(BF16)
    w_x = w_in[:, RWKV_COLS:RWKV_COLS + FOX_MAIN].astype(BF16)
    w_f = jnp.pad(w_in[:, RWKV_COLS + FOX_MAIN:], ((0, 0), (0, LANES - N_HEADS)))
    b_fp = jnp.pad(b_f, (0, LANES - N_HEADS)).reshape(1, LANES)
    qk_gain = jnp.concatenate([jnp.tile(q_norm_g, N_HEADS) * HEAD_DIM ** -0.5,
                               jnp.tile(k_norm_g, N_HEADS)]).reshape(1, -1)
    p_r, p_x, k_bias, q_bias = _inproj(x, shift1, scale1, row(norm1_g), w_r, w_x, w_f, b_fp,
                                       qk_gain, tm_in)

    zeros = jnp.zeros((LANES - 64, D_GRP), F32)
    w2p = jnp.concatenate([w2, zeros], axis=0).astype(BF16)
    a2p = jnp.concatenate([zeros, a2], axis=0).astype(BF16)
    y_r = _rwkv(p_r, row(mu_shift), row(w0), w2p, row(a0), a2p, g2.astype(BF16), row(k_k),
                row(k_a), row(r_k), row(gn_w), row(gn_b))

    y_f = _fox(p_x, k_bias, q_bias, jnp.tile(o_norm_g, 2).reshape(1, LANES), tq)

    t = bsz * seq
    w_rt = jnp.pad(w_router, ((0, 0), (0, LANES - N_EXPERTS)))
    b_rt = jnp.pad(b_router, (0, LANES - N_EXPERTS)).reshape(1, LANES)
    wo = w_out.astype(BF16)
    w_gu, w_d = w_gate_up, w_down
    b_gu, b_d = b_gate_up.reshape(N_EXPERTS, 1, -1), b_down.reshape(N_EXPERTS, 1, -1)
    t_part = t // MOE_SPLITS
    out = None
    for part in range(MOE_SPLITS):
        row0 = part * t_part
        x1, h2, idx, gates, rank, cnt = _outproj(
            x.reshape(t, D_MODEL), y_r.reshape(t, D_GRP), y_f.reshape(t, D_GRP), gate1, shift2,
            scale2, row(norm2_g), wo[:D_GRP], wo[D_GRP:], w_rt, b_rt, tm_out, seq, row0, t_part)
        counts = cnt[0, :N_EXPERTS].astype(jnp.int32)
        out = _moe(h2, idx[:, :TOP_K], gates, rank[:, :TOP_K], counts, x1, gate2, row(final_g),
                   w_gu, b_gu, w_d, b_d, seq, row0, t, out)
    return out.reshape(bsz, seq, D_MODEL)


def kernel(x, c, w_ada, b_ada, norm1_g, w_in, mu_shift, w0, w2, a0, a2, g2, k_k, k_a, r_k, gn_w,
           gn_b, b_f, q_norm_g, k_norm_g, o_norm_g, w_out, norm2_g, w_router, b_router, w_gate_up,
           b_gate_up, w_down, b_down, final_g):
    assert w_ada.shape[0] == 1, "single-layer block"
    c_mod = _adaln(c, w_ada[0], b_ada[0])
    return _layer(x, c_mod, norm1_g[0], w_in[0], mu_shift[0], w0[0], w2[0], a0[0], a2[0], g2[0],
                  k_k[0], k_a[0], r_k[0], gn_w[0], gn_b[0], b_f[0], q_norm_g[0], k_norm_g[0],
                  o_norm_g[0], w_out[0], norm2_g[0], w_router[0], b_router[0], w_gate_up[0],
                  b_gate_up[0], w_down[0], b_down[0], final_g,
                  tm_in=min(512, x.shape[1]), tq=min(1024, x.shape[1]), tm_out=min(1024, x.shape[1]))
```

```python
import functools

import jax
import jax.numpy as jnp
from jax import lax
from jax.experimental import pallas as pl
from jax.experimental.pallas import tpu as pltpu
from jax.experimental.pallas import tpu_sc as plsc

F32 = jnp.float32
BF16 = jnp.bfloat16
HIGHEST = lax.Precision.HIGHEST

D_MODEL = 1024
HEAD_DIM = 64
N_HEADS = 8
D_GRP = N_HEADS * HEAD_DIM
RWKV_COLS = 1792
LORA_OFF = 3 * D_GRP
GATE_OFF = LORA_OFF + 128
FOX_MAIN = 4 * D_GRP
N_EXPERTS = 32
TOP_K = 4
EXPERT_BLOCK = 512
CAST_COLS = 256
SWIGLU_ALPHA = 1.702
SWIGLU_LIMIT = 7.0
NORM_EPS = 1e-6
GN_EPS = 64e-5
LANES = 128
CHUNK = 64
FOX_SUB_KEYS = 256
HEADS_PER_SCAN = 4
SCAN_W = HEADS_PER_SCAN * HEAD_DIM
SEG_TERMS = 1
CUM_TERMS = 2
VMEM_LIMIT = 56 * 1024 * 1024


def _dot(a, b):
    return jnp.dot(a.astype(BF16), b.astype(BF16), preferred_element_type=F32)


def _dot_nt(a, b):
    return lax.dot_general(a.astype(BF16), b.astype(BF16), (((1,), (1,)), ((), ())),
                           preferred_element_type=F32)


def _dot_tn(a, b):
    return lax.dot_general(a.astype(BF16), b.astype(BF16), (((0,), (0,)), ((), ())),
                           preferred_element_type=F32)


def _fdot(a, b):
    return jnp.dot(a, b, precision=HIGHEST, preferred_element_type=F32)


def _split_dot(x, m, terms=2, left=False):
    acc = None
    rem = x
    for _ in range(terms):
        part = rem.astype(BF16)
        rem = rem - part.astype(F32)
        d = (jnp.dot(m, part, preferred_element_type=F32) if left
             else jnp.dot(part, m, preferred_element_type=F32))
        acc = d if acc is None else acc + d
    return acc


def _iota(shape, dim):
    return lax.broadcasted_iota(jnp.int32, shape, dim)


def _seg_reduce_mat(n):
    return (_iota((n, LANES), 0) // HEAD_DIM == _iota((n, LANES), 1)).astype(BF16)


def _seg_expand_mat(n):
    return (_iota((LANES, n), 1) // HEAD_DIM == _iota((LANES, n), 0)).astype(BF16)


def _tri(n, strict):
    r, c = _iota((n, n), 0), _iota((n, n), 1)
    return ((r > c) if strict else (r >= c)).astype(BF16)


D_PACK = D_MODEL // 2


def _pack_rows(x):
    lo = lax.bitcast_convert_type(x[:, :D_PACK].astype(BF16).astype(F32), jnp.uint32)
    hi = lax.bitcast_convert_type(x[:, D_PACK:].astype(BF16).astype(F32), jnp.uint32)
    return hi | (lo >> 16)


def _unpack_rows(p):
    lo = lax.bitcast_convert_type(p << 16, F32)
    hi = lax.bitcast_convert_type(p & jnp.uint32(0xFFFF0000), F32)
    return lo, hi


def _log_sigmoid(z):
    return jnp.minimum(z, 0.0) - jnp.log(1.0 + jnp.exp(-jnp.abs(z)))


def _sigmoid(z):
    return 1.0 / (1.0 + jnp.exp(-z))


def _adaln_kernel(c_ref, w_ref, b_ref, o_ref):
    c = c_ref[...]
    o_ref[...] = _fdot(c * _sigmoid(c), w_ref[...]) + b_ref[...]


def _adaln(c, w_ada, b_ada):
    bsz = c.shape[0]
    n_mod = w_ada.shape[1] // D_MODEL
    return pl.pallas_call(
        _adaln_kernel,
        grid=(n_mod,),
        in_specs=[pl.BlockSpec((bsz, D_MODEL), lambda j: (0, 0)),
                  pl.BlockSpec((D_MODEL, D_MODEL), lambda j: (0, j)),
                  pl.BlockSpec((1, D_MODEL), lambda j: (0, j))],
        out_specs=pl.BlockSpec((bsz, D_MODEL), lambda j: (0, j)),
        out_shape=jax.ShapeDtypeStruct((bsz, n_mod * D_MODEL), F32),
        name="adaln",
    )(c, w_ada, b_ada.reshape(1, -1))


def _inproj_kernel(x_ref, sh_ref, sc_ref, g_ref, wr_ref, wx_ref, wfh_ref, wfl_ref, bf_ref, qkg_ref,
                   pr_ref, px_ref, kb_ref, qb_ref, carry_ref):
    @pl.when(pl.program_id(1) == 0)
    def _():
        carry_ref[...] = jnp.zeros_like(carry_ref)

    x = x_ref[...]
    tm = x.shape[0]
    h = x * lax.rsqrt(jnp.mean(x * x, axis=-1, keepdims=True) + NORM_EPS) * g_ref[...]
    h = h * (1.0 + sc_ref[...]) + sh_ref[...]
    hb = h.astype(BF16)
    h_lo = (h - hb.astype(F32)).astype(BF16)

    pr_ref[...] = jnp.dot(hb, wr_ref[...], preferred_element_type=F32).astype(BF16)

    px = jnp.dot(hb, wx_ref[...], preferred_element_type=F32)
    qk = px[:, :2 * D_GRP]
    ss = _split_dot(qk * qk, _seg_reduce_mat(2 * D_GRP), SEG_TERMS)
    inv = lax.rsqrt(ss * (1.0 / HEAD_DIM) + NORM_EPS)
    qk = qk * _split_dot(inv, _seg_expand_mat(2 * D_GRP), SEG_TERMS) * qkg_ref[...]
    px_ref[:, :2 * D_GRP] = qk.astype(BF16)
    px_ref[:, 2 * D_GRP:] = px[:, 2 * D_GRP:].astype(BF16)

    z = (jnp.dot(hb, wfh_ref[...], preferred_element_type=F32)
         + jnp.dot(h_lo, wfh_ref[...], preferred_element_type=F32)
         + jnp.dot(hb, wfl_ref[...], preferred_element_type=F32)) + bf_ref[...]
    cum = _split_dot(_log_sigmoid(z), _tri(tm, False), terms=3, left=True) + carry_ref[...]
    carry_ref[...] = cum[tm - 1:tm, :]

    src, dst = _iota((LANES, LANES), 0), _iota((LANES, LANES), 1)
    parts = []
    rem = cum
    for _ in range(3):
        part = rem.astype(BF16)
        rem = rem - part.astype(F32)
        parts.append(part)

    def spread(offset):
        return sum(jnp.dot(part, ((dst == 8 * src + offset + t) & (src < N_HEADS)).astype(BF16),
                           preferred_element_type=F32) for t, part in enumerate(parts))

    slot = _iota((1, LANES), 1) % 8
    kb_ref[...] = (jnp.where((slot >= 3) & (slot < 6), 1.0, 0.0) - spread(0)).astype(BF16)
    qb_ref[...] = (jnp.where(slot < 3, 1.0, 0.0) + spread(3)).astype(BF16)


def _inproj(x, shift, scale, g, w_r, w_x, w_f, b_f, qk_gain, tm):
    w_f_hi = w_f.astype(BF16)
    w_f_lo = (w_f - w_f_hi.astype(F32)).astype(BF16)
    bsz, seq, _ = x.shape
    const = lambda b, s: (0, 0)
    return pl.pallas_call(
        _inproj_kernel,
        grid=(bsz, seq // tm),
        in_specs=[pl.BlockSpec((None, tm, D_MODEL), lambda b, s: (b, s, 0)),
                  pl.BlockSpec((None, 1, D_MODEL), lambda b, s: (b, 0, 0)),
                  pl.BlockSpec((None, 1, D_MODEL), lambda b, s: (b, 0, 0)),
                  pl.BlockSpec((1, D_MODEL), const),
                  pl.BlockSpec((D_MODEL, RWKV_COLS), const),
                  pl.BlockSpec((D_MODEL, FOX_MAIN), const),
                  pl.BlockSpec((D_MODEL, LANES), const),
                  pl.BlockSpec((D_MODEL, LANES), const),
                  pl.BlockSpec((1, LANES), const),
                  pl.BlockSpec((1, 2 * D_GRP), const)],
        out_specs=[pl.BlockSpec((None, tm, RWKV_COLS), lambda b, s: (b, s, 0)),
                   pl.BlockSpec((None, tm, FOX_MAIN), lambda b, s: (b, s, 0)),
                   pl.BlockSpec((None, tm, LANES), lambda b, s: (b, s, 0)),
                   pl.BlockSpec((None, tm, LANES), lambda b, s: (b, s, 0))],
        out_shape=[jax.ShapeDtypeStruct((bsz, seq, RWKV_COLS), BF16),
                   jax.ShapeDtypeStruct((bsz, seq, FOX_MAIN), BF16),
                   jax.ShapeDtypeStruct((bsz, seq, LANES), BF16),
                   jax.ShapeDtypeStruct((bsz, seq, LANES), BF16)],
        scratch_shapes=[pltpu.VMEM((1, LANES), F32)],
        compiler_params=pltpu.CompilerParams(
            dimension_semantics=("parallel", "arbitrary"), vmem_limit_bytes=VMEM_LIMIT),
        name="inproj",
    )(x, shift, scale, g, w_r, w_x, w_f_hi, w_f_lo, b_f, qk_gain)


_NN = (((1,), (0,)), ((), ()))
_NT = (((1,), (1,)), ((), ()))
_TN = (((0,), (0,)), ((), ()))
SCAN_N = HEADS_PER_SCAN * CHUNK
BATCH_PER_STEP = 8
INV_LEVELS = 5
M_HEAD, M_STRICT, M_INCL, M_EYE, M_BASE, M_OFF = 0, 1, 2, 3, 4, 5


def _bdot(a, b, dims):
    return lax.dot_general(a, b, dims, preferred_element_type=F32)


def _scan_masks():
    rr, cc = _iota((SCAN_N, SCAN_W), 0), _iota((SCAN_N, SCAN_W), 1)
    ri, ci = _iota((SCAN_N, SCAN_N), 0), _iota((SCAN_N, SCAN_N), 1)
    same = ri // CHUNK == ci // CHUNK
    masks = [rr // CHUNK == cc // HEAD_DIM, same & (ri > ci), same & (ri >= ci), ri == ci,
             (ri // 2 == ci // 2) & (ri > ci)]
    blk = 2
    while blk < CHUNK:
        masks.append((ri // (2 * blk) == ci // (2 * blk)) & (ri // blk != ci // blk) & (ri > ci))
        blk *= 2
    return jnp.stack(masks).astype(BF16)


def _rwkv_kernel(p_ref, masks_ref, mu_ref, w0_ref, w2_ref, a0_ref, a2_ref, g2_ref, kk_ref, ka_ref,
                 rk_ref, gnw_ref, gnb_ref, o_ref, last_ref, state_ref):
    @pl.when(pl.program_id(1) == 0)
    def _():
        last_ref[...] = jnp.zeros_like(last_ref)
        state_ref[...] = jnp.zeros_like(state_ref)

    mu, w0, w2, a0, a2, g2, k_k, k_a, r_k, gn_w, gn_b = (
        ref[...] for ref in (mu_ref, w0_ref, w2_ref, a0_ref, a2_ref, g2_ref, kk_ref, ka_ref,
                             rk_ref, gnw_ref, gnb_ref))
    rows = BATCH_PER_STEP * CHUNK
    p = p_ref[...].astype(F32).reshape(rows, RWKV_COLS)
    row_id = _iota((rows, 1), 0)
    prev = pltpu.roll(p, 1, axis=0)
    for bb in range(BATCH_PER_STEP):
        prev = jnp.where(row_id == bb * CHUNK, last_ref[bb], prev)
        last_ref[bb] = p[(bb + 1) * CHUNK - 1:(bb + 1) * CHUNK, :]
    pf = p + mu * (prev - p)
    r = pf[:, 0:D_GRP]
    k = pf[:, D_GRP:2 * D_GRP]
    v = pf[:, 2 * D_GRP:3 * D_GRP]
    lora = pf[:, LORA_OFF:GATE_OFF]
    gd = pf[:, GATE_OFF:RWKV_COLS]

    wlog = w0 + _dot(jnp.tanh(lora), w2)
    neg = -wlog
    softplus = jnp.maximum(neg, 0.0) + jnp.log(1.0 + jnp.exp(-jnp.abs(neg)))
    ld = -jnp.exp(-softplus - 0.5)
    a = _sigmoid(a0 + _dot(lora, a2))
    g = _dot(_sigmoid(gd), g2)

    red, exp_m = _seg_reduce_mat(D_GRP), _seg_expand_mat(D_GRP)
    kk = k * k_k
    n2 = _split_dot(kk * kk, red, SEG_TERMS)
    kk = kk * _split_dot(1.0 / jnp.maximum(jnp.sqrt(n2), 1e-12), exp_m, SEG_TERMS)
    k2 = k * (1.0 + (a - 1.0) * k_a)

    tr, tc = _iota((rows, rows), 0), _iota((rows, rows), 1)
    tri = ((tr >= tc) & (tr // CHUNK == tc // CHUNK)).astype(BF16)
    cl = _split_dot(ld, tri, terms=CUM_TERMS, left=True)
    cl_end = jnp.concatenate(
        [jnp.broadcast_to(cl[(bb + 1) * CHUNK - 1:(bb + 1) * CHUNK, :], (CHUNK, D_GRP))
         for bb in range(BATCH_PER_STEP)], axis=0)
    e_in = jnp.exp(cl)
    e_out = jnp.exp(-cl)
    e_rem = jnp.exp(cl_end - cl)
    p_end = jnp.exp(cl_end)
    kka = kk * a
    ops = [(-kk * jnp.exp(cl - ld)).astype(BF16), (kka * e_out).astype(BF16),
           (k2 * e_out).astype(BF16), (r * e_in).astype(BF16), v.astype(BF16),
           (kka * e_rem).astype(BF16), (k2 * e_rem).astype(BF16)]

    chains = [(bb, grp) for bb in range(BATCH_PER_STEP)
              for grp in range(N_HEADS // HEADS_PER_SCAN)]
    head_mask = masks_ref[M_HEAD]
    strict, incl = masks_ref[M_STRICT], masks_ref[M_INCL]

    def stacked(op, bb, grp):
        part = op[bb * CHUNK:(bb + 1) * CHUNK, grp * SCAN_W:(grp + 1) * SCAN_W]
        return jnp.concatenate([part] * HEADS_PER_SCAN, axis=0) * head_mask

    xs = [[stacked(op, bb, grp) for op in ops] for bb, grp in chains]
    st = [state_ref[bb, grp] for bb, grp in chains]
    sb = [s.astype(BF16) for s in st]
    nab = [_bdot(x[0], x[1], _NT).astype(BF16) for x in xs]
    aak = [_bdot(x[0], x[2], _NT).astype(BF16) * strict for x in xs]
    arb = [_bdot(x[3], x[1], _NT).astype(BF16) * incl for x in xs]
    ark = [_bdot(x[3], x[2], _NT).astype(BF16) * incl for x in xs]
    t_inv = [masks_ref[M_EYE] + n * masks_ref[M_BASE] for n in nab]
    for lvl in range(INV_LEVELS):
        half = [_bdot(t, n * masks_ref[M_OFF + lvl], _NN).astype(BF16) for t, n in zip(t_inv, nab)]
        t_inv = [t + _bdot(h, t, _NN).astype(BF16) for t, h in zip(t_inv, half)]
    rhs = [(_bdot(x[0], s, _NT) + _bdot(k, x[4], _NN)).astype(BF16)
           for x, s, k in zip(xs, sb, aak)]
    sa = [_bdot(t, h, _NN).astype(BF16) for t, h in zip(t_inv, rhs)]
    ys = [_bdot(x[3], s, _NT) + _bdot(b, u, _NN) + _bdot(k, x[4], _NN)
          for x, s, b, u, k in zip(xs, sb, arb, sa, ark)]
    for (bb, grp), x, s, u in zip(chains, xs, st, sa):
        decay = p_end[bb * CHUNK:bb * CHUNK + 1, grp * SCAN_W:(grp + 1) * SCAN_W]
        state_ref[bb, grp] = s * decay + _bdot(u, x[5], _TN) + _bdot(x[4], x[6], _TN)
    ys = [y[0:CHUNK] + y[CHUNK:2 * CHUNK] + y[2 * CHUNK:3 * CHUNK] + y[3 * CHUNK:4 * CHUNK]
          for y in ys]
    n_grp = N_HEADS // HEADS_PER_SCAN
    y = jnp.concatenate([jnp.concatenate(ys[bb * n_grp:(bb + 1) * n_grp], axis=1)
                         for bb in range(BATCH_PER_STEP)], axis=0)

    mean = _split_dot(_split_dot(y, red, SEG_TERMS) * (1.0 / HEAD_DIM), exp_m, SEG_TERMS)
    d = y - mean
    var = _split_dot(d * d, red, SEG_TERMS) * (1.0 / HEAD_DIM)
    yn = d * _split_dot(lax.rsqrt(var + GN_EPS), exp_m, SEG_TERMS) * gn_w + gn_b
    bonus = _split_dot(_split_dot(r * k2 * r_k, red, SEG_TERMS), exp_m, SEG_TERMS) * v
    o_ref[...] = ((yn + bonus) * g).astype(BF16).reshape(BATCH_PER_STEP, CHUNK, D_GRP)


def _rwkv(p_r, mu, w0, w2p, a0, a2p, g2, k_k, k_a, r_k, gn_w, gn_b):
    bsz, seq, _ = p_r.shape
    assert bsz % BATCH_PER_STEP == 0
    masks = _scan_masks()
    const = lambda b, s: (0, 0)
    vec = pl.BlockSpec((1, D_GRP), const)
    return pl.pallas_call(
        _rwkv_kernel,
        grid=(bsz // BATCH_PER_STEP, seq // CHUNK),
        in_specs=[pl.BlockSpec((BATCH_PER_STEP, CHUNK, RWKV_COLS), lambda b, s: (b, s, 0)),
                  pl.BlockSpec(masks.shape, lambda b, s: (0, 0, 0)),
                  pl.BlockSpec((1, RWKV_COLS), const),
                  vec, pl.BlockSpec((LANES, D_GRP), const),
                  vec, pl.BlockSpec((LANES, D_GRP), const),
                  pl.BlockSpec((LANES, D_GRP), const),
                  vec, vec, vec, vec, vec],
        out_specs=pl.BlockSpec((BATCH_PER_STEP, CHUNK, D_GRP), lambda b, s: (b, s, 0)),
        out_shape=jax.ShapeDtypeStruct((bsz, seq, D_GRP), BF16),
        scratch_shapes=[pltpu.VMEM((BATCH_PER_STEP, 1, RWKV_COLS), F32),
                        pltpu.VMEM((BATCH_PER_STEP, N_HEADS // HEADS_PER_SCAN, SCAN_W, SCAN_W), F32)],
        compiler_params=pltpu.CompilerParams(
            dimension_semantics=("parallel", "arbitrary"), vmem_limit_bytes=VMEM_LIMIT),
        name="rwkv",
    )(p_r, masks, mu, w0, w2p, a0, a2p, g2, k_k, k_a, r_k, gn_w, gn_b)


def _fox_kernel(q_ref, qb_ref, k_ref, kb_ref, vt_ref, og_ref, ong_ref, o_ref, m_ref, l_ref, acc_ref,
                *, seq):
    hp = pl.program_id(1)
    lane = _iota((1, LANES), 1)
    q = q_ref[...]
    qb = qb_ref[...]
    zero = jnp.zeros_like(q)
    qcat = [jnp.concatenate([jnp.where(lane // HEAD_DIM == hh, q, zero),
                             jnp.where(lane // 8 == hp * 2 + hh, qb, zero)], axis=1)
            for hh in range(2)]
    keys = min(FOX_SUB_KEYS, seq)
    n_sub = seq // keys
    diag = _iota((keys, keys), 1) >= _iota((keys, keys), 0)

    m_ref[...] = jnp.full(m_ref.shape, -jnp.inf, F32)
    l_ref[...] = jnp.zeros(l_ref.shape, F32)
    acc_ref[...] = jnp.zeros(acc_ref.shape, F32)

    def scores(s):
        lo = s * keys
        kcat = jnp.concatenate([k_ref[lo:lo + keys, :], kb_ref[lo:lo + keys, :]], axis=1)
        return [lax.dot_general(kcat, qc[lo:, :], _NT, preferred_element_type=F32)
                for qc in qcat]

    pending = scores(0)
    for s in range(n_sub):
        lo = s * keys
        nxt = scores(s + 1) if s + 1 < n_sub else None
        vt = vt_ref[:, lo:lo + keys]
        sts = [jnp.concatenate([jnp.where(diag, st[:, :keys], -jnp.inf), st[:, keys:]], axis=1)
               if st.shape[1] > keys else jnp.where(diag, st, -jnp.inf) for st in pending]
        m_old = [m_ref[hh, :, lo:] for hh in range(2)]
        m_new = [jnp.maximum(m, jnp.max(st, axis=0, keepdims=True)) for m, st in zip(m_old, sts)]
        pts = [jnp.exp(st - m) for st, m in zip(sts, m_new)]
        pvs = [jnp.dot(vt, pt.astype(BF16), preferred_element_type=F32) for pt in pts]
        for hh in range(2):
            alpha = jnp.exp(m_old[hh] - m_new[hh])
            m_ref[hh, :, lo:] = m_new[hh]
            l_ref[hh, :, lo:] = alpha * l_ref[hh, :, lo:] + jnp.sum(pts[hh], axis=0, keepdims=True)
            acc_ref[hh, :, lo:] = (alpha * acc_ref[hh, :, lo:]
                                   + pvs[hh][hh * HEAD_DIM:(hh + 1) * HEAD_DIM, :])
        pending = nxt

    outs = []
    for hh in range(2):
        o = acc_ref[hh] / l_ref[hh]
        outs.append(o * lax.rsqrt(jnp.mean(o * o, axis=0, keepdims=True) + NORM_EPS))
    o = jnp.concatenate(outs, axis=0).T
    o_ref[...] = (o * ong_ref[...] * _sigmoid(og_ref[...].astype(F32))).astype(BF16)


def _fox(p_x, k_bias, q_bias, o_gain):
    bsz, seq, _ = p_x.shape
    npair = N_HEADS // 2
    v_t = jnp.transpose(p_x[:, :, 2 * D_GRP:3 * D_GRP], (0, 2, 1))
    return pl.pallas_call(
        functools.partial(_fox_kernel, seq=seq),
        grid=(bsz, npair),
        in_specs=[pl.BlockSpec((None, seq, LANES), lambda b, h: (b, 0, h)),
                  pl.BlockSpec((None, seq, LANES), lambda b, h: (b, 0, 0)),
                  pl.BlockSpec((None, seq, LANES), lambda b, h: (b, 0, npair + h)),
                  pl.BlockSpec((None, seq, LANES), lambda b, h: (b, 0, 0)),
                  pl.BlockSpec((None, LANES, seq), lambda b, h: (b, h, 0)),
                  pl.BlockSpec((None, seq, LANES), lambda b, h: (b, 0, 3 * npair + h)),
                  pl.BlockSpec((1, LANES), lambda b, h: (0, 0))],
        out_specs=pl.BlockSpec((None, seq, LANES), lambda b, h: (b, 0, h)),
        out_shape=jax.ShapeDtypeStruct((bsz, seq, D_GRP), BF16),
        scratch_shapes=[pltpu.VMEM((2, 1, seq), F32), pltpu.VMEM((2, 1, seq), F32),
                        pltpu.VMEM((2, HEAD_DIM, seq), F32)],
        compiler_params=pltpu.CompilerParams(
            dimension_semantics=("parallel", "parallel"), vmem_limit_bytes=VMEM_LIMIT),
        name="fox",
    )(p_x, q_bias, p_x, k_bias, v_t, p_x, o_gain)


def _outproj_kernel(x_ref, yr_ref, yf_ref, g1_ref, sh_ref, sc_ref, ng_ref, wor_ref, wof_ref,
                    wrt_ref, wrl_ref, brt_ref, x1_ref, h2_ref, idx_ref, gate_ref, rank_ref, cnt_ref,
                    carry_ref):
    @pl.when(pl.program_id(0) == 0)
    def _():
        carry_ref[...] = jnp.zeros_like(carry_ref)

    y = (jnp.dot(yr_ref[...], wor_ref[...], preferred_element_type=F32)
         + jnp.dot(yf_ref[...], wof_ref[...], preferred_element_type=F32))
    x1 = x_ref[...] + g1_ref[...] * y
    x1_ref[...] = x1
    tm = x1.shape[0]
    h = x1 * lax.rsqrt(jnp.mean(x1 * x1, axis=-1, keepdims=True) + NORM_EPS) * ng_ref[...]
    h2 = h * (1.0 + sc_ref[...]) + sh_ref[...]
    h2_ref[...] = _pack_rows(h2)

    lane = _iota((tm, LANES), 1)
    h_hi = h2.astype(BF16)
    h_lo = (h2 - h_hi.astype(F32)).astype(BF16)
    logits = (jnp.dot(h_hi, wrt_ref[...], preferred_element_type=F32)
              + jnp.dot(h_lo, wrt_ref[...], preferred_element_type=F32)
              + jnp.dot(h_hi, wrl_ref[...], preferred_element_type=F32)) + brt_ref[...]
    lg = jnp.where(lane < N_EXPERTS, logits, -jnp.inf)
    picks = []
    hot_sum = jnp.zeros((tm, LANES), F32)
    for _ in range(TOP_K):
        m = jnp.max(lg, axis=-1, keepdims=True)
        sel = jnp.min(jnp.where(lg == m, lane, LANES), axis=-1, keepdims=True)
        hot = lane == sel
        picks.append((m, sel, hot))
        hot_sum = hot_sum + hot.astype(F32)
        lg = jnp.where(hot, -jnp.inf, lg)
    es = [jnp.exp(m - picks[0][0]) for m, _, _ in picks]
    den = es[0] + es[1] + es[2] + es[3]

    before = jnp.dot(_tri(tm, True), hot_sum.astype(BF16), preferred_element_type=F32)
    before = before + carry_ref[...]
    idx_out = jnp.zeros((tm, LANES), jnp.int32)
    gate_out = jnp.zeros((tm, LANES), F32)
    rank_out = jnp.zeros((tm, LANES), jnp.int32)
    for kk, (m, sel, hot) in enumerate(picks):
        rk = jnp.sum(jnp.where(hot, before, 0.0), axis=-1, keepdims=True).astype(jnp.int32)
        idx_out = jnp.where(lane == kk, sel, idx_out)
        gate_out = jnp.where(lane == kk, es[kk] / den, gate_out)
        rank_out = jnp.where(lane == kk, rk, rank_out)
    idx_ref[...] = idx_out
    gate_ref[...] = gate_out
    rank_ref[...] = rank_out
    carry_ref[...] = carry_ref[...] + jnp.sum(hot_sum, axis=0, keepdims=True)
    cnt_ref[...] = carry_ref[...]


def _outproj(x2d, y_r, y_f, gate1, shift2, scale2, norm_g, wo_r, wo_f, w_rt, b_rt, tm, seq,
             row0, t):
    w_rt_hi = w_rt.astype(BF16)
    w_rt_lo = (w_rt - w_rt_hi.astype(F32)).astype(BF16)
    per_b = seq // tm
    blk0 = row0 // tm
    const = lambda i: (0, 0)
    rows = lambda i: (i, 0)
    rows_in = lambda i: (i + blk0, 0)
    mod = pl.BlockSpec((None, 1, D_MODEL), lambda i: ((i + blk0) // per_b, 0, 0))
    return pl.pallas_call(
        _outproj_kernel,
        grid=(t // tm,),
        in_specs=[pl.BlockSpec((tm, D_MODEL), rows_in),
                  pl.BlockSpec((tm, D_GRP), rows_in),
                  pl.BlockSpec((tm, D_GRP), rows_in),
                  mod, mod, mod,
                  pl.BlockSpec((1, D_MODEL), const),
                  pl.BlockSpec((D_GRP, D_MODEL), const),
                  pl.BlockSpec((D_GRP, D_MODEL), const),
                  pl.BlockSpec((D_MODEL, LANES), const),
                  pl.BlockSpec((D_MODEL, LANES), const),
                  pl.BlockSpec((1, LANES), const)],
        out_specs=[pl.BlockSpec((tm, D_MODEL), rows),
                   pl.BlockSpec((tm, D_PACK), rows),
                   pl.BlockSpec((tm, LANES), rows),
                   pl.BlockSpec((tm, LANES), rows),
                   pl.BlockSpec((tm, LANES), rows),
                   pl.BlockSpec((1, LANES), const)],
        out_shape=[jax.ShapeDtypeStruct((t, D_MODEL), F32),
                   jax.ShapeDtypeStruct((t, D_PACK), jnp.uint32),
                   jax.ShapeDtypeStruct((t, LANES), jnp.int32),
                   jax.ShapeDtypeStruct((t, LANES), F32),
                   jax.ShapeDtypeStruct((t, LANES), jnp.int32),
                   jax.ShapeDtypeStruct((1, LANES), F32)],
        scratch_shapes=[pltpu.VMEM((1, LANES), F32)],
        compiler_params=pltpu.CompilerParams(
            dimension_semantics=("arbitrary",), vmem_limit_bytes=VMEM_LIMIT),
        name="outproj",
    )(x2d, y_r, y_f, gate1, shift2, scale2, norm_g, wo_r, wo_f, w_rt_hi, w_rt_lo, b_rt)


SC_CORES = 2
SC_SUBCORES = 16
SC_ROWS = 32


def _sc_gather_rows(idx, src):
    n_workers = SC_CORES * SC_SUBCORES
    m = idx.shape[0]
    d = src.shape[1]
    assert m % (n_workers * SC_ROWS) == 0
    n_chunks = m // (n_workers * SC_ROWS)
    mesh = plsc.VectorSubcoreMesh(core_axis_name="c", subcore_axis_name="s")

    @functools.partial(
        pl.kernel, mesh=mesh,
        out_type=jax.ShapeDtypeStruct((m, d), src.dtype),
        scratch_types=[pltpu.VMEM((n_chunks, SC_ROWS), jnp.int32),
                       pltpu.VMEM((SC_ROWS, d), src.dtype),
                       pltpu.SemaphoreType.DMA],
        name="sc_gather")
    def gather(src_hbm, idx_hbm, out_hbm, idx_v, rows_v, sem):
        wid = lax.axis_index("s") * SC_CORES + lax.axis_index("c")
        pltpu.sync_copy(idx_hbm.at[wid], idx_v)

        @pl.loop(0, n_chunks)
        def _(j):
            pltpu.async_copy(src_hbm.at[idx_v.at[j]], rows_v, sem).wait()
            pltpu.sync_copy(rows_v, out_hbm.at[pl.ds((wid * n_chunks + j) * SC_ROWS, SC_ROWS)])

    return gather(src, idx.reshape(n_workers, n_chunks, SC_ROWS))


def _sc_scatter_rows(src, dest, n_out):
    n_workers = SC_CORES * SC_SUBCORES
    t, d = src.shape
    n_slot = dest.shape[1]
    assert t % (n_workers * SC_ROWS) == 0
    n_chunks = t // (n_workers * SC_ROWS)
    mesh = plsc.VectorSubcoreMesh(core_axis_name="c", subcore_axis_name="s")
    idx = dest.reshape(n_workers, n_chunks, SC_ROWS, n_slot).transpose(0, 1, 3, 2)
    idx = idx.reshape(n_workers, n_chunks * n_slot, SC_ROWS)

    @functools.partial(
        pl.kernel, mesh=mesh,
        out_type=jax.ShapeDtypeStruct((n_out, d), src.dtype),
        scratch_types=[pltpu.VMEM((n_chunks * n_slot, SC_ROWS), jnp.int32),
                       pltpu.VMEM((SC_ROWS, d), src.dtype)],
        name="sc_scatter")
    def scatter(src_hbm, idx_hbm, out_hbm, idx_v, rows_v):
        wid = lax.axis_index("s") * SC_CORES + lax.axis_index("c")
        pltpu.sync_copy(idx_hbm.at[wid], idx_v)

        @pl.loop(0, n_chunks)
        def _(j):
            pltpu.sync_copy(src_hbm.at[pl.ds((wid * n_chunks + j) * SC_ROWS, SC_ROWS)], rows_v)
            for k in range(n_slot):
                pltpu.sync_copy(rows_v, out_hbm.at[idx_v.at[j * n_slot + k]])

    return scatter(src, idx)


def _expert_kernel(be_ref, nv_ref, x_ref, wgu_ref, bgu_ref, wd_ref, bd_ref, o_ref, wgu_bf, wd_bf):
    j = pl.program_id(0)

    @pl.when((j == 0) | (be_ref[j] != be_ref[jnp.maximum(j - 1, 0)]))
    def _():
        for c in range(0, 2 * D_MODEL, CAST_COLS):
            wgu_bf[:, c:c + CAST_COLS] = wgu_ref[:, c:c + CAST_COLS].astype(BF16)
        for c in range(0, D_MODEL, CAST_COLS):
            wd_bf[:, c:c + CAST_COLS] = wd_ref[:, c:c + CAST_COLS].astype(BF16)

    valid = _iota((EXPERT_BLOCK, 1), 0) < nv_ref[j]
    lo, hi = _unpack_rows(jnp.where(valid, x_ref[...], jnp.uint32(0)))
    x = jnp.concatenate([lo.astype(BF16), hi.astype(BF16)], axis=1)
    gu = jnp.dot(x, wgu_bf[...], preferred_element_type=F32) + bgu_ref[...]
    gate = jnp.minimum(gu[:, :D_MODEL], SWIGLU_LIMIT)
    up = jnp.clip(gu[:, D_MODEL:], -SWIGLU_LIMIT, SWIGLU_LIMIT)
    act = gate * _sigmoid(SWIGLU_ALPHA * gate) * (up + 1.0)
    o_ref[...] = _pack_rows(
        jnp.dot(act.astype(BF16), wd_bf[...], preferred_element_type=F32) + bd_ref[...])


def _experts(block_e, n_valid, xs, w_gu, b_gu, w_d, b_d):
    n_blocks = block_e.shape[0]
    grid_spec = pltpu.PrefetchScalarGridSpec(
        num_scalar_prefetch=2,
        grid=(n_blocks,),
        in_specs=[pl.BlockSpec((EXPERT_BLOCK, D_PACK), lambda j, be, nv: (j, 0)),
                  pl.BlockSpec((None, D_MODEL, 2 * D_MODEL), lambda j, be, nv: (be[j], 0, 0)),
                  pl.BlockSpec((None, 1, 2 * D_MODEL), lambda j, be, nv: (be[j], 0, 0)),
                  pl.BlockSpec((None, D_MODEL, D_MODEL), lambda j, be, nv: (be[j], 0, 0)),
                  pl.BlockSpec((None, 1, D_MODEL), lambda j, be, nv: (be[j], 0, 0))],
        out_specs=pl.BlockSpec((EXPERT_BLOCK, D_PACK), lambda j, be, nv: (j, 0)),
        scratch_shapes=[pltpu.VMEM((D_MODEL, 2 * D_MODEL), BF16),
                        pltpu.VMEM((D_MODEL, D_MODEL), BF16)],
    )
    return pl.pallas_call(
        _expert_kernel,
        grid_spec=grid_spec,
        out_shape=jax.ShapeDtypeStruct(xs.shape, jnp.uint32),
        compiler_params=pltpu.CompilerParams(
            dimension_semantics=("arbitrary",), vmem_limit_bytes=VMEM_LIMIT),
        name="experts",
    )(block_e, n_valid, xs, w_gu, b_gu, w_d, b_d)


COMBINE_TOKENS = 512
MOE_SPLITS = 2


def _combine_kernel(yg_ref, x1_ref, gate_ref, g2_ref, fg_ref, o_ref):
    gates = gate_ref[...]
    acc_lo = acc_hi = None
    for kk in range(TOP_K):
        lo, hi = _unpack_rows(yg_ref[kk * COMBINE_TOKENS:(kk + 1) * COMBINE_TOKENS, :])
        g = gates[:, kk:kk + 1]
        acc_lo = g * lo if acc_lo is None else acc_lo + g * lo
        acc_hi = g * hi if acc_hi is None else acc_hi + g * hi
    x2 = x1_ref[...] + g2_ref[...] * jnp.concatenate([acc_lo, acc_hi], axis=1)
    o_ref[...] = x2 * lax.rsqrt(jnp.mean(x2 * x2, axis=-1, keepdims=True) + NORM_EPS) * fg_ref[...]


def _combine_kernel_into(prev_ref, *refs):
    del prev_ref
    _combine_kernel(*refs)


def _combine(yg, x1, gates, gate2, final_g, seq, row0, t_total, prev):
    t = x1.shape[0]
    tm = COMBINE_TOKENS
    per_b = seq // tm
    blk0 = row0 // tm
    rows = lambda i: (i, 0)
    in_specs = [pl.BlockSpec((TOP_K * tm, D_PACK), rows),
                pl.BlockSpec((tm, D_MODEL), rows),
                pl.BlockSpec((tm, LANES), rows),
                pl.BlockSpec((None, 1, D_MODEL), lambda i: ((i + blk0) // per_b, 0, 0)),
                pl.BlockSpec((1, D_MODEL), lambda i: (0, 0))]
    args = (yg, x1, gates, gate2, final_g)
    if prev is not None:
        in_specs = [pl.BlockSpec(memory_space=pl.ANY)] + in_specs
        args = (prev,) + args
    return pl.pallas_call(
        _combine_kernel if prev is None else _combine_kernel_into,
        grid=(t // tm,),
        in_specs=in_specs,
        out_specs=pl.BlockSpec((tm, D_MODEL), lambda i: (i + blk0, 0)),
        out_shape=jax.ShapeDtypeStruct((t_total, D_MODEL), F32),
        input_output_aliases={} if prev is None else {0: 0},
        compiler_params=pltpu.CompilerParams(
            dimension_semantics=("parallel",), vmem_limit_bytes=VMEM_LIMIT),
        name="combine",
    )(*args)


def _moe(h2, idx, gates, rank, counts, x1, gate2, final_g, w_gu, b_gu, w_d, b_d, seq,
         row0, t_total, prev):
    t = h2.shape[0]
    n_slots = t * TOP_K
    n_blocks = -(-n_slots // EXPERT_BLOCK) + N_EXPERTS
    cap = n_blocks * EXPERT_BLOCK
    padded = (counts + EXPERT_BLOCK - 1) // EXPERT_BLOCK * EXPERT_BLOCK
    pad_ends = jnp.cumsum(padded)
    pad_starts = pad_ends - padded
    dest = pad_starts[idx] + rank
    block_starts = jnp.arange(n_blocks, dtype=jnp.int32) * EXPERT_BLOCK
    block_e = jnp.minimum(jnp.sum(block_starts[:, None] >= pad_ends[None, :], axis=1),
                          N_EXPERTS - 1).astype(jnp.int32)
    n_valid = jnp.clip(counts[block_e] - (block_starts - pad_starts[block_e]), 0, EXPERT_BLOCK)

    xs = _sc_scatter_rows(h2, dest, cap)
    yb = _experts(block_e, n_valid.astype(jnp.int32), xs, w_gu, b_gu, w_d, b_d)
    dest_blocks = dest.reshape(-1, COMBINE_TOKENS, TOP_K).transpose(0, 2, 1).reshape(-1)
    yg = _sc_gather_rows(dest_blocks, yb)
    return _combine(yg, x1, gates, gate2, final_g, seq, row0, t_total, prev)


def _layer(x, c_mod, norm1_g, w_in, mu_shift, w0, w2, a0, a2, g2, k_k, k_a, r_k, gn_w, gn_b, b_f,
           q_norm_g, k_norm_g, o_norm_g, w_out, norm2_g, w_router, b_router, w_gate_up,
           b_gate_up, w_down, b_down, final_g, tm_in, tm_out):
    bsz, seq, _ = x.shape
    shift1, scale1, gate1, shift2, scale2, gate2 = (
        m.reshape(bsz, 1, D_MODEL) for m in jnp.split(c_mod, 6, axis=-1))
    row = lambda v: v.reshape(1, -1)

    w_r = w_in[:, :RWKV_COLS].astype(BF16)
    w_x = w_in[:, RWKV_COLS:RWKV_COLS + FOX_MAIN].astype(BF16)
    w_f = jnp.pad(w_in[:, RWKV_COLS + FOX_MAIN:], ((0, 0), (0, LANES - N_HEADS)))
    b_fp = jnp.pad(b_f, (0, LANES - N_HEADS)).reshape(1, LANES)
    qk_gain = jnp.concatenate([jnp.tile(q_norm_g, N_HEADS) * HEAD_DIM ** -0.5,
                               jnp.tile(k_norm_g, N_HEADS)]).reshape(1, -1)
    p_r, p_x, k_bias, q_bias = _inproj(x, shift1, scale1, row(norm1_g), w_r, w_x, w_f, b_fp,
                                       qk_gain, tm_in)

    zeros = jnp.zeros((LANES - 64, D_GRP), F32)
    w2p = jnp.concatenate([w2, zeros], axis=0).astype(BF16)
    a2p = jnp.concatenate([zeros, a2], axis=0).astype(BF16)
    y_r = _rwkv(p_r, row(mu_shift), row(w0), w2p, row(a0), a2p, g2.astype(BF16), row(k_k),
                row(k_a), row(r_k), row(gn_w), row(gn_b))

    y_f = _fox(p_x, k_bias, q_bias, jnp.tile(o_norm_g, 2).reshape(1, LANES))

    t = bsz * seq
    w_rt = jnp.pad(w_router, ((0, 0), (0, LANES - N_EXPERTS)))
    b_rt = jnp.pad(b_router, (0, LANES - N_EXPERTS)).reshape(1, LANES)
    wo = w_out.astype(BF16)
    w_gu, w_d = w_gate_up, w_down
    b_gu, b_d = b_gate_up.reshape(N_EXPERTS, 1, -1), b_down.reshape(N_EXPERTS, 1, -1)
    t_part = t // MOE_SPLITS
    out = None
    for part in range(MOE_SPLITS):
        row0 = part * t_part
        x1, h2, idx, gates, rank, cnt = _outproj(
            x.reshape(t, D_MODEL), y_r.reshape(t, D_GRP), y_f.reshape(t, D_GRP), gate1, shift2,
            scale2, row(norm2_g), wo[:D_GRP], wo[D_GRP:], w_rt, b_rt, tm_out, seq, row0, t_part)
        counts = cnt[0, :N_EXPERTS].astype(jnp.int32)
        out = _moe(h2, idx[:, :TOP_K], gates, rank[:, :TOP_K], counts, x1, gate2, row(final_g),
                   w_gu, b_gu, w_d, b_d, seq, row0, t, out)
    return out.reshape(bsz, seq, D_MODEL)


def kernel(x, c, w_ada, b_ada, norm1_g, w_in, mu_shift, w0, w2, a0, a2, g2, k_k, k_a, r_k, gn_w,
           gn_b, b_f, q_norm_g, k_norm_g, o_norm_g, w_out, norm2_g, w_router, b_router, w_gate_up,
           b_gate_up, w_down, b_down, final_g):
    assert w_ada.shape[0] == 1, "single-layer block"
    c_mod = _adaln(c, w_ada[0], b_ada[0])
    return _layer(x, c_mod, norm1_g[0], w_in[0], mu_shift[0], w0[0], w2[0], a0[0], a2[0], g2[0],
                  k_k[0], k_a[0], r_k[0], gn_w[0], gn_b[0], b_f[0], q_norm_g[0], k_norm_g[0],
                  o_norm_g[0], w_out[0], norm2_g[0], w_router[0], b_router[0], w_gate_up[0],
                  b_gate_up[0], w_down[0], b_down[0], final_g,
                  tm_in=min(512, x.shape[1]), tm_out=min(1024, x.shape[1]))
```

```python
import functools

import jax
import jax.numpy as jnp
from jax import lax
from jax.experimental import pallas as pl
from jax.experimental.pallas import tpu as pltpu
from jax.experimental.pallas import tpu_sc as plsc

F32 = jnp.float32
BF16 = jnp.bfloat16
HIGHEST = lax.Precision.HIGHEST

D_MODEL = 1024
HEAD_DIM = 64
N_HEADS = 8
D_GRP = N_HEADS * HEAD_DIM
RWKV_COLS = 1792
LORA_OFF = 3 * D_GRP
GATE_OFF = LORA_OFF + 128
FOX_MAIN = 4 * D_GRP
N_EXPERTS = 32
TOP_K = 4
EXPERT_BLOCK = 512
CAST_COLS = 256
SWIGLU_ALPHA = 1.702
SWIGLU_LIMIT = 7.0
NORM_EPS = 1e-6
GN_EPS = 64e-5
LANES = 128
CHUNK = 64
FOX_SUB_KEYS = 512
HEADS_PER_SCAN = 4
SCAN_W = HEADS_PER_SCAN * HEAD_DIM
SEG_TERMS = 1
CUM_TERMS = 2
VMEM_LIMIT = 56 * 1024 * 1024


def _dot(a, b):
    return jnp.dot(a.astype(BF16), b.astype(BF16), preferred_element_type=F32)


def _dot_nt(a, b):
    return lax.dot_general(a.astype(BF16), b.astype(BF16), (((1,), (1,)), ((), ())),
                           preferred_element_type=F32)


def _dot_tn(a, b):
    return lax.dot_general(a.astype(BF16), b.astype(BF16), (((0,), (0,)), ((), ())),
                           preferred_element_type=F32)


def _fdot(a, b):
    return jnp.dot(a, b, precision=HIGHEST, preferred_element_type=F32)


def _split_dot(x, m, terms=2, left=False):
    acc = None
    rem = x
    for _ in range(terms):
        part = rem.astype(BF16)
        rem = rem - part.astype(F32)
        d = (jnp.dot(m, part, preferred_element_type=F32) if left
             else jnp.dot(part, m, preferred_element_type=F32))
        acc = d if acc is None else acc + d
    return acc


def _iota(shape, dim):
    return lax.broadcasted_iota(jnp.int32, shape, dim)


def _seg_reduce_mat(n):
    return (_iota((n, LANES), 0) // HEAD_DIM == _iota((n, LANES), 1)).astype(BF16)


def _seg_expand_mat(n):
    return (_iota((LANES, n), 1) // HEAD_DIM == _iota((LANES, n), 0)).astype(BF16)


def _tri(n, strict):
    r, c = _iota((n, n), 0), _iota((n, n), 1)
    return ((r > c) if strict else (r >= c)).astype(BF16)


D_PACK = D_MODEL // 2


def _pack_rows(x):
    lo = lax.bitcast_convert_type(x[:, :D_PACK].astype(BF16).astype(F32), jnp.uint32)
    hi = lax.bitcast_convert_type(x[:, D_PACK:].astype(BF16).astype(F32), jnp.uint32)
    return hi | (lo >> 16)


def _unpack_rows(p):
    lo = lax.bitcast_convert_type(p << 16, F32)
    hi = lax.bitcast_convert_type(p & jnp.uint32(0xFFFF0000), F32)
    return lo, hi


def _log_sigmoid(z):
    return jnp.minimum(z, 0.0) - jnp.log(1.0 + jnp.exp(-jnp.abs(z)))


def _sigmoid(z):
    return 1.0 / (1.0 + jnp.exp(-z))


def _adaln_kernel(c_ref, w_ref, b_ref, o_ref):
    c = c_ref[...]
    o_ref[...] = _fdot(c * _sigmoid(c), w_ref[...]) + b_ref[...]


def _adaln(c, w_ada, b_ada):
    bsz = c.shape[0]
    n_mod = w_ada.shape[1] // D_MODEL
    return pl.pallas_call(
        _adaln_kernel,
        grid=(n_mod,),
        in_specs=[pl.BlockSpec((bsz, D_MODEL), lambda j: (0, 0)),
                  pl.BlockSpec((D_MODEL, D_MODEL), lambda j: (0, j)),
                  pl.BlockSpec((1, D_MODEL), lambda j: (0, j))],
        out_specs=pl.BlockSpec((bsz, D_MODEL), lambda j: (0, j)),
        out_shape=jax.ShapeDtypeStruct((bsz, n_mod * D_MODEL), F32),
        name="adaln",
    )(c, w_ada, b_ada.reshape(1, -1))


def _inproj_kernel(x_ref, sh_ref, sc_ref, g_ref, wr_ref, wx_ref, wfh_ref, wfl_ref, bf_ref, qkg_ref,
                   pr_ref, px_ref, kb_ref, qb_ref, carry_ref):
    @pl.when(pl.program_id(1) == 0)
    def _():
        carry_ref[...] = jnp.zeros_like(carry_ref)

    x = x_ref[...]
    tm = x.shape[0]
    h = x * lax.rsqrt(jnp.mean(x * x, axis=-1, keepdims=True) + NORM_EPS) * g_ref[...]
    h = h * (1.0 + sc_ref[...]) + sh_ref[...]
    hb = h.astype(BF16)
    h_lo = (h - hb.astype(F32)).astype(BF16)

    pr_ref[...] = jnp.dot(hb, wr_ref[...], preferred_element_type=F32).astype(BF16)

    px = jnp.dot(hb, wx_ref[...], preferred_element_type=F32)
    qk = px[:, :2 * D_GRP]
    ss = _split_dot(qk * qk, _seg_reduce_mat(2 * D_GRP), SEG_TERMS)
    inv = lax.rsqrt(ss * (1.0 / HEAD_DIM) + NORM_EPS)
    qk = qk * _split_dot(inv, _seg_expand_mat(2 * D_GRP), SEG_TERMS) * qkg_ref[...]
    px_ref[:, :2 * D_GRP] = qk.astype(BF16)
    px_ref[:, 2 * D_GRP:] = px[:, 2 * D_GRP:].astype(BF16)

    z = (jnp.dot(hb, wfh_ref[...], preferred_element_type=F32)
         + jnp.dot(h_lo, wfh_ref[...], preferred_element_type=F32)
         + jnp.dot(hb, wfl_ref[...], preferred_element_type=F32)) + bf_ref[...]
    cum = _split_dot(_log_sigmoid(z), _tri(tm, False), terms=3, left=True) + carry_ref[...]
    carry_ref[...] = cum[tm - 1:tm, :]

    src, dst = _iota((LANES, LANES), 0), _iota((LANES, LANES), 1)
    parts = []
    rem = cum
    for _ in range(3):
        part = rem.astype(BF16)
        rem = rem - part.astype(F32)
        parts.append(part)

    def spread(offset):
        return sum(jnp.dot(part, ((dst == 8 * src + offset + t) & (src < N_HEADS)).astype(BF16),
                           preferred_element_type=F32) for t, part in enumerate(parts))

    slot = _iota((1, LANES), 1) % 8
    kb_ref[...] = (jnp.where((slot >= 3) & (slot < 6), 1.0, 0.0) - spread(0)).astype(BF16)
    qb_ref[...] = (jnp.where(slot < 3, 1.0, 0.0) + spread(3)).astype(BF16)


def _inproj(x, shift, scale, g, w_r, w_x, w_f, b_f, qk_gain, tm):
    w_f_hi = w_f.astype(BF16)
    w_f_lo = (w_f - w_f_hi.astype(F32)).astype(BF16)
    bsz, seq, _ = x.shape
    const = lambda b, s: (0, 0)
    return pl.pallas_call(
        _inproj_kernel,
        grid=(bsz, seq // tm),
        in_specs=[pl.BlockSpec((None, tm, D_MODEL), lambda b, s: (b, s, 0)),
                  pl.BlockSpec((None, 1, D_MODEL), lambda b, s: (b, 0, 0)),
                  pl.BlockSpec((None, 1, D_MODEL), lambda b, s: (b, 0, 0)),
                  pl.BlockSpec((1, D_MODEL), const),
                  pl.BlockSpec((D_MODEL, RWKV_COLS), const),
                  pl.BlockSpec((D_MODEL, FOX_MAIN), const),
                  pl.BlockSpec((D_MODEL, LANES), const),
                  pl.BlockSpec((D_MODEL, LANES), const),
                  pl.BlockSpec((1, LANES), const),
                  pl.BlockSpec((1, 2 * D_GRP), const)],
        out_specs=[pl.BlockSpec((None, tm, RWKV_COLS), lambda b, s: (b, s, 0)),
                   pl.BlockSpec((None, tm, FOX_MAIN), lambda b, s: (b, s, 0)),
                   pl.BlockSpec((None, tm, LANES), lambda b, s: (b, s, 0)),
                   pl.BlockSpec((None, tm, LANES), lambda b, s: (b, s, 0))],
        out_shape=[jax.ShapeDtypeStruct((bsz, seq, RWKV_COLS), BF16),
                   jax.ShapeDtypeStruct((bsz, seq, FOX_MAIN), BF16),
                   jax.ShapeDtypeStruct((bsz, seq, LANES), BF16),
                   jax.ShapeDtypeStruct((bsz, seq, LANES), BF16)],
        scratch_shapes=[pltpu.VMEM((1, LANES), F32)],
        compiler_params=pltpu.CompilerParams(
            dimension_semantics=("parallel", "arbitrary"), vmem_limit_bytes=VMEM_LIMIT),
        name="inproj",
    )(x, shift, scale, g, w_r, w_x, w_f_hi, w_f_lo, b_f, qk_gain)


_NN = (((1,), (0,)), ((), ()))
_NT = (((1,), (1,)), ((), ()))
_TN = (((0,), (0,)), ((), ()))
SCAN_N = HEADS_PER_SCAN * CHUNK
BATCH_PER_STEP = 8
INV_LEVELS = 5
M_HEAD, M_STRICT, M_INCL, M_EYE, M_BASE, M_OFF = 0, 1, 2, 3, 4, 5


def _bdot(a, b, dims):
    return lax.dot_general(a, b, dims, preferred_element_type=F32)


def _scan_masks():
    rr, cc = _iota((SCAN_N, SCAN_W), 0), _iota((SCAN_N, SCAN_W), 1)
    ri, ci = _iota((SCAN_N, SCAN_N), 0), _iota((SCAN_N, SCAN_N), 1)
    same = ri // CHUNK == ci // CHUNK
    masks = [rr // CHUNK == cc // HEAD_DIM, same & (ri > ci), same & (ri >= ci), ri == ci,
             (ri // 2 == ci // 2) & (ri > ci)]
    blk = 2
    while blk < CHUNK:
        masks.append((ri // (2 * blk) == ci // (2 * blk)) & (ri // blk != ci // blk) & (ri > ci))
        blk *= 2
    return jnp.stack(masks).astype(BF16)


def _rwkv_kernel(p_ref, masks_ref, mu_ref, w0_ref, w2_ref, a0_ref, a2_ref, g2_ref, kk_ref, ka_ref,
                 rk_ref, gnw_ref, gnb_ref, o_ref, last_ref, state_ref):
    @pl.when(pl.program_id(1) == 0)
    def _():
        last_ref[...] = jnp.zeros_like(last_ref)
        state_ref[...] = jnp.zeros_like(state_ref)

    mu, w0, w2, a0, a2, g2, k_k, k_a, r_k, gn_w, gn_b = (
        ref[...] for ref in (mu_ref, w0_ref, w2_ref, a0_ref, a2_ref, g2_ref, kk_ref, ka_ref,
                             rk_ref, gnw_ref, gnb_ref))
    rows = BATCH_PER_STEP * CHUNK
    p = p_ref[...].astype(F32).reshape(rows, RWKV_COLS)
    row_id = _iota((rows, 1), 0)
    prev = pltpu.roll(p, 1, axis=0)
    for bb in range(BATCH_PER_STEP):
        prev = jnp.where(row_id == bb * CHUNK, last_ref[bb], prev)
        last_ref[bb] = p[(bb + 1) * CHUNK - 1:(bb + 1) * CHUNK, :]
    pf = p + mu * (prev - p)
    r = pf[:, 0:D_GRP]
    k = pf[:, D_GRP:2 * D_GRP]
    v = pf[:, 2 * D_GRP:3 * D_GRP]
    lora = pf[:, LORA_OFF:GATE_OFF]
    gd = pf[:, GATE_OFF:RWKV_COLS]

    wlog = w0 + _dot(jnp.tanh(lora), w2)
    neg = -wlog
    softplus = jnp.maximum(neg, 0.0) + jnp.log(1.0 + jnp.exp(-jnp.abs(neg)))
    ld = -jnp.exp(-softplus - 0.5)
    a = _sigmoid(a0 + _dot(lora, a2))
    g = _dot(_sigmoid(gd), g2)

    red, exp_m = _seg_reduce_mat(D_GRP), _seg_expand_mat(D_GRP)
    kk = k * k_k
    n2 = _split_dot(kk * kk, red, SEG_TERMS)
    kk = kk * _split_dot(1.0 / jnp.maximum(jnp.sqrt(n2), 1e-12), exp_m, SEG_TERMS)
    k2 = k * (1.0 + (a - 1.0) * k_a)

    tr, tc = _iota((rows, rows), 0), _iota((rows, rows), 1)
    tri = ((tr >= tc) & (tr // CHUNK == tc // CHUNK)).astype(BF16)
    cl = _split_dot(ld, tri, terms=CUM_TERMS, left=True)
    cl_end = jnp.concatenate(
        [jnp.broadcast_to(cl[(bb + 1) * CHUNK - 1:(bb + 1) * CHUNK, :], (CHUNK, D_GRP))
         for bb in range(BATCH_PER_STEP)], axis=0)
    e_in = jnp.exp(cl)
    e_out = jnp.exp(-cl)
    e_rem = jnp.exp(cl_end - cl)
    p_end = jnp.exp(cl_end)
    kka = kk * a
    ops = [(-kk * jnp.exp(cl - ld)).astype(BF16), (kka * e_out).astype(BF16),
           (k2 * e_out).astype(BF16), (r * e_in).astype(BF16), v.astype(BF16),
           (kka * e_rem).astype(BF16), (k2 * e_rem).astype(BF16)]

    chains = [(bb, grp) for bb in range(BATCH_PER_STEP)
              for grp in range(N_HEADS // HEADS_PER_SCAN)]
    head_mask = masks_ref[M_HEAD]
    strict, incl = masks_ref[M_STRICT], masks_ref[M_INCL]

    def stacked(op, bb, grp):
        part = op[bb * CHUNK:(bb + 1) * CHUNK, grp * SCAN_W:(grp + 1) * SCAN_W]
        return jnp.concatenate([part] * HEADS_PER_SCAN, axis=0) * head_mask

    xs = [[stacked(op, bb, grp) for op in ops] for bb, grp in chains]
    st = [state_ref[bb, grp] for bb, grp in chains]
    sb = [s.astype(BF16) for s in st]
    nab = [_bdot(x[0], x[1], _NT).astype(BF16) for x in xs]
    aak = [_bdot(x[0], x[2], _NT).astype(BF16) * strict for x in xs]
    arb = [_bdot(x[3], x[1], _NT).astype(BF16) * incl for x in xs]
    ark = [_bdot(x[3], x[2], _NT).astype(BF16) * incl for x in xs]
    t_inv = [masks_ref[M_EYE] + n * masks_ref[M_BASE] for n in nab]
    for lvl in range(INV_LEVELS):
        half = [_bdot(t, n * masks_ref[M_OFF + lvl], _NN).astype(BF16) for t, n in zip(t_inv, nab)]
        t_inv = [t + _bdot(h, t, _NN).astype(BF16) for t, h in zip(t_inv, half)]
    rhs = [(_bdot(x[0], s, _NT) + _bdot(k, x[4], _NN)).astype(BF16)
           for x, s, k in zip(xs, sb, aak)]
    sa = [_bdot(t, h, _NN).astype(BF16) for t, h in zip(t_inv, rhs)]
    ys = [_bdot(x[3], s, _NT) + _bdot(b, u, _NN) + _bdot(k, x[4], _NN)
          for x, s, b, u, k in zip(xs, sb, arb, sa, ark)]
    for (bb, grp), x, s, u in zip(chains, xs, st, sa):
        decay = p_end[bb * CHUNK:bb * CHUNK + 1, grp * SCAN_W:(grp + 1) * SCAN_W]
        state_ref[bb, grp] = s * decay + _bdot(u, x[5], _TN) + _bdot(x[4], x[6], _TN)
    ys = [y[0:CHUNK] + y[CHUNK:2 * CHUNK] + y[2 * CHUNK:3 * CHUNK] + y[3 * CHUNK:4 * CHUNK]
          for y in ys]
    n_grp = N_HEADS // HEADS_PER_SCAN
    y = jnp.concatenate([jnp.concatenate(ys[bb * n_grp:(bb + 1) * n_grp], axis=1)
                         for bb in range(BATCH_PER_STEP)], axis=0)

    mean = _split_dot(_split_dot(y, red, SEG_TERMS) * (1.0 / HEAD_DIM), exp_m, SEG_TERMS)
    d = y - mean
    var = _split_dot(d * d, red, SEG_TERMS) * (1.0 / HEAD_DIM)
    yn = d * _split_dot(lax.rsqrt(var + GN_EPS), exp_m, SEG_TERMS) * gn_w + gn_b
    bonus = _split_dot(_split_dot(r * k2 * r_k, red, SEG_TERMS), exp_m, SEG_TERMS) * v
    o_ref[...] = ((yn + bonus) * g).astype(BF16).reshape(BATCH_PER_STEP, CHUNK, D_GRP)


def _rwkv(p_r, mu, w0, w2p, a0, a2p, g2, k_k, k_a, r_k, gn_w, gn_b):
    bsz, seq, _ = p_r.shape
    assert bsz % BATCH_PER_STEP == 0
    masks = _scan_masks()
    const = lambda b, s: (0, 0)
    vec = pl.BlockSpec((1, D_GRP), const)
    return pl.pallas_call(
        _rwkv_kernel,
        grid=(bsz // BATCH_PER_STEP, seq // CHUNK),
        in_specs=[pl.BlockSpec((BATCH_PER_STEP, CHUNK, RWKV_COLS), lambda b, s: (b, s, 0)),
                  pl.BlockSpec(masks.shape, lambda b, s: (0, 0, 0)),
                  pl.BlockSpec((1, RWKV_COLS), const),
                  vec, pl.BlockSpec((LANES, D_GRP), const),
                  vec, pl.BlockSpec((LANES, D_GRP), const),
                  pl.BlockSpec((LANES, D_GRP), const),
                  vec, vec, vec, vec, vec],
        out_specs=pl.BlockSpec((BATCH_PER_STEP, CHUNK, D_GRP), lambda b, s: (b, s, 0)),
        out_shape=jax.ShapeDtypeStruct((bsz, seq, D_GRP), BF16),
        scratch_shapes=[pltpu.VMEM((BATCH_PER_STEP, 1, RWKV_COLS), F32),
                        pltpu.VMEM((BATCH_PER_STEP, N_HEADS // HEADS_PER_SCAN, SCAN_W, SCAN_W), F32)],
        compiler_params=pltpu.CompilerParams(
            dimension_semantics=("parallel", "arbitrary"), vmem_limit_bytes=VMEM_LIMIT),
        name="rwkv",
    )(p_r, masks, mu, w0, w2p, a0, a2p, g2, k_k, k_a, r_k, gn_w, gn_b)


def _fox_kernel(q_ref, qb_ref, k_ref, kb_ref, vt_ref, og_ref, ong_ref, o_ref, m_ref, l_ref, acc_ref,
                *, seq):
    hp = pl.program_id(1)
    lane = _iota((1, LANES), 1)
    q = q_ref[...]
    qb = qb_ref[...]
    zero = jnp.zeros_like(q)
    qcat = [jnp.concatenate([jnp.where(lane // HEAD_DIM == hh, q, zero),
                             jnp.where(lane // 8 == hp * 2 + hh, qb, zero)], axis=1)
            for hh in range(2)]
    keys = min(FOX_SUB_KEYS, seq)
    n_sub = seq // keys
    diag = _iota((keys, keys), 1) >= _iota((keys, keys), 0)

    m_ref[...] = jnp.full(m_ref.shape, -jnp.inf, F32)
    l_ref[...] = jnp.zeros(l_ref.shape, F32)
    acc_ref[...] = jnp.zeros(acc_ref.shape, F32)

    def scores(s):
        lo = s * keys
        kcat = jnp.concatenate([k_ref[lo:lo + keys, :], kb_ref[lo:lo + keys, :]], axis=1)
        return [lax.dot_general(kcat, qc[lo:, :], _NT, preferred_element_type=F32)
                for qc in qcat]

    pending = scores(0)
    for s in range(n_sub):
        lo = s * keys
        nxt = scores(s + 1) if s + 1 < n_sub else None
        vt = vt_ref[:, lo:lo + keys]
        sts = [jnp.concatenate([jnp.where(diag, st[:, :keys], -jnp.inf), st[:, keys:]], axis=1)
               if st.shape[1] > keys else jnp.where(diag, st, -jnp.inf) for st in pending]
        m_old = [m_ref[hh, :, lo:] for hh in range(2)]
        m_new = [jnp.maximum(m, jnp.max(st, axis=0, keepdims=True)) for m, st in zip(m_old, sts)]
        pts = [jnp.exp(st - m) for st, m in zip(sts, m_new)]
        pvs = [jnp.dot(vt, pt.astype(BF16), preferred_element_type=F32) for pt in pts]
        for hh in range(2):
            alpha = jnp.exp(m_old[hh] - m_new[hh])
            m_ref[hh, :, lo:] = m_new[hh]
            l_ref[hh, :, lo:] = alpha * l_ref[hh, :, lo:] + jnp.sum(pts[hh], axis=0, keepdims=True)
            acc_ref[hh, :, lo:] = (alpha * acc_ref[hh, :, lo:]
                                   + pvs[hh][hh * HEAD_DIM:(hh + 1) * HEAD_DIM, :])
        pending = nxt

    outs = []
    for hh in range(2):
        o = acc_ref[hh] / l_ref[hh]
        outs.append(o * lax.rsqrt(jnp.mean(o * o, axis=0, keepdims=True) + NORM_EPS))
    o = jnp.concatenate(outs, axis=0).T
    o_ref[...] = (o * ong_ref[...] * _sigmoid(og_ref[...].astype(F32))).astype(BF16)


def _fox(p_x, k_bias, q_bias, o_gain):
    bsz, seq, _ = p_x.shape
    npair = N_HEADS // 2
    v_t = jnp.transpose(p_x[:, :, 2 * D_GRP:3 * D_GRP], (0, 2, 1))
    return pl.pallas_call(
        functools.partial(_fox_kernel, seq=seq),
        grid=(bsz, npair),
        in_specs=[pl.BlockSpec((None, seq, LANES), lambda b, h: (b, 0, h)),
                  pl.BlockSpec((None, seq, LANES), lambda b, h: (b, 0, 0)),
                  pl.BlockSpec((None, seq, LANES), lambda b, h: (b, 0, npair + h)),
                  pl.BlockSpec((None, seq, LANES), lambda b, h: (b, 0, 0)),
                  pl.BlockSpec((None, LANES, seq), lambda b, h: (b, h, 0)),
                  pl.BlockSpec((None, seq, LANES), lambda b, h: (b, 0, 3 * npair + h)),
                  pl.BlockSpec((1, LANES), lambda b, h: (0, 0))],
        out_specs=pl.BlockSpec((None, seq, LANES), lambda b, h: (b, 0, h)),
        out_shape=jax.ShapeDtypeStruct((bsz, seq, D_GRP), BF16),
        scratch_shapes=[pltpu.VMEM((2, 1, seq), F32), pltpu.VMEM((2, 1, seq), F32),
                        pltpu.VMEM((2, HEAD_DIM, seq), F32)],
        compiler_params=pltpu.CompilerParams(
            dimension_semantics=("parallel", "parallel"), vmem_limit_bytes=VMEM_LIMIT),
        name="fox",
    )(p_x, q_bias, p_x, k_bias, v_t, p_x, o_gain)


def _outproj_kernel(x_ref, yr_ref, yf_ref, g1_ref, sh_ref, sc_ref, ng_ref, wor_ref, wof_ref,
                    wrt_ref, wrl_ref, brt_ref, x1_ref, h2_ref, idx_ref, gate_ref, rank_ref, cnt_ref,
                    carry_ref):
    @pl.when(pl.program_id(0) == 0)
    def _():
        carry_ref[...] = jnp.zeros_like(carry_ref)

    y = (jnp.dot(yr_ref[...], wor_ref[...], preferred_element_type=F32)
         + jnp.dot(yf_ref[...], wof_ref[...], preferred_element_type=F32))
    x1 = x_ref[...] + g1_ref[...] * y
    x1_ref[...] = x1
    tm = x1.shape[0]
    h = x1 * lax.rsqrt(jnp.mean(x1 * x1, axis=-1, keepdims=True) + NORM_EPS) * ng_ref[...]
    h2 = h * (1.0 + sc_ref[...]) + sh_ref[...]
    h2_ref[...] = _pack_rows(h2)

    lane = _iota((tm, LANES), 1)
    h_hi = h2.astype(BF16)
    h_lo = (h2 - h_hi.astype(F32)).astype(BF16)
    logits = (jnp.dot(h_hi, wrt_ref[...], preferred_element_type=F32)
              + jnp.dot(h_lo, wrt_ref[...], preferred_element_type=F32)
              + jnp.dot(h_hi, wrl_ref[...], preferred_element_type=F32)) + brt_ref[...]
    lg = jnp.where(lane < N_EXPERTS, logits, -jnp.inf)
    picks = []
    hot_sum = jnp.zeros((tm, LANES), F32)
    for _ in range(TOP_K):
        m = jnp.max(lg, axis=-1, keepdims=True)
        sel = jnp.min(jnp.where(lg == m, lane, LANES), axis=-1, keepdims=True)
        hot = lane == sel
        picks.append((m, sel, hot))
        hot_sum = hot_sum + hot.astype(F32)
        lg = jnp.where(hot, -jnp.inf, lg)
    es = [jnp.exp(m - picks[0][0]) for m, _, _ in picks]
    den = es[0] + es[1] + es[2] + es[3]

    before = jnp.dot(_tri(tm, True), hot_sum.astype(BF16), preferred_element_type=F32)
    before = before + carry_ref[...]
    idx_out = jnp.zeros((tm, LANES), jnp.int32)
    gate_out = jnp.zeros((tm, LANES), F32)
    rank_out = jnp.zeros((tm, LANES), jnp.int32)
    for kk, (m, sel, hot) in enumerate(picks):
        rk = jnp.sum(jnp.where(hot, before, 0.0), axis=-1, keepdims=True).astype(jnp.int32)
        idx_out = jnp.where(lane == kk, sel, idx_out)
        gate_out = jnp.where(lane == kk, es[kk] / den, gate_out)
        rank_out = jnp.where(lane == kk, rk, rank_out)
    idx_ref[...] = idx_out
    gate_ref[...] = gate_out
    rank_ref[...] = rank_out
    carry_ref[...] = carry_ref[...] + jnp.sum(hot_sum, axis=0, keepdims=True)
    cnt_ref[...] = carry_ref[...]


def _outproj(x2d, y_r, y_f, gate1, shift2, scale2, norm_g, wo_r, wo_f, w_rt, b_rt, tm, seq,
             row0, t):
    w_rt_hi = w_rt.astype(BF16)
    w_rt_lo = (w_rt - w_rt_hi.astype(F32)).astype(BF16)
    per_b = seq // tm
    blk0 = row0 // tm
    const = lambda i: (0, 0)
    rows = lambda i: (i, 0)
    rows_in = lambda i: (i + blk0, 0)
    mod = pl.BlockSpec((None, 1, D_MODEL), lambda i: ((i + blk0) // per_b, 0, 0))
    return pl.pallas_call(
        _outproj_kernel,
        grid=(t // tm,),
        in_specs=[pl.BlockSpec((tm, D_MODEL), rows_in),
                  pl.BlockSpec((tm, D_GRP), rows_in),
                  pl.BlockSpec((tm, D_GRP), rows_in),
                  mod, mod, mod,
                  pl.BlockSpec((1, D_MODEL), const),
                  pl.BlockSpec((D_GRP, D_MODEL), const),
                  pl.BlockSpec((D_GRP, D_MODEL), const),
                  pl.BlockSpec((D_MODEL, LANES), const),
                  pl.BlockSpec((D_MODEL, LANES), const),
                  pl.BlockSpec((1, LANES), const)],
        out_specs=[pl.BlockSpec((tm, D_MODEL), rows),
                   pl.BlockSpec((tm, D_PACK), rows),
                   pl.BlockSpec((tm, LANES), rows),
                   pl.BlockSpec((tm, LANES), rows),
                   pl.BlockSpec((tm, LANES), rows),
                   pl.BlockSpec((1, LANES), const)],
        out_shape=[jax.ShapeDtypeStruct((t, D_MODEL), F32),
                   jax.ShapeDtypeStruct((t, D_PACK), jnp.uint32),
                   jax.ShapeDtypeStruct((t, LANES), jnp.int32),
                   jax.ShapeDtypeStruct((t, LANES), F32),
                   jax.ShapeDtypeStruct((t, LANES), jnp.int32),
                   jax.ShapeDtypeStruct((1, LANES), F32)],
        scratch_shapes=[pltpu.VMEM((1, LANES), F32)],
        compiler_params=pltpu.CompilerParams(
            dimension_semantics=("arbitrary",), vmem_limit_bytes=VMEM_LIMIT),
        name="outproj",
    )(x2d, y_r, y_f, gate1, shift2, scale2, norm_g, wo_r, wo_f, w_rt_hi, w_rt_lo, b_rt)


SC_CORES = 2
SC_SUBCORES = 16
SC_ROWS = 32


def _sc_gather_rows(idx, src):
    n_workers = SC_CORES * SC_SUBCORES
    m = idx.shape[0]
    d = src.shape[1]
    assert m % (n_workers * SC_ROWS) == 0
    n_chunks = m // (n_workers * SC_ROWS)
    mesh = plsc.VectorSubcoreMesh(core_axis_name="c", subcore_axis_name="s")

    @functools.partial(
        pl.kernel, mesh=mesh,
        out_type=jax.ShapeDtypeStruct((m, d), src.dtype),
        scratch_types=[pltpu.VMEM((n_chunks, SC_ROWS), jnp.int32),
                       pltpu.VMEM((SC_ROWS, d), src.dtype),
                       pltpu.SemaphoreType.DMA],
        name="sc_gather")
    def gather(src_hbm, idx_hbm, out_hbm, idx_v, rows_v, sem):
        wid = lax.axis_index("s") * SC_CORES + lax.axis_index("c")
        pltpu.sync_copy(idx_hbm.at[wid], idx_v)

        @pl.loop(0, n_chunks)
        def _(j):
            pltpu.async_copy(src_hbm.at[idx_v.at[j]], rows_v, sem).wait()
            pltpu.sync_copy(rows_v, out_hbm.at[pl.ds((wid * n_chunks + j) * SC_ROWS, SC_ROWS)])

    return gather(src, idx.reshape(n_workers, n_chunks, SC_ROWS))


def _sc_scatter_rows(src, dest, n_out):
    n_workers = SC_CORES * SC_SUBCORES
    t, d = src.shape
    n_slot = dest.shape[1]
    assert t % (n_workers * SC_ROWS) == 0
    n_chunks = t // (n_workers * SC_ROWS)
    mesh = plsc.VectorSubcoreMesh(core_axis_name="c", subcore_axis_name="s")
    idx = dest.reshape(n_workers, n_chunks, SC_ROWS, n_slot).transpose(0, 1, 3, 2)
    idx = idx.reshape(n_workers, n_chunks * n_slot, SC_ROWS)

    @functools.partial(
        pl.kernel, mesh=mesh,
        out_type=jax.ShapeDtypeStruct((n_out, d), src.dtype),
        scratch_types=[pltpu.VMEM((n_chunks * n_slot, SC_ROWS), jnp.int32),
                       pltpu.VMEM((SC_ROWS, d), src.dtype)],
        name="sc_scatter")
    def scatter(src_hbm, idx_hbm, out_hbm, idx_v, rows_v):
        wid = lax.axis_index("s") * SC_CORES + lax.axis_index("c")
        pltpu.sync_copy(idx_hbm.at[wid], idx_v)

        @pl.loop(0, n_chunks)
        def _(j):
            pltpu.sync_copy(src_hbm.at[pl.ds((wid * n_chunks + j) * SC_ROWS, SC_ROWS)], rows_v)
            for k in range(n_slot):
                pltpu.sync_copy(rows_v, out_hbm.at[idx_v.at[j * n_slot + k]])

    return scatter(src, idx)


def _expert_kernel(be_ref, nv_ref, x_ref, wgu_ref, bgu_ref, wd_ref, bd_ref, o_ref, wgu_bf, wd_bf):
    j = pl.program_id(0)

    @pl.when((j == 0) | (be_ref[j] != be_ref[jnp.maximum(j - 1, 0)]))
    def _():
        for c in range(0, 2 * D_MODEL, CAST_COLS):
            wgu_bf[:, c:c + CAST_COLS] = wgu_ref[:, c:c + CAST_COLS].astype(BF16)
        for c in range(0, D_MODEL, CAST_COLS):
            wd_bf[:, c:c + CAST_COLS] = wd_ref[:, c:c + CAST_COLS].astype(BF16)

    valid = _iota((EXPERT_BLOCK, 1), 0) < nv_ref[j]
    lo, hi = _unpack_rows(jnp.where(valid, x_ref[...], jnp.uint32(0)))
    x = jnp.concatenate([lo.astype(BF16), hi.astype(BF16)], axis=1)
    gu = jnp.dot(x, wgu_bf[...], preferred_element_type=F32) + bgu_ref[...]
    gate = jnp.minimum(gu[:, :D_MODEL], SWIGLU_LIMIT)
    up = jnp.clip(gu[:, D_MODEL:], -SWIGLU_LIMIT, SWIGLU_LIMIT)
    act = gate * _sigmoid(SWIGLU_ALPHA * gate) * (up + 1.0)
    o_ref[...] = _pack_rows(
        jnp.dot(act.astype(BF16), wd_bf[...], preferred_element_type=F32) + bd_ref[...])


def _experts(block_e, n_valid, xs, w_gu, b_gu, w_d, b_d):
    n_blocks = block_e.shape[0]
    grid_spec = pltpu.PrefetchScalarGridSpec(
        num_scalar_prefetch=2,
        grid=(n_blocks,),
        in_specs=[pl.BlockSpec((EXPERT_BLOCK, D_PACK), lambda j, be, nv: (j, 0)),
                  pl.BlockSpec((None, D_MODEL, 2 * D_MODEL), lambda j, be, nv: (be[j], 0, 0)),
                  pl.BlockSpec((None, 1, 2 * D_MODEL), lambda j, be, nv: (be[j], 0, 0)),
                  pl.BlockSpec((None, D_MODEL, D_MODEL), lambda j, be, nv: (be[j], 0, 0)),
                  pl.BlockSpec((None, 1, D_MODEL), lambda j, be, nv: (be[j], 0, 0))],
        out_specs=pl.BlockSpec((EXPERT_BLOCK, D_PACK), lambda j, be, nv: (j, 0)),
        scratch_shapes=[pltpu.VMEM((D_MODEL, 2 * D_MODEL), BF16),
                        pltpu.VMEM((D_MODEL, D_MODEL), BF16)],
    )
    return pl.pallas_call(
        _expert_kernel,
        grid_spec=grid_spec,
        out_shape=jax.ShapeDtypeStruct(xs.shape, jnp.uint32),
        compiler_params=pltpu.CompilerParams(
            dimension_semantics=("arbitrary",), vmem_limit_bytes=VMEM_LIMIT),
        name="experts",
    )(block_e, n_valid, xs, w_gu, b_gu, w_d, b_d)


COMBINE_TOKENS = 512
MOE_SPLITS = 2


def _combine_kernel(yg_ref, x1_ref, gate_ref, g2_ref, fg_ref, o_ref):
    gates = gate_ref[...]
    acc_lo = acc_hi = None
    for kk in range(TOP_K):
        lo, hi = _unpack_rows(yg_ref[kk * COMBINE_TOKENS:(kk + 1) * COMBINE_TOKENS, :])
        g = gates[:, kk:kk + 1]
        acc_lo = g * lo if acc_lo is None else acc_lo + g * lo
        acc_hi = g * hi if acc_hi is None else acc_hi + g * hi
    x2 = x1_ref[...] + g2_ref[...] * jnp.concatenate([acc_lo, acc_hi], axis=1)
    o_ref[...] = x2 * lax.rsqrt(jnp.mean(x2 * x2, axis=-1, keepdims=True) + NORM_EPS) * fg_ref[...]


def _combine_kernel_into(prev_ref, *refs):
    del prev_ref
    _combine_kernel(*refs)


def _combine(yg, x1, gates, gate2, final_g, seq, row0, t_total, prev):
    t = x1.shape[0]
    tm = COMBINE_TOKENS
    per_b = seq // tm
    blk0 = row0 // tm
    rows = lambda i: (i, 0)
    in_specs = [pl.BlockSpec((TOP_K * tm, D_PACK), rows),
                pl.BlockSpec((tm, D_MODEL), rows),
                pl.BlockSpec((tm, LANES), rows),
                pl.BlockSpec((None, 1, D_MODEL), lambda i: ((i + blk0) // per_b, 0, 0)),
                pl.BlockSpec((1, D_MODEL), lambda i: (0, 0))]
    args = (yg, x1, gates, gate2, final_g)
    if prev is not None:
        in_specs = [pl.BlockSpec(memory_space=pl.ANY)] + in_specs
        args = (prev,) + args
    return pl.pallas_call(
        _combine_kernel if prev is None else _combine_kernel_into,
        grid=(t // tm,),
        in_specs=in_specs,
        out_specs=pl.BlockSpec((tm, D_MODEL), lambda i: (i + blk0, 0)),
        out_shape=jax.ShapeDtypeStruct((t_total, D_MODEL), F32),
        input_output_aliases={} if prev is None else {0: 0},
        compiler_params=pltpu.CompilerParams(
            dimension_semantics=("parallel",), vmem_limit_bytes=VMEM_LIMIT),
        name="combine",
    )(*args)


def _moe(h2, idx, gates, rank, counts, x1, gate2, final_g, w_gu, b_gu, w_d, b_d, seq,
         row0, t_total, prev):
    t = h2.shape[0]
    n_slots = t * TOP_K
    n_blocks = -(-n_slots // EXPERT_BLOCK) + N_EXPERTS
    cap = n_blocks * EXPERT_BLOCK
    padded = (counts + EXPERT_BLOCK - 1) // EXPERT_BLOCK * EXPERT_BLOCK
    pad_ends = jnp.cumsum(padded)
    pad_starts = pad_ends - padded
    dest = pad_starts[idx] + rank
    block_starts = jnp.arange(n_blocks, dtype=jnp.int32) * EXPERT_BLOCK
    block_e = jnp.minimum(jnp.sum(block_starts[:, None] >= pad_ends[None, :], axis=1),
                          N_EXPERTS - 1).astype(jnp.int32)
    n_valid = jnp.clip(counts[block_e] - (block_starts - pad_starts[block_e]), 0, EXPERT_BLOCK)

    xs = _sc_scatter_rows(h2, dest, cap)
    yb = _experts(block_e, n_valid.astype(jnp.int32), xs, w_gu, b_gu, w_d, b_d)
    dest_blocks = dest.reshape(-1, COMBINE_TOKENS, TOP_K).transpose(0, 2, 1).reshape(-1)
    yg = _sc_gather_rows(dest_blocks, yb)
    return _combine(yg, x1, gates, gate2, final_g, seq, row0, t_total, prev)


def _layer(x, c_mod, norm1_g, w_in, mu_shift, w0, w2, a0, a2, g2, k_k, k_a, r_k, gn_w, gn_b, b_f,
           q_norm_g, k_norm_g, o_norm_g, w_out, norm2_g, w_router, b_router, w_gate_up,
           b_gate_up, w_down, b_down, final_g, tm_in, tm_out):
    bsz, seq, _ = x.shape
    shift1, scale1, gate1, shift2, scale2, gate2 = (
        m.reshape(bsz, 1, D_MODEL) for m in jnp.split(c_mod, 6, axis=-1))
    row = lambda v: v.reshape(1, -1)

    w_r = w_in[:, :RWKV_COLS].astype(BF16)
    w_x = w_in[:, RWKV_COLS:RWKV_COLS + FOX_MAIN].astype(BF16)
    w_f = jnp.pad(w_in[:, RWKV_COLS + FOX_MAIN:], ((0, 0), (0, LANES - N_HEADS)))
    b_fp = jnp.pad(b_f, (0, LANES - N_HEADS)).reshape(1, LANES)
    qk_gain = jnp.concatenate([jnp.tile(q_norm_g, N_HEADS) * HEAD_DIM ** -0.5,
                               jnp.tile(k_norm_g, N_HEADS)]).reshape(1, -1)
    p_r, p_x, k_bias, q_bias = _inproj(x, shift1, scale1, row(norm1_g), w_r, w_x, w_f, b_fp,
                                       qk_gain, tm_in)

    zeros = jnp.zeros((LANES - 64, D_GRP), F32)
    w2p = jnp.concatenate([w2, zeros], axis=0).astype(BF16)
    a2p = jnp.concatenate([zeros, a2], axis=0).astype(BF16)
    y_r = _rwkv(p_r, row(mu_shift), row(w0), w2p, row(a0), a2p, g2.astype(BF16), row(k_k),
                row(k_a), row(r_k), row(gn_w), row(gn_b))

    y_f = _fox(p_x, k_bias, q_bias, jnp.tile(o_norm_g, 2).reshape(1, LANES))

    t = bsz * seq
    w_rt = jnp.pad(w_router, ((0, 0), (0, LANES - N_EXPERTS)))
    b_rt = jnp.pad(b_router, (0, LANES - N_EXPERTS)).reshape(1, LANES)
    wo = w_out.astype(BF16)
    w_gu, w_d = w_gate_up, w_down
    b_gu, b_d = b_gate_up.reshape(N_EXPERTS, 1, -1), b_down.reshape(N_EXPERTS, 1, -1)
    t_part = t // MOE_SPLITS
    out = None
    for part in range(MOE_SPLITS):
        row0 = part * t_part
        x1, h2, idx, gates, rank, cnt = _outproj(
            x.reshape(t, D_MODEL), y_r.reshape(t, D_GRP), y_f.reshape(t, D_GRP), gate1, shift2,
            scale2, row(norm2_g), wo[:D_GRP], wo[D_GRP:], w_rt, b_rt, tm_out, seq, row0, t_part)
        counts = cnt[0, :N_EXPERTS].astype(jnp.int32)
        out = _moe(h2, idx[:, :TOP_K], gates, rank[:, :TOP_K], counts, x1, gate2, row(final_g),
                   w_gu, b_gu, w_d, b_d, seq, row0, t, out)
    return out.reshape(bsz, seq, D_MODEL)


def kernel(x, c, w_ada, b_ada, norm1_g, w_in, mu_shift, w0, w2, a0, a2, g2, k_k, k_a, r_k, gn_w,
           gn_b, b_f, q_norm_g, k_norm_g, o_norm_g, w_out, norm2_g, w_router, b_router, w_gate_up,
           b_gate_up, w_down, b_down, final_g):
    assert w_ada.shape[0] == 1, "single-layer block"
    c_mod = _adaln(c, w_ada[0], b_ada[0])
    return _layer(x, c_mod, norm1_g[0], w_in[0], mu_shift[0], w0[0], w2[0], a0[0], a2[0], g2[0],
                  k_k[0], k_a[0], r_k[0], gn_w[0], gn_b[0], b_f[0], q_norm_g[0], k_norm_g[0],
                  o_norm_g[0], w_out[0], norm2_g[0], w_router[0], b_router[0], w_gate_up[0],
                  b_gate_up[0], w_down[0], b_down[0], final_g,
                  tm_in=min(512, x.shape[1]), tm_out=min(1024, x.shape[1]))
```

```python
import functools

import jax
import jax.numpy as jnp
from jax import lax
from jax.experimental import pallas as pl
from jax.experimental.pallas import tpu as pltpu
from jax.experimental.pallas import tpu_sc as plsc

F32 = jnp.float32
BF16 = jnp.bfloat16
HIGHEST = lax.Precision.HIGHEST

D_MODEL = 1024
HEAD_DIM = 64
N_HEADS = 8
D_GRP = N_HEADS * HEAD_DIM
RWKV_COLS = 1792
LORA_OFF = 3 * D_GRP
GATE_OFF = LORA_OFF + 128
FOX_MAIN = 4 * D_GRP
N_EXPERTS = 32
TOP_K = 4
EXPERT_BLOCK = 512
CAST_COLS = 256
SWIGLU_ALPHA = 1.702
SWIGLU_LIMIT = 7.0
NORM_EPS = 1e-6
GN_EPS = 64e-5
LANES = 128
CHUNK = 64
FOX_SUB_KEYS = 512
HEADS_PER_SCAN = 4
SCAN_W = HEADS_PER_SCAN * HEAD_DIM
SEG_TERMS = 1
CUM_TERMS = 2
VMEM_LIMIT = 56 * 1024 * 1024


def _dot(a, b):
    return jnp.dot(a.astype(BF16), b.astype(BF16), preferred_element_type=F32)


def _dot_nt(a, b):
    return lax.dot_general(a.astype(BF16), b.astype(BF16), (((1,), (1,)), ((), ())),
                           preferred_element_type=F32)


def _dot_tn(a, b):
    return lax.dot_general(a.astype(BF16), b.astype(BF16), (((0,), (0,)), ((), ())),
                           preferred_element_type=F32)


def _fdot(a, b):
    return jnp.dot(a, b, precision=HIGHEST, preferred_element_type=F32)


def _split_dot(x, m, terms=2, left=False):
    acc = None
    rem = x
    for _ in range(terms):
        part = rem.astype(BF16)
        rem = rem - part.astype(F32)
        d = (jnp.dot(m, part, preferred_element_type=F32) if left
             else jnp.dot(part, m, preferred_element_type=F32))
        acc = d if acc is None else acc + d
    return acc


def _iota(shape, dim):
    return lax.broadcasted_iota(jnp.int32, shape, dim)


def _seg_reduce_mat(n):
    return (_iota((n, LANES), 0) // HEAD_DIM == _iota((n, LANES), 1)).astype(BF16)


def _seg_expand_mat(n):
    return (_iota((LANES, n), 1) // HEAD_DIM == _iota((LANES, n), 0)).astype(BF16)


def _tri(n, strict):
    r, c = _iota((n, n), 0), _iota((n, n), 1)
    return ((r > c) if strict else (r >= c)).astype(BF16)


D_PACK = D_MODEL // 2


def _pack_rows(x):
    lo = lax.bitcast_convert_type(x[:, :D_PACK].astype(BF16).astype(F32), jnp.uint32)
    hi = lax.bitcast_convert_type(x[:, D_PACK:].astype(BF16).astype(F32), jnp.uint32)
    return hi | (lo >> 16)


def _unpack_rows(p):
    lo = lax.bitcast_convert_type(p << 16, F32)
    hi = lax.bitcast_convert_type(p & jnp.uint32(0xFFFF0000), F32)
    return lo, hi


def _log_sigmoid(z):
    return jnp.minimum(z, 0.0) - jnp.log(1.0 + jnp.exp(-jnp.abs(z)))


def _sigmoid(z):
    return 1.0 / (1.0 + jnp.exp(-z))


def _adaln_kernel(c_ref, w_ref, b_ref, o_ref):
    c = c_ref[...]
    o_ref[...] = _fdot(c * _sigmoid(c), w_ref[...]) + b_ref[...]


def _adaln(c, w_ada, b_ada):
    bsz = c.shape[0]
    n_mod = w_ada.shape[1] // D_MODEL
    return pl.pallas_call(
        _adaln_kernel,
        grid=(n_mod,),
        in_specs=[pl.BlockSpec((bsz, D_MODEL), lambda j: (0, 0)),
                  pl.BlockSpec((D_MODEL, D_MODEL), lambda j: (0, j)),
                  pl.BlockSpec((1, D_MODEL), lambda j: (0, j))],
        out_specs=pl.BlockSpec((bsz, D_MODEL), lambda j: (0, j)),
        out_shape=jax.ShapeDtypeStruct((bsz, n_mod * D_MODEL), F32),
        name="adaln",
    )(c, w_ada, b_ada.reshape(1, -1))


def _inproj_kernel(x_ref, sh_ref, sc_ref, g_ref, wr_ref, wx_ref, wfh_ref, wfl_ref, bf_ref, qkg_ref,
                   pr_ref, px_ref, kb_ref, qb_ref, carry_ref):
    @pl.when(pl.program_id(1) == 0)
    def _():
        carry_ref[...] = jnp.zeros_like(carry_ref)

    x = x_ref[...]
    tm = x.shape[0]
    h = x * lax.rsqrt(jnp.mean(x * x, axis=-1, keepdims=True) + NORM_EPS) * g_ref[...]
    h = h * (1.0 + sc_ref[...]) + sh_ref[...]
    hb = h.astype(BF16)
    h_lo = (h - hb.astype(F32)).astype(BF16)

    pr_ref[...] = jnp.dot(hb, wr_ref[...], preferred_element_type=F32).astype(BF16)

    px = jnp.dot(hb, wx_ref[...], preferred_element_type=F32)
    qk = px[:, :2 * D_GRP]
    ss = _split_dot(qk * qk, _seg_reduce_mat(2 * D_GRP), SEG_TERMS)
    inv = lax.rsqrt(ss * (1.0 / HEAD_DIM) + NORM_EPS)
    qk = qk * _split_dot(inv, _seg_expand_mat(2 * D_GRP), SEG_TERMS) * qkg_ref[...]
    px_ref[:, :2 * D_GRP] = qk.astype(BF16)
    px_ref[:, 2 * D_GRP:] = px[:, 2 * D_GRP:].astype(BF16)

    z = (jnp.dot(hb, wfh_ref[...], preferred_element_type=F32)
         + jnp.dot(h_lo, wfh_ref[...], preferred_element_type=F32)
         + jnp.dot(hb, wfl_ref[...], preferred_element_type=F32)) + bf_ref[...]
    cum = _split_dot(_log_sigmoid(z), _tri(tm, False), terms=3, left=True) + carry_ref[...]
    carry_ref[...] = cum[tm - 1:tm, :]

    parts = []
    rem = cum
    for _ in range(3):
        part = rem.astype(BF16)
        rem = rem - part.astype(F32)
        parts.append(part)
    src, dst = _iota((3 * LANES, 2 * LANES), 0), _iota((3 * LANES, 2 * LANES), 1)
    term, head = src // LANES, src % LANES
    place = ((head < N_HEADS) & ((dst == 8 * head + term) | (dst == LANES + 8 * head + 3 + term)))
    spread = jnp.dot(jnp.concatenate(parts, axis=1), place.astype(BF16),
                     preferred_element_type=F32)

    slot = _iota((1, LANES), 1) % 8
    kb_ref[...] = (jnp.where((slot >= 3) & (slot < 6), 1.0, 0.0) - spread[:, :LANES]).astype(BF16)
    qb_ref[...] = (jnp.where(slot < 3, 1.0, 0.0) + spread[:, LANES:]).astype(BF16)


def _inproj(x, shift, scale, g, w_r, w_x, w_f, b_f, qk_gain, tm):
    w_f_hi = w_f.astype(BF16)
    w_f_lo = (w_f - w_f_hi.astype(F32)).astype(BF16)
    bsz, seq, _ = x.shape
    const = lambda b, s: (0, 0)
    return pl.pallas_call(
        _inproj_kernel,
        grid=(bsz, seq // tm),
        in_specs=[pl.BlockSpec((None, tm, D_MODEL), lambda b, s: (b, s, 0)),
                  pl.BlockSpec((None, 1, D_MODEL), lambda b, s: (b, 0, 0)),
                  pl.BlockSpec((None, 1, D_MODEL), lambda b, s: (b, 0, 0)),
                  pl.BlockSpec((1, D_MODEL), const),
                  pl.BlockSpec((D_MODEL, RWKV_COLS), const),
                  pl.BlockSpec((D_MODEL, FOX_MAIN), const),
                  pl.BlockSpec((D_MODEL, LANES), const),
                  pl.BlockSpec((D_MODEL, LANES), const),
                  pl.BlockSpec((1, LANES), const),
                  pl.BlockSpec((1, 2 * D_GRP), const)],
        out_specs=[pl.BlockSpec((None, tm, RWKV_COLS), lambda b, s: (b, s, 0)),
                   pl.BlockSpec((None, tm, FOX_MAIN), lambda b, s: (b, s, 0)),
                   pl.BlockSpec((None, tm, LANES), lambda b, s: (b, s, 0)),
                   pl.BlockSpec((None, tm, LANES), lambda b, s: (b, s, 0))],
        out_shape=[jax.ShapeDtypeStruct((bsz, seq, RWKV_COLS), BF16),
                   jax.ShapeDtypeStruct((bsz, seq, FOX_MAIN), BF16),
                   jax.ShapeDtypeStruct((bsz, seq, LANES), BF16),
                   jax.ShapeDtypeStruct((bsz, seq, LANES), BF16)],
        scratch_shapes=[pltpu.VMEM((1, LANES), F32)],
        compiler_params=pltpu.CompilerParams(
            dimension_semantics=("parallel", "arbitrary"), vmem_limit_bytes=VMEM_LIMIT),
        name="inproj",
    )(x, shift, scale, g, w_r, w_x, w_f_hi, w_f_lo, b_f, qk_gain)


_NN = (((1,), (0,)), ((), ()))
_NT = (((1,), (1,)), ((), ()))
_TN = (((0,), (0,)), ((), ()))
SCAN_N = HEADS_PER_SCAN * CHUNK
BATCH_PER_STEP = 8
INV_LEVELS = 5
M_HEAD, M_STRICT, M_INCL, M_EYE, M_BASE, M_OFF = 0, 1, 2, 3, 4, 5


def _bdot(a, b, dims):
    return lax.dot_general(a, b, dims, preferred_element_type=F32)


def _scan_masks():
    rr, cc = _iota((SCAN_N, SCAN_W), 0), _iota((SCAN_N, SCAN_W), 1)
    ri, ci = _iota((SCAN_N, SCAN_N), 0), _iota((SCAN_N, SCAN_N), 1)
    same = ri // CHUNK == ci // CHUNK
    masks = [rr // CHUNK == cc // HEAD_DIM, same & (ri > ci), same & (ri >= ci), ri == ci,
             (ri // 2 == ci // 2) & (ri > ci)]
    blk = 2
    while blk < CHUNK:
        masks.append((ri // (2 * blk) == ci // (2 * blk)) & (ri // blk != ci // blk) & (ri > ci))
        blk *= 2
    return jnp.stack(masks).astype(BF16)


def _rwkv_kernel(p_ref, masks_ref, mu_ref, w0_ref, w2_ref, a0_ref, a2_ref, g2_ref, kk_ref, ka_ref,
                 rk_ref, gnw_ref, gnb_ref, o_ref, last_ref, state_ref):
    @pl.when(pl.program_id(1) == 0)
    def _():
        last_ref[...] = jnp.zeros_like(last_ref)
        state_ref[...] = jnp.zeros_like(state_ref)

    mu, w0, w2, a0, a2, g2, k_k, k_a, r_k, gn_w, gn_b = (
        ref[...] for ref in (mu_ref, w0_ref, w2_ref, a0_ref, a2_ref, g2_ref, kk_ref, ka_ref,
                             rk_ref, gnw_ref, gnb_ref))
    rows = BATCH_PER_STEP * CHUNK
    p = p_ref[...].astype(F32).reshape(rows, RWKV_COLS)
    row_id = _iota((rows, 1), 0)
    prev = pltpu.roll(p, 1, axis=0)
    for bb in range(BATCH_PER_STEP):
        prev = jnp.where(row_id == bb * CHUNK, last_ref[bb], prev)
        last_ref[bb] = p[(bb + 1) * CHUNK - 1:(bb + 1) * CHUNK, :]
    pf = p + mu * (prev - p)
    r = pf[:, 0:D_GRP]
    k = pf[:, D_GRP:2 * D_GRP]
    v = pf[:, 2 * D_GRP:3 * D_GRP]
    lora = pf[:, LORA_OFF:GATE_OFF]
    gd = pf[:, GATE_OFF:RWKV_COLS]

    wlog = w0 + _dot(jnp.tanh(lora), w2)
    neg = -wlog
    softplus = jnp.maximum(neg, 0.0) + jnp.log(1.0 + jnp.exp(-jnp.abs(neg)))
    ld = -jnp.exp(-softplus - 0.5)
    a = _sigmoid(a0 + _dot(lora, a2))
    g = _dot(_sigmoid(gd), g2)

    red, exp_m = _seg_reduce_mat(D_GRP), _seg_expand_mat(D_GRP)
    kk = k * k_k
    n2 = _split_dot(kk * kk, red, SEG_TERMS)
    kk = kk * _split_dot(1.0 / jnp.maximum(jnp.sqrt(n2), 1e-12), exp_m, SEG_TERMS)
    k2 = k * (1.0 + (a - 1.0) * k_a)

    tr, tc = _iota((rows, rows), 0), _iota((rows, rows), 1)
    tri = ((tr >= tc) & (tr // CHUNK == tc // CHUNK)).astype(BF16)
    cl = _split_dot(ld, tri, terms=CUM_TERMS, left=True)
    cl_end = jnp.concatenate(
        [jnp.broadcast_to(cl[(bb + 1) * CHUNK - 1:(bb + 1) * CHUNK, :], (CHUNK, D_GRP))
         for bb in range(BATCH_PER_STEP)], axis=0)
    e_in = jnp.exp(cl)
    e_out = jnp.exp(-cl)
    e_rem = jnp.exp(cl_end - cl)
    p_end = jnp.exp(cl_end)
    kka = kk * a
    ops = [(-kk * jnp.exp(cl - ld)).astype(BF16), (kka * e_out).astype(BF16),
           (k2 * e_out).astype(BF16), (r * e_in).astype(BF16), v.astype(BF16),
           (kka * e_rem).astype(BF16), (k2 * e_rem).astype(BF16)]

    chains = [(bb, grp) for bb in range(BATCH_PER_STEP)
              for grp in range(N_HEADS // HEADS_PER_SCAN)]
    head_mask = masks_ref[M_HEAD]
    strict, incl = masks_ref[M_STRICT], masks_ref[M_INCL]

    def stacked(op, bb, grp):
        part = op[bb * CHUNK:(bb + 1) * CHUNK, grp * SCAN_W:(grp + 1) * SCAN_W]
        return jnp.concatenate([part] * HEADS_PER_SCAN, axis=0) * head_mask

    xs = [[stacked(op, bb, grp) for op in ops] for bb, grp in chains]
    st = [state_ref[bb, grp] for bb, grp in chains]
    sb = [s.astype(BF16) for s in st]
    nab = [_bdot(x[0], x[1], _NT).astype(BF16) for x in xs]
    aak = [_bdot(x[0], x[2], _NT).astype(BF16) * strict for x in xs]
    arb = [_bdot(x[3], x[1], _NT).astype(BF16) * incl for x in xs]
    ark = [_bdot(x[3], x[2], _NT).astype(BF16) * incl for x in xs]
    t_inv = [masks_ref[M_EYE] + n * masks_ref[M_BASE] for n in nab]
    for lvl in range(INV_LEVELS):
        half = [_bdot(t, n * masks_ref[M_OFF + lvl], _NN).astype(BF16) for t, n in zip(t_inv, nab)]
        t_inv = [t + _bdot(h, t, _NN).astype(BF16) for t, h in zip(t_inv, half)]
    rhs = [(_bdot(x[0], s, _NT) + _bdot(k, x[4], _NN)).astype(BF16)
           for x, s, k in zip(xs, sb, aak)]
    sa = [_bdot(t, h, _NN).astype(BF16) for t, h in zip(t_inv, rhs)]
    ys = [_bdot(x[3], s, _NT) + _bdot(b, u, _NN) + _bdot(k, x[4], _NN)
          for x, s, b, u, k in zip(xs, sb, arb, sa, ark)]
    for (bb, grp), x, s, u in zip(chains, xs, st, sa):
        decay = p_end[bb * CHUNK:bb * CHUNK + 1, grp * SCAN_W:(grp + 1) * SCAN_W]
        state_ref[bb, grp] = s * decay + _bdot(u, x[5], _TN) + _bdot(x[4], x[6], _TN)
    ys = [y[0:CHUNK] + y[CHUNK:2 * CHUNK] + y[2 * CHUNK:3 * CHUNK] + y[3 * CHUNK:4 * CHUNK]
          for y in ys]
    n_grp = N_HEADS // HEADS_PER_SCAN
    y = jnp.concatenate([jnp.concatenate(ys[bb * n_grp:(bb + 1) * n_grp], axis=1)
                         for bb in range(BATCH_PER_STEP)], axis=0)

    mean = _split_dot(_split_dot(y, red, SEG_TERMS) * (1.0 / HEAD_DIM), exp_m, SEG_TERMS)
    d = y - mean
    var = _split_dot(d * d, red, SEG_TERMS) * (1.0 / HEAD_DIM)
    yn = d * _split_dot(lax.rsqrt(var + GN_EPS), exp_m, SEG_TERMS) * gn_w + gn_b
    bonus = _split_dot(_split_dot(r * k2 * r_k, red, SEG_TERMS), exp_m, SEG_TERMS) * v
    o_ref[...] = ((yn + bonus) * g).astype(BF16).reshape(BATCH_PER_STEP, CHUNK, D_GRP)


def _rwkv(p_r, mu, w0, w2p, a0, a2p, g2, k_k, k_a, r_k, gn_w, gn_b):
    bsz, seq, _ = p_r.shape
    assert bsz % BATCH_PER_STEP == 0
    masks = _scan_masks()
    const = lambda b, s: (0, 0)
    vec = pl.BlockSpec((1, D_GRP), const)
    return pl.pallas_call(
        _rwkv_kernel,
        grid=(bsz // BATCH_PER_STEP, seq // CHUNK),
        in_specs=[pl.BlockSpec((BATCH_PER_STEP, CHUNK, RWKV_COLS), lambda b, s: (b, s, 0)),
                  pl.BlockSpec(masks.shape, lambda b, s: (0, 0, 0)),
                  pl.BlockSpec((1, RWKV_COLS), const),
                  vec, pl.BlockSpec((LANES, D_GRP), const),
                  vec, pl.BlockSpec((LANES, D_GRP), const),
                  pl.BlockSpec((LANES, D_GRP), const),
                  vec, vec, vec, vec, vec],
        out_specs=pl.BlockSpec((BATCH_PER_STEP, CHUNK, D_GRP), lambda b, s: (b, s, 0)),
        out_shape=jax.ShapeDtypeStruct((bsz, seq, D_GRP), BF16),
        scratch_shapes=[pltpu.VMEM((BATCH_PER_STEP, 1, RWKV_COLS), F32),
                        pltpu.VMEM((BATCH_PER_STEP, N_HEADS // HEADS_PER_SCAN, SCAN_W, SCAN_W), F32)],
        compiler_params=pltpu.CompilerParams(
            dimension_semantics=("parallel", "arbitrary"), vmem_limit_bytes=VMEM_LIMIT),
        name="rwkv",
    )(p_r, masks, mu, w0, w2p, a0, a2p, g2, k_k, k_a, r_k, gn_w, gn_b)


def _fox_kernel(q_ref, qb_ref, k_ref, kb_ref, vt_ref, og_ref, ong_ref, o_ref, m_ref, l_ref, acc_ref,
                *, seq):
    hp = pl.program_id(1)
    lane = _iota((1, LANES), 1)
    q = q_ref[...]
    qb = qb_ref[...]
    zero = jnp.zeros_like(q)
    qcat = [jnp.concatenate([jnp.where(lane // HEAD_DIM == hh, q, zero),
                             jnp.where(lane // 8 == hp * 2 + hh, qb, zero)], axis=1)
            for hh in range(2)]
    keys = min(FOX_SUB_KEYS, seq)
    n_sub = seq // keys
    diag = _iota((keys, keys), 1) >= _iota((keys, keys), 0)

    m_ref[...] = jnp.full(m_ref.shape, -jnp.inf, F32)
    l_ref[...] = jnp.zeros(l_ref.shape, F32)
    acc_ref[...] = jnp.zeros(acc_ref.shape, F32)

    def scores(s):
        lo = s * keys
        kcat = jnp.concatenate([k_ref[lo:lo + keys, :], kb_ref[lo:lo + keys, :]], axis=1)
        return [lax.dot_general(kcat, qc[lo:, :], _NT, preferred_element_type=F32)
                for qc in qcat]

    pending = scores(0)
    for s in range(n_sub):
        lo = s * keys
        nxt = scores(s + 1) if s + 1 < n_sub else None
        vt = vt_ref[:, lo:lo + keys]
        sts = [jnp.concatenate([jnp.where(diag, st[:, :keys], -jnp.inf), st[:, keys:]], axis=1)
               if st.shape[1] > keys else jnp.where(diag, st, -jnp.inf) for st in pending]
        m_old = [m_ref[hh, :, lo:] for hh in range(2)]
        m_new = [jnp.maximum(m, jnp.max(st, axis=0, keepdims=True)) for m, st in zip(m_old, sts)]
        pts = [jnp.exp(st - m) for st, m in zip(sts, m_new)]
        pvs = [jnp.dot(vt, pt.astype(BF16), preferred_element_type=F32) for pt in pts]
        for hh in range(2):
            alpha = jnp.exp(m_old[hh] - m_new[hh])
            m_ref[hh, :, lo:] = m_new[hh]
            l_ref[hh, :, lo:] = alpha * l_ref[hh, :, lo:] + jnp.sum(pts[hh], axis=0, keepdims=True)
            acc_ref[hh, :, lo:] = (alpha * acc_ref[hh, :, lo:]
                                   + pvs[hh][hh * HEAD_DIM:(hh + 1) * HEAD_DIM, :])
        pending = nxt

    outs = []
    for hh in range(2):
        o = acc_ref[hh] / l_ref[hh]
        outs.append(o * lax.rsqrt(jnp.mean(o * o, axis=0, keepdims=True) + NORM_EPS))
    o = jnp.concatenate(outs, axis=0).T
    o_ref[...] = (o * ong_ref[...] * _sigmoid(og_ref[...].astype(F32))).astype(BF16)


def _fox(p_x, k_bias, q_bias, o_gain):
    bsz, seq, _ = p_x.shape
    npair = N_HEADS // 2
    v_t = jnp.transpose(p_x[:, :, 2 * D_GRP:3 * D_GRP], (0, 2, 1))
    return pl.pallas_call(
        functools.partial(_fox_kernel, seq=seq),
        grid=(bsz, npair),
        in_specs=[pl.BlockSpec((None, seq, LANES), lambda b, h: (b, 0, h)),
                  pl.BlockSpec((None, seq, LANES), lambda b, h: (b, 0, 0)),
                  pl.BlockSpec((None, seq, LANES), lambda b, h: (b, 0, npair + h)),
                  pl.BlockSpec((None, seq, LANES), lambda b, h: (b, 0, 0)),
                  pl.BlockSpec((None, LANES, seq), lambda b, h: (b, h, 0)),
                  pl.BlockSpec((None, seq, LANES), lambda b, h: (b, 0, 3 * npair + h)),
                  pl.BlockSpec((1, LANES), lambda b, h: (0, 0))],
        out_specs=pl.BlockSpec((None, seq, LANES), lambda b, h: (b, 0, h)),
        out_shape=jax.ShapeDtypeStruct((bsz, seq, D_GRP), BF16),
        scratch_shapes=[pltpu.VMEM((2, 1, seq), F32), pltpu.VMEM((2, 1, seq), F32),
                        pltpu.VMEM((2, HEAD_DIM, seq), F32)],
        compiler_params=pltpu.CompilerParams(
            dimension_semantics=("parallel", "parallel"), vmem_limit_bytes=VMEM_LIMIT),
        name="fox",
    )(p_x, q_bias, p_x, k_bias, v_t, p_x, o_gain)


def _outproj_kernel(x_ref, yr_ref, yf_ref, g1_ref, sh_ref, sc_ref, ng_ref, wor_ref, wof_ref,
                    wrt_ref, wrl_ref, brt_ref, x1_ref, h2_ref, idx_ref, gate_ref, rank_ref, cnt_ref,
                    carry_ref):
    @pl.when(pl.program_id(0) == 0)
    def _():
        carry_ref[...] = jnp.zeros_like(carry_ref)

    y = (jnp.dot(yr_ref[...], wor_ref[...], preferred_element_type=F32)
         + jnp.dot(yf_ref[...], wof_ref[...], preferred_element_type=F32))
    x1 = x_ref[...] + g1_ref[...] * y
    x1_ref[...] = x1
    tm = x1.shape[0]
    h = x1 * lax.rsqrt(jnp.mean(x1 * x1, axis=-1, keepdims=True) + NORM_EPS) * ng_ref[...]
    h2 = h * (1.0 + sc_ref[...]) + sh_ref[...]
    h2_ref[...] = _pack_rows(h2)

    lane = _iota((tm, LANES), 1)
    h_hi = h2.astype(BF16)
    h_lo = (h2 - h_hi.astype(F32)).astype(BF16)
    logits = (jnp.dot(h_hi, wrt_ref[...], preferred_element_type=F32)
              + jnp.dot(h_lo, wrt_ref[...], preferred_element_type=F32)
              + jnp.dot(h_hi, wrl_ref[...], preferred_element_type=F32)) + brt_ref[...]
    lg = jnp.where(lane < N_EXPERTS, logits, -jnp.inf)
    picks = []
    hot_sum = jnp.zeros((tm, LANES), F32)
    for _ in range(TOP_K):
        m = jnp.max(lg, axis=-1, keepdims=True)
        sel = jnp.min(jnp.where(lg == m, lane, LANES), axis=-1, keepdims=True)
        hot = lane == sel
        picks.append((m, sel, hot))
        hot_sum = hot_sum + hot.astype(F32)
        lg = jnp.where(hot, -jnp.inf, lg)
    es = [jnp.exp(m - picks[0][0]) for m, _, _ in picks]
    den = es[0] + es[1] + es[2] + es[3]

    before = jnp.dot(_tri(tm, True), hot_sum.astype(BF16), preferred_element_type=F32)
    before = before + carry_ref[...]
    idx_out = jnp.zeros((tm, LANES), jnp.int32)
    gate_out = jnp.zeros((tm, LANES), F32)
    rank_out = jnp.zeros((tm, LANES), jnp.int32)
    for kk, (m, sel, hot) in enumerate(picks):
        rk = jnp.sum(jnp.where(hot, before, 0.0), axis=-1, keepdims=True).astype(jnp.int32)
        idx_out = jnp.where(lane == kk, sel, idx_out)
        gate_out = jnp.where(lane == kk, es[kk] / den, gate_out)
        rank_out = jnp.where(lane == kk, rk, rank_out)
    idx_ref[...] = idx_out
    gate_ref[...] = gate_out
    rank_ref[...] = rank_out
    carry_ref[...] = carry_ref[...] + jnp.sum(hot_sum, axis=0, keepdims=True)
    cnt_ref[...] = carry_ref[...]


def _outproj(x2d, y_r, y_f, gate1, shift2, scale2, norm_g, wo_r, wo_f, w_rt, b_rt, tm, seq,
             row0, t):
    w_rt_hi = w_rt.astype(BF16)
    w_rt_lo = (w_rt - w_rt_hi.astype(F32)).astype(BF16)
    per_b = seq // tm
    blk0 = row0 // tm
    const = lambda i: (0, 0)
    rows = lambda i: (i, 0)
    rows_in = lambda i: (i + blk0, 0)
    mod = pl.BlockSpec((None, 1, D_MODEL), lambda i: ((i + blk0) // per_b, 0, 0))
    return pl.pallas_call(
        _outproj_kernel,
        grid=(t // tm,),
        in_specs=[pl.BlockSpec((tm, D_MODEL), rows_in),
                  pl.BlockSpec((tm, D_GRP), rows_in),
                  pl.BlockSpec((tm, D_GRP), rows_in),
                  mod, mod, mod,
                  pl.BlockSpec((1, D_MODEL), const),
                  pl.BlockSpec((D_GRP, D_MODEL), const),
                  pl.BlockSpec((D_GRP, D_MODEL), const),
                  pl.BlockSpec((D_MODEL, LANES), const),
                  pl.BlockSpec((D_MODEL, LANES), const),
                  pl.BlockSpec((1, LANES), const)],
        out_specs=[pl.BlockSpec((tm, D_MODEL), rows),
                   pl.BlockSpec((tm, D_PACK), rows),
                   pl.BlockSpec((tm, LANES), rows),
                   pl.BlockSpec((tm, LANES), rows),
                   pl.BlockSpec((tm, LANES), rows),
                   pl.BlockSpec((1, LANES), const)],
        out_shape=[jax.ShapeDtypeStruct((t, D_MODEL), F32),
                   jax.ShapeDtypeStruct((t, D_PACK), jnp.uint32),
                   jax.ShapeDtypeStruct((t, LANES), jnp.int32),
                   jax.ShapeDtypeStruct((t, LANES), F32),
                   jax.ShapeDtypeStruct((t, LANES), jnp.int32),
                   jax.ShapeDtypeStruct((1, LANES), F32)],
        scratch_shapes=[pltpu.VMEM((1, LANES), F32)],
        compiler_params=pltpu.CompilerParams(
            dimension_semantics=("arbitrary",), vmem_limit_bytes=VMEM_LIMIT),
        name="outproj",
    )(x2d, y_r, y_f, gate1, shift2, scale2, norm_g, wo_r, wo_f, w_rt_hi, w_rt_lo, b_rt)


SC_CORES = 2
SC_SUBCORES = 16
SC_ROWS = 64


def _sc_gather_rows(idx, src):
    n_workers = SC_CORES * SC_SUBCORES
    m = idx.shape[0]
    d = src.shape[1]
    assert m % (n_workers * SC_ROWS) == 0
    n_chunks = m // (n_workers * SC_ROWS)
    mesh = plsc.VectorSubcoreMesh(core_axis_name="c", subcore_axis_name="s")

    @functools.partial(
        pl.kernel, mesh=mesh,
        out_type=jax.ShapeDtypeStruct((m, d), src.dtype),
        scratch_types=[pltpu.VMEM((n_chunks, SC_ROWS), jnp.int32),
                       pltpu.VMEM((SC_ROWS, d), src.dtype),
                       pltpu.SemaphoreType.DMA],
        name="sc_gather")
    def gather(src_hbm, idx_hbm, out_hbm, idx_v, rows_v, sem):
        wid = lax.axis_index("s") * SC_CORES + lax.axis_index("c")
        pltpu.sync_copy(idx_hbm.at[wid], idx_v)

        @pl.loop(0, n_chunks)
        def _(j):
            pltpu.async_copy(src_hbm.at[idx_v.at[j]], rows_v, sem).wait()
            pltpu.sync_copy(rows_v, out_hbm.at[pl.ds((wid * n_chunks + j) * SC_ROWS, SC_ROWS)])

    return gather(src, idx.reshape(n_workers, n_chunks, SC_ROWS))


def _sc_scatter_rows(src, dest, n_out):
    n_workers = SC_CORES * SC_SUBCORES
    t, d = src.shape
    n_slot = dest.shape[1]
    assert t % (n_workers * SC_ROWS) == 0
    n_chunks = t // (n_workers * SC_ROWS)
    mesh = plsc.VectorSubcoreMesh(core_axis_name="c", subcore_axis_name="s")
    idx = dest.reshape(n_workers, n_chunks, SC_ROWS, n_slot).transpose(0, 1, 3, 2)
    idx = idx.reshape(n_workers, n_chunks * n_slot, SC_ROWS)

    @functools.partial(
        pl.kernel, mesh=mesh,
        out_type=jax.ShapeDtypeStruct((n_out, d), src.dtype),
        scratch_types=[pltpu.VMEM((n_chunks * n_slot, SC_ROWS), jnp.int32),
                       pltpu.VMEM((SC_ROWS, d), src.dtype)],
        name="sc_scatter")
    def scatter(src_hbm, idx_hbm, out_hbm, idx_v, rows_v):
        wid = lax.axis_index("s") * SC_CORES + lax.axis_index("c")
        pltpu.sync_copy(idx_hbm.at[wid], idx_v)

        @pl.loop(0, n_chunks)
        def _(j):
            pltpu.sync_copy(src_hbm.at[pl.ds((wid * n_chunks + j) * SC_ROWS, SC_ROWS)], rows_v)
            for k in range(n_slot):
                pltpu.sync_copy(rows_v, out_hbm.at[idx_v.at[j * n_slot + k]])

    return scatter(src, idx)


def _expert_kernel(be_ref, nv_ref, x_ref, wgu_ref, bgu_ref, wd_ref, bd_ref, o_ref, wgu_bf, wd_bf):
    j = pl.program_id(0)

    @pl.when((j == 0) | (be_ref[j] != be_ref[jnp.maximum(j - 1, 0)]))
    def _():
        for c in range(0, 2 * D_MODEL, CAST_COLS):
            wgu_bf[:, c:c + CAST_COLS] = wgu_ref[:, c:c + CAST_COLS].astype(BF16)
        for c in range(0, D_MODEL, CAST_COLS):
            wd_bf[:, c:c + CAST_COLS] = wd_ref[:, c:c + CAST_COLS].astype(BF16)

    valid = _iota((EXPERT_BLOCK, 1), 0) < nv_ref[j]
    lo, hi = _unpack_rows(jnp.where(valid, x_ref[...], jnp.uint32(0)))
    x = jnp.concatenate([lo.astype(BF16), hi.astype(BF16)], axis=1)
    gu = jnp.dot(x, wgu_bf[...], preferred_element_type=F32) + bgu_ref[...]
    gate = jnp.minimum(gu[:, :D_MODEL], SWIGLU_LIMIT)
    up = jnp.clip(gu[:, D_MODEL:], -SWIGLU_LIMIT, SWIGLU_LIMIT)
    act = gate * _sigmoid(SWIGLU_ALPHA * gate) * (up + 1.0)
    o_ref[...] = _pack_rows(
        jnp.dot(act.astype(BF16), wd_bf[...], preferred_element_type=F32) + bd_ref[...])


def _experts(block_e, n_valid, xs, w_gu, b_gu, w_d, b_d):
    n_blocks = block_e.shape[0]
    grid_spec = pltpu.PrefetchScalarGridSpec(
        num_scalar_prefetch=2,
        grid=(n_blocks,),
        in_specs=[pl.BlockSpec((EXPERT_BLOCK, D_PACK), lambda j, be, nv: (j, 0)),
                  pl.BlockSpec((None, D_MODEL, 2 * D_MODEL), lambda j, be, nv: (be[j], 0, 0)),
                  pl.BlockSpec((None, 1, 2 * D_MODEL), lambda j, be, nv: (be[j], 0, 0)),
                  pl.BlockSpec((None, D_MODEL, D_MODEL), lambda j, be, nv: (be[j], 0, 0)),
                  pl.BlockSpec((None, 1, D_MODEL), lambda j, be, nv: (be[j], 0, 0))],
        out_specs=pl.BlockSpec((EXPERT_BLOCK, D_PACK), lambda j, be, nv: (j, 0)),
        scratch_shapes=[pltpu.VMEM((D_MODEL, 2 * D_MODEL), BF16),
                        pltpu.VMEM((D_MODEL, D_MODEL), BF16)],
    )
    return pl.pallas_call(
        _expert_kernel,
        grid_spec=grid_spec,
        out_shape=jax.ShapeDtypeStruct(xs.shape, jnp.uint32),
        compiler_params=pltpu.CompilerParams(
            dimension_semantics=("arbitrary",), vmem_limit_bytes=VMEM_LIMIT),
        name="experts",
    )(block_e, n_valid, xs, w_gu, b_gu, w_d, b_d)


COMBINE_TOKENS = 512
MOE_SPLITS = 2


def _combine_kernel(yg_ref, x1_ref, gate_ref, g2_ref, fg_ref, o_ref):
    gates = gate_ref[...]
    acc_lo = acc_hi = None
    for kk in range(TOP_K):
        lo, hi = _unpack_rows(yg_ref[kk * COMBINE_TOKENS:(kk + 1) * COMBINE_TOKENS, :])
        g = gates[:, kk:kk + 1]
        acc_lo = g * lo if acc_lo is None else acc_lo + g * lo
        acc_hi = g * hi if acc_hi is None else acc_hi + g * hi
    x2 = x1_ref[...] + g2_ref[...] * jnp.concatenate([acc_lo, acc_hi], axis=1)
    o_ref[...] = x2 * lax.rsqrt(jnp.mean(x2 * x2, axis=-1, keepdims=True) + NORM_EPS) * fg_ref[...]


def _combine_kernel_into(prev_ref, *refs):
    del prev_ref
    _combine_kernel(*refs)


def _combine(yg, x1, gates, gate2, final_g, seq, row0, t_total, prev):
    t = x1.shape[0]
    tm = COMBINE_TOKENS
    per_b = seq // tm
    blk0 = row0 // tm
    rows = lambda i: (i, 0)
    in_specs = [pl.BlockSpec((TOP_K * tm, D_PACK), rows),
                pl.BlockSpec((tm, D_MODEL), rows),
                pl.BlockSpec((tm, LANES), rows),
                pl.BlockSpec((None, 1, D_MODEL), lambda i: ((i + blk0) // per_b, 0, 0)),
                pl.BlockSpec((1, D_MODEL), lambda i: (0, 0))]
    args = (yg, x1, gates, gate2, final_g)
    if prev is not None:
        in_specs = [pl.BlockSpec(memory_space=pl.ANY)] + in_specs
        args = (prev,) + args
    return pl.pallas_call(
        _combine_kernel if prev is None else _combine_kernel_into,
        grid=(t // tm,),
        in_specs=in_specs,
        out_specs=pl.BlockSpec((tm, D_MODEL), lambda i: (i + blk0, 0)),
        out_shape=jax.ShapeDtypeStruct((t_total, D_MODEL), F32),
        input_output_aliases={} if prev is None else {0: 0},
        compiler_params=pltpu.CompilerParams(
            dimension_semantics=("parallel",), vmem_limit_bytes=VMEM_LIMIT),
        name="combine",
    )(*args)


def _moe(h2, idx, gates, rank, counts, x1, gate2, final_g, w_gu, b_gu, w_d, b_d, seq,
         row0, t_total, prev):
    t = h2.shape[0]
    n_slots = t * TOP_K
    n_blocks = -(-n_slots // EXPERT_BLOCK) + N_EXPERTS
    cap = n_blocks * EXPERT_BLOCK
    padded = (counts + EXPERT_BLOCK - 1) // EXPERT_BLOCK * EXPERT_BLOCK
    pad_ends = jnp.cumsum(padded)
    pad_starts = pad_ends - padded
    dest = pad_starts[idx] + rank
    block_starts = jnp.arange(n_blocks, dtype=jnp.int32) * EXPERT_BLOCK
    block_e = jnp.minimum(jnp.sum(block_starts[:, None] >= pad_ends[None, :], axis=1),
                          N_EXPERTS - 1).astype(jnp.int32)
    n_valid = jnp.clip(counts[block_e] - (block_starts - pad_starts[block_e]), 0, EXPERT_BLOCK)

    xs = _sc_scatter_rows(h2, dest, cap)
    yb = _experts(block_e, n_valid.astype(jnp.int32), xs, w_gu, b_gu, w_d, b_d)
    dest_blocks = dest.reshape(-1, COMBINE_TOKENS, TOP_K).transpose(0, 2, 1).reshape(-1)
    yg = _sc_gather_rows(dest_blocks, yb)
    return _combine(yg, x1, gates, gate2, final_g, seq, row0, t_total, prev)


def _layer(x, c_mod, norm1_g, w_in, mu_shift, w0, w2, a0, a2, g2, k_k, k_a, r_k, gn_w, gn_b, b_f,
           q_norm_g, k_norm_g, o_norm_g, w_out, norm2_g, w_router, b_router, w_gate_up,
           b_gate_up, w_down, b_down, final_g, tm_in, tm_out):
    bsz, seq, _ = x.shape
    shift1, scale1, gate1, shift2, scale2, gate2 = (
        m.reshape(bsz, 1, D_MODEL) for m in jnp.split(c_mod, 6, axis=-1))
    row = lambda v: v.reshape(1, -1)

    w_r = w_in[:, :RWKV_COLS].astype(BF16)
    w_x = w_in[:, RWKV_COLS:RWKV_COLS + FOX_MAIN].astype(BF16)
    w_f = jnp.pad(w_in[:, RWKV_COLS + FOX_MAIN:], ((0, 0), (0, LANES - N_HEADS)))
    b_fp = jnp.pad(b_f, (0, LANES - N_HEADS)).reshape(1, LANES)
    qk_gain = jnp.concatenate([jnp.tile(q_norm_g, N_HEADS) * HEAD_DIM ** -0.5,
                               jnp.tile(k_norm_g, N_HEADS)]).reshape(1, -1)
    p_r, p_x, k_bias, q_bias = _inproj(x, shift1, scale1, row(norm1_g), w_r, w_x, w_f, b_fp,
                                       qk_gain, tm_in)

    zeros = jnp.zeros((LANES - 64, D_GRP), F32)
    w2p = jnp.concatenate([w2, zeros], axis=0).astype(BF16)
    a2p = jnp.concatenate([zeros, a2], axis=0).astype(BF16)
    y_r = _rwkv(p_r, row(mu_shift), row(w0), w2p, row(a0), a2p, g2.astype(BF16), row(k_k),
                row(k_a), row(r_k), row(gn_w), row(gn_b))

    y_f = _fox(p_x, k_bias, q_bias, jnp.tile(o_norm_g, 2).reshape(1, LANES))

    t = bsz * seq
    w_rt = jnp.pad(w_router, ((0, 0), (0, LANES - N_EXPERTS)))
    b_rt = jnp.pad(b_router, (0, LANES - N_EXPERTS)).reshape(1, LANES)
    wo = w_out.astype(BF16)
    w_gu, w_d = w_gate_up, w_down
    b_gu, b_d = b_gate_up.reshape(N_EXPERTS, 1, -1), b_down.reshape(N_EXPERTS, 1, -1)
    t_part = t // MOE_SPLITS
    out = None
    for part in range(MOE_SPLITS):
        row0 = part * t_part
        x1, h2, idx, gates, rank, cnt = _outproj(
            x.reshape(t, D_MODEL), y_r.reshape(t, D_GRP), y_f.reshape(t, D_GRP), gate1, shift2,
            scale2, row(norm2_g), wo[:D_GRP], wo[D_GRP:], w_rt, b_rt, tm_out, seq, row0, t_part)
        counts = cnt[0, :N_EXPERTS].astype(jnp.int32)
        out = _moe(h2, idx[:, :TOP_K], gates, rank[:, :TOP_K], counts, x1, gate2, row(final_g),
                   w_gu, b_gu, w_d, b_d, seq, row0, t, out)
    return out.reshape(bsz, seq, D_MODEL)


def kernel(x, c, w_ada, b_ada, norm1_g, w_in, mu_shift, w0, w2, a0, a2, g2, k_k, k_a, r_k, gn_w,
           gn_b, b_f, q_norm_g, k_norm_g, o_norm_g, w_out, norm2_g, w_router, b_router, w_gate_up,
           b_gate_up, w_down, b_down, final_g):
    assert w_ada.shape[0] == 1, "single-layer block"
    c_mod = _adaln(c, w_ada[0], b_ada[0])
    return _layer(x, c_mod, norm1_g[0], w_in[0], mu_shift[0], w0[0], w2[0], a0[0], a2[0], g2[0],
                  k_k[0], k_a[0], r_k[0], gn_w[0], gn_b[0], b_f[0], q_norm_g[0], k_norm_g[0],
                  o_norm_g[0], w_out[0], norm2_g[0], w_router[0], b_router[0], w_gate_up[0],
                  b_gate_up[0], w_down[0], b_down[0], final_g,
                  tm_in=min(512, x.shape[1]), tm_out=min(1024, x.shape[1]))
```

```python
import functools

import jax
import jax.numpy as jnp
from jax import lax
from jax.experimental import pallas as pl
from jax.experimental.pallas import tpu as pltpu
from jax.experimental.pallas import tpu_sc as plsc

F32 = jnp.float32
BF16 = jnp.bfloat16
HIGHEST = lax.Precision.HIGHEST

D_MODEL = 1024
HEAD_DIM = 64
N_HEADS = 8
D_GRP = N_HEADS * HEAD_DIM
RWKV_COLS = 1792
LORA_OFF = 3 * D_GRP
GATE_OFF = LORA_OFF + 128
FOX_MAIN = 4 * D_GRP
N_EXPERTS = 32
TOP_K = 4
EXPERT_BLOCK = 512
CAST_COLS = 256
SWIGLU_ALPHA = 1.702
SWIGLU_LIMIT = 7.0
NORM_EPS = 1e-6
GN_EPS = 64e-5
LANES = 128
CHUNK = 64
FOX_SUB_KEYS = 512
HEADS_PER_SCAN = 4
SCAN_W = HEADS_PER_SCAN * HEAD_DIM
SEG_TERMS = 1
CUM_TERMS = 2
VMEM_LIMIT = 56 * 1024 * 1024


def _dot(a, b):
    return jnp.dot(a.astype(BF16), b.astype(BF16), preferred_element_type=F32)


def _dot_nt(a, b):
    return lax.dot_general(a.astype(BF16), b.astype(BF16), (((1,), (1,)), ((), ())),
                           preferred_element_type=F32)


def _dot_tn(a, b):
    return lax.dot_general(a.astype(BF16), b.astype(BF16), (((0,), (0,)), ((), ())),
                           preferred_element_type=F32)


def _fdot(a, b):
    return jnp.dot(a, b, precision=HIGHEST, preferred_element_type=F32)


def _split_dot(x, m, terms=2, left=False):
    acc = None
    rem = x
    for _ in range(terms):
        part = rem.astype(BF16)
        rem = rem - part.astype(F32)
        d = (jnp.dot(m, part, preferred_element_type=F32) if left
             else jnp.dot(part, m, preferred_element_type=F32))
        acc = d if acc is None else acc + d
    return acc


def _iota(shape, dim):
    return lax.broadcasted_iota(jnp.int32, shape, dim)


def _seg_reduce_mat(n):
    return (_iota((n, LANES), 0) // HEAD_DIM == _iota((n, LANES), 1)).astype(BF16)


def _seg_expand_mat(n):
    return (_iota((LANES, n), 1) // HEAD_DIM == _iota((LANES, n), 0)).astype(BF16)


def _tri(n, strict):
    r, c = _iota((n, n), 0), _iota((n, n), 1)
    return ((r > c) if strict else (r >= c)).astype(BF16)


D_PACK = D_MODEL // 2


def _pack_rows(x):
    lo = lax.bitcast_convert_type(x[:, :D_PACK].astype(BF16).astype(F32), jnp.uint32)
    hi = lax.bitcast_convert_type(x[:, D_PACK:].astype(BF16).astype(F32), jnp.uint32)
    return hi | (lo >> 16)


def _unpack_rows(p):
    lo = lax.bitcast_convert_type(p << 16, F32)
    hi = lax.bitcast_convert_type(p & jnp.uint32(0xFFFF0000), F32)
    return lo, hi


def _log_sigmoid(z):
    return jnp.minimum(z, 0.0) - jnp.log(1.0 + jnp.exp(-jnp.abs(z)))


def _sigmoid(z):
    return 1.0 / (1.0 + jnp.exp(-z))


def _adaln_kernel(c_ref, w_ref, b_ref, o_ref):
    c = c_ref[...]
    o_ref[...] = _fdot(c * _sigmoid(c), w_ref[...]) + b_ref[...]


def _adaln(c, w_ada, b_ada):
    bsz = c.shape[0]
    n_mod = w_ada.shape[1] // D_MODEL
    return pl.pallas_call(
        _adaln_kernel,
        grid=(n_mod,),
        in_specs=[pl.BlockSpec((bsz, D_MODEL), lambda j: (0, 0)),
                  pl.BlockSpec((D_MODEL, D_MODEL), lambda j: (0, j)),
                  pl.BlockSpec((1, D_MODEL), lambda j: (0, j))],
        out_specs=pl.BlockSpec((bsz, D_MODEL), lambda j: (0, j)),
        out_shape=jax.ShapeDtypeStruct((bsz, n_mod * D_MODEL), F32),
        name="adaln",
    )(c, w_ada, b_ada.reshape(1, -1))


def _inproj_kernel(x_ref, sh_ref, sc_ref, g_ref, wr_ref, wx_ref, wfh_ref, wfl_ref, bf_ref, qkg_ref,
                   pr_ref, px_ref, kb_ref, qb_ref, carry_ref):
    @pl.when(pl.program_id(1) == 0)
    def _():
        carry_ref[...] = jnp.zeros_like(carry_ref)

    x = x_ref[...]
    tm = x.shape[0]
    h = x * lax.rsqrt(jnp.mean(x * x, axis=-1, keepdims=True) + NORM_EPS) * g_ref[...]
    h = h * (1.0 + sc_ref[...]) + sh_ref[...]
    hb = h.astype(BF16)
    h_lo = (h - hb.astype(F32)).astype(BF16)

    pr_ref[...] = jnp.dot(hb, wr_ref[...], preferred_element_type=F32).astype(BF16)

    px = jnp.dot(hb, wx_ref[...], preferred_element_type=F32)
    qk = px[:, :2 * D_GRP]
    ss = _split_dot(qk * qk, _seg_reduce_mat(2 * D_GRP), SEG_TERMS)
    inv = lax.rsqrt(ss * (1.0 / HEAD_DIM) + NORM_EPS)
    qk = qk * _split_dot(inv, _seg_expand_mat(2 * D_GRP), SEG_TERMS) * qkg_ref[...]
    px_ref[:, :2 * D_GRP] = qk.astype(BF16)
    px_ref[:, 2 * D_GRP:] = px[:, 2 * D_GRP:].astype(BF16)

    z = (jnp.dot(hb, wfh_ref[...], preferred_element_type=F32)
         + jnp.dot(h_lo, wfh_ref[...], preferred_element_type=F32)
         + jnp.dot(hb, wfl_ref[...], preferred_element_type=F32)) + bf_ref[...]
    cum = _split_dot(_log_sigmoid(z), _tri(tm, False), terms=3, left=True) + carry_ref[...]
    carry_ref[...] = cum[tm - 1:tm, :]

    parts = []
    rem = cum
    for _ in range(3):
        part = rem.astype(BF16)
        rem = rem - part.astype(F32)
        parts.append(part)
    src, dst = _iota((LANES, LANES), 0), _iota((LANES, LANES), 1)

    def spread(offset):
        return sum(jnp.dot(part, ((dst == 8 * src + offset + t) & (src < N_HEADS)).astype(BF16),
                           preferred_element_type=F32) for t, part in enumerate(parts))

    slot = _iota((1, LANES), 1) % 8
    kb_ref[...] = (jnp.where((slot >= 3) & (slot < 6), 1.0, 0.0) - spread(0)).astype(BF16)
    qb_ref[...] = (jnp.where(slot < 3, 1.0, 0.0) + spread(3)).astype(BF16)


def _inproj(x, shift, scale, g, w_r, w_x, w_f, b_f, qk_gain, tm):
    w_f_hi = w_f.astype(BF16)
    w_f_lo = (w_f - w_f_hi.astype(F32)).astype(BF16)
    bsz, seq, _ = x.shape
    const = lambda b, s: (0, 0)
    return pl.pallas_call(
        _inproj_kernel,
        grid=(bsz, seq // tm),
        in_specs=[pl.BlockSpec((None, tm, D_MODEL), lambda b, s: (b, s, 0)),
                  pl.BlockSpec((None, 1, D_MODEL), lambda b, s: (b, 0, 0)),
                  pl.BlockSpec((None, 1, D_MODEL), lambda b, s: (b, 0, 0)),
                  pl.BlockSpec((1, D_MODEL), const),
                  pl.BlockSpec((D_MODEL, RWKV_COLS), const),
                  pl.BlockSpec((D_MODEL, FOX_MAIN), const),
                  pl.BlockSpec((D_MODEL, LANES), const),
                  pl.BlockSpec((D_MODEL, LANES), const),
                  pl.BlockSpec((1, LANES), const),
                  pl.BlockSpec((1, 2 * D_GRP), const)],
        out_specs=[pl.BlockSpec((None, tm, RWKV_COLS), lambda b, s: (b, s, 0)),
                   pl.BlockSpec((None, tm, FOX_MAIN), lambda b, s: (b, s, 0)),
                   pl.BlockSpec((None, tm, LANES), lambda b, s: (b, s, 0)),
                   pl.BlockSpec((None, tm, LANES), lambda b, s: (b, s, 0))],
        out_shape=[jax.ShapeDtypeStruct((bsz, seq, RWKV_COLS), BF16),
                   jax.ShapeDtypeStruct((bsz, seq, FOX_MAIN), BF16),
                   jax.ShapeDtypeStruct((bsz, seq, LANES), BF16),
                   jax.ShapeDtypeStruct((bsz, seq, LANES), BF16)],
        scratch_shapes=[pltpu.VMEM((1, LANES), F32)],
        compiler_params=pltpu.CompilerParams(
            dimension_semantics=("parallel", "arbitrary"), vmem_limit_bytes=VMEM_LIMIT),
        name="inproj",
    )(x, shift, scale, g, w_r, w_x, w_f_hi, w_f_lo, b_f, qk_gain)


_NN = (((1,), (0,)), ((), ()))
_NT = (((1,), (1,)), ((), ()))
_TN = (((0,), (0,)), ((), ()))
SCAN_N = HEADS_PER_SCAN * CHUNK
BATCH_PER_STEP = 8
INV_LEVELS = 5
M_HEAD, M_STRICT, M_INCL, M_EYE, M_BASE, M_OFF = 0, 1, 2, 3, 4, 5


def _bdot(a, b, dims):
    return lax.dot_general(a, b, dims, preferred_element_type=F32)


def _scan_masks():
    rr, cc = _iota((SCAN_N, SCAN_W), 0), _iota((SCAN_N, SCAN_W), 1)
    ri, ci = _iota((SCAN_N, SCAN_N), 0), _iota((SCAN_N, SCAN_N), 1)
    same = ri // CHUNK == ci // CHUNK
    masks = [rr // CHUNK == cc // HEAD_DIM, same & (ri > ci), same & (ri >= ci), ri == ci,
             (ri // 2 == ci // 2) & (ri > ci)]
    blk = 2
    while blk < CHUNK:
        masks.append((ri // (2 * blk) == ci // (2 * blk)) & (ri // blk != ci // blk) & (ri > ci))
        blk *= 2
    return jnp.stack(masks).astype(BF16)


def _rwkv_kernel(p_ref, masks_ref, mu_ref, w0_ref, w2_ref, a0_ref, a2_ref, g2_ref, kk_ref, ka_ref,
                 rk_ref, gnw_ref, gnb_ref, o_ref, last_ref, state_ref):
    @pl.when(pl.program_id(1) == 0)
    def _():
        last_ref[...] = jnp.zeros_like(last_ref)
        state_ref[...] = jnp.zeros_like(state_ref)

    mu, w0, w2, a0, a2, g2, k_k, k_a, r_k, gn_w, gn_b = (
        ref[...] for ref in (mu_ref, w0_ref, w2_ref, a0_ref, a2_ref, g2_ref, kk_ref, ka_ref,
                             rk_ref, gnw_ref, gnb_ref))
    rows = BATCH_PER_STEP * CHUNK
    p = p_ref[...].astype(F32).reshape(rows, RWKV_COLS)
    row_id = _iota((rows, 1), 0)
    prev = pltpu.roll(p, 1, axis=0)
    for bb in range(BATCH_PER_STEP):
        prev = jnp.where(row_id == bb * CHUNK, last_ref[bb], prev)
        last_ref[bb] = p[(bb + 1) * CHUNK - 1:(bb + 1) * CHUNK, :]
    pf = p + mu * (prev - p)
    r = pf[:, 0:D_GRP]
    k = pf[:, D_GRP:2 * D_GRP]
    v = pf[:, 2 * D_GRP:3 * D_GRP]
    lora = pf[:, LORA_OFF:GATE_OFF]
    gd = pf[:, GATE_OFF:RWKV_COLS]

    wlog = w0 + _dot(jnp.tanh(lora), w2)
    neg = -wlog
    softplus = jnp.maximum(neg, 0.0) + jnp.log(1.0 + jnp.exp(-jnp.abs(neg)))
    ld = -jnp.exp(-softplus - 0.5)
    a = _sigmoid(a0 + _dot(lora, a2))
    g = _dot(_sigmoid(gd), g2)

    red, exp_m = _seg_reduce_mat(D_GRP), _seg_expand_mat(D_GRP)
    kk = k * k_k
    n2 = _split_dot(kk * kk, red, SEG_TERMS)
    kk = kk * _split_dot(1.0 / jnp.maximum(jnp.sqrt(n2), 1e-12), exp_m, SEG_TERMS)
    k2 = k * (1.0 + (a - 1.0) * k_a)

    tr, tc = _iota((rows, rows), 0), _iota((rows, rows), 1)
    tri = ((tr >= tc) & (tr // CHUNK == tc // CHUNK)).astype(BF16)
    cl = _split_dot(ld, tri, terms=CUM_TERMS, left=True)
    cl_end = jnp.concatenate(
        [jnp.broadcast_to(cl[(bb + 1) * CHUNK - 1:(bb + 1) * CHUNK, :], (CHUNK, D_GRP))
         for bb in range(BATCH_PER_STEP)], axis=0)
    e_in = jnp.exp(cl)
    e_out = jnp.exp(-cl)
    e_rem = jnp.exp(cl_end - cl)
    p_end = jnp.exp(cl_end)
    kka = kk * a
    ops = [(-kk * jnp.exp(cl - ld)).astype(BF16), (kka * e_out).astype(BF16),
           (k2 * e_out).astype(BF16), (r * e_in).astype(BF16), v.astype(BF16),
           (kka * e_rem).astype(BF16), (k2 * e_rem).astype(BF16)]

    chains = [(bb, grp) for bb in range(BATCH_PER_STEP)
              for grp in range(N_HEADS // HEADS_PER_SCAN)]
    head_mask = masks_ref[M_HEAD]
    strict, incl = masks_ref[M_STRICT], masks_ref[M_INCL]

    def stacked(op, bb, grp):
        part = op[bb * CHUNK:(bb + 1) * CHUNK, grp * SCAN_W:(grp + 1) * SCAN_W]
        return jnp.concatenate([part] * HEADS_PER_SCAN, axis=0) * head_mask

    xs = [[stacked(op, bb, grp) for op in ops] for bb, grp in chains]
    st = [state_ref[bb, grp] for bb, grp in chains]
    sb = [s.astype(BF16) for s in st]
    nab = [_bdot(x[0], x[1], _NT).astype(BF16) for x in xs]
    aak = [_bdot(x[0], x[2], _NT).astype(BF16) * strict for x in xs]
    arb = [_bdot(x[3], x[1], _NT).astype(BF16) * incl for x in xs]
    ark = [_bdot(x[3], x[2], _NT).astype(BF16) * incl for x in xs]
    t_inv = [masks_ref[M_EYE] + n * masks_ref[M_BASE] for n in nab]
    for lvl in range(INV_LEVELS):
        half = [_bdot(t, n * masks_ref[M_OFF + lvl], _NN).astype(BF16) for t, n in zip(t_inv, nab)]
        t_inv = [t + _bdot(h, t, _NN).astype(BF16) for t, h in zip(t_inv, half)]
    rhs = [(_bdot(x[0], s, _NT) + _bdot(k, x[4], _NN)).astype(BF16)
           for x, s, k in zip(xs, sb, aak)]
    sa = [_bdot(t, h, _NN).astype(BF16) for t, h in zip(t_inv, rhs)]
    ys = [_bdot(x[3], s, _NT) + _bdot(b, u, _NN) + _bdot(k, x[4], _NN)
          for x, s, b, u, k in zip(xs, sb, arb, sa, ark)]
    for (bb, grp), x, s, u in zip(chains, xs, st, sa):
        decay = p_end[bb * CHUNK:bb * CHUNK + 1, grp * SCAN_W:(grp + 1) * SCAN_W]
        state_ref[bb, grp] = s * decay + _bdot(u, x[5], _TN) + _bdot(x[4], x[6], _TN)
    ys = [y[0:CHUNK] + y[CHUNK:2 * CHUNK] + y[2 * CHUNK:3 * CHUNK] + y[3 * CHUNK:4 * CHUNK]
          for y in ys]
    n_grp = N_HEADS // HEADS_PER_SCAN
    y = jnp.concatenate([jnp.concatenate(ys[bb * n_grp:(bb + 1) * n_grp], axis=1)
                         for bb in range(BATCH_PER_STEP)], axis=0)

    mean = _split_dot(_split_dot(y, red, SEG_TERMS) * (1.0 / HEAD_DIM), exp_m, SEG_TERMS)
    d = y - mean
    var = _split_dot(d * d, red, SEG_TERMS) * (1.0 / HEAD_DIM)
    yn = d * _split_dot(lax.rsqrt(var + GN_EPS), exp_m, SEG_TERMS) * gn_w + gn_b
    bonus = _split_dot(_split_dot(r * k2 * r_k, red, SEG_TERMS), exp_m, SEG_TERMS) * v
    o_ref[...] = ((yn + bonus) * g).astype(BF16).reshape(BATCH_PER_STEP, CHUNK, D_GRP)


def _rwkv(p_r, mu, w0, w2p, a0, a2p, g2, k_k, k_a, r_k, gn_w, gn_b):
    bsz, seq, _ = p_r.shape
    assert bsz % BATCH_PER_STEP == 0
    masks = _scan_masks()
    const = lambda b, s: (0, 0)
    vec = pl.BlockSpec((1, D_GRP), const)
    return pl.pallas_call(
        _rwkv_kernel,
        grid=(bsz // BATCH_PER_STEP, seq // CHUNK),
        in_specs=[pl.BlockSpec((BATCH_PER_STEP, CHUNK, RWKV_COLS), lambda b, s: (b, s, 0)),
                  pl.BlockSpec(masks.shape, lambda b, s: (0, 0, 0)),
                  pl.BlockSpec((1, RWKV_COLS), const),
                  vec, pl.BlockSpec((LANES, D_GRP), const),
                  vec, pl.BlockSpec((LANES, D_GRP), const),
                  pl.BlockSpec((LANES, D_GRP), const),
                  vec, vec, vec, vec, vec],
        out_specs=pl.BlockSpec((BATCH_PER_STEP, CHUNK, D_GRP), lambda b, s: (b, s, 0)),
        out_shape=jax.ShapeDtypeStruct((bsz, seq, D_GRP), BF16),
        scratch_shapes=[pltpu.VMEM((BATCH_PER_STEP, 1, RWKV_COLS), F32),
                        pltpu.VMEM((BATCH_PER_STEP, N_HEADS // HEADS_PER_SCAN, SCAN_W, SCAN_W), F32)],
        compiler_params=pltpu.CompilerParams(
            dimension_semantics=("parallel", "arbitrary"), vmem_limit_bytes=VMEM_LIMIT),
        name="rwkv",
    )(p_r, masks, mu, w0, w2p, a0, a2p, g2, k_k, k_a, r_k, gn_w, gn_b)


def _fox_kernel(q_ref, qb_ref, k_ref, kb_ref, vt_ref, og_ref, ong_ref, o_ref, m_ref, l_ref, acc_ref,
                *, seq):
    hp = pl.program_id(1)
    lane = _iota((1, LANES), 1)
    q = q_ref[...]
    qb = qb_ref[...]
    zero = jnp.zeros_like(q)
    qcat = [jnp.concatenate([jnp.where(lane // HEAD_DIM == hh, q, zero),
                             jnp.where(lane // 8 == hp * 2 + hh, qb, zero)], axis=1)
            for hh in range(2)]
    keys = min(FOX_SUB_KEYS, seq)
    n_sub = seq // keys
    diag = _iota((keys, keys), 1) >= _iota((keys, keys), 0)

    m_ref[...] = jnp.full(m_ref.shape, -jnp.inf, F32)
    l_ref[...] = jnp.zeros(l_ref.shape, F32)
    acc_ref[...] = jnp.zeros(acc_ref.shape, F32)

    def scores(s):
        lo = s * keys
        kcat = jnp.concatenate([k_ref[lo:lo + keys, :], kb_ref[lo:lo + keys, :]], axis=1)
        return [lax.dot_general(kcat, qc[lo:, :], _NT, preferred_element_type=F32)
                for qc in qcat]

    pending = scores(0)
    for s in range(n_sub):
        lo = s * keys
        nxt = scores(s + 1) if s + 1 < n_sub else None
        vt = vt_ref[:, lo:lo + keys]
        sts = [jnp.concatenate([jnp.where(diag, st[:, :keys], -jnp.inf), st[:, keys:]], axis=1)
               if st.shape[1] > keys else jnp.where(diag, st, -jnp.inf) for st in pending]
        m_old = [m_ref[hh, :, lo:] for hh in range(2)]
        m_new = [jnp.maximum(m, jnp.max(st, axis=0, keepdims=True)) for m, st in zip(m_old, sts)]
        pts = [jnp.exp(st - m) for st, m in zip(sts, m_new)]
        pvs = [jnp.dot(vt, pt.astype(BF16), preferred_element_type=F32) for pt in pts]
        for hh in range(2):
            alpha = jnp.exp(m_old[hh] - m_new[hh])
            m_ref[hh, :, lo:] = m_new[hh]
            l_ref[hh, :, lo:] = alpha * l_ref[hh, :, lo:] + jnp.sum(pts[hh], axis=0, keepdims=True)
            acc_ref[hh, :, lo:] = (alpha * acc_ref[hh, :, lo:]
                                   + pvs[hh][hh * HEAD_DIM:(hh + 1) * HEAD_DIM, :])
        pending = nxt

    outs = []
    for hh in range(2):
        o = acc_ref[hh] / l_ref[hh]
        outs.append(o * lax.rsqrt(jnp.mean(o * o, axis=0, keepdims=True) + NORM_EPS))
    o = jnp.concatenate(outs, axis=0).T
    o_ref[...] = (o * ong_ref[...] * _sigmoid(og_ref[...].astype(F32))).astype(BF16)


def _fox(p_x, k_bias, q_bias, o_gain):
    bsz, seq, _ = p_x.shape
    npair = N_HEADS // 2
    v_t = jnp.transpose(p_x[:, :, 2 * D_GRP:3 * D_GRP], (0, 2, 1))
    return pl.pallas_call(
        functools.partial(_fox_kernel, seq=seq),
        grid=(bsz, npair),
        in_specs=[pl.BlockSpec((None, seq, LANES), lambda b, h: (b, 0, h)),
                  pl.BlockSpec((None, seq, LANES), lambda b, h: (b, 0, 0)),
                  pl.BlockSpec((None, seq, LANES), lambda b, h: (b, 0, npair + h)),
                  pl.BlockSpec((None, seq, LANES), lambda b, h: (b, 0, 0)),
                  pl.BlockSpec((None, LANES, seq), lambda b, h: (b, h, 0)),
                  pl.BlockSpec((None, seq, LANES), lambda b, h: (b, 0, 3 * npair + h)),
                  pl.BlockSpec((1, LANES), lambda b, h: (0, 0))],
        out_specs=pl.BlockSpec((None, seq, LANES), lambda b, h: (b, 0, h)),
        out_shape=jax.ShapeDtypeStruct((bsz, seq, D_GRP), BF16),
        scratch_shapes=[pltpu.VMEM((2, 1, seq), F32), pltpu.VMEM((2, 1, seq), F32),
                        pltpu.VMEM((2, HEAD_DIM, seq), F32)],
        compiler_params=pltpu.CompilerParams(
            dimension_semantics=("parallel", "parallel"), vmem_limit_bytes=VMEM_LIMIT),
        name="fox",
    )(p_x, q_bias, p_x, k_bias, v_t, p_x, o_gain)


def _outproj_kernel(x_ref, yr_ref, yf_ref, g1_ref, sh_ref, sc_ref, ng_ref, wor_ref, wof_ref,
                    wrt_ref, wrl_ref, brt_ref, x1_ref, h2_ref, idx_ref, gate_ref, rank_ref, cnt_ref,
                    carry_ref):
    @pl.when(pl.program_id(0) == 0)
    def _():
        carry_ref[...] = jnp.zeros_like(carry_ref)

    y = (jnp.dot(yr_ref[...], wor_ref[...], preferred_element_type=F32)
         + jnp.dot(yf_ref[...], wof_ref[...], preferred_element_type=F32))
    x1 = x_ref[...] + g1_ref[...] * y
    x1_ref[...] = x1
    tm = x1.shape[0]
    h = x1 * lax.rsqrt(jnp.mean(x1 * x1, axis=-1, keepdims=True) + NORM_EPS) * ng_ref[...]
    h2 = h * (1.0 + sc_ref[...]) + sh_ref[...]
    h2_ref[...] = _pack_rows(h2)

    h_hi = h2.astype(BF16)
    h_lo = (h2 - h_hi.astype(F32)).astype(BF16)
    logits = (lax.dot_general(wrt_ref[...], h_hi, _NT, preferred_element_type=F32)
              + lax.dot_general(wrt_ref[...], h_lo, _NT, preferred_element_type=F32)
              + lax.dot_general(wrl_ref[...], h_hi, _NT, preferred_element_type=F32))
    lg = logits[:N_EXPERTS, :] + brt_ref[...]
    expert = _iota((N_EXPERTS, tm), 0)
    picks = []
    hot_sum = jnp.zeros((N_EXPERTS, tm), F32)
    for _ in range(TOP_K):
        m = jnp.max(lg, axis=0, keepdims=True)
        sel = jnp.min(jnp.where(lg == m, expert, N_EXPERTS), axis=0, keepdims=True)
        hot = expert == sel
        picks.append((m, sel, hot))
        hot_sum = hot_sum + hot.astype(F32)
        lg = jnp.where(hot, -jnp.inf, lg)
    es = [jnp.exp(m - picks[0][0]) for m, _, _ in picks]
    den = es[0] + es[1] + es[2] + es[3]

    earlier = (_iota((tm, tm), 0) < _iota((tm, tm), 1)).astype(BF16)
    before = jnp.dot(hot_sum.astype(BF16), earlier, preferred_element_type=F32) + carry_ref[...]
    ranks = [jnp.sum(jnp.where(hot, before, 0.0), axis=0, keepdims=True).astype(jnp.int32)
             for _, _, hot in picks]
    pad_i = jnp.zeros((8 - TOP_K, tm), jnp.int32)
    idx_ref[...] = jnp.concatenate([sel for _, sel, _ in picks] + [pad_i], axis=0)
    gate_ref[...] = jnp.concatenate([e / den for e in es] + [pad_i.astype(F32)], axis=0)
    rank_ref[...] = jnp.concatenate(ranks + [pad_i], axis=0)
    carry_ref[...] = carry_ref[...] + jnp.sum(hot_sum, axis=1, keepdims=True)
    cnt_ref[...] = jnp.broadcast_to(carry_ref[...], cnt_ref.shape)


def _outproj(x2d, y_r, y_f, gate1, shift2, scale2, norm_g, wo_r, wo_f, w_rt, b_rt, tm, seq,
             row0, t):
    w_rt_hi = w_rt.astype(BF16)
    w_rt_lo = (w_rt - w_rt_hi.astype(F32)).astype(BF16)
    per_b = seq // tm
    blk0 = row0 // tm
    const = lambda i: (0, 0)
    rows = lambda i: (i, 0)
    rows_in = lambda i: (i + blk0, 0)
    mod = pl.BlockSpec((None, 1, D_MODEL), lambda i: ((i + blk0) // per_b, 0, 0))
    return pl.pallas_call(
        _outproj_kernel,
        grid=(t // tm,),
        in_specs=[pl.BlockSpec((tm, D_MODEL), rows_in),
                  pl.BlockSpec((tm, D_GRP), rows_in),
                  pl.BlockSpec((tm, D_GRP), rows_in),
                  mod, mod, mod,
                  pl.BlockSpec((1, D_MODEL), const),
                  pl.BlockSpec((D_GRP, D_MODEL), const),
                  pl.BlockSpec((D_GRP, D_MODEL), const),
                  pl.BlockSpec((LANES, D_MODEL), const),
                  pl.BlockSpec((LANES, D_MODEL), const),
                  pl.BlockSpec((N_EXPERTS, 1), const)],
        out_specs=[pl.BlockSpec((tm, D_MODEL), rows),
                   pl.BlockSpec((tm, D_PACK), rows),
                   pl.BlockSpec((8, tm), lambda i: (0, i)),
                   pl.BlockSpec((8, tm), lambda i: (0, i)),
                   pl.BlockSpec((8, tm), lambda i: (0, i)),
                   pl.BlockSpec((N_EXPERTS, LANES), const)],
        out_shape=[jax.ShapeDtypeStruct((t, D_MODEL), F32),
                   jax.ShapeDtypeStruct((t, D_PACK), jnp.uint32),
                   jax.ShapeDtypeStruct((8, t), jnp.int32),
                   jax.ShapeDtypeStruct((8, t), F32),
                   jax.ShapeDtypeStruct((8, t), jnp.int32),
                   jax.ShapeDtypeStruct((N_EXPERTS, LANES), F32)],
        scratch_shapes=[pltpu.VMEM((N_EXPERTS, 1), F32)],
        compiler_params=pltpu.CompilerParams(
            dimension_semantics=("arbitrary",), vmem_limit_bytes=VMEM_LIMIT),
        name="outproj",
    )(x2d, y_r, y_f, gate1, shift2, scale2, norm_g, wo_r, wo_f, w_rt_hi, w_rt_lo, b_rt)


SC_CORES = 2
SC_SUBCORES = 16
SC_ROWS = 64


def _sc_gather_rows(idx, src):
    n_workers = SC_CORES * SC_SUBCORES
    m = idx.shape[0]
    d = src.shape[1]
    assert m % (n_workers * SC_ROWS) == 0
    n_chunks = m // (n_workers * SC_ROWS)
    mesh = plsc.VectorSubcoreMesh(core_axis_name="c", subcore_axis_name="s")

    @functools.partial(
        pl.kernel, mesh=mesh,
        out_type=jax.ShapeDtypeStruct((m, d), src.dtype),
        scratch_types=[pltpu.VMEM((n_chunks, SC_ROWS), jnp.int32),
                       pltpu.VMEM((SC_ROWS, d), src.dtype),
                       pltpu.SemaphoreType.DMA],
        name="sc_gather")
    def gather(src_hbm, idx_hbm, out_hbm, idx_v, rows_v, sem):
        wid = lax.axis_index("s") * SC_CORES + lax.axis_index("c")
        pltpu.sync_copy(idx_hbm.at[wid], idx_v)

        @pl.loop(0, n_chunks)
        def _(j):
            pltpu.async_copy(src_hbm.at[idx_v.at[j]], rows_v, sem).wait()
            pltpu.sync_copy(rows_v, out_hbm.at[pl.ds((wid * n_chunks + j) * SC_ROWS, SC_ROWS)])

    return gather(src, idx.reshape(n_workers, n_chunks, SC_ROWS))


def _sc_scatter_rows(src, dest, n_out):
    n_workers = SC_CORES * SC_SUBCORES
    t, d = src.shape
    n_slot = dest.shape[1]
    assert t % (n_workers * SC_ROWS) == 0
    n_chunks = t // (n_workers * SC_ROWS)
    mesh = plsc.VectorSubcoreMesh(core_axis_name="c", subcore_axis_name="s")
    idx = dest.reshape(n_workers, n_chunks, SC_ROWS, n_slot).transpose(0, 1, 3, 2)
    idx = idx.reshape(n_workers, n_chunks * n_slot, SC_ROWS)

    @functools.partial(
        pl.kernel, mesh=mesh,
        out_type=jax.ShapeDtypeStruct((n_out, d), src.dtype),
        scratch_types=[pltpu.VMEM((n_chunks * n_slot, SC_ROWS), jnp.int32),
                       pltpu.VMEM((SC_ROWS, d), src.dtype)],
        name="sc_scatter")
    def scatter(src_hbm, idx_hbm, out_hbm, idx_v, rows_v):
        wid = lax.axis_index("s") * SC_CORES + lax.axis_index("c")
        pltpu.sync_copy(idx_hbm.at[wid], idx_v)

        @pl.loop(0, n_chunks)
        def _(j):
            pltpu.sync_copy(src_hbm.at[pl.ds((wid * n_chunks + j) * SC_ROWS, SC_ROWS)], rows_v)
            for k in range(n_slot):
                pltpu.sync_copy(rows_v, out_hbm.at[idx_v.at[j * n_slot + k]])

    return scatter(src, idx)


def _expert_kernel(be_ref, nv_ref, x_ref, wgu_ref, bgu_ref, wd_ref, bd_ref, o_ref, wgu_bf, wd_bf):
    j = pl.program_id(0)

    @pl.when((j == 0) | (be_ref[j] != be_ref[jnp.maximum(j - 1, 0)]))
    def _():
        for c in range(0, 2 * D_MODEL, CAST_COLS):
            wgu_bf[:, c:c + CAST_COLS] = wgu_ref[:, c:c + CAST_COLS].astype(BF16)
        for c in range(0, D_MODEL, CAST_COLS):
            wd_bf[:, c:c + CAST_COLS] = wd_ref[:, c:c + CAST_COLS].astype(BF16)

    valid = _iota((EXPERT_BLOCK, 1), 0) < nv_ref[j]
    lo, hi = _unpack_rows(jnp.where(valid, x_ref[...], jnp.uint32(0)))
    x = jnp.concatenate([lo.astype(BF16), hi.astype(BF16)], axis=1)
    gu = jnp.dot(x, wgu_bf[...], preferred_element_type=F32) + bgu_ref[...]
    gate = jnp.minimum(gu[:, :D_MODEL], SWIGLU_LIMIT)
    up = jnp.clip(gu[:, D_MODEL:], -SWIGLU_LIMIT, SWIGLU_LIMIT)
    act = gate * _sigmoid(SWIGLU_ALPHA * gate) * (up + 1.0)
    o_ref[...] = _pack_rows(
        jnp.dot(act.astype(BF16), wd_bf[...], preferred_element_type=F32) + bd_ref[...])


def _experts(block_e, n_valid, xs, w_gu, b_gu, w_d, b_d):
    n_blocks = block_e.shape[0]
    grid_spec = pltpu.PrefetchScalarGridSpec(
        num_scalar_prefetch=2,
        grid=(n_blocks,),
        in_specs=[pl.BlockSpec((EXPERT_BLOCK, D_PACK), lambda j, be, nv: (j, 0)),
                  pl.BlockSpec((None, D_MODEL, 2 * D_MODEL), lambda j, be, nv: (be[j], 0, 0)),
                  pl.BlockSpec((None, 1, 2 * D_MODEL), lambda j, be, nv: (be[j], 0, 0)),
                  pl.BlockSpec((None, D_MODEL, D_MODEL), lambda j, be, nv: (be[j], 0, 0)),
                  pl.BlockSpec((None, 1, D_MODEL), lambda j, be, nv: (be[j], 0, 0))],
        out_specs=pl.BlockSpec((EXPERT_BLOCK, D_PACK), lambda j, be, nv: (j, 0)),
        scratch_shapes=[pltpu.VMEM((D_MODEL, 2 * D_MODEL), BF16),
                        pltpu.VMEM((D_MODEL, D_MODEL), BF16)],
    )
    return pl.pallas_call(
        _expert_kernel,
        grid_spec=grid_spec,
        out_shape=jax.ShapeDtypeStruct(xs.shape, jnp.uint32),
        compiler_params=pltpu.CompilerParams(
            dimension_semantics=("arbitrary",), vmem_limit_bytes=VMEM_LIMIT),
        name="experts",
    )(block_e, n_valid, xs, w_gu, b_gu, w_d, b_d)


COMBINE_TOKENS = 512
MOE_SPLITS = 2


def _combine_kernel(yg_ref, x1_ref, gate_ref, g2_ref, fg_ref, o_ref):
    gates = gate_ref[...]
    acc_lo = acc_hi = None
    for kk in range(TOP_K):
        lo, hi = _unpack_rows(yg_ref[kk * COMBINE_TOKENS:(kk + 1) * COMBINE_TOKENS, :])
        g = gates[:, kk:kk + 1]
        acc_lo = g * lo if acc_lo is None else acc_lo + g * lo
        acc_hi = g * hi if acc_hi is None else acc_hi + g * hi
    x2 = x1_ref[...] + g2_ref[...] * jnp.concatenate([acc_lo, acc_hi], axis=1)
    o_ref[...] = x2 * lax.rsqrt(jnp.mean(x2 * x2, axis=-1, keepdims=True) + NORM_EPS) * fg_ref[...]


def _combine_kernel_into(prev_ref, *refs):
    del prev_ref
    _combine_kernel(*refs)


def _combine(yg, x1, gates, gate2, final_g, seq, row0, t_total, prev):
    t = x1.shape[0]
    tm = COMBINE_TOKENS
    per_b = seq // tm
    blk0 = row0 // tm
    rows = lambda i: (i, 0)
    in_specs = [pl.BlockSpec((TOP_K * tm, D_PACK), rows),
                pl.BlockSpec((tm, D_MODEL), rows),
                pl.BlockSpec((tm, LANES), rows),
                pl.BlockSpec((None, 1, D_MODEL), lambda i: ((i + blk0) // per_b, 0, 0)),
                pl.BlockSpec((1, D_MODEL), lambda i: (0, 0))]
    args = (yg, x1, gates, gate2, final_g)
    if prev is not None:
        in_specs = [pl.BlockSpec(memory_space=pl.ANY)] + in_specs
        args = (prev,) + args
    return pl.pallas_call(
        _combine_kernel if prev is None else _combine_kernel_into,
        grid=(t // tm,),
        in_specs=in_specs,
        out_specs=pl.BlockSpec((tm, D_MODEL), lambda i: (i + blk0, 0)),
        out_shape=jax.ShapeDtypeStruct((t_total, D_MODEL), F32),
        input_output_aliases={} if prev is None else {0: 0},
        compiler_params=pltpu.CompilerParams(
            dimension_semantics=("parallel",), vmem_limit_bytes=VMEM_LIMIT),
        name="combine",
    )(*args)


def _moe(h2, idx, gates, rank, counts, x1, gate2, final_g, w_gu, b_gu, w_d, b_d, seq,
         row0, t_total, prev):
    t = h2.shape[0]
    n_slots = t * TOP_K
    n_blocks = -(-n_slots // EXPERT_BLOCK) + N_EXPERTS
    cap = n_blocks * EXPERT_BLOCK
    padded = (counts + EXPERT_BLOCK - 1) // EXPERT_BLOCK * EXPERT_BLOCK
    pad_ends = jnp.cumsum(padded)
    pad_starts = pad_ends - padded
    dest = pad_starts[idx] + rank
    block_starts = jnp.arange(n_blocks, dtype=jnp.int32) * EXPERT_BLOCK
    block_e = jnp.minimum(jnp.sum(block_starts[:, None] >= pad_ends[None, :], axis=1),
                          N_EXPERTS - 1).astype(jnp.int32)
    n_valid = jnp.clip(counts[block_e] - (block_starts - pad_starts[block_e]), 0, EXPERT_BLOCK)

    xs = _sc_scatter_rows(h2, dest, cap)
    yb = _experts(block_e, n_valid.astype(jnp.int32), xs, w_gu, b_gu, w_d, b_d)
    dest_blocks = dest.reshape(-1, COMBINE_TOKENS, TOP_K).transpose(0, 2, 1).reshape(-1)
    yg = _sc_gather_rows(dest_blocks, yb)
    return _combine(yg, x1, gates, gate2, final_g, seq, row0, t_total, prev)


def _layer(x, c_mod, norm1_g, w_in, mu_shift, w0, w2, a0, a2, g2, k_k, k_a, r_k, gn_w, gn_b, b_f,
           q_norm_g, k_norm_g, o_norm_g, w_out, norm2_g, w_router, b_router, w_gate_up,
           b_gate_up, w_down, b_down, final_g, tm_in, tm_out):
    bsz, seq, _ = x.shape
    shift1, scale1, gate1, shift2, scale2, gate2 = (
        m.reshape(bsz, 1, D_MODEL) for m in jnp.split(c_mod, 6, axis=-1))
    row = lambda v: v.reshape(1, -1)

    w_r = w_in[:, :RWKV_COLS].astype(BF16)
    w_x = w_in[:, RWKV_COLS:RWKV_COLS + FOX_MAIN].astype(BF16)
    w_f = jnp.pad(w_in[:, RWKV_COLS + FOX_MAIN:], ((0, 0), (0, LANES - N_HEADS)))
    b_fp = jnp.pad(b_f, (0, LANES - N_HEADS)).reshape(1, LANES)
    qk_gain = jnp.concatenate([jnp.tile(q_norm_g, N_HEADS) * HEAD_DIM ** -0.5,
                               jnp.tile(k_norm_g, N_HEADS)]).reshape(1, -1)
    p_r, p_x, k_bias, q_bias = _inproj(x, shift1, scale1, row(norm1_g), w_r, w_x, w_f, b_fp,
                                       qk_gain, tm_in)

    zeros = jnp.zeros((LANES - 64, D_GRP), F32)
    w2p = jnp.concatenate([w2, zeros], axis=0).astype(BF16)
    a2p = jnp.concatenate([zeros, a2], axis=0).astype(BF16)
    y_r = _rwkv(p_r, row(mu_shift), row(w0), w2p, row(a0), a2p, g2.astype(BF16), row(k_k),
                row(k_a), row(r_k), row(gn_w), row(gn_b))

    y_f = _fox(p_x, k_bias, q_bias, jnp.tile(o_norm_g, 2).reshape(1, LANES))

    t = bsz * seq
    w_rt = jnp.pad(w_router.T, ((0, LANES - N_EXPERTS), (0, 0)))
    b_rt = b_router.reshape(N_EXPERTS, 1)
    wo = w_out.astype(BF16)
    w_gu, w_d = w_gate_up, w_down
    b_gu, b_d = b_gate_up.reshape(N_EXPERTS, 1, -1), b_down.reshape(N_EXPERTS, 1, -1)
    t_part = t // MOE_SPLITS
    out = None
    for part in range(MOE_SPLITS):
        row0 = part * t_part
        x1, h2, idx, gates, rank, cnt = _outproj(
            x.reshape(t, D_MODEL), y_r.reshape(t, D_GRP), y_f.reshape(t, D_GRP), gate1, shift2,
            scale2, row(norm2_g), wo[:D_GRP], wo[D_GRP:], w_rt, b_rt, tm_out, seq, row0, t_part)
        counts = cnt[:, 0].astype(jnp.int32)
        gates = jnp.pad(gates[:TOP_K].T, ((0, 0), (0, LANES - TOP_K)))
        out = _moe(h2, idx[:TOP_K].T, gates, rank[:TOP_K].T, counts, x1, gate2, row(final_g),
                   w_gu, b_gu, w_d, b_d, seq, row0, t, out)
    return out.reshape(bsz, seq, D_MODEL)


def kernel(x, c, w_ada, b_ada, norm1_g, w_in, mu_shift, w0, w2, a0, a2, g2, k_k, k_a, r_k, gn_w,
           gn_b, b_f, q_norm_g, k_norm_g, o_norm_g, w_out, norm2_g, w_router, b_router, w_gate_up,
           b_gate_up, w_down, b_down, final_g):
    assert w_ada.shape[0] == 1, "single-layer block"
    c_mod = _adaln(c, w_ada[0], b_ada[0])
    return _layer(x, c_mod, norm1_g[0], w_in[0], mu_shift[0], w0[0], w2[0], a0[0], a2[0], g2[0],
                  k_k[0], k_a[0], r_k[0], gn_w[0], gn_b[0], b_f[0], q_norm_g[0], k_norm_g[0],
                  o_norm_g[0], w_out[0], norm2_g[0], w_router[0], b_router[0], w_gate_up[0],
                  b_gate_up[0], w_down[0], b_down[0], final_g,
                  tm_in=min(512, x.shape[1]), tm_out=min(1024, x.shape[1]))
```

```python
import functools

import jax
import jax.numpy as jnp
from jax import lax
from jax.experimental import pallas as pl
from jax.experimental.pallas import tpu as pltpu
from jax.experimental.pallas import tpu_sc as plsc

F32 = jnp.float32
BF16 = jnp.bfloat16
HIGHEST = lax.Precision.HIGHEST

D_MODEL = 1024
HEAD_DIM = 64
N_HEADS = 8
D_GRP = N_HEADS * HEAD_DIM
RWKV_COLS = 1792
LORA_OFF = 3 * D_GRP
GATE_OFF = LORA_OFF + 128
FOX_MAIN = 4 * D_GRP
N_EXPERTS = 32
TOP_K = 4
EXPERT_BLOCK = 512
SWIGLU_ALPHA = 1.702
SWIGLU_LIMIT = 7.0
NORM_EPS = 1e-6
GN_EPS = 64e-5
LANES = 128
CHUNK = 64
FOX_SUB_KEYS = 512
HEADS_PER_SCAN = 4
SCAN_W = HEADS_PER_SCAN * HEAD_DIM
SEG_TERMS = 1
CUM_TERMS = 2
VMEM_LIMIT = 56 * 1024 * 1024


def _dot(a, b):
    return jnp.dot(a.astype(BF16), b.astype(BF16), preferred_element_type=F32)


def _dot_nt(a, b):
    return lax.dot_general(a.astype(BF16), b.astype(BF16), (((1,), (1,)), ((), ())),
                           preferred_element_type=F32)


def _dot_tn(a, b):
    return lax.dot_general(a.astype(BF16), b.astype(BF16), (((0,), (0,)), ((), ())),
                           preferred_element_type=F32)


def _fdot(a, b):
    return jnp.dot(a, b, precision=HIGHEST, preferred_element_type=F32)


def _split_dot(x, m, terms=2, left=False):
    acc = None
    rem = x
    for _ in range(terms):
        part = rem.astype(BF16)
        rem = rem - part.astype(F32)
        d = (jnp.dot(m, part, preferred_element_type=F32) if left
             else jnp.dot(part, m, preferred_element_type=F32))
        acc = d if acc is None else acc + d
    return acc


def _iota(shape, dim):
    return lax.broadcasted_iota(jnp.int32, shape, dim)


def _seg_reduce_mat(n):
    return (_iota((n, LANES), 0) // HEAD_DIM == _iota((n, LANES), 1)).astype(BF16)


def _seg_expand_mat(n):
    return (_iota((LANES, n), 1) // HEAD_DIM == _iota((LANES, n), 0)).astype(BF16)


def _tri(n, strict):
    r, c = _iota((n, n), 0), _iota((n, n), 1)
    return ((r > c) if strict else (r >= c)).astype(BF16)


D_PACK = D_MODEL // 2


def _pack_rows(x):
    lo = lax.bitcast_convert_type(x[:, :D_PACK].astype(BF16).astype(F32), jnp.uint32)
    hi = lax.bitcast_convert_type(x[:, D_PACK:].astype(BF16).astype(F32), jnp.uint32)
    return hi | (lo >> 16)


def _unpack_rows(p):
    lo = lax.bitcast_convert_type(p << 16, F32)
    hi = lax.bitcast_convert_type(p & jnp.uint32(0xFFFF0000), F32)
    return lo, hi


def _log_sigmoid(z):
    return jnp.minimum(z, 0.0) - jnp.log(1.0 + jnp.exp(-jnp.abs(z)))


def _sigmoid(z):
    return 1.0 / (1.0 + jnp.exp(-z))


def _adaln_kernel(c_ref, w_ref, b_ref, o_ref):
    c = c_ref[...]
    o_ref[...] = _fdot(c * _sigmoid(c), w_ref[...]) + b_ref[...]


def _adaln(c, w_ada, b_ada):
    bsz = c.shape[0]
    n_mod = w_ada.shape[1] // D_MODEL
    return pl.pallas_call(
        _adaln_kernel,
        grid=(n_mod,),
        in_specs=[pl.BlockSpec((bsz, D_MODEL), lambda j: (0, 0)),
                  pl.BlockSpec((D_MODEL, D_MODEL), lambda j: (0, j)),
                  pl.BlockSpec((1, D_MODEL), lambda j: (0, j))],
        out_specs=pl.BlockSpec((bsz, D_MODEL), lambda j: (0, j)),
        out_shape=jax.ShapeDtypeStruct((bsz, n_mod * D_MODEL), F32),
        name="adaln",
    )(c, w_ada, b_ada.reshape(1, -1))


def _inproj_kernel(x_ref, sh_ref, sc_ref, g_ref, wr_ref, wx_ref, wfh_ref, wfl_ref, bf_ref, qkg_ref,
                   pr_ref, px_ref, kb_ref, qb_ref, carry_ref):
    @pl.when(pl.program_id(1) == 0)
    def _():
        carry_ref[...] = jnp.zeros_like(carry_ref)

    x = x_ref[...]
    tm = x.shape[0]
    h = x * lax.rsqrt(jnp.mean(x * x, axis=-1, keepdims=True) + NORM_EPS) * g_ref[...]
    h = h * (1.0 + sc_ref[...]) + sh_ref[...]
    hb = h.astype(BF16)
    h_lo = (h - hb.astype(F32)).astype(BF16)

    pr_ref[...] = jnp.dot(hb, wr_ref[...], preferred_element_type=F32).astype(BF16)

    px = jnp.dot(hb, wx_ref[...], preferred_element_type=F32)
    qk = px[:, :2 * D_GRP]
    ss = _split_dot(qk * qk, _seg_reduce_mat(2 * D_GRP), SEG_TERMS)
    inv = lax.rsqrt(ss * (1.0 / HEAD_DIM) + NORM_EPS)
    qk = qk * _split_dot(inv, _seg_expand_mat(2 * D_GRP), SEG_TERMS) * qkg_ref[...]
    px_ref[:, :2 * D_GRP] = qk.astype(BF16)
    px_ref[:, 2 * D_GRP:] = px[:, 2 * D_GRP:].astype(BF16)

    z = (jnp.dot(hb, wfh_ref[...], preferred_element_type=F32)
         + jnp.dot(h_lo, wfh_ref[...], preferred_element_type=F32)
         + jnp.dot(hb, wfl_ref[...], preferred_element_type=F32)) + bf_ref[...]
    cum = _split_dot(_log_sigmoid(z), _tri(tm, False), terms=3, left=True) + carry_ref[...]
    carry_ref[...] = cum[tm - 1:tm, :]

    parts = []
    rem = cum
    for _ in range(3):
        part = rem.astype(BF16)
        rem = rem - part.astype(F32)
        parts.append(part)
    src, dst = _iota((LANES, LANES), 0), _iota((LANES, LANES), 1)

    def spread(offset):
        return sum(jnp.dot(part, ((dst == 8 * src + offset + t) & (src < N_HEADS)).astype(BF16),
                           preferred_element_type=F32) for t, part in enumerate(parts))

    slot = _iota((1, LANES), 1) % 8
    kb_ref[...] = (jnp.where((slot >= 3) & (slot < 6), 1.0, 0.0) - spread(0)).astype(BF16)
    qb_ref[...] = (jnp.where(slot < 3, 1.0, 0.0) + spread(3)).astype(BF16)


def _inproj(x, shift, scale, g, w_r, w_x, w_f, b_f, qk_gain, tm):
    w_f_hi = w_f.astype(BF16)
    w_f_lo = (w_f - w_f_hi.astype(F32)).astype(BF16)
    bsz, seq, _ = x.shape
    const = lambda b, s: (0, 0)
    return pl.pallas_call(
        _inproj_kernel,
        grid=(bsz, seq // tm),
        in_specs=[pl.BlockSpec((None, tm, D_MODEL), lambda b, s: (b, s, 0)),
                  pl.BlockSpec((None, 1, D_MODEL), lambda b, s: (b, 0, 0)),
                  pl.BlockSpec((None, 1, D_MODEL), lambda b, s: (b, 0, 0)),
                  pl.BlockSpec((1, D_MODEL), const),
                  pl.BlockSpec((D_MODEL, RWKV_COLS), const),
                  pl.BlockSpec((D_MODEL, FOX_MAIN), const),
                  pl.BlockSpec((D_MODEL, LANES), const),
                  pl.BlockSpec((D_MODEL, LANES), const),
                  pl.BlockSpec((1, LANES), const),
                  pl.BlockSpec((1, 2 * D_GRP), const)],
        out_specs=[pl.BlockSpec((None, tm, RWKV_COLS), lambda b, s: (b, s, 0)),
                   pl.BlockSpec((None, tm, FOX_MAIN), lambda b, s: (b, s, 0)),
                   pl.BlockSpec((None, tm, LANES), lambda b, s: (b, s, 0)),
                   pl.BlockSpec((None, tm, LANES), lambda b, s: (b, s, 0))],
        out_shape=[jax.ShapeDtypeStruct((bsz, seq, RWKV_COLS), BF16),
                   jax.ShapeDtypeStruct((bsz, seq, FOX_MAIN), BF16),
                   jax.ShapeDtypeStruct((bsz, seq, LANES), BF16),
                   jax.ShapeDtypeStruct((bsz, seq, LANES), BF16)],
        scratch_shapes=[pltpu.VMEM((1, LANES), F32)],
        compiler_params=pltpu.CompilerParams(
            dimension_semantics=("parallel", "arbitrary"), vmem_limit_bytes=VMEM_LIMIT),
        name="inproj",
    )(x, shift, scale, g, w_r, w_x, w_f_hi, w_f_lo, b_f, qk_gain)


_NN = (((1,), (0,)), ((), ()))
_NT = (((1,), (1,)), ((), ()))
_TN = (((0,), (0,)), ((), ()))
SCAN_N = HEADS_PER_SCAN * CHUNK
BATCH_PER_STEP = 8
INV_LEVELS = 5
M_HEAD, M_STRICT, M_INCL, M_EYE, M_BASE, M_OFF = 0, 1, 2, 3, 4, 5


def _bdot(a, b, dims):
    return lax.dot_general(a, b, dims, preferred_element_type=F32)


def _scan_masks():
    rr, cc = _iota((SCAN_N, SCAN_W), 0), _iota((SCAN_N, SCAN_W), 1)
    ri, ci = _iota((SCAN_N, SCAN_N), 0), _iota((SCAN_N, SCAN_N), 1)
    same = ri // CHUNK == ci // CHUNK
    masks = [rr // CHUNK == cc // HEAD_DIM, same & (ri > ci), same & (ri >= ci), ri == ci,
             (ri // 2 == ci // 2) & (ri > ci)]
    blk = 2
    while blk < CHUNK:
        masks.append((ri // (2 * blk) == ci // (2 * blk)) & (ri // blk != ci // blk) & (ri > ci))
        blk *= 2
    return jnp.stack(masks).astype(BF16)


def _rwkv_kernel(p_ref, masks_ref, mu_ref, w0_ref, w2_ref, a0_ref, a2_ref, g2_ref, kk_ref, ka_ref,
                 rk_ref, gnw_ref, gnb_ref, wgu_ref, wd_ref, o_ref, wgu_bf_ref, wd_bf_ref,
                 last_ref, state_ref):
    wgu_bf_ref[...] = wgu_ref[...].astype(BF16)
    wd_bf_ref[...] = wd_ref[...].astype(BF16)

    @pl.when(pl.program_id(1) == 0)
    def _():
        last_ref[...] = jnp.zeros_like(last_ref)
        state_ref[...] = jnp.zeros_like(state_ref)

    mu, w0, w2, a0, a2, g2, k_k, k_a, r_k, gn_w, gn_b = (
        ref[...] for ref in (mu_ref, w0_ref, w2_ref, a0_ref, a2_ref, g2_ref, kk_ref, ka_ref,
                             rk_ref, gnw_ref, gnb_ref))
    rows = BATCH_PER_STEP * CHUNK
    p = p_ref[...].astype(F32).reshape(rows, RWKV_COLS)
    row_id = _iota((rows, 1), 0)
    prev = pltpu.roll(p, 1, axis=0)
    for bb in range(BATCH_PER_STEP):
        prev = jnp.where(row_id == bb * CHUNK, last_ref[bb], prev)
        last_ref[bb] = p[(bb + 1) * CHUNK - 1:(bb + 1) * CHUNK, :]
    pf = p + mu * (prev - p)
    r = pf[:, 0:D_GRP]
    k = pf[:, D_GRP:2 * D_GRP]
    v = pf[:, 2 * D_GRP:3 * D_GRP]
    lora = pf[:, LORA_OFF:GATE_OFF]
    gd = pf[:, GATE_OFF:RWKV_COLS]

    wlog = w0 + _dot(jnp.tanh(lora), w2)
    neg = -wlog
    softplus = jnp.maximum(neg, 0.0) + jnp.log(1.0 + jnp.exp(-jnp.abs(neg)))
    ld = -jnp.exp(-softplus - 0.5)
    a = _sigmoid(a0 + _dot(lora, a2))
    g = _dot(_sigmoid(gd), g2)

    red, exp_m = _seg_reduce_mat(D_GRP), _seg_expand_mat(D_GRP)
    kk = k * k_k
    n2 = _split_dot(kk * kk, red, SEG_TERMS)
    kk = kk * _split_dot(1.0 / jnp.maximum(jnp.sqrt(n2), 1e-12), exp_m, SEG_TERMS)
    k2 = k * (1.0 + (a - 1.0) * k_a)

    tr, tc = _iota((rows, rows), 0), _iota((rows, rows), 1)
    tri = ((tr >= tc) & (tr // CHUNK == tc // CHUNK)).astype(BF16)
    cl = _split_dot(ld, tri, terms=CUM_TERMS, left=True)
    cl_end = jnp.concatenate(
        [jnp.broadcast_to(cl[(bb + 1) * CHUNK - 1:(bb + 1) * CHUNK, :], (CHUNK, D_GRP))
         for bb in range(BATCH_PER_STEP)], axis=0)
    e_in = jnp.exp(cl)
    e_out = jnp.exp(-cl)
    e_rem = jnp.exp(cl_end - cl)
    p_end = jnp.exp(cl_end)
    kka = kk * a
    ops = [(-kk * jnp.exp(cl - ld)).astype(BF16), (kka * e_out).astype(BF16),
           (k2 * e_out).astype(BF16), (r * e_in).astype(BF16), v.astype(BF16),
           (kka * e_rem).astype(BF16), (k2 * e_rem).astype(BF16)]

    chains = [(bb, grp) for bb in range(BATCH_PER_STEP)
              for grp in range(N_HEADS // HEADS_PER_SCAN)]
    head_mask = masks_ref[M_HEAD]
    strict, incl = masks_ref[M_STRICT], masks_ref[M_INCL]

    def stacked(op, bb, grp):
        part = op[bb * CHUNK:(bb + 1) * CHUNK, grp * SCAN_W:(grp + 1) * SCAN_W]
        return jnp.concatenate([part] * HEADS_PER_SCAN, axis=0) * head_mask

    xs = [[stacked(op, bb, grp) for op in ops] for bb, grp in chains]
    st = [state_ref[bb, grp] for bb, grp in chains]
    sb = [s.astype(BF16) for s in st]
    nab = [_bdot(x[0], x[1], _NT).astype(BF16) for x in xs]
    aak = [_bdot(x[0], x[2], _NT).astype(BF16) * strict for x in xs]
    arb = [_bdot(x[3], x[1], _NT).astype(BF16) * incl for x in xs]
    ark = [_bdot(x[3], x[2], _NT).astype(BF16) * incl for x in xs]
    t_inv = [masks_ref[M_EYE] + n * masks_ref[M_BASE] for n in nab]
    for lvl in range(INV_LEVELS):
        half = [_bdot(t, n * masks_ref[M_OFF + lvl], _NN).astype(BF16) for t, n in zip(t_inv, nab)]
        t_inv = [t + _bdot(h, t, _NN).astype(BF16) for t, h in zip(t_inv, half)]
    rhs = [(_bdot(x[0], s, _NT) + _bdot(k, x[4], _NN)).astype(BF16)
           for x, s, k in zip(xs, sb, aak)]
    sa = [_bdot(t, h, _NN).astype(BF16) for t, h in zip(t_inv, rhs)]
    ys = [_bdot(x[3], s, _NT) + _bdot(b, u, _NN) + _bdot(k, x[4], _NN)
          for x, s, b, u, k in zip(xs, sb, arb, sa, ark)]
    for (bb, grp), x, s, u in zip(chains, xs, st, sa):
        decay = p_end[bb * CHUNK:bb * CHUNK + 1, grp * SCAN_W:(grp + 1) * SCAN_W]
        state_ref[bb, grp] = s * decay + _bdot(u, x[5], _TN) + _bdot(x[4], x[6], _TN)
    ys = [y[0:CHUNK] + y[CHUNK:2 * CHUNK] + y[2 * CHUNK:3 * CHUNK] + y[3 * CHUNK:4 * CHUNK]
          for y in ys]
    n_grp = N_HEADS // HEADS_PER_SCAN
    y = jnp.concatenate([jnp.concatenate(ys[bb * n_grp:(bb + 1) * n_grp], axis=1)
                         for bb in range(BATCH_PER_STEP)], axis=0)

    mean = _split_dot(_split_dot(y, red, SEG_TERMS) * (1.0 / HEAD_DIM), exp_m, SEG_TERMS)
    d = y - mean
    var = _split_dot(d * d, red, SEG_TERMS) * (1.0 / HEAD_DIM)
    yn = d * _split_dot(lax.rsqrt(var + GN_EPS), exp_m, SEG_TERMS) * gn_w + gn_b
    bonus = _split_dot(_split_dot(r * k2 * r_k, red, SEG_TERMS), exp_m, SEG_TERMS) * v
    o_ref[...] = ((yn + bonus) * g).astype(BF16).reshape(BATCH_PER_STEP, CHUNK, D_GRP)


def _rwkv(p_r, mu, w0, w2p, a0, a2p, g2, k_k, k_a, r_k, gn_w, gn_b, w_gate_up, w_down):
    bsz, seq, _ = p_r.shape
    assert bsz % BATCH_PER_STEP == 0
    n_chunk = seq // CHUNK
    n_step = (bsz // BATCH_PER_STEP) * n_chunk
    wgu2d = w_gate_up.reshape(-1, w_gate_up.shape[-1])
    wd2d = w_down.reshape(-1, w_down.shape[-1])
    assert wgu2d.shape[0] % (8 * n_step) == 0 and wd2d.shape[0] == wgu2d.shape[0]
    slab = wgu2d.shape[0] // n_step
    masks = _scan_masks()
    const = lambda b, s: (0, 0)
    step = lambda b, s: (b * n_chunk + s, 0)
    vec = pl.BlockSpec((1, D_GRP), const)
    y, wgu_bf, wd_bf = pl.pallas_call(
        _rwkv_kernel,
        grid=(bsz // BATCH_PER_STEP, n_chunk),
        in_specs=[pl.BlockSpec((BATCH_PER_STEP, CHUNK, RWKV_COLS), lambda b, s: (b, s, 0)),
                  pl.BlockSpec(masks.shape, lambda b, s: (0, 0, 0)),
                  pl.BlockSpec((1, RWKV_COLS), const),
                  vec, pl.BlockSpec((LANES, D_GRP), const),
                  vec, pl.BlockSpec((LANES, D_GRP), const),
                  pl.BlockSpec((LANES, D_GRP), const),
                  vec, vec, vec, vec, vec,
                  pl.BlockSpec((slab, wgu2d.shape[1]), step),
                  pl.BlockSpec((slab, wd2d.shape[1]), step)],
        out_specs=[pl.BlockSpec((BATCH_PER_STEP, CHUNK, D_GRP), lambda b, s: (b, s, 0)),
                   pl.BlockSpec((slab, wgu2d.shape[1]), step),
                   pl.BlockSpec((slab, wd2d.shape[1]), step)],
        out_shape=[jax.ShapeDtypeStruct((bsz, seq, D_GRP), BF16),
                   jax.ShapeDtypeStruct(wgu2d.shape, BF16),
                   jax.ShapeDtypeStruct(wd2d.shape, BF16)],
        scratch_shapes=[pltpu.VMEM((BATCH_PER_STEP, 1, RWKV_COLS), F32),
                        pltpu.VMEM((BATCH_PER_STEP, N_HEADS // HEADS_PER_SCAN, SCAN_W, SCAN_W), F32)],
        compiler_params=pltpu.CompilerParams(
            dimension_semantics=("parallel", "arbitrary"), vmem_limit_bytes=VMEM_LIMIT),
        name="rwkv",
    )(p_r, masks, mu, w0, w2p, a0, a2p, g2, k_k, k_a, r_k, gn_w, gn_b, wgu2d, wd2d)
    return y, wgu_bf.reshape(w_gate_up.shape), wd_bf.reshape(w_down.shape)


def _fox_kernel(q_ref, qb_ref, k_ref, kb_ref, vt_ref, og_ref, ong_ref, o_ref, m_ref, l_ref, acc_ref,
                *, seq):
    hp = pl.program_id(1)
    lane = _iota((1, LANES), 1)
    q = q_ref[...]
    qb = qb_ref[...]
    zero = jnp.zeros_like(q)
    qcat = [jnp.concatenate([jnp.where(lane // HEAD_DIM == hh, q, zero),
                             jnp.where(lane // 8 == hp * 2 + hh, qb, zero)], axis=1)
            for hh in range(2)]
    keys = min(FOX_SUB_KEYS, seq)
    n_sub = seq // keys
    diag = _iota((keys, keys), 1) >= _iota((keys, keys), 0)

    m_ref[...] = jnp.full(m_ref.shape, -jnp.inf, F32)
    l_ref[...] = jnp.zeros(l_ref.shape, F32)
    acc_ref[...] = jnp.zeros(acc_ref.shape, F32)

    def scores(s):
        lo = s * keys
        kcat = jnp.concatenate([k_ref[lo:lo + keys, :], kb_ref[lo:lo + keys, :]], axis=1)
        return [lax.dot_general(kcat, qc[lo:, :], _NT, preferred_element_type=F32)
                for qc in qcat]

    pending = scores(0)
    for s in range(n_sub):
        lo = s * keys
        nxt = scores(s + 1) if s + 1 < n_sub else None
        vt = vt_ref[:, lo:lo + keys]
        sts = [jnp.concatenate([jnp.where(diag, st[:, :keys], -jnp.inf), st[:, keys:]], axis=1)
               if st.shape[1] > keys else jnp.where(diag, st, -jnp.inf) for st in pending]
        m_old = [m_ref[hh, :, lo:] for hh in range(2)]
        m_new = [jnp.maximum(m, jnp.max(st, axis=0, keepdims=True)) for m, st in zip(m_old, sts)]
        pts = [jnp.exp(st - m) for st, m in zip(sts, m_new)]
        pvs = [jnp.dot(vt, pt.astype(BF16), preferred_element_type=F32) for pt in pts]
        for hh in range(2):
            alpha = jnp.exp(m_old[hh] - m_new[hh])
            m_ref[hh, :, lo:] = m_new[hh]
            l_ref[hh, :, lo:] = alpha * l_ref[hh, :, lo:] + jnp.sum(pts[hh], axis=0, keepdims=True)
            acc_ref[hh, :, lo:] = (alpha * acc_ref[hh, :, lo:]
                                   + pvs[hh][hh * HEAD_DIM:(hh + 1) * HEAD_DIM, :])
        pending = nxt

    outs = []
    for hh in range(2):
        o = acc_ref[hh] / l_ref[hh]
        outs.append(o * lax.rsqrt(jnp.mean(o * o, axis=0, keepdims=True) + NORM_EPS))
    o = jnp.concatenate(outs, axis=0).T
    o_ref[...] = (o * ong_ref[...] * _sigmoid(og_ref[...].astype(F32))).astype(BF16)


def _fox(p_x, k_bias, q_bias, o_gain):
    bsz, seq, _ = p_x.shape
    npair = N_HEADS // 2
    v_t = jnp.transpose(p_x[:, :, 2 * D_GRP:3 * D_GRP], (0, 2, 1))
    return pl.pallas_call(
        functools.partial(_fox_kernel, seq=seq),
        grid=(bsz, npair),
        in_specs=[pl.BlockSpec((None, seq, LANES), lambda b, h: (b, 0, h)),
                  pl.BlockSpec((None, seq, LANES), lambda b, h: (b, 0, 0)),
                  pl.BlockSpec((None, seq, LANES), lambda b, h: (b, 0, npair + h)),
                  pl.BlockSpec((None, seq, LANES), lambda b, h: (b, 0, 0)),
                  pl.BlockSpec((None, LANES, seq), lambda b, h: (b, h, 0)),
                  pl.BlockSpec((None, seq, LANES), lambda b, h: (b, 0, 3 * npair + h)),
                  pl.BlockSpec((1, LANES), lambda b, h: (0, 0))],
        out_specs=pl.BlockSpec((None, seq, LANES), lambda b, h: (b, 0, h)),
        out_shape=jax.ShapeDtypeStruct((bsz, seq, D_GRP), BF16),
        scratch_shapes=[pltpu.VMEM((2, 1, seq), F32), pltpu.VMEM((2, 1, seq), F32),
                        pltpu.VMEM((2, HEAD_DIM, seq), F32)],
        compiler_params=pltpu.CompilerParams(
            dimension_semantics=("parallel", "parallel"), vmem_limit_bytes=VMEM_LIMIT),
        name="fox",
    )(p_x, q_bias, p_x, k_bias, v_t, p_x, o_gain)


def _outproj_kernel(x_ref, yr_ref, yf_ref, g1_ref, sh_ref, sc_ref, ng_ref, wor_ref, wof_ref,
                    wrt_ref, wrl_ref, brt_ref, x1_ref, h2_ref, idx_ref, gate_ref, rank_ref, cnt_ref,
                    carry_ref):
    @pl.when(pl.program_id(0) == 0)
    def _():
        carry_ref[...] = jnp.zeros_like(carry_ref)

    y = (jnp.dot(yr_ref[...], wor_ref[...], preferred_element_type=F32)
         + jnp.dot(yf_ref[...], wof_ref[...], preferred_element_type=F32))
    x1 = x_ref[...] + g1_ref[...] * y
    x1_ref[...] = x1
    tm = x1.shape[0]
    h = x1 * lax.rsqrt(jnp.mean(x1 * x1, axis=-1, keepdims=True) + NORM_EPS) * ng_ref[...]
    h2 = h * (1.0 + sc_ref[...]) + sh_ref[...]
    h2_ref[...] = _pack_rows(h2)

    h_hi = h2.astype(BF16)
    h_lo = (h2 - h_hi.astype(F32)).astype(BF16)
    logits = (lax.dot_general(wrt_ref[...], h_hi, _NT, preferred_element_type=F32)
              + lax.dot_general(wrt_ref[...], h_lo, _NT, preferred_element_type=F32)
              + lax.dot_general(wrl_ref[...], h_hi, _NT, preferred_element_type=F32))
    lg = logits[:N_EXPERTS, :] + brt_ref[...]
    expert = _iota((N_EXPERTS, tm), 0)
    picks = []
    hot_sum = jnp.zeros((N_EXPERTS, tm), F32)
    for _ in range(TOP_K):
        m = jnp.max(lg, axis=0, keepdims=True)
        sel = jnp.min(jnp.where(lg == m, expert, N_EXPERTS), axis=0, keepdims=True)
        hot = expert == sel
        picks.append((m, sel, hot))
        hot_sum = hot_sum + hot.astype(F32)
        lg = jnp.where(hot, -jnp.inf, lg)
    es = [jnp.exp(m - picks[0][0]) for m, _, _ in picks]
    den = es[0] + es[1] + es[2] + es[3]

    earlier = (_iota((tm, tm), 0) < _iota((tm, tm), 1)).astype(BF16)
    before = jnp.dot(hot_sum.astype(BF16), earlier, preferred_element_type=F32) + carry_ref[...]
    ranks = [jnp.sum(jnp.where(hot, before, 0.0), axis=0, keepdims=True).astype(jnp.int32)
             for _, _, hot in picks]
    pad_i = jnp.zeros((8 - TOP_K, tm), jnp.int32)
    idx_ref[...] = jnp.concatenate([sel for _, sel, _ in picks] + [pad_i], axis=0)
    gate_ref[...] = jnp.concatenate([e / den for e in es] + [pad_i.astype(F32)], axis=0)
    rank_ref[...] = jnp.concatenate(ranks + [pad_i], axis=0)
    carry_ref[...] = carry_ref[...] + jnp.sum(hot_sum, axis=1, keepdims=True)
    cnt_ref[...] = jnp.broadcast_to(carry_ref[...], cnt_ref.shape)


def _outproj(x2d, y_r, y_f, gate1, shift2, scale2, norm_g, wo_r, wo_f, w_rt, b_rt, tm, seq,
             row0, t):
    w_rt_hi = w_rt.astype(BF16)
    w_rt_lo = (w_rt - w_rt_hi.astype(F32)).astype(BF16)
    per_b = seq // tm
    blk0 = row0 // tm
    const = lambda i: (0, 0)
    rows = lambda i: (i, 0)
    rows_in = lambda i: (i + blk0, 0)
    mod = pl.BlockSpec((None, 1, D_MODEL), lambda i: ((i + blk0) // per_b, 0, 0))
    return pl.pallas_call(
        _outproj_kernel,
        grid=(t // tm,),
        in_specs=[pl.BlockSpec((tm, D_MODEL), rows_in),
                  pl.BlockSpec((tm, D_GRP), rows_in),
                  pl.BlockSpec((tm, D_GRP), rows_in),
                  mod, mod, mod,
                  pl.BlockSpec((1, D_MODEL), const),
                  pl.BlockSpec((D_GRP, D_MODEL), const),
                  pl.BlockSpec((D_GRP, D_MODEL), const),
                  pl.BlockSpec((LANES, D_MODEL), const),
                  pl.BlockSpec((LANES, D_MODEL), const),
                  pl.BlockSpec((N_EXPERTS, 1), const)],
        out_specs=[pl.BlockSpec((tm, D_MODEL), rows),
                   pl.BlockSpec((tm, D_PACK), rows),
                   pl.BlockSpec((8, tm), lambda i: (0, i)),
                   pl.BlockSpec((8, tm), lambda i: (0, i)),
                   pl.BlockSpec((8, tm), lambda i: (0, i)),
                   pl.BlockSpec((N_EXPERTS, LANES), const)],
        out_shape=[jax.ShapeDtypeStruct((t, D_MODEL), F32),
                   jax.ShapeDtypeStruct((t, D_PACK), jnp.uint32),
                   jax.ShapeDtypeStruct((8, t), jnp.int32),
                   jax.ShapeDtypeStruct((8, t), F32),
                   jax.ShapeDtypeStruct((8, t), jnp.int32),
                   jax.ShapeDtypeStruct((N_EXPERTS, LANES), F32)],
        scratch_shapes=[pltpu.VMEM((N_EXPERTS, 1), F32)],
        compiler_params=pltpu.CompilerParams(
            dimension_semantics=("arbitrary",), vmem_limit_bytes=VMEM_LIMIT),
        name="outproj",
    )(x2d, y_r, y_f, gate1, shift2, scale2, norm_g, wo_r, wo_f, w_rt_hi, w_rt_lo, b_rt)


SC_CORES = 2
SC_SUBCORES = 16
SC_ROWS = 64


def _sc_gather_rows(idx, src):
    n_workers = SC_CORES * SC_SUBCORES
    m = idx.shape[0]
    d = src.shape[1]
    assert m % (n_workers * SC_ROWS) == 0
    n_chunks = m // (n_workers * SC_ROWS)
    mesh = plsc.VectorSubcoreMesh(core_axis_name="c", subcore_axis_name="s")

    @functools.partial(
        pl.kernel, mesh=mesh,
        out_type=jax.ShapeDtypeStruct((m, d), src.dtype),
        scratch_types=[pltpu.VMEM((n_chunks, SC_ROWS), jnp.int32),
                       pltpu.VMEM((SC_ROWS, d), src.dtype),
                       pltpu.SemaphoreType.DMA],
        name="sc_gather")
    def gather(src_hbm, idx_hbm, out_hbm, idx_v, rows_v, sem):
        wid = lax.axis_index("s") * SC_CORES + lax.axis_index("c")
        pltpu.sync_copy(idx_hbm.at[wid], idx_v)

        @pl.loop(0, n_chunks)
        def _(j):
            pltpu.async_copy(src_hbm.at[idx_v.at[j]], rows_v, sem).wait()
            pltpu.sync_copy(rows_v, out_hbm.at[pl.ds((wid * n_chunks + j) * SC_ROWS, SC_ROWS)])

    return gather(src, idx.reshape(n_workers, n_chunks, SC_ROWS))


def _sc_scatter_rows(src, dest, n_out):
    n_workers = SC_CORES * SC_SUBCORES
    t, d = src.shape
    n_slot = dest.shape[1]
    assert t % (n_workers * SC_ROWS) == 0
    n_chunks = t // (n_workers * SC_ROWS)
    mesh = plsc.VectorSubcoreMesh(core_axis_name="c", subcore_axis_name="s")
    idx = dest.reshape(n_workers, n_chunks, SC_ROWS, n_slot).transpose(0, 1, 3, 2)
    idx = idx.reshape(n_workers, n_chunks * n_slot, SC_ROWS)

    @functools.partial(
        pl.kernel, mesh=mesh,
        out_type=jax.ShapeDtypeStruct((n_out, d), src.dtype),
        scratch_types=[pltpu.VMEM((n_chunks * n_slot, SC_ROWS), jnp.int32),
                       pltpu.VMEM((SC_ROWS, d), src.dtype)],
        name="sc_scatter")
    def scatter(src_hbm, idx_hbm, out_hbm, idx_v, rows_v):
        wid = lax.axis_index("s") * SC_CORES + lax.axis_index("c")
        pltpu.sync_copy(idx_hbm.at[wid], idx_v)

        @pl.loop(0, n_chunks)
        def _(j):
            pltpu.sync_copy(src_hbm.at[pl.ds((wid * n_chunks + j) * SC_ROWS, SC_ROWS)], rows_v)
            for k in range(n_slot):
                pltpu.sync_copy(rows_v, out_hbm.at[idx_v.at[j * n_slot + k]])

    return scatter(src, idx)


def _expert_kernel(be_ref, nv_ref, x_ref, wgu_ref, bgu_ref, wd_ref, bd_ref, o_ref):
    del be_ref
    valid = _iota((EXPERT_BLOCK, 1), 0) < nv_ref[pl.program_id(0)]
    lo, hi = _unpack_rows(jnp.where(valid, x_ref[...], jnp.uint32(0)))
    x = jnp.concatenate([lo.astype(BF16), hi.astype(BF16)], axis=1)
    gu = jnp.dot(x, wgu_ref[...], preferred_element_type=F32) + bgu_ref[...]
    gate = jnp.minimum(gu[:, :D_MODEL], SWIGLU_LIMIT)
    up = jnp.clip(gu[:, D_MODEL:], -SWIGLU_LIMIT, SWIGLU_LIMIT)
    act = gate * _sigmoid(SWIGLU_ALPHA * gate) * (up + 1.0)
    o_ref[...] = _pack_rows(
        jnp.dot(act.astype(BF16), wd_ref[...], preferred_element_type=F32) + bd_ref[...])


def _experts(block_e, n_valid, xs, w_gu, b_gu, w_d, b_d):
    n_blocks = block_e.shape[0]
    grid_spec = pltpu.PrefetchScalarGridSpec(
        num_scalar_prefetch=2,
        grid=(n_blocks,),
        in_specs=[pl.BlockSpec((EXPERT_BLOCK, D_PACK), lambda j, be, nv: (j, 0)),
                  pl.BlockSpec((None, D_MODEL, 2 * D_MODEL), lambda j, be, nv: (be[j], 0, 0)),
                  pl.BlockSpec((None, 1, 2 * D_MODEL), lambda j, be, nv: (be[j], 0, 0)),
                  pl.BlockSpec((None, D_MODEL, D_MODEL), lambda j, be, nv: (be[j], 0, 0)),
                  pl.BlockSpec((None, 1, D_MODEL), lambda j, be, nv: (be[j], 0, 0))],
        out_specs=pl.BlockSpec((EXPERT_BLOCK, D_PACK), lambda j, be, nv: (j, 0)),
    )
    return pl.pallas_call(
        _expert_kernel,
        grid_spec=grid_spec,
        out_shape=jax.ShapeDtypeStruct(xs.shape, jnp.uint32),
        compiler_params=pltpu.CompilerParams(
            dimension_semantics=("arbitrary",), vmem_limit_bytes=VMEM_LIMIT),
        name="experts",
    )(block_e, n_valid, xs, w_gu, b_gu, w_d, b_d)


COMBINE_TOKENS = 512
MOE_SPLITS = 2


def _combine_kernel(yg_ref, x1_ref, gate_ref, g2_ref, fg_ref, o_ref):
    gates = gate_ref[...]
    acc_lo = acc_hi = None
    for kk in range(TOP_K):
        lo, hi = _unpack_rows(yg_ref[kk * COMBINE_TOKENS:(kk + 1) * COMBINE_TOKENS, :])
        g = gates[:, kk:kk + 1]
        acc_lo = g * lo if acc_lo is None else acc_lo + g * lo
        acc_hi = g * hi if acc_hi is None else acc_hi + g * hi
    x2 = x1_ref[...] + g2_ref[...] * jnp.concatenate([acc_lo, acc_hi], axis=1)
    o_ref[...] = x2 * lax.rsqrt(jnp.mean(x2 * x2, axis=-1, keepdims=True) + NORM_EPS) * fg_ref[...]


def _combine_kernel_into(prev_ref, *refs):
    del prev_ref
    _combine_kernel(*refs)


def _combine(yg, x1, gates, gate2, final_g, seq, row0, t_total, prev):
    t = x1.shape[0]
    tm = COMBINE_TOKENS
    per_b = seq // tm
    blk0 = row0 // tm
    rows = lambda i: (i, 0)
    in_specs = [pl.BlockSpec((TOP_K * tm, D_PACK), rows),
                pl.BlockSpec((tm, D_MODEL), rows),
                pl.BlockSpec((tm, LANES), rows),
                pl.BlockSpec((None, 1, D_MODEL), lambda i: ((i + blk0) // per_b, 0, 0)),
                pl.BlockSpec((1, D_MODEL), lambda i: (0, 0))]
    args = (yg, x1, gates, gate2, final_g)
    if prev is not None:
        in_specs = [pl.BlockSpec(memory_space=pl.ANY)] + in_specs
        args = (prev,) + args
    return pl.pallas_call(
        _combine_kernel if prev is None else _combine_kernel_into,
        grid=(t // tm,),
        in_specs=in_specs,
        out_specs=pl.BlockSpec((tm, D_MODEL), lambda i: (i + blk0, 0)),
        out_shape=jax.ShapeDtypeStruct((t_total, D_MODEL), F32),
        input_output_aliases={} if prev is None else {0: 0},
        compiler_params=pltpu.CompilerParams(
            dimension_semantics=("parallel",), vmem_limit_bytes=VMEM_LIMIT),
        name="combine",
    )(*args)


def _moe(h2, idx, gates, rank, counts, x1, gate2, final_g, w_gu, b_gu, w_d, b_d, seq,
         row0, t_total, prev):
    t = h2.shape[0]
    n_slots = t * TOP_K
    n_blocks = -(-n_slots // EXPERT_BLOCK) + N_EXPERTS
    cap = n_blocks * EXPERT_BLOCK
    padded = (counts + EXPERT_BLOCK - 1) // EXPERT_BLOCK * EXPERT_BLOCK
    pad_ends = jnp.cumsum(padded)
    pad_starts = pad_ends - padded
    dest = pad_starts[idx] + rank
    block_starts = jnp.arange(n_blocks, dtype=jnp.int32) * EXPERT_BLOCK
    block_e = jnp.minimum(jnp.sum(block_starts[:, None] >= pad_ends[None, :], axis=1),
                          N_EXPERTS - 1).astype(jnp.int32)
    n_valid = jnp.clip(counts[block_e] - (block_starts - pad_starts[block_e]), 0, EXPERT_BLOCK)

    xs = _sc_scatter_rows(h2, dest, cap)
    yb = _experts(block_e, n_valid.astype(jnp.int32), xs, w_gu, b_gu, w_d, b_d)
    dest_blocks = dest.reshape(-1, COMBINE_TOKENS, TOP_K).transpose(0, 2, 1).reshape(-1)
    yg = _sc_gather_rows(dest_blocks, yb)
    return _combine(yg, x1, gates, gate2, final_g, seq, row0, t_total, prev)


def _layer(x, c_mod, norm1_g, w_in, mu_shift, w0, w2, a0, a2, g2, k_k, k_a, r_k, gn_w, gn_b, b_f,
           q_norm_g, k_norm_g, o_norm_g, w_out, norm2_g, w_router, b_router, w_gate_up,
           b_gate_up, w_down, b_down, final_g, tm_in, tm_out):
    bsz, seq, _ = x.shape
    shift1, scale1, gate1, shift2, scale2, gate2 = (
        m.reshape(bsz, 1, D_MODEL) for m in jnp.split(c_mod, 6, axis=-1))
    row = lambda v: v.reshape(1, -1)

    w_r = w_in[:, :RWKV_COLS].astype(BF16)
    w_x = w_in[:, RWKV_COLS:RWKV_COLS + FOX_MAIN].astype(BF16)
    w_f = jnp.pad(w_in[:, RWKV_COLS + FOX_MAIN:], ((0, 0), (0, LANES - N_HEADS)))
    b_fp = jnp.pad(b_f, (0, LANES - N_HEADS)).reshape(1, LANES)
    qk_gain = jnp.concatenate([jnp.tile(q_norm_g, N_HEADS) * HEAD_DIM ** -0.5,
                               jnp.tile(k_norm_g, N_HEADS)]).reshape(1, -1)
    p_r, p_x, k_bias, q_bias = _inproj(x, shift1, scale1, row(norm1_g), w_r, w_x, w_f, b_fp,
                                       qk_gain, tm_in)

    zeros = jnp.zeros((LANES - 64, D_GRP), F32)
    w2p = jnp.concatenate([w2, zeros], axis=0).astype(BF16)
    a2p = jnp.concatenate([zeros, a2], axis=0).astype(BF16)
    y_r, w_gu, w_d = _rwkv(p_r, row(mu_shift), row(w0), w2p, row(a0), a2p, g2.astype(BF16),
                           row(k_k), row(k_a), row(r_k), row(gn_w), row(gn_b), w_gate_up, w_down)

    y_f = _fox(p_x, k_bias, q_bias, jnp.tile(o_norm_g, 2).reshape(1, LANES))

    t = bsz * seq
    w_rt = jnp.pad(w_router.T, ((0, LANES - N_EXPERTS), (0, 0)))
    b_rt = b_router.reshape(N_EXPERTS, 1)
    wo = w_out.astype(BF16)
    b_gu, b_d = b_gate_up.reshape(N_EXPERTS, 1, -1), b_down.reshape(N_EXPERTS, 1, -1)
    t_part = t // MOE_SPLITS
    out = None
    for part in range(MOE_SPLITS):
        row0 = part * t_part
        x1, h2, idx, gates, rank, cnt = _outproj(
            x.reshape(t, D_MODEL), y_r.reshape(t, D_GRP), y_f.reshape(t, D_GRP), gate1, shift2,
            scale2, row(norm2_g), wo[:D_GRP], wo[D_GRP:], w_rt, b_rt, tm_out, seq, row0, t_part)
        counts = cnt[:, 0].astype(jnp.int32)
        gates = jnp.pad(gates[:TOP_K].T, ((0, 0), (0, LANES - TOP_K)))
        out = _moe(h2, idx[:TOP_K].T, gates, rank[:TOP_K].T, counts, x1, gate2, row(final_g),
                   w_gu, b_gu, w_d, b_d, seq, row0, t, out)
    return out.reshape(bsz, seq, D_MODEL)


def kernel(x, c, w_ada, b_ada, norm1_g, w_in, mu_shift, w0, w2, a0, a2, g2, k_k, k_a, r_k, gn_w,
           gn_b, b_f, q_norm_g, k_norm_g, o_norm_g, w_out, norm2_g, w_router, b_router, w_gate_up,
           b_gate_up, w_down, b_down, final_g):
    assert w_ada.shape[0] == 1, "single-layer block"
    c_mod = _adaln(c, w_ada[0], b_ada[0])
    return _layer(x, c_mod, norm1_g[0], w_in[0], mu_shift[0], w0[0], w2[0], a0[0], a2[0], g2[0],
                  k_k[0], k_a[0], r_k[0], gn_w[0], gn_b[0], b_f[0], q_norm_g[0], k_norm_g[0],
                  o_norm_g[0], w_out[0], norm2_g[0], w_router[0], b_router[0], w_gate_up[0],
                  b_gate_up[0], w_down[0], b_down[0], final_g,
                  tm_in=min(512, x.shape[1]), tm_out=min(1024, x.shape[1]))
```

```python
import functools

import jax
import jax.numpy as jnp
from jax import lax
from jax.experimental import pallas as pl
from jax.experimental.pallas import tpu as pltpu
from jax.experimental.pallas import tpu_sc as plsc

F32 = jnp.float32
BF16 = jnp.bfloat16
HIGHEST = lax.Precision.HIGHEST

D_MODEL = 1024
HEAD_DIM = 64
N_HEADS = 8
D_GRP = N_HEADS * HEAD_DIM
RWKV_COLS = 1792
LORA_OFF = 3 * D_GRP
GATE_OFF = LORA_OFF + 128
FOX_MAIN = 4 * D_GRP
N_EXPERTS = 32
TOP_K = 4
EXPERT_BLOCK = 512
SWIGLU_ALPHA = 1.702
SWIGLU_LIMIT = 7.0
NORM_EPS = 1e-6
GN_EPS = 64e-5
LANES = 128
CHUNK = 64
FOX_SUB_KEYS = 512
HEADS_PER_SCAN = 4
SCAN_W = HEADS_PER_SCAN * HEAD_DIM
SEG_TERMS = 1
CUM_TERMS = 2
VMEM_LIMIT = 56 * 1024 * 1024


def _dot(a, b):
    return jnp.dot(a.astype(BF16), b.astype(BF16), preferred_element_type=F32)


def _dot_nt(a, b):
    return lax.dot_general(a.astype(BF16), b.astype(BF16), (((1,), (1,)), ((), ())),
                           preferred_element_type=F32)


def _dot_tn(a, b):
    return lax.dot_general(a.astype(BF16), b.astype(BF16), (((0,), (0,)), ((), ())),
                           preferred_element_type=F32)


def _fdot(a, b):
    return jnp.dot(a, b, precision=HIGHEST, preferred_element_type=F32)


def _split_dot(x, m, terms=2, left=False):
    acc = None
    rem = x
    for _ in range(terms):
        part = rem.astype(BF16)
        rem = rem - part.astype(F32)
        d = (jnp.dot(m, part, preferred_element_type=F32) if left
             else jnp.dot(part, m, preferred_element_type=F32))
        acc = d if acc is None else acc + d
    return acc


def _iota(shape, dim):
    return lax.broadcasted_iota(jnp.int32, shape, dim)


def _seg_reduce_mat(n):
    return (_iota((n, LANES), 0) // HEAD_DIM == _iota((n, LANES), 1)).astype(BF16)


def _seg_expand_mat(n):
    return (_iota((LANES, n), 1) // HEAD_DIM == _iota((LANES, n), 0)).astype(BF16)


def _tri(n, strict):
    r, c = _iota((n, n), 0), _iota((n, n), 1)
    return ((r > c) if strict else (r >= c)).astype(BF16)


D_PACK = D_MODEL // 2


def _pack_rows(x):
    lo = lax.bitcast_convert_type(x[:, :D_PACK].astype(BF16).astype(F32), jnp.uint32)
    hi = lax.bitcast_convert_type(x[:, D_PACK:].astype(BF16).astype(F32), jnp.uint32)
    return hi | (lo >> 16)


def _unpack_rows(p):
    lo = lax.bitcast_convert_type(p << 16, F32)
    hi = lax.bitcast_convert_type(p & jnp.uint32(0xFFFF0000), F32)
    return lo, hi


def _log_sigmoid(z):
    return jnp.minimum(z, 0.0) - jnp.log(1.0 + jnp.exp(-jnp.abs(z)))


def _sigmoid(z):
    return 1.0 / (1.0 + jnp.exp(-z))


def _adaln_kernel(c_ref, w_ref, b_ref, o_ref):
    c = c_ref[...]
    o_ref[...] = _fdot(c * _sigmoid(c), w_ref[...]) + b_ref[...]


def _adaln(c, w_ada, b_ada):
    bsz = c.shape[0]
    n_mod = w_ada.shape[1] // D_MODEL
    return pl.pallas_call(
        _adaln_kernel,
        grid=(n_mod,),
        in_specs=[pl.BlockSpec((bsz, D_MODEL), lambda j: (0, 0)),
                  pl.BlockSpec((D_MODEL, D_MODEL), lambda j: (0, j)),
                  pl.BlockSpec((1, D_MODEL), lambda j: (0, j))],
        out_specs=pl.BlockSpec((bsz, D_MODEL), lambda j: (0, j)),
        out_shape=jax.ShapeDtypeStruct((bsz, n_mod * D_MODEL), F32),
        name="adaln",
    )(c, w_ada, b_ada.reshape(1, -1))


def _inproj_kernel(x_ref, sh_ref, sc_ref, g_ref, wr_ref, wx_ref, wfh_ref, wfl_ref, bf_ref, qkg_ref,
                   pr_ref, px_ref, kb_ref, qb_ref, carry_ref):
    @pl.when(pl.program_id(1) == 0)
    def _():
        carry_ref[...] = jnp.zeros_like(carry_ref)

    x = x_ref[...]
    tm = x.shape[0]
    h = x * lax.rsqrt(jnp.mean(x * x, axis=-1, keepdims=True) + NORM_EPS) * g_ref[...]
    h = h * (1.0 + sc_ref[...]) + sh_ref[...]
    hb = h.astype(BF16)
    h_lo = (h - hb.astype(F32)).astype(BF16)

    pr_ref[...] = jnp.dot(hb, wr_ref[...], preferred_element_type=F32).astype(BF16)

    px = jnp.dot(hb, wx_ref[...], preferred_element_type=F32)
    qk = px[:, :2 * D_GRP]
    ss = _split_dot(qk * qk, _seg_reduce_mat(2 * D_GRP), SEG_TERMS)
    inv = lax.rsqrt(ss * (1.0 / HEAD_DIM) + NORM_EPS)
    qk = qk * _split_dot(inv, _seg_expand_mat(2 * D_GRP), SEG_TERMS) * qkg_ref[...]
    px_ref[:, :2 * D_GRP] = qk.astype(BF16)
    px_ref[:, 2 * D_GRP:] = px[:, 2 * D_GRP:].astype(BF16)

    z = (jnp.dot(hb, wfh_ref[...], preferred_element_type=F32)
         + jnp.dot(h_lo, wfh_ref[...], preferred_element_type=F32)
         + jnp.dot(hb, wfl_ref[...], preferred_element_type=F32)) + bf_ref[...]
    cum = _split_dot(_log_sigmoid(z), _tri(tm, False), terms=3, left=True) + carry_ref[...]
    carry_ref[...] = cum[tm - 1:tm, :]

    parts = []
    rem = cum
    for _ in range(3):
        part = rem.astype(BF16)
        rem = rem - part.astype(F32)
        parts.append(part)
    src, dst = _iota((LANES, LANES), 0), _iota((LANES, LANES), 1)

    def spread(offset):
        return sum(jnp.dot(part, ((dst == 8 * src + offset + t) & (src < N_HEADS)).astype(BF16),
                           preferred_element_type=F32) for t, part in enumerate(parts))

    slot = _iota((1, LANES), 1) % 8
    kb_ref[...] = (jnp.where((slot >= 3) & (slot < 6), 1.0, 0.0) - spread(0)).astype(BF16)
    qb_ref[...] = (jnp.where(slot < 3, 1.0, 0.0) + spread(3)).astype(BF16)


def _inproj(x, shift, scale, g, w_r, w_x, w_f, b_f, qk_gain, tm):
    w_f_hi = w_f.astype(BF16)
    w_f_lo = (w_f - w_f_hi.astype(F32)).astype(BF16)
    bsz, seq, _ = x.shape
    const = lambda b, s: (0, 0)
    return pl.pallas_call(
        _inproj_kernel,
        grid=(bsz, seq // tm),
        in_specs=[pl.BlockSpec((None, tm, D_MODEL), lambda b, s: (b, s, 0)),
                  pl.BlockSpec((None, 1, D_MODEL), lambda b, s: (b, 0, 0)),
                  pl.BlockSpec((None, 1, D_MODEL), lambda b, s: (b, 0, 0)),
                  pl.BlockSpec((1, D_MODEL), const),
                  pl.BlockSpec((D_MODEL, RWKV_COLS), const),
                  pl.BlockSpec((D_MODEL, FOX_MAIN), const),
                  pl.BlockSpec((D_MODEL, LANES), const),
                  pl.BlockSpec((D_MODEL, LANES), const),
                  pl.BlockSpec((1, LANES), const),
                  pl.BlockSpec((1, 2 * D_GRP), const)],
        out_specs=[pl.BlockSpec((None, tm, RWKV_COLS), lambda b, s: (b, s, 0)),
                   pl.BlockSpec((None, tm, FOX_MAIN), lambda b, s: (b, s, 0)),
                   pl.BlockSpec((None, tm, LANES), lambda b, s: (b, s, 0)),
                   pl.BlockSpec((None, tm, LANES), lambda b, s: (b, s, 0))],
        out_shape=[jax.ShapeDtypeStruct((bsz, seq, RWKV_COLS), BF16),
                   jax.ShapeDtypeStruct((bsz, seq, FOX_MAIN), BF16),
                   jax.ShapeDtypeStruct((bsz, seq, LANES), BF16),
                   jax.ShapeDtypeStruct((bsz, seq, LANES), BF16)],
        scratch_shapes=[pltpu.VMEM((1, LANES), F32)],
        compiler_params=pltpu.CompilerParams(
            dimension_semantics=("parallel", "arbitrary"), vmem_limit_bytes=VMEM_LIMIT),
        name="inproj",
    )(x, shift, scale, g, w_r, w_x, w_f_hi, w_f_lo, b_f, qk_gain)


_NN = (((1,), (0,)), ((), ()))
_NT = (((1,), (1,)), ((), ()))
_TN = (((0,), (0,)), ((), ()))
SCAN_N = HEADS_PER_SCAN * CHUNK
BATCH_PER_STEP = 8
INV_LEVELS = 5
M_HEAD, M_STRICT, M_INCL, M_EYE, M_BASE, M_OFF = 0, 1, 2, 3, 4, 5


def _bdot(a, b, dims):
    return lax.dot_general(a, b, dims, preferred_element_type=F32)


def _scan_masks():
    rr, cc = _iota((SCAN_N, SCAN_W), 0), _iota((SCAN_N, SCAN_W), 1)
    ri, ci = _iota((SCAN_N, SCAN_N), 0), _iota((SCAN_N, SCAN_N), 1)
    same = ri // CHUNK == ci // CHUNK
    masks = [rr // CHUNK == cc // HEAD_DIM, same & (ri > ci), same & (ri >= ci), ri == ci,
             (ri // 2 == ci // 2) & (ri > ci)]
    blk = 2
    while blk < CHUNK:
        masks.append((ri // (2 * blk) == ci // (2 * blk)) & (ri // blk != ci // blk) & (ri > ci))
        blk *= 2
    return jnp.stack(masks).astype(BF16)


def _rwkv_kernel(p_ref, masks_ref, mu_ref, w0_ref, w2_ref, a0_ref, a2_ref, g2_ref, kk_ref, ka_ref,
                 rk_ref, gnw_ref, gnb_ref, wgu_ref, wd_ref, o_ref, wgu_bf_ref, wd_bf_ref,
                 last_ref, state_ref):
    wgu_bf_ref[...] = wgu_ref[...].astype(BF16)
    wd_bf_ref[...] = wd_ref[...].astype(BF16)

    @pl.when(pl.program_id(1) == 0)
    def _():
        last_ref[...] = jnp.zeros_like(last_ref)
        state_ref[...] = jnp.zeros_like(state_ref)

    mu, w0, w2, a0, a2, g2, k_k, k_a, r_k, gn_w, gn_b = (
        ref[...] for ref in (mu_ref, w0_ref, w2_ref, a0_ref, a2_ref, g2_ref, kk_ref, ka_ref,
                             rk_ref, gnw_ref, gnb_ref))
    rows = BATCH_PER_STEP * CHUNK
    p = p_ref[...].astype(F32).reshape(rows, RWKV_COLS)
    row_id = _iota((rows, 1), 0)
    prev = pltpu.roll(p, 1, axis=0)
    for bb in range(BATCH_PER_STEP):
        prev = jnp.where(row_id == bb * CHUNK, last_ref[bb], prev)
        last_ref[bb] = p[(bb + 1) * CHUNK - 1:(bb + 1) * CHUNK, :]
    pf = p + mu * (prev - p)
    r = pf[:, 0:D_GRP]
    k = pf[:, D_GRP:2 * D_GRP]
    v = pf[:, 2 * D_GRP:3 * D_GRP]
    lora = pf[:, LORA_OFF:GATE_OFF]
    gd = pf[:, GATE_OFF:RWKV_COLS]

    wlog = w0 + _dot(jnp.tanh(lora), w2)
    neg = -wlog
    softplus = jnp.maximum(neg, 0.0) + jnp.log(1.0 + jnp.exp(-jnp.abs(neg)))
    ld = -jnp.exp(-softplus - 0.5)
    a = _sigmoid(a0 + _dot(lora, a2))
    g = _dot(_sigmoid(gd), g2)

    red, exp_m = _seg_reduce_mat(D_GRP), _seg_expand_mat(D_GRP)
    kk = k * k_k
    n2 = _split_dot(kk * kk, red, SEG_TERMS)
    kk = kk * _split_dot(1.0 / jnp.maximum(jnp.sqrt(n2), 1e-12), exp_m, SEG_TERMS)
    k2 = k * (1.0 + (a - 1.0) * k_a)

    tr, tc = _iota((rows, rows), 0), _iota((rows, rows), 1)
    tri = ((tr >= tc) & (tr // CHUNK == tc // CHUNK)).astype(BF16)
    cl = _split_dot(ld, tri, terms=CUM_TERMS, left=True)
    cl_end = jnp.concatenate(
        [jnp.broadcast_to(cl[(bb + 1) * CHUNK - 1:(bb + 1) * CHUNK, :], (CHUNK, D_GRP))
         for bb in range(BATCH_PER_STEP)], axis=0)
    e_in = jnp.exp(cl)
    e_out = jnp.exp(-cl)
    e_rem = jnp.exp(cl_end - cl)
    p_end = jnp.exp(cl_end)
    kka = kk * a
    ops = [(-kk * jnp.exp(cl - ld)).astype(BF16), (kka * e_out).astype(BF16),
           (k2 * e_out).astype(BF16), (r * e_in).astype(BF16), v.astype(BF16),
           (kka * e_rem).astype(BF16), (k2 * e_rem).astype(BF16)]

    chains = [(bb, grp) for bb in range(BATCH_PER_STEP)
              for grp in range(N_HEADS // HEADS_PER_SCAN)]
    head_mask = masks_ref[M_HEAD]
    strict, incl = masks_ref[M_STRICT], masks_ref[M_INCL]

    def stacked(op, bb, grp):
        part = op[bb * CHUNK:(bb + 1) * CHUNK, grp * SCAN_W:(grp + 1) * SCAN_W]
        return jnp.concatenate([part] * HEADS_PER_SCAN, axis=0) * head_mask

    xs = [[stacked(op, bb, grp) for op in ops] for bb, grp in chains]
    st = [state_ref[bb, grp] for bb, grp in chains]
    sb = [s.astype(BF16) for s in st]
    nab = [_bdot(x[0], x[1], _NT).astype(BF16) for x in xs]
    aak = [_bdot(x[0], x[2], _NT).astype(BF16) * strict for x in xs]
    arb = [_bdot(x[3], x[1], _NT).astype(BF16) * incl for x in xs]
    ark = [_bdot(x[3], x[2], _NT).astype(BF16) * incl for x in xs]
    t_inv = [masks_ref[M_EYE] + n * masks_ref[M_BASE] for n in nab]
    for lvl in range(INV_LEVELS):
        half = [_bdot(t, n * masks_ref[M_OFF + lvl], _NN).astype(BF16) for t, n in zip(t_inv, nab)]
        t_inv = [t + _bdot(h, t, _NN).astype(BF16) for t, h in zip(t_inv, half)]
    rhs = [(_bdot(x[0], s, _NT) + _bdot(k, x[4], _NN)).astype(BF16)
           for x, s, k in zip(xs, sb, aak)]
    sa = [_bdot(t, h, _NN).astype(BF16) for t, h in zip(t_inv, rhs)]
    ys = [_bdot(x[3], s, _NT) + _bdot(b, u, _NN) + _bdot(k, x[4], _NN)
          for x, s, b, u, k in zip(xs, sb, arb, sa, ark)]
    for (bb, grp), x, s, u in zip(chains, xs, st, sa):
        decay = p_end[bb * CHUNK:bb * CHUNK + 1, grp * SCAN_W:(grp + 1) * SCAN_W]
        state_ref[bb, grp] = s * decay + _bdot(u, x[5], _TN) + _bdot(x[4], x[6], _TN)
    ys = [y[0:CHUNK] + y[CHUNK:2 * CHUNK] + y[2 * CHUNK:3 * CHUNK] + y[3 * CHUNK:4 * CHUNK]
          for y in ys]
    n_grp = N_HEADS // HEADS_PER_SCAN
    y = jnp.concatenate([jnp.concatenate(ys[bb * n_grp:(bb + 1) * n_grp], axis=1)
                         for bb in range(BATCH_PER_STEP)], axis=0)

    mean = _split_dot(_split_dot(y, red, SEG_TERMS) * (1.0 / HEAD_DIM), exp_m, SEG_TERMS)
    d = y - mean
    var = _split_dot(d * d, red, SEG_TERMS) * (1.0 / HEAD_DIM)
    yn = d * _split_dot(lax.rsqrt(var + GN_EPS), exp_m, SEG_TERMS) * gn_w + gn_b
    bonus = _split_dot(_split_dot(r * k2 * r_k, red, SEG_TERMS), exp_m, SEG_TERMS) * v
    o_ref[...] = ((yn + bonus) * g).astype(BF16).reshape(BATCH_PER_STEP, CHUNK, D_GRP)


def _rwkv(p_r, mu, w0, w2p, a0, a2p, g2, k_k, k_a, r_k, gn_w, gn_b, w_gate_up, w_down):
    bsz, seq, _ = p_r.shape
    assert bsz % BATCH_PER_STEP == 0
    n_chunk = seq // CHUNK
    n_step = (bsz // BATCH_PER_STEP) * n_chunk
    wgu2d = w_gate_up.reshape(-1, w_gate_up.shape[-1])
    wd2d = w_down.reshape(-1, w_down.shape[-1])
    assert wgu2d.shape[0] % (8 * n_step) == 0 and wd2d.shape[0] == wgu2d.shape[0]
    slab = wgu2d.shape[0] // n_step
    masks = _scan_masks()
    const = lambda b, s: (0, 0)
    step = lambda b, s: (b * n_chunk + s, 0)
    vec = pl.BlockSpec((1, D_GRP), const)
    y, wgu_bf, wd_bf = pl.pallas_call(
        _rwkv_kernel,
        grid=(bsz // BATCH_PER_STEP, n_chunk),
        in_specs=[pl.BlockSpec((BATCH_PER_STEP, CHUNK, RWKV_COLS), lambda b, s: (b, s, 0)),
                  pl.BlockSpec(masks.shape, lambda b, s: (0, 0, 0)),
                  pl.BlockSpec((1, RWKV_COLS), const),
                  vec, pl.BlockSpec((LANES, D_GRP), const),
                  vec, pl.BlockSpec((LANES, D_GRP), const),
                  pl.BlockSpec((LANES, D_GRP), const),
                  vec, vec, vec, vec, vec,
                  pl.BlockSpec((slab, wgu2d.shape[1]), step),
                  pl.BlockSpec((slab, wd2d.shape[1]), step)],
        out_specs=[pl.BlockSpec((BATCH_PER_STEP, CHUNK, D_GRP), lambda b, s: (b, s, 0)),
                   pl.BlockSpec((slab, wgu2d.shape[1]), step),
                   pl.BlockSpec((slab, wd2d.shape[1]), step)],
        out_shape=[jax.ShapeDtypeStruct((bsz, seq, D_GRP), BF16),
                   jax.ShapeDtypeStruct(wgu2d.shape, BF16),
                   jax.ShapeDtypeStruct(wd2d.shape, BF16)],
        scratch_shapes=[pltpu.VMEM((BATCH_PER_STEP, 1, RWKV_COLS), F32),
                        pltpu.VMEM((BATCH_PER_STEP, N_HEADS // HEADS_PER_SCAN, SCAN_W, SCAN_W), F32)],
        compiler_params=pltpu.CompilerParams(
            dimension_semantics=("parallel", "arbitrary"), vmem_limit_bytes=VMEM_LIMIT),
        name="rwkv",
    )(p_r, masks, mu, w0, w2p, a0, a2p, g2, k_k, k_a, r_k, gn_w, gn_b, wgu2d, wd2d)
    return y, wgu_bf.reshape(w_gate_up.shape), wd_bf.reshape(w_down.shape)


def _fox_kernel(q_ref, qb_ref, k_ref, kb_ref, vt_ref, og_ref, ong_ref, o_ref, m_ref, l_ref, acc_ref,
                *, seq):
    hp = pl.program_id(1)
    lane = _iota((1, LANES), 1)
    q = q_ref[...]
    qb = qb_ref[...]
    zero = jnp.zeros_like(q)
    qcat = [jnp.concatenate([jnp.where(lane // HEAD_DIM == hh, q, zero),
                             jnp.where(lane // 8 == hp * 2 + hh, qb, zero)], axis=1)
            for hh in range(2)]
    keys = min(FOX_SUB_KEYS, seq)
    n_sub = seq // keys
    diag = _iota((keys, keys), 1) >= _iota((keys, keys), 0)

    m_ref[...] = jnp.full(m_ref.shape, -jnp.inf, F32)
    l_ref[...] = jnp.zeros(l_ref.shape, F32)
    acc_ref[...] = jnp.zeros(acc_ref.shape, F32)

    def scores(s):
        lo = s * keys
        kcat = jnp.concatenate([k_ref[lo:lo + keys, :], kb_ref[lo:lo + keys, :]], axis=1)
        return [lax.dot_general(kcat, qc[lo:, :], _NT, preferred_element_type=F32)
                for qc in qcat]

    pending = scores(0)
    for s in range(n_sub):
        lo = s * keys
        nxt = scores(s + 1) if s + 1 < n_sub else None
        vt = vt_ref[:, lo:lo + keys]
        sts = [jnp.concatenate([jnp.where(diag, st[:, :keys], -jnp.inf), st[:, keys:]], axis=1)
               if st.shape[1] > keys else jnp.where(diag, st, -jnp.inf) for st in pending]
        m_old = [m_ref[hh, :, lo:] for hh in range(2)]
        m_new = [jnp.maximum(m, jnp.max(st, axis=0, keepdims=True)) for m, st in zip(m_old, sts)]
        pts = [jnp.exp(st - m) for st, m in zip(sts, m_new)]
        pvs = [jnp.dot(vt, pt.astype(BF16), preferred_element_type=F32) for pt in pts]
        for hh in range(2):
            alpha = jnp.exp(m_old[hh] - m_new[hh])
            m_ref[hh, :, lo:] = m_new[hh]
            l_ref[hh, :, lo:] = alpha * l_ref[hh, :, lo:] + jnp.sum(pts[hh], axis=0, keepdims=True)
            acc_ref[hh, :, lo:] = (alpha * acc_ref[hh, :, lo:]
                                   + pvs[hh][hh * HEAD_DIM:(hh + 1) * HEAD_DIM, :])
        pending = nxt

    outs = []
    for hh in range(2):
        o = acc_ref[hh] / l_ref[hh]
        outs.append(o * lax.rsqrt(jnp.mean(o * o, axis=0, keepdims=True) + NORM_EPS))
    o = jnp.concatenate(outs, axis=0).T
    o_ref[...] = (o * ong_ref[...] * _sigmoid(og_ref[...].astype(F32))).astype(BF16)


def _fox(p_x, k_bias, q_bias, o_gain):
    bsz, seq, _ = p_x.shape
    npair = N_HEADS // 2
    v_t = jnp.transpose(p_x[:, :, 2 * D_GRP:3 * D_GRP], (0, 2, 1))
    return pl.pallas_call(
        functools.partial(_fox_kernel, seq=seq),
        grid=(bsz, npair),
        in_specs=[pl.BlockSpec((None, seq, LANES), lambda b, h: (b, 0, h)),
                  pl.BlockSpec((None, seq, LANES), lambda b, h: (b, 0, 0)),
                  pl.BlockSpec((None, seq, LANES), lambda b, h: (b, 0, npair + h)),
                  pl.BlockSpec((None, seq, LANES), lambda b, h: (b, 0, 0)),
                  pl.BlockSpec((None, LANES, seq), lambda b, h: (b, h, 0)),
                  pl.BlockSpec((None, seq, LANES), lambda b, h: (b, 0, 3 * npair + h)),
                  pl.BlockSpec((1, LANES), lambda b, h: (0, 0))],
        out_specs=pl.BlockSpec((None, seq, LANES), lambda b, h: (b, 0, h)),
        out_shape=jax.ShapeDtypeStruct((bsz, seq, D_GRP), BF16),
        scratch_shapes=[pltpu.VMEM((2, 1, seq), F32), pltpu.VMEM((2, 1, seq), F32),
                        pltpu.VMEM((2, HEAD_DIM, seq), F32)],
        compiler_params=pltpu.CompilerParams(
            dimension_semantics=("parallel", "parallel"), vmem_limit_bytes=VMEM_LIMIT),
        name="fox",
    )(p_x, q_bias, p_x, k_bias, v_t, p_x, o_gain)


def _outproj_kernel(x_ref, yr_ref, yf_ref, g1_ref, sh_ref, sc_ref, ng_ref, wor_ref, wof_ref,
                    wrt_ref, wrl_ref, brt_ref, x1_ref, h2_ref, idx_ref, gate_ref, rank_ref, cnt_ref,
                    carry_ref):
    @pl.when(pl.program_id(0) == 0)
    def _():
        carry_ref[...] = jnp.zeros_like(carry_ref)

    y = (jnp.dot(yr_ref[...], wor_ref[...], preferred_element_type=F32)
         + jnp.dot(yf_ref[...], wof_ref[...], preferred_element_type=F32))
    x1 = x_ref[...] + g1_ref[...] * y
    x1_ref[...] = x1
    tm = x1.shape[0]
    h = x1 * lax.rsqrt(jnp.mean(x1 * x1, axis=-1, keepdims=True) + NORM_EPS) * ng_ref[...]
    h2 = h * (1.0 + sc_ref[...]) + sh_ref[...]
    h2_ref[...] = _pack_rows(h2)

    h_hi = h2.astype(BF16)
    h_lo = (h2 - h_hi.astype(F32)).astype(BF16)
    logits = (lax.dot_general(wrt_ref[...], h_hi, _NT, preferred_element_type=F32)
              + lax.dot_general(wrt_ref[...], h_lo, _NT, preferred_element_type=F32)
              + lax.dot_general(wrl_ref[...], h_hi, _NT, preferred_element_type=F32))
    lg = logits[:N_EXPERTS, :] + brt_ref[...]
    expert = _iota((N_EXPERTS, tm), 0)
    picks = []
    hot_sum = jnp.zeros((N_EXPERTS, tm), F32)
    for _ in range(TOP_K):
        m = jnp.max(lg, axis=0, keepdims=True)
        sel = jnp.min(jnp.where(lg == m, expert, N_EXPERTS), axis=0, keepdims=True)
        hot = expert == sel
        picks.append((m, sel, hot))
        hot_sum = hot_sum + hot.astype(F32)
        lg = jnp.where(hot, -jnp.inf, lg)
    es = [jnp.exp(m - picks[0][0]) for m, _, _ in picks]
    den = es[0] + es[1] + es[2] + es[3]

    earlier = (_iota((tm, tm), 0) < _iota((tm, tm), 1)).astype(BF16)
    before = jnp.dot(hot_sum.astype(BF16), earlier, preferred_element_type=F32) + carry_ref[...]
    ranks = [jnp.sum(jnp.where(hot, before, 0.0), axis=0, keepdims=True).astype(jnp.int32)
             for _, _, hot in picks]
    pad_i = jnp.zeros((8 - TOP_K, tm), jnp.int32)
    idx_ref[...] = jnp.concatenate([sel for _, sel, _ in picks] + [pad_i], axis=0)
    gate_ref[...] = jnp.concatenate([e / den for e in es] + [pad_i.astype(F32)], axis=0)
    rank_ref[...] = jnp.concatenate(ranks + [pad_i], axis=0)
    carry_ref[...] = carry_ref[...] + jnp.sum(hot_sum, axis=1, keepdims=True)
    cnt_ref[...] = jnp.broadcast_to(carry_ref[...], cnt_ref.shape)


def _outproj(x2d, y_r, y_f, gate1, shift2, scale2, norm_g, wo_r, wo_f, w_rt, b_rt, tm, seq,
             row0, t):
    w_rt_hi = w_rt.astype(BF16)
    w_rt_lo = (w_rt - w_rt_hi.astype(F32)).astype(BF16)
    per_b = seq // tm
    blk0 = row0 // tm
    const = lambda i: (0, 0)
    rows = lambda i: (i, 0)
    rows_in = lambda i: (i + blk0, 0)
    mod = pl.BlockSpec((None, 1, D_MODEL), lambda i: ((i + blk0) // per_b, 0, 0))
    return pl.pallas_call(
        _outproj_kernel,
        grid=(t // tm,),
        in_specs=[pl.BlockSpec((tm, D_MODEL), rows_in),
                  pl.BlockSpec((tm, D_GRP), rows_in),
                  pl.BlockSpec((tm, D_GRP), rows_in),
                  mod, mod, mod,
                  pl.BlockSpec((1, D_MODEL), const),
                  pl.BlockSpec((D_GRP, D_MODEL), const),
                  pl.BlockSpec((D_GRP, D_MODEL), const),
                  pl.BlockSpec((LANES, D_MODEL), const),
                  pl.BlockSpec((LANES, D_MODEL), const),
                  pl.BlockSpec((N_EXPERTS, 1), const)],
        out_specs=[pl.BlockSpec((tm, D_MODEL), rows),
                   pl.BlockSpec((tm, D_PACK), rows),
                   pl.BlockSpec((8, tm), lambda i: (0, i)),
                   pl.BlockSpec((8, tm), lambda i: (0, i)),
                   pl.BlockSpec((8, tm), lambda i: (0, i)),
                   pl.BlockSpec((N_EXPERTS, LANES), const)],
        out_shape=[jax.ShapeDtypeStruct((t, D_MODEL), F32),
                   jax.ShapeDtypeStruct((t, D_PACK), jnp.uint32),
                   jax.ShapeDtypeStruct((8, t), jnp.int32),
                   jax.ShapeDtypeStruct((8, t), F32),
                   jax.ShapeDtypeStruct((8, t), jnp.int32),
                   jax.ShapeDtypeStruct((N_EXPERTS, LANES), F32)],
        scratch_shapes=[pltpu.VMEM((N_EXPERTS, 1), F32)],
        compiler_params=pltpu.CompilerParams(
            dimension_semantics=("arbitrary",), vmem_limit_bytes=VMEM_LIMIT),
        name="outproj",
    )(x2d, y_r, y_f, gate1, shift2, scale2, norm_g, wo_r, wo_f, w_rt_hi, w_rt_lo, b_rt)


SC_CORES = 2
SC_SUBCORES = 16
SC_ROWS = 64


def _sc_gather_rows(idx, src):
    n_workers = SC_CORES * SC_SUBCORES
    m = idx.shape[0]
    d = src.shape[1]
    assert m % (n_workers * SC_ROWS) == 0
    n_chunks = m // (n_workers * SC_ROWS)
    mesh = plsc.VectorSubcoreMesh(core_axis_name="c", subcore_axis_name="s")

    @functools.partial(
        pl.kernel, mesh=mesh,
        out_type=jax.ShapeDtypeStruct((m, d), src.dtype),
        scratch_types=[pltpu.VMEM((n_chunks, SC_ROWS), jnp.int32),
                       pltpu.VMEM((SC_ROWS, d), src.dtype),
                       pltpu.SemaphoreType.DMA],
        name="sc_gather")
    def gather(src_hbm, idx_hbm, out_hbm, idx_v, rows_v, sem):
        wid = lax.axis_index("s") * SC_CORES + lax.axis_index("c")
        pltpu.sync_copy(idx_hbm.at[wid], idx_v)

        @pl.loop(0, n_chunks)
        def _(j):
            pltpu.async_copy(src_hbm.at[idx_v.at[j]], rows_v, sem).wait()
            pltpu.sync_copy(rows_v, out_hbm.at[pl.ds((wid * n_chunks + j) * SC_ROWS, SC_ROWS)])

    return gather(src, idx.reshape(n_workers, n_chunks, SC_ROWS))


def _sc_scatter_rows(src, dest, n_out):
    n_workers = SC_CORES * SC_SUBCORES
    t, d = src.shape
    n_slot = dest.shape[0]
    assert t % (n_workers * SC_ROWS) == 0
    n_chunks = t // (n_workers * SC_ROWS)
    mesh = plsc.VectorSubcoreMesh(core_axis_name="c", subcore_axis_name="s")
    idx = dest.reshape(n_slot, n_workers, n_chunks, SC_ROWS).transpose(1, 2, 0, 3)
    idx = idx.reshape(n_workers, n_chunks * n_slot, SC_ROWS)

    @functools.partial(
        pl.kernel, mesh=mesh,
        out_type=jax.ShapeDtypeStruct((n_out, d), src.dtype),
        scratch_types=[pltpu.VMEM((n_chunks * n_slot, SC_ROWS), jnp.int32),
                       pltpu.VMEM((SC_ROWS, d), src.dtype)],
        name="sc_scatter")
    def scatter(src_hbm, idx_hbm, out_hbm, idx_v, rows_v):
        wid = lax.axis_index("s") * SC_CORES + lax.axis_index("c")
        pltpu.sync_copy(idx_hbm.at[wid], idx_v)

        @pl.loop(0, n_chunks)
        def _(j):
            pltpu.sync_copy(src_hbm.at[pl.ds((wid * n_chunks + j) * SC_ROWS, SC_ROWS)], rows_v)
            for k in range(n_slot):
                pltpu.sync_copy(rows_v, out_hbm.at[idx_v.at[j * n_slot + k]])

    return scatter(src, idx)


def _expert_kernel(be_ref, nv_ref, x_ref, wgu_ref, bgu_ref, wd_ref, bd_ref, o_ref):
    del be_ref
    valid = _iota((EXPERT_BLOCK, 1), 0) < nv_ref[pl.program_id(0)]
    lo, hi = _unpack_rows(jnp.where(valid, x_ref[...], jnp.uint32(0)))
    x = jnp.concatenate([lo.astype(BF16), hi.astype(BF16)], axis=1)
    gu = jnp.dot(x, wgu_ref[...], preferred_element_type=F32) + bgu_ref[...]
    gate = jnp.minimum(gu[:, :D_MODEL], SWIGLU_LIMIT)
    up = jnp.clip(gu[:, D_MODEL:], -SWIGLU_LIMIT, SWIGLU_LIMIT)
    act = gate * _sigmoid(SWIGLU_ALPHA * gate) * (up + 1.0)
    o_ref[...] = _pack_rows(
        jnp.dot(act.astype(BF16), wd_ref[...], preferred_element_type=F32) + bd_ref[...])


def _experts(block_e, n_valid, xs, w_gu, b_gu, w_d, b_d):
    n_blocks = block_e.shape[0]
    grid_spec = pltpu.PrefetchScalarGridSpec(
        num_scalar_prefetch=2,
        grid=(n_blocks,),
        in_specs=[pl.BlockSpec((EXPERT_BLOCK, D_PACK), lambda j, be, nv: (j, 0)),
                  pl.BlockSpec((None, D_MODEL, 2 * D_MODEL), lambda j, be, nv: (be[j], 0, 0)),
                  pl.BlockSpec((None, 1, 2 * D_MODEL), lambda j, be, nv: (be[j], 0, 0)),
                  pl.BlockSpec((None, D_MODEL, D_MODEL), lambda j, be, nv: (be[j], 0, 0)),
                  pl.BlockSpec((None, 1, D_MODEL), lambda j, be, nv: (be[j], 0, 0))],
        out_specs=pl.BlockSpec((EXPERT_BLOCK, D_PACK), lambda j, be, nv: (j, 0)),
    )
    return pl.pallas_call(
        _expert_kernel,
        grid_spec=grid_spec,
        out_shape=jax.ShapeDtypeStruct(xs.shape, jnp.uint32),
        compiler_params=pltpu.CompilerParams(
            dimension_semantics=("arbitrary",), vmem_limit_bytes=VMEM_LIMIT),
        name="experts",
    )(block_e, n_valid, xs, w_gu, b_gu, w_d, b_d)


COMBINE_TOKENS = 512
MOE_SPLITS = 2


def _combine_kernel(yg_ref, x1_ref, gate_ref, g2_ref, fg_ref, o_ref):
    gates = gate_ref[...].T
    acc_lo = acc_hi = None
    for kk in range(TOP_K):
        lo, hi = _unpack_rows(yg_ref[kk * COMBINE_TOKENS:(kk + 1) * COMBINE_TOKENS, :])
        g = gates[:, kk:kk + 1]
        acc_lo = g * lo if acc_lo is None else acc_lo + g * lo
        acc_hi = g * hi if acc_hi is None else acc_hi + g * hi
    x2 = x1_ref[...] + g2_ref[...] * jnp.concatenate([acc_lo, acc_hi], axis=1)
    o_ref[...] = x2 * lax.rsqrt(jnp.mean(x2 * x2, axis=-1, keepdims=True) + NORM_EPS) * fg_ref[...]


def _combine_kernel_into(prev_ref, *refs):
    del prev_ref
    _combine_kernel(*refs)


def _combine(yg, x1, gates, gate2, final_g, seq, row0, t_total, prev):
    t = x1.shape[0]
    tm = COMBINE_TOKENS
    per_b = seq // tm
    blk0 = row0 // tm
    rows = lambda i: (i, 0)
    in_specs = [pl.BlockSpec((TOP_K * tm, D_PACK), rows),
                pl.BlockSpec((tm, D_MODEL), rows),
                pl.BlockSpec((8, tm), lambda i: (0, i)),
                pl.BlockSpec((None, 1, D_MODEL), lambda i: ((i + blk0) // per_b, 0, 0)),
                pl.BlockSpec((1, D_MODEL), lambda i: (0, 0))]
    args = (yg, x1, gates, gate2, final_g)
    if prev is not None:
        in_specs = [pl.BlockSpec(memory_space=pl.ANY)] + in_specs
        args = (prev,) + args
    return pl.pallas_call(
        _combine_kernel if prev is None else _combine_kernel_into,
        grid=(t // tm,),
        in_specs=in_specs,
        out_specs=pl.BlockSpec((tm, D_MODEL), lambda i: (i + blk0, 0)),
        out_shape=jax.ShapeDtypeStruct((t_total, D_MODEL), F32),
        input_output_aliases={} if prev is None else {0: 0},
        compiler_params=pltpu.CompilerParams(
            dimension_semantics=("parallel",), vmem_limit_bytes=VMEM_LIMIT),
        name="combine",
    )(*args)


def _moe(h2, idx, gates, rank, counts, x1, gate2, final_g, w_gu, b_gu, w_d, b_d, seq,
         row0, t_total, prev):
    t = h2.shape[0]
    n_slots = t * TOP_K
    n_blocks = -(-n_slots // EXPERT_BLOCK) + N_EXPERTS
    cap = n_blocks * EXPERT_BLOCK
    padded = (counts + EXPERT_BLOCK - 1) // EXPERT_BLOCK * EXPERT_BLOCK
    pad_ends = jnp.cumsum(padded)
    pad_starts = pad_ends - padded
    dest = pad_starts[idx] + rank
    block_starts = jnp.arange(n_blocks, dtype=jnp.int32) * EXPERT_BLOCK
    block_e = jnp.minimum(jnp.sum(block_starts[:, None] >= pad_ends[None, :], axis=1),
                          N_EXPERTS - 1).astype(jnp.int32)
    n_valid = jnp.clip(counts[block_e] - (block_starts - pad_starts[block_e]), 0, EXPERT_BLOCK)

    xs = _sc_scatter_rows(h2, dest, cap)
    yb = _experts(block_e, n_valid.astype(jnp.int32), xs, w_gu, b_gu, w_d, b_d)
    dest_blocks = dest.reshape(TOP_K, -1, COMBINE_TOKENS).transpose(1, 0, 2).reshape(-1)
    yg = _sc_gather_rows(dest_blocks, yb)
    return _combine(yg, x1, gates, gate2, final_g, seq, row0, t_total, prev)


def _layer(x, c_mod, norm1_g, w_in, mu_shift, w0, w2, a0, a2, g2, k_k, k_a, r_k, gn_w, gn_b, b_f,
           q_norm_g, k_norm_g, o_norm_g, w_out, norm2_g, w_router, b_router, w_gate_up,
           b_gate_up, w_down, b_down, final_g, tm_in, tm_out):
    bsz, seq, _ = x.shape
    shift1, scale1, gate1, shift2, scale2, gate2 = (
        m.reshape(bsz, 1, D_MODEL) for m in jnp.split(c_mod, 6, axis=-1))
    row = lambda v: v.reshape(1, -1)

    w_r = w_in[:, :RWKV_COLS].astype(BF16)
    w_x = w_in[:, RWKV_COLS:RWKV_COLS + FOX_MAIN].astype(BF16)
    w_f = jnp.pad(w_in[:, RWKV_COLS + FOX_MAIN:], ((0, 0), (0, LANES - N_HEADS)))
    b_fp = jnp.pad(b_f, (0, LANES - N_HEADS)).reshape(1, LANES)
    qk_gain = jnp.concatenate([jnp.tile(q_norm_g, N_HEADS) * HEAD_DIM ** -0.5,
                               jnp.tile(k_norm_g, N_HEADS)]).reshape(1, -1)
    p_r, p_x, k_bias, q_bias = _inproj(x, shift1, scale1, row(norm1_g), w_r, w_x, w_f, b_fp,
                                       qk_gain, tm_in)

    zeros = jnp.zeros((LANES - 64, D_GRP), F32)
    w2p = jnp.concatenate([w2, zeros], axis=0).astype(BF16)
    a2p = jnp.concatenate([zeros, a2], axis=0).astype(BF16)
    y_r, w_gu, w_d = _rwkv(p_r, row(mu_shift), row(w0), w2p, row(a0), a2p, g2.astype(BF16),
                           row(k_k), row(k_a), row(r_k), row(gn_w), row(gn_b), w_gate_up, w_down)

    y_f = _fox(p_x, k_bias, q_bias, jnp.tile(o_norm_g, 2).reshape(1, LANES))

    t = bsz * seq
    w_rt = jnp.pad(w_router.T, ((0, LANES - N_EXPERTS), (0, 0)))
    b_rt = b_router.reshape(N_EXPERTS, 1)
    wo = w_out.astype(BF16)
    b_gu, b_d = b_gate_up.reshape(N_EXPERTS, 1, -1), b_down.reshape(N_EXPERTS, 1, -1)
    t_part = t // MOE_SPLITS
    out = None
    for part in range(MOE_SPLITS):
        row0 = part * t_part
        x1, h2, idx, gates, rank, cnt = _outproj(
            x.reshape(t, D_MODEL), y_r.reshape(t, D_GRP), y_f.reshape(t, D_GRP), gate1, shift2,
            scale2, row(norm2_g), wo[:D_GRP], wo[D_GRP:], w_rt, b_rt, tm_out, seq, row0, t_part)
        counts = cnt[:, 0].astype(jnp.int32)
        out = _moe(h2, idx[:TOP_K], gates, rank[:TOP_K], counts, x1, gate2, row(final_g),
                   w_gu, b_gu, w_d, b_d, seq, row0, t, out)
    return out.reshape(bsz, seq, D_MODEL)


def kernel(x, c, w_ada, b_ada, norm1_g, w_in, mu_shift, w0, w2, a0, a2, g2, k_k, k_a, r_k, gn_w,
           gn_b, b_f, q_norm_g, k_norm_g, o_norm_g, w_out, norm2_g, w_router, b_router, w_gate_up,
           b_gate_up, w_down, b_down, final_g):
    assert w_ada.shape[0] == 1, "single-layer block"
    c_mod = _adaln(c, w_ada[0], b_ada[0])
    return _layer(x, c_mod, norm1_g[0], w_in[0], mu_shift[0], w0[0], w2[0], a0[0], a2[0], g2[0],
                  k_k[0], k_a[0], r_k[0], gn_w[0], gn_b[0], b_f[0], q_norm_g[0], k_norm_g[0],
                  o_norm_g[0], w_out[0], norm2_g[0], w_router[0], b_router[0], w_gate_up[0],
                  b_gate_up[0], w_down[0], b_down[0], final_g,
                  tm_in=min(512, x.shape[1]), tm_out=min(1024, x.shape[1]))
```

```python
import functools

import jax
import jax.numpy as jnp
from jax import lax
from jax.experimental import pallas as pl
from jax.experimental.pallas import tpu as pltpu
from jax.experimental.pallas import tpu_sc as plsc

F32 = jnp.float32
BF16 = jnp.bfloat16
HIGHEST = lax.Precision.HIGHEST

D_MODEL = 1024
HEAD_DIM = 64
N_HEADS = 8
D_GRP = N_HEADS * HEAD_DIM
RWKV_COLS = 1792
LORA_OFF = 3 * D_GRP
GATE_OFF = LORA_OFF + 128
FOX_MAIN = 4 * D_GRP
N_EXPERTS = 32
TOP_K = 4
EXPERT_BLOCK = 512
SWIGLU_ALPHA = 1.702
SWIGLU_LIMIT = 7.0
NORM_EPS = 1e-6
GN_EPS = 64e-5
LANES = 128
CHUNK = 64
FOX_SUB_KEYS = 512
HEADS_PER_SCAN = 4
SCAN_W = HEADS_PER_SCAN * HEAD_DIM
SEG_TERMS = 1
CUM_TERMS = 2
VMEM_LIMIT = 56 * 1024 * 1024


def _dot(a, b):
    return jnp.dot(a.astype(BF16), b.astype(BF16), preferred_element_type=F32)


def _dot_nt(a, b):
    return lax.dot_general(a.astype(BF16), b.astype(BF16), (((1,), (1,)), ((), ())),
                           preferred_element_type=F32)


def _dot_tn(a, b):
    return lax.dot_general(a.astype(BF16), b.astype(BF16), (((0,), (0,)), ((), ())),
                           preferred_element_type=F32)


def _fdot(a, b):
    return jnp.dot(a, b, precision=HIGHEST, preferred_element_type=F32)


def _split_dot(x, m, terms=2, left=False):
    acc = None
    rem = x
    for _ in range(terms):
        part = rem.astype(BF16)
        rem = rem - part.astype(F32)
        d = (jnp.dot(m, part, preferred_element_type=F32) if left
             else jnp.dot(part, m, preferred_element_type=F32))
        acc = d if acc is None else acc + d
    return acc


def _iota(shape, dim):
    return lax.broadcasted_iota(jnp.int32, shape, dim)


def _seg_reduce_mat(n):
    return (_iota((n, LANES), 0) // HEAD_DIM == _iota((n, LANES), 1)).astype(BF16)


def _seg_expand_mat(n):
    return (_iota((LANES, n), 1) // HEAD_DIM == _iota((LANES, n), 0)).astype(BF16)


def _tri(n, strict):
    r, c = _iota((n, n), 0), _iota((n, n), 1)
    return ((r > c) if strict else (r >= c)).astype(BF16)


D_PACK = D_MODEL // 2


def _pack_rows(x):
    lo = lax.bitcast_convert_type(x[:, :D_PACK].astype(BF16).astype(F32), jnp.uint32)
    hi = lax.bitcast_convert_type(x[:, D_PACK:].astype(BF16).astype(F32), jnp.uint32)
    return hi | (lo >> 16)


def _unpack_rows(p):
    lo = lax.bitcast_convert_type(p << 16, F32)
    hi = lax.bitcast_convert_type(p & jnp.uint32(0xFFFF0000), F32)
    return lo, hi


def _log_sigmoid(z):
    return jnp.minimum(z, 0.0) - jnp.log(1.0 + jnp.exp(-jnp.abs(z)))


def _sigmoid(z):
    return 1.0 / (1.0 + jnp.exp(-z))


def _adaln_kernel(c_ref, w_ref, b_ref, o_ref):
    c = c_ref[...]
    o_ref[...] = _fdot(c * _sigmoid(c), w_ref[...]) + b_ref[...]


def _adaln(c, w_ada, b_ada):
    bsz = c.shape[0]
    n_mod = w_ada.shape[1] // D_MODEL
    return pl.pallas_call(
        _adaln_kernel,
        grid=(n_mod,),
        in_specs=[pl.BlockSpec((bsz, D_MODEL), lambda j: (0, 0)),
                  pl.BlockSpec((D_MODEL, D_MODEL), lambda j: (0, j)),
                  pl.BlockSpec((1, D_MODEL), lambda j: (0, j))],
        out_specs=pl.BlockSpec((bsz, D_MODEL), lambda j: (0, j)),
        out_shape=jax.ShapeDtypeStruct((bsz, n_mod * D_MODEL), F32),
        name="adaln",
    )(c, w_ada, b_ada.reshape(1, -1))


def _inproj_kernel(x_ref, sh_ref, sc_ref, g_ref, wr_ref, wx_ref, wfh_ref, wfl_ref, bf_ref, qkg_ref,
                   pr_ref, px_ref, kb_ref, qb_ref, carry_ref):
    @pl.when(pl.program_id(1) == 0)
    def _():
        carry_ref[...] = jnp.zeros_like(carry_ref)

    x = x_ref[...]
    tm = x.shape[0]
    h = x * lax.rsqrt(jnp.mean(x * x, axis=-1, keepdims=True) + NORM_EPS) * g_ref[...]
    h = h * (1.0 + sc_ref[...]) + sh_ref[...]
    hb = h.astype(BF16)
    h_lo = (h - hb.astype(F32)).astype(BF16)

    pr_ref[...] = jnp.dot(hb, wr_ref[...], preferred_element_type=F32).astype(BF16)

    px = jnp.dot(hb, wx_ref[...], preferred_element_type=F32)
    qk = px[:, :2 * D_GRP]
    ss = _split_dot(qk * qk, _seg_reduce_mat(2 * D_GRP), SEG_TERMS)
    inv = lax.rsqrt(ss * (1.0 / HEAD_DIM) + NORM_EPS)
    qk = qk * _split_dot(inv, _seg_expand_mat(2 * D_GRP), SEG_TERMS) * qkg_ref[...]
    px_ref[:, :2 * D_GRP] = qk.astype(BF16)
    px_ref[:, 2 * D_GRP:] = px[:, 2 * D_GRP:].astype(BF16)

    z = (jnp.dot(hb, wfh_ref[...], preferred_element_type=F32)
         + jnp.dot(h_lo, wfh_ref[...], preferred_element_type=F32)
         + jnp.dot(hb, wfl_ref[...], preferred_element_type=F32)) + bf_ref[...]
    cum = _split_dot(_log_sigmoid(z), _tri(tm, False), terms=3, left=True) + carry_ref[...]
    carry_ref[...] = cum[tm - 1:tm, :]

    parts = []
    rem = cum
    for _ in range(3):
        part = rem.astype(BF16)
        rem = rem - part.astype(F32)
        parts.append(part)
    src, dst = _iota((LANES, LANES), 0), _iota((LANES, LANES), 1)

    def spread(offset):
        return sum(jnp.dot(part, ((dst == 8 * src + offset + t) & (src < N_HEADS)).astype(BF16),
                           preferred_element_type=F32) for t, part in enumerate(parts))

    slot = _iota((1, LANES), 1) % 8
    kb_ref[...] = (jnp.where((slot >= 3) & (slot < 6), 1.0, 0.0) - spread(0)).astype(BF16)
    qb_ref[...] = (jnp.where(slot < 3, 1.0, 0.0) + spread(3)).astype(BF16)


def _inproj(x, shift, scale, g, w_r, w_x, w_f, b_f, qk_gain, tm):
    w_f_hi = w_f.astype(BF16)
    w_f_lo = (w_f - w_f_hi.astype(F32)).astype(BF16)
    bsz, seq, _ = x.shape
    const = lambda b, s: (0, 0)
    return pl.pallas_call(
        _inproj_kernel,
        grid=(bsz, seq // tm),
        in_specs=[pl.BlockSpec((None, tm, D_MODEL), lambda b, s: (b, s, 0)),
                  pl.BlockSpec((None, 1, D_MODEL), lambda b, s: (b, 0, 0)),
                  pl.BlockSpec((None, 1, D_MODEL), lambda b, s: (b, 0, 0)),
                  pl.BlockSpec((1, D_MODEL), const),
                  pl.BlockSpec((D_MODEL, RWKV_COLS), const),
                  pl.BlockSpec((D_MODEL, FOX_MAIN), const),
                  pl.BlockSpec((D_MODEL, LANES), const),
                  pl.BlockSpec((D_MODEL, LANES), const),
                  pl.BlockSpec((1, LANES), const),
                  pl.BlockSpec((1, 2 * D_GRP), const)],
        out_specs=[pl.BlockSpec((None, tm, RWKV_COLS), lambda b, s: (b, s, 0)),
                   pl.BlockSpec((None, tm, FOX_MAIN), lambda b, s: (b, s, 0)),
                   pl.BlockSpec((None, tm, LANES), lambda b, s: (b, s, 0)),
                   pl.BlockSpec((None, tm, LANES), lambda b, s: (b, s, 0))],
        out_shape=[jax.ShapeDtypeStruct((bsz, seq, RWKV_COLS), BF16),
                   jax.ShapeDtypeStruct((bsz, seq, FOX_MAIN), BF16),
                   jax.ShapeDtypeStruct((bsz, seq, LANES), BF16),
                   jax.ShapeDtypeStruct((bsz, seq, LANES), BF16)],
        scratch_shapes=[pltpu.VMEM((1, LANES), F32)],
        compiler_params=pltpu.CompilerParams(
            dimension_semantics=("parallel", "arbitrary"), vmem_limit_bytes=VMEM_LIMIT),
        name="inproj",
    )(x, shift, scale, g, w_r, w_x, w_f_hi, w_f_lo, b_f, qk_gain)


_NN = (((1,), (0,)), ((), ()))
_NT = (((1,), (1,)), ((), ()))
_TN = (((0,), (0,)), ((), ()))
SCAN_N = HEADS_PER_SCAN * CHUNK
BATCH_PER_STEP = 8
INV_LEVELS = 5
M_HEAD, M_STRICT, M_INCL, M_EYE, M_BASE, M_OFF = 0, 1, 2, 3, 4, 5


def _bdot(a, b, dims):
    return lax.dot_general(a, b, dims, preferred_element_type=F32)


def _scan_masks():
    rr, cc = _iota((SCAN_N, SCAN_W), 0), _iota((SCAN_N, SCAN_W), 1)
    ri, ci = _iota((SCAN_N, SCAN_N), 0), _iota((SCAN_N, SCAN_N), 1)
    same = ri // CHUNK == ci // CHUNK
    masks = [rr // CHUNK == cc // HEAD_DIM, same & (ri > ci), same & (ri >= ci), ri == ci,
             (ri // 2 == ci // 2) & (ri > ci)]
    blk = 2
    while blk < CHUNK:
        masks.append((ri // (2 * blk) == ci // (2 * blk)) & (ri // blk != ci // blk) & (ri > ci))
        blk *= 2
    return jnp.stack(masks).astype(BF16)


def _rwkv_kernel(p_ref, masks_ref, mu_ref, w0_ref, w2_ref, a0_ref, a2_ref, g2_ref, kk_ref, ka_ref,
                 rk_ref, gnw_ref, gnb_ref, wgu_ref, wd_ref, o_ref, wgu_bf_ref, wd_bf_ref,
                 last_ref, state_ref):
    wgu_bf_ref[...] = wgu_ref[...].astype(BF16)
    wd_bf_ref[...] = wd_ref[...].astype(BF16)

    @pl.when(pl.program_id(1) == 0)
    def _():
        last_ref[...] = jnp.zeros_like(last_ref)
        state_ref[...] = jnp.zeros_like(state_ref)

    mu, w0, w2, a0, a2, g2, k_k, k_a, r_k, gn_w, gn_b = (
        ref[...] for ref in (mu_ref, w0_ref, w2_ref, a0_ref, a2_ref, g2_ref, kk_ref, ka_ref,
                             rk_ref, gnw_ref, gnb_ref))
    rows = BATCH_PER_STEP * CHUNK
    p = p_ref[...].astype(F32).reshape(rows, RWKV_COLS)
    row_id = _iota((rows, 1), 0)
    prev = pltpu.roll(p, 1, axis=0)
    for bb in range(BATCH_PER_STEP):
        prev = jnp.where(row_id == bb * CHUNK, last_ref[bb], prev)
        last_ref[bb] = p[(bb + 1) * CHUNK - 1:(bb + 1) * CHUNK, :]
    pf = p + mu * (prev - p)
    r = pf[:, 0:D_GRP]
    k = pf[:, D_GRP:2 * D_GRP]
    v = pf[:, 2 * D_GRP:3 * D_GRP]
    lora = pf[:, LORA_OFF:GATE_OFF]
    gd = pf[:, GATE_OFF:RWKV_COLS]

    wlog = w0 + _dot(jnp.tanh(lora), w2)
    neg = -wlog
    softplus = jnp.maximum(neg, 0.0) + jnp.log(1.0 + jnp.exp(-jnp.abs(neg)))
    ld = -jnp.exp(-softplus - 0.5)
    a = _sigmoid(a0 + _dot(lora, a2))
    g = _dot(_sigmoid(gd), g2)

    red, exp_m = _seg_reduce_mat(D_GRP), _seg_expand_mat(D_GRP)
    kk = k * k_k
    n2 = _split_dot(kk * kk, red, SEG_TERMS)
    kk = kk * _split_dot(1.0 / jnp.maximum(jnp.sqrt(n2), 1e-12), exp_m, SEG_TERMS)
    k2 = k * (1.0 + (a - 1.0) * k_a)

    tr, tc = _iota((rows, rows), 0), _iota((rows, rows), 1)
    tri = ((tr >= tc) & (tr // CHUNK == tc // CHUNK)).astype(BF16)
    cl = _split_dot(ld, tri, terms=CUM_TERMS, left=True)
    cl_end = jnp.concatenate(
        [jnp.broadcast_to(cl[(bb + 1) * CHUNK - 1:(bb + 1) * CHUNK, :], (CHUNK, D_GRP))
         for bb in range(BATCH_PER_STEP)], axis=0)
    e_in = jnp.exp(cl)
    e_out = jnp.exp(-cl)
    e_rem = jnp.exp(cl_end - cl)
    p_end = jnp.exp(cl_end)
    kka = kk * a
    ops = [(-kk * jnp.exp(cl - ld)).astype(BF16), (kka * e_out).astype(BF16),
           (k2 * e_out).astype(BF16), (r * e_in).astype(BF16), v.astype(BF16),
           (kka * e_rem).astype(BF16), (k2 * e_rem).astype(BF16)]

    chains = [(bb, grp) for bb in range(BATCH_PER_STEP)
              for grp in range(N_HEADS // HEADS_PER_SCAN)]
    head_mask = masks_ref[M_HEAD]
    strict, incl = masks_ref[M_STRICT], masks_ref[M_INCL]

    def stacked(op, bb, grp):
        part = op[bb * CHUNK:(bb + 1) * CHUNK, grp * SCAN_W:(grp + 1) * SCAN_W]
        return jnp.concatenate([part] * HEADS_PER_SCAN, axis=0) * head_mask

    xs = [[stacked(op, bb, grp) for op in ops] for bb, grp in chains]
    st = [state_ref[bb, grp] for bb, grp in chains]
    sb = [s.astype(BF16) for s in st]
    nab = [_bdot(x[0], x[1], _NT).astype(BF16) for x in xs]
    aak = [_bdot(x[0], x[2], _NT).astype(BF16) * strict for x in xs]
    arb = [_bdot(x[3], x[1], _NT).astype(BF16) * incl for x in xs]
    ark = [_bdot(x[3], x[2], _NT).astype(BF16) * incl for x in xs]
    t_inv = [masks_ref[M_EYE] + n * masks_ref[M_BASE] for n in nab]
    for lvl in range(INV_LEVELS):
        half = [_bdot(t, n * masks_ref[M_OFF + lvl], _NN).astype(BF16) for t, n in zip(t_inv, nab)]
        t_inv = [t + _bdot(h, t, _NN).astype(BF16) for t, h in zip(t_inv, half)]
    rhs = [(_bdot(x[0], s, _NT) + _bdot(k, x[4], _NN)).astype(BF16)
           for x, s, k in zip(xs, sb, aak)]
    sa = [_bdot(t, h, _NN).astype(BF16) for t, h in zip(t_inv, rhs)]
    ys = [_bdot(x[3], s, _NT) + _bdot(b, u, _NN) + _bdot(k, x[4], _NN)
          for x, s, b, u, k in zip(xs, sb, arb, sa, ark)]
    for (bb, grp), x, s, u in zip(chains, xs, st, sa):
        decay = p_end[bb * CHUNK:bb * CHUNK + 1, grp * SCAN_W:(grp + 1) * SCAN_W]
        state_ref[bb, grp] = s * decay + _bdot(u, x[5], _TN) + _bdot(x[4], x[6], _TN)
    ys = [y[0:CHUNK] + y[CHUNK:2 * CHUNK] + y[2 * CHUNK:3 * CHUNK] + y[3 * CHUNK:4 * CHUNK]
          for y in ys]
    n_grp = N_HEADS // HEADS_PER_SCAN
    y = jnp.concatenate([jnp.concatenate(ys[bb * n_grp:(bb + 1) * n_grp], axis=1)
                         for bb in range(BATCH_PER_STEP)], axis=0)

    mean = _split_dot(_split_dot(y, red, SEG_TERMS) * (1.0 / HEAD_DIM), exp_m, SEG_TERMS)
    d = y - mean
    var = _split_dot(d * d, red, SEG_TERMS) * (1.0 / HEAD_DIM)
    yn = d * _split_dot(lax.rsqrt(var + GN_EPS), exp_m, SEG_TERMS) * gn_w + gn_b
    bonus = _split_dot(_split_dot(r * k2 * r_k, red, SEG_TERMS), exp_m, SEG_TERMS) * v
    o_ref[...] = ((yn + bonus) * g).astype(BF16).reshape(BATCH_PER_STEP, CHUNK, D_GRP)


def _rwkv(p_r, mu, w0, w2p, a0, a2p, g2, k_k, k_a, r_k, gn_w, gn_b, w_gate_up, w_down):
    bsz, seq, _ = p_r.shape
    assert bsz % BATCH_PER_STEP == 0
    n_chunk = seq // CHUNK
    n_step = (bsz // BATCH_PER_STEP) * n_chunk
    wgu2d = w_gate_up.reshape(-1, w_gate_up.shape[-1])
    wd2d = w_down.reshape(-1, w_down.shape[-1])
    assert wgu2d.shape[0] % (8 * n_step) == 0 and wd2d.shape[0] == wgu2d.shape[0]
    slab = wgu2d.shape[0] // n_step
    masks = _scan_masks()
    const = lambda b, s: (0, 0)
    step = lambda b, s: (b * n_chunk + s, 0)
    vec = pl.BlockSpec((1, D_GRP), const)
    y, wgu_bf, wd_bf = pl.pallas_call(
        _rwkv_kernel,
        grid=(bsz // BATCH_PER_STEP, n_chunk),
        in_specs=[pl.BlockSpec((BATCH_PER_STEP, CHUNK, RWKV_COLS), lambda b, s: (b, s, 0)),
                  pl.BlockSpec(masks.shape, lambda b, s: (0, 0, 0)),
                  pl.BlockSpec((1, RWKV_COLS), const),
                  vec, pl.BlockSpec((LANES, D_GRP), const),
                  vec, pl.BlockSpec((LANES, D_GRP), const),
                  pl.BlockSpec((LANES, D_GRP), const),
                  vec, vec, vec, vec, vec,
                  pl.BlockSpec((slab, wgu2d.shape[1]), step),
                  pl.BlockSpec((slab, wd2d.shape[1]), step)],
        out_specs=[pl.BlockSpec((BATCH_PER_STEP, CHUNK, D_GRP), lambda b, s: (b, s, 0)),
                   pl.BlockSpec((slab, wgu2d.shape[1]), step),
                   pl.BlockSpec((slab, wd2d.shape[1]), step)],
        out_shape=[jax.ShapeDtypeStruct((bsz, seq, D_GRP), BF16),
                   jax.ShapeDtypeStruct(wgu2d.shape, BF16),
                   jax.ShapeDtypeStruct(wd2d.shape, BF16)],
        scratch_shapes=[pltpu.VMEM((BATCH_PER_STEP, 1, RWKV_COLS), F32),
                        pltpu.VMEM((BATCH_PER_STEP, N_HEADS // HEADS_PER_SCAN, SCAN_W, SCAN_W), F32)],
        compiler_params=pltpu.CompilerParams(
            dimension_semantics=("parallel", "arbitrary"), vmem_limit_bytes=VMEM_LIMIT),
        name="rwkv",
    )(p_r, masks, mu, w0, w2p, a0, a2p, g2, k_k, k_a, r_k, gn_w, gn_b, wgu2d, wd2d)
    return y, wgu_bf.reshape(w_gate_up.shape), wd_bf.reshape(w_down.shape)


def _fox_kernel(q_ref, qb_ref, k_ref, kb_ref, vt_ref, og_ref, ong_ref, o_ref, m_ref, l_ref, acc_ref,
                *, seq):
    hp = pl.program_id(1)
    lane = _iota((1, LANES), 1)
    q = q_ref[...]
    qb = qb_ref[...]
    zero = jnp.zeros_like(q)
    qcat = [jnp.concatenate([jnp.where(lane // HEAD_DIM == hh, q, zero),
                             jnp.where(lane // 8 == hp * 2 + hh, qb, zero)], axis=1)
            for hh in range(2)]
    keys = min(FOX_SUB_KEYS, seq)
    n_sub = seq // keys
    diag = _iota((keys, keys), 1) >= _iota((keys, keys), 0)

    m_ref[...] = jnp.full(m_ref.shape, -jnp.inf, F32)
    l_ref[...] = jnp.zeros(l_ref.shape, F32)
    acc_ref[...] = jnp.zeros(acc_ref.shape, F32)

    def scores(s):
        lo = s * keys
        kcat = jnp.concatenate([k_ref[lo:lo + keys, :], kb_ref[lo:lo + keys, :]], axis=1)
        return [lax.dot_general(kcat, qc[lo:, :], _NT, preferred_element_type=F32)
                for qc in qcat]

    pending = scores(0)
    for s in range(n_sub):
        lo = s * keys
        nxt = scores(s + 1) if s + 1 < n_sub else None
        vt = vt_ref[:, lo:lo + keys]
        sts = [jnp.concatenate([jnp.where(diag, st[:, :keys], -jnp.inf), st[:, keys:]], axis=1)
               if st.shape[1] > keys else jnp.where(diag, st, -jnp.inf) for st in pending]
        m_old = [m_ref[hh, :, lo:] for hh in range(2)]
        m_new = [jnp.maximum(m, jnp.max(st, axis=0, keepdims=True)) for m, st in zip(m_old, sts)]
        pts = [jnp.exp(st - m) for st, m in zip(sts, m_new)]
        pvs = [jnp.dot(vt, pt.astype(BF16), preferred_element_type=F32) for pt in pts]
        for hh in range(2):
            alpha = jnp.exp(m_old[hh] - m_new[hh])
            m_ref[hh, :, lo:] = m_new[hh]
            l_ref[hh, :, lo:] = alpha * l_ref[hh, :, lo:] + jnp.sum(pts[hh], axis=0, keepdims=True)
            acc_ref[hh, :, lo:] = (alpha * acc_ref[hh, :, lo:]
                                   + pvs[hh][hh * HEAD_DIM:(hh + 1) * HEAD_DIM, :])
        pending = nxt

    outs = []
    for hh in range(2):
        o = acc_ref[hh] / l_ref[hh]
        outs.append(o * lax.rsqrt(jnp.mean(o * o, axis=0, keepdims=True) + NORM_EPS))
    o = jnp.concatenate(outs, axis=0).T
    o_ref[...] = (o * ong_ref[...] * _sigmoid(og_ref[...].astype(F32))).astype(BF16)


def _fox(p_x, k_bias, q_bias, o_gain):
    bsz, seq, _ = p_x.shape
    npair = N_HEADS // 2
    v_t = jnp.transpose(p_x[:, :, 2 * D_GRP:3 * D_GRP], (0, 2, 1))
    return pl.pallas_call(
        functools.partial(_fox_kernel, seq=seq),
        grid=(bsz, npair),
        in_specs=[pl.BlockSpec((None, seq, LANES), lambda b, h: (b, 0, h)),
                  pl.BlockSpec((None, seq, LANES), lambda b, h: (b, 0, 0)),
                  pl.BlockSpec((None, seq, LANES), lambda b, h: (b, 0, npair + h)),
                  pl.BlockSpec((None, seq, LANES), lambda b, h: (b, 0, 0)),
                  pl.BlockSpec((None, LANES, seq), lambda b, h: (b, h, 0)),
                  pl.BlockSpec((None, seq, LANES), lambda b, h: (b, 0, 3 * npair + h)),
                  pl.BlockSpec((1, LANES), lambda b, h: (0, 0))],
        out_specs=pl.BlockSpec((None, seq, LANES), lambda b, h: (b, 0, h)),
        out_shape=jax.ShapeDtypeStruct((bsz, seq, D_GRP), BF16),
        scratch_shapes=[pltpu.VMEM((2, 1, seq), F32), pltpu.VMEM((2, 1, seq), F32),
                        pltpu.VMEM((2, HEAD_DIM, seq), F32)],
        compiler_params=pltpu.CompilerParams(
            dimension_semantics=("parallel", "parallel"), vmem_limit_bytes=VMEM_LIMIT),
        name="fox",
    )(p_x, q_bias, p_x, k_bias, v_t, p_x, o_gain)


def _outproj_kernel(x_ref, yr_ref, yf_ref, g1_ref, sh_ref, sc_ref, ng_ref, wor_ref, wof_ref,
                    wrt_ref, wrl_ref, brt_ref, x1_ref, h2_ref, idx_ref, gate_ref, rank_ref, cnt_ref,
                    carry_ref):
    @pl.when(pl.program_id(0) == 0)
    def _():
        carry_ref[...] = jnp.zeros_like(carry_ref)

    y = (jnp.dot(yr_ref[...], wor_ref[...], preferred_element_type=F32)
         + jnp.dot(yf_ref[...], wof_ref[...], preferred_element_type=F32))
    x1 = x_ref[...] + g1_ref[...] * y
    x1_ref[...] = x1
    tm = x1.shape[0]
    h = x1 * lax.rsqrt(jnp.mean(x1 * x1, axis=-1, keepdims=True) + NORM_EPS) * ng_ref[...]
    h2 = h * (1.0 + sc_ref[...]) + sh_ref[...]
    h2_ref[...] = _pack_rows(h2)

    h_hi = h2.astype(BF16)
    h_lo = (h2 - h_hi.astype(F32)).astype(BF16)
    logits = (lax.dot_general(wrt_ref[...], h_hi, _NT, preferred_element_type=F32)
              + lax.dot_general(wrt_ref[...], h_lo, _NT, preferred_element_type=F32)
              + lax.dot_general(wrl_ref[...], h_hi, _NT, preferred_element_type=F32))
    lg = logits[:N_EXPERTS, :] + brt_ref[...]
    expert = _iota((N_EXPERTS, tm), 0)
    picks = []
    hot_sum = jnp.zeros((N_EXPERTS, tm), F32)
    for _ in range(TOP_K):
        m = jnp.max(lg, axis=0, keepdims=True)
        sel = jnp.min(jnp.where(lg == m, expert, N_EXPERTS), axis=0, keepdims=True)
        hot = expert == sel
        picks.append((m, sel, hot))
        hot_sum = hot_sum + hot.astype(F32)
        lg = jnp.where(hot, -jnp.inf, lg)
    es = [jnp.exp(m - picks[0][0]) for m, _, _ in picks]
    den = es[0] + es[1] + es[2] + es[3]

    earlier = (_iota((tm, tm), 0) < _iota((tm, tm), 1)).astype(BF16)
    before = jnp.dot(hot_sum.astype(BF16), earlier, preferred_element_type=F32) + carry_ref[...]
    ranks = [jnp.sum(jnp.where(hot, before, 0.0), axis=0, keepdims=True).astype(jnp.int32)
             for _, _, hot in picks]
    pad_i = jnp.zeros((8 - TOP_K, tm), jnp.int32)
    idx_ref[...] = jnp.concatenate([sel for _, sel, _ in picks] + [pad_i], axis=0)
    gate_ref[...] = jnp.concatenate([e / den for e in es] + [pad_i.astype(F32)], axis=0)
    rank_ref[...] = jnp.concatenate(ranks + [pad_i], axis=0)
    carry_ref[...] = carry_ref[...] + jnp.sum(hot_sum, axis=1, keepdims=True)
    cnt_ref[...] = jnp.broadcast_to(carry_ref[...], cnt_ref.shape)


def _outproj(x2d, y_r, y_f, gate1, shift2, scale2, norm_g, wo_r, wo_f, w_rt, b_rt, tm, seq,
             row0, t):
    w_rt_hi = w_rt.astype(BF16)
    w_rt_lo = (w_rt - w_rt_hi.astype(F32)).astype(BF16)
    per_b = seq // tm
    blk0 = row0 // tm
    const = lambda i: (0, 0)
    rows = lambda i: (i, 0)
    rows_in = lambda i: (i + blk0, 0)
    mod = pl.BlockSpec((None, 1, D_MODEL), lambda i: ((i + blk0) // per_b, 0, 0))
    return pl.pallas_call(
        _outproj_kernel,
        grid=(t // tm,),
        in_specs=[pl.BlockSpec((tm, D_MODEL), rows_in),
                  pl.BlockSpec((tm, D_GRP), rows_in),
                  pl.BlockSpec((tm, D_GRP), rows_in),
                  mod, mod, mod,
                  pl.BlockSpec((1, D_MODEL), const),
                  pl.BlockSpec((D_GRP, D_MODEL), const),
                  pl.BlockSpec((D_GRP, D_MODEL), const),
                  pl.BlockSpec((LANES, D_MODEL), const),
                  pl.BlockSpec((LANES, D_MODEL), const),
                  pl.BlockSpec((N_EXPERTS, 1), const)],
        out_specs=[pl.BlockSpec((tm, D_MODEL), rows),
                   pl.BlockSpec((tm, D_PACK), rows),
                   pl.BlockSpec((8, tm), lambda i: (0, i)),
                   pl.BlockSpec((8, tm), lambda i: (0, i)),
                   pl.BlockSpec((8, tm), lambda i: (0, i)),
                   pl.BlockSpec((N_EXPERTS, LANES), const)],
        out_shape=[jax.ShapeDtypeStruct((t, D_MODEL), F32),
                   jax.ShapeDtypeStruct((t, D_PACK), jnp.uint32),
                   jax.ShapeDtypeStruct((8, t), jnp.int32),
                   jax.ShapeDtypeStruct((8, t), F32),
                   jax.ShapeDtypeStruct((8, t), jnp.int32),
                   jax.ShapeDtypeStruct((N_EXPERTS, LANES), F32)],
        scratch_shapes=[pltpu.VMEM((N_EXPERTS, 1), F32)],
        compiler_params=pltpu.CompilerParams(
            dimension_semantics=("arbitrary",), vmem_limit_bytes=VMEM_LIMIT),
        name="outproj",
    )(x2d, y_r, y_f, gate1, shift2, scale2, norm_g, wo_r, wo_f, w_rt_hi, w_rt_lo, b_rt)


SC_CORES = 2
SC_SUBCORES = 16
SC_ROWS = 64


def _sc_gather_rows(idx, src):
    n_workers = SC_CORES * SC_SUBCORES
    m = idx.shape[0]
    d = src.shape[1]
    assert m % (n_workers * SC_ROWS) == 0
    n_chunks = m // (n_workers * SC_ROWS)
    mesh = plsc.VectorSubcoreMesh(core_axis_name="c", subcore_axis_name="s")

    @functools.partial(
        pl.kernel, mesh=mesh,
        out_type=jax.ShapeDtypeStruct((m, d), src.dtype),
        scratch_types=[pltpu.VMEM((n_chunks, SC_ROWS), jnp.int32),
                       pltpu.VMEM((SC_ROWS, d), src.dtype),
                       pltpu.SemaphoreType.DMA],
        name="sc_gather")
    def gather(src_hbm, idx_hbm, out_hbm, idx_v, rows_v, sem):
        wid = lax.axis_index("s") * SC_CORES + lax.axis_index("c")
        pltpu.sync_copy(idx_hbm.at[wid], idx_v)

        @pl.loop(0, n_chunks)
        def _(j):
            pltpu.async_copy(src_hbm.at[idx_v.at[j]], rows_v, sem).wait()
            pltpu.sync_copy(rows_v, out_hbm.at[pl.ds((wid * n_chunks + j) * SC_ROWS, SC_ROWS)])

    return gather(src, idx.reshape(n_workers, n_chunks, SC_ROWS))


def _sc_scatter_rows(src, dest, n_out):
    n_workers = SC_CORES * SC_SUBCORES
    t, d = src.shape
    n_slot = dest.shape[0]
    assert t % (n_workers * SC_ROWS) == 0
    n_chunks = t // (n_workers * SC_ROWS)
    mesh = plsc.VectorSubcoreMesh(core_axis_name="c", subcore_axis_name="s")
    idx = dest.reshape(n_slot, n_workers, n_chunks, SC_ROWS).transpose(1, 2, 0, 3)
    idx = idx.reshape(n_workers, n_chunks * n_slot, SC_ROWS)

    @functools.partial(
        pl.kernel, mesh=mesh,
        out_type=jax.ShapeDtypeStruct((n_out, d), src.dtype),
        scratch_types=[pltpu.VMEM((n_chunks * n_slot, SC_ROWS), jnp.int32),
                       pltpu.VMEM((SC_ROWS, d), src.dtype)],
        name="sc_scatter")
    def scatter(src_hbm, idx_hbm, out_hbm, idx_v, rows_v):
        wid = lax.axis_index("s") * SC_CORES + lax.axis_index("c")
        pltpu.sync_copy(idx_hbm.at[wid], idx_v)

        @pl.loop(0, n_chunks)
        def _(j):
            pltpu.sync_copy(src_hbm.at[pl.ds((wid * n_chunks + j) * SC_ROWS, SC_ROWS)], rows_v)
            for k in range(n_slot):
                pltpu.sync_copy(rows_v, out_hbm.at[idx_v.at[j * n_slot + k]])

    return scatter(src, idx)


def _expert_kernel(be_ref, nv_ref, x_ref, wgu_ref, bgu_ref, wd_ref, bd_ref, o_ref):
    del be_ref
    valid = _iota((EXPERT_BLOCK, 1), 0) < nv_ref[pl.program_id(0)]
    lo, hi = _unpack_rows(jnp.where(valid, x_ref[...], jnp.uint32(0)))
    x = jnp.concatenate([lo.astype(BF16), hi.astype(BF16)], axis=1)
    gu = jnp.dot(x, wgu_ref[...], preferred_element_type=F32) + bgu_ref[...]
    gate = jnp.minimum(gu[:, :D_MODEL], SWIGLU_LIMIT)
    up = jnp.clip(gu[:, D_MODEL:], -SWIGLU_LIMIT, SWIGLU_LIMIT)
    act = gate * _sigmoid(SWIGLU_ALPHA * gate) * (up + 1.0)
    o_ref[...] = _pack_rows(
        jnp.dot(act.astype(BF16), wd_ref[...], preferred_element_type=F32) + bd_ref[...])


def _experts(block_e, n_valid, xs, w_gu, b_gu, w_d, b_d):
    n_blocks = block_e.shape[0]
    grid_spec = pltpu.PrefetchScalarGridSpec(
        num_scalar_prefetch=2,
        grid=(n_blocks,),
        in_specs=[pl.BlockSpec((EXPERT_BLOCK, D_PACK), lambda j, be, nv: (j, 0)),
                  pl.BlockSpec((None, D_MODEL, 2 * D_MODEL), lambda j, be, nv: (be[j], 0, 0)),
                  pl.BlockSpec((None, 1, 2 * D_MODEL), lambda j, be, nv: (be[j], 0, 0)),
                  pl.BlockSpec((None, D_MODEL, D_MODEL), lambda j, be, nv: (be[j], 0, 0)),
                  pl.BlockSpec((None, 1, D_MODEL), lambda j, be, nv: (be[j], 0, 0))],
        out_specs=pl.BlockSpec((EXPERT_BLOCK, D_PACK), lambda j, be, nv: (j, 0)),
    )
    return pl.pallas_call(
        _expert_kernel,
        grid_spec=grid_spec,
        out_shape=jax.ShapeDtypeStruct(xs.shape, jnp.uint32),
        compiler_params=pltpu.CompilerParams(
            dimension_semantics=("arbitrary",), vmem_limit_bytes=VMEM_LIMIT),
        name="experts",
    )(block_e, n_valid, xs, w_gu, b_gu, w_d, b_d)


COMBINE_TOKENS = 512
MOE_SPLITS = 2


def _combine_kernel(yg_ref, x1_ref, gate_ref, g2_ref, fg_ref, o_ref):
    gates = gate_ref[...].T
    acc_lo = acc_hi = None
    for kk in range(TOP_K):
        lo, hi = _unpack_rows(yg_ref[kk * COMBINE_TOKENS:(kk + 1) * COMBINE_TOKENS, :])
        g = gates[:, kk:kk + 1]
        acc_lo = g * lo if acc_lo is None else acc_lo + g * lo
        acc_hi = g * hi if acc_hi is None else acc_hi + g * hi
    x2 = x1_ref[...] + g2_ref[...] * jnp.concatenate([acc_lo, acc_hi], axis=1)
    o_ref[...] = x2 * lax.rsqrt(jnp.mean(x2 * x2, axis=-1, keepdims=True) + NORM_EPS) * fg_ref[...]


def _combine_kernel_into(prev_ref, *refs):
    del prev_ref
    _combine_kernel(*refs)


def _combine(yg, x1, gates, gate2, final_g, seq, row0, t_total, prev):
    t = x1.shape[0]
    tm = COMBINE_TOKENS
    per_b = seq // tm
    blk0 = row0 // tm
    rows = lambda i: (i, 0)
    in_specs = [pl.BlockSpec((TOP_K * tm, D_PACK), rows),
                pl.BlockSpec((tm, D_MODEL), rows),
                pl.BlockSpec((8, tm), lambda i: (0, i)),
                pl.BlockSpec((None, 1, D_MODEL), lambda i: ((i + blk0) // per_b, 0, 0)),
                pl.BlockSpec((1, D_MODEL), lambda i: (0, 0))]
    args = (yg, x1, gates, gate2, final_g)
    if prev is not None:
        in_specs = [pl.BlockSpec(memory_space=pl.ANY)] + in_specs
        args = (prev,) + args
    return pl.pallas_call(
        _combine_kernel if prev is None else _combine_kernel_into,
        grid=(t // tm,),
        in_specs=in_specs,
        out_specs=pl.BlockSpec((tm, D_MODEL), lambda i: (i + blk0, 0)),
        out_shape=jax.ShapeDtypeStruct((t_total, D_MODEL), F32),
        input_output_aliases={} if prev is None else {0: 0},
        compiler_params=pltpu.CompilerParams(
            dimension_semantics=("parallel",), vmem_limit_bytes=VMEM_LIMIT),
        name="combine",
    )(*args)


def _moe(h2, idx, gates, rank, counts, x1, gate2, final_g, w_gu, b_gu, w_d, b_d, seq,
         row0, t_total, prev):
    t = h2.shape[0]
    n_slots = t * TOP_K
    n_blocks = -(-n_slots // EXPERT_BLOCK) + N_EXPERTS
    cap = n_blocks * EXPERT_BLOCK
    padded = (counts + EXPERT_BLOCK - 1) // EXPERT_BLOCK * EXPERT_BLOCK
    pad_ends = jnp.cumsum(padded)
    pad_starts = pad_ends - padded
    experts = jnp.arange(N_EXPERTS, dtype=jnp.int32)
    dest = jnp.sum(jnp.where(idx[..., None] == experts, pad_starts, 0), axis=-1) + rank
    block_starts = jnp.arange(n_blocks, dtype=jnp.int32) * EXPERT_BLOCK
    block_e = jnp.minimum(jnp.sum(block_starts[:, None] >= pad_ends[None, :], axis=1),
                          N_EXPERTS - 1).astype(jnp.int32)
    n_valid = jnp.clip(counts[block_e] - (block_starts - pad_starts[block_e]), 0, EXPERT_BLOCK)

    xs = _sc_scatter_rows(h2, dest, cap)
    yb = _experts(block_e, n_valid.astype(jnp.int32), xs, w_gu, b_gu, w_d, b_d)
    dest_blocks = dest.reshape(TOP_K, -1, COMBINE_TOKENS).transpose(1, 0, 2).reshape(-1)
    yg = _sc_gather_rows(dest_blocks, yb)
    return _combine(yg, x1, gates, gate2, final_g, seq, row0, t_total, prev)


def _layer(x, c_mod, norm1_g, w_in, mu_shift, w0, w2, a0, a2, g2, k_k, k_a, r_k, gn_w, gn_b, b_f,
           q_norm_g, k_norm_g, o_norm_g, w_out, norm2_g, w_router, b_router, w_gate_up,
           b_gate_up, w_down, b_down, final_g, tm_in, tm_out):
    bsz, seq, _ = x.shape
    shift1, scale1, gate1, shift2, scale2, gate2 = (
        m.reshape(bsz, 1, D_MODEL) for m in jnp.split(c_mod, 6, axis=-1))
    row = lambda v: v.reshape(1, -1)

    w_r = w_in[:, :RWKV_COLS].astype(BF16)
    w_x = w_in[:, RWKV_COLS:RWKV_COLS + FOX_MAIN].astype(BF16)
    w_f = jnp.pad(w_in[:, RWKV_COLS + FOX_MAIN:], ((0, 0), (0, LANES - N_HEADS)))
    b_fp = jnp.pad(b_f, (0, LANES - N_HEADS)).reshape(1, LANES)
    qk_gain = jnp.concatenate([jnp.tile(q_norm_g, N_HEADS) * HEAD_DIM ** -0.5,
                               jnp.tile(k_norm_g, N_HEADS)]).reshape(1, -1)
    p_r, p_x, k_bias, q_bias = _inproj(x, shift1, scale1, row(norm1_g), w_r, w_x, w_f, b_fp,
                                       qk_gain, tm_in)

    zeros = jnp.zeros((LANES - 64, D_GRP), F32)
    w2p = jnp.concatenate([w2, zeros], axis=0).astype(BF16)
    a2p = jnp.concatenate([zeros, a2], axis=0).astype(BF16)
    y_r, w_gu, w_d = _rwkv(p_r, row(mu_shift), row(w0), w2p, row(a0), a2p, g2.astype(BF16),
                           row(k_k), row(k_a), row(r_k), row(gn_w), row(gn_b), w_gate_up, w_down)

    y_f = _fox(p_x, k_bias, q_bias, jnp.tile(o_norm_g, 2).reshape(1, LANES))

    t = bsz * seq
    w_rt = jnp.pad(w_router.T, ((0, LANES - N_EXPERTS), (0, 0)))
    b_rt = b_router.reshape(N_EXPERTS, 1)
    wo = w_out.astype(BF16)
    b_gu, b_d = b_gate_up.reshape(N_EXPERTS, 1, -1), b_down.reshape(N_EXPERTS, 1, -1)
    t_part = t // MOE_SPLITS
    out = None
    for part in range(MOE_SPLITS):
        row0 = part * t_part
        x1, h2, idx, gates, rank, cnt = _outproj(
            x.reshape(t, D_MODEL), y_r.reshape(t, D_GRP), y_f.reshape(t, D_GRP), gate1, shift2,
            scale2, row(norm2_g), wo[:D_GRP], wo[D_GRP:], w_rt, b_rt, tm_out, seq, row0, t_part)
        counts = cnt[:, 0].astype(jnp.int32)
        out = _moe(h2, idx[:TOP_K], gates, rank[:TOP_K], counts, x1, gate2, row(final_g),
                   w_gu, b_gu, w_d, b_d, seq, row0, t, out)
    return out.reshape(bsz, seq, D_MODEL)


def kernel(x, c, w_ada, b_ada, norm1_g, w_in, mu_shift, w0, w2, a0, a2, g2, k_k, k_a, r_k, gn_w,
           gn_b, b_f, q_norm_g, k_norm_g, o_norm_g, w_out, norm2_g, w_router, b_router, w_gate_up,
           b_gate_up, w_down, b_down, final_g):
    assert w_ada.shape[0] == 1, "single-layer block"
    c_mod = _adaln(c, w_ada[0], b_ada[0])
    return _layer(x, c_mod, norm1_g[0], w_in[0], mu_shift[0], w0[0], w2[0], a0[0], a2[0], g2[0],
                  k_k[0], k_a[0], r_k[0], gn_w[0], gn_b[0], b_f[0], q_norm_g[0], k_norm_g[0],
                  o_norm_g[0], w_out[0], norm2_g[0], w_router[0], b_router[0], w_gate_up[0],
                  b_gate_up[0], w_down[0], b_down[0], final_g,
                  tm_in=min(512, x.shape[1]), tm_out=min(1024, x.shape[1]))
```

```python
import functools

import jax
import jax.numpy as jnp
from jax import lax
from jax.experimental import pallas as pl
from jax.experimental.pallas import tpu as pltpu
from jax.experimental.pallas import tpu_sc as plsc

F32 = jnp.float32
BF16 = jnp.bfloat16
HIGHEST = lax.Precision.HIGHEST

D_MODEL = 1024
HEAD_DIM = 64
N_HEADS = 8
D_GRP = N_HEADS * HEAD_DIM
RWKV_COLS = 1792
LORA_OFF = 3 * D_GRP
GATE_OFF = LORA_OFF + 128
FOX_MAIN = 4 * D_GRP
N_EXPERTS = 32
TOP_K = 4
EXPERT_BLOCK = 512
SWIGLU_ALPHA = 1.702
SWIGLU_LIMIT = 7.0
NORM_EPS = 1e-6
GN_EPS = 64e-5
LANES = 128
CHUNK = 64
FOX_SUB_KEYS = 512
HEADS_PER_SCAN = 4
SCAN_W = HEADS_PER_SCAN * HEAD_DIM
SEG_TERMS = 1
CUM_TERMS = 2
VMEM_LIMIT = 56 * 1024 * 1024


def _dot(a, b):
    return jnp.dot(a.astype(BF16), b.astype(BF16), preferred_element_type=F32)


def _dot_nt(a, b):
    return lax.dot_general(a.astype(BF16), b.astype(BF16), (((1,), (1,)), ((), ())),
                           preferred_element_type=F32)


def _dot_tn(a, b):
    return lax.dot_general(a.astype(BF16), b.astype(BF16), (((0,), (0,)), ((), ())),
                           preferred_element_type=F32)


def _fdot(a, b):
    return jnp.dot(a, b, precision=HIGHEST, preferred_element_type=F32)


def _split_dot(x, m, terms=2, left=False):
    acc = None
    rem = x
    for _ in range(terms):
        part = rem.astype(BF16)
        rem = rem - part.astype(F32)
        d = (jnp.dot(m, part, preferred_element_type=F32) if left
             else jnp.dot(part, m, preferred_element_type=F32))
        acc = d if acc is None else acc + d
    return acc


def _iota(shape, dim):
    return lax.broadcasted_iota(jnp.int32, shape, dim)


def _seg_reduce_mat(n):
    return (_iota((n, LANES), 0) // HEAD_DIM == _iota((n, LANES), 1)).astype(BF16)


def _seg_expand_mat(n):
    return (_iota((LANES, n), 1) // HEAD_DIM == _iota((LANES, n), 0)).astype(BF16)


def _tri(n, strict):
    r, c = _iota((n, n), 0), _iota((n, n), 1)
    return ((r > c) if strict else (r >= c)).astype(BF16)


D_PACK = D_MODEL // 2


def _pack_rows(x):
    lo = lax.bitcast_convert_type(x[:, :D_PACK].astype(BF16).astype(F32), jnp.uint32)
    hi = lax.bitcast_convert_type(x[:, D_PACK:].astype(BF16).astype(F32), jnp.uint32)
    return hi | (lo >> 16)


def _unpack_rows(p):
    lo = lax.bitcast_convert_type(p << 16, F32)
    hi = lax.bitcast_convert_type(p & jnp.uint32(0xFFFF0000), F32)
    return lo, hi


def _log_sigmoid(z):
    return jnp.minimum(z, 0.0) - jnp.log(1.0 + jnp.exp(-jnp.abs(z)))


def _sigmoid(z):
    return 1.0 / (1.0 + jnp.exp(-z))


def _adaln_kernel(c_ref, w_ref, b_ref, o_ref):
    c = c_ref[...]
    o_ref[...] = _fdot(c * _sigmoid(c), w_ref[...]) + b_ref[...]


def _adaln(c, w_ada, b_ada):
    bsz = c.shape[0]
    n_mod = w_ada.shape[1] // D_MODEL
    return pl.pallas_call(
        _adaln_kernel,
        grid=(n_mod,),
        in_specs=[pl.BlockSpec((bsz, D_MODEL), lambda j: (0, 0)),
                  pl.BlockSpec((D_MODEL, D_MODEL), lambda j: (0, j)),
                  pl.BlockSpec((1, D_MODEL), lambda j: (0, j))],
        out_specs=pl.BlockSpec((bsz, D_MODEL), lambda j: (0, j)),
        out_shape=jax.ShapeDtypeStruct((bsz, n_mod * D_MODEL), F32),
        name="adaln",
    )(c, w_ada, b_ada.reshape(1, -1))


def _inproj_kernel(x_ref, sh_ref, sc_ref, g_ref, wr_ref, wx_ref, wft_ref, bf_ref, qkg_ref,
                   pr_ref, px_ref, kb_ref, qb_ref, carry_ref):
    @pl.when(pl.program_id(1) == 0)
    def _():
        carry_ref[...] = jnp.zeros_like(carry_ref)

    x = x_ref[...]
    tm = x.shape[0]
    h = x * lax.rsqrt(jnp.mean(x * x, axis=-1, keepdims=True) + NORM_EPS) * g_ref[...]
    h = h * (1.0 + sc_ref[...]) + sh_ref[...]
    hb = h.astype(BF16)

    pr_ref[...] = jnp.dot(hb, wr_ref[...], preferred_element_type=F32).astype(BF16)

    px = jnp.dot(hb, wx_ref[...], preferred_element_type=F32)
    qk = px[:, :2 * D_GRP]
    ss = _split_dot(qk * qk, _seg_reduce_mat(2 * D_GRP), SEG_TERMS)
    inv = lax.rsqrt(ss * (1.0 / HEAD_DIM) + NORM_EPS)
    qk = qk * _split_dot(inv, _seg_expand_mat(2 * D_GRP), SEG_TERMS) * qkg_ref[...]
    px_ref[:, :2 * D_GRP] = qk.astype(BF16)
    px_ref[:, 2 * D_GRP:] = px[:, 2 * D_GRP:].astype(BF16)

    lane = _iota((1, LANES), 1)
    z = jnp.zeros((tm, LANES), F32)
    for hd in range(N_HEADS):
        zh = jnp.sum(h * wft_ref[hd:hd + 1, :], axis=-1, keepdims=True)
        z = jnp.where(lane == hd, zh, z)
    cum = _log_sigmoid(z + bf_ref[...])
    row_id = _iota((tm, 1), 0)
    shift = 1
    while shift < tm:
        cum = cum + jnp.where(row_id >= shift, pltpu.roll(cum, shift, axis=0), 0.0)
        shift *= 2
    cum = cum + carry_ref[...]
    carry_ref[...] = cum[tm - 1:tm, :]

    parts = []
    rem = cum
    for _ in range(3):
        part = rem.astype(BF16)
        rem = rem - part.astype(F32)
        parts.append(part)
    src, dst = _iota((LANES, LANES), 0), _iota((LANES, LANES), 1)

    def spread(offset):
        return sum(jnp.dot(part, ((dst == 8 * src + offset + t) & (src < N_HEADS)).astype(BF16),
                           preferred_element_type=F32) for t, part in enumerate(parts))

    slot = _iota((1, LANES), 1) % 8
    kb_ref[...] = (jnp.where((slot >= 3) & (slot < 6), 1.0, 0.0) - spread(0)).astype(BF16)
    qb_ref[...] = (jnp.where(slot < 3, 1.0, 0.0) + spread(3)).astype(BF16)


def _inproj(x, shift, scale, g, w_r, w_x, w_f_t, b_f, qk_gain, tm):
    bsz, seq, _ = x.shape
    const = lambda b, s: (0, 0)
    return pl.pallas_call(
        _inproj_kernel,
        grid=(bsz, seq // tm),
        in_specs=[pl.BlockSpec((None, tm, D_MODEL), lambda b, s: (b, s, 0)),
                  pl.BlockSpec((None, 1, D_MODEL), lambda b, s: (b, 0, 0)),
                  pl.BlockSpec((None, 1, D_MODEL), lambda b, s: (b, 0, 0)),
                  pl.BlockSpec((1, D_MODEL), const),
                  pl.BlockSpec((D_MODEL, RWKV_COLS), const),
                  pl.BlockSpec((D_MODEL, FOX_MAIN), const),
                  pl.BlockSpec((N_HEADS, D_MODEL), const),
                  pl.BlockSpec((1, LANES), const),
                  pl.BlockSpec((1, 2 * D_GRP), const)],
        out_specs=[pl.BlockSpec((None, tm, RWKV_COLS), lambda b, s: (b, s, 0)),
                   pl.BlockSpec((None, tm, FOX_MAIN), lambda b, s: (b, s, 0)),
                   pl.BlockSpec((None, tm, LANES), lambda b, s: (b, s, 0)),
                   pl.BlockSpec((None, tm, LANES), lambda b, s: (b, s, 0))],
        out_shape=[jax.ShapeDtypeStruct((bsz, seq, RWKV_COLS), BF16),
                   jax.ShapeDtypeStruct((bsz, seq, FOX_MAIN), BF16),
                   jax.ShapeDtypeStruct((bsz, seq, LANES), BF16),
                   jax.ShapeDtypeStruct((bsz, seq, LANES), BF16)],
        scratch_shapes=[pltpu.VMEM((1, LANES), F32)],
        compiler_params=pltpu.CompilerParams(
            dimension_semantics=("parallel", "arbitrary"), vmem_limit_bytes=VMEM_LIMIT),
        name="inproj",
    )(x, shift, scale, g, w_r, w_x, w_f_t, b_f, qk_gain)


_NN = (((1,), (0,)), ((), ()))
_NT = (((1,), (1,)), ((), ()))
_TN = (((0,), (0,)), ((), ()))
SCAN_N = HEADS_PER_SCAN * CHUNK
BATCH_PER_STEP = 8
INV_LEVELS = 5
M_HEAD, M_STRICT, M_INCL, M_EYE, M_BASE, M_OFF = 0, 1, 2, 3, 4, 5


def _bdot(a, b, dims):
    return lax.dot_general(a, b, dims, preferred_element_type=F32)


def _scan_masks():
    rr, cc = _iota((SCAN_N, SCAN_W), 0), _iota((SCAN_N, SCAN_W), 1)
    ri, ci = _iota((SCAN_N, SCAN_N), 0), _iota((SCAN_N, SCAN_N), 1)
    same = ri // CHUNK == ci // CHUNK
    masks = [rr // CHUNK == cc // HEAD_DIM, same & (ri > ci), same & (ri >= ci), ri == ci,
             (ri // 2 == ci // 2) & (ri > ci)]
    blk = 2
    while blk < CHUNK:
        masks.append((ri // (2 * blk) == ci // (2 * blk)) & (ri // blk != ci // blk) & (ri > ci))
        blk *= 2
    return jnp.stack(masks).astype(BF16)


def _rwkv_kernel(p_ref, masks_ref, mu_ref, w0_ref, w2_ref, a0_ref, a2_ref, g2_ref, kk_ref, ka_ref,
                 rk_ref, gnw_ref, gnb_ref, wgu_ref, wd_ref, o_ref, wgu_bf_ref, wd_bf_ref,
                 last_ref, state_ref):
    wgu_bf_ref[...] = wgu_ref[...].astype(BF16)
    wd_bf_ref[...] = wd_ref[...].astype(BF16)

    @pl.when(pl.program_id(1) == 0)
    def _():
        last_ref[...] = jnp.zeros_like(last_ref)
        state_ref[...] = jnp.zeros_like(state_ref)

    mu, w0, w2, a0, a2, g2, k_k, k_a, r_k, gn_w, gn_b = (
        ref[...] for ref in (mu_ref, w0_ref, w2_ref, a0_ref, a2_ref, g2_ref, kk_ref, ka_ref,
                             rk_ref, gnw_ref, gnb_ref))
    rows = BATCH_PER_STEP * CHUNK
    p = p_ref[...].astype(F32).reshape(rows, RWKV_COLS)
    row_id = _iota((rows, 1), 0)
    prev = pltpu.roll(p, 1, axis=0)
    for bb in range(BATCH_PER_STEP):
        prev = jnp.where(row_id == bb * CHUNK, last_ref[bb], prev)
        last_ref[bb] = p[(bb + 1) * CHUNK - 1:(bb + 1) * CHUNK, :]
    pf = p + mu * (prev - p)
    r = pf[:, 0:D_GRP]
    k = pf[:, D_GRP:2 * D_GRP]
    v = pf[:, 2 * D_GRP:3 * D_GRP]
    lora = pf[:, LORA_OFF:GATE_OFF]
    gd = pf[:, GATE_OFF:RWKV_COLS]

    wlog = w0 + _dot(jnp.tanh(lora), w2)
    neg = -wlog
    softplus = jnp.maximum(neg, 0.0) + jnp.log(1.0 + jnp.exp(-jnp.abs(neg)))
    ld = -jnp.exp(-softplus - 0.5)
    a = _sigmoid(a0 + _dot(lora, a2))
    g = _dot(_sigmoid(gd), g2)

    red, exp_m = _seg_reduce_mat(D_GRP), _seg_expand_mat(D_GRP)
    kk = k * k_k
    n2 = _split_dot(kk * kk, red, SEG_TERMS)
    kk = kk * _split_dot(1.0 / jnp.maximum(jnp.sqrt(n2), 1e-12), exp_m, SEG_TERMS)
    k2 = k * (1.0 + (a - 1.0) * k_a)

    tr, tc = _iota((rows, rows), 0), _iota((rows, rows), 1)
    tri = ((tr >= tc) & (tr // CHUNK == tc // CHUNK)).astype(BF16)
    cl = _split_dot(ld, tri, terms=CUM_TERMS, left=True)
    cl_end = jnp.concatenate(
        [jnp.broadcast_to(cl[(bb + 1) * CHUNK - 1:(bb + 1) * CHUNK, :], (CHUNK, D_GRP))
         for bb in range(BATCH_PER_STEP)], axis=0)
    e_in = jnp.exp(cl)
    e_out = jnp.exp(-cl)
    e_rem = jnp.exp(cl_end - cl)
    p_end = jnp.exp(cl_end)
    kka = kk * a
    ops = [(-kk * jnp.exp(cl - ld)).astype(BF16), (kka * e_out).astype(BF16),
           (k2 * e_out).astype(BF16), (r * e_in).astype(BF16), v.astype(BF16),
           (kka * e_rem).astype(BF16), (k2 * e_rem).astype(BF16)]

    chains = [(bb, grp) for bb in range(BATCH_PER_STEP)
              for grp in range(N_HEADS // HEADS_PER_SCAN)]
    head_mask = masks_ref[M_HEAD]
    strict, incl = masks_ref[M_STRICT], masks_ref[M_INCL]

    def stacked(op, bb, grp):
        part = op[bb * CHUNK:(bb + 1) * CHUNK, grp * SCAN_W:(grp + 1) * SCAN_W]
        return jnp.concatenate([part] * HEADS_PER_SCAN, axis=0) * head_mask

    xs = [[stacked(op, bb, grp) for op in ops] for bb, grp in chains]
    st = [state_ref[bb, grp] for bb, grp in chains]
    sb = [s.astype(BF16) for s in st]
    nab = [_bdot(x[0], x[1], _NT).astype(BF16) for x in xs]
    aak = [_bdot(x[0], x[2], _NT).astype(BF16) * strict for x in xs]
    arb = [_bdot(x[3], x[1], _NT).astype(BF16) * incl for x in xs]
    ark = [_bdot(x[3], x[2], _NT).astype(BF16) * incl for x in xs]
    t_inv = [masks_ref[M_EYE] + n * masks_ref[M_BASE] for n in nab]
    for lvl in range(INV_LEVELS):
        half = [_bdot(t, n * masks_ref[M_OFF + lvl], _NN).astype(BF16) for t, n in zip(t_inv, nab)]
        t_inv = [t + _bdot(h, t, _NN).astype(BF16) for t, h in zip(t_inv, half)]
    rhs = [(_bdot(x[0], s, _NT) + _bdot(k, x[4], _NN)).astype(BF16)
           for x, s, k in zip(xs, sb, aak)]
    sa = [_bdot(t, h, _NN).astype(BF16) for t, h in zip(t_inv, rhs)]
    ys = [_bdot(x[3], s, _NT) + _bdot(b, u, _NN) + _bdot(k, x[4], _NN)
          for x, s, b, u, k in zip(xs, sb, arb, sa, ark)]
    for (bb, grp), x, s, u in zip(chains, xs, st, sa):
        decay = p_end[bb * CHUNK:bb * CHUNK + 1, grp * SCAN_W:(grp + 1) * SCAN_W]
        state_ref[bb, grp] = s * decay + _bdot(u, x[5], _TN) + _bdot(x[4], x[6], _TN)
    ys = [y[0:CHUNK] + y[CHUNK:2 * CHUNK] + y[2 * CHUNK:3 * CHUNK] + y[3 * CHUNK:4 * CHUNK]
          for y in ys]
    n_grp = N_HEADS // HEADS_PER_SCAN
    y = jnp.concatenate([jnp.concatenate(ys[bb * n_grp:(bb + 1) * n_grp], axis=1)
                         for bb in range(BATCH_PER_STEP)], axis=0)

    mean = _split_dot(_split_dot(y, red, SEG_TERMS) * (1.0 / HEAD_DIM), exp_m, SEG_TERMS)
    d = y - mean
    var = _split_dot(d * d, red, SEG_TERMS) * (1.0 / HEAD_DIM)
    yn = d * _split_dot(lax.rsqrt(var + GN_EPS), exp_m, SEG_TERMS) * gn_w + gn_b
    bonus = _split_dot(_split_dot(r * k2 * r_k, red, SEG_TERMS), exp_m, SEG_TERMS) * v
    o_ref[...] = ((yn + bonus) * g).astype(BF16).reshape(BATCH_PER_STEP, CHUNK, D_GRP)


def _rwkv(p_r, mu, w0, w2p, a0, a2p, g2, k_k, k_a, r_k, gn_w, gn_b, w_gate_up, w_down):
    bsz, seq, _ = p_r.shape
    assert bsz % BATCH_PER_STEP == 0
    n_chunk = seq // CHUNK
    n_step = (bsz // BATCH_PER_STEP) * n_chunk
    wgu2d = w_gate_up.reshape(-1, w_gate_up.shape[-1])
    wd2d = w_down.reshape(-1, w_down.shape[-1])
    assert wgu2d.shape[0] % (8 * n_step) == 0 and wd2d.shape[0] == wgu2d.shape[0]
    slab = wgu2d.shape[0] // n_step
    masks = _scan_masks()
    const = lambda b, s: (0, 0)
    step = lambda b, s: (b * n_chunk + s, 0)
    vec = pl.BlockSpec((1, D_GRP), const)
    y, wgu_bf, wd_bf = pl.pallas_call(
        _rwkv_kernel,
        grid=(bsz // BATCH_PER_STEP, n_chunk),
        in_specs=[pl.BlockSpec((BATCH_PER_STEP, CHUNK, RWKV_COLS), lambda b, s: (b, s, 0)),
                  pl.BlockSpec(masks.shape, lambda b, s: (0, 0, 0)),
                  pl.BlockSpec((1, RWKV_COLS), const),
                  vec, pl.BlockSpec((LANES, D_GRP), const),
                  vec, pl.BlockSpec((LANES, D_GRP), const),
                  pl.BlockSpec((LANES, D_GRP), const),
                  vec, vec, vec, vec, vec,
                  pl.BlockSpec((slab, wgu2d.shape[1]), step),
                  pl.BlockSpec((slab, wd2d.shape[1]), step)],
        out_specs=[pl.BlockSpec((BATCH_PER_STEP, CHUNK, D_GRP), lambda b, s: (b, s, 0)),
                   pl.BlockSpec((slab, wgu2d.shape[1]), step),
                   pl.BlockSpec((slab, wd2d.shape[1]), step)],
        out_shape=[jax.ShapeDtypeStruct((bsz, seq, D_GRP), BF16),
                   jax.ShapeDtypeStruct(wgu2d.shape, BF16),
                   jax.ShapeDtypeStruct(wd2d.shape, BF16)],
        scratch_shapes=[pltpu.VMEM((BATCH_PER_STEP, 1, RWKV_COLS), F32),
                        pltpu.VMEM((BATCH_PER_STEP, N_HEADS // HEADS_PER_SCAN, SCAN_W, SCAN_W), F32)],
        compiler_params=pltpu.CompilerParams(
            dimension_semantics=("parallel", "arbitrary"), vmem_limit_bytes=VMEM_LIMIT),
        name="rwkv",
    )(p_r, masks, mu, w0, w2p, a0, a2p, g2, k_k, k_a, r_k, gn_w, gn_b, wgu2d, wd2d)
    return y, wgu_bf.reshape(w_gate_up.shape), wd_bf.reshape(w_down.shape)


def _fox_kernel(q_ref, qb_ref, k_ref, kb_ref, vt_ref, og_ref, ong_ref, o_ref, m_ref, l_ref, acc_ref,
                *, seq):
    hp = pl.program_id(1)
    lane = _iota((1, LANES), 1)
    q = q_ref[...]
    qb = qb_ref[...]
    zero = jnp.zeros_like(q)
    qcat = [jnp.concatenate([jnp.where(lane // HEAD_DIM == hh, q, zero),
                             jnp.where(lane // 8 == hp * 2 + hh, qb, zero)], axis=1)
            for hh in range(2)]
    keys = min(FOX_SUB_KEYS, seq)
    n_sub = seq // keys
    diag = _iota((keys, keys), 1) >= _iota((keys, keys), 0)

    m_ref[...] = jnp.full(m_ref.shape, -jnp.inf, F32)
    l_ref[...] = jnp.zeros(l_ref.shape, F32)
    acc_ref[...] = jnp.zeros(acc_ref.shape, F32)

    def scores(s):
        lo = s * keys
        kcat = jnp.concatenate([k_ref[lo:lo + keys, :], kb_ref[lo:lo + keys, :]], axis=1)
        return [lax.dot_general(kcat, qc[lo:, :], _NT, preferred_element_type=F32)
                for qc in qcat]

    pending = scores(0)
    for s in range(n_sub):
        lo = s * keys
        nxt = scores(s + 1) if s + 1 < n_sub else None
        vt = vt_ref[:, lo:lo + keys]
        sts = [jnp.concatenate([jnp.where(diag, st[:, :keys], -jnp.inf), st[:, keys:]], axis=1)
               if st.shape[1] > keys else jnp.where(diag, st, -jnp.inf) for st in pending]
        m_old = [m_ref[hh, :, lo:] for hh in range(2)]
        m_new = [jnp.maximum(m, jnp.max(st, axis=0, keepdims=True)) for m, st in zip(m_old, sts)]
        pts = [jnp.exp(st - m) for st, m in zip(sts, m_new)]
        pvs = [jnp.dot(vt, pt.astype(BF16), preferred_element_type=F32) for pt in pts]
        for hh in range(2):
            alpha = jnp.exp(m_old[hh] - m_new[hh])
            m_ref[hh, :, lo:] = m_new[hh]
            l_ref[hh, :, lo:] = alpha * l_ref[hh, :, lo:] + jnp.sum(pts[hh], axis=0, keepdims=True)
            acc_ref[hh, :, lo:] = (alpha * acc_ref[hh, :, lo:]
                                   + pvs[hh][hh * HEAD_DIM:(hh + 1) * HEAD_DIM, :])
        pending = nxt

    outs = []
    for hh in range(2):
        o = acc_ref[hh] / l_ref[hh]
        outs.append(o * lax.rsqrt(jnp.mean(o * o, axis=0, keepdims=True) + NORM_EPS))
    o = jnp.concatenate(outs, axis=0).T
    o_ref[...] = (o * ong_ref[...] * _sigmoid(og_ref[...].astype(F32))).astype(BF16)


def _fox(p_x, k_bias, q_bias, o_gain):
    bsz, seq, _ = p_x.shape
    npair = N_HEADS // 2
    v_t = jnp.transpose(p_x[:, :, 2 * D_GRP:3 * D_GRP], (0, 2, 1))
    return pl.pallas_call(
        functools.partial(_fox_kernel, seq=seq),
        grid=(bsz, npair),
        in_specs=[pl.BlockSpec((None, seq, LANES), lambda b, h: (b, 0, h)),
                  pl.BlockSpec((None, seq, LANES), lambda b, h: (b, 0, 0)),
                  pl.BlockSpec((None, seq, LANES), lambda b, h: (b, 0, npair + h)),
                  pl.BlockSpec((None, seq, LANES), lambda b, h: (b, 0, 0)),
                  pl.BlockSpec((None, LANES, seq), lambda b, h: (b, h, 0)),
                  pl.BlockSpec((None, seq, LANES), lambda b, h: (b, 0, 3 * npair + h)),
                  pl.BlockSpec((1, LANES), lambda b, h: (0, 0))],
        out_specs=pl.BlockSpec((None, seq, LANES), lambda b, h: (b, 0, h)),
        out_shape=jax.ShapeDtypeStruct((bsz, seq, D_GRP), BF16),
        scratch_shapes=[pltpu.VMEM((2, 1, seq), F32), pltpu.VMEM((2, 1, seq), F32),
                        pltpu.VMEM((2, HEAD_DIM, seq), F32)],
        compiler_params=pltpu.CompilerParams(
            dimension_semantics=("parallel", "parallel"), vmem_limit_bytes=VMEM_LIMIT),
        name="fox",
    )(p_x, q_bias, p_x, k_bias, v_t, p_x, o_gain)


def _outproj_kernel(x_ref, yr_ref, yf_ref, g1_ref, sh_ref, sc_ref, ng_ref, wor_ref, wof_ref,
                    wrt_ref, wrl_ref, brt_ref, x1_ref, h2_ref, idx_ref, gate_ref, rank_ref, cnt_ref,
                    carry_ref):
    @pl.when(pl.program_id(0) == 0)
    def _():
        carry_ref[...] = jnp.zeros_like(carry_ref)

    y = (jnp.dot(yr_ref[...], wor_ref[...], preferred_element_type=F32)
         + jnp.dot(yf_ref[...], wof_ref[...], preferred_element_type=F32))
    x1 = x_ref[...] + g1_ref[...] * y
    x1_ref[...] = x1
    tm = x1.shape[0]
    h = x1 * lax.rsqrt(jnp.mean(x1 * x1, axis=-1, keepdims=True) + NORM_EPS) * ng_ref[...]
    h2 = h * (1.0 + sc_ref[...]) + sh_ref[...]
    h2_ref[...] = _pack_rows(h2)

    h_hi = h2.astype(BF16)
    h_lo = (h2 - h_hi.astype(F32)).astype(BF16)
    logits = (lax.dot_general(wrt_ref[...], h_hi, _NT, preferred_element_type=F32)
              + lax.dot_general(wrt_ref[...], h_lo, _NT, preferred_element_type=F32)
              + lax.dot_general(wrl_ref[...], h_hi, _NT, preferred_element_type=F32))
    lg = logits[:N_EXPERTS, :] + brt_ref[...]
    expert = _iota((N_EXPERTS, tm), 0)
    picks = []
    hot_sum = jnp.zeros((N_EXPERTS, tm), F32)
    for _ in range(TOP_K):
        m = jnp.max(lg, axis=0, keepdims=True)
        sel = jnp.min(jnp.where(lg == m, expert, N_EXPERTS), axis=0, keepdims=True)
        hot = expert == sel
        picks.append((m, sel, hot))
        hot_sum = hot_sum + hot.astype(F32)
        lg = jnp.where(hot, -jnp.inf, lg)
    es = [jnp.exp(m - picks[0][0]) for m, _, _ in picks]
    den = es[0] + es[1] + es[2] + es[3]

    earlier = (_iota((tm, tm), 0) < _iota((tm, tm), 1)).astype(BF16)
    before = jnp.dot(hot_sum.astype(BF16), earlier, preferred_element_type=F32) + carry_ref[...]
    ranks = [jnp.sum(jnp.where(hot, before, 0.0), axis=0, keepdims=True).astype(jnp.int32)
             for _, _, hot in picks]
    pad_i = jnp.zeros((8 - TOP_K, tm), jnp.int32)
    idx_ref[...] = jnp.concatenate([sel for _, sel, _ in picks] + [pad_i], axis=0)
    gate_ref[...] = jnp.concatenate([e / den for e in es] + [pad_i.astype(F32)], axis=0)
    rank_ref[...] = jnp.concatenate(ranks + [pad_i], axis=0)
    carry_ref[...] = carry_ref[...] + jnp.sum(hot_sum, axis=1, keepdims=True)
    cnt_ref[...] = jnp.broadcast_to(carry_ref[...], cnt_ref.shape)


def _outproj(x2d, y_r, y_f, gate1, shift2, scale2, norm_g, wo_r, wo_f, w_rt, b_rt, tm, seq,
             row0, t):
    w_rt_hi = w_rt.astype(BF16)
    w_rt_lo = (w_rt - w_rt_hi.astype(F32)).astype(BF16)
    per_b = seq // tm
    blk0 = row0 // tm
    const = lambda i: (0, 0)
    rows = lambda i: (i, 0)
    rows_in = lambda i: (i + blk0, 0)
    mod = pl.BlockSpec((None, 1, D_MODEL), lambda i: ((i + blk0) // per_b, 0, 0))
    return pl.pallas_call(
        _outproj_kernel,
        grid=(t // tm,),
        in_specs=[pl.BlockSpec((tm, D_MODEL), rows_in),
                  pl.BlockSpec((tm, D_GRP), rows_in),
                  pl.BlockSpec((tm, D_GRP), rows_in),
                  mod, mod, mod,
                  pl.BlockSpec((1, D_MODEL), const),
                  pl.BlockSpec((D_GRP, D_MODEL), const),
                  pl.BlockSpec((D_GRP, D_MODEL), const),
                  pl.BlockSpec((LANES, D_MODEL), const),
                  pl.BlockSpec((LANES, D_MODEL), const),
                  pl.BlockSpec((N_EXPERTS, 1), const)],
        out_specs=[pl.BlockSpec((tm, D_MODEL), rows),
                   pl.BlockSpec((tm, D_PACK), rows),
                   pl.BlockSpec((8, tm), lambda i: (0, i)),
                   pl.BlockSpec((8, tm), lambda i: (0, i)),
                   pl.BlockSpec((8, tm), lambda i: (0, i)),
                   pl.BlockSpec((N_EXPERTS, LANES), const)],
        out_shape=[jax.ShapeDtypeStruct((t, D_MODEL), F32),
                   jax.ShapeDtypeStruct((t, D_PACK), jnp.uint32),
                   jax.ShapeDtypeStruct((8, t), jnp.int32),
                   jax.ShapeDtypeStruct((8, t), F32),
                   jax.ShapeDtypeStruct((8, t), jnp.int32),
                   jax.ShapeDtypeStruct((N_EXPERTS, LANES), F32)],
        scratch_shapes=[pltpu.VMEM((N_EXPERTS, 1), F32)],
        compiler_params=pltpu.CompilerParams(
            dimension_semantics=("arbitrary",), vmem_limit_bytes=VMEM_LIMIT),
        name="outproj",
    )(x2d, y_r, y_f, gate1, shift2, scale2, norm_g, wo_r, wo_f, w_rt_hi, w_rt_lo, b_rt)


SC_CORES = 2
SC_SUBCORES = 16
SC_ROWS = 64


def _sc_gather_rows(idx, src):
    n_workers = SC_CORES * SC_SUBCORES
    m = idx.shape[0]
    d = src.shape[1]
    assert m % (n_workers * SC_ROWS) == 0
    n_chunks = m // (n_workers * SC_ROWS)
    mesh = plsc.VectorSubcoreMesh(core_axis_name="c", subcore_axis_name="s")

    @functools.partial(
        pl.kernel, mesh=mesh,
        out_type=jax.ShapeDtypeStruct((m, d), src.dtype),
        scratch_types=[pltpu.VMEM((n_chunks, SC_ROWS), jnp.int32),
                       pltpu.VMEM((SC_ROWS, d), src.dtype),
                       pltpu.SemaphoreType.DMA],
        name="sc_gather")
    def gather(src_hbm, idx_hbm, out_hbm, idx_v, rows_v, sem):
        wid = lax.axis_index("s") * SC_CORES + lax.axis_index("c")
        pltpu.sync_copy(idx_hbm.at[wid], idx_v)

        @pl.loop(0, n_chunks)
        def _(j):
            pltpu.async_copy(src_hbm.at[idx_v.at[j]], rows_v, sem).wait()
            pltpu.sync_copy(rows_v, out_hbm.at[pl.ds((wid * n_chunks + j) * SC_ROWS, SC_ROWS)])

    return gather(src, idx.reshape(n_workers, n_chunks, SC_ROWS))


def _sc_scatter_rows(src, dest, n_out):
    n_workers = SC_CORES * SC_SUBCORES
    t, d = src.shape
    n_slot = dest.shape[0]
    assert t % (n_workers * SC_ROWS) == 0
    n_chunks = t // (n_workers * SC_ROWS)
    mesh = plsc.VectorSubcoreMesh(core_axis_name="c", subcore_axis_name="s")
    idx = dest.reshape(n_slot, n_workers, n_chunks, SC_ROWS).transpose(1, 2, 0, 3)
    idx = idx.reshape(n_workers, n_chunks * n_slot, SC_ROWS)

    @functools.partial(
        pl.kernel, mesh=mesh,
        out_type=jax.ShapeDtypeStruct((n_out, d), src.dtype),
        scratch_types=[pltpu.VMEM((n_chunks * n_slot, SC_ROWS), jnp.int32),
                       pltpu.VMEM((SC_ROWS, d), src.dtype)],
        name="sc_scatter")
    def scatter(src_hbm, idx_hbm, out_hbm, idx_v, rows_v):
        wid = lax.axis_index("s") * SC_CORES + lax.axis_index("c")
        pltpu.sync_copy(idx_hbm.at[wid], idx_v)

        @pl.loop(0, n_chunks)
        def _(j):
            pltpu.sync_copy(src_hbm.at[pl.ds((wid * n_chunks + j) * SC_ROWS, SC_ROWS)], rows_v)
            for k in range(n_slot):
                pltpu.sync_copy(rows_v, out_hbm.at[idx_v.at[j * n_slot + k]])

    return scatter(src, idx)


def _expert_kernel(be_ref, nv_ref, x_ref, wgu_ref, bgu_ref, wd_ref, bd_ref, o_ref):
    del be_ref
    valid = _iota((EXPERT_BLOCK, 1), 0) < nv_ref[pl.program_id(0)]
    lo, hi = _unpack_rows(jnp.where(valid, x_ref[...], jnp.uint32(0)))
    x = jnp.concatenate([lo.astype(BF16), hi.astype(BF16)], axis=1)
    gu = jnp.dot(x, wgu_ref[...], preferred_element_type=F32) + bgu_ref[...]
    gate = jnp.minimum(gu[:, :D_MODEL], SWIGLU_LIMIT)
    up = jnp.clip(gu[:, D_MODEL:], -SWIGLU_LIMIT, SWIGLU_LIMIT)
    act = gate * _sigmoid(SWIGLU_ALPHA * gate) * (up + 1.0)
    o_ref[...] = _pack_rows(
        jnp.dot(act.astype(BF16), wd_ref[...], preferred_element_type=F32) + bd_ref[...])


def _experts(block_e, n_valid, xs, w_gu, b_gu, w_d, b_d):
    n_blocks = block_e.shape[0]
    grid_spec = pltpu.PrefetchScalarGridSpec(
        num_scalar_prefetch=2,
        grid=(n_blocks,),
        in_specs=[pl.BlockSpec((EXPERT_BLOCK, D_PACK), lambda j, be, nv: (j, 0)),
                  pl.BlockSpec((None, D_MODEL, 2 * D_MODEL), lambda j, be, nv: (be[j], 0, 0)),
                  pl.BlockSpec((None, 1, 2 * D_MODEL), lambda j, be, nv: (be[j], 0, 0)),
                  pl.BlockSpec((None, D_MODEL, D_MODEL), lambda j, be, nv: (be[j], 0, 0)),
                  pl.BlockSpec((None, 1, D_MODEL), lambda j, be, nv: (be[j], 0, 0))],
        out_specs=pl.BlockSpec((EXPERT_BLOCK, D_PACK), lambda j, be, nv: (j, 0)),
    )
    return pl.pallas_call(
        _expert_kernel,
        grid_spec=grid_spec,
        out_shape=jax.ShapeDtypeStruct(xs.shape, jnp.uint32),
        compiler_params=pltpu.CompilerParams(
            dimension_semantics=("arbitrary",), vmem_limit_bytes=VMEM_LIMIT),
        name="experts",
    )(block_e, n_valid, xs, w_gu, b_gu, w_d, b_d)


COMBINE_TOKENS = 512
MOE_SPLITS = 2


def _combine_kernel(yg_ref, x1_ref, gate_ref, g2_ref, fg_ref, o_ref):
    gates = gate_ref[...].T
    acc_lo = acc_hi = None
    for kk in range(TOP_K):
        lo, hi = _unpack_rows(yg_ref[kk * COMBINE_TOKENS:(kk + 1) * COMBINE_TOKENS, :])
        g = gates[:, kk:kk + 1]
        acc_lo = g * lo if acc_lo is None else acc_lo + g * lo
        acc_hi = g * hi if acc_hi is None else acc_hi + g * hi
    x2 = x1_ref[...] + g2_ref[...] * jnp.concatenate([acc_lo, acc_hi], axis=1)
    o_ref[...] = x2 * lax.rsqrt(jnp.mean(x2 * x2, axis=-1, keepdims=True) + NORM_EPS) * fg_ref[...]


def _combine_kernel_into(prev_ref, *refs):
    del prev_ref
    _combine_kernel(*refs)


def _combine(yg, x1, gates, gate2, final_g, seq, row0, t_total, prev):
    t = x1.shape[0]
    tm = COMBINE_TOKENS
    per_b = seq // tm
    blk0 = row0 // tm
    rows = lambda i: (i, 0)
    in_specs = [pl.BlockSpec((TOP_K * tm, D_PACK), rows),
                pl.BlockSpec((tm, D_MODEL), rows),
                pl.BlockSpec((8, tm), lambda i: (0, i)),
                pl.BlockSpec((None, 1, D_MODEL), lambda i: ((i + blk0) // per_b, 0, 0)),
                pl.BlockSpec((1, D_MODEL), lambda i: (0, 0))]
    args = (yg, x1, gates, gate2, final_g)
    if prev is not None:
        in_specs = [pl.BlockSpec(memory_space=pl.ANY)] + in_specs
        args = (prev,) + args
    return pl.pallas_call(
        _combine_kernel if prev is None else _combine_kernel_into,
        grid=(t // tm,),
        in_specs=in_specs,
        out_specs=pl.BlockSpec((tm, D_MODEL), lambda i: (i + blk0, 0)),
        out_shape=jax.ShapeDtypeStruct((t_total, D_MODEL), F32),
        input_output_aliases={} if prev is None else {0: 0},
        compiler_params=pltpu.CompilerParams(
            dimension_semantics=("parallel",), vmem_limit_bytes=VMEM_LIMIT),
        name="combine",
    )(*args)


def _moe(h2, idx, gates, rank, counts, x1, gate2, final_g, w_gu, b_gu, w_d, b_d, seq,
         row0, t_total, prev):
    t = h2.shape[0]
    n_slots = t * TOP_K
    n_blocks = -(-n_slots // EXPERT_BLOCK) + N_EXPERTS
    cap = n_blocks * EXPERT_BLOCK
    padded = (counts + EXPERT_BLOCK - 1) // EXPERT_BLOCK * EXPERT_BLOCK
    pad_ends = jnp.cumsum(padded)
    pad_starts = pad_ends - padded
    experts = jnp.arange(N_EXPERTS, dtype=jnp.int32)
    dest = jnp.sum(jnp.where(idx[..., None] == experts, pad_starts, 0), axis=-1) + rank
    block_starts = jnp.arange(n_blocks, dtype=jnp.int32) * EXPERT_BLOCK
    block_e = jnp.minimum(jnp.sum(block_starts[:, None] >= pad_ends[None, :], axis=1),
                          N_EXPERTS - 1).astype(jnp.int32)
    n_valid = jnp.clip(counts[block_e] - (block_starts - pad_starts[block_e]), 0, EXPERT_BLOCK)

    xs = _sc_scatter_rows(h2, dest, cap)
    yb = _experts(block_e, n_valid.astype(jnp.int32), xs, w_gu, b_gu, w_d, b_d)
    dest_blocks = dest.reshape(TOP_K, -1, COMBINE_TOKENS).transpose(1, 0, 2).reshape(-1)
    yg = _sc_gather_rows(dest_blocks, yb)
    return _combine(yg, x1, gates, gate2, final_g, seq, row0, t_total, prev)


def _layer(x, c_mod, norm1_g, w_in, mu_shift, w0, w2, a0, a2, g2, k_k, k_a, r_k, gn_w, gn_b, b_f,
           q_norm_g, k_norm_g, o_norm_g, w_out, norm2_g, w_router, b_router, w_gate_up,
           b_gate_up, w_down, b_down, final_g, tm_in, tm_out):
    bsz, seq, _ = x.shape
    shift1, scale1, gate1, shift2, scale2, gate2 = (
        m.reshape(bsz, 1, D_MODEL) for m in jnp.split(c_mod, 6, axis=-1))
    row = lambda v: v.reshape(1, -1)

    w_r = w_in[:, :RWKV_COLS].astype(BF16)
    w_x = w_in[:, RWKV_COLS:RWKV_COLS + FOX_MAIN].astype(BF16)
    w_f = w_in[:, RWKV_COLS + FOX_MAIN:].T
    b_fp = jnp.pad(b_f, (0, LANES - N_HEADS)).reshape(1, LANES)
    qk_gain = jnp.concatenate([jnp.tile(q_norm_g, N_HEADS) * HEAD_DIM ** -0.5,
                               jnp.tile(k_norm_g, N_HEADS)]).reshape(1, -1)
    p_r, p_x, k_bias, q_bias = _inproj(x, shift1, scale1, row(norm1_g), w_r, w_x, w_f, b_fp,
                                       qk_gain, tm_in)

    zeros = jnp.zeros((LANES - 64, D_GRP), F32)
    w2p = jnp.concatenate([w2, zeros], axis=0).astype(BF16)
    a2p = jnp.concatenate([zeros, a2], axis=0).astype(BF16)
    y_r, w_gu, w_d = _rwkv(p_r, row(mu_shift), row(w0), w2p, row(a0), a2p, g2.astype(BF16),
                           row(k_k), row(k_a), row(r_k), row(gn_w), row(gn_b), w_gate_up, w_down)

    y_f = _fox(p_x, k_bias, q_bias, jnp.tile(o_norm_g, 2).reshape(1, LANES))

    t = bsz * seq
    w_rt = jnp.pad(w_router.T, ((0, LANES - N_EXPERTS), (0, 0)))
    b_rt = b_router.reshape(N_EXPERTS, 1)
    wo = w_out.astype(BF16)
    b_gu, b_d = b_gate_up.reshape(N_EXPERTS, 1, -1), b_down.reshape(N_EXPERTS, 1, -1)
    t_part = t // MOE_SPLITS
    out = None
    for part in range(MOE_SPLITS):
        row0 = part * t_part
        x1, h2, idx, gates, rank, cnt = _outproj(
            x.reshape(t, D_MODEL), y_r.reshape(t, D_GRP), y_f.reshape(t, D_GRP), gate1, shift2,
            scale2, row(norm2_g), wo[:D_GRP], wo[D_GRP:], w_rt, b_rt, tm_out, seq, row0, t_part)
        counts = cnt[:, 0].astype(jnp.int32)
        out = _moe(h2, idx[:TOP_K], gates, rank[:TOP_K], counts, x1, gate2, row(final_g),
                   w_gu, b_gu, w_d, b_d, seq, row0, t, out)
    return out.reshape(bsz, seq, D_MODEL)


def kernel(x, c, w_ada, b_ada, norm1_g, w_in, mu_shift, w0, w2, a0, a2, g2, k_k, k_a, r_k, gn_w,
           gn_b, b_f, q_norm_g, k_norm_g, o_norm_g, w_out, norm2_g, w_router, b_router, w_gate_up,
           b_gate_up, w_down, b_down, final_g):
    assert w_ada.shape[0] == 1, "single-layer block"
    c_mod = _adaln(c, w_ada[0], b_ada[0])
    return _layer(x, c_mod, norm1_g[0], w_in[0], mu_shift[0], w0[0], w2[0], a0[0], a2[0], g2[0],
                  k_k[0], k_a[0], r_k[0], gn_w[0], gn_b[0], b_f[0], q_norm_g[0], k_norm_g[0],
                  o_norm_g[0], w_out[0], norm2_g[0], w_router[0], b_router[0], w_gate_up[0],
                  b_gate_up[0], w_down[0], b_down[0], final_g,
                  tm_in=min(512, x.shape[1]), tm_out=min(1024, x.shape[1]))
```

```python
import functools

import jax
import jax.numpy as jnp
from jax import lax
from jax.experimental import pallas as pl
from jax.experimental.pallas import tpu as pltpu
from jax.experimental.pallas import tpu_sc as plsc

F32 = jnp.float32
BF16 = jnp.bfloat16
HIGHEST = lax.Precision.HIGHEST

D_MODEL = 1024
HEAD_DIM = 64
N_HEADS = 8
D_GRP = N_HEADS * HEAD_DIM
RWKV_COLS = 1792
LORA_OFF = 3 * D_GRP
GATE_OFF = LORA_OFF + 128
FOX_MAIN = 4 * D_GRP
N_EXPERTS = 32
TOP_K = 4
EXPERT_BLOCK = 512
SWIGLU_ALPHA = 1.702
SWIGLU_LIMIT = 7.0
NORM_EPS = 1e-6
GN_EPS = 64e-5
LANES = 128
CHUNK = 64
FOX_SUB_KEYS = 512
HEADS_PER_SCAN = 4
SCAN_W = HEADS_PER_SCAN * HEAD_DIM
SEG_TERMS = 1
CUM_TERMS = 2
VMEM_LIMIT = 56 * 1024 * 1024


def _dot(a, b):
    return jnp.dot(a.astype(BF16), b.astype(BF16), preferred_element_type=F32)


def _dot_nt(a, b):
    return lax.dot_general(a.astype(BF16), b.astype(BF16), (((1,), (1,)), ((), ())),
                           preferred_element_type=F32)


def _dot_tn(a, b):
    return lax.dot_general(a.astype(BF16), b.astype(BF16), (((0,), (0,)), ((), ())),
                           preferred_element_type=F32)


def _fdot(a, b):
    return jnp.dot(a, b, precision=HIGHEST, preferred_element_type=F32)


def _split_dot(x, m, terms=2, left=False):
    acc = None
    rem = x
    for _ in range(terms):
        part = rem.astype(BF16)
        rem = rem - part.astype(F32)
        d = (jnp.dot(m, part, preferred_element_type=F32) if left
             else jnp.dot(part, m, preferred_element_type=F32))
        acc = d if acc is None else acc + d
    return acc


def _iota(shape, dim):
    return lax.broadcasted_iota(jnp.int32, shape, dim)


def _seg_reduce_mat(n):
    return (_iota((n, LANES), 0) // HEAD_DIM == _iota((n, LANES), 1)).astype(BF16)


def _seg_expand_mat(n):
    return (_iota((LANES, n), 1) // HEAD_DIM == _iota((LANES, n), 0)).astype(BF16)


def _tri(n, strict):
    r, c = _iota((n, n), 0), _iota((n, n), 1)
    return ((r > c) if strict else (r >= c)).astype(BF16)


D_PACK = D_MODEL // 2


def _pack_rows(x):
    lo = lax.bitcast_convert_type(x[:, :D_PACK].astype(BF16).astype(F32), jnp.uint32)
    hi = lax.bitcast_convert_type(x[:, D_PACK:].astype(BF16).astype(F32), jnp.uint32)
    return hi | (lo >> 16)


def _unpack_rows(p):
    lo = lax.bitcast_convert_type(p << 16, F32)
    hi = lax.bitcast_convert_type(p & jnp.uint32(0xFFFF0000), F32)
    return lo, hi


def _log_sigmoid(z):
    return jnp.minimum(z, 0.0) - jnp.log(1.0 + jnp.exp(-jnp.abs(z)))


def _sigmoid(z):
    return 1.0 / (1.0 + jnp.exp(-z))


def _adaln_kernel(c_ref, w_ref, b_ref, o_ref):
    c = c_ref[...]
    o_ref[...] = _fdot(c * _sigmoid(c), w_ref[...]) + b_ref[...]


def _adaln(c, w_ada, b_ada):
    bsz = c.shape[0]
    n_mod = w_ada.shape[1] // D_MODEL
    return pl.pallas_call(
        _adaln_kernel,
        grid=(n_mod,),
        in_specs=[pl.BlockSpec((bsz, D_MODEL), lambda j: (0, 0)),
                  pl.BlockSpec((D_MODEL, D_MODEL), lambda j: (0, j)),
                  pl.BlockSpec((1, D_MODEL), lambda j: (0, j))],
        out_specs=pl.BlockSpec((bsz, D_MODEL), lambda j: (0, j)),
        out_shape=jax.ShapeDtypeStruct((bsz, n_mod * D_MODEL), F32),
        name="adaln",
    )(c, w_ada, b_ada.reshape(1, -1))


def _inproj_kernel(x_ref, sh_ref, sc_ref, g_ref, wr_ref, wx_ref, wft_ref, bf_ref, qkg_ref,
                   pr_ref, px_ref, kb_ref, qb_ref, carry_ref):
    @pl.when(pl.program_id(1) == 0)
    def _():
        carry_ref[...] = jnp.zeros_like(carry_ref)

    x = x_ref[...]
    tm = x.shape[0]
    h = x * lax.rsqrt(jnp.mean(x * x, axis=-1, keepdims=True) + NORM_EPS) * g_ref[...]
    h = h * (1.0 + sc_ref[...]) + sh_ref[...]
    hb = h.astype(BF16)

    pr_ref[...] = jnp.dot(hb, wr_ref[...], preferred_element_type=F32).astype(BF16)

    px = jnp.dot(hb, wx_ref[...], preferred_element_type=F32)
    qk = px[:, :2 * D_GRP]
    ss = _split_dot(qk * qk, _seg_reduce_mat(2 * D_GRP), SEG_TERMS)
    inv = lax.rsqrt(ss * (1.0 / HEAD_DIM) + NORM_EPS)
    qk = qk * _split_dot(inv, _seg_expand_mat(2 * D_GRP), SEG_TERMS) * qkg_ref[...]
    px_ref[:, :2 * D_GRP] = qk.astype(BF16)
    px_ref[:, 2 * D_GRP:] = px[:, 2 * D_GRP:].astype(BF16)

    lane = _iota((1, LANES), 1)
    z = jnp.zeros((tm, LANES), F32)
    for hd in range(N_HEADS):
        zh = jnp.sum(h * wft_ref[hd:hd + 1, :], axis=-1, keepdims=True)
        z = jnp.where(lane == hd, zh, z)
    cum = _log_sigmoid(z + bf_ref[...])
    row_id = _iota((tm, 1), 0)
    shift = 1
    while shift < tm:
        cum = cum + jnp.where(row_id >= shift, pltpu.roll(cum, shift, axis=0), 0.0)
        shift *= 2
    cum = cum + carry_ref[...]
    carry_ref[...] = cum[tm - 1:tm, :]

    parts = []
    rem = cum
    for _ in range(3):
        part = rem.astype(BF16)
        rem = rem - part.astype(F32)
        parts.append(part)
    src, dst = _iota((LANES, LANES), 0), _iota((LANES, LANES), 1)

    def spread(offset):
        return sum(jnp.dot(part, ((dst == 8 * src + offset + t) & (src < N_HEADS)).astype(BF16),
                           preferred_element_type=F32) for t, part in enumerate(parts))

    slot = _iota((1, LANES), 1) % 8
    kb_ref[...] = (jnp.where((slot >= 3) & (slot < 6), 1.0, 0.0) - spread(0)).astype(BF16)
    qb_ref[...] = (jnp.where(slot < 3, 1.0, 0.0) + spread(3)).astype(BF16)


def _inproj(x, shift, scale, g, w_r, w_x, w_f_t, b_f, qk_gain, tm):
    bsz, seq, _ = x.shape
    const = lambda b, s: (0, 0)
    return pl.pallas_call(
        _inproj_kernel,
        grid=(bsz, seq // tm),
        in_specs=[pl.BlockSpec((None, tm, D_MODEL), lambda b, s: (b, s, 0)),
                  pl.BlockSpec((None, 1, D_MODEL), lambda b, s: (b, 0, 0)),
                  pl.BlockSpec((None, 1, D_MODEL), lambda b, s: (b, 0, 0)),
                  pl.BlockSpec((1, D_MODEL), const),
                  pl.BlockSpec((D_MODEL, RWKV_COLS), const),
                  pl.BlockSpec((D_MODEL, FOX_MAIN), const),
                  pl.BlockSpec((N_HEADS, D_MODEL), const),
                  pl.BlockSpec((1, LANES), const),
                  pl.BlockSpec((1, 2 * D_GRP), const)],
        out_specs=[pl.BlockSpec((None, tm, RWKV_COLS), lambda b, s: (b, s, 0)),
                   pl.BlockSpec((None, tm, FOX_MAIN), lambda b, s: (b, s, 0)),
                   pl.BlockSpec((None, tm, LANES), lambda b, s: (b, s, 0)),
                   pl.BlockSpec((None, tm, LANES), lambda b, s: (b, s, 0))],
        out_shape=[jax.ShapeDtypeStruct((bsz, seq, RWKV_COLS), BF16),
                   jax.ShapeDtypeStruct((bsz, seq, FOX_MAIN), BF16),
                   jax.ShapeDtypeStruct((bsz, seq, LANES), BF16),
                   jax.ShapeDtypeStruct((bsz, seq, LANES), BF16)],
        scratch_shapes=[pltpu.VMEM((1, LANES), F32)],
        compiler_params=pltpu.CompilerParams(
            dimension_semantics=("parallel", "arbitrary"), vmem_limit_bytes=VMEM_LIMIT),
        name="inproj",
    )(x, shift, scale, g, w_r, w_x, w_f_t, b_f, qk_gain)


_NN = (((1,), (0,)), ((), ()))
_NT = (((1,), (1,)), ((), ()))
_TN = (((0,), (0,)), ((), ()))
SCAN_N = HEADS_PER_SCAN * CHUNK
BATCH_PER_STEP = 8
INV_LEVELS = 5
M_HEAD, M_STRICT, M_INCL, M_EYE, M_BASE, M_OFF = 0, 1, 2, 3, 4, 5


def _bdot(a, b, dims):
    return lax.dot_general(a, b, dims, preferred_element_type=F32)


def _scan_masks():
    rr, cc = _iota((SCAN_N, SCAN_W), 0), _iota((SCAN_N, SCAN_W), 1)
    ri, ci = _iota((SCAN_N, SCAN_N), 0), _iota((SCAN_N, SCAN_N), 1)
    same = ri // CHUNK == ci // CHUNK
    masks = [rr // CHUNK == cc // HEAD_DIM, same & (ri > ci), same & (ri >= ci), ri == ci,
             (ri // 2 == ci // 2) & (ri > ci)]
    blk = 2
    while blk < CHUNK:
        masks.append((ri // (2 * blk) == ci // (2 * blk)) & (ri // blk != ci // blk) & (ri > ci))
        blk *= 2
    return jnp.stack(masks).astype(BF16)


def _rwkv_kernel(p_ref, masks_ref, mu_ref, w0_ref, w2_ref, a0_ref, a2_ref, g2_ref, kk_ref, ka_ref,
                 rk_ref, gnw_ref, gnb_ref, wgu_ref, wd_ref, o_ref, wgu_bf_ref, wd_bf_ref,
                 last_ref, state_ref):
    wgu_bf_ref[...] = wgu_ref[...].astype(BF16)
    wd_bf_ref[...] = wd_ref[...].astype(BF16)

    @pl.when(pl.program_id(1) == 0)
    def _():
        last_ref[...] = jnp.zeros_like(last_ref)
        state_ref[...] = jnp.zeros_like(state_ref)

    mu, w0, w2, a0, a2, g2, k_k, k_a, r_k, gn_w, gn_b = (
        ref[...] for ref in (mu_ref, w0_ref, w2_ref, a0_ref, a2_ref, g2_ref, kk_ref, ka_ref,
                             rk_ref, gnw_ref, gnb_ref))
    rows = BATCH_PER_STEP * CHUNK
    p = p_ref[...].astype(F32).reshape(rows, RWKV_COLS)
    row_id = _iota((rows, 1), 0)
    prev = pltpu.roll(p, 1, axis=0)
    for bb in range(BATCH_PER_STEP):
        prev = jnp.where(row_id == bb * CHUNK, last_ref[bb], prev)
        last_ref[bb] = p[(bb + 1) * CHUNK - 1:(bb + 1) * CHUNK, :]
    pf = p + mu * (prev - p)
    r = pf[:, 0:D_GRP]
    k = pf[:, D_GRP:2 * D_GRP]
    v = pf[:, 2 * D_GRP:3 * D_GRP]
    lora = pf[:, LORA_OFF:GATE_OFF]
    gd = pf[:, GATE_OFF:RWKV_COLS]

    wlog = w0 + _dot(jnp.tanh(lora), w2)
    neg = -wlog
    softplus = jnp.maximum(neg, 0.0) + jnp.log(1.0 + jnp.exp(-jnp.abs(neg)))
    ld = -jnp.exp(-softplus - 0.5)
    a = _sigmoid(a0 + _dot(lora, a2))
    g = _dot(_sigmoid(gd), g2)

    red, exp_m = _seg_reduce_mat(D_GRP), _seg_expand_mat(D_GRP)
    kk = k * k_k
    n2 = _split_dot(kk * kk, red, SEG_TERMS)
    kk = kk * _split_dot(1.0 / jnp.maximum(jnp.sqrt(n2), 1e-12), exp_m, SEG_TERMS)
    k2 = k * (1.0 + (a - 1.0) * k_a)

    tr, tc = _iota((rows, rows), 0), _iota((rows, rows), 1)
    tri = ((tr >= tc) & (tr // CHUNK == tc // CHUNK)).astype(BF16)
    cl = _split_dot(ld, tri, terms=CUM_TERMS, left=True)
    cl_end = jnp.concatenate(
        [jnp.broadcast_to(cl[(bb + 1) * CHUNK - 1:(bb + 1) * CHUNK, :], (CHUNK, D_GRP))
         for bb in range(BATCH_PER_STEP)], axis=0)
    e_in = jnp.exp(cl)
    e_out = jnp.exp(-cl)
    e_rem = jnp.exp(cl_end - cl)
    p_end = jnp.exp(cl_end)
    kka = kk * a
    ops = [(-kk * jnp.exp(cl - ld)).astype(BF16), (kka * e_out).astype(BF16),
           (k2 * e_out).astype(BF16), (r * e_in).astype(BF16), v.astype(BF16),
           (kka * e_rem).astype(BF16), (k2 * e_rem).astype(BF16)]

    chains = [(bb, grp) for bb in range(BATCH_PER_STEP)
              for grp in range(N_HEADS // HEADS_PER_SCAN)]
    head_mask = masks_ref[M_HEAD]
    strict, incl = masks_ref[M_STRICT], masks_ref[M_INCL]

    def stacked(op, bb, grp):
        part = op[bb * CHUNK:(bb + 1) * CHUNK, grp * SCAN_W:(grp + 1) * SCAN_W]
        return jnp.concatenate([part] * HEADS_PER_SCAN, axis=0) * head_mask

    xs = [[stacked(op, bb, grp) for op in ops] for bb, grp in chains]
    st = [state_ref[bb, grp] for bb, grp in chains]
    sb = [s.astype(BF16) for s in st]
    nab = [_bdot(x[0], x[1], _NT).astype(BF16) for x in xs]
    aak = [_bdot(x[0], x[2], _NT).astype(BF16) * strict for x in xs]
    arb = [_bdot(x[3], x[1], _NT).astype(BF16) * incl for x in xs]
    ark = [_bdot(x[3], x[2], _NT).astype(BF16) * incl for x in xs]
    t_inv = [masks_ref[M_EYE] + n * masks_ref[M_BASE] for n in nab]
    for lvl in range(INV_LEVELS):
        half = [_bdot(t, n * masks_ref[M_OFF + lvl], _NN).astype(BF16) for t, n in zip(t_inv, nab)]
        t_inv = [t + _bdot(h, t, _NN).astype(BF16) for t, h in zip(t_inv, half)]
    rhs = [(_bdot(x[0], s, _NT) + _bdot(k, x[4], _NN)).astype(BF16)
           for x, s, k in zip(xs, sb, aak)]
    sa = [_bdot(t, h, _NN).astype(BF16) for t, h in zip(t_inv, rhs)]
    ys = [_bdot(x[3], s, _NT) + _bdot(b, u, _NN) + _bdot(k, x[4], _NN)
          for x, s, b, u, k in zip(xs, sb, arb, sa, ark)]
    for (bb, grp), x, s, u in zip(chains, xs, st, sa):
        decay = p_end[bb * CHUNK:bb * CHUNK + 1, grp * SCAN_W:(grp + 1) * SCAN_W]
        state_ref[bb, grp] = s * decay + _bdot(u, x[5], _TN) + _bdot(x[4], x[6], _TN)
    ys = [y[0:CHUNK] + y[CHUNK:2 * CHUNK] + y[2 * CHUNK:3 * CHUNK] + y[3 * CHUNK:4 * CHUNK]
          for y in ys]
    n_grp = N_HEADS // HEADS_PER_SCAN
    y = jnp.concatenate([jnp.concatenate(ys[bb * n_grp:(bb + 1) * n_grp], axis=1)
                         for bb in range(BATCH_PER_STEP)], axis=0)

    mean = _split_dot(_split_dot(y, red, SEG_TERMS) * (1.0 / HEAD_DIM), exp_m, SEG_TERMS)
    d = y - mean
    var = _split_dot(d * d, red, SEG_TERMS) * (1.0 / HEAD_DIM)
    yn = d * _split_dot(lax.rsqrt(var + GN_EPS), exp_m, SEG_TERMS) * gn_w + gn_b
    bonus = _split_dot(_split_dot(r * k2 * r_k, red, SEG_TERMS), exp_m, SEG_TERMS) * v
    o_ref[...] = ((yn + bonus) * g).astype(BF16).reshape(BATCH_PER_STEP, CHUNK, D_GRP)


def _rwkv(p_r, mu, w0, w2p, a0, a2p, g2, k_k, k_a, r_k, gn_w, gn_b, w_gate_up, w_down):
    bsz, seq, _ = p_r.shape
    assert bsz % BATCH_PER_STEP == 0
    n_chunk = seq // CHUNK
    n_step = (bsz // BATCH_PER_STEP) * n_chunk
    wgu2d = w_gate_up.reshape(-1, w_gate_up.shape[-1])
    wd2d = w_down.reshape(-1, w_down.shape[-1])
    assert wgu2d.shape[0] % (8 * n_step) == 0 and wd2d.shape[0] == wgu2d.shape[0]
    slab = wgu2d.shape[0] // n_step
    masks = _scan_masks()
    const = lambda b, s: (0, 0)
    step = lambda b, s: (b * n_chunk + s, 0)
    vec = pl.BlockSpec((1, D_GRP), const)
    y, wgu_bf, wd_bf = pl.pallas_call(
        _rwkv_kernel,
        grid=(bsz // BATCH_PER_STEP, n_chunk),
        in_specs=[pl.BlockSpec((BATCH_PER_STEP, CHUNK, RWKV_COLS), lambda b, s: (b, s, 0)),
                  pl.BlockSpec(masks.shape, lambda b, s: (0, 0, 0)),
                  pl.BlockSpec((1, RWKV_COLS), const),
                  vec, pl.BlockSpec((LANES, D_GRP), const),
                  vec, pl.BlockSpec((LANES, D_GRP), const),
                  pl.BlockSpec((LANES, D_GRP), const),
                  vec, vec, vec, vec, vec,
                  pl.BlockSpec((slab, wgu2d.shape[1]), step),
                  pl.BlockSpec((slab, wd2d.shape[1]), step)],
        out_specs=[pl.BlockSpec((BATCH_PER_STEP, CHUNK, D_GRP), lambda b, s: (b, s, 0)),
                   pl.BlockSpec((slab, wgu2d.shape[1]), step),
                   pl.BlockSpec((slab, wd2d.shape[1]), step)],
        out_shape=[jax.ShapeDtypeStruct((bsz, seq, D_GRP), BF16),
                   jax.ShapeDtypeStruct(wgu2d.shape, BF16),
                   jax.ShapeDtypeStruct(wd2d.shape, BF16)],
        scratch_shapes=[pltpu.VMEM((BATCH_PER_STEP, 1, RWKV_COLS), F32),
                        pltpu.VMEM((BATCH_PER_STEP, N_HEADS // HEADS_PER_SCAN, SCAN_W, SCAN_W), F32)],
        compiler_params=pltpu.CompilerParams(
            dimension_semantics=("parallel", "arbitrary"), vmem_limit_bytes=VMEM_LIMIT),
        name="rwkv",
    )(p_r, masks, mu, w0, w2p, a0, a2p, g2, k_k, k_a, r_k, gn_w, gn_b, wgu2d, wd2d)
    return y, wgu_bf.reshape(w_gate_up.shape), wd_bf.reshape(w_down.shape)


def _fox_kernel(q_ref, qb_ref, k_ref, kb_ref, vt_ref, og_ref, ong_ref, o_ref, m_ref, l_ref, acc_ref,
                *, seq):
    hp = pl.program_id(1)
    lane = _iota((1, LANES), 1)
    q = q_ref[...]
    qb = qb_ref[...]
    zero = jnp.zeros_like(q)
    qcat = [jnp.concatenate([jnp.where(lane // HEAD_DIM == hh, q, zero),
                             jnp.where(lane // 8 == hp * 2 + hh, qb, zero)], axis=1)
            for hh in range(2)]
    keys = min(FOX_SUB_KEYS, seq)
    n_sub = seq // keys
    diag = _iota((keys, keys), 1) >= _iota((keys, keys), 0)

    m_ref[...] = jnp.full(m_ref.shape, -jnp.inf, F32)
    l_ref[...] = jnp.zeros(l_ref.shape, F32)
    acc_ref[...] = jnp.zeros(acc_ref.shape, F32)

    def scores(s):
        lo = s * keys
        kcat = jnp.concatenate([k_ref[lo:lo + keys, :], kb_ref[lo:lo + keys, :]], axis=1)
        return [lax.dot_general(kcat, qc[lo:, :], _NT, preferred_element_type=F32)
                for qc in qcat]

    pending = scores(0)
    for s in range(n_sub):
        lo = s * keys
        nxt = scores(s + 1) if s + 1 < n_sub else None
        vt = vt_ref[:, lo:lo + keys]
        sts = [jnp.concatenate([jnp.where(diag, st[:, :keys], -jnp.inf), st[:, keys:]], axis=1)
               if st.shape[1] > keys else jnp.where(diag, st, -jnp.inf) for st in pending]
        m_old = [m_ref[hh, :, lo:] for hh in range(2)]
        m_new = [jnp.maximum(m, jnp.max(st, axis=0, keepdims=True)) for m, st in zip(m_old, sts)]
        pts = [jnp.exp(st - m) for st, m in zip(sts, m_new)]
        pvs = [jnp.dot(vt, pt.astype(BF16), preferred_element_type=F32) for pt in pts]
        for hh in range(2):
            alpha = jnp.exp(m_old[hh] - m_new[hh])
            m_ref[hh, :, lo:] = m_new[hh]
            l_ref[hh, :, lo:] = alpha * l_ref[hh, :, lo:] + jnp.sum(pts[hh], axis=0, keepdims=True)
            acc_ref[hh, :, lo:] = (alpha * acc_ref[hh, :, lo:]
                                   + pvs[hh][hh * HEAD_DIM:(hh + 1) * HEAD_DIM, :])
        pending = nxt

    outs = []
    for hh in range(2):
        o = acc_ref[hh] / l_ref[hh]
        outs.append(o * lax.rsqrt(jnp.mean(o * o, axis=0, keepdims=True) + NORM_EPS))
    o = jnp.concatenate(outs, axis=0).T
    o_ref[...] = (o * ong_ref[...] * _sigmoid(og_ref[...].astype(F32))).astype(BF16)


def _fox(p_x, k_bias, q_bias, o_gain):
    bsz, seq, _ = p_x.shape
    npair = N_HEADS // 2
    v_t = jnp.transpose(p_x[:, :, 2 * D_GRP:3 * D_GRP], (0, 2, 1))
    return pl.pallas_call(
        functools.partial(_fox_kernel, seq=seq),
        grid=(bsz, npair),
        in_specs=[pl.BlockSpec((None, seq, LANES), lambda b, h: (b, 0, h)),
                  pl.BlockSpec((None, seq, LANES), lambda b, h: (b, 0, 0)),
                  pl.BlockSpec((None, seq, LANES), lambda b, h: (b, 0, npair + h)),
                  pl.BlockSpec((None, seq, LANES), lambda b, h: (b, 0, 0)),
                  pl.BlockSpec((None, LANES, seq), lambda b, h: (b, h, 0)),
                  pl.BlockSpec((None, seq, LANES), lambda b, h: (b, 0, 3 * npair + h)),
                  pl.BlockSpec((1, LANES), lambda b, h: (0, 0))],
        out_specs=pl.BlockSpec((None, seq, LANES), lambda b, h: (b, 0, h)),
        out_shape=jax.ShapeDtypeStruct((bsz, seq, D_GRP), BF16),
        scratch_shapes=[pltpu.VMEM((2, 1, seq), F32), pltpu.VMEM((2, 1, seq), F32),
                        pltpu.VMEM((2, HEAD_DIM, seq), F32)],
        compiler_params=pltpu.CompilerParams(
            dimension_semantics=("parallel", "parallel"), vmem_limit_bytes=VMEM_LIMIT),
        name="fox",
    )(p_x, q_bias, p_x, k_bias, v_t, p_x, o_gain)


def _outproj_kernel(x_ref, yr_ref, yf_ref, g1_ref, sh_ref, sc_ref, ng_ref, wor_ref, wof_ref,
                    wrt_ref, wrl_ref, brt_ref, x1_ref, h2_ref, idx_ref, gate_ref, rank_ref, cnt_ref,
                    carry_ref):
    @pl.when(pl.program_id(0) == 0)
    def _():
        carry_ref[...] = jnp.zeros_like(carry_ref)

    y = (jnp.dot(yr_ref[...], wor_ref[...], preferred_element_type=F32)
         + jnp.dot(yf_ref[...], wof_ref[...], preferred_element_type=F32))
    x1 = x_ref[...] + g1_ref[...] * y
    x1_ref[...] = x1
    tm = x1.shape[0]
    h = x1 * lax.rsqrt(jnp.mean(x1 * x1, axis=-1, keepdims=True) + NORM_EPS) * ng_ref[...]
    h2 = h * (1.0 + sc_ref[...]) + sh_ref[...]
    h2_ref[...] = _pack_rows(h2)

    h_hi = h2.astype(BF16)
    h_lo = (h2 - h_hi.astype(F32)).astype(BF16)
    logits = (lax.dot_general(wrt_ref[...], h_hi, _NT, preferred_element_type=F32)
              + lax.dot_general(wrt_ref[...], h_lo, _NT, preferred_element_type=F32)
              + lax.dot_general(wrl_ref[...], h_hi, _NT, preferred_element_type=F32))
    lg = logits[:N_EXPERTS, :] + brt_ref[...]
    expert = _iota((N_EXPERTS, tm), 0)
    picks = []
    hot_sum = jnp.zeros((N_EXPERTS, tm), F32)
    for _ in range(TOP_K):
        m = jnp.max(lg, axis=0, keepdims=True)
        sel = jnp.min(jnp.where(lg == m, expert, N_EXPERTS), axis=0, keepdims=True)
        hot = expert == sel
        picks.append((m, sel, hot))
        hot_sum = hot_sum + hot.astype(F32)
        lg = jnp.where(hot, -jnp.inf, lg)
    es = [jnp.exp(m - picks[0][0]) for m, _, _ in picks]
    den = es[0] + es[1] + es[2] + es[3]

    earlier = (_iota((tm, tm), 0) < _iota((tm, tm), 1)).astype(BF16)
    before = jnp.dot(hot_sum.astype(BF16), earlier, preferred_element_type=F32) + carry_ref[...]
    ranks = [jnp.sum(jnp.where(hot, before, 0.0), axis=0, keepdims=True).astype(jnp.int32)
             for _, _, hot in picks]
    pad_i = jnp.zeros((8 - TOP_K, tm), jnp.int32)
    idx_ref[...] = jnp.concatenate([sel for _, sel, _ in picks] + [pad_i], axis=0)
    gate_ref[...] = jnp.concatenate([e / den for e in es] + [pad_i.astype(F32)], axis=0)
    rank_ref[...] = jnp.concatenate(ranks + [pad_i], axis=0)
    carry_ref[...] = carry_ref[...] + jnp.sum(hot_sum, axis=1, keepdims=True)
    cnt_ref[...] = jnp.broadcast_to(carry_ref[...], cnt_ref.shape)


def _outproj(x2d, y_r, y_f, gate1, shift2, scale2, norm_g, wo_r, wo_f, w_rt, b_rt, tm, seq,
             row0, t):
    w_rt_hi = w_rt.astype(BF16)
    w_rt_lo = (w_rt - w_rt_hi.astype(F32)).astype(BF16)
    per_b = seq // tm
    blk0 = row0 // tm
    const = lambda i: (0, 0)
    rows = lambda i: (i, 0)
    rows_in = lambda i: (i + blk0, 0)
    mod = pl.BlockSpec((None, 1, D_MODEL), lambda i: ((i + blk0) // per_b, 0, 0))
    return pl.pallas_call(
        _outproj_kernel,
        grid=(t // tm,),
        in_specs=[pl.BlockSpec((tm, D_MODEL), rows_in),
                  pl.BlockSpec((tm, D_GRP), rows_in),
                  pl.BlockSpec((tm, D_GRP), rows_in),
                  mod, mod, mod,
                  pl.BlockSpec((1, D_MODEL), const),
                  pl.BlockSpec((D_GRP, D_MODEL), const),
                  pl.BlockSpec((D_GRP, D_MODEL), const),
                  pl.BlockSpec((LANES, D_MODEL), const),
                  pl.BlockSpec((LANES, D_MODEL), const),
                  pl.BlockSpec((N_EXPERTS, 1), const)],
        out_specs=[pl.BlockSpec((tm, D_MODEL), rows),
                   pl.BlockSpec((tm, D_PACK), rows),
                   pl.BlockSpec((8, tm), lambda i: (0, i)),
                   pl.BlockSpec((8, tm), lambda i: (0, i)),
                   pl.BlockSpec((8, tm), lambda i: (0, i)),
                   pl.BlockSpec((N_EXPERTS, LANES), const)],
        out_shape=[jax.ShapeDtypeStruct((t, D_MODEL), F32),
                   jax.ShapeDtypeStruct((t, D_PACK), jnp.uint32),
                   jax.ShapeDtypeStruct((8, t), jnp.int32),
                   jax.ShapeDtypeStruct((8, t), F32),
                   jax.ShapeDtypeStruct((8, t), jnp.int32),
                   jax.ShapeDtypeStruct((N_EXPERTS, LANES), F32)],
        scratch_shapes=[pltpu.VMEM((N_EXPERTS, 1), F32)],
        compiler_params=pltpu.CompilerParams(
            dimension_semantics=("arbitrary",), vmem_limit_bytes=VMEM_LIMIT),
        name="outproj",
    )(x2d, y_r, y_f, gate1, shift2, scale2, norm_g, wo_r, wo_f, w_rt_hi, w_rt_lo, b_rt)


SC_CORES = 2
SC_SUBCORES = 16
SC_ROWS = 64


def _sc_gather_rows(idx, src):
    n_workers = SC_CORES * SC_SUBCORES
    m = idx.shape[0]
    d = src.shape[1]
    assert m % (n_workers * SC_ROWS) == 0
    n_chunks = m // (n_workers * SC_ROWS)
    mesh = plsc.VectorSubcoreMesh(core_axis_name="c", subcore_axis_name="s")

    @functools.partial(
        pl.kernel, mesh=mesh,
        out_type=jax.ShapeDtypeStruct((m, d), src.dtype),
        scratch_types=[pltpu.VMEM((n_chunks, SC_ROWS), jnp.int32),
                       pltpu.VMEM((SC_ROWS, d), src.dtype),
                       pltpu.SemaphoreType.DMA],
        name="sc_gather")
    def gather(src_hbm, idx_hbm, out_hbm, idx_v, rows_v, sem):
        wid = lax.axis_index("s") * SC_CORES + lax.axis_index("c")
        pltpu.sync_copy(idx_hbm.at[wid], idx_v)

        @pl.loop(0, n_chunks)
        def _(j):
            pltpu.async_copy(src_hbm.at[idx_v.at[j]], rows_v, sem).wait()
            pltpu.sync_copy(rows_v, out_hbm.at[pl.ds((wid * n_chunks + j) * SC_ROWS, SC_ROWS)])

    return gather(src, idx.reshape(n_workers, n_chunks, SC_ROWS))


def _sc_scatter_rows(src, dest, n_out):
    n_workers = SC_CORES * SC_SUBCORES
    t, d = src.shape
    n_slot = dest.shape[0]
    assert t % (n_workers * SC_ROWS) == 0
    n_chunks = t // (n_workers * SC_ROWS)
    mesh = plsc.VectorSubcoreMesh(core_axis_name="c", subcore_axis_name="s")
    idx = dest.reshape(n_slot, n_workers, n_chunks, SC_ROWS).transpose(1, 2, 0, 3)
    idx = idx.reshape(n_workers, n_chunks * n_slot, SC_ROWS)

    @functools.partial(
        pl.kernel, mesh=mesh,
        out_type=jax.ShapeDtypeStruct((n_out, d), src.dtype),
        scratch_types=[pltpu.VMEM((n_chunks * n_slot, SC_ROWS), jnp.int32),
                       pltpu.VMEM((SC_ROWS, d), src.dtype)],
        name="sc_scatter")
    def scatter(src_hbm, idx_hbm, out_hbm, idx_v, rows_v):
        wid = lax.axis_index("s") * SC_CORES + lax.axis_index("c")
        pltpu.sync_copy(idx_hbm.at[wid], idx_v)

        @pl.loop(0, n_chunks)
        def _(j):
            pltpu.sync_copy(src_hbm.at[pl.ds((wid * n_chunks + j) * SC_ROWS, SC_ROWS)], rows_v)
            for k in range(n_slot):
                pltpu.sync_copy(rows_v, out_hbm.at[idx_v.at[j * n_slot + k]])

    return scatter(src, idx)


def _expert_kernel(be_ref, nv_ref, x_ref, wgu_ref, bgu_ref, wd_ref, bd_ref, o_ref):
    del be_ref
    valid = _iota((EXPERT_BLOCK, 1), 0) < nv_ref[pl.program_id(0)]
    lo, hi = _unpack_rows(jnp.where(valid, x_ref[...], jnp.uint32(0)))
    x = jnp.concatenate([lo.astype(BF16), hi.astype(BF16)], axis=1)
    half = EXPERT_BLOCK // 2
    gus = [jnp.dot(x[r * half:(r + 1) * half], wgu_ref[...], preferred_element_type=F32)
           + bgu_ref[...] for r in range(2)]
    for r, gu in enumerate(gus):
        gate = jnp.minimum(gu[:, :D_MODEL], SWIGLU_LIMIT)
        up = jnp.clip(gu[:, D_MODEL:], -SWIGLU_LIMIT, SWIGLU_LIMIT)
        act = gate * _sigmoid(SWIGLU_ALPHA * gate) * (up + 1.0)
        o_ref[r * half:(r + 1) * half, :] = _pack_rows(
            jnp.dot(act.astype(BF16), wd_ref[...], preferred_element_type=F32) + bd_ref[...])


def _experts(block_e, n_valid, xs, w_gu, b_gu, w_d, b_d):
    n_blocks = block_e.shape[0]
    grid_spec = pltpu.PrefetchScalarGridSpec(
        num_scalar_prefetch=2,
        grid=(n_blocks,),
        in_specs=[pl.BlockSpec((EXPERT_BLOCK, D_PACK), lambda j, be, nv: (j, 0)),
                  pl.BlockSpec((None, D_MODEL, 2 * D_MODEL), lambda j, be, nv: (be[j], 0, 0)),
                  pl.BlockSpec((None, 1, 2 * D_MODEL), lambda j, be, nv: (be[j], 0, 0)),
                  pl.BlockSpec((None, D_MODEL, D_MODEL), lambda j, be, nv: (be[j], 0, 0)),
                  pl.BlockSpec((None, 1, D_MODEL), lambda j, be, nv: (be[j], 0, 0))],
        out_specs=pl.BlockSpec((EXPERT_BLOCK, D_PACK), lambda j, be, nv: (j, 0)),
    )
    return pl.pallas_call(
        _expert_kernel,
        grid_spec=grid_spec,
        out_shape=jax.ShapeDtypeStruct(xs.shape, jnp.uint32),
        compiler_params=pltpu.CompilerParams(
            dimension_semantics=("arbitrary",), vmem_limit_bytes=VMEM_LIMIT),
        name="experts",
    )(block_e, n_valid, xs, w_gu, b_gu, w_d, b_d)


COMBINE_TOKENS = 512
MOE_SPLITS = 2


def _combine_kernel(yg_ref, x1_ref, gate_ref, g2_ref, fg_ref, o_ref):
    gates = gate_ref[...].T
    acc_lo = acc_hi = None
    for kk in range(TOP_K):
        lo, hi = _unpack_rows(yg_ref[kk * COMBINE_TOKENS:(kk + 1) * COMBINE_TOKENS, :])
        g = gates[:, kk:kk + 1]
        acc_lo = g * lo if acc_lo is None else acc_lo + g * lo
        acc_hi = g * hi if acc_hi is None else acc_hi + g * hi
    x2 = x1_ref[...] + g2_ref[...] * jnp.concatenate([acc_lo, acc_hi], axis=1)
    o_ref[...] = x2 * lax.rsqrt(jnp.mean(x2 * x2, axis=-1, keepdims=True) + NORM_EPS) * fg_ref[...]


def _combine_kernel_into(prev_ref, *refs):
    del prev_ref
    _combine_kernel(*refs)


def _combine(yg, x1, gates, gate2, final_g, seq, row0, t_total, prev):
    t = x1.shape[0]
    tm = COMBINE_TOKENS
    per_b = seq // tm
    blk0 = row0 // tm
    rows = lambda i: (i, 0)
    in_specs = [pl.BlockSpec((TOP_K * tm, D_PACK), rows),
                pl.BlockSpec((tm, D_MODEL), rows),
                pl.BlockSpec((8, tm), lambda i: (0, i)),
                pl.BlockSpec((None, 1, D_MODEL), lambda i: ((i + blk0) // per_b, 0, 0)),
                pl.BlockSpec((1, D_MODEL), lambda i: (0, 0))]
    args = (yg, x1, gates, gate2, final_g)
    if prev is not None:
        in_specs = [pl.BlockSpec(memory_space=pl.ANY)] + in_specs
        args = (prev,) + args
    return pl.pallas_call(
        _combine_kernel if prev is None else _combine_kernel_into,
        grid=(t // tm,),
        in_specs=in_specs,
        out_specs=pl.BlockSpec((tm, D_MODEL), lambda i: (i + blk0, 0)),
        out_shape=jax.ShapeDtypeStruct((t_total, D_MODEL), F32),
        input_output_aliases={} if prev is None else {0: 0},
        compiler_params=pltpu.CompilerParams(
            dimension_semantics=("parallel",), vmem_limit_bytes=VMEM_LIMIT),
        name="combine",
    )(*args)


def _moe(h2, idx, gates, rank, counts, x1, gate2, final_g, w_gu, b_gu, w_d, b_d, seq,
         row0, t_total, prev):
    t = h2.shape[0]
    n_slots = t * TOP_K
    n_blocks = -(-n_slots // EXPERT_BLOCK) + N_EXPERTS
    cap = n_blocks * EXPERT_BLOCK
    padded = (counts + EXPERT_BLOCK - 1) // EXPERT_BLOCK * EXPERT_BLOCK
    pad_ends = jnp.cumsum(padded)
    pad_starts = pad_ends - padded
    experts = jnp.arange(N_EXPERTS, dtype=jnp.int32)
    dest = jnp.sum(jnp.where(idx[..., None] == experts, pad_starts, 0), axis=-1) + rank
    block_starts = jnp.arange(n_blocks, dtype=jnp.int32) * EXPERT_BLOCK
    block_e = jnp.minimum(jnp.sum(block_starts[:, None] >= pad_ends[None, :], axis=1),
                          N_EXPERTS - 1).astype(jnp.int32)
    n_valid = jnp.clip(counts[block_e] - (block_starts - pad_starts[block_e]), 0, EXPERT_BLOCK)

    xs = _sc_scatter_rows(h2, dest, cap)
    yb = _experts(block_e, n_valid.astype(jnp.int32), xs, w_gu, b_gu, w_d, b_d)
    dest_blocks = dest.reshape(TOP_K, -1, COMBINE_TOKENS).transpose(1, 0, 2).reshape(-1)
    yg = _sc_gather_rows(dest_blocks, yb)
    return _combine(yg, x1, gates, gate2, final_g, seq, row0, t_total, prev)


def _layer(x, c_mod, norm1_g, w_in, mu_shift, w0, w2, a0, a2, g2, k_k, k_a, r_k, gn_w, gn_b, b_f,
           q_norm_g, k_norm_g, o_norm_g, w_out, norm2_g, w_router, b_router, w_gate_up,
           b_gate_up, w_down, b_down, final_g, tm_in, tm_out):
    bsz, seq, _ = x.shape
    shift1, scale1, gate1, shift2, scale2, gate2 = (
        m.reshape(bsz, 1, D_MODEL) for m in jnp.split(c_mod, 6, axis=-1))
    row = lambda v: v.reshape(1, -1)

    w_r = w_in[:, :RWKV_COLS].astype(BF16)
    w_x = w_in[:, RWKV_COLS:RWKV_COLS + FOX_MAIN].astype(BF16)
    w_f = w_in[:, RWKV_COLS + FOX_MAIN:].T
    b_fp = jnp.pad(b_f, (0, LANES - N_HEADS)).reshape(1, LANES)
    qk_gain = jnp.concatenate([jnp.tile(q_norm_g, N_HEADS) * HEAD_DIM ** -0.5,
                               jnp.tile(k_norm_g, N_HEADS)]).reshape(1, -1)
    p_r, p_x, k_bias, q_bias = _inproj(x, shift1, scale1, row(norm1_g), w_r, w_x, w_f, b_fp,
                                       qk_gain, tm_in)

    zeros = jnp.zeros((LANES - 64, D_GRP), F32)
    w2p = jnp.concatenate([w2, zeros], axis=0).astype(BF16)
    a2p = jnp.concatenate([zeros, a2], axis=0).astype(BF16)
    y_r, w_gu, w_d = _rwkv(p_r, row(mu_shift), row(w0), w2p, row(a0), a2p, g2.astype(BF16),
                           row(k_k), row(k_a), row(r_k), row(gn_w), row(gn_b), w_gate_up, w_down)

    y_f = _fox(p_x, k_bias, q_bias, jnp.tile(o_norm_g, 2).reshape(1, LANES))

    t = bsz * seq
    w_rt = jnp.pad(w_router.T, ((0, LANES - N_EXPERTS), (0, 0)))
    b_rt = b_router.reshape(N_EXPERTS, 1)
    wo = w_out.astype(BF16)
    b_gu, b_d = b_gate_up.reshape(N_EXPERTS, 1, -1), b_down.reshape(N_EXPERTS, 1, -1)
    t_part = t // MOE_SPLITS
    out = None
    for part in range(MOE_SPLITS):
        row0 = part * t_part
        x1, h2, idx, gates, rank, cnt = _outproj(
            x.reshape(t, D_MODEL), y_r.reshape(t, D_GRP), y_f.reshape(t, D_GRP), gate1, shift2,
            scale2, row(norm2_g), wo[:D_GRP], wo[D_GRP:], w_rt, b_rt, tm_out, seq, row0, t_part)
        counts = cnt[:, 0].astype(jnp.int32)
        out = _moe(h2, idx[:TOP_K], gates, rank[:TOP_K], counts, x1, gate2, row(final_g),
                   w_gu, b_gu, w_d, b_d, seq, row0, t, out)
    return out.reshape(bsz, seq, D_MODEL)


def kernel(x, c, w_ada, b_ada, norm1_g, w_in, mu_shift, w0, w2, a0, a2, g2, k_k, k_a, r_k, gn_w,
           gn_b, b_f, q_norm_g, k_norm_g, o_norm_g, w_out, norm2_g, w_router, b_router, w_gate_up,
           b_gate_up, w_down, b_down, final_g):
    assert w_ada.shape[0] == 1, "single-layer block"
    c_mod = _adaln(c, w_ada[0], b_ada[0])
    return _layer(x, c_mod, norm1_g[0], w_in[0], mu_shift[0], w0[0], w2[0], a0[0], a2[0], g2[0],
                  k_k[0], k_a[0], r_k[0], gn_w[0], gn_b[0], b_f[0], q_norm_g[0], k_norm_g[0],
                  o_norm_g[0], w_out[0], norm2_g[0], w_router[0], b_router[0], w_gate_up[0],
                  b_gate_up[0], w_down[0], b_down[0], final_g,
                  tm_in=min(512, x.shape[1]), tm_out=min(1024, x.shape[1]))
```

```python
import functools

import jax
import jax.numpy as jnp
from jax import lax
from jax.experimental import pallas as pl
from jax.experimental.pallas import tpu as pltpu
from jax.experimental.pallas import tpu_sc as plsc

F32 = jnp.float32
BF16 = jnp.bfloat16
HIGHEST = lax.Precision.HIGHEST

D_MODEL = 1024
HEAD_DIM = 64
N_HEADS = 8
D_GRP = N_HEADS * HEAD_DIM
RWKV_COLS = 1792
LORA_OFF = 3 * D_GRP
GATE_OFF = LORA_OFF + 128
FOX_MAIN = 4 * D_GRP
N_EXPERTS = 32
TOP_K = 4
EXPERT_BLOCK = 512
SWIGLU_ALPHA = 1.702
SWIGLU_LIMIT = 7.0
NORM_EPS = 1e-6
GN_EPS = 64e-5
LOG2_E = 1.4426950408889634
LANES = 128
CHUNK = 64
FOX_SUB_KEYS = 512
HEADS_PER_SCAN = 4
SCAN_W = HEADS_PER_SCAN * HEAD_DIM
SEG_TERMS = 1
CUM_TERMS = 2
VMEM_LIMIT = 56 * 1024 * 1024


def _dot(a, b):
    return jnp.dot(a.astype(BF16), b.astype(BF16), preferred_element_type=F32)


def _dot_nt(a, b):
    return lax.dot_general(a.astype(BF16), b.astype(BF16), (((1,), (1,)), ((), ())),
                           preferred_element_type=F32)


def _dot_tn(a, b):
    return lax.dot_general(a.astype(BF16), b.astype(BF16), (((0,), (0,)), ((), ())),
                           preferred_element_type=F32)


def _fdot(a, b):
    return jnp.dot(a, b, precision=HIGHEST, preferred_element_type=F32)


def _split_dot(x, m, terms=2, left=False):
    acc = None
    rem = x
    for _ in range(terms):
        part = rem.astype(BF16)
        rem = rem - part.astype(F32)
        d = (jnp.dot(m, part, preferred_element_type=F32) if left
             else jnp.dot(part, m, preferred_element_type=F32))
        acc = d if acc is None else acc + d
    return acc


def _iota(shape, dim):
    return lax.broadcasted_iota(jnp.int32, shape, dim)


def _seg_reduce_mat(n):
    return (_iota((n, LANES), 0) // HEAD_DIM == _iota((n, LANES), 1)).astype(BF16)


def _seg_expand_mat(n):
    return (_iota((LANES, n), 1) // HEAD_DIM == _iota((LANES, n), 0)).astype(BF16)


def _tri(n, strict):
    r, c = _iota((n, n), 0), _iota((n, n), 1)
    return ((r > c) if strict else (r >= c)).astype(BF16)


D_PACK = D_MODEL // 2


def _pack_rows(x):
    lo = lax.bitcast_convert_type(x[:, :D_PACK].astype(BF16).astype(F32), jnp.uint32)
    hi = lax.bitcast_convert_type(x[:, D_PACK:].astype(BF16).astype(F32), jnp.uint32)
    return hi | (lo >> 16)


def _unpack_rows(p):
    lo = lax.bitcast_convert_type(p << 16, F32)
    hi = lax.bitcast_convert_type(p & jnp.uint32(0xFFFF0000), F32)
    return lo, hi


def _log_sigmoid(z):
    return jnp.minimum(z, 0.0) - jnp.log(1.0 + jnp.exp(-jnp.abs(z)))


def _sigmoid(z):
    return 1.0 / (1.0 + jnp.exp(-z))


def _adaln_kernel(c_ref, w_ref, b_ref, o_ref):
    c = c_ref[...]
    o_ref[...] = _fdot(c * _sigmoid(c), w_ref[...]) + b_ref[...]


def _adaln(c, w_ada, b_ada):
    bsz = c.shape[0]
    n_mod = w_ada.shape[1] // D_MODEL
    return pl.pallas_call(
        _adaln_kernel,
        grid=(n_mod,),
        in_specs=[pl.BlockSpec((bsz, D_MODEL), lambda j: (0, 0)),
                  pl.BlockSpec((D_MODEL, D_MODEL), lambda j: (0, j)),
                  pl.BlockSpec((1, D_MODEL), lambda j: (0, j))],
        out_specs=pl.BlockSpec((bsz, D_MODEL), lambda j: (0, j)),
        out_shape=jax.ShapeDtypeStruct((bsz, n_mod * D_MODEL), F32),
        name="adaln",
    )(c, w_ada, b_ada.reshape(1, -1))


def _inproj_kernel(x_ref, sh_ref, sc_ref, g_ref, wr_ref, wx_ref, wft_ref, bf_ref, qkg_ref,
                   pr_ref, px_ref, kb_ref, qb_ref, carry_ref):
    @pl.when(pl.program_id(1) == 0)
    def _():
        carry_ref[...] = jnp.zeros_like(carry_ref)

    x = x_ref[...]
    tm = x.shape[0]
    h = x * lax.rsqrt(jnp.mean(x * x, axis=-1, keepdims=True) + NORM_EPS) * g_ref[...]
    h = h * (1.0 + sc_ref[...]) + sh_ref[...]
    hb = h.astype(BF16)

    pr_ref[...] = jnp.dot(hb, wr_ref[...], preferred_element_type=F32).astype(BF16)

    px = jnp.dot(hb, wx_ref[...], preferred_element_type=F32)
    qk = px[:, :2 * D_GRP]
    ss = _split_dot(qk * qk, _seg_reduce_mat(2 * D_GRP), SEG_TERMS)
    inv = lax.rsqrt(ss * (1.0 / HEAD_DIM) + NORM_EPS)
    qk = qk * _split_dot(inv, _seg_expand_mat(2 * D_GRP), SEG_TERMS) * qkg_ref[...]
    px_ref[:, :2 * D_GRP] = qk.astype(BF16)
    px_ref[:, 2 * D_GRP:] = px[:, 2 * D_GRP:].astype(BF16)

    lane = _iota((1, LANES), 1)
    z = jnp.zeros((tm, LANES), F32)
    for hd in range(N_HEADS):
        zh = jnp.sum(h * wft_ref[hd:hd + 1, :], axis=-1, keepdims=True)
        z = jnp.where(lane == hd, zh, z)
    cum = _log_sigmoid(z + bf_ref[...])
    row_id = _iota((tm, 1), 0)
    shift = 1
    while shift < tm:
        cum = cum + jnp.where(row_id >= shift, pltpu.roll(cum, shift, axis=0), 0.0)
        shift *= 2
    cum = cum + carry_ref[...]
    carry_ref[...] = cum[tm - 1:tm, :]

    parts = []
    rem = cum * LOG2_E
    for _ in range(3):
        part = rem.astype(BF16)
        rem = rem - part.astype(F32)
        parts.append(part)
    src, dst = _iota((LANES, LANES), 0), _iota((LANES, LANES), 1)

    def spread(offset):
        return sum(jnp.dot(part, ((dst == 8 * src + offset + t) & (src < N_HEADS)).astype(BF16),
                           preferred_element_type=F32) for t, part in enumerate(parts))

    slot = _iota((1, LANES), 1) % 8
    kb_ref[...] = (jnp.where((slot >= 3) & (slot < 6), 1.0, 0.0) - spread(0)).astype(BF16)
    qb_ref[...] = (jnp.where(slot < 3, 1.0, 0.0) + spread(3)).astype(BF16)


def _inproj(x, shift, scale, g, w_r, w_x, w_f_t, b_f, qk_gain, tm):
    bsz, seq, _ = x.shape
    const = lambda b, s: (0, 0)
    return pl.pallas_call(
        _inproj_kernel,
        grid=(bsz, seq // tm),
        in_specs=[pl.BlockSpec((None, tm, D_MODEL), lambda b, s: (b, s, 0)),
                  pl.BlockSpec((None, 1, D_MODEL), lambda b, s: (b, 0, 0)),
                  pl.BlockSpec((None, 1, D_MODEL), lambda b, s: (b, 0, 0)),
                  pl.BlockSpec((1, D_MODEL), const),
                  pl.BlockSpec((D_MODEL, RWKV_COLS), const),
                  pl.BlockSpec((D_MODEL, FOX_MAIN), const),
                  pl.BlockSpec((N_HEADS, D_MODEL), const),
                  pl.BlockSpec((1, LANES), const),
                  pl.BlockSpec((1, 2 * D_GRP), const)],
        out_specs=[pl.BlockSpec((None, tm, RWKV_COLS), lambda b, s: (b, s, 0)),
                   pl.BlockSpec((None, tm, FOX_MAIN), lambda b, s: (b, s, 0)),
                   pl.BlockSpec((None, tm, LANES), lambda b, s: (b, s, 0)),
                   pl.BlockSpec((None, tm, LANES), lambda b, s: (b, s, 0))],
        out_shape=[jax.ShapeDtypeStruct((bsz, seq, RWKV_COLS), BF16),
                   jax.ShapeDtypeStruct((bsz, seq, FOX_MAIN), BF16),
                   jax.ShapeDtypeStruct((bsz, seq, LANES), BF16),
                   jax.ShapeDtypeStruct((bsz, seq, LANES), BF16)],
        scratch_shapes=[pltpu.VMEM((1, LANES), F32)],
        compiler_params=pltpu.CompilerParams(
            dimension_semantics=("parallel", "arbitrary"), vmem_limit_bytes=VMEM_LIMIT),
        name="inproj",
    )(x, shift, scale, g, w_r, w_x, w_f_t, b_f, qk_gain)


_NN = (((1,), (0,)), ((), ()))
_NT = (((1,), (1,)), ((), ()))
_TN = (((0,), (0,)), ((), ()))
SCAN_N = HEADS_PER_SCAN * CHUNK
BATCH_PER_STEP = 8
INV_LEVELS = 5
M_HEAD, M_STRICT, M_INCL, M_EYE, M_BASE, M_OFF = 0, 1, 2, 3, 4, 5


def _bdot(a, b, dims):
    return lax.dot_general(a, b, dims, preferred_element_type=F32)


def _scan_masks():
    rr, cc = _iota((SCAN_N, SCAN_W), 0), _iota((SCAN_N, SCAN_W), 1)
    ri, ci = _iota((SCAN_N, SCAN_N), 0), _iota((SCAN_N, SCAN_N), 1)
    same = ri // CHUNK == ci // CHUNK
    masks = [rr // CHUNK == cc // HEAD_DIM, same & (ri > ci), same & (ri >= ci), ri == ci,
             (ri // 2 == ci // 2) & (ri > ci)]
    blk = 2
    while blk < CHUNK:
        masks.append((ri // (2 * blk) == ci // (2 * blk)) & (ri // blk != ci // blk) & (ri > ci))
        blk *= 2
    return jnp.stack(masks).astype(BF16)


def _rwkv_kernel(p_ref, masks_ref, mu_ref, w0_ref, w2_ref, a0_ref, a2_ref, g2_ref, kk_ref, ka_ref,
                 rk_ref, gnw_ref, gnb_ref, wgu_ref, wd_ref, o_ref, wgu_bf_ref, wd_bf_ref,
                 last_ref, state_ref):
    wgu_bf_ref[...] = wgu_ref[...].astype(BF16)
    wd_bf_ref[...] = wd_ref[...].astype(BF16)

    @pl.when(pl.program_id(1) == 0)
    def _():
        last_ref[...] = jnp.zeros_like(last_ref)
        state_ref[...] = jnp.zeros_like(state_ref)

    mu, w0, w2, a0, a2, g2, k_k, k_a, r_k, gn_w, gn_b = (
        ref[...] for ref in (mu_ref, w0_ref, w2_ref, a0_ref, a2_ref, g2_ref, kk_ref, ka_ref,
                             rk_ref, gnw_ref, gnb_ref))
    rows = BATCH_PER_STEP * CHUNK
    p = p_ref[...].astype(F32).reshape(rows, RWKV_COLS)
    row_id = _iota((rows, 1), 0)
    prev = pltpu.roll(p, 1, axis=0)
    for bb in range(BATCH_PER_STEP):
        prev = jnp.where(row_id == bb * CHUNK, last_ref[bb], prev)
        last_ref[bb] = p[(bb + 1) * CHUNK - 1:(bb + 1) * CHUNK, :]
    pf = p + mu * (prev - p)
    r = pf[:, 0:D_GRP]
    k = pf[:, D_GRP:2 * D_GRP]
    v = pf[:, 2 * D_GRP:3 * D_GRP]
    lora = pf[:, LORA_OFF:GATE_OFF]
    gd = pf[:, GATE_OFF:RWKV_COLS]

    wlog = w0 + _dot(jnp.tanh(lora), w2)
    neg = -wlog
    softplus = jnp.maximum(neg, 0.0) + jnp.log(1.0 + jnp.exp(-jnp.abs(neg)))
    ld = -jnp.exp(-softplus - 0.5)
    a = _sigmoid(a0 + _dot(lora, a2))
    g = _dot(_sigmoid(gd), g2)

    red, exp_m = _seg_reduce_mat(D_GRP), _seg_expand_mat(D_GRP)
    kk = k * k_k
    n2 = _split_dot(kk * kk, red, SEG_TERMS)
    kk = kk * _split_dot(1.0 / jnp.maximum(jnp.sqrt(n2), 1e-12), exp_m, SEG_TERMS)
    k2 = k * (1.0 + (a - 1.0) * k_a)

    tr, tc = _iota((rows, rows), 0), _iota((rows, rows), 1)
    tri = ((tr >= tc) & (tr // CHUNK == tc // CHUNK)).astype(BF16)
    cl = _split_dot(ld, tri, terms=CUM_TERMS, left=True)
    cl_end = jnp.concatenate(
        [jnp.broadcast_to(cl[(bb + 1) * CHUNK - 1:(bb + 1) * CHUNK, :], (CHUNK, D_GRP))
         for bb in range(BATCH_PER_STEP)], axis=0)
    e_in = jnp.exp(cl)
    e_out = jnp.exp(-cl)
    e_rem = jnp.exp(cl_end - cl)
    p_end = jnp.exp(cl_end)
    kka = kk * a
    ops = [(-kk * jnp.exp(cl - ld)).astype(BF16), (kka * e_out).astype(BF16),
           (k2 * e_out).astype(BF16), (r * e_in).astype(BF16), v.astype(BF16),
           (kka * e_rem).astype(BF16), (k2 * e_rem).astype(BF16)]

    chains = [(bb, grp) for bb in range(BATCH_PER_STEP)
              for grp in range(N_HEADS // HEADS_PER_SCAN)]
    head_mask = masks_ref[M_HEAD]
    strict, incl = masks_ref[M_STRICT], masks_ref[M_INCL]

    def stacked(op, bb, grp):
        part = op[bb * CHUNK:(bb + 1) * CHUNK, grp * SCAN_W:(grp + 1) * SCAN_W]
        return jnp.concatenate([part] * HEADS_PER_SCAN, axis=0) * head_mask

    xs = [[stacked(op, bb, grp) for op in ops] for bb, grp in chains]
    st = [state_ref[bb, grp] for bb, grp in chains]
    sb = [s.astype(BF16) for s in st]
    nab = [_bdot(x[0], x[1], _NT).astype(BF16) for x in xs]
    aak = [_bdot(x[0], x[2], _NT).astype(BF16) * strict for x in xs]
    arb = [_bdot(x[3], x[1], _NT).astype(BF16) * incl for x in xs]
    ark = [_bdot(x[3], x[2], _NT).astype(BF16) * incl for x in xs]
    t_inv = [masks_ref[M_EYE] + n * masks_ref[M_BASE] for n in nab]
    for lvl in range(INV_LEVELS):
        half = [_bdot(t, n * masks_ref[M_OFF + lvl], _NN).astype(BF16) for t, n in zip(t_inv, nab)]
        t_inv = [t + _bdot(h, t, _NN).astype(BF16) for t, h in zip(t_inv, half)]
    rhs = [(_bdot(x[0], s, _NT) + _bdot(k, x[4], _NN)).astype(BF16)
           for x, s, k in zip(xs, sb, aak)]
    sa = [_bdot(t, h, _NN).astype(BF16) for t, h in zip(t_inv, rhs)]
    ys = [_bdot(x[3], s, _NT) + _bdot(b, u, _NN) + _bdot(k, x[4], _NN)
          for x, s, b, u, k in zip(xs, sb, arb, sa, ark)]
    for (bb, grp), x, s, u in zip(chains, xs, st, sa):
        decay = p_end[bb * CHUNK:bb * CHUNK + 1, grp * SCAN_W:(grp + 1) * SCAN_W]
        state_ref[bb, grp] = s * decay + _bdot(u, x[5], _TN) + _bdot(x[4], x[6], _TN)
    ys = [y[0:CHUNK] + y[CHUNK:2 * CHUNK] + y[2 * CHUNK:3 * CHUNK] + y[3 * CHUNK:4 * CHUNK]
          for y in ys]
    n_grp = N_HEADS // HEADS_PER_SCAN
    y = jnp.concatenate([jnp.concatenate(ys[bb * n_grp:(bb + 1) * n_grp], axis=1)
                         for bb in range(BATCH_PER_STEP)], axis=0)

    mean = _split_dot(_split_dot(y, red, SEG_TERMS) * (1.0 / HEAD_DIM), exp_m, SEG_TERMS)
    d = y - mean
    var = _split_dot(d * d, red, SEG_TERMS) * (1.0 / HEAD_DIM)
    yn = d * _split_dot(lax.rsqrt(var + GN_EPS), exp_m, SEG_TERMS) * gn_w + gn_b
    bonus = _split_dot(_split_dot(r * k2 * r_k, red, SEG_TERMS), exp_m, SEG_TERMS) * v
    o_ref[...] = ((yn + bonus) * g).astype(BF16).reshape(BATCH_PER_STEP, CHUNK, D_GRP)


def _rwkv(p_r, mu, w0, w2p, a0, a2p, g2, k_k, k_a, r_k, gn_w, gn_b, w_gate_up, w_down):
    bsz, seq, _ = p_r.shape
    assert bsz % BATCH_PER_STEP == 0
    n_chunk = seq // CHUNK
    n_step = (bsz // BATCH_PER_STEP) * n_chunk
    wgu2d = w_gate_up.reshape(-1, w_gate_up.shape[-1])
    wd2d = w_down.reshape(-1, w_down.shape[-1])
    assert wgu2d.shape[0] % (8 * n_step) == 0 and wd2d.shape[0] == wgu2d.shape[0]
    slab = wgu2d.shape[0] // n_step
    masks = _scan_masks()
    const = lambda b, s: (0, 0)
    step = lambda b, s: (b * n_chunk + s, 0)
    vec = pl.BlockSpec((1, D_GRP), const)
    y, wgu_bf, wd_bf = pl.pallas_call(
        _rwkv_kernel,
        grid=(bsz // BATCH_PER_STEP, n_chunk),
        in_specs=[pl.BlockSpec((BATCH_PER_STEP, CHUNK, RWKV_COLS), lambda b, s: (b, s, 0)),
                  pl.BlockSpec(masks.shape, lambda b, s: (0, 0, 0)),
                  pl.BlockSpec((1, RWKV_COLS), const),
                  vec, pl.BlockSpec((LANES, D_GRP), const),
                  vec, pl.BlockSpec((LANES, D_GRP), const),
                  pl.BlockSpec((LANES, D_GRP), const),
                  vec, vec, vec, vec, vec,
                  pl.BlockSpec((slab, wgu2d.shape[1]), step),
                  pl.BlockSpec((slab, wd2d.shape[1]), step)],
        out_specs=[pl.BlockSpec((BATCH_PER_STEP, CHUNK, D_GRP), lambda b, s: (b, s, 0)),
                   pl.BlockSpec((slab, wgu2d.shape[1]), step),
                   pl.BlockSpec((slab, wd2d.shape[1]), step)],
        out_shape=[jax.ShapeDtypeStruct((bsz, seq, D_GRP), BF16),
                   jax.ShapeDtypeStruct(wgu2d.shape, BF16),
                   jax.ShapeDtypeStruct(wd2d.shape, BF16)],
        scratch_shapes=[pltpu.VMEM((BATCH_PER_STEP, 1, RWKV_COLS), F32),
                        pltpu.VMEM((BATCH_PER_STEP, N_HEADS // HEADS_PER_SCAN, SCAN_W, SCAN_W), F32)],
        compiler_params=pltpu.CompilerParams(
            dimension_semantics=("parallel", "arbitrary"), vmem_limit_bytes=VMEM_LIMIT),
        name="rwkv",
    )(p_r, masks, mu, w0, w2p, a0, a2p, g2, k_k, k_a, r_k, gn_w, gn_b, wgu2d, wd2d)
    return y, wgu_bf.reshape(w_gate_up.shape), wd_bf.reshape(w_down.shape)


def _fox_kernel(q_ref, qb_ref, k_ref, kb_ref, vt_ref, og_ref, ong_ref, o_ref, m_ref, l_ref, acc_ref,
                *, seq):
    hp = pl.program_id(1)
    lane = _iota((1, LANES), 1)
    q = q_ref[...]
    qb = qb_ref[...]
    zero = jnp.zeros_like(q)
    qcat = [jnp.concatenate([jnp.where(lane // HEAD_DIM == hh, q, zero),
                             jnp.where(lane // 8 == hp * 2 + hh, qb, zero)], axis=1)
            for hh in range(2)]
    keys = min(FOX_SUB_KEYS, seq)
    n_sub = seq // keys
    diag = _iota((keys, keys), 1) >= _iota((keys, keys), 0)

    m_ref[...] = jnp.full(m_ref.shape, -jnp.inf, F32)
    l_ref[...] = jnp.zeros(l_ref.shape, F32)
    acc_ref[...] = jnp.zeros(acc_ref.shape, F32)

    def scores(s):
        lo = s * keys
        kcat = jnp.concatenate([k_ref[lo:lo + keys, :], kb_ref[lo:lo + keys, :]], axis=1)
        return [lax.dot_general(kcat, qc[lo:, :], _NT, preferred_element_type=F32)
                for qc in qcat]

    pending = scores(0)
    for s in range(n_sub):
        lo = s * keys
        nxt = scores(s + 1) if s + 1 < n_sub else None
        vt = vt_ref[:, lo:lo + keys]
        sts = [jnp.concatenate([jnp.where(diag, st[:, :keys], -jnp.inf), st[:, keys:]], axis=1)
               if st.shape[1] > keys else jnp.where(diag, st, -jnp.inf) for st in pending]
        m_old = [m_ref[hh, :, lo:] for hh in range(2)]
        m_new = [jnp.maximum(m, jnp.max(st, axis=0, keepdims=True)) for m, st in zip(m_old, sts)]
        pts = [jnp.exp2(st - m) for st, m in zip(sts, m_new)]
        pvs = [jnp.dot(vt, pt.astype(BF16), preferred_element_type=F32) for pt in pts]
        for hh in range(2):
            alpha = jnp.exp2(m_old[hh] - m_new[hh])
            m_ref[hh, :, lo:] = m_new[hh]
            l_ref[hh, :, lo:] = alpha * l_ref[hh, :, lo:] + jnp.sum(pts[hh], axis=0, keepdims=True)
            acc_ref[hh, :, lo:] = (alpha * acc_ref[hh, :, lo:]
                                   + pvs[hh][hh * HEAD_DIM:(hh + 1) * HEAD_DIM, :])
        pending = nxt

    outs = []
    for hh in range(2):
        o = acc_ref[hh] / l_ref[hh]
        outs.append(o * lax.rsqrt(jnp.mean(o * o, axis=0, keepdims=True) + NORM_EPS))
    o = jnp.concatenate(outs, axis=0).T
    o_ref[...] = (o * ong_ref[...] * _sigmoid(og_ref[...].astype(F32))).astype(BF16)


def _fox(p_x, k_bias, q_bias, o_gain):
    bsz, seq, _ = p_x.shape
    npair = N_HEADS // 2
    v_t = jnp.transpose(p_x[:, :, 2 * D_GRP:3 * D_GRP], (0, 2, 1))
    return pl.pallas_call(
        functools.partial(_fox_kernel, seq=seq),
        grid=(bsz, npair),
        in_specs=[pl.BlockSpec((None, seq, LANES), lambda b, h: (b, 0, h)),
                  pl.BlockSpec((None, seq, LANES), lambda b, h: (b, 0, 0)),
                  pl.BlockSpec((None, seq, LANES), lambda b, h: (b, 0, npair + h)),
                  pl.BlockSpec((None, seq, LANES), lambda b, h: (b, 0, 0)),
                  pl.BlockSpec((None, LANES, seq), lambda b, h: (b, h, 0)),
                  pl.BlockSpec((None, seq, LANES), lambda b, h: (b, 0, 3 * npair + h)),
                  pl.BlockSpec((1, LANES), lambda b, h: (0, 0))],
        out_specs=pl.BlockSpec((None, seq, LANES), lambda b, h: (b, 0, h)),
        out_shape=jax.ShapeDtypeStruct((bsz, seq, D_GRP), BF16),
        scratch_shapes=[pltpu.VMEM((2, 1, seq), F32), pltpu.VMEM((2, 1, seq), F32),
                        pltpu.VMEM((2, HEAD_DIM, seq), F32)],
        compiler_params=pltpu.CompilerParams(
            dimension_semantics=("parallel", "parallel"), vmem_limit_bytes=VMEM_LIMIT),
        name="fox",
    )(p_x, q_bias, p_x, k_bias, v_t, p_x, o_gain)


def _outproj_kernel(x_ref, yr_ref, yf_ref, g1_ref, sh_ref, sc_ref, ng_ref, wor_ref, wof_ref,
                    wrt_ref, wrl_ref, brt_ref, x1_ref, h2_ref, idx_ref, gate_ref, rank_ref, cnt_ref,
                    carry_ref):
    @pl.when(pl.program_id(0) == 0)
    def _():
        carry_ref[...] = jnp.zeros_like(carry_ref)

    y = (jnp.dot(yr_ref[...], wor_ref[...], preferred_element_type=F32)
         + jnp.dot(yf_ref[...], wof_ref[...], preferred_element_type=F32))
    x1 = x_ref[...] + g1_ref[...] * y
    x1_ref[...] = x1
    tm = x1.shape[0]
    h = x1 * lax.rsqrt(jnp.mean(x1 * x1, axis=-1, keepdims=True) + NORM_EPS) * ng_ref[...]
    h2 = h * (1.0 + sc_ref[...]) + sh_ref[...]
    h2_ref[...] = _pack_rows(h2)

    h_hi = h2.astype(BF16)
    h_lo = (h2 - h_hi.astype(F32)).astype(BF16)
    logits = (lax.dot_general(wrt_ref[...], h_hi, _NT, preferred_element_type=F32)
              + lax.dot_general(wrt_ref[...], h_lo, _NT, preferred_element_type=F32)
              + lax.dot_general(wrl_ref[...], h_hi, _NT, preferred_element_type=F32))
    lg = logits[:N_EXPERTS, :] + brt_ref[...]
    expert = _iota((N_EXPERTS, tm), 0)
    picks = []
    hot_sum = jnp.zeros((N_EXPERTS, tm), F32)
    for _ in range(TOP_K):
        m = jnp.max(lg, axis=0, keepdims=True)
        sel = jnp.min(jnp.where(lg == m, expert, N_EXPERTS), axis=0, keepdims=True)
        hot = expert == sel
        picks.append((m, sel, hot))
        hot_sum = hot_sum + hot.astype(F32)
        lg = jnp.where(hot, -jnp.inf, lg)
    es = [jnp.exp(m - picks[0][0]) for m, _, _ in picks]
    den = es[0] + es[1] + es[2] + es[3]

    earlier = (_iota((tm, tm), 0) < _iota((tm, tm), 1)).astype(BF16)
    before = jnp.dot(hot_sum.astype(BF16), earlier, preferred_element_type=F32) + carry_ref[...]
    ranks = [jnp.sum(jnp.where(hot, before, 0.0), axis=0, keepdims=True).astype(jnp.int32)
             for _, _, hot in picks]
    pad_i = jnp.zeros((8 - TOP_K, tm), jnp.int32)
    idx_ref[...] = jnp.concatenate([sel for _, sel, _ in picks] + [pad_i], axis=0)
    gate_ref[...] = jnp.concatenate([e / den for e in es] + [pad_i.astype(F32)], axis=0)
    rank_ref[...] = jnp.concatenate(ranks + [pad_i], axis=0)
    carry_ref[...] = carry_ref[...] + jnp.sum(hot_sum, axis=1, keepdims=True)
    cnt_ref[...] = jnp.broadcast_to(carry_ref[...], cnt_ref.shape)


def _outproj(x2d, y_r, y_f, gate1, shift2, scale2, norm_g, wo_r, wo_f, w_rt, b_rt, tm, seq,
             row0, t):
    w_rt_hi = w_rt.astype(BF16)
    w_rt_lo = (w_rt - w_rt_hi.astype(F32)).astype(BF16)
    per_b = seq // tm
    blk0 = row0 // tm
    const = lambda i: (0, 0)
    rows = lambda i: (i, 0)
    rows_in = lambda i: (i + blk0, 0)
    mod = pl.BlockSpec((None, 1, D_MODEL), lambda i: ((i + blk0) // per_b, 0, 0))
    return pl.pallas_call(
        _outproj_kernel,
        grid=(t // tm,),
        in_specs=[pl.BlockSpec((tm, D_MODEL), rows_in),
                  pl.BlockSpec((tm, D_GRP), rows_in),
                  pl.BlockSpec((tm, D_GRP), rows_in),
                  mod, mod, mod,
                  pl.BlockSpec((1, D_MODEL), const),
                  pl.BlockSpec((D_GRP, D_MODEL), const),
                  pl.BlockSpec((D_GRP, D_MODEL), const),
                  pl.BlockSpec((LANES, D_MODEL), const),
                  pl.BlockSpec((LANES, D_MODEL), const),
                  pl.BlockSpec((N_EXPERTS, 1), const)],
        out_specs=[pl.BlockSpec((tm, D_MODEL), rows),
                   pl.BlockSpec((tm, D_PACK), rows),
                   pl.BlockSpec((8, tm), lambda i: (0, i)),
                   pl.BlockSpec((8, tm), lambda i: (0, i)),
                   pl.BlockSpec((8, tm), lambda i: (0, i)),
                   pl.BlockSpec((N_EXPERTS, LANES), const)],
        out_shape=[jax.ShapeDtypeStruct((t, D_MODEL), F32),
                   jax.ShapeDtypeStruct((t, D_PACK), jnp.uint32),
                   jax.ShapeDtypeStruct((8, t), jnp.int32),
                   jax.ShapeDtypeStruct((8, t), F32),
                   jax.ShapeDtypeStruct((8, t), jnp.int32),
                   jax.ShapeDtypeStruct((N_EXPERTS, LANES), F32)],
        scratch_shapes=[pltpu.VMEM((N_EXPERTS, 1), F32)],
        compiler_params=pltpu.CompilerParams(
            dimension_semantics=("arbitrary",), vmem_limit_bytes=VMEM_LIMIT),
        name="outproj",
    )(x2d, y_r, y_f, gate1, shift2, scale2, norm_g, wo_r, wo_f, w_rt_hi, w_rt_lo, b_rt)


SC_CORES = 2
SC_SUBCORES = 16
SC_ROWS = 64


def _sc_gather_rows(idx, src):
    n_workers = SC_CORES * SC_SUBCORES
    m = idx.shape[0]
    d = src.shape[1]
    assert m % (n_workers * SC_ROWS) == 0
    n_chunks = m // (n_workers * SC_ROWS)
    mesh = plsc.VectorSubcoreMesh(core_axis_name="c", subcore_axis_name="s")

    @functools.partial(
        pl.kernel, mesh=mesh,
        out_type=jax.ShapeDtypeStruct((m, d), src.dtype),
        scratch_types=[pltpu.VMEM((n_chunks, SC_ROWS), jnp.int32),
                       pltpu.VMEM((SC_ROWS, d), src.dtype),
                       pltpu.SemaphoreType.DMA],
        name="sc_gather")
    def gather(src_hbm, idx_hbm, out_hbm, idx_v, rows_v, sem):
        wid = lax.axis_index("s") * SC_CORES + lax.axis_index("c")
        pltpu.sync_copy(idx_hbm.at[wid], idx_v)

        @pl.loop(0, n_chunks)
        def _(j):
            pltpu.async_copy(src_hbm.at[idx_v.at[j]], rows_v, sem).wait()
            pltpu.sync_copy(rows_v, out_hbm.at[pl.ds((wid * n_chunks + j) * SC_ROWS, SC_ROWS)])

    return gather(src, idx.reshape(n_workers, n_chunks, SC_ROWS))


def _sc_scatter_rows(src, dest, n_out):
    n_workers = SC_CORES * SC_SUBCORES
    t, d = src.shape
    n_slot = dest.shape[0]
    assert t % (n_workers * SC_ROWS) == 0
    n_chunks = t // (n_workers * SC_ROWS)
    mesh = plsc.VectorSubcoreMesh(core_axis_name="c", subcore_axis_name="s")
    idx = dest.reshape(n_slot, n_workers, n_chunks, SC_ROWS).transpose(1, 2, 0, 3)
    idx = idx.reshape(n_workers, n_chunks * n_slot, SC_ROWS)

    @functools.partial(
        pl.kernel, mesh=mesh,
        out_type=jax.ShapeDtypeStruct((n_out, d), src.dtype),
        scratch_types=[pltpu.VMEM((n_chunks * n_slot, SC_ROWS), jnp.int32),
                       pltpu.VMEM((SC_ROWS, d), src.dtype)],
        name="sc_scatter")
    def scatter(src_hbm, idx_hbm, out_hbm, idx_v, rows_v):
        wid = lax.axis_index("s") * SC_CORES + lax.axis_index("c")
        pltpu.sync_copy(idx_hbm.at[wid], idx_v)

        @pl.loop(0, n_chunks)
        def _(j):
            pltpu.sync_copy(src_hbm.at[pl.ds((wid * n_chunks + j) * SC_ROWS, SC_ROWS)], rows_v)
            for k in range(n_slot):
                pltpu.sync_copy(rows_v, out_hbm.at[idx_v.at[j * n_slot + k]])

    return scatter(src, idx)


def _expert_kernel(be_ref, nv_ref, x_ref, wgu_ref, bgu_ref, wd_ref, bd_ref, o_ref):
    del be_ref
    valid = _iota((EXPERT_BLOCK, 1), 0) < nv_ref[pl.program_id(0)]
    lo, hi = _unpack_rows(jnp.where(valid, x_ref[...], jnp.uint32(0)))
    x = jnp.concatenate([lo.astype(BF16), hi.astype(BF16)], axis=1)
    half = EXPERT_BLOCK // 2
    gus = [jnp.dot(x[r * half:(r + 1) * half], wgu_ref[...], preferred_element_type=F32)
           + bgu_ref[...] for r in range(2)]
    for r, gu in enumerate(gus):
        gate = jnp.minimum(gu[:, :D_MODEL], SWIGLU_LIMIT)
        up = jnp.clip(gu[:, D_MODEL:], -SWIGLU_LIMIT, SWIGLU_LIMIT)
        act = gate * _sigmoid(SWIGLU_ALPHA * gate) * (up + 1.0)
        o_ref[r * half:(r + 1) * half, :] = _pack_rows(
            jnp.dot(act.astype(BF16), wd_ref[...], preferred_element_type=F32) + bd_ref[...])


def _experts(block_e, n_valid, xs, w_gu, b_gu, w_d, b_d):
    n_blocks = block_e.shape[0]
    grid_spec = pltpu.PrefetchScalarGridSpec(
        num_scalar_prefetch=2,
        grid=(n_blocks,),
        in_specs=[pl.BlockSpec((EXPERT_BLOCK, D_PACK), lambda j, be, nv: (j, 0)),
                  pl.BlockSpec((None, D_MODEL, 2 * D_MODEL), lambda j, be, nv: (be[j], 0, 0)),
                  pl.BlockSpec((None, 1, 2 * D_MODEL), lambda j, be, nv: (be[j], 0, 0)),
                  pl.BlockSpec((None, D_MODEL, D_MODEL), lambda j, be, nv: (be[j], 0, 0)),
                  pl.BlockSpec((None, 1, D_MODEL), lambda j, be, nv: (be[j], 0, 0))],
        out_specs=pl.BlockSpec((EXPERT_BLOCK, D_PACK), lambda j, be, nv: (j, 0)),
    )
    return pl.pallas_call(
        _expert_kernel,
        grid_spec=grid_spec,
        out_shape=jax.ShapeDtypeStruct(xs.shape, jnp.uint32),
        compiler_params=pltpu.CompilerParams(
            dimension_semantics=("arbitrary",), vmem_limit_bytes=VMEM_LIMIT),
        name="experts",
    )(block_e, n_valid, xs, w_gu, b_gu, w_d, b_d)


COMBINE_TOKENS = 512
MOE_SPLITS = 2


def _combine_kernel(yg_ref, x1_ref, gate_ref, g2_ref, fg_ref, o_ref):
    gates = gate_ref[...].T
    acc_lo = acc_hi = None
    for kk in range(TOP_K):
        lo, hi = _unpack_rows(yg_ref[kk * COMBINE_TOKENS:(kk + 1) * COMBINE_TOKENS, :])
        g = gates[:, kk:kk + 1]
        acc_lo = g * lo if acc_lo is None else acc_lo + g * lo
        acc_hi = g * hi if acc_hi is None else acc_hi + g * hi
    x2 = x1_ref[...] + g2_ref[...] * jnp.concatenate([acc_lo, acc_hi], axis=1)
    o_ref[...] = x2 * lax.rsqrt(jnp.mean(x2 * x2, axis=-1, keepdims=True) + NORM_EPS) * fg_ref[...]


def _combine_kernel_into(prev_ref, *refs):
    del prev_ref
    _combine_kernel(*refs)


def _combine(yg, x1, gates, gate2, final_g, seq, row0, t_total, prev):
    t = x1.shape[0]
    tm = COMBINE_TOKENS
    per_b = seq // tm
    blk0 = row0 // tm
    rows = lambda i: (i, 0)
    in_specs = [pl.BlockSpec((TOP_K * tm, D_PACK), rows),
                pl.BlockSpec((tm, D_MODEL), rows),
                pl.BlockSpec((8, tm), lambda i: (0, i)),
                pl.BlockSpec((None, 1, D_MODEL), lambda i: ((i + blk0) // per_b, 0, 0)),
                pl.BlockSpec((1, D_MODEL), lambda i: (0, 0))]
    args = (yg, x1, gates, gate2, final_g)
    if prev is not None:
        in_specs = [pl.BlockSpec(memory_space=pl.ANY)] + in_specs
        args = (prev,) + args
    return pl.pallas_call(
        _combine_kernel if prev is None else _combine_kernel_into,
        grid=(t // tm,),
        in_specs=in_specs,
        out_specs=pl.BlockSpec((tm, D_MODEL), lambda i: (i + blk0, 0)),
        out_shape=jax.ShapeDtypeStruct((t_total, D_MODEL), F32),
        input_output_aliases={} if prev is None else {0: 0},
        compiler_params=pltpu.CompilerParams(
            dimension_semantics=("parallel",), vmem_limit_bytes=VMEM_LIMIT),
        name="combine",
    )(*args)


def _moe(h2, idx, gates, rank, counts, x1, gate2, final_g, w_gu, b_gu, w_d, b_d, seq,
         row0, t_total, prev):
    t = h2.shape[0]
    n_slots = t * TOP_K
    n_blocks = -(-n_slots // EXPERT_BLOCK) + N_EXPERTS
    cap = n_blocks * EXPERT_BLOCK
    padded = (counts + EXPERT_BLOCK - 1) // EXPERT_BLOCK * EXPERT_BLOCK
    pad_ends = jnp.cumsum(padded)
    pad_starts = pad_ends - padded
    experts = jnp.arange(N_EXPERTS, dtype=jnp.int32)
    dest = jnp.sum(jnp.where(idx[..., None] == experts, pad_starts, 0), axis=-1) + rank
    block_starts = jnp.arange(n_blocks, dtype=jnp.int32) * EXPERT_BLOCK
    block_e = jnp.minimum(jnp.sum(block_starts[:, None] >= pad_ends[None, :], axis=1),
                          N_EXPERTS - 1).astype(jnp.int32)
    n_valid = jnp.clip(counts[block_e] - (block_starts - pad_starts[block_e]), 0, EXPERT_BLOCK)

    xs = _sc_scatter_rows(h2, dest, cap)
    yb = _experts(block_e, n_valid.astype(jnp.int32), xs, w_gu, b_gu, w_d, b_d)
    dest_blocks = dest.reshape(TOP_K, -1, COMBINE_TOKENS).transpose(1, 0, 2).reshape(-1)
    yg = _sc_gather_rows(dest_blocks, yb)
    return _combine(yg, x1, gates, gate2, final_g, seq, row0, t_total, prev)


def _layer(x, c_mod, norm1_g, w_in, mu_shift, w0, w2, a0, a2, g2, k_k, k_a, r_k, gn_w, gn_b, b_f,
           q_norm_g, k_norm_g, o_norm_g, w_out, norm2_g, w_router, b_router, w_gate_up,
           b_gate_up, w_down, b_down, final_g, tm_in, tm_out):
    bsz, seq, _ = x.shape
    shift1, scale1, gate1, shift2, scale2, gate2 = (
        m.reshape(bsz, 1, D_MODEL) for m in jnp.split(c_mod, 6, axis=-1))
    row = lambda v: v.reshape(1, -1)

    w_r = w_in[:, :RWKV_COLS].astype(BF16)
    w_x = w_in[:, RWKV_COLS:RWKV_COLS + FOX_MAIN].astype(BF16)
    w_f = w_in[:, RWKV_COLS + FOX_MAIN:].T
    b_fp = jnp.pad(b_f, (0, LANES - N_HEADS)).reshape(1, LANES)
    qk_gain = jnp.concatenate([jnp.tile(q_norm_g, N_HEADS) * (HEAD_DIM ** -0.5 * LOG2_E),
                               jnp.tile(k_norm_g, N_HEADS)]).reshape(1, -1)
    p_r, p_x, k_bias, q_bias = _inproj(x, shift1, scale1, row(norm1_g), w_r, w_x, w_f, b_fp,
                                       qk_gain, tm_in)

    zeros = jnp.zeros((LANES - 64, D_GRP), F32)
    w2p = jnp.concatenate([w2, zeros], axis=0).astype(BF16)
    a2p = jnp.concatenate([zeros, a2], axis=0).astype(BF16)
    y_r, w_gu, w_d = _rwkv(p_r, row(mu_shift), row(w0), w2p, row(a0), a2p, g2.astype(BF16),
                           row(k_k), row(k_a), row(r_k), row(gn_w), row(gn_b), w_gate_up, w_down)

    y_f = _fox(p_x, k_bias, q_bias, jnp.tile(o_norm_g, 2).reshape(1, LANES))

    t = bsz * seq
    w_rt = jnp.pad(w_router.T, ((0, LANES - N_EXPERTS), (0, 0)))
    b_rt = b_router.reshape(N_EXPERTS, 1)
    wo = w_out.astype(BF16)
    b_gu, b_d = b_gate_up.reshape(N_EXPERTS, 1, -1), b_down.reshape(N_EXPERTS, 1, -1)
    t_part = t // MOE_SPLITS
    out = None
    for part in range(MOE_SPLITS):
        row0 = part * t_part
        x1, h2, idx, gates, rank, cnt = _outproj(
            x.reshape(t, D_MODEL), y_r.reshape(t, D_GRP), y_f.reshape(t, D_GRP), gate1, shift2,
            scale2, row(norm2_g), wo[:D_GRP], wo[D_GRP:], w_rt, b_rt, tm_out, seq, row0, t_part)
        counts = cnt[:, 0].astype(jnp.int32)
        out = _moe(h2, idx[:TOP_K], gates, rank[:TOP_K], counts, x1, gate2, row(final_g),
                   w_gu, b_gu, w_d, b_d, seq, row0, t, out)
    return out.reshape(bsz, seq, D_MODEL)


def kernel(x, c, w_ada, b_ada, norm1_g, w_in, mu_shift, w0, w2, a0, a2, g2, k_k, k_a, r_k, gn_w,
           gn_b, b_f, q_norm_g, k_norm_g, o_norm_g, w_out, norm2_g, w_router, b_router, w_gate_up,
           b_gate_up, w_down, b_down, final_g):
    assert w_ada.shape[0] == 1, "single-layer block"
    c_mod = _adaln(c, w_ada[0], b_ada[0])
    return _layer(x, c_mod, norm1_g[0], w_in[0], mu_shift[0], w0[0], w2[0], a0[0], a2[0], g2[0],
                  k_k[0], k_a[0], r_k[0], gn_w[0], gn_b[0], b_f[0], q_norm_g[0], k_norm_g[0],
                  o_norm_g[0], w_out[0], norm2_g[0], w_router[0], b_router[0], w_gate_up[0],
                  b_gate_up[0], w_down[0], b_down[0], final_g,
                  tm_in=min(512, x.shape[1]), tm_out=min(1024, x.shape[1]))
```

```python
import functools

import jax
import jax.numpy as jnp
from jax import lax
from jax.experimental import pallas as pl
from jax.experimental.pallas import tpu as pltpu
from jax.experimental.pallas import tpu_sc as plsc

F32 = jnp.float32
BF16 = jnp.bfloat16
HIGHEST = lax.Precision.HIGHEST

D_MODEL = 1024
HEAD_DIM = 64
N_HEADS = 8
D_GRP = N_HEADS * HEAD_DIM
RWKV_COLS = 1792
LORA_OFF = 3 * D_GRP
GATE_OFF = LORA_OFF + 128
FOX_MAIN = 4 * D_GRP
N_EXPERTS = 32
TOP_K = 4
EXPERT_BLOCK = 512
SWIGLU_ALPHA = 1.702
SWIGLU_LIMIT = 7.0
NORM_EPS = 1e-6
GN_EPS = 64e-5
LOG2_E = 1.4426950408889634
LANES = 128
CHUNK = 64
FOX_SUB_KEYS = 512
HEADS_PER_SCAN = 4
SCAN_W = HEADS_PER_SCAN * HEAD_DIM
SEG_TERMS = 1
VMEM_LIMIT = 56 * 1024 * 1024


def _dot(a, b):
    return jnp.dot(a.astype(BF16), b.astype(BF16), preferred_element_type=F32)


def _fdot(a, b):
    return jnp.dot(a, b, precision=HIGHEST, preferred_element_type=F32)


def _split_dot(x, m, terms):
    acc = None
    rem = x
    for _ in range(terms):
        part = rem.astype(BF16)
        rem = rem - part.astype(F32)
        d = jnp.dot(part, m, preferred_element_type=F32)
        acc = d if acc is None else acc + d
    return acc


def _iota(shape, dim):
    return lax.broadcasted_iota(jnp.int32, shape, dim)


def _seg_reduce_mat(n):
    return (_iota((n, LANES), 0) // HEAD_DIM == _iota((n, LANES), 1)).astype(BF16)


def _seg_expand_mat(n):
    return (_iota((LANES, n), 1) // HEAD_DIM == _iota((LANES, n), 0)).astype(BF16)


D_PACK = D_MODEL // 2


def _pack_rows(x):
    lo = lax.bitcast_convert_type(x[:, :D_PACK].astype(BF16).astype(F32), jnp.uint32)
    hi = lax.bitcast_convert_type(x[:, D_PACK:].astype(BF16).astype(F32), jnp.uint32)
    return hi | (lo >> 16)


def _unpack_rows(p):
    lo = lax.bitcast_convert_type(p << 16, F32)
    hi = lax.bitcast_convert_type(p & jnp.uint32(0xFFFF0000), F32)
    return lo, hi


def _log_sigmoid(z):
    return jnp.minimum(z, 0.0) - jnp.log(1.0 + jnp.exp(-jnp.abs(z)))


def _sigmoid(z):
    return 1.0 / (1.0 + jnp.exp(-z))


def _adaln_kernel(c_ref, w_ref, b_ref, o_ref):
    c = c_ref[...]
    o_ref[...] = _fdot(c * _sigmoid(c), w_ref[...]) + b_ref[...]


def _adaln(c, w_ada, b_ada):
    bsz = c.shape[0]
    n_mod = w_ada.shape[1] // D_MODEL
    return pl.pallas_call(
        _adaln_kernel,
        grid=(n_mod,),
        in_specs=[pl.BlockSpec((bsz, D_MODEL), lambda j: (0, 0)),
                  pl.BlockSpec((D_MODEL, D_MODEL), lambda j: (0, j)),
                  pl.BlockSpec((1, D_MODEL), lambda j: (0, j))],
        out_specs=pl.BlockSpec((bsz, D_MODEL), lambda j: (0, j)),
        out_shape=jax.ShapeDtypeStruct((bsz, n_mod * D_MODEL), F32),
        name="adaln",
    )(c, w_ada, b_ada.reshape(1, -1))


def _inproj_kernel(x_ref, sh_ref, sc_ref, g_ref, wr_ref, wx_ref, wft_ref, bf_ref, qkg_ref,
                   pr_ref, px_ref, kb_ref, qb_ref, carry_ref):
    @pl.when(pl.program_id(1) == 0)
    def _():
        carry_ref[...] = jnp.zeros_like(carry_ref)

    x = x_ref[...]
    tm = x.shape[0]
    h = x * lax.rsqrt(jnp.mean(x * x, axis=-1, keepdims=True) + NORM_EPS) * g_ref[...]
    h = h * (1.0 + sc_ref[...]) + sh_ref[...]
    hb = h.astype(BF16)

    pr_ref[...] = jnp.dot(hb, wr_ref[...], preferred_element_type=F32).astype(BF16)

    px = jnp.dot(hb, wx_ref[...], preferred_element_type=F32)
    qk = px[:, :2 * D_GRP]
    ss = _split_dot(qk * qk, _seg_reduce_mat(2 * D_GRP), SEG_TERMS)
    inv = lax.rsqrt(ss * (1.0 / HEAD_DIM) + NORM_EPS)
    qk = qk * _split_dot(inv, _seg_expand_mat(2 * D_GRP), SEG_TERMS) * qkg_ref[...]
    px_ref[:, :2 * D_GRP] = qk.astype(BF16)
    px_ref[:, 2 * D_GRP:] = px[:, 2 * D_GRP:].astype(BF16)

    lane = _iota((1, LANES), 1)
    z = jnp.zeros((tm, LANES), F32)
    for hd in range(N_HEADS):
        zh = jnp.sum(h * wft_ref[hd:hd + 1, :], axis=-1, keepdims=True)
        z = jnp.where(lane == hd, zh, z)
    cum = _log_sigmoid(z + bf_ref[...])
    row_id = _iota((tm, 1), 0)
    shift = 1
    while shift < tm:
        cum = cum + jnp.where(row_id >= shift, pltpu.roll(cum, shift, axis=0), 0.0)
        shift *= 2
    cum = cum + carry_ref[...]
    carry_ref[...] = cum[tm - 1:tm, :]

    parts = []
    rem = cum * LOG2_E
    for _ in range(3):
        part = rem.astype(BF16)
        rem = rem - part.astype(F32)
        parts.append(part)
    src, dst = _iota((LANES, LANES), 0), _iota((LANES, LANES), 1)

    def spread(offset):
        return sum(jnp.dot(part, ((dst == 8 * src + offset + t) & (src < N_HEADS)).astype(BF16),
                           preferred_element_type=F32) for t, part in enumerate(parts))

    slot = _iota((1, LANES), 1) % 8
    kb_ref[...] = (jnp.where((slot >= 3) & (slot < 6), 1.0, 0.0) - spread(0)).astype(BF16)
    qb_ref[...] = (jnp.where(slot < 3, 1.0, 0.0) + spread(3)).astype(BF16)


def _inproj(x, shift, scale, g, w_r, w_x, w_f_t, b_f, qk_gain, tm):
    bsz, seq, _ = x.shape
    const = lambda b, s: (0, 0)
    return pl.pallas_call(
        _inproj_kernel,
        grid=(bsz, seq // tm),
        in_specs=[pl.BlockSpec((None, tm, D_MODEL), lambda b, s: (b, s, 0)),
                  pl.BlockSpec((None, 1, D_MODEL), lambda b, s: (b, 0, 0)),
                  pl.BlockSpec((None, 1, D_MODEL), lambda b, s: (b, 0, 0)),
                  pl.BlockSpec((1, D_MODEL), const),
                  pl.BlockSpec((D_MODEL, RWKV_COLS), const),
                  pl.BlockSpec((D_MODEL, FOX_MAIN), const),
                  pl.BlockSpec((N_HEADS, D_MODEL), const),
                  pl.BlockSpec((1, LANES), const),
                  pl.BlockSpec((1, 2 * D_GRP), const)],
        out_specs=[pl.BlockSpec((None, tm, RWKV_COLS), lambda b, s: (b, s, 0)),
                   pl.BlockSpec((None, tm, FOX_MAIN), lambda b, s: (b, s, 0)),
                   pl.BlockSpec((None, tm, LANES), lambda b, s: (b, s, 0)),
                   pl.BlockSpec((None, tm, LANES), lambda b, s: (b, s, 0))],
        out_shape=[jax.ShapeDtypeStruct((bsz, seq, RWKV_COLS), BF16),
                   jax.ShapeDtypeStruct((bsz, seq, FOX_MAIN), BF16),
                   jax.ShapeDtypeStruct((bsz, seq, LANES), BF16),
                   jax.ShapeDtypeStruct((bsz, seq, LANES), BF16)],
        scratch_shapes=[pltpu.VMEM((1, LANES), F32)],
        compiler_params=pltpu.CompilerParams(
            dimension_semantics=("parallel", "arbitrary"), vmem_limit_bytes=VMEM_LIMIT),
        name="inproj",
    )(x, shift, scale, g, w_r, w_x, w_f_t, b_f, qk_gain)


_NN = (((1,), (0,)), ((), ()))
_NT = (((1,), (1,)), ((), ()))
_TN = (((0,), (0,)), ((), ()))
SCAN_N = HEADS_PER_SCAN * CHUNK
BATCH_PER_STEP = 8
INV_LEVELS = 5
M_HEAD, M_STRICT, M_INCL, M_EYE, M_BASE, M_OFF = 0, 1, 2, 3, 4, 5


def _bdot(a, b, dims):
    return lax.dot_general(a, b, dims, preferred_element_type=F32)


def _scan_masks():
    rr, cc = _iota((SCAN_N, SCAN_W), 0), _iota((SCAN_N, SCAN_W), 1)
    ri, ci = _iota((SCAN_N, SCAN_N), 0), _iota((SCAN_N, SCAN_N), 1)
    same = ri // CHUNK == ci // CHUNK
    masks = [rr // CHUNK == cc // HEAD_DIM, same & (ri > ci), same & (ri >= ci), ri == ci,
             (ri // 2 == ci // 2) & (ri > ci)]
    blk = 2
    while blk < CHUNK:
        masks.append((ri // (2 * blk) == ci // (2 * blk)) & (ri // blk != ci // blk) & (ri > ci))
        blk *= 2
    return jnp.stack(masks).astype(BF16)


def _rwkv_kernel(p_ref, masks_ref, mu_ref, w0_ref, w2_ref, a0_ref, a2_ref, g2_ref, kk_ref, ka_ref,
                 rk_ref, gnw_ref, gnb_ref, wgu_ref, wd_ref, o_ref, wgu_bf_ref, wd_bf_ref,
                 last_ref, state_ref):
    wgu_bf_ref[...] = wgu_ref[...].astype(BF16)
    wd_bf_ref[...] = wd_ref[...].astype(BF16)

    @pl.when(pl.program_id(1) == 0)
    def _():
        last_ref[...] = jnp.zeros_like(last_ref)
        state_ref[...] = jnp.zeros_like(state_ref)

    mu, w0, w2, a0, a2, g2, k_k, k_a, r_k, gn_w, gn_b = (
        ref[...] for ref in (mu_ref, w0_ref, w2_ref, a0_ref, a2_ref, g2_ref, kk_ref, ka_ref,
                             rk_ref, gnw_ref, gnb_ref))
    rows = BATCH_PER_STEP * CHUNK
    p = p_ref[...].astype(F32).reshape(rows, RWKV_COLS)
    row_id = _iota((rows, 1), 0)
    prev = pltpu.roll(p, 1, axis=0)
    for bb in range(BATCH_PER_STEP):
        prev = jnp.where(row_id == bb * CHUNK, last_ref[bb], prev)
        last_ref[bb] = p[(bb + 1) * CHUNK - 1:(bb + 1) * CHUNK, :]
    pf = p + mu * (prev - p)
    r = pf[:, 0:D_GRP]
    k = pf[:, D_GRP:2 * D_GRP]
    v = pf[:, 2 * D_GRP:3 * D_GRP]
    lora = pf[:, LORA_OFF:GATE_OFF]
    gd = pf[:, GATE_OFF:RWKV_COLS]

    wlog = w0 + _dot(jnp.tanh(lora), w2)
    neg = -wlog
    softplus = jnp.maximum(neg, 0.0) + jnp.log(1.0 + jnp.exp(-jnp.abs(neg)))
    ld = -jnp.exp(-softplus - 0.5)
    a = _sigmoid(a0 + _dot(lora, a2))
    g = _dot(_sigmoid(gd), g2)

    red, exp_m = _seg_reduce_mat(D_GRP), _seg_expand_mat(D_GRP)
    kk = k * k_k
    n2 = _split_dot(kk * kk, red, SEG_TERMS)
    kk = kk * _split_dot(1.0 / jnp.maximum(jnp.sqrt(n2), 1e-12), exp_m, SEG_TERMS)
    k2 = k * (1.0 + (a - 1.0) * k_a)

    in_chunk = row_id % CHUNK
    cl = ld
    shift = 1
    while shift < CHUNK:
        cl = cl + jnp.where(in_chunk >= shift, pltpu.roll(cl, shift, axis=0), 0.0)
        shift *= 2
    cl_end = jnp.concatenate(
        [jnp.broadcast_to(cl[(bb + 1) * CHUNK - 1:(bb + 1) * CHUNK, :], (CHUNK, D_GRP))
         for bb in range(BATCH_PER_STEP)], axis=0)
    e_in = jnp.exp(cl)
    e_out = jnp.exp(-cl)
    e_rem = jnp.exp(cl_end - cl)
    p_end = jnp.exp(cl_end)
    kka = kk * a
    ops = [(-kk * jnp.exp(cl - ld)).astype(BF16), (kka * e_out).astype(BF16),
           (k2 * e_out).astype(BF16), (r * e_in).astype(BF16), v.astype(BF16),
           (kka * e_rem).astype(BF16), (k2 * e_rem).astype(BF16)]

    chains = [(bb, grp) for bb in range(BATCH_PER_STEP)
              for grp in range(N_HEADS // HEADS_PER_SCAN)]
    head_mask = masks_ref[M_HEAD]
    strict, incl = masks_ref[M_STRICT], masks_ref[M_INCL]

    def stacked(op, bb, grp):
        part = op[bb * CHUNK:(bb + 1) * CHUNK, grp * SCAN_W:(grp + 1) * SCAN_W]
        return jnp.concatenate([part] * HEADS_PER_SCAN, axis=0) * head_mask

    xs = [[stacked(op, bb, grp) for op in ops] for bb, grp in chains]
    st = [state_ref[bb, grp] for bb, grp in chains]
    sb = [s.astype(BF16) for s in st]
    nab = [_bdot(x[0], x[1], _NT).astype(BF16) for x in xs]
    aak = [_bdot(x[0], x[2], _NT).astype(BF16) * strict for x in xs]
    arb = [_bdot(x[3], x[1], _NT).astype(BF16) * incl for x in xs]
    ark = [_bdot(x[3], x[2], _NT).astype(BF16) * incl for x in xs]
    t_inv = [masks_ref[M_EYE] + n * masks_ref[M_BASE] for n in nab]
    for lvl in range(INV_LEVELS):
        half = [_bdot(t, n * masks_ref[M_OFF + lvl], _NN).astype(BF16) for t, n in zip(t_inv, nab)]
        t_inv = [t + _bdot(h, t, _NN).astype(BF16) for t, h in zip(t_inv, half)]
    rhs = [(_bdot(x[0], s, _NT) + _bdot(k, x[4], _NN)).astype(BF16)
           for x, s, k in zip(xs, sb, aak)]
    sa = [_bdot(t, h, _NN).astype(BF16) for t, h in zip(t_inv, rhs)]
    ys = [_bdot(x[3], s, _NT) + _bdot(b, u, _NN) + _bdot(k, x[4], _NN)
          for x, s, b, u, k in zip(xs, sb, arb, sa, ark)]
    for (bb, grp), x, s, u in zip(chains, xs, st, sa):
        decay = p_end[bb * CHUNK:bb * CHUNK + 1, grp * SCAN_W:(grp + 1) * SCAN_W]
        state_ref[bb, grp] = s * decay + _bdot(u, x[5], _TN) + _bdot(x[4], x[6], _TN)
    ys = [y[0:CHUNK] + y[CHUNK:2 * CHUNK] + y[2 * CHUNK:3 * CHUNK] + y[3 * CHUNK:4 * CHUNK]
          for y in ys]
    n_grp = N_HEADS // HEADS_PER_SCAN
    y = jnp.concatenate([jnp.concatenate(ys[bb * n_grp:(bb + 1) * n_grp], axis=1)
                         for bb in range(BATCH_PER_STEP)], axis=0)

    mean = _split_dot(_split_dot(y, red, SEG_TERMS) * (1.0 / HEAD_DIM), exp_m, SEG_TERMS)
    d = y - mean
    var = _split_dot(d * d, red, SEG_TERMS) * (1.0 / HEAD_DIM)
    yn = d * _split_dot(lax.rsqrt(var + GN_EPS), exp_m, SEG_TERMS) * gn_w + gn_b
    bonus = _split_dot(_split_dot(r * k2 * r_k, red, SEG_TERMS), exp_m, SEG_TERMS) * v
    o_ref[...] = ((yn + bonus) * g).astype(BF16).reshape(BATCH_PER_STEP, CHUNK, D_GRP)


def _rwkv(p_r, mu, w0, w2p, a0, a2p, g2, k_k, k_a, r_k, gn_w, gn_b, w_gate_up, w_down):
    bsz, seq, _ = p_r.shape
    assert bsz % BATCH_PER_STEP == 0
    n_chunk = seq // CHUNK
    n_step = (bsz // BATCH_PER_STEP) * n_chunk
    wgu2d = w_gate_up.reshape(-1, w_gate_up.shape[-1])
    wd2d = w_down.reshape(-1, w_down.shape[-1])
    assert wgu2d.shape[0] % (8 * n_step) == 0 and wd2d.shape[0] == wgu2d.shape[0]
    slab = wgu2d.shape[0] // n_step
    masks = _scan_masks()
    const = lambda b, s: (0, 0)
    step = lambda b, s: (b * n_chunk + s, 0)
    vec = pl.BlockSpec((1, D_GRP), const)
    y, wgu_bf, wd_bf = pl.pallas_call(
        _rwkv_kernel,
        grid=(bsz // BATCH_PER_STEP, n_chunk),
        in_specs=[pl.BlockSpec((BATCH_PER_STEP, CHUNK, RWKV_COLS), lambda b, s: (b, s, 0)),
                  pl.BlockSpec(masks.shape, lambda b, s: (0, 0, 0)),
                  pl.BlockSpec((1, RWKV_COLS), const),
                  vec, pl.BlockSpec((LANES, D_GRP), const),
                  vec, pl.BlockSpec((LANES, D_GRP), const),
                  pl.BlockSpec((LANES, D_GRP), const),
                  vec, vec, vec, vec, vec,
                  pl.BlockSpec((slab, wgu2d.shape[1]), step),
                  pl.BlockSpec((slab, wd2d.shape[1]), step)],
        out_specs=[pl.BlockSpec((BATCH_PER_STEP, CHUNK, D_GRP), lambda b, s: (b, s, 0)),
                   pl.BlockSpec((slab, wgu2d.shape[1]), step),
                   pl.BlockSpec((slab, wd2d.shape[1]), step)],
        out_shape=[jax.ShapeDtypeStruct((bsz, seq, D_GRP), BF16),
                   jax.ShapeDtypeStruct(wgu2d.shape, BF16),
                   jax.ShapeDtypeStruct(wd2d.shape, BF16)],
        scratch_shapes=[pltpu.VMEM((BATCH_PER_STEP, 1, RWKV_COLS), F32),
                        pltpu.VMEM((BATCH_PER_STEP, N_HEADS // HEADS_PER_SCAN, SCAN_W, SCAN_W), F32)],
        compiler_params=pltpu.CompilerParams(
            dimension_semantics=("parallel", "arbitrary"), vmem_limit_bytes=VMEM_LIMIT),
        name="rwkv",
    )(p_r, masks, mu, w0, w2p, a0, a2p, g2, k_k, k_a, r_k, gn_w, gn_b, wgu2d, wd2d)
    return y, wgu_bf.reshape(w_gate_up.shape), wd_bf.reshape(w_down.shape)


def _fox_kernel(q_ref, qb_ref, k_ref, kb_ref, vt_ref, og_ref, ong_ref, o_ref, m_ref, l_ref, acc_ref,
                *, seq):
    hp = pl.program_id(1)
    lane = _iota((1, LANES), 1)
    q = q_ref[...]
    qb = qb_ref[...]
    zero = jnp.zeros_like(q)
    qcat = [jnp.concatenate([jnp.where(lane // HEAD_DIM == hh, q, zero),
                             jnp.where(lane // 8 == hp * 2 + hh, qb, zero)], axis=1)
            for hh in range(2)]
    keys = min(FOX_SUB_KEYS, seq)
    n_sub = seq // keys
    diag = _iota((keys, keys), 1) >= _iota((keys, keys), 0)

    m_ref[...] = jnp.full(m_ref.shape, -jnp.inf, F32)
    l_ref[...] = jnp.zeros(l_ref.shape, F32)
    acc_ref[...] = jnp.zeros(acc_ref.shape, F32)

    def scores(s):
        lo = s * keys
        kcat = jnp.concatenate([k_ref[lo:lo + keys, :], kb_ref[lo:lo + keys, :]], axis=1)
        return [lax.dot_general(kcat, qc[lo:, :], _NT, preferred_element_type=F32)
                for qc in qcat]

    pending = scores(0)
    for s in range(n_sub):
        lo = s * keys
        nxt = scores(s + 1) if s + 1 < n_sub else None
        vt = vt_ref[:, lo:lo + keys]
        sts = [jnp.concatenate([jnp.where(diag, st[:, :keys], -jnp.inf), st[:, keys:]], axis=1)
               if st.shape[1] > keys else jnp.where(diag, st, -jnp.inf) for st in pending]
        m_old = [m_ref[hh, :, lo:] for hh in range(2)]
        m_new = [jnp.maximum(m, jnp.max(st, axis=0, keepdims=True)) for m, st in zip(m_old, sts)]
        pts = [jnp.exp2(st - m) for st, m in zip(sts, m_new)]
        pvs = [jnp.dot(vt, pt.astype(BF16), preferred_element_type=F32) for pt in pts]
        for hh in range(2):
            alpha = jnp.exp2(m_old[hh] - m_new[hh])
            m_ref[hh, :, lo:] = m_new[hh]
            l_ref[hh, :, lo:] = alpha * l_ref[hh, :, lo:] + jnp.sum(pts[hh], axis=0, keepdims=True)
            acc_ref[hh, :, lo:] = (alpha * acc_ref[hh, :, lo:]
                                   + pvs[hh][hh * HEAD_DIM:(hh + 1) * HEAD_DIM, :])
        pending = nxt

    outs = []
    for hh in range(2):
        o = acc_ref[hh] / l_ref[hh]
        outs.append(o * lax.rsqrt(jnp.mean(o * o, axis=0, keepdims=True) + NORM_EPS))
    o = jnp.concatenate(outs, axis=0).T
    o_ref[...] = (o * ong_ref[...] * _sigmoid(og_ref[...].astype(F32))).astype(BF16)


def _fox(p_x, k_bias, q_bias, o_gain):
    bsz, seq, _ = p_x.shape
    npair = N_HEADS // 2
    v_t = jnp.transpose(p_x[:, :, 2 * D_GRP:3 * D_GRP], (0, 2, 1))
    return pl.pallas_call(
        functools.partial(_fox_kernel, seq=seq),
        grid=(bsz, npair),
        in_specs=[pl.BlockSpec((None, seq, LANES), lambda b, h: (b, 0, h)),
                  pl.BlockSpec((None, seq, LANES), lambda b, h: (b, 0, 0)),
                  pl.BlockSpec((None, seq, LANES), lambda b, h: (b, 0, npair + h)),
                  pl.BlockSpec((None, seq, LANES), lambda b, h: (b, 0, 0)),
                  pl.BlockSpec((None, LANES, seq), lambda b, h: (b, h, 0)),
                  pl.BlockSpec((None, seq, LANES), lambda b, h: (b, 0, 3 * npair + h)),
                  pl.BlockSpec((1, LANES), lambda b, h: (0, 0))],
        out_specs=pl.BlockSpec((None, seq, LANES), lambda b, h: (b, 0, h)),
        out_shape=jax.ShapeDtypeStruct((bsz, seq, D_GRP), BF16),
        scratch_shapes=[pltpu.VMEM((2, 1, seq), F32), pltpu.VMEM((2, 1, seq), F32),
                        pltpu.VMEM((2, HEAD_DIM, seq), F32)],
        compiler_params=pltpu.CompilerParams(
            dimension_semantics=("parallel", "parallel"), vmem_limit_bytes=VMEM_LIMIT),
        name="fox",
    )(p_x, q_bias, p_x, k_bias, v_t, p_x, o_gain)


def _outproj_kernel(x_ref, yr_ref, yf_ref, g1_ref, sh_ref, sc_ref, ng_ref, wor_ref, wof_ref,
                    wrt_ref, wrl_ref, brt_ref, x1_ref, h2_ref, idx_ref, gate_ref, rank_ref, cnt_ref,
                    carry_ref):
    @pl.when(pl.program_id(0) == 0)
    def _():
        carry_ref[...] = jnp.zeros_like(carry_ref)

    y = (jnp.dot(yr_ref[...], wor_ref[...], preferred_element_type=F32)
         + jnp.dot(yf_ref[...], wof_ref[...], preferred_element_type=F32))
    x1 = x_ref[...] + g1_ref[...] * y
    x1_ref[...] = x1
    tm = x1.shape[0]
    h = x1 * lax.rsqrt(jnp.mean(x1 * x1, axis=-1, keepdims=True) + NORM_EPS) * ng_ref[...]
    h2 = h * (1.0 + sc_ref[...]) + sh_ref[...]
    h2_ref[...] = _pack_rows(h2)

    h_hi = h2.astype(BF16)
    h_lo = (h2 - h_hi.astype(F32)).astype(BF16)
    logits = (lax.dot_general(wrt_ref[...], h_hi, _NT, preferred_element_type=F32)
              + lax.dot_general(wrt_ref[...], h_lo, _NT, preferred_element_type=F32)
              + lax.dot_general(wrl_ref[...], h_hi, _NT, preferred_element_type=F32))
    lg = logits[:N_EXPERTS, :] + brt_ref[...]
    expert = _iota((N_EXPERTS, tm), 0)
    picks = []
    hot_sum = jnp.zeros((N_EXPERTS, tm), F32)
    for _ in range(TOP_K):
        m = jnp.max(lg, axis=0, keepdims=True)
        sel = jnp.min(jnp.where(lg == m, expert, N_EXPERTS), axis=0, keepdims=True)
        hot = expert == sel
        picks.append((m, sel, hot))
        hot_sum = hot_sum + hot.astype(F32)
        lg = jnp.where(hot, -jnp.inf, lg)
    es = [jnp.exp(m - picks[0][0]) for m, _, _ in picks]
    den = es[0] + es[1] + es[2] + es[3]

    earlier = (_iota((tm, tm), 0) < _iota((tm, tm), 1)).astype(BF16)
    before = jnp.dot(hot_sum.astype(BF16), earlier, preferred_element_type=F32) + carry_ref[...]
    ranks = [jnp.sum(jnp.where(hot, before, 0.0), axis=0, keepdims=True).astype(jnp.int32)
             for _, _, hot in picks]
    pad_i = jnp.zeros((8 - TOP_K, tm), jnp.int32)
    idx_ref[...] = jnp.concatenate([sel for _, sel, _ in picks] + [pad_i], axis=0)
    gate_ref[...] = jnp.concatenate([e / den for e in es] + [pad_i.astype(F32)], axis=0)
    rank_ref[...] = jnp.concatenate(ranks + [pad_i], axis=0)
    carry_ref[...] = carry_ref[...] + jnp.sum(hot_sum, axis=1, keepdims=True)
    cnt_ref[...] = jnp.broadcast_to(carry_ref[...], cnt_ref.shape)


def _outproj(x2d, y_r, y_f, gate1, shift2, scale2, norm_g, wo_r, wo_f, w_rt, b_rt, tm, seq,
             row0, t):
    w_rt_hi = w_rt.astype(BF16)
    w_rt_lo = (w_rt - w_rt_hi.astype(F32)).astype(BF16)
    per_b = seq // tm
    blk0 = row0 // tm
    const = lambda i: (0, 0)
    rows = lambda i: (i, 0)
    rows_in = lambda i: (i + blk0, 0)
    mod = pl.BlockSpec((None, 1, D_MODEL), lambda i: ((i + blk0) // per_b, 0, 0))
    return pl.pallas_call(
        _outproj_kernel,
        grid=(t // tm,),
        in_specs=[pl.BlockSpec((tm, D_MODEL), rows_in),
                  pl.BlockSpec((tm, D_GRP), rows_in),
                  pl.BlockSpec((tm, D_GRP), rows_in),
                  mod, mod, mod,
                  pl.BlockSpec((1, D_MODEL), const),
                  pl.BlockSpec((D_GRP, D_MODEL), const),
                  pl.BlockSpec((D_GRP, D_MODEL), const),
                  pl.BlockSpec((LANES, D_MODEL), const),
                  pl.BlockSpec((LANES, D_MODEL), const),
                  pl.BlockSpec((N_EXPERTS, 1), const)],
        out_specs=[pl.BlockSpec((tm, D_MODEL), rows),
                   pl.BlockSpec((tm, D_PACK), rows),
                   pl.BlockSpec((8, tm), lambda i: (0, i)),
                   pl.BlockSpec((8, tm), lambda i: (0, i)),
                   pl.BlockSpec((8, tm), lambda i: (0, i)),
                   pl.BlockSpec((N_EXPERTS, LANES), const)],
        out_shape=[jax.ShapeDtypeStruct((t, D_MODEL), F32),
                   jax.ShapeDtypeStruct((t, D_PACK), jnp.uint32),
                   jax.ShapeDtypeStruct((8, t), jnp.int32),
                   jax.ShapeDtypeStruct((8, t), F32),
                   jax.ShapeDtypeStruct((8, t), jnp.int32),
                   jax.ShapeDtypeStruct((N_EXPERTS, LANES), F32)],
        scratch_shapes=[pltpu.VMEM((N_EXPERTS, 1), F32)],
        compiler_params=pltpu.CompilerParams(
            dimension_semantics=("arbitrary",), vmem_limit_bytes=VMEM_LIMIT),
        name="outproj",
    )(x2d, y_r, y_f, gate1, shift2, scale2, norm_g, wo_r, wo_f, w_rt_hi, w_rt_lo, b_rt)


SC_CORES = 2
SC_SUBCORES = 16
SC_ROWS = 64


def _sc_gather_rows(idx, src):
    n_workers = SC_CORES * SC_SUBCORES
    m = idx.shape[0]
    d = src.shape[1]
    assert m % (n_workers * SC_ROWS) == 0
    n_chunks = m // (n_workers * SC_ROWS)
    mesh = plsc.VectorSubcoreMesh(core_axis_name="c", subcore_axis_name="s")

    @functools.partial(
        pl.kernel, mesh=mesh,
        out_type=jax.ShapeDtypeStruct((m, d), src.dtype),
        scratch_types=[pltpu.VMEM((n_chunks, SC_ROWS), jnp.int32),
                       pltpu.VMEM((SC_ROWS, d), src.dtype),
                       pltpu.SemaphoreType.DMA],
        name="sc_gather")
    def gather(src_hbm, idx_hbm, out_hbm, idx_v, rows_v, sem):
        wid = lax.axis_index("s") * SC_CORES + lax.axis_index("c")
        pltpu.sync_copy(idx_hbm.at[wid], idx_v)

        @pl.loop(0, n_chunks)
        def _(j):
            pltpu.async_copy(src_hbm.at[idx_v.at[j]], rows_v, sem).wait()
            pltpu.sync_copy(rows_v, out_hbm.at[pl.ds((wid * n_chunks + j) * SC_ROWS, SC_ROWS)])

    return gather(src, idx.reshape(n_workers, n_chunks, SC_ROWS))


def _sc_scatter_rows(src, dest, n_out):
    n_workers = SC_CORES * SC_SUBCORES
    t, d = src.shape
    n_slot = dest.shape[0]
    assert t % (n_workers * SC_ROWS) == 0
    n_chunks = t // (n_workers * SC_ROWS)
    mesh = plsc.VectorSubcoreMesh(core_axis_name="c", subcore_axis_name="s")
    idx = dest.reshape(n_slot, n_workers, n_chunks, SC_ROWS).transpose(1, 2, 0, 3)
    idx = idx.reshape(n_workers, n_chunks * n_slot, SC_ROWS)

    @functools.partial(
        pl.kernel, mesh=mesh,
        out_type=jax.ShapeDtypeStruct((n_out, d), src.dtype),
        scratch_types=[pltpu.VMEM((n_chunks * n_slot, SC_ROWS), jnp.int32),
                       pltpu.VMEM((SC_ROWS, d), src.dtype)],
        name="sc_scatter")
    def scatter(src_hbm, idx_hbm, out_hbm, idx_v, rows_v):
        wid = lax.axis_index("s") * SC_CORES + lax.axis_index("c")
        pltpu.sync_copy(idx_hbm.at[wid], idx_v)

        @pl.loop(0, n_chunks)
        def _(j):
            pltpu.sync_copy(src_hbm.at[pl.ds((wid * n_chunks + j) * SC_ROWS, SC_ROWS)], rows_v)
            for k in range(n_slot):
                pltpu.sync_copy(rows_v, out_hbm.at[idx_v.at[j * n_slot + k]])

    return scatter(src, idx)


def _expert_kernel(be_ref, nv_ref, x_ref, wgu_ref, bgu_ref, wd_ref, bd_ref, o_ref):
    del be_ref
    valid = _iota((EXPERT_BLOCK, 1), 0) < nv_ref[pl.program_id(0)]
    lo, hi = _unpack_rows(jnp.where(valid, x_ref[...], jnp.uint32(0)))
    x = jnp.concatenate([lo.astype(BF16), hi.astype(BF16)], axis=1)
    half = EXPERT_BLOCK // 2
    gus = [jnp.dot(x[r * half:(r + 1) * half], wgu_ref[...], preferred_element_type=F32)
           + bgu_ref[...] for r in range(2)]
    for r, gu in enumerate(gus):
        gate = jnp.minimum(gu[:, :D_MODEL], SWIGLU_LIMIT)
        up = jnp.clip(gu[:, D_MODEL:], -SWIGLU_LIMIT, SWIGLU_LIMIT)
        act = gate * _sigmoid(SWIGLU_ALPHA * gate) * (up + 1.0)
        o_ref[r * half:(r + 1) * half, :] = _pack_rows(
            jnp.dot(act.astype(BF16), wd_ref[...], preferred_element_type=F32) + bd_ref[...])


def _experts(block_e, n_valid, xs, w_gu, b_gu, w_d, b_d):
    n_blocks = block_e.shape[0]
    grid_spec = pltpu.PrefetchScalarGridSpec(
        num_scalar_prefetch=2,
        grid=(n_blocks,),
        in_specs=[pl.BlockSpec((EXPERT_BLOCK, D_PACK), lambda j, be, nv: (j, 0)),
                  pl.BlockSpec((None, D_MODEL, 2 * D_MODEL), lambda j, be, nv: (be[j], 0, 0)),
                  pl.BlockSpec((None, 1, 2 * D_MODEL), lambda j, be, nv: (be[j], 0, 0)),
                  pl.BlockSpec((None, D_MODEL, D_MODEL), lambda j, be, nv: (be[j], 0, 0)),
                  pl.BlockSpec((None, 1, D_MODEL), lambda j, be, nv: (be[j], 0, 0))],
        out_specs=pl.BlockSpec((EXPERT_BLOCK, D_PACK), lambda j, be, nv: (j, 0)),
    )
    return pl.pallas_call(
        _expert_kernel,
        grid_spec=grid_spec,
        out_shape=jax.ShapeDtypeStruct(xs.shape, jnp.uint32),
        compiler_params=pltpu.CompilerParams(
            dimension_semantics=("arbitrary",), vmem_limit_bytes=VMEM_LIMIT),
        name="experts",
    )(block_e, n_valid, xs, w_gu, b_gu, w_d, b_d)


COMBINE_TOKENS = 512
MOE_SPLITS = 2


def _combine_kernel(yg_ref, x1_ref, gate_ref, g2_ref, fg_ref, o_ref):
    gates = gate_ref[...].T
    acc_lo = acc_hi = None
    for kk in range(TOP_K):
        lo, hi = _unpack_rows(yg_ref[kk * COMBINE_TOKENS:(kk + 1) * COMBINE_TOKENS, :])
        g = gates[:, kk:kk + 1]
        acc_lo = g * lo if acc_lo is None else acc_lo + g * lo
        acc_hi = g * hi if acc_hi is None else acc_hi + g * hi
    x2 = x1_ref[...] + g2_ref[...] * jnp.concatenate([acc_lo, acc_hi], axis=1)
    o_ref[...] = x2 * lax.rsqrt(jnp.mean(x2 * x2, axis=-1, keepdims=True) + NORM_EPS) * fg_ref[...]


def _combine_kernel_into(prev_ref, *refs):
    del prev_ref
    _combine_kernel(*refs)


def _combine(yg, x1, gates, gate2, final_g, seq, row0, t_total, prev):
    t = x1.shape[0]
    tm = COMBINE_TOKENS
    per_b = seq // tm
    blk0 = row0 // tm
    rows = lambda i: (i, 0)
    in_specs = [pl.BlockSpec((TOP_K * tm, D_PACK), rows),
                pl.BlockSpec((tm, D_MODEL), rows),
                pl.BlockSpec((8, tm), lambda i: (0, i)),
                pl.BlockSpec((None, 1, D_MODEL), lambda i: ((i + blk0) // per_b, 0, 0)),
                pl.BlockSpec((1, D_MODEL), lambda i: (0, 0))]
    args = (yg, x1, gates, gate2, final_g)
    if prev is not None:
        in_specs = [pl.BlockSpec(memory_space=pl.ANY)] + in_specs
        args = (prev,) + args
    return pl.pallas_call(
        _combine_kernel if prev is None else _combine_kernel_into,
        grid=(t // tm,),
        in_specs=in_specs,
        out_specs=pl.BlockSpec((tm, D_MODEL), lambda i: (i + blk0, 0)),
        out_shape=jax.ShapeDtypeStruct((t_total, D_MODEL), F32),
        input_output_aliases={} if prev is None else {0: 0},
        compiler_params=pltpu.CompilerParams(
            dimension_semantics=("parallel",), vmem_limit_bytes=VMEM_LIMIT),
        name="combine",
    )(*args)


def _moe(h2, idx, gates, rank, counts, x1, gate2, final_g, w_gu, b_gu, w_d, b_d, seq,
         row0, t_total, prev):
    t = h2.shape[0]
    n_slots = t * TOP_K
    n_blocks = -(-n_slots // EXPERT_BLOCK) + N_EXPERTS
    cap = n_blocks * EXPERT_BLOCK
    padded = (counts + EXPERT_BLOCK - 1) // EXPERT_BLOCK * EXPERT_BLOCK
    pad_ends = jnp.cumsum(padded)
    pad_starts = pad_ends - padded
    experts = jnp.arange(N_EXPERTS, dtype=jnp.int32)
    dest = jnp.sum(jnp.where(idx[..., None] == experts, pad_starts, 0), axis=-1) + rank
    block_starts = jnp.arange(n_blocks, dtype=jnp.int32) * EXPERT_BLOCK
    block_e = jnp.minimum(jnp.sum(block_starts[:, None] >= pad_ends[None, :], axis=1),
                          N_EXPERTS - 1).astype(jnp.int32)
    n_valid = jnp.clip(counts[block_e] - (block_starts - pad_starts[block_e]), 0, EXPERT_BLOCK)

    xs = _sc_scatter_rows(h2, dest, cap)
    yb = _experts(block_e, n_valid.astype(jnp.int32), xs, w_gu, b_gu, w_d, b_d)
    dest_blocks = dest.reshape(TOP_K, -1, COMBINE_TOKENS).transpose(1, 0, 2).reshape(-1)
    yg = _sc_gather_rows(dest_blocks, yb)
    return _combine(yg, x1, gates, gate2, final_g, seq, row0, t_total, prev)


def _layer(x, c_mod, norm1_g, w_in, mu_shift, w0, w2, a0, a2, g2, k_k, k_a, r_k, gn_w, gn_b, b_f,
           q_norm_g, k_norm_g, o_norm_g, w_out, norm2_g, w_router, b_router, w_gate_up,
           b_gate_up, w_down, b_down, final_g, tm_in, tm_out):
    bsz, seq, _ = x.shape
    shift1, scale1, gate1, shift2, scale2, gate2 = (
        m.reshape(bsz, 1, D_MODEL) for m in jnp.split(c_mod, 6, axis=-1))
    row = lambda v: v.reshape(1, -1)

    w_r = w_in[:, :RWKV_COLS].astype(BF16)
    w_x = w_in[:, RWKV_COLS:RWKV_COLS + FOX_MAIN].astype(BF16)
    w_f = w_in[:, RWKV_COLS + FOX_MAIN:].T
    b_fp = jnp.pad(b_f, (0, LANES - N_HEADS)).reshape(1, LANES)
    qk_gain = jnp.concatenate([jnp.tile(q_norm_g, N_HEADS) * (HEAD_DIM ** -0.5 * LOG2_E),
                               jnp.tile(k_norm_g, N_HEADS)]).reshape(1, -1)
    p_r, p_x, k_bias, q_bias = _inproj(x, shift1, scale1, row(norm1_g), w_r, w_x, w_f, b_fp,
                                       qk_gain, tm_in)

    zeros = jnp.zeros((LANES - 64, D_GRP), F32)
    w2p = jnp.concatenate([w2, zeros], axis=0).astype(BF16)
    a2p = jnp.concatenate([zeros, a2], axis=0).astype(BF16)
    y_r, w_gu, w_d = _rwkv(p_r, row(mu_shift), row(w0), w2p, row(a0), a2p, g2.astype(BF16),
                           row(k_k), row(k_a), row(r_k), row(gn_w), row(gn_b), w_gate_up, w_down)

    y_f = _fox(p_x, k_bias, q_bias, jnp.tile(o_norm_g, 2).reshape(1, LANES))

    t = bsz * seq
    w_rt = jnp.pad(w_router.T, ((0, LANES - N_EXPERTS), (0, 0)))
    b_rt = b_router.reshape(N_EXPERTS, 1)
    wo = w_out.astype(BF16)
    b_gu, b_d = b_gate_up.reshape(N_EXPERTS, 1, -1), b_down.reshape(N_EXPERTS, 1, -1)
    t_part = t // MOE_SPLITS
    out = None
    for part in range(MOE_SPLITS):
        row0 = part * t_part
        x1, h2, idx, gates, rank, cnt = _outproj(
            x.reshape(t, D_MODEL), y_r.reshape(t, D_GRP), y_f.reshape(t, D_GRP), gate1, shift2,
            scale2, row(norm2_g), wo[:D_GRP], wo[D_GRP:], w_rt, b_rt, tm_out, seq, row0, t_part)
        counts = cnt[:, 0].astype(jnp.int32)
        out = _moe(h2, idx[:TOP_K], gates, rank[:TOP_K], counts, x1, gate2, row(final_g),
                   w_gu, b_gu, w_d, b_d, seq, row0, t, out)
    return out.reshape(bsz, seq, D_MODEL)


def kernel(x, c, w_ada, b_ada, norm1_g, w_in, mu_shift, w0, w2, a0, a2, g2, k_k, k_a, r_k, gn_w,
           gn_b, b_f, q_norm_g, k_norm_g, o_norm_g, w_out, norm2_g, w_router, b_router, w_gate_up,
           b_gate_up, w_down, b_down, final_g):
    assert w_ada.shape[0] == 1, "single-layer block"
    c_mod = _adaln(c, w_ada[0], b_ada[0])
    return _layer(x, c_mod, norm1_g[0], w_in[0], mu_shift[0], w0[0], w2[0], a0[0], a2[0], g2[0],
                  k_k[0], k_a[0], r_k[0], gn_w[0], gn_b[0], b_f[0], q_norm_g[0], k_norm_g[0],
                  o_norm_g[0], w_out[0], norm2_g[0], w_router[0], b_router[0], w_gate_up[0],
                  b_gate_up[0], w_down[0], b_down[0], final_g,
                  tm_in=min(512, x.shape[1]), tm_out=min(1024, x.shape[1]))
```

```python
import functools

import jax
import jax.numpy as jnp
from jax import lax
from jax.experimental import pallas as pl
from jax.experimental.pallas import tpu as pltpu
from jax.experimental.pallas import tpu_sc as plsc

F32 = jnp.float32
BF16 = jnp.bfloat16
HIGHEST = lax.Precision.HIGHEST

D_MODEL = 1024
HEAD_DIM = 64
N_HEADS = 8
D_GRP = N_HEADS * HEAD_DIM
RWKV_COLS = 1792
LORA_OFF = 3 * D_GRP
GATE_OFF = LORA_OFF + 128
FOX_MAIN = 4 * D_GRP
N_EXPERTS = 32
TOP_K = 4
EXPERT_BLOCK = 512
SWIGLU_ALPHA = 1.702
SWIGLU_LIMIT = 7.0
NORM_EPS = 1e-6
GN_EPS = 64e-5
LOG2_E = 1.4426950408889634
LANES = 128
CHUNK = 64
FOX_SUB_KEYS = 512
HEADS_PER_SCAN = 4
SCAN_W = HEADS_PER_SCAN * HEAD_DIM
SEG_TERMS = 1
CUM_TERMS = 2
VMEM_LIMIT = 56 * 1024 * 1024


def _dot(a, b):
    return jnp.dot(a.astype(BF16), b.astype(BF16), preferred_element_type=F32)


def _fdot(a, b):
    return jnp.dot(a, b, precision=HIGHEST, preferred_element_type=F32)


def _split_dot(x, m, terms):
    acc = None
    rem = x
    for _ in range(terms):
        part = rem.astype(BF16)
        rem = rem - part.astype(F32)
        d = jnp.dot(part, m, preferred_element_type=F32)
        acc = d if acc is None else acc + d
    return acc


def _tri_dot(m, x, terms):
    acc = None
    rem = x
    for _ in range(terms):
        part = rem.astype(BF16)
        rem = rem - part.astype(F32)
        d = jnp.dot(m, part, preferred_element_type=F32)
        acc = d if acc is None else acc + d
    return acc


def _iota(shape, dim):
    return lax.broadcasted_iota(jnp.int32, shape, dim)


def _seg_reduce_mat(n):
    return (_iota((n, LANES), 0) // HEAD_DIM == _iota((n, LANES), 1)).astype(BF16)


def _seg_expand_mat(n):
    return (_iota((LANES, n), 1) // HEAD_DIM == _iota((LANES, n), 0)).astype(BF16)


D_PACK = D_MODEL // 2


def _pack_rows(x):
    lo = lax.bitcast_convert_type(x[:, :D_PACK].astype(BF16).astype(F32), jnp.uint32)
    hi = lax.bitcast_convert_type(x[:, D_PACK:].astype(BF16).astype(F32), jnp.uint32)
    return hi | (lo >> 16)


def _unpack_rows(p):
    lo = lax.bitcast_convert_type(p << 16, F32)
    hi = lax.bitcast_convert_type(p & jnp.uint32(0xFFFF0000), F32)
    return lo, hi


def _log_sigmoid(z):
    return jnp.minimum(z, 0.0) - jnp.log(1.0 + jnp.exp(-jnp.abs(z)))


def _sigmoid(z):
    return 1.0 / (1.0 + jnp.exp(-z))


def _adaln_kernel(c_ref, w_ref, b_ref, o_ref):
    c = c_ref[...]
    o_ref[...] = _fdot(c * _sigmoid(c), w_ref[...]) + b_ref[...]


def _adaln(c, w_ada, b_ada):
    bsz = c.shape[0]
    n_mod = w_ada.shape[1] // D_MODEL
    return pl.pallas_call(
        _adaln_kernel,
        grid=(n_mod,),
        in_specs=[pl.BlockSpec((bsz, D_MODEL), lambda j: (0, 0)),
                  pl.BlockSpec((D_MODEL, D_MODEL), lambda j: (0, j)),
                  pl.BlockSpec((1, D_MODEL), lambda j: (0, j))],
        out_specs=pl.BlockSpec((bsz, D_MODEL), lambda j: (0, j)),
        out_shape=jax.ShapeDtypeStruct((bsz, n_mod * D_MODEL), F32),
        name="adaln",
    )(c, w_ada, b_ada.reshape(1, -1))


def _inproj_kernel(x_ref, sh_ref, sc_ref, g_ref, wr_ref, wx_ref, wft_ref, bf_ref, qkg_ref,
                   pr_ref, px_ref, kb_ref, qb_ref, carry_ref):
    @pl.when(pl.program_id(1) == 0)
    def _():
        carry_ref[...] = jnp.zeros_like(carry_ref)

    x = x_ref[...]
    tm = x.shape[0]
    h = x * lax.rsqrt(jnp.mean(x * x, axis=-1, keepdims=True) + NORM_EPS) * g_ref[...]
    h = h * (1.0 + sc_ref[...]) + sh_ref[...]
    hb = h.astype(BF16)

    pr_ref[...] = jnp.dot(hb, wr_ref[...], preferred_element_type=F32).astype(BF16)

    px = jnp.dot(hb, wx_ref[...], preferred_element_type=F32)
    qk = px[:, :2 * D_GRP]
    ss = _split_dot(qk * qk, _seg_reduce_mat(2 * D_GRP), SEG_TERMS)
    inv = lax.rsqrt(ss * (1.0 / HEAD_DIM) + NORM_EPS)
    qk = qk * _split_dot(inv, _seg_expand_mat(2 * D_GRP), SEG_TERMS) * qkg_ref[...]
    px_ref[:, :2 * D_GRP] = qk.astype(BF16)
    px_ref[:, 2 * D_GRP:] = px[:, 2 * D_GRP:].astype(BF16)

    lane = _iota((1, LANES), 1)
    z = jnp.zeros((tm, LANES), F32)
    for hd in range(N_HEADS):
        zh = jnp.sum(h * wft_ref[hd:hd + 1, :], axis=-1, keepdims=True)
        z = jnp.where(lane == hd, zh, z)
    cum = _log_sigmoid(z + bf_ref[...])
    row_id = _iota((tm, 1), 0)
    shift = 1
    while shift < tm:
        cum = cum + jnp.where(row_id >= shift, pltpu.roll(cum, shift, axis=0), 0.0)
        shift *= 2
    cum = cum + carry_ref[...]
    carry_ref[...] = cum[tm - 1:tm, :]

    parts = []
    rem = cum * LOG2_E
    for _ in range(3):
        part = rem.astype(BF16)
        rem = rem - part.astype(F32)
        parts.append(part)
    src, dst = _iota((LANES, LANES), 0), _iota((LANES, LANES), 1)

    def spread(offset):
        return sum(jnp.dot(part, ((dst == 8 * src + offset + t) & (src < N_HEADS)).astype(BF16),
                           preferred_element_type=F32) for t, part in enumerate(parts))

    slot = _iota((1, LANES), 1) % 8
    kb_ref[...] = (jnp.where((slot >= 3) & (slot < 6), 1.0, 0.0) - spread(0)).astype(BF16)
    qb_ref[...] = (jnp.where(slot < 3, 1.0, 0.0) + spread(3)).astype(BF16)


def _inproj(x, shift, scale, g, w_r, w_x, w_f_t, b_f, qk_gain, tm):
    bsz, seq, _ = x.shape
    const = lambda b, s: (0, 0)
    return pl.pallas_call(
        _inproj_kernel,
        grid=(bsz, seq // tm),
        in_specs=[pl.BlockSpec((None, tm, D_MODEL), lambda b, s: (b, s, 0)),
                  pl.BlockSpec((None, 1, D_MODEL), lambda b, s: (b, 0, 0)),
                  pl.BlockSpec((None, 1, D_MODEL), lambda b, s: (b, 0, 0)),
                  pl.BlockSpec((1, D_MODEL), const),
                  pl.BlockSpec((D_MODEL, RWKV_COLS), const),
                  pl.BlockSpec((D_MODEL, FOX_MAIN), const),
                  pl.BlockSpec((N_HEADS, D_MODEL), const),
                  pl.BlockSpec((1, LANES), const),
                  pl.BlockSpec((1, 2 * D_GRP), const)],
        out_specs=[pl.BlockSpec((None, tm, RWKV_COLS), lambda b, s: (b, s, 0)),
                   pl.BlockSpec((None, tm, FOX_MAIN), lambda b, s: (b, s, 0)),
                   pl.BlockSpec((None, tm, LANES), lambda b, s: (b, s, 0)),
                   pl.BlockSpec((None, tm, LANES), lambda b, s: (b, s, 0))],
        out_shape=[jax.ShapeDtypeStruct((bsz, seq, RWKV_COLS), BF16),
                   jax.ShapeDtypeStruct((bsz, seq, FOX_MAIN), BF16),
                   jax.ShapeDtypeStruct((bsz, seq, LANES), BF16),
                   jax.ShapeDtypeStruct((bsz, seq, LANES), BF16)],
        scratch_shapes=[pltpu.VMEM((1, LANES), F32)],
        compiler_params=pltpu.CompilerParams(
            dimension_semantics=("parallel", "arbitrary"), vmem_limit_bytes=VMEM_LIMIT),
        name="inproj",
    )(x, shift, scale, g, w_r, w_x, w_f_t, b_f, qk_gain)


_NN = (((1,), (0,)), ((), ()))
_NT = (((1,), (1,)), ((), ()))
_TN = (((0,), (0,)), ((), ()))
SCAN_N = HEADS_PER_SCAN * CHUNK
BATCH_PER_STEP = 8
INV_LEVELS = 5
M_HEAD, M_STRICT, M_INCL, M_EYE, M_BASE, M_OFF = 0, 1, 2, 3, 4, 5


def _bdot(a, b, dims):
    return lax.dot_general(a, b, dims, preferred_element_type=F32)


def _scan_masks():
    rr, cc = _iota((SCAN_N, SCAN_W), 0), _iota((SCAN_N, SCAN_W), 1)
    ri, ci = _iota((SCAN_N, SCAN_N), 0), _iota((SCAN_N, SCAN_N), 1)
    same = ri // CHUNK == ci // CHUNK
    masks = [rr // CHUNK == cc // HEAD_DIM, same & (ri > ci), same & (ri >= ci), ri == ci,
             (ri // 2 == ci // 2) & (ri > ci)]
    blk = 2
    while blk < CHUNK:
        masks.append((ri // (2 * blk) == ci // (2 * blk)) & (ri // blk != ci // blk) & (ri > ci))
        blk *= 2
    return jnp.stack(masks).astype(BF16)


def _rwkv_kernel(p_ref, masks_ref, mu_ref, w0_ref, w2_ref, a0_ref, a2_ref, g2_ref, kk_ref, ka_ref,
                 rk_ref, gnw_ref, gnb_ref, wgu_ref, wd_ref, o_ref, wgu_bf_ref, wd_bf_ref,
                 last_ref, state_ref):
    wgu_bf_ref[...] = wgu_ref[...].astype(BF16)
    wd_bf_ref[...] = wd_ref[...].astype(BF16)

    @pl.when(pl.program_id(1) == 0)
    def _():
        last_ref[...] = jnp.zeros_like(last_ref)
        state_ref[...] = jnp.zeros_like(state_ref)

    mu, w0, w2, a0, a2, g2, k_k, k_a, r_k, gn_w, gn_b = (
        ref[...] for ref in (mu_ref, w0_ref, w2_ref, a0_ref, a2_ref, g2_ref, kk_ref, ka_ref,
                             rk_ref, gnw_ref, gnb_ref))
    rows = BATCH_PER_STEP * CHUNK
    p = p_ref[...].astype(F32).reshape(rows, RWKV_COLS)
    row_id = _iota((rows, 1), 0)
    prev = pltpu.roll(p, 1, axis=0)
    for bb in range(BATCH_PER_STEP):
        prev = jnp.where(row_id == bb * CHUNK, last_ref[bb], prev)
        last_ref[bb] = p[(bb + 1) * CHUNK - 1:(bb + 1) * CHUNK, :]
    pf = p + mu * (prev - p)
    r = pf[:, 0:D_GRP]
    k = pf[:, D_GRP:2 * D_GRP]
    v = pf[:, 2 * D_GRP:3 * D_GRP]
    lora = pf[:, LORA_OFF:GATE_OFF]
    gd = pf[:, GATE_OFF:RWKV_COLS]

    wlog = w0 + _dot(jnp.tanh(lora), w2)
    neg = -wlog
    softplus = jnp.maximum(neg, 0.0) + jnp.log(1.0 + jnp.exp(-jnp.abs(neg)))
    ld = -jnp.exp(-softplus - 0.5)
    a = _sigmoid(a0 + _dot(lora, a2))
    g = _dot(_sigmoid(gd), g2)

    red, exp_m = _seg_reduce_mat(D_GRP), _seg_expand_mat(D_GRP)
    kk = k * k_k
    n2 = _split_dot(kk * kk, red, SEG_TERMS)
    kk = kk * _split_dot(1.0 / jnp.maximum(jnp.sqrt(n2), 1e-12), exp_m, SEG_TERMS)
    k2 = k * (1.0 + (a - 1.0) * k_a)

    tr, tc = _iota((rows, rows), 0), _iota((rows, rows), 1)
    tri = ((tr >= tc) & (tr // CHUNK == tc // CHUNK)).astype(BF16)
    cl = _tri_dot(tri, ld, CUM_TERMS)
    cl_end = jnp.concatenate(
        [jnp.broadcast_to(cl[(bb + 1) * CHUNK - 1:(bb + 1) * CHUNK, :], (CHUNK, D_GRP))
         for bb in range(BATCH_PER_STEP)], axis=0)
    e_in = jnp.exp(cl)
    e_out = jnp.exp(-cl)
    e_rem = jnp.exp(cl_end - cl)
    p_end = jnp.exp(cl_end)
    kka = kk * a
    ops = [(-kk * jnp.exp(cl - ld)).astype(BF16), (kka * e_out).astype(BF16),
           (k2 * e_out).astype(BF16), (r * e_in).astype(BF16), v.astype(BF16),
           (kka * e_rem).astype(BF16), (k2 * e_rem).astype(BF16)]

    chains = [(bb, grp) for bb in range(BATCH_PER_STEP)
              for grp in range(N_HEADS // HEADS_PER_SCAN)]
    head_mask = masks_ref[M_HEAD]
    strict, incl = masks_ref[M_STRICT], masks_ref[M_INCL]

    def stacked(op, bb, grp):
        part = op[bb * CHUNK:(bb + 1) * CHUNK, grp * SCAN_W:(grp + 1) * SCAN_W]
        return jnp.concatenate([part] * HEADS_PER_SCAN, axis=0) * head_mask

    xs = [[stacked(op, bb, grp) for op in ops] for bb, grp in chains]
    st = [state_ref[bb, grp] for bb, grp in chains]
    sb = [s.astype(BF16) for s in st]
    nab = [_bdot(x[0], x[1], _NT).astype(BF16) for x in xs]
    aak = [_bdot(x[0], x[2], _NT).astype(BF16) * strict for x in xs]
    arb = [_bdot(x[3], x[1], _NT).astype(BF16) * incl for x in xs]
    ark = [_bdot(x[3], x[2], _NT).astype(BF16) * incl for x in xs]
    t_inv = [masks_ref[M_EYE] + n * masks_ref[M_BASE] for n in nab]
    for lvl in range(INV_LEVELS):
        half = [_bdot(t, n * masks_ref[M_OFF + lvl], _NN).astype(BF16) for t, n in zip(t_inv, nab)]
        t_inv = [t + _bdot(h, t, _NN).astype(BF16) for t, h in zip(t_inv, half)]
    rhs = [(_bdot(x[0], s, _NT) + _bdot(k, x[4], _NN)).astype(BF16)
           for x, s, k in zip(xs, sb, aak)]
    sa = [_bdot(t, h, _NN).astype(BF16) for t, h in zip(t_inv, rhs)]
    ys = [_bdot(x[3], s, _NT) + _bdot(b, u, _NN) + _bdot(k, x[4], _NN)
          for x, s, b, u, k in zip(xs, sb, arb, sa, ark)]
    for (bb, grp), x, s, u in zip(chains, xs, st, sa):
        decay = p_end[bb * CHUNK:bb * CHUNK + 1, grp * SCAN_W:(grp + 1) * SCAN_W]
        state_ref[bb, grp] = s * decay + _bdot(u, x[5], _TN) + _bdot(x[4], x[6], _TN)
    ys = [y[0:CHUNK] + y[CHUNK:2 * CHUNK] + y[2 * CHUNK:3 * CHUNK] + y[3 * CHUNK:4 * CHUNK]
          for y in ys]
    n_grp = N_HEADS // HEADS_PER_SCAN
    y = jnp.concatenate([jnp.concatenate(ys[bb * n_grp:(bb + 1) * n_grp], axis=1)
                         for bb in range(BATCH_PER_STEP)], axis=0)

    mean = _split_dot(_split_dot(y, red, SEG_TERMS) * (1.0 / HEAD_DIM), exp_m, SEG_TERMS)
    d = y - mean
    var = _split_dot(d * d, red, SEG_TERMS) * (1.0 / HEAD_DIM)
    yn = d * _split_dot(lax.rsqrt(var + GN_EPS), exp_m, SEG_TERMS) * gn_w + gn_b
    bonus = _split_dot(_split_dot(r * k2 * r_k, red, SEG_TERMS), exp_m, SEG_TERMS) * v
    o_ref[...] = ((yn + bonus) * g).astype(BF16).reshape(BATCH_PER_STEP, CHUNK, D_GRP)


def _rwkv(p_r, mu, w0, w2p, a0, a2p, g2, k_k, k_a, r_k, gn_w, gn_b, w_gate_up, w_down):
    bsz, seq, _ = p_r.shape
    assert bsz % BATCH_PER_STEP == 0
    n_chunk = seq // CHUNK
    n_step = (bsz // BATCH_PER_STEP) * n_chunk
    wgu2d = w_gate_up.reshape(-1, w_gate_up.shape[-1])
    wd2d = w_down.reshape(-1, w_down.shape[-1])
    assert wgu2d.shape[0] % (8 * n_step) == 0 and wd2d.shape[0] == wgu2d.shape[0]
    slab = wgu2d.shape[0] // n_step
    masks = _scan_masks()
    const = lambda b, s: (0, 0)
    step = lambda b, s: (b * n_chunk + s, 0)
    vec = pl.BlockSpec((1, D_GRP), const)
    y, wgu_bf, wd_bf = pl.pallas_call(
        _rwkv_kernel,
        grid=(bsz // BATCH_PER_STEP, n_chunk),
        in_specs=[pl.BlockSpec((BATCH_PER_STEP, CHUNK, RWKV_COLS), lambda b, s: (b, s, 0)),
                  pl.BlockSpec(masks.shape, lambda b, s: (0, 0, 0)),
                  pl.BlockSpec((1, RWKV_COLS), const),
                  vec, pl.BlockSpec((LANES, D_GRP), const),
                  vec, pl.BlockSpec((LANES, D_GRP), const),
                  pl.BlockSpec((LANES, D_GRP), const),
                  vec, vec, vec, vec, vec,
                  pl.BlockSpec((slab, wgu2d.shape[1]), step),
                  pl.BlockSpec((slab, wd2d.shape[1]), step)],
        out_specs=[pl.BlockSpec((BATCH_PER_STEP, CHUNK, D_GRP), lambda b, s: (b, s, 0)),
                   pl.BlockSpec((slab, wgu2d.shape[1]), step),
                   pl.BlockSpec((slab, wd2d.shape[1]), step)],
        out_shape=[jax.ShapeDtypeStruct((bsz, seq, D_GRP), BF16),
                   jax.ShapeDtypeStruct(wgu2d.shape, BF16),
                   jax.ShapeDtypeStruct(wd2d.shape, BF16)],
        scratch_shapes=[pltpu.VMEM((BATCH_PER_STEP, 1, RWKV_COLS), F32),
                        pltpu.VMEM((BATCH_PER_STEP, N_HEADS // HEADS_PER_SCAN, SCAN_W, SCAN_W), F32)],
        compiler_params=pltpu.CompilerParams(
            dimension_semantics=("parallel", "arbitrary"), vmem_limit_bytes=VMEM_LIMIT),
        name="rwkv",
    )(p_r, masks, mu, w0, w2p, a0, a2p, g2, k_k, k_a, r_k, gn_w, gn_b, wgu2d, wd2d)
    return y, wgu_bf.reshape(w_gate_up.shape), wd_bf.reshape(w_down.shape)


def _fox_kernel(q_ref, qb_ref, k_ref, kb_ref, vt_ref, og_ref, ong_ref, o_ref, m_ref, l_ref, acc_ref,
                *, seq):
    hp = pl.program_id(1)
    lane = _iota((1, LANES), 1)
    q = q_ref[...]
    qb = qb_ref[...]
    zero = jnp.zeros_like(q)
    qcat = [jnp.concatenate([jnp.where(lane // HEAD_DIM == hh, q, zero),
                             jnp.where(lane // 8 == hp * 2 + hh, qb, zero)], axis=1)
            for hh in range(2)]
    keys = min(FOX_SUB_KEYS, seq)
    n_sub = seq // keys
    diag = _iota((keys, keys), 1) >= _iota((keys, keys), 0)

    m_ref[...] = jnp.full(m_ref.shape, -jnp.inf, F32)
    l_ref[...] = jnp.zeros(l_ref.shape, F32)
    acc_ref[...] = jnp.zeros(acc_ref.shape, F32)

    def scores(s):
        lo = s * keys
        kcat = jnp.concatenate([k_ref[lo:lo + keys, :], kb_ref[lo:lo + keys, :]], axis=1)
        return [lax.dot_general(kcat, qc[lo:, :], _NT, preferred_element_type=F32)
                for qc in qcat]

    pending = scores(0)
    for s in range(n_sub):
        lo = s * keys
        nxt = scores(s + 1) if s + 1 < n_sub else None
        vt = vt_ref[:, lo:lo + keys]
        sts = [jnp.concatenate([jnp.where(diag, st[:, :keys], -jnp.inf), st[:, keys:]], axis=1)
               if st.shape[1] > keys else jnp.where(diag, st, -jnp.inf) for st in pending]
        m_old = [m_ref[hh, :, lo:] for hh in range(2)]
        m_new = [jnp.maximum(m, jnp.max(st, axis=0, keepdims=True)) for m, st in zip(m_old, sts)]
        pts = [jnp.exp2(st - m) for st, m in zip(sts, m_new)]
        pvs = [jnp.dot(vt[hh * HEAD_DIM:(hh + 1) * HEAD_DIM, :], pt.astype(BF16),
                       preferred_element_type=F32) for hh, pt in enumerate(pts)]
        for hh in range(2):
            alpha = jnp.exp2(m_old[hh] - m_new[hh])
            m_ref[hh, :, lo:] = m_new[hh]
            l_ref[hh, :, lo:] = alpha * l_ref[hh, :, lo:] + jnp.sum(pts[hh], axis=0, keepdims=True)
            acc_ref[hh, :, lo:] = alpha * acc_ref[hh, :, lo:] + pvs[hh]
        pending = nxt

    outs = []
    for hh in range(2):
        o = acc_ref[hh] / l_ref[hh]
        outs.append(o * lax.rsqrt(jnp.mean(o * o, axis=0, keepdims=True) + NORM_EPS))
    o = jnp.concatenate(outs, axis=0).T
    o_ref[...] = (o * ong_ref[...] * _sigmoid(og_ref[...].astype(F32))).astype(BF16)


def _fox(p_x, k_bias, q_bias, o_gain):
    bsz, seq, _ = p_x.shape
    npair = N_HEADS // 2
    v_t = jnp.transpose(p_x[:, :, 2 * D_GRP:3 * D_GRP], (0, 2, 1))
    return pl.pallas_call(
        functools.partial(_fox_kernel, seq=seq),
        grid=(bsz, npair),
        in_specs=[pl.BlockSpec((None, seq, LANES), lambda b, h: (b, 0, h)),
                  pl.BlockSpec((None, seq, LANES), lambda b, h: (b, 0, 0)),
                  pl.BlockSpec((None, seq, LANES), lambda b, h: (b, 0, npair + h)),
                  pl.BlockSpec((None, seq, LANES), lambda b, h: (b, 0, 0)),
                  pl.BlockSpec((None, LANES, seq), lambda b, h: (b, h, 0)),
                  pl.BlockSpec((None, seq, LANES), lambda b, h: (b, 0, 3 * npair + h)),
                  pl.BlockSpec((1, LANES), lambda b, h: (0, 0))],
        out_specs=pl.BlockSpec((None, seq, LANES), lambda b, h: (b, 0, h)),
        out_shape=jax.ShapeDtypeStruct((bsz, seq, D_GRP), BF16),
        scratch_shapes=[pltpu.VMEM((2, 1, seq), F32), pltpu.VMEM((2, 1, seq), F32),
                        pltpu.VMEM((2, HEAD_DIM, seq), F32)],
        compiler_params=pltpu.CompilerParams(
            dimension_semantics=("parallel", "parallel"), vmem_limit_bytes=VMEM_LIMIT),
        name="fox",
    )(p_x, q_bias, p_x, k_bias, v_t, p_x, o_gain)


def _outproj_kernel(x_ref, yr_ref, yf_ref, g1_ref, sh_ref, sc_ref, ng_ref, wor_ref, wof_ref,
                    wrt_ref, wrl_ref, brt_ref, x1_ref, h2_ref, idx_ref, gate_ref, rank_ref, cnt_ref,
                    carry_ref):
    @pl.when(pl.program_id(0) == 0)
    def _():
        carry_ref[...] = jnp.zeros_like(carry_ref)

    y = (jnp.dot(yr_ref[...], wor_ref[...], preferred_element_type=F32)
         + jnp.dot(yf_ref[...], wof_ref[...], preferred_element_type=F32))
    x1 = x_ref[...] + g1_ref[...] * y
    x1_ref[...] = x1
    tm = x1.shape[0]
    h = x1 * lax.rsqrt(jnp.mean(x1 * x1, axis=-1, keepdims=True) + NORM_EPS) * ng_ref[...]
    h2 = h * (1.0 + sc_ref[...]) + sh_ref[...]
    h2_ref[...] = _pack_rows(h2)

    h_hi = h2.astype(BF16)
    h_lo = (h2 - h_hi.astype(F32)).astype(BF16)
    logits = (lax.dot_general(wrt_ref[...], h_hi, _NT, preferred_element_type=F32)
              + lax.dot_general(wrt_ref[...], h_lo, _NT, preferred_element_type=F32)
              + lax.dot_general(wrl_ref[...], h_hi, _NT, preferred_element_type=F32))
    lg = logits[:N_EXPERTS, :] + brt_ref[...]
    expert = _iota((N_EXPERTS, tm), 0)
    picks = []
    hot_sum = jnp.zeros((N_EXPERTS, tm), F32)
    for _ in range(TOP_K):
        m = jnp.max(lg, axis=0, keepdims=True)
        sel = jnp.min(jnp.where(lg == m, expert, N_EXPERTS), axis=0, keepdims=True)
        hot = expert == sel
        picks.append((m, sel, hot))
        hot_sum = hot_sum + hot.astype(F32)
        lg = jnp.where(hot, -jnp.inf, lg)
    es = [jnp.exp(m - picks[0][0]) for m, _, _ in picks]
    den = es[0] + es[1] + es[2] + es[3]

    earlier = (_iota((tm, tm), 0) < _iota((tm, tm), 1)).astype(BF16)
    before = jnp.dot(hot_sum.astype(BF16), earlier, preferred_element_type=F32) + carry_ref[...]
    ranks = [jnp.sum(jnp.where(hot, before, 0.0), axis=0, keepdims=True).astype(jnp.int32)
             for _, _, hot in picks]
    pad_i = jnp.zeros((8 - TOP_K, tm), jnp.int32)
    idx_ref[...] = jnp.concatenate([sel for _, sel, _ in picks] + [pad_i], axis=0)
    gate_ref[...] = jnp.concatenate([e / den for e in es] + [pad_i.astype(F32)], axis=0)
    rank_ref[...] = jnp.concatenate(ranks + [pad_i], axis=0)
    carry_ref[...] = carry_ref[...] + jnp.sum(hot_sum, axis=1, keepdims=True)
    cnt_ref[...] = jnp.broadcast_to(carry_ref[...], cnt_ref.shape)


def _outproj(x2d, y_r, y_f, gate1, shift2, scale2, norm_g, wo_r, wo_f, w_rt, b_rt, tm, seq,
             row0, t):
    w_rt_hi = w_rt.astype(BF16)
    w_rt_lo = (w_rt - w_rt_hi.astype(F32)).astype(BF16)
    per_b = seq // tm
    blk0 = row0 // tm
    const = lambda i: (0, 0)
    rows = lambda i: (i, 0)
    rows_in = lambda i: (i + blk0, 0)
    mod = pl.BlockSpec((None, 1, D_MODEL), lambda i: ((i + blk0) // per_b, 0, 0))
    return pl.pallas_call(
        _outproj_kernel,
        grid=(t // tm,),
        in_specs=[pl.BlockSpec((tm, D_MODEL), rows_in),
                  pl.BlockSpec((tm, D_GRP), rows_in),
                  pl.BlockSpec((tm, D_GRP), rows_in),
                  mod, mod, mod,
                  pl.BlockSpec((1, D_MODEL), const),
                  pl.BlockSpec((D_GRP, D_MODEL), const),
                  pl.BlockSpec((D_GRP, D_MODEL), const),
                  pl.BlockSpec((LANES, D_MODEL), const),
                  pl.BlockSpec((LANES, D_MODEL), const),
                  pl.BlockSpec((N_EXPERTS, 1), const)],
        out_specs=[pl.BlockSpec((tm, D_MODEL), rows),
                   pl.BlockSpec((tm, D_PACK), rows),
                   pl.BlockSpec((8, tm), lambda i: (0, i)),
                   pl.BlockSpec((8, tm), lambda i: (0, i)),
                   pl.BlockSpec((8, tm), lambda i: (0, i)),
                   pl.BlockSpec((N_EXPERTS, LANES), const)],
        out_shape=[jax.ShapeDtypeStruct((t, D_MODEL), F32),
                   jax.ShapeDtypeStruct((t, D_PACK), jnp.uint32),
                   jax.ShapeDtypeStruct((8, t), jnp.int32),
                   jax.ShapeDtypeStruct((8, t), F32),
                   jax.ShapeDtypeStruct((8, t), jnp.int32),
                   jax.ShapeDtypeStruct((N_EXPERTS, LANES), F32)],
        scratch_shapes=[pltpu.VMEM((N_EXPERTS, 1), F32)],
        compiler_params=pltpu.CompilerParams(
            dimension_semantics=("arbitrary",), vmem_limit_bytes=VMEM_LIMIT),
        name="outproj",
    )(x2d, y_r, y_f, gate1, shift2, scale2, norm_g, wo_r, wo_f, w_rt_hi, w_rt_lo, b_rt)


SC_CORES = 2
SC_SUBCORES = 16
SC_ROWS = 64


def _sc_gather_rows(idx, src):
    n_workers = SC_CORES * SC_SUBCORES
    m = idx.shape[0]
    d = src.shape[1]
    assert m % (n_workers * SC_ROWS) == 0
    n_chunks = m // (n_workers * SC_ROWS)
    mesh = plsc.VectorSubcoreMesh(core_axis_name="c", subcore_axis_name="s")

    @functools.partial(
        pl.kernel, mesh=mesh,
        out_type=jax.ShapeDtypeStruct((m, d), src.dtype),
        scratch_types=[pltpu.VMEM((n_chunks, SC_ROWS), jnp.int32),
                       pltpu.VMEM((SC_ROWS, d), src.dtype),
                       pltpu.SemaphoreType.DMA],
        name="sc_gather")
    def gather(src_hbm, idx_hbm, out_hbm, idx_v, rows_v, sem):
        wid = lax.axis_index("s") * SC_CORES + lax.axis_index("c")
        pltpu.sync_copy(idx_hbm.at[wid], idx_v)

        @pl.loop(0, n_chunks)
        def _(j):
            pltpu.async_copy(src_hbm.at[idx_v.at[j]], rows_v, sem).wait()
            pltpu.sync_copy(rows_v, out_hbm.at[pl.ds((wid * n_chunks + j) * SC_ROWS, SC_ROWS)])

    return gather(src, idx.reshape(n_workers, n_chunks, SC_ROWS))


def _sc_scatter_rows(src, dest, n_out):
    n_workers = SC_CORES * SC_SUBCORES
    t, d = src.shape
    n_slot = dest.shape[0]
    assert t % (n_workers * SC_ROWS) == 0
    n_chunks = t // (n_workers * SC_ROWS)
    mesh = plsc.VectorSubcoreMesh(core_axis_name="c", subcore_axis_name="s")
    idx = dest.reshape(n_slot, n_workers, n_chunks, SC_ROWS).transpose(1, 2, 0, 3)
    idx = idx.reshape(n_workers, n_chunks * n_slot, SC_ROWS)

    @functools.partial(
        pl.kernel, mesh=mesh,
        out_type=jax.ShapeDtypeStruct((n_out, d), src.dtype),
        scratch_types=[pltpu.VMEM((n_chunks * n_slot, SC_ROWS), jnp.int32),
                       pltpu.VMEM((SC_ROWS, d), src.dtype)],
        name="sc_scatter")
    def scatter(src_hbm, idx_hbm, out_hbm, idx_v, rows_v):
        wid = lax.axis_index("s") * SC_CORES + lax.axis_index("c")
        pltpu.sync_copy(idx_hbm.at[wid], idx_v)

        @pl.loop(0, n_chunks)
        def _(j):
            pltpu.sync_copy(src_hbm.at[pl.ds((wid * n_chunks + j) * SC_ROWS, SC_ROWS)], rows_v)
            for k in range(n_slot):
                pltpu.sync_copy(rows_v, out_hbm.at[idx_v.at[j * n_slot + k]])

    return scatter(src, idx)


def _expert_kernel(be_ref, nv_ref, x_ref, wgu_ref, bgu_ref, wd_ref, bd_ref, o_ref):
    del be_ref
    valid = _iota((EXPERT_BLOCK, 1), 0) < nv_ref[pl.program_id(0)]
    lo, hi = _unpack_rows(jnp.where(valid, x_ref[...], jnp.uint32(0)))
    x = jnp.concatenate([lo.astype(BF16), hi.astype(BF16)], axis=1)
    half = EXPERT_BLOCK // 2
    gus = [jnp.dot(x[r * half:(r + 1) * half], wgu_ref[...], preferred_element_type=F32)
           + bgu_ref[...] for r in range(2)]
    for r, gu in enumerate(gus):
        gate = jnp.minimum(gu[:, :D_MODEL], SWIGLU_LIMIT)
        up = jnp.clip(gu[:, D_MODEL:], -SWIGLU_LIMIT, SWIGLU_LIMIT)
        act = gate * _sigmoid(SWIGLU_ALPHA * gate) * (up + 1.0)
        o_ref[r * half:(r + 1) * half, :] = _pack_rows(
            jnp.dot(act.astype(BF16), wd_ref[...], preferred_element_type=F32) + bd_ref[...])


def _experts(block_e, n_valid, xs, w_gu, b_gu, w_d, b_d):
    n_blocks = block_e.shape[0]
    grid_spec = pltpu.PrefetchScalarGridSpec(
        num_scalar_prefetch=2,
        grid=(n_blocks,),
        in_specs=[pl.BlockSpec((EXPERT_BLOCK, D_PACK), lambda j, be, nv: (j, 0)),
                  pl.BlockSpec((None, D_MODEL, 2 * D_MODEL), lambda j, be, nv: (be[j], 0, 0)),
                  pl.BlockSpec((None, 1, 2 * D_MODEL), lambda j, be, nv: (be[j], 0, 0)),
                  pl.BlockSpec((None, D_MODEL, D_MODEL), lambda j, be, nv: (be[j], 0, 0)),
                  pl.BlockSpec((None, 1, D_MODEL), lambda j, be, nv: (be[j], 0, 0))],
        out_specs=pl.BlockSpec((EXPERT_BLOCK, D_PACK), lambda j, be, nv: (j, 0)),
    )
    return pl.pallas_call(
        _expert_kernel,
        grid_spec=grid_spec,
        out_shape=jax.ShapeDtypeStruct(xs.shape, jnp.uint32),
        compiler_params=pltpu.CompilerParams(
            dimension_semantics=("arbitrary",), vmem_limit_bytes=VMEM_LIMIT),
        name="experts",
    )(block_e, n_valid, xs, w_gu, b_gu, w_d, b_d)


COMBINE_TOKENS = 512
MOE_SPLITS = 2


def _combine_kernel(yg_ref, x1_ref, gate_ref, g2_ref, fg_ref, o_ref):
    gates = gate_ref[...].T
    acc_lo = acc_hi = None
    for kk in range(TOP_K):
        lo, hi = _unpack_rows(yg_ref[kk * COMBINE_TOKENS:(kk + 1) * COMBINE_TOKENS, :])
        g = gates[:, kk:kk + 1]
        acc_lo = g * lo if acc_lo is None else acc_lo + g * lo
        acc_hi = g * hi if acc_hi is None else acc_hi + g * hi
    x2 = x1_ref[...] + g2_ref[...] * jnp.concatenate([acc_lo, acc_hi], axis=1)
    o_ref[...] = x2 * lax.rsqrt(jnp.mean(x2 * x2, axis=-1, keepdims=True) + NORM_EPS) * fg_ref[...]


def _combine_kernel_into(prev_ref, *refs):
    del prev_ref
    _combine_kernel(*refs)


def _combine(yg, x1, gates, gate2, final_g, seq, row0, t_total, prev):
    t = x1.shape[0]
    tm = COMBINE_TOKENS
    per_b = seq // tm
    blk0 = row0 // tm
    rows = lambda i: (i, 0)
    in_specs = [pl.BlockSpec((TOP_K * tm, D_PACK), rows),
                pl.BlockSpec((tm, D_MODEL), rows),
                pl.BlockSpec((8, tm), lambda i: (0, i)),
                pl.BlockSpec((None, 1, D_MODEL), lambda i: ((i + blk0) // per_b, 0, 0)),
                pl.BlockSpec((1, D_MODEL), lambda i: (0, 0))]
    args = (yg, x1, gates, gate2, final_g)
    if prev is not None:
        in_specs = [pl.BlockSpec(memory_space=pl.ANY)] + in_specs
        args = (prev,) + args
    return pl.pallas_call(
        _combine_kernel if prev is None else _combine_kernel_into,
        grid=(t // tm,),
        in_specs=in_specs,
        out_specs=pl.BlockSpec((tm, D_MODEL), lambda i: (i + blk0, 0)),
        out_shape=jax.ShapeDtypeStruct((t_total, D_MODEL), F32),
        input_output_aliases={} if prev is None else {0: 0},
        compiler_params=pltpu.CompilerParams(
            dimension_semantics=("parallel",), vmem_limit_bytes=VMEM_LIMIT),
        name="combine",
    )(*args)


def _moe(h2, idx, gates, rank, counts, x1, gate2, final_g, w_gu, b_gu, w_d, b_d, seq,
         row0, t_total, prev):
    t = h2.shape[0]
    n_slots = t * TOP_K
    n_blocks = -(-n_slots // EXPERT_BLOCK) + N_EXPERTS
    cap = n_blocks * EXPERT_BLOCK
    padded = (counts + EXPERT_BLOCK - 1) // EXPERT_BLOCK * EXPERT_BLOCK
    pad_ends = jnp.cumsum(padded)
    pad_starts = pad_ends - padded
    experts = jnp.arange(N_EXPERTS, dtype=jnp.int32)
    dest = jnp.sum(jnp.where(idx[..., None] == experts, pad_starts, 0), axis=-1) + rank
    block_starts = jnp.arange(n_blocks, dtype=jnp.int32) * EXPERT_BLOCK
    block_e = jnp.minimum(jnp.sum(block_starts[:, None] >= pad_ends[None, :], axis=1),
                          N_EXPERTS - 1).astype(jnp.int32)
    n_valid = jnp.clip(counts[block_e] - (block_starts - pad_starts[block_e]), 0, EXPERT_BLOCK)

    xs = _sc_scatter_rows(h2, dest, cap)
    yb = _experts(block_e, n_valid.astype(jnp.int32), xs, w_gu, b_gu, w_d, b_d)
    dest_blocks = dest.reshape(TOP_K, -1, COMBINE_TOKENS).transpose(1, 0, 2).reshape(-1)
    yg = _sc_gather_rows(dest_blocks, yb)
    return _combine(yg, x1, gates, gate2, final_g, seq, row0, t_total, prev)


def _layer(x, c_mod, norm1_g, w_in, mu_shift, w0, w2, a0, a2, g2, k_k, k_a, r_k, gn_w, gn_b, b_f,
           q_norm_g, k_norm_g, o_norm_g, w_out, norm2_g, w_router, b_router, w_gate_up,
           b_gate_up, w_down, b_down, final_g, tm_in, tm_out):
    bsz, seq, _ = x.shape
    shift1, scale1, gate1, shift2, scale2, gate2 = (
        m.reshape(bsz, 1, D_MODEL) for m in jnp.split(c_mod, 6, axis=-1))
    row = lambda v: v.reshape(1, -1)

    w_r = w_in[:, :RWKV_COLS].astype(BF16)
    w_x = w_in[:, RWKV_COLS:RWKV_COLS + FOX_MAIN].astype(BF16)
    w_f = w_in[:, RWKV_COLS + FOX_MAIN:].T
    b_fp = jnp.pad(b_f, (0, LANES - N_HEADS)).reshape(1, LANES)
    qk_gain = jnp.concatenate([jnp.tile(q_norm_g, N_HEADS) * (HEAD_DIM ** -0.5 * LOG2_E),
                               jnp.tile(k_norm_g, N_HEADS)]).reshape(1, -1)
    p_r, p_x, k_bias, q_bias = _inproj(x, shift1, scale1, row(norm1_g), w_r, w_x, w_f, b_fp,
                                       qk_gain, tm_in)

    zeros = jnp.zeros((LANES - 64, D_GRP), F32)
    w2p = jnp.concatenate([w2, zeros], axis=0).astype(BF16)
    a2p = jnp.concatenate([zeros, a2], axis=0).astype(BF16)
    y_r, w_gu, w_d = _rwkv(p_r, row(mu_shift), row(w0), w2p, row(a0), a2p, g2.astype(BF16),
                           row(k_k), row(k_a), row(r_k), row(gn_w), row(gn_b), w_gate_up, w_down)

    y_f = _fox(p_x, k_bias, q_bias, jnp.tile(o_norm_g, 2).reshape(1, LANES))

    t = bsz * seq
    w_rt = jnp.pad(w_router.T, ((0, LANES - N_EXPERTS), (0, 0)))
    b_rt = b_router.reshape(N_EXPERTS, 1)
    wo = w_out.astype(BF16)
    b_gu, b_d = b_gate_up.reshape(N_EXPERTS, 1, -1), b_down.reshape(N_EXPERTS, 1, -1)
    t_part = t // MOE_SPLITS
    out = None
    for part in range(MOE_SPLITS):
        row0 = part * t_part
        x1, h2, idx, gates, rank, cnt = _outproj(
            x.reshape(t, D_MODEL), y_r.reshape(t, D_GRP), y_f.reshape(t, D_GRP), gate1, shift2,
            scale2, row(norm2_g), wo[:D_GRP], wo[D_GRP:], w_rt, b_rt, tm_out, seq, row0, t_part)
        counts = cnt[:, 0].astype(jnp.int32)
        out = _moe(h2, idx[:TOP_K], gates, rank[:TOP_K], counts, x1, gate2, row(final_g),
                   w_gu, b_gu, w_d, b_d, seq, row0, t, out)
    return out.reshape(bsz, seq, D_MODEL)


def kernel(x, c, w_ada, b_ada, norm1_g, w_in, mu_shift, w0, w2, a0, a2, g2, k_k, k_a, r_k, gn_w,
           gn_b, b_f, q_norm_g, k_norm_g, o_norm_g, w_out, norm2_g, w_router, b_router, w_gate_up,
           b_gate_up, w_down, b_down, final_g):
    assert w_ada.shape[0] == 1, "single-layer block"
    c_mod = _adaln(c, w_ada[0], b_ada[0])
    return _layer(x, c_mod, norm1_g[0], w_in[0], mu_shift[0], w0[0], w2[0], a0[0], a2[0], g2[0],
                  k_k[0], k_a[0], r_k[0], gn_w[0], gn_b[0], b_f[0], q_norm_g[0], k_norm_g[0],
                  o_norm_g[0], w_out[0], norm2_g[0], w_router[0], b_router[0], w_gate_up[0],
                  b_gate_up[0], w_down[0], b_down[0], final_g,
                  tm_in=min(512, x.shape[1]), tm_out=min(1024, x.shape[1]))
```

```python
import functools

import jax
import jax.numpy as jnp
from jax import lax
from jax.experimental import pallas as pl
from jax.experimental.pallas import tpu as pltpu
from jax.experimental.pallas import tpu_sc as plsc

F32 = jnp.float32
BF16 = jnp.bfloat16
HIGHEST = lax.Precision.HIGHEST

D_MODEL = 1024
HEAD_DIM = 64
N_HEADS = 8
D_GRP = N_HEADS * HEAD_DIM
RWKV_COLS = 1792
LORA_OFF = 3 * D_GRP
GATE_OFF = LORA_OFF + 128
FOX_MAIN = 4 * D_GRP
N_EXPERTS = 32
TOP_K = 4
EXPERT_BLOCK = 512
SWIGLU_ALPHA = 1.702
SWIGLU_LIMIT = 7.0
NORM_EPS = 1e-6
GN_EPS = 64e-5
LOG2_E = 1.4426950408889634
LANES = 128
CHUNK = 64
FOX_SUB_KEYS = 512
HEADS_PER_SCAN = 4
SCAN_W = HEADS_PER_SCAN * HEAD_DIM
SEG_TERMS = 1
CUM_TERMS = 2
VMEM_LIMIT = 56 * 1024 * 1024


def _dot(a, b):
    return jnp.dot(a.astype(BF16), b.astype(BF16), preferred_element_type=F32)


def _fdot(a, b):
    return jnp.dot(a, b, precision=HIGHEST, preferred_element_type=F32)


def _split_dot(x, m, terms):
    acc = None
    rem = x
    for _ in range(terms):
        part = rem.astype(BF16)
        rem = rem - part.astype(F32)
        d = jnp.dot(part, m, preferred_element_type=F32)
        acc = d if acc is None else acc + d
    return acc


def _tri_dot(m, x, terms):
    acc = None
    rem = x
    for _ in range(terms):
        part = rem.astype(BF16)
        rem = rem - part.astype(F32)
        d = jnp.dot(m, part, preferred_element_type=F32)
        acc = d if acc is None else acc + d
    return acc


def _iota(shape, dim):
    return lax.broadcasted_iota(jnp.int32, shape, dim)


def _seg_reduce_mat(n):
    return (_iota((n, LANES), 0) // HEAD_DIM == _iota((n, LANES), 1)).astype(BF16)


def _seg_expand_mat(n):
    return (_iota((LANES, n), 1) // HEAD_DIM == _iota((LANES, n), 0)).astype(BF16)


D_PACK = D_MODEL // 2


def _pack_rows(x):
    lo = lax.bitcast_convert_type(x[:, :D_PACK].astype(BF16).astype(F32), jnp.uint32)
    hi = lax.bitcast_convert_type(x[:, D_PACK:].astype(BF16).astype(F32), jnp.uint32)
    return hi | (lo >> 16)


def _unpack_rows(p):
    lo = lax.bitcast_convert_type(p << 16, F32)
    hi = lax.bitcast_convert_type(p & jnp.uint32(0xFFFF0000), F32)
    return lo, hi


def _log_sigmoid(z):
    return jnp.minimum(z, 0.0) - jnp.log(1.0 + jnp.exp(-jnp.abs(z)))


def _sigmoid(z):
    return 1.0 / (1.0 + jnp.exp(-z))


def _adaln_kernel(c_ref, w_ref, b_ref, o_ref):
    c = c_ref[...]
    o_ref[...] = _fdot(c * _sigmoid(c), w_ref[...]) + b_ref[...]


def _adaln(c, w_ada, b_ada):
    bsz = c.shape[0]
    n_mod = w_ada.shape[1] // D_MODEL
    return pl.pallas_call(
        _adaln_kernel,
        grid=(n_mod,),
        in_specs=[pl.BlockSpec((bsz, D_MODEL), lambda j: (0, 0)),
                  pl.BlockSpec((D_MODEL, D_MODEL), lambda j: (0, j)),
                  pl.BlockSpec((1, D_MODEL), lambda j: (0, j))],
        out_specs=pl.BlockSpec((bsz, D_MODEL), lambda j: (0, j)),
        out_shape=jax.ShapeDtypeStruct((bsz, n_mod * D_MODEL), F32),
        name="adaln",
    )(c, w_ada, b_ada.reshape(1, -1))


def _inproj_kernel(x_ref, sh_ref, sc_ref, g_ref, wr_ref, wx_ref, wft_ref, bf_ref, qkg_ref,
                   pr_ref, px_ref, kb_ref, qb_ref, carry_ref):
    @pl.when(pl.program_id(1) == 0)
    def _():
        carry_ref[...] = jnp.zeros_like(carry_ref)

    x = x_ref[...]
    tm = x.shape[0]
    h = x * lax.rsqrt(jnp.mean(x * x, axis=-1, keepdims=True) + NORM_EPS) * g_ref[...]
    h = h * (1.0 + sc_ref[...]) + sh_ref[...]
    hb = h.astype(BF16)

    pr_ref[...] = jnp.dot(hb, wr_ref[...], preferred_element_type=F32).astype(BF16)

    px = jnp.dot(hb, wx_ref[...], preferred_element_type=F32)
    qk = px[:, :2 * D_GRP]
    ss = _split_dot(qk * qk, _seg_reduce_mat(2 * D_GRP), SEG_TERMS)
    inv = lax.rsqrt(ss * (1.0 / HEAD_DIM) + NORM_EPS)
    qk = qk * _split_dot(inv, _seg_expand_mat(2 * D_GRP), SEG_TERMS) * qkg_ref[...]
    px_ref[:, :2 * D_GRP] = qk.astype(BF16)
    px_ref[:, 2 * D_GRP:] = px[:, 2 * D_GRP:].astype(BF16)

    lane = _iota((1, LANES), 1)
    z = jnp.zeros((tm, LANES), F32)
    for hd in range(N_HEADS):
        zh = jnp.sum(h * wft_ref[hd:hd + 1, :], axis=-1, keepdims=True)
        z = jnp.where(lane == hd, zh, z)
    cum = _log_sigmoid(z + bf_ref[...])
    row_id = _iota((tm, 1), 0)
    shift = 1
    while shift < tm:
        cum = cum + jnp.where(row_id >= shift, pltpu.roll(cum, shift, axis=0), 0.0)
        shift *= 2
    cum = cum + carry_ref[...]
    carry_ref[...] = cum[tm - 1:tm, :]

    parts = []
    rem = cum * LOG2_E
    for _ in range(3):
        part = rem.astype(BF16)
        rem = rem - part.astype(F32)
        parts.append(part)
    src, dst = _iota((LANES, LANES), 0), _iota((LANES, LANES), 1)

    def spread(offset):
        return sum(jnp.dot(part, ((dst == 8 * src + offset + t) & (src < N_HEADS)).astype(BF16),
                           preferred_element_type=F32) for t, part in enumerate(parts))

    slot = _iota((1, LANES), 1) % 8
    kb_ref[...] = (jnp.where((slot >= 3) & (slot < 6), 1.0, 0.0) - spread(0)).astype(BF16)
    qb_ref[...] = (jnp.where(slot < 3, 1.0, 0.0) + spread(3)).astype(BF16)


def _inproj(x, shift, scale, g, w_r, w_x, w_f_t, b_f, qk_gain, tm):
    bsz, seq, _ = x.shape
    const = lambda b, s: (0, 0)
    return pl.pallas_call(
        _inproj_kernel,
        grid=(bsz, seq // tm),
        in_specs=[pl.BlockSpec((None, tm, D_MODEL), lambda b, s: (b, s, 0)),
                  pl.BlockSpec((None, 1, D_MODEL), lambda b, s: (b, 0, 0)),
                  pl.BlockSpec((None, 1, D_MODEL), lambda b, s: (b, 0, 0)),
                  pl.BlockSpec((1, D_MODEL), const),
                  pl.BlockSpec((D_MODEL, RWKV_COLS), const),
                  pl.BlockSpec((D_MODEL, FOX_MAIN), const),
                  pl.BlockSpec((N_HEADS, D_MODEL), const),
                  pl.BlockSpec((1, LANES), const),
                  pl.BlockSpec((1, 2 * D_GRP), const)],
        out_specs=[pl.BlockSpec((None, tm, RWKV_COLS), lambda b, s: (b, s, 0)),
                   pl.BlockSpec((None, tm, FOX_MAIN), lambda b, s: (b, s, 0)),
                   pl.BlockSpec((None, tm, LANES), lambda b, s: (b, s, 0)),
                   pl.BlockSpec((None, tm, LANES), lambda b, s: (b, s, 0))],
        out_shape=[jax.ShapeDtypeStruct((bsz, seq, RWKV_COLS), BF16),
                   jax.ShapeDtypeStruct((bsz, seq, FOX_MAIN), BF16),
                   jax.ShapeDtypeStruct((bsz, seq, LANES), BF16),
                   jax.ShapeDtypeStruct((bsz, seq, LANES), BF16)],
        scratch_shapes=[pltpu.VMEM((1, LANES), F32)],
        compiler_params=pltpu.CompilerParams(
            dimension_semantics=("parallel", "arbitrary"), vmem_limit_bytes=VMEM_LIMIT),
        name="inproj",
    )(x, shift, scale, g, w_r, w_x, w_f_t, b_f, qk_gain)


_NN = (((1,), (0,)), ((), ()))
_NT = (((1,), (1,)), ((), ()))
_TN = (((0,), (0,)), ((), ()))
SCAN_N = HEADS_PER_SCAN * CHUNK
BATCH_PER_STEP = 8
INV_LEVELS = 5
M_HEAD, M_STRICT, M_INCL, M_EYE, M_BASE, M_OFF = 0, 1, 2, 3, 4, 5


def _bdot(a, b, dims):
    return lax.dot_general(a, b, dims, preferred_element_type=F32)


def _scan_masks():
    rr, cc = _iota((SCAN_N, SCAN_W), 0), _iota((SCAN_N, SCAN_W), 1)
    ri, ci = _iota((SCAN_N, SCAN_N), 0), _iota((SCAN_N, SCAN_N), 1)
    same = ri // CHUNK == ci // CHUNK
    masks = [rr // CHUNK == cc // HEAD_DIM, same & (ri > ci), same & (ri >= ci), ri == ci,
             (ri // 2 == ci // 2) & (ri > ci)]
    blk = 2
    while blk < CHUNK:
        masks.append((ri // (2 * blk) == ci // (2 * blk)) & (ri // blk != ci // blk) & (ri > ci))
        blk *= 2
    return jnp.stack(masks).astype(BF16)


def _rwkv_kernel(p_ref, masks_ref, mu_ref, w0_ref, w2_ref, a0_ref, a2_ref, g2_ref, kk_ref, ka_ref,
                 rk_ref, gnw_ref, gnb_ref, wgu_ref, wd_ref, o_ref, wgu_bf_ref, wd_bf_ref,
                 last_ref, state_ref):
    wgu_bf_ref[...] = wgu_ref[...].astype(BF16)
    wd_bf_ref[...] = wd_ref[...].astype(BF16)

    @pl.when(pl.program_id(1) == 0)
    def _():
        last_ref[...] = jnp.zeros_like(last_ref)
        state_ref[...] = jnp.zeros_like(state_ref)

    mu, w0, w2, a0, a2, g2, k_k, k_a, r_k, gn_w, gn_b = (
        ref[...] for ref in (mu_ref, w0_ref, w2_ref, a0_ref, a2_ref, g2_ref, kk_ref, ka_ref,
                             rk_ref, gnw_ref, gnb_ref))
    rows = BATCH_PER_STEP * CHUNK
    p = p_ref[...].astype(F32).reshape(rows, RWKV_COLS)
    row_id = _iota((rows, 1), 0)
    prev = pltpu.roll(p, 1, axis=0)
    for bb in range(BATCH_PER_STEP):
        prev = jnp.where(row_id == bb * CHUNK, last_ref[bb], prev)
        last_ref[bb] = p[(bb + 1) * CHUNK - 1:(bb + 1) * CHUNK, :]
    pf = p + mu * (prev - p)
    r = pf[:, 0:D_GRP]
    k = pf[:, D_GRP:2 * D_GRP]
    v = pf[:, 2 * D_GRP:3 * D_GRP]
    lora = pf[:, LORA_OFF:GATE_OFF]
    gd = pf[:, GATE_OFF:RWKV_COLS]

    wlog = w0 + _dot(jnp.tanh(lora), w2)
    neg = -wlog
    softplus = jnp.maximum(neg, 0.0) + jnp.log(1.0 + jnp.exp(-jnp.abs(neg)))
    ld = -jnp.exp(-softplus - 0.5)
    a = _sigmoid(a0 + _dot(lora, a2))
    g = _dot(_sigmoid(gd), g2)

    red, exp_m = _seg_reduce_mat(D_GRP), _seg_expand_mat(D_GRP)
    kk = k * k_k
    n2 = _split_dot(kk * kk, red, SEG_TERMS)
    kk = kk * _split_dot(1.0 / jnp.maximum(jnp.sqrt(n2), 1e-12), exp_m, SEG_TERMS)
    k2 = k * (1.0 + (a - 1.0) * k_a)

    tr, tc = _iota((rows, rows), 0), _iota((rows, rows), 1)
    tri = ((tr >= tc) & (tr // CHUNK == tc // CHUNK)).astype(BF16)
    cl = _tri_dot(tri, ld, CUM_TERMS)
    cl_end = jnp.concatenate(
        [jnp.broadcast_to(cl[(bb + 1) * CHUNK - 1:(bb + 1) * CHUNK, :], (CHUNK, D_GRP))
         for bb in range(BATCH_PER_STEP)], axis=0)
    e_in = jnp.exp(cl)
    e_out = jnp.exp(-cl)
    e_rem = jnp.exp(cl_end - cl)
    p_end = jnp.exp(cl_end)
    kka = kk * a
    ops = [(-kk * jnp.exp(cl - ld)).astype(BF16), (kka * e_out).astype(BF16),
           (k2 * e_out).astype(BF16), (r * e_in).astype(BF16), v.astype(BF16),
           (kka * e_rem).astype(BF16), (k2 * e_rem).astype(BF16)]

    chains = [(bb, grp) for bb in range(BATCH_PER_STEP)
              for grp in range(N_HEADS // HEADS_PER_SCAN)]
    head_mask = masks_ref[M_HEAD]
    strict, incl = masks_ref[M_STRICT], masks_ref[M_INCL]

    def stacked(op, bb, grp):
        part = op[bb * CHUNK:(bb + 1) * CHUNK, grp * SCAN_W:(grp + 1) * SCAN_W]
        return jnp.concatenate([part] * HEADS_PER_SCAN, axis=0) * head_mask

    xs = [[stacked(op, bb, grp) for op in ops] for bb, grp in chains]
    st = [state_ref[bb, grp] for bb, grp in chains]
    sb = [s.astype(BF16) for s in st]
    nab = [_bdot(x[0], x[1], _NT).astype(BF16) for x in xs]
    aak = [_bdot(x[0], x[2], _NT).astype(BF16) * strict for x in xs]
    arb = [_bdot(x[3], x[1], _NT).astype(BF16) * incl for x in xs]
    ark = [_bdot(x[3], x[2], _NT).astype(BF16) * incl for x in xs]
    t_inv = [masks_ref[M_EYE] + n * masks_ref[M_BASE] for n in nab]
    for lvl in range(INV_LEVELS):
        half = [_bdot(t, n * masks_ref[M_OFF + lvl], _NN).astype(BF16) for t, n in zip(t_inv, nab)]
        t_inv = [t + _bdot(h, t, _NN).astype(BF16) for t, h in zip(t_inv, half)]
    rhs = [(_bdot(x[0], s, _NT) + _bdot(k, x[4], _NN)).astype(BF16)
           for x, s, k in zip(xs, sb, aak)]
    sa = [_bdot(t, h, _NN).astype(BF16) for t, h in zip(t_inv, rhs)]
    ys = [_bdot(x[3], s, _NT) + _bdot(b, u, _NN) + _bdot(k, x[4], _NN)
          for x, s, b, u, k in zip(xs, sb, arb, sa, ark)]
    for (bb, grp), x, s, u in zip(chains, xs, st, sa):
        decay = p_end[bb * CHUNK:bb * CHUNK + 1, grp * SCAN_W:(grp + 1) * SCAN_W]
        state_ref[bb, grp] = s * decay + _bdot(u, x[5], _TN) + _bdot(x[4], x[6], _TN)
    ys = [y[0:CHUNK] + y[CHUNK:2 * CHUNK] + y[2 * CHUNK:3 * CHUNK] + y[3 * CHUNK:4 * CHUNK]
          for y in ys]
    n_grp = N_HEADS // HEADS_PER_SCAN
    y = jnp.concatenate([jnp.concatenate(ys[bb * n_grp:(bb + 1) * n_grp], axis=1)
                         for bb in range(BATCH_PER_STEP)], axis=0)

    mean = _split_dot(_split_dot(y, red, SEG_TERMS) * (1.0 / HEAD_DIM), exp_m, SEG_TERMS)
    d = y - mean
    var = _split_dot(d * d, red, SEG_TERMS) * (1.0 / HEAD_DIM)
    yn = d * _split_dot(lax.rsqrt(var + GN_EPS), exp_m, SEG_TERMS) * gn_w + gn_b
    bonus = _split_dot(_split_dot(r * k2 * r_k, red, SEG_TERMS), exp_m, SEG_TERMS) * v
    o_ref[...] = ((yn + bonus) * g).astype(BF16).reshape(BATCH_PER_STEP, CHUNK, D_GRP)


def _rwkv(p_r, mu, w0, w2p, a0, a2p, g2, k_k, k_a, r_k, gn_w, gn_b, w_gate_up, w_down):
    bsz, seq, _ = p_r.shape
    assert bsz % BATCH_PER_STEP == 0
    n_chunk = seq // CHUNK
    n_step = (bsz // BATCH_PER_STEP) * n_chunk
    wgu2d = w_gate_up.reshape(-1, w_gate_up.shape[-1])
    wd2d = w_down.reshape(-1, w_down.shape[-1])
    assert wgu2d.shape[0] % (8 * n_step) == 0 and wd2d.shape[0] == wgu2d.shape[0]
    slab = wgu2d.shape[0] // n_step
    masks = _scan_masks()
    const = lambda b, s: (0, 0)
    step = lambda b, s: (b * n_chunk + s, 0)
    vec = pl.BlockSpec((1, D_GRP), const)
    y, wgu_bf, wd_bf = pl.pallas_call(
        _rwkv_kernel,
        grid=(bsz // BATCH_PER_STEP, n_chunk),
        in_specs=[pl.BlockSpec((BATCH_PER_STEP, CHUNK, RWKV_COLS), lambda b, s: (b, s, 0)),
                  pl.BlockSpec(masks.shape, lambda b, s: (0, 0, 0)),
                  pl.BlockSpec((1, RWKV_COLS), const),
                  vec, pl.BlockSpec((LANES, D_GRP), const),
                  vec, pl.BlockSpec((LANES, D_GRP), const),
                  pl.BlockSpec((LANES, D_GRP), const),
                  vec, vec, vec, vec, vec,
                  pl.BlockSpec((slab, wgu2d.shape[1]), step),
                  pl.BlockSpec((slab, wd2d.shape[1]), step)],
        out_specs=[pl.BlockSpec((BATCH_PER_STEP, CHUNK, D_GRP), lambda b, s: (b, s, 0)),
                   pl.BlockSpec((slab, wgu2d.shape[1]), step),
                   pl.BlockSpec((slab, wd2d.shape[1]), step)],
        out_shape=[jax.ShapeDtypeStruct((bsz, seq, D_GRP), BF16),
                   jax.ShapeDtypeStruct(wgu2d.shape, BF16),
                   jax.ShapeDtypeStruct(wd2d.shape, BF16)],
        scratch_shapes=[pltpu.VMEM((BATCH_PER_STEP, 1, RWKV_COLS), F32),
                        pltpu.VMEM((BATCH_PER_STEP, N_HEADS // HEADS_PER_SCAN, SCAN_W, SCAN_W), F32)],
        compiler_params=pltpu.CompilerParams(
            dimension_semantics=("parallel", "arbitrary"), vmem_limit_bytes=VMEM_LIMIT),
        name="rwkv",
    )(p_r, masks, mu, w0, w2p, a0, a2p, g2, k_k, k_a, r_k, gn_w, gn_b, wgu2d, wd2d)
    return y, wgu_bf.reshape(w_gate_up.shape), wd_bf.reshape(w_down.shape)


def _fox_kernel(q_ref, qb_ref, k_ref, kb_ref, vt_ref, og_ref, ong_ref, o_ref, m_ref, l_ref, acc_ref,
                *, seq):
    hp = pl.program_id(1)
    lane = _iota((1, LANES), 1)
    q = q_ref[...]
    qb = qb_ref[...]
    zero = jnp.zeros_like(q)
    qcat = [jnp.concatenate([jnp.where(lane // HEAD_DIM == hh, q, zero),
                             jnp.where(lane // 8 == hp * 2 + hh, qb, zero)], axis=1)
            for hh in range(2)]
    keys = min(FOX_SUB_KEYS, seq)
    n_sub = seq // keys
    diag = _iota((keys, keys), 1) >= _iota((keys, keys), 0)

    m_ref[...] = jnp.full(m_ref.shape, -jnp.inf, F32)
    l_ref[...] = jnp.zeros(l_ref.shape, F32)
    acc_ref[...] = jnp.zeros(acc_ref.shape, F32)

    def scores(s):
        lo = s * keys
        kcat = jnp.concatenate([k_ref[lo:lo + keys, :], kb_ref[lo:lo + keys, :]], axis=1)
        return [lax.dot_general(kcat, qc[lo:, :], _NT, preferred_element_type=F32)
                for qc in qcat]

    pending = scores(0)
    for s in range(n_sub):
        lo = s * keys
        nxt = scores(s + 1) if s + 1 < n_sub else None
        vt = vt_ref[:, lo:lo + keys]
        sts = [jnp.concatenate([jnp.where(diag, st[:, :keys], -jnp.inf), st[:, keys:]], axis=1)
               if st.shape[1] > keys else jnp.where(diag, st, -jnp.inf) for st in pending]
        m_old = [m_ref[hh, :, lo:] for hh in range(2)]
        m_new = [jnp.maximum(m, jnp.max(st, axis=0, keepdims=True)) for m, st in zip(m_old, sts)]
        pts = [jnp.exp2(st - m) for st, m in zip(sts, m_new)]
        pvs = [jnp.dot(vt, pt.astype(BF16), preferred_element_type=F32) for pt in pts]
        for hh in range(2):
            alpha = jnp.exp2(m_old[hh] - m_new[hh])
            m_ref[hh, :, lo:] = m_new[hh]
            l_ref[hh, :, lo:] = alpha * l_ref[hh, :, lo:] + jnp.sum(pts[hh], axis=0, keepdims=True)
            acc_ref[hh, :, lo:] = (alpha * acc_ref[hh, :, lo:]
                                   + pvs[hh][hh * HEAD_DIM:(hh + 1) * HEAD_DIM, :])
        pending = nxt

    outs = []
    for hh in range(2):
        o = acc_ref[hh] / l_ref[hh]
        outs.append(o * lax.rsqrt(jnp.mean(o * o, axis=0, keepdims=True) + NORM_EPS))
    o = jnp.concatenate(outs, axis=0).T
    o_ref[...] = (o * ong_ref[...] * _sigmoid(og_ref[...].astype(F32))).astype(BF16)


def _fox(p_x, k_bias, q_bias, o_gain):
    bsz, seq, _ = p_x.shape
    npair = N_HEADS // 2
    v_t = jnp.transpose(p_x[:, :, 2 * D_GRP:3 * D_GRP], (0, 2, 1))
    return pl.pallas_call(
        functools.partial(_fox_kernel, seq=seq),
        grid=(bsz, npair),
        in_specs=[pl.BlockSpec((None, seq, LANES), lambda b, h: (b, 0, h)),
                  pl.BlockSpec((None, seq, LANES), lambda b, h: (b, 0, 0)),
                  pl.BlockSpec((None, seq, LANES), lambda b, h: (b, 0, npair + h)),
                  pl.BlockSpec((None, seq, LANES), lambda b, h: (b, 0, 0)),
                  pl.BlockSpec((None, LANES, seq), lambda b, h: (b, h, 0)),
                  pl.BlockSpec((None, seq, LANES), lambda b, h: (b, 0, 3 * npair + h)),
                  pl.BlockSpec((1, LANES), lambda b, h: (0, 0))],
        out_specs=pl.BlockSpec((None, seq, LANES), lambda b, h: (b, 0, h)),
        out_shape=jax.ShapeDtypeStruct((bsz, seq, D_GRP), BF16),
        scratch_shapes=[pltpu.VMEM((2, 1, seq), F32), pltpu.VMEM((2, 1, seq), F32),
                        pltpu.VMEM((2, HEAD_DIM, seq), F32)],
        compiler_params=pltpu.CompilerParams(
            dimension_semantics=("parallel", "parallel"), vmem_limit_bytes=VMEM_LIMIT),
        name="fox",
    )(p_x, q_bias, p_x, k_bias, v_t, p_x, o_gain)


def _outproj_kernel(x_ref, yr_ref, yf_ref, g1_ref, sh_ref, sc_ref, ng_ref, wor_ref, wof_ref,
                    wrt_ref, wrl_ref, brt_ref, x1_ref, h2_ref, idx_ref, gate_ref, rank_ref, cnt_ref,
                    carry_ref):
    @pl.when(pl.program_id(0) == 0)
    def _():
        carry_ref[...] = jnp.zeros_like(carry_ref)

    y = (jnp.dot(yr_ref[...], wor_ref[...], preferred_element_type=F32)
         + jnp.dot(yf_ref[...], wof_ref[...], preferred_element_type=F32))
    x1 = x_ref[...] + g1_ref[...] * y
    x1_ref[...] = x1
    tm = x1.shape[0]
    h = x1 * lax.rsqrt(jnp.mean(x1 * x1, axis=-1, keepdims=True) + NORM_EPS) * ng_ref[...]
    h2 = h * (1.0 + sc_ref[...]) + sh_ref[...]
    h2_ref[...] = _pack_rows(h2)

    h_hi = h2.astype(BF16)
    h_lo = (h2 - h_hi.astype(F32)).astype(BF16)
    logits = (lax.dot_general(wrt_ref[...], h_hi, _NT, preferred_element_type=F32)
              + lax.dot_general(wrt_ref[...], h_lo, _NT, preferred_element_type=F32)
              + lax.dot_general(wrl_ref[...], h_hi, _NT, preferred_element_type=F32))
    lg = logits[:N_EXPERTS, :] + brt_ref[...]
    expert = _iota((N_EXPERTS, tm), 0)
    picks = []
    hot_sum = jnp.zeros((N_EXPERTS, tm), F32)
    for _ in range(TOP_K):
        m = jnp.max(lg, axis=0, keepdims=True)
        sel = jnp.min(jnp.where(lg == m, expert, N_EXPERTS), axis=0, keepdims=True)
        hot = expert == sel
        picks.append((m, sel, hot))
        hot_sum = hot_sum + hot.astype(F32)
        lg = jnp.where(hot, -jnp.inf, lg)
    es = [jnp.exp(m - picks[0][0]) for m, _, _ in picks]
    den = es[0] + es[1] + es[2] + es[3]

    earlier = (_iota((tm, tm), 0) < _iota((tm, tm), 1)).astype(BF16)
    before = jnp.dot(hot_sum.astype(BF16), earlier, preferred_element_type=F32) + carry_ref[...]
    ranks = [jnp.sum(jnp.where(hot, before, 0.0), axis=0, keepdims=True).astype(jnp.int32)
             for _, _, hot in picks]
    pad_i = jnp.zeros((8 - TOP_K, tm), jnp.int32)
    idx_ref[...] = jnp.concatenate([sel for _, sel, _ in picks] + [pad_i], axis=0)
    gate_ref[...] = jnp.concatenate([e / den for e in es] + [pad_i.astype(F32)], axis=0)
    rank_ref[...] = jnp.concatenate(ranks + [pad_i], axis=0)
    carry_ref[...] = carry_ref[...] + jnp.sum(hot_sum, axis=1, keepdims=True)
    cnt_ref[...] = jnp.broadcast_to(carry_ref[...], cnt_ref.shape)


def _outproj(x2d, y_r, y_f, gate1, shift2, scale2, norm_g, wo_r, wo_f, w_rt, b_rt, tm, seq,
             row0, t):
    w_rt_hi = w_rt.astype(BF16)
    w_rt_lo = (w_rt - w_rt_hi.astype(F32)).astype(BF16)
    per_b = seq // tm
    blk0 = row0 // tm
    const = lambda i: (0, 0)
    rows = lambda i: (i, 0)
    rows_in = lambda i: (i + blk0, 0)
    mod = pl.BlockSpec((None, 1, D_MODEL), lambda i: ((i + blk0) // per_b, 0, 0))
    return pl.pallas_call(
        _outproj_kernel,
        grid=(t // tm,),
        in_specs=[pl.BlockSpec((tm, D_MODEL), rows_in),
                  pl.BlockSpec((tm, D_GRP), rows_in),
                  pl.BlockSpec((tm, D_GRP), rows_in),
                  mod, mod, mod,
                  pl.BlockSpec((1, D_MODEL), const),
                  pl.BlockSpec((D_GRP, D_MODEL), const),
                  pl.BlockSpec((D_GRP, D_MODEL), const),
                  pl.BlockSpec((LANES, D_MODEL), const),
                  pl.BlockSpec((LANES, D_MODEL), const),
                  pl.BlockSpec((N_EXPERTS, 1), const)],
        out_specs=[pl.BlockSpec((tm, D_MODEL), rows),
                   pl.BlockSpec((tm, D_PACK), rows),
                   pl.BlockSpec((8, tm), lambda i: (0, i)),
                   pl.BlockSpec((8, tm), lambda i: (0, i)),
                   pl.BlockSpec((8, tm), lambda i: (0, i)),
                   pl.BlockSpec((N_EXPERTS, LANES), const)],
        out_shape=[jax.ShapeDtypeStruct((t, D_MODEL), F32),
                   jax.ShapeDtypeStruct((t, D_PACK), jnp.uint32),
                   jax.ShapeDtypeStruct((8, t), jnp.int32),
                   jax.ShapeDtypeStruct((8, t), F32),
                   jax.ShapeDtypeStruct((8, t), jnp.int32),
                   jax.ShapeDtypeStruct((N_EXPERTS, LANES), F32)],
        scratch_shapes=[pltpu.VMEM((N_EXPERTS, 1), F32)],
        compiler_params=pltpu.CompilerParams(
            dimension_semantics=("arbitrary",), vmem_limit_bytes=VMEM_LIMIT),
        name="outproj",
    )(x2d, y_r, y_f, gate1, shift2, scale2, norm_g, wo_r, wo_f, w_rt_hi, w_rt_lo, b_rt)


SC_CORES = 2
SC_SUBCORES = 16
SC_ROWS = 64


def _sc_gather_rows(idx, src):
    n_workers = SC_CORES * SC_SUBCORES
    m = idx.shape[0]
    d = src.shape[1]
    assert m % (n_workers * SC_ROWS) == 0
    n_chunks = m // (n_workers * SC_ROWS)
    mesh = plsc.VectorSubcoreMesh(core_axis_name="c", subcore_axis_name="s")

    @functools.partial(
        pl.kernel, mesh=mesh,
        out_type=jax.ShapeDtypeStruct((m, d), src.dtype),
        scratch_types=[pltpu.VMEM((n_chunks, SC_ROWS), jnp.int32),
                       pltpu.VMEM((SC_ROWS, d), src.dtype),
                       pltpu.SemaphoreType.DMA],
        name="sc_gather")
    def gather(src_hbm, idx_hbm, out_hbm, idx_v, rows_v, sem):
        wid = lax.axis_index("s") * SC_CORES + lax.axis_index("c")
        pltpu.sync_copy(idx_hbm.at[wid], idx_v)

        @pl.loop(0, n_chunks)
        def _(j):
            pltpu.async_copy(src_hbm.at[idx_v.at[j]], rows_v, sem).wait()
            pltpu.sync_copy(rows_v, out_hbm.at[pl.ds((wid * n_chunks + j) * SC_ROWS, SC_ROWS)])

    return gather(src, idx.reshape(n_workers, n_chunks, SC_ROWS))


def _sc_scatter_rows(src, dest, n_out):
    n_workers = SC_CORES * SC_SUBCORES
    t, d = src.shape
    n_slot = dest.shape[0]
    assert t % (n_workers * SC_ROWS) == 0
    n_chunks = t // (n_workers * SC_ROWS)
    mesh = plsc.VectorSubcoreMesh(core_axis_name="c", subcore_axis_name="s")
    idx = dest.reshape(n_slot, n_workers, n_chunks, SC_ROWS).transpose(1, 2, 0, 3)
    idx = idx.reshape(n_workers, n_chunks * n_slot, SC_ROWS)

    @functools.partial(
        pl.kernel, mesh=mesh,
        out_type=jax.ShapeDtypeStruct((n_out, d), src.dtype),
        scratch_types=[pltpu.VMEM((n_chunks * n_slot, SC_ROWS), jnp.int32),
                       pltpu.VMEM((SC_ROWS, d), src.dtype)],
        name="sc_scatter")
    def scatter(src_hbm, idx_hbm, out_hbm, idx_v, rows_v):
        wid = lax.axis_index("s") * SC_CORES + lax.axis_index("c")
        pltpu.sync_copy(idx_hbm.at[wid], idx_v)

        @pl.loop(0, n_chunks)
        def _(j):
            pltpu.sync_copy(src_hbm.at[pl.ds((wid * n_chunks + j) * SC_ROWS, SC_ROWS)], rows_v)
            for k in range(n_slot):
                pltpu.sync_copy(rows_v, out_hbm.at[idx_v.at[j * n_slot + k]])

    return scatter(src, idx)


def _expert_kernel(be_ref, nv_ref, x_ref, wgu_ref, bgu_ref, wd_ref, bd_ref, o_ref):
    del be_ref
    n_valid = nv_ref[pl.program_id(0)]
    half = EXPERT_BLOCK // 2
    for r in range(2):
        @pl.when(n_valid > r * half)
        def _(r=r):
            rows = pl.ds(r * half, half)
            valid = _iota((half, 1), 0) + r * half < n_valid
            lo, hi = _unpack_rows(jnp.where(valid, x_ref[rows, :], jnp.uint32(0)))
            x = jnp.concatenate([lo.astype(BF16), hi.astype(BF16)], axis=1)
            gu = jnp.dot(x, wgu_ref[...], preferred_element_type=F32) + bgu_ref[...]
            gate = jnp.minimum(gu[:, :D_MODEL], SWIGLU_LIMIT)
            up = jnp.clip(gu[:, D_MODEL:], -SWIGLU_LIMIT, SWIGLU_LIMIT)
            act = gate * _sigmoid(SWIGLU_ALPHA * gate) * (up + 1.0)
            o_ref[rows, :] = _pack_rows(
                jnp.dot(act.astype(BF16), wd_ref[...], preferred_element_type=F32) + bd_ref[...])


def _experts(block_e, n_valid, xs, w_gu, b_gu, w_d, b_d):
    n_blocks = block_e.shape[0]
    grid_spec = pltpu.PrefetchScalarGridSpec(
        num_scalar_prefetch=2,
        grid=(n_blocks,),
        in_specs=[pl.BlockSpec((EXPERT_BLOCK, D_PACK), lambda j, be, nv: (j, 0)),
                  pl.BlockSpec((None, D_MODEL, 2 * D_MODEL), lambda j, be, nv: (be[j], 0, 0)),
                  pl.BlockSpec((None, 1, 2 * D_MODEL), lambda j, be, nv: (be[j], 0, 0)),
                  pl.BlockSpec((None, D_MODEL, D_MODEL), lambda j, be, nv: (be[j], 0, 0)),
                  pl.BlockSpec((None, 1, D_MODEL), lambda j, be, nv: (be[j], 0, 0))],
        out_specs=pl.BlockSpec((EXPERT_BLOCK, D_PACK), lambda j, be, nv: (j, 0)),
    )
    return pl.pallas_call(
        _expert_kernel,
        grid_spec=grid_spec,
        out_shape=jax.ShapeDtypeStruct(xs.shape, jnp.uint32),
        compiler_params=pltpu.CompilerParams(
            dimension_semantics=("arbitrary",), vmem_limit_bytes=VMEM_LIMIT),
        name="experts",
    )(block_e, n_valid, xs, w_gu, b_gu, w_d, b_d)


COMBINE_TOKENS = 512
MOE_SPLITS = 2


def _combine_kernel(yg_ref, x1_ref, gate_ref, g2_ref, fg_ref, o_ref):
    gates = gate_ref[...].T
    acc_lo = acc_hi = None
    for kk in range(TOP_K):
        lo, hi = _unpack_rows(yg_ref[kk * COMBINE_TOKENS:(kk + 1) * COMBINE_TOKENS, :])
        g = gates[:, kk:kk + 1]
        acc_lo = g * lo if acc_lo is None else acc_lo + g * lo
        acc_hi = g * hi if acc_hi is None else acc_hi + g * hi
    x2 = x1_ref[...] + g2_ref[...] * jnp.concatenate([acc_lo, acc_hi], axis=1)
    o_ref[...] = x2 * lax.rsqrt(jnp.mean(x2 * x2, axis=-1, keepdims=True) + NORM_EPS) * fg_ref[...]


def _combine_kernel_into(prev_ref, *refs):
    del prev_ref
    _combine_kernel(*refs)


def _combine(yg, x1, gates, gate2, final_g, seq, row0, t_total, prev):
    t = x1.shape[0]
    tm = COMBINE_TOKENS
    per_b = seq // tm
    blk0 = row0 // tm
    rows = lambda i: (i, 0)
    in_specs = [pl.BlockSpec((TOP_K * tm, D_PACK), rows),
                pl.BlockSpec((tm, D_MODEL), rows),
                pl.BlockSpec((8, tm), lambda i: (0, i)),
                pl.BlockSpec((None, 1, D_MODEL), lambda i: ((i + blk0) // per_b, 0, 0)),
                pl.BlockSpec((1, D_MODEL), lambda i: (0, 0))]
    args = (yg, x1, gates, gate2, final_g)
    if prev is not None:
        in_specs = [pl.BlockSpec(memory_space=pl.ANY)] + in_specs
        args = (prev,) + args
    return pl.pallas_call(
        _combine_kernel if prev is None else _combine_kernel_into,
        grid=(t // tm,),
        in_specs=in_specs,
        out_specs=pl.BlockSpec((tm, D_MODEL), lambda i: (i + blk0, 0)),
        out_shape=jax.ShapeDtypeStruct((t_total, D_MODEL), F32),
        input_output_aliases={} if prev is None else {0: 0},
        compiler_params=pltpu.CompilerParams(
            dimension_semantics=("parallel",), vmem_limit_bytes=VMEM_LIMIT),
        name="combine",
    )(*args)


def _moe(h2, idx, gates, rank, counts, x1, gate2, final_g, w_gu, b_gu, w_d, b_d, seq,
         row0, t_total, prev):
    t = h2.shape[0]
    n_slots = t * TOP_K
    n_blocks = -(-n_slots // EXPERT_BLOCK) + N_EXPERTS
    cap = n_blocks * EXPERT_BLOCK
    padded = (counts + EXPERT_BLOCK - 1) // EXPERT_BLOCK * EXPERT_BLOCK
    pad_ends = jnp.cumsum(padded)
    pad_starts = pad_ends - padded
    experts = jnp.arange(N_EXPERTS, dtype=jnp.int32)
    dest = jnp.sum(jnp.where(idx[..., None] == experts, pad_starts, 0), axis=-1) + rank
    block_starts = jnp.arange(n_blocks, dtype=jnp.int32) * EXPERT_BLOCK
    block_e = jnp.minimum(jnp.sum(block_starts[:, None] >= pad_ends[None, :], axis=1),
                          N_EXPERTS - 1).astype(jnp.int32)
    n_valid = jnp.clip(counts[block_e] - (block_starts - pad_starts[block_e]), 0, EXPERT_BLOCK)

    xs = _sc_scatter_rows(h2, dest, cap)
    yb = _experts(block_e, n_valid.astype(jnp.int32), xs, w_gu, b_gu, w_d, b_d)
    dest_blocks = dest.reshape(TOP_K, -1, COMBINE_TOKENS).transpose(1, 0, 2).reshape(-1)
    yg = _sc_gather_rows(dest_blocks, yb)
    return _combine(yg, x1, gates, gate2, final_g, seq, row0, t_total, prev)


def _layer(x, c_mod, norm1_g, w_in, mu_shift, w0, w2, a0, a2, g2, k_k, k_a, r_k, gn_w, gn_b, b_f,
           q_norm_g, k_norm_g, o_norm_g, w_out, norm2_g, w_router, b_router, w_gate_up,
           b_gate_up, w_down, b_down, final_g, tm_in, tm_out):
    bsz, seq, _ = x.shape
    shift1, scale1, gate1, shift2, scale2, gate2 = (
        m.reshape(bsz, 1, D_MODEL) for m in jnp.split(c_mod, 6, axis=-1))
    row = lambda v: v.reshape(1, -1)

    w_r = w_in[:, :RWKV_COLS].astype(BF16)
    w_x = w_in[:, RWKV_COLS:RWKV_COLS + FOX_MAIN].astype(BF16)
    w_f = w_in[:, RWKV_COLS + FOX_MAIN:].T
    b_fp = jnp.pad(b_f, (0, LANES - N_HEADS)).reshape(1, LANES)
    qk_gain = jnp.concatenate([jnp.tile(q_norm_g, N_HEADS) * (HEAD_DIM ** -0.5 * LOG2_E),
                               jnp.tile(k_norm_g, N_HEADS)]).reshape(1, -1)
    p_r, p_x, k_bias, q_bias = _inproj(x, shift1, scale1, row(norm1_g), w_r, w_x, w_f, b_fp,
                                       qk_gain, tm_in)

    zeros = jnp.zeros((LANES - 64, D_GRP), F32)
    w2p = jnp.concatenate([w2, zeros], axis=0).astype(BF16)
    a2p = jnp.concatenate([zeros, a2], axis=0).astype(BF16)
    y_r, w_gu, w_d = _rwkv(p_r, row(mu_shift), row(w0), w2p, row(a0), a2p, g2.astype(BF16),
                           row(k_k), row(k_a), row(r_k), row(gn_w), row(gn_b), w_gate_up, w_down)

    y_f = _fox(p_x, k_bias, q_bias, jnp.tile(o_norm_g, 2).reshape(1, LANES))

    t = bsz * seq
    w_rt = jnp.pad(w_router.T, ((0, LANES - N_EXPERTS), (0, 0)))
    b_rt = b_router.reshape(N_EXPERTS, 1)
    wo = w_out.astype(BF16)
    b_gu, b_d = b_gate_up.reshape(N_EXPERTS, 1, -1), b_down.reshape(N_EXPERTS, 1, -1)
    t_part = t // MOE_SPLITS
    out = None
    for part in range(MOE_SPLITS):
        row0 = part * t_part
        x1, h2, idx, gates, rank, cnt = _outproj(
            x.reshape(t, D_MODEL), y_r.reshape(t, D_GRP), y_f.reshape(t, D_GRP), gate1, shift2,
            scale2, row(norm2_g), wo[:D_GRP], wo[D_GRP:], w_rt, b_rt, tm_out, seq, row0, t_part)
        counts = cnt[:, 0].astype(jnp.int32)
        out = _moe(h2, idx[:TOP_K], gates, rank[:TOP_K], counts, x1, gate2, row(final_g),
                   w_gu, b_gu, w_d, b_d, seq, row0, t, out)
    return out.reshape(bsz, seq, D_MODEL)


def kernel(x, c, w_ada, b_ada, norm1_g, w_in, mu_shift, w0, w2, a0, a2, g2, k_k, k_a, r_k, gn_w,
           gn_b, b_f, q_norm_g, k_norm_g, o_norm_g, w_out, norm2_g, w_router, b_router, w_gate_up,
           b_gate_up, w_down, b_down, final_g):
    assert w_ada.shape[0] == 1, "single-layer block"
    c_mod = _adaln(c, w_ada[0], b_ada[0])
    return _layer(x, c_mod, norm1_g[0], w_in[0], mu_shift[0], w0[0], w2[0], a0[0], a2[0], g2[0],
                  k_k[0], k_a[0], r_k[0], gn_w[0], gn_b[0], b_f[0], q_norm_g[0], k_norm_g[0],
                  o_norm_g[0], w_out[0], norm2_g[0], w_router[0], b_router[0], w_gate_up[0],
                  b_gate_up[0], w_down[0], b_down[0], final_g,
                  tm_in=min(512, x.shape[1]), tm_out=min(1024, x.shape[1]))
```

```python
import functools

import jax
import jax.numpy as jnp
from jax import lax
from jax.experimental import pallas as pl
from jax.experimental.pallas import tpu as pltpu
from jax.experimental.pallas import tpu_sc as plsc

F32 = jnp.float32
BF16 = jnp.bfloat16
HIGHEST = lax.Precision.HIGHEST

D_MODEL = 1024
HEAD_DIM = 64
N_HEADS = 8
D_GRP = N_HEADS * HEAD_DIM
RWKV_COLS = 1792
LORA_OFF = 3 * D_GRP
GATE_OFF = LORA_OFF + 128
FOX_MAIN = 4 * D_GRP
N_EXPERTS = 32
TOP_K = 4
EXPERT_BLOCK = 512
SWIGLU_ALPHA = 1.702
SWIGLU_LIMIT = 7.0
NORM_EPS = 1e-6
GN_EPS = 64e-5
LOG2_E = 1.4426950408889634
LANES = 128
CHUNK = 64
FOX_SUB_KEYS = 512
HEADS_PER_SCAN = 4
SCAN_W = HEADS_PER_SCAN * HEAD_DIM
SEG_TERMS = 1
CUM_TERMS = 2
VMEM_LIMIT = 56 * 1024 * 1024


def _dot(a, b):
    return jnp.dot(a.astype(BF16), b.astype(BF16), preferred_element_type=F32)


def _fdot(a, b):
    return jnp.dot(a, b, precision=HIGHEST, preferred_element_type=F32)


def _split_dot(x, m, terms):
    acc = None
    rem = x
    for _ in range(terms):
        part = rem.astype(BF16)
        rem = rem - part.astype(F32)
        d = jnp.dot(part, m, preferred_element_type=F32)
        acc = d if acc is None else acc + d
    return acc


def _tri_dot(m, x, terms):
    acc = None
    rem = x
    for _ in range(terms):
        part = rem.astype(BF16)
        rem = rem - part.astype(F32)
        d = jnp.dot(m, part, preferred_element_type=F32)
        acc = d if acc is None else acc + d
    return acc


def _iota(shape, dim):
    return lax.broadcasted_iota(jnp.int32, shape, dim)


def _seg_reduce_mat(n):
    return (_iota((n, LANES), 0) // HEAD_DIM == _iota((n, LANES), 1)).astype(BF16)


def _seg_expand_mat(n):
    return (_iota((LANES, n), 1) // HEAD_DIM == _iota((LANES, n), 0)).astype(BF16)


D_PACK = D_MODEL // 2


def _pack_rows(x):
    lo = lax.bitcast_convert_type(x[:, :D_PACK].astype(BF16).astype(F32), jnp.uint32)
    hi = lax.bitcast_convert_type(x[:, D_PACK:].astype(BF16).astype(F32), jnp.uint32)
    return hi | (lo >> 16)


def _unpack_rows(p):
    lo = lax.bitcast_convert_type(p << 16, F32)
    hi = lax.bitcast_convert_type(p & jnp.uint32(0xFFFF0000), F32)
    return lo, hi


def _log_sigmoid(z):
    return jnp.minimum(z, 0.0) - jnp.log(1.0 + jnp.exp(-jnp.abs(z)))


def _sigmoid(z):
    return 1.0 / (1.0 + jnp.exp(-z))


def _adaln_kernel(c_ref, w_ref, b_ref, o_ref):
    c = c_ref[...]
    o_ref[...] = _fdot(c * _sigmoid(c), w_ref[...]) + b_ref[...]


def _adaln(c, w_ada, b_ada):
    bsz = c.shape[0]
    n_mod = w_ada.shape[1] // D_MODEL
    return pl.pallas_call(
        _adaln_kernel,
        grid=(n_mod,),
        in_specs=[pl.BlockSpec((bsz, D_MODEL), lambda j: (0, 0)),
                  pl.BlockSpec((D_MODEL, D_MODEL), lambda j: (0, j)),
                  pl.BlockSpec((1, D_MODEL), lambda j: (0, j))],
        out_specs=pl.BlockSpec((bsz, D_MODEL), lambda j: (0, j)),
        out_shape=jax.ShapeDtypeStruct((bsz, n_mod * D_MODEL), F32),
        name="adaln",
    )(c, w_ada, b_ada.reshape(1, -1))


def _inproj_kernel(x_ref, sh_ref, sc_ref, g_ref, wr_ref, wx_ref, wft_ref, bf_ref, qkg_ref,
                   pr_ref, px_ref, kb_ref, qb_ref, carry_ref):
    @pl.when(pl.program_id(1) == 0)
    def _():
        carry_ref[...] = jnp.zeros_like(carry_ref)

    x = x_ref[...]
    tm = x.shape[0]
    h = x * lax.rsqrt(jnp.mean(x * x, axis=-1, keepdims=True) + NORM_EPS) * g_ref[...]
    h = h * (1.0 + sc_ref[...]) + sh_ref[...]
    hb = h.astype(BF16)

    pr_ref[...] = jnp.dot(hb, wr_ref[...], preferred_element_type=F32).astype(BF16)

    px = jnp.dot(hb, wx_ref[...], preferred_element_type=F32)
    qk = px[:, :2 * D_GRP]
    ss = _split_dot(qk * qk, _seg_reduce_mat(2 * D_GRP), SEG_TERMS)
    inv = lax.rsqrt(ss * (1.0 / HEAD_DIM) + NORM_EPS)
    qk = qk * _split_dot(inv, _seg_expand_mat(2 * D_GRP), SEG_TERMS) * qkg_ref[...]
    px_ref[:, :2 * D_GRP] = qk.astype(BF16)
    px_ref[:, 2 * D_GRP:] = px[:, 2 * D_GRP:].astype(BF16)

    lane = _iota((1, LANES), 1)
    z = jnp.zeros((tm, LANES), F32)
    for hd in range(N_HEADS):
        zh = jnp.sum(h * wft_ref[hd:hd + 1, :], axis=-1, keepdims=True)
        z = jnp.where(lane == hd, zh, z)
    cum = _log_sigmoid(z + bf_ref[...])
    row_id = _iota((tm, 1), 0)
    shift = 1
    while shift < tm:
        cum = cum + jnp.where(row_id >= shift, pltpu.roll(cum, shift, axis=0), 0.0)
        shift *= 2
    cum = cum + carry_ref[...]
    carry_ref[...] = cum[tm - 1:tm, :]

    parts = []
    rem = cum * LOG2_E
    for _ in range(3):
        part = rem.astype(BF16)
        rem = rem - part.astype(F32)
        parts.append(part)
    src, dst = _iota((LANES, LANES), 0), _iota((LANES, LANES), 1)

    def spread(offset):
        return sum(jnp.dot(part, ((dst == 8 * src + offset + t) & (src < N_HEADS)).astype(BF16),
                           preferred_element_type=F32) for t, part in enumerate(parts))

    slot = _iota((1, LANES), 1) % 8
    kb_ref[...] = (jnp.where((slot >= 3) & (slot < 6), 1.0, 0.0) - spread(0)).astype(BF16)
    qb_ref[...] = (jnp.where(slot < 3, 1.0, 0.0) + spread(3)).astype(BF16)


def _inproj(x, shift, scale, g, w_r, w_x, w_f_t, b_f, qk_gain, tm):
    bsz, seq, _ = x.shape
    const = lambda b, s: (0, 0)
    return pl.pallas_call(
        _inproj_kernel,
        grid=(bsz, seq // tm),
        in_specs=[pl.BlockSpec((None, tm, D_MODEL), lambda b, s: (b, s, 0)),
                  pl.BlockSpec((None, 1, D_MODEL), lambda b, s: (b, 0, 0)),
                  pl.BlockSpec((None, 1, D_MODEL), lambda b, s: (b, 0, 0)),
                  pl.BlockSpec((1, D_MODEL), const),
                  pl.BlockSpec((D_MODEL, RWKV_COLS), const),
                  pl.BlockSpec((D_MODEL, FOX_MAIN), const),
                  pl.BlockSpec((N_HEADS, D_MODEL), const),
                  pl.BlockSpec((1, LANES), const),
                  pl.BlockSpec((1, 2 * D_GRP), const)],
        out_specs=[pl.BlockSpec((None, tm, RWKV_COLS), lambda b, s: (b, s, 0)),
                   pl.BlockSpec((None, tm, FOX_MAIN), lambda b, s: (b, s, 0)),
                   pl.BlockSpec((None, tm, LANES), lambda b, s: (b, s, 0)),
                   pl.BlockSpec((None, tm, LANES), lambda b, s: (b, s, 0))],
        out_shape=[jax.ShapeDtypeStruct((bsz, seq, RWKV_COLS), BF16),
                   jax.ShapeDtypeStruct((bsz, seq, FOX_MAIN), BF16),
                   jax.ShapeDtypeStruct((bsz, seq, LANES), BF16),
                   jax.ShapeDtypeStruct((bsz, seq, LANES), BF16)],
        scratch_shapes=[pltpu.VMEM((1, LANES), F32)],
        compiler_params=pltpu.CompilerParams(
            dimension_semantics=("parallel", "arbitrary"), vmem_limit_bytes=VMEM_LIMIT),
        name="inproj",
    )(x, shift, scale, g, w_r, w_x, w_f_t, b_f, qk_gain)


_NN = (((1,), (0,)), ((), ()))
_NT = (((1,), (1,)), ((), ()))
_TN = (((0,), (0,)), ((), ()))
SCAN_N = HEADS_PER_SCAN * CHUNK
BATCH_PER_STEP = 8
INV_LEVELS = 5
M_HEAD, M_STRICT, M_INCL, M_EYE, M_BASE, M_OFF = 0, 1, 2, 3, 4, 5


def _bdot(a, b, dims):
    return lax.dot_general(a, b, dims, preferred_element_type=F32)


def _scan_masks():
    rr, cc = _iota((SCAN_N, SCAN_W), 0), _iota((SCAN_N, SCAN_W), 1)
    ri, ci = _iota((SCAN_N, SCAN_N), 0), _iota((SCAN_N, SCAN_N), 1)
    same = ri // CHUNK == ci // CHUNK
    masks = [rr // CHUNK == cc // HEAD_DIM, same & (ri > ci), same & (ri >= ci), ri == ci,
             (ri // 2 == ci // 2) & (ri > ci)]
    blk = 2
    while blk < CHUNK:
        masks.append((ri // (2 * blk) == ci // (2 * blk)) & (ri // blk != ci // blk) & (ri > ci))
        blk *= 2
    return jnp.stack(masks).astype(BF16)


def _rwkv_kernel(p_ref, masks_ref, mu_ref, w0_ref, w2_ref, a0_ref, a2_ref, g2_ref, kk_ref, ka_ref,
                 rk_ref, gnw_ref, gnb_ref, wgu_ref, wd_ref, o_ref, wgu_bf_ref, wd_bf_ref,
                 last_ref, state_ref):
    wgu_bf_ref[...] = wgu_ref[...].astype(BF16)
    wd_bf_ref[...] = wd_ref[...].astype(BF16)

    @pl.when(pl.program_id(1) == 0)
    def _():
        last_ref[...] = jnp.zeros_like(last_ref)
        state_ref[...] = jnp.zeros_like(state_ref)

    mu, w0, w2, a0, a2, g2, k_k, k_a, r_k, gn_w, gn_b = (
        ref[...] for ref in (mu_ref, w0_ref, w2_ref, a0_ref, a2_ref, g2_ref, kk_ref, ka_ref,
                             rk_ref, gnw_ref, gnb_ref))
    rows = BATCH_PER_STEP * CHUNK
    p = p_ref[...].astype(F32).reshape(rows, RWKV_COLS)
    row_id = _iota((rows, 1), 0)
    prev = pltpu.roll(p, 1, axis=0)
    for bb in range(BATCH_PER_STEP):
        prev = jnp.where(row_id == bb * CHUNK, last_ref[bb], prev)
        last_ref[bb] = p[(bb + 1) * CHUNK - 1:(bb + 1) * CHUNK, :]
    pf = p + mu * (prev - p)
    r = pf[:, 0:D_GRP]
    k = pf[:, D_GRP:2 * D_GRP]
    v = pf[:, 2 * D_GRP:3 * D_GRP]
    lora = pf[:, LORA_OFF:GATE_OFF]
    gd = pf[:, GATE_OFF:RWKV_COLS]

    wlog = w0 + _dot(jnp.tanh(lora), w2)
    neg = -wlog
    softplus = jnp.maximum(neg, 0.0) + jnp.log(1.0 + jnp.exp(-jnp.abs(neg)))
    ld = -jnp.exp(-softplus - 0.5)
    a = _sigmoid(a0 + _dot(lora, a2))
    g = _dot(_sigmoid(gd), g2)

    red, exp_m = _seg_reduce_mat(D_GRP), _seg_expand_mat(D_GRP)
    kk = k * k_k
    n2 = _split_dot(kk * kk, red, SEG_TERMS)
    kk = kk * _split_dot(1.0 / jnp.maximum(jnp.sqrt(n2), 1e-12), exp_m, SEG_TERMS)
    k2 = k * (1.0 + (a - 1.0) * k_a)

    tr, tc = _iota((rows, rows), 0), _iota((rows, rows), 1)
    tri = ((tr >= tc) & (tr // CHUNK == tc // CHUNK)).astype(BF16)
    cl = _tri_dot(tri, ld, CUM_TERMS)
    cl_end = jnp.concatenate(
        [jnp.broadcast_to(cl[(bb + 1) * CHUNK - 1:(bb + 1) * CHUNK, :], (CHUNK, D_GRP))
         for bb in range(BATCH_PER_STEP)], axis=0)
    e_in = jnp.exp(cl)
    e_out = jnp.exp(-cl)
    e_rem = jnp.exp(cl_end - cl)
    p_end = jnp.exp(cl_end)
    kka = kk * a
    ops = [(-kk * jnp.exp(cl - ld)).astype(BF16), (kka * e_out).astype(BF16),
           (k2 * e_out).astype(BF16), (r * e_in).astype(BF16), v.astype(BF16),
           (kka * e_rem).astype(BF16), (k2 * e_rem).astype(BF16)]

    chains = [(bb, grp) for bb in range(BATCH_PER_STEP)
              for grp in range(N_HEADS // HEADS_PER_SCAN)]
    head_mask = masks_ref[M_HEAD]
    strict, incl = masks_ref[M_STRICT], masks_ref[M_INCL]

    def stacked(op, bb, grp):
        part = op[bb * CHUNK:(bb + 1) * CHUNK, grp * SCAN_W:(grp + 1) * SCAN_W]
        return jnp.concatenate([part] * HEADS_PER_SCAN, axis=0) * head_mask

    xs = [[stacked(op, bb, grp) for op in ops] for bb, grp in chains]
    st = [state_ref[bb, grp] for bb, grp in chains]
    sb = [s.astype(BF16) for s in st]
    nab = [_bdot(x[0], x[1], _NT).astype(BF16) for x in xs]
    aak = [_bdot(x[0], x[2], _NT).astype(BF16) * strict for x in xs]
    arb = [_bdot(x[3], x[1], _NT).astype(BF16) * incl for x in xs]
    ark = [_bdot(x[3], x[2], _NT).astype(BF16) * incl for x in xs]
    t_inv = [masks_ref[M_EYE] + n * masks_ref[M_BASE] for n in nab]
    for lvl in range(INV_LEVELS):
        half = [_bdot(t, n * masks_ref[M_OFF + lvl], _NN).astype(BF16) for t, n in zip(t_inv, nab)]
        t_inv = [t + _bdot(h, t, _NN).astype(BF16) for t, h in zip(t_inv, half)]
    rhs = [(_bdot(x[0], s, _NT) + _bdot(k, x[4], _NN)).astype(BF16)
           for x, s, k in zip(xs, sb, aak)]
    sa = [_bdot(t, h, _NN).astype(BF16) for t, h in zip(t_inv, rhs)]
    ys = [_bdot(x[3], s, _NT) + _bdot(b, u, _NN) + _bdot(k, x[4], _NN)
          for x, s, b, u, k in zip(xs, sb, arb, sa, ark)]
    for (bb, grp), x, s, u in zip(chains, xs, st, sa):
        decay = p_end[bb * CHUNK:bb * CHUNK + 1, grp * SCAN_W:(grp + 1) * SCAN_W]
        state_ref[bb, grp] = s * decay + _bdot(u, x[5], _TN) + _bdot(x[4], x[6], _TN)
    ys = [y[0:CHUNK] + y[CHUNK:2 * CHUNK] + y[2 * CHUNK:3 * CHUNK] + y[3 * CHUNK:4 * CHUNK]
          for y in ys]
    n_grp = N_HEADS // HEADS_PER_SCAN
    y = jnp.concatenate([jnp.concatenate(ys[bb * n_grp:(bb + 1) * n_grp], axis=1)
                         for bb in range(BATCH_PER_STEP)], axis=0)

    mean = _split_dot(_split_dot(y, red, SEG_TERMS) * (1.0 / HEAD_DIM), exp_m, SEG_TERMS)
    d = y - mean
    var = _split_dot(d * d, red, SEG_TERMS) * (1.0 / HEAD_DIM)
    yn = d * _split_dot(lax.rsqrt(var + GN_EPS), exp_m, SEG_TERMS) * gn_w + gn_b
    bonus = _split_dot(_split_dot(r * k2 * r_k, red, SEG_TERMS), exp_m, SEG_TERMS) * v
    o_ref[...] = ((yn + bonus) * g).astype(BF16).reshape(BATCH_PER_STEP, CHUNK, D_GRP)


def _rwkv(p_r, mu, w0, w2p, a0, a2p, g2, k_k, k_a, r_k, gn_w, gn_b, w_gate_up, w_down):
    bsz, seq, _ = p_r.shape
    assert bsz % BATCH_PER_STEP == 0
    n_chunk = seq // CHUNK
    n_step = (bsz // BATCH_PER_STEP) * n_chunk
    wgu2d = w_gate_up.reshape(-1, w_gate_up.shape[-1])
    wd2d = w_down.reshape(-1, w_down.shape[-1])
    assert wgu2d.shape[0] % (8 * n_step) == 0 and wd2d.shape[0] == wgu2d.shape[0]
    slab = wgu2d.shape[0] // n_step
    masks = _scan_masks()
    const = lambda b, s: (0, 0)
    step = lambda b, s: (b * n_chunk + s, 0)
    vec = pl.BlockSpec((1, D_GRP), const)
    y, wgu_bf, wd_bf = pl.pallas_call(
        _rwkv_kernel,
        grid=(bsz // BATCH_PER_STEP, n_chunk),
        in_specs=[pl.BlockSpec((BATCH_PER_STEP, CHUNK, RWKV_COLS), lambda b, s: (b, s, 0)),
                  pl.BlockSpec(masks.shape, lambda b, s: (0, 0, 0)),
                  pl.BlockSpec((1, RWKV_COLS), const),
                  vec, pl.BlockSpec((LANES, D_GRP), const),
                  vec, pl.BlockSpec((LANES, D_GRP), const),
                  pl.BlockSpec((LANES, D_GRP), const),
                  vec, vec, vec, vec, vec,
                  pl.BlockSpec((slab, wgu2d.shape[1]), step),
                  pl.BlockSpec((slab, wd2d.shape[1]), step)],
        out_specs=[pl.BlockSpec((BATCH_PER_STEP, CHUNK, D_GRP), lambda b, s: (b, s, 0)),
                   pl.BlockSpec((slab, wgu2d.shape[1]), step),
                   pl.BlockSpec((slab, wd2d.shape[1]), step)],
        out_shape=[jax.ShapeDtypeStruct((bsz, seq, D_GRP), BF16),
                   jax.ShapeDtypeStruct(wgu2d.shape, BF16),
                   jax.ShapeDtypeStruct(wd2d.shape, BF16)],
        scratch_shapes=[pltpu.VMEM((BATCH_PER_STEP, 1, RWKV_COLS), F32),
                        pltpu.VMEM((BATCH_PER_STEP, N_HEADS // HEADS_PER_SCAN, SCAN_W, SCAN_W), F32)],
        compiler_params=pltpu.CompilerParams(
            dimension_semantics=("parallel", "arbitrary"), vmem_limit_bytes=VMEM_LIMIT),
        name="rwkv",
    )(p_r, masks, mu, w0, w2p, a0, a2p, g2, k_k, k_a, r_k, gn_w, gn_b, wgu2d, wd2d)
    return y, wgu_bf.reshape(w_gate_up.shape), wd_bf.reshape(w_down.shape)


def _fox_kernel(q_ref, qb_ref, k_ref, kb_ref, vt_ref, og_ref, ong_ref, o_ref, m_ref, l_ref, acc_ref,
                *, seq):
    hp = pl.program_id(1)
    lane = _iota((1, LANES), 1)
    q = q_ref[...]
    qb = qb_ref[...]
    zero = jnp.zeros_like(q)
    qcat = [jnp.concatenate([jnp.where(lane // HEAD_DIM == hh, q, zero),
                             jnp.where(lane // 8 == hp * 2 + hh, qb, zero)], axis=1)
            for hh in range(2)]
    keys = min(FOX_SUB_KEYS, seq)
    n_sub = seq // keys
    diag = _iota((keys, keys), 1) >= _iota((keys, keys), 0)

    m_ref[...] = jnp.full(m_ref.shape, -jnp.inf, F32)
    l_ref[...] = jnp.zeros(l_ref.shape, F32)
    acc_ref[...] = jnp.zeros(acc_ref.shape, F32)

    def scores(s):
        lo = s * keys
        kcat = jnp.concatenate([k_ref[lo:lo + keys, :], kb_ref[lo:lo + keys, :]], axis=1)
        return [lax.dot_general(kcat, qc[lo:, :], _NT, preferred_element_type=F32)
                for qc in qcat]

    pending = scores(0)
    for s in range(n_sub):
        lo = s * keys
        nxt = scores(s + 1) if s + 1 < n_sub else None
        vt = vt_ref[:, lo:lo + keys]
        sts = [jnp.concatenate([jnp.where(diag, st[:, :keys], -jnp.inf), st[:, keys:]], axis=1)
               if st.shape[1] > keys else jnp.where(diag, st, -jnp.inf) for st in pending]
        m_old = [m_ref[hh, :, lo:] for hh in range(2)]
        m_new = [jnp.maximum(m, jnp.max(st, axis=0, keepdims=True)) for m, st in zip(m_old, sts)]
        pts = [jnp.exp2(st - m) for st, m in zip(sts, m_new)]
        pvs = [jnp.dot(vt, pt.astype(BF16), preferred_element_type=F32) for pt in pts]
        for hh in range(2):
            alpha = jnp.exp2(m_old[hh] - m_new[hh])
            m_ref[hh, :, lo:] = m_new[hh]
            l_ref[hh, :, lo:] = alpha * l_ref[hh, :, lo:] + jnp.sum(pts[hh], axis=0, keepdims=True)
            acc_ref[hh, :, lo:] = (alpha * acc_ref[hh, :, lo:]
                                   + pvs[hh][hh * HEAD_DIM:(hh + 1) * HEAD_DIM, :])
        pending = nxt

    outs = []
    for hh in range(2):
        o = acc_ref[hh] / l_ref[hh]
        outs.append(o * lax.rsqrt(jnp.mean(o * o, axis=0, keepdims=True) + NORM_EPS))
    o = jnp.concatenate(outs, axis=0).T
    o_ref[...] = (o * ong_ref[...] * _sigmoid(og_ref[...].astype(F32))).astype(BF16)


def _fox(p_x, k_bias, q_bias, o_gain):
    bsz, seq, _ = p_x.shape
    npair = N_HEADS // 2
    v_t = jnp.transpose(p_x[:, :, 2 * D_GRP:3 * D_GRP], (0, 2, 1))
    return pl.pallas_call(
        functools.partial(_fox_kernel, seq=seq),
        grid=(bsz, npair),
        in_specs=[pl.BlockSpec((None, seq, LANES), lambda b, h: (b, 0, h)),
                  pl.BlockSpec((None, seq, LANES), lambda b, h: (b, 0, 0)),
                  pl.BlockSpec((None, seq, LANES), lambda b, h: (b, 0, npair + h)),
                  pl.BlockSpec((None, seq, LANES), lambda b, h: (b, 0, 0)),
                  pl.BlockSpec((None, LANES, seq), lambda b, h: (b, h, 0)),
                  pl.BlockSpec((None, seq, LANES), lambda b, h: (b, 0, 3 * npair + h)),
                  pl.BlockSpec((1, LANES), lambda b, h: (0, 0))],
        out_specs=pl.BlockSpec((None, seq, LANES), lambda b, h: (b, 0, h)),
        out_shape=jax.ShapeDtypeStruct((bsz, seq, D_GRP), BF16),
        scratch_shapes=[pltpu.VMEM((2, 1, seq), F32), pltpu.VMEM((2, 1, seq), F32),
                        pltpu.VMEM((2, HEAD_DIM, seq), F32)],
        compiler_params=pltpu.CompilerParams(
            dimension_semantics=("parallel", "parallel"), vmem_limit_bytes=VMEM_LIMIT),
        name="fox",
    )(p_x, q_bias, p_x, k_bias, v_t, p_x, o_gain)


def _outproj_kernel(x_ref, yr_ref, yf_ref, g1_ref, sh_ref, sc_ref, ng_ref, wor_ref, wof_ref,
                    wrt_ref, wrl_ref, brt_ref, x1_ref, h2_ref, idx_ref, gate_ref, rank_ref, cnt_ref,
                    carry_ref):
    @pl.when(pl.program_id(0) == 0)
    def _():
        carry_ref[...] = jnp.zeros_like(carry_ref)

    y = (jnp.dot(yr_ref[...], wor_ref[...], preferred_element_type=F32)
         + jnp.dot(yf_ref[...], wof_ref[...], preferred_element_type=F32))
    x1 = x_ref[...] + g1_ref[...] * y
    x1_ref[...] = x1
    tm = x1.shape[0]
    h = x1 * lax.rsqrt(jnp.mean(x1 * x1, axis=-1, keepdims=True) + NORM_EPS) * ng_ref[...]
    h2 = h * (1.0 + sc_ref[...]) + sh_ref[...]
    h2_ref[...] = _pack_rows(h2)

    h_hi = h2.astype(BF16)
    h_lo = (h2 - h_hi.astype(F32)).astype(BF16)
    logits = (lax.dot_general(wrt_ref[...], h_hi, _NT, preferred_element_type=F32)
              + lax.dot_general(wrt_ref[...], h_lo, _NT, preferred_element_type=F32)
              + lax.dot_general(wrl_ref[...], h_hi, _NT, preferred_element_type=F32))
    lg = logits[:N_EXPERTS, :] + brt_ref[...]
    expert = _iota((N_EXPERTS, tm), 0)
    picks = []
    hot_sum = jnp.zeros((N_EXPERTS, tm), F32)
    for _ in range(TOP_K):
        m = jnp.max(lg, axis=0, keepdims=True)
        sel = jnp.min(jnp.where(lg == m, expert, N_EXPERTS), axis=0, keepdims=True)
        hot = expert == sel
        picks.append((m, sel, hot))
        hot_sum = hot_sum + hot.astype(F32)
        lg = jnp.where(hot, -jnp.inf, lg)
    es = [jnp.exp(m - picks[0][0]) for m, _, _ in picks]
    den = es[0] + es[1] + es[2] + es[3]

    earlier = (_iota((tm, tm), 0) < _iota((tm, tm), 1)).astype(BF16)
    before = jnp.dot(hot_sum.astype(BF16), earlier, preferred_element_type=F32) + carry_ref[...]
    ranks = [jnp.sum(jnp.where(hot, before, 0.0), axis=0, keepdims=True).astype(jnp.int32)
             for _, _, hot in picks]
    pad_i = jnp.zeros((8 - TOP_K, tm), jnp.int32)
    idx_ref[...] = jnp.concatenate([sel for _, sel, _ in picks] + [pad_i], axis=0)
    gate_ref[...] = jnp.concatenate([e / den for e in es] + [pad_i.astype(F32)], axis=0)
    rank_ref[...] = jnp.concatenate(ranks + [pad_i], axis=0)
    carry_ref[...] = carry_ref[...] + jnp.sum(hot_sum, axis=1, keepdims=True)
    cnt_ref[...] = jnp.broadcast_to(carry_ref[...], cnt_ref.shape)


def _outproj(x2d, y_r, y_f, gate1, shift2, scale2, norm_g, wo_r, wo_f, w_rt, b_rt, tm, seq,
             row0, t):
    w_rt_hi = w_rt.astype(BF16)
    w_rt_lo = (w_rt - w_rt_hi.astype(F32)).astype(BF16)
    per_b = seq // tm
    blk0 = row0 // tm
    const = lambda i: (0, 0)
    rows = lambda i: (i, 0)
    rows_in = lambda i: (i + blk0, 0)
    mod = pl.BlockSpec((None, 1, D_MODEL), lambda i: ((i + blk0) // per_b, 0, 0))
    return pl.pallas_call(
        _outproj_kernel,
        grid=(t // tm,),
        in_specs=[pl.BlockSpec((tm, D_MODEL), rows_in),
                  pl.BlockSpec((tm, D_GRP), rows_in),
                  pl.BlockSpec((tm, D_GRP), rows_in),
                  mod, mod, mod,
                  pl.BlockSpec((1, D_MODEL), const),
                  pl.BlockSpec((D_GRP, D_MODEL), const),
                  pl.BlockSpec((D_GRP, D_MODEL), const),
                  pl.BlockSpec((LANES, D_MODEL), const),
                  pl.BlockSpec((LANES, D_MODEL), const),
                  pl.BlockSpec((N_EXPERTS, 1), const)],
        out_specs=[pl.BlockSpec((tm, D_MODEL), rows),
                   pl.BlockSpec((tm, D_PACK), rows),
                   pl.BlockSpec((8, tm), lambda i: (0, i)),
                   pl.BlockSpec((8, tm), lambda i: (0, i)),
                   pl.BlockSpec((8, tm), lambda i: (0, i)),
                   pl.BlockSpec((N_EXPERTS, LANES), const)],
        out_shape=[jax.ShapeDtypeStruct((t, D_MODEL), F32),
                   jax.ShapeDtypeStruct((t, D_PACK), jnp.uint32),
                   jax.ShapeDtypeStruct((8, t), jnp.int32),
                   jax.ShapeDtypeStruct((8, t), F32),
                   jax.ShapeDtypeStruct((8, t), jnp.int32),
                   jax.ShapeDtypeStruct((N_EXPERTS, LANES), F32)],
        scratch_shapes=[pltpu.VMEM((N_EXPERTS, 1), F32)],
        compiler_params=pltpu.CompilerParams(
            dimension_semantics=("arbitrary",), vmem_limit_bytes=VMEM_LIMIT),
        name="outproj",
    )(x2d, y_r, y_f, gate1, shift2, scale2, norm_g, wo_r, wo_f, w_rt_hi, w_rt_lo, b_rt)


SC_CORES = 2
SC_SUBCORES = 16
SC_ROWS = 64


def _sc_gather_rows(idx, src):
    n_workers = SC_CORES * SC_SUBCORES
    m = idx.shape[0]
    d = src.shape[1]
    assert m % (n_workers * SC_ROWS) == 0
    n_chunks = m // (n_workers * SC_ROWS)
    mesh = plsc.VectorSubcoreMesh(core_axis_name="c", subcore_axis_name="s")

    @functools.partial(
        pl.kernel, mesh=mesh,
        out_type=jax.ShapeDtypeStruct((m, d), src.dtype),
        scratch_types=[pltpu.VMEM((n_chunks, SC_ROWS), jnp.int32),
                       pltpu.VMEM((SC_ROWS, d), src.dtype),
                       pltpu.SemaphoreType.DMA],
        name="sc_gather")
    def gather(src_hbm, idx_hbm, out_hbm, idx_v, rows_v, sem):
        wid = lax.axis_index("s") * SC_CORES + lax.axis_index("c")
        pltpu.sync_copy(idx_hbm.at[wid], idx_v)

        @pl.loop(0, n_chunks)
        def _(j):
            pltpu.async_copy(src_hbm.at[idx_v.at[j]], rows_v, sem).wait()
            pltpu.sync_copy(rows_v, out_hbm.at[pl.ds((wid * n_chunks + j) * SC_ROWS, SC_ROWS)])

    return gather(src, idx.reshape(n_workers, n_chunks, SC_ROWS))


def _sc_scatter_rows(src, dest, n_out):
    n_workers = SC_CORES * SC_SUBCORES
    t, d = src.shape
    n_slot = dest.shape[0]
    assert t % (n_workers * SC_ROWS) == 0
    n_chunks = t // (n_workers * SC_ROWS)
    mesh = plsc.VectorSubcoreMesh(core_axis_name="c", subcore_axis_name="s")
    idx = dest.reshape(n_slot, n_workers, n_chunks, SC_ROWS).transpose(1, 2, 0, 3)
    idx = idx.reshape(n_workers, n_chunks * n_slot, SC_ROWS)

    @functools.partial(
        pl.kernel, mesh=mesh,
        out_type=jax.ShapeDtypeStruct((n_out, d), src.dtype),
        scratch_types=[pltpu.VMEM((n_chunks * n_slot, SC_ROWS), jnp.int32),
                       pltpu.VMEM((SC_ROWS, d), src.dtype)],
        name="sc_scatter")
    def scatter(src_hbm, idx_hbm, out_hbm, idx_v, rows_v):
        wid = lax.axis_index("s") * SC_CORES + lax.axis_index("c")
        pltpu.sync_copy(idx_hbm.at[wid], idx_v)

        @pl.loop(0, n_chunks)
        def _(j):
            pltpu.sync_copy(src_hbm.at[pl.ds((wid * n_chunks + j) * SC_ROWS, SC_ROWS)], rows_v)
            for k in range(n_slot):
                pltpu.sync_copy(rows_v, out_hbm.at[idx_v.at[j * n_slot + k]])

    return scatter(src, idx)


def _expert_kernel(be_ref, nv_ref, x_ref, wgu_ref, bgu_ref, wd_ref, bd_ref, o_ref):
    del be_ref
    valid = _iota((EXPERT_BLOCK, 1), 0) < nv_ref[pl.program_id(0)]
    lo, hi = _unpack_rows(jnp.where(valid, x_ref[...], jnp.uint32(0)))
    x = jnp.concatenate([lo.astype(BF16), hi.astype(BF16)], axis=1)
    half = EXPERT_BLOCK // 2
    gus = [jnp.dot(x[r * half:(r + 1) * half], wgu_ref[...], preferred_element_type=F32)
           + bgu_ref[...] for r in range(2)]
    for r, gu in enumerate(gus):
        gate = jnp.minimum(gu[:, :D_MODEL], SWIGLU_LIMIT)
        up = jnp.clip(gu[:, D_MODEL:], -SWIGLU_LIMIT, SWIGLU_LIMIT)
        act = gate * _sigmoid(SWIGLU_ALPHA * gate) * (up + 1.0)
        o_ref[r * half:(r + 1) * half, :] = _pack_rows(
            jnp.dot(act.astype(BF16), wd_ref[...], preferred_element_type=F32) + bd_ref[...])


def _experts(block_e, n_valid, xs, w_gu, b_gu, w_d, b_d):
    n_blocks = block_e.shape[0]
    grid_spec = pltpu.PrefetchScalarGridSpec(
        num_scalar_prefetch=2,
        grid=(n_blocks,),
        in_specs=[pl.BlockSpec((EXPERT_BLOCK, D_PACK), lambda j, be, nv: (j, 0)),
                  pl.BlockSpec((None, D_MODEL, 2 * D_MODEL), lambda j, be, nv: (be[j], 0, 0)),
                  pl.BlockSpec((None, 1, 2 * D_MODEL), lambda j, be, nv: (be[j], 0, 0)),
                  pl.BlockSpec((None, D_MODEL, D_MODEL), lambda j, be, nv: (be[j], 0, 0)),
                  pl.BlockSpec((None, 1, D_MODEL), lambda j, be, nv: (be[j], 0, 0))],
        out_specs=pl.BlockSpec((EXPERT_BLOCK, D_PACK), lambda j, be, nv: (j, 0)),
    )
    return pl.pallas_call(
        _expert_kernel,
        grid_spec=grid_spec,
        out_shape=jax.ShapeDtypeStruct(xs.shape, jnp.uint32),
        compiler_params=pltpu.CompilerParams(
            dimension_semantics=("arbitrary",), vmem_limit_bytes=VMEM_LIMIT),
        name="experts",
    )(block_e, n_valid, xs, w_gu, b_gu, w_d, b_d)


COMBINE_TOKENS = 512
MOE_SPLITS = 2


def _combine_kernel(yg_ref, x1_ref, gate_ref, g2_ref, fg_ref, o_ref):
    gates = gate_ref[...].T
    acc_lo = acc_hi = None
    for kk in range(TOP_K):
        lo, hi = _unpack_rows(yg_ref[kk * COMBINE_TOKENS:(kk + 1) * COMBINE_TOKENS, :])
        g = gates[:, kk:kk + 1]
        acc_lo = g * lo if acc_lo is None else acc_lo + g * lo
        acc_hi = g * hi if acc_hi is None else acc_hi + g * hi
    x2 = x1_ref[...] + g2_ref[...] * jnp.concatenate([acc_lo, acc_hi], axis=1)
    o_ref[...] = x2 * lax.rsqrt(jnp.mean(x2 * x2, axis=-1, keepdims=True) + NORM_EPS) * fg_ref[...]


def _combine_kernel_into(prev_ref, *refs):
    del prev_ref
    _combine_kernel(*refs)


def _combine(yg, x1, gates, gate2, final_g, seq, row0, t_total, prev):
    t = x1.shape[0]
    tm = COMBINE_TOKENS
    per_b = seq // tm
    blk0 = row0 // tm
    rows = lambda i: (i, 0)
    in_specs = [pl.BlockSpec((TOP_K * tm, D_PACK), rows),
                pl.BlockSpec((tm, D_MODEL), rows),
                pl.BlockSpec((8, tm), lambda i: (0, i)),
                pl.BlockSpec((None, 1, D_MODEL), lambda i: ((i + blk0) // per_b, 0, 0)),
                pl.BlockSpec((1, D_MODEL), lambda i: (0, 0))]
    args = (yg, x1, gates, gate2, final_g)
    if prev is not None:
        in_specs = [pl.BlockSpec(memory_space=pl.ANY)] + in_specs
        args = (prev,) + args
    return pl.pallas_call(
        _combine_kernel if prev is None else _combine_kernel_into,
        grid=(t // tm,),
        in_specs=in_specs,
        out_specs=pl.BlockSpec((tm, D_MODEL), lambda i: (i + blk0, 0)),
        out_shape=jax.ShapeDtypeStruct((t_total, D_MODEL), F32),
        input_output_aliases={} if prev is None else {0: 0},
        compiler_params=pltpu.CompilerParams(
            dimension_semantics=("parallel",), vmem_limit_bytes=VMEM_LIMIT),
        name="combine",
    )(*args)


def _moe(h2, idx, gates, rank, counts, x1, gate2, final_g, w_gu, b_gu, w_d, b_d, seq,
         row0, t_total, prev):
    t = h2.shape[0]
    n_slots = t * TOP_K
    n_blocks = -(-n_slots // EXPERT_BLOCK) + N_EXPERTS
    cap = n_blocks * EXPERT_BLOCK
    padded = (counts + EXPERT_BLOCK - 1) // EXPERT_BLOCK * EXPERT_BLOCK
    pad_ends = jnp.cumsum(padded)
    pad_starts = pad_ends - padded
    experts = jnp.arange(N_EXPERTS, dtype=jnp.int32)
    dest = jnp.sum(jnp.where(idx[..., None] == experts, pad_starts, 0), axis=-1) + rank
    block_starts = jnp.arange(n_blocks, dtype=jnp.int32) * EXPERT_BLOCK
    block_e = jnp.minimum(jnp.sum(block_starts[:, None] >= pad_ends[None, :], axis=1),
                          N_EXPERTS - 1).astype(jnp.int32)
    n_valid = jnp.clip(counts[block_e] - (block_starts - pad_starts[block_e]), 0, EXPERT_BLOCK)

    xs = _sc_scatter_rows(h2, dest, cap)
    yb = _experts(block_e, n_valid.astype(jnp.int32), xs, w_gu, b_gu, w_d, b_d)
    dest_blocks = dest.reshape(TOP_K, -1, COMBINE_TOKENS).transpose(1, 0, 2).reshape(-1)
    yg = _sc_gather_rows(dest_blocks, yb)
    return _combine(yg, x1, gates, gate2, final_g, seq, row0, t_total, prev)


def _layer(x, c_mod, norm1_g, w_in, mu_shift, w0, w2, a0, a2, g2, k_k, k_a, r_k, gn_w, gn_b, b_f,
           q_norm_g, k_norm_g, o_norm_g, w_out, norm2_g, w_router, b_router, w_gate_up,
           b_gate_up, w_down, b_down, final_g, tm_in, tm_out):
    bsz, seq, _ = x.shape
    shift1, scale1, gate1, shift2, scale2, gate2 = (
        m.reshape(bsz, 1, D_MODEL) for m in jnp.split(c_mod, 6, axis=-1))
    row = lambda v: v.reshape(1, -1)

    w_r = w_in[:, :RWKV_COLS].astype(BF16)
    w_x = w_in[:, RWKV_COLS:RWKV_COLS + FOX_MAIN].astype(BF16)
    w_f = w_in[:, RWKV_COLS + FOX_MAIN:].T
    b_fp = jnp.pad(b_f, (0, LANES - N_HEADS)).reshape(1, LANES)
    qk_gain = jnp.concatenate([jnp.tile(q_norm_g, N_HEADS) * (HEAD_DIM ** -0.5 * LOG2_E),
                               jnp.tile(k_norm_g, N_HEADS)]).reshape(1, -1)
    p_r, p_x, k_bias, q_bias = _inproj(x, shift1, scale1, row(norm1_g), w_r, w_x, w_f, b_fp,
                                       qk_gain, tm_in)

    zeros = jnp.zeros((LANES - 64, D_GRP), F32)
    w2p = jnp.concatenate([w2, zeros], axis=0).astype(BF16)
    a2p = jnp.concatenate([zeros, a2], axis=0).astype(BF16)
    y_r, w_gu, w_d = _rwkv(p_r, row(mu_shift), row(w0), w2p, row(a0), a2p, g2.astype(BF16),
                           row(k_k), row(k_a), row(r_k), row(gn_w), row(gn_b), w_gate_up, w_down)

    y_f = _fox(p_x, k_bias, q_bias, jnp.tile(o_norm_g, 2).reshape(1, LANES))

    t = bsz * seq
    w_rt = jnp.pad(w_router.T, ((0, LANES - N_EXPERTS), (0, 0)))
    b_rt = b_router.reshape(N_EXPERTS, 1)
    wo = w_out.astype(BF16)
    b_gu, b_d = b_gate_up.reshape(N_EXPERTS, 1, -1), b_down.reshape(N_EXPERTS, 1, -1)
    t_part = t // MOE_SPLITS
    out = None
    for part in range(MOE_SPLITS):
        row0 = part * t_part
        x1, h2, idx, gates, rank, cnt = _outproj(
            x.reshape(t, D_MODEL), y_r.reshape(t, D_GRP), y_f.reshape(t, D_GRP), gate1, shift2,
            scale2, row(norm2_g), wo[:D_GRP], wo[D_GRP:], w_rt, b_rt, tm_out, seq, row0, t_part)
        counts = cnt[:, 0].astype(jnp.int32)
        out = _moe(h2, idx[:TOP_K], gates, rank[:TOP_K], counts, x1, gate2, row(final_g),
                   w_gu, b_gu, w_d, b_d, seq, row0, t, out)
    return out.reshape(bsz, seq, D_MODEL)


def kernel(x, c, w_ada, b_ada, norm1_g, w_in, mu_shift, w0, w2, a0, a2, g2, k_k, k_a, r_k, gn_w,
           gn_b, b_f, q_norm_g, k_norm_g, o_norm_g, w_out, norm2_g, w_router, b_router, w_gate_up,
           b_gate_up, w_down, b_down, final_g):
    assert w_ada.shape[0] == 1, "single-layer block"
    c_mod = _adaln(c, w_ada[0], b_ada[0])
    return _layer(x, c_mod, norm1_g[0], w_in[0], mu_shift[0], w0[0], w2[0], a0[0], a2[0], g2[0],
                  k_k[0], k_a[0], r_k[0], gn_w[0], gn_b[0], b_f[0], q_norm_g[0], k_norm_g[0],
                  o_norm_g[0], w_out[0], norm2_g[0], w_router[0], b_router[0], w_gate_up[0],
                  b_gate_up[0], w_down[0], b_down[0], final_g,
                  tm_in=min(512, x.shape[1]), tm_out=min(1024, x.shape[1]))
```

```python
import functools

import jax
import jax.numpy as jnp
from jax import lax
from jax.experimental import pallas as pl
from jax.experimental.pallas import tpu as pltpu
from jax.experimental.pallas import tpu_sc as plsc

F32 = jnp.float32
BF16 = jnp.bfloat16
HIGHEST = lax.Precision.HIGHEST

D_MODEL = 1024
HEAD_DIM = 64
N_HEADS = 8
D_GRP = N_HEADS * HEAD_DIM
RWKV_COLS = 1792
LORA_OFF = 3 * D_GRP
GATE_OFF = LORA_OFF + 128
FOX_MAIN = 4 * D_GRP
N_EXPERTS = 32
TOP_K = 4
EXPERT_BLOCK = 512
SWIGLU_ALPHA = 1.702
SWIGLU_LIMIT = 7.0
NORM_EPS = 1e-6
GN_EPS = 64e-5
LOG2_E = 1.4426950408889634
LANES = 128
CHUNK = 64
FOX_SUB_KEYS = 512
HEADS_PER_SCAN = 4
SCAN_W = HEADS_PER_SCAN * HEAD_DIM
SEG_TERMS = 1
CUM_TERMS = 2
VMEM_LIMIT = 56 * 1024 * 1024


def _dot(a, b):
    return jnp.dot(a.astype(BF16), b.astype(BF16), preferred_element_type=F32)


def _fdot(a, b):
    return jnp.dot(a, b, precision=HIGHEST, preferred_element_type=F32)


def _split_dot(x, m, terms):
    acc = None
    rem = x
    for _ in range(terms):
        part = rem.astype(BF16)
        rem = rem - part.astype(F32)
        d = jnp.dot(part, m, preferred_element_type=F32)
        acc = d if acc is None else acc + d
    return acc


def _tri_dot(m, x, terms):
    acc = None
    rem = x
    for _ in range(terms):
        part = rem.astype(BF16)
        rem = rem - part.astype(F32)
        d = jnp.dot(m, part, preferred_element_type=F32)
        acc = d if acc is None else acc + d
    return acc


def _iota(shape, dim):
    return lax.broadcasted_iota(jnp.int32, shape, dim)


def _seg_reduce_mat(n):
    return (_iota((n, LANES), 0) // HEAD_DIM == _iota((n, LANES), 1)).astype(BF16)


def _seg_expand_mat(n):
    return (_iota((LANES, n), 1) // HEAD_DIM == _iota((LANES, n), 0)).astype(BF16)


D_PACK = D_MODEL // 2


def _pack_rows(x):
    lo = lax.bitcast_convert_type(x[:, :D_PACK].astype(BF16).astype(F32), jnp.uint32)
    hi = lax.bitcast_convert_type(x[:, D_PACK:].astype(BF16).astype(F32), jnp.uint32)
    return hi | (lo >> 16)


def _unpack_rows(p):
    lo = lax.bitcast_convert_type(p << 16, F32)
    hi = lax.bitcast_convert_type(p & jnp.uint32(0xFFFF0000), F32)
    return lo, hi


def _log_sigmoid(z):
    return jnp.minimum(z, 0.0) - jnp.log(1.0 + jnp.exp(-jnp.abs(z)))


def _sigmoid(z):
    return 1.0 / (1.0 + jnp.exp(-z))


def _adaln_kernel(c_ref, w_ref, b_ref, o_ref):
    c = c_ref[...]
    o_ref[...] = _fdot(c * _sigmoid(c), w_ref[...]) + b_ref[...]


def _adaln(c, w_ada, b_ada):
    bsz = c.shape[0]
    n_mod = w_ada.shape[1] // D_MODEL
    return pl.pallas_call(
        _adaln_kernel,
        grid=(n_mod,),
        in_specs=[pl.BlockSpec((bsz, D_MODEL), lambda j: (0, 0)),
                  pl.BlockSpec((D_MODEL, D_MODEL), lambda j: (0, j)),
                  pl.BlockSpec((1, D_MODEL), lambda j: (0, j))],
        out_specs=pl.BlockSpec((bsz, D_MODEL), lambda j: (0, j)),
        out_shape=jax.ShapeDtypeStruct((bsz, n_mod * D_MODEL), F32),
        name="adaln",
    )(c, w_ada, b_ada.reshape(1, -1))


def _inproj_kernel(x_ref, sh_ref, sc_ref, g_ref, wr_ref, wx_ref, wft_ref, bf_ref, qkg_ref,
                   pr_ref, px_ref, kb_ref, qb_ref, carry_ref):
    @pl.when(pl.program_id(1) == 0)
    def _():
        carry_ref[...] = jnp.zeros_like(carry_ref)

    x = x_ref[...]
    tm = x.shape[0]
    h = x * lax.rsqrt(jnp.mean(x * x, axis=-1, keepdims=True) + NORM_EPS) * g_ref[...]
    h = h * (1.0 + sc_ref[...]) + sh_ref[...]
    hb = h.astype(BF16)

    pr_ref[...] = jnp.dot(hb, wr_ref[...], preferred_element_type=F32).astype(BF16)

    px = jnp.dot(hb, wx_ref[...], preferred_element_type=F32)
    qk = px[:, :2 * D_GRP]
    ss = _split_dot(qk * qk, _seg_reduce_mat(2 * D_GRP), SEG_TERMS)
    inv = lax.rsqrt(ss * (1.0 / HEAD_DIM) + NORM_EPS)
    qk = qk * _split_dot(inv, _seg_expand_mat(2 * D_GRP), SEG_TERMS) * qkg_ref[...]
    px_ref[:, :2 * D_GRP] = qk.astype(BF16)
    px_ref[:, 2 * D_GRP:] = px[:, 2 * D_GRP:].astype(BF16)

    lane = _iota((1, LANES), 1)
    z = jnp.zeros((tm, LANES), F32)
    for hd in range(N_HEADS):
        zh = jnp.sum(h * wft_ref[hd:hd + 1, :], axis=-1, keepdims=True)
        z = jnp.where(lane == hd, zh, z)
    cum = _log_sigmoid(z + bf_ref[...])
    row_id = _iota((tm, 1), 0)
    shift = 1
    while shift < tm:
        cum = cum + jnp.where(row_id >= shift, pltpu.roll(cum, shift, axis=0), 0.0)
        shift *= 2
    cum = cum + carry_ref[...]
    carry_ref[...] = cum[tm - 1:tm, :]

    parts = []
    rem = cum * LOG2_E
    for _ in range(3):
        part = rem.astype(BF16)
        rem = rem - part.astype(F32)
        parts.append(part)
    src, dst = _iota((LANES, LANES), 0), _iota((LANES, LANES), 1)

    def spread(offset):
        return sum(jnp.dot(part, ((dst == 8 * src + offset + t) & (src < N_HEADS)).astype(BF16),
                           preferred_element_type=F32) for t, part in enumerate(parts))

    slot = _iota((1, LANES), 1) % 8
    kb_ref[...] = (jnp.where((slot >= 3) & (slot < 6), 1.0, 0.0) - spread(0)).astype(BF16)
    qb_ref[...] = (jnp.where(slot < 3, 1.0, 0.0) + spread(3)).astype(BF16)


def _inproj(x, shift, scale, g, w_r, w_x, w_f_t, b_f, qk_gain, tm):
    bsz, seq, _ = x.shape
    const = lambda b, s: (0, 0)
    return pl.pallas_call(
        _inproj_kernel,
        grid=(bsz, seq // tm),
        in_specs=[pl.BlockSpec((None, tm, D_MODEL), lambda b, s: (b, s, 0)),
                  pl.BlockSpec((None, 1, D_MODEL), lambda b, s: (b, 0, 0)),
                  pl.BlockSpec((None, 1, D_MODEL), lambda b, s: (b, 0, 0)),
                  pl.BlockSpec((1, D_MODEL), const),
                  pl.BlockSpec((D_MODEL, RWKV_COLS), const),
                  pl.BlockSpec((D_MODEL, FOX_MAIN), const),
                  pl.BlockSpec((N_HEADS, D_MODEL), const),
                  pl.BlockSpec((1, LANES), const),
                  pl.BlockSpec((1, 2 * D_GRP), const)],
        out_specs=[pl.BlockSpec((None, tm, RWKV_COLS), lambda b, s: (b, s, 0)),
                   pl.BlockSpec((None, tm, FOX_MAIN), lambda b, s: (b, s, 0)),
                   pl.BlockSpec((None, tm, LANES), lambda b, s: (b, s, 0)),
                   pl.BlockSpec((None, tm, LANES), lambda b, s: (b, s, 0))],
        out_shape=[jax.ShapeDtypeStruct((bsz, seq, RWKV_COLS), BF16),
                   jax.ShapeDtypeStruct((bsz, seq, FOX_MAIN), BF16),
                   jax.ShapeDtypeStruct((bsz, seq, LANES), BF16),
                   jax.ShapeDtypeStruct((bsz, seq, LANES), BF16)],
        scratch_shapes=[pltpu.VMEM((1, LANES), F32)],
        compiler_params=pltpu.CompilerParams(
            dimension_semantics=("parallel", "arbitrary"), vmem_limit_bytes=VMEM_LIMIT),
        name="inproj",
    )(x, shift, scale, g, w_r, w_x, w_f_t, b_f, qk_gain)


_NN = (((1,), (0,)), ((), ()))
_NT = (((1,), (1,)), ((), ()))
_TN = (((0,), (0,)), ((), ()))
SCAN_N = HEADS_PER_SCAN * CHUNK
BATCH_PER_STEP = 8
INV_LEVELS = 5
M_HEAD, M_STRICT, M_INCL, M_EYE, M_BASE, M_OFF = 0, 1, 2, 3, 4, 5


def _bdot(a, b, dims):
    return lax.dot_general(a, b, dims, preferred_element_type=F32)


def _scan_masks():
    rr, cc = _iota((SCAN_N, SCAN_W), 0), _iota((SCAN_N, SCAN_W), 1)
    ri, ci = _iota((SCAN_N, SCAN_N), 0), _iota((SCAN_N, SCAN_N), 1)
    same = ri // CHUNK == ci // CHUNK
    masks = [rr // CHUNK == cc // HEAD_DIM, same & (ri > ci), same & (ri >= ci), ri == ci,
             (ri // 2 == ci // 2) & (ri > ci)]
    blk = 2
    while blk < CHUNK:
        masks.append((ri // (2 * blk) == ci // (2 * blk)) & (ri // blk != ci // blk) & (ri > ci))
        blk *= 2
    return jnp.stack(masks).astype(BF16)


def _rwkv_kernel(p_ref, masks_ref, mu_ref, w0_ref, w2_ref, a0_ref, a2_ref, g2_ref, kk_ref, ka_ref,
                 rk_ref, gnw_ref, gnb_ref, wgu_ref, wd_ref, o_ref, wgu_bf_ref, wd_bf_ref,
                 last_ref, state_ref):
    wgu_bf_ref[...] = wgu_ref[...].astype(BF16)
    wd_bf_ref[...] = wd_ref[...].astype(BF16)

    @pl.when(pl.program_id(1) == 0)
    def _():
        last_ref[...] = jnp.zeros_like(last_ref)
        state_ref[...] = jnp.zeros_like(state_ref)

    mu, w0, w2, a0, a2, g2, k_k, k_a, r_k, gn_w, gn_b = (
        ref[...] for ref in (mu_ref, w0_ref, w2_ref, a0_ref, a2_ref, g2_ref, kk_ref, ka_ref,
                             rk_ref, gnw_ref, gnb_ref))
    rows = BATCH_PER_STEP * CHUNK
    p = p_ref[...].astype(F32).reshape(rows, RWKV_COLS)
    row_id = _iota((rows, 1), 0)
    prev = pltpu.roll(p, 1, axis=0)
    for bb in range(BATCH_PER_STEP):
        prev = jnp.where(row_id == bb * CHUNK, last_ref[bb], prev)
        last_ref[bb] = p[(bb + 1) * CHUNK - 1:(bb + 1) * CHUNK, :]
    pf = p + mu * (prev - p)
    r = pf[:, 0:D_GRP]
    k = pf[:, D_GRP:2 * D_GRP]
    v = pf[:, 2 * D_GRP:3 * D_GRP]
    lora = pf[:, LORA_OFF:GATE_OFF]
    gd = pf[:, GATE_OFF:RWKV_COLS]

    wlog = w0 + _dot(jnp.tanh(lora), w2)
    neg = -wlog
    softplus = jnp.maximum(neg, 0.0) + jnp.log(1.0 + jnp.exp(-jnp.abs(neg)))
    ld = -jnp.exp(-softplus - 0.5)
    a = _sigmoid(a0 + _dot(lora, a2))
    g = _dot(_sigmoid(gd), g2)

    red, exp_m = _seg_reduce_mat(D_GRP), _seg_expand_mat(D_GRP)
    kk = k * k_k
    n2 = _split_dot(kk * kk, red, SEG_TERMS)
    kk = kk * _split_dot(1.0 / jnp.maximum(jnp.sqrt(n2), 1e-12), exp_m, SEG_TERMS)
    k2 = k * (1.0 + (a - 1.0) * k_a)

    tr, tc = _iota((rows, rows), 0), _iota((rows, rows), 1)
    tri = ((tr >= tc) & (tr // CHUNK == tc // CHUNK)).astype(BF16)
    cl = _tri_dot(tri, ld, CUM_TERMS)
    cl_end = jnp.concatenate(
        [jnp.broadcast_to(cl[(bb + 1) * CHUNK - 1:(bb + 1) * CHUNK, :], (CHUNK, D_GRP))
         for bb in range(BATCH_PER_STEP)], axis=0)
    e_in = jnp.exp(cl)
    e_out = jnp.exp(-cl)
    e_rem = jnp.exp(cl_end - cl)
    p_end = jnp.exp(cl_end)
    kka = kk * a
    ops = [(-kk * jnp.exp(cl - ld)).astype(BF16), (kka * e_out).astype(BF16),
           (k2 * e_out).astype(BF16), (r * e_in).astype(BF16), v.astype(BF16),
           (kka * e_rem).astype(BF16), (k2 * e_rem).astype(BF16)]

    chains = [(bb, grp) for bb in range(BATCH_PER_STEP)
              for grp in range(N_HEADS // HEADS_PER_SCAN)]
    head_mask = masks_ref[M_HEAD]
    strict, incl = masks_ref[M_STRICT], masks_ref[M_INCL]

    def stacked(op, bb, grp):
        part = op[bb * CHUNK:(bb + 1) * CHUNK, grp * SCAN_W:(grp + 1) * SCAN_W]
        return jnp.concatenate([part] * HEADS_PER_SCAN, axis=0) * head_mask

    xs = [[stacked(op, bb, grp) for op in ops] for bb, grp in chains]
    st = [state_ref[bb, grp] for bb, grp in chains]
    sb = [s.astype(BF16) for s in st]
    nab = [_bdot(x[0], x[1], _NT).astype(BF16) for x in xs]
    aak = [_bdot(x[0], x[2], _NT).astype(BF16) * strict for x in xs]
    arb = [_bdot(x[3], x[1], _NT).astype(BF16) * incl for x in xs]
    ark = [_bdot(x[3], x[2], _NT).astype(BF16) * incl for x in xs]
    t_inv = [masks_ref[M_EYE] + n * masks_ref[M_BASE] for n in nab]
    for lvl in range(INV_LEVELS):
        half = [_bdot(t, n * masks_ref[M_OFF + lvl], _NN).astype(BF16) for t, n in zip(t_inv, nab)]
        t_inv = [t + _bdot(h, t, _NN).astype(BF16) for t, h in zip(t_inv, half)]
    rhs = [(_bdot(x[0], s, _NT) + _bdot(k, x[4], _NN)).astype(BF16)
           for x, s, k in zip(xs, sb, aak)]
    sa = [_bdot(t, h, _NN).astype(BF16) for t, h in zip(t_inv, rhs)]
    ys = [_bdot(x[3], s, _NT) + _bdot(b, u, _NN) + _bdot(k, x[4], _NN)
          for x, s, b, u, k in zip(xs, sb, arb, sa, ark)]
    for (bb, grp), x, s, u in zip(chains, xs, st, sa):
        decay = p_end[bb * CHUNK:bb * CHUNK + 1, grp * SCAN_W:(grp + 1) * SCAN_W]
        state_ref[bb, grp] = s * decay + _bdot(u, x[5], _TN) + _bdot(x[4], x[6], _TN)
    ys = [y[0:CHUNK] + y[CHUNK:2 * CHUNK] + y[2 * CHUNK:3 * CHUNK] + y[3 * CHUNK:4 * CHUNK]
          for y in ys]
    n_grp = N_HEADS // HEADS_PER_SCAN
    y = jnp.concatenate([jnp.concatenate(ys[bb * n_grp:(bb + 1) * n_grp], axis=1)
                         for bb in range(BATCH_PER_STEP)], axis=0)

    mean = _split_dot(_split_dot(y, red, SEG_TERMS) * (1.0 / HEAD_DIM), exp_m, SEG_TERMS)
    d = y - mean
    var = _split_dot(d * d, red, SEG_TERMS) * (1.0 / HEAD_DIM)
    yn = d * _split_dot(lax.rsqrt(var + GN_EPS), exp_m, SEG_TERMS) * gn_w + gn_b
    bonus = _split_dot(_split_dot(r * k2 * r_k, red, SEG_TERMS), exp_m, SEG_TERMS) * v
    o_ref[...] = ((yn + bonus) * g).astype(BF16).reshape(BATCH_PER_STEP, CHUNK, D_GRP)


def _rwkv(p_r, mu, w0, w2p, a0, a2p, g2, k_k, k_a, r_k, gn_w, gn_b, w_gate_up, w_down):
    bsz, seq, _ = p_r.shape
    assert bsz % BATCH_PER_STEP == 0
    n_chunk = seq // CHUNK
    n_step = (bsz // BATCH_PER_STEP) * n_chunk
    wgu2d = w_gate_up.reshape(-1, w_gate_up.shape[-1])
    wd2d = w_down.reshape(-1, w_down.shape[-1])
    assert wgu2d.shape[0] % (8 * n_step) == 0 and wd2d.shape[0] == wgu2d.shape[0]
    slab = wgu2d.shape[0] // n_step
    masks = _scan_masks()
    const = lambda b, s: (0, 0)
    step = lambda b, s: (b * n_chunk + s, 0)
    vec = pl.BlockSpec((1, D_GRP), const)
    y, wgu_bf, wd_bf = pl.pallas_call(
        _rwkv_kernel,
        grid=(bsz // BATCH_PER_STEP, n_chunk),
        in_specs=[pl.BlockSpec((BATCH_PER_STEP, CHUNK, RWKV_COLS), lambda b, s: (b, s, 0)),
                  pl.BlockSpec(masks.shape, lambda b, s: (0, 0, 0)),
                  pl.BlockSpec((1, RWKV_COLS), const),
                  vec, pl.BlockSpec((LANES, D_GRP), const),
                  vec, pl.BlockSpec((LANES, D_GRP), const),
                  pl.BlockSpec((LANES, D_GRP), const),
                  vec, vec, vec, vec, vec,
                  pl.BlockSpec((slab, wgu2d.shape[1]), step),
                  pl.BlockSpec((slab, wd2d.shape[1]), step)],
        out_specs=[pl.BlockSpec((BATCH_PER_STEP, CHUNK, D_GRP), lambda b, s: (b, s, 0)),
                   pl.BlockSpec((slab, wgu2d.shape[1]), step),
                   pl.BlockSpec((slab, wd2d.shape[1]), step)],
        out_shape=[jax.ShapeDtypeStruct((bsz, seq, D_GRP), BF16),
                   jax.ShapeDtypeStruct(wgu2d.shape, BF16),
                   jax.ShapeDtypeStruct(wd2d.shape, BF16)],
        scratch_shapes=[pltpu.VMEM((BATCH_PER_STEP, 1, RWKV_COLS), F32),
                        pltpu.VMEM((BATCH_PER_STEP, N_HEADS // HEADS_PER_SCAN, SCAN_W, SCAN_W), F32)],
        compiler_params=pltpu.CompilerParams(
            dimension_semantics=("parallel", "arbitrary"), vmem_limit_bytes=VMEM_LIMIT),
        name="rwkv",
    )(p_r, masks, mu, w0, w2p, a0, a2p, g2, k_k, k_a, r_k, gn_w, gn_b, wgu2d, wd2d)
    return y, wgu_bf.reshape(w_gate_up.shape), wd_bf.reshape(w_down.shape)


def _fox_kernel(q_ref, qb_ref, k_ref, kb_ref, vt_ref, og_ref, ong_ref, o_ref, m_ref, l_ref, acc_ref,
                *, seq):
    hp = pl.program_id(1)
    lane = _iota((1, LANES), 1)
    q = q_ref[...]
    qb = qb_ref[...]
    zero = jnp.zeros_like(q)
    qcat = [jnp.concatenate([jnp.where(lane // HEAD_DIM == hh, q, zero),
                             jnp.where(lane // 8 == hp * 2 + hh, qb, zero)], axis=1)
            for hh in range(2)]
    keys = min(FOX_SUB_KEYS, seq)
    n_sub = seq // keys
    diag = _iota((keys, keys), 1) >= _iota((keys, keys), 0)

    m_ref[...] = jnp.full(m_ref.shape, -jnp.inf, F32)
    l_ref[...] = jnp.zeros(l_ref.shape, F32)
    acc_ref[...] = jnp.zeros(acc_ref.shape, F32)

    def scores(s):
        lo = s * keys
        kcat = jnp.concatenate([k_ref[lo:lo + keys, :], kb_ref[lo:lo + keys, :]], axis=1)
        out = []
        for qc in qcat:
            for q_lo, q_hi in ((lo, lo + keys), (lo + keys, seq)):
                if q_hi > q_lo:
                    out.append(lax.dot_general(kcat, qc[q_lo:q_hi, :], _NT,
                                               preferred_element_type=F32))
        return out

    pending = scores(0)
    for s in range(n_sub):
        lo = s * keys
        nxt = scores(s + 1) if s + 1 < n_sub else None
        vt = vt_ref[:, lo:lo + keys]
        n_piece = len(pending) // 2
        pieces = []
        for hh in range(2):
            pieces.append((hh, slice(lo, lo + keys), jnp.where(diag, pending[hh * n_piece], -jnp.inf)))
            if n_piece == 2:
                pieces.append((hh, slice(lo + keys, seq), pending[hh * n_piece + 1]))
        m_old = [m_ref[hh, :, qs] for hh, qs, _ in pieces]
        m_new = [jnp.maximum(m, jnp.max(st, axis=0, keepdims=True))
                 for m, (_, _, st) in zip(m_old, pieces)]
        pts = [jnp.exp2(st - m) for (_, _, st), m in zip(pieces, m_new)]
        pvs = [jnp.dot(vt, pt.astype(BF16), preferred_element_type=F32) for pt in pts]
        for (hh, qs, _), mo, mn, pt, pv in zip(pieces, m_old, m_new, pts, pvs):
            alpha = jnp.exp2(mo - mn)
            m_ref[hh, :, qs] = mn
            l_ref[hh, :, qs] = alpha * l_ref[hh, :, qs] + jnp.sum(pt, axis=0, keepdims=True)
            acc_ref[hh, :, qs] = (alpha * acc_ref[hh, :, qs]
                                  + pv[hh * HEAD_DIM:(hh + 1) * HEAD_DIM, :])
        pending = nxt

    outs = []
    for hh in range(2):
        o = acc_ref[hh] / l_ref[hh]
        outs.append(o * lax.rsqrt(jnp.mean(o * o, axis=0, keepdims=True) + NORM_EPS))
    o = jnp.concatenate(outs, axis=0).T
    o_ref[...] = (o * ong_ref[...] * _sigmoid(og_ref[...].astype(F32))).astype(BF16)


def _fox(p_x, k_bias, q_bias, o_gain):
    bsz, seq, _ = p_x.shape
    npair = N_HEADS // 2
    v_t = jnp.transpose(p_x[:, :, 2 * D_GRP:3 * D_GRP], (0, 2, 1))
    return pl.pallas_call(
        functools.partial(_fox_kernel, seq=seq),
        grid=(bsz, npair),
        in_specs=[pl.BlockSpec((None, seq, LANES), lambda b, h: (b, 0, h)),
                  pl.BlockSpec((None, seq, LANES), lambda b, h: (b, 0, 0)),
                  pl.BlockSpec((None, seq, LANES), lambda b, h: (b, 0, npair + h)),
                  pl.BlockSpec((None, seq, LANES), lambda b, h: (b, 0, 0)),
                  pl.BlockSpec((None, LANES, seq), lambda b, h: (b, h, 0)),
                  pl.BlockSpec((None, seq, LANES), lambda b, h: (b, 0, 3 * npair + h)),
                  pl.BlockSpec((1, LANES), lambda b, h: (0, 0))],
        out_specs=pl.BlockSpec((None, seq, LANES), lambda b, h: (b, 0, h)),
        out_shape=jax.ShapeDtypeStruct((bsz, seq, D_GRP), BF16),
        scratch_shapes=[pltpu.VMEM((2, 1, seq), F32), pltpu.VMEM((2, 1, seq), F32),
                        pltpu.VMEM((2, HEAD_DIM, seq), F32)],
        compiler_params=pltpu.CompilerParams(
            dimension_semantics=("parallel", "parallel"), vmem_limit_bytes=VMEM_LIMIT),
        name="fox",
    )(p_x, q_bias, p_x, k_bias, v_t, p_x, o_gain)


def _outproj_kernel(x_ref, yr_ref, yf_ref, g1_ref, sh_ref, sc_ref, ng_ref, wor_ref, wof_ref,
                    wrt_ref, wrl_ref, brt_ref, x1_ref, h2_ref, idx_ref, gate_ref, rank_ref, cnt_ref,
                    carry_ref):
    @pl.when(pl.program_id(0) == 0)
    def _():
        carry_ref[...] = jnp.zeros_like(carry_ref)

    y = (jnp.dot(yr_ref[...], wor_ref[...], preferred_element_type=F32)
         + jnp.dot(yf_ref[...], wof_ref[...], preferred_element_type=F32))
    x1 = x_ref[...] + g1_ref[...] * y
    x1_ref[...] = x1
    tm = x1.shape[0]
    h = x1 * lax.rsqrt(jnp.mean(x1 * x1, axis=-1, keepdims=True) + NORM_EPS) * ng_ref[...]
    h2 = h * (1.0 + sc_ref[...]) + sh_ref[...]
    h2_ref[...] = _pack_rows(h2)

    h_hi = h2.astype(BF16)
    h_lo = (h2 - h_hi.astype(F32)).astype(BF16)
    logits = (lax.dot_general(wrt_ref[...], h_hi, _NT, preferred_element_type=F32)
              + lax.dot_general(wrt_ref[...], h_lo, _NT, preferred_element_type=F32)
              + lax.dot_general(wrl_ref[...], h_hi, _NT, preferred_element_type=F32))
    lg = logits[:N_EXPERTS, :] + brt_ref[...]
    expert = _iota((N_EXPERTS, tm), 0)
    picks = []
    hot_sum = jnp.zeros((N_EXPERTS, tm), F32)
    for _ in range(TOP_K):
        m = jnp.max(lg, axis=0, keepdims=True)
        sel = jnp.min(jnp.where(lg == m, expert, N_EXPERTS), axis=0, keepdims=True)
        hot = expert == sel
        picks.append((m, sel, hot))
        hot_sum = hot_sum + hot.astype(F32)
        lg = jnp.where(hot, -jnp.inf, lg)
    es = [jnp.exp(m - picks[0][0]) for m, _, _ in picks]
    den = es[0] + es[1] + es[2] + es[3]

    earlier = (_iota((tm, tm), 0) < _iota((tm, tm), 1)).astype(BF16)
    before = jnp.dot(hot_sum.astype(BF16), earlier, preferred_element_type=F32) + carry_ref[...]
    ranks = [jnp.sum(jnp.where(hot, before, 0.0), axis=0, keepdims=True).astype(jnp.int32)
             for _, _, hot in picks]
    pad_i = jnp.zeros((8 - TOP_K, tm), jnp.int32)
    idx_ref[...] = jnp.concatenate([sel for _, sel, _ in picks] + [pad_i], axis=0)
    gate_ref[...] = jnp.concatenate([e / den for e in es] + [pad_i.astype(F32)], axis=0)
    rank_ref[...] = jnp.concatenate(ranks + [pad_i], axis=0)
    carry_ref[...] = carry_ref[...] + jnp.sum(hot_sum, axis=1, keepdims=True)
    cnt_ref[...] = jnp.broadcast_to(carry_ref[...], cnt_ref.shape)


def _outproj(x2d, y_r, y_f, gate1, shift2, scale2, norm_g, wo_r, wo_f, w_rt, b_rt, tm, seq,
             row0, t):
    w_rt_hi = w_rt.astype(BF16)
    w_rt_lo = (w_rt - w_rt_hi.astype(F32)).astype(BF16)
    per_b = seq // tm
    blk0 = row0 // tm
    const = lambda i: (0, 0)
    rows = lambda i: (i, 0)
    rows_in = lambda i: (i + blk0, 0)
    mod = pl.BlockSpec((None, 1, D_MODEL), lambda i: ((i + blk0) // per_b, 0, 0))
    return pl.pallas_call(
        _outproj_kernel,
        grid=(t // tm,),
        in_specs=[pl.BlockSpec((tm, D_MODEL), rows_in),
                  pl.BlockSpec((tm, D_GRP), rows_in),
                  pl.BlockSpec((tm, D_GRP), rows_in),
                  mod, mod, mod,
                  pl.BlockSpec((1, D_MODEL), const),
                  pl.BlockSpec((D_GRP, D_MODEL), const),
                  pl.BlockSpec((D_GRP, D_MODEL), const),
                  pl.BlockSpec((LANES, D_MODEL), const),
                  pl.BlockSpec((LANES, D_MODEL), const),
                  pl.BlockSpec((N_EXPERTS, 1), const)],
        out_specs=[pl.BlockSpec((tm, D_MODEL), rows),
                   pl.BlockSpec((tm, D_PACK), rows),
                   pl.BlockSpec((8, tm), lambda i: (0, i)),
                   pl.BlockSpec((8, tm), lambda i: (0, i)),
                   pl.BlockSpec((8, tm), lambda i: (0, i)),
                   pl.BlockSpec((N_EXPERTS, LANES), const)],
        out_shape=[jax.ShapeDtypeStruct((t, D_MODEL), F32),
                   jax.ShapeDtypeStruct((t, D_PACK), jnp.uint32),
                   jax.ShapeDtypeStruct((8, t), jnp.int32),
                   jax.ShapeDtypeStruct((8, t), F32),
                   jax.ShapeDtypeStruct((8, t), jnp.int32),
                   jax.ShapeDtypeStruct((N_EXPERTS, LANES), F32)],
        scratch_shapes=[pltpu.VMEM((N_EXPERTS, 1), F32)],
        compiler_params=pltpu.CompilerParams(
            dimension_semantics=("arbitrary",), vmem_limit_bytes=VMEM_LIMIT),
        name="outproj",
    )(x2d, y_r, y_f, gate1, shift2, scale2, norm_g, wo_r, wo_f, w_rt_hi, w_rt_lo, b_rt)


SC_CORES = 2
SC_SUBCORES = 16
SC_ROWS = 64


def _sc_gather_rows(idx, src):
    n_workers = SC_CORES * SC_SUBCORES
    m = idx.shape[0]
    d = src.shape[1]
    assert m % (n_workers * SC_ROWS) == 0
    n_chunks = m // (n_workers * SC_ROWS)
    mesh = plsc.VectorSubcoreMesh(core_axis_name="c", subcore_axis_name="s")

    @functools.partial(
        pl.kernel, mesh=mesh,
        out_type=jax.ShapeDtypeStruct((m, d), src.dtype),
        scratch_types=[pltpu.VMEM((n_chunks, SC_ROWS), jnp.int32),
                       pltpu.VMEM((SC_ROWS, d), src.dtype),
                       pltpu.SemaphoreType.DMA],
        name="sc_gather")
    def gather(src_hbm, idx_hbm, out_hbm, idx_v, rows_v, sem):
        wid = lax.axis_index("s") * SC_CORES + lax.axis_index("c")
        pltpu.sync_copy(idx_hbm.at[wid], idx_v)

        @pl.loop(0, n_chunks)
        def _(j):
            pltpu.async_copy(src_hbm.at[idx_v.at[j]], rows_v, sem).wait()
            pltpu.sync_copy(rows_v, out_hbm.at[pl.ds((wid * n_chunks + j) * SC_ROWS, SC_ROWS)])

    return gather(src, idx.reshape(n_workers, n_chunks, SC_ROWS))


def _sc_scatter_rows(src, dest, n_out):
    n_workers = SC_CORES * SC_SUBCORES
    t, d = src.shape
    n_slot = dest.shape[0]
    assert t % (n_workers * SC_ROWS) == 0
    n_chunks = t // (n_workers * SC_ROWS)
    mesh = plsc.VectorSubcoreMesh(core_axis_name="c", subcore_axis_name="s")
    idx = dest.reshape(n_slot, n_workers, n_chunks, SC_ROWS).transpose(1, 2, 0, 3)
    idx = idx.reshape(n_workers, n_chunks * n_slot, SC_ROWS)

    @functools.partial(
        pl.kernel, mesh=mesh,
        out_type=jax.ShapeDtypeStruct((n_out, d), src.dtype),
        scratch_types=[pltpu.VMEM((n_chunks * n_slot, SC_ROWS), jnp.int32),
                       pltpu.VMEM((SC_ROWS, d), src.dtype)],
        name="sc_scatter")
    def scatter(src_hbm, idx_hbm, out_hbm, idx_v, rows_v):
        wid = lax.axis_index("s") * SC_CORES + lax.axis_index("c")
        pltpu.sync_copy(idx_hbm.at[wid], idx_v)

        @pl.loop(0, n_chunks)
        def _(j):
            pltpu.sync_copy(src_hbm.at[pl.ds((wid * n_chunks + j) * SC_ROWS, SC_ROWS)], rows_v)
            for k in range(n_slot):
                pltpu.sync_copy(rows_v, out_hbm.at[idx_v.at[j * n_slot + k]])

    return scatter(src, idx)


def _expert_kernel(be_ref, nv_ref, x_ref, wgu_ref, bgu_ref, wd_ref, bd_ref, o_ref):
    del be_ref
    valid = _iota((EXPERT_BLOCK, 1), 0) < nv_ref[pl.program_id(0)]
    lo, hi = _unpack_rows(jnp.where(valid, x_ref[...], jnp.uint32(0)))
    x = jnp.concatenate([lo.astype(BF16), hi.astype(BF16)], axis=1)
    half = EXPERT_BLOCK // 2
    gus = [jnp.dot(x[r * half:(r + 1) * half], wgu_ref[...], preferred_element_type=F32)
           + bgu_ref[...] for r in range(2)]
    for r, gu in enumerate(gus):
        gate = jnp.minimum(gu[:, :D_MODEL], SWIGLU_LIMIT)
        up = jnp.clip(gu[:, D_MODEL:], -SWIGLU_LIMIT, SWIGLU_LIMIT)
        act = gate * _sigmoid(SWIGLU_ALPHA * gate) * (up + 1.0)
        o_ref[r * half:(r + 1) * half, :] = _pack_rows(
            jnp.dot(act.astype(BF16), wd_ref[...], preferred_element_type=F32) + bd_ref[...])


def _experts(block_e, n_valid, xs, w_gu, b_gu, w_d, b_d):
    n_blocks = block_e.shape[0]
    grid_spec = pltpu.PrefetchScalarGridSpec(
        num_scalar_prefetch=2,
        grid=(n_blocks,),
        in_specs=[pl.BlockSpec((EXPERT_BLOCK, D_PACK), lambda j, be, nv: (j, 0)),
                  pl.BlockSpec((None, D_MODEL, 2 * D_MODEL), lambda j, be, nv: (be[j], 0, 0)),
                  pl.BlockSpec((None, 1, 2 * D_MODEL), lambda j, be, nv: (be[j], 0, 0)),
                  pl.BlockSpec((None, D_MODEL, D_MODEL), lambda j, be, nv: (be[j], 0, 0)),
                  pl.BlockSpec((None, 1, D_MODEL), lambda j, be, nv: (be[j], 0, 0))],
        out_specs=pl.BlockSpec((EXPERT_BLOCK, D_PACK), lambda j, be, nv: (j, 0)),
    )
    return pl.pallas_call(
        _expert_kernel,
        grid_spec=grid_spec,
        out_shape=jax.ShapeDtypeStruct(xs.shape, jnp.uint32),
        compiler_params=pltpu.CompilerParams(
            dimension_semantics=("arbitrary",), vmem_limit_bytes=VMEM_LIMIT),
        name="experts",
    )(block_e, n_valid, xs, w_gu, b_gu, w_d, b_d)


COMBINE_TOKENS = 512
MOE_SPLITS = 2


def _combine_kernel(yg_ref, x1_ref, gate_ref, g2_ref, fg_ref, o_ref):
    gates = gate_ref[...].T
    acc_lo = acc_hi = None
    for kk in range(TOP_K):
        lo, hi = _unpack_rows(yg_ref[kk * COMBINE_TOKENS:(kk + 1) * COMBINE_TOKENS, :])
        g = gates[:, kk:kk + 1]
        acc_lo = g * lo if acc_lo is None else acc_lo + g * lo
        acc_hi = g * hi if acc_hi is None else acc_hi + g * hi
    x2 = x1_ref[...] + g2_ref[...] * jnp.concatenate([acc_lo, acc_hi], axis=1)
    o_ref[...] = x2 * lax.rsqrt(jnp.mean(x2 * x2, axis=-1, keepdims=True) + NORM_EPS) * fg_ref[...]


def _combine_kernel_into(prev_ref, *refs):
    del prev_ref
    _combine_kernel(*refs)


def _combine(yg, x1, gates, gate2, final_g, seq, row0, t_total, prev):
    t = x1.shape[0]
    tm = COMBINE_TOKENS
    per_b = seq // tm
    blk0 = row0 // tm
    rows = lambda i: (i, 0)
    in_specs = [pl.BlockSpec((TOP_K * tm, D_PACK), rows),
                pl.BlockSpec((tm, D_MODEL), rows),
                pl.BlockSpec((8, tm), lambda i: (0, i)),
                pl.BlockSpec((None, 1, D_MODEL), lambda i: ((i + blk0) // per_b, 0, 0)),
                pl.BlockSpec((1, D_MODEL), lambda i: (0, 0))]
    args = (yg, x1, gates, gate2, final_g)
    if prev is not None:
        in_specs = [pl.BlockSpec(memory_space=pl.ANY)] + in_specs
        args = (prev,) + args
    return pl.pallas_call(
        _combine_kernel if prev is None else _combine_kernel_into,
        grid=(t // tm,),
        in_specs=in_specs,
        out_specs=pl.BlockSpec((tm, D_MODEL), lambda i: (i + blk0, 0)),
        out_shape=jax.ShapeDtypeStruct((t_total, D_MODEL), F32),
        input_output_aliases={} if prev is None else {0: 0},
        compiler_params=pltpu.CompilerParams(
            dimension_semantics=("parallel",), vmem_limit_bytes=VMEM_LIMIT),
        name="combine",
    )(*args)


def _moe(h2, idx, gates, rank, counts, x1, gate2, final_g, w_gu, b_gu, w_d, b_d, seq,
         row0, t_total, prev):
    t = h2.shape[0]
    n_slots = t * TOP_K
    n_blocks = -(-n_slots // EXPERT_BLOCK) + N_EXPERTS
    cap = n_blocks * EXPERT_BLOCK
    padded = (counts + EXPERT_BLOCK - 1) // EXPERT_BLOCK * EXPERT_BLOCK
    pad_ends = jnp.cumsum(padded)
    pad_starts = pad_ends - padded
    experts = jnp.arange(N_EXPERTS, dtype=jnp.int32)
    dest = jnp.sum(jnp.where(idx[..., None] == experts, pad_starts, 0), axis=-1) + rank
    block_starts = jnp.arange(n_blocks, dtype=jnp.int32) * EXPERT_BLOCK
    block_e = jnp.minimum(jnp.sum(block_starts[:, None] >= pad_ends[None, :], axis=1),
                          N_EXPERTS - 1).astype(jnp.int32)
    n_valid = jnp.clip(counts[block_e] - (block_starts - pad_starts[block_e]), 0, EXPERT_BLOCK)

    xs = _sc_scatter_rows(h2, dest, cap)
    yb = _experts(block_e, n_valid.astype(jnp.int32), xs, w_gu, b_gu, w_d, b_d)
    dest_blocks = dest.reshape(TOP_K, -1, COMBINE_TOKENS).transpose(1, 0, 2).reshape(-1)
    yg = _sc_gather_rows(dest_blocks, yb)
    return _combine(yg, x1, gates, gate2, final_g, seq, row0, t_total, prev)


def _layer(x, c_mod, norm1_g, w_in, mu_shift, w0, w2, a0, a2, g2, k_k, k_a, r_k, gn_w, gn_b, b_f,
           q_norm_g, k_norm_g, o_norm_g, w_out, norm2_g, w_router, b_router, w_gate_up,
           b_gate_up, w_down, b_down, final_g, tm_in, tm_out):
    bsz, seq, _ = x.shape
    shift1, scale1, gate1, shift2, scale2, gate2 = (
        m.reshape(bsz, 1, D_MODEL) for m in jnp.split(c_mod, 6, axis=-1))
    row = lambda v: v.reshape(1, -1)

    w_r = w_in[:, :RWKV_COLS].astype(BF16)
    w_x = w_in[:, RWKV_COLS:RWKV_COLS + FOX_MAIN].astype(BF16)
    w_f = w_in[:, RWKV_COLS + FOX_MAIN:].T
    b_fp = jnp.pad(b_f, (0, LANES - N_HEADS)).reshape(1, LANES)
    qk_gain = jnp.concatenate([jnp.tile(q_norm_g, N_HEADS) * (HEAD_DIM ** -0.5 * LOG2_E),
                               jnp.tile(k_norm_g, N_HEADS)]).reshape(1, -1)
    p_r, p_x, k_bias, q_bias = _inproj(x, shift1, scale1, row(norm1_g), w_r, w_x, w_f, b_fp,
                                       qk_gain, tm_in)

    zeros = jnp.zeros((LANES - 64, D_GRP), F32)
    w2p = jnp.concatenate([w2, zeros], axis=0).astype(BF16)
    a2p = jnp.concatenate([zeros, a2], axis=0).astype(BF16)
    y_r, w_gu, w_d = _rwkv(p_r, row(mu_shift), row(w0), w2p, row(a0), a2p, g2.astype(BF16),
                           row(k_k), row(k_a), row(r_k), row(gn_w), row(gn_b), w_gate_up, w_down)

    y_f = _fox(p_x, k_bias, q_bias, jnp.tile(o_norm_g, 2).reshape(1, LANES))

    t = bsz * seq
    w_rt = jnp.pad(w_router.T, ((0, LANES - N_EXPERTS), (0, 0)))
    b_rt = b_router.reshape(N_EXPERTS, 1)
    wo = w_out.astype(BF16)
    b_gu, b_d = b_gate_up.reshape(N_EXPERTS, 1, -1), b_down.reshape(N_EXPERTS, 1, -1)
    t_part = t // MOE_SPLITS
    out = None
    for part in range(MOE_SPLITS):
        row0 = part * t_part
        x1, h2, idx, gates, rank, cnt = _outproj(
            x.reshape(t, D_MODEL), y_r.reshape(t, D_GRP), y_f.reshape(t, D_GRP), gate1, shift2,
            scale2, row(norm2_g), wo[:D_GRP], wo[D_GRP:], w_rt, b_rt, tm_out, seq, row0, t_part)
        counts = cnt[:, 0].astype(jnp.int32)
        out = _moe(h2, idx[:TOP_K], gates, rank[:TOP_K], counts, x1, gate2, row(final_g),
                   w_gu, b_gu, w_d, b_d, seq, row0, t, out)
    return out.reshape(bsz, seq, D_MODEL)


def kernel(x, c, w_ada, b_ada, norm1_g, w_in, mu_shift, w0, w2, a0, a2, g2, k_k, k_a, r_k, gn_w,
           gn_b, b_f, q_norm_g, k_norm_g, o_norm_g, w_out, norm2_g, w_router, b_router, w_gate_up,
           b_gate_up, w_down, b_down, final_g):
    assert w_ada.shape[0] == 1, "single-layer block"
    c_mod = _adaln(c, w_ada[0], b_ada[0])
    return _layer(x, c_mod, norm1_g[0], w_in[0], mu_shift[0], w0[0], w2[0], a0[0], a2[0], g2[0],
                  k_k[0], k_a[0], r_k[0], gn_w[0], gn_b[0], b_f[0], q_norm_g[0], k_norm_g[0],
                  o_norm_g[0], w_out[0], norm2_g[0], w_router[0], b_router[0], w_gate_up[0],
                  b_gate_up[0], w_down[0], b_down[0], final_g,
                  tm_in=min(512, x.shape[1]), tm_out=min(1024, x.shape[1]))
```

```python
import functools

import jax
import jax.numpy as jnp
from jax import lax
from jax.experimental import pallas as pl
from jax.experimental.pallas import tpu as pltpu
from jax.experimental.pallas import tpu_sc as plsc

F32 = jnp.float32
BF16 = jnp.bfloat16
HIGHEST = lax.Precision.HIGHEST

D_MODEL = 1024
HEAD_DIM = 64
N_HEADS = 8
D_GRP = N_HEADS * HEAD_DIM
RWKV_COLS = 1792
LORA_OFF = 3 * D_GRP
GATE_OFF = LORA_OFF + 128
FOX_MAIN = 4 * D_GRP
N_EXPERTS = 32
TOP_K = 4
EXPERT_BLOCK = 512
SWIGLU_ALPHA = 1.702
SWIGLU_LIMIT = 7.0
NORM_EPS = 1e-6
GN_EPS = 64e-5
LOG2_E = 1.4426950408889634
LANES = 128
CHUNK = 64
FOX_SUB_KEYS = 512
HEADS_PER_SCAN = 4
SCAN_W = HEADS_PER_SCAN * HEAD_DIM
SEG_TERMS = 1
CUM_TERMS = 2
VMEM_LIMIT = 56 * 1024 * 1024


def _dot(a, b):
    return jnp.dot(a.astype(BF16), b.astype(BF16), preferred_element_type=F32)


def _fdot(a, b):
    return jnp.dot(a, b, precision=HIGHEST, preferred_element_type=F32)


def _split_dot(x, m, terms):
    acc = None
    rem = x
    for _ in range(terms):
        part = rem.astype(BF16)
        rem = rem - part.astype(F32)
        d = jnp.dot(part, m, preferred_element_type=F32)
        acc = d if acc is None else acc + d
    return acc


def _tri_dot(m, x, terms):
    acc = None
    rem = x
    for _ in range(terms):
        part = rem.astype(BF16)
        rem = rem - part.astype(F32)
        d = jnp.dot(m, part, preferred_element_type=F32)
        acc = d if acc is None else acc + d
    return acc


def _iota(shape, dim):
    return lax.broadcasted_iota(jnp.int32, shape, dim)


def _seg_reduce_mat(n):
    return (_iota((n, LANES), 0) // HEAD_DIM == _iota((n, LANES), 1)).astype(BF16)


def _seg_expand_mat(n):
    return (_iota((LANES, n), 1) // HEAD_DIM == _iota((LANES, n), 0)).astype(BF16)


D_PACK = D_MODEL // 2


def _pack_rows(x):
    lo = lax.bitcast_convert_type(x[:, :D_PACK].astype(BF16).astype(F32), jnp.uint32)
    hi = lax.bitcast_convert_type(x[:, D_PACK:].astype(BF16).astype(F32), jnp.uint32)
    return hi | (lo >> 16)


def _unpack_rows(p):
    lo = lax.bitcast_convert_type(p << 16, F32)
    hi = lax.bitcast_convert_type(p & jnp.uint32(0xFFFF0000), F32)
    return lo, hi


def _log_sigmoid(z):
    return jnp.minimum(z, 0.0) - jnp.log(1.0 + jnp.exp(-jnp.abs(z)))


def _sigmoid(z):
    return 1.0 / (1.0 + jnp.exp(-z))


def _adaln_kernel(c_ref, w_ref, b_ref, o_ref):
    c = c_ref[...]
    o_ref[...] = _fdot(c * _sigmoid(c), w_ref[...]) + b_ref[...]


def _adaln(c, w_ada, b_ada):
    bsz = c.shape[0]
    n_mod = w_ada.shape[1] // D_MODEL
    return pl.pallas_call(
        _adaln_kernel,
        grid=(n_mod,),
        in_specs=[pl.BlockSpec((bsz, D_MODEL), lambda j: (0, 0)),
                  pl.BlockSpec((D_MODEL, D_MODEL), lambda j: (0, j)),
                  pl.BlockSpec((1, D_MODEL), lambda j: (0, j))],
        out_specs=pl.BlockSpec((bsz, D_MODEL), lambda j: (0, j)),
        out_shape=jax.ShapeDtypeStruct((bsz, n_mod * D_MODEL), F32),
        name="adaln",
    )(c, w_ada, b_ada.reshape(1, -1))


def _inproj_kernel(x_ref, sh_ref, sc_ref, g_ref, wr_ref, wx_ref, wft_ref, bf_ref, qkg_ref,
                   pr_ref, px_ref, kb_ref, qb_ref, carry_ref):
    @pl.when(pl.program_id(1) == 0)
    def _():
        carry_ref[...] = jnp.zeros_like(carry_ref)

    x = x_ref[...]
    tm = x.shape[0]
    h = x * lax.rsqrt(jnp.mean(x * x, axis=-1, keepdims=True) + NORM_EPS) * g_ref[...]
    h = h * (1.0 + sc_ref[...]) + sh_ref[...]
    hb = h.astype(BF16)

    pr_ref[...] = jnp.dot(hb, wr_ref[...], preferred_element_type=F32).astype(BF16)

    px = jnp.dot(hb, wx_ref[...], preferred_element_type=F32)
    qk = px[:, :2 * D_GRP]
    ss = _split_dot(qk * qk, _seg_reduce_mat(2 * D_GRP), SEG_TERMS)
    inv = lax.rsqrt(ss * (1.0 / HEAD_DIM) + NORM_EPS)
    qk = qk * _split_dot(inv, _seg_expand_mat(2 * D_GRP), SEG_TERMS) * qkg_ref[...]
    px_ref[:, :2 * D_GRP] = qk.astype(BF16)
    px_ref[:, 2 * D_GRP:] = px[:, 2 * D_GRP:].astype(BF16)

    lane = _iota((1, LANES), 1)
    z = jnp.zeros((tm, LANES), F32)
    for hd in range(N_HEADS):
        zh = jnp.sum(h * wft_ref[hd:hd + 1, :], axis=-1, keepdims=True)
        z = jnp.where(lane == hd, zh, z)
    cum = _log_sigmoid(z + bf_ref[...])
    row_id = _iota((tm, 1), 0)
    shift = 1
    while shift < tm:
        cum = cum + jnp.where(row_id >= shift, pltpu.roll(cum, shift, axis=0), 0.0)
        shift *= 2
    cum = cum + carry_ref[...]
    carry_ref[...] = cum[tm - 1:tm, :]

    parts = []
    rem = cum * LOG2_E
    for _ in range(3):
        part = rem.astype(BF16)
        rem = rem - part.astype(F32)
        parts.append(part)
    src, dst = _iota((LANES, LANES), 0), _iota((LANES, LANES), 1)

    def spread(offset):
        return sum(jnp.dot(part, ((dst == 8 * src + offset + t) & (src < N_HEADS)).astype(BF16),
                           preferred_element_type=F32) for t, part in enumerate(parts))

    slot = _iota((1, LANES), 1) % 8
    kb_ref[...] = (jnp.where((slot >= 3) & (slot < 6), 1.0, 0.0) - spread(0)).astype(BF16)
    qb_ref[...] = (jnp.where(slot < 3, 1.0, 0.0) + spread(3)).astype(BF16)


def _inproj(x, shift, scale, g, w_r, w_x, w_f_t, b_f, qk_gain, tm):
    bsz, seq, _ = x.shape
    const = lambda b, s: (0, 0)
    return pl.pallas_call(
        _inproj_kernel,
        grid=(bsz, seq // tm),
        in_specs=[pl.BlockSpec((None, tm, D_MODEL), lambda b, s: (b, s, 0)),
                  pl.BlockSpec((None, 1, D_MODEL), lambda b, s: (b, 0, 0)),
                  pl.BlockSpec((None, 1, D_MODEL), lambda b, s: (b, 0, 0)),
                  pl.BlockSpec((1, D_MODEL), const),
                  pl.BlockSpec((D_MODEL, RWKV_COLS), const),
                  pl.BlockSpec((D_MODEL, FOX_MAIN), const),
                  pl.BlockSpec((N_HEADS, D_MODEL), const),
                  pl.BlockSpec((1, LANES), const),
                  pl.BlockSpec((1, 2 * D_GRP), const)],
        out_specs=[pl.BlockSpec((None, tm, RWKV_COLS), lambda b, s: (b, s, 0)),
                   pl.BlockSpec((None, tm, FOX_MAIN), lambda b, s: (b, s, 0)),
                   pl.BlockSpec((None, tm, LANES), lambda b, s: (b, s, 0)),
                   pl.BlockSpec((None, tm, LANES), lambda b, s: (b, s, 0))],
        out_shape=[jax.ShapeDtypeStruct((bsz, seq, RWKV_COLS), BF16),
                   jax.ShapeDtypeStruct((bsz, seq, FOX_MAIN), BF16),
                   jax.ShapeDtypeStruct((bsz, seq, LANES), BF16),
                   jax.ShapeDtypeStruct((bsz, seq, LANES), BF16)],
        scratch_shapes=[pltpu.VMEM((1, LANES), F32)],
        compiler_params=pltpu.CompilerParams(
            dimension_semantics=("parallel", "arbitrary"), vmem_limit_bytes=VMEM_LIMIT),
        name="inproj",
    )(x, shift, scale, g, w_r, w_x, w_f_t, b_f, qk_gain)


_NN = (((1,), (0,)), ((), ()))
_NT = (((1,), (1,)), ((), ()))
_TN = (((0,), (0,)), ((), ()))
SCAN_N = HEADS_PER_SCAN * CHUNK
BATCH_PER_STEP = 8
INV_LEVELS = 5
M_HEAD, M_STRICT, M_INCL, M_EYE, M_BASE, M_OFF = 0, 1, 2, 3, 4, 5


def _bdot(a, b, dims):
    return lax.dot_general(a, b, dims, preferred_element_type=F32)


def _scan_masks():
    rr, cc = _iota((SCAN_N, SCAN_W), 0), _iota((SCAN_N, SCAN_W), 1)
    ri, ci = _iota((SCAN_N, SCAN_N), 0), _iota((SCAN_N, SCAN_N), 1)
    same = ri // CHUNK == ci // CHUNK
    masks = [rr // CHUNK == cc // HEAD_DIM, same & (ri > ci), same & (ri >= ci), ri == ci,
             (ri // 2 == ci // 2) & (ri > ci)]
    blk = 2
    while blk < CHUNK:
        masks.append((ri // (2 * blk) == ci // (2 * blk)) & (ri // blk != ci // blk) & (ri > ci))
        blk *= 2
    return jnp.stack(masks).astype(BF16)


def _rwkv_kernel(p_ref, masks_ref, mu_ref, w0_ref, w2_ref, a0_ref, a2_ref, g2_ref, kk_ref, ka_ref,
                 rk_ref, gnw_ref, gnb_ref, wgu_ref, wd_ref, o_ref, wgu_bf_ref, wd_bf_ref,
                 last_ref, state_ref):
    wgu_bf_ref[...] = wgu_ref[...].astype(BF16)
    wd_bf_ref[...] = wd_ref[...].astype(BF16)

    @pl.when(pl.program_id(1) == 0)
    def _():
        last_ref[...] = jnp.zeros_like(last_ref)
        state_ref[...] = jnp.zeros_like(state_ref)

    mu, w0, w2, a0, a2, g2, k_k, k_a, r_k, gn_w, gn_b = (
        ref[...] for ref in (mu_ref, w0_ref, w2_ref, a0_ref, a2_ref, g2_ref, kk_ref, ka_ref,
                             rk_ref, gnw_ref, gnb_ref))
    rows = BATCH_PER_STEP * CHUNK
    p = p_ref[...].astype(F32).reshape(rows, RWKV_COLS)
    row_id = _iota((rows, 1), 0)
    prev = pltpu.roll(p, 1, axis=0)
    for bb in range(BATCH_PER_STEP):
        prev = jnp.where(row_id == bb * CHUNK, last_ref[bb], prev)
        last_ref[bb] = p[(bb + 1) * CHUNK - 1:(bb + 1) * CHUNK, :]
    pf = p + mu * (prev - p)
    r = pf[:, 0:D_GRP]
    k = pf[:, D_GRP:2 * D_GRP]
    v = pf[:, 2 * D_GRP:3 * D_GRP]
    lora = pf[:, LORA_OFF:GATE_OFF]
    gd = pf[:, GATE_OFF:RWKV_COLS]

    wlog = w0 + _dot(jnp.tanh(lora), w2)
    neg = -wlog
    softplus = jnp.maximum(neg, 0.0) + jnp.log(1.0 + jnp.exp(-jnp.abs(neg)))
    ld = -jnp.exp(-softplus - 0.5)
    a = _sigmoid(a0 + _dot(lora, a2))
    g = _dot(_sigmoid(gd), g2)

    red, exp_m = _seg_reduce_mat(D_GRP), _seg_expand_mat(D_GRP)
    kk = k * k_k
    n2 = _split_dot(kk * kk, red, SEG_TERMS)
    kk = kk * _split_dot(1.0 / jnp.maximum(jnp.sqrt(n2), 1e-12), exp_m, SEG_TERMS)
    k2 = k * (1.0 + (a - 1.0) * k_a)

    tr, tc = _iota((rows, rows), 0), _iota((rows, rows), 1)
    tri = ((tr >= tc) & (tr // CHUNK == tc // CHUNK)).astype(BF16)
    cl = _tri_dot(tri, ld, CUM_TERMS)
    cl_end = jnp.concatenate(
        [jnp.broadcast_to(cl[(bb + 1) * CHUNK - 1:(bb + 1) * CHUNK, :], (CHUNK, D_GRP))
         for bb in range(BATCH_PER_STEP)], axis=0)
    e_in = jnp.exp(cl)
    e_out = jnp.exp(-cl)
    e_rem = jnp.exp(cl_end - cl)
    p_end = jnp.exp(cl_end)
    kka = kk * a
    ops = [(-kk * jnp.exp(cl - ld)).astype(BF16), (kka * e_out).astype(BF16),
           (k2 * e_out).astype(BF16), (r * e_in).astype(BF16), v.astype(BF16),
           (kka * e_rem).astype(BF16), (k2 * e_rem).astype(BF16)]

    chains = [(bb, grp) for bb in range(BATCH_PER_STEP)
              for grp in range(N_HEADS // HEADS_PER_SCAN)]
    head_mask = masks_ref[M_HEAD]
    strict, incl = masks_ref[M_STRICT], masks_ref[M_INCL]

    def stacked(op, bb, grp):
        part = op[bb * CHUNK:(bb + 1) * CHUNK, grp * SCAN_W:(grp + 1) * SCAN_W]
        return jnp.concatenate([part] * HEADS_PER_SCAN, axis=0) * head_mask

    xs = [[stacked(op, bb, grp) for op in ops] for bb, grp in chains]
    st = [state_ref[bb, grp] for bb, grp in chains]
    sb = [s.astype(BF16) for s in st]
    nab = [_bdot(x[0], x[1], _NT).astype(BF16) for x in xs]
    aak = [_bdot(x[0], x[2], _NT).astype(BF16) * strict for x in xs]
    arb = [_bdot(x[3], x[1], _NT).astype(BF16) * incl for x in xs]
    ark = [_bdot(x[3], x[2], _NT).astype(BF16) * incl for x in xs]
    t_inv = [masks_ref[M_EYE] + n * masks_ref[M_BASE] for n in nab]
    for lvl in range(INV_LEVELS):
        half = [_bdot(t, n * masks_ref[M_OFF + lvl], _NN).astype(BF16) for t, n in zip(t_inv, nab)]
        t_inv = [t + _bdot(h, t, _NN).astype(BF16) for t, h in zip(t_inv, half)]
    rhs = [(_bdot(x[0], s, _NT) + _bdot(k, x[4], _NN)).astype(BF16)
           for x, s, k in zip(xs, sb, aak)]
    sa = [_bdot(t, h, _NN).astype(BF16) for t, h in zip(t_inv, rhs)]
    ys = [_bdot(x[3], s, _NT) + _bdot(b, u, _NN) + _bdot(k, x[4], _NN)
          for x, s, b, u, k in zip(xs, sb, arb, sa, ark)]
    for (bb, grp), x, s, u in zip(chains, xs, st, sa):
        decay = p_end[bb * CHUNK:bb * CHUNK + 1, grp * SCAN_W:(grp + 1) * SCAN_W]
        state_ref[bb, grp] = s * decay + _bdot(u, x[5], _TN) + _bdot(x[4], x[6], _TN)
    ys = [y[0:CHUNK] + y[CHUNK:2 * CHUNK] + y[2 * CHUNK:3 * CHUNK] + y[3 * CHUNK:4 * CHUNK]
          for y in ys]
    n_grp = N_HEADS // HEADS_PER_SCAN
    y = jnp.concatenate([jnp.concatenate(ys[bb * n_grp:(bb + 1) * n_grp], axis=1)
                         for bb in range(BATCH_PER_STEP)], axis=0)

    mean = _split_dot(_split_dot(y, red, SEG_TERMS) * (1.0 / HEAD_DIM), exp_m, SEG_TERMS)
    d = y - mean
    var = _split_dot(d * d, red, SEG_TERMS) * (1.0 / HEAD_DIM)
    yn = d * _split_dot(lax.rsqrt(var + GN_EPS), exp_m, SEG_TERMS) * gn_w + gn_b
    bonus = _split_dot(_split_dot(r * k2 * r_k, red, SEG_TERMS), exp_m, SEG_TERMS) * v
    o_ref[...] = ((yn + bonus) * g).astype(BF16).reshape(BATCH_PER_STEP, CHUNK, D_GRP)


def _rwkv(p_r, mu, w0, w2p, a0, a2p, g2, k_k, k_a, r_k, gn_w, gn_b, w_gate_up, w_down):
    bsz, seq, _ = p_r.shape
    assert bsz % BATCH_PER_STEP == 0
    n_chunk = seq // CHUNK
    n_step = (bsz // BATCH_PER_STEP) * n_chunk
    wgu2d = w_gate_up.reshape(-1, w_gate_up.shape[-1])
    wd2d = w_down.reshape(-1, w_down.shape[-1])
    assert wgu2d.shape[0] % (8 * n_step) == 0 and wd2d.shape[0] == wgu2d.shape[0]
    slab = wgu2d.shape[0] // n_step
    masks = _scan_masks()
    const = lambda b, s: (0, 0)
    step = lambda b, s: (b * n_chunk + s, 0)
    vec = pl.BlockSpec((1, D_GRP), const)
    y, wgu_bf, wd_bf = pl.pallas_call(
        _rwkv_kernel,
        grid=(bsz // BATCH_PER_STEP, n_chunk),
        in_specs=[pl.BlockSpec((BATCH_PER_STEP, CHUNK, RWKV_COLS), lambda b, s: (b, s, 0)),
                  pl.BlockSpec(masks.shape, lambda b, s: (0, 0, 0)),
                  pl.BlockSpec((1, RWKV_COLS), const),
                  vec, pl.BlockSpec((LANES, D_GRP), const),
                  vec, pl.BlockSpec((LANES, D_GRP), const),
                  pl.BlockSpec((LANES, D_GRP), const),
                  vec, vec, vec, vec, vec,
                  pl.BlockSpec((slab, wgu2d.shape[1]), step),
                  pl.BlockSpec((slab, wd2d.shape[1]), step)],
        out_specs=[pl.BlockSpec((BATCH_PER_STEP, CHUNK, D_GRP), lambda b, s: (b, s, 0)),
                   pl.BlockSpec((slab, wgu2d.shape[1]), step),
                   pl.BlockSpec((slab, wd2d.shape[1]), step)],
        out_shape=[jax.ShapeDtypeStruct((bsz, seq, D_GRP), BF16),
                   jax.ShapeDtypeStruct(wgu2d.shape, BF16),
                   jax.ShapeDtypeStruct(wd2d.shape, BF16)],
        scratch_shapes=[pltpu.VMEM((BATCH_PER_STEP, 1, RWKV_COLS), F32),
                        pltpu.VMEM((BATCH_PER_STEP, N_HEADS // HEADS_PER_SCAN, SCAN_W, SCAN_W), F32)],
        compiler_params=pltpu.CompilerParams(
            dimension_semantics=("parallel", "arbitrary"), vmem_limit_bytes=VMEM_LIMIT),
        name="rwkv",
    )(p_r, masks, mu, w0, w2p, a0, a2p, g2, k_k, k_a, r_k, gn_w, gn_b, wgu2d, wd2d)
    return y, wgu_bf.reshape(w_gate_up.shape), wd_bf.reshape(w_down.shape)


def _fox_kernel(q_ref, qb_ref, k_ref, kb_ref, vt_ref, og_ref, ong_ref, o_ref, m_ref, l_ref, acc_ref,
                *, seq):
    hp = pl.program_id(1)
    lane = _iota((1, LANES), 1)
    q = q_ref[...]
    qb = qb_ref[...]
    zero = jnp.zeros_like(q)
    qcat = [jnp.concatenate([jnp.where(lane // HEAD_DIM == hh, q, zero),
                             jnp.where(lane // 8 == hp * 2 + hh, qb, zero)], axis=1)
            for hh in range(2)]
    keys = min(FOX_SUB_KEYS, seq)
    n_sub = seq // keys
    half = keys // 2
    mask_a = _iota((half, half), 1) >= _iota((half, half), 0)
    mask_b = _iota((keys, half), 1) + half >= _iota((keys, half), 0)

    m_ref[...] = jnp.full(m_ref.shape, -jnp.inf, F32)
    l_ref[...] = jnp.zeros(l_ref.shape, F32)
    acc_ref[...] = jnp.zeros(acc_ref.shape, F32)

    def plan(s):
        lo = s * keys
        pieces = [(half, slice(lo, lo + half), mask_a), (keys, slice(lo + half, lo + keys), mask_b)]
        if lo + keys < seq:
            pieces.append((keys, slice(lo + keys, seq), None))
        return pieces

    def scores(s):
        lo = s * keys
        kcat = jnp.concatenate([k_ref[lo:lo + keys, :], kb_ref[lo:lo + keys, :]], axis=1)
        return [lax.dot_general(kcat[:nk], qc[qs, :], _NT, preferred_element_type=F32)
                for qc in qcat for nk, qs, _ in plan(s)]

    pending = scores(0)
    for s in range(n_sub):
        lo = s * keys
        nxt = scores(s + 1) if s + 1 < n_sub else None
        n_piece = len(plan(s))
        pieces = []
        for hh in range(2):
            for (nk, qs, mask), st in zip(plan(s), pending[hh * n_piece:(hh + 1) * n_piece]):
                pieces.append((hh, nk, qs, st if mask is None else jnp.where(mask, st, -jnp.inf)))
        m_old = [m_ref[hh, :, qs] for hh, _, qs, _ in pieces]
        m_new = [jnp.maximum(m, jnp.max(st, axis=0, keepdims=True))
                 for m, (_, _, _, st) in zip(m_old, pieces)]
        pts = [jnp.exp2(st - m) for (_, _, _, st), m in zip(pieces, m_new)]
        pvs = [jnp.dot(vt_ref[:, lo:lo + nk], pt.astype(BF16), preferred_element_type=F32)
               for (_, nk, _, _), pt in zip(pieces, pts)]
        for (hh, _, qs, _), mo, mn, pt, pv in zip(pieces, m_old, m_new, pts, pvs):
            alpha = jnp.exp2(mo - mn)
            m_ref[hh, :, qs] = mn
            l_ref[hh, :, qs] = alpha * l_ref[hh, :, qs] + jnp.sum(pt, axis=0, keepdims=True)
            acc_ref[hh, :, qs] = (alpha * acc_ref[hh, :, qs]
                                  + pv[hh * HEAD_DIM:(hh + 1) * HEAD_DIM, :])
        pending = nxt

    outs = []
    for hh in range(2):
        o = acc_ref[hh] / l_ref[hh]
        outs.append(o * lax.rsqrt(jnp.mean(o * o, axis=0, keepdims=True) + NORM_EPS))
    o = jnp.concatenate(outs, axis=0).T
    o_ref[...] = (o * ong_ref[...] * _sigmoid(og_ref[...].astype(F32))).astype(BF16)


def _fox(p_x, k_bias, q_bias, o_gain):
    bsz, seq, _ = p_x.shape
    npair = N_HEADS // 2
    v_t = jnp.transpose(p_x[:, :, 2 * D_GRP:3 * D_GRP], (0, 2, 1))
    return pl.pallas_call(
        functools.partial(_fox_kernel, seq=seq),
        grid=(bsz, npair),
        in_specs=[pl.BlockSpec((None, seq, LANES), lambda b, h: (b, 0, h)),
                  pl.BlockSpec((None, seq, LANES), lambda b, h: (b, 0, 0)),
                  pl.BlockSpec((None, seq, LANES), lambda b, h: (b, 0, npair + h)),
                  pl.BlockSpec((None, seq, LANES), lambda b, h: (b, 0, 0)),
                  pl.BlockSpec((None, LANES, seq), lambda b, h: (b, h, 0)),
                  pl.BlockSpec((None, seq, LANES), lambda b, h: (b, 0, 3 * npair + h)),
                  pl.BlockSpec((1, LANES), lambda b, h: (0, 0))],
        out_specs=pl.BlockSpec((None, seq, LANES), lambda b, h: (b, 0, h)),
        out_shape=jax.ShapeDtypeStruct((bsz, seq, D_GRP), BF16),
        scratch_shapes=[pltpu.VMEM((2, 1, seq), F32), pltpu.VMEM((2, 1, seq), F32),
                        pltpu.VMEM((2, HEAD_DIM, seq), F32)],
        compiler_params=pltpu.CompilerParams(
            dimension_semantics=("parallel", "parallel"), vmem_limit_bytes=VMEM_LIMIT),
        name="fox",
    )(p_x, q_bias, p_x, k_bias, v_t, p_x, o_gain)


def _outproj_kernel(x_ref, yr_ref, yf_ref, g1_ref, sh_ref, sc_ref, ng_ref, wor_ref, wof_ref,
                    wrt_ref, wrl_ref, brt_ref, x1_ref, h2_ref, idx_ref, gate_ref, rank_ref, cnt_ref,
                    carry_ref):
    @pl.when(pl.program_id(0) == 0)
    def _():
        carry_ref[...] = jnp.zeros_like(carry_ref)

    y = (jnp.dot(yr_ref[...], wor_ref[...], preferred_element_type=F32)
         + jnp.dot(yf_ref[...], wof_ref[...], preferred_element_type=F32))
    x1 = x_ref[...] + g1_ref[...] * y
    x1_ref[...] = x1
    tm = x1.shape[0]
    h = x1 * lax.rsqrt(jnp.mean(x1 * x1, axis=-1, keepdims=True) + NORM_EPS) * ng_ref[...]
    h2 = h * (1.0 + sc_ref[...]) + sh_ref[...]
    h2_ref[...] = _pack_rows(h2)

    h_hi = h2.astype(BF16)
    h_lo = (h2 - h_hi.astype(F32)).astype(BF16)
    logits = (lax.dot_general(wrt_ref[...], h_hi, _NT, preferred_element_type=F32)
              + lax.dot_general(wrt_ref[...], h_lo, _NT, preferred_element_type=F32)
              + lax.dot_general(wrl_ref[...], h_hi, _NT, preferred_element_type=F32))
    lg = logits[:N_EXPERTS, :] + brt_ref[...]
    expert = _iota((N_EXPERTS, tm), 0)
    picks = []
    hot_sum = jnp.zeros((N_EXPERTS, tm), F32)
    for _ in range(TOP_K):
        m = jnp.max(lg, axis=0, keepdims=True)
        sel = jnp.min(jnp.where(lg == m, expert, N_EXPERTS), axis=0, keepdims=True)
        hot = expert == sel
        picks.append((m, sel, hot))
        hot_sum = hot_sum + hot.astype(F32)
        lg = jnp.where(hot, -jnp.inf, lg)
    es = [jnp.exp(m - picks[0][0]) for m, _, _ in picks]
    den = es[0] + es[1] + es[2] + es[3]

    earlier = (_iota((tm, tm), 0) < _iota((tm, tm), 1)).astype(BF16)
    before = jnp.dot(hot_sum.astype(BF16), earlier, preferred_element_type=F32) + carry_ref[...]
    ranks = [jnp.sum(jnp.where(hot, before, 0.0), axis=0, keepdims=True).astype(jnp.int32)
             for _, _, hot in picks]
    pad_i = jnp.zeros((8 - TOP_K, tm), jnp.int32)
    idx_ref[...] = jnp.concatenate([sel for _, sel, _ in picks] + [pad_i], axis=0)
    gate_ref[...] = jnp.concatenate([e / den for e in es] + [pad_i.astype(F32)], axis=0)
    rank_ref[...] = jnp.concatenate(ranks + [pad_i], axis=0)
    carry_ref[...] = carry_ref[...] + jnp.sum(hot_sum, axis=1, keepdims=True)
    cnt_ref[...] = jnp.broadcast_to(carry_ref[...], cnt_ref.shape)


def _outproj(x2d, y_r, y_f, gate1, shift2, scale2, norm_g, wo_r, wo_f, w_rt, b_rt, tm, seq,
             row0, t):
    w_rt_hi = w_rt.astype(BF16)
    w_rt_lo = (w_rt - w_rt_hi.astype(F32)).astype(BF16)
    per_b = seq // tm
    blk0 = row0 // tm
    const = lambda i: (0, 0)
    rows = lambda i: (i, 0)
    rows_in = lambda i: (i + blk0, 0)
    mod = pl.BlockSpec((None, 1, D_MODEL), lambda i: ((i + blk0) // per_b, 0, 0))
    return pl.pallas_call(
        _outproj_kernel,
        grid=(t // tm,),
        in_specs=[pl.BlockSpec((tm, D_MODEL), rows_in),
                  pl.BlockSpec((tm, D_GRP), rows_in),
                  pl.BlockSpec((tm, D_GRP), rows_in),
                  mod, mod, mod,
                  pl.BlockSpec((1, D_MODEL), const),
                  pl.BlockSpec((D_GRP, D_MODEL), const),
                  pl.BlockSpec((D_GRP, D_MODEL), const),
                  pl.BlockSpec((LANES, D_MODEL), const),
                  pl.BlockSpec((LANES, D_MODEL), const),
                  pl.BlockSpec((N_EXPERTS, 1), const)],
        out_specs=[pl.BlockSpec((tm, D_MODEL), rows),
                   pl.BlockSpec((tm, D_PACK), rows),
                   pl.BlockSpec((8, tm), lambda i: (0, i)),
                   pl.BlockSpec((8, tm), lambda i: (0, i)),
                   pl.BlockSpec((8, tm), lambda i: (0, i)),
                   pl.BlockSpec((N_EXPERTS, LANES), const)],
        out_shape=[jax.ShapeDtypeStruct((t, D_MODEL), F32),
                   jax.ShapeDtypeStruct((t, D_PACK), jnp.uint32),
                   jax.ShapeDtypeStruct((8, t), jnp.int32),
                   jax.ShapeDtypeStruct((8, t), F32),
                   jax.ShapeDtypeStruct((8, t), jnp.int32),
                   jax.ShapeDtypeStruct((N_EXPERTS, LANES), F32)],
        scratch_shapes=[pltpu.VMEM((N_EXPERTS, 1), F32)],
        compiler_params=pltpu.CompilerParams(
            dimension_semantics=("arbitrary",), vmem_limit_bytes=VMEM_LIMIT),
        name="outproj",
    )(x2d, y_r, y_f, gate1, shift2, scale2, norm_g, wo_r, wo_f, w_rt_hi, w_rt_lo, b_rt)


SC_CORES = 2
SC_SUBCORES = 16
SC_ROWS = 64


def _sc_gather_rows(idx, src):
    n_workers = SC_CORES * SC_SUBCORES
    m = idx.shape[0]
    d = src.shape[1]
    assert m % (n_workers * SC_ROWS) == 0
    n_chunks = m // (n_workers * SC_ROWS)
    mesh = plsc.VectorSubcoreMesh(core_axis_name="c", subcore_axis_name="s")

    @functools.partial(
        pl.kernel, mesh=mesh,
        out_type=jax.ShapeDtypeStruct((m, d), src.dtype),
        scratch_types=[pltpu.VMEM((n_chunks, SC_ROWS), jnp.int32),
                       pltpu.VMEM((SC_ROWS, d), src.dtype),
                       pltpu.SemaphoreType.DMA],
        name="sc_gather")
    def gather(src_hbm, idx_hbm, out_hbm, idx_v, rows_v, sem):
        wid = lax.axis_index("s") * SC_CORES + lax.axis_index("c")
        pltpu.sync_copy(idx_hbm.at[wid], idx_v)

        @pl.loop(0, n_chunks)
        def _(j):
            pltpu.async_copy(src_hbm.at[idx_v.at[j]], rows_v, sem).wait()
            pltpu.sync_copy(rows_v, out_hbm.at[pl.ds((wid * n_chunks + j) * SC_ROWS, SC_ROWS)])

    return gather(src, idx.reshape(n_workers, n_chunks, SC_ROWS))


def _sc_scatter_rows(src, dest, n_out):
    n_workers = SC_CORES * SC_SUBCORES
    t, d = src.shape
    n_slot = dest.shape[0]
    assert t % (n_workers * SC_ROWS) == 0
    n_chunks = t // (n_workers * SC_ROWS)
    mesh = plsc.VectorSubcoreMesh(core_axis_name="c", subcore_axis_name="s")
    idx = dest.reshape(n_slot, n_workers, n_chunks, SC_ROWS).transpose(1, 2, 0, 3)
    idx = idx.reshape(n_workers, n_chunks * n_slot, SC_ROWS)

    @functools.partial(
        pl.kernel, mesh=mesh,
        out_type=jax.ShapeDtypeStruct((n_out, d), src.dtype),
        scratch_types=[pltpu.VMEM((n_chunks * n_slot, SC_ROWS), jnp.int32),
                       pltpu.VMEM((SC_ROWS, d), src.dtype)],
        name="sc_scatter")
    def scatter(src_hbm, idx_hbm, out_hbm, idx_v, rows_v):
        wid = lax.axis_index("s") * SC_CORES + lax.axis_index("c")
        pltpu.sync_copy(idx_hbm.at[wid], idx_v)

        @pl.loop(0, n_chunks)
        def _(j):
            pltpu.sync_copy(src_hbm.at[pl.ds((wid * n_chunks + j) * SC_ROWS, SC_ROWS)], rows_v)
            for k in range(n_slot):
                pltpu.sync_copy(rows_v, out_hbm.at[idx_v.at[j * n_slot + k]])

    return scatter(src, idx)


def _expert_kernel(be_ref, nv_ref, x_ref, wgu_ref, bgu_ref, wd_ref, bd_ref, o_ref):
    del be_ref
    valid = _iota((EXPERT_BLOCK, 1), 0) < nv_ref[pl.program_id(0)]
    lo, hi = _unpack_rows(jnp.where(valid, x_ref[...], jnp.uint32(0)))
    x = jnp.concatenate([lo.astype(BF16), hi.astype(BF16)], axis=1)
    half = EXPERT_BLOCK // 2
    gus = [jnp.dot(x[r * half:(r + 1) * half], wgu_ref[...], preferred_element_type=F32)
           + bgu_ref[...] for r in range(2)]
    for r, gu in enumerate(gus):
        gate = jnp.minimum(gu[:, :D_MODEL], SWIGLU_LIMIT)
        up = jnp.clip(gu[:, D_MODEL:], -SWIGLU_LIMIT, SWIGLU_LIMIT)
        act = gate * _sigmoid(SWIGLU_ALPHA * gate) * (up + 1.0)
        o_ref[r * half:(r + 1) * half, :] = _pack_rows(
            jnp.dot(act.astype(BF16), wd_ref[...], preferred_element_type=F32) + bd_ref[...])


def _experts(block_e, n_valid, xs, w_gu, b_gu, w_d, b_d):
    n_blocks = block_e.shape[0]
    grid_spec = pltpu.PrefetchScalarGridSpec(
        num_scalar_prefetch=2,
        grid=(n_blocks,),
        in_specs=[pl.BlockSpec((EXPERT_BLOCK, D_PACK), lambda j, be, nv: (j, 0)),
                  pl.BlockSpec((None, D_MODEL, 2 * D_MODEL), lambda j, be, nv: (be[j], 0, 0)),
                  pl.BlockSpec((None, 1, 2 * D_MODEL), lambda j, be, nv: (be[j], 0, 0)),
                  pl.BlockSpec((None, D_MODEL, D_MODEL), lambda j, be, nv: (be[j], 0, 0)),
                  pl.BlockSpec((None, 1, D_MODEL), lambda j, be, nv: (be[j], 0, 0))],
        out_specs=pl.BlockSpec((EXPERT_BLOCK, D_PACK), lambda j, be, nv: (j, 0)),
    )
    return pl.pallas_call(
        _expert_kernel,
        grid_spec=grid_spec,
        out_shape=jax.ShapeDtypeStruct(xs.shape, jnp.uint32),
        compiler_params=pltpu.CompilerParams(
            dimension_semantics=("arbitrary",), vmem_limit_bytes=VMEM_LIMIT),
        name="experts",
    )(block_e, n_valid, xs, w_gu, b_gu, w_d, b_d)


COMBINE_TOKENS = 512
MOE_SPLITS = 2


def _combine_kernel(yg_ref, x1_ref, gate_ref, g2_ref, fg_ref, o_ref):
    gates = gate_ref[...].T
    acc_lo = acc_hi = None
    for kk in range(TOP_K):
        lo, hi = _unpack_rows(yg_ref[kk * COMBINE_TOKENS:(kk + 1) * COMBINE_TOKENS, :])
        g = gates[:, kk:kk + 1]
        acc_lo = g * lo if acc_lo is None else acc_lo + g * lo
        acc_hi = g * hi if acc_hi is None else acc_hi + g * hi
    x2 = x1_ref[...] + g2_ref[...] * jnp.concatenate([acc_lo, acc_hi], axis=1)
    o_ref[...] = x2 * lax.rsqrt(jnp.mean(x2 * x2, axis=-1, keepdims=True) + NORM_EPS) * fg_ref[...]


def _combine_kernel_into(prev_ref, *refs):
    del prev_ref
    _combine_kernel(*refs)


def _combine(yg, x1, gates, gate2, final_g, seq, row0, t_total, prev):
    t = x1.shape[0]
    tm = COMBINE_TOKENS
    per_b = seq // tm
    blk0 = row0 // tm
    rows = lambda i: (i, 0)
    in_specs = [pl.BlockSpec((TOP_K * tm, D_PACK), rows),
                pl.BlockSpec((tm, D_MODEL), rows),
                pl.BlockSpec((8, tm), lambda i: (0, i)),
                pl.BlockSpec((None, 1, D_MODEL), lambda i: ((i + blk0) // per_b, 0, 0)),
                pl.BlockSpec((1, D_MODEL), lambda i: (0, 0))]
    args = (yg, x1, gates, gate2, final_g)
    if prev is not None:
        in_specs = [pl.BlockSpec(memory_space=pl.ANY)] + in_specs
        args = (prev,) + args
    return pl.pallas_call(
        _combine_kernel if prev is None else _combine_kernel_into,
        grid=(t // tm,),
        in_specs=in_specs,
        out_specs=pl.BlockSpec((tm, D_MODEL), lambda i: (i + blk0, 0)),
        out_shape=jax.ShapeDtypeStruct((t_total, D_MODEL), F32),
        input_output_aliases={} if prev is None else {0: 0},
        compiler_params=pltpu.CompilerParams(
            dimension_semantics=("parallel",), vmem_limit_bytes=VMEM_LIMIT),
        name="combine",
    )(*args)


def _moe(h2, idx, gates, rank, counts, x1, gate2, final_g, w_gu, b_gu, w_d, b_d, seq,
         row0, t_total, prev):
    t = h2.shape[0]
    n_slots = t * TOP_K
    n_blocks = -(-n_slots // EXPERT_BLOCK) + N_EXPERTS
    cap = n_blocks * EXPERT_BLOCK
    padded = (counts + EXPERT_BLOCK - 1) // EXPERT_BLOCK * EXPERT_BLOCK
    pad_ends = jnp.cumsum(padded)
    pad_starts = pad_ends - padded
    experts = jnp.arange(N_EXPERTS, dtype=jnp.int32)
    dest = jnp.sum(jnp.where(idx[..., None] == experts, pad_starts, 0), axis=-1) + rank
    block_starts = jnp.arange(n_blocks, dtype=jnp.int32) * EXPERT_BLOCK
    block_e = jnp.minimum(jnp.sum(block_starts[:, None] >= pad_ends[None, :], axis=1),
                          N_EXPERTS - 1).astype(jnp.int32)
    n_valid = jnp.clip(counts[block_e] - (block_starts - pad_starts[block_e]), 0, EXPERT_BLOCK)

    xs = _sc_scatter_rows(h2, dest, cap)
    yb = _experts(block_e, n_valid.astype(jnp.int32), xs, w_gu, b_gu, w_d, b_d)
    dest_blocks = dest.reshape(TOP_K, -1, COMBINE_TOKENS).transpose(1, 0, 2).reshape(-1)
    yg = _sc_gather_rows(dest_blocks, yb)
    return _combine(yg, x1, gates, gate2, final_g, seq, row0, t_total, prev)


def _layer(x, c_mod, norm1_g, w_in, mu_shift, w0, w2, a0, a2, g2, k_k, k_a, r_k, gn_w, gn_b, b_f,
           q_norm_g, k_norm_g, o_norm_g, w_out, norm2_g, w_router, b_router, w_gate_up,
           b_gate_up, w_down, b_down, final_g, tm_in, tm_out):
    bsz, seq, _ = x.shape
    shift1, scale1, gate1, shift2, scale2, gate2 = (
        m.reshape(bsz, 1, D_MODEL) for m in jnp.split(c_mod, 6, axis=-1))
    row = lambda v: v.reshape(1, -1)

    w_r = w_in[:, :RWKV_COLS].astype(BF16)
    w_x = w_in[:, RWKV_COLS:RWKV_COLS + FOX_MAIN].astype(BF16)
    w_f = w_in[:, RWKV_COLS + FOX_MAIN:].T
    b_fp = jnp.pad(b_f, (0, LANES - N_HEADS)).reshape(1, LANES)
    qk_gain = jnp.concatenate([jnp.tile(q_norm_g, N_HEADS) * (HEAD_DIM ** -0.5 * LOG2_E),
                               jnp.tile(k_norm_g, N_HEADS)]).reshape(1, -1)
    p_r, p_x, k_bias, q_bias = _inproj(x, shift1, scale1, row(norm1_g), w_r, w_x, w_f, b_fp,
                                       qk_gain, tm_in)

    zeros = jnp.zeros((LANES - 64, D_GRP), F32)
    w2p = jnp.concatenate([w2, zeros], axis=0).astype(BF16)
    a2p = jnp.concatenate([zeros, a2], axis=0).astype(BF16)
    y_r, w_gu, w_d = _rwkv(p_r, row(mu_shift), row(w0), w2p, row(a0), a2p, g2.astype(BF16),
                           row(k_k), row(k_a), row(r_k), row(gn_w), row(gn_b), w_gate_up, w_down)

    y_f = _fox(p_x, k_bias, q_bias, jnp.tile(o_norm_g, 2).reshape(1, LANES))

    t = bsz * seq
    w_rt = jnp.pad(w_router.T, ((0, LANES - N_EXPERTS), (0, 0)))
    b_rt = b_router.reshape(N_EXPERTS, 1)
    wo = w_out.astype(BF16)
    b_gu, b_d = b_gate_up.reshape(N_EXPERTS, 1, -1), b_down.reshape(N_EXPERTS, 1, -1)
    t_part = t // MOE_SPLITS
    out = None
    for part in range(MOE_SPLITS):
        row0 = part * t_part
        x1, h2, idx, gates, rank, cnt = _outproj(
            x.reshape(t, D_MODEL), y_r.reshape(t, D_GRP), y_f.reshape(t, D_GRP), gate1, shift2,
            scale2, row(norm2_g), wo[:D_GRP], wo[D_GRP:], w_rt, b_rt, tm_out, seq, row0, t_part)
        counts = cnt[:, 0].astype(jnp.int32)
        out = _moe(h2, idx[:TOP_K], gates, rank[:TOP_K], counts, x1, gate2, row(final_g),
                   w_gu, b_gu, w_d, b_d, seq, row0, t, out)
    return out.reshape(bsz, seq, D_MODEL)


def kernel(x, c, w_ada, b_ada, norm1_g, w_in, mu_shift, w0, w2, a0, a2, g2, k_k, k_a, r_k, gn_w,
           gn_b, b_f, q_norm_g, k_norm_g, o_norm_g, w_out, norm2_g, w_router, b_router, w_gate_up,
           b_gate_up, w_down, b_down, final_g):
    assert w_ada.shape[0] == 1, "single-layer block"
    c_mod = _adaln(c, w_ada[0], b_ada[0])
    return _layer(x, c_mod, norm1_g[0], w_in[0], mu_shift[0], w0[0], w2[0], a0[0], a2[0], g2[0],
                  k_k[0], k_a[0], r_k[0], gn_w[0], gn_b[0], b_f[0], q_norm_g[0], k_norm_g[0],
                  o_norm_g[0], w_out[0], norm2_g[0], w_router[0], b_router[0], w_gate_up[0],
                  b_gate_up[0], w_down[0], b_down[0], final_g,
                  tm_in=min(512, x.shape[1]), tm_out=min(1024, x.shape[1]))
```

```python
import functools

import jax
import jax.numpy as jnp
from jax import lax
from jax.experimental import pallas as pl
from jax.experimental.pallas import tpu as pltpu
from jax.experimental.pallas import tpu_sc as plsc

F32 = jnp.float32
BF16 = jnp.bfloat16
HIGHEST = lax.Precision.HIGHEST

D_MODEL = 1024
HEAD_DIM = 64
N_HEADS = 8
D_GRP = N_HEADS * HEAD_DIM
RWKV_COLS = 1792
LORA_OFF = 3 * D_GRP
GATE_OFF = LORA_OFF + 128
FOX_MAIN = 4 * D_GRP
N_EXPERTS = 32
TOP_K = 4
EXPERT_BLOCK = 512
SWIGLU_ALPHA = 1.702
SWIGLU_LIMIT = 7.0
NORM_EPS = 1e-6
GN_EPS = 64e-5
LOG2_E = 1.4426950408889634
LANES = 128
CHUNK = 64
FOX_SUB_KEYS = 512
HEADS_PER_SCAN = 4
SCAN_W = HEADS_PER_SCAN * HEAD_DIM
SEG_TERMS = 1
CUM_TERMS = 2
VMEM_LIMIT = 56 * 1024 * 1024


def _dot(a, b):
    return jnp.dot(a.astype(BF16), b.astype(BF16), preferred_element_type=F32)


def _fdot(a, b):
    return jnp.dot(a, b, precision=HIGHEST, preferred_element_type=F32)


def _split_dot(x, m, terms):
    acc = None
    rem = x
    for _ in range(terms):
        part = rem.astype(BF16)
        rem = rem - part.astype(F32)
        d = jnp.dot(part, m, preferred_element_type=F32)
        acc = d if acc is None else acc + d
    return acc


def _tri_dot(m, x, terms):
    acc = None
    rem = x
    for _ in range(terms):
        part = rem.astype(BF16)
        rem = rem - part.astype(F32)
        d = jnp.dot(m, part, preferred_element_type=F32)
        acc = d if acc is None else acc + d
    return acc


def _iota(shape, dim):
    return lax.broadcasted_iota(jnp.int32, shape, dim)


def _seg_reduce_mat(n):
    return (_iota((n, LANES), 0) // HEAD_DIM == _iota((n, LANES), 1)).astype(BF16)


def _seg_expand_mat(n):
    return (_iota((LANES, n), 1) // HEAD_DIM == _iota((LANES, n), 0)).astype(BF16)


D_PACK = D_MODEL // 2


def _pack_rows(x):
    lo = lax.bitcast_convert_type(x[:, :D_PACK].astype(BF16).astype(F32), jnp.uint32)
    hi = lax.bitcast_convert_type(x[:, D_PACK:].astype(BF16).astype(F32), jnp.uint32)
    return hi | (lo >> 16)


def _unpack_rows(p):
    lo = lax.bitcast_convert_type(p << 16, F32)
    hi = lax.bitcast_convert_type(p & jnp.uint32(0xFFFF0000), F32)
    return lo, hi


def _log_sigmoid(z):
    return jnp.minimum(z, 0.0) - jnp.log(1.0 + jnp.exp(-jnp.abs(z)))


def _sigmoid(z):
    return 1.0 / (1.0 + jnp.exp(-z))


def _adaln_kernel(c_ref, w_ref, b_ref, o_ref):
    c = c_ref[...]
    o_ref[...] = _fdot(c * _sigmoid(c), w_ref[...]) + b_ref[...]


def _adaln(c, w_ada, b_ada):
    bsz = c.shape[0]
    n_mod = w_ada.shape[1] // D_MODEL
    return pl.pallas_call(
        _adaln_kernel,
        grid=(n_mod,),
        in_specs=[pl.BlockSpec((bsz, D_MODEL), lambda j: (0, 0)),
                  pl.BlockSpec((D_MODEL, D_MODEL), lambda j: (0, j)),
                  pl.BlockSpec((1, D_MODEL), lambda j: (0, j))],
        out_specs=pl.BlockSpec((bsz, D_MODEL), lambda j: (0, j)),
        out_shape=jax.ShapeDtypeStruct((bsz, n_mod * D_MODEL), F32),
        name="adaln",
    )(c, w_ada, b_ada.reshape(1, -1))


def _inproj_kernel(x_ref, sh_ref, sc_ref, g_ref, wr_ref, wx_ref, wft_ref, bf_ref, qkg_ref,
                   pr_ref, px_ref, kb_ref, qb_ref, carry_ref):
    @pl.when(pl.program_id(1) == 0)
    def _():
        carry_ref[...] = jnp.zeros_like(carry_ref)

    x = x_ref[...]
    tm = x.shape[0]
    h = x * lax.rsqrt(jnp.mean(x * x, axis=-1, keepdims=True) + NORM_EPS) * g_ref[...]
    h = h * (1.0 + sc_ref[...]) + sh_ref[...]
    hb = h.astype(BF16)

    pr_ref[...] = jnp.dot(hb, wr_ref[...], preferred_element_type=F32).astype(BF16)

    px = jnp.dot(hb, wx_ref[...], preferred_element_type=F32)
    qk = px[:, :2 * D_GRP]
    ss = _split_dot(qk * qk, _seg_reduce_mat(2 * D_GRP), SEG_TERMS)
    inv = lax.rsqrt(ss * (1.0 / HEAD_DIM) + NORM_EPS)
    qk = qk * _split_dot(inv, _seg_expand_mat(2 * D_GRP), SEG_TERMS) * qkg_ref[...]
    px_ref[:, :2 * D_GRP] = qk.astype(BF16)
    px_ref[:, 2 * D_GRP:] = px[:, 2 * D_GRP:].astype(BF16)

    lane = _iota((1, LANES), 1)
    z = jnp.zeros((tm, LANES), F32)
    for hd in range(N_HEADS):
        zh = jnp.sum(h * wft_ref[hd:hd + 1, :], axis=-1, keepdims=True)
        z = jnp.where(lane == hd, zh, z)
    cum = _log_sigmoid(z + bf_ref[...])
    row_id = _iota((tm, 1), 0)
    shift = 1
    while shift < tm:
        cum = cum + jnp.where(row_id >= shift, pltpu.roll(cum, shift, axis=0), 0.0)
        shift *= 2
    cum = cum + carry_ref[...]
    carry_ref[...] = cum[tm - 1:tm, :]

    parts = []
    rem = cum * LOG2_E
    for _ in range(3):
        part = rem.astype(BF16)
        rem = rem - part.astype(F32)
        parts.append(part)
    src, dst = _iota((LANES, LANES), 0), _iota((LANES, LANES), 1)

    def spread(offset):
        return sum(jnp.dot(part, ((dst == 8 * src + offset + t) & (src < N_HEADS)).astype(BF16),
                           preferred_element_type=F32) for t, part in enumerate(parts))

    slot = _iota((1, LANES), 1) % 8
    kb_ref[...] = (jnp.where((slot >= 3) & (slot < 6), 1.0, 0.0) - spread(0)).astype(BF16)
    qb_ref[...] = (jnp.where(slot < 3, 1.0, 0.0) + spread(3)).astype(BF16)


def _inproj(x, shift, scale, g, w_r, w_x, w_f_t, b_f, qk_gain, tm):
    bsz, seq, _ = x.shape
    const = lambda b, s: (0, 0)
    return pl.pallas_call(
        _inproj_kernel,
        grid=(bsz, seq // tm),
        in_specs=[pl.BlockSpec((None, tm, D_MODEL), lambda b, s: (b, s, 0)),
                  pl.BlockSpec((None, 1, D_MODEL), lambda b, s: (b, 0, 0)),
                  pl.BlockSpec((None, 1, D_MODEL), lambda b, s: (b, 0, 0)),
                  pl.BlockSpec((1, D_MODEL), const),
                  pl.BlockSpec((D_MODEL, RWKV_COLS), const),
                  pl.BlockSpec((D_MODEL, FOX_MAIN), const),
                  pl.BlockSpec((N_HEADS, D_MODEL), const),
                  pl.BlockSpec((1, LANES), const),
                  pl.BlockSpec((1, 2 * D_GRP), const)],
        out_specs=[pl.BlockSpec((None, tm, RWKV_COLS), lambda b, s: (b, s, 0)),
                   pl.BlockSpec((None, tm, FOX_MAIN), lambda b, s: (b, s, 0)),
                   pl.BlockSpec((None, tm, LANES), lambda b, s: (b, s, 0)),
                   pl.BlockSpec((None, tm, LANES), lambda b, s: (b, s, 0))],
        out_shape=[jax.ShapeDtypeStruct((bsz, seq, RWKV_COLS), BF16),
                   jax.ShapeDtypeStruct((bsz, seq, FOX_MAIN), BF16),
                   jax.ShapeDtypeStruct((bsz, seq, LANES), BF16),
                   jax.ShapeDtypeStruct((bsz, seq, LANES), BF16)],
        scratch_shapes=[pltpu.VMEM((1, LANES), F32)],
        compiler_params=pltpu.CompilerParams(
            dimension_semantics=("parallel", "arbitrary"), vmem_limit_bytes=VMEM_LIMIT),
        name="inproj",
    )(x, shift, scale, g, w_r, w_x, w_f_t, b_f, qk_gain)


_NN = (((1,), (0,)), ((), ()))
_NT = (((1,), (1,)), ((), ()))
_TN = (((0,), (0,)), ((), ()))
SCAN_N = HEADS_PER_SCAN * CHUNK
BATCH_PER_STEP = 8
INV_LEVELS = 5
M_HEAD, M_STRICT, M_INCL, M_EYE, M_BASE, M_OFF = 0, 1, 2, 3, 4, 5


def _bdot(a, b, dims):
    return lax.dot_general(a, b, dims, preferred_element_type=F32)


def _scan_masks():
    rr, cc = _iota((SCAN_N, SCAN_W), 0), _iota((SCAN_N, SCAN_W), 1)
    ri, ci = _iota((SCAN_N, SCAN_N), 0), _iota((SCAN_N, SCAN_N), 1)
    same = ri // CHUNK == ci // CHUNK
    masks = [rr // CHUNK == cc // HEAD_DIM, same & (ri > ci), same & (ri >= ci), ri == ci,
             (ri // 2 == ci // 2) & (ri > ci)]
    blk = 2
    while blk < CHUNK:
        masks.append((ri // (2 * blk) == ci // (2 * blk)) & (ri // blk != ci // blk) & (ri > ci))
        blk *= 2
    return jnp.stack(masks).astype(BF16)


def _rwkv_kernel(p_ref, masks_ref, mu_ref, w0_ref, w2_ref, a0_ref, a2_ref, g2_ref, kk_ref, ka_ref,
                 rk_ref, gnw_ref, gnb_ref, wgu_ref, wd_ref, o_ref, wgu_bf_ref, wd_bf_ref,
                 last_ref, state_ref):
    wgu_bf_ref[...] = wgu_ref[...].astype(BF16)
    wd_bf_ref[...] = wd_ref[...].astype(BF16)

    @pl.when(pl.program_id(1) == 0)
    def _():
        last_ref[...] = jnp.zeros_like(last_ref)
        state_ref[...] = jnp.zeros_like(state_ref)

    mu, w0, w2, a0, a2, g2, k_k, k_a, r_k, gn_w, gn_b = (
        ref[...] for ref in (mu_ref, w0_ref, w2_ref, a0_ref, a2_ref, g2_ref, kk_ref, ka_ref,
                             rk_ref, gnw_ref, gnb_ref))
    rows = BATCH_PER_STEP * CHUNK
    p = p_ref[...].astype(F32).reshape(rows, RWKV_COLS)
    row_id = _iota((rows, 1), 0)
    prev = pltpu.roll(p, 1, axis=0)
    for bb in range(BATCH_PER_STEP):
        prev = jnp.where(row_id == bb * CHUNK, last_ref[bb], prev)
        last_ref[bb] = p[(bb + 1) * CHUNK - 1:(bb + 1) * CHUNK, :]
    pf = p + mu * (prev - p)
    r = pf[:, 0:D_GRP]
    k = pf[:, D_GRP:2 * D_GRP]
    v = pf[:, 2 * D_GRP:3 * D_GRP]
    lora = pf[:, LORA_OFF:GATE_OFF]
    gd = pf[:, GATE_OFF:RWKV_COLS]

    wlog = w0 + _dot(jnp.tanh(lora), w2)
    neg = -wlog
    softplus = jnp.maximum(neg, 0.0) + jnp.log(1.0 + jnp.exp(-jnp.abs(neg)))
    ld = -jnp.exp(-softplus - 0.5)
    a = _sigmoid(a0 + _dot(lora, a2))
    g = _dot(_sigmoid(gd), g2)

    red, exp_m = _seg_reduce_mat(D_GRP), _seg_expand_mat(D_GRP)
    kk = k * k_k
    n2 = _split_dot(kk * kk, red, SEG_TERMS)
    kk = kk * _split_dot(1.0 / jnp.maximum(jnp.sqrt(n2), 1e-12), exp_m, SEG_TERMS)
    k2 = k * (1.0 + (a - 1.0) * k_a)

    tr, tc = _iota((rows, rows), 0), _iota((rows, rows), 1)
    tri = ((tr >= tc) & (tr // CHUNK == tc // CHUNK)).astype(BF16)
    cl = _tri_dot(tri, ld, CUM_TERMS)
    cl_end = jnp.concatenate(
        [jnp.broadcast_to(cl[(bb + 1) * CHUNK - 1:(bb + 1) * CHUNK, :], (CHUNK, D_GRP))
         for bb in range(BATCH_PER_STEP)], axis=0)
    e_in = jnp.exp(cl)
    e_out = jnp.exp(-cl)
    e_rem = jnp.exp(cl_end - cl)
    p_end = jnp.exp(cl_end)
    kka = kk * a
    ops = [(-kk * jnp.exp(cl - ld)).astype(BF16), (kka * e_out).astype(BF16),
           (k2 * e_out).astype(BF16), (r * e_in).astype(BF16), v.astype(BF16),
           (kka * e_rem).astype(BF16), (k2 * e_rem).astype(BF16)]

    chains = [(bb, grp) for bb in range(BATCH_PER_STEP)
              for grp in range(N_HEADS // HEADS_PER_SCAN)]
    head_mask = masks_ref[M_HEAD]
    strict, incl = masks_ref[M_STRICT], masks_ref[M_INCL]

    def stacked(op, bb, grp):
        part = op[bb * CHUNK:(bb + 1) * CHUNK, grp * SCAN_W:(grp + 1) * SCAN_W]
        return jnp.concatenate([part] * HEADS_PER_SCAN, axis=0) * head_mask

    xs = [[stacked(op, bb, grp) for op in ops] for bb, grp in chains]
    st = [state_ref[bb, grp] for bb, grp in chains]
    sb = [s.astype(BF16) for s in st]
    nab = [_bdot(x[0], x[1], _NT).astype(BF16) for x in xs]
    aak = [_bdot(x[0], x[2], _NT).astype(BF16) * strict for x in xs]
    arb = [_bdot(x[3], x[1], _NT).astype(BF16) * incl for x in xs]
    ark = [_bdot(x[3], x[2], _NT).astype(BF16) * incl for x in xs]
    t_inv = [masks_ref[M_EYE] + n * masks_ref[M_BASE] for n in nab]
    for lvl in range(INV_LEVELS):
        half = [_bdot(t, n * masks_ref[M_OFF + lvl], _NN).astype(BF16) for t, n in zip(t_inv, nab)]
        t_inv = [t + _bdot(h, t, _NN).astype(BF16) for t, h in zip(t_inv, half)]
    rhs = [(_bdot(x[0], s, _NT) + _bdot(k, x[4], _NN)).astype(BF16)
           for x, s, k in zip(xs, sb, aak)]
    sa = [_bdot(t, h, _NN).astype(BF16) for t, h in zip(t_inv, rhs)]
    ys = [_bdot(x[3], s, _NT) + _bdot(b, u, _NN) + _bdot(k, x[4], _NN)
          for x, s, b, u, k in zip(xs, sb, arb, sa, ark)]
    for (bb, grp), x, s, u in zip(chains, xs, st, sa):
        decay = p_end[bb * CHUNK:bb * CHUNK + 1, grp * SCAN_W:(grp + 1) * SCAN_W]
        state_ref[bb, grp] = s * decay + _bdot(u, x[5], _TN) + _bdot(x[4], x[6], _TN)
    ys = [y[0:CHUNK] + y[CHUNK:2 * CHUNK] + y[2 * CHUNK:3 * CHUNK] + y[3 * CHUNK:4 * CHUNK]
          for y in ys]
    n_grp = N_HEADS // HEADS_PER_SCAN
    y = jnp.concatenate([jnp.concatenate(ys[bb * n_grp:(bb + 1) * n_grp], axis=1)
                         for bb in range(BATCH_PER_STEP)], axis=0)

    mean = _split_dot(_split_dot(y, red, SEG_TERMS) * (1.0 / HEAD_DIM), exp_m, SEG_TERMS)
    d = y - mean
    var = _split_dot(d * d, red, SEG_TERMS) * (1.0 / HEAD_DIM)
    yn = d * _split_dot(lax.rsqrt(var + GN_EPS), exp_m, SEG_TERMS) * gn_w + gn_b
    bonus = _split_dot(_split_dot(r * k2 * r_k, red, SEG_TERMS), exp_m, SEG_TERMS) * v
    o_ref[...] = ((yn + bonus) * g).astype(BF16).reshape(BATCH_PER_STEP, CHUNK, D_GRP)


def _rwkv(p_r, mu, w0, w2p, a0, a2p, g2, k_k, k_a, r_k, gn_w, gn_b, w_gate_up, w_down):
    bsz, seq, _ = p_r.shape
    assert bsz % BATCH_PER_STEP == 0
    n_chunk = seq // CHUNK
    n_step = (bsz // BATCH_PER_STEP) * n_chunk
    wgu2d = w_gate_up.reshape(-1, w_gate_up.shape[-1])
    wd2d = w_down.reshape(-1, w_down.shape[-1])
    assert wgu2d.shape[0] % (8 * n_step) == 0 and wd2d.shape[0] == wgu2d.shape[0]
    slab = wgu2d.shape[0] // n_step
    masks = _scan_masks()
    const = lambda b, s: (0, 0)
    step = lambda b, s: (b * n_chunk + s, 0)
    vec = pl.BlockSpec((1, D_GRP), const)
    y, wgu_bf, wd_bf = pl.pallas_call(
        _rwkv_kernel,
        grid=(bsz // BATCH_PER_STEP, n_chunk),
        in_specs=[pl.BlockSpec((BATCH_PER_STEP, CHUNK, RWKV_COLS), lambda b, s: (b, s, 0)),
                  pl.BlockSpec(masks.shape, lambda b, s: (0, 0, 0)),
                  pl.BlockSpec((1, RWKV_COLS), const),
                  vec, pl.BlockSpec((LANES, D_GRP), const),
                  vec, pl.BlockSpec((LANES, D_GRP), const),
                  pl.BlockSpec((LANES, D_GRP), const),
                  vec, vec, vec, vec, vec,
                  pl.BlockSpec((slab, wgu2d.shape[1]), step),
                  pl.BlockSpec((slab, wd2d.shape[1]), step)],
        out_specs=[pl.BlockSpec((BATCH_PER_STEP, CHUNK, D_GRP), lambda b, s: (b, s, 0)),
                   pl.BlockSpec((slab, wgu2d.shape[1]), step),
                   pl.BlockSpec((slab, wd2d.shape[1]), step)],
        out_shape=[jax.ShapeDtypeStruct((bsz, seq, D_GRP), BF16),
                   jax.ShapeDtypeStruct(wgu2d.shape, BF16),
                   jax.ShapeDtypeStruct(wd2d.shape, BF16)],
        scratch_shapes=[pltpu.VMEM((BATCH_PER_STEP, 1, RWKV_COLS), F32),
                        pltpu.VMEM((BATCH_PER_STEP, N_HEADS // HEADS_PER_SCAN, SCAN_W, SCAN_W), F32)],
        compiler_params=pltpu.CompilerParams(
            dimension_semantics=("parallel", "arbitrary"), vmem_limit_bytes=VMEM_LIMIT),
        name="rwkv",
    )(p_r, masks, mu, w0, w2p, a0, a2p, g2, k_k, k_a, r_k, gn_w, gn_b, wgu2d, wd2d)
    return y, wgu_bf.reshape(w_gate_up.shape), wd_bf.reshape(w_down.shape)


def _fox_kernel(q_ref, qb_ref, k_ref, kb_ref, vt_ref, og_ref, ong_ref, o_ref, m_ref, l_ref, acc_ref,
                *, seq):
    hp = pl.program_id(1)
    lane = _iota((1, LANES), 1)
    q = q_ref[...]
    qb = qb_ref[...]
    zero = jnp.zeros_like(q)
    qcat = [jnp.concatenate([jnp.where(lane // HEAD_DIM == hh, q, zero),
                             jnp.where(lane // 8 == hp * 2 + hh, qb, zero)], axis=1)
            for hh in range(2)]
    keys = min(FOX_SUB_KEYS, seq)
    n_sub = seq // keys
    half = keys // 2
    mask_a = _iota((half, half), 1) >= _iota((half, half), 0)
    mask_b = _iota((keys, half), 1) + half >= _iota((keys, half), 0)

    m_ref[...] = jnp.full(m_ref.shape, -jnp.inf, F32)
    l_ref[...] = jnp.zeros(l_ref.shape, F32)
    acc_ref[...] = jnp.zeros(acc_ref.shape, F32)

    def plan(s):
        lo = s * keys
        pieces = [(half, slice(lo, lo + half), mask_a), (keys, slice(lo + half, lo + keys), mask_b)]
        if lo + keys < seq:
            pieces.append((keys, slice(lo + keys, seq), None))
        return pieces

    def scores(s):
        lo = s * keys
        kcat = jnp.concatenate([k_ref[lo:lo + keys, :], kb_ref[lo:lo + keys, :]], axis=1)
        return [lax.dot_general(kcat[:nk], qc[qs, :], _NT, preferred_element_type=F32)
                for qc in qcat for nk, qs, _ in plan(s)]

    pending = scores(0)
    for s in range(n_sub):
        lo = s * keys
        nxt = scores(s + 1) if s + 1 < n_sub else None
        n_piece = len(plan(s))
        pieces = []
        for hh in range(2):
            for (nk, qs, mask), st in zip(plan(s), pending[hh * n_piece:(hh + 1) * n_piece]):
                pieces.append((hh, nk, qs, st if mask is None else jnp.where(mask, st, -jnp.inf)))
        m_old = [m_ref[hh, :, qs] for hh, _, qs, _ in pieces]
        m_new = [jnp.maximum(m, jnp.max(st, axis=0, keepdims=True))
                 for m, (_, _, _, st) in zip(m_old, pieces)]
        pts = [jnp.exp2(st - m) for (_, _, _, st), m in zip(pieces, m_new)]
        pvs = [jnp.dot(vt_ref[:, lo:lo + nk], pt.astype(BF16), preferred_element_type=F32)
               for (_, nk, _, _), pt in zip(pieces, pts)]
        for (hh, _, qs, _), mo, mn, pt, pv in zip(pieces, m_old, m_new, pts, pvs):
            alpha = jnp.exp2(mo - mn)
            m_ref[hh, :, qs] = mn
            l_ref[hh, :, qs] = alpha * l_ref[hh, :, qs] + jnp.sum(pt, axis=0, keepdims=True)
            acc_ref[hh, :, qs] = (alpha * acc_ref[hh, :, qs]
                                  + pv[hh * HEAD_DIM:(hh + 1) * HEAD_DIM, :])
        pending = nxt

    outs = []
    for hh in range(2):
        o = acc_ref[hh] / l_ref[hh]
        outs.append(o * lax.rsqrt(jnp.mean(o * o, axis=0, keepdims=True) + NORM_EPS))
    o = jnp.concatenate(outs, axis=0).T
    o_ref[...] = (o * ong_ref[...] * _sigmoid(og_ref[...].astype(F32))).astype(BF16)


def _fox(p_x, k_bias, q_bias, o_gain):
    bsz, seq, _ = p_x.shape
    npair = N_HEADS // 2
    v_t = jnp.transpose(p_x[:, :, 2 * D_GRP:3 * D_GRP], (0, 2, 1))
    return pl.pallas_call(
        functools.partial(_fox_kernel, seq=seq),
        grid=(bsz, npair),
        in_specs=[pl.BlockSpec((None, seq, LANES), lambda b, h: (b, 0, h)),
                  pl.BlockSpec((None, seq, LANES), lambda b, h: (b, 0, 0)),
                  pl.BlockSpec((None, seq, LANES), lambda b, h: (b, 0, npair + h)),
                  pl.BlockSpec((None, seq, LANES), lambda b, h: (b, 0, 0)),
                  pl.BlockSpec((None, LANES, seq), lambda b, h: (b, h, 0)),
                  pl.BlockSpec((None, seq, LANES), lambda b, h: (b, 0, 3 * npair + h)),
                  pl.BlockSpec((1, LANES), lambda b, h: (0, 0))],
        out_specs=pl.BlockSpec((None, seq, LANES), lambda b, h: (b, 0, h)),
        out_shape=jax.ShapeDtypeStruct((bsz, seq, D_GRP), BF16),
        scratch_shapes=[pltpu.VMEM((2, 1, seq), F32), pltpu.VMEM((2, 1, seq), F32),
                        pltpu.VMEM((2, HEAD_DIM, seq), F32)],
        compiler_params=pltpu.CompilerParams(
            dimension_semantics=("parallel", "parallel"), vmem_limit_bytes=VMEM_LIMIT),
        name="fox",
    )(p_x, q_bias, p_x, k_bias, v_t, p_x, o_gain)


def _outproj_kernel(x_ref, yr_ref, yf_ref, g1_ref, sh_ref, sc_ref, ng_ref, wor_ref, wof_ref,
                    wrt_ref, wrl_ref, brt_ref, x1_ref, h2_ref, idx_ref, gate_ref, rank_ref, cnt_ref,
                    carry_ref):
    @pl.when(pl.program_id(0) == 0)
    def _():
        carry_ref[...] = jnp.zeros_like(carry_ref)

    y = (jnp.dot(yr_ref[...], wor_ref[...], preferred_element_type=F32)
         + jnp.dot(yf_ref[...], wof_ref[...], preferred_element_type=F32))
    x1 = x_ref[...] + g1_ref[...] * y
    x1_ref[...] = x1
    tm = x1.shape[0]
    h = x1 * lax.rsqrt(jnp.mean(x1 * x1, axis=-1, keepdims=True) + NORM_EPS) * ng_ref[...]
    h2 = h * (1.0 + sc_ref[...]) + sh_ref[...]
    h2_ref[...] = _pack_rows(h2)

    h_hi = h2.astype(BF16)
    h_lo = (h2 - h_hi.astype(F32)).astype(BF16)
    logits = (lax.dot_general(wrt_ref[...], h_hi, _NT, preferred_element_type=F32)
              + lax.dot_general(wrt_ref[...], h_lo, _NT, preferred_element_type=F32)
              + lax.dot_general(wrl_ref[...], h_hi, _NT, preferred_element_type=F32))
    lg = logits[:N_EXPERTS, :] + brt_ref[...]
    expert = _iota((N_EXPERTS, tm), 0)
    picks = []
    hot_sum = jnp.zeros((N_EXPERTS, tm), F32)
    for _ in range(TOP_K):
        m = jnp.max(lg, axis=0, keepdims=True)
        sel = jnp.min(jnp.where(lg == m, expert, N_EXPERTS), axis=0, keepdims=True)
        hot = expert == sel
        picks.append((m, sel, hot))
        hot_sum = hot_sum + hot.astype(F32)
        lg = jnp.where(hot, -jnp.inf, lg)
    es = [jnp.exp(m - picks[0][0]) for m, _, _ in picks]
    den = es[0] + es[1] + es[2] + es[3]

    earlier = (_iota((tm, tm), 0) < _iota((tm, tm), 1)).astype(BF16)
    before = jnp.dot(hot_sum.astype(BF16), earlier, preferred_element_type=F32) + carry_ref[...]
    ranks = [jnp.sum(jnp.where(hot, before, 0.0), axis=0, keepdims=True).astype(jnp.int32)
             for _, _, hot in picks]
    pad_i = jnp.zeros((8 - TOP_K, tm), jnp.int32)
    idx_ref[...] = jnp.concatenate([sel for _, sel, _ in picks] + [pad_i], axis=0)
    gate_ref[...] = jnp.concatenate([e / den for e in es] + [pad_i.astype(F32)], axis=0)
    rank_ref[...] = jnp.concatenate(ranks + [pad_i], axis=0)
    carry_ref[...] = carry_ref[...] + jnp.sum(hot_sum, axis=1, keepdims=True)
    cnt_ref[...] = jnp.broadcast_to(carry_ref[...], cnt_ref.shape)


def _outproj(x2d, y_r, y_f, gate1, shift2, scale2, norm_g, wo_r, wo_f, w_rt, b_rt, tm, seq,
             row0, t):
    w_rt_hi = w_rt.astype(BF16)
    w_rt_lo = (w_rt - w_rt_hi.astype(F32)).astype(BF16)
    per_b = seq // tm
    blk0 = row0 // tm
    const = lambda i: (0, 0)
    rows = lambda i: (i, 0)
    rows_in = lambda i: (i + blk0, 0)
    mod = pl.BlockSpec((None, 1, D_MODEL), lambda i: ((i + blk0) // per_b, 0, 0))
    return pl.pallas_call(
        _outproj_kernel,
        grid=(t // tm,),
        in_specs=[pl.BlockSpec((tm, D_MODEL), rows_in),
                  pl.BlockSpec((tm, D_GRP), rows_in),
                  pl.BlockSpec((tm, D_GRP), rows_in),
                  mod, mod, mod,
                  pl.BlockSpec((1, D_MODEL), const),
                  pl.BlockSpec((D_GRP, D_MODEL), const),
                  pl.BlockSpec((D_GRP, D_MODEL), const),
                  pl.BlockSpec((LANES, D_MODEL), const),
                  pl.BlockSpec((LANES, D_MODEL), const),
                  pl.BlockSpec((N_EXPERTS, 1), const)],
        out_specs=[pl.BlockSpec((tm, D_MODEL), rows),
                   pl.BlockSpec((tm, D_PACK), rows),
                   pl.BlockSpec((8, tm), lambda i: (0, i)),
                   pl.BlockSpec((8, tm), lambda i: (0, i)),
                   pl.BlockSpec((8, tm), lambda i: (0, i)),
                   pl.BlockSpec((N_EXPERTS, LANES), const)],
        out_shape=[jax.ShapeDtypeStruct((t, D_MODEL), F32),
                   jax.ShapeDtypeStruct((t, D_PACK), jnp.uint32),
                   jax.ShapeDtypeStruct((8, t), jnp.int32),
                   jax.ShapeDtypeStruct((8, t), F32),
                   jax.ShapeDtypeStruct((8, t), jnp.int32),
                   jax.ShapeDtypeStruct((N_EXPERTS, LANES), F32)],
        scratch_shapes=[pltpu.VMEM((N_EXPERTS, 1), F32)],
        compiler_params=pltpu.CompilerParams(
            dimension_semantics=("arbitrary",), vmem_limit_bytes=VMEM_LIMIT),
        name="outproj",
    )(x2d, y_r, y_f, gate1, shift2, scale2, norm_g, wo_r, wo_f, w_rt_hi, w_rt_lo, b_rt)


SC_CORES = 2
SC_SUBCORES = 16
SC_ROWS = 64


def _sc_gather_rows(idx, src):
    n_workers = SC_CORES * SC_SUBCORES
    m = idx.shape[0]
    d = src.shape[1]
    assert m % (n_workers * SC_ROWS) == 0
    n_chunks = m // (n_workers * SC_ROWS)
    mesh = plsc.VectorSubcoreMesh(core_axis_name="c", subcore_axis_name="s")

    @functools.partial(
        pl.kernel, mesh=mesh,
        out_type=jax.ShapeDtypeStruct((m, d), src.dtype),
        scratch_types=[pltpu.VMEM((n_chunks, SC_ROWS), jnp.int32),
                       pltpu.VMEM((SC_ROWS, d), src.dtype),
                       pltpu.SemaphoreType.DMA],
        name="sc_gather")
    def gather(src_hbm, idx_hbm, out_hbm, idx_v, rows_v, sem):
        wid = lax.axis_index("s") * SC_CORES + lax.axis_index("c")
        pltpu.sync_copy(idx_hbm.at[wid], idx_v)

        @pl.loop(0, n_chunks)
        def _(j):
            pltpu.async_copy(src_hbm.at[idx_v.at[j]], rows_v, sem).wait()
            pltpu.sync_copy(rows_v, out_hbm.at[pl.ds((wid * n_chunks + j) * SC_ROWS, SC_ROWS)])

    return gather(src, idx.reshape(n_workers, n_chunks, SC_ROWS))


def _sc_scatter_rows(src, dest, n_out):
    n_workers = SC_CORES * SC_SUBCORES
    t, d = src.shape
    n_slot = dest.shape[0]
    assert t % (n_workers * SC_ROWS) == 0
    n_chunks = t // (n_workers * SC_ROWS)
    mesh = plsc.VectorSubcoreMesh(core_axis_name="c", subcore_axis_name="s")
    idx = dest.reshape(n_slot, n_workers, n_chunks, SC_ROWS).transpose(1, 2, 0, 3)
    idx = idx.reshape(n_workers, n_chunks * n_slot, SC_ROWS)

    @functools.partial(
        pl.kernel, mesh=mesh,
        out_type=jax.ShapeDtypeStruct((n_out, d), src.dtype),
        scratch_types=[pltpu.VMEM((n_chunks * n_slot, SC_ROWS), jnp.int32),
                       pltpu.VMEM((SC_ROWS, d), src.dtype)],
        name="sc_scatter")
    def scatter(src_hbm, idx_hbm, out_hbm, idx_v, rows_v):
        wid = lax.axis_index("s") * SC_CORES + lax.axis_index("c")
        pltpu.sync_copy(idx_hbm.at[wid], idx_v)

        @pl.loop(0, n_chunks)
        def _(j):
            pltpu.sync_copy(src_hbm.at[pl.ds((wid * n_chunks + j) * SC_ROWS, SC_ROWS)], rows_v)
            for k in range(n_slot):
                pltpu.sync_copy(rows_v, out_hbm.at[idx_v.at[j * n_slot + k]])

    return scatter(src, idx)


def _expert_kernel(be_ref, nv_ref, x_ref, wgu_ref, bgu_ref, wd_ref, bd_ref, o_ref):
    del be_ref
    valid = _iota((EXPERT_BLOCK, 1), 0) < nv_ref[pl.program_id(0)]
    lo, hi = _unpack_rows(jnp.where(valid, x_ref[...], jnp.uint32(0)))
    x = jnp.concatenate([lo.astype(BF16), hi.astype(BF16)], axis=1)
    half = EXPERT_BLOCK // 2
    gus = [jnp.dot(x[r * half:(r + 1) * half], wgu_ref[...], preferred_element_type=F32)
           + bgu_ref[...] for r in range(2)]
    for r, gu in enumerate(gus):
        gate = jnp.minimum(gu[:, :D_MODEL], SWIGLU_LIMIT)
        up = jnp.clip(gu[:, D_MODEL:], -SWIGLU_LIMIT, SWIGLU_LIMIT)
        act = gate * _sigmoid(SWIGLU_ALPHA * gate) * (up + 1.0)
        o_ref[r * half:(r + 1) * half, :] = _pack_rows(
            jnp.dot(act.astype(BF16), wd_ref[...], preferred_element_type=F32) + bd_ref[...])


def _experts(block_e, n_valid, xs, w_gu, b_gu, w_d, b_d):
    n_blocks = block_e.shape[0]
    grid_spec = pltpu.PrefetchScalarGridSpec(
        num_scalar_prefetch=2,
        grid=(n_blocks,),
        in_specs=[pl.BlockSpec((EXPERT_BLOCK, D_PACK), lambda j, be, nv: (j, 0)),
                  pl.BlockSpec((None, D_MODEL, 2 * D_MODEL), lambda j, be, nv: (be[j], 0, 0)),
                  pl.BlockSpec((None, 1, 2 * D_MODEL), lambda j, be, nv: (be[j], 0, 0)),
                  pl.BlockSpec((None, D_MODEL, D_MODEL), lambda j, be, nv: (be[j], 0, 0)),
                  pl.BlockSpec((None, 1, D_MODEL), lambda j, be, nv: (be[j], 0, 0))],
        out_specs=pl.BlockSpec((EXPERT_BLOCK, D_PACK), lambda j, be, nv: (j, 0)),
    )
    return pl.pallas_call(
        _expert_kernel,
        grid_spec=grid_spec,
        out_shape=jax.ShapeDtypeStruct(xs.shape, jnp.uint32),
        compiler_params=pltpu.CompilerParams(
            dimension_semantics=("arbitrary",), vmem_limit_bytes=VMEM_LIMIT),
        name="experts",
    )(block_e, n_valid, xs, w_gu, b_gu, w_d, b_d)


COMBINE_TOKENS = 1024
MOE_SPLITS = 2


def _combine_kernel(yg_ref, x1_ref, gate_ref, g2_ref, fg_ref, o_ref):
    gates = gate_ref[...].T
    acc_lo = acc_hi = None
    for kk in range(TOP_K):
        lo, hi = _unpack_rows(yg_ref[kk * COMBINE_TOKENS:(kk + 1) * COMBINE_TOKENS, :])
        g = gates[:, kk:kk + 1]
        acc_lo = g * lo if acc_lo is None else acc_lo + g * lo
        acc_hi = g * hi if acc_hi is None else acc_hi + g * hi
    x2 = x1_ref[...] + g2_ref[...] * jnp.concatenate([acc_lo, acc_hi], axis=1)
    o_ref[...] = x2 * lax.rsqrt(jnp.mean(x2 * x2, axis=-1, keepdims=True) + NORM_EPS) * fg_ref[...]


def _combine_kernel_into(prev_ref, *refs):
    del prev_ref
    _combine_kernel(*refs)


def _combine(yg, x1, gates, gate2, final_g, seq, row0, t_total, prev):
    t = x1.shape[0]
    tm = COMBINE_TOKENS
    per_b = seq // tm
    blk0 = row0 // tm
    rows = lambda i: (i, 0)
    in_specs = [pl.BlockSpec((TOP_K * tm, D_PACK), rows),
                pl.BlockSpec((tm, D_MODEL), rows),
                pl.BlockSpec((8, tm), lambda i: (0, i)),
                pl.BlockSpec((None, 1, D_MODEL), lambda i: ((i + blk0) // per_b, 0, 0)),
                pl.BlockSpec((1, D_MODEL), lambda i: (0, 0))]
    args = (yg, x1, gates, gate2, final_g)
    if prev is not None:
        in_specs = [pl.BlockSpec(memory_space=pl.ANY)] + in_specs
        args = (prev,) + args
    return pl.pallas_call(
        _combine_kernel if prev is None else _combine_kernel_into,
        grid=(t // tm,),
        in_specs=in_specs,
        out_specs=pl.BlockSpec((tm, D_MODEL), lambda i: (i + blk0, 0)),
        out_shape=jax.ShapeDtypeStruct((t_total, D_MODEL), F32),
        input_output_aliases={} if prev is None else {0: 0},
        compiler_params=pltpu.CompilerParams(
            dimension_semantics=("parallel",), vmem_limit_bytes=VMEM_LIMIT),
        name="combine",
    )(*args)


def _moe(h2, idx, gates, rank, counts, x1, gate2, final_g, w_gu, b_gu, w_d, b_d, seq,
         row0, t_total, prev):
    t = h2.shape[0]
    n_slots = t * TOP_K
    n_blocks = -(-n_slots // EXPERT_BLOCK) + N_EXPERTS
    cap = n_blocks * EXPERT_BLOCK
    padded = (counts + EXPERT_BLOCK - 1) // EXPERT_BLOCK * EXPERT_BLOCK
    pad_ends = jnp.cumsum(padded)
    pad_starts = pad_ends - padded
    experts = jnp.arange(N_EXPERTS, dtype=jnp.int32)
    dest = jnp.sum(jnp.where(idx[..., None] == experts, pad_starts, 0), axis=-1) + rank
    block_starts = jnp.arange(n_blocks, dtype=jnp.int32) * EXPERT_BLOCK
    block_e = jnp.minimum(jnp.sum(block_starts[:, None] >= pad_ends[None, :], axis=1),
                          N_EXPERTS - 1).astype(jnp.int32)
    n_valid = jnp.clip(counts[block_e] - (block_starts - pad_starts[block_e]), 0, EXPERT_BLOCK)

    xs = _sc_scatter_rows(h2, dest, cap)
    yb = _experts(block_e, n_valid.astype(jnp.int32), xs, w_gu, b_gu, w_d, b_d)
    dest_blocks = dest.reshape(TOP_K, -1, COMBINE_TOKENS).transpose(1, 0, 2).reshape(-1)
    yg = _sc_gather_rows(dest_blocks, yb)
    return _combine(yg, x1, gates, gate2, final_g, seq, row0, t_total, prev)


def _layer(x, c_mod, norm1_g, w_in, mu_shift, w0, w2, a0, a2, g2, k_k, k_a, r_k, gn_w, gn_b, b_f,
           q_norm_g, k_norm_g, o_norm_g, w_out, norm2_g, w_router, b_router, w_gate_up,
           b_gate_up, w_down, b_down, final_g, tm_in, tm_out):
    bsz, seq, _ = x.shape
    shift1, scale1, gate1, shift2, scale2, gate2 = (
        m.reshape(bsz, 1, D_MODEL) for m in jnp.split(c_mod, 6, axis=-1))
    row = lambda v: v.reshape(1, -1)

    w_r = w_in[:, :RWKV_COLS].astype(BF16)
    w_x = w_in[:, RWKV_COLS:RWKV_COLS + FOX_MAIN].astype(BF16)
    w_f = w_in[:, RWKV_COLS + FOX_MAIN:].T
    b_fp = jnp.pad(b_f, (0, LANES - N_HEADS)).reshape(1, LANES)
    qk_gain = jnp.concatenate([jnp.tile(q_norm_g, N_HEADS) * (HEAD_DIM ** -0.5 * LOG2_E),
                               jnp.tile(k_norm_g, N_HEADS)]).reshape(1, -1)
    p_r, p_x, k_bias, q_bias = _inproj(x, shift1, scale1, row(norm1_g), w_r, w_x, w_f, b_fp,
                                       qk_gain, tm_in)

    zeros = jnp.zeros((LANES - 64, D_GRP), F32)
    w2p = jnp.concatenate([w2, zeros], axis=0).astype(BF16)
    a2p = jnp.concatenate([zeros, a2], axis=0).astype(BF16)
    y_r, w_gu, w_d = _rwkv(p_r, row(mu_shift), row(w0), w2p, row(a0), a2p, g2.astype(BF16),
                           row(k_k), row(k_a), row(r_k), row(gn_w), row(gn_b), w_gate_up, w_down)

    y_f = _fox(p_x, k_bias, q_bias, jnp.tile(o_norm_g, 2).reshape(1, LANES))

    t = bsz * seq
    w_rt = jnp.pad(w_router.T, ((0, LANES - N_EXPERTS), (0, 0)))
    b_rt = b_router.reshape(N_EXPERTS, 1)
    wo = w_out.astype(BF16)
    b_gu, b_d = b_gate_up.reshape(N_EXPERTS, 1, -1), b_down.reshape(N_EXPERTS, 1, -1)
    t_part = t // MOE_SPLITS
    out = None
    for part in range(MOE_SPLITS):
        row0 = part * t_part
        x1, h2, idx, gates, rank, cnt = _outproj(
            x.reshape(t, D_MODEL), y_r.reshape(t, D_GRP), y_f.reshape(t, D_GRP), gate1, shift2,
            scale2, row(norm2_g), wo[:D_GRP], wo[D_GRP:], w_rt, b_rt, tm_out, seq, row0, t_part)
        counts = cnt[:, 0].astype(jnp.int32)
        out = _moe(h2, idx[:TOP_K], gates, rank[:TOP_K], counts, x1, gate2, row(final_g),
                   w_gu, b_gu, w_d, b_d, seq, row0, t, out)
    return out.reshape(bsz, seq, D_MODEL)


def kernel(x, c, w_ada, b_ada, norm1_g, w_in, mu_shift, w0, w2, a0, a2, g2, k_k, k_a, r_k, gn_w,
           gn_b, b_f, q_norm_g, k_norm_g, o_norm_g, w_out, norm2_g, w_router, b_router, w_gate_up,
           b_gate_up, w_down, b_down, final_g):
    assert w_ada.shape[0] == 1, "single-layer block"
    c_mod = _adaln(c, w_ada[0], b_ada[0])
    return _layer(x, c_mod, norm1_g[0], w_in[0], mu_shift[0], w0[0], w2[0], a0[0], a2[0], g2[0],
                  k_k[0], k_a[0], r_k[0], gn_w[0], gn_b[0], b_f[0], q_norm_g[0], k_norm_g[0],
                  o_norm_g[0], w_out[0], norm2_g[0], w_router[0], b_router[0], w_gate_up[0],
                  b_gate_up[0], w_down[0], b_down[0], final_g,
                  tm_in=min(512, x.shape[1]), tm_out=min(1024, x.shape[1]))
```

```python
import functools

import jax
import jax.numpy as jnp
from jax import lax
from jax.experimental import pallas as pl
from jax.experimental.pallas import tpu as pltpu
from jax.experimental.pallas import tpu_sc as plsc

F32 = jnp.float32
BF16 = jnp.bfloat16
HIGHEST = lax.Precision.HIGHEST

D_MODEL = 1024
HEAD_DIM = 64
N_HEADS = 8
D_GRP = N_HEADS * HEAD_DIM
RWKV_COLS = 1792
LORA_OFF = 3 * D_GRP
GATE_OFF = LORA_OFF + 128
FOX_MAIN = 4 * D_GRP
N_EXPERTS = 32
TOP_K = 4
EXPERT_BLOCK = 512
SWIGLU_ALPHA = 1.702
SWIGLU_LIMIT = 7.0
NORM_EPS = 1e-6
GN_EPS = 64e-5
LOG2_E = 1.4426950408889634
LANES = 128
CHUNK = 64
FOX_SUB_KEYS = 512
HEADS_PER_SCAN = 4
SCAN_W = HEADS_PER_SCAN * HEAD_DIM
SEG_TERMS = 1
CUM_TERMS = 2
VMEM_LIMIT = 56 * 1024 * 1024


def _dot(a, b):
    return jnp.dot(a.astype(BF16), b.astype(BF16), preferred_element_type=F32)


def _fdot(a, b):
    return jnp.dot(a, b, precision=HIGHEST, preferred_element_type=F32)


def _split_dot(x, m, terms):
    acc = None
    rem = x
    for _ in range(terms):
        part = rem.astype(BF16)
        rem = rem - part.astype(F32)
        d = jnp.dot(part, m, preferred_element_type=F32)
        acc = d if acc is None else acc + d
    return acc


def _tri_dot(m, x, terms):
    acc = None
    rem = x
    for _ in range(terms):
        part = rem.astype(BF16)
        rem = rem - part.astype(F32)
        d = jnp.dot(m, part, preferred_element_type=F32)
        acc = d if acc is None else acc + d
    return acc


def _iota(shape, dim):
    return lax.broadcasted_iota(jnp.int32, shape, dim)


def _seg_reduce_mat(n):
    return (_iota((n, LANES), 0) // HEAD_DIM == _iota((n, LANES), 1)).astype(BF16)


def _seg_expand_mat(n):
    return (_iota((LANES, n), 1) // HEAD_DIM == _iota((LANES, n), 0)).astype(BF16)


D_PACK = D_MODEL // 2


def _pack_rows(x):
    lo = lax.bitcast_convert_type(x[:, :D_PACK].astype(BF16).astype(F32), jnp.uint32)
    hi = lax.bitcast_convert_type(x[:, D_PACK:].astype(BF16).astype(F32), jnp.uint32)
    return hi | (lo >> 16)


def _unpack_rows(p):
    lo = lax.bitcast_convert_type(p << 16, F32)
    hi = lax.bitcast_convert_type(p & jnp.uint32(0xFFFF0000), F32)
    return lo, hi


def _log_sigmoid(z):
    return jnp.minimum(z, 0.0) - jnp.log(1.0 + jnp.exp(-jnp.abs(z)))


def _sigmoid(z):
    return 1.0 / (1.0 + jnp.exp(-z))


def _adaln_kernel(c_ref, w_ref, b_ref, o_ref):
    c = c_ref[...]
    o_ref[...] = _fdot(c * _sigmoid(c), w_ref[...]) + b_ref[...]


def _adaln(c, w_ada, b_ada):
    bsz = c.shape[0]
    n_mod = w_ada.shape[1] // D_MODEL
    return pl.pallas_call(
        _adaln_kernel,
        grid=(n_mod,),
        in_specs=[pl.BlockSpec((bsz, D_MODEL), lambda j: (0, 0)),
                  pl.BlockSpec((D_MODEL, D_MODEL), lambda j: (0, j)),
                  pl.BlockSpec((1, D_MODEL), lambda j: (0, j))],
        out_specs=pl.BlockSpec((bsz, D_MODEL), lambda j: (0, j)),
        out_shape=jax.ShapeDtypeStruct((bsz, n_mod * D_MODEL), F32),
        name="adaln",
    )(c, w_ada, b_ada.reshape(1, -1))


def _inproj_kernel(x_ref, sh_ref, sc_ref, g_ref, wr_ref, wx_ref, wft_ref, bf_ref, qkg_ref,
                   pr_ref, px_ref, kb_ref, qb_ref, vt_ref, carry_ref):
    @pl.when(pl.program_id(1) == 0)
    def _():
        carry_ref[...] = jnp.zeros_like(carry_ref)

    x = x_ref[...]
    tm = x.shape[0]
    h = x * lax.rsqrt(jnp.mean(x * x, axis=-1, keepdims=True) + NORM_EPS) * g_ref[...]
    h = h * (1.0 + sc_ref[...]) + sh_ref[...]
    hb = h.astype(BF16)

    pr_ref[...] = jnp.dot(hb, wr_ref[...], preferred_element_type=F32).astype(BF16)

    px = jnp.dot(hb, wx_ref[...], preferred_element_type=F32)
    qk = px[:, :2 * D_GRP]
    ss = _split_dot(qk * qk, _seg_reduce_mat(2 * D_GRP), SEG_TERMS)
    inv = lax.rsqrt(ss * (1.0 / HEAD_DIM) + NORM_EPS)
    qk = qk * _split_dot(inv, _seg_expand_mat(2 * D_GRP), SEG_TERMS) * qkg_ref[...]
    px_ref[:, :2 * D_GRP] = qk.astype(BF16)
    px_ref[:, 2 * D_GRP:] = px[:, 2 * D_GRP:].astype(BF16)
    vt_ref[...] = px[:, 2 * D_GRP:3 * D_GRP].T.astype(BF16)

    lane = _iota((1, LANES), 1)
    z = jnp.zeros((tm, LANES), F32)
    for hd in range(N_HEADS):
        zh = jnp.sum(h * wft_ref[hd:hd + 1, :], axis=-1, keepdims=True)
        z = jnp.where(lane == hd, zh, z)
    cum = _log_sigmoid(z + bf_ref[...])
    row_id = _iota((tm, 1), 0)
    shift = 1
    while shift < tm:
        cum = cum + jnp.where(row_id >= shift, pltpu.roll(cum, shift, axis=0), 0.0)
        shift *= 2
    cum = cum + carry_ref[...]
    carry_ref[...] = cum[tm - 1:tm, :]

    parts = []
    rem = cum * LOG2_E
    for _ in range(3):
        part = rem.astype(BF16)
        rem = rem - part.astype(F32)
        parts.append(part)
    src, dst = _iota((LANES, LANES), 0), _iota((LANES, LANES), 1)

    def spread(offset):
        return sum(jnp.dot(part, ((dst == 8 * src + offset + t) & (src < N_HEADS)).astype(BF16),
                           preferred_element_type=F32) for t, part in enumerate(parts))

    slot = _iota((1, LANES), 1) % 8
    kb_ref[...] = (jnp.where((slot >= 3) & (slot < 6), 1.0, 0.0) - spread(0)).astype(BF16)
    qb_ref[...] = (jnp.where(slot < 3, 1.0, 0.0) + spread(3)).astype(BF16)


def _inproj(x, shift, scale, g, w_r, w_x, w_f_t, b_f, qk_gain, tm):
    bsz, seq, _ = x.shape
    const = lambda b, s: (0, 0)
    return pl.pallas_call(
        _inproj_kernel,
        grid=(bsz, seq // tm),
        in_specs=[pl.BlockSpec((None, tm, D_MODEL), lambda b, s: (b, s, 0)),
                  pl.BlockSpec((None, 1, D_MODEL), lambda b, s: (b, 0, 0)),
                  pl.BlockSpec((None, 1, D_MODEL), lambda b, s: (b, 0, 0)),
                  pl.BlockSpec((1, D_MODEL), const),
                  pl.BlockSpec((D_MODEL, RWKV_COLS), const),
                  pl.BlockSpec((D_MODEL, FOX_MAIN), const),
                  pl.BlockSpec((N_HEADS, D_MODEL), const),
                  pl.BlockSpec((1, LANES), const),
                  pl.BlockSpec((1, 2 * D_GRP), const)],
        out_specs=[pl.BlockSpec((None, tm, RWKV_COLS), lambda b, s: (b, s, 0)),
                   pl.BlockSpec((None, tm, FOX_MAIN), lambda b, s: (b, s, 0)),
                   pl.BlockSpec((None, tm, LANES), lambda b, s: (b, s, 0)),
                   pl.BlockSpec((None, tm, LANES), lambda b, s: (b, s, 0)),
                   pl.BlockSpec((None, D_GRP, tm), lambda b, s: (b, 0, s))],
        out_shape=[jax.ShapeDtypeStruct((bsz, seq, RWKV_COLS), BF16),
                   jax.ShapeDtypeStruct((bsz, seq, FOX_MAIN), BF16),
                   jax.ShapeDtypeStruct((bsz, seq, LANES), BF16),
                   jax.ShapeDtypeStruct((bsz, seq, LANES), BF16),
                   jax.ShapeDtypeStruct((bsz, D_GRP, seq), BF16)],
        scratch_shapes=[pltpu.VMEM((1, LANES), F32)],
        compiler_params=pltpu.CompilerParams(
            dimension_semantics=("parallel", "arbitrary"), vmem_limit_bytes=VMEM_LIMIT),
        name="inproj",
    )(x, shift, scale, g, w_r, w_x, w_f_t, b_f, qk_gain)


_NN = (((1,), (0,)), ((), ()))
_NT = (((1,), (1,)), ((), ()))
_TN = (((0,), (0,)), ((), ()))
SCAN_N = HEADS_PER_SCAN * CHUNK
BATCH_PER_STEP = 8
INV_LEVELS = 5
M_HEAD, M_STRICT, M_INCL, M_EYE, M_BASE, M_OFF = 0, 1, 2, 3, 4, 5


def _bdot(a, b, dims):
    return lax.dot_general(a, b, dims, preferred_element_type=F32)


def _scan_masks():
    rr, cc = _iota((SCAN_N, SCAN_W), 0), _iota((SCAN_N, SCAN_W), 1)
    ri, ci = _iota((SCAN_N, SCAN_N), 0), _iota((SCAN_N, SCAN_N), 1)
    same = ri // CHUNK == ci // CHUNK
    masks = [rr // CHUNK == cc // HEAD_DIM, same & (ri > ci), same & (ri >= ci), ri == ci,
             (ri // 2 == ci // 2) & (ri > ci)]
    blk = 2
    while blk < CHUNK:
        masks.append((ri // (2 * blk) == ci // (2 * blk)) & (ri // blk != ci // blk) & (ri > ci))
        blk *= 2
    return jnp.stack(masks).astype(BF16)


def _rwkv_kernel(p_ref, masks_ref, mu_ref, w0_ref, w2_ref, a0_ref, a2_ref, g2_ref, kk_ref, ka_ref,
                 rk_ref, gnw_ref, gnb_ref, wgu_ref, wd_ref, o_ref, wgu_bf_ref, wd_bf_ref,
                 last_ref, state_ref):
    wgu_bf_ref[...] = wgu_ref[...].astype(BF16)
    wd_bf_ref[...] = wd_ref[...].astype(BF16)

    @pl.when(pl.program_id(1) == 0)
    def _():
        last_ref[...] = jnp.zeros_like(last_ref)
        state_ref[...] = jnp.zeros_like(state_ref)

    mu, w0, w2, a0, a2, g2, k_k, k_a, r_k, gn_w, gn_b = (
        ref[...] for ref in (mu_ref, w0_ref, w2_ref, a0_ref, a2_ref, g2_ref, kk_ref, ka_ref,
                             rk_ref, gnw_ref, gnb_ref))
    rows = BATCH_PER_STEP * CHUNK
    p = p_ref[...].astype(F32).reshape(rows, RWKV_COLS)
    row_id = _iota((rows, 1), 0)
    prev = pltpu.roll(p, 1, axis=0)
    for bb in range(BATCH_PER_STEP):
        prev = jnp.where(row_id == bb * CHUNK, last_ref[bb], prev)
        last_ref[bb] = p[(bb + 1) * CHUNK - 1:(bb + 1) * CHUNK, :]
    pf = p + mu * (prev - p)
    r = pf[:, 0:D_GRP]
    k = pf[:, D_GRP:2 * D_GRP]
    v = pf[:, 2 * D_GRP:3 * D_GRP]
    lora = pf[:, LORA_OFF:GATE_OFF]
    gd = pf[:, GATE_OFF:RWKV_COLS]

    wlog = w0 + _dot(jnp.tanh(lora), w2)
    neg = -wlog
    softplus = jnp.maximum(neg, 0.0) + jnp.log(1.0 + jnp.exp(-jnp.abs(neg)))
    ld = -jnp.exp(-softplus - 0.5)
    a = _sigmoid(a0 + _dot(lora, a2))
    g = _dot(_sigmoid(gd), g2)

    red, exp_m = _seg_reduce_mat(D_GRP), _seg_expand_mat(D_GRP)
    kk = k * k_k
    n2 = _split_dot(kk * kk, red, SEG_TERMS)
    kk = kk * _split_dot(1.0 / jnp.maximum(jnp.sqrt(n2), 1e-12), exp_m, SEG_TERMS)
    k2 = k * (1.0 + (a - 1.0) * k_a)

    tr, tc = _iota((rows, rows), 0), _iota((rows, rows), 1)
    tri = ((tr >= tc) & (tr // CHUNK == tc // CHUNK)).astype(BF16)
    cl = _tri_dot(tri, ld, CUM_TERMS)
    cl_end = jnp.concatenate(
        [jnp.broadcast_to(cl[(bb + 1) * CHUNK - 1:(bb + 1) * CHUNK, :], (CHUNK, D_GRP))
         for bb in range(BATCH_PER_STEP)], axis=0)
    e_in = jnp.exp(cl)
    e_out = jnp.exp(-cl)
    e_rem = jnp.exp(cl_end - cl)
    p_end = jnp.exp(cl_end)
    kka = kk * a
    ops = [(-kk * jnp.exp(cl - ld)).astype(BF16), (kka * e_out).astype(BF16),
           (k2 * e_out).astype(BF16), (r * e_in).astype(BF16), v.astype(BF16),
           (kka * e_rem).astype(BF16), (k2 * e_rem).astype(BF16)]

    chains = [(bb, grp) for bb in range(BATCH_PER_STEP)
              for grp in range(N_HEADS // HEADS_PER_SCAN)]
    head_mask = masks_ref[M_HEAD]
    strict, incl = masks_ref[M_STRICT], masks_ref[M_INCL]

    def stacked(op, bb, grp):
        part = op[bb * CHUNK:(bb + 1) * CHUNK, grp * SCAN_W:(grp + 1) * SCAN_W]
        return jnp.concatenate([part] * HEADS_PER_SCAN, axis=0) * head_mask

    xs = [[stacked(op, bb, grp) for op in ops] for bb, grp in chains]
    st = [state_ref[bb, grp] for bb, grp in chains]
    sb = [s.astype(BF16) for s in st]
    nab = [_bdot(x[0], x[1], _NT).astype(BF16) for x in xs]
    aak = [_bdot(x[0], x[2], _NT).astype(BF16) * strict for x in xs]
    arb = [_bdot(x[3], x[1], _NT).astype(BF16) * incl for x in xs]
    ark = [_bdot(x[3], x[2], _NT).astype(BF16) * incl for x in xs]
    t_inv = [masks_ref[M_EYE] + n * masks_ref[M_BASE] for n in nab]
    for lvl in range(INV_LEVELS):
        half = [_bdot(t, n * masks_ref[M_OFF + lvl], _NN).astype(BF16) for t, n in zip(t_inv, nab)]
        t_inv = [t + _bdot(h, t, _NN).astype(BF16) for t, h in zip(t_inv, half)]
    rhs = [(_bdot(x[0], s, _NT) + _bdot(k, x[4], _NN)).astype(BF16)
           for x, s, k in zip(xs, sb, aak)]
    sa = [_bdot(t, h, _NN).astype(BF16) for t, h in zip(t_inv, rhs)]
    ys = [_bdot(x[3], s, _NT) + _bdot(b, u, _NN) + _bdot(k, x[4], _NN)
          for x, s, b, u, k in zip(xs, sb, arb, sa, ark)]
    for (bb, grp), x, s, u in zip(chains, xs, st, sa):
        decay = p_end[bb * CHUNK:bb * CHUNK + 1, grp * SCAN_W:(grp + 1) * SCAN_W]
        state_ref[bb, grp] = s * decay + _bdot(u, x[5], _TN) + _bdot(x[4], x[6], _TN)
    ys = [y[0:CHUNK] + y[CHUNK:2 * CHUNK] + y[2 * CHUNK:3 * CHUNK] + y[3 * CHUNK:4 * CHUNK]
          for y in ys]
    n_grp = N_HEADS // HEADS_PER_SCAN
    y = jnp.concatenate([jnp.concatenate(ys[bb * n_grp:(bb + 1) * n_grp], axis=1)
                         for bb in range(BATCH_PER_STEP)], axis=0)

    mean = _split_dot(_split_dot(y, red, SEG_TERMS) * (1.0 / HEAD_DIM), exp_m, SEG_TERMS)
    d = y - mean
    var = _split_dot(d * d, red, SEG_TERMS) * (1.0 / HEAD_DIM)
    yn = d * _split_dot(lax.rsqrt(var + GN_EPS), exp_m, SEG_TERMS) * gn_w + gn_b
    bonus = _split_dot(_split_dot(r * k2 * r_k, red, SEG_TERMS), exp_m, SEG_TERMS) * v
    o_ref[...] = ((yn + bonus) * g).astype(BF16).reshape(BATCH_PER_STEP, CHUNK, D_GRP)


def _rwkv(p_r, mu, w0, w2p, a0, a2p, g2, k_k, k_a, r_k, gn_w, gn_b, w_gate_up, w_down):
    bsz, seq, _ = p_r.shape
    assert bsz % BATCH_PER_STEP == 0
    n_chunk = seq // CHUNK
    n_step = (bsz // BATCH_PER_STEP) * n_chunk
    wgu2d = w_gate_up.reshape(-1, w_gate_up.shape[-1])
    wd2d = w_down.reshape(-1, w_down.shape[-1])
    assert wgu2d.shape[0] % (8 * n_step) == 0 and wd2d.shape[0] == wgu2d.shape[0]
    slab = wgu2d.shape[0] // n_step
    masks = _scan_masks()
    const = lambda b, s: (0, 0)
    step = lambda b, s: (b * n_chunk + s, 0)
    vec = pl.BlockSpec((1, D_GRP), const)
    y, wgu_bf, wd_bf = pl.pallas_call(
        _rwkv_kernel,
        grid=(bsz // BATCH_PER_STEP, n_chunk),
        in_specs=[pl.BlockSpec((BATCH_PER_STEP, CHUNK, RWKV_COLS), lambda b, s: (b, s, 0)),
                  pl.BlockSpec(masks.shape, lambda b, s: (0, 0, 0)),
                  pl.BlockSpec((1, RWKV_COLS), const),
                  vec, pl.BlockSpec((LANES, D_GRP), const),
                  vec, pl.BlockSpec((LANES, D_GRP), const),
                  pl.BlockSpec((LANES, D_GRP), const),
                  vec, vec, vec, vec, vec,
                  pl.BlockSpec((slab, wgu2d.shape[1]), step),
                  pl.BlockSpec((slab, wd2d.shape[1]), step)],
        out_specs=[pl.BlockSpec((BATCH_PER_STEP, CHUNK, D_GRP), lambda b, s: (b, s, 0)),
                   pl.BlockSpec((slab, wgu2d.shape[1]), step),
                   pl.BlockSpec((slab, wd2d.shape[1]), step)],
        out_shape=[jax.ShapeDtypeStruct((bsz, seq, D_GRP), BF16),
                   jax.ShapeDtypeStruct(wgu2d.shape, BF16),
                   jax.ShapeDtypeStruct(wd2d.shape, BF16)],
        scratch_shapes=[pltpu.VMEM((BATCH_PER_STEP, 1, RWKV_COLS), F32),
                        pltpu.VMEM((BATCH_PER_STEP, N_HEADS // HEADS_PER_SCAN, SCAN_W, SCAN_W), F32)],
        compiler_params=pltpu.CompilerParams(
            dimension_semantics=("parallel", "arbitrary"), vmem_limit_bytes=VMEM_LIMIT),
        name="rwkv",
    )(p_r, masks, mu, w0, w2p, a0, a2p, g2, k_k, k_a, r_k, gn_w, gn_b, wgu2d, wd2d)
    return y, wgu_bf.reshape(w_gate_up.shape), wd_bf.reshape(w_down.shape)


def _fox_kernel(q_ref, qb_ref, k_ref, kb_ref, vt_ref, og_ref, ong_ref, o_ref, m_ref, l_ref, acc_ref,
                *, seq):
    hp = pl.program_id(1)
    lane = _iota((1, LANES), 1)
    q = q_ref[...]
    qb = qb_ref[...]
    zero = jnp.zeros_like(q)
    qcat = [jnp.concatenate([jnp.where(lane // HEAD_DIM == hh, q, zero),
                             jnp.where(lane // 8 == hp * 2 + hh, qb, zero)], axis=1)
            for hh in range(2)]
    keys = min(FOX_SUB_KEYS, seq)
    n_sub = seq // keys
    half = keys // 2
    mask_a = _iota((half, half), 1) >= _iota((half, half), 0)
    mask_b = _iota((keys, half), 1) + half >= _iota((keys, half), 0)

    m_ref[...] = jnp.full(m_ref.shape, -jnp.inf, F32)
    l_ref[...] = jnp.zeros(l_ref.shape, F32)
    acc_ref[...] = jnp.zeros(acc_ref.shape, F32)

    def plan(s):
        lo = s * keys
        pieces = [(half, slice(lo, lo + half), mask_a), (keys, slice(lo + half, lo + keys), mask_b)]
        if lo + keys < seq:
            pieces.append((keys, slice(lo + keys, seq), None))
        return pieces

    def scores(s):
        lo = s * keys
        kcat = jnp.concatenate([k_ref[lo:lo + keys, :], kb_ref[lo:lo + keys, :]], axis=1)
        return [lax.dot_general(kcat[:nk], qc[qs, :], _NT, preferred_element_type=F32)
                for qc in qcat for nk, qs, _ in plan(s)]

    pending = scores(0)
    for s in range(n_sub):
        lo = s * keys
        nxt = scores(s + 1) if s + 1 < n_sub else None
        n_piece = len(plan(s))
        pieces = []
        for hh in range(2):
            for (nk, qs, mask), st in zip(plan(s), pending[hh * n_piece:(hh + 1) * n_piece]):
                pieces.append((hh, nk, qs, st if mask is None else jnp.where(mask, st, -jnp.inf)))
        m_old = [m_ref[hh, :, qs] for hh, _, qs, _ in pieces]
        m_new = [jnp.maximum(m, jnp.max(st, axis=0, keepdims=True))
                 for m, (_, _, _, st) in zip(m_old, pieces)]
        pts = [jnp.exp2(st - m) for (_, _, _, st), m in zip(pieces, m_new)]
        pvs = [jnp.dot(vt_ref[:, lo:lo + nk], pt.astype(BF16), preferred_element_type=F32)
               for (_, nk, _, _), pt in zip(pieces, pts)]
        for (hh, _, qs, _), mo, mn, pt, pv in zip(pieces, m_old, m_new, pts, pvs):
            alpha = jnp.exp2(mo - mn)
            m_ref[hh, :, qs] = mn
            l_ref[hh, :, qs] = alpha * l_ref[hh, :, qs] + jnp.sum(pt, axis=0, keepdims=True)
            acc_ref[hh, :, qs] = (alpha * acc_ref[hh, :, qs]
                                  + pv[hh * HEAD_DIM:(hh + 1) * HEAD_DIM, :])
        pending = nxt

    outs = []
    for hh in range(2):
        o = acc_ref[hh] / l_ref[hh]
        outs.append(o * lax.rsqrt(jnp.mean(o * o, axis=0, keepdims=True) + NORM_EPS))
    o = jnp.concatenate(outs, axis=0).T
    o_ref[...] = (o * ong_ref[...] * _sigmoid(og_ref[...].astype(F32))).astype(BF16)


def _fox(p_x, k_bias, q_bias, v_t, o_gain):
    bsz, seq, _ = p_x.shape
    npair = N_HEADS // 2
    return pl.pallas_call(
        functools.partial(_fox_kernel, seq=seq),
        grid=(bsz, npair),
        in_specs=[pl.BlockSpec((None, seq, LANES), lambda b, h: (b, 0, h)),
                  pl.BlockSpec((None, seq, LANES), lambda b, h: (b, 0, 0)),
                  pl.BlockSpec((None, seq, LANES), lambda b, h: (b, 0, npair + h)),
                  pl.BlockSpec((None, seq, LANES), lambda b, h: (b, 0, 0)),
                  pl.BlockSpec((None, LANES, seq), lambda b, h: (b, h, 0)),
                  pl.BlockSpec((None, seq, LANES), lambda b, h: (b, 0, 3 * npair + h)),
                  pl.BlockSpec((1, LANES), lambda b, h: (0, 0))],
        out_specs=pl.BlockSpec((None, seq, LANES), lambda b, h: (b, 0, h)),
        out_shape=jax.ShapeDtypeStruct((bsz, seq, D_GRP), BF16),
        scratch_shapes=[pltpu.VMEM((2, 1, seq), F32), pltpu.VMEM((2, 1, seq), F32),
                        pltpu.VMEM((2, HEAD_DIM, seq), F32)],
        compiler_params=pltpu.CompilerParams(
            dimension_semantics=("parallel", "parallel"), vmem_limit_bytes=VMEM_LIMIT),
        name="fox",
    )(p_x, q_bias, p_x, k_bias, v_t, p_x, o_gain)


def _outproj_kernel(x_ref, yr_ref, yf_ref, g1_ref, sh_ref, sc_ref, ng_ref, wor_ref, wof_ref,
                    wrt_ref, wrl_ref, brt_ref, x1_ref, h2_ref, idx_ref, gate_ref, rank_ref, cnt_ref,
                    carry_ref):
    @pl.when(pl.program_id(0) == 0)
    def _():
        carry_ref[...] = jnp.zeros_like(carry_ref)

    y = (jnp.dot(yr_ref[...], wor_ref[...], preferred_element_type=F32)
         + jnp.dot(yf_ref[...], wof_ref[...], preferred_element_type=F32))
    x1 = x_ref[...] + g1_ref[...] * y
    x1_ref[...] = x1
    tm = x1.shape[0]
    h = x1 * lax.rsqrt(jnp.mean(x1 * x1, axis=-1, keepdims=True) + NORM_EPS) * ng_ref[...]
    h2 = h * (1.0 + sc_ref[...]) + sh_ref[...]
    h2_ref[...] = _pack_rows(h2)

    h_hi = h2.astype(BF16)
    h_lo = (h2 - h_hi.astype(F32)).astype(BF16)
    logits = (lax.dot_general(wrt_ref[...], h_hi, _NT, preferred_element_type=F32)
              + lax.dot_general(wrt_ref[...], h_lo, _NT, preferred_element_type=F32)
              + lax.dot_general(wrl_ref[...], h_hi, _NT, preferred_element_type=F32))
    lg = logits[:N_EXPERTS, :] + brt_ref[...]
    expert = _iota((N_EXPERTS, tm), 0)
    picks = []
    hot_sum = jnp.zeros((N_EXPERTS, tm), F32)
    for _ in range(TOP_K):
        m = jnp.max(lg, axis=0, keepdims=True)
        sel = jnp.min(jnp.where(lg == m, expert, N_EXPERTS), axis=0, keepdims=True)
        hot = expert == sel
        picks.append((m, sel, hot))
        hot_sum = hot_sum + hot.astype(F32)
        lg = jnp.where(hot, -jnp.inf, lg)
    es = [jnp.exp(m - picks[0][0]) for m, _, _ in picks]
    den = es[0] + es[1] + es[2] + es[3]

    earlier = (_iota((tm, tm), 0) < _iota((tm, tm), 1)).astype(BF16)
    before = jnp.dot(hot_sum.astype(BF16), earlier, preferred_element_type=F32) + carry_ref[...]
    ranks = [jnp.sum(jnp.where(hot, before, 0.0), axis=0, keepdims=True).astype(jnp.int32)
             for _, _, hot in picks]
    pad_i = jnp.zeros((8 - TOP_K, tm), jnp.int32)
    idx_ref[...] = jnp.concatenate([sel for _, sel, _ in picks] + [pad_i], axis=0)
    gate_ref[...] = jnp.concatenate([e / den for e in es] + [pad_i.astype(F32)], axis=0)
    rank_ref[...] = jnp.concatenate(ranks + [pad_i], axis=0)
    carry_ref[...] = carry_ref[...] + jnp.sum(hot_sum, axis=1, keepdims=True)
    cnt_ref[...] = jnp.broadcast_to(carry_ref[...], cnt_ref.shape)


def _outproj(x2d, y_r, y_f, gate1, shift2, scale2, norm_g, wo_r, wo_f, w_rt, b_rt, tm, seq,
             row0, t):
    w_rt_hi = w_rt.astype(BF16)
    w_rt_lo = (w_rt - w_rt_hi.astype(F32)).astype(BF16)
    per_b = seq // tm
    blk0 = row0 // tm
    const = lambda i: (0, 0)
    rows = lambda i: (i, 0)
    rows_in = lambda i: (i + blk0, 0)
    mod = pl.BlockSpec((None, 1, D_MODEL), lambda i: ((i + blk0) // per_b, 0, 0))
    return pl.pallas_call(
        _outproj_kernel,
        grid=(t // tm,),
        in_specs=[pl.BlockSpec((tm, D_MODEL), rows_in),
                  pl.BlockSpec((tm, D_GRP), rows_in),
                  pl.BlockSpec((tm, D_GRP), rows_in),
                  mod, mod, mod,
                  pl.BlockSpec((1, D_MODEL), const),
                  pl.BlockSpec((D_GRP, D_MODEL), const),
                  pl.BlockSpec((D_GRP, D_MODEL), const),
                  pl.BlockSpec((LANES, D_MODEL), const),
                  pl.BlockSpec((LANES, D_MODEL), const),
                  pl.BlockSpec((N_EXPERTS, 1), const)],
        out_specs=[pl.BlockSpec((tm, D_MODEL), rows),
                   pl.BlockSpec((tm, D_PACK), rows),
                   pl.BlockSpec((8, tm), lambda i: (0, i)),
                   pl.BlockSpec((8, tm), lambda i: (0, i)),
                   pl.BlockSpec((8, tm), lambda i: (0, i)),
                   pl.BlockSpec((N_EXPERTS, LANES), const)],
        out_shape=[jax.ShapeDtypeStruct((t, D_MODEL), F32),
                   jax.ShapeDtypeStruct((t, D_PACK), jnp.uint32),
                   jax.ShapeDtypeStruct((8, t), jnp.int32),
                   jax.ShapeDtypeStruct((8, t), F32),
                   jax.ShapeDtypeStruct((8, t), jnp.int32),
                   jax.ShapeDtypeStruct((N_EXPERTS, LANES), F32)],
        scratch_shapes=[pltpu.VMEM((N_EXPERTS, 1), F32)],
        compiler_params=pltpu.CompilerParams(
            dimension_semantics=("arbitrary",), vmem_limit_bytes=VMEM_LIMIT),
        name="outproj",
    )(x2d, y_r, y_f, gate1, shift2, scale2, norm_g, wo_r, wo_f, w_rt_hi, w_rt_lo, b_rt)


SC_CORES = 2
SC_SUBCORES = 16
SC_ROWS = 64


def _sc_gather_rows(idx, src):
    n_workers = SC_CORES * SC_SUBCORES
    m = idx.shape[0]
    d = src.shape[1]
    assert m % (n_workers * SC_ROWS) == 0
    n_chunks = m // (n_workers * SC_ROWS)
    mesh = plsc.VectorSubcoreMesh(core_axis_name="c", subcore_axis_name="s")

    @functools.partial(
        pl.kernel, mesh=mesh,
        out_type=jax.ShapeDtypeStruct((m, d), src.dtype),
        scratch_types=[pltpu.VMEM((n_chunks, SC_ROWS), jnp.int32),
                       pltpu.VMEM((SC_ROWS, d), src.dtype),
                       pltpu.SemaphoreType.DMA],
        name="sc_gather")
    def gather(src_hbm, idx_hbm, out_hbm, idx_v, rows_v, sem):
        wid = lax.axis_index("s") * SC_CORES + lax.axis_index("c")
        pltpu.sync_copy(idx_hbm.at[wid], idx_v)

        @pl.loop(0, n_chunks)
        def _(j):
            pltpu.async_copy(src_hbm.at[idx_v.at[j]], rows_v, sem).wait()
            pltpu.sync_copy(rows_v, out_hbm.at[pl.ds((wid * n_chunks + j) * SC_ROWS, SC_ROWS)])

    return gather(src, idx.reshape(n_workers, n_chunks, SC_ROWS))


def _sc_scatter_rows(src, dest, n_out):
    n_workers = SC_CORES * SC_SUBCORES
    t, d = src.shape
    n_slot = dest.shape[0]
    assert t % (n_workers * SC_ROWS) == 0
    n_chunks = t // (n_workers * SC_ROWS)
    mesh = plsc.VectorSubcoreMesh(core_axis_name="c", subcore_axis_name="s")
    idx = dest.reshape(n_slot, n_workers, n_chunks, SC_ROWS).transpose(1, 2, 0, 3)
    idx = idx.reshape(n_workers, n_chunks * n_slot, SC_ROWS)

    @functools.partial(
        pl.kernel, mesh=mesh,
        out_type=jax.ShapeDtypeStruct((n_out, d), src.dtype),
        scratch_types=[pltpu.VMEM((n_chunks * n_slot, SC_ROWS), jnp.int32),
                       pltpu.VMEM((SC_ROWS, d), src.dtype)],
        name="sc_scatter")
    def scatter(src_hbm, idx_hbm, out_hbm, idx_v, rows_v):
        wid = lax.axis_index("s") * SC_CORES + lax.axis_index("c")
        pltpu.sync_copy(idx_hbm.at[wid], idx_v)

        @pl.loop(0, n_chunks)
        def _(j):
            pltpu.sync_copy(src_hbm.at[pl.ds((wid * n_chunks + j) * SC_ROWS, SC_ROWS)], rows_v)
            for k in range(n_slot):
                pltpu.sync_copy(rows_v, out_hbm.at[idx_v.at[j * n_slot + k]])

    return scatter(src, idx)


def _expert_kernel(be_ref, nv_ref, x_ref, wgu_ref, bgu_ref, wd_ref, bd_ref, o_ref):
    del be_ref
    valid = _iota((EXPERT_BLOCK, 1), 0) < nv_ref[pl.program_id(0)]
    lo, hi = _unpack_rows(jnp.where(valid, x_ref[...], jnp.uint32(0)))
    x = jnp.concatenate([lo.astype(BF16), hi.astype(BF16)], axis=1)
    half = EXPERT_BLOCK // 2
    gus = [jnp.dot(x[r * half:(r + 1) * half], wgu_ref[...], preferred_element_type=F32)
           + bgu_ref[...] for r in range(2)]
    for r, gu in enumerate(gus):
        gate = jnp.minimum(gu[:, :D_MODEL], SWIGLU_LIMIT)
        up = jnp.clip(gu[:, D_MODEL:], -SWIGLU_LIMIT, SWIGLU_LIMIT)
        act = gate * _sigmoid(SWIGLU_ALPHA * gate) * (up + 1.0)
        o_ref[r * half:(r + 1) * half, :] = _pack_rows(
            jnp.dot(act.astype(BF16), wd_ref[...], preferred_element_type=F32) + bd_ref[...])


def _experts(block_e, n_valid, xs, w_gu, b_gu, w_d, b_d):
    n_blocks = block_e.shape[0]
    grid_spec = pltpu.PrefetchScalarGridSpec(
        num_scalar_prefetch=2,
        grid=(n_blocks,),
        in_specs=[pl.BlockSpec((EXPERT_BLOCK, D_PACK), lambda j, be, nv: (j, 0)),
                  pl.BlockSpec((None, D_MODEL, 2 * D_MODEL), lambda j, be, nv: (be[j], 0, 0)),
                  pl.BlockSpec((None, 1, 2 * D_MODEL), lambda j, be, nv: (be[j], 0, 0)),
                  pl.BlockSpec((None, D_MODEL, D_MODEL), lambda j, be, nv: (be[j], 0, 0)),
                  pl.BlockSpec((None, 1, D_MODEL), lambda j, be, nv: (be[j], 0, 0))],
        out_specs=pl.BlockSpec((EXPERT_BLOCK, D_PACK), lambda j, be, nv: (j, 0)),
    )
    return pl.pallas_call(
        _expert_kernel,
        grid_spec=grid_spec,
        out_shape=jax.ShapeDtypeStruct(xs.shape, jnp.uint32),
        compiler_params=pltpu.CompilerParams(
            dimension_semantics=("arbitrary",), vmem_limit_bytes=VMEM_LIMIT),
        name="experts",
    )(block_e, n_valid, xs, w_gu, b_gu, w_d, b_d)


COMBINE_TOKENS = 1024
MOE_SPLITS = 2


def _combine_kernel(yg_ref, x1_ref, gate_ref, g2_ref, fg_ref, o_ref):
    gates = gate_ref[...].T
    acc_lo = acc_hi = None
    for kk in range(TOP_K):
        lo, hi = _unpack_rows(yg_ref[kk * COMBINE_TOKENS:(kk + 1) * COMBINE_TOKENS, :])
        g = gates[:, kk:kk + 1]
        acc_lo = g * lo if acc_lo is None else acc_lo + g * lo
        acc_hi = g * hi if acc_hi is None else acc_hi + g * hi
    x2 = x1_ref[...] + g2_ref[...] * jnp.concatenate([acc_lo, acc_hi], axis=1)
    o_ref[...] = x2 * lax.rsqrt(jnp.mean(x2 * x2, axis=-1, keepdims=True) + NORM_EPS) * fg_ref[...]


def _combine_kernel_into(prev_ref, *refs):
    del prev_ref
    _combine_kernel(*refs)


def _combine(yg, x1, gates, gate2, final_g, seq, row0, t_total, prev):
    t = x1.shape[0]
    tm = COMBINE_TOKENS
    per_b = seq // tm
    blk0 = row0 // tm
    rows = lambda i: (i, 0)
    in_specs = [pl.BlockSpec((TOP_K * tm, D_PACK), rows),
                pl.BlockSpec((tm, D_MODEL), rows),
                pl.BlockSpec((8, tm), lambda i: (0, i)),
                pl.BlockSpec((None, 1, D_MODEL), lambda i: ((i + blk0) // per_b, 0, 0)),
                pl.BlockSpec((1, D_MODEL), lambda i: (0, 0))]
    args = (yg, x1, gates, gate2, final_g)
    if prev is not None:
        in_specs = [pl.BlockSpec(memory_space=pl.ANY)] + in_specs
        args = (prev,) + args
    return pl.pallas_call(
        _combine_kernel if prev is None else _combine_kernel_into,
        grid=(t // tm,),
        in_specs=in_specs,
        out_specs=pl.BlockSpec((tm, D_MODEL), lambda i: (i + blk0, 0)),
        out_shape=jax.ShapeDtypeStruct((t_total, D_MODEL), F32),
        input_output_aliases={} if prev is None else {0: 0},
        compiler_params=pltpu.CompilerParams(
            dimension_semantics=("parallel",), vmem_limit_bytes=VMEM_LIMIT),
        name="combine",
    )(*args)


def _moe(h2, idx, gates, rank, counts, x1, gate2, final_g, w_gu, b_gu, w_d, b_d, seq,
         row0, t_total, prev):
    t = h2.shape[0]
    n_slots = t * TOP_K
    n_blocks = -(-n_slots // EXPERT_BLOCK) + N_EXPERTS
    cap = n_blocks * EXPERT_BLOCK
    padded = (counts + EXPERT_BLOCK - 1) // EXPERT_BLOCK * EXPERT_BLOCK
    pad_ends = jnp.cumsum(padded)
    pad_starts = pad_ends - padded
    experts = jnp.arange(N_EXPERTS, dtype=jnp.int32)
    dest = jnp.sum(jnp.where(idx[..., None] == experts, pad_starts, 0), axis=-1) + rank
    block_starts = jnp.arange(n_blocks, dtype=jnp.int32) * EXPERT_BLOCK
    block_e = jnp.minimum(jnp.sum(block_starts[:, None] >= pad_ends[None, :], axis=1),
                          N_EXPERTS - 1).astype(jnp.int32)
    n_valid = jnp.clip(counts[block_e] - (block_starts - pad_starts[block_e]), 0, EXPERT_BLOCK)

    xs = _sc_scatter_rows(h2, dest, cap)
    yb = _experts(block_e, n_valid.astype(jnp.int32), xs, w_gu, b_gu, w_d, b_d)
    dest_blocks = dest.reshape(TOP_K, -1, COMBINE_TOKENS).transpose(1, 0, 2).reshape(-1)
    yg = _sc_gather_rows(dest_blocks, yb)
    return _combine(yg, x1, gates, gate2, final_g, seq, row0, t_total, prev)


def _layer(x, c_mod, norm1_g, w_in, mu_shift, w0, w2, a0, a2, g2, k_k, k_a, r_k, gn_w, gn_b, b_f,
           q_norm_g, k_norm_g, o_norm_g, w_out, norm2_g, w_router, b_router, w_gate_up,
           b_gate_up, w_down, b_down, final_g, tm_in, tm_out):
    bsz, seq, _ = x.shape
    shift1, scale1, gate1, shift2, scale2, gate2 = (
        m.reshape(bsz, 1, D_MODEL) for m in jnp.split(c_mod, 6, axis=-1))
    row = lambda v: v.reshape(1, -1)

    w_r = w_in[:, :RWKV_COLS].astype(BF16)
    w_x = w_in[:, RWKV_COLS:RWKV_COLS + FOX_MAIN].astype(BF16)
    w_f = w_in[:, RWKV_COLS + FOX_MAIN:].T
    b_fp = jnp.pad(b_f, (0, LANES - N_HEADS)).reshape(1, LANES)
    qk_gain = jnp.concatenate([jnp.tile(q_norm_g, N_HEADS) * (HEAD_DIM ** -0.5 * LOG2_E),
                               jnp.tile(k_norm_g, N_HEADS)]).reshape(1, -1)
    p_r, p_x, k_bias, q_bias, v_t = _inproj(x, shift1, scale1, row(norm1_g), w_r, w_x, w_f, b_fp,
                                            qk_gain, tm_in)

    zeros = jnp.zeros((LANES - 64, D_GRP), F32)
    w2p = jnp.concatenate([w2, zeros], axis=0).astype(BF16)
    a2p = jnp.concatenate([zeros, a2], axis=0).astype(BF16)
    y_r, w_gu, w_d = _rwkv(p_r, row(mu_shift), row(w0), w2p, row(a0), a2p, g2.astype(BF16),
                           row(k_k), row(k_a), row(r_k), row(gn_w), row(gn_b), w_gate_up, w_down)

    y_f = _fox(p_x, k_bias, q_bias, v_t, jnp.tile(o_norm_g, 2).reshape(1, LANES))

    t = bsz * seq
    w_rt = jnp.pad(w_router.T, ((0, LANES - N_EXPERTS), (0, 0)))
    b_rt = b_router.reshape(N_EXPERTS, 1)
    wo = w_out.astype(BF16)
    b_gu, b_d = b_gate_up.reshape(N_EXPERTS, 1, -1), b_down.reshape(N_EXPERTS, 1, -1)
    t_part = t // MOE_SPLITS
    out = None
    for part in range(MOE_SPLITS):
        row0 = part * t_part
        x1, h2, idx, gates, rank, cnt = _outproj(
            x.reshape(t, D_MODEL), y_r.reshape(t, D_GRP), y_f.reshape(t, D_GRP), gate1, shift2,
            scale2, row(norm2_g), wo[:D_GRP], wo[D_GRP:], w_rt, b_rt, tm_out, seq, row0, t_part)
        counts = cnt[:, 0].astype(jnp.int32)
        out = _moe(h2, idx[:TOP_K], gates, rank[:TOP_K], counts, x1, gate2, row(final_g),
                   w_gu, b_gu, w_d, b_d, seq, row0, t, out)
    return out.reshape(bsz, seq, D_MODEL)


def kernel(x, c, w_ada, b_ada, norm1_g, w_in, mu_shift, w0, w2, a0, a2, g2, k_k, k_a, r_k, gn_w,
           gn_b, b_f, q_norm_g, k_norm_g, o_norm_g, w_out, norm2_g, w_router, b_router, w_gate_up,
           b_gate_up, w_down, b_down, final_g):
    assert w_ada.shape[0] == 1, "single-layer block"
    c_mod = _adaln(c, w_ada[0], b_ada[0])
    return _layer(x, c_mod, norm1_g[0], w_in[0], mu_shift[0], w0[0], w2[0], a0[0], a2[0], g2[0],
                  k_k[0], k_a[0], r_k[0], gn_w[0], gn_b[0], b_f[0], q_norm_g[0], k_norm_g[0],
                  o_norm_g[0], w_out[0], norm2_g[0], w_router[0], b_router[0], w_gate_up[0],
                  b_gate_up[0], w_down[0], b_down[0], final_g,
                  tm_in=min(512, x.shape[1]), tm_out=min(1024, x.shape[1]))
```

```python
import functools

import jax
import jax.numpy as jnp
from jax import lax
from jax.experimental import pallas as pl
from jax.experimental.pallas import tpu as pltpu
from jax.experimental.pallas import tpu_sc as plsc

F32 = jnp.float32
BF16 = jnp.bfloat16
HIGHEST = lax.Precision.HIGHEST

D_MODEL = 1024
HEAD_DIM = 64
N_HEADS = 8
D_GRP = N_HEADS * HEAD_DIM
RWKV_COLS = 1792
LORA_OFF = 3 * D_GRP
GATE_OFF = LORA_OFF + 128
FOX_MAIN = 4 * D_GRP
N_EXPERTS = 32
TOP_K = 4
EXPERT_BLOCK = 512
SWIGLU_ALPHA = 1.702
SWIGLU_LIMIT = 7.0
NORM_EPS = 1e-6
GN_EPS = 64e-5
LOG2_E = 1.4426950408889634
LANES = 128
CHUNK = 64
FOX_SUB_KEYS = 512
HEADS_PER_SCAN = 4
SCAN_W = HEADS_PER_SCAN * HEAD_DIM
SEG_TERMS = 1
CUM_TERMS = 2
VMEM_LIMIT = 56 * 1024 * 1024


def _dot(a, b):
    return jnp.dot(a.astype(BF16), b.astype(BF16), preferred_element_type=F32)


def _fdot(a, b):
    return jnp.dot(a, b, precision=HIGHEST, preferred_element_type=F32)


def _split_dot(x, m, terms):
    acc = None
    rem = x
    for _ in range(terms):
        part = rem.astype(BF16)
        rem = rem - part.astype(F32)
        d = jnp.dot(part, m, preferred_element_type=F32)
        acc = d if acc is None else acc + d
    return acc


def _tri_dot(m, x, terms):
    acc = None
    rem = x
    for _ in range(terms):
        part = rem.astype(BF16)
        rem = rem - part.astype(F32)
        d = jnp.dot(m, part, preferred_element_type=F32)
        acc = d if acc is None else acc + d
    return acc


def _iota(shape, dim):
    return lax.broadcasted_iota(jnp.int32, shape, dim)


def _seg_reduce_mat(n):
    return (_iota((n, LANES), 0) // HEAD_DIM == _iota((n, LANES), 1)).astype(BF16)


def _seg_expand_mat(n):
    return (_iota((LANES, n), 1) // HEAD_DIM == _iota((LANES, n), 0)).astype(BF16)


D_PACK = D_MODEL // 2


def _pack_rows(x):
    lo = lax.bitcast_convert_type(x[:, :D_PACK].astype(BF16).astype(F32), jnp.uint32)
    hi = lax.bitcast_convert_type(x[:, D_PACK:].astype(BF16).astype(F32), jnp.uint32)
    return hi | (lo >> 16)


def _unpack_rows(p):
    lo = lax.bitcast_convert_type(p << 16, F32)
    hi = lax.bitcast_convert_type(p & jnp.uint32(0xFFFF0000), F32)
    return lo, hi


def _log_sigmoid(z):
    return jnp.minimum(z, 0.0) - jnp.log(1.0 + jnp.exp(-jnp.abs(z)))


def _sigmoid(z):
    return 1.0 / (1.0 + jnp.exp(-z))


def _adaln_kernel(c_ref, w_ref, b_ref, o_ref):
    c = c_ref[...]
    o_ref[...] = _fdot(c * _sigmoid(c), w_ref[...]) + b_ref[...]


def _adaln(c, w_ada, b_ada):
    bsz = c.shape[0]
    n_mod = w_ada.shape[1] // D_MODEL
    return pl.pallas_call(
        _adaln_kernel,
        grid=(n_mod,),
        in_specs=[pl.BlockSpec((bsz, D_MODEL), lambda j: (0, 0)),
                  pl.BlockSpec((D_MODEL, D_MODEL), lambda j: (0, j)),
                  pl.BlockSpec((1, D_MODEL), lambda j: (0, j))],
        out_specs=pl.BlockSpec((bsz, D_MODEL), lambda j: (0, j)),
        out_shape=jax.ShapeDtypeStruct((bsz, n_mod * D_MODEL), F32),
        name="adaln",
    )(c, w_ada, b_ada.reshape(1, -1))


def _inproj_kernel(x_ref, sh_ref, sc_ref, g_ref, wr_ref, wx_ref, wft_ref, bf_ref, qkg_ref,
                   pr_ref, px_ref, kb_ref, qb_ref, vt_ref, carry_ref):
    @pl.when(pl.program_id(1) == 0)
    def _():
        carry_ref[...] = jnp.zeros_like(carry_ref)

    x = x_ref[...]
    tm = x.shape[0]
    h = x * lax.rsqrt(jnp.mean(x * x, axis=-1, keepdims=True) + NORM_EPS) * g_ref[...]
    h = h * (1.0 + sc_ref[...]) + sh_ref[...]
    hb = h.astype(BF16)

    pr_ref[...] = jnp.dot(hb, wr_ref[...], preferred_element_type=F32).astype(BF16)

    px = jnp.dot(hb, wx_ref[...], preferred_element_type=F32)
    qk = px[:, :2 * D_GRP]
    ss = _split_dot(qk * qk, _seg_reduce_mat(2 * D_GRP), SEG_TERMS)
    inv = lax.rsqrt(ss * (1.0 / HEAD_DIM) + NORM_EPS)
    qk = qk * _split_dot(inv, _seg_expand_mat(2 * D_GRP), SEG_TERMS) * qkg_ref[...]
    px_ref[:, :2 * D_GRP] = qk.astype(BF16)
    px_ref[:, 2 * D_GRP:] = px[:, 3 * D_GRP:].astype(BF16)
    vt_ref[...] = px[:, 2 * D_GRP:3 * D_GRP].T.astype(BF16)

    lane = _iota((1, LANES), 1)
    z = jnp.zeros((tm, LANES), F32)
    for hd in range(N_HEADS):
        zh = jnp.sum(h * wft_ref[hd:hd + 1, :], axis=-1, keepdims=True)
        z = jnp.where(lane == hd, zh, z)
    cum = _log_sigmoid(z + bf_ref[...])
    row_id = _iota((tm, 1), 0)
    shift = 1
    while shift < tm:
        cum = cum + jnp.where(row_id >= shift, pltpu.roll(cum, shift, axis=0), 0.0)
        shift *= 2
    cum = cum + carry_ref[...]
    carry_ref[...] = cum[tm - 1:tm, :]

    parts = []
    rem = cum * LOG2_E
    for _ in range(3):
        part = rem.astype(BF16)
        rem = rem - part.astype(F32)
        parts.append(part)
    src, dst = _iota((LANES, LANES), 0), _iota((LANES, LANES), 1)

    def spread(offset):
        return sum(jnp.dot(part, ((dst == 8 * src + offset + t) & (src < N_HEADS)).astype(BF16),
                           preferred_element_type=F32) for t, part in enumerate(parts))

    slot = _iota((1, LANES), 1) % 8
    kb_ref[...] = (jnp.where((slot >= 3) & (slot < 6), 1.0, 0.0) - spread(0)).astype(BF16)
    qb_ref[...] = (jnp.where(slot < 3, 1.0, 0.0) + spread(3)).astype(BF16)


def _inproj(x, shift, scale, g, w_r, w_x, w_f_t, b_f, qk_gain, tm):
    bsz, seq, _ = x.shape
    const = lambda b, s: (0, 0)
    return pl.pallas_call(
        _inproj_kernel,
        grid=(bsz, seq // tm),
        in_specs=[pl.BlockSpec((None, tm, D_MODEL), lambda b, s: (b, s, 0)),
                  pl.BlockSpec((None, 1, D_MODEL), lambda b, s: (b, 0, 0)),
                  pl.BlockSpec((None, 1, D_MODEL), lambda b, s: (b, 0, 0)),
                  pl.BlockSpec((1, D_MODEL), const),
                  pl.BlockSpec((D_MODEL, RWKV_COLS), const),
                  pl.BlockSpec((D_MODEL, FOX_MAIN), const),
                  pl.BlockSpec((N_HEADS, D_MODEL), const),
                  pl.BlockSpec((1, LANES), const),
                  pl.BlockSpec((1, 2 * D_GRP), const)],
        out_specs=[pl.BlockSpec((None, tm, RWKV_COLS), lambda b, s: (b, s, 0)),
                   pl.BlockSpec((None, tm, 3 * D_GRP), lambda b, s: (b, s, 0)),
                   pl.BlockSpec((None, tm, LANES), lambda b, s: (b, s, 0)),
                   pl.BlockSpec((None, tm, LANES), lambda b, s: (b, s, 0)),
                   pl.BlockSpec((None, D_GRP, tm), lambda b, s: (b, 0, s))],
        out_shape=[jax.ShapeDtypeStruct((bsz, seq, RWKV_COLS), BF16),
                   jax.ShapeDtypeStruct((bsz, seq, 3 * D_GRP), BF16),
                   jax.ShapeDtypeStruct((bsz, seq, LANES), BF16),
                   jax.ShapeDtypeStruct((bsz, seq, LANES), BF16),
                   jax.ShapeDtypeStruct((bsz, D_GRP, seq), BF16)],
        scratch_shapes=[pltpu.VMEM((1, LANES), F32)],
        compiler_params=pltpu.CompilerParams(
            dimension_semantics=("parallel", "arbitrary"), vmem_limit_bytes=VMEM_LIMIT),
        name="inproj",
    )(x, shift, scale, g, w_r, w_x, w_f_t, b_f, qk_gain)


_NN = (((1,), (0,)), ((), ()))
_NT = (((1,), (1,)), ((), ()))
_TN = (((0,), (0,)), ((), ()))
SCAN_N = HEADS_PER_SCAN * CHUNK
BATCH_PER_STEP = 8
INV_LEVELS = 5
M_HEAD, M_STRICT, M_INCL, M_EYE, M_BASE, M_OFF = 0, 1, 2, 3, 4, 5


def _bdot(a, b, dims):
    return lax.dot_general(a, b, dims, preferred_element_type=F32)


def _scan_masks():
    rr, cc = _iota((SCAN_N, SCAN_W), 0), _iota((SCAN_N, SCAN_W), 1)
    ri, ci = _iota((SCAN_N, SCAN_N), 0), _iota((SCAN_N, SCAN_N), 1)
    same = ri // CHUNK == ci // CHUNK
    masks = [rr // CHUNK == cc // HEAD_DIM, same & (ri > ci), same & (ri >= ci), ri == ci,
             (ri // 2 == ci // 2) & (ri > ci)]
    blk = 2
    while blk < CHUNK:
        masks.append((ri // (2 * blk) == ci // (2 * blk)) & (ri // blk != ci // blk) & (ri > ci))
        blk *= 2
    return jnp.stack(masks).astype(BF16)


def _rwkv_kernel(p_ref, masks_ref, mu_ref, w0_ref, w2_ref, a0_ref, a2_ref, g2_ref, kk_ref, ka_ref,
                 rk_ref, gnw_ref, gnb_ref, wgu_ref, wd_ref, o_ref, wgu_bf_ref, wd_bf_ref,
                 last_ref, state_ref):
    wgu_bf_ref[...] = wgu_ref[...].astype(BF16)
    wd_bf_ref[...] = wd_ref[...].astype(BF16)

    @pl.when(pl.program_id(1) == 0)
    def _():
        last_ref[...] = jnp.zeros_like(last_ref)
        state_ref[...] = jnp.zeros_like(state_ref)

    mu, w0, w2, a0, a2, g2, k_k, k_a, r_k, gn_w, gn_b = (
        ref[...] for ref in (mu_ref, w0_ref, w2_ref, a0_ref, a2_ref, g2_ref, kk_ref, ka_ref,
                             rk_ref, gnw_ref, gnb_ref))
    rows = BATCH_PER_STEP * CHUNK
    p = p_ref[...].astype(F32).reshape(rows, RWKV_COLS)
    row_id = _iota((rows, 1), 0)
    prev = pltpu.roll(p, 1, axis=0)
    for bb in range(BATCH_PER_STEP):
        prev = jnp.where(row_id == bb * CHUNK, last_ref[bb], prev)
        last_ref[bb] = p[(bb + 1) * CHUNK - 1:(bb + 1) * CHUNK, :]
    pf = p + mu * (prev - p)
    r = pf[:, 0:D_GRP]
    k = pf[:, D_GRP:2 * D_GRP]
    v = pf[:, 2 * D_GRP:3 * D_GRP]
    lora = pf[:, LORA_OFF:GATE_OFF]
    gd = pf[:, GATE_OFF:RWKV_COLS]

    wlog = w0 + _dot(jnp.tanh(lora), w2)
    neg = -wlog
    softplus = jnp.maximum(neg, 0.0) + jnp.log(1.0 + jnp.exp(-jnp.abs(neg)))
    ld = -jnp.exp(-softplus - 0.5)
    a = _sigmoid(a0 + _dot(lora, a2))
    g = _dot(_sigmoid(gd), g2)

    red, exp_m = _seg_reduce_mat(D_GRP), _seg_expand_mat(D_GRP)
    kk = k * k_k
    n2 = _split_dot(kk * kk, red, SEG_TERMS)
    kk = kk * _split_dot(1.0 / jnp.maximum(jnp.sqrt(n2), 1e-12), exp_m, SEG_TERMS)
    k2 = k * (1.0 + (a - 1.0) * k_a)

    tr, tc = _iota((rows, rows), 0), _iota((rows, rows), 1)
    tri = ((tr >= tc) & (tr // CHUNK == tc // CHUNK)).astype(BF16)
    cl = _tri_dot(tri, ld, CUM_TERMS)
    cl_end = jnp.concatenate(
        [jnp.broadcast_to(cl[(bb + 1) * CHUNK - 1:(bb + 1) * CHUNK, :], (CHUNK, D_GRP))
         for bb in range(BATCH_PER_STEP)], axis=0)
    e_in = jnp.exp(cl)
    e_out = jnp.exp(-cl)
    e_rem = jnp.exp(cl_end - cl)
    p_end = jnp.exp(cl_end)
    kka = kk * a
    ops = [(-kk * jnp.exp(cl - ld)).astype(BF16), (kka * e_out).astype(BF16),
           (k2 * e_out).astype(BF16), (r * e_in).astype(BF16), v.astype(BF16),
           (kka * e_rem).astype(BF16), (k2 * e_rem).astype(BF16)]

    chains = [(bb, grp) for bb in range(BATCH_PER_STEP)
              for grp in range(N_HEADS // HEADS_PER_SCAN)]
    head_mask = masks_ref[M_HEAD]
    strict, incl = masks_ref[M_STRICT], masks_ref[M_INCL]

    def stacked(op, bb, grp):
        part = op[bb * CHUNK:(bb + 1) * CHUNK, grp * SCAN_W:(grp + 1) * SCAN_W]
        return jnp.concatenate([part] * HEADS_PER_SCAN, axis=0) * head_mask

    xs = [[stacked(op, bb, grp) for op in ops] for bb, grp in chains]
    st = [state_ref[bb, grp] for bb, grp in chains]
    sb = [s.astype(BF16) for s in st]
    nab = [_bdot(x[0], x[1], _NT).astype(BF16) for x in xs]
    aak = [_bdot(x[0], x[2], _NT).astype(BF16) * strict for x in xs]
    arb = [_bdot(x[3], x[1], _NT).astype(BF16) * incl for x in xs]
    ark = [_bdot(x[3], x[2], _NT).astype(BF16) * incl for x in xs]
    t_inv = [masks_ref[M_EYE] + n * masks_ref[M_BASE] for n in nab]
    for lvl in range(INV_LEVELS):
        half = [_bdot(t, n * masks_ref[M_OFF + lvl], _NN).astype(BF16) for t, n in zip(t_inv, nab)]
        t_inv = [t + _bdot(h, t, _NN).astype(BF16) for t, h in zip(t_inv, half)]
    rhs = [(_bdot(x[0], s, _NT) + _bdot(k, x[4], _NN)).astype(BF16)
           for x, s, k in zip(xs, sb, aak)]
    sa = [_bdot(t, h, _NN).astype(BF16) for t, h in zip(t_inv, rhs)]
    ys = [_bdot(x[3], s, _NT) + _bdot(b, u, _NN) + _bdot(k, x[4], _NN)
          for x, s, b, u, k in zip(xs, sb, arb, sa, ark)]
    for (bb, grp), x, s, u in zip(chains, xs, st, sa):
        decay = p_end[bb * CHUNK:bb * CHUNK + 1, grp * SCAN_W:(grp + 1) * SCAN_W]
        state_ref[bb, grp] = s * decay + _bdot(u, x[5], _TN) + _bdot(x[4], x[6], _TN)
    ys = [y[0:CHUNK] + y[CHUNK:2 * CHUNK] + y[2 * CHUNK:3 * CHUNK] + y[3 * CHUNK:4 * CHUNK]
          for y in ys]
    n_grp = N_HEADS // HEADS_PER_SCAN
    y = jnp.concatenate([jnp.concatenate(ys[bb * n_grp:(bb + 1) * n_grp], axis=1)
                         for bb in range(BATCH_PER_STEP)], axis=0)

    mean = _split_dot(_split_dot(y, red, SEG_TERMS) * (1.0 / HEAD_DIM), exp_m, SEG_TERMS)
    d = y - mean
    var = _split_dot(d * d, red, SEG_TERMS) * (1.0 / HEAD_DIM)
    yn = d * _split_dot(lax.rsqrt(var + GN_EPS), exp_m, SEG_TERMS) * gn_w + gn_b
    bonus = _split_dot(_split_dot(r * k2 * r_k, red, SEG_TERMS), exp_m, SEG_TERMS) * v
    o_ref[...] = ((yn + bonus) * g).astype(BF16).reshape(BATCH_PER_STEP, CHUNK, D_GRP)


def _rwkv(p_r, mu, w0, w2p, a0, a2p, g2, k_k, k_a, r_k, gn_w, gn_b, w_gate_up, w_down):
    bsz, seq, _ = p_r.shape
    assert bsz % BATCH_PER_STEP == 0
    n_chunk = seq // CHUNK
    n_step = (bsz // BATCH_PER_STEP) * n_chunk
    wgu2d = w_gate_up.reshape(-1, w_gate_up.shape[-1])
    wd2d = w_down.reshape(-1, w_down.shape[-1])
    assert wgu2d.shape[0] % (8 * n_step) == 0 and wd2d.shape[0] == wgu2d.shape[0]
    slab = wgu2d.shape[0] // n_step
    masks = _scan_masks()
    const = lambda b, s: (0, 0)
    step = lambda b, s: (b * n_chunk + s, 0)
    vec = pl.BlockSpec((1, D_GRP), const)
    y, wgu_bf, wd_bf = pl.pallas_call(
        _rwkv_kernel,
        grid=(bsz // BATCH_PER_STEP, n_chunk),
        in_specs=[pl.BlockSpec((BATCH_PER_STEP, CHUNK, RWKV_COLS), lambda b, s: (b, s, 0)),
                  pl.BlockSpec(masks.shape, lambda b, s: (0, 0, 0)),
                  pl.BlockSpec((1, RWKV_COLS), const),
                  vec, pl.BlockSpec((LANES, D_GRP), const),
                  vec, pl.BlockSpec((LANES, D_GRP), const),
                  pl.BlockSpec((LANES, D_GRP), const),
                  vec, vec, vec, vec, vec,
                  pl.BlockSpec((slab, wgu2d.shape[1]), step),
                  pl.BlockSpec((slab, wd2d.shape[1]), step)],
        out_specs=[pl.BlockSpec((BATCH_PER_STEP, CHUNK, D_GRP), lambda b, s: (b, s, 0)),
                   pl.BlockSpec((slab, wgu2d.shape[1]), step),
                   pl.BlockSpec((slab, wd2d.shape[1]), step)],
        out_shape=[jax.ShapeDtypeStruct((bsz, seq, D_GRP), BF16),
                   jax.ShapeDtypeStruct(wgu2d.shape, BF16),
                   jax.ShapeDtypeStruct(wd2d.shape, BF16)],
        scratch_shapes=[pltpu.VMEM((BATCH_PER_STEP, 1, RWKV_COLS), F32),
                        pltpu.VMEM((BATCH_PER_STEP, N_HEADS // HEADS_PER_SCAN, SCAN_W, SCAN_W), F32)],
        compiler_params=pltpu.CompilerParams(
            dimension_semantics=("parallel", "arbitrary"), vmem_limit_bytes=VMEM_LIMIT),
        name="rwkv",
    )(p_r, masks, mu, w0, w2p, a0, a2p, g2, k_k, k_a, r_k, gn_w, gn_b, wgu2d, wd2d)
    return y, wgu_bf.reshape(w_gate_up.shape), wd_bf.reshape(w_down.shape)


def _fox_kernel(q_ref, qb_ref, k_ref, kb_ref, vt_ref, og_ref, ong_ref, o_ref, m_ref, l_ref, acc_ref,
                *, seq):
    hp = pl.program_id(1)
    lane = _iota((1, LANES), 1)
    q = q_ref[...]
    qb = qb_ref[...]
    zero = jnp.zeros_like(q)
    qcat = [jnp.concatenate([jnp.where(lane // HEAD_DIM == hh, q, zero),
                             jnp.where(lane // 8 == hp * 2 + hh, qb, zero)], axis=1)
            for hh in range(2)]
    keys = min(FOX_SUB_KEYS, seq)
    n_sub = seq // keys
    half = keys // 2
    mask_a = _iota((half, half), 1) >= _iota((half, half), 0)
    mask_b = _iota((keys, half), 1) + half >= _iota((keys, half), 0)

    m_ref[...] = jnp.full(m_ref.shape, -jnp.inf, F32)
    l_ref[...] = jnp.zeros(l_ref.shape, F32)
    acc_ref[...] = jnp.zeros(acc_ref.shape, F32)

    def plan(s):
        lo = s * keys
        pieces = [(half, slice(lo, lo + half), mask_a), (keys, slice(lo + half, lo + keys), mask_b)]
        if lo + keys < seq:
            pieces.append((keys, slice(lo + keys, seq), None))
        return pieces

    def scores(s):
        lo = s * keys
        kcat = jnp.concatenate([k_ref[lo:lo + keys, :], kb_ref[lo:lo + keys, :]], axis=1)
        return [lax.dot_general(kcat[:nk], qc[qs, :], _NT, preferred_element_type=F32)
                for qc in qcat for nk, qs, _ in plan(s)]

    pending = scores(0)
    for s in range(n_sub):
        lo = s * keys
        nxt = scores(s + 1) if s + 1 < n_sub else None
        n_piece = len(plan(s))
        pieces = []
        for hh in range(2):
            for (nk, qs, mask), st in zip(plan(s), pending[hh * n_piece:(hh + 1) * n_piece]):
                pieces.append((hh, nk, qs, st if mask is None else jnp.where(mask, st, -jnp.inf)))
        m_old = [m_ref[hh, :, qs] for hh, _, qs, _ in pieces]
        m_new = [jnp.maximum(m, jnp.max(st, axis=0, keepdims=True))
                 for m, (_, _, _, st) in zip(m_old, pieces)]
        pts = [jnp.exp2(st - m) for (_, _, _, st), m in zip(pieces, m_new)]
        pvs = [jnp.dot(vt_ref[:, lo:lo + nk], pt.astype(BF16), preferred_element_type=F32)
               for (_, nk, _, _), pt in zip(pieces, pts)]
        for (hh, _, qs, _), mo, mn, pt, pv in zip(pieces, m_old, m_new, pts, pvs):
            alpha = jnp.exp2(mo - mn)
            m_ref[hh, :, qs] = mn
            l_ref[hh, :, qs] = alpha * l_ref[hh, :, qs] + jnp.sum(pt, axis=0, keepdims=True)
            acc_ref[hh, :, qs] = (alpha * acc_ref[hh, :, qs]
                                  + pv[hh * HEAD_DIM:(hh + 1) * HEAD_DIM, :])
        pending = nxt

    outs = []
    for hh in range(2):
        o = acc_ref[hh] / l_ref[hh]
        outs.append(o * lax.rsqrt(jnp.mean(o * o, axis=0, keepdims=True) + NORM_EPS))
    o = jnp.concatenate(outs, axis=0).T
    o_ref[...] = (o * ong_ref[...] * _sigmoid(og_ref[...].astype(F32))).astype(BF16)


def _fox(p_x, k_bias, q_bias, v_t, o_gain):
    bsz, seq, _ = p_x.shape
    npair = N_HEADS // 2
    return pl.pallas_call(
        functools.partial(_fox_kernel, seq=seq),
        grid=(bsz, npair),
        in_specs=[pl.BlockSpec((None, seq, LANES), lambda b, h: (b, 0, h)),
                  pl.BlockSpec((None, seq, LANES), lambda b, h: (b, 0, 0)),
                  pl.BlockSpec((None, seq, LANES), lambda b, h: (b, 0, npair + h)),
                  pl.BlockSpec((None, seq, LANES), lambda b, h: (b, 0, 0)),
                  pl.BlockSpec((None, LANES, seq), lambda b, h: (b, h, 0)),
                  pl.BlockSpec((None, seq, LANES), lambda b, h: (b, 0, 2 * npair + h)),
                  pl.BlockSpec((1, LANES), lambda b, h: (0, 0))],
        out_specs=pl.BlockSpec((None, seq, LANES), lambda b, h: (b, 0, h)),
        out_shape=jax.ShapeDtypeStruct((bsz, seq, D_GRP), BF16),
        scratch_shapes=[pltpu.VMEM((2, 1, seq), F32), pltpu.VMEM((2, 1, seq), F32),
                        pltpu.VMEM((2, HEAD_DIM, seq), F32)],
        compiler_params=pltpu.CompilerParams(
            dimension_semantics=("parallel", "parallel"), vmem_limit_bytes=VMEM_LIMIT),
        name="fox",
    )(p_x, q_bias, p_x, k_bias, v_t, p_x, o_gain)


def _outproj_kernel(x_ref, yr_ref, yf_ref, g1_ref, sh_ref, sc_ref, ng_ref, wor_ref, wof_ref,
                    wrt_ref, wrl_ref, brt_ref, x1_ref, h2_ref, idx_ref, gate_ref, rank_ref, cnt_ref,
                    carry_ref):
    @pl.when(pl.program_id(0) == 0)
    def _():
        carry_ref[...] = jnp.zeros_like(carry_ref)

    y = (jnp.dot(yr_ref[...], wor_ref[...], preferred_element_type=F32)
         + jnp.dot(yf_ref[...], wof_ref[...], preferred_element_type=F32))
    x1 = x_ref[...] + g1_ref[...] * y
    x1_ref[...] = x1
    tm = x1.shape[0]
    h = x1 * lax.rsqrt(jnp.mean(x1 * x1, axis=-1, keepdims=True) + NORM_EPS) * ng_ref[...]
    h2 = h * (1.0 + sc_ref[...]) + sh_ref[...]
    h2_ref[...] = _pack_rows(h2)

    h_hi = h2.astype(BF16)
    h_lo = (h2 - h_hi.astype(F32)).astype(BF16)
    logits = (lax.dot_general(wrt_ref[...], h_hi, _NT, preferred_element_type=F32)
              + lax.dot_general(wrt_ref[...], h_lo, _NT, preferred_element_type=F32)
              + lax.dot_general(wrl_ref[...], h_hi, _NT, preferred_element_type=F32))
    lg = logits[:N_EXPERTS, :] + brt_ref[...]
    expert = _iota((N_EXPERTS, tm), 0)
    picks = []
    hot_sum = jnp.zeros((N_EXPERTS, tm), F32)
    for _ in range(TOP_K):
        m = jnp.max(lg, axis=0, keepdims=True)
        sel = jnp.min(jnp.where(lg == m, expert, N_EXPERTS), axis=0, keepdims=True)
        hot = expert == sel
        picks.append((m, sel, hot))
        hot_sum = hot_sum + hot.astype(F32)
        lg = jnp.where(hot, -jnp.inf, lg)
    es = [jnp.exp(m - picks[0][0]) for m, _, _ in picks]
    den = es[0] + es[1] + es[2] + es[3]

    earlier = (_iota((tm, tm), 0) < _iota((tm, tm), 1)).astype(BF16)
    before = jnp.dot(hot_sum.astype(BF16), earlier, preferred_element_type=F32) + carry_ref[...]
    ranks = [jnp.sum(jnp.where(hot, before, 0.0), axis=0, keepdims=True).astype(jnp.int32)
             for _, _, hot in picks]
    pad_i = jnp.zeros((8 - TOP_K, tm), jnp.int32)
    idx_ref[...] = jnp.concatenate([sel for _, sel, _ in picks] + [pad_i], axis=0)
    gate_ref[...] = jnp.concatenate([e / den for e in es] + [pad_i.astype(F32)], axis=0)
    rank_ref[...] = jnp.concatenate(ranks + [pad_i], axis=0)
    carry_ref[...] = carry_ref[...] + jnp.sum(hot_sum, axis=1, keepdims=True)
    cnt_ref[...] = jnp.broadcast_to(carry_ref[...], cnt_ref.shape)


def _outproj(x2d, y_r, y_f, gate1, shift2, scale2, norm_g, wo_r, wo_f, w_rt, b_rt, tm, seq,
             row0, t):
    w_rt_hi = w_rt.astype(BF16)
    w_rt_lo = (w_rt - w_rt_hi.astype(F32)).astype(BF16)
    per_b = seq // tm
    blk0 = row0 // tm
    const = lambda i: (0, 0)
    rows = lambda i: (i, 0)
    rows_in = lambda i: (i + blk0, 0)
    mod = pl.BlockSpec((None, 1, D_MODEL), lambda i: ((i + blk0) // per_b, 0, 0))
    return pl.pallas_call(
        _outproj_kernel,
        grid=(t // tm,),
        in_specs=[pl.BlockSpec((tm, D_MODEL), rows_in),
                  pl.BlockSpec((tm, D_GRP), rows_in),
                  pl.BlockSpec((tm, D_GRP), rows_in),
                  mod, mod, mod,
                  pl.BlockSpec((1, D_MODEL), const),
                  pl.BlockSpec((D_GRP, D_MODEL), const),
                  pl.BlockSpec((D_GRP, D_MODEL), const),
                  pl.BlockSpec((LANES, D_MODEL), const),
                  pl.BlockSpec((LANES, D_MODEL), const),
                  pl.BlockSpec((N_EXPERTS, 1), const)],
        out_specs=[pl.BlockSpec((tm, D_MODEL), rows),
                   pl.BlockSpec((tm, D_PACK), rows),
                   pl.BlockSpec((8, tm), lambda i: (0, i)),
                   pl.BlockSpec((8, tm), lambda i: (0, i)),
                   pl.BlockSpec((8, tm), lambda i: (0, i)),
                   pl.BlockSpec((N_EXPERTS, LANES), const)],
        out_shape=[jax.ShapeDtypeStruct((t, D_MODEL), F32),
                   jax.ShapeDtypeStruct((t, D_PACK), jnp.uint32),
                   jax.ShapeDtypeStruct((8, t), jnp.int32),
                   jax.ShapeDtypeStruct((8, t), F32),
                   jax.ShapeDtypeStruct((8, t), jnp.int32),
                   jax.ShapeDtypeStruct((N_EXPERTS, LANES), F32)],
        scratch_shapes=[pltpu.VMEM((N_EXPERTS, 1), F32)],
        compiler_params=pltpu.CompilerParams(
            dimension_semantics=("arbitrary",), vmem_limit_bytes=VMEM_LIMIT),
        name="outproj",
    )(x2d, y_r, y_f, gate1, shift2, scale2, norm_g, wo_r, wo_f, w_rt_hi, w_rt_lo, b_rt)


SC_CORES = 2
SC_SUBCORES = 16
SC_ROWS = 64


def _sc_gather_rows(idx, src):
    n_workers = SC_CORES * SC_SUBCORES
    m = idx.shape[0]
    d = src.shape[1]
    assert m % (n_workers * SC_ROWS) == 0
    n_chunks = m // (n_workers * SC_ROWS)
    mesh = plsc.VectorSubcoreMesh(core_axis_name="c", subcore_axis_name="s")

    @functools.partial(
        pl.kernel, mesh=mesh,
        out_type=jax.ShapeDtypeStruct((m, d), src.dtype),
        scratch_types=[pltpu.VMEM((n_chunks, SC_ROWS), jnp.int32),
                       pltpu.VMEM((SC_ROWS, d), src.dtype),
                       pltpu.SemaphoreType.DMA],
        name="sc_gather")
    def gather(src_hbm, idx_hbm, out_hbm, idx_v, rows_v, sem):
        wid = lax.axis_index("s") * SC_CORES + lax.axis_index("c")
        pltpu.sync_copy(idx_hbm.at[wid], idx_v)

        @pl.loop(0, n_chunks)
        def _(j):
            pltpu.async_copy(src_hbm.at[idx_v.at[j]], rows_v, sem).wait()
            pltpu.sync_copy(rows_v, out_hbm.at[pl.ds((wid * n_chunks + j) * SC_ROWS, SC_ROWS)])

    return gather(src, idx.reshape(n_workers, n_chunks, SC_ROWS))


def _sc_scatter_rows(src, dest, n_out):
    n_workers = SC_CORES * SC_SUBCORES
    t, d = src.shape
    n_slot = dest.shape[0]
    assert t % (n_workers * SC_ROWS) == 0
    n_chunks = t // (n_workers * SC_ROWS)
    mesh = plsc.VectorSubcoreMesh(core_axis_name="c", subcore_axis_name="s")
    idx = dest.reshape(n_slot, n_workers, n_chunks, SC_ROWS).transpose(1, 2, 0, 3)
    idx = idx.reshape(n_workers, n_chunks * n_slot, SC_ROWS)

    @functools.partial(
        pl.kernel, mesh=mesh,
        out_type=jax.ShapeDtypeStruct((n_out, d), src.dtype),
        scratch_types=[pltpu.VMEM((n_chunks * n_slot, SC_ROWS), jnp.int32),
                       pltpu.VMEM((SC_ROWS, d), src.dtype)],
        name="sc_scatter")
    def scatter(src_hbm, idx_hbm, out_hbm, idx_v, rows_v):
        wid = lax.axis_index("s") * SC_CORES + lax.axis_index("c")
        pltpu.sync_copy(idx_hbm.at[wid], idx_v)

        @pl.loop(0, n_chunks)
        def _(j):
            pltpu.sync_copy(src_hbm.at[pl.ds((wid * n_chunks + j) * SC_ROWS, SC_ROWS)], rows_v)
            for k in range(n_slot):
                pltpu.sync_copy(rows_v, out_hbm.at[idx_v.at[j * n_slot + k]])

    return scatter(src, idx)


def _expert_kernel(be_ref, nv_ref, x_ref, wgu_ref, bgu_ref, wd_ref, bd_ref, o_ref):
    del be_ref
    valid = _iota((EXPERT_BLOCK, 1), 0) < nv_ref[pl.program_id(0)]
    lo, hi = _unpack_rows(jnp.where(valid, x_ref[...], jnp.uint32(0)))
    x = jnp.concatenate([lo.astype(BF16), hi.astype(BF16)], axis=1)
    half = EXPERT_BLOCK // 2
    gus = [jnp.dot(x[r * half:(r + 1) * half], wgu_ref[...], preferred_element_type=F32)
           + bgu_ref[...] for r in range(2)]
    for r, gu in enumerate(gus):
        gate = jnp.minimum(gu[:, :D_MODEL], SWIGLU_LIMIT)
        up = jnp.clip(gu[:, D_MODEL:], -SWIGLU_LIMIT, SWIGLU_LIMIT)
        act = gate * _sigmoid(SWIGLU_ALPHA * gate) * (up + 1.0)
        o_ref[r * half:(r + 1) * half, :] = _pack_rows(
            jnp.dot(act.astype(BF16), wd_ref[...], preferred_element_type=F32) + bd_ref[...])


def _experts(block_e, n_valid, xs, w_gu, b_gu, w_d, b_d):
    n_blocks = block_e.shape[0]
    grid_spec = pltpu.PrefetchScalarGridSpec(
        num_scalar_prefetch=2,
        grid=(n_blocks,),
        in_specs=[pl.BlockSpec((EXPERT_BLOCK, D_PACK), lambda j, be, nv: (j, 0)),
                  pl.BlockSpec((None, D_MODEL, 2 * D_MODEL), lambda j, be, nv: (be[j], 0, 0)),
                  pl.BlockSpec((None, 1, 2 * D_MODEL), lambda j, be, nv: (be[j], 0, 0)),
                  pl.BlockSpec((None, D_MODEL, D_MODEL), lambda j, be, nv: (be[j], 0, 0)),
                  pl.BlockSpec((None, 1, D_MODEL), lambda j, be, nv: (be[j], 0, 0))],
        out_specs=pl.BlockSpec((EXPERT_BLOCK, D_PACK), lambda j, be, nv: (j, 0)),
    )
    return pl.pallas_call(
        _expert_kernel,
        grid_spec=grid_spec,
        out_shape=jax.ShapeDtypeStruct(xs.shape, jnp.uint32),
        compiler_params=pltpu.CompilerParams(
            dimension_semantics=("arbitrary",), vmem_limit_bytes=VMEM_LIMIT),
        name="experts",
    )(block_e, n_valid, xs, w_gu, b_gu, w_d, b_d)


COMBINE_TOKENS = 1024
MOE_SPLITS = 2


def _combine_kernel(yg_ref, x1_ref, gate_ref, g2_ref, fg_ref, o_ref):
    gates = gate_ref[...].T
    acc_lo = acc_hi = None
    for kk in range(TOP_K):
        lo, hi = _unpack_rows(yg_ref[kk * COMBINE_TOKENS:(kk + 1) * COMBINE_TOKENS, :])
        g = gates[:, kk:kk + 1]
        acc_lo = g * lo if acc_lo is None else acc_lo + g * lo
        acc_hi = g * hi if acc_hi is None else acc_hi + g * hi
    x2 = x1_ref[...] + g2_ref[...] * jnp.concatenate([acc_lo, acc_hi], axis=1)
    o_ref[...] = x2 * lax.rsqrt(jnp.mean(x2 * x2, axis=-1, keepdims=True) + NORM_EPS) * fg_ref[...]


def _combine_kernel_into(prev_ref, *refs):
    del prev_ref
    _combine_kernel(*refs)


def _combine(yg, x1, gates, gate2, final_g, seq, row0, t_total, prev):
    t = x1.shape[0]
    tm = COMBINE_TOKENS
    per_b = seq // tm
    blk0 = row0 // tm
    rows = lambda i: (i, 0)
    in_specs = [pl.BlockSpec((TOP_K * tm, D_PACK), rows),
                pl.BlockSpec((tm, D_MODEL), rows),
                pl.BlockSpec((8, tm), lambda i: (0, i)),
                pl.BlockSpec((None, 1, D_MODEL), lambda i: ((i + blk0) // per_b, 0, 0)),
                pl.BlockSpec((1, D_MODEL), lambda i: (0, 0))]
    args = (yg, x1, gates, gate2, final_g)
    if prev is not None:
        in_specs = [pl.BlockSpec(memory_space=pl.ANY)] + in_specs
        args = (prev,) + args
    return pl.pallas_call(
        _combine_kernel if prev is None else _combine_kernel_into,
        grid=(t // tm,),
        in_specs=in_specs,
        out_specs=pl.BlockSpec((tm, D_MODEL), lambda i: (i + blk0, 0)),
        out_shape=jax.ShapeDtypeStruct((t_total, D_MODEL), F32),
        input_output_aliases={} if prev is None else {0: 0},
        compiler_params=pltpu.CompilerParams(
            dimension_semantics=("parallel",), vmem_limit_bytes=VMEM_LIMIT),
        name="combine",
    )(*args)


def _moe(h2, idx, gates, rank, counts, x1, gate2, final_g, w_gu, b_gu, w_d, b_d, seq,
         row0, t_total, prev):
    t = h2.shape[0]
    n_slots = t * TOP_K
    n_blocks = -(-n_slots // EXPERT_BLOCK) + N_EXPERTS
    cap = n_blocks * EXPERT_BLOCK
    padded = (counts + EXPERT_BLOCK - 1) // EXPERT_BLOCK * EXPERT_BLOCK
    pad_ends = jnp.cumsum(padded)
    pad_starts = pad_ends - padded
    experts = jnp.arange(N_EXPERTS, dtype=jnp.int32)
    dest = jnp.sum(jnp.where(idx[..., None] == experts, pad_starts, 0), axis=-1) + rank
    block_starts = jnp.arange(n_blocks, dtype=jnp.int32) * EXPERT_BLOCK
    block_e = jnp.minimum(jnp.sum(block_starts[:, None] >= pad_ends[None, :], axis=1),
                          N_EXPERTS - 1).astype(jnp.int32)
    n_valid = jnp.clip(counts[block_e] - (block_starts - pad_starts[block_e]), 0, EXPERT_BLOCK)

    xs = _sc_scatter_rows(h2, dest, cap)
    yb = _experts(block_e, n_valid.astype(jnp.int32), xs, w_gu, b_gu, w_d, b_d)
    dest_blocks = dest.reshape(TOP_K, -1, COMBINE_TOKENS).transpose(1, 0, 2).reshape(-1)
    yg = _sc_gather_rows(dest_blocks, yb)
    return _combine(yg, x1, gates, gate2, final_g, seq, row0, t_total, prev)


def _layer(x, c_mod, norm1_g, w_in, mu_shift, w0, w2, a0, a2, g2, k_k, k_a, r_k, gn_w, gn_b, b_f,
           q_norm_g, k_norm_g, o_norm_g, w_out, norm2_g, w_router, b_router, w_gate_up,
           b_gate_up, w_down, b_down, final_g, tm_in, tm_out):
    bsz, seq, _ = x.shape
    shift1, scale1, gate1, shift2, scale2, gate2 = (
        m.reshape(bsz, 1, D_MODEL) for m in jnp.split(c_mod, 6, axis=-1))
    row = lambda v: v.reshape(1, -1)

    w_r = w_in[:, :RWKV_COLS].astype(BF16)
    w_x = w_in[:, RWKV_COLS:RWKV_COLS + FOX_MAIN].astype(BF16)
    w_f = w_in[:, RWKV_COLS + FOX_MAIN:].T
    b_fp = jnp.pad(b_f, (0, LANES - N_HEADS)).reshape(1, LANES)
    qk_gain = jnp.concatenate([jnp.tile(q_norm_g, N_HEADS) * (HEAD_DIM ** -0.5 * LOG2_E),
                               jnp.tile(k_norm_g, N_HEADS)]).reshape(1, -1)
    p_r, p_x, k_bias, q_bias, v_t = _inproj(x, shift1, scale1, row(norm1_g), w_r, w_x, w_f, b_fp,
                                            qk_gain, tm_in)

    zeros = jnp.zeros((LANES - 64, D_GRP), F32)
    w2p = jnp.concatenate([w2, zeros], axis=0).astype(BF16)
    a2p = jnp.concatenate([zeros, a2], axis=0).astype(BF16)
    y_r, w_gu, w_d = _rwkv(p_r, row(mu_shift), row(w0), w2p, row(a0), a2p, g2.astype(BF16),
                           row(k_k), row(k_a), row(r_k), row(gn_w), row(gn_b), w_gate_up, w_down)

    y_f = _fox(p_x, k_bias, q_bias, v_t, jnp.tile(o_norm_g, 2).reshape(1, LANES))

    t = bsz * seq
    w_rt = jnp.pad(w_router.T, ((0, LANES - N_EXPERTS), (0, 0)))
    b_rt = b_router.reshape(N_EXPERTS, 1)
    wo = w_out.astype(BF16)
    b_gu, b_d = b_gate_up.reshape(N_EXPERTS, 1, -1), b_down.reshape(N_EXPERTS, 1, -1)
    t_part = t // MOE_SPLITS
    out = None
    for part in range(MOE_SPLITS):
        row0 = part * t_part
        x1, h2, idx, gates, rank, cnt = _outproj(
            x.reshape(t, D_MODEL), y_r.reshape(t, D_GRP), y_f.reshape(t, D_GRP), gate1, shift2,
            scale2, row(norm2_g), wo[:D_GRP], wo[D_GRP:], w_rt, b_rt, tm_out, seq, row0, t_part)
        counts = cnt[:, 0].astype(jnp.int32)
        out = _moe(h2, idx[:TOP_K], gates, rank[:TOP_K], counts, x1, gate2, row(final_g),
                   w_gu, b_gu, w_d, b_d, seq, row0, t, out)
    return out.reshape(bsz, seq, D_MODEL)


def kernel(x, c, w_ada, b_ada, norm1_g, w_in, mu_shift, w0, w2, a0, a2, g2, k_k, k_a, r_k, gn_w,
           gn_b, b_f, q_norm_g, k_norm_g, o_norm_g, w_out, norm2_g, w_router, b_router, w_gate_up,
           b_gate_up, w_down, b_down, final_g):
    assert w_ada.shape[0] == 1, "single-layer block"
    c_mod = _adaln(c, w_ada[0], b_ada[0])
    return _layer(x, c_mod, norm1_g[0], w_in[0], mu_shift[0], w0[0], w2[0], a0[0], a2[0], g2[0],
                  k_k[0], k_a[0], r_k[0], gn_w[0], gn_b[0], b_f[0], q_norm_g[0], k_norm_g[0],
                  o_norm_g[0], w_out[0], norm2_g[0], w_router[0], b_router[0], w_gate_up[0],
                  b_gate_up[0], w_down[0], b_down[0], final_g,
                  tm_in=min(512, x.shape[1]), tm_out=min(1024, x.shape[1]))
```

```python
import functools

import jax
import jax.numpy as jnp
from jax import lax
from jax.experimental import pallas as pl
from jax.experimental.pallas import tpu as pltpu
from jax.experimental.pallas import tpu_sc as plsc

F32 = jnp.float32
BF16 = jnp.bfloat16
HIGHEST = lax.Precision.HIGHEST

D_MODEL = 1024
HEAD_DIM = 64
N_HEADS = 8
D_GRP = N_HEADS * HEAD_DIM
RWKV_COLS = 1792
LORA_OFF = 3 * D_GRP
GATE_OFF = LORA_OFF + 128
FOX_MAIN = 4 * D_GRP
N_EXPERTS = 32
TOP_K = 4
EXPERT_BLOCK = 512
SWIGLU_ALPHA = 1.702
SWIGLU_LIMIT = 7.0
NORM_EPS = 1e-6
GN_EPS = 64e-5
LOG2_E = 1.4426950408889634
LANES = 128
CHUNK = 64
FOX_SUB_KEYS = 512
HEADS_PER_SCAN = 4
SCAN_W = HEADS_PER_SCAN * HEAD_DIM
SEG_TERMS = 1
CUM_TERMS = 2
VMEM_LIMIT = 56 * 1024 * 1024


def _dot(a, b):
    return jnp.dot(a.astype(BF16), b.astype(BF16), preferred_element_type=F32)


def _fdot(a, b):
    return jnp.dot(a, b, precision=HIGHEST, preferred_element_type=F32)


def _split_dot(x, m, terms):
    acc = None
    rem = x
    for _ in range(terms):
        part = rem.astype(BF16)
        rem = rem - part.astype(F32)
        d = jnp.dot(part, m, preferred_element_type=F32)
        acc = d if acc is None else acc + d
    return acc


def _tri_dot(m, x, terms):
    acc = None
    rem = x
    for _ in range(terms):
        part = rem.astype(BF16)
        rem = rem - part.astype(F32)
        d = jnp.dot(m, part, preferred_element_type=F32)
        acc = d if acc is None else acc + d
    return acc


def _iota(shape, dim):
    return lax.broadcasted_iota(jnp.int32, shape, dim)


def _seg_reduce_mat(n):
    return (_iota((n, LANES), 0) // HEAD_DIM == _iota((n, LANES), 1)).astype(BF16)


def _seg_expand_mat(n):
    return (_iota((LANES, n), 1) // HEAD_DIM == _iota((LANES, n), 0)).astype(BF16)


D_PACK = D_MODEL // 2


def _pack_rows(x):
    lo = lax.bitcast_convert_type(x[:, :D_PACK].astype(BF16).astype(F32), jnp.uint32)
    hi = lax.bitcast_convert_type(x[:, D_PACK:].astype(BF16).astype(F32), jnp.uint32)
    return hi | (lo >> 16)


def _unpack_rows(p):
    lo = lax.bitcast_convert_type(p << 16, F32)
    hi = lax.bitcast_convert_type(p & jnp.uint32(0xFFFF0000), F32)
    return lo, hi


def _log_sigmoid(z):
    return jnp.minimum(z, 0.0) - jnp.log(1.0 + jnp.exp(-jnp.abs(z)))


def _sigmoid(z):
    return 1.0 / (1.0 + jnp.exp(-z))


def _adaln_kernel(c_ref, w_ref, b_ref, o_ref):
    c = c_ref[...]
    o_ref[...] = _fdot(c * _sigmoid(c), w_ref[...]) + b_ref[...]


def _adaln(c, w_ada, b_ada):
    bsz = c.shape[0]
    n_mod = w_ada.shape[1] // D_MODEL
    return pl.pallas_call(
        _adaln_kernel,
        grid=(n_mod,),
        in_specs=[pl.BlockSpec((bsz, D_MODEL), lambda j: (0, 0)),
                  pl.BlockSpec((D_MODEL, D_MODEL), lambda j: (0, j)),
                  pl.BlockSpec((1, D_MODEL), lambda j: (0, j))],
        out_specs=pl.BlockSpec((bsz, D_MODEL), lambda j: (0, j)),
        out_shape=jax.ShapeDtypeStruct((bsz, n_mod * D_MODEL), F32),
        name="adaln",
    )(c, w_ada, b_ada.reshape(1, -1))


def _inproj_kernel(x_ref, sh_ref, sc_ref, g_ref, wr_ref, wx_ref, wft_ref, bf_ref, qkg_ref,
                   pr_ref, px_ref, kb_ref, qb_ref, vt_ref, carry_ref):
    @pl.when(pl.program_id(1) == 0)
    def _():
        carry_ref[...] = jnp.zeros_like(carry_ref)

    x = x_ref[...]
    tm = x.shape[0]
    h = x * lax.rsqrt(jnp.mean(x * x, axis=-1, keepdims=True) + NORM_EPS) * g_ref[...]
    h = h * (1.0 + sc_ref[...]) + sh_ref[...]
    hb = h.astype(BF16)

    pr_ref[...] = jnp.dot(hb, wr_ref[...], preferred_element_type=F32).astype(BF16)

    px = jnp.dot(hb, wx_ref[...], preferred_element_type=F32)
    qk = px[:, :2 * D_GRP]
    ss = _split_dot(qk * qk, _seg_reduce_mat(2 * D_GRP), SEG_TERMS)
    inv = lax.rsqrt(ss * (1.0 / HEAD_DIM) + NORM_EPS)
    qk = qk * _split_dot(inv, _seg_expand_mat(2 * D_GRP), SEG_TERMS) * qkg_ref[...]
    px_ref[:, :2 * D_GRP] = qk.astype(BF16)
    px_ref[:, 2 * D_GRP:] = px[:, 2 * D_GRP:].astype(BF16)
    vt_ref[...] = px[:, 2 * D_GRP:3 * D_GRP].T.astype(BF16)

    lane = _iota((1, LANES), 1)
    z = jnp.zeros((tm, LANES), F32)
    for hd in range(N_HEADS):
        zh = jnp.sum(h * wft_ref[hd:hd + 1, :], axis=-1, keepdims=True)
        z = jnp.where(lane == hd, zh, z)
    cum = _log_sigmoid(z + bf_ref[...])
    row_id = _iota((tm, 1), 0)
    shift = 1
    while shift < tm:
        cum = cum + jnp.where(row_id >= shift, pltpu.roll(cum, shift, axis=0), 0.0)
        shift *= 2
    cum = cum + carry_ref[...]
    carry_ref[...] = cum[tm - 1:tm, :]

    parts = []
    rem = cum * LOG2_E
    for _ in range(3):
        part = rem.astype(BF16)
        rem = rem - part.astype(F32)
        parts.append(part)
    src, dst = _iota((LANES, LANES), 0), _iota((LANES, LANES), 1)

    def spread(offset):
        return sum(jnp.dot(part, ((dst == 8 * src + offset + t) & (src < N_HEADS)).astype(BF16),
                           preferred_element_type=F32) for t, part in enumerate(parts))

    slot = _iota((1, LANES), 1) % 8
    kb_ref[...] = (jnp.where((slot >= 3) & (slot < 6), 1.0, 0.0) - spread(0)).astype(BF16)
    qb_ref[...] = (jnp.where(slot < 3, 1.0, 0.0) + spread(3)).astype(BF16)


def _inproj(x, shift, scale, g, w_r, w_x, w_f_t, b_f, qk_gain, tm):
    bsz, seq, _ = x.shape
    const = lambda b, s: (0, 0)
    return pl.pallas_call(
        _inproj_kernel,
        grid=(bsz, seq // tm),
        in_specs=[pl.BlockSpec((None, tm, D_MODEL), lambda b, s: (b, s, 0)),
                  pl.BlockSpec((None, 1, D_MODEL), lambda b, s: (b, 0, 0)),
                  pl.BlockSpec((None, 1, D_MODEL), lambda b, s: (b, 0, 0)),
                  pl.BlockSpec((1, D_MODEL), const),
                  pl.BlockSpec((D_MODEL, RWKV_COLS), const),
                  pl.BlockSpec((D_MODEL, FOX_MAIN), const),
                  pl.BlockSpec((N_HEADS, D_MODEL), const),
                  pl.BlockSpec((1, LANES), const),
                  pl.BlockSpec((1, 2 * D_GRP), const)],
        out_specs=[pl.BlockSpec((None, tm, RWKV_COLS), lambda b, s: (b, s, 0)),
                   pl.BlockSpec((None, tm, FOX_MAIN), lambda b, s: (b, s, 0)),
                   pl.BlockSpec((None, tm, LANES), lambda b, s: (b, s, 0)),
                   pl.BlockSpec((None, tm, LANES), lambda b, s: (b, s, 0)),
                   pl.BlockSpec((None, D_GRP, tm), lambda b, s: (b, 0, s))],
        out_shape=[jax.ShapeDtypeStruct((bsz, seq, RWKV_COLS), BF16),
                   jax.ShapeDtypeStruct((bsz, seq, FOX_MAIN), BF16),
                   jax.ShapeDtypeStruct((bsz, seq, LANES), BF16),
                   jax.ShapeDtypeStruct((bsz, seq, LANES), BF16),
                   jax.ShapeDtypeStruct((bsz, D_GRP, seq), BF16)],
        scratch_shapes=[pltpu.VMEM((1, LANES), F32)],
        compiler_params=pltpu.CompilerParams(
            dimension_semantics=("parallel", "arbitrary"), vmem_limit_bytes=VMEM_LIMIT),
        name="inproj",
    )(x, shift, scale, g, w_r, w_x, w_f_t, b_f, qk_gain)


_NN = (((1,), (0,)), ((), ()))
_NT = (((1,), (1,)), ((), ()))
_TN = (((0,), (0,)), ((), ()))
SCAN_N = HEADS_PER_SCAN * CHUNK
BATCH_PER_STEP = 8
INV_LEVELS = 5
M_HEAD, M_STRICT, M_INCL, M_EYE, M_BASE, M_OFF = 0, 1, 2, 3, 4, 5


def _bdot(a, b, dims):
    return lax.dot_general(a, b, dims, preferred_element_type=F32)


def _scan_masks():
    rr, cc = _iota((SCAN_N, SCAN_W), 0), _iota((SCAN_N, SCAN_W), 1)
    ri, ci = _iota((SCAN_N, SCAN_N), 0), _iota((SCAN_N, SCAN_N), 1)
    same = ri // CHUNK == ci // CHUNK
    masks = [rr // CHUNK == cc // HEAD_DIM, same & (ri > ci), same & (ri >= ci), ri == ci,
             (ri // 2 == ci // 2) & (ri > ci)]
    blk = 2
    while blk < CHUNK:
        masks.append((ri // (2 * blk) == ci // (2 * blk)) & (ri // blk != ci // blk) & (ri > ci))
        blk *= 2
    return jnp.stack(masks).astype(BF16)


def _rwkv_kernel(p_ref, masks_ref, mu_ref, w0_ref, w2_ref, a0_ref, a2_ref, g2_ref, kk_ref, ka_ref,
                 rk_ref, gnw_ref, gnb_ref, wgu_ref, wd_ref, o_ref, wgu_bf_ref, wd_bf_ref,
                 last_ref, state_ref):
    wgu_bf_ref[...] = wgu_ref[...].astype(BF16)
    wd_bf_ref[...] = wd_ref[...].astype(BF16)

    @pl.when(pl.program_id(1) == 0)
    def _():
        last_ref[...] = jnp.zeros_like(last_ref)
        state_ref[...] = jnp.zeros_like(state_ref)

    mu, w0, w2, a0, a2, g2, k_k, k_a, r_k, gn_w, gn_b = (
        ref[...] for ref in (mu_ref, w0_ref, w2_ref, a0_ref, a2_ref, g2_ref, kk_ref, ka_ref,
                             rk_ref, gnw_ref, gnb_ref))
    rows = BATCH_PER_STEP * CHUNK
    p = p_ref[...].astype(F32).reshape(rows, RWKV_COLS)
    row_id = _iota((rows, 1), 0)
    prev = pltpu.roll(p, 1, axis=0)
    for bb in range(BATCH_PER_STEP):
        prev = jnp.where(row_id == bb * CHUNK, last_ref[bb], prev)
        last_ref[bb] = p[(bb + 1) * CHUNK - 1:(bb + 1) * CHUNK, :]
    pf = p + mu * (prev - p)
    r = pf[:, 0:D_GRP]
    k = pf[:, D_GRP:2 * D_GRP]
    v = pf[:, 2 * D_GRP:3 * D_GRP]
    lora = pf[:, LORA_OFF:GATE_OFF]
    gd = pf[:, GATE_OFF:RWKV_COLS]

    wlog = w0 + _dot(jnp.tanh(lora), w2)
    neg = -wlog
    softplus = jnp.maximum(neg, 0.0) + jnp.log(1.0 + jnp.exp(-jnp.abs(neg)))
    ld = -jnp.exp(-softplus - 0.5)
    a = _sigmoid(a0 + _dot(lora, a2))
    g = _dot(_sigmoid(gd), g2)

    red, exp_m = _seg_reduce_mat(D_GRP), _seg_expand_mat(D_GRP)
    kk = k * k_k
    n2 = _split_dot(kk * kk, red, SEG_TERMS)
    kk = kk * _split_dot(1.0 / jnp.maximum(jnp.sqrt(n2), 1e-12), exp_m, SEG_TERMS)
    k2 = k * (1.0 + (a - 1.0) * k_a)

    tr, tc = _iota((rows, rows), 0), _iota((rows, rows), 1)
    tri = ((tr >= tc) & (tr // CHUNK == tc // CHUNK)).astype(BF16)
    cl = _tri_dot(tri, ld, CUM_TERMS)
    cl_end = jnp.concatenate(
        [jnp.broadcast_to(cl[(bb + 1) * CHUNK - 1:(bb + 1) * CHUNK, :], (CHUNK, D_GRP))
         for bb in range(BATCH_PER_STEP)], axis=0)
    e_in = jnp.exp(cl)
    e_out = jnp.exp(-cl)
    e_rem = jnp.exp(cl_end - cl)
    p_end = jnp.exp(cl_end)
    kka = kk * a
    ops = [(-kk * jnp.exp(cl - ld)).astype(BF16), (kka * e_out).astype(BF16),
           (k2 * e_out).astype(BF16), (r * e_in).astype(BF16), v.astype(BF16),
           (kka * e_rem).astype(BF16), (k2 * e_rem).astype(BF16)]

    chains = [(bb, grp) for bb in range(BATCH_PER_STEP)
              for grp in range(N_HEADS // HEADS_PER_SCAN)]
    head_mask = masks_ref[M_HEAD]
    strict, incl = masks_ref[M_STRICT], masks_ref[M_INCL]

    def stacked(op, bb, grp):
        part = op[bb * CHUNK:(bb + 1) * CHUNK, grp * SCAN_W:(grp + 1) * SCAN_W]
        return jnp.concatenate([part] * HEADS_PER_SCAN, axis=0) * head_mask

    xs = [[stacked(op, bb, grp) for op in ops] for bb, grp in chains]
    st = [state_ref[bb, grp] for bb, grp in chains]
    sb = [s.astype(BF16) for s in st]
    nab = [_bdot(x[0], x[1], _NT).astype(BF16) for x in xs]
    aak = [_bdot(x[0], x[2], _NT).astype(BF16) * strict for x in xs]
    arb = [_bdot(x[3], x[1], _NT).astype(BF16) * incl for x in xs]
    ark = [_bdot(x[3], x[2], _NT).astype(BF16) * incl for x in xs]
    t_inv = [masks_ref[M_EYE] + n * masks_ref[M_BASE] for n in nab]
    for lvl in range(INV_LEVELS):
        half = [_bdot(t, n * masks_ref[M_OFF + lvl], _NN).astype(BF16) for t, n in zip(t_inv, nab)]
        t_inv = [t + _bdot(h, t, _NN).astype(BF16) for t, h in zip(t_inv, half)]
    rhs = [(_bdot(x[0], s, _NT) + _bdot(k, x[4], _NN)).astype(BF16)
           for x, s, k in zip(xs, sb, aak)]
    sa = [_bdot(t, h, _NN).astype(BF16) for t, h in zip(t_inv, rhs)]
    ys = [_bdot(x[3], s, _NT) + _bdot(b, u, _NN) + _bdot(k, x[4], _NN)
          for x, s, b, u, k in zip(xs, sb, arb, sa, ark)]
    for (bb, grp), x, s, u in zip(chains, xs, st, sa):
        decay = p_end[bb * CHUNK:bb * CHUNK + 1, grp * SCAN_W:(grp + 1) * SCAN_W]
        state_ref[bb, grp] = s * decay + _bdot(u, x[5], _TN) + _bdot(x[4], x[6], _TN)
    ys = [y[0:CHUNK] + y[CHUNK:2 * CHUNK] + y[2 * CHUNK:3 * CHUNK] + y[3 * CHUNK:4 * CHUNK]
          for y in ys]
    n_grp = N_HEADS // HEADS_PER_SCAN
    y = jnp.concatenate([jnp.concatenate(ys[bb * n_grp:(bb + 1) * n_grp], axis=1)
                         for bb in range(BATCH_PER_STEP)], axis=0)

    mean = _split_dot(_split_dot(y, red, SEG_TERMS) * (1.0 / HEAD_DIM), exp_m, SEG_TERMS)
    d = y - mean
    var = _split_dot(d * d, red, SEG_TERMS) * (1.0 / HEAD_DIM)
    yn = d * _split_dot(lax.rsqrt(var + GN_EPS), exp_m, SEG_TERMS) * gn_w + gn_b
    bonus = _split_dot(_split_dot(r * k2 * r_k, red, SEG_TERMS), exp_m, SEG_TERMS) * v
    o_ref[...] = ((yn + bonus) * g).astype(BF16).reshape(BATCH_PER_STEP, CHUNK, D_GRP)


def _rwkv(p_r, mu, w0, w2p, a0, a2p, g2, k_k, k_a, r_k, gn_w, gn_b, w_gate_up, w_down):
    bsz, seq, _ = p_r.shape
    assert bsz % BATCH_PER_STEP == 0
    n_chunk = seq // CHUNK
    n_step = (bsz // BATCH_PER_STEP) * n_chunk
    wgu2d = w_gate_up.reshape(-1, w_gate_up.shape[-1])
    wd2d = w_down.reshape(-1, w_down.shape[-1])
    assert wgu2d.shape[0] % (8 * n_step) == 0 and wd2d.shape[0] == wgu2d.shape[0]
    slab = wgu2d.shape[0] // n_step
    masks = _scan_masks()
    const = lambda b, s: (0, 0)
    step = lambda b, s: (b * n_chunk + s, 0)
    vec = pl.BlockSpec((1, D_GRP), const)
    y, wgu_bf, wd_bf = pl.pallas_call(
        _rwkv_kernel,
        grid=(bsz // BATCH_PER_STEP, n_chunk),
        in_specs=[pl.BlockSpec((BATCH_PER_STEP, CHUNK, RWKV_COLS), lambda b, s: (b, s, 0)),
                  pl.BlockSpec(masks.shape, lambda b, s: (0, 0, 0)),
                  pl.BlockSpec((1, RWKV_COLS), const),
                  vec, pl.BlockSpec((LANES, D_GRP), const),
                  vec, pl.BlockSpec((LANES, D_GRP), const),
                  pl.BlockSpec((LANES, D_GRP), const),
                  vec, vec, vec, vec, vec,
                  pl.BlockSpec((slab, wgu2d.shape[1]), step),
                  pl.BlockSpec((slab, wd2d.shape[1]), step)],
        out_specs=[pl.BlockSpec((BATCH_PER_STEP, CHUNK, D_GRP), lambda b, s: (b, s, 0)),
                   pl.BlockSpec((slab, wgu2d.shape[1]), step),
                   pl.BlockSpec((slab, wd2d.shape[1]), step)],
        out_shape=[jax.ShapeDtypeStruct((bsz, seq, D_GRP), BF16),
                   jax.ShapeDtypeStruct(wgu2d.shape, BF16),
                   jax.ShapeDtypeStruct(wd2d.shape, BF16)],
        scratch_shapes=[pltpu.VMEM((BATCH_PER_STEP, 1, RWKV_COLS), F32),
                        pltpu.VMEM((BATCH_PER_STEP, N_HEADS // HEADS_PER_SCAN, SCAN_W, SCAN_W), F32)],
        compiler_params=pltpu.CompilerParams(
            dimension_semantics=("parallel", "arbitrary"), vmem_limit_bytes=VMEM_LIMIT),
        name="rwkv",
    )(p_r, masks, mu, w0, w2p, a0, a2p, g2, k_k, k_a, r_k, gn_w, gn_b, wgu2d, wd2d)
    return y, wgu_bf.reshape(w_gate_up.shape), wd_bf.reshape(w_down.shape)


def _fox_kernel(q_ref, qb_ref, k_ref, kb_ref, vt_ref, og_ref, ong_ref, o_ref, m_ref, l_ref, acc_ref,
                *, seq):
    hp = pl.program_id(1)
    lane = _iota((1, LANES), 1)
    q = q_ref[...]
    qb = qb_ref[...]
    zero = jnp.zeros_like(q)
    qcat = [jnp.concatenate([jnp.where(lane // HEAD_DIM == hh, q, zero),
                             jnp.where(lane // 8 == hp * 2 + hh, qb, zero)], axis=1)
            for hh in range(2)]
    keys = min(FOX_SUB_KEYS, seq)
    n_sub = seq // keys
    half = keys // 2
    mask_a = _iota((half, half), 1) >= _iota((half, half), 0)
    mask_b = _iota((keys, half), 1) + half >= _iota((keys, half), 0)

    m_ref[...] = jnp.full(m_ref.shape, -jnp.inf, F32)
    l_ref[...] = jnp.zeros(l_ref.shape, F32)
    acc_ref[...] = jnp.zeros(acc_ref.shape, F32)

    def plan(s):
        lo = s * keys
        pieces = [(half, slice(lo, lo + half), mask_a), (keys, slice(lo + half, lo + keys), mask_b)]
        if lo + keys < seq:
            pieces.append((keys, slice(lo + keys, seq), None))
        return pieces

    def scores(s):
        lo = s * keys
        kcat = jnp.concatenate([k_ref[lo:lo + keys, :], kb_ref[lo:lo + keys, :]], axis=1)
        return [lax.dot_general(kcat[:nk], qc[qs, :], _NT, preferred_element_type=F32)
                for qc in qcat for nk, qs, _ in plan(s)]

    pending = scores(0)
    for s in range(n_sub):
        lo = s * keys
        nxt = scores(s + 1) if s + 1 < n_sub else None
        n_piece = len(plan(s))
        pieces = []
        for hh in range(2):
            for (nk, qs, mask), st in zip(plan(s), pending[hh * n_piece:(hh + 1) * n_piece]):
                pieces.append((hh, nk, qs, st if mask is None else jnp.where(mask, st, -jnp.inf)))
        m_old = [m_ref[hh, :, qs] for hh, _, qs, _ in pieces]
        m_new = [jnp.maximum(m, jnp.max(st, axis=0, keepdims=True))
                 for m, (_, _, _, st) in zip(m_old, pieces)]
        pts = [jnp.exp2(st - m) for (_, _, _, st), m in zip(pieces, m_new)]
        pvs = [jnp.dot(vt_ref[:, lo:lo + nk], pt.astype(BF16), preferred_element_type=F32)
               for (_, nk, _, _), pt in zip(pieces, pts)]
        for (hh, _, qs, _), mo, mn, pt, pv in zip(pieces, m_old, m_new, pts, pvs):
            alpha = jnp.exp2(mo - mn)
            m_ref[hh, :, qs] = mn
            l_ref[hh, :, qs] = alpha * l_ref[hh, :, qs] + jnp.sum(pt, axis=0, keepdims=True)
            acc_ref[hh, :, qs] = (alpha * acc_ref[hh, :, qs]
                                  + pv[hh * HEAD_DIM:(hh + 1) * HEAD_DIM, :])
        pending = nxt

    outs = []
    for hh in range(2):
        o = acc_ref[hh] / l_ref[hh]
        outs.append(o * lax.rsqrt(jnp.mean(o * o, axis=0, keepdims=True) + NORM_EPS))
    o = jnp.concatenate(outs, axis=0).T
    o_ref[...] = (o * ong_ref[...] * _sigmoid(og_ref[...].astype(F32))).astype(BF16)


def _fox(p_x, k_bias, q_bias, v_t, o_gain):
    bsz, seq, _ = p_x.shape
    npair = N_HEADS // 2
    return pl.pallas_call(
        functools.partial(_fox_kernel, seq=seq),
        grid=(bsz, npair),
        in_specs=[pl.BlockSpec((None, seq, LANES), lambda b, h: (b, 0, h)),
                  pl.BlockSpec((None, seq, LANES), lambda b, h: (b, 0, 0)),
                  pl.BlockSpec((None, seq, LANES), lambda b, h: (b, 0, npair + h)),
                  pl.BlockSpec((None, seq, LANES), lambda b, h: (b, 0, 0)),
                  pl.BlockSpec((None, LANES, seq), lambda b, h: (b, h, 0)),
                  pl.BlockSpec((None, seq, LANES), lambda b, h: (b, 0, 3 * npair + h)),
                  pl.BlockSpec((1, LANES), lambda b, h: (0, 0))],
        out_specs=pl.BlockSpec((None, seq, LANES), lambda b, h: (b, 0, h)),
        out_shape=jax.ShapeDtypeStruct((bsz, seq, D_GRP), BF16),
        scratch_shapes=[pltpu.VMEM((2, 1, seq), F32), pltpu.VMEM((2, 1, seq), F32),
                        pltpu.VMEM((2, HEAD_DIM, seq), F32)],
        compiler_params=pltpu.CompilerParams(
            dimension_semantics=("parallel", "parallel"), vmem_limit_bytes=VMEM_LIMIT),
        name="fox",
    )(p_x, q_bias, p_x, k_bias, v_t, p_x, o_gain)


def _outproj_kernel(x_ref, yr_ref, yf_ref, g1_ref, sh_ref, sc_ref, ng_ref, wor_ref, wof_ref,
                    wrt_ref, wrl_ref, brt_ref, x1_ref, h2_ref, idx_ref, gate_ref, rank_ref, cnt_ref,
                    carry_ref):
    @pl.when(pl.program_id(0) == 0)
    def _():
        carry_ref[...] = jnp.zeros_like(carry_ref)

    y = (jnp.dot(yr_ref[...], wor_ref[...], preferred_element_type=F32)
         + jnp.dot(yf_ref[...], wof_ref[...], preferred_element_type=F32))
    x1 = x_ref[...] + g1_ref[...] * y
    x1_ref[...] = x1
    tm = x1.shape[0]
    h = x1 * lax.rsqrt(jnp.mean(x1 * x1, axis=-1, keepdims=True) + NORM_EPS) * ng_ref[...]
    h2 = h * (1.0 + sc_ref[...]) + sh_ref[...]
    h2_ref[...] = _pack_rows(h2)

    h_hi = h2.astype(BF16)
    h_lo = (h2 - h_hi.astype(F32)).astype(BF16)
    logits = (lax.dot_general(wrt_ref[...], h_hi, _NT, preferred_element_type=F32)
              + lax.dot_general(wrt_ref[...], h_lo, _NT, preferred_element_type=F32)
              + lax.dot_general(wrl_ref[...], h_hi, _NT, preferred_element_type=F32))
    lg = logits[:N_EXPERTS, :] + brt_ref[...]
    expert = _iota((N_EXPERTS, tm), 0)
    picks = []
    hot_sum = jnp.zeros((N_EXPERTS, tm), F32)
    for _ in range(TOP_K):
        m = jnp.max(lg, axis=0, keepdims=True)
        sel = jnp.min(jnp.where(lg == m, expert, N_EXPERTS), axis=0, keepdims=True)
        hot = expert == sel
        picks.append((m, sel, hot))
        hot_sum = hot_sum + hot.astype(F32)
        lg = jnp.where(hot, -jnp.inf, lg)
    es = [jnp.exp(m - picks[0][0]) for m, _, _ in picks]
    den = es[0] + es[1] + es[2] + es[3]

    earlier = (_iota((tm, tm), 0) < _iota((tm, tm), 1)).astype(BF16)
    before = jnp.dot(hot_sum.astype(BF16), earlier, preferred_element_type=F32) + carry_ref[...]
    ranks = [jnp.sum(jnp.where(hot, before, 0.0), axis=0, keepdims=True).astype(jnp.int32)
             for _, _, hot in picks]
    pad_i = jnp.zeros((8 - TOP_K, tm), jnp.int32)
    idx_ref[...] = jnp.concatenate([sel for _, sel, _ in picks] + [pad_i], axis=0)
    gate_ref[...] = jnp.concatenate([e / den for e in es] + [pad_i.astype(F32)], axis=0)
    rank_ref[...] = jnp.concatenate(ranks + [pad_i], axis=0)
    carry_ref[...] = carry_ref[...] + jnp.sum(hot_sum, axis=1, keepdims=True)
    cnt_ref[...] = jnp.broadcast_to(carry_ref[...], cnt_ref.shape)


def _outproj(x2d, y_r, y_f, gate1, shift2, scale2, norm_g, wo_r, wo_f, w_rt, b_rt, tm, seq,
             row0, t):
    w_rt_hi = w_rt.astype(BF16)
    w_rt_lo = (w_rt - w_rt_hi.astype(F32)).astype(BF16)
    per_b = seq // tm
    blk0 = row0 // tm
    const = lambda i: (0, 0)
    rows = lambda i: (i, 0)
    rows_in = lambda i: (i + blk0, 0)
    mod = pl.BlockSpec((None, 1, D_MODEL), lambda i: ((i + blk0) // per_b, 0, 0))
    return pl.pallas_call(
        _outproj_kernel,
        grid=(t // tm,),
        in_specs=[pl.BlockSpec((tm, D_MODEL), rows_in),
                  pl.BlockSpec((tm, D_GRP), rows_in),
                  pl.BlockSpec((tm, D_GRP), rows_in),
                  mod, mod, mod,
                  pl.BlockSpec((1, D_MODEL), const),
                  pl.BlockSpec((D_GRP, D_MODEL), const),
                  pl.BlockSpec((D_GRP, D_MODEL), const),
                  pl.BlockSpec((LANES, D_MODEL), const),
                  pl.BlockSpec((LANES, D_MODEL), const),
                  pl.BlockSpec((N_EXPERTS, 1), const)],
        out_specs=[pl.BlockSpec((tm, D_MODEL), rows),
                   pl.BlockSpec((tm, D_PACK), rows),
                   pl.BlockSpec((8, tm), lambda i: (0, i)),
                   pl.BlockSpec((8, tm), lambda i: (0, i)),
                   pl.BlockSpec((8, tm), lambda i: (0, i)),
                   pl.BlockSpec((N_EXPERTS, LANES), const)],
        out_shape=[jax.ShapeDtypeStruct((t, D_MODEL), F32),
                   jax.ShapeDtypeStruct((t, D_PACK), jnp.uint32),
                   jax.ShapeDtypeStruct((8, t), jnp.int32),
                   jax.ShapeDtypeStruct((8, t), F32),
                   jax.ShapeDtypeStruct((8, t), jnp.int32),
                   jax.ShapeDtypeStruct((N_EXPERTS, LANES), F32)],
        scratch_shapes=[pltpu.VMEM((N_EXPERTS, 1), F32)],
        compiler_params=pltpu.CompilerParams(
            dimension_semantics=("arbitrary",), vmem_limit_bytes=VMEM_LIMIT),
        name="outproj",
    )(x2d, y_r, y_f, gate1, shift2, scale2, norm_g, wo_r, wo_f, w_rt_hi, w_rt_lo, b_rt)


SC_CORES = 2
SC_SUBCORES = 16
SC_ROWS = 64


def _sc_gather_rows(idx, src):
    n_workers = SC_CORES * SC_SUBCORES
    m = idx.shape[0]
    d = src.shape[1]
    assert m % (n_workers * SC_ROWS) == 0
    n_chunks = m // (n_workers * SC_ROWS)
    mesh = plsc.VectorSubcoreMesh(core_axis_name="c", subcore_axis_name="s")

    @functools.partial(
        pl.kernel, mesh=mesh,
        out_type=jax.ShapeDtypeStruct((m, d), src.dtype),
        scratch_types=[pltpu.VMEM((n_chunks, SC_ROWS), jnp.int32),
                       pltpu.VMEM((SC_ROWS, d), src.dtype),
                       pltpu.SemaphoreType.DMA],
        name="sc_gather")
    def gather(src_hbm, idx_hbm, out_hbm, idx_v, rows_v, sem):
        wid = lax.axis_index("s") * SC_CORES + lax.axis_index("c")
        pltpu.sync_copy(idx_hbm.at[wid], idx_v)

        @pl.loop(0, n_chunks)
        def _(j):
            pltpu.async_copy(src_hbm.at[idx_v.at[j]], rows_v, sem).wait()
            pltpu.sync_copy(rows_v, out_hbm.at[pl.ds((wid * n_chunks + j) * SC_ROWS, SC_ROWS)])

    return gather(src, idx.reshape(n_workers, n_chunks, SC_ROWS))


def _sc_scatter_rows(src, dest, n_out):
    n_workers = SC_CORES * SC_SUBCORES
    t, d = src.shape
    n_slot = dest.shape[0]
    assert t % (n_workers * SC_ROWS) == 0
    n_chunks = t // (n_workers * SC_ROWS)
    mesh = plsc.VectorSubcoreMesh(core_axis_name="c", subcore_axis_name="s")
    idx = dest.reshape(n_slot, n_workers, n_chunks, SC_ROWS).transpose(1, 2, 0, 3)
    idx = idx.reshape(n_workers, n_chunks * n_slot, SC_ROWS)

    @functools.partial(
        pl.kernel, mesh=mesh,
        out_type=jax.ShapeDtypeStruct((n_out, d), src.dtype),
        scratch_types=[pltpu.VMEM((n_chunks * n_slot, SC_ROWS), jnp.int32),
                       pltpu.VMEM((SC_ROWS, d), src.dtype)],
        name="sc_scatter")
    def scatter(src_hbm, idx_hbm, out_hbm, idx_v, rows_v):
        wid = lax.axis_index("s") * SC_CORES + lax.axis_index("c")
        pltpu.sync_copy(idx_hbm.at[wid], idx_v)

        @pl.loop(0, n_chunks)
        def _(j):
            pltpu.sync_copy(src_hbm.at[pl.ds((wid * n_chunks + j) * SC_ROWS, SC_ROWS)], rows_v)
            for k in range(n_slot):
                pltpu.sync_copy(rows_v, out_hbm.at[idx_v.at[j * n_slot + k]])

    return scatter(src, idx)


def _expert_kernel(be_ref, nv_ref, x_ref, wgu_ref, bgu_ref, wd_ref, bd_ref, o_ref):
    del be_ref
    valid = _iota((EXPERT_BLOCK, 1), 0) < nv_ref[pl.program_id(0)]
    lo, hi = _unpack_rows(jnp.where(valid, x_ref[...], jnp.uint32(0)))
    x = jnp.concatenate([lo.astype(BF16), hi.astype(BF16)], axis=1)
    half = EXPERT_BLOCK // 2
    gus = [jnp.dot(x[r * half:(r + 1) * half], wgu_ref[...], preferred_element_type=F32)
           + bgu_ref[...] for r in range(2)]
    for r, gu in enumerate(gus):
        gate = jnp.minimum(gu[:, :D_MODEL], SWIGLU_LIMIT)
        up = jnp.clip(gu[:, D_MODEL:], -SWIGLU_LIMIT, SWIGLU_LIMIT)
        act = gate * _sigmoid(SWIGLU_ALPHA * gate) * (up + 1.0)
        o_ref[r * half:(r + 1) * half, :] = _pack_rows(
            jnp.dot(act.astype(BF16), wd_ref[...], preferred_element_type=F32) + bd_ref[...])


def _experts(block_e, n_valid, xs, w_gu, b_gu, w_d, b_d):
    n_blocks = block_e.shape[0]
    grid_spec = pltpu.PrefetchScalarGridSpec(
        num_scalar_prefetch=2,
        grid=(n_blocks,),
        in_specs=[pl.BlockSpec((EXPERT_BLOCK, D_PACK), lambda j, be, nv: (j, 0)),
                  pl.BlockSpec((None, D_MODEL, 2 * D_MODEL), lambda j, be, nv: (be[j], 0, 0)),
                  pl.BlockSpec((None, 1, 2 * D_MODEL), lambda j, be, nv: (be[j], 0, 0)),
                  pl.BlockSpec((None, D_MODEL, D_MODEL), lambda j, be, nv: (be[j], 0, 0)),
                  pl.BlockSpec((None, 1, D_MODEL), lambda j, be, nv: (be[j], 0, 0))],
        out_specs=pl.BlockSpec((EXPERT_BLOCK, D_PACK), lambda j, be, nv: (j, 0)),
    )
    return pl.pallas_call(
        _expert_kernel,
        grid_spec=grid_spec,
        out_shape=jax.ShapeDtypeStruct(xs.shape, jnp.uint32),
        compiler_params=pltpu.CompilerParams(
            dimension_semantics=("arbitrary",), vmem_limit_bytes=VMEM_LIMIT),
        name="experts",
    )(block_e, n_valid, xs, w_gu, b_gu, w_d, b_d)


COMBINE_TOKENS = 1024
MOE_SPLITS = 2


def _combine_kernel(yg_ref, x1_ref, gate_ref, g2_ref, fg_ref, o_ref):
    gates = gate_ref[...].T
    acc_lo = acc_hi = None
    for kk in range(TOP_K):
        lo, hi = _unpack_rows(yg_ref[kk * COMBINE_TOKENS:(kk + 1) * COMBINE_TOKENS, :])
        g = gates[:, kk:kk + 1]
        acc_lo = g * lo if acc_lo is None else acc_lo + g * lo
        acc_hi = g * hi if acc_hi is None else acc_hi + g * hi
    x2 = x1_ref[...] + g2_ref[...] * jnp.concatenate([acc_lo, acc_hi], axis=1)
    o_ref[...] = x2 * lax.rsqrt(jnp.mean(x2 * x2, axis=-1, keepdims=True) + NORM_EPS) * fg_ref[...]


def _combine_kernel_into(prev_ref, *refs):
    del prev_ref
    _combine_kernel(*refs)


def _combine(yg, x1, gates, gate2, final_g, seq, row0, t_total, prev):
    t = x1.shape[0]
    tm = COMBINE_TOKENS
    per_b = seq // tm
    blk0 = row0 // tm
    rows = lambda i: (i, 0)
    in_specs = [pl.BlockSpec((TOP_K * tm, D_PACK), rows),
                pl.BlockSpec((tm, D_MODEL), rows),
                pl.BlockSpec((8, tm), lambda i: (0, i)),
                pl.BlockSpec((None, 1, D_MODEL), lambda i: ((i + blk0) // per_b, 0, 0)),
                pl.BlockSpec((1, D_MODEL), lambda i: (0, 0))]
    args = (yg, x1, gates, gate2, final_g)
    if prev is not None:
        in_specs = [pl.BlockSpec(memory_space=pl.ANY)] + in_specs
        args = (prev,) + args
    return pl.pallas_call(
        _combine_kernel if prev is None else _combine_kernel_into,
        grid=(t // tm,),
        in_specs=in_specs,
        out_specs=pl.BlockSpec((tm, D_MODEL), lambda i: (i + blk0, 0)),
        out_shape=jax.ShapeDtypeStruct((t_total, D_MODEL), F32),
        input_output_aliases={} if prev is None else {0: 0},
        compiler_params=pltpu.CompilerParams(
            dimension_semantics=("parallel",), vmem_limit_bytes=VMEM_LIMIT),
        name="combine",
    )(*args)


def _moe(h2, idx, gates, rank, counts, x1, gate2, final_g, w_gu, b_gu, w_d, b_d, seq,
         row0, t_total, prev):
    t = h2.shape[0]
    n_slots = t * TOP_K
    n_blocks = -(-n_slots // EXPERT_BLOCK) + N_EXPERTS
    cap = n_blocks * EXPERT_BLOCK
    padded = (counts + EXPERT_BLOCK - 1) // EXPERT_BLOCK * EXPERT_BLOCK
    pad_ends = jnp.cumsum(padded)
    pad_starts = pad_ends - padded
    experts = jnp.arange(N_EXPERTS, dtype=jnp.int32)
    dest = jnp.sum(jnp.where(idx[..., None] == experts, pad_starts, 0), axis=-1) + rank
    block_starts = jnp.arange(n_blocks, dtype=jnp.int32) * EXPERT_BLOCK
    block_e = jnp.minimum(jnp.sum(block_starts[:, None] >= pad_ends[None, :], axis=1),
                          N_EXPERTS - 1).astype(jnp.int32)
    n_valid = jnp.clip(counts[block_e] - (block_starts - pad_starts[block_e]), 0, EXPERT_BLOCK)

    xs = _sc_scatter_rows(h2, dest, cap)
    yb = _experts(block_e, n_valid.astype(jnp.int32), xs, w_gu, b_gu, w_d, b_d)
    dest_blocks = dest.reshape(TOP_K, -1, COMBINE_TOKENS).transpose(1, 0, 2).reshape(-1)
    yg = _sc_gather_rows(dest_blocks, yb)
    return _combine(yg, x1, gates, gate2, final_g, seq, row0, t_total, prev)


def _layer(x, c_mod, norm1_g, w_in, mu_shift, w0, w2, a0, a2, g2, k_k, k_a, r_k, gn_w, gn_b, b_f,
           q_norm_g, k_norm_g, o_norm_g, w_out, norm2_g, w_router, b_router, w_gate_up,
           b_gate_up, w_down, b_down, final_g, tm_in, tm_out):
    bsz, seq, _ = x.shape
    shift1, scale1, gate1, shift2, scale2, gate2 = (
        m.reshape(bsz, 1, D_MODEL) for m in jnp.split(c_mod, 6, axis=-1))
    row = lambda v: v.reshape(1, -1)

    w_r = w_in[:, :RWKV_COLS].astype(BF16)
    w_x = w_in[:, RWKV_COLS:RWKV_COLS + FOX_MAIN].astype(BF16)
    w_f = w_in[:, RWKV_COLS + FOX_MAIN:].T
    b_fp = jnp.pad(b_f, (0, LANES - N_HEADS)).reshape(1, LANES)
    qk_gain = jnp.concatenate([jnp.tile(q_norm_g, N_HEADS) * (HEAD_DIM ** -0.5 * LOG2_E),
                               jnp.tile(k_norm_g, N_HEADS)]).reshape(1, -1)
    p_r, p_x, k_bias, q_bias, v_t = _inproj(x, shift1, scale1, row(norm1_g), w_r, w_x, w_f, b_fp,
                                            qk_gain, tm_in)

    zeros = jnp.zeros((LANES - 64, D_GRP), F32)
    w2p = jnp.concatenate([w2, zeros], axis=0).astype(BF16)
    a2p = jnp.concatenate([zeros, a2], axis=0).astype(BF16)
    y_r, w_gu, w_d = _rwkv(p_r, row(mu_shift), row(w0), w2p, row(a0), a2p, g2.astype(BF16),
                           row(k_k), row(k_a), row(r_k), row(gn_w), row(gn_b), w_gate_up, w_down)

    y_f = _fox(p_x, k_bias, q_bias, v_t, jnp.tile(o_norm_g, 2).reshape(1, LANES))

    t = bsz * seq
    w_rt = jnp.pad(w_router.T, ((0, LANES - N_EXPERTS), (0, 0)))
    b_rt = b_router.reshape(N_EXPERTS, 1)
    wo = w_out.astype(BF16)
    b_gu, b_d = b_gate_up.reshape(N_EXPERTS, 1, -1), b_down.reshape(N_EXPERTS, 1, -1)
    first = t * 3 // 8
    out = None
    for row0, t_part in ((0, first), (first, t - first)):
        x1, h2, idx, gates, rank, cnt = _outproj(
            x.reshape(t, D_MODEL), y_r.reshape(t, D_GRP), y_f.reshape(t, D_GRP), gate1, shift2,
            scale2, row(norm2_g), wo[:D_GRP], wo[D_GRP:], w_rt, b_rt, tm_out, seq, row0, t_part)
        counts = cnt[:, 0].astype(jnp.int32)
        out = _moe(h2, idx[:TOP_K], gates, rank[:TOP_K], counts, x1, gate2, row(final_g),
                   w_gu, b_gu, w_d, b_d, seq, row0, t, out)
    return out.reshape(bsz, seq, D_MODEL)


def kernel(x, c, w_ada, b_ada, norm1_g, w_in, mu_shift, w0, w2, a0, a2, g2, k_k, k_a, r_k, gn_w,
           gn_b, b_f, q_norm_g, k_norm_g, o_norm_g, w_out, norm2_g, w_router, b_router, w_gate_up,
           b_gate_up, w_down, b_down, final_g):
    assert w_ada.shape[0] == 1, "single-layer block"
    c_mod = _adaln(c, w_ada[0], b_ada[0])
    return _layer(x, c_mod, norm1_g[0], w_in[0], mu_shift[0], w0[0], w2[0], a0[0], a2[0], g2[0],
                  k_k[0], k_a[0], r_k[0], gn_w[0], gn_b[0], b_f[0], q_norm_g[0], k_norm_g[0],
                  o_norm_g[0], w_out[0], norm2_g[0], w_router[0], b_router[0], w_gate_up[0],
                  b_gate_up[0], w_down[0], b_down[0], final_g,
                  tm_in=min(512, x.shape[1]), tm_out=min(1024, x.shape[1]))
```

```python
import functools

import jax
import jax.numpy as jnp
from jax import lax
from jax.experimental import pallas as pl
from jax.experimental.pallas import tpu as pltpu
from jax.experimental.pallas import tpu_sc as plsc

F32 = jnp.float32
BF16 = jnp.bfloat16
HIGHEST = lax.Precision.HIGHEST

D_MODEL = 1024
HEAD_DIM = 64
N_HEADS = 8
D_GRP = N_HEADS * HEAD_DIM
RWKV_COLS = 1792
LORA_OFF = 3 * D_GRP
GATE_OFF = LORA_OFF + 128
FOX_MAIN = 4 * D_GRP
N_EXPERTS = 32
TOP_K = 4
EXPERT_BLOCK = 512
SWIGLU_ALPHA = 1.702
SWIGLU_LIMIT = 7.0
NORM_EPS = 1e-6
GN_EPS = 64e-5
LOG2_E = 1.4426950408889634
LANES = 128
CHUNK = 64
FOX_SUB_KEYS = 512
HEADS_PER_SCAN = 4
SCAN_W = HEADS_PER_SCAN * HEAD_DIM
SEG_TERMS = 1
CUM_TERMS = 2
VMEM_LIMIT = 56 * 1024 * 1024


def _dot(a, b):
    return jnp.dot(a.astype(BF16), b.astype(BF16), preferred_element_type=F32)


def _fdot(a, b):
    return jnp.dot(a, b, precision=HIGHEST, preferred_element_type=F32)


def _split_dot(x, m, terms):
    acc = None
    rem = x
    for _ in range(terms):
        part = rem.astype(BF16)
        rem = rem - part.astype(F32)
        d = jnp.dot(part, m, preferred_element_type=F32)
        acc = d if acc is None else acc + d
    return acc


def _tri_dot(m, x, terms):
    acc = None
    rem = x
    for _ in range(terms):
        part = rem.astype(BF16)
        rem = rem - part.astype(F32)
        d = jnp.dot(m, part, preferred_element_type=F32)
        acc = d if acc is None else acc + d
    return acc


def _iota(shape, dim):
    return lax.broadcasted_iota(jnp.int32, shape, dim)


def _seg_reduce_mat(n):
    return (_iota((n, LANES), 0) // HEAD_DIM == _iota((n, LANES), 1)).astype(BF16)


def _seg_expand_mat(n):
    return (_iota((LANES, n), 1) // HEAD_DIM == _iota((LANES, n), 0)).astype(BF16)


D_PACK = D_MODEL // 2


def _pack_rows(x):
    lo = lax.bitcast_convert_type(x[:, :D_PACK].astype(BF16).astype(F32), jnp.uint32)
    hi = lax.bitcast_convert_type(x[:, D_PACK:].astype(BF16).astype(F32), jnp.uint32)
    return hi | (lo >> 16)


def _unpack_rows(p):
    lo = lax.bitcast_convert_type(p << 16, F32)
    hi = lax.bitcast_convert_type(p & jnp.uint32(0xFFFF0000), F32)
    return lo, hi


def _log_sigmoid(z):
    return jnp.minimum(z, 0.0) - jnp.log(1.0 + jnp.exp(-jnp.abs(z)))


def _sigmoid(z):
    return 1.0 / (1.0 + jnp.exp(-z))


def _adaln_kernel(c_ref, w_ref, b_ref, o_ref):
    c = c_ref[...]
    o_ref[...] = _fdot(c * _sigmoid(c), w_ref[...]) + b_ref[...]


def _adaln(c, w_ada, b_ada):
    bsz = c.shape[0]
    n_mod = w_ada.shape[1] // D_MODEL
    return pl.pallas_call(
        _adaln_kernel,
        grid=(n_mod,),
        in_specs=[pl.BlockSpec((bsz, D_MODEL), lambda j: (0, 0)),
                  pl.BlockSpec((D_MODEL, D_MODEL), lambda j: (0, j)),
                  pl.BlockSpec((1, D_MODEL), lambda j: (0, j))],
        out_specs=pl.BlockSpec((bsz, D_MODEL), lambda j: (0, j)),
        out_shape=jax.ShapeDtypeStruct((bsz, n_mod * D_MODEL), F32),
        name="adaln",
    )(c, w_ada, b_ada.reshape(1, -1))


def _inproj_kernel(x_ref, sh_ref, sc_ref, g_ref, wr_ref, wx_ref, wft_ref, bf_ref, qkg_ref,
                   pr_ref, px_ref, kb_ref, qb_ref, vt_ref, carry_ref):
    @pl.when(pl.program_id(1) == 0)
    def _():
        carry_ref[...] = jnp.zeros_like(carry_ref)

    x = x_ref[...]
    tm = x.shape[0]
    h = x * lax.rsqrt(jnp.mean(x * x, axis=-1, keepdims=True) + NORM_EPS) * g_ref[...]
    h = h * (1.0 + sc_ref[...]) + sh_ref[...]
    hb = h.astype(BF16)

    pr_ref[...] = jnp.dot(hb, wr_ref[...], preferred_element_type=F32).astype(BF16)

    px = jnp.dot(hb, wx_ref[...], preferred_element_type=F32)
    qk = px[:, :2 * D_GRP]
    ss = _split_dot(qk * qk, _seg_reduce_mat(2 * D_GRP), SEG_TERMS)
    inv = lax.rsqrt(ss * (1.0 / HEAD_DIM) + NORM_EPS)
    qk = qk * _split_dot(inv, _seg_expand_mat(2 * D_GRP), SEG_TERMS) * qkg_ref[...]
    px_ref[:, :2 * D_GRP] = qk.astype(BF16)
    px_ref[:, 2 * D_GRP:] = px[:, 2 * D_GRP:].astype(BF16)
    vt_ref[...] = px[:, 2 * D_GRP:3 * D_GRP].T.astype(BF16)

    lane = _iota((1, LANES), 1)
    z = jnp.zeros((tm, LANES), F32)
    for hd in range(N_HEADS):
        zh = jnp.sum(h * wft_ref[hd:hd + 1, :], axis=-1, keepdims=True)
        z = jnp.where(lane == hd, zh, z)
    cum = _log_sigmoid(z + bf_ref[...])
    row_id = _iota((tm, 1), 0)
    shift = 1
    while shift < tm:
        cum = cum + jnp.where(row_id >= shift, pltpu.roll(cum, shift, axis=0), 0.0)
        shift *= 2
    cum = cum + carry_ref[...]
    carry_ref[...] = cum[tm - 1:tm, :]

    parts = []
    rem = cum * LOG2_E
    for _ in range(3):
        part = rem.astype(BF16)
        rem = rem - part.astype(F32)
        parts.append(part)
    src, dst = _iota((LANES, LANES), 0), _iota((LANES, LANES), 1)

    def spread(offset):
        return sum(jnp.dot(part, ((dst == 8 * src + offset + t) & (src < N_HEADS)).astype(BF16),
                           preferred_element_type=F32) for t, part in enumerate(parts))

    slot = _iota((1, LANES), 1) % 8
    kb_ref[...] = (jnp.where((slot >= 3) & (slot < 6), 1.0, 0.0) - spread(0)).astype(BF16)
    qb_ref[...] = (jnp.where(slot < 3, 1.0, 0.0) + spread(3)).astype(BF16)


def _inproj(x, shift, scale, g, w_r, w_x, w_f_t, b_f, qk_gain, tm):
    bsz, seq, _ = x.shape
    const = lambda b, s: (0, 0)
    return pl.pallas_call(
        _inproj_kernel,
        grid=(bsz, seq // tm),
        in_specs=[pl.BlockSpec((None, tm, D_MODEL), lambda b, s: (b, s, 0)),
                  pl.BlockSpec((None, 1, D_MODEL), lambda b, s: (b, 0, 0)),
                  pl.BlockSpec((None, 1, D_MODEL), lambda b, s: (b, 0, 0)),
                  pl.BlockSpec((1, D_MODEL), const),
                  pl.BlockSpec((D_MODEL, RWKV_COLS), const),
                  pl.BlockSpec((D_MODEL, FOX_MAIN), const),
                  pl.BlockSpec((N_HEADS, D_MODEL), const),
                  pl.BlockSpec((1, LANES), const),
                  pl.BlockSpec((1, 2 * D_GRP), const)],
        out_specs=[pl.BlockSpec((None, tm, RWKV_COLS), lambda b, s: (b, s, 0)),
                   pl.BlockSpec((None, tm, FOX_MAIN), lambda b, s: (b, s, 0)),
                   pl.BlockSpec((None, tm, LANES), lambda b, s: (b, s, 0)),
                   pl.BlockSpec((None, tm, LANES), lambda b, s: (b, s, 0)),
                   pl.BlockSpec((None, D_GRP, tm), lambda b, s: (b, 0, s))],
        out_shape=[jax.ShapeDtypeStruct((bsz, seq, RWKV_COLS), BF16),
                   jax.ShapeDtypeStruct((bsz, seq, FOX_MAIN), BF16),
                   jax.ShapeDtypeStruct((bsz, seq, LANES), BF16),
                   jax.ShapeDtypeStruct((bsz, seq, LANES), BF16),
                   jax.ShapeDtypeStruct((bsz, D_GRP, seq), BF16)],
        scratch_shapes=[pltpu.VMEM((1, LANES), F32)],
        compiler_params=pltpu.CompilerParams(
            dimension_semantics=("parallel", "arbitrary"), vmem_limit_bytes=VMEM_LIMIT),
        name="inproj",
    )(x, shift, scale, g, w_r, w_x, w_f_t, b_f, qk_gain)


_NN = (((1,), (0,)), ((), ()))
_NT = (((1,), (1,)), ((), ()))
_TN = (((0,), (0,)), ((), ()))
SCAN_N = HEADS_PER_SCAN * CHUNK
BATCH_PER_STEP = 8
INV_LEVELS = 5
M_HEAD, M_STRICT, M_INCL, M_EYE, M_BASE, M_OFF = 0, 1, 2, 3, 4, 5


def _bdot(a, b, dims):
    return lax.dot_general(a, b, dims, preferred_element_type=F32)


def _scan_masks():
    rr, cc = _iota((SCAN_N, SCAN_W), 0), _iota((SCAN_N, SCAN_W), 1)
    ri, ci = _iota((SCAN_N, SCAN_N), 0), _iota((SCAN_N, SCAN_N), 1)
    same = ri // CHUNK == ci // CHUNK
    masks = [rr // CHUNK == cc // HEAD_DIM, same & (ri > ci), same & (ri >= ci), ri == ci,
             (ri // 2 == ci // 2) & (ri > ci)]
    blk = 2
    while blk < CHUNK:
        masks.append((ri // (2 * blk) == ci // (2 * blk)) & (ri // blk != ci // blk) & (ri > ci))
        blk *= 2
    return jnp.stack(masks).astype(BF16)


def _rwkv_kernel(p_ref, masks_ref, mu_ref, w0_ref, w2_ref, a0_ref, a2_ref, g2_ref, kk_ref, ka_ref,
                 rk_ref, gnw_ref, gnb_ref, wgu_ref, wd_ref, o_ref, wgu_bf_ref, wd_bf_ref,
                 last_ref, state_ref):
    wgu_bf_ref[...] = wgu_ref[...].astype(BF16)
    wd_bf_ref[...] = wd_ref[...].astype(BF16)

    @pl.when(pl.program_id(1) == 0)
    def _():
        last_ref[...] = jnp.zeros_like(last_ref)
        state_ref[...] = jnp.zeros_like(state_ref)

    mu, w0, w2, a0, a2, g2, k_k, k_a, r_k, gn_w, gn_b = (
        ref[...] for ref in (mu_ref, w0_ref, w2_ref, a0_ref, a2_ref, g2_ref, kk_ref, ka_ref,
                             rk_ref, gnw_ref, gnb_ref))
    rows = BATCH_PER_STEP * CHUNK
    p = p_ref[...].astype(F32).reshape(rows, RWKV_COLS)
    row_id = _iota((rows, 1), 0)
    prev = pltpu.roll(p, 1, axis=0)
    for bb in range(BATCH_PER_STEP):
        prev = jnp.where(row_id == bb * CHUNK, last_ref[bb], prev)
        last_ref[bb] = p[(bb + 1) * CHUNK - 1:(bb + 1) * CHUNK, :]
    pf = p + mu * (prev - p)
    r = pf[:, 0:D_GRP]
    k = pf[:, D_GRP:2 * D_GRP]
    v = pf[:, 2 * D_GRP:3 * D_GRP]
    lora = pf[:, LORA_OFF:GATE_OFF]
    gd = pf[:, GATE_OFF:RWKV_COLS]

    wlog = w0 + _dot(jnp.tanh(lora), w2)
    neg = -wlog
    softplus = jnp.maximum(neg, 0.0) + jnp.log(1.0 + jnp.exp(-jnp.abs(neg)))
    ld = -jnp.exp(-softplus - 0.5)
    a = _sigmoid(a0 + _dot(lora, a2))
    g = _dot(_sigmoid(gd), g2)

    red, exp_m = _seg_reduce_mat(D_GRP), _seg_expand_mat(D_GRP)
    kk = k * k_k
    n2 = _split_dot(kk * kk, red, SEG_TERMS)
    kk = kk * _split_dot(1.0 / jnp.maximum(jnp.sqrt(n2), 1e-12), exp_m, SEG_TERMS)
    k2 = k * (1.0 + (a - 1.0) * k_a)

    tr, tc = _iota((rows, rows), 0), _iota((rows, rows), 1)
    tri = ((tr >= tc) & (tr // CHUNK == tc // CHUNK)).astype(BF16)
    cl = _tri_dot(tri, ld, CUM_TERMS)
    cl_end = jnp.concatenate(
        [jnp.broadcast_to(cl[(bb + 1) * CHUNK - 1:(bb + 1) * CHUNK, :], (CHUNK, D_GRP))
         for bb in range(BATCH_PER_STEP)], axis=0)
    e_in = jnp.exp(cl)
    e_out = jnp.exp(-cl)
    e_rem = jnp.exp(cl_end - cl)
    p_end = jnp.exp(cl_end)
    kka = kk * a
    ops = [(-kk * jnp.exp(cl - ld)).astype(BF16), (kka * e_out).astype(BF16),
           (k2 * e_out).astype(BF16), (r * e_in).astype(BF16), v.astype(BF16),
           (kka * e_rem).astype(BF16), (k2 * e_rem).astype(BF16)]

    chains = [(bb, grp) for bb in range(BATCH_PER_STEP)
              for grp in range(N_HEADS // HEADS_PER_SCAN)]
    head_mask = masks_ref[M_HEAD]
    strict, incl = masks_ref[M_STRICT], masks_ref[M_INCL]

    def stacked(op, bb, grp):
        part = op[bb * CHUNK:(bb + 1) * CHUNK, grp * SCAN_W:(grp + 1) * SCAN_W]
        return jnp.concatenate([part] * HEADS_PER_SCAN, axis=0) * head_mask

    xs = [[stacked(op, bb, grp) for op in ops] for bb, grp in chains]
    st = [state_ref[bb, grp] for bb, grp in chains]
    sb = [s.astype(BF16) for s in st]
    nab = [_bdot(x[0], x[1], _NT).astype(BF16) for x in xs]
    aak = [_bdot(x[0], x[2], _NT).astype(BF16) * strict for x in xs]
    arb = [_bdot(x[3], x[1], _NT).astype(BF16) * incl for x in xs]
    ark = [_bdot(x[3], x[2], _NT).astype(BF16) * incl for x in xs]
    t_inv = [masks_ref[M_EYE] + n * masks_ref[M_BASE] for n in nab]
    for lvl in range(INV_LEVELS):
        half = [_bdot(t, n * masks_ref[M_OFF + lvl], _NN).astype(BF16) for t, n in zip(t_inv, nab)]
        t_inv = [t + _bdot(h, t, _NN).astype(BF16) for t, h in zip(t_inv, half)]
    rhs = [(_bdot(x[0], s, _NT) + _bdot(k, x[4], _NN)).astype(BF16)
           for x, s, k in zip(xs, sb, aak)]
    sa = [_bdot(t, h, _NN).astype(BF16) for t, h in zip(t_inv, rhs)]
    ys = [_bdot(x[3], s, _NT) + _bdot(b, u, _NN) + _bdot(k, x[4], _NN)
          for x, s, b, u, k in zip(xs, sb, arb, sa, ark)]
    for (bb, grp), x, s, u in zip(chains, xs, st, sa):
        decay = p_end[bb * CHUNK:bb * CHUNK + 1, grp * SCAN_W:(grp + 1) * SCAN_W]
        state_ref[bb, grp] = s * decay + _bdot(u, x[5], _TN) + _bdot(x[4], x[6], _TN)
    ys = [y[0:CHUNK] + y[CHUNK:2 * CHUNK] + y[2 * CHUNK:3 * CHUNK] + y[3 * CHUNK:4 * CHUNK]
          for y in ys]
    n_grp = N_HEADS // HEADS_PER_SCAN
    y = jnp.concatenate([jnp.concatenate(ys[bb * n_grp:(bb + 1) * n_grp], axis=1)
                         for bb in range(BATCH_PER_STEP)], axis=0)

    mean = _split_dot(_split_dot(y, red, SEG_TERMS) * (1.0 / HEAD_DIM), exp_m, SEG_TERMS)
    d = y - mean
    var = _split_dot(d * d, red, SEG_TERMS) * (1.0 / HEAD_DIM)
    yn = d * _split_dot(lax.rsqrt(var + GN_EPS), exp_m, SEG_TERMS) * gn_w + gn_b
    bonus = _split_dot(_split_dot(r * k2 * r_k, red, SEG_TERMS), exp_m, SEG_TERMS) * v
    o_ref[...] = ((yn + bonus) * g).astype(BF16).reshape(BATCH_PER_STEP, CHUNK, D_GRP)


def _rwkv(p_r, mu, w0, w2p, a0, a2p, g2, k_k, k_a, r_k, gn_w, gn_b, w_gate_up, w_down):
    bsz, seq, _ = p_r.shape
    assert bsz % BATCH_PER_STEP == 0
    n_chunk = seq // CHUNK
    n_step = (bsz // BATCH_PER_STEP) * n_chunk
    wgu2d = w_gate_up.reshape(-1, w_gate_up.shape[-1])
    wd2d = w_down.reshape(-1, w_down.shape[-1])
    assert wgu2d.shape[0] % (8 * n_step) == 0 and wd2d.shape[0] == wgu2d.shape[0]
    slab = wgu2d.shape[0] // n_step
    masks = _scan_masks()
    const = lambda b, s: (0, 0)
    step = lambda b, s: (b * n_chunk + s, 0)
    vec = pl.BlockSpec((1, D_GRP), const)
    y, wgu_bf, wd_bf = pl.pallas_call(
        _rwkv_kernel,
        grid=(bsz // BATCH_PER_STEP, n_chunk),
        in_specs=[pl.BlockSpec((BATCH_PER_STEP, CHUNK, RWKV_COLS), lambda b, s: (b, s, 0)),
                  pl.BlockSpec(masks.shape, lambda b, s: (0, 0, 0)),
                  pl.BlockSpec((1, RWKV_COLS), const),
                  vec, pl.BlockSpec((LANES, D_GRP), const),
                  vec, pl.BlockSpec((LANES, D_GRP), const),
                  pl.BlockSpec((LANES, D_GRP), const),
                  vec, vec, vec, vec, vec,
                  pl.BlockSpec((slab, wgu2d.shape[1]), step),
                  pl.BlockSpec((slab, wd2d.shape[1]), step)],
        out_specs=[pl.BlockSpec((BATCH_PER_STEP, CHUNK, D_GRP), lambda b, s: (b, s, 0)),
                   pl.BlockSpec((slab, wgu2d.shape[1]), step),
                   pl.BlockSpec((slab, wd2d.shape[1]), step)],
        out_shape=[jax.ShapeDtypeStruct((bsz, seq, D_GRP), BF16),
                   jax.ShapeDtypeStruct(wgu2d.shape, BF16),
                   jax.ShapeDtypeStruct(wd2d.shape, BF16)],
        scratch_shapes=[pltpu.VMEM((BATCH_PER_STEP, 1, RWKV_COLS), F32),
                        pltpu.VMEM((BATCH_PER_STEP, N_HEADS // HEADS_PER_SCAN, SCAN_W, SCAN_W), F32)],
        compiler_params=pltpu.CompilerParams(
            dimension_semantics=("parallel", "arbitrary"), vmem_limit_bytes=VMEM_LIMIT),
        name="rwkv",
    )(p_r, masks, mu, w0, w2p, a0, a2p, g2, k_k, k_a, r_k, gn_w, gn_b, wgu2d, wd2d)
    return y, wgu_bf.reshape(w_gate_up.shape), wd_bf.reshape(w_down.shape)


def _fox_kernel(q_ref, qb_ref, k_ref, kb_ref, vt_ref, og_ref, ong_ref, o_ref, m_ref, l_ref, acc_ref,
                *, seq):
    hp = pl.program_id(1)
    lane = _iota((1, LANES), 1)
    q = q_ref[...]
    qb = qb_ref[...]
    zero = jnp.zeros_like(q)
    qcat = [jnp.concatenate([jnp.where(lane // HEAD_DIM == hh, q, zero),
                             jnp.where(lane // 8 == hp * 2 + hh, qb, zero)], axis=1)
            for hh in range(2)]
    keys = min(FOX_SUB_KEYS, seq)
    n_sub = seq // keys
    half = keys // 2
    mask_a = _iota((half, half), 1) >= _iota((half, half), 0)
    mask_b = _iota((keys, half), 1) + half >= _iota((keys, half), 0)

    m_ref[...] = jnp.full(m_ref.shape, -jnp.inf, F32)
    l_ref[...] = jnp.zeros(l_ref.shape, F32)
    acc_ref[...] = jnp.zeros(acc_ref.shape, F32)

    def plan(s):
        lo = s * keys
        pieces = [(half, slice(lo, lo + half), mask_a), (keys, slice(lo + half, lo + keys), mask_b)]
        if lo + keys < seq:
            pieces.append((keys, slice(lo + keys, seq), None))
        return pieces

    def scores(s):
        lo = s * keys
        kcat = jnp.concatenate([k_ref[lo:lo + keys, :], kb_ref[lo:lo + keys, :]], axis=1)
        return [lax.dot_general(kcat[:nk], qc[qs, :], _NT, preferred_element_type=F32)
                for qc in qcat for nk, qs, _ in plan(s)]

    pending = scores(0)
    for s in range(n_sub):
        lo = s * keys
        nxt = scores(s + 1) if s + 1 < n_sub else None
        n_piece = len(plan(s))
        pieces = []
        for hh in range(2):
            for (nk, qs, mask), st in zip(plan(s), pending[hh * n_piece:(hh + 1) * n_piece]):
                pieces.append((hh, nk, qs, st if mask is None else jnp.where(mask, st, -jnp.inf)))
        m_old = [m_ref[hh, :, qs] for hh, _, qs, _ in pieces]
        m_new = [jnp.maximum(m, jnp.max(st, axis=0, keepdims=True))
                 for m, (_, _, _, st) in zip(m_old, pieces)]
        pts = [jnp.exp2(st - m) for (_, _, _, st), m in zip(pieces, m_new)]
        pvs = [jnp.dot(vt_ref[:, lo:lo + nk], pt.astype(BF16), preferred_element_type=F32)
               for (_, nk, _, _), pt in zip(pieces, pts)]
        for (hh, _, qs, _), mo, mn, pt, pv in zip(pieces, m_old, m_new, pts, pvs):
            alpha = jnp.exp2(mo - mn)
            m_ref[hh, :, qs] = mn
            l_ref[hh, :, qs] = alpha * l_ref[hh, :, qs] + jnp.sum(pt, axis=0, keepdims=True)
            acc_ref[hh, :, qs] = (alpha * acc_ref[hh, :, qs]
                                  + pv[hh * HEAD_DIM:(hh + 1) * HEAD_DIM, :])
        pending = nxt

    outs = []
    for hh in range(2):
        o = acc_ref[hh] / l_ref[hh]
        outs.append(o * lax.rsqrt(jnp.mean(o * o, axis=0, keepdims=True) + NORM_EPS))
    o = jnp.concatenate(outs, axis=0).T
    o_ref[...] = (o * ong_ref[...] * _sigmoid(og_ref[...].astype(F32))).astype(BF16)


def _fox(p_x, k_bias, q_bias, v_t, o_gain):
    bsz, seq, _ = p_x.shape
    npair = N_HEADS // 2
    return pl.pallas_call(
        functools.partial(_fox_kernel, seq=seq),
        grid=(bsz, npair),
        in_specs=[pl.BlockSpec((None, seq, LANES), lambda b, h: (b, 0, h)),
                  pl.BlockSpec((None, seq, LANES), lambda b, h: (b, 0, 0)),
                  pl.BlockSpec((None, seq, LANES), lambda b, h: (b, 0, npair + h)),
                  pl.BlockSpec((None, seq, LANES), lambda b, h: (b, 0, 0)),
                  pl.BlockSpec((None, LANES, seq), lambda b, h: (b, h, 0)),
                  pl.BlockSpec((None, seq, LANES), lambda b, h: (b, 0, 3 * npair + h)),
                  pl.BlockSpec((1, LANES), lambda b, h: (0, 0))],
        out_specs=pl.BlockSpec((None, seq, LANES), lambda b, h: (b, 0, h)),
        out_shape=jax.ShapeDtypeStruct((bsz, seq, D_GRP), BF16),
        scratch_shapes=[pltpu.VMEM((2, 1, seq), F32), pltpu.VMEM((2, 1, seq), F32),
                        pltpu.VMEM((2, HEAD_DIM, seq), F32)],
        compiler_params=pltpu.CompilerParams(
            dimension_semantics=("parallel", "parallel"), vmem_limit_bytes=VMEM_LIMIT),
        name="fox",
    )(p_x, q_bias, p_x, k_bias, v_t, p_x, o_gain)


def _outproj_kernel(x_ref, yr_ref, yf_ref, g1_ref, sh_ref, sc_ref, ng_ref, wor_ref, wof_ref,
                    wrt_ref, wrl_ref, brt_ref, x1_ref, h2_ref, idx_ref, gate_ref, rank_ref, cnt_ref,
                    carry_ref):
    @pl.when(pl.program_id(0) == 0)
    def _():
        carry_ref[...] = jnp.zeros_like(carry_ref)

    y = (jnp.dot(yr_ref[...], wor_ref[...], preferred_element_type=F32)
         + jnp.dot(yf_ref[...], wof_ref[...], preferred_element_type=F32))
    x1 = x_ref[...] + g1_ref[...] * y
    x1_ref[...] = x1
    tm = x1.shape[0]
    h = x1 * lax.rsqrt(jnp.mean(x1 * x1, axis=-1, keepdims=True) + NORM_EPS) * ng_ref[...]
    h2 = h * (1.0 + sc_ref[...]) + sh_ref[...]
    h2_ref[...] = _pack_rows(h2)

    h_hi = h2.astype(BF16)
    h_lo = (h2 - h_hi.astype(F32)).astype(BF16)
    logits = (lax.dot_general(wrt_ref[...], h_hi, _NT, preferred_element_type=F32)
              + lax.dot_general(wrt_ref[...], h_lo, _NT, preferred_element_type=F32)
              + lax.dot_general(wrl_ref[...], h_hi, _NT, preferred_element_type=F32))
    lg = logits[:N_EXPERTS, :] + brt_ref[...]
    expert = _iota((N_EXPERTS, tm), 0)
    picks = []
    hot_sum = jnp.zeros((N_EXPERTS, tm), F32)
    for _ in range(TOP_K):
        m = jnp.max(lg, axis=0, keepdims=True)
        sel = jnp.min(jnp.where(lg == m, expert, N_EXPERTS), axis=0, keepdims=True)
        hot = expert == sel
        picks.append((m, sel, hot))
        hot_sum = hot_sum + hot.astype(F32)
        lg = jnp.where(hot, -jnp.inf, lg)
    es = [jnp.exp(m - picks[0][0]) for m, _, _ in picks]
    den = es[0] + es[1] + es[2] + es[3]

    earlier = (_iota((tm, tm), 0) < _iota((tm, tm), 1)).astype(BF16)
    before = jnp.dot(hot_sum.astype(BF16), earlier, preferred_element_type=F32) + carry_ref[...]
    ranks = [jnp.sum(jnp.where(hot, before, 0.0), axis=0, keepdims=True).astype(jnp.int32)
             for _, _, hot in picks]
    pad_i = jnp.zeros((8 - TOP_K, tm), jnp.int32)
    idx_ref[...] = jnp.concatenate([sel for _, sel, _ in picks] + [pad_i], axis=0)
    gate_ref[...] = jnp.concatenate([e / den for e in es] + [pad_i.astype(F32)], axis=0)
    rank_ref[...] = jnp.concatenate(ranks + [pad_i], axis=0)
    carry_ref[...] = carry_ref[...] + jnp.sum(hot_sum, axis=1, keepdims=True)
    cnt_ref[...] = jnp.broadcast_to(carry_ref[...], cnt_ref.shape)


def _outproj(x2d, y_r, y_f, gate1, shift2, scale2, norm_g, wo_r, wo_f, w_rt, b_rt, tm, seq,
             row0, t):
    w_rt_hi = w_rt.astype(BF16)
    w_rt_lo = (w_rt - w_rt_hi.astype(F32)).astype(BF16)
    per_b = seq // tm
    blk0 = row0 // tm
    const = lambda i: (0, 0)
    rows = lambda i: (i, 0)
    rows_in = lambda i: (i + blk0, 0)
    mod = pl.BlockSpec((None, 1, D_MODEL), lambda i: ((i + blk0) // per_b, 0, 0))
    return pl.pallas_call(
        _outproj_kernel,
        grid=(t // tm,),
        in_specs=[pl.BlockSpec((tm, D_MODEL), rows_in),
                  pl.BlockSpec((tm, D_GRP), rows_in),
                  pl.BlockSpec((tm, D_GRP), rows_in),
                  mod, mod, mod,
                  pl.BlockSpec((1, D_MODEL), const),
                  pl.BlockSpec((D_GRP, D_MODEL), const),
                  pl.BlockSpec((D_GRP, D_MODEL), const),
                  pl.BlockSpec((LANES, D_MODEL), const),
                  pl.BlockSpec((LANES, D_MODEL), const),
                  pl.BlockSpec((N_EXPERTS, 1), const)],
        out_specs=[pl.BlockSpec((tm, D_MODEL), rows),
                   pl.BlockSpec((tm, D_PACK), rows),
                   pl.BlockSpec((8, tm), lambda i: (0, i)),
                   pl.BlockSpec((8, tm), lambda i: (0, i)),
                   pl.BlockSpec((8, tm), lambda i: (0, i)),
                   pl.BlockSpec((N_EXPERTS, LANES), const)],
        out_shape=[jax.ShapeDtypeStruct((t, D_MODEL), F32),
                   jax.ShapeDtypeStruct((t, D_PACK), jnp.uint32),
                   jax.ShapeDtypeStruct((8, t), jnp.int32),
                   jax.ShapeDtypeStruct((8, t), F32),
                   jax.ShapeDtypeStruct((8, t), jnp.int32),
                   jax.ShapeDtypeStruct((N_EXPERTS, LANES), F32)],
        scratch_shapes=[pltpu.VMEM((N_EXPERTS, 1), F32)],
        compiler_params=pltpu.CompilerParams(
            dimension_semantics=("arbitrary",), vmem_limit_bytes=VMEM_LIMIT),
        name="outproj",
    )(x2d, y_r, y_f, gate1, shift2, scale2, norm_g, wo_r, wo_f, w_rt_hi, w_rt_lo, b_rt)


SC_CORES = 2
SC_SUBCORES = 16
SC_ROWS = 128


def _sc_gather_rows(idx, src):
    n_workers = SC_CORES * SC_SUBCORES
    m = idx.shape[0]
    d = src.shape[1]
    assert m % (n_workers * SC_ROWS) == 0
    n_chunks = m // (n_workers * SC_ROWS)
    mesh = plsc.VectorSubcoreMesh(core_axis_name="c", subcore_axis_name="s")

    @functools.partial(
        pl.kernel, mesh=mesh,
        out_type=jax.ShapeDtypeStruct((m, d), src.dtype),
        scratch_types=[pltpu.VMEM((n_chunks, SC_ROWS), jnp.int32),
                       pltpu.VMEM((SC_ROWS, d), src.dtype),
                       pltpu.SemaphoreType.DMA],
        name="sc_gather")
    def gather(src_hbm, idx_hbm, out_hbm, idx_v, rows_v, sem):
        wid = lax.axis_index("s") * SC_CORES + lax.axis_index("c")
        pltpu.sync_copy(idx_hbm.at[wid], idx_v)

        @pl.loop(0, n_chunks)
        def _(j):
            pltpu.async_copy(src_hbm.at[idx_v.at[j]], rows_v, sem).wait()
            pltpu.sync_copy(rows_v, out_hbm.at[pl.ds((wid * n_chunks + j) * SC_ROWS, SC_ROWS)])

    return gather(src, idx.reshape(n_workers, n_chunks, SC_ROWS))


def _sc_scatter_rows(src, dest, n_out):
    n_workers = SC_CORES * SC_SUBCORES
    t, d = src.shape
    n_slot = dest.shape[0]
    assert t % (n_workers * SC_ROWS) == 0
    n_chunks = t // (n_workers * SC_ROWS)
    mesh = plsc.VectorSubcoreMesh(core_axis_name="c", subcore_axis_name="s")
    idx = dest.reshape(n_slot, n_workers, n_chunks, SC_ROWS).transpose(1, 2, 0, 3)
    idx = idx.reshape(n_workers, n_chunks * n_slot, SC_ROWS)

    @functools.partial(
        pl.kernel, mesh=mesh,
        out_type=jax.ShapeDtypeStruct((n_out, d), src.dtype),
        scratch_types=[pltpu.VMEM((n_chunks * n_slot, SC_ROWS), jnp.int32),
                       pltpu.VMEM((SC_ROWS, d), src.dtype)],
        name="sc_scatter")
    def scatter(src_hbm, idx_hbm, out_hbm, idx_v, rows_v):
        wid = lax.axis_index("s") * SC_CORES + lax.axis_index("c")
        pltpu.sync_copy(idx_hbm.at[wid], idx_v)

        @pl.loop(0, n_chunks)
        def _(j):
            pltpu.sync_copy(src_hbm.at[pl.ds((wid * n_chunks + j) * SC_ROWS, SC_ROWS)], rows_v)
            for k in range(n_slot):
                pltpu.sync_copy(rows_v, out_hbm.at[idx_v.at[j * n_slot + k]])

    return scatter(src, idx)


def _expert_kernel(be_ref, nv_ref, x_ref, wgu_ref, bgu_ref, wd_ref, bd_ref, o_ref):
    del be_ref
    valid = _iota((EXPERT_BLOCK, 1), 0) < nv_ref[pl.program_id(0)]
    lo, hi = _unpack_rows(jnp.where(valid, x_ref[...], jnp.uint32(0)))
    x = jnp.concatenate([lo.astype(BF16), hi.astype(BF16)], axis=1)
    half = EXPERT_BLOCK // 2
    gus = [jnp.dot(x[r * half:(r + 1) * half], wgu_ref[...], preferred_element_type=F32)
           + bgu_ref[...] for r in range(2)]
    for r, gu in enumerate(gus):
        gate = jnp.minimum(gu[:, :D_MODEL], SWIGLU_LIMIT)
        up = jnp.clip(gu[:, D_MODEL:], -SWIGLU_LIMIT, SWIGLU_LIMIT)
        act = gate * _sigmoid(SWIGLU_ALPHA * gate) * (up + 1.0)
        o_ref[r * half:(r + 1) * half, :] = _pack_rows(
            jnp.dot(act.astype(BF16), wd_ref[...], preferred_element_type=F32) + bd_ref[...])


def _experts(block_e, n_valid, xs, w_gu, b_gu, w_d, b_d):
    n_blocks = block_e.shape[0]
    grid_spec = pltpu.PrefetchScalarGridSpec(
        num_scalar_prefetch=2,
        grid=(n_blocks,),
        in_specs=[pl.BlockSpec((EXPERT_BLOCK, D_PACK), lambda j, be, nv: (j, 0)),
                  pl.BlockSpec((None, D_MODEL, 2 * D_MODEL), lambda j, be, nv: (be[j], 0, 0)),
                  pl.BlockSpec((None, 1, 2 * D_MODEL), lambda j, be, nv: (be[j], 0, 0)),
                  pl.BlockSpec((None, D_MODEL, D_MODEL), lambda j, be, nv: (be[j], 0, 0)),
                  pl.BlockSpec((None, 1, D_MODEL), lambda j, be, nv: (be[j], 0, 0))],
        out_specs=pl.BlockSpec((EXPERT_BLOCK, D_PACK), lambda j, be, nv: (j, 0)),
    )
    return pl.pallas_call(
        _expert_kernel,
        grid_spec=grid_spec,
        out_shape=jax.ShapeDtypeStruct(xs.shape, jnp.uint32),
        compiler_params=pltpu.CompilerParams(
            dimension_semantics=("arbitrary",), vmem_limit_bytes=VMEM_LIMIT),
        name="experts",
    )(block_e, n_valid, xs, w_gu, b_gu, w_d, b_d)


COMBINE_TOKENS = 1024
MOE_SPLITS = 2


def _combine_kernel(yg_ref, x1_ref, gate_ref, g2_ref, fg_ref, o_ref):
    gates = gate_ref[...].T
    acc_lo = acc_hi = None
    for kk in range(TOP_K):
        lo, hi = _unpack_rows(yg_ref[kk * COMBINE_TOKENS:(kk + 1) * COMBINE_TOKENS, :])
        g = gates[:, kk:kk + 1]
        acc_lo = g * lo if acc_lo is None else acc_lo + g * lo
        acc_hi = g * hi if acc_hi is None else acc_hi + g * hi
    x2 = x1_ref[...] + g2_ref[...] * jnp.concatenate([acc_lo, acc_hi], axis=1)
    o_ref[...] = x2 * lax.rsqrt(jnp.mean(x2 * x2, axis=-1, keepdims=True) + NORM_EPS) * fg_ref[...]


def _combine_kernel_into(prev_ref, *refs):
    del prev_ref
    _combine_kernel(*refs)


def _combine(yg, x1, gates, gate2, final_g, seq, row0, t_total, prev):
    t = x1.shape[0]
    tm = COMBINE_TOKENS
    per_b = seq // tm
    blk0 = row0 // tm
    rows = lambda i: (i, 0)
    in_specs = [pl.BlockSpec((TOP_K * tm, D_PACK), rows),
                pl.BlockSpec((tm, D_MODEL), rows),
                pl.BlockSpec((8, tm), lambda i: (0, i)),
                pl.BlockSpec((None, 1, D_MODEL), lambda i: ((i + blk0) // per_b, 0, 0)),
                pl.BlockSpec((1, D_MODEL), lambda i: (0, 0))]
    args = (yg, x1, gates, gate2, final_g)
    if prev is not None:
        in_specs = [pl.BlockSpec(memory_space=pl.ANY)] + in_specs
        args = (prev,) + args
    return pl.pallas_call(
        _combine_kernel if prev is None else _combine_kernel_into,
        grid=(t // tm,),
        in_specs=in_specs,
        out_specs=pl.BlockSpec((tm, D_MODEL), lambda i: (i + blk0, 0)),
        out_shape=jax.ShapeDtypeStruct((t_total, D_MODEL), F32),
        input_output_aliases={} if prev is None else {0: 0},
        compiler_params=pltpu.CompilerParams(
            dimension_semantics=("parallel",), vmem_limit_bytes=VMEM_LIMIT),
        name="combine",
    )(*args)


def _moe(h2, idx, gates, rank, counts, x1, gate2, final_g, w_gu, b_gu, w_d, b_d, seq,
         row0, t_total, prev):
    t = h2.shape[0]
    n_slots = t * TOP_K
    n_blocks = -(-n_slots // EXPERT_BLOCK) + N_EXPERTS
    cap = n_blocks * EXPERT_BLOCK
    padded = (counts + EXPERT_BLOCK - 1) // EXPERT_BLOCK * EXPERT_BLOCK
    pad_ends = jnp.cumsum(padded)
    pad_starts = pad_ends - padded
    experts = jnp.arange(N_EXPERTS, dtype=jnp.int32)
    dest = jnp.sum(jnp.where(idx[..., None] == experts, pad_starts, 0), axis=-1) + rank
    block_starts = jnp.arange(n_blocks, dtype=jnp.int32) * EXPERT_BLOCK
    block_e = jnp.minimum(jnp.sum(block_starts[:, None] >= pad_ends[None, :], axis=1),
                          N_EXPERTS - 1).astype(jnp.int32)
    n_valid = jnp.clip(counts[block_e] - (block_starts - pad_starts[block_e]), 0, EXPERT_BLOCK)

    xs = _sc_scatter_rows(h2, dest, cap)
    yb = _experts(block_e, n_valid.astype(jnp.int32), xs, w_gu, b_gu, w_d, b_d)
    dest_blocks = dest.reshape(TOP_K, -1, COMBINE_TOKENS).transpose(1, 0, 2).reshape(-1)
    yg = _sc_gather_rows(dest_blocks, yb)
    return _combine(yg, x1, gates, gate2, final_g, seq, row0, t_total, prev)


def _layer(x, c_mod, norm1_g, w_in, mu_shift, w0, w2, a0, a2, g2, k_k, k_a, r_k, gn_w, gn_b, b_f,
           q_norm_g, k_norm_g, o_norm_g, w_out, norm2_g, w_router, b_router, w_gate_up,
           b_gate_up, w_down, b_down, final_g, tm_in, tm_out):
    bsz, seq, _ = x.shape
    shift1, scale1, gate1, shift2, scale2, gate2 = (
        m.reshape(bsz, 1, D_MODEL) for m in jnp.split(c_mod, 6, axis=-1))
    row = lambda v: v.reshape(1, -1)

    w_r = w_in[:, :RWKV_COLS].astype(BF16)
    w_x = w_in[:, RWKV_COLS:RWKV_COLS + FOX_MAIN].astype(BF16)
    w_f = w_in[:, RWKV_COLS + FOX_MAIN:].T
    b_fp = jnp.pad(b_f, (0, LANES - N_HEADS)).reshape(1, LANES)
    qk_gain = jnp.concatenate([jnp.tile(q_norm_g, N_HEADS) * (HEAD_DIM ** -0.5 * LOG2_E),
                               jnp.tile(k_norm_g, N_HEADS)]).reshape(1, -1)
    p_r, p_x, k_bias, q_bias, v_t = _inproj(x, shift1, scale1, row(norm1_g), w_r, w_x, w_f, b_fp,
                                            qk_gain, tm_in)

    zeros = jnp.zeros((LANES - 64, D_GRP), F32)
    w2p = jnp.concatenate([w2, zeros], axis=0).astype(BF16)
    a2p = jnp.concatenate([zeros, a2], axis=0).astype(BF16)
    y_r, w_gu, w_d = _rwkv(p_r, row(mu_shift), row(w0), w2p, row(a0), a2p, g2.astype(BF16),
                           row(k_k), row(k_a), row(r_k), row(gn_w), row(gn_b), w_gate_up, w_down)

    y_f = _fox(p_x, k_bias, q_bias, v_t, jnp.tile(o_norm_g, 2).reshape(1, LANES))

    t = bsz * seq
    w_rt = jnp.pad(w_router.T, ((0, LANES - N_EXPERTS), (0, 0)))
    b_rt = b_router.reshape(N_EXPERTS, 1)
    wo = w_out.astype(BF16)
    b_gu, b_d = b_gate_up.reshape(N_EXPERTS, 1, -1), b_down.reshape(N_EXPERTS, 1, -1)
    t_part = t // MOE_SPLITS
    out = None
    for part in range(MOE_SPLITS):
        row0 = part * t_part
        x1, h2, idx, gates, rank, cnt = _outproj(
            x.reshape(t, D_MODEL), y_r.reshape(t, D_GRP), y_f.reshape(t, D_GRP), gate1, shift2,
            scale2, row(norm2_g), wo[:D_GRP], wo[D_GRP:], w_rt, b_rt, tm_out, seq, row0, t_part)
        counts = cnt[:, 0].astype(jnp.int32)
        out = _moe(h2, idx[:TOP_K], gates, rank[:TOP_K], counts, x1, gate2, row(final_g),
                   w_gu, b_gu, w_d, b_d, seq, row0, t, out)
    return out.reshape(bsz, seq, D_MODEL)


def kernel(x, c, w_ada, b_ada, norm1_g, w_in, mu_shift, w0, w2, a0, a2, g2, k_k, k_a, r_k, gn_w,
           gn_b, b_f, q_norm_g, k_norm_g, o_norm_g, w_out, norm2_g, w_router, b_router, w_gate_up,
           b_gate_up, w_down, b_down, final_g):
    assert w_ada.shape[0] == 1, "single-layer block"
    c_mod = _adaln(c, w_ada[0], b_ada[0])
    return _layer(x, c_mod, norm1_g[0], w_in[0], mu_shift[0], w0[0], w2[0], a0[0], a2[0], g2[0],
                  k_k[0], k_a[0], r_k[0], gn_w[0], gn_b[0], b_f[0], q_norm_g[0], k_norm_g[0],
                  o_norm_g[0], w_out[0], norm2_g[0], w_router[0], b_router[0], w_gate_up[0],
                  b_gate_up[0], w_down[0], b_down[0], final_g,
                  tm_in=min(512, x.shape[1]), tm_out=min(1024, x.shape[1]))
```

```python
import functools

import jax
import jax.numpy as jnp
from jax import lax
from jax.experimental import pallas as pl
from jax.experimental.pallas import tpu as pltpu
from jax.experimental.pallas import tpu_sc as plsc

F32 = jnp.float32
BF16 = jnp.bfloat16
HIGHEST = lax.Precision.HIGHEST

D_MODEL = 1024
HEAD_DIM = 64
N_HEADS = 8
D_GRP = N_HEADS * HEAD_DIM
RWKV_COLS = 1792
LORA_OFF = 3 * D_GRP
GATE_OFF = LORA_OFF + 128
FOX_MAIN = 4 * D_GRP
N_EXPERTS = 32
TOP_K = 4
EXPERT_BLOCK = 512
SWIGLU_ALPHA = 1.702
SWIGLU_LIMIT = 7.0
NORM_EPS = 1e-6
GN_EPS = 64e-5
LOG2_E = 1.4426950408889634
LANES = 128
CHUNK = 64
FOX_SUB_KEYS = 512
HEADS_PER_SCAN = 4
SCAN_W = HEADS_PER_SCAN * HEAD_DIM
SEG_TERMS = 1
CUM_TERMS = 2
VMEM_LIMIT = 56 * 1024 * 1024


def _dot(a, b):
    return jnp.dot(a.astype(BF16), b.astype(BF16), preferred_element_type=F32)


def _fdot(a, b):
    return jnp.dot(a, b, precision=HIGHEST, preferred_element_type=F32)


def _split_dot(x, m, terms):
    acc = None
    rem = x
    for _ in range(terms):
        part = rem.astype(BF16)
        rem = rem - part.astype(F32)
        d = jnp.dot(part, m, preferred_element_type=F32)
        acc = d if acc is None else acc + d
    return acc


def _tri_dot(m, x, terms):
    acc = None
    rem = x
    for _ in range(terms):
        part = rem.astype(BF16)
        rem = rem - part.astype(F32)
        d = jnp.dot(m, part, preferred_element_type=F32)
        acc = d if acc is None else acc + d
    return acc


def _iota(shape, dim):
    return lax.broadcasted_iota(jnp.int32, shape, dim)


def _seg_reduce_mat(n):
    return (_iota((n, LANES), 0) // HEAD_DIM == _iota((n, LANES), 1)).astype(BF16)


def _seg_expand_mat(n):
    return (_iota((LANES, n), 1) // HEAD_DIM == _iota((LANES, n), 0)).astype(BF16)


D_PACK = D_MODEL // 2


def _pack_rows(x):
    lo = lax.bitcast_convert_type(x[:, :D_PACK].astype(BF16).astype(F32), jnp.uint32)
    hi = lax.bitcast_convert_type(x[:, D_PACK:].astype(BF16).astype(F32), jnp.uint32)
    return hi | (lo >> 16)


def _unpack_rows(p):
    lo = lax.bitcast_convert_type(p << 16, F32)
    hi = lax.bitcast_convert_type(p & jnp.uint32(0xFFFF0000), F32)
    return lo, hi


def _log_sigmoid(z):
    return jnp.minimum(z, 0.0) - jnp.log(1.0 + jnp.exp(-jnp.abs(z)))


def _sigmoid(z):
    return 1.0 / (1.0 + jnp.exp(-z))


def _adaln_kernel(c_ref, w_ref, b_ref, o_ref):
    c = c_ref[...]
    o_ref[...] = _fdot(c * _sigmoid(c), w_ref[...]) + b_ref[...]


def _adaln(c, w_ada, b_ada):
    bsz = c.shape[0]
    n_mod = w_ada.shape[1] // D_MODEL
    return pl.pallas_call(
        _adaln_kernel,
        grid=(n_mod,),
        in_specs=[pl.BlockSpec((bsz, D_MODEL), lambda j: (0, 0)),
                  pl.BlockSpec((D_MODEL, D_MODEL), lambda j: (0, j)),
                  pl.BlockSpec((1, D_MODEL), lambda j: (0, j))],
        out_specs=pl.BlockSpec((bsz, D_MODEL), lambda j: (0, j)),
        out_shape=jax.ShapeDtypeStruct((bsz, n_mod * D_MODEL), F32),
        name="adaln",
    )(c, w_ada, b_ada.reshape(1, -1))


def _inproj_kernel(x_ref, sh_ref, sc_ref, g_ref, wr_ref, wx_ref, wft_ref, bf_ref, qkg_ref,
                   pr_ref, px_ref, kb_ref, qb_ref, vt_ref, carry_ref):
    @pl.when(pl.program_id(1) == 0)
    def _():
        carry_ref[...] = jnp.zeros_like(carry_ref)

    x = x_ref[...]
    tm = x.shape[0]
    h = x * lax.rsqrt(jnp.mean(x * x, axis=-1, keepdims=True) + NORM_EPS) * g_ref[...]
    h = h * (1.0 + sc_ref[...]) + sh_ref[...]
    hb = h.astype(BF16)

    pr_ref[...] = jnp.dot(hb, wr_ref[...], preferred_element_type=F32).astype(BF16)

    px = jnp.dot(hb, wx_ref[...], preferred_element_type=F32)
    qk = px[:, :2 * D_GRP]
    ss = _split_dot(qk * qk, _seg_reduce_mat(2 * D_GRP), SEG_TERMS)
    inv = lax.rsqrt(ss * (1.0 / HEAD_DIM) + NORM_EPS)
    qk = qk * _split_dot(inv, _seg_expand_mat(2 * D_GRP), SEG_TERMS) * qkg_ref[...]
    px_ref[:, :2 * D_GRP] = qk.astype(BF16)
    px_ref[:, 2 * D_GRP:] = px[:, 2 * D_GRP:].astype(BF16)
    vt_ref[...] = px[:, 2 * D_GRP:3 * D_GRP].T.astype(BF16)

    lane = _iota((1, LANES), 1)
    z = jnp.zeros((tm, LANES), F32)
    for hd in range(N_HEADS):
        zh = jnp.sum(h * wft_ref[hd:hd + 1, :], axis=-1, keepdims=True)
        z = jnp.where(lane == hd, zh, z)
    cum = _log_sigmoid(z + bf_ref[...])
    row_id = _iota((tm, 1), 0)
    shift = 1
    while shift < tm:
        cum = cum + jnp.where(row_id >= shift, pltpu.roll(cum, shift, axis=0), 0.0)
        shift *= 2
    cum = cum + carry_ref[...]
    carry_ref[...] = cum[tm - 1:tm, :]

    parts = []
    rem = cum * LOG2_E
    for _ in range(3):
        part = rem.astype(BF16)
        rem = rem - part.astype(F32)
        parts.append(part)
    src, dst = _iota((LANES, LANES), 0), _iota((LANES, LANES), 1)

    def spread(offset):
        return sum(jnp.dot(part, ((dst == 8 * src + offset + t) & (src < N_HEADS)).astype(BF16),
                           preferred_element_type=F32) for t, part in enumerate(parts))

    slot = _iota((1, LANES), 1) % 8
    kb_ref[...] = (jnp.where((slot >= 3) & (slot < 6), 1.0, 0.0) - spread(0)).astype(BF16)
    qb_ref[...] = (jnp.where(slot < 3, 1.0, 0.0) + spread(3)).astype(BF16)


def _inproj(x, shift, scale, g, w_r, w_x, w_f_t, b_f, qk_gain, tm):
    bsz, seq, _ = x.shape
    const = lambda b, s: (0, 0)
    return pl.pallas_call(
        _inproj_kernel,
        grid=(bsz, seq // tm),
        in_specs=[pl.BlockSpec((None, tm, D_MODEL), lambda b, s: (b, s, 0)),
                  pl.BlockSpec((None, 1, D_MODEL), lambda b, s: (b, 0, 0)),
                  pl.BlockSpec((None, 1, D_MODEL), lambda b, s: (b, 0, 0)),
                  pl.BlockSpec((1, D_MODEL), const),
                  pl.BlockSpec((D_MODEL, RWKV_COLS), const),
                  pl.BlockSpec((D_MODEL, FOX_MAIN), const),
                  pl.BlockSpec((N_HEADS, D_MODEL), const),
                  pl.BlockSpec((1, LANES), const),
                  pl.BlockSpec((1, 2 * D_GRP), const)],
        out_specs=[pl.BlockSpec((None, tm, RWKV_COLS), lambda b, s: (b, s, 0)),
                   pl.BlockSpec((None, tm, FOX_MAIN), lambda b, s: (b, s, 0)),
                   pl.BlockSpec((None, tm, LANES), lambda b, s: (b, s, 0)),
                   pl.BlockSpec((None, tm, LANES), lambda b, s: (b, s, 0)),
                   pl.BlockSpec((None, D_GRP, tm), lambda b, s: (b, 0, s))],
        out_shape=[jax.ShapeDtypeStruct((bsz, seq, RWKV_COLS), BF16),
                   jax.ShapeDtypeStruct((bsz, seq, FOX_MAIN), BF16),
                   jax.ShapeDtypeStruct((bsz, seq, LANES), BF16),
                   jax.ShapeDtypeStruct((bsz, seq, LANES), BF16),
                   jax.ShapeDtypeStruct((bsz, D_GRP, seq), BF16)],
        scratch_shapes=[pltpu.VMEM((1, LANES), F32)],
        compiler_params=pltpu.CompilerParams(
            dimension_semantics=("parallel", "arbitrary"), vmem_limit_bytes=VMEM_LIMIT),
        name="inproj",
    )(x, shift, scale, g, w_r, w_x, w_f_t, b_f, qk_gain)


_NN = (((1,), (0,)), ((), ()))
_NT = (((1,), (1,)), ((), ()))
_TN = (((0,), (0,)), ((), ()))
SCAN_N = HEADS_PER_SCAN * CHUNK
BATCH_PER_STEP = 8
INV_LEVELS = 5
M_HEAD, M_STRICT, M_INCL, M_EYE, M_BASE, M_OFF = 0, 1, 2, 3, 4, 5


def _bdot(a, b, dims):
    return lax.dot_general(a, b, dims, preferred_element_type=F32)


def _scan_masks():
    rr, cc = _iota((SCAN_N, SCAN_W), 0), _iota((SCAN_N, SCAN_W), 1)
    ri, ci = _iota((SCAN_N, SCAN_N), 0), _iota((SCAN_N, SCAN_N), 1)
    same = ri // CHUNK == ci // CHUNK
    masks = [rr // CHUNK == cc // HEAD_DIM, same & (ri > ci), same & (ri >= ci), ri == ci,
             (ri // 2 == ci // 2) & (ri > ci)]
    blk = 2
    while blk < CHUNK:
        masks.append((ri // (2 * blk) == ci // (2 * blk)) & (ri // blk != ci // blk) & (ri > ci))
        blk *= 2
    return jnp.stack(masks).astype(BF16)


def _rwkv_kernel(p_ref, masks_ref, mu_ref, w0_ref, w2_ref, a0_ref, a2_ref, g2_ref, kk_ref, ka_ref,
                 rk_ref, gnw_ref, gnb_ref, wgu_ref, wd_ref, o_ref, wgu_bf_ref, wd_bf_ref,
                 last_ref, state_ref):
    wgu_bf_ref[...] = wgu_ref[...].astype(BF16)
    wd_bf_ref[...] = wd_ref[...].astype(BF16)

    @pl.when(pl.program_id(1) == 0)
    def _():
        last_ref[...] = jnp.zeros_like(last_ref)
        state_ref[...] = jnp.zeros_like(state_ref)

    mu, w0, w2, a0, a2, g2, k_k, k_a, r_k, gn_w, gn_b = (
        ref[...] for ref in (mu_ref, w0_ref, w2_ref, a0_ref, a2_ref, g2_ref, kk_ref, ka_ref,
                             rk_ref, gnw_ref, gnb_ref))
    rows = BATCH_PER_STEP * CHUNK
    p = p_ref[...].astype(F32).reshape(rows, RWKV_COLS)
    row_id = _iota((rows, 1), 0)
    prev = pltpu.roll(p, 1, axis=0)
    for bb in range(BATCH_PER_STEP):
        prev = jnp.where(row_id == bb * CHUNK, last_ref[bb], prev)
        last_ref[bb] = p[(bb + 1) * CHUNK - 1:(bb + 1) * CHUNK, :]
    pf = p + mu * (prev - p)
    r = pf[:, 0:D_GRP]
    k = pf[:, D_GRP:2 * D_GRP]
    v = pf[:, 2 * D_GRP:3 * D_GRP]
    lora = pf[:, LORA_OFF:GATE_OFF]
    gd = pf[:, GATE_OFF:RWKV_COLS]

    wlog = w0 + _dot(jnp.tanh(lora), w2)
    neg = -wlog
    softplus = jnp.maximum(neg, 0.0) + jnp.log(1.0 + jnp.exp(-jnp.abs(neg)))
    ld = -jnp.exp(-softplus - 0.5)
    a = _sigmoid(a0 + _dot(lora, a2))
    g = _dot(_sigmoid(gd), g2)

    red, exp_m = _seg_reduce_mat(D_GRP), _seg_expand_mat(D_GRP)
    kk = k * k_k
    n2 = _split_dot(kk * kk, red, SEG_TERMS)
    kk = kk * _split_dot(1.0 / jnp.maximum(jnp.sqrt(n2), 1e-12), exp_m, SEG_TERMS)
    k2 = k * (1.0 + (a - 1.0) * k_a)

    tr, tc = _iota((rows, rows), 0), _iota((rows, rows), 1)
    tri = ((tr >= tc) & (tr // CHUNK == tc // CHUNK)).astype(BF16)
    cl = _tri_dot(tri, ld, CUM_TERMS)
    cl_end = jnp.concatenate(
        [jnp.broadcast_to(cl[(bb + 1) * CHUNK - 1:(bb + 1) * CHUNK, :], (CHUNK, D_GRP))
         for bb in range(BATCH_PER_STEP)], axis=0)
    e_in = jnp.exp(cl)
    e_out = jnp.exp(-cl)
    e_rem = jnp.exp(cl_end - cl)
    p_end = jnp.exp(cl_end)
    kka = kk * a
    ops = [(-kk * jnp.exp(cl - ld)).astype(BF16), (kka * e_out).astype(BF16),
           (k2 * e_out).astype(BF16), (r * e_in).astype(BF16), v.astype(BF16),
           (kka * e_rem).astype(BF16), (k2 * e_rem).astype(BF16)]

    chains = [(bb, grp) for bb in range(BATCH_PER_STEP)
              for grp in range(N_HEADS // HEADS_PER_SCAN)]
    head_mask = masks_ref[M_HEAD]
    strict, incl = masks_ref[M_STRICT], masks_ref[M_INCL]

    def stacked(op, bb, grp):
        part = op[bb * CHUNK:(bb + 1) * CHUNK, grp * SCAN_W:(grp + 1) * SCAN_W]
        return jnp.concatenate([part] * HEADS_PER_SCAN, axis=0) * head_mask

    xs = [[stacked(op, bb, grp) for op in ops] for bb, grp in chains]
    st = [state_ref[bb, grp] for bb, grp in chains]
    sb = [s.astype(BF16) for s in st]
    nab = [_bdot(x[0], x[1], _NT).astype(BF16) for x in xs]
    aak = [_bdot(x[0], x[2], _NT).astype(BF16) * strict for x in xs]
    arb = [_bdot(x[3], x[1], _NT).astype(BF16) * incl for x in xs]
    ark = [_bdot(x[3], x[2], _NT).astype(BF16) * incl for x in xs]
    t_inv = [masks_ref[M_EYE] + n * masks_ref[M_BASE] for n in nab]
    for lvl in range(INV_LEVELS):
        half = [_bdot(t, n * masks_ref[M_OFF + lvl], _NN).astype(BF16) for t, n in zip(t_inv, nab)]
        t_inv = [t + _bdot(h, t, _NN).astype(BF16) for t, h in zip(t_inv, half)]
    rhs = [(_bdot(x[0], s, _NT) + _bdot(k, x[4], _NN)).astype(BF16)
           for x, s, k in zip(xs, sb, aak)]
    sa = [_bdot(t, h, _NN).astype(BF16) for t, h in zip(t_inv, rhs)]
    ys = [_bdot(x[3], s, _NT) + _bdot(b, u, _NN) + _bdot(k, x[4], _NN)
          for x, s, b, u, k in zip(xs, sb, arb, sa, ark)]
    for (bb, grp), x, s, u in zip(chains, xs, st, sa):
        decay = p_end[bb * CHUNK:bb * CHUNK + 1, grp * SCAN_W:(grp + 1) * SCAN_W]
        state_ref[bb, grp] = s * decay + _bdot(u, x[5], _TN) + _bdot(x[4], x[6], _TN)
    ys = [y[0:CHUNK] + y[CHUNK:2 * CHUNK] + y[2 * CHUNK:3 * CHUNK] + y[3 * CHUNK:4 * CHUNK]
          for y in ys]
    n_grp = N_HEADS // HEADS_PER_SCAN
    y = jnp.concatenate([jnp.concatenate(ys[bb * n_grp:(bb + 1) * n_grp], axis=1)
                         for bb in range(BATCH_PER_STEP)], axis=0)

    mean = _split_dot(_split_dot(y, red, SEG_TERMS) * (1.0 / HEAD_DIM), exp_m, SEG_TERMS)
    d = y - mean
    var = _split_dot(d * d, red, SEG_TERMS) * (1.0 / HEAD_DIM)
    yn = d * _split_dot(lax.rsqrt(var + GN_EPS), exp_m, SEG_TERMS) * gn_w + gn_b
    bonus = _split_dot(_split_dot(r * k2 * r_k, red, SEG_TERMS), exp_m, SEG_TERMS) * v
    o_ref[...] = ((yn + bonus) * g).astype(BF16).reshape(BATCH_PER_STEP, CHUNK, D_GRP)


def _rwkv(p_r, mu, w0, w2p, a0, a2p, g2, k_k, k_a, r_k, gn_w, gn_b, w_gate_up, w_down):
    bsz, seq, _ = p_r.shape
    assert bsz % BATCH_PER_STEP == 0
    n_chunk = seq // CHUNK
    n_step = (bsz // BATCH_PER_STEP) * n_chunk
    wgu2d = w_gate_up.reshape(-1, w_gate_up.shape[-1])
    wd2d = w_down.reshape(-1, w_down.shape[-1])
    assert wgu2d.shape[0] % (8 * n_step) == 0 and wd2d.shape[0] == wgu2d.shape[0]
    slab = wgu2d.shape[0] // n_step
    masks = _scan_masks()
    const = lambda b, s: (0, 0)
    step = lambda b, s: (b * n_chunk + s, 0)
    vec = pl.BlockSpec((1, D_GRP), const)
    y, wgu_bf, wd_bf = pl.pallas_call(
        _rwkv_kernel,
        grid=(bsz // BATCH_PER_STEP, n_chunk),
        in_specs=[pl.BlockSpec((BATCH_PER_STEP, CHUNK, RWKV_COLS), lambda b, s: (b, s, 0)),
                  pl.BlockSpec(masks.shape, lambda b, s: (0, 0, 0)),
                  pl.BlockSpec((1, RWKV_COLS), const),
                  vec, pl.BlockSpec((LANES, D_GRP), const),
                  vec, pl.BlockSpec((LANES, D_GRP), const),
                  pl.BlockSpec((LANES, D_GRP), const),
                  vec, vec, vec, vec, vec,
                  pl.BlockSpec((slab, wgu2d.shape[1]), step),
                  pl.BlockSpec((slab, wd2d.shape[1]), step)],
        out_specs=[pl.BlockSpec((BATCH_PER_STEP, CHUNK, D_GRP), lambda b, s: (b, s, 0)),
                   pl.BlockSpec((slab, wgu2d.shape[1]), step),
                   pl.BlockSpec((slab, wd2d.shape[1]), step)],
        out_shape=[jax.ShapeDtypeStruct((bsz, seq, D_GRP), BF16),
                   jax.ShapeDtypeStruct(wgu2d.shape, BF16),
                   jax.ShapeDtypeStruct(wd2d.shape, BF16)],
        scratch_shapes=[pltpu.VMEM((BATCH_PER_STEP, 1, RWKV_COLS), F32),
                        pltpu.VMEM((BATCH_PER_STEP, N_HEADS // HEADS_PER_SCAN, SCAN_W, SCAN_W), F32)],
        compiler_params=pltpu.CompilerParams(
            dimension_semantics=("parallel", "arbitrary"), vmem_limit_bytes=VMEM_LIMIT),
        name="rwkv",
    )(p_r, masks, mu, w0, w2p, a0, a2p, g2, k_k, k_a, r_k, gn_w, gn_b, wgu2d, wd2d)
    return y, wgu_bf.reshape(w_gate_up.shape), wd_bf.reshape(w_down.shape)


def _fox_kernel(q_ref, qb_ref, k_ref, kb_ref, vt_ref, og_ref, ong_ref, o_ref, m_ref, l_ref, acc_ref,
                *, seq):
    hp = pl.program_id(1)
    lane = _iota((1, LANES), 1)
    q = q_ref[...]
    qb = qb_ref[...]
    zero = jnp.zeros_like(q)
    qcat = [jnp.concatenate([jnp.where(lane // HEAD_DIM == hh, q, zero),
                             jnp.where(lane // 8 == hp * 2 + hh, qb, zero)], axis=1)
            for hh in range(2)]
    keys = min(FOX_SUB_KEYS, seq)
    n_sub = seq // keys
    half = keys // 2
    mask_a = _iota((half, half), 1) >= _iota((half, half), 0)
    mask_b = _iota((keys, half), 1) + half >= _iota((keys, half), 0)

    m_ref[...] = jnp.full(m_ref.shape, -jnp.inf, F32)
    l_ref[...] = jnp.zeros(l_ref.shape, F32)
    acc_ref[...] = jnp.zeros(acc_ref.shape, F32)

    def plan(s):
        lo = s * keys
        pieces = [(half, slice(lo, lo + half), mask_a), (keys, slice(lo + half, lo + keys), mask_b)]
        if lo + keys < seq:
            pieces.append((keys, slice(lo + keys, seq), None))
        return pieces

    def scores(s):
        lo = s * keys
        kcat = jnp.concatenate([k_ref[lo:lo + keys, :], kb_ref[lo:lo + keys, :]], axis=1)
        return [lax.dot_general(kcat[:nk], qc[qs, :], _NT, preferred_element_type=F32)
                for qc in qcat for nk, qs, _ in plan(s)]

    pending = scores(0)
    for s in range(n_sub):
        lo = s * keys
        nxt = scores(s + 1) if s + 1 < n_sub else None
        n_piece = len(plan(s))
        pieces = []
        for hh in range(2):
            for (nk, qs, mask), st in zip(plan(s), pending[hh * n_piece:(hh + 1) * n_piece]):
                pieces.append((hh, nk, qs, st if mask is None else jnp.where(mask, st, -jnp.inf)))
        m_old = [m_ref[hh, :, qs] for hh, _, qs, _ in pieces]
        m_new = [jnp.maximum(m, jnp.max(st, axis=0, keepdims=True))
                 for m, (_, _, _, st) in zip(m_old, pieces)]
        pts = [jnp.exp2(st - m) for (_, _, _, st), m in zip(pieces, m_new)]
        pvs = [jnp.dot(vt_ref[:, lo:lo + nk], pt.astype(BF16), preferred_element_type=F32)
               for (_, nk, _, _), pt in zip(pieces, pts)]
        for (hh, _, qs, _), mo, mn, pt, pv in zip(pieces, m_old, m_new, pts, pvs):
            alpha = jnp.exp2(mo - mn)
            m_ref[hh, :, qs] = mn
            l_ref[hh, :, qs] = alpha * l_ref[hh, :, qs] + jnp.sum(pt, axis=0, keepdims=True)
            acc_ref[hh, :, qs] = (alpha * acc_ref[hh, :, qs]
                                  + pv[hh * HEAD_DIM:(hh + 1) * HEAD_DIM, :])
        pending = nxt

    outs = []
    for hh in range(2):
        o = acc_ref[hh] / l_ref[hh]
        outs.append(o * lax.rsqrt(jnp.mean(o * o, axis=0, keepdims=True) + NORM_EPS))
    o = jnp.concatenate(outs, axis=0).T
    o_ref[...] = (o * ong_ref[...] * _sigmoid(og_ref[...].astype(F32))).astype(BF16)


def _fox(p_x, k_bias, q_bias, v_t, o_gain):
    bsz, seq, _ = p_x.shape
    npair = N_HEADS // 2
    return pl.pallas_call(
        functools.partial(_fox_kernel, seq=seq),
        grid=(bsz, npair),
        in_specs=[pl.BlockSpec((None, seq, LANES), lambda b, h: (b, 0, h)),
                  pl.BlockSpec((None, seq, LANES), lambda b, h: (b, 0, 0)),
                  pl.BlockSpec((None, seq, LANES), lambda b, h: (b, 0, npair + h)),
                  pl.BlockSpec((None, seq, LANES), lambda b, h: (b, 0, 0)),
                  pl.BlockSpec((None, LANES, seq), lambda b, h: (b, h, 0)),
                  pl.BlockSpec((None, seq, LANES), lambda b, h: (b, 0, 3 * npair + h)),
                  pl.BlockSpec((1, LANES), lambda b, h: (0, 0))],
        out_specs=pl.BlockSpec((None, seq, LANES), lambda b, h: (b, 0, h)),
        out_shape=jax.ShapeDtypeStruct((bsz, seq, D_GRP), BF16),
        scratch_shapes=[pltpu.VMEM((2, 1, seq), F32), pltpu.VMEM((2, 1, seq), F32),
                        pltpu.VMEM((2, HEAD_DIM, seq), F32)],
        compiler_params=pltpu.CompilerParams(
            dimension_semantics=("parallel", "parallel"), vmem_limit_bytes=VMEM_LIMIT),
        name="fox",
    )(p_x, q_bias, p_x, k_bias, v_t, p_x, o_gain)


def _outproj_kernel(x_ref, yr_ref, yf_ref, g1_ref, sh_ref, sc_ref, ng_ref, wor_ref, wof_ref,
                    wrt_ref, wrl_ref, brt_ref, x1_ref, h2_ref, idx_ref, gate_ref, rank_ref, cnt_ref,
                    carry_ref):
    @pl.when(pl.program_id(0) == 0)
    def _():
        carry_ref[...] = jnp.zeros_like(carry_ref)

    y = (jnp.dot(yr_ref[...], wor_ref[...], preferred_element_type=F32)
         + jnp.dot(yf_ref[...], wof_ref[...], preferred_element_type=F32))
    x1 = x_ref[...] + g1_ref[...] * y
    x1_ref[...] = x1
    tm = x1.shape[0]
    h = x1 * lax.rsqrt(jnp.mean(x1 * x1, axis=-1, keepdims=True) + NORM_EPS) * ng_ref[...]
    h2 = h * (1.0 + sc_ref[...]) + sh_ref[...]
    h2_ref[...] = _pack_rows(h2)

    h_hi = h2.astype(BF16)
    h_lo = (h2 - h_hi.astype(F32)).astype(BF16)
    logits = (lax.dot_general(wrt_ref[...], h_hi, _NT, preferred_element_type=F32)
              + lax.dot_general(wrt_ref[...], h_lo, _NT, preferred_element_type=F32)
              + lax.dot_general(wrl_ref[...], h_hi, _NT, preferred_element_type=F32))
    lg = logits[:N_EXPERTS, :] + brt_ref[...]
    expert = _iota((N_EXPERTS, tm), 0)
    picks = []
    hot_sum = jnp.zeros((N_EXPERTS, tm), F32)
    for _ in range(TOP_K):
        m = jnp.max(lg, axis=0, keepdims=True)
        sel = jnp.min(jnp.where(lg == m, expert, N_EXPERTS), axis=0, keepdims=True)
        hot = expert == sel
        picks.append((m, sel, hot))
        hot_sum = hot_sum + hot.astype(F32)
        lg = jnp.where(hot, -jnp.inf, lg)
    es = [jnp.exp(m - picks[0][0]) for m, _, _ in picks]
    den = es[0] + es[1] + es[2] + es[3]

    earlier = (_iota((tm, tm), 0) < _iota((tm, tm), 1)).astype(BF16)
    before = jnp.dot(hot_sum.astype(BF16), earlier, preferred_element_type=F32) + carry_ref[...]
    ranks = [jnp.sum(jnp.where(hot, before, 0.0), axis=0, keepdims=True).astype(jnp.int32)
             for _, _, hot in picks]
    pad_i = jnp.zeros((8 - TOP_K, tm), jnp.int32)
    idx_ref[...] = jnp.concatenate([sel for _, sel, _ in picks] + [pad_i], axis=0)
    gate_ref[...] = jnp.concatenate([e / den for e in es] + [pad_i.astype(F32)], axis=0)
    rank_ref[...] = jnp.concatenate(ranks + [pad_i], axis=0)
    carry_ref[...] = carry_ref[...] + jnp.sum(hot_sum, axis=1, keepdims=True)
    cnt_ref[...] = jnp.broadcast_to(carry_ref[...], cnt_ref.shape)


def _outproj(x2d, y_r, y_f, gate1, shift2, scale2, norm_g, wo_r, wo_f, w_rt, b_rt, tm, seq,
             row0, t):
    w_rt_hi = w_rt.astype(BF16)
    w_rt_lo = (w_rt - w_rt_hi.astype(F32)).astype(BF16)
    per_b = seq // tm
    blk0 = row0 // tm
    const = lambda i: (0, 0)
    rows = lambda i: (i, 0)
    rows_in = lambda i: (i + blk0, 0)
    mod = pl.BlockSpec((None, 1, D_MODEL), lambda i: ((i + blk0) // per_b, 0, 0))
    return pl.pallas_call(
        _outproj_kernel,
        grid=(t // tm,),
        in_specs=[pl.BlockSpec((tm, D_MODEL), rows_in),
                  pl.BlockSpec((tm, D_GRP), rows_in),
                  pl.BlockSpec((tm, D_GRP), rows_in),
                  mod, mod, mod,
                  pl.BlockSpec((1, D_MODEL), const),
                  pl.BlockSpec((D_GRP, D_MODEL), const),
                  pl.BlockSpec((D_GRP, D_MODEL), const),
                  pl.BlockSpec((LANES, D_MODEL), const),
                  pl.BlockSpec((LANES, D_MODEL), const),
                  pl.BlockSpec((N_EXPERTS, 1), const)],
        out_specs=[pl.BlockSpec((tm, D_MODEL), rows),
                   pl.BlockSpec((tm, D_PACK), rows),
                   pl.BlockSpec((8, tm), lambda i: (0, i)),
                   pl.BlockSpec((8, tm), lambda i: (0, i)),
                   pl.BlockSpec((8, tm), lambda i: (0, i)),
                   pl.BlockSpec((N_EXPERTS, LANES), const)],
        out_shape=[jax.ShapeDtypeStruct((t, D_MODEL), F32),
                   jax.ShapeDtypeStruct((t, D_PACK), jnp.uint32),
                   jax.ShapeDtypeStruct((8, t), jnp.int32),
                   jax.ShapeDtypeStruct((8, t), F32),
                   jax.ShapeDtypeStruct((8, t), jnp.int32),
                   jax.ShapeDtypeStruct((N_EXPERTS, LANES), F32)],
        scratch_shapes=[pltpu.VMEM((N_EXPERTS, 1), F32)],
        compiler_params=pltpu.CompilerParams(
            dimension_semantics=("arbitrary",), vmem_limit_bytes=VMEM_LIMIT),
        name="outproj",
    )(x2d, y_r, y_f, gate1, shift2, scale2, norm_g, wo_r, wo_f, w_rt_hi, w_rt_lo, b_rt)


SC_CORES = 2
SC_SUBCORES = 16
SC_ROWS = 64


def _sc_gather_rows(idx, src):
    n_workers = SC_CORES * SC_SUBCORES
    m = idx.shape[0]
    d = src.shape[1]
    assert m % (n_workers * SC_ROWS) == 0
    n_chunks = m // (n_workers * SC_ROWS)
    mesh = plsc.VectorSubcoreMesh(core_axis_name="c", subcore_axis_name="s")

    @functools.partial(
        pl.kernel, mesh=mesh,
        out_type=jax.ShapeDtypeStruct((m, d), src.dtype),
        scratch_types=[pltpu.VMEM((n_chunks, SC_ROWS), jnp.int32),
                       pltpu.VMEM((SC_ROWS, d), src.dtype),
                       pltpu.SemaphoreType.DMA],
        name="sc_gather")
    def gather(src_hbm, idx_hbm, out_hbm, idx_v, rows_v, sem):
        wid = lax.axis_index("s") * SC_CORES + lax.axis_index("c")
        pltpu.sync_copy(idx_hbm.at[wid], idx_v)

        @pl.loop(0, n_chunks)
        def _(j):
            pltpu.async_copy(src_hbm.at[idx_v.at[j]], rows_v, sem).wait()
            pltpu.sync_copy(rows_v, out_hbm.at[pl.ds((wid * n_chunks + j) * SC_ROWS, SC_ROWS)])

    return gather(src, idx.reshape(n_workers, n_chunks, SC_ROWS))


def _sc_scatter_rows(src, dest, n_out):
    n_workers = SC_CORES * SC_SUBCORES
    t, d = src.shape
    n_slot = dest.shape[0]
    assert t % (n_workers * SC_ROWS) == 0
    n_chunks = t // (n_workers * SC_ROWS)
    mesh = plsc.VectorSubcoreMesh(core_axis_name="c", subcore_axis_name="s")
    idx = dest.reshape(n_slot, n_workers, n_chunks, SC_ROWS).transpose(1, 2, 0, 3)
    idx = idx.reshape(n_workers, n_chunks * n_slot, SC_ROWS)

    @functools.partial(
        pl.kernel, mesh=mesh,
        out_type=jax.ShapeDtypeStruct((n_out, d), src.dtype),
        scratch_types=[pltpu.VMEM((n_chunks * n_slot, SC_ROWS), jnp.int32),
                       pltpu.VMEM((SC_ROWS, d), src.dtype)],
        name="sc_scatter")
    def scatter(src_hbm, idx_hbm, out_hbm, idx_v, rows_v):
        wid = lax.axis_index("s") * SC_CORES + lax.axis_index("c")
        pltpu.sync_copy(idx_hbm.at[wid], idx_v)

        @pl.loop(0, n_chunks)
        def _(j):
            pltpu.sync_copy(src_hbm.at[pl.ds((wid * n_chunks + j) * SC_ROWS, SC_ROWS)], rows_v)
            for k in range(n_slot):
                pltpu.sync_copy(rows_v, out_hbm.at[idx_v.at[j * n_slot + k]])

    return scatter(src, idx)


def _expert_kernel(be_ref, nv_ref, x_ref, wgu_ref, bgu_ref, wd_ref, bd_ref, o_ref):
    del be_ref
    n_valid = nv_ref[pl.program_id(0)]

    @pl.when(n_valid > 0)
    def _():
        valid = _iota((EXPERT_BLOCK, 1), 0) < n_valid
        lo, hi = _unpack_rows(jnp.where(valid, x_ref[...], jnp.uint32(0)))
        x = jnp.concatenate([lo.astype(BF16), hi.astype(BF16)], axis=1)
        half = EXPERT_BLOCK // 2
        gus = [jnp.dot(x[r * half:(r + 1) * half], wgu_ref[...], preferred_element_type=F32)
               + bgu_ref[...] for r in range(2)]
        for r, gu in enumerate(gus):
            gate = jnp.minimum(gu[:, :D_MODEL], SWIGLU_LIMIT)
            up = jnp.clip(gu[:, D_MODEL:], -SWIGLU_LIMIT, SWIGLU_LIMIT)
            act = gate * _sigmoid(SWIGLU_ALPHA * gate) * (up + 1.0)
            o_ref[r * half:(r + 1) * half, :] = _pack_rows(
                jnp.dot(act.astype(BF16), wd_ref[...], preferred_element_type=F32) + bd_ref[...])


def _experts(block_e, n_valid, xs, w_gu, b_gu, w_d, b_d):
    n_blocks = block_e.shape[0]
    grid_spec = pltpu.PrefetchScalarGridSpec(
        num_scalar_prefetch=2,
        grid=(n_blocks,),
        in_specs=[pl.BlockSpec((EXPERT_BLOCK, D_PACK), lambda j, be, nv: (j, 0)),
                  pl.BlockSpec((None, D_MODEL, 2 * D_MODEL), lambda j, be, nv: (be[j], 0, 0)),
                  pl.BlockSpec((None, 1, 2 * D_MODEL), lambda j, be, nv: (be[j], 0, 0)),
                  pl.BlockSpec((None, D_MODEL, D_MODEL), lambda j, be, nv: (be[j], 0, 0)),
                  pl.BlockSpec((None, 1, D_MODEL), lambda j, be, nv: (be[j], 0, 0))],
        out_specs=pl.BlockSpec((EXPERT_BLOCK, D_PACK), lambda j, be, nv: (j, 0)),
    )
    return pl.pallas_call(
        _expert_kernel,
        grid_spec=grid_spec,
        out_shape=jax.ShapeDtypeStruct(xs.shape, jnp.uint32),
        compiler_params=pltpu.CompilerParams(
            dimension_semantics=("arbitrary",), vmem_limit_bytes=VMEM_LIMIT),
        name="experts",
    )(block_e, n_valid, xs, w_gu, b_gu, w_d, b_d)


COMBINE_TOKENS = 1024
MOE_SPLITS = 2


def _combine_kernel(yg_ref, x1_ref, gate_ref, g2_ref, fg_ref, o_ref):
    gates = gate_ref[...].T
    acc_lo = acc_hi = None
    for kk in range(TOP_K):
        lo, hi = _unpack_rows(yg_ref[kk * COMBINE_TOKENS:(kk + 1) * COMBINE_TOKENS, :])
        g = gates[:, kk:kk + 1]
        acc_lo = g * lo if acc_lo is None else acc_lo + g * lo
        acc_hi = g * hi if acc_hi is None else acc_hi + g * hi
    x2 = x1_ref[...] + g2_ref[...] * jnp.concatenate([acc_lo, acc_hi], axis=1)
    o_ref[...] = x2 * lax.rsqrt(jnp.mean(x2 * x2, axis=-1, keepdims=True) + NORM_EPS) * fg_ref[...]


def _combine_kernel_into(prev_ref, *refs):
    del prev_ref
    _combine_kernel(*refs)


def _combine(yg, x1, gates, gate2, final_g, seq, row0, t_total, prev):
    t = x1.shape[0]
    tm = COMBINE_TOKENS
    per_b = seq // tm
    blk0 = row0 // tm
    rows = lambda i: (i, 0)
    in_specs = [pl.BlockSpec((TOP_K * tm, D_PACK), rows),
                pl.BlockSpec((tm, D_MODEL), rows),
                pl.BlockSpec((8, tm), lambda i: (0, i)),
                pl.BlockSpec((None, 1, D_MODEL), lambda i: ((i + blk0) // per_b, 0, 0)),
                pl.BlockSpec((1, D_MODEL), lambda i: (0, 0))]
    args = (yg, x1, gates, gate2, final_g)
    if prev is not None:
        in_specs = [pl.BlockSpec(memory_space=pl.ANY)] + in_specs
        args = (prev,) + args
    return pl.pallas_call(
        _combine_kernel if prev is None else _combine_kernel_into,
        grid=(t // tm,),
        in_specs=in_specs,
        out_specs=pl.BlockSpec((tm, D_MODEL), lambda i: (i + blk0, 0)),
        out_shape=jax.ShapeDtypeStruct((t_total, D_MODEL), F32),
        input_output_aliases={} if prev is None else {0: 0},
        compiler_params=pltpu.CompilerParams(
            dimension_semantics=("parallel",), vmem_limit_bytes=VMEM_LIMIT),
        name="combine",
    )(*args)


def _moe(h2, idx, gates, rank, counts, x1, gate2, final_g, w_gu, b_gu, w_d, b_d, seq,
         row0, t_total, prev):
    t = h2.shape[0]
    n_slots = t * TOP_K
    n_blocks = -(-n_slots // EXPERT_BLOCK) + N_EXPERTS
    cap = n_blocks * EXPERT_BLOCK
    padded = (counts + EXPERT_BLOCK - 1) // EXPERT_BLOCK * EXPERT_BLOCK
    pad_ends = jnp.cumsum(padded)
    pad_starts = pad_ends - padded
    experts = jnp.arange(N_EXPERTS, dtype=jnp.int32)
    dest = jnp.sum(jnp.where(idx[..., None] == experts, pad_starts, 0), axis=-1) + rank
    block_starts = jnp.arange(n_blocks, dtype=jnp.int32) * EXPERT_BLOCK
    block_e = jnp.minimum(jnp.sum(block_starts[:, None] >= pad_ends[None, :], axis=1),
                          N_EXPERTS - 1).astype(jnp.int32)
    n_valid = jnp.clip(counts[block_e] - (block_starts - pad_starts[block_e]), 0, EXPERT_BLOCK)

    xs = _sc_scatter_rows(h2, dest, cap)
    yb = _experts(block_e, n_valid.astype(jnp.int32), xs, w_gu, b_gu, w_d, b_d)
    dest_blocks = dest.reshape(TOP_K, -1, COMBINE_TOKENS).transpose(1, 0, 2).reshape(-1)
    yg = _sc_gather_rows(dest_blocks, yb)
    return _combine(yg, x1, gates, gate2, final_g, seq, row0, t_total, prev)


def _layer(x, c_mod, norm1_g, w_in, mu_shift, w0, w2, a0, a2, g2, k_k, k_a, r_k, gn_w, gn_b, b_f,
           q_norm_g, k_norm_g, o_norm_g, w_out, norm2_g, w_router, b_router, w_gate_up,
           b_gate_up, w_down, b_down, final_g, tm_in, tm_out):
    bsz, seq, _ = x.shape
    shift1, scale1, gate1, shift2, scale2, gate2 = (
        m.reshape(bsz, 1, D_MODEL) for m in jnp.split(c_mod, 6, axis=-1))
    row = lambda v: v.reshape(1, -1)

    w_r = w_in[:, :RWKV_COLS].astype(BF16)
    w_x = w_in[:, RWKV_COLS:RWKV_COLS + FOX_MAIN].astype(BF16)
    w_f = w_in[:, RWKV_COLS + FOX_MAIN:].T
    b_fp = jnp.pad(b_f, (0, LANES - N_HEADS)).reshape(1, LANES)
    qk_gain = jnp.concatenate([jnp.tile(q_norm_g, N_HEADS) * (HEAD_DIM ** -0.5 * LOG2_E),
                               jnp.tile(k_norm_g, N_HEADS)]).reshape(1, -1)
    p_r, p_x, k_bias, q_bias, v_t = _inproj(x, shift1, scale1, row(norm1_g), w_r, w_x, w_f, b_fp,
                                            qk_gain, tm_in)

    zeros = jnp.zeros((LANES - 64, D_GRP), F32)
    w2p = jnp.concatenate([w2, zeros], axis=0).astype(BF16)
    a2p = jnp.concatenate([zeros, a2], axis=0).astype(BF16)
    y_r, w_gu, w_d = _rwkv(p_r, row(mu_shift), row(w0), w2p, row(a0), a2p, g2.astype(BF16),
                           row(k_k), row(k_a), row(r_k), row(gn_w), row(gn_b), w_gate_up, w_down)

    y_f = _fox(p_x, k_bias, q_bias, v_t, jnp.tile(o_norm_g, 2).reshape(1, LANES))

    t = bsz * seq
    w_rt = jnp.pad(w_router.T, ((0, LANES - N_EXPERTS), (0, 0)))
    b_rt = b_router.reshape(N_EXPERTS, 1)
    wo = w_out.astype(BF16)
    b_gu, b_d = b_gate_up.reshape(N_EXPERTS, 1, -1), b_down.reshape(N_EXPERTS, 1, -1)
    t_part = t // MOE_SPLITS
    out = None
    for part in range(MOE_SPLITS):
        row0 = part * t_part
        x1, h2, idx, gates, rank, cnt = _outproj(
            x.reshape(t, D_MODEL), y_r.reshape(t, D_GRP), y_f.reshape(t, D_GRP), gate1, shift2,
            scale2, row(norm2_g), wo[:D_GRP], wo[D_GRP:], w_rt, b_rt, tm_out, seq, row0, t_part)
        counts = cnt[:, 0].astype(jnp.int32)
        out = _moe(h2, idx[:TOP_K], gates, rank[:TOP_K], counts, x1, gate2, row(final_g),
                   w_gu, b_gu, w_d, b_d, seq, row0, t, out)
    return out.reshape(bsz, seq, D_MODEL)


def kernel(x, c, w_ada, b_ada, norm1_g, w_in, mu_shift, w0, w2, a0, a2, g2, k_k, k_a, r_k, gn_w,
           gn_b, b_f, q_norm_g, k_norm_g, o_norm_g, w_out, norm2_g, w_router, b_router, w_gate_up,
           b_gate_up, w_down, b_down, final_g):
    assert w_ada.shape[0] == 1, "single-layer block"
    c_mod = _adaln(c, w_ada[0], b_ada[0])
    return _layer(x, c_mod, norm1_g[0], w_in[0], mu_shift[0], w0[0], w2[0], a0[0], a2[0], g2[0],
                  k_k[0], k_a[0], r_k[0], gn_w[0], gn_b[0], b_f[0], q_norm_g[0], k_norm_g[0],
                  o_norm_g[0], w_out[0], norm2_g[0], w_router[0], b_router[0], w_gate_up[0],
                  b_gate_up[0], w_down[0], b_down[0], final_g,
                  tm_in=min(512, x.shape[1]), tm_out=min(1024, x.shape[1]))
```

```python
import functools

import jax
import jax.numpy as jnp
from jax import lax
from jax.experimental import pallas as pl
from jax.experimental.pallas import tpu as pltpu
from jax.experimental.pallas import tpu_sc as plsc

F32 = jnp.float32
BF16 = jnp.bfloat16
HIGHEST = lax.Precision.HIGHEST

D_MODEL = 1024
HEAD_DIM = 64
N_HEADS = 8
D_GRP = N_HEADS * HEAD_DIM
RWKV_COLS = 1792
LORA_OFF = 3 * D_GRP
GATE_OFF = LORA_OFF + 128
FOX_MAIN = 4 * D_GRP
N_EXPERTS = 32
TOP_K = 4
EXPERT_BLOCK = 512
SWIGLU_ALPHA = 1.702
SWIGLU_LIMIT = 7.0
NORM_EPS = 1e-6
GN_EPS = 64e-5
LOG2_E = 1.4426950408889634
LANES = 128
CHUNK = 64
FOX_SUB_KEYS = 512
HEADS_PER_SCAN = 4
SCAN_W = HEADS_PER_SCAN * HEAD_DIM
SEG_TERMS = 1
CUM_TERMS = 2
VMEM_LIMIT = 56 * 1024 * 1024


def _dot(a, b):
    return jnp.dot(a.astype(BF16), b.astype(BF16), preferred_element_type=F32)


def _fdot(a, b):
    return jnp.dot(a, b, precision=HIGHEST, preferred_element_type=F32)


def _split_dot(x, m, terms):
    acc = None
    rem = x
    for _ in range(terms):
        part = rem.astype(BF16)
        rem = rem - part.astype(F32)
        d = jnp.dot(part, m, preferred_element_type=F32)
        acc = d if acc is None else acc + d
    return acc


def _tri_dot(m, x, terms):
    acc = None
    rem = x
    for _ in range(terms):
        part = rem.astype(BF16)
        rem = rem - part.astype(F32)
        d = jnp.dot(m, part, preferred_element_type=F32)
        acc = d if acc is None else acc + d
    return acc


def _iota(shape, dim):
    return lax.broadcasted_iota(jnp.int32, shape, dim)


def _seg_reduce_mat(n):
    return (_iota((n, LANES), 0) // HEAD_DIM == _iota((n, LANES), 1)).astype(BF16)


def _seg_expand_mat(n):
    return (_iota((LANES, n), 1) // HEAD_DIM == _iota((LANES, n), 0)).astype(BF16)


D_PACK = D_MODEL // 2


def _pack_rows(x):
    lo = lax.bitcast_convert_type(x[:, :D_PACK].astype(BF16).astype(F32), jnp.uint32)
    hi = lax.bitcast_convert_type(x[:, D_PACK:].astype(BF16).astype(F32), jnp.uint32)
    return hi | (lo >> 16)


def _unpack_rows(p):
    lo = lax.bitcast_convert_type(p << 16, F32)
    hi = lax.bitcast_convert_type(p & jnp.uint32(0xFFFF0000), F32)
    return lo, hi


def _log_sigmoid(z):
    return jnp.minimum(z, 0.0) - jnp.log(1.0 + jnp.exp(-jnp.abs(z)))


def _sigmoid(z):
    return 1.0 / (1.0 + jnp.exp(-z))


def _adaln_kernel(c_ref, w_ref, b_ref, o_ref):
    c = c_ref[...]
    o_ref[...] = _fdot(c * _sigmoid(c), w_ref[...]) + b_ref[...]


def _adaln(c, w_ada, b_ada):
    bsz = c.shape[0]
    n_mod = w_ada.shape[1] // D_MODEL
    return pl.pallas_call(
        _adaln_kernel,
        grid=(n_mod,),
        in_specs=[pl.BlockSpec((bsz, D_MODEL), lambda j: (0, 0)),
                  pl.BlockSpec((D_MODEL, D_MODEL), lambda j: (0, j)),
                  pl.BlockSpec((1, D_MODEL), lambda j: (0, j))],
        out_specs=pl.BlockSpec((bsz, D_MODEL), lambda j: (0, j)),
        out_shape=jax.ShapeDtypeStruct((bsz, n_mod * D_MODEL), F32),
        name="adaln",
    )(c, w_ada, b_ada.reshape(1, -1))


def _inproj_kernel(x_ref, sh_ref, sc_ref, g_ref, wr_ref, wx_ref, wft_ref, bf_ref, qkg_ref,
                   pr_ref, px_ref, kb_ref, qb_ref, vt_ref, carry_ref):
    @pl.when(pl.program_id(1) == 0)
    def _():
        carry_ref[...] = jnp.zeros_like(carry_ref)

    x = x_ref[...]
    tm = x.shape[0]
    h = x * lax.rsqrt(jnp.mean(x * x, axis=-1, keepdims=True) + NORM_EPS) * g_ref[...]
    h = h * (1.0 + sc_ref[...]) + sh_ref[...]
    hb = h.astype(BF16)

    pr_ref[...] = jnp.dot(hb, wr_ref[...], preferred_element_type=F32).astype(BF16)

    px = jnp.dot(hb, wx_ref[...], preferred_element_type=F32)
    qk = px[:, :2 * D_GRP]
    ss = _split_dot(qk * qk, _seg_reduce_mat(2 * D_GRP), SEG_TERMS)
    inv = lax.rsqrt(ss * (1.0 / HEAD_DIM) + NORM_EPS)
    qk = qk * _split_dot(inv, _seg_expand_mat(2 * D_GRP), SEG_TERMS) * qkg_ref[...]
    px_ref[:, :2 * D_GRP] = qk.astype(BF16)
    px_ref[:, 2 * D_GRP:] = px[:, 2 * D_GRP:].astype(BF16)
    vt_ref[...] = px[:, 2 * D_GRP:3 * D_GRP].T.astype(BF16)

    lane = _iota((1, LANES), 1)
    z = jnp.zeros((tm, LANES), F32)
    for hd in range(N_HEADS):
        zh = jnp.sum(h * wft_ref[hd:hd + 1, :], axis=-1, keepdims=True)
        z = jnp.where(lane == hd, zh, z)
    cum = _log_sigmoid(z + bf_ref[...])
    row_id = _iota((tm, 1), 0)
    shift = 1
    while shift < tm:
        cum = cum + jnp.where(row_id >= shift, pltpu.roll(cum, shift, axis=0), 0.0)
        shift *= 2
    cum = cum + carry_ref[...]
    carry_ref[...] = cum[tm - 1:tm, :]

    parts = []
    rem = cum * LOG2_E
    for _ in range(3):
        part = rem.astype(BF16)
        rem = rem - part.astype(F32)
        parts.append(part)
    src, dst = _iota((LANES, LANES), 0), _iota((LANES, LANES), 1)

    def spread(offset):
        return sum(jnp.dot(part, ((dst == 8 * src + offset + t) & (src < N_HEADS)).astype(BF16),
                           preferred_element_type=F32) for t, part in enumerate(parts))

    slot = _iota((1, LANES), 1) % 8
    kb_ref[...] = (jnp.where((slot >= 3) & (slot < 6), 1.0, 0.0) - spread(0)).astype(BF16)
    qb_ref[...] = (jnp.where(slot < 3, 1.0, 0.0) + spread(3)).astype(BF16)


def _inproj(x, shift, scale, g, w_r, w_x, w_f_t, b_f, qk_gain, tm):
    bsz, seq, _ = x.shape
    const = lambda b, s: (0, 0)
    return pl.pallas_call(
        _inproj_kernel,
        grid=(bsz, seq // tm),
        in_specs=[pl.BlockSpec((None, tm, D_MODEL), lambda b, s: (b, s, 0)),
                  pl.BlockSpec((None, 1, D_MODEL), lambda b, s: (b, 0, 0)),
                  pl.BlockSpec((None, 1, D_MODEL), lambda b, s: (b, 0, 0)),
                  pl.BlockSpec((1, D_MODEL), const),
                  pl.BlockSpec((D_MODEL, RWKV_COLS), const),
                  pl.BlockSpec((D_MODEL, FOX_MAIN), const),
                  pl.BlockSpec((N_HEADS, D_MODEL), const),
                  pl.BlockSpec((1, LANES), const),
                  pl.BlockSpec((1, 2 * D_GRP), const)],
        out_specs=[pl.BlockSpec((None, tm, RWKV_COLS), lambda b, s: (b, s, 0)),
                   pl.BlockSpec((None, tm, FOX_MAIN), lambda b, s: (b, s, 0)),
                   pl.BlockSpec((None, tm, LANES), lambda b, s: (b, s, 0)),
                   pl.BlockSpec((None, tm, LANES), lambda b, s: (b, s, 0)),
                   pl.BlockSpec((None, D_GRP, tm), lambda b, s: (b, 0, s))],
        out_shape=[jax.ShapeDtypeStruct((bsz, seq, RWKV_COLS), BF16),
                   jax.ShapeDtypeStruct((bsz, seq, FOX_MAIN), BF16),
                   jax.ShapeDtypeStruct((bsz, seq, LANES), BF16),
                   jax.ShapeDtypeStruct((bsz, seq, LANES), BF16),
                   jax.ShapeDtypeStruct((bsz, D_GRP, seq), BF16)],
        scratch_shapes=[pltpu.VMEM((1, LANES), F32)],
        compiler_params=pltpu.CompilerParams(
            dimension_semantics=("parallel", "arbitrary"), vmem_limit_bytes=VMEM_LIMIT),
        name="inproj",
    )(x, shift, scale, g, w_r, w_x, w_f_t, b_f, qk_gain)


_NN = (((1,), (0,)), ((), ()))
_NT = (((1,), (1,)), ((), ()))
_TN = (((0,), (0,)), ((), ()))
SCAN_N = HEADS_PER_SCAN * CHUNK
BATCH_PER_STEP = 8
INV_LEVELS = 5
M_HEAD, M_STRICT, M_INCL, M_EYE, M_BASE, M_OFF = 0, 1, 2, 3, 4, 5


def _bdot(a, b, dims):
    return lax.dot_general(a, b, dims, preferred_element_type=F32)


def _scan_masks():
    rr, cc = _iota((SCAN_N, SCAN_W), 0), _iota((SCAN_N, SCAN_W), 1)
    ri, ci = _iota((SCAN_N, SCAN_N), 0), _iota((SCAN_N, SCAN_N), 1)
    same = ri // CHUNK == ci // CHUNK
    masks = [rr // CHUNK == cc // HEAD_DIM, same & (ri > ci), same & (ri >= ci), ri == ci,
             (ri // 2 == ci // 2) & (ri > ci)]
    blk = 2
    while blk < CHUNK:
        masks.append((ri // (2 * blk) == ci // (2 * blk)) & (ri // blk != ci // blk) & (ri > ci))
        blk *= 2
    return jnp.stack(masks).astype(BF16)


def _rwkv_kernel(p_ref, masks_ref, mu_ref, w0_ref, w2_ref, a0_ref, a2_ref, g2_ref, kk_ref, ka_ref,
                 rk_ref, gnw_ref, gnb_ref, wgu_ref, wd_ref, o_ref, wgu_bf_ref, wd_bf_ref,
                 last_ref, state_ref):
    wgu_bf_ref[...] = wgu_ref[...].astype(BF16)
    wd_bf_ref[...] = wd_ref[...].astype(BF16)

    @pl.when(pl.program_id(1) == 0)
    def _():
        last_ref[...] = jnp.zeros_like(last_ref)
        state_ref[...] = jnp.zeros_like(state_ref)

    mu, w0, w2, a0, a2, g2, k_k, k_a, r_k, gn_w, gn_b = (
        ref[...] for ref in (mu_ref, w0_ref, w2_ref, a0_ref, a2_ref, g2_ref, kk_ref, ka_ref,
                             rk_ref, gnw_ref, gnb_ref))
    rows = BATCH_PER_STEP * CHUNK
    p = p_ref[...].astype(F32).reshape(rows, RWKV_COLS)
    row_id = _iota((rows, 1), 0)
    prev = pltpu.roll(p, 1, axis=0)
    for bb in range(BATCH_PER_STEP):
        prev = jnp.where(row_id == bb * CHUNK, last_ref[bb], prev)
        last_ref[bb] = p[(bb + 1) * CHUNK - 1:(bb + 1) * CHUNK, :]
    pf = p + mu * (prev - p)
    r = pf[:, 0:D_GRP]
    k = pf[:, D_GRP:2 * D_GRP]
    v = pf[:, 2 * D_GRP:3 * D_GRP]
    lora = pf[:, LORA_OFF:GATE_OFF]
    gd = pf[:, GATE_OFF:RWKV_COLS]

    wlog = w0 + _dot(jnp.tanh(lora), w2)
    neg = -wlog
    softplus = jnp.maximum(neg, 0.0) + jnp.log(1.0 + jnp.exp(-jnp.abs(neg)))
    ld = -jnp.exp(-softplus - 0.5)
    a = _sigmoid(a0 + _dot(lora, a2))
    g = _dot(_sigmoid(gd), g2)

    red, exp_m = _seg_reduce_mat(D_GRP), _seg_expand_mat(D_GRP)
    kk = k * k_k
    n2 = _split_dot(kk * kk, red, SEG_TERMS)
    kk = kk * _split_dot(1.0 / jnp.maximum(jnp.sqrt(n2), 1e-12), exp_m, SEG_TERMS)
    k2 = k * (1.0 + (a - 1.0) * k_a)

    tr, tc = _iota((rows, rows), 0), _iota((rows, rows), 1)
    tri = ((tr >= tc) & (tr // CHUNK == tc // CHUNK)).astype(BF16)
    cl = _tri_dot(tri, ld, CUM_TERMS)
    cl_end = jnp.concatenate(
        [jnp.broadcast_to(cl[(bb + 1) * CHUNK - 1:(bb + 1) * CHUNK, :], (CHUNK, D_GRP))
         for bb in range(BATCH_PER_STEP)], axis=0)
    e_in = jnp.exp(cl)
    e_out = jnp.exp(-cl)
    e_rem = jnp.exp(cl_end - cl)
    p_end = jnp.exp(cl_end)
    kka = kk * a
    ops = [(-kk * jnp.exp(cl - ld)).astype(BF16), (kka * e_out).astype(BF16),
           (k2 * e_out).astype(BF16), (r * e_in).astype(BF16), v.astype(BF16),
           (kka * e_rem).astype(BF16), (k2 * e_rem).astype(BF16)]

    chains = [(bb, grp) for bb in range(BATCH_PER_STEP)
              for grp in range(N_HEADS // HEADS_PER_SCAN)]
    head_mask = masks_ref[M_HEAD]
    strict, incl = masks_ref[M_STRICT], masks_ref[M_INCL]

    def stacked(op, bb, grp):
        part = op[bb * CHUNK:(bb + 1) * CHUNK, grp * SCAN_W:(grp + 1) * SCAN_W]
        return jnp.concatenate([part] * HEADS_PER_SCAN, axis=0) * head_mask

    xs = [[stacked(op, bb, grp) for op in ops] for bb, grp in chains]
    st = [state_ref[bb, grp] for bb, grp in chains]
    sb = [s.astype(BF16) for s in st]
    nab = [_bdot(x[0], x[1], _NT).astype(BF16) for x in xs]
    aak = [_bdot(x[0], x[2], _NT).astype(BF16) * strict for x in xs]
    arb = [_bdot(x[3], x[1], _NT).astype(BF16) * incl for x in xs]
    ark = [_bdot(x[3], x[2], _NT).astype(BF16) * incl for x in xs]
    t_inv = [masks_ref[M_EYE] + n * masks_ref[M_BASE] for n in nab]
    for lvl in range(INV_LEVELS):
        half = [_bdot(t, n * masks_ref[M_OFF + lvl], _NN).astype(BF16) for t, n in zip(t_inv, nab)]
        t_inv = [t + _bdot(h, t, _NN).astype(BF16) for t, h in zip(t_inv, half)]
    rhs = [(_bdot(x[0], s, _NT) + _bdot(k, x[4], _NN)).astype(BF16)
           for x, s, k in zip(xs, sb, aak)]
    sa = [_bdot(t, h, _NN).astype(BF16) for t, h in zip(t_inv, rhs)]
    ys = [_bdot(x[3], s, _NT) + _bdot(b, u, _NN) + _bdot(k, x[4], _NN)
          for x, s, b, u, k in zip(xs, sb, arb, sa, ark)]
    for (bb, grp), x, s, u in zip(chains, xs, st, sa):
        decay = p_end[bb * CHUNK:bb * CHUNK + 1, grp * SCAN_W:(grp + 1) * SCAN_W]
        state_ref[bb, grp] = s * decay + _bdot(u, x[5], _TN) + _bdot(x[4], x[6], _TN)
    ys = [y[0:CHUNK] + y[CHUNK:2 * CHUNK] + y[2 * CHUNK:3 * CHUNK] + y[3 * CHUNK:4 * CHUNK]
          for y in ys]
    n_grp = N_HEADS // HEADS_PER_SCAN
    y = jnp.concatenate([jnp.concatenate(ys[bb * n_grp:(bb + 1) * n_grp], axis=1)
                         for bb in range(BATCH_PER_STEP)], axis=0)

    mean = _split_dot(_split_dot(y, red, SEG_TERMS) * (1.0 / HEAD_DIM), exp_m, SEG_TERMS)
    d = y - mean
    var = _split_dot(d * d, red, SEG_TERMS) * (1.0 / HEAD_DIM)
    yn = d * _split_dot(lax.rsqrt(var + GN_EPS), exp_m, SEG_TERMS) * gn_w + gn_b
    bonus = _split_dot(_split_dot(r * k2 * r_k, red, SEG_TERMS), exp_m, SEG_TERMS) * v
    o_ref[...] = ((yn + bonus) * g).astype(BF16).reshape(BATCH_PER_STEP, CHUNK, D_GRP)


def _rwkv(p_r, mu, w0, w2p, a0, a2p, g2, k_k, k_a, r_k, gn_w, gn_b, w_gate_up, w_down):
    bsz, seq, _ = p_r.shape
    assert bsz % BATCH_PER_STEP == 0
    n_chunk = seq // CHUNK
    n_step = (bsz // BATCH_PER_STEP) * n_chunk
    wgu2d = w_gate_up.reshape(-1, w_gate_up.shape[-1])
    wd2d = w_down.reshape(-1, w_down.shape[-1])
    assert wgu2d.shape[0] % (8 * n_step) == 0 and wd2d.shape[0] == wgu2d.shape[0]
    slab = wgu2d.shape[0] // n_step
    masks = _scan_masks()
    const = lambda b, s: (0, 0)
    step = lambda b, s: (b * n_chunk + s, 0)
    vec = pl.BlockSpec((1, D_GRP), const)
    y, wgu_bf, wd_bf = pl.pallas_call(
        _rwkv_kernel,
        grid=(bsz // BATCH_PER_STEP, n_chunk),
        in_specs=[pl.BlockSpec((BATCH_PER_STEP, CHUNK, RWKV_COLS), lambda b, s: (b, s, 0)),
                  pl.BlockSpec(masks.shape, lambda b, s: (0, 0, 0)),
                  pl.BlockSpec((1, RWKV_COLS), const),
                  vec, pl.BlockSpec((LANES, D_GRP), const),
                  vec, pl.BlockSpec((LANES, D_GRP), const),
                  pl.BlockSpec((LANES, D_GRP), const),
                  vec, vec, vec, vec, vec,
                  pl.BlockSpec((slab, wgu2d.shape[1]), step),
                  pl.BlockSpec((slab, wd2d.shape[1]), step)],
        out_specs=[pl.BlockSpec((BATCH_PER_STEP, CHUNK, D_GRP), lambda b, s: (b, s, 0)),
                   pl.BlockSpec((slab, wgu2d.shape[1]), step),
                   pl.BlockSpec((slab, wd2d.shape[1]), step)],
        out_shape=[jax.ShapeDtypeStruct((bsz, seq, D_GRP), BF16),
                   jax.ShapeDtypeStruct(wgu2d.shape, BF16),
                   jax.ShapeDtypeStruct(wd2d.shape, BF16)],
        scratch_shapes=[pltpu.VMEM((BATCH_PER_STEP, 1, RWKV_COLS), F32),
                        pltpu.VMEM((BATCH_PER_STEP, N_HEADS // HEADS_PER_SCAN, SCAN_W, SCAN_W), F32)],
        compiler_params=pltpu.CompilerParams(
            dimension_semantics=("parallel", "arbitrary"), vmem_limit_bytes=VMEM_LIMIT),
        name="rwkv",
    )(p_r, masks, mu, w0, w2p, a0, a2p, g2, k_k, k_a, r_k, gn_w, gn_b, wgu2d, wd2d)
    return y, wgu_bf.reshape(w_gate_up.shape), wd_bf.reshape(w_down.shape)


def _fox_kernel(q_ref, qb_ref, k_ref, kb_ref, vt_ref, og_ref, ong_ref, o_ref, m_ref, l_ref, acc_ref,
                *, seq):
    hp = pl.program_id(1)
    lane = _iota((1, LANES), 1)
    q = q_ref[...]
    qb = qb_ref[...]
    zero = jnp.zeros_like(q)
    qcat = [jnp.concatenate([jnp.where(lane // HEAD_DIM == hh, q, zero),
                             jnp.where(lane // 8 == hp * 2 + hh, qb, zero)], axis=1)
            for hh in range(2)]
    keys = min(FOX_SUB_KEYS, seq)
    n_sub = seq // keys
    half = keys // 2
    mask_a = _iota((half, half), 1) >= _iota((half, half), 0)
    mask_b = _iota((keys, half), 1) + half >= _iota((keys, half), 0)

    m_ref[...] = jnp.full(m_ref.shape, -jnp.inf, F32)
    l_ref[...] = jnp.zeros(l_ref.shape, F32)
    acc_ref[...] = jnp.zeros(acc_ref.shape, F32)

    def plan(s):
        lo = s * keys
        pieces = [(half, slice(lo, lo + half), mask_a), (keys, slice(lo + half, lo + keys), mask_b)]
        if lo + keys < seq:
            pieces.append((keys, slice(lo + keys, seq), None))
        return pieces

    def scores(s):
        lo = s * keys
        kcat = jnp.concatenate([k_ref[lo:lo + keys, :], kb_ref[lo:lo + keys, :]], axis=1)
        return [lax.dot_general(kcat[:nk], qc[qs, :], _NT, preferred_element_type=F32)
                for qc in qcat for nk, qs, _ in plan(s)]

    pending = scores(0)
    for s in range(n_sub):
        lo = s * keys
        nxt = scores(s + 1) if s + 1 < n_sub else None
        n_piece = len(plan(s))
        pieces = []
        for hh in range(2):
            for (nk, qs, mask), st in zip(plan(s), pending[hh * n_piece:(hh + 1) * n_piece]):
                pieces.append((hh, nk, qs, st if mask is None else jnp.where(mask, st, -jnp.inf)))
        m_old = [m_ref[hh, :, qs] for hh, _, qs, _ in pieces]
        m_new = [jnp.maximum(m, jnp.max(st, axis=0, keepdims=True))
                 for m, (_, _, _, st) in zip(m_old, pieces)]
        pts = [jnp.exp2(st - m) for (_, _, _, st), m in zip(pieces, m_new)]
        pvs = [jnp.dot(vt_ref[:, lo:lo + nk], pt.astype(BF16), preferred_element_type=F32)
               for (_, nk, _, _), pt in zip(pieces, pts)]
        for (hh, _, qs, _), mo, mn, pt, pv in zip(pieces, m_old, m_new, pts, pvs):
            alpha = jnp.exp2(mo - mn)
            m_ref[hh, :, qs] = mn
            l_ref[hh, :, qs] = alpha * l_ref[hh, :, qs] + jnp.sum(pt, axis=0, keepdims=True)
            acc_ref[hh, :, qs] = (alpha * acc_ref[hh, :, qs]
                                  + pv[hh * HEAD_DIM:(hh + 1) * HEAD_DIM, :])
        pending = nxt

    outs = []
    for hh in range(2):
        o = acc_ref[hh] / l_ref[hh]
        outs.append(o * lax.rsqrt(jnp.mean(o * o, axis=0, keepdims=True) + NORM_EPS))
    o = jnp.concatenate(outs, axis=0).T
    o_ref[...] = (o * ong_ref[...] * _sigmoid(og_ref[...].astype(F32))).astype(BF16)


def _fox(p_x, k_bias, q_bias, v_t, o_gain):
    bsz, seq, _ = p_x.shape
    npair = N_HEADS // 2
    return pl.pallas_call(
        functools.partial(_fox_kernel, seq=seq),
        grid=(bsz, npair),
        in_specs=[pl.BlockSpec((None, seq, LANES), lambda b, h: (b, 0, h)),
                  pl.BlockSpec((None, seq, LANES), lambda b, h: (b, 0, 0)),
                  pl.BlockSpec((None, seq, LANES), lambda b, h: (b, 0, npair + h)),
                  pl.BlockSpec((None, seq, LANES), lambda b, h: (b, 0, 0)),
                  pl.BlockSpec((None, LANES, seq), lambda b, h: (b, h, 0)),
                  pl.BlockSpec((None, seq, LANES), lambda b, h: (b, 0, 3 * npair + h)),
                  pl.BlockSpec((1, LANES), lambda b, h: (0, 0))],
        out_specs=pl.BlockSpec((None, seq, LANES), lambda b, h: (b, 0, h)),
        out_shape=jax.ShapeDtypeStruct((bsz, seq, D_GRP), BF16),
        scratch_shapes=[pltpu.VMEM((2, 1, seq), F32), pltpu.VMEM((2, 1, seq), F32),
                        pltpu.VMEM((2, HEAD_DIM, seq), F32)],
        compiler_params=pltpu.CompilerParams(
            dimension_semantics=("parallel", "parallel"), vmem_limit_bytes=VMEM_LIMIT),
        name="fox",
    )(p_x, q_bias, p_x, k_bias, v_t, p_x, o_gain)


def _outproj_kernel(x_ref, yr_ref, yf_ref, g1_ref, sh_ref, sc_ref, ng_ref, wor_ref, wof_ref,
                    wrt_ref, wrl_ref, brt_ref, x1_ref, h2_ref, idx_ref, gate_ref, rank_ref, cnt_ref,
                    carry_ref):
    @pl.when(pl.program_id(0) == 0)
    def _():
        carry_ref[...] = jnp.zeros_like(carry_ref)

    y = (jnp.dot(yr_ref[...], wor_ref[...], preferred_element_type=F32)
         + jnp.dot(yf_ref[...], wof_ref[...], preferred_element_type=F32))
    x1 = x_ref[...] + g1_ref[...] * y
    x1_ref[...] = x1
    tm = x1.shape[0]
    h = x1 * lax.rsqrt(jnp.mean(x1 * x1, axis=-1, keepdims=True) + NORM_EPS) * ng_ref[...]
    h2 = h * (1.0 + sc_ref[...]) + sh_ref[...]
    h2_ref[...] = _pack_rows(h2)

    h_hi = h2.astype(BF16)
    h_lo = (h2 - h_hi.astype(F32)).astype(BF16)
    logits = (lax.dot_general(wrt_ref[...], h_hi, _NT, preferred_element_type=F32)
              + lax.dot_general(wrt_ref[...], h_lo, _NT, preferred_element_type=F32)
              + lax.dot_general(wrl_ref[...], h_hi, _NT, preferred_element_type=F32))
    lg = logits[:N_EXPERTS, :] + brt_ref[...]
    expert = _iota((N_EXPERTS, tm), 0)
    picks = []
    hot_sum = jnp.zeros((N_EXPERTS, tm), F32)
    for _ in range(TOP_K):
        m = jnp.max(lg, axis=0, keepdims=True)
        sel = jnp.min(jnp.where(lg == m, expert, N_EXPERTS), axis=0, keepdims=True)
        hot = expert == sel
        picks.append((m, sel, hot))
        hot_sum = hot_sum + hot.astype(F32)
        lg = jnp.where(hot, -jnp.inf, lg)
    es = [jnp.exp(m - picks[0][0]) for m, _, _ in picks]
    den = es[0] + es[1] + es[2] + es[3]

    earlier = (_iota((tm, tm), 0) < _iota((tm, tm), 1)).astype(BF16)
    before = jnp.dot(hot_sum.astype(BF16), earlier, preferred_element_type=F32) + carry_ref[...]
    ranks = [jnp.sum(jnp.where(hot, before, 0.0), axis=0, keepdims=True).astype(jnp.int32)
             for _, _, hot in picks]
    pad_i = jnp.zeros((8 - TOP_K, tm), jnp.int32)
    idx_ref[...] = jnp.concatenate([sel for _, sel, _ in picks] + [pad_i], axis=0)
    gate_ref[...] = jnp.concatenate([e / den for e in es] + [pad_i.astype(F32)], axis=0)
    rank_ref[...] = jnp.concatenate(ranks + [pad_i], axis=0)
    carry_ref[...] = carry_ref[...] + jnp.sum(hot_sum, axis=1, keepdims=True)
    cnt_ref[...] = jnp.broadcast_to(carry_ref[...], cnt_ref.shape)


def _outproj(x2d, y_r, y_f, gate1, shift2, scale2, norm_g, wo_r, wo_f, w_rt, b_rt, tm, seq,
             row0, t):
    w_rt_hi = w_rt.astype(BF16)
    w_rt_lo = (w_rt - w_rt_hi.astype(F32)).astype(BF16)
    per_b = seq // tm
    blk0 = row0 // tm
    const = lambda i: (0, 0)
    rows = lambda i: (i, 0)
    rows_in = lambda i: (i + blk0, 0)
    mod = pl.BlockSpec((None, 1, D_MODEL), lambda i: ((i + blk0) // per_b, 0, 0))
    return pl.pallas_call(
        _outproj_kernel,
        grid=(t // tm,),
        in_specs=[pl.BlockSpec((tm, D_MODEL), rows_in),
                  pl.BlockSpec((tm, D_GRP), rows_in),
                  pl.BlockSpec((tm, D_GRP), rows_in),
                  mod, mod, mod,
                  pl.BlockSpec((1, D_MODEL), const),
                  pl.BlockSpec((D_GRP, D_MODEL), const),
                  pl.BlockSpec((D_GRP, D_MODEL), const),
                  pl.BlockSpec((LANES, D_MODEL), const),
                  pl.BlockSpec((LANES, D_MODEL), const),
                  pl.BlockSpec((N_EXPERTS, 1), const)],
        out_specs=[pl.BlockSpec((tm, D_MODEL), rows),
                   pl.BlockSpec((tm, D_PACK), rows),
                   pl.BlockSpec((8, tm), lambda i: (0, i)),
                   pl.BlockSpec((8, tm), lambda i: (0, i)),
                   pl.BlockSpec((8, tm), lambda i: (0, i)),
                   pl.BlockSpec((N_EXPERTS, LANES), const)],
        out_shape=[jax.ShapeDtypeStruct((t, D_MODEL), F32),
                   jax.ShapeDtypeStruct((t, D_PACK), jnp.uint32),
                   jax.ShapeDtypeStruct((8, t), jnp.int32),
                   jax.ShapeDtypeStruct((8, t), F32),
                   jax.ShapeDtypeStruct((8, t), jnp.int32),
                   jax.ShapeDtypeStruct((N_EXPERTS, LANES), F32)],
        scratch_shapes=[pltpu.VMEM((N_EXPERTS, 1), F32)],
        compiler_params=pltpu.CompilerParams(
            dimension_semantics=("arbitrary",), vmem_limit_bytes=VMEM_LIMIT),
        name="outproj",
    )(x2d, y_r, y_f, gate1, shift2, scale2, norm_g, wo_r, wo_f, w_rt_hi, w_rt_lo, b_rt)


SC_CORES = 2
SC_SUBCORES = 16
SC_ROWS = 64


def _sc_gather_rows(idx, src):
    n_workers = SC_CORES * SC_SUBCORES
    m = idx.shape[0]
    d = src.shape[1]
    assert m % (n_workers * SC_ROWS) == 0
    n_chunks = m // (n_workers * SC_ROWS)
    mesh = plsc.VectorSubcoreMesh(core_axis_name="c", subcore_axis_name="s")

    @functools.partial(
        pl.kernel, mesh=mesh,
        out_type=jax.ShapeDtypeStruct((m, d), src.dtype),
        scratch_types=[pltpu.VMEM((n_chunks, SC_ROWS), jnp.int32),
                       pltpu.VMEM((SC_ROWS, d), src.dtype),
                       pltpu.SemaphoreType.DMA],
        name="sc_gather")
    def gather(src_hbm, idx_hbm, out_hbm, idx_v, rows_v, sem):
        wid = lax.axis_index("s") * SC_CORES + lax.axis_index("c")
        pltpu.sync_copy(idx_hbm.at[wid], idx_v)

        @pl.loop(0, n_chunks)
        def _(j):
            pltpu.async_copy(src_hbm.at[idx_v.at[j]], rows_v, sem).wait()
            pltpu.sync_copy(rows_v, out_hbm.at[pl.ds((wid * n_chunks + j) * SC_ROWS, SC_ROWS)])

    return gather(src, idx.reshape(n_workers, n_chunks, SC_ROWS))


def _sc_scatter_rows(src, dest, n_out):
    n_workers = SC_CORES * SC_SUBCORES
    t, d = src.shape
    n_slot = dest.shape[0]
    assert t % (n_workers * SC_ROWS) == 0
    n_chunks = t // (n_workers * SC_ROWS)
    mesh = plsc.VectorSubcoreMesh(core_axis_name="c", subcore_axis_name="s")
    idx = dest.reshape(n_slot, n_workers, n_chunks, SC_ROWS).transpose(1, 2, 0, 3)
    idx = idx.reshape(n_workers, n_chunks * n_slot, SC_ROWS)

    @functools.partial(
        pl.kernel, mesh=mesh,
        out_type=jax.ShapeDtypeStruct((n_out, d), src.dtype),
        scratch_types=[pltpu.VMEM((n_chunks * n_slot, SC_ROWS), jnp.int32),
                       pltpu.VMEM((SC_ROWS, d), src.dtype)],
        name="sc_scatter")
    def scatter(src_hbm, idx_hbm, out_hbm, idx_v, rows_v):
        wid = lax.axis_index("s") * SC_CORES + lax.axis_index("c")
        pltpu.sync_copy(idx_hbm.at[wid], idx_v)

        @pl.loop(0, n_chunks)
        def _(j):
            pltpu.sync_copy(src_hbm.at[pl.ds((wid * n_chunks + j) * SC_ROWS, SC_ROWS)], rows_v)
            for k in range(n_slot):
                pltpu.sync_copy(rows_v, out_hbm.at[idx_v.at[j * n_slot + k]])

    return scatter(src, idx)


def _expert_kernel(be_ref, nv_ref, x_ref, wgu_ref, bgu_ref, wd_ref, bd_ref, o_ref):
    del be_ref
    n_valid = nv_ref[pl.program_id(0)]

    half = EXPERT_BLOCK // 2

    def run(n_half):
        rows = n_half * half
        valid = _iota((rows, 1), 0) < n_valid
        lo, hi = _unpack_rows(jnp.where(valid, x_ref[0:rows, :], jnp.uint32(0)))
        x = jnp.concatenate([lo.astype(BF16), hi.astype(BF16)], axis=1)
        gus = [jnp.dot(x[r * half:(r + 1) * half], wgu_ref[...], preferred_element_type=F32)
               + bgu_ref[...] for r in range(n_half)]
        for r, gu in enumerate(gus):
            gate = jnp.minimum(gu[:, :D_MODEL], SWIGLU_LIMIT)
            up = jnp.clip(gu[:, D_MODEL:], -SWIGLU_LIMIT, SWIGLU_LIMIT)
            act = gate * _sigmoid(SWIGLU_ALPHA * gate) * (up + 1.0)
            o_ref[r * half:(r + 1) * half, :] = _pack_rows(
                jnp.dot(act.astype(BF16), wd_ref[...], preferred_element_type=F32) + bd_ref[...])

    pl.when(n_valid > half)(lambda: run(2))
    pl.when((n_valid > 0) & (n_valid <= half))(lambda: run(1))


def _experts(block_e, n_valid, xs, w_gu, b_gu, w_d, b_d):
    n_blocks = block_e.shape[0]
    grid_spec = pltpu.PrefetchScalarGridSpec(
        num_scalar_prefetch=2,
        grid=(n_blocks,),
        in_specs=[pl.BlockSpec((EXPERT_BLOCK, D_PACK), lambda j, be, nv: (j, 0)),
                  pl.BlockSpec((None, D_MODEL, 2 * D_MODEL), lambda j, be, nv: (be[j], 0, 0)),
                  pl.BlockSpec((None, 1, 2 * D_MODEL), lambda j, be, nv: (be[j], 0, 0)),
                  pl.BlockSpec((None, D_MODEL, D_MODEL), lambda j, be, nv: (be[j], 0, 0)),
                  pl.BlockSpec((None, 1, D_MODEL), lambda j, be, nv: (be[j], 0, 0))],
        out_specs=pl.BlockSpec((EXPERT_BLOCK, D_PACK), lambda j, be, nv: (j, 0)),
    )
    return pl.pallas_call(
        _expert_kernel,
        grid_spec=grid_spec,
        out_shape=jax.ShapeDtypeStruct(xs.shape, jnp.uint32),
        compiler_params=pltpu.CompilerParams(
            dimension_semantics=("arbitrary",), vmem_limit_bytes=VMEM_LIMIT),
        name="experts",
    )(block_e, n_valid, xs, w_gu, b_gu, w_d, b_d)


COMBINE_TOKENS = 1024
MOE_SPLITS = 2


def _combine_kernel(yg_ref, x1_ref, gate_ref, g2_ref, fg_ref, o_ref):
    gates = gate_ref[...].T
    acc_lo = acc_hi = None
    for kk in range(TOP_K):
        lo, hi = _unpack_rows(yg_ref[kk * COMBINE_TOKENS:(kk + 1) * COMBINE_TOKENS, :])
        g = gates[:, kk:kk + 1]
        acc_lo = g * lo if acc_lo is None else acc_lo + g * lo
        acc_hi = g * hi if acc_hi is None else acc_hi + g * hi
    x2 = x1_ref[...] + g2_ref[...] * jnp.concatenate([acc_lo, acc_hi], axis=1)
    o_ref[...] = x2 * lax.rsqrt(jnp.mean(x2 * x2, axis=-1, keepdims=True) + NORM_EPS) * fg_ref[...]


def _combine_kernel_into(prev_ref, *refs):
    del prev_ref
    _combine_kernel(*refs)


def _combine(yg, x1, gates, gate2, final_g, seq, row0, t_total, prev):
    t = x1.shape[0]
    tm = COMBINE_TOKENS
    per_b = seq // tm
    blk0 = row0 // tm
    rows = lambda i: (i, 0)
    in_specs = [pl.BlockSpec((TOP_K * tm, D_PACK), rows),
                pl.BlockSpec((tm, D_MODEL), rows),
                pl.BlockSpec((8, tm), lambda i: (0, i)),
                pl.BlockSpec((None, 1, D_MODEL), lambda i: ((i + blk0) // per_b, 0, 0)),
                pl.BlockSpec((1, D_MODEL), lambda i: (0, 0))]
    args = (yg, x1, gates, gate2, final_g)
    if prev is not None:
        in_specs = [pl.BlockSpec(memory_space=pl.ANY)] + in_specs
        args = (prev,) + args
    return pl.pallas_call(
        _combine_kernel if prev is None else _combine_kernel_into,
        grid=(t // tm,),
        in_specs=in_specs,
        out_specs=pl.BlockSpec((tm, D_MODEL), lambda i: (i + blk0, 0)),
        out_shape=jax.ShapeDtypeStruct((t_total, D_MODEL), F32),
        input_output_aliases={} if prev is None else {0: 0},
        compiler_params=pltpu.CompilerParams(
            dimension_semantics=("parallel",), vmem_limit_bytes=VMEM_LIMIT),
        name="combine",
    )(*args)


def _moe(h2, idx, gates, rank, counts, x1, gate2, final_g, w_gu, b_gu, w_d, b_d, seq,
         row0, t_total, prev):
    t = h2.shape[0]
    n_slots = t * TOP_K
    n_blocks = -(-n_slots // EXPERT_BLOCK) + N_EXPERTS
    cap = n_blocks * EXPERT_BLOCK
    padded = (counts + EXPERT_BLOCK - 1) // EXPERT_BLOCK * EXPERT_BLOCK
    pad_ends = jnp.cumsum(padded)
    pad_starts = pad_ends - padded
    experts = jnp.arange(N_EXPERTS, dtype=jnp.int32)
    dest = jnp.sum(jnp.where(idx[..., None] == experts, pad_starts, 0), axis=-1) + rank
    block_starts = jnp.arange(n_blocks, dtype=jnp.int32) * EXPERT_BLOCK
    block_e = jnp.minimum(jnp.sum(block_starts[:, None] >= pad_ends[None, :], axis=1),
                          N_EXPERTS - 1).astype(jnp.int32)
    n_valid = jnp.clip(counts[block_e] - (block_starts - pad_starts[block_e]), 0, EXPERT_BLOCK)

    xs = _sc_scatter_rows(h2, dest, cap)
    yb = _experts(block_e, n_valid.astype(jnp.int32), xs, w_gu, b_gu, w_d, b_d)
    dest_blocks = dest.reshape(TOP_K, -1, COMBINE_TOKENS).transpose(1, 0, 2).reshape(-1)
    yg = _sc_gather_rows(dest_blocks, yb)
    return _combine(yg, x1, gates, gate2, final_g, seq, row0, t_total, prev)


def _layer(x, c_mod, norm1_g, w_in, mu_shift, w0, w2, a0, a2, g2, k_k, k_a, r_k, gn_w, gn_b, b_f,
           q_norm_g, k_norm_g, o_norm_g, w_out, norm2_g, w_router, b_router, w_gate_up,
           b_gate_up, w_down, b_down, final_g, tm_in, tm_out):
    bsz, seq, _ = x.shape
    shift1, scale1, gate1, shift2, scale2, gate2 = (
        m.reshape(bsz, 1, D_MODEL) for m in jnp.split(c_mod, 6, axis=-1))
    row = lambda v: v.reshape(1, -1)

    w_r = w_in[:, :RWKV_COLS].astype(BF16)
    w_x = w_in[:, RWKV_COLS:RWKV_COLS + FOX_MAIN].astype(BF16)
    w_f = w_in[:, RWKV_COLS + FOX_MAIN:].T
    b_fp = jnp.pad(b_f, (0, LANES - N_HEADS)).reshape(1, LANES)
    qk_gain = jnp.concatenate([jnp.tile(q_norm_g, N_HEADS) * (HEAD_DIM ** -0.5 * LOG2_E),
                               jnp.tile(k_norm_g, N_HEADS)]).reshape(1, -1)
    p_r, p_x, k_bias, q_bias, v_t = _inproj(x, shift1, scale1, row(norm1_g), w_r, w_x, w_f, b_fp,
                                            qk_gain, tm_in)

    zeros = jnp.zeros((LANES - 64, D_GRP), F32)
    w2p = jnp.concatenate([w2, zeros], axis=0).astype(BF16)
    a2p = jnp.concatenate([zeros, a2], axis=0).astype(BF16)
    y_r, w_gu, w_d = _rwkv(p_r, row(mu_shift), row(w0), w2p, row(a0), a2p, g2.astype(BF16),
                           row(k_k), row(k_a), row(r_k), row(gn_w), row(gn_b), w_gate_up, w_down)

    y_f = _fox(p_x, k_bias, q_bias, v_t, jnp.tile(o_norm_g, 2).reshape(1, LANES))

    t = bsz * seq
    w_rt = jnp.pad(w_router.T, ((0, LANES - N_EXPERTS), (0, 0)))
    b_rt = b_router.reshape(N_EXPERTS, 1)
    wo = w_out.astype(BF16)
    b_gu, b_d = b_gate_up.reshape(N_EXPERTS, 1, -1), b_down.reshape(N_EXPERTS, 1, -1)
    t_part = t // MOE_SPLITS
    out = None
    for part in range(MOE_SPLITS):
        row0 = part * t_part
        x1, h2, idx, gates, rank, cnt = _outproj(
            x.reshape(t, D_MODEL), y_r.reshape(t, D_GRP), y_f.reshape(t, D_GRP), gate1, shift2,
            scale2, row(norm2_g), wo[:D_GRP], wo[D_GRP:], w_rt, b_rt, tm_out, seq, row0, t_part)
        counts = cnt[:, 0].astype(jnp.int32)
        out = _moe(h2, idx[:TOP_K], gates, rank[:TOP_K], counts, x1, gate2, row(final_g),
                   w_gu, b_gu, w_d, b_d, seq, row0, t, out)
    return out.reshape(bsz, seq, D_MODEL)


def kernel(x, c, w_ada, b_ada, norm1_g, w_in, mu_shift, w0, w2, a0, a2, g2, k_k, k_a, r_k, gn_w,
           gn_b, b_f, q_norm_g, k_norm_g, o_norm_g, w_out, norm2_g, w_router, b_router, w_gate_up,
           b_gate_up, w_down, b_down, final_g):
    assert w_ada.shape[0] == 1, "single-layer block"
    c_mod = _adaln(c, w_ada[0], b_ada[0])
    return _layer(x, c_mod, norm1_g[0], w_in[0], mu_shift[0], w0[0], w2[0], a0[0], a2[0], g2[0],
                  k_k[0], k_a[0], r_k[0], gn_w[0], gn_b[0], b_f[0], q_norm_g[0], k_norm_g[0],
                  o_norm_g[0], w_out[0], norm2_g[0], w_router[0], b_router[0], w_gate_up[0],
                  b_gate_up[0], w_down[0], b_down[0], final_g,
                  tm_in=min(512, x.shape[1]), tm_out=min(1024, x.shape[1]))
```
